```python
import math
import jax, jax.numpy as jnp
from jax import lax
import numpy as np

D_MODEL = 1024
BATCH = 32
SEQ = 2048
DEPTH = 1

N_META = 16
EPS = 1e-6

GLA_HEADS = 4
GLA_DK = 128
GLA_DV = 256
GLA_GATE_RANK = 16
GLA_GATE_NORMALIZER = 16.0
GLA_CHUNK = 64
GLA_KW = GLA_HEADS * GLA_DK
GLA_VW = GLA_HEADS * GLA_DV

MLA_HEADS = 8
MLA_NOPE = 128
MLA_ROPE = 64
MLA_DV = 128
MLA_Q_RANK = 256
MLA_KV_RANK = 128
MLA_QK = MLA_NOPE + MLA_ROPE
MLA_VW = MLA_HEADS * MLA_DV
ROPE_BASE = 10000.0
ATTN_BLOCK = 128

SPLITS = (GLA_KW, GLA_KW, GLA_VW, GLA_GATE_RANK, GLA_VW,
          MLA_Q_RANK, MLA_KV_RANK, MLA_ROPE, MLA_VW,
          D_MODEL, D_MODEL)
IN_WIDTH = (2 * GLA_KW + 2 * GLA_VW + GLA_GATE_RANK + MLA_Q_RANK + MLA_KV_RANK
            + MLA_ROPE + MLA_VW + 2 * D_MODEL)

kernel_name = "hybrid_gla_mla_gated_meta"


def rms_norm(x, g):
    xf = x.astype(jnp.float32)
    y = xf * lax.rsqrt(jnp.mean(xf * xf, axis=-1, keepdims=True) + EPS)
    return (y * g.astype(jnp.float32)).astype(x.dtype)


def rope_tables(n, dim):
    inv = 1.0 / (ROPE_BASE ** (jnp.arange(0, dim, 2, dtype=jnp.float32) / dim))
    ang = jnp.arange(n, dtype=jnp.float32)[:, None] * inv[None, :]
    return jnp.cos(ang), jnp.sin(ang)


def apply_rope(x, cos, sin):
    xf = x.astype(jnp.float32)
    x1, x2 = jnp.split(xf, 2, axis=-1)
    return jnp.concatenate([x1 * cos - x2 * sin, x2 * cos + x1 * sin], axis=-1).astype(x.dtype)


def gla_chunked(q, k, v, g):
    B, L, H, dk = q.shape
    dv = v.shape[-1]
    C = GLA_CHUNK
    front = (-N_META) % C
    back = (-(L - N_META)) % C
    padw = ((0, 0), (front, back), (0, 0), (0, 0))
    q, k, v, g = [jnp.pad(t.astype(jnp.float32), padw) for t in (q, k, v, g)]
    N = q.shape[1] // C

    def chunks(t):
        return t.reshape(B, N, C, H, t.shape[-1]).transpose(1, 0, 3, 2, 4)

    q, k, v, g = chunks(q), chunks(k), chunks(v), chunks(g)
    b = jnp.cumsum(g, axis=3)
    b_last = b[:, :, :, -1:, :]
    qe = q * jnp.exp(b)
    ke = k * jnp.exp(-b)
    kl = k * jnp.exp(b_last - b)
    mask = jnp.tril(jnp.ones((C, C), dtype=bool))
    A = jnp.where(mask, jnp.einsum('nbhid,nbhjd->nbhij', qe, ke), 0.0)
    o_intra = jnp.einsum('nbhij,nbhjv->nbhiv', A, v)
    decay = jnp.exp(b_last[:, :, :, 0, :])

    def step(S, inp):
        qe_n, kl_n, v_n, d_n = inp
        o = jnp.einsum('bhid,bhdv->bhiv', qe_n, S)
        S = S * d_n[..., None] + jnp.einsum('bhjd,bhjv->bhdv', kl_n, v_n)
        return S, o

    S0 = jnp.zeros((B, H, dk, dv), jnp.float32)
    _, o_inter = lax.scan(step, S0, (qe, kl, v, decay))
    o = (o_intra + o_inter).transpose(1, 0, 3, 2, 4).reshape(B, N * C, H, dv)
    return o[:, front:front + L]


def mla_attention(c_q, c_kv, k_rope, q_norm_g, w_uq, kv_norm_g, w_ukv):
    B, L, _ = c_q.shape
    H = MLA_HEADS
    cos, sin = rope_tables(L, MLA_ROPE)
    q = (rms_norm(c_q, q_norm_g) @ w_uq).reshape(B, L, H, MLA_QK)
    q_nope, q_rope = q[..., :MLA_NOPE], q[..., MLA_NOPE:]
    q_rope = apply_rope(q_rope, cos[:, None, :], sin[:, None, :])
    kv = (rms_norm(c_kv, kv_norm_g) @ w_ukv).reshape(B, L, H, MLA_NOPE + MLA_DV)
    k_nope, v = kv[..., :MLA_NOPE], kv[..., MLA_NOPE:]
    k_rope = apply_rope(k_rope, cos, sin)
    k = jnp.concatenate([k_nope, jnp.broadcast_to(k_rope[:, :, None, :], (B, L, H, MLA_ROPE))], axis=-1)
    q = jnp.concatenate([q_nope, q_rope], axis=-1)

    Lp = ((L + ATTN_BLOCK - 1) // ATTN_BLOCK) * ATTN_BLOCK
    nb = Lp // ATTN_BLOCK
    padw = ((0, 0), (0, Lp - L), (0, 0), (0, 0))
    q = jnp.pad(q, padw).transpose(0, 2, 1, 3)
    k = jnp.pad(k, padw).transpose(0, 2, 1, 3)
    v = jnp.pad(v, padw).transpose(0, 2, 1, 3)
    qb = q.reshape(B, H, nb, ATTN_BLOCK, MLA_QK).transpose(2, 0, 1, 3, 4)
    scale = 1.0 / math.sqrt(MLA_QK)
    kpos = jnp.arange(Lp)

    def block(args):
        q_blk, i = args
        s = jnp.einsum('bhqd,bhkd->bhqk', q_blk, k).astype(jnp.float32) * scale
        qpos = i * ATTN_BLOCK + jnp.arange(ATTN_BLOCK)
        s = jnp.where(kpos[None, :] <= qpos[:, None], s, -jnp.inf)
        p = jax.nn.softmax(s, axis=-1)
        return jnp.einsum('bhqk,bhkv->bhqv', p.astype(v.dtype), v)

    o = lax.map(block, (qb, jnp.arange(nb)))
    o = o.transpose(1, 0, 3, 2, 4).reshape(B, Lp, H * MLA_DV)
    return o[:, :L]


def _fwd_setup_inputs(seed: int = 0) -> dict:
    key = jax.random.key(seed)
    ks = jax.random.split(key, 16)
    f = jnp.float32
    n = lambda k, s, sc: jax.random.normal(k, s, f) * sc
    return {
        "x": n(ks[0], (BATCH, SEQ, D_MODEL), 1.0),
        "meta_tokens": n(ks[1], (N_META, D_MODEL), 1.0),
        "norm_g": 1.0 + n(ks[2], (DEPTH, D_MODEL), 0.02),
        "w_in": n(ks[3], (DEPTH, D_MODEL, IN_WIDTH), D_MODEL ** -0.5),
        "gla_gate_w": n(ks[4], (DEPTH, GLA_GATE_RANK, GLA_KW), GLA_GATE_RANK ** -0.5),
        "gla_gate_b": n(ks[5], (DEPTH, GLA_KW), 0.1),
        "gla_norm_g": 1.0 + n(ks[6], (DEPTH, GLA_DV), 0.02),
        "gla_proj": n(ks[7], (DEPTH, GLA_VW, D_MODEL), GLA_VW ** -0.5),
        "mla_q_norm_g": 1.0 + n(ks[8], (DEPTH, MLA_Q_RANK), 0.02),
        "mla_w_uq": n(ks[9], (DEPTH, MLA_Q_RANK, MLA_HEADS * MLA_QK), MLA_Q_RANK ** -0.5),
        "mla_kv_norm_g": 1.0 + n(ks[10], (DEPTH, MLA_KV_RANK), 0.02),
        "mla_w_ukv": n(ks[11], (DEPTH, MLA_KV_RANK, MLA_HEADS * (MLA_NOPE + MLA_DV)), MLA_KV_RANK ** -0.5),
        "mla_proj": n(ks[12], (DEPTH, MLA_VW, D_MODEL), MLA_VW ** -0.5),
        "w_out": n(ks[13], (DEPTH, D_MODEL, D_MODEL), D_MODEL ** -0.5),
        "final_norm_g": 1.0 + n(ks[14], (D_MODEL,), 0.02),
    }


def _fwd_reference(x, meta_tokens, norm_g, w_in, gla_gate_w, gla_gate_b, gla_norm_g, gla_proj,
              mla_q_norm_g, mla_w_uq, mla_kv_norm_g, mla_w_ukv, mla_proj, w_out, final_norm_g):
    B = x.shape[0]
    meta = jnp.broadcast_to(meta_tokens[None].astype(x.dtype), (B, N_META, D_MODEL))
    h = jnp.concatenate([meta, x], axis=1)
    L = h.shape[1]
    cuts = [int(c) for c in np.cumsum(SPLITS)[:-1]]
    for l in range(DEPTH):
        u = rms_norm(h, norm_g[l])
        proj = u @ w_in[l]
        (g_q, g_k, g_v, g_lr, g_z, m_cq, m_ckv, m_kr, m_z,
         gate_gla, gate_mla) = jnp.split(proj, cuts, axis=-1)

        q = g_q.reshape(B, L, GLA_HEADS, GLA_DK) * (GLA_DK ** -0.5)
        k = g_k.reshape(B, L, GLA_HEADS, GLA_DK)
        v = g_v.reshape(B, L, GLA_HEADS, GLA_DV)
        gk = jax.nn.log_sigmoid((g_lr @ gla_gate_w[l] + gla_gate_b[l]).astype(jnp.float32)) / GLA_GATE_NORMALIZER
        gk = gk.reshape(B, L, GLA_HEADS, GLA_DK)
        o_a = gla_chunked(q, k, v, gk)
        o_a = rms_norm(o_a, gla_norm_g[l]).reshape(B, L, GLA_VW).astype(h.dtype)
        y_a = (o_a * jax.nn.silu(g_z)) @ gla_proj[l]

        o_b = mla_attention(m_cq, m_ckv, m_kr, mla_q_norm_g[l], mla_w_uq[l],
                            mla_kv_norm_g[l], mla_w_ukv[l])
        y_b = (o_b * jax.nn.silu(m_z)) @ mla_proj[l]

        merged = jax.nn.sigmoid(gate_gla) * y_a + jax.nn.sigmoid(gate_mla) * y_b
        h = h + merged @ w_out[l]
    out = rms_norm(h, final_norm_g)
    return out[:, N_META:]


import jax as _jax
import jax.numpy as _jnp

TWIN_FORMAT = 'train_step'
FWD_PARAMS = ['x', 'meta_tokens', 'norm_g', 'w_in', 'gla_gate_w', 'gla_gate_b', 'gla_norm_g', 'gla_proj', 'mla_q_norm_g', 'mla_w_uq', 'mla_kv_norm_g', 'mla_w_ukv', 'mla_proj', 'w_out', 'final_norm_g']
TWIN_WEIGHTS = ['meta_tokens', 'norm_g', 'w_in', 'gla_gate_w', 'gla_gate_b', 'gla_norm_g', 'gla_proj', 'mla_q_norm_g', 'mla_w_uq', 'mla_kv_norm_g', 'mla_w_ukv', 'mla_proj', 'w_out', 'final_norm_g']
TWIN_DIFF_INPUT = 'x'
TWIN_INPUTS = ['x', 'meta_tokens', 'norm_g', 'w_in', 'gla_gate_w', 'gla_gate_b', 'gla_norm_g', 'gla_proj', 'mla_q_norm_g', 'mla_w_uq', 'mla_kv_norm_g', 'mla_w_ukv', 'mla_proj', 'w_out', 'final_norm_g', 'loss_target', 'm_meta_tokens', 'm_norm_g', 'm_w_in', 'm_gla_gate_w', 'm_gla_gate_b', 'm_gla_norm_g', 'm_gla_proj', 'm_mla_q_norm_g', 'm_mla_w_uq', 'm_mla_kv_norm_g', 'm_mla_w_ukv', 'm_mla_proj', 'm_w_out', 'm_final_norm_g', 'v_meta_tokens', 'v_norm_g', 'v_w_in', 'v_gla_gate_w', 'v_gla_gate_b', 'v_gla_norm_g', 'v_gla_proj', 'v_mla_q_norm_g', 'v_mla_w_uq', 'v_mla_kv_norm_g', 'v_mla_w_ukv', 'v_mla_proj', 'v_w_out', 'v_final_norm_g']
TWIN_OUTPUTS = ['loss', 'grad_x', 'grad_meta_tokens', 'grad_norm_g', 'grad_w_in', 'grad_gla_gate_w', 'grad_gla_gate_b', 'grad_gla_norm_g', 'grad_gla_proj', 'grad_mla_q_norm_g', 'grad_mla_w_uq', 'grad_mla_kv_norm_g', 'grad_mla_w_ukv', 'grad_mla_proj', 'grad_w_out', 'grad_final_norm_g', 'delta_meta_tokens', 'delta_norm_g', 'delta_w_in', 'delta_gla_gate_w', 'delta_gla_gate_b', 'delta_gla_norm_g', 'delta_gla_proj', 'delta_mla_q_norm_g', 'delta_mla_w_uq', 'delta_mla_kv_norm_g', 'delta_mla_w_ukv', 'delta_mla_proj', 'delta_w_out', 'delta_final_norm_g', 'new_m_meta_tokens', 'new_m_norm_g', 'new_m_w_in', 'new_m_gla_gate_w', 'new_m_gla_gate_b', 'new_m_gla_norm_g', 'new_m_gla_proj', 'new_m_mla_q_norm_g', 'new_m_mla_w_uq', 'new_m_mla_kv_norm_g', 'new_m_mla_w_ukv', 'new_m_mla_proj', 'new_m_w_out', 'new_m_final_norm_g', 'new_v_meta_tokens', 'new_v_norm_g', 'new_v_w_in', 'new_v_gla_gate_w', 'new_v_gla_gate_b', 'new_v_gla_norm_g', 'new_v_gla_proj', 'new_v_mla_q_norm_g', 'new_v_mla_w_uq', 'new_v_mla_kv_norm_g', 'new_v_mla_w_ukv', 'new_v_mla_proj', 'new_v_w_out', 'new_v_final_norm_g']
TWIN_LEAF_KINDS = {'loss': 'loss', 'grad_x': 'grad_x', 'grad_meta_tokens': 'grad_w', 'grad_norm_g': 'grad_w', 'grad_w_in': 'grad_w', 'grad_gla_gate_w': 'grad_w', 'grad_gla_gate_b': 'grad_w', 'grad_gla_norm_g': 'grad_w', 'grad_gla_proj': 'grad_w', 'grad_mla_q_norm_g': 'grad_w', 'grad_mla_w_uq': 'grad_w', 'grad_mla_kv_norm_g': 'grad_w', 'grad_mla_w_ukv': 'grad_w', 'grad_mla_proj': 'grad_w', 'grad_w_out': 'grad_w', 'grad_final_norm_g': 'grad_w', 'delta_meta_tokens': 'delta_w', 'delta_norm_g': 'delta_w', 'delta_w_in': 'delta_w', 'delta_gla_gate_w': 'delta_w', 'delta_gla_gate_b': 'delta_w', 'delta_gla_norm_g': 'delta_w', 'delta_gla_proj': 'delta_w', 'delta_mla_q_norm_g': 'delta_w', 'delta_mla_w_uq': 'delta_w', 'delta_mla_kv_norm_g': 'delta_w', 'delta_mla_w_ukv': 'delta_w', 'delta_mla_proj': 'delta_w', 'delta_w_out': 'delta_w', 'delta_final_norm_g': 'delta_w', 'new_m_meta_tokens': 'new_m', 'new_m_norm_g': 'new_m', 'new_m_w_in': 'new_m', 'new_m_gla_gate_w': 'new_m', 'new_m_gla_gate_b': 'new_m', 'new_m_gla_norm_g': 'new_m', 'new_m_gla_proj': 'new_m', 'new_m_mla_q_norm_g': 'new_m', 'new_m_mla_w_uq': 'new_m', 'new_m_mla_kv_norm_g': 'new_m', 'new_m_mla_w_ukv': 'new_m', 'new_m_mla_proj': 'new_m', 'new_m_w_out': 'new_m', 'new_m_final_norm_g': 'new_m', 'new_v_meta_tokens': 'new_v', 'new_v_norm_g': 'new_v', 'new_v_w_in': 'new_v', 'new_v_gla_gate_w': 'new_v', 'new_v_gla_gate_b': 'new_v', 'new_v_gla_norm_g': 'new_v', 'new_v_gla_proj': 'new_v', 'new_v_mla_q_norm_g': 'new_v', 'new_v_mla_w_uq': 'new_v', 'new_v_mla_kv_norm_g': 'new_v', 'new_v_mla_w_ukv': 'new_v', 'new_v_mla_proj': 'new_v', 'new_v_w_out': 'new_v', 'new_v_final_norm_g': 'new_v'}


def _forward(args):
    return _fwd_reference(*[args[k] for k in FWD_PARAMS])


def _output_shape():
    out = _jax.eval_shape(lambda: _forward(_fwd_setup_inputs(0)))
    return out.shape, out.dtype

N_MICROBATCH = 1
ADAM_LR = 0.001
ADAM_B1 = 0.9
ADAM_B2 = 0.999
ADAM_EPS = 1e-08
ADAM_WD = 0.01
ADAM_STEP = 10
PER_EXAMPLE_BATCH_AXIS = {'x': 0, 'loss_target': 0}
SHARED_INPUTS = []
_WEIGHT_DTYPES = {'meta_tokens': _jnp.float32, 'norm_g': _jnp.float32, 'w_in': _jnp.float32, 'gla_gate_w': _jnp.float32, 'gla_gate_b': _jnp.float32, 'gla_norm_g': _jnp.float32, 'gla_proj': _jnp.float32, 'mla_q_norm_g': _jnp.float32, 'mla_w_uq': _jnp.float32, 'mla_kv_norm_g': _jnp.float32, 'mla_w_ukv': _jnp.float32, 'mla_proj': _jnp.float32, 'w_out': _jnp.float32, 'final_norm_g': _jnp.float32}
MOMENT_SCALE = {'meta_tokens': 5.588295e-03, 'norm_g': 1.707194e-01, 'w_in': 6.432089e-02, 'gla_gate_w': 1.226427e-02, 'gla_gate_b': 4.826277e-02, 'gla_norm_g': 1.858181e-01, 'gla_proj': 7.720932e-02, 'mla_q_norm_g': 2.572007e-02, 'mla_w_uq': 1.052569e-02, 'mla_kv_norm_g': 5.451866e-02, 'mla_w_ukv': 1.241253e-02, 'mla_proj': 1.388488e-02, 'w_out': 7.842144e-02, 'final_norm_g': 6.393442e+01}


def _to_microbatches(a, axis):
    t = _jnp.moveaxis(a, axis, 0)
    t = t.reshape((N_MICROBATCH, t.shape[0] // N_MICROBATCH) + t.shape[1:])
    return _jnp.moveaxis(t, 1, axis + 1)


def setup_inputs(seed: int = 0) -> dict:
    inp = _fwd_setup_inputs(seed)
    key = _jax.random.fold_in(_jax.random.key(seed), 7919)
    shape, _ = _output_shape()
    out = dict(inp)
    out["loss_target"] = _jax.random.normal(_jax.random.fold_in(key, 0), shape, _jnp.float32)
    for i, name in enumerate(TWIN_WEIGHTS):
        w = inp[name].astype(_jnp.float32)
        if MOMENT_SCALE is None:
            s = _jnp.sqrt(_jnp.mean(_jnp.square(w)) + 1e-30)
        else:
            s = MOMENT_SCALE[name]
        km, kv = _jax.random.split(_jax.random.fold_in(key, i + 1))
        out[name] = w
        out["m_" + name] = s * _jax.random.normal(km, w.shape, _jnp.float32)
        out["v_" + name] = (s * s) * _jax.random.uniform(kv, w.shape, _jnp.float32, 0.5, 1.5)
    if N_MICROBATCH > 1:
        for name, axis in PER_EXAMPLE_BATCH_AXIS.items():
            out[name] = _to_microbatches(out[name], axis)
    return {'x': out['x'], 'meta_tokens': out['meta_tokens'], 'norm_g': out['norm_g'], 'w_in': out['w_in'], 'gla_gate_w': out['gla_gate_w'], 'gla_gate_b': out['gla_gate_b'], 'gla_norm_g': out['gla_norm_g'], 'gla_proj': out['gla_proj'], 'mla_q_norm_g': out['mla_q_norm_g'], 'mla_w_uq': out['mla_w_uq'], 'mla_kv_norm_g': out['mla_kv_norm_g'], 'mla_w_ukv': out['mla_w_ukv'], 'mla_proj': out['mla_proj'], 'w_out': out['w_out'], 'final_norm_g': out['final_norm_g'], 'loss_target': out['loss_target'], 'm_meta_tokens': out['m_meta_tokens'], 'm_norm_g': out['m_norm_g'], 'm_w_in': out['m_w_in'], 'm_gla_gate_w': out['m_gla_gate_w'], 'm_gla_gate_b': out['m_gla_gate_b'], 'm_gla_norm_g': out['m_gla_norm_g'], 'm_gla_proj': out['m_gla_proj'], 'm_mla_q_norm_g': out['m_mla_q_norm_g'], 'm_mla_w_uq': out['m_mla_w_uq'], 'm_mla_kv_norm_g': out['m_mla_kv_norm_g'], 'm_mla_w_ukv': out['m_mla_w_ukv'], 'm_mla_proj': out['m_mla_proj'], 'm_w_out': out['m_w_out'], 'm_final_norm_g': out['m_final_norm_g'], 'v_meta_tokens': out['v_meta_tokens'], 'v_norm_g': out['v_norm_g'], 'v_w_in': out['v_w_in'], 'v_gla_gate_w': out['v_gla_gate_w'], 'v_gla_gate_b': out['v_gla_gate_b'], 'v_gla_norm_g': out['v_gla_norm_g'], 'v_gla_proj': out['v_gla_proj'], 'v_mla_q_norm_g': out['v_mla_q_norm_g'], 'v_mla_w_uq': out['v_mla_w_uq'], 'v_mla_kv_norm_g': out['v_mla_kv_norm_g'], 'v_mla_w_ukv': out['v_mla_w_ukv'], 'v_mla_proj': out['v_mla_proj'], 'v_w_out': out['v_w_out'], 'v_final_norm_g': out['v_final_norm_g']}


def _loss(weights, diff, rest, loss_target):
    with _jax.named_scope("forward"):
        args = {**rest, TWIN_DIFF_INPUT: diff, **{k: w.astype(_WEIGHT_DTYPES[k]) for k, w in weights.items()}}
        y = _forward(args)
    with _jax.named_scope("loss_head"):
        err = _jnp.square(y.astype(_jnp.float32) - loss_target)
        return 0.5 * _jnp.sum(_jnp.mean(err, axis=-1)) if err.ndim else 0.5 * err


def _adamw(w, g, m, v):
    m = ADAM_B1 * m + (1.0 - ADAM_B1) * g
    v = ADAM_B2 * v + (1.0 - ADAM_B2) * _jnp.square(g)
    m_hat = m / (1.0 - ADAM_B1 ** ADAM_STEP)
    v_hat = v / (1.0 - ADAM_B2 ** ADAM_STEP)
    delta = -ADAM_LR * (m_hat / (_jnp.sqrt(v_hat) + ADAM_EPS) + ADAM_WD * w)
    return delta, m, v


def reference(x, meta_tokens, norm_g, w_in, gla_gate_w, gla_gate_b, gla_norm_g, gla_proj, mla_q_norm_g, mla_w_uq, mla_kv_norm_g, mla_w_ukv, mla_proj, w_out, final_norm_g, loss_target, m_meta_tokens, m_norm_g, m_w_in, m_gla_gate_w, m_gla_gate_b, m_gla_norm_g, m_gla_proj, m_mla_q_norm_g, m_mla_w_uq, m_mla_kv_norm_g, m_mla_w_ukv, m_mla_proj, m_w_out, m_final_norm_g, v_meta_tokens, v_norm_g, v_w_in, v_gla_gate_w, v_gla_gate_b, v_gla_norm_g, v_gla_proj, v_mla_q_norm_g, v_mla_w_uq, v_mla_kv_norm_g, v_mla_w_ukv, v_mla_proj, v_w_out, v_final_norm_g):
    given = dict(x=x, meta_tokens=meta_tokens, norm_g=norm_g, w_in=w_in, gla_gate_w=gla_gate_w, gla_gate_b=gla_gate_b, gla_norm_g=gla_norm_g, gla_proj=gla_proj, mla_q_norm_g=mla_q_norm_g, mla_w_uq=mla_w_uq, mla_kv_norm_g=mla_kv_norm_g, mla_w_ukv=mla_w_ukv, mla_proj=mla_proj, w_out=w_out, final_norm_g=final_norm_g, loss_target=loss_target, m_meta_tokens=m_meta_tokens, m_norm_g=m_norm_g, m_w_in=m_w_in, m_gla_gate_w=m_gla_gate_w, m_gla_gate_b=m_gla_gate_b, m_gla_norm_g=m_gla_norm_g, m_gla_proj=m_gla_proj, m_mla_q_norm_g=m_mla_q_norm_g, m_mla_w_uq=m_mla_w_uq, m_mla_kv_norm_g=m_mla_kv_norm_g, m_mla_w_ukv=m_mla_w_ukv, m_mla_proj=m_mla_proj, m_w_out=m_w_out, m_final_norm_g=m_final_norm_g, v_meta_tokens=v_meta_tokens, v_norm_g=v_norm_g, v_w_in=v_w_in, v_gla_gate_w=v_gla_gate_w, v_gla_gate_b=v_gla_gate_b, v_gla_norm_g=v_gla_norm_g, v_gla_proj=v_gla_proj, v_mla_q_norm_g=v_mla_q_norm_g, v_mla_w_uq=v_mla_w_uq, v_mla_kv_norm_g=v_mla_kv_norm_g, v_mla_w_ukv=v_mla_w_ukv, v_mla_proj=v_mla_proj, v_w_out=v_w_out, v_final_norm_g=v_final_norm_g)
    weights = {n: given[n] for n in TWIN_WEIGHTS}
    shared = {n: given[n] for n in SHARED_INPUTS}
    per_example = {n: given[n] for n in ['x']}
    grad_fn = _jax.value_and_grad(_loss, argnums=(0, 1))

    def one_microbatch(ex, loss_target):
        ex = dict(ex)
        diff = ex.pop(TWIN_DIFF_INPUT)
        return grad_fn(weights, diff, {**shared, **ex}, loss_target)

    if N_MICROBATCH == 1:
        loss, (grad_w, grad_x) = one_microbatch(per_example, given["loss_target"])
    else:
        def body(carry, xs):
            loss_sum, grad_sum = carry
            l_k, (gw_k, gx_k) = one_microbatch(xs[0], xs[1])
            with _jax.named_scope("update"):
                return (loss_sum + l_k, _jax.tree.map(_jnp.add, grad_sum, gw_k)), gx_k

        init = (_jnp.zeros((), _jnp.float32), _jax.tree.map(_jnp.zeros_like, weights))
        (loss, grad_w), grad_x = _jax.lax.scan(body, init, (per_example, given["loss_target"]))
    with _jax.named_scope("update"):
        delta_w, new_m, new_v = {}, {}, {}
        for n in TWIN_WEIGHTS:
            delta_w[n], new_m[n], new_v[n] = _adamw(weights[n], grad_w[n], given["m_" + n], given["v_" + n])
    return (loss, grad_x, *[grad_w[n] for n in TWIN_WEIGHTS], *[delta_w[n] for n in TWIN_WEIGHTS],
            *[new_m[n] for n in TWIN_WEIGHTS], *[new_v[n] for n in TWIN_WEIGHTS])
```

```python
import functools

import jax
import jax.numpy as jnp
from jax import lax
from jax.experimental import pallas as pl
from jax.experimental.pallas import tpu as pltpu

F32 = jnp.float32
BF16 = jnp.bfloat16

D_MODEL = 1024
N_META = 16
EPS = 1e-6
FRONT = 48
X0 = FRONT + N_META
GLA_HEADS, GLA_DK, GLA_DV, GLA_RANK, GLA_CHUNK = 4, 128, 256, 16, 64
GLA_GATE_NORMALIZER = 16.0
GLA_KW = GLA_HEADS * GLA_DK
MLA_HEADS, MLA_NOPE, MLA_ROPE, MLA_DV, MLA_QR, MLA_KVR = 8, 128, 64, 128, 256, 128
MLA_QK = MLA_NOPE + MLA_ROPE
ROPE_BASE = 10000.0
LANE = 128
QKW = 2 * LANE

C_V, C_Z, C_MZ, C_GG, C_GM = 0, 1024, 2048, 3072, 4096
C_Q, C_K, C_CQ, C_CKV, C_KR, C_KROT, C_LR = 5120, 5632, 6144, 6400, 6528, 6656, 6784
N_EXT = 6912
O_Q, O_K, O_V, O_LR, O_Z, O_CQ, O_CKV, O_KR, O_MZ, O_GG, O_GM, N_IN = (
    0, 512, 1024, 2048, 2064, 3088, 3344, 3472, 3536, 4560, 5584, 6608)

ADAM_LR, ADAM_B1, ADAM_B2, ADAM_EPS, ADAM_WD, ADAM_STEP = 0.001, 0.9, 0.999, 1e-08, 0.01, 10

N_DEV = 8
TOK = 192
NEG = -1e30


def _cp(sems=None, vmem_mb=None):
    kw = {}
    if sems is not None:
        kw["dimension_semantics"] = sems
    if vmem_mb is not None:
        kw["vmem_limit_bytes"] = vmem_mb * 1024 * 1024
    return pltpu.CompilerParams(**kw)


def _dot(a, b):
    return jnp.dot(a, b, preferred_element_type=F32)


def _dot_nt(a, b):
    return lax.dot_general(a, b, (((1,), (1,)), ((), ())), preferred_element_type=F32)


def _dot_tn(a, b):
    return lax.dot_general(a, b, (((0,), (0,)), ((), ())), preferred_element_type=F32)


def _sigmoid(x):
    return 1.0 / (1.0 + jnp.exp(-x))


def _bf(x):
    return x.astype(BF16)


def _big_tok(tp):
    return 4 * TOK if tp % (4 * TOK) == 0 else TOK


def _proj_in(hp, norm_g, w_ext):
    tp = hp.shape[0]
    tm, tn = _big_tok(tp), 768

    def body(h_ref, g_ref, w_ref, u_ref, o_ref, u_scr):
        @pl.when(pl.program_id(1) == 0)
        def _():
            x = h_ref[...]
            r = lax.rsqrt(jnp.mean(x * x, axis=-1, keepdims=True) + EPS)
            u = _bf(x * r * g_ref[...])
            u_scr[...] = u
            u_ref[...] = u

        o_ref[...] = _bf(_dot(u_scr[...], w_ref[...]))

    return pl.pallas_call(
        body, name="proj_in", grid=(tp // tm, N_EXT // tn),
        in_specs=[pl.BlockSpec((tm, D_MODEL), lambda i, j: (i, 0)),
                  pl.BlockSpec((1, D_MODEL), lambda i, j: (0, 0)),
                  pl.BlockSpec((D_MODEL, tn), lambda i, j: (0, j))],
        out_specs=[pl.BlockSpec((tm, D_MODEL), lambda i, j: (i, 0)),
                   pl.BlockSpec((tm, tn), lambda i, j: (i, j))],
        out_shape=[jax.ShapeDtypeStruct((tp, D_MODEL), BF16), jax.ShapeDtypeStruct((tp, N_EXT), BF16)],
        scratch_shapes=[pltpu.VMEM((tm, D_MODEL), BF16)],
        compiler_params=_cp(("parallel", "arbitrary"), 48),
    )(hp, norm_g, w_ext)


def _gla_chunk(q_ref, k_ref, lr_ref, gw_ref, gb_ref, n):
    z = _dot(lr_ref[...], gw_ref[...]) + gb_ref[...]
    logsig = jnp.minimum(z, 0.0) - jnp.log(1.0 + jnp.exp(-jnp.abs(z)))
    row = lax.broadcasted_iota(jnp.int32, (GLA_CHUNK, GLA_DK), 0)
    live = jnp.logical_or(n > 0, row >= FRONT)
    g = jnp.where(live, logsig * (1.0 / GLA_GATE_NORMALIZER), 0.0)
    ri = lax.broadcasted_iota(jnp.int32, (GLA_CHUNK, GLA_CHUNK), 0)
    ci = lax.broadcasted_iota(jnp.int32, (GLA_CHUNK, GLA_CHUNK), 1)
    tril = ci <= ri
    b = jnp.dot(tril.astype(F32), g, precision=lax.Precision.HIGHEST, preferred_element_type=F32)
    bl = jnp.sum(jnp.where(row == GLA_CHUNK - 1, b, 0.0), axis=0, keepdims=True)
    eb, enb, elb, ebl = jnp.exp(b), jnp.exp(-b), jnp.exp(bl - b), jnp.exp(bl)
    q = q_ref[...].astype(F32) * (GLA_DK ** -0.5)
    k = k_ref[...].astype(F32)
    qe, ke, kl = q * eb, k * enb, k * elb
    a = jnp.where(tril, _dot_nt(_bf(qe), _bf(ke)), 0.0)
    return dict(z=z, live=live, tril=tril, eb=eb, enb=enb, elb=elb, ebl=ebl, qe=qe, ke=ke, kl=kl, a=a)


def _gla_specs(b_count, n_chunks):
    def rows(b, h, n):
        return b * n_chunks + n
    return rows


def _gla_fwd(proj, gw_pad, gate_b, gla_norm_g, bsz, lp):
    n_chunks = lp // GLA_CHUNK
    tp = bsz * lp

    def body(q_ref, k_ref, v_ref, z_ref, lr_ref, gw_ref, gb_ref, gn_ref, oraw_ref, ya_ref, sall_ref, st_scr):
        n = pl.program_id(2)

        @pl.when(n == 0)
        def _():
            st_scr[...] = jnp.zeros_like(st_scr)

        st = st_scr[...]
        sall_ref[0, 0, 0] = st
        c = _gla_chunk(q_ref, k_ref, lr_ref, gw_ref, gb_ref, n)
        v = v_ref[...]
        o = _dot(_bf(c["a"]), v) + _dot_nt(_bf(c["qe"]), _bf(st))
        st_scr[...] = st * c["ebl"] + _dot_tn(v, _bf(c["kl"]))
        oraw_ref[...] = o
        r = lax.rsqrt(jnp.mean(o * o, axis=-1, keepdims=True) + EPS)
        zg = z_ref[...].astype(F32)
        ya_ref[...] = _bf((o * r * gn_ref[...]) * (zg * _sigmoid(zg)))

    def rb(b, h, n):
        return b * n_chunks + n

    return pl.pallas_call(
        body, name="gla_fwd", grid=(bsz, GLA_HEADS, n_chunks),
        in_specs=[pl.BlockSpec((GLA_CHUNK, GLA_DK), lambda b, h, n: (rb(b, h, n), C_Q // GLA_DK + h)),
                  pl.BlockSpec((GLA_CHUNK, GLA_DK), lambda b, h, n: (rb(b, h, n), C_K // GLA_DK + h)),
                  pl.BlockSpec((GLA_CHUNK, GLA_DV), lambda b, h, n: (rb(b, h, n), C_V // GLA_DV + h)),
                  pl.BlockSpec((GLA_CHUNK, GLA_DV), lambda b, h, n: (rb(b, h, n), C_Z // GLA_DV + h)),
                  pl.BlockSpec((GLA_CHUNK, LANE), lambda b, h, n: (rb(b, h, n), C_LR // LANE)),
                  pl.BlockSpec((LANE, GLA_DK), lambda b, h, n: (0, h)),
                  pl.BlockSpec((1, GLA_DK), lambda b, h, n: (0, h)),
                  pl.BlockSpec((1, GLA_DV), lambda b, h, n: (0, 0))],
        out_specs=[pl.BlockSpec((GLA_CHUNK, GLA_DV), lambda b, h, n: (rb(b, h, n), h)),
                   pl.BlockSpec((GLA_CHUNK, GLA_DV), lambda b, h, n: (rb(b, h, n), h)),
                   pl.BlockSpec((1, 1, 1, GLA_DV, GLA_DK), lambda b, h, n: (b, h, n, 0, 0))],
        out_shape=[jax.ShapeDtypeStruct((tp, GLA_HEADS * GLA_DV), F32),
                   jax.ShapeDtypeStruct((tp, GLA_HEADS * GLA_DV), BF16),
                   jax.ShapeDtypeStruct((bsz, GLA_HEADS, n_chunks, GLA_DV, GLA_DK), F32)],
        scratch_shapes=[pltpu.VMEM((GLA_DV, GLA_DK), F32)],
        compiler_params=_cp(("parallel", "parallel", "arbitrary")),
    )(proj, proj, proj, proj, proj, gw_pad, gate_b, gla_norm_g)


def _gla_bwd(proj, gw_pad, gate_b, gla_norm_g, o_raw, s_all, d_ya, bsz, lp):
    n_chunks = lp // GLA_CHUNK
    tp = bsz * lp

    def body(q_ref, k_ref, v_ref, z_ref, lr_ref, gw_ref, gb_ref, gn_ref, o_ref, s_ref, dya_ref,
             dq_ref, dk_ref, dv_ref, dzg_ref, dz_ref, dgn_ref, dst_scr):
        first = jnp.logical_and(jnp.logical_and(pl.program_id(0) == 0, pl.program_id(1) == 0), pl.program_id(2) == 0)

        @pl.when(first)
        def _():
            dgn_ref[...] = jnp.zeros_like(dgn_ref)

        @pl.when(pl.program_id(2) == 0)
        def _():
            dst_scr[...] = jnp.zeros_like(dst_scr)

        n = n_chunks - 1 - pl.program_id(2)
        c = _gla_chunk(q_ref, k_ref, lr_ref, gw_ref, gb_ref, n)
        v = v_ref[...]
        st = s_ref[0, 0, 0]
        dst = dst_scr[...]
        o = o_ref[...]
        r = lax.rsqrt(jnp.mean(o * o, axis=-1, keepdims=True) + EPS)
        xh = o * r
        zg = z_ref[...].astype(F32)
        sg = _sigmoid(zg)
        dy = dya_ref[...].astype(F32)
        gn = gn_ref[...]
        dzg_ref[...] = _bf(dy * (xh * gn) * (sg * (1.0 + zg * (1.0 - sg))))
        t = dy * (zg * sg)
        dgn_ref[...] += jnp.sum(t * xh, axis=0, keepdims=True)
        dxh = t * gn
        do = r * (dxh - xh * jnp.mean(dxh * xh, axis=-1, keepdims=True))
        do_b = _bf(do)
        qe_b, ke_b, kl_b, dst_b = _bf(c["qe"]), _bf(c["ke"]), _bf(c["kl"]), _bf(dst)
        da = jnp.where(c["tril"], _dot_nt(do_b, v), 0.0)
        da_b = _bf(da)
        dqe = _dot(da_b, ke_b) + _dot(do_b, _bf(st))
        dke = _dot_tn(da_b, qe_b)
        dkl = _dot(v, dst_b)
        dv_ref[...] = _bf(_dot_tn(_bf(c["a"]), do_b) + _dot_nt(kl_b, dst_b))
        ddecay = jnp.sum(dst * st, axis=0, keepdims=True)
        dbl = jnp.sum(dkl * c["kl"], axis=0, keepdims=True) + ddecay * c["ebl"]
        row = lax.broadcasted_iota(jnp.int32, (GLA_CHUNK, GLA_DK), 0)
        db = dqe * c["qe"] - dke * c["ke"] - dkl * c["kl"] + jnp.where(row == GLA_CHUNK - 1, dbl, 0.0)
        triu = jnp.logical_not(c["tril"]) | (lax.broadcasted_iota(jnp.int32, (GLA_CHUNK, GLA_CHUNK), 0)
                                              == lax.broadcasted_iota(jnp.int32, (GLA_CHUNK, GLA_CHUNK), 1))
        dg = jnp.dot(triu.astype(F32), db, precision=lax.Precision.HIGHEST, preferred_element_type=F32)
        dg = jnp.where(c["live"], dg, 0.0)
        dz_ref[...] = dg * (1.0 / GLA_GATE_NORMALIZER) * _sigmoid(-c["z"])
        dq_ref[...] = _bf(dqe * c["eb"] * (GLA_DK ** -0.5))
        dk_ref[...] = _bf(dke * c["enb"] + dkl * c["elb"])
        dst_scr[...] = dst * c["ebl"] + _dot_tn(do_b, qe_b)

    def rb(b, h, n):
        return b * n_chunks + (n_chunks - 1 - n)

    return pl.pallas_call(
        body, name="gla_bwd", grid=(bsz, GLA_HEADS, n_chunks),
        in_specs=[pl.BlockSpec((GLA_CHUNK, GLA_DK), lambda b, h, n: (rb(b, h, n), C_Q // GLA_DK + h)),
                  pl.BlockSpec((GLA_CHUNK, GLA_DK), lambda b, h, n: (rb(b, h, n), C_K // GLA_DK + h)),
                  pl.BlockSpec((GLA_CHUNK, GLA_DV), lambda b, h, n: (rb(b, h, n), C_V // GLA_DV + h)),
                  pl.BlockSpec((GLA_CHUNK, GLA_DV), lambda b, h, n: (rb(b, h, n), C_Z // GLA_DV + h)),
                  pl.BlockSpec((GLA_CHUNK, LANE), lambda b, h, n: (rb(b, h, n), C_LR // LANE)),
                  pl.BlockSpec((LANE, GLA_DK), lambda b, h, n: (0, h)),
                  pl.BlockSpec((1, GLA_DK), lambda b, h, n: (0, h)),
                  pl.BlockSpec((1, GLA_DV), lambda b, h, n: (0, 0)),
                  pl.BlockSpec((GLA_CHUNK, GLA_DV), lambda b, h, n: (rb(b, h, n), h)),
                  pl.BlockSpec((1, 1, 1, GLA_DV, GLA_DK), lambda b, h, n: (b, h, n_chunks - 1 - n, 0, 0)),
                  pl.BlockSpec((GLA_CHUNK, GLA_DV), lambda b, h, n: (rb(b, h, n), h))],
        out_specs=[pl.BlockSpec((GLA_CHUNK, GLA_DK), lambda b, h, n: (rb(b, h, n), h)),
                   pl.BlockSpec((GLA_CHUNK, GLA_DK), lambda b, h, n: (rb(b, h, n), h)),
                   pl.BlockSpec((GLA_CHUNK, GLA_DV), lambda b, h, n: (rb(b, h, n), h)),
                   pl.BlockSpec((GLA_CHUNK, GLA_DV), lambda b, h, n: (rb(b, h, n), h)),
                   pl.BlockSpec((GLA_CHUNK, GLA_DK), lambda b, h, n: (rb(b, h, n), h)),
                   pl.BlockSpec((1, GLA_DV), lambda b, h, n: (0, 0))],
        out_shape=[jax.ShapeDtypeStruct((tp, GLA_KW), BF16), jax.ShapeDtypeStruct((tp, GLA_KW), BF16),
                   jax.ShapeDtypeStruct((tp, GLA_HEADS * GLA_DV), BF16),
                   jax.ShapeDtypeStruct((tp, GLA_HEADS * GLA_DV), BF16),
                   jax.ShapeDtypeStruct((tp, GLA_KW), F32), jax.ShapeDtypeStruct((1, GLA_DV), F32)],
        scratch_shapes=[pltpu.VMEM((GLA_DV, GLA_DK), F32)],
        compiler_params=_cp(("arbitrary", "arbitrary", "arbitrary")),
    )(proj, proj, proj, proj, proj, gw_pad, gate_b, gla_norm_g, o_raw, s_all, d_ya)


def _gate_bwd(dz, proj, gw_pad_t):
    tp = dz.shape[0]
    tm = _big_tok(tp)

    def body(dz_ref, lr_ref, gwt_ref, dlr_ref, dgw_ref, dgb_ref):
        @pl.when(pl.program_id(0) == 0)
        def _():
            dgw_ref[...] = jnp.zeros_like(dgw_ref)
            dgb_ref[...] = jnp.zeros_like(dgb_ref)

        dz = dz_ref[...]
        dz_b = _bf(dz)
        dlr_ref[...] = _bf(_dot(dz_b, gwt_ref[...]))
        dgw_ref[...] += _dot_tn(lr_ref[...], dz_b)
        dgb_ref[...] += jnp.sum(dz, axis=0, keepdims=True)

    return pl.pallas_call(
        body, name="gate_bwd", grid=(tp // tm,),
        in_specs=[pl.BlockSpec((tm, GLA_KW), lambda i: (i, 0)),
                  pl.BlockSpec((tm, LANE), lambda i: (i, C_LR // LANE)),
                  pl.BlockSpec((GLA_KW, LANE), lambda i: (0, 0))],
        out_specs=[pl.BlockSpec((tm, LANE), lambda i: (i, 0)),
                   pl.BlockSpec((LANE, GLA_KW), lambda i: (0, 0)),
                   pl.BlockSpec((1, GLA_KW), lambda i: (0, 0))],
        out_shape=[jax.ShapeDtypeStruct((tp, LANE), BF16), jax.ShapeDtypeStruct((LANE, GLA_KW), F32),
                   jax.ShapeDtypeStruct((1, GLA_KW), F32)],
        compiler_params=_cp(("arbitrary",)),
    )(dz, proj, gw_pad_t)


def _rms_fwd(x, g):
    r = lax.rsqrt(jnp.mean(x * x, axis=-1, keepdims=True) + EPS)
    return x * r, r


def _rms_bwd(dy, xh, r, g):
    dxh = dy * g
    dx = r * (dxh - xh * jnp.mean(dxh * xh, axis=-1, keepdims=True))
    return dx, jnp.sum(dy * xh, axis=0, keepdims=True)


def _q_up(proj, q_norm_g, wn, wr, wt, cos_t, sin_t, bsz, lp):
    tp = bsz * lp
    nb = lp // TOK

    def body(cq_ref, g_ref, wn_ref, wr_ref, wt_ref, cos_ref, sin_ref, q_ref):
        xh, _ = _rms_fwd(cq_ref[...].astype(F32), None)
        cqn = _bf(xh * g_ref[...])
        nope = _dot(cqn, wn_ref[...])
        rope = _dot(cqn, wr_ref[...])
        rot = _dot(cqn, wt_ref[...])
        cos, sin = cos_ref[...], sin_ref[...]
        for h in range(MLA_HEADS):
            sl = slice(h * LANE, (h + 1) * LANE)
            q_ref[:, h * QKW:h * QKW + LANE] = _bf(nope[:, sl])
            q_ref[:, h * QKW + LANE:(h + 1) * QKW] = _bf(rope[:, sl] * cos + rot[:, sl] * sin)

    wspec = pl.BlockSpec((MLA_QR, MLA_HEADS * LANE), lambda b, i: (0, 0))
    tspec = pl.BlockSpec((TOK, LANE), lambda b, i: (i, 0))
    return pl.pallas_call(
        body, name="mla_q_up", grid=(bsz, nb),
        in_specs=[pl.BlockSpec((TOK, MLA_QR), lambda b, i: (b * nb + i, C_CQ // MLA_QR)),
                  pl.BlockSpec((1, MLA_QR), lambda b, i: (0, 0)), wspec, wspec, wspec, tspec, tspec],
        out_specs=pl.BlockSpec((TOK, MLA_HEADS * QKW), lambda b, i: (b * nb + i, 0)),
        out_shape=jax.ShapeDtypeStruct((tp, MLA_HEADS * QKW), BF16),
        compiler_params=_cp(("parallel", "parallel")),
    )(proj, q_norm_g, wn, wr, wt, cos_t, sin_t)


def _kv_up(proj, kv_norm_g, wk, wv, cos_t, sin_t, bsz, lp):
    tp = bsz * lp
    nb = lp // TOK

    def body(ckv_ref, kr_ref, krot_ref, g_ref, wk_ref, wv_ref, cos_ref, sin_ref, k_ref, v_ref):
        xh, _ = _rms_fwd(ckv_ref[...].astype(F32), None)
        cn = _bf(xh * g_ref[...])
        kn = _dot(cn, wk_ref[...])
        v_ref[...] = _bf(_dot(cn, wv_ref[...]))
        kr = _bf(kr_ref[...].astype(F32) * cos_ref[...] + krot_ref[...].astype(F32) * sin_ref[...])
        for h in range(MLA_HEADS):
            k_ref[:, h * QKW:h * QKW + LANE] = _bf(kn[:, h * LANE:(h + 1) * LANE])
            k_ref[:, h * QKW + LANE:(h + 1) * QKW] = kr

    wspec = pl.BlockSpec((MLA_KVR, MLA_HEADS * LANE), lambda b, i: (0, 0))
    tspec = pl.BlockSpec((TOK, LANE), lambda b, i: (i, 0))
    return pl.pallas_call(
        body, name="mla_kv_up", grid=(bsz, nb),
        in_specs=[pl.BlockSpec((TOK, LANE), lambda b, i: (b * nb + i, C_CKV // LANE)),
                  pl.BlockSpec((TOK, LANE), lambda b, i: (b * nb + i, C_KR // LANE)),
                  pl.BlockSpec((TOK, LANE), lambda b, i: (b * nb + i, C_KROT // LANE)),
                  pl.BlockSpec((1, MLA_KVR), lambda b, i: (0, 0)), wspec, wspec, tspec, tspec],
        out_specs=[pl.BlockSpec((TOK, MLA_HEADS * QKW), lambda b, i: (b * nb + i, 0)),
                   pl.BlockSpec((TOK, MLA_HEADS * LANE), lambda b, i: (b * nb + i, 0))],
        out_shape=[jax.ShapeDtypeStruct((tp, MLA_HEADS * QKW), BF16),
                   jax.ShapeDtypeStruct((tp, MLA_HEADS * LANE), BF16)],
        compiler_params=_cp(("parallel", "parallel")),
    )(proj, proj, proj, kv_norm_g, wk, wv, cos_t, sin_t)


def _attn_mask(i, j):
    qpos = i * TOK + lax.broadcasted_iota(jnp.int32, (TOK, TOK), 0)
    kpos = j * TOK + lax.broadcasted_iota(jnp.int32, (TOK, TOK), 1)
    return jnp.logical_and(kpos <= qpos, jnp.logical_or(kpos >= FRONT, qpos < FRONT))


ATT_SCALE = MLA_QK ** -0.5


def _flash_fwd(qf, kf, vf, proj, bsz, lp):
    tp = bsz * lp
    nb = lp // TOK

    def body(q_ref, k_ref, v_ref, mz_ref, ob_ref, yb_ref, lse_ref):
        i = pl.program_id(2)
        q = q_ref[...]

        def step(j, carry):
            m, l, acc = carry
            off = pl.multiple_of(j * TOK, TOK)
            kj = k_ref[pl.ds(off, TOK), :]
            vj = v_ref[pl.ds(off, TOK), :]
            s = jnp.where(_attn_mask(i, j), _dot_nt(q, kj) * ATT_SCALE, NEG)
            m_new = jnp.maximum(m, jnp.max(s, axis=-1, keepdims=True))
            p = jnp.exp(s - m_new)
            alpha = jnp.exp(m - m_new)
            return m_new, alpha * l + jnp.sum(p, axis=-1, keepdims=True), alpha * acc + _dot(_bf(p), vj)

        m, l, acc = lax.fori_loop(0, i + 1, step, (jnp.full((TOK, 1), NEG, F32), jnp.zeros((TOK, 1), F32),
                                                   jnp.zeros((TOK, MLA_DV), F32)))
        o = acc / l
        ob_ref[...] = _bf(o)
        mz = mz_ref[...].astype(F32)
        yb_ref[...] = _bf(o * (mz * _sigmoid(mz)))
        lse_ref[0, 0] = jnp.broadcast_to(m + jnp.log(l), (TOK, LANE))

    return pl.pallas_call(
        body, name="mla_flash_fwd", grid=(bsz, MLA_HEADS, nb),
        in_specs=[pl.BlockSpec((TOK, QKW), lambda b, h, i: (b * nb + i, h)),
                  pl.BlockSpec((lp, QKW), lambda b, h, i: (b, h)),
                  pl.BlockSpec((lp, MLA_DV), lambda b, h, i: (b, h)),
                  pl.BlockSpec((TOK, MLA_DV), lambda b, h, i: (b * nb + i, C_MZ // MLA_DV + h))],
        out_specs=[pl.BlockSpec((TOK, MLA_DV), lambda b, h, i: (b * nb + i, h)),
                   pl.BlockSpec((TOK, MLA_DV), lambda b, h, i: (b * nb + i, h)),
                   pl.BlockSpec((1, 1, TOK, LANE), lambda b, h, i: (b, h, i, 0))],
        out_shape=[jax.ShapeDtypeStruct((tp, MLA_HEADS * MLA_DV), BF16),
                   jax.ShapeDtypeStruct((tp, MLA_HEADS * MLA_DV), BF16),
                   jax.ShapeDtypeStruct((bsz, MLA_HEADS, lp, LANE), F32)],
        compiler_params=_cp(("parallel", "parallel", "arbitrary")),
    )(qf, kf, vf, proj)


def _flash_bwd_pre(d_yb, proj, o_b, bsz, lp):
    tp = bsz * lp
    nb = lp // TOK
    w = MLA_HEADS * MLA_DV

    def body(dy_ref, mz_ref, o_ref, do_ref, dmz_ref, dl_ref):
        dy = dy_ref[...].astype(F32)
        mz = mz_ref[...].astype(F32)
        o = o_ref[...].astype(F32)
        s = _sigmoid(mz)
        do = _bf(dy * (mz * s))
        do_ref[...] = do
        dmz_ref[...] = _bf(dy * o * (s * (1.0 + mz * (1.0 - s))))
        prod = do.astype(F32) * o
        for h in range(MLA_HEADS):
            dl = jnp.sum(prod[:, h * MLA_DV:(h + 1) * MLA_DV], axis=-1, keepdims=True)
            dl_ref[0, h] = jnp.broadcast_to(dl, (TOK, LANE))

    return pl.pallas_call(
        body, name="mla_flash_bwd_pre", grid=(bsz, nb),
        in_specs=[pl.BlockSpec((TOK, w), lambda b, i: (b * nb + i, 0)),
                  pl.BlockSpec((TOK, w), lambda b, i: (b * nb + i, C_MZ // w)),
                  pl.BlockSpec((TOK, w), lambda b, i: (b * nb + i, 0))],
        out_specs=[pl.BlockSpec((TOK, w), lambda b, i: (b * nb + i, 0)),
                   pl.BlockSpec((TOK, w), lambda b, i: (b * nb + i, 0)),
                   pl.BlockSpec((1, MLA_HEADS, TOK, LANE), lambda b, i: (b, 0, i, 0))],
        out_shape=[jax.ShapeDtypeStruct((tp, w), BF16), jax.ShapeDtypeStruct((tp, w), BF16),
                   jax.ShapeDtypeStruct((bsz, MLA_HEADS, lp, LANE), F32)],
        compiler_params=_cp(("parallel", "parallel")),
    )(d_yb, proj, o_b)


def _flash_bwd_dq(qf, kf, vf, d_o, lse, delta, bsz, lp):
    tp = bsz * lp
    nb = lp // TOK

    def body(q_ref, k_ref, v_ref, do_ref, lse_ref, dl_ref, dq_ref):
        i = pl.program_id(2)
        q = q_ref[...]
        do = do_ref[...]
        lse_c = lse_ref[0, 0][:, :1]
        dl_c = dl_ref[0, 0][:, :1]

        def step(j, dq):
            off = pl.multiple_of(j * TOK, TOK)
            kj = k_ref[pl.ds(off, TOK), :]
            vj = v_ref[pl.ds(off, TOK), :]
            s = _dot_nt(q, kj) * ATT_SCALE
            p = jnp.where(_attn_mask(i, j), jnp.exp(s - lse_c), 0.0)
            ds = p * (_dot_nt(do, vj) - dl_c) * ATT_SCALE
            return dq + _dot(_bf(ds), kj)

        dq_ref[...] = _bf(lax.fori_loop(0, i + 1, step, jnp.zeros((TOK, QKW), F32)))

    return pl.pallas_call(
        body, name="mla_flash_bwd_dq", grid=(bsz, MLA_HEADS, nb),
        in_specs=[pl.BlockSpec((TOK, QKW), lambda b, h, i: (b * nb + i, h)),
                  pl.BlockSpec((lp, QKW), lambda b, h, i: (b, h)),
                  pl.BlockSpec((lp, MLA_DV), lambda b, h, i: (b, h)),
                  pl.BlockSpec((TOK, MLA_DV), lambda b, h, i: (b * nb + i, h)),
                  pl.BlockSpec((1, 1, TOK, LANE), lambda b, h, i: (b, h, i, 0)),
                  pl.BlockSpec((1, 1, TOK, LANE), lambda b, h, i: (b, h, i, 0))],
        out_specs=pl.BlockSpec((TOK, QKW), lambda b, h, i: (b * nb + i, h)),
        out_shape=jax.ShapeDtypeStruct((tp, MLA_HEADS * QKW), BF16),
        compiler_params=_cp(("parallel", "parallel", "arbitrary")),
    )(qf, kf, vf, d_o, lse, delta)


def _flash_bwd_dkv(qf, kf, vf, d_o, lse, delta, bsz, lp):
    tp = bsz * lp
    nb = lp // TOK

    def body(q_ref, k_ref, v_ref, do_ref, lse_ref, dl_ref, dk_ref, dv_ref):
        j = pl.program_id(2)
        k = k_ref[...]
        v = v_ref[...]

        def step(i, carry):
            dk, dv = carry
            off = pl.multiple_of(i * TOK, TOK)
            qi = q_ref[pl.ds(off, TOK), :]
            doi = do_ref[pl.ds(off, TOK), :]
            lse_c = lse_ref[0, 0, pl.ds(off, TOK), :][:, :1]
            dl_c = dl_ref[0, 0, pl.ds(off, TOK), :][:, :1]
            s = _dot_nt(qi, k) * ATT_SCALE
            p = jnp.where(_attn_mask(i, j), jnp.exp(s - lse_c), 0.0)
            dv = dv + _dot_tn(_bf(p), doi)
            ds = p * (_dot_nt(doi, v) - dl_c) * ATT_SCALE
            return dk + _dot_tn(_bf(ds), qi), dv

        dk, dv = lax.fori_loop(j, nb, step, (jnp.zeros((TOK, QKW), F32), jnp.zeros((TOK, MLA_DV), F32)))
        dk_ref[...] = _bf(dk)
        dv_ref[...] = _bf(dv)

    return pl.pallas_call(
        body, name="mla_flash_bwd_dkv", grid=(bsz, MLA_HEADS, nb),
        in_specs=[pl.BlockSpec((lp, QKW), lambda b, h, j: (b, h)),
                  pl.BlockSpec((TOK, QKW), lambda b, h, j: (b * nb + j, h)),
                  pl.BlockSpec((TOK, MLA_DV), lambda b, h, j: (b * nb + j, h)),
                  pl.BlockSpec((lp, MLA_DV), lambda b, h, j: (b, h)),
                  pl.BlockSpec((1, 1, lp, LANE), lambda b, h, j: (b, h, 0, 0)),
                  pl.BlockSpec((1, 1, lp, LANE), lambda b, h, j: (b, h, 0, 0))],
        out_specs=[pl.BlockSpec((TOK, QKW), lambda b, h, j: (b * nb + j, h)),
                   pl.BlockSpec((TOK, MLA_DV), lambda b, h, j: (b * nb + j, h))],
        out_shape=[jax.ShapeDtypeStruct((tp, MLA_HEADS * QKW), BF16),
                   jax.ShapeDtypeStruct((tp, MLA_HEADS * MLA_DV), BF16)],
        compiler_params=_cp(("parallel", "parallel", "arbitrary")),
    )(qf, kf, vf, d_o, lse, delta)


def _q_up_bwd(dqf, proj, q_norm_g, wn_t, wr_t, wt_t, cos_t, sin_t, bsz, lp):
    tp = bsz * lp
    nb = lp // TOK
    hw = MLA_HEADS * LANE

    def body(dq_ref, cq_ref, g_ref, wn_ref, wr_ref, wt_ref, cos_ref, sin_ref,
             dcq_ref, dwn_ref, dwr_ref, dwt_ref, dg_ref):
        @pl.when(jnp.logical_and(pl.program_id(0) == 0, pl.program_id(1) == 0))
        def _():
            for r in (dwn_ref, dwr_ref, dwt_ref, dg_ref):
                r[...] = jnp.zeros_like(r)

        g = g_ref[...]
        xh, r = _rms_fwd(cq_ref[...].astype(F32), None)
        cqn = _bf(xh * g)
        cos, sin = cos_ref[...], sin_ref[...]
        dcqn = jnp.zeros((TOK, MLA_QR), F32)
        for h in range(MLA_HEADS):
            sl = slice(h * LANE, (h + 1) * LANE)
            dn = dq_ref[:, h * QKW:h * QKW + LANE]
            dr = dq_ref[:, h * QKW + LANE:(h + 1) * QKW].astype(F32)
            dr_c, dr_s = _bf(dr * cos), _bf(dr * sin)
            dcqn += _dot(dn, wn_ref[sl, :]) + _dot(dr_c, wr_ref[sl, :]) + _dot(dr_s, wt_ref[sl, :])
            dwn_ref[:, sl] += _dot_tn(cqn, dn)
            dwr_ref[:, sl] += _dot_tn(cqn, dr_c)
            dwt_ref[:, sl] += _dot_tn(cqn, dr_s)
        dx, dg = _rms_bwd(dcqn, xh, r, g)
        dcq_ref[...] = _bf(dx)
        dg_ref[...] += dg

    wspec = pl.BlockSpec((hw, MLA_QR), lambda b, i: (0, 0))
    aspec = pl.BlockSpec((MLA_QR, hw), lambda b, i: (0, 0))
    tspec = pl.BlockSpec((TOK, LANE), lambda b, i: (i, 0))
    return pl.pallas_call(
        body, name="mla_q_up_bwd", grid=(bsz, nb),
        in_specs=[pl.BlockSpec((TOK, MLA_HEADS * QKW), lambda b, i: (b * nb + i, 0)),
                  pl.BlockSpec((TOK, MLA_QR), lambda b, i: (b * nb + i, C_CQ // MLA_QR)),
                  pl.BlockSpec((1, MLA_QR), lambda b, i: (0, 0)), wspec, wspec, wspec, tspec, tspec],
        out_specs=[pl.BlockSpec((TOK, MLA_QR), lambda b, i: (b * nb + i, 0)), aspec, aspec, aspec,
                   pl.BlockSpec((1, MLA_QR), lambda b, i: (0, 0))],
        out_shape=[jax.ShapeDtypeStruct((tp, MLA_QR), BF16)] + [jax.ShapeDtypeStruct((MLA_QR, hw), F32)] * 3
        + [jax.ShapeDtypeStruct((1, MLA_QR), F32)],
        compiler_params=_cp(("arbitrary", "arbitrary")),
    )(dqf, proj, q_norm_g, wn_t, wr_t, wt_t, cos_t, sin_t)


def _kv_up_bwd(dkf, dvf, proj, kv_norm_g, wk_t, wv_t, cos_t, sin_t, bsz, lp):
    tp = bsz * lp
    nb = lp // TOK
    hw = MLA_HEADS * LANE

    def body(dk_ref, dv_ref, ckv_ref, g_ref, wk_ref, wv_ref, cos_ref, sin_ref,
             dckv_ref, dkr_ref, dkrot_ref, dwk_ref, dwv_ref, dg_ref):
        @pl.when(jnp.logical_and(pl.program_id(0) == 0, pl.program_id(1) == 0))
        def _():
            for r in (dwk_ref, dwv_ref, dg_ref):
                r[...] = jnp.zeros_like(r)

        g = g_ref[...]
        xh, r = _rms_fwd(ckv_ref[...].astype(F32), None)
        cn = _bf(xh * g)
        dv = dv_ref[...]
        dcn = _dot(dv, wv_ref[...])
        dwv_ref[...] += _dot_tn(cn, dv)
        drope = jnp.zeros((TOK, LANE), F32)
        for h in range(MLA_HEADS):
            sl = slice(h * LANE, (h + 1) * LANE)
            dn = dk_ref[:, h * QKW:h * QKW + LANE]
            drope += dk_ref[:, h * QKW + LANE:(h + 1) * QKW].astype(F32)
            dcn += _dot(dn, wk_ref[sl, :])
            dwk_ref[:, sl] += _dot_tn(cn, dn)
        dkr_ref[...] = _bf(drope * cos_ref[...])
        dkrot_ref[...] = _bf(drope * sin_ref[...])
        dx, dg = _rms_bwd(dcn, xh, r, g)
        dckv_ref[...] = _bf(dx)
        dg_ref[...] += dg

    wspec = pl.BlockSpec((hw, MLA_KVR), lambda b, i: (0, 0))
    aspec = pl.BlockSpec((MLA_KVR, hw), lambda b, i: (0, 0))
    tspec = pl.BlockSpec((TOK, LANE), lambda b, i: (i, 0))
    ospec = pl.BlockSpec((TOK, LANE), lambda b, i: (b * nb + i, 0))
    return pl.pallas_call(
        body, name="mla_kv_up_bwd", grid=(bsz, nb),
        in_specs=[pl.BlockSpec((TOK, MLA_HEADS * QKW), lambda b, i: (b * nb + i, 0)),
                  pl.BlockSpec((TOK, hw), lambda b, i: (b * nb + i, 0)),
                  pl.BlockSpec((TOK, LANE), lambda b, i: (b * nb + i, C_CKV // LANE)),
                  pl.BlockSpec((1, MLA_KVR), lambda b, i: (0, 0)), wspec, wspec, tspec, tspec],
        out_specs=[ospec, ospec, ospec, aspec, aspec, pl.BlockSpec((1, MLA_KVR), lambda b, i: (0, 0))],
        out_shape=[jax.ShapeDtypeStruct((tp, LANE), BF16)] * 3 + [jax.ShapeDtypeStruct((MLA_KVR, hw), F32)] * 2
        + [jax.ShapeDtypeStruct((1, MLA_KVR), F32)],
        compiler_params=_cp(("arbitrary", "arbitrary")),
    )(dkf, dvf, proj, kv_norm_g, wk_t, wv_t, cos_t, sin_t)


def _mid_fwd(ya_in, yb_in, proj, hp, target, w_gp, w_mp, w_o, final_g, bsz, lp):
    tp = bsz * lp
    nb = lp // TOK

    def body(ya_ref, yb_ref, gg_ref, gm_ref, h_ref, t_ref, wgp_ref, wmp_ref, wo_ref, fg_ref,
             ya_out, yb_out, dh_ref, loss_ref, dfg_ref):
        @pl.when(jnp.logical_and(pl.program_id(0) == 0, pl.program_id(1) == 0))
        def _():
            loss_ref[...] = jnp.zeros_like(loss_ref)
            dfg_ref[...] = jnp.zeros_like(dfg_ref)

        y_a = _dot(ya_ref[...], wgp_ref[...])
        y_b = _dot(yb_ref[...], wmp_ref[...])
        ya_out[...] = _bf(y_a)
        yb_out[...] = _bf(y_b)
        merged = _sigmoid(gg_ref[...].astype(F32)) * y_a + _sigmoid(gm_ref[...].astype(F32)) * y_b
        h2 = h_ref[...] + _dot(_bf(merged), wo_ref[...])
        fg = fg_ref[...]
        xh, r = _rms_fwd(h2, None)
        pos = pl.program_id(1) * TOK + lax.broadcasted_iota(jnp.int32, (TOK, 1), 0)
        err = jnp.where(pos >= X0, xh * fg - t_ref[...], 0.0)
        loss_ref[...] += 0.5 * jnp.sum(jnp.mean(err * err, axis=-1, keepdims=True), axis=0, keepdims=True)
        dy = err * (1.0 / D_MODEL)
        dx, dfg = _rms_bwd(dy, xh, r, fg)
        dh_ref[...] = dx
        dfg_ref[...] += dfg

    tok = lambda c: pl.BlockSpec((TOK, D_MODEL), lambda b, i: (b * nb + i, c))
    wspec = pl.BlockSpec((D_MODEL, D_MODEL), lambda b, i: (0, 0))
    return pl.pallas_call(
        body, name="mid_fwd", grid=(bsz, nb),
        in_specs=[tok(0), tok(0), tok(C_GG // D_MODEL), tok(C_GM // D_MODEL), tok(0), tok(0),
                  wspec, wspec, wspec, pl.BlockSpec((1, D_MODEL), lambda b, i: (0, 0))],
        out_specs=[tok(0), tok(0), tok(0), pl.BlockSpec((1, LANE), lambda b, i: (0, 0)),
                   pl.BlockSpec((1, D_MODEL), lambda b, i: (0, 0))],
        out_shape=[jax.ShapeDtypeStruct((tp, D_MODEL), BF16), jax.ShapeDtypeStruct((tp, D_MODEL), BF16),
                   jax.ShapeDtypeStruct((tp, D_MODEL), F32), jax.ShapeDtypeStruct((1, LANE), F32),
                   jax.ShapeDtypeStruct((1, D_MODEL), F32)],
        compiler_params=_cp(("arbitrary", "arbitrary"), 48),
    )(ya_in, yb_in, proj, proj, hp, target, w_gp, w_mp, w_o, final_g)


def _mid_bwd(dh2, y_a, y_b, proj, ya_in, yb_in, w_o_t, w_gp_t, w_mp_t):
    tp = dh2.shape[0]
    tm = TOK
    nsteps = tp // tm

    def body(dh_ref, ya_ref, yb_ref, gg_ref, gm_ref, yai_ref, ybi_ref, wo_ref, wgp_ref, wmp_ref,
             dyai_ref, dybi_ref, dgg_ref, dgm_ref, dwo_ref, dwgp_ref, dwmp_ref, a_o, a_gp, a_mp):
        @pl.when(pl.program_id(0) == 0)
        def _():
            for r in (a_o, a_gp, a_mp):
                r[...] = jnp.zeros_like(r)

        dh = _bf(dh_ref[...])
        dm = _dot(dh, wo_ref[...])
        y_a, y_b = ya_ref[...].astype(F32), yb_ref[...].astype(F32)
        sg, sm = _sigmoid(gg_ref[...].astype(F32)), _sigmoid(gm_ref[...].astype(F32))
        d_ya, d_yb = _bf(sg * dm), _bf(sm * dm)
        dgg_ref[...] = _bf(dm * y_a * sg * (1.0 - sg))
        dgm_ref[...] = _bf(dm * y_b * sm * (1.0 - sm))
        a_o[...] += _dot_tn(_bf(sg * y_a + sm * y_b), dh)
        a_gp[...] += _dot_tn(yai_ref[...], d_ya)
        a_mp[...] += _dot_tn(ybi_ref[...], d_yb)
        dyai_ref[...] = _bf(_dot(d_ya, wgp_ref[...]))
        dybi_ref[...] = _bf(_dot(d_yb, wmp_ref[...]))

        @pl.when(pl.program_id(0) == nsteps - 1)
        def _():
            pltpu.sync_copy(a_o, dwo_ref)
            pltpu.sync_copy(a_gp, dwgp_ref)
            pltpu.sync_copy(a_mp, dwmp_ref)

    tok = lambda c: pl.BlockSpec((tm, D_MODEL), lambda i: (i, c))
    wspec = pl.BlockSpec((D_MODEL, D_MODEL), lambda i: (0, 0))
    anyspec = pl.BlockSpec(memory_space=pl.ANY)
    wshape = jax.ShapeDtypeStruct((D_MODEL, D_MODEL), F32)
    return pl.pallas_call(
        body, name="mid_bwd", grid=(nsteps,),
        in_specs=[tok(0), tok(0), tok(0), tok(C_GG // D_MODEL), tok(C_GM // D_MODEL), tok(0), tok(0),
                  wspec, wspec, wspec],
        out_specs=[tok(0), tok(0), tok(0), tok(0), anyspec, anyspec, anyspec],
        out_shape=[jax.ShapeDtypeStruct((tp, D_MODEL), BF16)] * 4 + [wshape] * 3,
        scratch_shapes=[pltpu.VMEM((D_MODEL, D_MODEL), F32)] * 3,
        compiler_params=_cp(("arbitrary",), 56),
    )(dh2, y_a, y_b, proj, proj, ya_in, yb_in, w_o_t, w_gp_t, w_mp_t)


def _dw_in(u, dproj):
    tp = u.shape[0]
    tm, tn = _big_tok(tp), 768

    def body(u_ref, d_ref, o_ref):
        @pl.when(pl.program_id(1) == 0)
        def _():
            o_ref[...] = jnp.zeros_like(o_ref)

        o_ref[...] += _dot_tn(u_ref[...], d_ref[...])

    return pl.pallas_call(
        body, name="dw_in", grid=(N_EXT // tn, tp // tm),
        in_specs=[pl.BlockSpec((tm, D_MODEL), lambda j, i: (i, 0)), pl.BlockSpec((tm, tn), lambda j, i: (i, j))],
        out_specs=pl.BlockSpec((D_MODEL, tn), lambda j, i: (0, j)),
        out_shape=jax.ShapeDtypeStruct((D_MODEL, N_EXT), F32),
        compiler_params=_cp(("parallel", "arbitrary"), 48),
    )(u, dproj)


def _dx_in(dproj, w_ext_t, hp, dh2, norm_g):
    tp = hp.shape[0]
    tm, tk = _big_tok(tp), 768
    nk = N_EXT // tk

    def body(d_ref, w_ref, h_ref, dh_ref, g_ref, o_ref, dg_ref, acc):
        k = pl.program_id(1)

        @pl.when(jnp.logical_and(pl.program_id(0) == 0, k == 0))
        def _():
            dg_ref[...] = jnp.zeros_like(dg_ref)

        @pl.when(k == 0)
        def _():
            acc[...] = jnp.zeros_like(acc)

        acc[...] += _dot(d_ref[...], w_ref[...])

        @pl.when(k == nk - 1)
        def _():
            g = g_ref[...]
            xh, r = _rms_fwd(h_ref[...], None)
            dx, dg = _rms_bwd(acc[...], xh, r, g)
            o_ref[...] = dh_ref[...] + dx
            dg_ref[...] += dg

    tok = pl.BlockSpec((tm, D_MODEL), lambda i, k: (i, 0))
    return pl.pallas_call(
        body, name="dx_in", grid=(tp // tm, nk),
        in_specs=[pl.BlockSpec((tm, tk), lambda i, k: (i, k)), pl.BlockSpec((tk, D_MODEL), lambda i, k: (k, 0)),
                  tok, tok, pl.BlockSpec((1, D_MODEL), lambda i, k: (0, 0))],
        out_specs=[tok, pl.BlockSpec((1, D_MODEL), lambda i, k: (0, 0))],
        out_shape=[jax.ShapeDtypeStruct((tp, D_MODEL), F32), jax.ShapeDtypeStruct((1, D_MODEL), F32)],
        scratch_shapes=[pltpu.VMEM((tm, D_MODEL), F32)],
        compiler_params=_cp(("arbitrary", "arbitrary"), 56),
    )(dproj, w_ext_t, hp, dh2, norm_g)


def _meta_grad(dhp3):
    bsz = dhp3.shape[0]

    def body(d_ref, o_ref):
        @pl.when(pl.program_id(0) == 0)
        def _():
            o_ref[...] = jnp.zeros_like(o_ref)

        o_ref[...] += d_ref[0]

    return pl.pallas_call(
        body, name="meta_grad", grid=(bsz,),
        in_specs=[pl.BlockSpec((1, N_META, D_MODEL), lambda b: (b, FRONT // N_META, 0))],
        out_specs=pl.BlockSpec((N_META, D_MODEL), lambda b: (0, 0)),
        out_shape=jax.ShapeDtypeStruct((N_META, D_MODEL), F32),
        compiler_params=_cp(("arbitrary",)),
    )(dhp3)


def _pad_lanes(a, width=LANE):
    return jnp.pad(a, [(0, 0)] * (a.ndim - 1) + [(0, width - a.shape[-1])])


def _rot_cols(w):
    half = w.shape[-1] // 2
    return jnp.concatenate([-w[..., half:], w[..., :half]], axis=-1)


def _unrot_cols(dw):
    half = dw.shape[-1] // 2
    return jnp.concatenate([dw[..., half:], -dw[..., :half]], axis=-1)


def _w_in_ext(w):
    kr = w[:, O_KR:O_MZ]
    return jnp.concatenate([
        w[:, O_V:O_LR], w[:, O_Z:O_CQ], w[:, O_MZ:O_GG], w[:, O_GG:O_GM], w[:, O_GM:N_IN],
        w[:, O_Q:O_K], w[:, O_K:O_V], w[:, O_CQ:O_CKV], w[:, O_CKV:O_KR],
        _pad_lanes(kr), _pad_lanes(_rot_cols(kr)), _pad_lanes(w[:, O_LR:O_Z])], axis=1)


def _w_in_grad(dw):
    kr = dw[:, C_KR:C_KR + MLA_ROPE] + _unrot_cols(dw[:, C_KROT:C_KROT + MLA_ROPE])
    return jnp.concatenate([
        dw[:, C_Q:C_K], dw[:, C_K:C_CQ], dw[:, C_V:C_Z], dw[:, C_LR:C_LR + GLA_RANK], dw[:, C_Z:C_MZ],
        dw[:, C_CQ:C_CKV], dw[:, C_CKV:C_KR], kr, dw[:, C_MZ:C_GG], dw[:, C_GG:C_GM], dw[:, C_GM:C_Q]], axis=1)


def _rope_tables(lp):
    inv = 1.0 / (ROPE_BASE ** (jnp.arange(0, MLA_ROPE, 2, dtype=F32) / MLA_ROPE))
    ang = (jnp.arange(lp, dtype=F32) - FRONT)[:, None] * inv[None, :]
    cos, sin = jnp.cos(ang), jnp.sin(ang)
    return _pad_lanes(jnp.concatenate([cos, cos], axis=1)), _pad_lanes(jnp.concatenate([sin, sin], axis=1))


def _local_step(x, loss_target, w):
    bsz, seq, _ = x.shape
    lp = X0 + seq
    tp = bsz * lp
    assert lp % TOK == 0 and lp % GLA_CHUNK == 0
    meta = jnp.broadcast_to(w["meta_tokens"][None], (bsz, N_META, D_MODEL))
    hp = jnp.concatenate([jnp.zeros((bsz, FRONT, D_MODEL), F32), meta, x], axis=1).reshape(tp, D_MODEL)
    target = jnp.pad(loss_target, ((0, 0), (X0, 0), (0, 0))).reshape(tp, D_MODEL)
    cos_t, sin_t = _rope_tables(lp)

    w_ext = _w_in_ext(w["w_in"])
    gw_pad = jnp.pad(w["gla_gate_w"], ((0, LANE - GLA_RANK), (0, 0)))
    uq = w["mla_w_uq"].reshape(MLA_QR, MLA_HEADS, MLA_QK)
    rope_w = uq[:, :, MLA_NOPE:]
    hw = MLA_HEADS * LANE
    wn = uq[:, :, :MLA_NOPE].reshape(MLA_QR, hw)
    wr = _pad_lanes(rope_w).reshape(MLA_QR, hw)
    wt = _pad_lanes(_rot_cols(rope_w)).reshape(MLA_QR, hw)
    ukv = w["mla_w_ukv"].reshape(MLA_KVR, MLA_HEADS, MLA_NOPE + MLA_DV)
    wk = ukv[:, :, :MLA_NOPE].reshape(MLA_KVR, hw)
    wv = ukv[:, :, MLA_NOPE:].reshape(MLA_KVR, hw)

    u, proj = _proj_in(hp, w["norm_g"], w_ext)
    o_raw, ya_in, s_all = _gla_fwd(proj, gw_pad, w["gla_gate_b"], w["gla_norm_g"], bsz, lp)
    qf = _q_up(proj, w["mla_q_norm_g"], wn, wr, wt, cos_t, sin_t, bsz, lp)
    kf, vf = _kv_up(proj, w["mla_kv_norm_g"], wk, wv, cos_t, sin_t, bsz, lp)
    o_b, yb_in, lse = _flash_fwd(qf, kf, vf, proj, bsz, lp)
    y_a, y_b, dh2, loss, d_final_g = _mid_fwd(ya_in, yb_in, proj, hp, target, w["gla_proj"], w["mla_proj"],
                                              w["w_out"], w["final_norm_g"], bsz, lp)
    d_ya, d_yb, d_gg, d_gm, d_w_out, d_gla_proj, d_mla_proj = _mid_bwd(
        dh2, y_a, y_b, proj, ya_in, yb_in, w["w_out"].T, w["gla_proj"].T, w["mla_proj"].T)
    dq, dk, dv, d_z, dzg, d_gla_norm = _gla_bwd(proj, gw_pad, w["gla_gate_b"], w["gla_norm_g"], o_raw, s_all,
                                                d_ya, bsz, lp)
    d_lr, d_gw_pad, d_gate_b = _gate_bwd(dzg, proj, gw_pad.T)
    d_o, d_mz, delta = _flash_bwd_pre(d_yb, proj, o_b, bsz, lp)
    dqf = _flash_bwd_dq(qf, kf, vf, d_o, lse, delta, bsz, lp)
    dkf, dvf = _flash_bwd_dkv(qf, kf, vf, d_o, lse, delta, bsz, lp)
    d_cq, d_wn, d_wr, d_wt, d_qn = _q_up_bwd(dqf, proj, w["mla_q_norm_g"], wn.T, wr.T, wt.T, cos_t, sin_t, bsz, lp)
    d_ckv, d_kr, d_krot, d_wk, d_wv, d_kvn = _kv_up_bwd(dkf, dvf, proj, w["mla_kv_norm_g"], wk.T, wv.T,
                                                        cos_t, sin_t, bsz, lp)
    dproj = jnp.concatenate([dv, d_z, d_mz, d_gg, d_gm, dq, dk, d_cq, d_ckv, d_kr, d_krot, d_lr], axis=1)
    d_w_ext = _dw_in(u, dproj)
    d_hp, d_norm_g = _dx_in(dproj, w_ext.T, hp, dh2, w["norm_g"])
    d_hp3 = d_hp.reshape(bsz, lp, D_MODEL)

    d_rope = (d_wr.reshape(MLA_QR, MLA_HEADS, LANE)[:, :, :MLA_ROPE]
              + _unrot_cols(d_wt.reshape(MLA_QR, MLA_HEADS, LANE)[:, :, :MLA_ROPE]))
    d_uq = jnp.concatenate([d_wn.reshape(MLA_QR, MLA_HEADS, LANE), d_rope], axis=-1).reshape(MLA_QR, MLA_HEADS * MLA_QK)
    d_ukv = jnp.concatenate([d_wk.reshape(MLA_KVR, MLA_HEADS, LANE), d_wv.reshape(MLA_KVR, MLA_HEADS, LANE)],
                            axis=-1).reshape(MLA_KVR, MLA_HEADS * (MLA_NOPE + MLA_DV))
    grads = dict(meta_tokens=_meta_grad(d_hp3), norm_g=d_norm_g, w_in=_w_in_grad(d_w_ext),
                 gla_gate_w=d_gw_pad[:GLA_RANK], gla_gate_b=d_gate_b, gla_norm_g=d_gla_norm, gla_proj=d_gla_proj,
                 mla_q_norm_g=d_qn, mla_w_uq=d_uq, mla_kv_norm_g=d_kvn, mla_w_ukv=d_ukv, mla_proj=d_mla_proj,
                 w_out=d_w_out, final_norm_g=d_final_g)
    return loss, d_hp3[:, X0:, :], grads


MESH_ID = pl.DeviceIdType.MESH
SHARDED = (("w_in", (D_MODEL, N_IN // N_DEV), 1), ("gla_gate_w", (GLA_RANK, GLA_KW // N_DEV), 1),
           ("gla_proj", (D_MODEL // N_DEV, D_MODEL), 0), ("mla_w_uq", (MLA_QR, MLA_HEADS * MLA_QK // N_DEV), 1),
           ("mla_w_ukv", (MLA_KVR, MLA_HEADS * (MLA_NOPE + MLA_DV) // N_DEV), 1),
           ("mla_proj", (D_MODEL // N_DEV, D_MODEL), 0), ("w_out", (D_MODEL // N_DEV, D_MODEL), 0),
           ("meta_tokens", (N_META, D_MODEL // N_DEV), 1))
REPLICATED = (("norm_g", D_MODEL), ("gla_gate_b", GLA_KW), ("gla_norm_g", GLA_DV), ("mla_q_norm_g", MLA_QR),
              ("mla_kv_norm_g", MLA_KVR), ("final_norm_g", D_MODEL))
PACK_ROWS = 10368
PACK_BLOCK = 1152
SMALL_ROWS = 32
LOSS_ROW = 25


def _my_place():
    return lax.axis_index("x"), lax.axis_index("y"), lax.axis_index("c")


def _all_gather(pack):
    rows = pack.shape[0]

    def body(x_ref, out_ref, send_sems, recv_sems, local_sem):
        x, y, c = _my_place()
        me, sibling = (x, y, c), (x, y, 1 - c)
        chips = [(1 - x, y), (x, 1 - y), (1 - x, 1 - y)]

        def slab(px, py, pc):
            return out_ref.at[4 * px + 2 * py + pc]

        def copy(k, block, to, src=None):
            return pltpu.make_async_remote_copy(
                src_ref=slab(*block) if src is None else src, dst_ref=slab(*block),
                send_sem=send_sems.at[k], recv_sem=recv_sems.at[k], device_id=to, device_id_type=MESH_ID)

        mine = pltpu.make_async_copy(x_ref, slab(*me), local_sem)
        mine.start()
        first = [copy(0, me, sibling, src=x_ref)]
        first += [copy(1 + j, me, (*chip, c), src=x_ref) for j, chip in enumerate(chips)]
        for cp in first:
            cp.start()
        passed = [copy(4 + j, (*chip, c), sibling) for j, chip in enumerate(chips)]
        for j, chip in enumerate(chips):
            copy(1 + j, (*chip, c), me).wait_recv()
            passed[j].start()
        copy(0, sibling, me).wait_recv()
        for j, chip in enumerate(chips):
            copy(4 + j, (*chip, 1 - c), me).wait_recv()
        for cp in first + passed:
            cp.wait_send()
        mine.wait()

    return pl.pallas_call(
        body, name="weights_all_gather",
        out_shape=jax.ShapeDtypeStruct((N_DEV, rows, LANE), pack.dtype),
        in_specs=[pl.BlockSpec(memory_space=pl.ANY)], out_specs=pl.BlockSpec(memory_space=pl.ANY),
        scratch_shapes=[pltpu.SemaphoreType.DMA((7,)), pltpu.SemaphoreType.DMA((7,)), pltpu.SemaphoreType.DMA],
    )(pack)


def _grad_exchange(slabs, small):
    rows = slabs.shape[1]

    def body(g_ref, s_ref, recv_ref, sall_ref, send_sems, recv_sems, local_sem):
        x, y, c = _my_place()
        me = 4 * x + 2 * y + c
        own = pltpu.make_async_copy(g_ref.at[me], recv_ref.at[me], local_sem)
        own.start()
        sall_ref[me] = s_ref[...]
        sends, waits = [], []
        for d in range(1, N_DEV):
            px = 1 - x if d & 4 else x
            py = 1 - y if d & 2 else y
            pc = 1 - c if d & 1 else c
            peer = 4 * px + 2 * py + pc
            big = pltpu.make_async_remote_copy(
                src_ref=g_ref.at[peer], dst_ref=recv_ref.at[me], send_sem=send_sems.at[d - 1],
                recv_sem=recv_sems.at[d - 1], device_id=(px, py, pc), device_id_type=MESH_ID)
            sm = pltpu.make_async_remote_copy(
                src_ref=s_ref, dst_ref=sall_ref.at[me], send_sem=send_sems.at[6 + d],
                recv_sem=recv_sems.at[6 + d], device_id=(px, py, pc), device_id_type=MESH_ID)
            big.start()
            sm.start()
            sends += [big, sm]
            waits.append(pltpu.make_async_remote_copy(
                src_ref=g_ref.at[peer], dst_ref=recv_ref.at[peer], send_sem=send_sems.at[d - 1],
                recv_sem=recv_sems.at[d - 1], device_id=(px, py, pc), device_id_type=MESH_ID))
            waits.append(pltpu.make_async_remote_copy(
                src_ref=s_ref, dst_ref=sall_ref.at[peer], send_sem=send_sems.at[6 + d],
                recv_sem=recv_sems.at[6 + d], device_id=(px, py, pc), device_id_type=MESH_ID))
        for cp in waits:
            cp.wait_recv()
        for cp in sends:
            cp.wait_send()
        own.wait()

    return pl.pallas_call(
        body, name="grad_exchange",
        out_shape=[jax.ShapeDtypeStruct((N_DEV, rows, LANE), F32), jax.ShapeDtypeStruct((N_DEV, SMALL_ROWS, LANE), F32)],
        in_specs=[pl.BlockSpec(memory_space=pl.ANY), pl.BlockSpec(memory_space=pltpu.VMEM)],
        out_specs=[pl.BlockSpec(memory_space=pl.ANY), pl.BlockSpec(memory_space=pltpu.VMEM)],
        scratch_shapes=[pltpu.SemaphoreType.DMA((14,)), pltpu.SemaphoreType.DMA((14,)), pltpu.SemaphoreType.DMA],
    )(slabs, small)


def _adamw(parts, w, m, v, name):
    rows = w.shape[0]
    tr = PACK_BLOCK if rows % PACK_BLOCK == 0 else rows

    def body(p_ref, w_ref, m_ref, v_ref, g_out, d_out, m_out, v_out):
        g = p_ref[0]
        for s in range(1, N_DEV):
            g = g + p_ref[s]
        m_new = ADAM_B1 * m_ref[...] + (1.0 - ADAM_B1) * g
        v_new = ADAM_B2 * v_ref[...] + (1.0 - ADAM_B2) * (g * g)
        m_hat = m_new / (1.0 - ADAM_B1 ** ADAM_STEP)
        v_hat = v_new / (1.0 - ADAM_B2 ** ADAM_STEP)
        g_out[...] = g
        d_out[...] = -ADAM_LR * (m_hat / (jnp.sqrt(v_hat) + ADAM_EPS) + ADAM_WD * w_ref[...])
        m_out[...] = m_new
        v_out[...] = v_new

    spec = pl.BlockSpec((tr, LANE), lambda i: (i, 0))
    return pl.pallas_call(
        body, name=name, grid=(rows // tr,),
        in_specs=[pl.BlockSpec((N_DEV, tr, LANE), lambda i: (0, i, 0)), spec, spec, spec],
        out_specs=[spec] * 4, out_shape=[jax.ShapeDtypeStruct((rows, LANE), F32)] * 4,
        compiler_params=_cp(("parallel",), 48),
    )(parts, w, m, v)


def _pad_rows(flat, rows):
    pad = rows * LANE - flat.shape[-1]
    flat = jnp.pad(flat, [(0, 0)] * (flat.ndim - 1) + [(0, pad)])
    return flat.reshape(flat.shape[:-1] + (rows, LANE))


def _pack_shards(shards):
    return _pad_rows(jnp.concatenate([shards[n].reshape(-1) for n, _, _ in SHARDED]), PACK_ROWS)


def _unpack_shards(packed):
    flat, out, off = packed.reshape(-1), {}, 0
    for n, shape, _ in SHARDED:
        size = shape[0] * shape[1]
        out[n] = flat[off:off + size].reshape(shape)
        off += size
    return out


def _split8(full, axis):
    r, c = full.shape
    if axis == 0:
        return full.reshape(N_DEV, -1)
    return full.reshape(r, N_DEV, c // N_DEV).transpose(1, 0, 2).reshape(N_DEV, -1)


def _join8(flat8, shape, axis):
    r, c = shape
    if axis == 0:
        return flat8.reshape(N_DEV * r, c)
    return flat8.reshape(N_DEV, r, c).transpose(1, 0, 2).reshape(r, N_DEV * c)


def _pack_small(vals, loss_row):
    rows = [vals[n].reshape(-1, LANE) for n, _ in REPLICATED] + [loss_row]
    flat = jnp.concatenate(rows, axis=0)
    return jnp.pad(flat, ((0, SMALL_ROWS - flat.shape[0]), (0, 0)))


def _unpack_small(packed):
    out, off = {}, 0
    for n, size in REPLICATED:
        out[n] = packed[off:off + size // LANE].reshape(1, size)
        off += size // LANE
    return out


def kernel(x, meta_tokens, norm_g, w_in, gla_gate_w, gla_gate_b, gla_norm_g, gla_proj, mla_q_norm_g, mla_w_uq, mla_kv_norm_g, mla_w_ukv, mla_proj, w_out, final_norm_g, loss_target, m_meta_tokens, m_norm_g, m_w_in, m_gla_gate_w, m_gla_gate_b, m_gla_norm_g, m_gla_proj, m_mla_q_norm_g, m_mla_w_uq, m_mla_kv_norm_g, m_mla_w_ukv, m_mla_proj, m_w_out, m_final_norm_g, v_meta_tokens, v_norm_g, v_w_in, v_gla_gate_w, v_gla_gate_b, v_gla_norm_g, v_gla_proj, v_mla_q_norm_g, v_mla_w_uq, v_mla_kv_norm_g, v_mla_w_ukv, v_mla_proj, v_w_out, v_final_norm_g):
    names = [n for n, _, _ in SHARDED] + [n for n, _ in REPLICATED]
    given = dict(meta_tokens=meta_tokens, norm_g=norm_g, w_in=w_in, gla_gate_w=gla_gate_w, gla_gate_b=gla_gate_b,
                 gla_norm_g=gla_norm_g, gla_proj=gla_proj, mla_q_norm_g=mla_q_norm_g, mla_w_uq=mla_w_uq,
                 mla_kv_norm_g=mla_kv_norm_g, mla_w_ukv=mla_w_ukv, mla_proj=mla_proj, w_out=w_out,
                 final_norm_g=final_norm_g)
    mom_m = dict(meta_tokens=m_meta_tokens, norm_g=m_norm_g, w_in=m_w_in, gla_gate_w=m_gla_gate_w,
                 gla_gate_b=m_gla_gate_b, gla_norm_g=m_gla_norm_g, gla_proj=m_gla_proj, mla_q_norm_g=m_mla_q_norm_g,
                 mla_w_uq=m_mla_w_uq, mla_kv_norm_g=m_mla_kv_norm_g, mla_w_ukv=m_mla_w_ukv, mla_proj=m_mla_proj,
                 w_out=m_w_out, final_norm_g=m_final_norm_g)
    mom_v = dict(meta_tokens=v_meta_tokens, norm_g=v_norm_g, w_in=v_w_in, gla_gate_w=v_gla_gate_w,
                 gla_gate_b=v_gla_gate_b, gla_norm_g=v_gla_norm_g, gla_proj=v_gla_proj, mla_q_norm_g=v_mla_q_norm_g,
                 mla_w_uq=v_mla_w_uq, mla_kv_norm_g=v_mla_kv_norm_g, mla_w_ukv=v_mla_w_ukv, mla_proj=v_mla_proj,
                 w_out=v_w_out, final_norm_g=v_final_norm_g)
    shapes = {n: given[n].shape for n in names}
    shard2d = {n: s for n, s, _ in SHARDED}

    def as2d(tree):
        out = {n: tree[n].reshape(shard2d[n]) for n in shard2d}
        out.update({n: tree[n].reshape(1, size) for n, size in REPLICATED})
        return out

    w_loc, m_loc, v_loc = as2d(given), as2d(mom_m), as2d(mom_v)

    meta_bits = lax.bitcast_convert_type(w_loc["meta_tokens"], BF16).reshape(-1)
    flat = jnp.concatenate([w_loc[n].astype(BF16).reshape(-1) for n, _, _ in SHARDED[:-1]] + [meta_bits])
    gathered = _all_gather(_pad_rows(flat, PACK_ROWS)).reshape(N_DEV, -1)
    full, off = {}, 0
    for n, shape, axis in SHARDED[:-1]:
        size = shape[0] * shape[1]
        full[n] = _join8(gathered[:, off:off + size], shape, axis)
        off += size
    meta8 = lax.bitcast_convert_type(gathered[:, off:off + 2 * N_META * LANE].reshape(N_DEV, N_META, LANE, 2), F32)
    full["meta_tokens"] = _join8(meta8.reshape(N_DEV, -1), (N_META, LANE), 1)
    for n, _ in REPLICATED:
        full[n] = w_loc[n]

    loss_part, grad_x, grads = _local_step(x, loss_target, full)

    slabs = _pad_rows(jnp.concatenate([_split8(grads[n], axis) for n, _, axis in SHARDED], axis=1), PACK_ROWS)
    small = _pack_small(grads, jnp.broadcast_to(loss_part[:, :1], (1, LANE)))
    parts, small_all = _grad_exchange(slabs, small)

    g_p, d_p, m_p, v_p = _adamw(parts, _pack_shards(w_loc), _pack_shards(m_loc), _pack_shards(v_loc), "adamw_shards")
    zero_row = jnp.zeros((1, LANE), F32)
    g_s, d_s, m_s, v_s = _adamw(small_all, _pack_small(w_loc, zero_row), _pack_small(m_loc, zero_row),
                                _pack_small(v_loc, zero_row), "adamw_replicated")
    loss = g_s[LOSS_ROW, 0]

    outs = []
    for packed_sh, packed_sm in ((g_p, g_s), (d_p, d_s), (m_p, m_s), (v_p, v_s)):
        tree = _unpack_shards(packed_sh)
        tree.update(_unpack_small(packed_sm))
        outs.append(tree)
    order = ["meta_tokens", "norm_g", "w_in", "gla_gate_w", "gla_gate_b", "gla_norm_g", "gla_proj", "mla_q_norm_g",
             "mla_w_uq", "mla_kv_norm_g", "mla_w_ukv", "mla_proj", "w_out", "final_norm_g"]
    result = [loss, grad_x]
    for tree in outs:
        result += [tree[n].reshape(shapes[n]) for n in order]
    return tuple(result)
```

```python
import jax
import jax.numpy as jnp
from jax import lax
from jax.experimental import pallas as pl
from jax.experimental.pallas import tpu as pltpu

F32 = jnp.float32
BF16 = jnp.bfloat16

D_MODEL = 1024
N_META = 16
EPS = 1e-6
FRONT = 48
X0 = FRONT + N_META
GLA_HEADS, GLA_DK, GLA_DV, GLA_RANK, GLA_CHUNK = 4, 128, 256, 16, 64
GLA_GATE_NORMALIZER = 16.0
GLA_KW = GLA_HEADS * GLA_DK
GLA_VW = GLA_HEADS * GLA_DV
MLA_HEADS, MLA_NOPE, MLA_ROPE, MLA_DV, MLA_QR, MLA_KVR = 8, 128, 64, 128, 256, 128
MLA_QK = MLA_NOPE + MLA_ROPE
ROPE_BASE = 10000.0
LANE = 128
QKW = 2 * LANE

C_V, C_Z, C_MZ, C_GG, C_GM = 0, 1024, 2048, 3072, 4096
C_Q, C_K, C_CQ, C_CKV, C_KR, C_KROT, C_LR = 5120, 5632, 6144, 6400, 6528, 6656, 6784
N_EXT = 6912
O_Q, O_K, O_V, O_LR, O_Z, O_CQ, O_CKV, O_KR, O_MZ, O_GG, O_GM, N_IN = (
    0, 512, 1024, 2048, 2064, 3088, 3344, 3472, 3536, 4560, 5584, 6608)

ADAM_LR, ADAM_B1, ADAM_B2, ADAM_EPS, ADAM_WD, ADAM_STEP = 0.001, 0.9, 0.999, 1e-08, 0.01, 10

N_DEV = 8
TOK = 192
ATT_BLOCK = 352
NEG = -1e30


def _cp(sems=None, vmem_mb=None):
    kw = {}
    if sems is not None:
        kw["dimension_semantics"] = sems
    if vmem_mb is not None:
        kw["vmem_limit_bytes"] = vmem_mb * 1024 * 1024
    return pltpu.CompilerParams(**kw)


def _dot(a, b):
    return jnp.dot(a, b, preferred_element_type=F32)


def _dot_nt(a, b):
    return lax.dot_general(a, b, (((1,), (1,)), ((), ())), preferred_element_type=F32)


def _dot_tn(a, b):
    return lax.dot_general(a, b, (((0,), (0,)), ((), ())), preferred_element_type=F32)


def _sigmoid(x):
    return 1.0 / (1.0 + jnp.exp(-x))


def _bf(x):
    return x.astype(BF16)


def _big_tok(tp):
    return 4 * TOK if tp % (4 * TOK) == 0 else TOK


def _attn_block(lp):
    return ATT_BLOCK if lp % ATT_BLOCK == 0 else TOK


def _proj_in(hp, norm_g, w_ext):
    tp = hp.shape[0]
    tm, tn = _big_tok(tp), 768

    def body(h_ref, g_ref, w_ref, u_ref, o_ref, u_scr):
        @pl.when(pl.program_id(1) == 0)
        def _():
            x = h_ref[...]
            r = lax.rsqrt(jnp.mean(x * x, axis=-1, keepdims=True) + EPS)
            u = _bf(x * r * g_ref[...])
            u_scr[...] = u
            u_ref[...] = u

        o_ref[...] = _bf(_dot(u_scr[...], w_ref[...]))

    return pl.pallas_call(
        body, name="proj_in", grid=(tp // tm, N_EXT // tn),
        in_specs=[pl.BlockSpec((tm, D_MODEL), lambda i, j: (i, 0)),
                  pl.BlockSpec((1, D_MODEL), lambda i, j: (0, 0)),
                  pl.BlockSpec((D_MODEL, tn), lambda i, j: (0, j))],
        out_specs=[pl.BlockSpec((tm, D_MODEL), lambda i, j: (i, 0)),
                   pl.BlockSpec((tm, tn), lambda i, j: (i, j))],
        out_shape=[jax.ShapeDtypeStruct((tp, D_MODEL), BF16), jax.ShapeDtypeStruct((tp, N_EXT), BF16)],
        scratch_shapes=[pltpu.VMEM((tm, D_MODEL), BF16)],
        compiler_params=_cp(("parallel", "arbitrary"), 48),
    )(hp, norm_g, w_ext)


def _gla_gates(q_ref, k_ref, lr_ref, gw_ref, gb_ref, n):
    z = _dot(lr_ref[...], gw_ref[...]) + gb_ref[...]
    logsig = jnp.minimum(z, 0.0) - jnp.log(1.0 + jnp.exp(-jnp.abs(z)))
    row = lax.broadcasted_iota(jnp.int32, (GLA_CHUNK, GLA_KW), 0)
    live = jnp.logical_or(n > 0, row >= FRONT)
    g = jnp.where(live, logsig * (1.0 / GLA_GATE_NORMALIZER), 0.0)
    ri = lax.broadcasted_iota(jnp.int32, (GLA_CHUNK, GLA_CHUNK), 0)
    ci = lax.broadcasted_iota(jnp.int32, (GLA_CHUNK, GLA_CHUNK), 1)
    tril = ci <= ri
    b = jnp.dot(tril.astype(F32), g, precision=lax.Precision.HIGHEST, preferred_element_type=F32)
    bl = jnp.sum(jnp.where(row == GLA_CHUNK - 1, b, 0.0), axis=0, keepdims=True)
    eb, enb, elb, ebl = jnp.exp(b), jnp.exp(-b), jnp.exp(bl - b), jnp.exp(bl)
    q = q_ref[...].astype(F32) * (GLA_DK ** -0.5)
    k = k_ref[...].astype(F32)
    return dict(z=z, live=live, tril=tril, row=row, eb=eb, enb=enb, elb=elb, ebl=ebl,
                qe=q * eb, ke=k * enb, kl=k * elb)


def _gla_in_specs(n_chunks, rev):
    def rb(b, n):
        return b * n_chunks + ((n_chunks - 1 - n) if rev else n)

    return rb, [pl.BlockSpec((GLA_CHUNK, GLA_KW), lambda b, n: (rb(b, n), C_Q // GLA_KW)),
                pl.BlockSpec((GLA_CHUNK, GLA_KW), lambda b, n: (rb(b, n), C_K // GLA_KW)),
                pl.BlockSpec((GLA_CHUNK, GLA_VW), lambda b, n: (rb(b, n), C_V // GLA_VW)),
                pl.BlockSpec((GLA_CHUNK, GLA_VW), lambda b, n: (rb(b, n), C_Z // GLA_VW)),
                pl.BlockSpec((GLA_CHUNK, LANE), lambda b, n: (rb(b, n), C_LR // LANE)),
                pl.BlockSpec((LANE, GLA_KW), lambda b, n: (0, 0)),
                pl.BlockSpec((1, GLA_KW), lambda b, n: (0, 0)),
                pl.BlockSpec((1, GLA_DV), lambda b, n: (0, 0))]


def _gla_fwd(proj, gw_pad, gate_b, gla_norm_g, bsz, lp):
    n_chunks = lp // GLA_CHUNK
    tp = bsz * lp

    def body(q_ref, k_ref, v_ref, z_ref, lr_ref, gw_ref, gb_ref, gn_ref, oraw_ref, ya_ref, sall_ref, st_scr):
        n = pl.program_id(1)

        @pl.when(n == 0)
        def _():
            st_scr[...] = jnp.zeros_like(st_scr)

        c = _gla_gates(q_ref, k_ref, lr_ref, gw_ref, gb_ref, n)
        qe_b, ke_b, kl_b = _bf(c["qe"]), _bf(c["ke"]), _bf(c["kl"])
        gn = gn_ref[...]
        for h in range(GLA_HEADS):
            ks, vs = slice(h * GLA_DK, (h + 1) * GLA_DK), slice(h * GLA_DV, (h + 1) * GLA_DV)
            st = st_scr[h]
            sall_ref[0, 0, h] = st
            v = v_ref[:, vs]
            a = jnp.where(c["tril"], _dot_nt(qe_b[:, ks], ke_b[:, ks]), 0.0)
            o = _dot(_bf(a), v) + _dot_nt(qe_b[:, ks], _bf(st))
            st_scr[h] = st * c["ebl"][:, ks] + _dot_tn(v, kl_b[:, ks])
            oraw_ref[:, vs] = o
            r = lax.rsqrt(jnp.mean(o * o, axis=-1, keepdims=True) + EPS)
            zg = z_ref[:, vs].astype(F32)
            ya_ref[:, vs] = _bf((o * r * gn) * (zg * _sigmoid(zg)))

    rb, in_specs = _gla_in_specs(n_chunks, False)
    return pl.pallas_call(
        body, name="gla_fwd", grid=(bsz, n_chunks), in_specs=in_specs,
        out_specs=[pl.BlockSpec((GLA_CHUNK, GLA_VW), lambda b, n: (rb(b, n), 0)),
                   pl.BlockSpec((GLA_CHUNK, GLA_VW), lambda b, n: (rb(b, n), 0)),
                   pl.BlockSpec((1, 1, GLA_HEADS, GLA_DV, GLA_DK), lambda b, n: (b, n, 0, 0, 0))],
        out_shape=[jax.ShapeDtypeStruct((tp, GLA_VW), F32), jax.ShapeDtypeStruct((tp, GLA_VW), BF16),
                   jax.ShapeDtypeStruct((bsz, n_chunks, GLA_HEADS, GLA_DV, GLA_DK), F32)],
        scratch_shapes=[pltpu.VMEM((GLA_HEADS, GLA_DV, GLA_DK), F32)],
        compiler_params=_cp(("parallel", "arbitrary")),
    )(proj, proj, proj, proj, proj, gw_pad, gate_b, gla_norm_g)


def _gla_bwd(proj, gw_pad, gate_b, gla_norm_g, o_raw, s_all, d_ya, bsz, lp):
    n_chunks = lp // GLA_CHUNK
    tp = bsz * lp

    def body(q_ref, k_ref, v_ref, z_ref, lr_ref, gw_ref, gb_ref, gn_ref, o_ref, s_ref, dya_ref,
             dq_ref, dk_ref, dv_ref, dzg_ref, dz_ref, dgn_ref, dst_scr):
        @pl.when(jnp.logical_and(pl.program_id(0) == 0, pl.program_id(1) == 0))
        def _():
            dgn_ref[...] = jnp.zeros_like(dgn_ref)

        @pl.when(pl.program_id(1) == 0)
        def _():
            dst_scr[...] = jnp.zeros_like(dst_scr)

        n = n_chunks - 1 - pl.program_id(1)
        c = _gla_gates(q_ref, k_ref, lr_ref, gw_ref, gb_ref, n)
        qe_b, ke_b, kl_b = _bf(c["qe"]), _bf(c["ke"]), _bf(c["kl"])
        gn = gn_ref[...]
        dgn = jnp.zeros((1, GLA_DV), F32)
        dqe_h, dke_h, dkl_h, dbl_h = [], [], [], []
        for h in range(GLA_HEADS):
            ks, vs = slice(h * GLA_DK, (h + 1) * GLA_DK), slice(h * GLA_DV, (h + 1) * GLA_DV)
            v = v_ref[:, vs]
            st = s_ref[0, 0, h]
            dst = dst_scr[h]
            o = o_ref[:, vs]
            r = lax.rsqrt(jnp.mean(o * o, axis=-1, keepdims=True) + EPS)
            xh = o * r
            zg = z_ref[:, vs].astype(F32)
            sg = _sigmoid(zg)
            dy = dya_ref[:, vs].astype(F32)
            dzg_ref[:, vs] = _bf(dy * (xh * gn) * (sg * (1.0 + zg * (1.0 - sg))))
            t = dy * (zg * sg)
            dgn += jnp.sum(t * xh, axis=0, keepdims=True)
            dxh = t * gn
            do_b = _bf(r * (dxh - xh * jnp.mean(dxh * xh, axis=-1, keepdims=True)))
            dst_b = _bf(dst)
            a = jnp.where(c["tril"], _dot_nt(qe_b[:, ks], ke_b[:, ks]), 0.0)
            da_b = _bf(jnp.where(c["tril"], _dot_nt(do_b, v), 0.0))
            dqe_h.append(_dot(da_b, ke_b[:, ks]) + _dot(do_b, _bf(st)))
            dke_h.append(_dot_tn(da_b, qe_b[:, ks]))
            dkl = _dot(v, dst_b)
            dkl_h.append(dkl)
            dv_ref[:, vs] = _bf(_dot_tn(_bf(a), do_b) + _dot_nt(kl_b[:, ks], dst_b))
            ddecay = jnp.sum(dst * st, axis=0, keepdims=True)
            dbl_h.append(jnp.sum(dkl * c["kl"][:, ks], axis=0, keepdims=True) + ddecay * c["ebl"][:, ks])
            dst_scr[h] = dst * c["ebl"][:, ks] + _dot_tn(do_b, qe_b[:, ks])
        dgn_ref[...] += dgn
        dqe, dke, dkl = (jnp.concatenate(p, axis=1) for p in (dqe_h, dke_h, dkl_h))
        dbl = jnp.concatenate(dbl_h, axis=1)
        db = dqe * c["qe"] - dke * c["ke"] - dkl * c["kl"] + jnp.where(c["row"] == GLA_CHUNK - 1, dbl, 0.0)
        ri = lax.broadcasted_iota(jnp.int32, (GLA_CHUNK, GLA_CHUNK), 0)
        ci = lax.broadcasted_iota(jnp.int32, (GLA_CHUNK, GLA_CHUNK), 1)
        dg = jnp.dot((ci >= ri).astype(F32), db, precision=lax.Precision.HIGHEST, preferred_element_type=F32)
        dg = jnp.where(c["live"], dg, 0.0)
        dz_ref[...] = dg * (1.0 / GLA_GATE_NORMALIZER) * _sigmoid(-c["z"])
        dq_ref[...] = _bf(dqe * c["eb"] * (GLA_DK ** -0.5))
        dk_ref[...] = _bf(dke * c["enb"] + dkl * c["elb"])

    rb, in_specs = _gla_in_specs(n_chunks, True)
    wide = pl.BlockSpec((GLA_CHUNK, GLA_VW), lambda b, n: (rb(b, n), 0))
    narrow = pl.BlockSpec((GLA_CHUNK, GLA_KW), lambda b, n: (rb(b, n), 0))
    return pl.pallas_call(
        body, name="gla_bwd", grid=(bsz, n_chunks),
        in_specs=in_specs + [wide, pl.BlockSpec((1, 1, GLA_HEADS, GLA_DV, GLA_DK),
                                                lambda b, n: (b, n_chunks - 1 - n, 0, 0, 0)), wide],
        out_specs=[narrow, narrow, wide, wide, narrow, pl.BlockSpec((1, GLA_DV), lambda b, n: (0, 0))],
        out_shape=[jax.ShapeDtypeStruct((tp, GLA_KW), BF16), jax.ShapeDtypeStruct((tp, GLA_KW), BF16),
                   jax.ShapeDtypeStruct((tp, GLA_VW), BF16), jax.ShapeDtypeStruct((tp, GLA_VW), BF16),
                   jax.ShapeDtypeStruct((tp, GLA_KW), F32), jax.ShapeDtypeStruct((1, GLA_DV), F32)],
        scratch_shapes=[pltpu.VMEM((GLA_HEADS, GLA_DV, GLA_DK), F32)],
        compiler_params=_cp(("arbitrary", "arbitrary")),
    )(proj, proj, proj, proj, proj, gw_pad, gate_b, gla_norm_g, o_raw, s_all, d_ya)


def _gate_bwd(dz, proj, gw_pad_t):
    tp = dz.shape[0]
    tm = _big_tok(tp)

    def body(dz_ref, lr_ref, gwt_ref, dlr_ref, dgw_ref, dgb_ref):
        @pl.when(pl.program_id(0) == 0)
        def _():
            dgw_ref[...] = jnp.zeros_like(dgw_ref)
            dgb_ref[...] = jnp.zeros_like(dgb_ref)

        dz = dz_ref[...]
        dz_b = _bf(dz)
        dlr_ref[...] = _bf(_dot(dz_b, gwt_ref[...]))
        dgw_ref[...] += _dot_tn(lr_ref[...], dz_b)
        dgb_ref[...] += jnp.sum(dz, axis=0, keepdims=True)

    return pl.pallas_call(
        body, name="gate_bwd", grid=(tp // tm,),
        in_specs=[pl.BlockSpec((tm, GLA_KW), lambda i: (i, 0)),
                  pl.BlockSpec((tm, LANE), lambda i: (i, C_LR // LANE)),
                  pl.BlockSpec((GLA_KW, LANE), lambda i: (0, 0))],
        out_specs=[pl.BlockSpec((tm, LANE), lambda i: (i, 0)),
                   pl.BlockSpec((LANE, GLA_KW), lambda i: (0, 0)),
                   pl.BlockSpec((1, GLA_KW), lambda i: (0, 0))],
        out_shape=[jax.ShapeDtypeStruct((tp, LANE), BF16), jax.ShapeDtypeStruct((LANE, GLA_KW), F32),
                   jax.ShapeDtypeStruct((1, GLA_KW), F32)],
        compiler_params=_cp(("arbitrary",)),
    )(dz, proj, gw_pad_t)


def _rms_fwd(x):
    r = lax.rsqrt(jnp.mean(x * x, axis=-1, keepdims=True) + EPS)
    return x * r, r


def _rms_bwd(dy, xh, r, g):
    dxh = dy * g
    dx = r * (dxh - xh * jnp.mean(dxh * xh, axis=-1, keepdims=True))
    return dx, jnp.sum(dy * xh, axis=0, keepdims=True)


def _q_up(proj, q_norm_g, wn, wr, wt, cos_t, sin_t, bsz, lp):
    tp = bsz * lp
    nb = lp // TOK

    def body(cq_ref, g_ref, wn_ref, wr_ref, wt_ref, cos_ref, sin_ref, q_ref):
        xh, _ = _rms_fwd(cq_ref[...].astype(F32))
        cqn = _bf(xh * g_ref[...])
        nope = _dot(cqn, wn_ref[...])
        rope = _dot(cqn, wr_ref[...])
        rot = _dot(cqn, wt_ref[...])
        cos, sin = cos_ref[...], sin_ref[...]
        for h in range(MLA_HEADS):
            sl = slice(h * LANE, (h + 1) * LANE)
            q_ref[:, h * QKW:h * QKW + LANE] = _bf(nope[:, sl])
            q_ref[:, h * QKW + LANE:(h + 1) * QKW] = _bf(rope[:, sl] * cos + rot[:, sl] * sin)

    wspec = pl.BlockSpec((MLA_QR, MLA_HEADS * LANE), lambda b, i: (0, 0))
    tspec = pl.BlockSpec((TOK, LANE), lambda b, i: (i, 0))
    return pl.pallas_call(
        body, name="mla_q_up", grid=(bsz, nb),
        in_specs=[pl.BlockSpec((TOK, MLA_QR), lambda b, i: (b * nb + i, C_CQ // MLA_QR)),
                  pl.BlockSpec((1, MLA_QR), lambda b, i: (0, 0)), wspec, wspec, wspec, tspec, tspec],
        out_specs=pl.BlockSpec((TOK, MLA_HEADS * QKW), lambda b, i: (b * nb + i, 0)),
        out_shape=jax.ShapeDtypeStruct((tp, MLA_HEADS * QKW), BF16),
        compiler_params=_cp(("parallel", "parallel")),
    )(proj, q_norm_g, wn, wr, wt, cos_t, sin_t)


def _kv_up(proj, kv_norm_g, wk, wv, cos_t, sin_t, bsz, lp):
    tp = bsz * lp
    nb = lp // TOK

    def body(ckv_ref, kr_ref, krot_ref, g_ref, wk_ref, wv_ref, cos_ref, sin_ref, k_ref, v_ref):
        xh, _ = _rms_fwd(ckv_ref[...].astype(F32))
        cn = _bf(xh * g_ref[...])
        kn = _dot(cn, wk_ref[...])
        v_ref[...] = _bf(_dot(cn, wv_ref[...]))
        kr = _bf(kr_ref[...].astype(F32) * cos_ref[...] + krot_ref[...].astype(F32) * sin_ref[...])
        for h in range(MLA_HEADS):
            k_ref[:, h * QKW:h * QKW + LANE] = _bf(kn[:, h * LANE:(h + 1) * LANE])
            k_ref[:, h * QKW + LANE:(h + 1) * QKW] = kr

    wspec = pl.BlockSpec((MLA_KVR, MLA_HEADS * LANE), lambda b, i: (0, 0))
    tspec = pl.BlockSpec((TOK, LANE), lambda b, i: (i, 0))
    return pl.pallas_call(
        body, name="mla_kv_up", grid=(bsz, nb),
        in_specs=[pl.BlockSpec((TOK, LANE), lambda b, i: (b * nb + i, C_CKV // LANE)),
                  pl.BlockSpec((TOK, LANE), lambda b, i: (b * nb + i, C_KR // LANE)),
                  pl.BlockSpec((TOK, LANE), lambda b, i: (b * nb + i, C_KROT // LANE)),
                  pl.BlockSpec((1, MLA_KVR), lambda b, i: (0, 0)), wspec, wspec, tspec, tspec],
        out_specs=[pl.BlockSpec((TOK, MLA_HEADS * QKW), lambda b, i: (b * nb + i, 0)),
                   pl.BlockSpec((TOK, MLA_HEADS * LANE), lambda b, i: (b * nb + i, 0))],
        out_shape=[jax.ShapeDtypeStruct((tp, MLA_HEADS * QKW), BF16),
                   jax.ShapeDtypeStruct((tp, MLA_HEADS * LANE), BF16)],
        compiler_params=_cp(("parallel", "parallel")),
    )(proj, proj, proj, kv_norm_g, wk, wv, cos_t, sin_t)


ATT_SCALE = MLA_QK ** -0.5


def _attn_mask(r0, tq, kmax):
    qpos = r0 + lax.broadcasted_iota(jnp.int32, (tq, kmax), 0)
    kpos = lax.broadcasted_iota(jnp.int32, (tq, kmax), 1)
    real = kpos >= FRONT
    if r0 < FRONT:
        real = jnp.logical_or(real, qpos < FRONT)
    return jnp.logical_and(kpos <= qpos, real)


def _attn_fwd(qf, kf, vf, proj, bsz, lp):
    tp = bsz * lp
    tq = _attn_block(lp)

    def body(q_ref, k_ref, v_ref, mz_ref, ob_ref, yb_ref, lse_ref):
        for r0 in range(0, lp, tq):
            rows, kmax = slice(r0, r0 + tq), r0 + tq
            s = _dot_nt(q_ref[rows, :], k_ref[0:kmax, :]) * ATT_SCALE
            s = jnp.where(_attn_mask(r0, tq, kmax), s, NEG)
            m = jnp.max(s, axis=-1, keepdims=True)
            p = jnp.exp(s - m)
            l = jnp.sum(p, axis=-1, keepdims=True)
            o = _dot(_bf(p), v_ref[0:kmax, :]) / l
            ob_ref[rows, :] = _bf(o)
            mz = mz_ref[rows, :].astype(F32)
            yb_ref[rows, :] = _bf(o * (mz * _sigmoid(mz)))
            lse_ref[0, 0, rows, :] = jnp.broadcast_to(m + jnp.log(l), (tq, LANE))

    head = lambda off: pl.BlockSpec((lp, MLA_DV), lambda b, h: (b, off + h))
    return pl.pallas_call(
        body, name="mla_attn_fwd", grid=(bsz, MLA_HEADS),
        in_specs=[pl.BlockSpec((lp, QKW), lambda b, h: (b, h)), pl.BlockSpec((lp, QKW), lambda b, h: (b, h)),
                  head(0), head(C_MZ // MLA_DV)],
        out_specs=[head(0), head(0), pl.BlockSpec((1, 1, lp, LANE), lambda b, h: (b, h, 0, 0))],
        out_shape=[jax.ShapeDtypeStruct((tp, MLA_HEADS * MLA_DV), BF16),
                   jax.ShapeDtypeStruct((tp, MLA_HEADS * MLA_DV), BF16),
                   jax.ShapeDtypeStruct((bsz, MLA_HEADS, lp, LANE), F32)],
        compiler_params=_cp(("parallel", "parallel"), 56),
    )(qf, kf, vf, proj)


def _attn_bwd_pre(d_yb, proj, o_b, bsz, lp):
    tp = bsz * lp
    nb = lp // TOK
    w = MLA_HEADS * MLA_DV

    def body(dy_ref, mz_ref, o_ref, do_ref, dmz_ref, dl_ref):
        dy = dy_ref[...].astype(F32)
        mz = mz_ref[...].astype(F32)
        o = o_ref[...].astype(F32)
        s = _sigmoid(mz)
        do = _bf(dy * (mz * s))
        do_ref[...] = do
        dmz_ref[...] = _bf(dy * o * (s * (1.0 + mz * (1.0 - s))))
        prod = do.astype(F32) * o
        for h in range(MLA_HEADS):
            dl = jnp.sum(prod[:, h * MLA_DV:(h + 1) * MLA_DV], axis=-1, keepdims=True)
            dl_ref[0, h] = jnp.broadcast_to(dl, (TOK, LANE))

    return pl.pallas_call(
        body, name="mla_attn_bwd_pre", grid=(bsz, nb),
        in_specs=[pl.BlockSpec((TOK, w), lambda b, i: (b * nb + i, 0)),
                  pl.BlockSpec((TOK, w), lambda b, i: (b * nb + i, C_MZ // w)),
                  pl.BlockSpec((TOK, w), lambda b, i: (b * nb + i, 0))],
        out_specs=[pl.BlockSpec((TOK, w), lambda b, i: (b * nb + i, 0)),
                   pl.BlockSpec((TOK, w), lambda b, i: (b * nb + i, 0)),
                   pl.BlockSpec((1, MLA_HEADS, TOK, LANE), lambda b, i: (b, 0, i, 0))],
        out_shape=[jax.ShapeDtypeStruct((tp, w), BF16), jax.ShapeDtypeStruct((tp, w), BF16),
                   jax.ShapeDtypeStruct((bsz, MLA_HEADS, lp, LANE), F32)],
        compiler_params=_cp(("parallel", "parallel")),
    )(d_yb, proj, o_b)


def _attn_bwd(qf, kf, vf, d_o, lse, delta, bsz, lp):
    tp = bsz * lp
    tq = _attn_block(lp)

    def body(q_ref, k_ref, v_ref, do_ref, lse_ref, dl_ref, dq_ref, dk_ref, dv_ref, dk_acc, dv_acc):
        dk_acc[...] = jnp.zeros_like(dk_acc)
        dv_acc[...] = jnp.zeros_like(dv_acc)
        for r0 in range(0, lp, tq):
            rows, kmax = slice(r0, r0 + tq), r0 + tq
            q, do = q_ref[rows, :], do_ref[rows, :]
            k, v = k_ref[0:kmax, :], v_ref[0:kmax, :]
            s = _dot_nt(q, k) * ATT_SCALE
            p = jnp.where(_attn_mask(r0, tq, kmax), jnp.exp(s - lse_ref[0, 0, rows, :][:, :1]), 0.0)
            ds = _bf(p * (_dot_nt(do, v) - dl_ref[0, 0, rows, :][:, :1]) * ATT_SCALE)
            dq_ref[rows, :] = _bf(_dot(ds, k))
            dk_acc[0:kmax, :] += _dot_tn(ds, q)
            dv_acc[0:kmax, :] += _dot_tn(_bf(p), do)
        dk_ref[...] = _bf(dk_acc[...])
        dv_ref[...] = _bf(dv_acc[...])

    wide = pl.BlockSpec((lp, QKW), lambda b, h: (b, h))
    narrow = pl.BlockSpec((lp, MLA_DV), lambda b, h: (b, h))
    stat = pl.BlockSpec((1, 1, lp, LANE), lambda b, h: (b, h, 0, 0))
    return pl.pallas_call(
        body, name="mla_attn_bwd", grid=(bsz, MLA_HEADS),
        in_specs=[wide, wide, narrow, narrow, stat, stat], out_specs=[wide, wide, narrow],
        out_shape=[jax.ShapeDtypeStruct((tp, MLA_HEADS * QKW), BF16), jax.ShapeDtypeStruct((tp, MLA_HEADS * QKW), BF16),
                   jax.ShapeDtypeStruct((tp, MLA_HEADS * MLA_DV), BF16)],
        scratch_shapes=[pltpu.VMEM((lp, QKW), F32), pltpu.VMEM((lp, MLA_DV), F32)],
        compiler_params=_cp(("parallel", "parallel"), 56),
    )(qf, kf, vf, d_o, lse, delta)


def _q_up_bwd(dqf, proj, q_norm_g, wn_t, wr_t, wt_t, cos_t, sin_t, bsz, lp):
    tp = bsz * lp
    nb = lp // TOK
    hw = MLA_HEADS * LANE

    def body(dq_ref, cq_ref, g_ref, wn_ref, wr_ref, wt_ref, cos_ref, sin_ref,
             dcq_ref, dwn_ref, dwr_ref, dwt_ref, dg_ref):
        @pl.when(jnp.logical_and(pl.program_id(0) == 0, pl.program_id(1) == 0))
        def _():
            for r in (dwn_ref, dwr_ref, dwt_ref, dg_ref):
                r[...] = jnp.zeros_like(r)

        g = g_ref[...]
        xh, r = _rms_fwd(cq_ref[...].astype(F32))
        cqn = _bf(xh * g)
        cos, sin = cos_ref[...], sin_ref[...]
        dcqn = jnp.zeros((TOK, MLA_QR), F32)
        for h in range(MLA_HEADS):
            sl = slice(h * LANE, (h + 1) * LANE)
            dn = dq_ref[:, h * QKW:h * QKW + LANE]
            dr = dq_ref[:, h * QKW + LANE:(h + 1) * QKW].astype(F32)
            dr_c, dr_s = _bf(dr * cos), _bf(dr * sin)
            dcqn += _dot(dn, wn_ref[sl, :]) + _dot(dr_c, wr_ref[sl, :]) + _dot(dr_s, wt_ref[sl, :])
            dwn_ref[:, sl] += _dot_tn(cqn, dn)
            dwr_ref[:, sl] += _dot_tn(cqn, dr_c)
            dwt_ref[:, sl] += _dot_tn(cqn, dr_s)
        dx, dg = _rms_bwd(dcqn, xh, r, g)
        dcq_ref[...] = _bf(dx)
        dg_ref[...] += dg

    wspec = pl.BlockSpec((hw, MLA_QR), lambda b, i: (0, 0))
    aspec = pl.BlockSpec((MLA_QR, hw), lambda b, i: (0, 0))
    tspec = pl.BlockSpec((TOK, LANE), lambda b, i: (i, 0))
    return pl.pallas_call(
        body, name="mla_q_up_bwd", grid=(bsz, nb),
        in_specs=[pl.BlockSpec((TOK, MLA_HEADS * QKW), lambda b, i: (b * nb + i, 0)),
                  pl.BlockSpec((TOK, MLA_QR), lambda b, i: (b * nb + i, C_CQ // MLA_QR)),
                  pl.BlockSpec((1, MLA_QR), lambda b, i: (0, 0)), wspec, wspec, wspec, tspec, tspec],
        out_specs=[pl.BlockSpec((TOK, MLA_QR), lambda b, i: (b * nb + i, 0)), aspec, aspec, aspec,
                   pl.BlockSpec((1, MLA_QR), lambda b, i: (0, 0))],
        out_shape=[jax.ShapeDtypeStruct((tp, MLA_QR), BF16)] + [jax.ShapeDtypeStruct((MLA_QR, hw), F32)] * 3
        + [jax.ShapeDtypeStruct((1, MLA_QR), F32)],
        compiler_params=_cp(("arbitrary", "arbitrary")),
    )(dqf, proj, q_norm_g, wn_t, wr_t, wt_t, cos_t, sin_t)


def _kv_up_bwd(dkf, dvf, proj, kv_norm_g, wk_t, wv_t, cos_t, sin_t, bsz, lp):
    tp = bsz * lp
    nb = lp // TOK
    hw = MLA_HEADS * LANE

    def body(dk_ref, dv_ref, ckv_ref, g_ref, wk_ref, wv_ref, cos_ref, sin_ref,
             dckv_ref, dkr_ref, dkrot_ref, dwk_ref, dwv_ref, dg_ref):
        @pl.when(jnp.logical_and(pl.program_id(0) == 0, pl.program_id(1) == 0))
        def _():
            for r in (dwk_ref, dwv_ref, dg_ref):
                r[...] = jnp.zeros_like(r)

        g = g_ref[...]
        xh, r = _rms_fwd(ckv_ref[...].astype(F32))
        cn = _bf(xh * g)
        dv = dv_ref[...]
        dcn = _dot(dv, wv_ref[...])
        dwv_ref[...] += _dot_tn(cn, dv)
        drope = jnp.zeros((TOK, LANE), F32)
        for h in range(MLA_HEADS):
            sl = slice(h * LANE, (h + 1) * LANE)
            dn = dk_ref[:, h * QKW:h * QKW + LANE]
            drope += dk_ref[:, h * QKW + LANE:(h + 1) * QKW].astype(F32)
            dcn += _dot(dn, wk_ref[sl, :])
            dwk_ref[:, sl] += _dot_tn(cn, dn)
        dkr_ref[...] = _bf(drope * cos_ref[...])
        dkrot_ref[...] = _bf(drope * sin_ref[...])
        dx, dg = _rms_bwd(dcn, xh, r, g)
        dckv_ref[...] = _bf(dx)
        dg_ref[...] += dg

    wspec = pl.BlockSpec((hw, MLA_KVR), lambda b, i: (0, 0))
    aspec = pl.BlockSpec((MLA_KVR, hw), lambda b, i: (0, 0))
    tspec = pl.BlockSpec((TOK, LANE), lambda b, i: (i, 0))
    ospec = pl.BlockSpec((TOK, LANE), lambda b, i: (b * nb + i, 0))
    return pl.pallas_call(
        body, name="mla_kv_up_bwd", grid=(bsz, nb),
        in_specs=[pl.BlockSpec((TOK, MLA_HEADS * QKW), lambda b, i: (b * nb + i, 0)),
                  pl.BlockSpec((TOK, hw), lambda b, i: (b * nb + i, 0)),
                  pl.BlockSpec((TOK, LANE), lambda b, i: (b * nb + i, C_CKV // LANE)),
                  pl.BlockSpec((1, MLA_KVR), lambda b, i: (0, 0)), wspec, wspec, tspec, tspec],
        out_specs=[ospec, ospec, ospec, aspec, aspec, pl.BlockSpec((1, MLA_KVR), lambda b, i: (0, 0))],
        out_shape=[jax.ShapeDtypeStruct((tp, LANE), BF16)] * 3 + [jax.ShapeDtypeStruct((MLA_KVR, hw), F32)] * 2
        + [jax.ShapeDtypeStruct((1, MLA_KVR), F32)],
        compiler_params=_cp(("arbitrary", "arbitrary")),
    )(dkf, dvf, proj, kv_norm_g, wk_t, wv_t, cos_t, sin_t)


def _mid_fwd(ya_in, yb_in, proj, hp, target, w_gp, w_mp, w_o, final_g, bsz, lp):
    tp = bsz * lp
    nb = lp // TOK

    def body(ya_ref, yb_ref, gg_ref, gm_ref, h_ref, t_ref, wgp_ref, wmp_ref, wo_ref, fg_ref,
             ya_out, yb_out, dh_ref, loss_ref, dfg_ref):
        @pl.when(jnp.logical_and(pl.program_id(0) == 0, pl.program_id(1) == 0))
        def _():
            loss_ref[...] = jnp.zeros_like(loss_ref)
            dfg_ref[...] = jnp.zeros_like(dfg_ref)

        y_a = _dot(ya_ref[...], wgp_ref[...])
        y_b = _dot(yb_ref[...], wmp_ref[...])
        ya_out[...] = _bf(y_a)
        yb_out[...] = _bf(y_b)
        merged = _sigmoid(gg_ref[...].astype(F32)) * y_a + _sigmoid(gm_ref[...].astype(F32)) * y_b
        h2 = h_ref[...] + _dot(_bf(merged), wo_ref[...])
        fg = fg_ref[...]
        xh, r = _rms_fwd(h2)
        pos = pl.program_id(1) * TOK + lax.broadcasted_iota(jnp.int32, (TOK, 1), 0)
        err = jnp.where(pos >= X0, xh * fg - t_ref[...], 0.0)
        loss_ref[...] += 0.5 * jnp.sum(jnp.mean(err * err, axis=-1, keepdims=True), axis=0, keepdims=True)
        dy = err * (1.0 / D_MODEL)
        dx, dfg = _rms_bwd(dy, xh, r, fg)
        dh_ref[...] = dx
        dfg_ref[...] += dfg

    tok = lambda c: pl.BlockSpec((TOK, D_MODEL), lambda b, i: (b * nb + i, c))
    wspec = pl.BlockSpec((D_MODEL, D_MODEL), lambda b, i: (0, 0))
    return pl.pallas_call(
        body, name="mid_fwd", grid=(bsz, nb),
        in_specs=[tok(0), tok(0), tok(C_GG // D_MODEL), tok(C_GM // D_MODEL), tok(0), tok(0),
                  wspec, wspec, wspec, pl.BlockSpec((1, D_MODEL), lambda b, i: (0, 0))],
        out_specs=[tok(0), tok(0), tok(0), pl.BlockSpec((1, LANE), lambda b, i: (0, 0)),
                   pl.BlockSpec((1, D_MODEL), lambda b, i: (0, 0))],
        out_shape=[jax.ShapeDtypeStruct((tp, D_MODEL), BF16), jax.ShapeDtypeStruct((tp, D_MODEL), BF16),
                   jax.ShapeDtypeStruct((tp, D_MODEL), F32), jax.ShapeDtypeStruct((1, LANE), F32),
                   jax.ShapeDtypeStruct((1, D_MODEL), F32)],
        compiler_params=_cp(("arbitrary", "arbitrary"), 48),
    )(ya_in, yb_in, proj, proj, hp, target, w_gp, w_mp, w_o, final_g)


def _mid_bwd(dh2, y_a, y_b, proj, ya_in, yb_in, w_o_t, w_gp_t, w_mp_t):
    tp = dh2.shape[0]
    tm = TOK
    nsteps = tp // tm

    def body(dh_ref, ya_ref, yb_ref, gg_ref, gm_ref, yai_ref, ybi_ref, wo_ref, wgp_ref, wmp_ref,
             dyai_ref, dybi_ref, dgg_ref, dgm_ref, dwo_ref, dwgp_ref, dwmp_ref, a_o, a_gp, a_mp):
        @pl.when(pl.program_id(0) == 0)
        def _():
            for r in (a_o, a_gp, a_mp):
                r[...] = jnp.zeros_like(r)

        dh = _bf(dh_ref[...])
        dm = _dot(dh, wo_ref[...])
        y_a, y_b = ya_ref[...].astype(F32), yb_ref[...].astype(F32)
        sg, sm = _sigmoid(gg_ref[...].astype(F32)), _sigmoid(gm_ref[...].astype(F32))
        d_ya, d_yb = _bf(sg * dm), _bf(sm * dm)
        dgg_ref[...] = _bf(dm * y_a * sg * (1.0 - sg))
        dgm_ref[...] = _bf(dm * y_b * sm * (1.0 - sm))
        a_o[...] += _dot_tn(_bf(sg * y_a + sm * y_b), dh)
        a_gp[...] += _dot_tn(yai_ref[...], d_ya)
        a_mp[...] += _dot_tn(ybi_ref[...], d_yb)
        dyai_ref[...] = _bf(_dot(d_ya, wgp_ref[...]))
        dybi_ref[...] = _bf(_dot(d_yb, wmp_ref[...]))

        @pl.when(pl.program_id(0) == nsteps - 1)
        def _():
            pltpu.sync_copy(a_o, dwo_ref)
            pltpu.sync_copy(a_gp, dwgp_ref)
            pltpu.sync_copy(a_mp, dwmp_ref)

    tok = lambda c: pl.BlockSpec((tm, D_MODEL), lambda i: (i, c))
    wspec = pl.BlockSpec((D_MODEL, D_MODEL), lambda i: (0, 0))
    anyspec = pl.BlockSpec(memory_space=pl.ANY)
    wshape = jax.ShapeDtypeStruct((D_MODEL, D_MODEL), F32)
    return pl.pallas_call(
        body, name="mid_bwd", grid=(nsteps,),
        in_specs=[tok(0), tok(0), tok(0), tok(C_GG // D_MODEL), tok(C_GM // D_MODEL), tok(0), tok(0),
                  wspec, wspec, wspec],
        out_specs=[tok(0), tok(0), tok(0), tok(0), anyspec, anyspec, anyspec],
        out_shape=[jax.ShapeDtypeStruct((tp, D_MODEL), BF16)] * 4 + [wshape] * 3,
        scratch_shapes=[pltpu.VMEM((D_MODEL, D_MODEL), F32)] * 3,
        compiler_params=_cp(("arbitrary",), 56),
    )(dh2, y_a, y_b, proj, proj, ya_in, yb_in, w_o_t, w_gp_t, w_mp_t)


def _dw_in(u, dproj):
    tp = u.shape[0]
    tm, tn = _big_tok(tp), 768

    def body(u_ref, d_ref, o_ref):
        @pl.when(pl.program_id(1) == 0)
        def _():
            o_ref[...] = jnp.zeros_like(o_ref)

        o_ref[...] += _dot_tn(u_ref[...], d_ref[...])

    return pl.pallas_call(
        body, name="dw_in", grid=(N_EXT // tn, tp // tm),
        in_specs=[pl.BlockSpec((tm, D_MODEL), lambda j, i: (i, 0)), pl.BlockSpec((tm, tn), lambda j, i: (i, j))],
        out_specs=pl.BlockSpec((D_MODEL, tn), lambda j, i: (0, j)),
        out_shape=jax.ShapeDtypeStruct((D_MODEL, N_EXT), F32),
        compiler_params=_cp(("parallel", "arbitrary"), 48),
    )(u, dproj)


def _dx_in(dproj, w_ext_t, hp, dh2, norm_g):
    tp = hp.shape[0]
    tm, tk = _big_tok(tp), 768
    nk = N_EXT // tk

    def body(d_ref, w_ref, h_ref, dh_ref, g_ref, o_ref, dg_ref, acc):
        k = pl.program_id(1)

        @pl.when(jnp.logical_and(pl.program_id(0) == 0, k == 0))
        def _():
            dg_ref[...] = jnp.zeros_like(dg_ref)

        @pl.when(k == 0)
        def _():
            acc[...] = jnp.zeros_like(acc)

        acc[...] += _dot(d_ref[...], w_ref[...])

        @pl.when(k == nk - 1)
        def _():
            g = g_ref[...]
            xh, r = _rms_fwd(h_ref[...])
            dx, dg = _rms_bwd(acc[...], xh, r, g)
            o_ref[...] = dh_ref[...] + dx
            dg_ref[...] += dg

    tok = pl.BlockSpec((tm, D_MODEL), lambda i, k: (i, 0))
    return pl.pallas_call(
        body, name="dx_in", grid=(tp // tm, nk),
        in_specs=[pl.BlockSpec((tm, tk), lambda i, k: (i, k)), pl.BlockSpec((tk, D_MODEL), lambda i, k: (k, 0)),
                  tok, tok, pl.BlockSpec((1, D_MODEL), lambda i, k: (0, 0))],
        out_specs=[tok, pl.BlockSpec((1, D_MODEL), lambda i, k: (0, 0))],
        out_shape=[jax.ShapeDtypeStruct((tp, D_MODEL), F32), jax.ShapeDtypeStruct((1, D_MODEL), F32)],
        scratch_shapes=[pltpu.VMEM((tm, D_MODEL), F32)],
        compiler_params=_cp(("arbitrary", "arbitrary"), 56),
    )(dproj, w_ext_t, hp, dh2, norm_g)


def _meta_grad(dhp3):
    bsz = dhp3.shape[0]

    def body(d_ref, o_ref):
        @pl.when(pl.program_id(0) == 0)
        def _():
            o_ref[...] = jnp.zeros_like(o_ref)

        o_ref[...] += d_ref[0]

    return pl.pallas_call(
        body, name="meta_grad", grid=(bsz,),
        in_specs=[pl.BlockSpec((1, N_META, D_MODEL), lambda b: (b, FRONT // N_META, 0))],
        out_specs=pl.BlockSpec((N_META, D_MODEL), lambda b: (0, 0)),
        out_shape=jax.ShapeDtypeStruct((N_META, D_MODEL), F32),
        compiler_params=_cp(("arbitrary",)),
    )(dhp3)


W_IN_SHARD = N_IN // N_DEV


def _pad_lanes(a, width=LANE):
    return jnp.pad(a, [(0, 0)] * (a.ndim - 1) + [(0, width - a.shape[-1])])


def _rot_cols(w):
    half = w.shape[-1] // 2
    return jnp.concatenate([-w[..., half:], w[..., :half]], axis=-1)


def _unrot_cols(dw):
    half = dw.shape[-1] // 2
    return jnp.concatenate([dw[..., half:], -dw[..., :half]], axis=-1)


def _w_in_cols(shards, lo, hi):
    parts = []
    for k in range(lo // W_IN_SHARD, (hi - 1) // W_IN_SHARD + 1):
        a, b = max(lo, k * W_IN_SHARD), min(hi, (k + 1) * W_IN_SHARD)
        parts.append(shards[k][:, a - k * W_IN_SHARD:b - k * W_IN_SHARD])
    return parts[0] if len(parts) == 1 else jnp.concatenate(parts, axis=1)


def _w_in_ext(shards):
    c = lambda lo, hi: _w_in_cols(shards, lo, hi)
    kr = c(O_KR, O_MZ)
    return jnp.concatenate([
        c(O_V, O_LR), c(O_Z, O_CQ), c(O_MZ, O_GG), c(O_GG, O_GM), c(O_GM, N_IN), c(O_Q, O_K), c(O_K, O_V),
        c(O_CQ, O_CKV), c(O_CKV, O_KR), _pad_lanes(kr), _pad_lanes(_rot_cols(kr)), _pad_lanes(c(O_LR, O_Z))], axis=1)


def _w_in_grad(dw):
    kr = dw[:, C_KR:C_KR + MLA_ROPE] + _unrot_cols(dw[:, C_KROT:C_KROT + MLA_ROPE])
    return jnp.concatenate([
        dw[:, C_Q:C_K], dw[:, C_K:C_CQ], dw[:, C_V:C_Z], dw[:, C_LR:C_LR + GLA_RANK], dw[:, C_Z:C_MZ],
        dw[:, C_CQ:C_CKV], dw[:, C_CKV:C_KR], kr, dw[:, C_MZ:C_GG], dw[:, C_GG:C_GM], dw[:, C_GM:C_Q]], axis=1)


def _rope_tables(lp):
    inv = 1.0 / (ROPE_BASE ** (jnp.arange(0, MLA_ROPE, 2, dtype=F32) / MLA_ROPE))
    ang = (jnp.arange(lp, dtype=F32) - FRONT)[:, None] * inv[None, :]
    cos, sin = jnp.cos(ang), jnp.sin(ang)
    return _pad_lanes(jnp.concatenate([cos, cos], axis=1)), _pad_lanes(jnp.concatenate([sin, sin], axis=1))


def _local_step(x, loss_target, w):
    bsz, seq, _ = x.shape
    lp = X0 + seq
    tp = bsz * lp
    assert lp % TOK == 0 and lp % GLA_CHUNK == 0
    meta = jnp.broadcast_to(w["meta_tokens"][None], (bsz, N_META, D_MODEL))
    hp = jnp.concatenate([jnp.zeros((bsz, FRONT, D_MODEL), F32), meta, x], axis=1).reshape(tp, D_MODEL)
    target = jnp.pad(loss_target, ((0, 0), (X0, 0), (0, 0))).reshape(tp, D_MODEL)
    cos_t, sin_t = _rope_tables(lp)

    w_ext = _w_in_ext(w["w_in"])
    gw_pad = jnp.pad(w["gla_gate_w"], ((0, LANE - GLA_RANK), (0, 0)))
    uq = w["mla_w_uq"].reshape(MLA_QR, MLA_HEADS, MLA_QK)
    rope_w = uq[:, :, MLA_NOPE:]
    hw = MLA_HEADS * LANE
    wn = uq[:, :, :MLA_NOPE].reshape(MLA_QR, hw)
    wr = _pad_lanes(rope_w).reshape(MLA_QR, hw)
    wt = _pad_lanes(_rot_cols(rope_w)).reshape(MLA_QR, hw)
    ukv = w["mla_w_ukv"].reshape(MLA_KVR, MLA_HEADS, MLA_NOPE + MLA_DV)
    wk = ukv[:, :, :MLA_NOPE].reshape(MLA_KVR, hw)
    wv = ukv[:, :, MLA_NOPE:].reshape(MLA_KVR, hw)

    u, proj = _proj_in(hp, w["norm_g"], w_ext)
    o_raw, ya_in, s_all = _gla_fwd(proj, gw_pad, w["gla_gate_b"], w["gla_norm_g"], bsz, lp)
    qf = _q_up(proj, w["mla_q_norm_g"], wn, wr, wt, cos_t, sin_t, bsz, lp)
    kf, vf = _kv_up(proj, w["mla_kv_norm_g"], wk, wv, cos_t, sin_t, bsz, lp)
    o_b, yb_in, lse = _attn_fwd(qf, kf, vf, proj, bsz, lp)
    y_a, y_b, dh2, loss, d_final_g = _mid_fwd(ya_in, yb_in, proj, hp, target, w["gla_proj"], w["mla_proj"],
                                              w["w_out"], w["final_norm_g"], bsz, lp)
    d_ya, d_yb, d_gg, d_gm, d_w_out, d_gla_proj, d_mla_proj = _mid_bwd(
        dh2, y_a, y_b, proj, ya_in, yb_in, w["w_out"].T, w["gla_proj"].T, w["mla_proj"].T)
    dq, dk, dv, d_z, d_gate, d_gla_norm = _gla_bwd(proj, gw_pad, w["gla_gate_b"], w["gla_norm_g"], o_raw, s_all,
                                                   d_ya, bsz, lp)
    d_lr, d_gw_pad, d_gate_b = _gate_bwd(d_gate, proj, gw_pad.T)
    d_o, d_mz, delta = _attn_bwd_pre(d_yb, proj, o_b, bsz, lp)
    dqf, dkf, dvf = _attn_bwd(qf, kf, vf, d_o, lse, delta, bsz, lp)
    d_cq, d_wn, d_wr, d_wt, d_qn = _q_up_bwd(dqf, proj, w["mla_q_norm_g"], wn.T, wr.T, wt.T, cos_t, sin_t, bsz, lp)
    d_ckv, d_kr, d_krot, d_wk, d_wv, d_kvn = _kv_up_bwd(dkf, dvf, proj, w["mla_kv_norm_g"], wk.T, wv.T,
                                                        cos_t, sin_t, bsz, lp)
    dproj = jnp.concatenate([dv, d_z, d_mz, d_gg, d_gm, dq, dk, d_cq, d_ckv, d_kr, d_krot, d_lr], axis=1)
    d_w_ext = _dw_in(u, dproj)
    d_hp, d_norm_g = _dx_in(dproj, w_ext.T, hp, dh2, w["norm_g"])
    d_hp3 = d_hp.reshape(bsz, lp, D_MODEL)

    d_rope = (d_wr.reshape(MLA_QR, MLA_HEADS, LANE)[:, :, :MLA_ROPE]
              + _unrot_cols(d_wt.reshape(MLA_QR, MLA_HEADS, LANE)[:, :, :MLA_ROPE]))
    d_uq = jnp.concatenate([d_wn.reshape(MLA_QR, MLA_HEADS, LANE), d_rope], axis=-1).reshape(MLA_QR, MLA_HEADS * MLA_QK)
    d_ukv = jnp.concatenate([d_wk.reshape(MLA_KVR, MLA_HEADS, LANE), d_wv.reshape(MLA_KVR, MLA_HEADS, LANE)],
                            axis=-1).reshape(MLA_KVR, MLA_HEADS * (MLA_NOPE + MLA_DV))
    grads = dict(meta_tokens=_meta_grad(d_hp3), norm_g=d_norm_g, w_in=_w_in_grad(d_w_ext),
                 gla_gate_w=d_gw_pad[:GLA_RANK], gla_gate_b=d_gate_b, gla_norm_g=d_gla_norm, gla_proj=d_gla_proj,
                 mla_q_norm_g=d_qn, mla_w_uq=d_uq, mla_kv_norm_g=d_kvn, mla_w_ukv=d_ukv, mla_proj=d_mla_proj,
                 w_out=d_w_out, final_norm_g=d_final_g)
    return loss, d_hp3[:, X0:, :], grads


MESH_ID = pl.DeviceIdType.MESH
PACKED = (("gla_gate_w", (GLA_RANK, GLA_KW // N_DEV), 1),
          ("gla_proj", (D_MODEL // N_DEV, D_MODEL), 0), ("mla_w_uq", (MLA_QR, MLA_HEADS * MLA_QK // N_DEV), 1),
          ("mla_w_ukv", (MLA_KVR, MLA_HEADS * (MLA_NOPE + MLA_DV) // N_DEV), 1),
          ("mla_proj", (D_MODEL // N_DEV, D_MODEL), 0), ("w_out", (D_MODEL // N_DEV, D_MODEL), 0),
          ("meta_tokens", (N_META, D_MODEL // N_DEV), 1))
REPLICATED = (("norm_g", D_MODEL), ("gla_gate_b", GLA_KW), ("gla_norm_g", GLA_DV), ("mla_q_norm_g", MLA_QR),
              ("mla_kv_norm_g", MLA_KVR), ("final_norm_g", D_MODEL))
PACK_ROWS = 3744
PACK_BLOCK = 1248
GATHER_ROWS = 3760
SMALL_ROWS = 32
LOSS_ROW = 25
W_IN_BLOCK = 128


def _my_place():
    return lax.axis_index("x"), lax.axis_index("y"), lax.axis_index("c")


def _all_gather(shards):
    n_arr = len(shards)

    def body(*refs):
        x_refs, out_refs = refs[:n_arr], refs[n_arr:2 * n_arr]
        send_sems, recv_sems, local_sems = refs[2 * n_arr:]
        x, y, c = _my_place()
        me, sibling = (x, y, c), (x, y, 1 - c)
        chips = [(1 - x, y), (x, 1 - y), (1 - x, 1 - y)]

        def copy(a, k, block, to, from_input=False):
            slab = out_refs[a].at[4 * block[0] + 2 * block[1] + block[2]]
            return pltpu.make_async_remote_copy(
                src_ref=x_refs[a] if from_input else slab, dst_ref=slab,
                send_sem=send_sems.at[7 * a + k], recv_sem=recv_sems.at[7 * a + k], device_id=to,
                device_id_type=MESH_ID)

        arrays = range(n_arr)
        mine = [pltpu.make_async_copy(x_refs[a], out_refs[a].at[4 * x + 2 * y + c], local_sems.at[a]) for a in arrays]
        for cp in mine:
            cp.start()
        first = [copy(a, 0, me, sibling, True) for a in arrays]
        first += [copy(a, 1 + j, me, (*chip, c), True) for j, chip in enumerate(chips) for a in arrays]
        for cp in first:
            cp.start()
        passed = []
        for j, chip in enumerate(chips):
            for a in arrays:
                copy(a, 1 + j, (*chip, c), me).wait_recv()
                passed.append(copy(a, 4 + j, (*chip, c), sibling))
                passed[-1].start()
        for a in arrays:
            copy(a, 0, sibling, me).wait_recv()
        for j, chip in enumerate(chips):
            for a in arrays:
                copy(a, 4 + j, (*chip, 1 - c), me).wait_recv()
        for cp in first + passed:
            cp.wait_send()
        for cp in mine:
            cp.wait()

    anyspec = pl.BlockSpec(memory_space=pl.ANY)
    return pl.pallas_call(
        body, name="weights_all_gather",
        out_shape=[jax.ShapeDtypeStruct((N_DEV,) + s.shape, s.dtype) for s in shards],
        in_specs=[anyspec] * n_arr, out_specs=[anyspec] * n_arr,
        scratch_shapes=[pltpu.SemaphoreType.DMA((7 * n_arr,)), pltpu.SemaphoreType.DMA((7 * n_arr,)),
                        pltpu.SemaphoreType.DMA((n_arr,))],
    )(*shards)


def _grad_exchange(slabs, small):
    n_arr = len(slabs)

    def body(*refs):
        g_refs, s_ref = refs[:n_arr], refs[n_arr]
        recv_refs, sall_ref = refs[n_arr + 1:2 * n_arr + 1], refs[2 * n_arr + 1]
        send_sems, recv_sems, local_sems = refs[2 * n_arr + 2:]
        x, y, c = _my_place()
        me = 4 * x + 2 * y + c
        own = [pltpu.make_async_copy(g_refs[a].at[me], recv_refs[a].at[me], local_sems.at[a]) for a in range(n_arr)]
        for cp in own:
            cp.start()
        sall_ref[me] = s_ref[...]
        sends, waits = [], []
        for d in range(1, N_DEV):
            px = 1 - x if d & 4 else x
            py = 1 - y if d & 2 else y
            pc = 1 - c if d & 1 else c
            peer = 4 * px + 2 * py + pc

            def copy(src, dst_ref, slot, k):
                return pltpu.make_async_remote_copy(
                    src_ref=src, dst_ref=dst_ref.at[slot], send_sem=send_sems.at[k], recv_sem=recv_sems.at[k],
                    device_id=(px, py, pc), device_id_type=MESH_ID)

            for a in range(n_arr + 1):
                k = (n_arr + 1) * (d - 1) + a
                src = s_ref if a == n_arr else g_refs[a].at[peer]
                dst = sall_ref if a == n_arr else recv_refs[a]
                sends.append(copy(src, dst, me, k))
                sends[-1].start()
                waits.append(copy(src, dst, peer, k))
        for cp in waits:
            cp.wait_recv()
        for cp in sends:
            cp.wait_send()
        for cp in own:
            cp.wait()

    anyspec = pl.BlockSpec(memory_space=pl.ANY)
    vmem = pl.BlockSpec(memory_space=pltpu.VMEM)
    n_sem = (n_arr + 1) * (N_DEV - 1)
    return pl.pallas_call(
        body, name="grad_exchange",
        out_shape=[jax.ShapeDtypeStruct(s.shape, s.dtype) for s in slabs]
        + [jax.ShapeDtypeStruct((N_DEV, SMALL_ROWS, LANE), F32)],
        in_specs=[anyspec] * n_arr + [vmem], out_specs=[anyspec] * n_arr + [vmem],
        scratch_shapes=[pltpu.SemaphoreType.DMA((n_sem,)), pltpu.SemaphoreType.DMA((n_sem,)),
                        pltpu.SemaphoreType.DMA((n_arr,))],
    )(*slabs, small)


def _adamw(parts, w, m, v, block_rows, name):
    rows, cols = w.shape

    def body(p_ref, w_ref, m_ref, v_ref, g_out, d_out, m_out, v_out):
        g = p_ref[0]
        for s in range(1, N_DEV):
            g = g + p_ref[s]
        m_new = ADAM_B1 * m_ref[...] + (1.0 - ADAM_B1) * g
        v_new = ADAM_B2 * v_ref[...] + (1.0 - ADAM_B2) * (g * g)
        m_hat = m_new / (1.0 - ADAM_B1 ** ADAM_STEP)
        v_hat = v_new / (1.0 - ADAM_B2 ** ADAM_STEP)
        g_out[...] = g
        d_out[...] = -ADAM_LR * (m_hat / (jnp.sqrt(v_hat) + ADAM_EPS) + ADAM_WD * w_ref[...])
        m_out[...] = m_new
        v_out[...] = v_new

    spec = pl.BlockSpec((block_rows, cols), lambda i: (i, 0))
    return pl.pallas_call(
        body, name=name, grid=(rows // block_rows,),
        in_specs=[pl.BlockSpec((N_DEV, block_rows, cols), lambda i: (0, i, 0)), spec, spec, spec],
        out_specs=[spec] * 4, out_shape=[jax.ShapeDtypeStruct((rows, cols), F32)] * 4,
        compiler_params=_cp(("parallel",), 48),
    )(parts, w, m, v)


def _pad_rows(flat, rows):
    pad = rows * LANE - flat.shape[-1]
    flat = jnp.pad(flat, [(0, 0)] * (flat.ndim - 1) + [(0, pad)])
    return flat.reshape(flat.shape[:-1] + (rows, LANE))


def _pack_shards(shards):
    return _pad_rows(jnp.concatenate([shards[n].reshape(-1) for n, _, _ in PACKED]), PACK_ROWS)


def _unpack_shards(packed):
    flat, out, off = packed.reshape(-1), {}, 0
    for n, shape, _ in PACKED:
        size = shape[0] * shape[1]
        out[n] = flat[off:off + size].reshape(shape)
        off += size
    return out


def _split8(full, axis):
    r, c = full.shape
    if axis == 0:
        return full.reshape(N_DEV, r // N_DEV, c)
    return full.reshape(r, N_DEV, c // N_DEV).transpose(1, 0, 2)


def _join8(shards, axis):
    _, r, c = shards.shape
    if axis == 0:
        return shards.reshape(N_DEV * r, c)
    return shards.transpose(1, 0, 2).reshape(r, N_DEV * c)


def _pack_small(vals, loss_row):
    rows = [vals[n].reshape(-1, LANE) for n, _ in REPLICATED] + [loss_row]
    flat = jnp.concatenate(rows, axis=0)
    return jnp.pad(flat, ((0, SMALL_ROWS - flat.shape[0]), (0, 0)))


def _unpack_small(packed):
    out, off = {}, 0
    for n, size in REPLICATED:
        out[n] = packed[off:off + size // LANE].reshape(1, size)
        off += size // LANE
    return out


def kernel(x, meta_tokens, norm_g, w_in, gla_gate_w, gla_gate_b, gla_norm_g, gla_proj, mla_q_norm_g, mla_w_uq, mla_kv_norm_g, mla_w_ukv, mla_proj, w_out, final_norm_g, loss_target, m_meta_tokens, m_norm_g, m_w_in, m_gla_gate_w, m_gla_gate_b, m_gla_norm_g, m_gla_proj, m_mla_q_norm_g, m_mla_w_uq, m_mla_kv_norm_g, m_mla_w_ukv, m_mla_proj, m_w_out, m_final_norm_g, v_meta_tokens, v_norm_g, v_w_in, v_gla_gate_w, v_gla_gate_b, v_gla_norm_g, v_gla_proj, v_mla_q_norm_g, v_mla_w_uq, v_mla_kv_norm_g, v_mla_w_ukv, v_mla_proj, v_w_out, v_final_norm_g):
    given = dict(meta_tokens=meta_tokens, norm_g=norm_g, w_in=w_in, gla_gate_w=gla_gate_w, gla_gate_b=gla_gate_b,
                 gla_norm_g=gla_norm_g, gla_proj=gla_proj, mla_q_norm_g=mla_q_norm_g, mla_w_uq=mla_w_uq,
                 mla_kv_norm_g=mla_kv_norm_g, mla_w_ukv=mla_w_ukv, mla_proj=mla_proj, w_out=w_out,
                 final_norm_g=final_norm_g)
    mom_m = dict(meta_tokens=m_meta_tokens, norm_g=m_norm_g, w_in=m_w_in, gla_gate_w=m_gla_gate_w,
                 gla_gate_b=m_gla_gate_b, gla_norm_g=m_gla_norm_g, gla_proj=m_gla_proj, mla_q_norm_g=m_mla_q_norm_g,
                 mla_w_uq=m_mla_w_uq, mla_kv_norm_g=m_mla_kv_norm_g, mla_w_ukv=m_mla_w_ukv, mla_proj=m_mla_proj,
                 w_out=m_w_out, final_norm_g=m_final_norm_g)
    mom_v = dict(meta_tokens=v_meta_tokens, norm_g=v_norm_g, w_in=v_w_in, gla_gate_w=v_gla_gate_w,
                 gla_gate_b=v_gla_gate_b, gla_norm_g=v_gla_norm_g, gla_proj=v_gla_proj, mla_q_norm_g=v_mla_q_norm_g,
                 mla_w_uq=v_mla_w_uq, mla_kv_norm_g=v_mla_kv_norm_g, mla_w_ukv=v_mla_w_ukv, mla_proj=v_mla_proj,
                 w_out=v_w_out, final_norm_g=v_final_norm_g)
    shapes = {n: a.shape for n, a in given.items()}
    shard2d = {n: s for n, s, _ in PACKED}
    shard2d["w_in"] = (D_MODEL, W_IN_SHARD)

    def as2d(tree):
        out = {n: tree[n].reshape(shard2d[n]) for n in shard2d}
        out.update({n: tree[n].reshape(1, size) for n, size in REPLICATED})
        return out

    w_loc, m_loc, v_loc = as2d(given), as2d(mom_m), as2d(mom_v)

    meta_bits = lax.bitcast_convert_type(w_loc["meta_tokens"], BF16).reshape(-1)
    flat = jnp.concatenate([w_loc[n].astype(BF16).reshape(-1) for n, _, _ in PACKED[:-1]] + [meta_bits])
    w_in_all, packed_all = _all_gather([w_loc["w_in"].astype(BF16), _pad_rows(flat, GATHER_ROWS)])
    packed_all = packed_all.reshape(N_DEV, -1)
    full, off = {"w_in": w_in_all}, 0
    for n, shape, axis in PACKED[:-1]:
        size = shape[0] * shape[1]
        full[n] = _join8(packed_all[:, off:off + size].reshape((N_DEV,) + shape), axis)
        off += size
    meta8 = lax.bitcast_convert_type(packed_all[:, off:off + 2 * N_META * LANE].reshape(N_DEV, N_META, LANE, 2), F32)
    full["meta_tokens"] = _join8(meta8, 1)
    for n, _ in REPLICATED:
        full[n] = w_loc[n]

    loss_part, grad_x, grads = _local_step(x, loss_target, full)

    packed_g = _pad_rows(jnp.concatenate([_split8(grads[n], axis).reshape(N_DEV, -1) for n, _, axis in PACKED], axis=1),
                         PACK_ROWS)
    small = _pack_small(grads, jnp.broadcast_to(loss_part[:, :1], (1, LANE)))
    w_in_parts, packed_parts, small_all = _grad_exchange([_split8(grads["w_in"], 1), packed_g], small)

    g_w, d_w, m_w, v_w = _adamw(w_in_parts, w_loc["w_in"], m_loc["w_in"], v_loc["w_in"], W_IN_BLOCK, "adamw_w_in")
    g_p, d_p, m_p, v_p = _adamw(packed_parts, _pack_shards(w_loc), _pack_shards(m_loc), _pack_shards(v_loc),
                                PACK_BLOCK, "adamw_packed")
    zero_row = jnp.zeros((1, LANE), F32)
    g_s, d_s, m_s, v_s = _adamw(small_all, _pack_small(w_loc, zero_row), _pack_small(m_loc, zero_row),
                                _pack_small(v_loc, zero_row), SMALL_ROWS, "adamw_replicated")
    loss = g_s[LOSS_ROW, 0]

    order = ["meta_tokens", "norm_g", "w_in", "gla_gate_w", "gla_gate_b", "gla_norm_g", "gla_proj", "mla_q_norm_g",
             "mla_w_uq", "mla_kv_norm_g", "mla_w_ukv", "mla_proj", "w_out", "final_norm_g"]
    result = [loss, grad_x]
    for w_in_out, packed_sh, packed_sm in ((g_w, g_p, g_s), (d_w, d_p, d_s), (m_w, m_p, m_s), (v_w, v_p, v_s)):
        tree = _unpack_shards(packed_sh)
        tree.update(_unpack_small(packed_sm))
        tree["w_in"] = w_in_out
        result += [tree[n].reshape(shapes[n]) for n in order]
    return tuple(result)
```

```python
import jax
import jax.numpy as jnp
from jax import lax
from jax.experimental import pallas as pl
from jax.experimental.pallas import tpu as pltpu

F32 = jnp.float32
BF16 = jnp.bfloat16

D_MODEL = 1024
N_META = 16
EPS = 1e-6
FRONT = 48
X0 = FRONT + N_META
GLA_HEADS, GLA_DK, GLA_DV, GLA_RANK, GLA_CHUNK = 4, 128, 256, 16, 64
GLA_GATE_NORMALIZER = 16.0
GLA_KW = GLA_HEADS * GLA_DK
GLA_VW = GLA_HEADS * GLA_DV
MLA_HEADS, MLA_NOPE, MLA_ROPE, MLA_DV, MLA_QR, MLA_KVR = 8, 128, 64, 128, 256, 128
MLA_QK = MLA_NOPE + MLA_ROPE
ROPE_BASE = 10000.0
LANE = 128
QKW = 2 * LANE

C_V, C_Z, C_Q, C_K = 0, 1024, 2048, 2560
C_MZ = 3072
C_GG, C_GM = 4096, 5120
C_CKV, C_KR, C_KROT, C_LR = 6144, 6272, 6400, 6528
C_CQ = 6656
N_EXT = 6912
O_Q, O_K, O_V, O_LR, O_Z, O_CQ, O_CKV, O_KR, O_MZ, O_GG, O_GM, N_IN = (
    0, 512, 1024, 2048, 2064, 3088, 3344, 3472, 3536, 4560, 5584, 6608)

ADAM_LR, ADAM_B1, ADAM_B2, ADAM_EPS, ADAM_WD, ADAM_STEP = 0.001, 0.9, 0.999, 1e-08, 0.01, 10

N_DEV = 8
TOK = 192
ATT_BLOCK = 352
NEG = -1e30


def _cp(sems=None, vmem_mb=None):
    kw = {}
    if sems is not None:
        kw["dimension_semantics"] = sems
    if vmem_mb is not None:
        kw["vmem_limit_bytes"] = vmem_mb * 1024 * 1024
    return pltpu.CompilerParams(**kw)


def _dot(a, b):
    return jnp.dot(a, b, preferred_element_type=F32)


def _dot_nt(a, b):
    return lax.dot_general(a, b, (((1,), (1,)), ((), ())), preferred_element_type=F32)


def _dot_tn(a, b):
    return lax.dot_general(a, b, (((0,), (0,)), ((), ())), preferred_element_type=F32)


def _sigmoid(x):
    return 1.0 / (1.0 + jnp.exp(-x))


def _bf(x):
    return x.astype(BF16)


def _big_tok(tp):
    return 4 * TOK if tp % (4 * TOK) == 0 else TOK


def _attn_block(lp):
    return ATT_BLOCK if lp % ATT_BLOCK == 0 else TOK


def _proj_in(hp, norm_g, w_ext):
    tp = hp.shape[0]
    tm, tn = _big_tok(tp), 768

    def body(h_ref, g_ref, w_ref, u_ref, o_ref, u_scr):
        @pl.when(pl.program_id(1) == 0)
        def _():
            x = h_ref[...]
            r = lax.rsqrt(jnp.mean(x * x, axis=-1, keepdims=True) + EPS)
            u = _bf(x * r * g_ref[...])
            u_scr[...] = u
            u_ref[...] = u

        o_ref[...] = _bf(_dot(u_scr[...], w_ref[...]))

    return pl.pallas_call(
        body, name="proj_in", grid=(tp // tm, N_EXT // tn),
        in_specs=[pl.BlockSpec((tm, D_MODEL), lambda i, j: (i, 0)),
                  pl.BlockSpec((1, D_MODEL), lambda i, j: (0, 0)),
                  pl.BlockSpec((D_MODEL, tn), lambda i, j: (0, j))],
        out_specs=[pl.BlockSpec((tm, D_MODEL), lambda i, j: (i, 0)),
                   pl.BlockSpec((tm, tn), lambda i, j: (i, j))],
        out_shape=[jax.ShapeDtypeStruct((tp, D_MODEL), BF16), jax.ShapeDtypeStruct((tp, N_EXT), BF16)],
        scratch_shapes=[pltpu.VMEM((tm, D_MODEL), BF16)],
        compiler_params=_cp(("parallel", "arbitrary"), 48),
    )(hp, norm_g, w_ext)


def _gla_gates(q_ref, k_ref, lr_ref, gw_ref, gb_ref, n):
    z = _dot(lr_ref[...], gw_ref[...]) + gb_ref[...]
    logsig = jnp.minimum(z, 0.0) - jnp.log(1.0 + jnp.exp(-jnp.abs(z)))
    row = lax.broadcasted_iota(jnp.int32, (GLA_CHUNK, GLA_KW), 0)
    live = jnp.logical_or(n > 0, row >= FRONT)
    g = jnp.where(live, logsig * (1.0 / GLA_GATE_NORMALIZER), 0.0)
    ri = lax.broadcasted_iota(jnp.int32, (GLA_CHUNK, GLA_CHUNK), 0)
    ci = lax.broadcasted_iota(jnp.int32, (GLA_CHUNK, GLA_CHUNK), 1)
    tril = ci <= ri
    b = jnp.dot(tril.astype(F32), g, precision=lax.Precision.HIGHEST, preferred_element_type=F32)
    bl = jnp.sum(jnp.where(row == GLA_CHUNK - 1, b, 0.0), axis=0, keepdims=True)
    eb, enb, elb, ebl = jnp.exp(b), jnp.exp(-b), jnp.exp(bl - b), jnp.exp(bl)
    q = q_ref[...].astype(F32) * (GLA_DK ** -0.5)
    k = k_ref[...].astype(F32)
    return dict(z=z, live=live, tril=tril, row=row, eb=eb, enb=enb, elb=elb, ebl=ebl,
                qe=q * eb, ke=k * enb, kl=k * elb)


def _gla_in_specs(n_chunks, rev):
    def rb(b, n):
        return b * n_chunks + ((n_chunks - 1 - n) if rev else n)

    return rb, [pl.BlockSpec((GLA_CHUNK, GLA_KW), lambda b, n: (rb(b, n), C_Q // GLA_KW)),
                pl.BlockSpec((GLA_CHUNK, GLA_KW), lambda b, n: (rb(b, n), C_K // GLA_KW)),
                pl.BlockSpec((GLA_CHUNK, GLA_VW), lambda b, n: (rb(b, n), C_V // GLA_VW)),
                pl.BlockSpec((GLA_CHUNK, GLA_VW), lambda b, n: (rb(b, n), C_Z // GLA_VW)),
                pl.BlockSpec((GLA_CHUNK, LANE), lambda b, n: (rb(b, n), C_LR // LANE)),
                pl.BlockSpec((LANE, GLA_KW), lambda b, n: (0, 0)),
                pl.BlockSpec((1, GLA_KW), lambda b, n: (0, 0)),
                pl.BlockSpec((1, GLA_DV), lambda b, n: (0, 0))]


def _gla_fwd(proj, gw_pad, gate_b, gla_norm_g, bsz, lp):
    n_chunks = lp // GLA_CHUNK
    tp = bsz * lp

    def body(q_ref, k_ref, v_ref, z_ref, lr_ref, gw_ref, gb_ref, gn_ref, oraw_ref, ya_ref, sall_ref, st_scr):
        n = pl.program_id(1)

        @pl.when(n == 0)
        def _():
            st_scr[...] = jnp.zeros_like(st_scr)

        c = _gla_gates(q_ref, k_ref, lr_ref, gw_ref, gb_ref, n)
        qe_b, ke_b, kl_b = _bf(c["qe"]), _bf(c["ke"]), _bf(c["kl"])
        gn = gn_ref[...]
        for h in range(GLA_HEADS):
            ks, vs = slice(h * GLA_DK, (h + 1) * GLA_DK), slice(h * GLA_DV, (h + 1) * GLA_DV)
            st = st_scr[h]
            sall_ref[0, 0, h] = st
            v = v_ref[:, vs]
            a = jnp.where(c["tril"], _dot_nt(qe_b[:, ks], ke_b[:, ks]), 0.0)
            o = _dot(_bf(a), v) + _dot_nt(qe_b[:, ks], _bf(st))
            st_scr[h] = st * c["ebl"][:, ks] + _dot_tn(v, kl_b[:, ks])
            oraw_ref[:, vs] = o
            r = lax.rsqrt(jnp.mean(o * o, axis=-1, keepdims=True) + EPS)
            zg = z_ref[:, vs].astype(F32)
            ya_ref[:, vs] = _bf((o * r * gn) * (zg * _sigmoid(zg)))

    rb, in_specs = _gla_in_specs(n_chunks, False)
    return pl.pallas_call(
        body, name="gla_fwd", grid=(bsz, n_chunks), in_specs=in_specs,
        out_specs=[pl.BlockSpec((GLA_CHUNK, GLA_VW), lambda b, n: (rb(b, n), 0)),
                   pl.BlockSpec((GLA_CHUNK, GLA_VW), lambda b, n: (rb(b, n), 0)),
                   pl.BlockSpec((1, 1, GLA_HEADS, GLA_DV, GLA_DK), lambda b, n: (b, n, 0, 0, 0))],
        out_shape=[jax.ShapeDtypeStruct((tp, GLA_VW), F32), jax.ShapeDtypeStruct((tp, GLA_VW), BF16),
                   jax.ShapeDtypeStruct((bsz, n_chunks, GLA_HEADS, GLA_DV, GLA_DK), F32)],
        scratch_shapes=[pltpu.VMEM((GLA_HEADS, GLA_DV, GLA_DK), F32)],
        compiler_params=_cp(("parallel", "arbitrary")),
    )(proj, proj, proj, proj, proj, gw_pad, gate_b, gla_norm_g)


def _gla_bwd(proj, gw_pad, gate_b, gla_norm_g, o_raw, s_all, d_ya, dproj, bsz, lp):
    n_chunks = lp // GLA_CHUNK
    tp = bsz * lp

    def body(q_ref, k_ref, v_ref, z_ref, lr_ref, gw_ref, gb_ref, gn_ref, o_ref, s_ref, dya_ref, _,
             dp_ref, dz_ref, dgn_ref, dst_scr):
        dv_ref, dzg_ref = dp_ref.at[:, C_V:C_V + GLA_VW], dp_ref.at[:, C_Z:C_Z + GLA_VW]
        @pl.when(jnp.logical_and(pl.program_id(0) == 0, pl.program_id(1) == 0))
        def _():
            dgn_ref[...] = jnp.zeros_like(dgn_ref)

        @pl.when(pl.program_id(1) == 0)
        def _():
            dst_scr[...] = jnp.zeros_like(dst_scr)

        n = n_chunks - 1 - pl.program_id(1)
        c = _gla_gates(q_ref, k_ref, lr_ref, gw_ref, gb_ref, n)
        qe_b, ke_b, kl_b = _bf(c["qe"]), _bf(c["ke"]), _bf(c["kl"])
        gn = gn_ref[...]
        dgn = jnp.zeros((1, GLA_DV), F32)
        dqe_h, dke_h, dkl_h, dbl_h = [], [], [], []
        for h in range(GLA_HEADS):
            ks, vs = slice(h * GLA_DK, (h + 1) * GLA_DK), slice(h * GLA_DV, (h + 1) * GLA_DV)
            v = v_ref[:, vs]
            st = s_ref[0, 0, h]
            dst = dst_scr[h]
            o = o_ref[:, vs]
            r = lax.rsqrt(jnp.mean(o * o, axis=-1, keepdims=True) + EPS)
            xh = o * r
            zg = z_ref[:, vs].astype(F32)
            sg = _sigmoid(zg)
            dy = dya_ref[:, vs].astype(F32)
            dzg_ref[:, vs] = _bf(dy * (xh * gn) * (sg * (1.0 + zg * (1.0 - sg))))
            t = dy * (zg * sg)
            dgn += jnp.sum(t * xh, axis=0, keepdims=True)
            dxh = t * gn
            do_b = _bf(r * (dxh - xh * jnp.mean(dxh * xh, axis=-1, keepdims=True)))
            dst_b = _bf(dst)
            a = jnp.where(c["tril"], _dot_nt(qe_b[:, ks], ke_b[:, ks]), 0.0)
            da_b = _bf(jnp.where(c["tril"], _dot_nt(do_b, v), 0.0))
            dqe_h.append(_dot(da_b, ke_b[:, ks]) + _dot(do_b, _bf(st)))
            dke_h.append(_dot_tn(da_b, qe_b[:, ks]))
            dkl = _dot(v, dst_b)
            dkl_h.append(dkl)
            dv_ref[:, vs] = _bf(_dot_tn(_bf(a), do_b) + _dot_nt(kl_b[:, ks], dst_b))
            ddecay = jnp.sum(dst * st, axis=0, keepdims=True)
            dbl_h.append(jnp.sum(dkl * c["kl"][:, ks], axis=0, keepdims=True) + ddecay * c["ebl"][:, ks])
            dst_scr[h] = dst * c["ebl"][:, ks] + _dot_tn(do_b, qe_b[:, ks])
        dgn_ref[...] += dgn
        dqe, dke, dkl = (jnp.concatenate(p, axis=1) for p in (dqe_h, dke_h, dkl_h))
        dbl = jnp.concatenate(dbl_h, axis=1)
        db = dqe * c["qe"] - dke * c["ke"] - dkl * c["kl"] + jnp.where(c["row"] == GLA_CHUNK - 1, dbl, 0.0)
        ri = lax.broadcasted_iota(jnp.int32, (GLA_CHUNK, GLA_CHUNK), 0)
        ci = lax.broadcasted_iota(jnp.int32, (GLA_CHUNK, GLA_CHUNK), 1)
        dg = jnp.dot((ci >= ri).astype(F32), db, precision=lax.Precision.HIGHEST, preferred_element_type=F32)
        dg = jnp.where(c["live"], dg, 0.0)
        dz_ref[...] = dg * (1.0 / GLA_GATE_NORMALIZER) * _sigmoid(-c["z"])
        dp_ref[:, C_Q:C_Q + GLA_KW] = _bf(dqe * c["eb"] * (GLA_DK ** -0.5))
        dp_ref[:, C_K:C_K + GLA_KW] = _bf(dke * c["enb"] + dkl * c["elb"])

    rb, in_specs = _gla_in_specs(n_chunks, True)
    wide = pl.BlockSpec((GLA_CHUNK, GLA_VW), lambda b, n: (rb(b, n), 0))
    group = C_MZ
    return pl.pallas_call(
        body, name="gla_bwd", grid=(bsz, n_chunks),
        in_specs=in_specs + [wide, pl.BlockSpec((1, 1, GLA_HEADS, GLA_DV, GLA_DK),
                                                lambda b, n: (b, n_chunks - 1 - n, 0, 0, 0)), wide,
                             pl.BlockSpec(memory_space=pl.ANY)],
        out_specs=[pl.BlockSpec((GLA_CHUNK, group), lambda b, n: (rb(b, n), 0)),
                   pl.BlockSpec((GLA_CHUNK, GLA_KW), lambda b, n: (rb(b, n), 0)),
                   pl.BlockSpec((1, GLA_DV), lambda b, n: (0, 0))],
        out_shape=[jax.ShapeDtypeStruct((tp, N_EXT), BF16), jax.ShapeDtypeStruct((tp, GLA_KW), F32),
                   jax.ShapeDtypeStruct((1, GLA_DV), F32)],
        input_output_aliases={11: 0},
        scratch_shapes=[pltpu.VMEM((GLA_HEADS, GLA_DV, GLA_DK), F32)],
        compiler_params=_cp(("arbitrary", "arbitrary")),
    )(proj, proj, proj, proj, proj, gw_pad, gate_b, gla_norm_g, o_raw, s_all, d_ya, dproj)


def _gate_bwd(dz, proj, gw_pad_t):
    tp = dz.shape[0]
    tm = _big_tok(tp)

    def body(dz_ref, lr_ref, gwt_ref, dlr_ref, dgw_ref, dgb_ref):
        @pl.when(pl.program_id(0) == 0)
        def _():
            dgw_ref[...] = jnp.zeros_like(dgw_ref)
            dgb_ref[...] = jnp.zeros_like(dgb_ref)

        dz = dz_ref[...]
        dz_b = _bf(dz)
        dlr_ref[...] = _bf(_dot(dz_b, gwt_ref[...]))
        dgw_ref[...] += _dot_tn(lr_ref[...], dz_b)
        dgb_ref[...] += jnp.sum(dz, axis=0, keepdims=True)

    return pl.pallas_call(
        body, name="gate_bwd", grid=(tp // tm,),
        in_specs=[pl.BlockSpec((tm, GLA_KW), lambda i: (i, 0)),
                  pl.BlockSpec((tm, LANE), lambda i: (i, C_LR // LANE)),
                  pl.BlockSpec((GLA_KW, LANE), lambda i: (0, 0))],
        out_specs=[pl.BlockSpec((tm, LANE), lambda i: (i, 0)),
                   pl.BlockSpec((LANE, GLA_KW), lambda i: (0, 0)),
                   pl.BlockSpec((1, GLA_KW), lambda i: (0, 0))],
        out_shape=[jax.ShapeDtypeStruct((tp, LANE), BF16), jax.ShapeDtypeStruct((LANE, GLA_KW), F32),
                   jax.ShapeDtypeStruct((1, GLA_KW), F32)],
        compiler_params=_cp(("arbitrary",)),
    )(dz, proj, gw_pad_t)


def _rms_fwd(x):
    r = lax.rsqrt(jnp.mean(x * x, axis=-1, keepdims=True) + EPS)
    return x * r, r


def _rms_bwd(dy, xh, r, g):
    dxh = dy * g
    dx = r * (dxh - xh * jnp.mean(dxh * xh, axis=-1, keepdims=True))
    return dx, jnp.sum(dy * xh, axis=0, keepdims=True)


def _q_up(proj, q_norm_g, wn, wr, wt, cos_t, sin_t, bsz, lp):
    tp = bsz * lp
    nb = lp // TOK

    def body(cq_ref, g_ref, wn_ref, wr_ref, wt_ref, cos_ref, sin_ref, q_ref):
        xh, _ = _rms_fwd(cq_ref[...].astype(F32))
        cqn = _bf(xh * g_ref[...])
        nope = _dot(cqn, wn_ref[...])
        rope = _dot(cqn, wr_ref[...])
        rot = _dot(cqn, wt_ref[...])
        cos, sin = cos_ref[...], sin_ref[...]
        for h in range(MLA_HEADS):
            sl = slice(h * LANE, (h + 1) * LANE)
            q_ref[:, h * QKW:h * QKW + LANE] = _bf(nope[:, sl])
            q_ref[:, h * QKW + LANE:(h + 1) * QKW] = _bf(rope[:, sl] * cos + rot[:, sl] * sin)

    wspec = pl.BlockSpec((MLA_QR, MLA_HEADS * LANE), lambda b, i: (0, 0))
    tspec = pl.BlockSpec((TOK, LANE), lambda b, i: (i, 0))
    return pl.pallas_call(
        body, name="mla_q_up", grid=(bsz, nb),
        in_specs=[pl.BlockSpec((TOK, MLA_QR), lambda b, i: (b * nb + i, C_CQ // MLA_QR)),
                  pl.BlockSpec((1, MLA_QR), lambda b, i: (0, 0)), wspec, wspec, wspec, tspec, tspec],
        out_specs=pl.BlockSpec((TOK, MLA_HEADS * QKW), lambda b, i: (b * nb + i, 0)),
        out_shape=jax.ShapeDtypeStruct((tp, MLA_HEADS * QKW), BF16),
        compiler_params=_cp(("parallel", "parallel")),
    )(proj, q_norm_g, wn, wr, wt, cos_t, sin_t)


def _kv_up(proj, kv_norm_g, wk, wv, cos_t, sin_t, bsz, lp):
    tp = bsz * lp
    nb = lp // TOK

    def body(ckv_ref, kr_ref, krot_ref, g_ref, wk_ref, wv_ref, cos_ref, sin_ref, k_ref, v_ref):
        xh, _ = _rms_fwd(ckv_ref[...].astype(F32))
        cn = _bf(xh * g_ref[...])
        kn = _dot(cn, wk_ref[...])
        v_ref[...] = _bf(_dot(cn, wv_ref[...]))
        kr = _bf(kr_ref[...].astype(F32) * cos_ref[...] + krot_ref[...].astype(F32) * sin_ref[...])
        for h in range(MLA_HEADS):
            k_ref[:, h * QKW:h * QKW + LANE] = _bf(kn[:, h * LANE:(h + 1) * LANE])
            k_ref[:, h * QKW + LANE:(h + 1) * QKW] = kr

    wspec = pl.BlockSpec((MLA_KVR, MLA_HEADS * LANE), lambda b, i: (0, 0))
    tspec = pl.BlockSpec((TOK, LANE), lambda b, i: (i, 0))
    return pl.pallas_call(
        body, name="mla_kv_up", grid=(bsz, nb),
        in_specs=[pl.BlockSpec((TOK, LANE), lambda b, i: (b * nb + i, C_CKV // LANE)),
                  pl.BlockSpec((TOK, LANE), lambda b, i: (b * nb + i, C_KR // LANE)),
                  pl.BlockSpec((TOK, LANE), lambda b, i: (b * nb + i, C_KROT // LANE)),
                  pl.BlockSpec((1, MLA_KVR), lambda b, i: (0, 0)), wspec, wspec, tspec, tspec],
        out_specs=[pl.BlockSpec((TOK, MLA_HEADS * QKW), lambda b, i: (b * nb + i, 0)),
                   pl.BlockSpec((TOK, MLA_HEADS * LANE), lambda b, i: (b * nb + i, 0))],
        out_shape=[jax.ShapeDtypeStruct((tp, MLA_HEADS * QKW), BF16),
                   jax.ShapeDtypeStruct((tp, MLA_HEADS * LANE), BF16)],
        compiler_params=_cp(("parallel", "parallel")),
    )(proj, proj, proj, kv_norm_g, wk, wv, cos_t, sin_t)


ATT_SCALE = MLA_QK ** -0.5


def _attn_mask(r0, tq, kmax):
    qpos = r0 + lax.broadcasted_iota(jnp.int32, (tq, kmax), 0)
    kpos = lax.broadcasted_iota(jnp.int32, (tq, kmax), 1)
    real = kpos >= FRONT
    if r0 < FRONT:
        real = jnp.logical_or(real, qpos < FRONT)
    return jnp.logical_and(kpos <= qpos, real)


def _attn_fwd(qf, kf, vf, proj, bsz, lp):
    tp = bsz * lp
    tq = _attn_block(lp)

    def body(q_ref, k_ref, v_ref, mz_ref, ob_ref, yb_ref, lse_ref):
        for r0 in range(0, lp, tq):
            rows, kmax = slice(r0, r0 + tq), r0 + tq
            s = _dot_nt(q_ref[rows, :], k_ref[0:kmax, :]) * ATT_SCALE
            s = jnp.where(_attn_mask(r0, tq, kmax), s, NEG)
            m = jnp.max(s, axis=-1, keepdims=True)
            p = jnp.exp(s - m)
            l = jnp.sum(p, axis=-1, keepdims=True)
            o = _dot(_bf(p), v_ref[0:kmax, :]) / l
            ob_ref[rows, :] = _bf(o)
            mz = mz_ref[rows, :].astype(F32)
            yb_ref[rows, :] = _bf(o * (mz * _sigmoid(mz)))
            lse_ref[0, 0, rows, :] = jnp.broadcast_to(m + jnp.log(l), (tq, LANE))

    head = lambda off: pl.BlockSpec((lp, MLA_DV), lambda b, h: (b, off + h))
    return pl.pallas_call(
        body, name="mla_attn_fwd", grid=(bsz, MLA_HEADS),
        in_specs=[pl.BlockSpec((lp, QKW), lambda b, h: (b, h)), pl.BlockSpec((lp, QKW), lambda b, h: (b, h)),
                  head(0), head(C_MZ // MLA_DV)],
        out_specs=[head(0), head(0), pl.BlockSpec((1, 1, lp, LANE), lambda b, h: (b, h, 0, 0))],
        out_shape=[jax.ShapeDtypeStruct((tp, MLA_HEADS * MLA_DV), BF16),
                   jax.ShapeDtypeStruct((tp, MLA_HEADS * MLA_DV), BF16),
                   jax.ShapeDtypeStruct((bsz, MLA_HEADS, lp, LANE), F32)],
        compiler_params=_cp(("parallel", "parallel"), 56),
    )(qf, kf, vf, proj)


def _attn_bwd_pre(d_yb, proj, o_b, dproj, bsz, lp):
    tp = bsz * lp
    nb = lp // TOK
    w = MLA_HEADS * MLA_DV

    def body(dy_ref, mz_ref, o_ref, _, do_ref, dmz_ref, dl_ref):
        dy = dy_ref[...].astype(F32)
        mz = mz_ref[...].astype(F32)
        o = o_ref[...].astype(F32)
        s = _sigmoid(mz)
        do = _bf(dy * (mz * s))
        do_ref[...] = do
        dmz_ref[...] = _bf(dy * o * (s * (1.0 + mz * (1.0 - s))))
        prod = do.astype(F32) * o
        for h in range(MLA_HEADS):
            dl = jnp.sum(prod[:, h * MLA_DV:(h + 1) * MLA_DV], axis=-1, keepdims=True)
            dl_ref[0, h] = jnp.broadcast_to(dl, (TOK, LANE))

    return pl.pallas_call(
        body, name="mla_attn_bwd_pre", grid=(bsz, nb),
        in_specs=[pl.BlockSpec((TOK, w), lambda b, i: (b * nb + i, 0)),
                  pl.BlockSpec((TOK, w), lambda b, i: (b * nb + i, C_MZ // w)),
                  pl.BlockSpec((TOK, w), lambda b, i: (b * nb + i, 0)), pl.BlockSpec(memory_space=pl.ANY)],
        out_specs=[pl.BlockSpec((TOK, w), lambda b, i: (b * nb + i, 0)),
                   pl.BlockSpec((TOK, w), lambda b, i: (b * nb + i, C_MZ // w)),
                   pl.BlockSpec((1, MLA_HEADS, TOK, LANE), lambda b, i: (b, 0, i, 0))],
        out_shape=[jax.ShapeDtypeStruct((tp, w), BF16), jax.ShapeDtypeStruct((tp, N_EXT), BF16),
                   jax.ShapeDtypeStruct((bsz, MLA_HEADS, lp, LANE), F32)],
        input_output_aliases={3: 1},
        compiler_params=_cp(("parallel", "parallel")),
    )(d_yb, proj, o_b, dproj)


def _attn_bwd(qf, kf, vf, d_o, lse, delta, bsz, lp):
    tp = bsz * lp
    tq = _attn_block(lp)

    def body(q_ref, k_ref, v_ref, do_ref, lse_ref, dl_ref, dq_ref, dk_ref, dv_ref, dk_acc, dv_acc):
        dk_acc[...] = jnp.zeros_like(dk_acc)
        dv_acc[...] = jnp.zeros_like(dv_acc)
        for r0 in range(0, lp, tq):
            rows, kmax = slice(r0, r0 + tq), r0 + tq
            q, do = q_ref[rows, :], do_ref[rows, :]
            k, v = k_ref[0:kmax, :], v_ref[0:kmax, :]
            s = _dot_nt(q, k) * ATT_SCALE
            p = jnp.where(_attn_mask(r0, tq, kmax), jnp.exp(s - lse_ref[0, 0, rows, :][:, :1]), 0.0)
            ds = _bf(p * (_dot_nt(do, v) - dl_ref[0, 0, rows, :][:, :1]) * ATT_SCALE)
            dq_ref[rows, :] = _bf(_dot(ds, k))
            dk_acc[0:kmax, :] += _dot_tn(ds, q)
            dv_acc[0:kmax, :] += _dot_tn(_bf(p), do)
        dk_ref[...] = _bf(dk_acc[...])
        dv_ref[...] = _bf(dv_acc[...])

    wide = pl.BlockSpec((lp, QKW), lambda b, h: (b, h))
    narrow = pl.BlockSpec((lp, MLA_DV), lambda b, h: (b, h))
    stat = pl.BlockSpec((1, 1, lp, LANE), lambda b, h: (b, h, 0, 0))
    return pl.pallas_call(
        body, name="mla_attn_bwd", grid=(bsz, MLA_HEADS),
        in_specs=[wide, wide, narrow, narrow, stat, stat], out_specs=[wide, wide, narrow],
        out_shape=[jax.ShapeDtypeStruct((tp, MLA_HEADS * QKW), BF16), jax.ShapeDtypeStruct((tp, MLA_HEADS * QKW), BF16),
                   jax.ShapeDtypeStruct((tp, MLA_HEADS * MLA_DV), BF16)],
        scratch_shapes=[pltpu.VMEM((lp, QKW), F32), pltpu.VMEM((lp, MLA_DV), F32)],
        compiler_params=_cp(("parallel", "parallel"), 56),
    )(qf, kf, vf, d_o, lse, delta)


def _q_up_bwd(dqf, proj, q_norm_g, wn_t, wr_t, wt_t, cos_t, sin_t, dproj, bsz, lp):
    tp = bsz * lp
    nb = lp // TOK
    hw = MLA_HEADS * LANE

    def body(dq_ref, cq_ref, g_ref, wn_ref, wr_ref, wt_ref, cos_ref, sin_ref, _,
             dcq_ref, dwn_ref, dwr_ref, dwt_ref, dg_ref):
        @pl.when(jnp.logical_and(pl.program_id(0) == 0, pl.program_id(1) == 0))
        def _():
            for r in (dwn_ref, dwr_ref, dwt_ref, dg_ref):
                r[...] = jnp.zeros_like(r)

        g = g_ref[...]
        xh, r = _rms_fwd(cq_ref[...].astype(F32))
        cqn = _bf(xh * g)
        cos, sin = cos_ref[...], sin_ref[...]
        dcqn = jnp.zeros((TOK, MLA_QR), F32)
        for h in range(MLA_HEADS):
            sl = slice(h * LANE, (h + 1) * LANE)
            dn = dq_ref[:, h * QKW:h * QKW + LANE]
            dr = dq_ref[:, h * QKW + LANE:(h + 1) * QKW].astype(F32)
            dr_c, dr_s = _bf(dr * cos), _bf(dr * sin)
            dcqn += _dot(dn, wn_ref[sl, :]) + _dot(dr_c, wr_ref[sl, :]) + _dot(dr_s, wt_ref[sl, :])
            dwn_ref[:, sl] += _dot_tn(cqn, dn)
            dwr_ref[:, sl] += _dot_tn(cqn, dr_c)
            dwt_ref[:, sl] += _dot_tn(cqn, dr_s)
        dx, dg = _rms_bwd(dcqn, xh, r, g)
        dcq_ref[...] = _bf(dx)
        dg_ref[...] += dg

    wspec = pl.BlockSpec((hw, MLA_QR), lambda b, i: (0, 0))
    aspec = pl.BlockSpec((MLA_QR, hw), lambda b, i: (0, 0))
    tspec = pl.BlockSpec((TOK, LANE), lambda b, i: (i, 0))
    return pl.pallas_call(
        body, name="mla_q_up_bwd", grid=(bsz, nb),
        in_specs=[pl.BlockSpec((TOK, MLA_HEADS * QKW), lambda b, i: (b * nb + i, 0)),
                  pl.BlockSpec((TOK, MLA_QR), lambda b, i: (b * nb + i, C_CQ // MLA_QR)),
                  pl.BlockSpec((1, MLA_QR), lambda b, i: (0, 0)), wspec, wspec, wspec, tspec, tspec,
                  pl.BlockSpec(memory_space=pl.ANY)],
        out_specs=[pl.BlockSpec((TOK, MLA_QR), lambda b, i: (b * nb + i, C_CQ // MLA_QR)), aspec, aspec, aspec,
                   pl.BlockSpec((1, MLA_QR), lambda b, i: (0, 0))],
        out_shape=[jax.ShapeDtypeStruct((tp, N_EXT), BF16)] + [jax.ShapeDtypeStruct((MLA_QR, hw), F32)] * 3
        + [jax.ShapeDtypeStruct((1, MLA_QR), F32)],
        input_output_aliases={8: 0},
        compiler_params=_cp(("arbitrary", "arbitrary")),
    )(dqf, proj, q_norm_g, wn_t, wr_t, wt_t, cos_t, sin_t, dproj)


def _kv_up_bwd(dkf, dvf, proj, kv_norm_g, wk_t, wv_t, cos_t, sin_t, d_lr, dproj, bsz, lp):
    tp = bsz * lp
    nb = lp // TOK
    hw = MLA_HEADS * LANE

    def body(dk_ref, dv_ref, ckv_ref, g_ref, wk_ref, wv_ref, cos_ref, sin_ref, dlr_ref, _,
             dp_ref, dwk_ref, dwv_ref, dg_ref):
        dckv_ref, dkr_ref, dkrot_ref = (dp_ref.at[:, j * LANE:(j + 1) * LANE] for j in range(3))
        dp_ref[:, 3 * LANE:] = dlr_ref[...]
        @pl.when(jnp.logical_and(pl.program_id(0) == 0, pl.program_id(1) == 0))
        def _():
            for r in (dwk_ref, dwv_ref, dg_ref):
                r[...] = jnp.zeros_like(r)

        g = g_ref[...]
        xh, r = _rms_fwd(ckv_ref[...].astype(F32))
        cn = _bf(xh * g)
        dv = dv_ref[...]
        dcn = _dot(dv, wv_ref[...])
        dwv_ref[...] += _dot_tn(cn, dv)
        drope = jnp.zeros((TOK, LANE), F32)
        for h in range(MLA_HEADS):
            sl = slice(h * LANE, (h + 1) * LANE)
            dn = dk_ref[:, h * QKW:h * QKW + LANE]
            drope += dk_ref[:, h * QKW + LANE:(h + 1) * QKW].astype(F32)
            dcn += _dot(dn, wk_ref[sl, :])
            dwk_ref[:, sl] += _dot_tn(cn, dn)
        dkr_ref[...] = _bf(drope * cos_ref[...])
        dkrot_ref[...] = _bf(drope * sin_ref[...])
        dx, dg = _rms_bwd(dcn, xh, r, g)
        dckv_ref[...] = _bf(dx)
        dg_ref[...] += dg

    wspec = pl.BlockSpec((hw, MLA_KVR), lambda b, i: (0, 0))
    aspec = pl.BlockSpec((MLA_KVR, hw), lambda b, i: (0, 0))
    tspec = pl.BlockSpec((TOK, LANE), lambda b, i: (i, 0))
    ospec = pl.BlockSpec((TOK, LANE), lambda b, i: (b * nb + i, 0))
    return pl.pallas_call(
        body, name="mla_kv_up_bwd", grid=(bsz, nb),
        in_specs=[pl.BlockSpec((TOK, MLA_HEADS * QKW), lambda b, i: (b * nb + i, 0)),
                  pl.BlockSpec((TOK, hw), lambda b, i: (b * nb + i, 0)),
                  pl.BlockSpec((TOK, LANE), lambda b, i: (b * nb + i, C_CKV // LANE)),
                  pl.BlockSpec((1, MLA_KVR), lambda b, i: (0, 0)), wspec, wspec, tspec, tspec, ospec,
                  pl.BlockSpec(memory_space=pl.ANY)],
        out_specs=[pl.BlockSpec((TOK, 4 * LANE), lambda b, i: (b * nb + i, C_CKV // (4 * LANE))), aspec, aspec,
                   pl.BlockSpec((1, MLA_KVR), lambda b, i: (0, 0))],
        out_shape=[jax.ShapeDtypeStruct((tp, N_EXT), BF16)] + [jax.ShapeDtypeStruct((MLA_KVR, hw), F32)] * 2
        + [jax.ShapeDtypeStruct((1, MLA_KVR), F32)],
        input_output_aliases={9: 0},
        compiler_params=_cp(("arbitrary", "arbitrary")),
    )(dkf, dvf, proj, kv_norm_g, wk_t, wv_t, cos_t, sin_t, d_lr, dproj)


def _mid_fwd(ya_in, yb_in, proj, hp, target, w_gp, w_mp, w_o, final_g, bsz, lp):
    tp = bsz * lp
    nb = lp // TOK

    def body(ya_ref, yb_ref, gg_ref, gm_ref, h_ref, t_ref, wgp_ref, wmp_ref, wo_ref, fg_ref,
             ya_out, yb_out, dh_ref, loss_ref, dfg_ref):
        @pl.when(jnp.logical_and(pl.program_id(0) == 0, pl.program_id(1) == 0))
        def _():
            loss_ref[...] = jnp.zeros_like(loss_ref)
            dfg_ref[...] = jnp.zeros_like(dfg_ref)

        y_a = _dot(ya_ref[...], wgp_ref[...])
        y_b = _dot(yb_ref[...], wmp_ref[...])
        ya_out[...] = _bf(y_a)
        yb_out[...] = _bf(y_b)
        merged = _sigmoid(gg_ref[...].astype(F32)) * y_a + _sigmoid(gm_ref[...].astype(F32)) * y_b
        h2 = h_ref[...] + _dot(_bf(merged), wo_ref[...])
        fg = fg_ref[...]
        xh, r = _rms_fwd(h2)
        pos = pl.program_id(1) * TOK + lax.broadcasted_iota(jnp.int32, (TOK, 1), 0)
        err = jnp.where(pos >= X0, xh * fg - t_ref[...], 0.0)
        loss_ref[...] += 0.5 * jnp.sum(jnp.mean(err * err, axis=-1, keepdims=True), axis=0, keepdims=True)
        dy = err * (1.0 / D_MODEL)
        dx, dfg = _rms_bwd(dy, xh, r, fg)
        dh_ref[...] = dx
        dfg_ref[...] += dfg

    tok = lambda c: pl.BlockSpec((TOK, D_MODEL), lambda b, i: (b * nb + i, c))
    wspec = pl.BlockSpec((D_MODEL, D_MODEL), lambda b, i: (0, 0))
    return pl.pallas_call(
        body, name="mid_fwd", grid=(bsz, nb),
        in_specs=[tok(0), tok(0), tok(C_GG // D_MODEL), tok(C_GM // D_MODEL), tok(0), tok(0),
                  wspec, wspec, wspec, pl.BlockSpec((1, D_MODEL), lambda b, i: (0, 0))],
        out_specs=[tok(0), tok(0), tok(0), pl.BlockSpec((1, LANE), lambda b, i: (0, 0)),
                   pl.BlockSpec((1, D_MODEL), lambda b, i: (0, 0))],
        out_shape=[jax.ShapeDtypeStruct((tp, D_MODEL), BF16), jax.ShapeDtypeStruct((tp, D_MODEL), BF16),
                   jax.ShapeDtypeStruct((tp, D_MODEL), F32), jax.ShapeDtypeStruct((1, LANE), F32),
                   jax.ShapeDtypeStruct((1, D_MODEL), F32)],
        compiler_params=_cp(("arbitrary", "arbitrary"), 48),
    )(ya_in, yb_in, proj, proj, hp, target, w_gp, w_mp, w_o, final_g)


def _mid_bwd(dh2, y_a, y_b, proj, ya_in, yb_in, w_o_t, w_gp_t, w_mp_t):
    tp = dh2.shape[0]
    tm = TOK
    nsteps = tp // tm

    def body(dh_ref, ya_ref, yb_ref, gg_ref, gm_ref, yai_ref, ybi_ref, wo_ref, wgp_ref, wmp_ref,
             dyai_ref, dybi_ref, dgate_ref, dwo_ref, dwgp_ref, dwmp_ref, a_o, a_gp, a_mp):
        @pl.when(pl.program_id(0) == 0)
        def _():
            for r in (a_o, a_gp, a_mp):
                r[...] = jnp.zeros_like(r)

        dh = _bf(dh_ref[...])
        dm = _dot(dh, wo_ref[...])
        y_a, y_b = ya_ref[...].astype(F32), yb_ref[...].astype(F32)
        sg, sm = _sigmoid(gg_ref[...].astype(F32)), _sigmoid(gm_ref[...].astype(F32))
        d_ya, d_yb = _bf(sg * dm), _bf(sm * dm)
        dgate_ref[:, :D_MODEL] = _bf(dm * y_a * sg * (1.0 - sg))
        dgate_ref[:, D_MODEL:] = _bf(dm * y_b * sm * (1.0 - sm))
        a_o[...] += _dot_tn(_bf(sg * y_a + sm * y_b), dh)
        a_gp[...] += _dot_tn(yai_ref[...], d_ya)
        a_mp[...] += _dot_tn(ybi_ref[...], d_yb)
        dyai_ref[...] = _bf(_dot(d_ya, wgp_ref[...]))
        dybi_ref[...] = _bf(_dot(d_yb, wmp_ref[...]))

        @pl.when(pl.program_id(0) == nsteps - 1)
        def _():
            pltpu.sync_copy(a_o, dwo_ref)
            pltpu.sync_copy(a_gp, dwgp_ref)
            pltpu.sync_copy(a_mp, dwmp_ref)

    tok = lambda c: pl.BlockSpec((tm, D_MODEL), lambda i: (i, c))
    wspec = pl.BlockSpec((D_MODEL, D_MODEL), lambda i: (0, 0))
    anyspec = pl.BlockSpec(memory_space=pl.ANY)
    wshape = jax.ShapeDtypeStruct((D_MODEL, D_MODEL), F32)
    return pl.pallas_call(
        body, name="mid_bwd", grid=(nsteps,),
        in_specs=[tok(0), tok(0), tok(0), tok(C_GG // D_MODEL), tok(C_GM // D_MODEL), tok(0), tok(0),
                  wspec, wspec, wspec],
        out_specs=[tok(0), tok(0), pl.BlockSpec((tm, 2 * D_MODEL), lambda i: (i, C_GG // (2 * D_MODEL))),
                   anyspec, anyspec, anyspec],
        out_shape=[jax.ShapeDtypeStruct((tp, D_MODEL), BF16)] * 2 + [jax.ShapeDtypeStruct((tp, N_EXT), BF16)]
        + [wshape] * 3,
        scratch_shapes=[pltpu.VMEM((D_MODEL, D_MODEL), F32)] * 3,
        compiler_params=_cp(("arbitrary",), 56),
    )(dh2, y_a, y_b, proj, proj, ya_in, yb_in, w_o_t, w_gp_t, w_mp_t)


MESH_ID = pl.DeviceIdType.MESH
EXCHANGE_SEMS = [pltpu.SemaphoreType.DMA((N_DEV - 1,)), pltpu.SemaphoreType.DMA((N_DEV - 1,)), pltpu.SemaphoreType.DMA]


def _my_place():
    return lax.axis_index("x"), lax.axis_index("y"), lax.axis_index("c")


def _exchange(g_ref, recv_ref, send_sems, recv_sems, local_sem, start):
    x, y, c = _my_place()
    me = 4 * x + 2 * y + c
    own = pltpu.make_async_copy(g_ref.at[me], recv_ref.at[me], local_sem)
    sends, lands = [], []
    for d in range(1, N_DEV):
        px = 1 - x if d & 4 else x
        py = 1 - y if d & 2 else y
        pc = 1 - c if d & 1 else c
        peer = 4 * px + 2 * py + pc
        for slot, group in ((me, sends),) if start else ((me, sends), (peer, lands)):
            group.append(pltpu.make_async_remote_copy(
                src_ref=g_ref.at[peer], dst_ref=recv_ref.at[slot], send_sem=send_sems.at[d - 1],
                recv_sem=recv_sems.at[d - 1], device_id=(px, py, pc), device_id_type=MESH_ID))
    if start:
        own.start()
        for cp in sends:
            cp.start()
    else:
        for cp in lands:
            cp.wait_recv()
        for cp in sends:
            cp.wait_send()
        own.wait()


def _dw_in(u, dproj, slabs):
    tp = u.shape[0]
    tm, tn = _big_tok(tp), 768
    nj, ni = N_EXT // tn, tp // tm

    def body(u_ref, d_ref, g_ref, o_ref, recv_ref, send_sems, recv_sems, local_sem):
        j, i = pl.program_id(0), pl.program_id(1)

        @pl.when(jnp.logical_and(j == 0, i == 0))
        def _():
            _exchange(g_ref, recv_ref, send_sems, recv_sems, local_sem, True)

        @pl.when(i == 0)
        def _():
            o_ref[...] = jnp.zeros_like(o_ref)

        o_ref[...] += _dot_tn(u_ref[...], d_ref[...])

        @pl.when(jnp.logical_and(j == nj - 1, i == ni - 1))
        def _():
            _exchange(g_ref, recv_ref, send_sems, recv_sems, local_sem, False)

    anyspec = pl.BlockSpec(memory_space=pl.ANY)
    return pl.pallas_call(
        body, name="dw_in", grid=(nj, ni),
        in_specs=[pl.BlockSpec((tm, D_MODEL), lambda j, i: (i, 0)), pl.BlockSpec((tm, tn), lambda j, i: (i, j)), anyspec],
        out_specs=[pl.BlockSpec((D_MODEL, tn), lambda j, i: (0, j)), anyspec],
        out_shape=[jax.ShapeDtypeStruct((D_MODEL, N_EXT), F32), jax.ShapeDtypeStruct(slabs.shape, slabs.dtype)],
        scratch_shapes=EXCHANGE_SEMS,
        compiler_params=_cp(("arbitrary", "arbitrary"), 48),
    )(u, dproj, slabs)


def _dx_in(dproj, w_ext_t, hp, dh2, norm_g, slabs):
    tp = hp.shape[0]
    tm, tk = _big_tok(tp), 768
    nk = N_EXT // tk
    ni = tp // tm

    def body(d_ref, w_ref, h_ref, dh_ref, g_ref, s_ref, o_ref, dg_ref, recv_ref, acc, send_sems, recv_sems, local_sem):
        k = pl.program_id(1)

        @pl.when(jnp.logical_and(pl.program_id(0) == 0, k == 0))
        def _():
            _exchange(s_ref, recv_ref, send_sems, recv_sems, local_sem, True)

        @pl.when(jnp.logical_and(pl.program_id(0) == 0, k == 0))
        def _():
            dg_ref[...] = jnp.zeros_like(dg_ref)

        @pl.when(k == 0)
        def _():
            acc[...] = jnp.zeros_like(acc)

        acc[...] += _dot(d_ref[...], w_ref[...])

        @pl.when(k == nk - 1)
        def _():
            g = g_ref[...]
            xh, r = _rms_fwd(h_ref[...])
            dx, dg = _rms_bwd(acc[...], xh, r, g)
            o_ref[...] = dh_ref[...] + dx
            dg_ref[...] += dg

        @pl.when(jnp.logical_and(pl.program_id(0) == ni - 1, k == nk - 1))
        def _():
            _exchange(s_ref, recv_ref, send_sems, recv_sems, local_sem, False)

    tok = pl.BlockSpec((tm, D_MODEL), lambda i, k: (i, 0))
    anyspec = pl.BlockSpec(memory_space=pl.ANY)
    return pl.pallas_call(
        body, name="dx_in", grid=(ni, nk),
        in_specs=[pl.BlockSpec((tm, tk), lambda i, k: (i, k)), pl.BlockSpec((tk, D_MODEL), lambda i, k: (k, 0)),
                  tok, tok, pl.BlockSpec((1, D_MODEL), lambda i, k: (0, 0)), anyspec],
        out_specs=[tok, pl.BlockSpec((1, D_MODEL), lambda i, k: (0, 0)), anyspec],
        out_shape=[jax.ShapeDtypeStruct((tp, D_MODEL), F32), jax.ShapeDtypeStruct((1, D_MODEL), F32),
                   jax.ShapeDtypeStruct(slabs.shape, slabs.dtype)],
        scratch_shapes=[pltpu.VMEM((tm, D_MODEL), F32)] + EXCHANGE_SEMS,
        compiler_params=_cp(("arbitrary", "arbitrary"), 56),
    )(dproj, w_ext_t, hp, dh2, norm_g, slabs)


def _meta_grad(dhp3):
    bsz = dhp3.shape[0]

    def body(d_ref, o_ref):
        @pl.when(pl.program_id(0) == 0)
        def _():
            o_ref[...] = jnp.zeros_like(o_ref)

        o_ref[...] += d_ref[0]

    return pl.pallas_call(
        body, name="meta_grad", grid=(bsz,),
        in_specs=[pl.BlockSpec((1, N_META, D_MODEL), lambda b: (b, FRONT // N_META, 0))],
        out_specs=pl.BlockSpec((N_META, D_MODEL), lambda b: (0, 0)),
        out_shape=jax.ShapeDtypeStruct((N_META, D_MODEL), F32),
        compiler_params=_cp(("arbitrary",)),
    )(dhp3)


W_IN_SHARD = N_IN // N_DEV


def _pad_lanes(a, width=LANE):
    return jnp.pad(a, [(0, 0)] * (a.ndim - 1) + [(0, width - a.shape[-1])])


def _rot_cols(w):
    half = w.shape[-1] // 2
    return jnp.concatenate([-w[..., half:], w[..., :half]], axis=-1)


def _unrot_cols(dw):
    half = dw.shape[-1] // 2
    return jnp.concatenate([dw[..., half:], -dw[..., :half]], axis=-1)


def _w_in_cols(shards, lo, hi):
    parts = []
    for k in range(lo // W_IN_SHARD, (hi - 1) // W_IN_SHARD + 1):
        a, b = max(lo, k * W_IN_SHARD), min(hi, (k + 1) * W_IN_SHARD)
        parts.append(shards[k][:, a - k * W_IN_SHARD:b - k * W_IN_SHARD])
    return parts[0] if len(parts) == 1 else jnp.concatenate(parts, axis=1)


def _w_in_ext(shards):
    c = lambda lo, hi: _w_in_cols(shards, lo, hi)
    kr = c(O_KR, O_MZ)
    return jnp.concatenate([
        c(O_V, O_LR), c(O_Z, O_CQ), c(O_Q, O_K), c(O_K, O_V), c(O_MZ, O_GG), c(O_GG, O_GM), c(O_GM, N_IN),
        c(O_CKV, O_KR), _pad_lanes(kr), _pad_lanes(_rot_cols(kr)), _pad_lanes(c(O_LR, O_Z)), c(O_CQ, O_CKV)], axis=1)


def _w_in_grad(dw):
    g = lambda start, width: dw[:, start:start + width]
    kr = g(C_KR, MLA_ROPE) + _unrot_cols(g(C_KROT, MLA_ROPE))
    return jnp.concatenate([
        g(C_Q, GLA_KW), g(C_K, GLA_KW), g(C_V, GLA_VW), g(C_LR, GLA_RANK), g(C_Z, GLA_VW), g(C_CQ, MLA_QR),
        g(C_CKV, MLA_KVR), kr, g(C_MZ, D_MODEL), g(C_GG, D_MODEL), g(C_GM, D_MODEL)], axis=1)


def _rope_tables(lp):
    inv = 1.0 / (ROPE_BASE ** (jnp.arange(0, MLA_ROPE, 2, dtype=F32) / MLA_ROPE))
    ang = (jnp.arange(lp, dtype=F32) - FRONT)[:, None] * inv[None, :]
    cos, sin = jnp.cos(ang), jnp.sin(ang)
    return _pad_lanes(jnp.concatenate([cos, cos], axis=1)), _pad_lanes(jnp.concatenate([sin, sin], axis=1))


def _local_step(x, loss_target, w):
    bsz, seq, _ = x.shape
    lp = X0 + seq
    tp = bsz * lp
    assert lp % TOK == 0 and lp % GLA_CHUNK == 0
    meta = jnp.broadcast_to(w["meta_tokens"][None], (bsz, N_META, D_MODEL))
    hp = jnp.concatenate([jnp.zeros((bsz, FRONT, D_MODEL), F32), meta, x], axis=1).reshape(tp, D_MODEL)
    target = jnp.pad(loss_target, ((0, 0), (X0, 0), (0, 0))).reshape(tp, D_MODEL)
    cos_t, sin_t = _rope_tables(lp)

    w_ext = _w_in_ext(w["w_in"])
    gw_pad = jnp.pad(w["gla_gate_w"], ((0, LANE - GLA_RANK), (0, 0)))
    uq = w["mla_w_uq"].reshape(MLA_QR, MLA_HEADS, MLA_QK)
    rope_w = uq[:, :, MLA_NOPE:]
    hw = MLA_HEADS * LANE
    wn = uq[:, :, :MLA_NOPE].reshape(MLA_QR, hw)
    wr = _pad_lanes(rope_w).reshape(MLA_QR, hw)
    wt = _pad_lanes(_rot_cols(rope_w)).reshape(MLA_QR, hw)
    ukv = w["mla_w_ukv"].reshape(MLA_KVR, MLA_HEADS, MLA_NOPE + MLA_DV)
    wk = ukv[:, :, :MLA_NOPE].reshape(MLA_KVR, hw)
    wv = ukv[:, :, MLA_NOPE:].reshape(MLA_KVR, hw)

    u, proj = _proj_in(hp, w["norm_g"], w_ext)
    o_raw, ya_in, s_all = _gla_fwd(proj, gw_pad, w["gla_gate_b"], w["gla_norm_g"], bsz, lp)
    qf = _q_up(proj, w["mla_q_norm_g"], wn, wr, wt, cos_t, sin_t, bsz, lp)
    kf, vf = _kv_up(proj, w["mla_kv_norm_g"], wk, wv, cos_t, sin_t, bsz, lp)
    o_b, yb_in, lse = _attn_fwd(qf, kf, vf, proj, bsz, lp)
    y_a, y_b, dh2, loss, d_final_g = _mid_fwd(ya_in, yb_in, proj, hp, target, w["gla_proj"], w["mla_proj"],
                                              w["w_out"], w["final_norm_g"], bsz, lp)
    d_ya, d_yb, dproj, d_w_out, d_gla_proj, d_mla_proj = _mid_bwd(
        dh2, y_a, y_b, proj, ya_in, yb_in, w["w_out"].T, w["gla_proj"].T, w["mla_proj"].T)
    dproj, d_gate, d_gla_norm = _gla_bwd(proj, gw_pad, w["gla_gate_b"], w["gla_norm_g"], o_raw, s_all, d_ya, dproj,
                                         bsz, lp)
    d_lr, d_gw_pad, d_gate_b = _gate_bwd(d_gate, proj, gw_pad.T)
    d_o, dproj, delta = _attn_bwd_pre(d_yb, proj, o_b, dproj, bsz, lp)
    dqf, dkf, dvf = _attn_bwd(qf, kf, vf, d_o, lse, delta, bsz, lp)
    dproj, d_wn, d_wr, d_wt, d_qn = _q_up_bwd(dqf, proj, w["mla_q_norm_g"], wn.T, wr.T, wt.T, cos_t, sin_t, dproj,
                                              bsz, lp)
    dproj, d_wk, d_wv, d_kvn = _kv_up_bwd(dkf, dvf, proj, w["mla_kv_norm_g"], wk.T, wv.T, cos_t, sin_t, d_lr, dproj,
                                          bsz, lp)

    d_rope = (d_wr.reshape(MLA_QR, MLA_HEADS, LANE)[:, :, :MLA_ROPE]
              + _unrot_cols(d_wt.reshape(MLA_QR, MLA_HEADS, LANE)[:, :, :MLA_ROPE]))
    d_uq = jnp.concatenate([d_wn.reshape(MLA_QR, MLA_HEADS, LANE), d_rope], axis=-1).reshape(MLA_QR, MLA_HEADS * MLA_QK)
    d_ukv = jnp.concatenate([d_wk.reshape(MLA_KVR, MLA_HEADS, LANE), d_wv.reshape(MLA_KVR, MLA_HEADS, LANE)],
                            axis=-1).reshape(MLA_KVR, MLA_HEADS * (MLA_NOPE + MLA_DV))
    mats = dict(gla_gate_w=d_gw_pad[:GLA_RANK], gla_proj=d_gla_proj, mla_w_uq=d_uq, mla_w_ukv=d_ukv,
                mla_proj=d_mla_proj, w_out=d_w_out)
    packed = _pad_rows(jnp.concatenate([_split8(mats[n], axis).reshape(N_DEV, -1) for n, _, axis in PACKED], axis=1),
                       PACK_ROWS)
    d_w_ext, packed_parts = _dw_in(u, dproj, _bf(packed))
    d_hp, d_norm_g, w_in_parts = _dx_in(dproj, w_ext.T, hp, dh2, w["norm_g"], _bf(_split8(_w_in_grad(d_w_ext), 1)))
    d_hp3 = d_hp.reshape(bsz, lp, D_MODEL)
    small = dict(meta_tokens=_meta_grad(d_hp3), norm_g=d_norm_g, gla_gate_b=d_gate_b, gla_norm_g=d_gla_norm,
                 mla_q_norm_g=d_qn, mla_kv_norm_g=d_kvn, final_norm_g=d_final_g)
    return loss, d_hp3[:, X0:, :], w_in_parts, packed_parts, small


PACKED = (("gla_gate_w", (GLA_RANK, GLA_KW // N_DEV), 1),
          ("gla_proj", (D_MODEL // N_DEV, D_MODEL), 0), ("mla_w_uq", (MLA_QR, MLA_HEADS * MLA_QK // N_DEV), 1),
          ("mla_w_ukv", (MLA_KVR, MLA_HEADS * (MLA_NOPE + MLA_DV) // N_DEV), 1),
          ("mla_proj", (D_MODEL // N_DEV, D_MODEL), 0), ("w_out", (D_MODEL // N_DEV, D_MODEL), 0))
REPLICATED = (("norm_g", D_MODEL), ("gla_gate_b", GLA_KW), ("gla_norm_g", GLA_DV), ("mla_q_norm_g", MLA_QR),
              ("mla_kv_norm_g", MLA_KVR), ("final_norm_g", D_MODEL))
PACK_ROWS = 3744
PACK_BLOCK = 1248
GATHER_ROWS = 3760
SMALL_ROWS = 48
LOSS_ROW = N_META + 25
W_IN_BLOCK = 128


def _all_gather(shards):
    n_arr = len(shards)

    def body(*refs):
        x_refs, out_refs = refs[:n_arr], refs[n_arr:2 * n_arr]
        send_sems, recv_sems, local_sems = refs[2 * n_arr:]
        x, y, c = _my_place()
        me, sibling = (x, y, c), (x, y, 1 - c)
        chips = [(1 - x, y), (x, 1 - y), (1 - x, 1 - y)]

        def copy(a, k, block, to, from_input=False):
            slab = out_refs[a].at[4 * block[0] + 2 * block[1] + block[2]]
            return pltpu.make_async_remote_copy(
                src_ref=x_refs[a] if from_input else slab, dst_ref=slab,
                send_sem=send_sems.at[7 * a + k], recv_sem=recv_sems.at[7 * a + k], device_id=to,
                device_id_type=MESH_ID)

        arrays = range(n_arr)
        mine = [pltpu.make_async_copy(x_refs[a], out_refs[a].at[4 * x + 2 * y + c], local_sems.at[a]) for a in arrays]
        for cp in mine:
            cp.start()
        first = [copy(a, 0, me, sibling, True) for a in arrays]
        first += [copy(a, 1 + j, me, (*chip, c), True) for j, chip in enumerate(chips) for a in arrays]
        for cp in first:
            cp.start()
        passed = []
        for j, chip in enumerate(chips):
            for a in arrays:
                copy(a, 1 + j, (*chip, c), me).wait_recv()
                passed.append(copy(a, 4 + j, (*chip, c), sibling))
                passed[-1].start()
        for a in arrays:
            copy(a, 0, sibling, me).wait_recv()
        for j, chip in enumerate(chips):
            for a in arrays:
                copy(a, 4 + j, (*chip, 1 - c), me).wait_recv()
        for cp in first + passed:
            cp.wait_send()
        for cp in mine:
            cp.wait()

    anyspec = pl.BlockSpec(memory_space=pl.ANY)
    return pl.pallas_call(
        body, name="weights_all_gather",
        out_shape=[jax.ShapeDtypeStruct((N_DEV,) + s.shape, s.dtype) for s in shards],
        in_specs=[anyspec] * n_arr, out_specs=[anyspec] * n_arr,
        scratch_shapes=[pltpu.SemaphoreType.DMA((7 * n_arr,)), pltpu.SemaphoreType.DMA((7 * n_arr,)),
                        pltpu.SemaphoreType.DMA((n_arr,))],
    )(*shards)


def _small_exchange(slabs):
    def body(g_ref, recv_ref, send_sems, recv_sems, local_sem):
        _exchange(g_ref, recv_ref, send_sems, recv_sems, local_sem, True)
        _exchange(g_ref, recv_ref, send_sems, recv_sems, local_sem, False)

    vmem = pl.BlockSpec(memory_space=pltpu.VMEM)
    return pl.pallas_call(
        body, name="small_exchange", out_shape=jax.ShapeDtypeStruct(slabs.shape, slabs.dtype),
        in_specs=[vmem], out_specs=vmem, scratch_shapes=EXCHANGE_SEMS,
    )(slabs)


def _adamw(parts, w, m, v, block_rows, name):
    rows, cols = w.shape

    def body(p_ref, w_ref, m_ref, v_ref, g_out, d_out, m_out, v_out):
        g = p_ref[0].astype(F32)
        for s in range(1, N_DEV):
            g = g + p_ref[s].astype(F32)
        m_new = ADAM_B1 * m_ref[...] + (1.0 - ADAM_B1) * g
        v_new = ADAM_B2 * v_ref[...] + (1.0 - ADAM_B2) * (g * g)
        m_hat = m_new / (1.0 - ADAM_B1 ** ADAM_STEP)
        v_hat = v_new / (1.0 - ADAM_B2 ** ADAM_STEP)
        g_out[...] = g
        d_out[...] = -ADAM_LR * (m_hat / (jnp.sqrt(v_hat) + ADAM_EPS) + ADAM_WD * w_ref[...])
        m_out[...] = m_new
        v_out[...] = v_new

    spec = pl.BlockSpec((block_rows, cols), lambda i: (i, 0))
    return pl.pallas_call(
        body, name=name, grid=(rows // block_rows,),
        in_specs=[pl.BlockSpec((N_DEV, block_rows, cols), lambda i: (0, i, 0)), spec, spec, spec],
        out_specs=[spec] * 4, out_shape=[jax.ShapeDtypeStruct((rows, cols), F32)] * 4,
        compiler_params=_cp(("parallel",), 48),
    )(parts, w, m, v)


def _pad_rows(flat, rows):
    pad = rows * LANE - flat.shape[-1]
    flat = jnp.pad(flat, [(0, 0)] * (flat.ndim - 1) + [(0, pad)])
    return flat.reshape(flat.shape[:-1] + (rows, LANE))


def _pack_shards(shards):
    return _pad_rows(jnp.concatenate([shards[n].reshape(-1) for n, _, _ in PACKED]), PACK_ROWS)


def _unpack_shards(packed):
    flat, out, off = packed.reshape(-1), {}, 0
    for n, shape, _ in PACKED:
        size = shape[0] * shape[1]
        out[n] = flat[off:off + size].reshape(shape)
        off += size
    return out


def _split8(full, axis):
    r, c = full.shape
    if axis == 0:
        return full.reshape(N_DEV, r // N_DEV, c)
    return full.reshape(r, N_DEV, c // N_DEV).transpose(1, 0, 2)


def _join8(shards, axis):
    _, r, c = shards.shape
    if axis == 0:
        return shards.reshape(N_DEV * r, c)
    return shards.transpose(1, 0, 2).reshape(r, N_DEV * c)


def _pack_small(meta_shard, vals, loss_row):
    rows = jnp.concatenate([vals[n].reshape(-1, LANE) for n, _ in REPLICATED] + [loss_row], axis=0)
    rows = jnp.pad(rows, ((0, SMALL_ROWS - N_META - rows.shape[0]), (0, 0)))
    return jnp.concatenate([meta_shard, jnp.broadcast_to(rows, meta_shard.shape[:-2] + rows.shape)], axis=-2)


def _unpack_small(packed):
    out, off = {"meta_tokens": packed[:N_META]}, N_META
    for n, size in REPLICATED:
        out[n] = packed[off:off + size // LANE].reshape(1, size)
        off += size // LANE
    return out


def kernel(x, meta_tokens, norm_g, w_in, gla_gate_w, gla_gate_b, gla_norm_g, gla_proj, mla_q_norm_g, mla_w_uq, mla_kv_norm_g, mla_w_ukv, mla_proj, w_out, final_norm_g, loss_target, m_meta_tokens, m_norm_g, m_w_in, m_gla_gate_w, m_gla_gate_b, m_gla_norm_g, m_gla_proj, m_mla_q_norm_g, m_mla_w_uq, m_mla_kv_norm_g, m_mla_w_ukv, m_mla_proj, m_w_out, m_final_norm_g, v_meta_tokens, v_norm_g, v_w_in, v_gla_gate_w, v_gla_gate_b, v_gla_norm_g, v_gla_proj, v_mla_q_norm_g, v_mla_w_uq, v_mla_kv_norm_g, v_mla_w_ukv, v_mla_proj, v_w_out, v_final_norm_g):
    given = dict(meta_tokens=meta_tokens, norm_g=norm_g, w_in=w_in, gla_gate_w=gla_gate_w, gla_gate_b=gla_gate_b,
                 gla_norm_g=gla_norm_g, gla_proj=gla_proj, mla_q_norm_g=mla_q_norm_g, mla_w_uq=mla_w_uq,
                 mla_kv_norm_g=mla_kv_norm_g, mla_w_ukv=mla_w_ukv, mla_proj=mla_proj, w_out=w_out,
                 final_norm_g=final_norm_g)
    mom_m = dict(meta_tokens=m_meta_tokens, norm_g=m_norm_g, w_in=m_w_in, gla_gate_w=m_gla_gate_w,
                 gla_gate_b=m_gla_gate_b, gla_norm_g=m_gla_norm_g, gla_proj=m_gla_proj, mla_q_norm_g=m_mla_q_norm_g,
                 mla_w_uq=m_mla_w_uq, mla_kv_norm_g=m_mla_kv_norm_g, mla_w_ukv=m_mla_w_ukv, mla_proj=m_mla_proj,
                 w_out=m_w_out, final_norm_g=m_final_norm_g)
    mom_v = dict(meta_tokens=v_meta_tokens, norm_g=v_norm_g, w_in=v_w_in, gla_gate_w=v_gla_gate_w,
                 gla_gate_b=v_gla_gate_b, gla_norm_g=v_gla_norm_g, gla_proj=v_gla_proj, mla_q_norm_g=v_mla_q_norm_g,
                 mla_w_uq=v_mla_w_uq, mla_kv_norm_g=v_mla_kv_norm_g, mla_w_ukv=v_mla_w_ukv, mla_proj=v_mla_proj,
                 w_out=v_w_out, final_norm_g=v_final_norm_g)
    shapes = {n: a.shape for n, a in given.items()}
    shard2d = {n: s for n, s, _ in PACKED}
    shard2d["w_in"] = (D_MODEL, W_IN_SHARD)
    shard2d["meta_tokens"] = (N_META, LANE)

    def as2d(tree):
        out = {n: tree[n].reshape(shard2d[n]) for n in shard2d}
        out.update({n: tree[n].reshape(1, size) for n, size in REPLICATED})
        return out

    w_loc, m_loc, v_loc = as2d(given), as2d(mom_m), as2d(mom_v)

    meta_bits = lax.bitcast_convert_type(w_loc["meta_tokens"], BF16).reshape(-1)
    flat = jnp.concatenate([w_loc[n].astype(BF16).reshape(-1) for n, _, _ in PACKED] + [meta_bits])
    w_in_all, packed_all = _all_gather([w_loc["w_in"].astype(BF16), _pad_rows(flat, GATHER_ROWS)])
    packed_all = packed_all.reshape(N_DEV, -1)
    full, off = {"w_in": w_in_all}, 0
    for n, shape, axis in PACKED:
        size = shape[0] * shape[1]
        full[n] = _join8(packed_all[:, off:off + size].reshape((N_DEV,) + shape), axis)
        off += size
    meta8 = lax.bitcast_convert_type(packed_all[:, off:off + 2 * N_META * LANE].reshape(N_DEV, N_META, LANE, 2), F32)
    full["meta_tokens"] = _join8(meta8, 1)
    for n, _ in REPLICATED:
        full[n] = w_loc[n]

    loss_part, grad_x, w_in_parts, packed_parts, small = _local_step(x, loss_target, full)
    small_all = _small_exchange(_pack_small(_split8(small["meta_tokens"], 1), small,
                                            jnp.broadcast_to(loss_part[:, :1], (1, LANE))))

    g_w, d_w, m_w, v_w = _adamw(w_in_parts, w_loc["w_in"], m_loc["w_in"], v_loc["w_in"], W_IN_BLOCK, "adamw_w_in")
    g_p, d_p, m_p, v_p = _adamw(packed_parts, _pack_shards(w_loc), _pack_shards(m_loc), _pack_shards(v_loc),
                                PACK_BLOCK, "adamw_packed")
    zero_row = jnp.zeros((1, LANE), F32)
    g_s, d_s, m_s, v_s = _adamw(small_all, *(_pack_small(t["meta_tokens"], t, zero_row) for t in (w_loc, m_loc, v_loc)),
                                SMALL_ROWS, "adamw_small")
    loss = g_s[LOSS_ROW, 0]

    order = ["meta_tokens", "norm_g", "w_in", "gla_gate_w", "gla_gate_b", "gla_norm_g", "gla_proj", "mla_q_norm_g",
             "mla_w_uq", "mla_kv_norm_g", "mla_w_ukv", "mla_proj", "w_out", "final_norm_g"]
    result = [loss, grad_x]
    for w_in_out, packed_sh, packed_sm in ((g_w, g_p, g_s), (d_w, d_p, d_s), (m_w, m_p, m_s), (v_w, v_p, v_s)):
        tree = _unpack_shards(packed_sh)
        tree.update(_unpack_small(packed_sm))
        tree["w_in"] = w_in_out
        result += [tree[n].reshape(shapes[n]) for n in order]
    return tuple(result)
```

```python
import jax
import jax.numpy as jnp
from jax import lax
from jax.experimental import pallas as pl
from jax.experimental.pallas import tpu as pltpu

F32 = jnp.float32
BF16 = jnp.bfloat16

D_MODEL = 1024
N_META = 16
EPS = 1e-6
FRONT = 48
X0 = FRONT + N_META
GLA_HEADS, GLA_DK, GLA_DV, GLA_RANK, GLA_CHUNK = 4, 128, 256, 16, 64
GLA_GATE_NORMALIZER = 16.0
GLA_KW = GLA_HEADS * GLA_DK
GLA_VW = GLA_HEADS * GLA_DV
MLA_HEADS, MLA_NOPE, MLA_ROPE, MLA_DV, MLA_QR, MLA_KVR = 8, 128, 64, 128, 256, 128
MLA_QK = MLA_NOPE + MLA_ROPE
ROPE_BASE = 10000.0
LANE = 128
QKW = 2 * LANE

C_V, C_Z, C_Q, C_K = 0, 1024, 2048, 2560
C_MZ = 3072
C_GG, C_GM = 4096, 5120
C_CKV, C_KR, C_KROT, C_LR = 6144, 6272, 6400, 6528
C_CQ = 6656
N_EXT = 6912
O_Q, O_K, O_V, O_LR, O_Z, O_CQ, O_CKV, O_KR, O_MZ, O_GG, O_GM, N_IN = (
    0, 512, 1024, 2048, 2064, 3088, 3344, 3472, 3536, 4560, 5584, 6608)

ADAM_LR, ADAM_B1, ADAM_B2, ADAM_EPS, ADAM_WD, ADAM_STEP = 0.001, 0.9, 0.999, 1e-08, 0.01, 10

N_DEV = 8
TOK = 192
ATT_BLOCK = 352
EXT_BLOCK = 1152
NEG = -1e30


def _cp(sems=None, vmem_mb=None):
    kw = {}
    if sems is not None:
        kw["dimension_semantics"] = sems
    if vmem_mb is not None:
        kw["vmem_limit_bytes"] = vmem_mb * 1024 * 1024
    return pltpu.CompilerParams(**kw)


def _dot(a, b):
    return jnp.dot(a, b, preferred_element_type=F32)


def _dot_nt(a, b):
    return lax.dot_general(a, b, (((1,), (1,)), ((), ())), preferred_element_type=F32)


def _dot_tn(a, b):
    return lax.dot_general(a, b, (((0,), (0,)), ((), ())), preferred_element_type=F32)


def _sigmoid(x):
    return 1.0 / (1.0 + jnp.exp(-x))


def _bf(x):
    return x.astype(BF16)


def _big_tok(tp):
    return 4 * TOK if tp % (4 * TOK) == 0 else TOK


def _attn_block(lp):
    return ATT_BLOCK if lp % ATT_BLOCK == 0 else TOK


def _proj_in(hp, norm_g, w_ext):
    tp = hp.shape[0]
    tm, tn = _big_tok(tp), EXT_BLOCK

    def body(h_ref, g_ref, w_ref, u_ref, o_ref, u_scr):
        @pl.when(pl.program_id(1) == 0)
        def _():
            x = h_ref[...]
            r = lax.rsqrt(jnp.mean(x * x, axis=-1, keepdims=True) + EPS)
            u = _bf(x * r * g_ref[...])
            u_scr[...] = u
            u_ref[...] = u

        o_ref[...] = _bf(_dot(u_scr[...], w_ref[...]))

    return pl.pallas_call(
        body, name="proj_in", grid=(tp // tm, N_EXT // tn),
        in_specs=[pl.BlockSpec((tm, D_MODEL), lambda i, j: (i, 0)),
                  pl.BlockSpec((1, D_MODEL), lambda i, j: (0, 0)),
                  pl.BlockSpec((D_MODEL, tn), lambda i, j: (0, j))],
        out_specs=[pl.BlockSpec((tm, D_MODEL), lambda i, j: (i, 0)),
                   pl.BlockSpec((tm, tn), lambda i, j: (i, j))],
        out_shape=[jax.ShapeDtypeStruct((tp, D_MODEL), BF16), jax.ShapeDtypeStruct((tp, N_EXT), BF16)],
        scratch_shapes=[pltpu.VMEM((tm, D_MODEL), BF16)],
        compiler_params=_cp(("parallel", "arbitrary"), 48),
    )(hp, norm_g, w_ext)


GLA_GROUP = 3
GLA_ROWS = GLA_GROUP * GLA_CHUNK


def _gla_gates(q_ref, k_ref, lr_ref, gw_ref, gb_ref, rows, not_first):
    z = _dot(lr_ref[rows, :], gw_ref[...]) + gb_ref[...]
    logsig = jnp.minimum(z, 0.0) - jnp.log(1.0 + jnp.exp(-jnp.abs(z)))
    row = lax.broadcasted_iota(jnp.int32, (GLA_CHUNK, GLA_KW), 0)
    live = jnp.logical_or(not_first, row >= FRONT)
    g = jnp.where(live, logsig * (1.0 / GLA_GATE_NORMALIZER), 0.0)
    ri = lax.broadcasted_iota(jnp.int32, (GLA_CHUNK, GLA_CHUNK), 0)
    ci = lax.broadcasted_iota(jnp.int32, (GLA_CHUNK, GLA_CHUNK), 1)
    tril = ci <= ri
    b = jnp.dot(tril.astype(F32), g, precision=lax.Precision.HIGHEST, preferred_element_type=F32)
    bl = jnp.sum(jnp.where(row == GLA_CHUNK - 1, b, 0.0), axis=0, keepdims=True)
    eb, enb, elb, ebl = jnp.exp(b), jnp.exp(-b), jnp.exp(bl - b), jnp.exp(bl)
    q = q_ref[rows, :].astype(F32) * (GLA_DK ** -0.5)
    k = k_ref[rows, :].astype(F32)
    qe, ke, kl = q * eb, k * enb, k * elb
    return dict(z=z, live=live, tril=tril, row=row, eb=eb, enb=enb, elb=elb, ebl=ebl, qe=qe, ke=ke, kl=kl,
                qe_b=_bf(qe), ke_b=_bf(ke), kl_b=_bf(kl))


def _gla_in_specs(n_groups, rev):
    def rb(b, n):
        return b * n_groups + ((n_groups - 1 - n) if rev else n)

    return rb, [pl.BlockSpec((GLA_ROWS, GLA_KW), lambda b, n: (rb(b, n), C_Q // GLA_KW)),
                pl.BlockSpec((GLA_ROWS, GLA_KW), lambda b, n: (rb(b, n), C_K // GLA_KW)),
                pl.BlockSpec((GLA_ROWS, GLA_VW), lambda b, n: (rb(b, n), C_V // GLA_VW)),
                pl.BlockSpec((GLA_ROWS, GLA_VW), lambda b, n: (rb(b, n), C_Z // GLA_VW)),
                pl.BlockSpec((GLA_ROWS, LANE), lambda b, n: (rb(b, n), C_LR // LANE)),
                pl.BlockSpec((LANE, GLA_KW), lambda b, n: (0, 0)),
                pl.BlockSpec((1, GLA_KW), lambda b, n: (0, 0)),
                pl.BlockSpec((1, GLA_DV), lambda b, n: (0, 0))]


def _gla_fwd(proj, gw_pad, gate_b, gla_norm_g, bsz, lp):
    n_chunks = lp // GLA_CHUNK
    n_groups = n_chunks // GLA_GROUP
    tp = bsz * lp

    def body(q_ref, k_ref, v_ref, z_ref, lr_ref, gw_ref, gb_ref, gn_ref, oraw_ref, ya_ref, sall_ref, st_scr):
        grp = pl.program_id(1)

        @pl.when(grp == 0)
        def _():
            st_scr[...] = jnp.zeros_like(st_scr)

        chunks = [slice(j * GLA_CHUNK, (j + 1) * GLA_CHUNK) for j in range(GLA_GROUP)]
        cs = [_gla_gates(q_ref, k_ref, lr_ref, gw_ref, gb_ref, rows, True if j else grp > 0)
              for j, rows in enumerate(chunks)]
        gn = gn_ref[...]
        for h in range(GLA_HEADS):
            ks, vs = slice(h * GLA_DK, (h + 1) * GLA_DK), slice(h * GLA_DV, (h + 1) * GLA_DV)
            st = st_scr[h]
            for j, (rows, c) in enumerate(zip(chunks, cs)):
                sall_ref[0, j, h] = st
                v = v_ref[rows, vs]
                a = jnp.where(c["tril"], _dot_nt(c["qe_b"][:, ks], c["ke_b"][:, ks]), 0.0)
                o = _dot(_bf(a), v) + _dot_nt(c["qe_b"][:, ks], _bf(st))
                st = st * c["ebl"][:, ks] + _dot_tn(v, c["kl_b"][:, ks])
                oraw_ref[rows, vs] = o
                r = lax.rsqrt(jnp.mean(o * o, axis=-1, keepdims=True) + EPS)
                zg = z_ref[rows, vs].astype(F32)
                ya_ref[rows, vs] = _bf((o * r * gn) * (zg * _sigmoid(zg)))
            st_scr[h] = st

    rb, in_specs = _gla_in_specs(n_groups, False)
    return pl.pallas_call(
        body, name="gla_fwd", grid=(bsz, n_groups), in_specs=in_specs,
        out_specs=[pl.BlockSpec((GLA_ROWS, GLA_VW), lambda b, n: (rb(b, n), 0)),
                   pl.BlockSpec((GLA_ROWS, GLA_VW), lambda b, n: (rb(b, n), 0)),
                   pl.BlockSpec((1, GLA_GROUP, GLA_HEADS, GLA_DV, GLA_DK), lambda b, n: (b, n, 0, 0, 0))],
        out_shape=[jax.ShapeDtypeStruct((tp, GLA_VW), F32), jax.ShapeDtypeStruct((tp, GLA_VW), BF16),
                   jax.ShapeDtypeStruct((bsz, n_chunks, GLA_HEADS, GLA_DV, GLA_DK), F32)],
        scratch_shapes=[pltpu.VMEM((GLA_HEADS, GLA_DV, GLA_DK), F32)],
        compiler_params=_cp(("parallel", "arbitrary")),
    )(proj, proj, proj, proj, proj, gw_pad, gate_b, gla_norm_g)


def _gla_bwd(proj, gw_pad, gate_b, gla_norm_g, o_raw, s_all, d_ya, dproj, bsz, lp):
    n_chunks = lp // GLA_CHUNK
    n_groups = n_chunks // GLA_GROUP
    tp = bsz * lp

    def body(q_ref, k_ref, v_ref, z_ref, lr_ref, gw_ref, gb_ref, gn_ref, o_ref, s_ref, dya_ref, _,
             dp_ref, dz_ref, dgn_ref, dst_scr):
        dv_ref, dzg_ref = dp_ref.at[:, C_V:C_V + GLA_VW], dp_ref.at[:, C_Z:C_Z + GLA_VW]

        @pl.when(jnp.logical_and(pl.program_id(0) == 0, pl.program_id(1) == 0))
        def _():
            dgn_ref[...] = jnp.zeros_like(dgn_ref)

        @pl.when(pl.program_id(1) == 0)
        def _():
            dst_scr[...] = jnp.zeros_like(dst_scr)

        grp = n_groups - 1 - pl.program_id(1)
        chunks = [slice(j * GLA_CHUNK, (j + 1) * GLA_CHUNK) for j in range(GLA_GROUP)]
        cs = [_gla_gates(q_ref, k_ref, lr_ref, gw_ref, gb_ref, rows, True if j else grp > 0)
              for j, rows in enumerate(chunks)]
        gn = gn_ref[...]
        dgn = jnp.zeros((1, GLA_DV), F32)
        dqe_h, dke_h, dkl_h, dbl_h = ([[None] * GLA_HEADS for _ in chunks] for _ in range(4))
        for h in range(GLA_HEADS):
            ks, vs = slice(h * GLA_DK, (h + 1) * GLA_DK), slice(h * GLA_DV, (h + 1) * GLA_DV)
            dst = dst_scr[h]
            for j in reversed(range(GLA_GROUP)):
                rows, c = chunks[j], cs[j]
                v = v_ref[rows, vs]
                st = s_ref[0, j, h]
                o = o_ref[rows, vs]
                r = lax.rsqrt(jnp.mean(o * o, axis=-1, keepdims=True) + EPS)
                xh = o * r
                zg = z_ref[rows, vs].astype(F32)
                sg = _sigmoid(zg)
                dy = dya_ref[rows, vs].astype(F32)
                dzg_ref[rows, vs] = _bf(dy * (xh * gn) * (sg * (1.0 + zg * (1.0 - sg))))
                t = dy * (zg * sg)
                dgn += jnp.sum(t * xh, axis=0, keepdims=True)
                dxh = t * gn
                do_b = _bf(r * (dxh - xh * jnp.mean(dxh * xh, axis=-1, keepdims=True)))
                qe_b, ke_b, kl_b, dst_b = c["qe_b"][:, ks], c["ke_b"][:, ks], c["kl_b"][:, ks], _bf(dst)
                a = jnp.where(c["tril"], _dot_nt(qe_b, ke_b), 0.0)
                da_b = _bf(jnp.where(c["tril"], _dot_nt(do_b, v), 0.0))
                dqe_h[j][h] = _dot(da_b, ke_b) + _dot(do_b, _bf(st))
                dke_h[j][h] = _dot_tn(da_b, qe_b)
                dkl = _dot(v, dst_b)
                dkl_h[j][h] = dkl
                dv_ref[rows, vs] = _bf(_dot_tn(_bf(a), do_b) + _dot_nt(kl_b, dst_b))
                ddecay = jnp.sum(dst * st, axis=0, keepdims=True)
                dbl_h[j][h] = jnp.sum(dkl * c["kl"][:, ks], axis=0, keepdims=True) + ddecay * c["ebl"][:, ks]
                dst = dst * c["ebl"][:, ks] + _dot_tn(do_b, qe_b)
            dst_scr[h] = dst
        dgn_ref[...] += dgn
        ri = lax.broadcasted_iota(jnp.int32, (GLA_CHUNK, GLA_CHUNK), 0)
        ci = lax.broadcasted_iota(jnp.int32, (GLA_CHUNK, GLA_CHUNK), 1)
        triu = (ci >= ri).astype(F32)
        for j, (rows, c) in enumerate(zip(chunks, cs)):
            dqe, dke, dkl, dbl = (jnp.concatenate(p[j], axis=1) for p in (dqe_h, dke_h, dkl_h, dbl_h))
            db = dqe * c["qe"] - dke * c["ke"] - dkl * c["kl"] + jnp.where(c["row"] == GLA_CHUNK - 1, dbl, 0.0)
            dg = jnp.dot(triu, db, precision=lax.Precision.HIGHEST, preferred_element_type=F32)
            dg = jnp.where(c["live"], dg, 0.0)
            dz_ref[rows, :] = dg * (1.0 / GLA_GATE_NORMALIZER) * _sigmoid(-c["z"])
            dp_ref[rows, C_Q:C_Q + GLA_KW] = _bf(dqe * c["eb"] * (GLA_DK ** -0.5))
            dp_ref[rows, C_K:C_K + GLA_KW] = _bf(dke * c["enb"] + dkl * c["elb"])

    rb, in_specs = _gla_in_specs(n_groups, True)
    wide = pl.BlockSpec((GLA_ROWS, GLA_VW), lambda b, n: (rb(b, n), 0))
    group = C_MZ
    return pl.pallas_call(
        body, name="gla_bwd", grid=(bsz, n_groups),
        in_specs=in_specs + [wide, pl.BlockSpec((1, GLA_GROUP, GLA_HEADS, GLA_DV, GLA_DK),
                                                lambda b, n: (b, n_groups - 1 - n, 0, 0, 0)), wide,
                             pl.BlockSpec(memory_space=pl.ANY)],
        out_specs=[pl.BlockSpec((GLA_ROWS, group), lambda b, n: (rb(b, n), 0)),
                   pl.BlockSpec((GLA_ROWS, GLA_KW), lambda b, n: (rb(b, n), 0)),
                   pl.BlockSpec((1, GLA_DV), lambda b, n: (0, 0))],
        out_shape=[jax.ShapeDtypeStruct((tp, N_EXT), BF16), jax.ShapeDtypeStruct((tp, GLA_KW), F32),
                   jax.ShapeDtypeStruct((1, GLA_DV), F32)],
        input_output_aliases={11: 0},
        scratch_shapes=[pltpu.VMEM((GLA_HEADS, GLA_DV, GLA_DK), F32)],
        compiler_params=_cp(("arbitrary", "arbitrary")),
    )(proj, proj, proj, proj, proj, gw_pad, gate_b, gla_norm_g, o_raw, s_all, d_ya, dproj)


def _gate_bwd(dz, proj, gw_pad_t):
    tp = dz.shape[0]
    tm = _big_tok(tp)

    def body(dz_ref, lr_ref, gwt_ref, dlr_ref, dgw_ref, dgb_ref):
        @pl.when(pl.program_id(0) == 0)
        def _():
            dgw_ref[...] = jnp.zeros_like(dgw_ref)
            dgb_ref[...] = jnp.zeros_like(dgb_ref)

        dz = dz_ref[...]
        dz_b = _bf(dz)
        dlr_ref[...] = _bf(_dot(dz_b, gwt_ref[...]))
        dgw_ref[...] += _dot_tn(lr_ref[...], dz_b)
        dgb_ref[...] += jnp.sum(dz, axis=0, keepdims=True)

    return pl.pallas_call(
        body, name="gate_bwd", grid=(tp // tm,),
        in_specs=[pl.BlockSpec((tm, GLA_KW), lambda i: (i, 0)),
                  pl.BlockSpec((tm, LANE), lambda i: (i, C_LR // LANE)),
                  pl.BlockSpec((GLA_KW, LANE), lambda i: (0, 0))],
        out_specs=[pl.BlockSpec((tm, LANE), lambda i: (i, 0)),
                   pl.BlockSpec((LANE, GLA_KW), lambda i: (0, 0)),
                   pl.BlockSpec((1, GLA_KW), lambda i: (0, 0))],
        out_shape=[jax.ShapeDtypeStruct((tp, LANE), BF16), jax.ShapeDtypeStruct((LANE, GLA_KW), F32),
                   jax.ShapeDtypeStruct((1, GLA_KW), F32)],
        compiler_params=_cp(("arbitrary",)),
    )(dz, proj, gw_pad_t)


def _rms_fwd(x):
    r = lax.rsqrt(jnp.mean(x * x, axis=-1, keepdims=True) + EPS)
    return x * r, r


def _rms_bwd(dy, xh, r, g):
    dxh = dy * g
    dx = r * (dxh - xh * jnp.mean(dxh * xh, axis=-1, keepdims=True))
    return dx, jnp.sum(dy * xh, axis=0, keepdims=True)


def _q_up(proj, q_norm_g, wn, wr, wt, cos_t, sin_t, bsz, lp):
    tp = bsz * lp
    nb = lp // TOK

    def body(cq_ref, g_ref, wn_ref, wr_ref, wt_ref, cos_ref, sin_ref, q_ref):
        xh, _ = _rms_fwd(cq_ref[...].astype(F32))
        cqn = _bf(xh * g_ref[...])
        nope = _dot(cqn, wn_ref[...])
        rope = _dot(cqn, wr_ref[...])
        rot = _dot(cqn, wt_ref[...])
        cos, sin = cos_ref[...], sin_ref[...]
        for h in range(MLA_HEADS):
            sl = slice(h * LANE, (h + 1) * LANE)
            q_ref[:, h * QKW:h * QKW + LANE] = _bf(nope[:, sl])
            q_ref[:, h * QKW + LANE:(h + 1) * QKW] = _bf(rope[:, sl] * cos + rot[:, sl] * sin)

    wspec = pl.BlockSpec((MLA_QR, MLA_HEADS * LANE), lambda b, i: (0, 0))
    tspec = pl.BlockSpec((TOK, LANE), lambda b, i: (i, 0))
    return pl.pallas_call(
        body, name="mla_q_up", grid=(bsz, nb),
        in_specs=[pl.BlockSpec((TOK, MLA_QR), lambda b, i: (b * nb + i, C_CQ // MLA_QR)),
                  pl.BlockSpec((1, MLA_QR), lambda b, i: (0, 0)), wspec, wspec, wspec, tspec, tspec],
        out_specs=pl.BlockSpec((TOK, MLA_HEADS * QKW), lambda b, i: (b * nb + i, 0)),
        out_shape=jax.ShapeDtypeStruct((tp, MLA_HEADS * QKW), BF16),
        compiler_params=_cp(("parallel", "parallel")),
    )(proj, q_norm_g, wn, wr, wt, cos_t, sin_t)


def _kv_up(proj, kv_norm_g, wk, wv, cos_t, sin_t, bsz, lp):
    tp = bsz * lp
    nb = lp // TOK

    def body(ckv_ref, kr_ref, krot_ref, g_ref, wk_ref, wv_ref, cos_ref, sin_ref, k_ref, v_ref):
        xh, _ = _rms_fwd(ckv_ref[...].astype(F32))
        cn = _bf(xh * g_ref[...])
        kn = _dot(cn, wk_ref[...])
        v_ref[...] = _bf(_dot(cn, wv_ref[...]))
        kr = _bf(kr_ref[...].astype(F32) * cos_ref[...] + krot_ref[...].astype(F32) * sin_ref[...])
        for h in range(MLA_HEADS):
            k_ref[:, h * QKW:h * QKW + LANE] = _bf(kn[:, h * LANE:(h + 1) * LANE])
            k_ref[:, h * QKW + LANE:(h + 1) * QKW] = kr

    wspec = pl.BlockSpec((MLA_KVR, MLA_HEADS * LANE), lambda b, i: (0, 0))
    tspec = pl.BlockSpec((TOK, LANE), lambda b, i: (i, 0))
    return pl.pallas_call(
        body, name="mla_kv_up", grid=(bsz, nb),
        in_specs=[pl.BlockSpec((TOK, LANE), lambda b, i: (b * nb + i, C_CKV // LANE)),
                  pl.BlockSpec((TOK, LANE), lambda b, i: (b * nb + i, C_KR // LANE)),
                  pl.BlockSpec((TOK, LANE), lambda b, i: (b * nb + i, C_KROT // LANE)),
                  pl.BlockSpec((1, MLA_KVR), lambda b, i: (0, 0)), wspec, wspec, tspec, tspec],
        out_specs=[pl.BlockSpec((TOK, MLA_HEADS * QKW), lambda b, i: (b * nb + i, 0)),
                   pl.BlockSpec((TOK, MLA_HEADS * LANE), lambda b, i: (b * nb + i, 0))],
        out_shape=[jax.ShapeDtypeStruct((tp, MLA_HEADS * QKW), BF16),
                   jax.ShapeDtypeStruct((tp, MLA_HEADS * LANE), BF16)],
        compiler_params=_cp(("parallel", "parallel")),
    )(proj, proj, proj, kv_norm_g, wk, wv, cos_t, sin_t)


ATT_SCALE = MLA_QK ** -0.5


def _attn_mask(r0, tq, kmax):
    qpos = r0 + lax.broadcasted_iota(jnp.int32, (tq, kmax), 0)
    kpos = lax.broadcasted_iota(jnp.int32, (tq, kmax), 1)
    real = kpos >= FRONT
    if r0 < FRONT:
        real = jnp.logical_or(real, qpos < FRONT)
    return jnp.logical_and(kpos <= qpos, real)


def _attn_fwd(qf, kf, vf, proj, bsz, lp):
    tp = bsz * lp
    tq = _attn_block(lp)

    def body(q_ref, k_ref, v_ref, mz_ref, ob_ref, yb_ref, lse_ref):
        for r0 in range(0, lp, tq):
            rows, kmax = slice(r0, r0 + tq), r0 + tq
            s = _dot_nt(q_ref[rows, :], k_ref[0:kmax, :]) * ATT_SCALE
            s = jnp.where(_attn_mask(r0, tq, kmax), s, NEG)
            m = jnp.max(s, axis=-1, keepdims=True)
            p = jnp.exp(s - m)
            l = jnp.sum(p, axis=-1, keepdims=True)
            o = _dot(_bf(p), v_ref[0:kmax, :]) / l
            ob_ref[rows, :] = _bf(o)
            mz = mz_ref[rows, :].astype(F32)
            yb_ref[rows, :] = _bf(o * (mz * _sigmoid(mz)))
            lse_ref[0, 0, rows, :] = jnp.broadcast_to(m + jnp.log(l), (tq, LANE))

    head = lambda off: pl.BlockSpec((lp, MLA_DV), lambda b, h: (b, off + h))
    return pl.pallas_call(
        body, name="mla_attn_fwd", grid=(bsz, MLA_HEADS),
        in_specs=[pl.BlockSpec((lp, QKW), lambda b, h: (b, h)), pl.BlockSpec((lp, QKW), lambda b, h: (b, h)),
                  head(0), head(C_MZ // MLA_DV)],
        out_specs=[head(0), head(0), pl.BlockSpec((1, 1, lp, LANE), lambda b, h: (b, h, 0, 0))],
        out_shape=[jax.ShapeDtypeStruct((tp, MLA_HEADS * MLA_DV), BF16),
                   jax.ShapeDtypeStruct((tp, MLA_HEADS * MLA_DV), BF16),
                   jax.ShapeDtypeStruct((bsz, MLA_HEADS, lp, LANE), F32)],
        compiler_params=_cp(("parallel", "parallel"), 56),
    )(qf, kf, vf, proj)


def _attn_bwd_pre(d_yb, proj, o_b, dproj, bsz, lp):
    tp = bsz * lp
    nb = lp // TOK
    w = MLA_HEADS * MLA_DV

    def body(dy_ref, mz_ref, o_ref, _, do_ref, dmz_ref, dl_ref):
        dy = dy_ref[...].astype(F32)
        mz = mz_ref[...].astype(F32)
        o = o_ref[...].astype(F32)
        s = _sigmoid(mz)
        do = _bf(dy * (mz * s))
        do_ref[...] = do
        dmz_ref[...] = _bf(dy * o * (s * (1.0 + mz * (1.0 - s))))
        prod = do.astype(F32) * o
        for h in range(MLA_HEADS):
            dl = jnp.sum(prod[:, h * MLA_DV:(h + 1) * MLA_DV], axis=-1, keepdims=True)
            dl_ref[0, h] = jnp.broadcast_to(dl, (TOK, LANE))

    return pl.pallas_call(
        body, name="mla_attn_bwd_pre", grid=(bsz, nb),
        in_specs=[pl.BlockSpec((TOK, w), lambda b, i: (b * nb + i, 0)),
                  pl.BlockSpec((TOK, w), lambda b, i: (b * nb + i, C_MZ // w)),
                  pl.BlockSpec((TOK, w), lambda b, i: (b * nb + i, 0)), pl.BlockSpec(memory_space=pl.ANY)],
        out_specs=[pl.BlockSpec((TOK, w), lambda b, i: (b * nb + i, 0)),
                   pl.BlockSpec((TOK, w), lambda b, i: (b * nb + i, C_MZ // w)),
                   pl.BlockSpec((1, MLA_HEADS, TOK, LANE), lambda b, i: (b, 0, i, 0))],
        out_shape=[jax.ShapeDtypeStruct((tp, w), BF16), jax.ShapeDtypeStruct((tp, N_EXT), BF16),
                   jax.ShapeDtypeStruct((bsz, MLA_HEADS, lp, LANE), F32)],
        input_output_aliases={3: 1},
        compiler_params=_cp(("parallel", "parallel")),
    )(d_yb, proj, o_b, dproj)


def _attn_bwd(qf, kf, vf, d_o, lse, delta, bsz, lp):
    tp = bsz * lp
    tq = _attn_block(lp)

    def body(q_ref, k_ref, v_ref, do_ref, lse_ref, dl_ref, dq_ref, dk_ref, dv_ref, dk_acc, dv_acc):
        dk_acc[...] = jnp.zeros_like(dk_acc)
        dv_acc[...] = jnp.zeros_like(dv_acc)
        for r0 in range(0, lp, tq):
            rows, kmax = slice(r0, r0 + tq), r0 + tq
            q, do = q_ref[rows, :], do_ref[rows, :]
            k, v = k_ref[0:kmax, :], v_ref[0:kmax, :]
            s = _dot_nt(q, k) * ATT_SCALE
            p = jnp.where(_attn_mask(r0, tq, kmax), jnp.exp(s - lse_ref[0, 0, rows, :][:, :1]), 0.0)
            ds = _bf(p * (_dot_nt(do, v) - dl_ref[0, 0, rows, :][:, :1]) * ATT_SCALE)
            dq_ref[rows, :] = _bf(_dot(ds, k))
            dk_acc[0:kmax, :] += _dot_tn(ds, q)
            dv_acc[0:kmax, :] += _dot_tn(_bf(p), do)
        dk_ref[...] = _bf(dk_acc[...])
        dv_ref[...] = _bf(dv_acc[...])

    wide = pl.BlockSpec((lp, QKW), lambda b, h: (b, h))
    narrow = pl.BlockSpec((lp, MLA_DV), lambda b, h: (b, h))
    stat = pl.BlockSpec((1, 1, lp, LANE), lambda b, h: (b, h, 0, 0))
    return pl.pallas_call(
        body, name="mla_attn_bwd", grid=(bsz, MLA_HEADS),
        in_specs=[wide, wide, narrow, narrow, stat, stat], out_specs=[wide, wide, narrow],
        out_shape=[jax.ShapeDtypeStruct((tp, MLA_HEADS * QKW), BF16), jax.ShapeDtypeStruct((tp, MLA_HEADS * QKW), BF16),
                   jax.ShapeDtypeStruct((tp, MLA_HEADS * MLA_DV), BF16)],
        scratch_shapes=[pltpu.VMEM((lp, QKW), F32), pltpu.VMEM((lp, MLA_DV), F32)],
        compiler_params=_cp(("parallel", "parallel"), 56),
    )(qf, kf, vf, d_o, lse, delta)


def _q_up_bwd(dqf, proj, q_norm_g, wn_t, wr_t, wt_t, cos_t, sin_t, dproj, bsz, lp):
    tp = bsz * lp
    nb = lp // TOK
    hw = MLA_HEADS * LANE

    def body(dq_ref, cq_ref, g_ref, wn_ref, wr_ref, wt_ref, cos_ref, sin_ref, _,
             dcq_ref, dwn_ref, dwr_ref, dwt_ref, dg_ref):
        @pl.when(jnp.logical_and(pl.program_id(0) == 0, pl.program_id(1) == 0))
        def _():
            for r in (dwn_ref, dwr_ref, dwt_ref, dg_ref):
                r[...] = jnp.zeros_like(r)

        g = g_ref[...]
        xh, r = _rms_fwd(cq_ref[...].astype(F32))
        cqn = _bf(xh * g)
        cos, sin = cos_ref[...], sin_ref[...]
        dcqn = jnp.zeros((TOK, MLA_QR), F32)
        for h in range(MLA_HEADS):
            sl = slice(h * LANE, (h + 1) * LANE)
            dn = dq_ref[:, h * QKW:h * QKW + LANE]
            dr = dq_ref[:, h * QKW + LANE:(h + 1) * QKW].astype(F32)
            dr_c, dr_s = _bf(dr * cos), _bf(dr * sin)
            dcqn += _dot(dn, wn_ref[sl, :]) + _dot(dr_c, wr_ref[sl, :]) + _dot(dr_s, wt_ref[sl, :])
            dwn_ref[:, sl] += _dot_tn(cqn, dn)
            dwr_ref[:, sl] += _dot_tn(cqn, dr_c)
            dwt_ref[:, sl] += _dot_tn(cqn, dr_s)
        dx, dg = _rms_bwd(dcqn, xh, r, g)
        dcq_ref[...] = _bf(dx)
        dg_ref[...] += dg

    wspec = pl.BlockSpec((hw, MLA_QR), lambda b, i: (0, 0))
    aspec = pl.BlockSpec((MLA_QR, hw), lambda b, i: (0, 0))
    tspec = pl.BlockSpec((TOK, LANE), lambda b, i: (i, 0))
    return pl.pallas_call(
        body, name="mla_q_up_bwd", grid=(bsz, nb),
        in_specs=[pl.BlockSpec((TOK, MLA_HEADS * QKW), lambda b, i: (b * nb + i, 0)),
                  pl.BlockSpec((TOK, MLA_QR), lambda b, i: (b * nb + i, C_CQ // MLA_QR)),
                  pl.BlockSpec((1, MLA_QR), lambda b, i: (0, 0)), wspec, wspec, wspec, tspec, tspec,
                  pl.BlockSpec(memory_space=pl.ANY)],
        out_specs=[pl.BlockSpec((TOK, MLA_QR), lambda b, i: (b * nb + i, C_CQ // MLA_QR)), aspec, aspec, aspec,
                   pl.BlockSpec((1, MLA_QR), lambda b, i: (0, 0))],
        out_shape=[jax.ShapeDtypeStruct((tp, N_EXT), BF16)] + [jax.ShapeDtypeStruct((MLA_QR, hw), F32)] * 3
        + [jax.ShapeDtypeStruct((1, MLA_QR), F32)],
        input_output_aliases={8: 0},
        compiler_params=_cp(("arbitrary", "arbitrary")),
    )(dqf, proj, q_norm_g, wn_t, wr_t, wt_t, cos_t, sin_t, dproj)


def _kv_up_bwd(dkf, dvf, proj, kv_norm_g, wk_t, wv_t, cos_t, sin_t, d_lr, dproj, bsz, lp):
    tp = bsz * lp
    nb = lp // TOK
    hw = MLA_HEADS * LANE

    def body(dk_ref, dv_ref, ckv_ref, g_ref, wk_ref, wv_ref, cos_ref, sin_ref, dlr_ref, _,
             dp_ref, dwk_ref, dwv_ref, dg_ref):
        dckv_ref, dkr_ref, dkrot_ref = (dp_ref.at[:, j * LANE:(j + 1) * LANE] for j in range(3))
        dp_ref[:, 3 * LANE:] = dlr_ref[...]
        @pl.when(jnp.logical_and(pl.program_id(0) == 0, pl.program_id(1) == 0))
        def _():
            for r in (dwk_ref, dwv_ref, dg_ref):
                r[...] = jnp.zeros_like(r)

        g = g_ref[...]
        xh, r = _rms_fwd(ckv_ref[...].astype(F32))
        cn = _bf(xh * g)
        dv = dv_ref[...]
        dcn = _dot(dv, wv_ref[...])
        dwv_ref[...] += _dot_tn(cn, dv)
        drope = jnp.zeros((TOK, LANE), F32)
        for h in range(MLA_HEADS):
            sl = slice(h * LANE, (h + 1) * LANE)
            dn = dk_ref[:, h * QKW:h * QKW + LANE]
            drope += dk_ref[:, h * QKW + LANE:(h + 1) * QKW].astype(F32)
            dcn += _dot(dn, wk_ref[sl, :])
            dwk_ref[:, sl] += _dot_tn(cn, dn)
        dkr_ref[...] = _bf(drope * cos_ref[...])
        dkrot_ref[...] = _bf(drope * sin_ref[...])
        dx, dg = _rms_bwd(dcn, xh, r, g)
        dckv_ref[...] = _bf(dx)
        dg_ref[...] += dg

    wspec = pl.BlockSpec((hw, MLA_KVR), lambda b, i: (0, 0))
    aspec = pl.BlockSpec((MLA_KVR, hw), lambda b, i: (0, 0))
    tspec = pl.BlockSpec((TOK, LANE), lambda b, i: (i, 0))
    ospec = pl.BlockSpec((TOK, LANE), lambda b, i: (b * nb + i, 0))
    return pl.pallas_call(
        body, name="mla_kv_up_bwd", grid=(bsz, nb),
        in_specs=[pl.BlockSpec((TOK, MLA_HEADS * QKW), lambda b, i: (b * nb + i, 0)),
                  pl.BlockSpec((TOK, hw), lambda b, i: (b * nb + i, 0)),
                  pl.BlockSpec((TOK, LANE), lambda b, i: (b * nb + i, C_CKV // LANE)),
                  pl.BlockSpec((1, MLA_KVR), lambda b, i: (0, 0)), wspec, wspec, tspec, tspec, ospec,
                  pl.BlockSpec(memory_space=pl.ANY)],
        out_specs=[pl.BlockSpec((TOK, 4 * LANE), lambda b, i: (b * nb + i, C_CKV // (4 * LANE))), aspec, aspec,
                   pl.BlockSpec((1, MLA_KVR), lambda b, i: (0, 0))],
        out_shape=[jax.ShapeDtypeStruct((tp, N_EXT), BF16)] + [jax.ShapeDtypeStruct((MLA_KVR, hw), F32)] * 2
        + [jax.ShapeDtypeStruct((1, MLA_KVR), F32)],
        input_output_aliases={9: 0},
        compiler_params=_cp(("arbitrary", "arbitrary")),
    )(dkf, dvf, proj, kv_norm_g, wk_t, wv_t, cos_t, sin_t, d_lr, dproj)


def _mid_fwd(ya_in, yb_in, proj, hp, target, w_gp, w_mp, w_o, final_g, bsz, lp):
    tp = bsz * lp
    tm = _attn_block(lp)
    nb = lp // tm

    def body(ya_ref, yb_ref, gg_ref, gm_ref, h_ref, t_ref, wgp_ref, wmp_ref, wo_ref, fg_ref,
             ya_out, yb_out, dh_ref, loss_ref, dfg_ref):
        @pl.when(jnp.logical_and(pl.program_id(0) == 0, pl.program_id(1) == 0))
        def _():
            loss_ref[...] = jnp.zeros_like(loss_ref)
            dfg_ref[...] = jnp.zeros_like(dfg_ref)

        y_a = _dot(ya_ref[...], wgp_ref[...])
        y_b = _dot(yb_ref[...], wmp_ref[...])
        ya_out[...] = _bf(y_a)
        yb_out[...] = _bf(y_b)
        merged = _sigmoid(gg_ref[...].astype(F32)) * y_a + _sigmoid(gm_ref[...].astype(F32)) * y_b
        h2 = h_ref[...] + _dot(_bf(merged), wo_ref[...])
        fg = fg_ref[...]
        xh, r = _rms_fwd(h2)
        pos = pl.program_id(1) * tm + lax.broadcasted_iota(jnp.int32, (tm, 1), 0)
        err = jnp.where(pos >= X0, xh * fg - t_ref[...], 0.0)
        loss_ref[...] += 0.5 * jnp.sum(jnp.mean(err * err, axis=-1, keepdims=True), axis=0, keepdims=True)
        dy = err * (1.0 / D_MODEL)
        dx, dfg = _rms_bwd(dy, xh, r, fg)
        dh_ref[...] = dx
        dfg_ref[...] += dfg

    tok = lambda c: pl.BlockSpec((tm, D_MODEL), lambda b, i: (b * nb + i, c))
    wspec = pl.BlockSpec((D_MODEL, D_MODEL), lambda b, i: (0, 0))
    return pl.pallas_call(
        body, name="mid_fwd", grid=(bsz, nb),
        in_specs=[tok(0), tok(0), tok(C_GG // D_MODEL), tok(C_GM // D_MODEL), tok(0), tok(0),
                  wspec, wspec, wspec, pl.BlockSpec((1, D_MODEL), lambda b, i: (0, 0))],
        out_specs=[tok(0), tok(0), tok(0), pl.BlockSpec((1, LANE), lambda b, i: (0, 0)),
                   pl.BlockSpec((1, D_MODEL), lambda b, i: (0, 0))],
        out_shape=[jax.ShapeDtypeStruct((tp, D_MODEL), BF16), jax.ShapeDtypeStruct((tp, D_MODEL), BF16),
                   jax.ShapeDtypeStruct((tp, D_MODEL), F32), jax.ShapeDtypeStruct((1, LANE), F32),
                   jax.ShapeDtypeStruct((1, D_MODEL), F32)],
        compiler_params=_cp(("arbitrary", "arbitrary"), 48),
    )(ya_in, yb_in, proj, proj, hp, target, w_gp, w_mp, w_o, final_g)


def _mid_bwd(dh2, y_a, y_b, proj, ya_in, yb_in, w_o_t, w_gp_t, w_mp_t):
    tp = dh2.shape[0]
    tm = _attn_block(tp)
    nsteps = tp // tm

    def body(dh_ref, ya_ref, yb_ref, gg_ref, gm_ref, yai_ref, ybi_ref, wo_ref, wgp_ref, wmp_ref,
             dyai_ref, dybi_ref, dgate_ref, dwo_ref, dwgp_ref, dwmp_ref, a_o, a_gp, a_mp):
        @pl.when(pl.program_id(0) == 0)
        def _():
            for r in (a_o, a_gp, a_mp):
                r[...] = jnp.zeros_like(r)

        dh = _bf(dh_ref[...])
        dm = _dot(dh, wo_ref[...])
        y_a, y_b = ya_ref[...].astype(F32), yb_ref[...].astype(F32)
        sg, sm = _sigmoid(gg_ref[...].astype(F32)), _sigmoid(gm_ref[...].astype(F32))
        d_ya, d_yb = _bf(sg * dm), _bf(sm * dm)
        dgate_ref[:, :D_MODEL] = _bf(dm * y_a * sg * (1.0 - sg))
        dgate_ref[:, D_MODEL:] = _bf(dm * y_b * sm * (1.0 - sm))
        a_o[...] += _dot_tn(_bf(sg * y_a + sm * y_b), dh)
        a_gp[...] += _dot_tn(yai_ref[...], d_ya)
        a_mp[...] += _dot_tn(ybi_ref[...], d_yb)
        dyai_ref[...] = _bf(_dot(d_ya, wgp_ref[...]))
        dybi_ref[...] = _bf(_dot(d_yb, wmp_ref[...]))

        @pl.when(pl.program_id(0) == nsteps - 1)
        def _():
            pltpu.sync_copy(a_o, dwo_ref)
            pltpu.sync_copy(a_gp, dwgp_ref)
            pltpu.sync_copy(a_mp, dwmp_ref)

    tok = lambda c: pl.BlockSpec((tm, D_MODEL), lambda i: (i, c))
    wspec = pl.BlockSpec((D_MODEL, D_MODEL), lambda i: (0, 0))
    anyspec = pl.BlockSpec(memory_space=pl.ANY)
    wshape = jax.ShapeDtypeStruct((D_MODEL, D_MODEL), F32)
    return pl.pallas_call(
        body, name="mid_bwd", grid=(nsteps,),
        in_specs=[tok(0), tok(0), tok(0), tok(C_GG // D_MODEL), tok(C_GM // D_MODEL), tok(0), tok(0),
                  wspec, wspec, wspec],
        out_specs=[tok(0), tok(0), pl.BlockSpec((tm, 2 * D_MODEL), lambda i: (i, C_GG // (2 * D_MODEL))),
                   anyspec, anyspec, anyspec],
        out_shape=[jax.ShapeDtypeStruct((tp, D_MODEL), BF16)] * 2 + [jax.ShapeDtypeStruct((tp, N_EXT), BF16)]
        + [wshape] * 3,
        scratch_shapes=[pltpu.VMEM((D_MODEL, D_MODEL), F32)] * 3,
        compiler_params=_cp(("arbitrary",), 56),
    )(dh2, y_a, y_b, proj, proj, ya_in, yb_in, w_o_t, w_gp_t, w_mp_t)


MESH_ID = pl.DeviceIdType.MESH
EXCHANGE_SEMS = [pltpu.SemaphoreType.DMA((N_DEV - 1,)), pltpu.SemaphoreType.DMA((N_DEV - 1,)), pltpu.SemaphoreType.DMA]


def _my_place():
    return lax.axis_index("x"), lax.axis_index("y"), lax.axis_index("c")


def _exchange(g_ref, recv_ref, send_sems, recv_sems, local_sem, start):
    x, y, c = _my_place()
    me = 4 * x + 2 * y + c
    own = pltpu.make_async_copy(g_ref.at[me], recv_ref.at[me], local_sem)
    sends, lands = [], []
    for d in range(1, N_DEV):
        px = 1 - x if d & 4 else x
        py = 1 - y if d & 2 else y
        pc = 1 - c if d & 1 else c
        peer = 4 * px + 2 * py + pc
        for slot, group in ((me, sends),) if start else ((me, sends), (peer, lands)):
            group.append(pltpu.make_async_remote_copy(
                src_ref=g_ref.at[peer], dst_ref=recv_ref.at[slot], send_sem=send_sems.at[d - 1],
                recv_sem=recv_sems.at[d - 1], device_id=(px, py, pc), device_id_type=MESH_ID))
    if start:
        own.start()
        for cp in sends:
            cp.start()
    else:
        for cp in lands:
            cp.wait_recv()
        for cp in sends:
            cp.wait_send()
        own.wait()


def _dw_in(u, dproj, slabs):
    tp = u.shape[0]
    tm, tn = _big_tok(tp), EXT_BLOCK
    nj, ni = N_EXT // tn, tp // tm

    def body(u_ref, d_ref, g_ref, o_ref, recv_ref, send_sems, recv_sems, local_sem):
        j, i = pl.program_id(0), pl.program_id(1)

        @pl.when(jnp.logical_and(j == 0, i == 0))
        def _():
            _exchange(g_ref, recv_ref, send_sems, recv_sems, local_sem, True)

        @pl.when(i == 0)
        def _():
            o_ref[...] = jnp.zeros_like(o_ref)

        o_ref[...] += _dot_tn(u_ref[...], d_ref[...])

        @pl.when(jnp.logical_and(j == nj - 1, i == ni - 1))
        def _():
            _exchange(g_ref, recv_ref, send_sems, recv_sems, local_sem, False)

    anyspec = pl.BlockSpec(memory_space=pl.ANY)
    return pl.pallas_call(
        body, name="dw_in", grid=(nj, ni),
        in_specs=[pl.BlockSpec((tm, D_MODEL), lambda j, i: (i, 0)), pl.BlockSpec((tm, tn), lambda j, i: (i, j)), anyspec],
        out_specs=[pl.BlockSpec((D_MODEL, tn), lambda j, i: (0, j)), anyspec],
        out_shape=[jax.ShapeDtypeStruct((D_MODEL, N_EXT), F32), jax.ShapeDtypeStruct(slabs.shape, slabs.dtype)],
        scratch_shapes=EXCHANGE_SEMS,
        compiler_params=_cp(("arbitrary", "arbitrary"), 48),
    )(u, dproj, slabs)


def _dx_in(dproj, w_ext_t, hp, dh2, norm_g, slabs):
    tp = hp.shape[0]
    tm, tk = _big_tok(tp), EXT_BLOCK
    nk = N_EXT // tk
    ni = tp // tm

    def body(d_ref, w_ref, h_ref, dh_ref, g_ref, s_ref, o_ref, dg_ref, recv_ref, acc, send_sems, recv_sems, local_sem):
        k = pl.program_id(1)

        @pl.when(jnp.logical_and(pl.program_id(0) == 0, k == 0))
        def _():
            _exchange(s_ref, recv_ref, send_sems, recv_sems, local_sem, True)

        @pl.when(jnp.logical_and(pl.program_id(0) == 0, k == 0))
        def _():
            dg_ref[...] = jnp.zeros_like(dg_ref)

        @pl.when(k == 0)
        def _():
            acc[...] = jnp.zeros_like(acc)

        acc[...] += _dot(d_ref[...], w_ref[...])

        @pl.when(k == nk - 1)
        def _():
            g = g_ref[...]
            xh, r = _rms_fwd(h_ref[...])
            dx, dg = _rms_bwd(acc[...], xh, r, g)
            o_ref[...] = dh_ref[...] + dx
            dg_ref[...] += dg

        @pl.when(jnp.logical_and(pl.program_id(0) == ni - 1, k == nk - 1))
        def _():
            _exchange(s_ref, recv_ref, send_sems, recv_sems, local_sem, False)

    tok = pl.BlockSpec((tm, D_MODEL), lambda i, k: (i, 0))
    anyspec = pl.BlockSpec(memory_space=pl.ANY)
    return pl.pallas_call(
        body, name="dx_in", grid=(ni, nk),
        in_specs=[pl.BlockSpec((tm, tk), lambda i, k: (i, k)), pl.BlockSpec((tk, D_MODEL), lambda i, k: (k, 0)),
                  tok, tok, pl.BlockSpec((1, D_MODEL), lambda i, k: (0, 0)), anyspec],
        out_specs=[tok, pl.BlockSpec((1, D_MODEL), lambda i, k: (0, 0)), anyspec],
        out_shape=[jax.ShapeDtypeStruct((tp, D_MODEL), F32), jax.ShapeDtypeStruct((1, D_MODEL), F32),
                   jax.ShapeDtypeStruct(slabs.shape, slabs.dtype)],
        scratch_shapes=[pltpu.VMEM((tm, D_MODEL), F32)] + EXCHANGE_SEMS,
        compiler_params=_cp(("arbitrary", "arbitrary"), 56),
    )(dproj, w_ext_t, hp, dh2, norm_g, slabs)


def _meta_grad(dhp3):
    bsz = dhp3.shape[0]

    def body(d_ref, o_ref):
        @pl.when(pl.program_id(0) == 0)
        def _():
            o_ref[...] = jnp.zeros_like(o_ref)

        o_ref[...] += d_ref[0]

    return pl.pallas_call(
        body, name="meta_grad", grid=(bsz,),
        in_specs=[pl.BlockSpec((1, N_META, D_MODEL), lambda b: (b, FRONT // N_META, 0))],
        out_specs=pl.BlockSpec((N_META, D_MODEL), lambda b: (0, 0)),
        out_shape=jax.ShapeDtypeStruct((N_META, D_MODEL), F32),
        compiler_params=_cp(("arbitrary",)),
    )(dhp3)


W_IN_SHARD = N_IN // N_DEV


def _pad_lanes(a, width=LANE):
    return jnp.pad(a, [(0, 0)] * (a.ndim - 1) + [(0, width - a.shape[-1])])


def _rot_cols(w):
    half = w.shape[-1] // 2
    return jnp.concatenate([-w[..., half:], w[..., :half]], axis=-1)


def _unrot_cols(dw):
    half = dw.shape[-1] // 2
    return jnp.concatenate([dw[..., half:], -dw[..., :half]], axis=-1)


def _w_in_cols(shards, lo, hi):
    parts = []
    for k in range(lo // W_IN_SHARD, (hi - 1) // W_IN_SHARD + 1):
        a, b = max(lo, k * W_IN_SHARD), min(hi, (k + 1) * W_IN_SHARD)
        parts.append(shards[k][:, a - k * W_IN_SHARD:b - k * W_IN_SHARD])
    return parts[0] if len(parts) == 1 else jnp.concatenate(parts, axis=1)


def _w_in_ext(shards):
    c = lambda lo, hi: _w_in_cols(shards, lo, hi)
    kr = c(O_KR, O_MZ)
    return jnp.concatenate([
        c(O_V, O_LR), c(O_Z, O_CQ), c(O_Q, O_K), c(O_K, O_V), c(O_MZ, O_GG), c(O_GG, O_GM), c(O_GM, N_IN),
        c(O_CKV, O_KR), _pad_lanes(kr), _pad_lanes(_rot_cols(kr)), _pad_lanes(c(O_LR, O_Z)), c(O_CQ, O_CKV)], axis=1)


def _w_in_grad(dw):
    g = lambda start, width: dw[:, start:start + width]
    kr = g(C_KR, MLA_ROPE) + _unrot_cols(g(C_KROT, MLA_ROPE))
    return jnp.concatenate([
        g(C_Q, GLA_KW), g(C_K, GLA_KW), g(C_V, GLA_VW), g(C_LR, GLA_RANK), g(C_Z, GLA_VW), g(C_CQ, MLA_QR),
        g(C_CKV, MLA_KVR), kr, g(C_MZ, D_MODEL), g(C_GG, D_MODEL), g(C_GM, D_MODEL)], axis=1)


def _rope_tables(lp):
    inv = 1.0 / (ROPE_BASE ** (jnp.arange(0, MLA_ROPE, 2, dtype=F32) / MLA_ROPE))
    ang = (jnp.arange(lp, dtype=F32) - FRONT)[:, None] * inv[None, :]
    cos, sin = jnp.cos(ang), jnp.sin(ang)
    return _pad_lanes(jnp.concatenate([cos, cos], axis=1)), _pad_lanes(jnp.concatenate([sin, sin], axis=1))


def _local_step(x, loss_target, w):
    bsz, seq, _ = x.shape
    lp = X0 + seq
    tp = bsz * lp
    assert lp % TOK == 0 and lp % GLA_CHUNK == 0
    meta = jnp.broadcast_to(w["meta_tokens"][None], (bsz, N_META, D_MODEL))
    hp = jnp.concatenate([jnp.zeros((bsz, FRONT, D_MODEL), F32), meta, x], axis=1).reshape(tp, D_MODEL)
    target = jnp.pad(loss_target, ((0, 0), (X0, 0), (0, 0))).reshape(tp, D_MODEL)
    cos_t, sin_t = _rope_tables(lp)

    w_ext = _w_in_ext(w["w_in"])
    gw_pad = jnp.pad(w["gla_gate_w"], ((0, LANE - GLA_RANK), (0, 0)))
    uq = w["mla_w_uq"].reshape(MLA_QR, MLA_HEADS, MLA_QK)
    rope_w = uq[:, :, MLA_NOPE:]
    hw = MLA_HEADS * LANE
    wn = uq[:, :, :MLA_NOPE].reshape(MLA_QR, hw)
    wr = _pad_lanes(rope_w).reshape(MLA_QR, hw)
    wt = _pad_lanes(_rot_cols(rope_w)).reshape(MLA_QR, hw)
    ukv = w["mla_w_ukv"].reshape(MLA_KVR, MLA_HEADS, MLA_NOPE + MLA_DV)
    wk = ukv[:, :, :MLA_NOPE].reshape(MLA_KVR, hw)
    wv = ukv[:, :, MLA_NOPE:].reshape(MLA_KVR, hw)

    u, proj = _proj_in(hp, w["norm_g"], w_ext)
    o_raw, ya_in, s_all = _gla_fwd(proj, gw_pad, w["gla_gate_b"], w["gla_norm_g"], bsz, lp)
    qf = _q_up(proj, w["mla_q_norm_g"], wn, wr, wt, cos_t, sin_t, bsz, lp)
    kf, vf = _kv_up(proj, w["mla_kv_norm_g"], wk, wv, cos_t, sin_t, bsz, lp)
    o_b, yb_in, lse = _attn_fwd(qf, kf, vf, proj, bsz, lp)
    y_a, y_b, dh2, loss, d_final_g = _mid_fwd(ya_in, yb_in, proj, hp, target, w["gla_proj"], w["mla_proj"],
                                              w["w_out"], w["final_norm_g"], bsz, lp)
    d_ya, d_yb, dproj, d_w_out, d_gla_proj, d_mla_proj = _mid_bwd(
        dh2, y_a, y_b, proj, ya_in, yb_in, w["w_out"].T, w["gla_proj"].T, w["mla_proj"].T)
    dproj, d_gate, d_gla_norm = _gla_bwd(proj, gw_pad, w["gla_gate_b"], w["gla_norm_g"], o_raw, s_all, d_ya, dproj,
                                         bsz, lp)
    d_lr, d_gw_pad, d_gate_b = _gate_bwd(d_gate, proj, gw_pad.T)
    d_o, dproj, delta = _attn_bwd_pre(d_yb, proj, o_b, dproj, bsz, lp)
    dqf, dkf, dvf = _attn_bwd(qf, kf, vf, d_o, lse, delta, bsz, lp)
    dproj, d_wn, d_wr, d_wt, d_qn = _q_up_bwd(dqf, proj, w["mla_q_norm_g"], wn.T, wr.T, wt.T, cos_t, sin_t, dproj,
                                              bsz, lp)
    dproj, d_wk, d_wv, d_kvn = _kv_up_bwd(dkf, dvf, proj, w["mla_kv_norm_g"], wk.T, wv.T, cos_t, sin_t, d_lr, dproj,
                                          bsz, lp)

    d_rope = (d_wr.reshape(MLA_QR, MLA_HEADS, LANE)[:, :, :MLA_ROPE]
              + _unrot_cols(d_wt.reshape(MLA_QR, MLA_HEADS, LANE)[:, :, :MLA_ROPE]))
    d_uq = jnp.concatenate([d_wn.reshape(MLA_QR, MLA_HEADS, LANE), d_rope], axis=-1).reshape(MLA_QR, MLA_HEADS * MLA_QK)
    d_ukv = jnp.concatenate([d_wk.reshape(MLA_KVR, MLA_HEADS, LANE), d_wv.reshape(MLA_KVR, MLA_HEADS, LANE)],
                            axis=-1).reshape(MLA_KVR, MLA_HEADS * (MLA_NOPE + MLA_DV))
    mats = dict(gla_gate_w=d_gw_pad[:GLA_RANK], gla_proj=d_gla_proj, mla_w_uq=d_uq, mla_w_ukv=d_ukv,
                mla_proj=d_mla_proj, w_out=d_w_out)
    packed = _pad_rows(jnp.concatenate([_split8(mats[n], axis).reshape(N_DEV, -1) for n, _, axis in PACKED], axis=1),
                       PACK_ROWS)
    d_w_ext, packed_parts = _dw_in(u, dproj, _bf(packed))
    d_hp, d_norm_g, w_in_parts = _dx_in(dproj, w_ext.T, hp, dh2, w["norm_g"], _bf(_split8(_w_in_grad(d_w_ext), 1)))
    d_hp3 = d_hp.reshape(bsz, lp, D_MODEL)
    small = dict(meta_tokens=_meta_grad(d_hp3), norm_g=d_norm_g, gla_gate_b=d_gate_b, gla_norm_g=d_gla_norm,
                 mla_q_norm_g=d_qn, mla_kv_norm_g=d_kvn, final_norm_g=d_final_g)
    return loss, d_hp3[:, X0:, :], w_in_parts, packed_parts, small


PACKED = (("gla_gate_w", (GLA_RANK, GLA_KW // N_DEV), 1),
          ("gla_proj", (D_MODEL // N_DEV, D_MODEL), 0), ("mla_w_uq", (MLA_QR, MLA_HEADS * MLA_QK // N_DEV), 1),
          ("mla_w_ukv", (MLA_KVR, MLA_HEADS * (MLA_NOPE + MLA_DV) // N_DEV), 1),
          ("mla_proj", (D_MODEL // N_DEV, D_MODEL), 0), ("w_out", (D_MODEL // N_DEV, D_MODEL), 0))
REPLICATED = (("norm_g", D_MODEL), ("gla_gate_b", GLA_KW), ("gla_norm_g", GLA_DV), ("mla_q_norm_g", MLA_QR),
              ("mla_kv_norm_g", MLA_KVR), ("final_norm_g", D_MODEL))
PACK_ROWS = 3744
PACK_BLOCK = 1248
GATHER_ROWS = 3760
SMALL_ROWS = 48
LOSS_ROW = N_META + 25
W_IN_BLOCK = 128


def _all_gather(shards):
    n_arr = len(shards)

    def body(*refs):
        x_refs, out_refs = refs[:n_arr], refs[n_arr:2 * n_arr]
        send_sems, recv_sems, local_sems = refs[2 * n_arr:]
        x, y, c = _my_place()
        me, sibling = (x, y, c), (x, y, 1 - c)
        chips = [(1 - x, y), (x, 1 - y), (1 - x, 1 - y)]

        def copy(a, k, block, to, from_input=False):
            slab = out_refs[a].at[4 * block[0] + 2 * block[1] + block[2]]
            return pltpu.make_async_remote_copy(
                src_ref=x_refs[a] if from_input else slab, dst_ref=slab,
                send_sem=send_sems.at[7 * a + k], recv_sem=recv_sems.at[7 * a + k], device_id=to,
                device_id_type=MESH_ID)

        arrays = range(n_arr)
        mine = [pltpu.make_async_copy(x_refs[a], out_refs[a].at[4 * x + 2 * y + c], local_sems.at[a]) for a in arrays]
        for cp in mine:
            cp.start()
        first = [copy(a, 0, me, sibling, True) for a in arrays]
        first += [copy(a, 1 + j, me, (*chip, c), True) for j, chip in enumerate(chips) for a in arrays]
        for cp in first:
            cp.start()
        passed = []
        for j, chip in enumerate(chips):
            for a in arrays:
                copy(a, 1 + j, (*chip, c), me).wait_recv()
                passed.append(copy(a, 4 + j, (*chip, c), sibling))
                passed[-1].start()
        for a in arrays:
            copy(a, 0, sibling, me).wait_recv()
        for j, chip in enumerate(chips):
            for a in arrays:
                copy(a, 4 + j, (*chip, 1 - c), me).wait_recv()
        for cp in first + passed:
            cp.wait_send()
        for cp in mine:
            cp.wait()

    anyspec = pl.BlockSpec(memory_space=pl.ANY)
    return pl.pallas_call(
        body, name="weights_all_gather",
        out_shape=[jax.ShapeDtypeStruct((N_DEV,) + s.shape, s.dtype) for s in shards],
        in_specs=[anyspec] * n_arr, out_specs=[anyspec] * n_arr,
        scratch_shapes=[pltpu.SemaphoreType.DMA((7 * n_arr,)), pltpu.SemaphoreType.DMA((7 * n_arr,)),
                        pltpu.SemaphoreType.DMA((n_arr,))],
    )(*shards)


def _small_exchange(slabs):
    def body(g_ref, recv_ref, send_sems, recv_sems, local_sem):
        _exchange(g_ref, recv_ref, send_sems, recv_sems, local_sem, True)
        _exchange(g_ref, recv_ref, send_sems, recv_sems, local_sem, False)

    vmem = pl.BlockSpec(memory_space=pltpu.VMEM)
    return pl.pallas_call(
        body, name="small_exchange", out_shape=jax.ShapeDtypeStruct(slabs.shape, slabs.dtype),
        in_specs=[vmem], out_specs=vmem, scratch_shapes=EXCHANGE_SEMS,
    )(slabs)


def _adamw(parts, w, m, v, block_rows, name):
    rows, cols = w.shape

    def body(p_ref, w_ref, m_ref, v_ref, g_out, d_out, m_out, v_out):
        g = p_ref[0].astype(F32)
        for s in range(1, N_DEV):
            g = g + p_ref[s].astype(F32)
        m_new = ADAM_B1 * m_ref[...] + (1.0 - ADAM_B1) * g
        v_new = ADAM_B2 * v_ref[...] + (1.0 - ADAM_B2) * (g * g)
        m_hat = m_new / (1.0 - ADAM_B1 ** ADAM_STEP)
        v_hat = v_new / (1.0 - ADAM_B2 ** ADAM_STEP)
        g_out[...] = g
        d_out[...] = -ADAM_LR * (m_hat / (jnp.sqrt(v_hat) + ADAM_EPS) + ADAM_WD * w_ref[...])
        m_out[...] = m_new
        v_out[...] = v_new

    spec = pl.BlockSpec((block_rows, cols), lambda i: (i, 0))
    return pl.pallas_call(
        body, name=name, grid=(rows // block_rows,),
        in_specs=[pl.BlockSpec((N_DEV, block_rows, cols), lambda i: (0, i, 0)), spec, spec, spec],
        out_specs=[spec] * 4, out_shape=[jax.ShapeDtypeStruct((rows, cols), F32)] * 4,
        compiler_params=_cp(("parallel",), 48),
    )(parts, w, m, v)


def _pad_rows(flat, rows):
    pad = rows * LANE - flat.shape[-1]
    flat = jnp.pad(flat, [(0, 0)] * (flat.ndim - 1) + [(0, pad)])
    return flat.reshape(flat.shape[:-1] + (rows, LANE))


def _pack_shards(shards):
    return _pad_rows(jnp.concatenate([shards[n].reshape(-1) for n, _, _ in PACKED]), PACK_ROWS)


def _unpack_shards(packed):
    flat, out, off = packed.reshape(-1), {}, 0
    for n, shape, _ in PACKED:
        size = shape[0] * shape[1]
        out[n] = flat[off:off + size].reshape(shape)
        off += size
    return out


def _split8(full, axis):
    r, c = full.shape
    if axis == 0:
        return full.reshape(N_DEV, r // N_DEV, c)
    return full.reshape(r, N_DEV, c // N_DEV).transpose(1, 0, 2)


def _join8(shards, axis):
    _, r, c = shards.shape
    if axis == 0:
        return shards.reshape(N_DEV * r, c)
    return shards.transpose(1, 0, 2).reshape(r, N_DEV * c)


def _pack_small(meta_shard, vals, loss_row):
    rows = jnp.concatenate([vals[n].reshape(-1, LANE) for n, _ in REPLICATED] + [loss_row], axis=0)
    rows = jnp.pad(rows, ((0, SMALL_ROWS - N_META - rows.shape[0]), (0, 0)))
    return jnp.concatenate([meta_shard, jnp.broadcast_to(rows, meta_shard.shape[:-2] + rows.shape)], axis=-2)


def _unpack_small(packed):
    out, off = {"meta_tokens": packed[:N_META]}, N_META
    for n, size in REPLICATED:
        out[n] = packed[off:off + size // LANE].reshape(1, size)
        off += size // LANE
    return out


def kernel(x, meta_tokens, norm_g, w_in, gla_gate_w, gla_gate_b, gla_norm_g, gla_proj, mla_q_norm_g, mla_w_uq, mla_kv_norm_g, mla_w_ukv, mla_proj, w_out, final_norm_g, loss_target, m_meta_tokens, m_norm_g, m_w_in, m_gla_gate_w, m_gla_gate_b, m_gla_norm_g, m_gla_proj, m_mla_q_norm_g, m_mla_w_uq, m_mla_kv_norm_g, m_mla_w_ukv, m_mla_proj, m_w_out, m_final_norm_g, v_meta_tokens, v_norm_g, v_w_in, v_gla_gate_w, v_gla_gate_b, v_gla_norm_g, v_gla_proj, v_mla_q_norm_g, v_mla_w_uq, v_mla_kv_norm_g, v_mla_w_ukv, v_mla_proj, v_w_out, v_final_norm_g):
    given = dict(meta_tokens=meta_tokens, norm_g=norm_g, w_in=w_in, gla_gate_w=gla_gate_w, gla_gate_b=gla_gate_b,
                 gla_norm_g=gla_norm_g, gla_proj=gla_proj, mla_q_norm_g=mla_q_norm_g, mla_w_uq=mla_w_uq,
                 mla_kv_norm_g=mla_kv_norm_g, mla_w_ukv=mla_w_ukv, mla_proj=mla_proj, w_out=w_out,
                 final_norm_g=final_norm_g)
    mom_m = dict(meta_tokens=m_meta_tokens, norm_g=m_norm_g, w_in=m_w_in, gla_gate_w=m_gla_gate_w,
                 gla_gate_b=m_gla_gate_b, gla_norm_g=m_gla_norm_g, gla_proj=m_gla_proj, mla_q_norm_g=m_mla_q_norm_g,
                 mla_w_uq=m_mla_w_uq, mla_kv_norm_g=m_mla_kv_norm_g, mla_w_ukv=m_mla_w_ukv, mla_proj=m_mla_proj,
                 w_out=m_w_out, final_norm_g=m_final_norm_g)
    mom_v = dict(meta_tokens=v_meta_tokens, norm_g=v_norm_g, w_in=v_w_in, gla_gate_w=v_gla_gate_w,
                 gla_gate_b=v_gla_gate_b, gla_norm_g=v_gla_norm_g, gla_proj=v_gla_proj, mla_q_norm_g=v_mla_q_norm_g,
                 mla_w_uq=v_mla_w_uq, mla_kv_norm_g=v_mla_kv_norm_g, mla_w_ukv=v_mla_w_ukv, mla_proj=v_mla_proj,
                 w_out=v_w_out, final_norm_g=v_final_norm_g)
    shapes = {n: a.shape for n, a in given.items()}
    shard2d = {n: s for n, s, _ in PACKED}
    shard2d["w_in"] = (D_MODEL, W_IN_SHARD)
    shard2d["meta_tokens"] = (N_META, LANE)

    def as2d(tree):
        out = {n: tree[n].reshape(shard2d[n]) for n in shard2d}
        out.update({n: tree[n].reshape(1, size) for n, size in REPLICATED})
        return out

    w_loc, m_loc, v_loc = as2d(given), as2d(mom_m), as2d(mom_v)

    meta_bits = lax.bitcast_convert_type(w_loc["meta_tokens"], BF16).reshape(-1)
    flat = jnp.concatenate([w_loc[n].astype(BF16).reshape(-1) for n, _, _ in PACKED] + [meta_bits])
    w_in_all, packed_all = _all_gather([w_loc["w_in"].astype(BF16), _pad_rows(flat, GATHER_ROWS)])
    packed_all = packed_all.reshape(N_DEV, -1)
    full, off = {"w_in": w_in_all}, 0
    for n, shape, axis in PACKED:
        size = shape[0] * shape[1]
        full[n] = _join8(packed_all[:, off:off + size].reshape((N_DEV,) + shape), axis)
        off += size
    meta8 = lax.bitcast_convert_type(packed_all[:, off:off + 2 * N_META * LANE].reshape(N_DEV, N_META, LANE, 2), F32)
    full["meta_tokens"] = _join8(meta8, 1)
    for n, _ in REPLICATED:
        full[n] = w_loc[n]

    loss_part, grad_x, w_in_parts, packed_parts, small = _local_step(x, loss_target, full)
    small_all = _small_exchange(_pack_small(_split8(small["meta_tokens"], 1), small,
                                            jnp.broadcast_to(loss_part[:, :1], (1, LANE))))

    g_w, d_w, m_w, v_w = _adamw(w_in_parts, w_loc["w_in"], m_loc["w_in"], v_loc["w_in"], W_IN_BLOCK, "adamw_w_in")
    g_p, d_p, m_p, v_p = _adamw(packed_parts, _pack_shards(w_loc), _pack_shards(m_loc), _pack_shards(v_loc),
                                PACK_BLOCK, "adamw_packed")
    zero_row = jnp.zeros((1, LANE), F32)
    g_s, d_s, m_s, v_s = _adamw(small_all, *(_pack_small(t["meta_tokens"], t, zero_row) for t in (w_loc, m_loc, v_loc)),
                                SMALL_ROWS, "adamw_small")
    loss = g_s[LOSS_ROW, 0]

    order = ["meta_tokens", "norm_g", "w_in", "gla_gate_w", "gla_gate_b", "gla_norm_g", "gla_proj", "mla_q_norm_g",
             "mla_w_uq", "mla_kv_norm_g", "mla_w_ukv", "mla_proj", "w_out", "final_norm_g"]
    result = [loss, grad_x]
    for w_in_out, packed_sh, packed_sm in ((g_w, g_p, g_s), (d_w, d_p, d_s), (m_w, m_p, m_s), (v_w, v_p, v_s)):
        tree = _unpack_shards(packed_sh)
        tree.update(_unpack_small(packed_sm))
        tree["w_in"] = w_in_out
        result += [tree[n].reshape(shapes[n]) for n in order]
    return tuple(result)
```

```python
import jax
import jax.numpy as jnp
from jax import lax
from jax.experimental import pallas as pl
from jax.experimental.pallas import tpu as pltpu

F32 = jnp.float32
BF16 = jnp.bfloat16

D_MODEL = 1024
N_META = 16
EPS = 1e-6
FRONT = 48
X0 = FRONT + N_META
GLA_HEADS, GLA_DK, GLA_DV, GLA_RANK, GLA_CHUNK = 4, 128, 256, 16, 64
GLA_GATE_NORMALIZER = 16.0
GLA_KW = GLA_HEADS * GLA_DK
GLA_VW = GLA_HEADS * GLA_DV
MLA_HEADS, MLA_NOPE, MLA_ROPE, MLA_DV, MLA_QR, MLA_KVR = 8, 128, 64, 128, 256, 128
MLA_QK = MLA_NOPE + MLA_ROPE
ROPE_BASE = 10000.0
LANE = 128
QKW = 2 * LANE

C_V, C_Z, C_Q, C_K = 0, 1024, 2048, 2560
C_MZ = 3072
C_GG, C_GM = 4096, 5120
C_CKV, C_KR, C_KROT, C_LR = 6144, 6272, 6400, 6528
C_CQ = 6656
N_EXT = 6912
O_Q, O_K, O_V, O_LR, O_Z, O_CQ, O_CKV, O_KR, O_MZ, O_GG, O_GM, N_IN = (
    0, 512, 1024, 2048, 2064, 3088, 3344, 3472, 3536, 4560, 5584, 6608)

ADAM_LR, ADAM_B1, ADAM_B2, ADAM_EPS, ADAM_WD, ADAM_STEP = 0.001, 0.9, 0.999, 1e-08, 0.01, 10

N_DEV = 8
TOK = 192
ATT_BLOCK = 352
EXT_BLOCK = 1152


def _cp(sems=None, vmem_mb=None):
    kw = {}
    if sems is not None:
        kw["dimension_semantics"] = sems
    if vmem_mb is not None:
        kw["vmem_limit_bytes"] = vmem_mb * 1024 * 1024
    return pltpu.CompilerParams(**kw)


def _dot(a, b):
    return jnp.dot(a, b, preferred_element_type=F32)


def _dot_nt(a, b):
    return lax.dot_general(a, b, (((1,), (1,)), ((), ())), preferred_element_type=F32)


def _dot_tn(a, b):
    return lax.dot_general(a, b, (((0,), (0,)), ((), ())), preferred_element_type=F32)


def _sigmoid(x):
    return 1.0 / (1.0 + jnp.exp(-x))


def _bf(x):
    return x.astype(BF16)


def _big_tok(tp):
    return 4 * TOK if tp % (4 * TOK) == 0 else TOK


def _attn_block(lp):
    return ATT_BLOCK if lp % ATT_BLOCK == 0 else TOK


def _proj_in(hp, norm_g, w_ext):
    tp = hp.shape[0]
    tm, tn = _big_tok(tp), EXT_BLOCK

    def body(h_ref, g_ref, w_ref, u_ref, o_ref, u_scr):
        @pl.when(pl.program_id(1) == 0)
        def _():
            x = h_ref[...]
            r = lax.rsqrt(jnp.mean(x * x, axis=-1, keepdims=True) + EPS)
            u = _bf(x * r * g_ref[...])
            u_scr[...] = u
            u_ref[...] = u

        o_ref[...] = _bf(_dot(u_scr[...], w_ref[...]))

    return pl.pallas_call(
        body, name="proj_in", grid=(tp // tm, N_EXT // tn),
        in_specs=[pl.BlockSpec((tm, D_MODEL), lambda i, j: (i, 0)),
                  pl.BlockSpec((1, D_MODEL), lambda i, j: (0, 0)),
                  pl.BlockSpec((D_MODEL, tn), lambda i, j: (0, j))],
        out_specs=[pl.BlockSpec((tm, D_MODEL), lambda i, j: (i, 0)),
                   pl.BlockSpec((tm, tn), lambda i, j: (i, j))],
        out_shape=[jax.ShapeDtypeStruct((tp, D_MODEL), BF16), jax.ShapeDtypeStruct((tp, N_EXT), BF16)],
        scratch_shapes=[pltpu.VMEM((tm, D_MODEL), BF16)],
        compiler_params=_cp(("parallel", "arbitrary"), 48),
    )(hp, norm_g, w_ext)


GLA_GROUP = 3
GLA_ROWS = GLA_GROUP * GLA_CHUNK


def _gla_gates(q_ref, k_ref, lr_ref, gw_ref, gb_ref, rows, not_first):
    z = _dot(lr_ref[rows, :], gw_ref[...]) + gb_ref[...]
    logsig = jnp.minimum(z, 0.0) - jnp.log(1.0 + jnp.exp(-jnp.abs(z)))
    row = lax.broadcasted_iota(jnp.int32, (GLA_CHUNK, GLA_KW), 0)
    live = jnp.logical_or(not_first, row >= FRONT)
    g = jnp.where(live, logsig * (1.0 / GLA_GATE_NORMALIZER), 0.0)
    ri = lax.broadcasted_iota(jnp.int32, (GLA_CHUNK, GLA_CHUNK), 0)
    ci = lax.broadcasted_iota(jnp.int32, (GLA_CHUNK, GLA_CHUNK), 1)
    tril = ci <= ri
    b = jnp.dot(tril.astype(F32), g, precision=lax.Precision.HIGHEST, preferred_element_type=F32)
    bl = jnp.sum(jnp.where(row == GLA_CHUNK - 1, b, 0.0), axis=0, keepdims=True)
    eb, enb, elb, ebl = jnp.exp(b), jnp.exp(-b), jnp.exp(bl - b), jnp.exp(bl)
    q = q_ref[rows, :].astype(F32) * (GLA_DK ** -0.5)
    k = k_ref[rows, :].astype(F32)
    qe, ke, kl = q * eb, k * enb, k * elb
    return dict(z=z, live=live, tril=tril, row=row, eb=eb, enb=enb, elb=elb, ebl=ebl, qe=qe, ke=ke, kl=kl,
                qe_b=_bf(qe), ke_b=_bf(ke), kl_b=_bf(kl))


def _gla_in_specs(n_groups, rev):
    def rb(b, n):
        return b * n_groups + ((n_groups - 1 - n) if rev else n)

    return rb, [pl.BlockSpec((GLA_ROWS, GLA_KW), lambda b, n: (rb(b, n), C_Q // GLA_KW)),
                pl.BlockSpec((GLA_ROWS, GLA_KW), lambda b, n: (rb(b, n), C_K // GLA_KW)),
                pl.BlockSpec((GLA_ROWS, GLA_VW), lambda b, n: (rb(b, n), C_V // GLA_VW)),
                pl.BlockSpec((GLA_ROWS, GLA_VW), lambda b, n: (rb(b, n), C_Z // GLA_VW)),
                pl.BlockSpec((GLA_ROWS, LANE), lambda b, n: (rb(b, n), C_LR // LANE)),
                pl.BlockSpec((LANE, GLA_KW), lambda b, n: (0, 0)),
                pl.BlockSpec((1, GLA_KW), lambda b, n: (0, 0)),
                pl.BlockSpec((1, GLA_DV), lambda b, n: (0, 0))]


def _gla_fwd(proj, gw_pad, gate_b, gla_norm_g, bsz, lp):
    n_chunks = lp // GLA_CHUNK
    n_groups = n_chunks // GLA_GROUP
    tp = bsz * lp

    def body(q_ref, k_ref, v_ref, z_ref, lr_ref, gw_ref, gb_ref, gn_ref, oraw_ref, ya_ref, sall_ref, st_scr):
        grp = pl.program_id(1)

        @pl.when(grp == 0)
        def _():
            st_scr[...] = jnp.zeros_like(st_scr)

        chunks = [slice(j * GLA_CHUNK, (j + 1) * GLA_CHUNK) for j in range(GLA_GROUP)]
        cs = [_gla_gates(q_ref, k_ref, lr_ref, gw_ref, gb_ref, rows, True if j else grp > 0)
              for j, rows in enumerate(chunks)]
        gn = gn_ref[...]
        for h in range(GLA_HEADS):
            ks, vs = slice(h * GLA_DK, (h + 1) * GLA_DK), slice(h * GLA_DV, (h + 1) * GLA_DV)
            st = st_scr[h]
            for j, (rows, c) in enumerate(zip(chunks, cs)):
                sall_ref[0, j, h] = st
                v = v_ref[rows, vs]
                a = jnp.where(c["tril"], _dot_nt(c["qe_b"][:, ks], c["ke_b"][:, ks]), 0.0)
                o = _dot(_bf(a), v) + _dot_nt(c["qe_b"][:, ks], _bf(st))
                st = st * c["ebl"][:, ks] + _dot_tn(v, c["kl_b"][:, ks])
                oraw_ref[rows, vs] = o
                r = lax.rsqrt(jnp.mean(o * o, axis=-1, keepdims=True) + EPS)
                zg = z_ref[rows, vs].astype(F32)
                ya_ref[rows, vs] = _bf((o * r * gn) * (zg * _sigmoid(zg)))
            st_scr[h] = st

    rb, in_specs = _gla_in_specs(n_groups, False)
    return pl.pallas_call(
        body, name="gla_fwd", grid=(bsz, n_groups), in_specs=in_specs,
        out_specs=[pl.BlockSpec((GLA_ROWS, GLA_VW), lambda b, n: (rb(b, n), 0)),
                   pl.BlockSpec((GLA_ROWS, GLA_VW), lambda b, n: (rb(b, n), 0)),
                   pl.BlockSpec((1, GLA_GROUP, GLA_HEADS, GLA_DV, GLA_DK), lambda b, n: (b, n, 0, 0, 0))],
        out_shape=[jax.ShapeDtypeStruct((tp, GLA_VW), F32), jax.ShapeDtypeStruct((tp, GLA_VW), BF16),
                   jax.ShapeDtypeStruct((bsz, n_chunks, GLA_HEADS, GLA_DV, GLA_DK), F32)],
        scratch_shapes=[pltpu.VMEM((GLA_HEADS, GLA_DV, GLA_DK), F32)],
        compiler_params=_cp(("parallel", "arbitrary")),
    )(proj, proj, proj, proj, proj, gw_pad, gate_b, gla_norm_g)


def _gla_bwd(proj, gw_pad, gate_b, gla_norm_g, o_raw, s_all, d_ya, dproj, bsz, lp):
    n_chunks = lp // GLA_CHUNK
    n_groups = n_chunks // GLA_GROUP
    tp = bsz * lp

    def body(q_ref, k_ref, v_ref, z_ref, lr_ref, gw_ref, gb_ref, gn_ref, o_ref, s_ref, dya_ref, _,
             dp_ref, dz_ref, dgn_ref, dst_scr):
        dv_ref, dzg_ref = dp_ref.at[:, C_V:C_V + GLA_VW], dp_ref.at[:, C_Z:C_Z + GLA_VW]

        @pl.when(jnp.logical_and(pl.program_id(0) == 0, pl.program_id(1) == 0))
        def _():
            dgn_ref[...] = jnp.zeros_like(dgn_ref)

        @pl.when(pl.program_id(1) == 0)
        def _():
            dst_scr[...] = jnp.zeros_like(dst_scr)

        grp = n_groups - 1 - pl.program_id(1)
        chunks = [slice(j * GLA_CHUNK, (j + 1) * GLA_CHUNK) for j in range(GLA_GROUP)]
        cs = [_gla_gates(q_ref, k_ref, lr_ref, gw_ref, gb_ref, rows, True if j else grp > 0)
              for j, rows in enumerate(chunks)]
        gn = gn_ref[...]
        dgn = jnp.zeros((1, GLA_DV), F32)
        dqe_h, dke_h, dkl_h, dbl_h = ([[None] * GLA_HEADS for _ in chunks] for _ in range(4))
        for h in range(GLA_HEADS):
            ks, vs = slice(h * GLA_DK, (h + 1) * GLA_DK), slice(h * GLA_DV, (h + 1) * GLA_DV)
            dst = dst_scr[h]
            for j in reversed(range(GLA_GROUP)):
                rows, c = chunks[j], cs[j]
                v = v_ref[rows, vs]
                st = s_ref[0, j, h]
                o = o_ref[rows, vs]
                r = lax.rsqrt(jnp.mean(o * o, axis=-1, keepdims=True) + EPS)
                xh = o * r
                zg = z_ref[rows, vs].astype(F32)
                sg = _sigmoid(zg)
                dy = dya_ref[rows, vs].astype(F32)
                dzg_ref[rows, vs] = _bf(dy * (xh * gn) * (sg * (1.0 + zg * (1.0 - sg))))
                t = dy * (zg * sg)
                dgn += jnp.sum(t * xh, axis=0, keepdims=True)
                dxh = t * gn
                do_b = _bf(r * (dxh - xh * jnp.mean(dxh * xh, axis=-1, keepdims=True)))
                qe_b, ke_b, kl_b, dst_b = c["qe_b"][:, ks], c["ke_b"][:, ks], c["kl_b"][:, ks], _bf(dst)
                a = jnp.where(c["tril"], _dot_nt(qe_b, ke_b), 0.0)
                da_b = _bf(jnp.where(c["tril"], _dot_nt(do_b, v), 0.0))
                dqe_h[j][h] = _dot(da_b, ke_b) + _dot(do_b, _bf(st))
                dke_h[j][h] = _dot_tn(da_b, qe_b)
                dkl = _dot(v, dst_b)
                dkl_h[j][h] = dkl
                dv_ref[rows, vs] = _bf(_dot_tn(_bf(a), do_b) + _dot_nt(kl_b, dst_b))
                ddecay = jnp.sum(dst * st, axis=0, keepdims=True)
                dbl_h[j][h] = jnp.sum(dkl * c["kl"][:, ks], axis=0, keepdims=True) + ddecay * c["ebl"][:, ks]
                dst = dst * c["ebl"][:, ks] + _dot_tn(do_b, qe_b)
            dst_scr[h] = dst
        dgn_ref[...] += dgn
        ri = lax.broadcasted_iota(jnp.int32, (GLA_CHUNK, GLA_CHUNK), 0)
        ci = lax.broadcasted_iota(jnp.int32, (GLA_CHUNK, GLA_CHUNK), 1)
        triu = (ci >= ri).astype(F32)
        for j, (rows, c) in enumerate(zip(chunks, cs)):
            dqe, dke, dkl, dbl = (jnp.concatenate(p[j], axis=1) for p in (dqe_h, dke_h, dkl_h, dbl_h))
            db = dqe * c["qe"] - dke * c["ke"] - dkl * c["kl"] + jnp.where(c["row"] == GLA_CHUNK - 1, dbl, 0.0)
            dg = jnp.dot(triu, db, precision=lax.Precision.HIGHEST, preferred_element_type=F32)
            dg = jnp.where(c["live"], dg, 0.0)
            dz_ref[rows, :] = dg * (1.0 / GLA_GATE_NORMALIZER) * _sigmoid(-c["z"])
            dp_ref[rows, C_Q:C_Q + GLA_KW] = _bf(dqe * c["eb"] * (GLA_DK ** -0.5))
            dp_ref[rows, C_K:C_K + GLA_KW] = _bf(dke * c["enb"] + dkl * c["elb"])

    rb, in_specs = _gla_in_specs(n_groups, True)
    wide = pl.BlockSpec((GLA_ROWS, GLA_VW), lambda b, n: (rb(b, n), 0))
    group = C_MZ
    return pl.pallas_call(
        body, name="gla_bwd", grid=(bsz, n_groups),
        in_specs=in_specs + [wide, pl.BlockSpec((1, GLA_GROUP, GLA_HEADS, GLA_DV, GLA_DK),
                                                lambda b, n: (b, n_groups - 1 - n, 0, 0, 0)), wide,
                             pl.BlockSpec(memory_space=pl.ANY)],
        out_specs=[pl.BlockSpec((GLA_ROWS, group), lambda b, n: (rb(b, n), 0)),
                   pl.BlockSpec((GLA_ROWS, GLA_KW), lambda b, n: (rb(b, n), 0)),
                   pl.BlockSpec((1, GLA_DV), lambda b, n: (0, 0))],
        out_shape=[jax.ShapeDtypeStruct((tp, N_EXT), BF16), jax.ShapeDtypeStruct((tp, GLA_KW), F32),
                   jax.ShapeDtypeStruct((1, GLA_DV), F32)],
        input_output_aliases={11: 0},
        scratch_shapes=[pltpu.VMEM((GLA_HEADS, GLA_DV, GLA_DK), F32)],
        compiler_params=_cp(("arbitrary", "arbitrary")),
    )(proj, proj, proj, proj, proj, gw_pad, gate_b, gla_norm_g, o_raw, s_all, d_ya, dproj)


def _gate_bwd(dz, proj, gw_pad):
    tp = dz.shape[0]
    tm = _big_tok(tp)

    def body(dz_ref, lr_ref, gw_ref, dlr_ref, dgw_ref, dgb_ref):
        @pl.when(pl.program_id(0) == 0)
        def _():
            dgw_ref[...] = jnp.zeros_like(dgw_ref)
            dgb_ref[...] = jnp.zeros_like(dgb_ref)

        dz = dz_ref[...]
        dz_b = _bf(dz)
        dlr_ref[...] = _bf(_dot_nt(dz_b, gw_ref[...]))
        dgw_ref[...] += _dot_tn(lr_ref[...], dz_b)
        dgb_ref[...] += jnp.sum(dz, axis=0, keepdims=True)

    return pl.pallas_call(
        body, name="gate_bwd", grid=(tp // tm,),
        in_specs=[pl.BlockSpec((tm, GLA_KW), lambda i: (i, 0)),
                  pl.BlockSpec((tm, LANE), lambda i: (i, C_LR // LANE)),
                  pl.BlockSpec((LANE, GLA_KW), lambda i: (0, 0))],
        out_specs=[pl.BlockSpec((tm, LANE), lambda i: (i, 0)),
                   pl.BlockSpec((LANE, GLA_KW), lambda i: (0, 0)),
                   pl.BlockSpec((1, GLA_KW), lambda i: (0, 0))],
        out_shape=[jax.ShapeDtypeStruct((tp, LANE), BF16), jax.ShapeDtypeStruct((LANE, GLA_KW), F32),
                   jax.ShapeDtypeStruct((1, GLA_KW), F32)],
        compiler_params=_cp(("arbitrary",)),
    )(dz, proj, gw_pad)


def _rms_fwd(x):
    r = lax.rsqrt(jnp.mean(x * x, axis=-1, keepdims=True) + EPS)
    return x * r, r


def _rms_bwd(dy, xh, r, g):
    dxh = dy * g
    dx = r * (dxh - xh * jnp.mean(dxh * xh, axis=-1, keepdims=True))
    return dx, jnp.sum(dy * xh, axis=0, keepdims=True)


def _q_up(proj, q_norm_g, wn, wr, wt, cos_t, sin_t, bsz, lp):
    tp = bsz * lp
    nb = lp // TOK

    def body(cq_ref, g_ref, wn_ref, wr_ref, wt_ref, cos_ref, sin_ref, q_ref):
        xh, _ = _rms_fwd(cq_ref[...].astype(F32))
        cqn = _bf(xh * g_ref[...])
        nope = _dot(cqn, wn_ref[...])
        rope = _dot(cqn, wr_ref[...])
        rot = _dot(cqn, wt_ref[...])
        cos, sin = cos_ref[...], sin_ref[...]
        one = (lax.broadcasted_iota(jnp.int32, (TOK, LANE), 1) == BIAS_LANE).astype(F32)
        for h in range(MLA_HEADS):
            sl = slice(h * LANE, (h + 1) * LANE)
            q_ref[:, h * QKW:h * QKW + LANE] = _bf(nope[:, sl])
            q_ref[:, h * QKW + LANE:(h + 1) * QKW] = _bf(rope[:, sl] * cos + rot[:, sl] * sin + one)

    wspec = pl.BlockSpec((MLA_QR, MLA_HEADS * LANE), lambda b, i: (0, 0))
    tspec = pl.BlockSpec((TOK, LANE), lambda b, i: (i, 0))
    return pl.pallas_call(
        body, name="mla_q_up", grid=(bsz, nb),
        in_specs=[pl.BlockSpec((TOK, MLA_QR), lambda b, i: (b * nb + i, C_CQ // MLA_QR)),
                  pl.BlockSpec((1, MLA_QR), lambda b, i: (0, 0)), wspec, wspec, wspec, tspec, tspec],
        out_specs=pl.BlockSpec((TOK, MLA_HEADS * QKW), lambda b, i: (b * nb + i, 0)),
        out_shape=jax.ShapeDtypeStruct((tp, MLA_HEADS * QKW), BF16),
        compiler_params=_cp(("parallel", "parallel")),
    )(proj, q_norm_g, wn, wr, wt, cos_t, sin_t)


def _kv_up(proj, kv_norm_g, wk, wv, cos_t, sin_t, bsz, lp):
    tp = bsz * lp
    nb = lp // TOK

    def body(ckv_ref, kr_ref, krot_ref, g_ref, wk_ref, wv_ref, cos_ref, sin_ref, k_ref, v_ref):
        xh, _ = _rms_fwd(ckv_ref[...].astype(F32))
        cn = _bf(xh * g_ref[...])
        kn = _dot(cn, wk_ref[...])
        v_ref[...] = _bf(_dot(cn, wv_ref[...]))
        pos = pl.program_id(1) * TOK + lax.broadcasted_iota(jnp.int32, (TOK, LANE), 0)
        lane = lax.broadcasted_iota(jnp.int32, (TOK, LANE), 1)
        bias = jnp.where(jnp.logical_and(lane == BIAS_LANE, pos < FRONT), KEY_BIAS, 0.0)
        kr = _bf(kr_ref[...].astype(F32) * cos_ref[...] + krot_ref[...].astype(F32) * sin_ref[...] + bias)
        for h in range(MLA_HEADS):
            k_ref[:, h * QKW:h * QKW + LANE] = _bf(kn[:, h * LANE:(h + 1) * LANE])
            k_ref[:, h * QKW + LANE:(h + 1) * QKW] = kr

    wspec = pl.BlockSpec((MLA_KVR, MLA_HEADS * LANE), lambda b, i: (0, 0))
    tspec = pl.BlockSpec((TOK, LANE), lambda b, i: (i, 0))
    return pl.pallas_call(
        body, name="mla_kv_up", grid=(bsz, nb),
        in_specs=[pl.BlockSpec((TOK, LANE), lambda b, i: (b * nb + i, C_CKV // LANE)),
                  pl.BlockSpec((TOK, LANE), lambda b, i: (b * nb + i, C_KR // LANE)),
                  pl.BlockSpec((TOK, LANE), lambda b, i: (b * nb + i, C_KROT // LANE)),
                  pl.BlockSpec((1, MLA_KVR), lambda b, i: (0, 0)), wspec, wspec, tspec, tspec],
        out_specs=[pl.BlockSpec((TOK, MLA_HEADS * QKW), lambda b, i: (b * nb + i, 0)),
                   pl.BlockSpec((TOK, MLA_HEADS * LANE), lambda b, i: (b * nb + i, 0))],
        out_shape=[jax.ShapeDtypeStruct((tp, MLA_HEADS * QKW), BF16),
                   jax.ShapeDtypeStruct((tp, MLA_HEADS * LANE), BF16)],
        compiler_params=_cp(("parallel", "parallel")),
    )(proj, proj, proj, kv_norm_g, wk, wv, cos_t, sin_t)


ATT_SCALE = MLA_QK ** -0.5


ATT_ROWS = 256
KEY_BIAS = -1e30
BIAS_LANE = MLA_ROPE
NEG = 2 * KEY_BIAS


def _attn_blocks(lp):
    return [(0, X0)] + [(r0, min(ATT_ROWS, lp - r0)) for r0 in range(X0, lp, ATT_ROWS)]


def _causal(tq):
    return lax.broadcasted_iota(jnp.int32, (tq, tq), 1) <= lax.broadcasted_iota(jnp.int32, (tq, tq), 0)


def _attn_fwd(qf, kf, vf, proj, bsz, lp):
    tp = bsz * lp

    def body(q_ref, k_ref, v_ref, mz_ref, ob_ref, yb_ref, lse_ref):
        for r0, tq in _attn_blocks(lp):
            rows = slice(r0, r0 + tq)
            q = q_ref[rows, :]
            s_own = jnp.where(_causal(tq), _dot_nt(q, k_ref[rows, :]), NEG)
            m = jnp.max(s_own, axis=-1, keepdims=True)
            if r0:
                s_before = _dot_nt(q, k_ref[0:r0, :])
                m = jnp.maximum(m, jnp.max(s_before, axis=-1, keepdims=True))
            p = jnp.exp((s_own - m) * ATT_SCALE)
            l = jnp.sum(p, axis=-1, keepdims=True)
            acc = _dot(_bf(p), v_ref[rows, :])
            if r0:
                p = jnp.exp((s_before - m) * ATT_SCALE)
                l += jnp.sum(p, axis=-1, keepdims=True)
                acc += _dot(_bf(p), v_ref[0:r0, :])
            o = acc / l
            ob_ref[rows, :] = _bf(o)
            mz = mz_ref[rows, :].astype(F32)
            yb_ref[rows, :] = _bf(o * (mz * _sigmoid(mz)))
            lse_ref[0, 0, rows, :] = jnp.broadcast_to(m * ATT_SCALE + jnp.log(l), (tq, LANE))

    head = lambda off: pl.BlockSpec((lp, MLA_DV), lambda b, h: (b, off + h))
    return pl.pallas_call(
        body, name="mla_attn_fwd", grid=(bsz, MLA_HEADS),
        in_specs=[pl.BlockSpec((lp, QKW), lambda b, h: (b, h)), pl.BlockSpec((lp, QKW), lambda b, h: (b, h)),
                  head(0), head(C_MZ // MLA_DV)],
        out_specs=[head(0), head(0), pl.BlockSpec((1, 1, lp, LANE), lambda b, h: (b, h, 0, 0))],
        out_shape=[jax.ShapeDtypeStruct((tp, MLA_HEADS * MLA_DV), BF16),
                   jax.ShapeDtypeStruct((tp, MLA_HEADS * MLA_DV), BF16),
                   jax.ShapeDtypeStruct((bsz, MLA_HEADS, lp, LANE), F32)],
        compiler_params=_cp(("parallel", "parallel"), 56),
    )(qf, kf, vf, proj)


def _attn_bwd_pre(d_yb, proj, o_b, dproj, bsz, lp):
    tp = bsz * lp
    nb = lp // TOK
    w = MLA_HEADS * MLA_DV

    def body(dy_ref, mz_ref, o_ref, _, do_ref, dmz_ref, dl_ref):
        dy = dy_ref[...].astype(F32)
        mz = mz_ref[...].astype(F32)
        o = o_ref[...].astype(F32)
        s = _sigmoid(mz)
        do = _bf(dy * (mz * s))
        do_ref[...] = do
        dmz_ref[...] = _bf(dy * o * (s * (1.0 + mz * (1.0 - s))))
        prod = do.astype(F32) * o
        for h in range(MLA_HEADS):
            dl = jnp.sum(prod[:, h * MLA_DV:(h + 1) * MLA_DV], axis=-1, keepdims=True)
            dl_ref[0, h] = jnp.broadcast_to(dl, (TOK, LANE))

    return pl.pallas_call(
        body, name="mla_attn_bwd_pre", grid=(bsz, nb),
        in_specs=[pl.BlockSpec((TOK, w), lambda b, i: (b * nb + i, 0)),
                  pl.BlockSpec((TOK, w), lambda b, i: (b * nb + i, C_MZ // w)),
                  pl.BlockSpec((TOK, w), lambda b, i: (b * nb + i, 0)), pl.BlockSpec(memory_space=pl.ANY)],
        out_specs=[pl.BlockSpec((TOK, w), lambda b, i: (b * nb + i, 0)),
                   pl.BlockSpec((TOK, w), lambda b, i: (b * nb + i, C_MZ // w)),
                   pl.BlockSpec((1, MLA_HEADS, TOK, LANE), lambda b, i: (b, 0, i, 0))],
        out_shape=[jax.ShapeDtypeStruct((tp, w), BF16), jax.ShapeDtypeStruct((tp, N_EXT), BF16),
                   jax.ShapeDtypeStruct((bsz, MLA_HEADS, lp, LANE), F32)],
        input_output_aliases={3: 1},
        compiler_params=_cp(("parallel", "parallel")),
    )(d_yb, proj, o_b, dproj)


def _attn_bwd(qf, kf, vf, d_o, lse, delta, bsz, lp):
    tp = bsz * lp

    def body(q_ref, k_ref, v_ref, do_ref, lse_ref, dl_ref, dq_ref, dk_ref, dv_ref, dk_acc, dv_acc):
        dk_acc[...] = jnp.zeros_like(dk_acc)
        dv_acc[...] = jnp.zeros_like(dv_acc)
        for r0, tq in _attn_blocks(lp):
            rows = slice(r0, r0 + tq)
            q, do = q_ref[rows, :], do_ref[rows, :]
            lse_c, dl_c = lse_ref[0, 0, rows, :][:, :1], dl_ref[0, 0, rows, :][:, :1]
            dq = jnp.zeros((tq, QKW), F32)
            for keys in (rows, slice(0, r0)) if r0 else (rows,):
                k, v = k_ref[keys, :], v_ref[keys, :]
                p = jnp.exp(_dot_nt(q, k) * ATT_SCALE - lse_c)
                if keys is rows:
                    p = jnp.where(_causal(tq), p, 0.0)
                ds = _bf(p * (_dot_nt(do, v) - dl_c) * ATT_SCALE)
                dq += _dot(ds, k)
                dk_acc[keys, :] += _dot_tn(ds, q)
                dv_acc[keys, :] += _dot_tn(_bf(p), do)
            dq_ref[rows, :] = _bf(dq)
        dk_ref[...] = _bf(dk_acc[...])
        dv_ref[...] = _bf(dv_acc[...])

    wide = pl.BlockSpec((lp, QKW), lambda b, h: (b, h))
    narrow = pl.BlockSpec((lp, MLA_DV), lambda b, h: (b, h))
    stat = pl.BlockSpec((1, 1, lp, LANE), lambda b, h: (b, h, 0, 0))
    return pl.pallas_call(
        body, name="mla_attn_bwd", grid=(bsz, MLA_HEADS),
        in_specs=[wide, wide, narrow, narrow, stat, stat], out_specs=[wide, wide, narrow],
        out_shape=[jax.ShapeDtypeStruct((tp, MLA_HEADS * QKW), BF16), jax.ShapeDtypeStruct((tp, MLA_HEADS * QKW), BF16),
                   jax.ShapeDtypeStruct((tp, MLA_HEADS * MLA_DV), BF16)],
        scratch_shapes=[pltpu.VMEM((lp, QKW), F32), pltpu.VMEM((lp, MLA_DV), F32)],
        compiler_params=_cp(("parallel", "parallel"), 56),
    )(qf, kf, vf, d_o, lse, delta)


def _q_up_bwd(dqf, proj, q_norm_g, wn, wr, wt, cos_t, sin_t, dproj, bsz, lp):
    tp = bsz * lp
    nb = lp // TOK
    hw = MLA_HEADS * LANE

    def body(dq_ref, cq_ref, g_ref, wn_ref, wr_ref, wt_ref, cos_ref, sin_ref, _,
             dcq_ref, dwn_ref, dwr_ref, dwt_ref, dg_ref):
        @pl.when(jnp.logical_and(pl.program_id(0) == 0, pl.program_id(1) == 0))
        def _():
            for r in (dwn_ref, dwr_ref, dwt_ref, dg_ref):
                r[...] = jnp.zeros_like(r)

        g = g_ref[...]
        xh, r = _rms_fwd(cq_ref[...].astype(F32))
        cqn = _bf(xh * g)
        cos, sin = cos_ref[...], sin_ref[...]
        dcqn = jnp.zeros((TOK, MLA_QR), F32)
        for h in range(MLA_HEADS):
            sl = slice(h * LANE, (h + 1) * LANE)
            dn = dq_ref[:, h * QKW:h * QKW + LANE]
            dr = dq_ref[:, h * QKW + LANE:(h + 1) * QKW].astype(F32)
            dr_c, dr_s = _bf(dr * cos), _bf(dr * sin)
            dcqn += _dot_nt(dn, wn_ref[:, sl]) + _dot_nt(dr_c, wr_ref[:, sl]) + _dot_nt(dr_s, wt_ref[:, sl])
            dwn_ref[:, sl] += _dot_tn(cqn, dn)
            dwr_ref[:, sl] += _dot_tn(cqn, dr_c)
            dwt_ref[:, sl] += _dot_tn(cqn, dr_s)
        dx, dg = _rms_bwd(dcqn, xh, r, g)
        dcq_ref[...] = _bf(dx)
        dg_ref[...] += dg

    aspec = pl.BlockSpec((MLA_QR, hw), lambda b, i: (0, 0))
    tspec = pl.BlockSpec((TOK, LANE), lambda b, i: (i, 0))
    return pl.pallas_call(
        body, name="mla_q_up_bwd", grid=(bsz, nb),
        in_specs=[pl.BlockSpec((TOK, MLA_HEADS * QKW), lambda b, i: (b * nb + i, 0)),
                  pl.BlockSpec((TOK, MLA_QR), lambda b, i: (b * nb + i, C_CQ // MLA_QR)),
                  pl.BlockSpec((1, MLA_QR), lambda b, i: (0, 0)), aspec, aspec, aspec, tspec, tspec,
                  pl.BlockSpec(memory_space=pl.ANY)],
        out_specs=[pl.BlockSpec((TOK, MLA_QR), lambda b, i: (b * nb + i, C_CQ // MLA_QR)), aspec, aspec, aspec,
                   pl.BlockSpec((1, MLA_QR), lambda b, i: (0, 0))],
        out_shape=[jax.ShapeDtypeStruct((tp, N_EXT), BF16)] + [jax.ShapeDtypeStruct((MLA_QR, hw), F32)] * 3
        + [jax.ShapeDtypeStruct((1, MLA_QR), F32)],
        input_output_aliases={8: 0},
        compiler_params=_cp(("arbitrary", "arbitrary")),
    )(dqf, proj, q_norm_g, wn, wr, wt, cos_t, sin_t, dproj)


def _kv_up_bwd(dkf, dvf, proj, kv_norm_g, wk, wv, cos_t, sin_t, d_lr, dproj, bsz, lp):
    tp = bsz * lp
    nb = lp // TOK
    hw = MLA_HEADS * LANE

    def body(dk_ref, dv_ref, ckv_ref, g_ref, wk_ref, wv_ref, cos_ref, sin_ref, dlr_ref, _,
             dp_ref, dwk_ref, dwv_ref, dg_ref):
        dckv_ref, dkr_ref, dkrot_ref = (dp_ref.at[:, j * LANE:(j + 1) * LANE] for j in range(3))
        dp_ref[:, 3 * LANE:] = dlr_ref[...]
        @pl.when(jnp.logical_and(pl.program_id(0) == 0, pl.program_id(1) == 0))
        def _():
            for r in (dwk_ref, dwv_ref, dg_ref):
                r[...] = jnp.zeros_like(r)

        g = g_ref[...]
        xh, r = _rms_fwd(ckv_ref[...].astype(F32))
        cn = _bf(xh * g)
        dv = dv_ref[...]
        dcn = _dot_nt(dv, wv_ref[...])
        dwv_ref[...] += _dot_tn(cn, dv)
        drope = jnp.zeros((TOK, LANE), F32)
        for h in range(MLA_HEADS):
            sl = slice(h * LANE, (h + 1) * LANE)
            dn = dk_ref[:, h * QKW:h * QKW + LANE]
            drope += dk_ref[:, h * QKW + LANE:(h + 1) * QKW].astype(F32)
            dcn += _dot_nt(dn, wk_ref[:, sl])
            dwk_ref[:, sl] += _dot_tn(cn, dn)
        dkr_ref[...] = _bf(drope * cos_ref[...])
        dkrot_ref[...] = _bf(drope * sin_ref[...])
        dx, dg = _rms_bwd(dcn, xh, r, g)
        dckv_ref[...] = _bf(dx)
        dg_ref[...] += dg

    aspec = pl.BlockSpec((MLA_KVR, hw), lambda b, i: (0, 0))
    tspec = pl.BlockSpec((TOK, LANE), lambda b, i: (i, 0))
    ospec = pl.BlockSpec((TOK, LANE), lambda b, i: (b * nb + i, 0))
    return pl.pallas_call(
        body, name="mla_kv_up_bwd", grid=(bsz, nb),
        in_specs=[pl.BlockSpec((TOK, MLA_HEADS * QKW), lambda b, i: (b * nb + i, 0)),
                  pl.BlockSpec((TOK, hw), lambda b, i: (b * nb + i, 0)),
                  pl.BlockSpec((TOK, LANE), lambda b, i: (b * nb + i, C_CKV // LANE)),
                  pl.BlockSpec((1, MLA_KVR), lambda b, i: (0, 0)), aspec, aspec, tspec, tspec, ospec,
                  pl.BlockSpec(memory_space=pl.ANY)],
        out_specs=[pl.BlockSpec((TOK, 4 * LANE), lambda b, i: (b * nb + i, C_CKV // (4 * LANE))), aspec, aspec,
                   pl.BlockSpec((1, MLA_KVR), lambda b, i: (0, 0))],
        out_shape=[jax.ShapeDtypeStruct((tp, N_EXT), BF16)] + [jax.ShapeDtypeStruct((MLA_KVR, hw), F32)] * 2
        + [jax.ShapeDtypeStruct((1, MLA_KVR), F32)],
        input_output_aliases={9: 0},
        compiler_params=_cp(("arbitrary", "arbitrary")),
    )(dkf, dvf, proj, kv_norm_g, wk, wv, cos_t, sin_t, d_lr, dproj)


def _mid_fwd(ya_in, yb_in, proj, hp, target, w_gp, w_mp, w_o, final_g, bsz, lp):
    tp = bsz * lp
    tm = _attn_block(lp)
    nb = lp // tm

    def body(ya_ref, yb_ref, gg_ref, gm_ref, h_ref, t_ref, wgp_ref, wmp_ref, wo_ref, fg_ref,
             ya_out, yb_out, dh_ref, loss_ref, dfg_ref):
        @pl.when(jnp.logical_and(pl.program_id(0) == 0, pl.program_id(1) == 0))
        def _():
            loss_ref[...] = jnp.zeros_like(loss_ref)
            dfg_ref[...] = jnp.zeros_like(dfg_ref)

        y_a = _dot(ya_ref[...], wgp_ref[...])
        y_b = _dot(yb_ref[...], wmp_ref[...])
        ya_out[...] = _bf(y_a)
        yb_out[...] = _bf(y_b)
        merged = _sigmoid(gg_ref[...].astype(F32)) * y_a + _sigmoid(gm_ref[...].astype(F32)) * y_b
        h2 = h_ref[...] + _dot(_bf(merged), wo_ref[...])
        fg = fg_ref[...]
        xh, r = _rms_fwd(h2)
        pos = pl.program_id(1) * tm + lax.broadcasted_iota(jnp.int32, (tm, 1), 0)
        err = jnp.where(pos >= X0, xh * fg - t_ref[...], 0.0)
        loss_ref[...] += 0.5 * jnp.sum(jnp.mean(err * err, axis=-1, keepdims=True), axis=0, keepdims=True)
        dy = err * (1.0 / D_MODEL)
        dx, dfg = _rms_bwd(dy, xh, r, fg)
        dh_ref[...] = dx
        dfg_ref[...] += dfg

    tok = lambda c: pl.BlockSpec((tm, D_MODEL), lambda b, i: (b * nb + i, c))
    wspec = pl.BlockSpec((D_MODEL, D_MODEL), lambda b, i: (0, 0))
    return pl.pallas_call(
        body, name="mid_fwd", grid=(bsz, nb),
        in_specs=[tok(0), tok(0), tok(C_GG // D_MODEL), tok(C_GM // D_MODEL), tok(0), tok(0),
                  wspec, wspec, wspec, pl.BlockSpec((1, D_MODEL), lambda b, i: (0, 0))],
        out_specs=[tok(0), tok(0), tok(0), pl.BlockSpec((1, LANE), lambda b, i: (0, 0)),
                   pl.BlockSpec((1, D_MODEL), lambda b, i: (0, 0))],
        out_shape=[jax.ShapeDtypeStruct((tp, D_MODEL), BF16), jax.ShapeDtypeStruct((tp, D_MODEL), BF16),
                   jax.ShapeDtypeStruct((tp, D_MODEL), F32), jax.ShapeDtypeStruct((1, LANE), F32),
                   jax.ShapeDtypeStruct((1, D_MODEL), F32)],
        compiler_params=_cp(("arbitrary", "arbitrary"), 48),
    )(ya_in, yb_in, proj, proj, hp, target, w_gp, w_mp, w_o, final_g)


def _mid_bwd(dh2, y_a, y_b, proj, ya_in, yb_in, w_o, w_gp, w_mp):
    tp = dh2.shape[0]
    tm = _attn_block(tp)
    nsteps = tp // tm

    def body(dh_ref, ya_ref, yb_ref, gg_ref, gm_ref, yai_ref, ybi_ref, wo_ref, wgp_ref, wmp_ref,
             dyai_ref, dybi_ref, dgate_ref, dwo_ref, dwgp_ref, dwmp_ref, a_o, a_gp, a_mp):
        @pl.when(pl.program_id(0) == 0)
        def _():
            for r in (a_o, a_gp, a_mp):
                r[...] = jnp.zeros_like(r)

        dh = _bf(dh_ref[...])
        dm = _dot_nt(dh, wo_ref[...])
        y_a, y_b = ya_ref[...].astype(F32), yb_ref[...].astype(F32)
        sg, sm = _sigmoid(gg_ref[...].astype(F32)), _sigmoid(gm_ref[...].astype(F32))
        d_ya, d_yb = _bf(sg * dm), _bf(sm * dm)
        dgate_ref[:, :D_MODEL] = _bf(dm * y_a * sg * (1.0 - sg))
        dgate_ref[:, D_MODEL:] = _bf(dm * y_b * sm * (1.0 - sm))
        a_o[...] += _dot_tn(_bf(sg * y_a + sm * y_b), dh)
        a_gp[...] += _dot_tn(yai_ref[...], d_ya)
        a_mp[...] += _dot_tn(ybi_ref[...], d_yb)
        dyai_ref[...] = _bf(_dot_nt(d_ya, wgp_ref[...]))
        dybi_ref[...] = _bf(_dot_nt(d_yb, wmp_ref[...]))

        @pl.when(pl.program_id(0) == nsteps - 1)
        def _():
            pltpu.sync_copy(a_o, dwo_ref)
            pltpu.sync_copy(a_gp, dwgp_ref)
            pltpu.sync_copy(a_mp, dwmp_ref)

    tok = lambda c: pl.BlockSpec((tm, D_MODEL), lambda i: (i, c))
    wspec = pl.BlockSpec((D_MODEL, D_MODEL), lambda i: (0, 0))
    anyspec = pl.BlockSpec(memory_space=pl.ANY)
    wshape = jax.ShapeDtypeStruct((D_MODEL, D_MODEL), F32)
    return pl.pallas_call(
        body, name="mid_bwd", grid=(nsteps,),
        in_specs=[tok(0), tok(0), tok(0), tok(C_GG // D_MODEL), tok(C_GM // D_MODEL), tok(0), tok(0),
                  wspec, wspec, wspec],
        out_specs=[tok(0), tok(0), pl.BlockSpec((tm, 2 * D_MODEL), lambda i: (i, C_GG // (2 * D_MODEL))),
                   anyspec, anyspec, anyspec],
        out_shape=[jax.ShapeDtypeStruct((tp, D_MODEL), BF16)] * 2 + [jax.ShapeDtypeStruct((tp, N_EXT), BF16)]
        + [wshape] * 3,
        scratch_shapes=[pltpu.VMEM((D_MODEL, D_MODEL), F32)] * 3,
        compiler_params=_cp(("arbitrary",), 56),
    )(dh2, y_a, y_b, proj, proj, ya_in, yb_in, w_o, w_gp, w_mp)


MESH_ID = pl.DeviceIdType.MESH
EXCHANGE_SEMS = [pltpu.SemaphoreType.DMA((N_DEV - 1,)), pltpu.SemaphoreType.DMA((N_DEV - 1,)), pltpu.SemaphoreType.DMA]


def _my_place():
    return lax.axis_index("x"), lax.axis_index("y"), lax.axis_index("c")


def _exchange(g_ref, recv_ref, send_sems, recv_sems, local_sem, start):
    x, y, c = _my_place()
    me = 4 * x + 2 * y + c
    own = pltpu.make_async_copy(g_ref.at[me], recv_ref.at[me], local_sem)
    sends, lands = [], []
    for d in range(1, N_DEV):
        px = 1 - x if d & 4 else x
        py = 1 - y if d & 2 else y
        pc = 1 - c if d & 1 else c
        peer = 4 * px + 2 * py + pc
        for slot, group in ((me, sends),) if start else ((me, sends), (peer, lands)):
            group.append(pltpu.make_async_remote_copy(
                src_ref=g_ref.at[peer], dst_ref=recv_ref.at[slot], send_sem=send_sems.at[d - 1],
                recv_sem=recv_sems.at[d - 1], device_id=(px, py, pc), device_id_type=MESH_ID))
    if start:
        own.start()
        for cp in sends:
            cp.start()
    else:
        for cp in lands:
            cp.wait_recv()
        for cp in sends:
            cp.wait_send()
        own.wait()


def _dw_in(u, dproj, slabs):
    tp = u.shape[0]
    tm, tn = _big_tok(tp), EXT_BLOCK
    nj, ni = N_EXT // tn, tp // tm

    def body(u_ref, d_ref, g_ref, o_ref, recv_ref, send_sems, recv_sems, local_sem):
        j, i = pl.program_id(0), pl.program_id(1)

        @pl.when(jnp.logical_and(j == 0, i == 0))
        def _():
            _exchange(g_ref, recv_ref, send_sems, recv_sems, local_sem, True)

        @pl.when(i == 0)
        def _():
            o_ref[...] = jnp.zeros_like(o_ref)

        o_ref[...] += _dot_tn(u_ref[...], d_ref[...])

        @pl.when(jnp.logical_and(j == nj - 1, i == ni - 1))
        def _():
            _exchange(g_ref, recv_ref, send_sems, recv_sems, local_sem, False)

    anyspec = pl.BlockSpec(memory_space=pl.ANY)
    return pl.pallas_call(
        body, name="dw_in", grid=(nj, ni),
        in_specs=[pl.BlockSpec((tm, D_MODEL), lambda j, i: (i, 0)), pl.BlockSpec((tm, tn), lambda j, i: (i, j)), anyspec],
        out_specs=[pl.BlockSpec((D_MODEL, tn), lambda j, i: (0, j)), anyspec],
        out_shape=[jax.ShapeDtypeStruct((D_MODEL, N_EXT), F32), jax.ShapeDtypeStruct(slabs.shape, slabs.dtype)],
        scratch_shapes=EXCHANGE_SEMS,
        compiler_params=_cp(("arbitrary", "arbitrary"), 48),
    )(u, dproj, slabs)


def _dx_in(dproj, w_ext, hp, dh2, norm_g, slabs):
    tp = hp.shape[0]
    tm, tk = _big_tok(tp), EXT_BLOCK
    nk = N_EXT // tk
    ni = tp // tm

    def body(d_ref, w_ref, h_ref, dh_ref, g_ref, s_ref, o_ref, dg_ref, recv_ref, acc, send_sems, recv_sems, local_sem):
        k = pl.program_id(1)

        @pl.when(jnp.logical_and(pl.program_id(0) == 0, k == 0))
        def _():
            _exchange(s_ref, recv_ref, send_sems, recv_sems, local_sem, True)

        @pl.when(jnp.logical_and(pl.program_id(0) == 0, k == 0))
        def _():
            dg_ref[...] = jnp.zeros_like(dg_ref)

        @pl.when(k == 0)
        def _():
            acc[...] = jnp.zeros_like(acc)

        acc[...] += _dot_nt(d_ref[...], w_ref[...])

        @pl.when(k == nk - 1)
        def _():
            g = g_ref[...]
            xh, r = _rms_fwd(h_ref[...])
            dx, dg = _rms_bwd(acc[...], xh, r, g)
            o_ref[...] = dh_ref[...] + dx
            dg_ref[...] += dg

        @pl.when(jnp.logical_and(pl.program_id(0) == ni - 1, k == nk - 1))
        def _():
            _exchange(s_ref, recv_ref, send_sems, recv_sems, local_sem, False)

    tok = pl.BlockSpec((tm, D_MODEL), lambda i, k: (i, 0))
    anyspec = pl.BlockSpec(memory_space=pl.ANY)
    return pl.pallas_call(
        body, name="dx_in", grid=(ni, nk),
        in_specs=[pl.BlockSpec((tm, tk), lambda i, k: (i, k)), pl.BlockSpec((D_MODEL, tk), lambda i, k: (0, k)),
                  tok, tok, pl.BlockSpec((1, D_MODEL), lambda i, k: (0, 0)), anyspec],
        out_specs=[tok, pl.BlockSpec((1, D_MODEL), lambda i, k: (0, 0)), anyspec],
        out_shape=[jax.ShapeDtypeStruct((tp, D_MODEL), F32), jax.ShapeDtypeStruct((1, D_MODEL), F32),
                   jax.ShapeDtypeStruct(slabs.shape, slabs.dtype)],
        scratch_shapes=[pltpu.VMEM((tm, D_MODEL), F32)] + EXCHANGE_SEMS,
        compiler_params=_cp(("arbitrary", "arbitrary"), 56),
    )(dproj, w_ext, hp, dh2, norm_g, slabs)


def _meta_grad(dhp3):
    bsz = dhp3.shape[0]

    def body(d_ref, o_ref):
        @pl.when(pl.program_id(0) == 0)
        def _():
            o_ref[...] = jnp.zeros_like(o_ref)

        o_ref[...] += d_ref[0]

    return pl.pallas_call(
        body, name="meta_grad", grid=(bsz,),
        in_specs=[pl.BlockSpec((1, N_META, D_MODEL), lambda b: (b, FRONT // N_META, 0))],
        out_specs=pl.BlockSpec((N_META, D_MODEL), lambda b: (0, 0)),
        out_shape=jax.ShapeDtypeStruct((N_META, D_MODEL), F32),
        compiler_params=_cp(("arbitrary",)),
    )(dhp3)


W_IN_SHARD = N_IN // N_DEV


def _pad_lanes(a, width=LANE):
    return jnp.pad(a, [(0, 0)] * (a.ndim - 1) + [(0, width - a.shape[-1])])


def _rot_cols(w):
    half = w.shape[-1] // 2
    return jnp.concatenate([-w[..., half:], w[..., :half]], axis=-1)


def _unrot_cols(dw):
    half = dw.shape[-1] // 2
    return jnp.concatenate([dw[..., half:], -dw[..., :half]], axis=-1)


def _w_in_cols(shards, lo, hi):
    parts = []
    for k in range(lo // W_IN_SHARD, (hi - 1) // W_IN_SHARD + 1):
        a, b = max(lo, k * W_IN_SHARD), min(hi, (k + 1) * W_IN_SHARD)
        parts.append(shards[k][:, a - k * W_IN_SHARD:b - k * W_IN_SHARD])
    return parts[0] if len(parts) == 1 else jnp.concatenate(parts, axis=1)


def _w_in_ext(shards):
    c = lambda lo, hi: _w_in_cols(shards, lo, hi)
    kr = c(O_KR, O_MZ)
    return jnp.concatenate([
        c(O_V, O_LR), c(O_Z, O_CQ), c(O_Q, O_K), c(O_K, O_V), c(O_MZ, O_GG), c(O_GG, O_GM), c(O_GM, N_IN),
        c(O_CKV, O_KR), _pad_lanes(kr), _pad_lanes(_rot_cols(kr)), _pad_lanes(c(O_LR, O_Z)), c(O_CQ, O_CKV)], axis=1)


def _w_in_grad(dw):
    g = lambda start, width: dw[:, start:start + width]
    kr = g(C_KR, MLA_ROPE) + _unrot_cols(g(C_KROT, MLA_ROPE))
    return jnp.concatenate([
        g(C_Q, GLA_KW), g(C_K, GLA_KW), g(C_V, GLA_VW), g(C_LR, GLA_RANK), g(C_Z, GLA_VW), g(C_CQ, MLA_QR),
        g(C_CKV, MLA_KVR), kr, g(C_MZ, D_MODEL), g(C_GG, D_MODEL), g(C_GM, D_MODEL)], axis=1)


def _rope_tables(lp):
    inv = 1.0 / (ROPE_BASE ** (jnp.arange(0, MLA_ROPE, 2, dtype=F32) / MLA_ROPE))
    ang = (jnp.arange(lp, dtype=F32) - FRONT)[:, None] * inv[None, :]
    cos, sin = jnp.cos(ang), jnp.sin(ang)
    return _pad_lanes(jnp.concatenate([cos, cos], axis=1)), _pad_lanes(jnp.concatenate([sin, sin], axis=1))


def _local_step(x, loss_target, w):
    bsz, seq, _ = x.shape
    lp = X0 + seq
    tp = bsz * lp
    assert lp % TOK == 0 and lp % GLA_CHUNK == 0
    meta = jnp.broadcast_to(w["meta_tokens"][None], (bsz, N_META, D_MODEL))
    hp = jnp.concatenate([jnp.zeros((bsz, FRONT, D_MODEL), F32), meta, x], axis=1).reshape(tp, D_MODEL)
    target = jnp.pad(loss_target, ((0, 0), (X0, 0), (0, 0))).reshape(tp, D_MODEL)
    cos_t, sin_t = _rope_tables(lp)

    w_ext = _w_in_ext(w["w_in"])
    gw_pad = jnp.pad(w["gla_gate_w"], ((0, LANE - GLA_RANK), (0, 0)))
    uq = w["mla_w_uq"].reshape(MLA_QR, MLA_HEADS, MLA_QK)
    rope_w = uq[:, :, MLA_NOPE:]
    hw = MLA_HEADS * LANE
    wn = uq[:, :, :MLA_NOPE].reshape(MLA_QR, hw)
    wr = _pad_lanes(rope_w).reshape(MLA_QR, hw)
    wt = _pad_lanes(_rot_cols(rope_w)).reshape(MLA_QR, hw)
    ukv = w["mla_w_ukv"].reshape(MLA_KVR, MLA_HEADS, MLA_NOPE + MLA_DV)
    wk = ukv[:, :, :MLA_NOPE].reshape(MLA_KVR, hw)
    wv = ukv[:, :, MLA_NOPE:].reshape(MLA_KVR, hw)

    u, proj = _proj_in(hp, w["norm_g"], w_ext)
    o_raw, ya_in, s_all = _gla_fwd(proj, gw_pad, w["gla_gate_b"], w["gla_norm_g"], bsz, lp)
    qf = _q_up(proj, w["mla_q_norm_g"], wn, wr, wt, cos_t, sin_t, bsz, lp)
    kf, vf = _kv_up(proj, w["mla_kv_norm_g"], wk, wv, cos_t, sin_t, bsz, lp)
    o_b, yb_in, lse = _attn_fwd(qf, kf, vf, proj, bsz, lp)
    y_a, y_b, dh2, loss, d_final_g = _mid_fwd(ya_in, yb_in, proj, hp, target, w["gla_proj"], w["mla_proj"],
                                              w["w_out"], w["final_norm_g"], bsz, lp)
    d_ya, d_yb, dproj, d_w_out, d_gla_proj, d_mla_proj = _mid_bwd(
        dh2, y_a, y_b, proj, ya_in, yb_in, w["w_out"], w["gla_proj"], w["mla_proj"])
    dproj, d_gate, d_gla_norm = _gla_bwd(proj, gw_pad, w["gla_gate_b"], w["gla_norm_g"], o_raw, s_all, d_ya, dproj,
                                         bsz, lp)
    d_lr, d_gw_pad, d_gate_b = _gate_bwd(d_gate, proj, gw_pad)
    d_o, dproj, delta = _attn_bwd_pre(d_yb, proj, o_b, dproj, bsz, lp)
    dqf, dkf, dvf = _attn_bwd(qf, kf, vf, d_o, lse, delta, bsz, lp)
    dproj, d_wn, d_wr, d_wt, d_qn = _q_up_bwd(dqf, proj, w["mla_q_norm_g"], wn, wr, wt, cos_t, sin_t, dproj,
                                              bsz, lp)
    dproj, d_wk, d_wv, d_kvn = _kv_up_bwd(dkf, dvf, proj, w["mla_kv_norm_g"], wk, wv, cos_t, sin_t, d_lr, dproj,
                                          bsz, lp)

    d_rope = (d_wr.reshape(MLA_QR, MLA_HEADS, LANE)[:, :, :MLA_ROPE]
              + _unrot_cols(d_wt.reshape(MLA_QR, MLA_HEADS, LANE)[:, :, :MLA_ROPE]))
    d_uq = jnp.concatenate([d_wn.reshape(MLA_QR, MLA_HEADS, LANE), d_rope], axis=-1).reshape(MLA_QR, MLA_HEADS * MLA_QK)
    d_ukv = jnp.concatenate([d_wk.reshape(MLA_KVR, MLA_HEADS, LANE), d_wv.reshape(MLA_KVR, MLA_HEADS, LANE)],
                            axis=-1).reshape(MLA_KVR, MLA_HEADS * (MLA_NOPE + MLA_DV))
    mats = dict(gla_gate_w=d_gw_pad[:GLA_RANK], gla_proj=d_gla_proj, mla_w_uq=d_uq, mla_w_ukv=d_ukv,
                mla_proj=d_mla_proj, w_out=d_w_out)
    packed = _pad_rows(jnp.concatenate([_split8(mats[n], axis).reshape(N_DEV, -1) for n, _, axis in PACKED], axis=1),
                       PACK_ROWS)
    d_w_ext, packed_parts = _dw_in(u, dproj, _bf(packed))
    d_hp, d_norm_g, w_in_parts = _dx_in(dproj, w_ext, hp, dh2, w["norm_g"], _bf(_split8(_w_in_grad(d_w_ext), 1)))
    d_hp3 = d_hp.reshape(bsz, lp, D_MODEL)
    small = dict(meta_tokens=_meta_grad(d_hp3), norm_g=d_norm_g, gla_gate_b=d_gate_b, gla_norm_g=d_gla_norm,
                 mla_q_norm_g=d_qn, mla_kv_norm_g=d_kvn, final_norm_g=d_final_g)
    return loss, d_hp3[:, X0:, :], w_in_parts, packed_parts, small


PACKED = (("gla_gate_w", (GLA_RANK, GLA_KW // N_DEV), 1),
          ("gla_proj", (D_MODEL // N_DEV, D_MODEL), 0), ("mla_w_uq", (MLA_QR, MLA_HEADS * MLA_QK // N_DEV), 1),
          ("mla_w_ukv", (MLA_KVR, MLA_HEADS * (MLA_NOPE + MLA_DV) // N_DEV), 1),
          ("mla_proj", (D_MODEL // N_DEV, D_MODEL), 0), ("w_out", (D_MODEL // N_DEV, D_MODEL), 0))
REPLICATED = (("norm_g", D_MODEL), ("gla_gate_b", GLA_KW), ("gla_norm_g", GLA_DV), ("mla_q_norm_g", MLA_QR),
              ("mla_kv_norm_g", MLA_KVR), ("final_norm_g", D_MODEL))
PACK_ROWS = 3744
PACK_BLOCK = 1248
GATHER_ROWS = 3760
SMALL_ROWS = 48
LOSS_ROW = N_META + 25
W_IN_BLOCK = 128


def _all_gather(shards):
    n_arr = len(shards)

    def body(*refs):
        x_refs, out_refs = refs[:n_arr], refs[n_arr:2 * n_arr]
        send_sems, recv_sems, local_sems = refs[2 * n_arr:]
        x, y, c = _my_place()
        me, sibling = (x, y, c), (x, y, 1 - c)
        chips = [(1 - x, y), (x, 1 - y), (1 - x, 1 - y)]

        def copy(a, k, block, to, from_input=False):
            slab = out_refs[a].at[4 * block[0] + 2 * block[1] + block[2]]
            return pltpu.make_async_remote_copy(
                src_ref=x_refs[a] if from_input else slab, dst_ref=slab,
                send_sem=send_sems.at[7 * a + k], recv_sem=recv_sems.at[7 * a + k], device_id=to,
                device_id_type=MESH_ID)

        arrays = range(n_arr)
        mine = [pltpu.make_async_copy(x_refs[a], out_refs[a].at[4 * x + 2 * y + c], local_sems.at[a]) for a in arrays]
        for cp in mine:
            cp.start()
        first = [copy(a, 0, me, sibling, True) for a in arrays]
        first += [copy(a, 1 + j, me, (*chip, c), True) for j, chip in enumerate(chips) for a in arrays]
        for cp in first:
            cp.start()
        passed = []
        for j, chip in enumerate(chips):
            for a in arrays:
                copy(a, 1 + j, (*chip, c), me).wait_recv()
                passed.append(copy(a, 4 + j, (*chip, c), sibling))
                passed[-1].start()
        for a in arrays:
            copy(a, 0, sibling, me).wait_recv()
        for j, chip in enumerate(chips):
            for a in arrays:
                copy(a, 4 + j, (*chip, 1 - c), me).wait_recv()
        for cp in first + passed:
            cp.wait_send()
        for cp in mine:
            cp.wait()

    anyspec = pl.BlockSpec(memory_space=pl.ANY)
    return pl.pallas_call(
        body, name="weights_all_gather",
        out_shape=[jax.ShapeDtypeStruct((N_DEV,) + s.shape, s.dtype) for s in shards],
        in_specs=[anyspec] * n_arr, out_specs=[anyspec] * n_arr,
        scratch_shapes=[pltpu.SemaphoreType.DMA((7 * n_arr,)), pltpu.SemaphoreType.DMA((7 * n_arr,)),
                        pltpu.SemaphoreType.DMA((n_arr,))],
    )(*shards)


def _small_exchange(slabs):
    def body(g_ref, recv_ref, send_sems, recv_sems, local_sem):
        _exchange(g_ref, recv_ref, send_sems, recv_sems, local_sem, True)
        _exchange(g_ref, recv_ref, send_sems, recv_sems, local_sem, False)

    vmem = pl.BlockSpec(memory_space=pltpu.VMEM)
    return pl.pallas_call(
        body, name="small_exchange", out_shape=jax.ShapeDtypeStruct(slabs.shape, slabs.dtype),
        in_specs=[vmem], out_specs=vmem, scratch_shapes=EXCHANGE_SEMS,
    )(slabs)


def _adamw(parts, w, m, v, block_rows, name):
    rows, cols = w.shape

    def body(p_ref, w_ref, m_ref, v_ref, g_out, d_out, m_out, v_out):
        g = p_ref[0].astype(F32)
        for s in range(1, N_DEV):
            g = g + p_ref[s].astype(F32)
        m_new = ADAM_B1 * m_ref[...] + (1.0 - ADAM_B1) * g
        v_new = ADAM_B2 * v_ref[...] + (1.0 - ADAM_B2) * (g * g)
        m_hat = m_new / (1.0 - ADAM_B1 ** ADAM_STEP)
        v_hat = v_new / (1.0 - ADAM_B2 ** ADAM_STEP)
        g_out[...] = g
        d_out[...] = -ADAM_LR * (m_hat / (jnp.sqrt(v_hat) + ADAM_EPS) + ADAM_WD * w_ref[...])
        m_out[...] = m_new
        v_out[...] = v_new

    spec = pl.BlockSpec((block_rows, cols), lambda i: (i, 0))
    return pl.pallas_call(
        body, name=name, grid=(rows // block_rows,),
        in_specs=[pl.BlockSpec((N_DEV, block_rows, cols), lambda i: (0, i, 0)), spec, spec, spec],
        out_specs=[spec] * 4, out_shape=[jax.ShapeDtypeStruct((rows, cols), F32)] * 4,
        compiler_params=_cp(("parallel",), 48),
    )(parts, w, m, v)


def _pad_rows(flat, rows):
    pad = rows * LANE - flat.shape[-1]
    flat = jnp.pad(flat, [(0, 0)] * (flat.ndim - 1) + [(0, pad)])
    return flat.reshape(flat.shape[:-1] + (rows, LANE))


def _pack_shards(shards):
    return _pad_rows(jnp.concatenate([shards[n].reshape(-1) for n, _, _ in PACKED]), PACK_ROWS)


def _unpack_shards(packed):
    flat, out, off = packed.reshape(-1), {}, 0
    for n, shape, _ in PACKED:
        size = shape[0] * shape[1]
        out[n] = flat[off:off + size].reshape(shape)
        off += size
    return out


def _split8(full, axis):
    r, c = full.shape
    if axis == 0:
        return full.reshape(N_DEV, r // N_DEV, c)
    return full.reshape(r, N_DEV, c // N_DEV).transpose(1, 0, 2)


def _join8(shards, axis):
    _, r, c = shards.shape
    if axis == 0:
        return shards.reshape(N_DEV * r, c)
    return shards.transpose(1, 0, 2).reshape(r, N_DEV * c)


def _pack_small(meta_shard, vals, loss_row):
    rows = jnp.concatenate([vals[n].reshape(-1, LANE) for n, _ in REPLICATED] + [loss_row], axis=0)
    rows = jnp.pad(rows, ((0, SMALL_ROWS - N_META - rows.shape[0]), (0, 0)))
    return jnp.concatenate([meta_shard, jnp.broadcast_to(rows, meta_shard.shape[:-2] + rows.shape)], axis=-2)


def _unpack_small(packed):
    out, off = {"meta_tokens": packed[:N_META]}, N_META
    for n, size in REPLICATED:
        out[n] = packed[off:off + size // LANE].reshape(1, size)
        off += size // LANE
    return out


def kernel(x, meta_tokens, norm_g, w_in, gla_gate_w, gla_gate_b, gla_norm_g, gla_proj, mla_q_norm_g, mla_w_uq, mla_kv_norm_g, mla_w_ukv, mla_proj, w_out, final_norm_g, loss_target, m_meta_tokens, m_norm_g, m_w_in, m_gla_gate_w, m_gla_gate_b, m_gla_norm_g, m_gla_proj, m_mla_q_norm_g, m_mla_w_uq, m_mla_kv_norm_g, m_mla_w_ukv, m_mla_proj, m_w_out, m_final_norm_g, v_meta_tokens, v_norm_g, v_w_in, v_gla_gate_w, v_gla_gate_b, v_gla_norm_g, v_gla_proj, v_mla_q_norm_g, v_mla_w_uq, v_mla_kv_norm_g, v_mla_w_ukv, v_mla_proj, v_w_out, v_final_norm_g):
    given = dict(meta_tokens=meta_tokens, norm_g=norm_g, w_in=w_in, gla_gate_w=gla_gate_w, gla_gate_b=gla_gate_b,
                 gla_norm_g=gla_norm_g, gla_proj=gla_proj, mla_q_norm_g=mla_q_norm_g, mla_w_uq=mla_w_uq,
                 mla_kv_norm_g=mla_kv_norm_g, mla_w_ukv=mla_w_ukv, mla_proj=mla_proj, w_out=w_out,
                 final_norm_g=final_norm_g)
    mom_m = dict(meta_tokens=m_meta_tokens, norm_g=m_norm_g, w_in=m_w_in, gla_gate_w=m_gla_gate_w,
                 gla_gate_b=m_gla_gate_b, gla_norm_g=m_gla_norm_g, gla_proj=m_gla_proj, mla_q_norm_g=m_mla_q_norm_g,
                 mla_w_uq=m_mla_w_uq, mla_kv_norm_g=m_mla_kv_norm_g, mla_w_ukv=m_mla_w_ukv, mla_proj=m_mla_proj,
                 w_out=m_w_out, final_norm_g=m_final_norm_g)
    mom_v = dict(meta_tokens=v_meta_tokens, norm_g=v_norm_g, w_in=v_w_in, gla_gate_w=v_gla_gate_w,
                 gla_gate_b=v_gla_gate_b, gla_norm_g=v_gla_norm_g, gla_proj=v_gla_proj, mla_q_norm_g=v_mla_q_norm_g,
                 mla_w_uq=v_mla_w_uq, mla_kv_norm_g=v_mla_kv_norm_g, mla_w_ukv=v_mla_w_ukv, mla_proj=v_mla_proj,
                 w_out=v_w_out, final_norm_g=v_final_norm_g)
    shapes = {n: a.shape for n, a in given.items()}
    shard2d = {n: s for n, s, _ in PACKED}
    shard2d["w_in"] = (D_MODEL, W_IN_SHARD)
    shard2d["meta_tokens"] = (N_META, LANE)

    def as2d(tree):
        out = {n: tree[n].reshape(shard2d[n]) for n in shard2d}
        out.update({n: tree[n].reshape(1, size) for n, size in REPLICATED})
        return out

    w_loc, m_loc, v_loc = as2d(given), as2d(mom_m), as2d(mom_v)

    meta_bits = lax.bitcast_convert_type(w_loc["meta_tokens"], BF16).reshape(-1)
    flat = jnp.concatenate([w_loc[n].astype(BF16).reshape(-1) for n, _, _ in PACKED] + [meta_bits])
    w_in_all, packed_all = _all_gather([w_loc["w_in"].astype(BF16), _pad_rows(flat, GATHER_ROWS)])
    packed_all = packed_all.reshape(N_DEV, -1)
    full, off = {"w_in": w_in_all}, 0
    for n, shape, axis in PACKED:
        size = shape[0] * shape[1]
        full[n] = _join8(packed_all[:, off:off + size].reshape((N_DEV,) + shape), axis)
        off += size
    meta8 = lax.bitcast_convert_type(packed_all[:, off:off + 2 * N_META * LANE].reshape(N_DEV, N_META, LANE, 2), F32)
    full["meta_tokens"] = _join8(meta8, 1)
    for n, _ in REPLICATED:
        full[n] = w_loc[n]

    loss_part, grad_x, w_in_parts, packed_parts, small = _local_step(x, loss_target, full)
    small_all = _small_exchange(_pack_small(_split8(small["meta_tokens"], 1), small,
                                            jnp.broadcast_to(loss_part[:, :1], (1, LANE))))

    g_w, d_w, m_w, v_w = _adamw(w_in_parts, w_loc["w_in"], m_loc["w_in"], v_loc["w_in"], W_IN_BLOCK, "adamw_w_in")
    g_p, d_p, m_p, v_p = _adamw(packed_parts, _pack_shards(w_loc), _pack_shards(m_loc), _pack_shards(v_loc),
                                PACK_BLOCK, "adamw_packed")
    zero_row = jnp.zeros((1, LANE), F32)
    g_s, d_s, m_s, v_s = _adamw(small_all, *(_pack_small(t["meta_tokens"], t, zero_row) for t in (w_loc, m_loc, v_loc)),
                                SMALL_ROWS, "adamw_small")
    loss = g_s[LOSS_ROW, 0]

    order = ["meta_tokens", "norm_g", "w_in", "gla_gate_w", "gla_gate_b", "gla_norm_g", "gla_proj", "mla_q_norm_g",
             "mla_w_uq", "mla_kv_norm_g", "mla_w_ukv", "mla_proj", "w_out", "final_norm_g"]
    result = [loss, grad_x]
    for w_in_out, packed_sh, packed_sm in ((g_w, g_p, g_s), (d_w, d_p, d_s), (m_w, m_p, m_s), (v_w, v_p, v_s)):
        tree = _unpack_shards(packed_sh)
        tree.update(_unpack_small(packed_sm))
        tree["w_in"] = w_in_out
        result += [tree[n].reshape(shapes[n]) for n in order]
    return tuple(result)
```

```python
import jax
import jax.numpy as jnp
from jax import lax
from jax.experimental import pallas as pl
from jax.experimental.pallas import tpu as pltpu

F32 = jnp.float32
BF16 = jnp.bfloat16

D_MODEL = 1024
N_META = 16
EPS = 1e-6
FRONT = 48
X0 = FRONT + N_META
GLA_HEADS, GLA_DK, GLA_DV, GLA_RANK, GLA_CHUNK = 4, 128, 256, 16, 64
GLA_GATE_NORMALIZER = 16.0
GLA_KW = GLA_HEADS * GLA_DK
GLA_VW = GLA_HEADS * GLA_DV
MLA_HEADS, MLA_NOPE, MLA_ROPE, MLA_DV, MLA_QR, MLA_KVR = 8, 128, 64, 128, 256, 128
MLA_QK = MLA_NOPE + MLA_ROPE
ROPE_BASE = 10000.0
LANE = 128
QKW = 2 * LANE

C_V, C_Z, C_Q, C_K = 0, 1024, 2048, 2560
C_MZ = 3072
C_GG, C_GM = 4096, 5120
C_CKV, C_KR, C_KROT, C_LR = 6144, 6272, 6400, 6528
C_CQ = 6656
N_EXT = 6912
O_Q, O_K, O_V, O_LR, O_Z, O_CQ, O_CKV, O_KR, O_MZ, O_GG, O_GM, N_IN = (
    0, 512, 1024, 2048, 2064, 3088, 3344, 3472, 3536, 4560, 5584, 6608)

ADAM_LR, ADAM_B1, ADAM_B2, ADAM_EPS, ADAM_WD, ADAM_STEP = 0.001, 0.9, 0.999, 1e-08, 0.01, 10

N_DEV = 8
TOK = 192
ATT_BLOCK = 352
EXT_BLOCK = 1152


def _cp(sems=None, vmem_mb=None):
    kw = {}
    if sems is not None:
        kw["dimension_semantics"] = sems
    if vmem_mb is not None:
        kw["vmem_limit_bytes"] = vmem_mb * 1024 * 1024
    return pltpu.CompilerParams(**kw)


def _dot(a, b):
    return jnp.dot(a, b, preferred_element_type=F32)


def _dot_nt(a, b):
    return lax.dot_general(a, b, (((1,), (1,)), ((), ())), preferred_element_type=F32)


def _dot_tn(a, b):
    return lax.dot_general(a, b, (((0,), (0,)), ((), ())), preferred_element_type=F32)


def _sigmoid(x):
    return 1.0 / (1.0 + jnp.exp(-x))


def _bf(x):
    return x.astype(BF16)


def _big_tok(tp):
    return 4 * TOK if tp % (4 * TOK) == 0 else TOK


def _attn_block(lp):
    return ATT_BLOCK if lp % ATT_BLOCK == 0 else TOK


def _proj_in(hp, norm_g, w_ext):
    tp = hp.shape[0]
    tm, tn = _big_tok(tp), EXT_BLOCK

    def body(h_ref, g_ref, w_ref, u_ref, o_ref, u_scr):
        @pl.when(pl.program_id(1) == 0)
        def _():
            x = h_ref[...]
            r = lax.rsqrt(jnp.mean(x * x, axis=-1, keepdims=True) + EPS)
            u = _bf(x * r * g_ref[...])
            u_scr[...] = u
            u_ref[...] = u

        o_ref[...] = _bf(_dot(u_scr[...], w_ref[...]))

    return pl.pallas_call(
        body, name="proj_in", grid=(tp // tm, N_EXT // tn),
        in_specs=[pl.BlockSpec((tm, D_MODEL), lambda i, j: (i, 0)),
                  pl.BlockSpec((1, D_MODEL), lambda i, j: (0, 0)),
                  pl.BlockSpec((D_MODEL, tn), lambda i, j: (0, j))],
        out_specs=[pl.BlockSpec((tm, D_MODEL), lambda i, j: (i, 0)),
                   pl.BlockSpec((tm, tn), lambda i, j: (i, j))],
        out_shape=[jax.ShapeDtypeStruct((tp, D_MODEL), BF16), jax.ShapeDtypeStruct((tp, N_EXT), BF16)],
        scratch_shapes=[pltpu.VMEM((tm, D_MODEL), BF16)],
        compiler_params=_cp(("parallel", "arbitrary"), 48),
    )(hp, norm_g, w_ext)


GLA_GROUP = 3
GLA_ROWS = GLA_GROUP * GLA_CHUNK


def _gla_gates(q_ref, k_ref, lr_ref, gw_ref, gb_ref, rows, not_first):
    z = _dot(lr_ref[rows, :], gw_ref[...]) + gb_ref[...]
    logsig = jnp.minimum(z, 0.0) - jnp.log(1.0 + jnp.exp(-jnp.abs(z)))
    row = lax.broadcasted_iota(jnp.int32, (GLA_CHUNK, GLA_KW), 0)
    live = jnp.logical_or(not_first, row >= FRONT)
    g = jnp.where(live, logsig * (1.0 / GLA_GATE_NORMALIZER), 0.0)
    ri = lax.broadcasted_iota(jnp.int32, (GLA_CHUNK, GLA_CHUNK), 0)
    ci = lax.broadcasted_iota(jnp.int32, (GLA_CHUNK, GLA_CHUNK), 1)
    tril = ci <= ri
    b = jnp.dot(tril.astype(F32), g, precision=lax.Precision.HIGHEST, preferred_element_type=F32)
    bl = jnp.sum(jnp.where(row == GLA_CHUNK - 1, b, 0.0), axis=0, keepdims=True)
    eb, enb, elb, ebl = jnp.exp(b), jnp.exp(-b), jnp.exp(bl - b), jnp.exp(bl)
    q = q_ref[rows, :].astype(F32) * (GLA_DK ** -0.5)
    k = k_ref[rows, :].astype(F32)
    qe, ke, kl = q * eb, k * enb, k * elb
    return dict(z=z, live=live, tril=tril, row=row, eb=eb, enb=enb, elb=elb, ebl=ebl, qe=qe, ke=ke, kl=kl,
                qe_b=_bf(qe), ke_b=_bf(ke), kl_b=_bf(kl))


def _gla_in_specs(n_groups, rev):
    def rb(b, n):
        return b * n_groups + ((n_groups - 1 - n) if rev else n)

    return rb, [pl.BlockSpec((GLA_ROWS, GLA_KW), lambda b, n: (rb(b, n), C_Q // GLA_KW)),
                pl.BlockSpec((GLA_ROWS, GLA_KW), lambda b, n: (rb(b, n), C_K // GLA_KW)),
                pl.BlockSpec((GLA_ROWS, GLA_VW), lambda b, n: (rb(b, n), C_V // GLA_VW)),
                pl.BlockSpec((GLA_ROWS, GLA_VW), lambda b, n: (rb(b, n), C_Z // GLA_VW)),
                pl.BlockSpec((GLA_ROWS, LANE), lambda b, n: (rb(b, n), C_LR // LANE)),
                pl.BlockSpec((LANE, GLA_KW), lambda b, n: (0, 0)),
                pl.BlockSpec((1, GLA_KW), lambda b, n: (0, 0)),
                pl.BlockSpec((1, GLA_DV), lambda b, n: (0, 0))]


def _gla_fwd(proj, gw_pad, gate_b, gla_norm_g, bsz, lp):
    n_chunks = lp // GLA_CHUNK
    n_groups = n_chunks // GLA_GROUP
    tp = bsz * lp

    def body(q_ref, k_ref, v_ref, z_ref, lr_ref, gw_ref, gb_ref, gn_ref, oraw_ref, ya_ref, sall_ref, st_scr):
        grp = pl.program_id(1)

        @pl.when(grp == 0)
        def _():
            st_scr[...] = jnp.zeros_like(st_scr)

        chunks = [slice(j * GLA_CHUNK, (j + 1) * GLA_CHUNK) for j in range(GLA_GROUP)]
        cs = [_gla_gates(q_ref, k_ref, lr_ref, gw_ref, gb_ref, rows, True if j else grp > 0)
              for j, rows in enumerate(chunks)]
        gn = gn_ref[...]
        for h in range(GLA_HEADS):
            ks, vs = slice(h * GLA_DK, (h + 1) * GLA_DK), slice(h * GLA_DV, (h + 1) * GLA_DV)
            st = st_scr[h]
            for j, (rows, c) in enumerate(zip(chunks, cs)):
                sall_ref[0, j, h] = st
                v = v_ref[rows, vs]
                a = jnp.where(c["tril"], _dot_nt(c["qe_b"][:, ks], c["ke_b"][:, ks]), 0.0)
                o = _dot(_bf(a), v) + _dot_nt(c["qe_b"][:, ks], _bf(st))
                st = st * c["ebl"][:, ks] + _dot_tn(v, c["kl_b"][:, ks])
                oraw_ref[rows, vs] = o
                r = lax.rsqrt(jnp.mean(o * o, axis=-1, keepdims=True) + EPS)
                zg = z_ref[rows, vs].astype(F32)
                ya_ref[rows, vs] = _bf((o * r * gn) * (zg * _sigmoid(zg)))
            st_scr[h] = st

    rb, in_specs = _gla_in_specs(n_groups, False)
    return pl.pallas_call(
        body, name="gla_fwd", grid=(bsz, n_groups), in_specs=in_specs,
        out_specs=[pl.BlockSpec((GLA_ROWS, GLA_VW), lambda b, n: (rb(b, n), 0)),
                   pl.BlockSpec((GLA_ROWS, GLA_VW), lambda b, n: (rb(b, n), 0)),
                   pl.BlockSpec((1, GLA_GROUP, GLA_HEADS, GLA_DV, GLA_DK), lambda b, n: (b, n, 0, 0, 0))],
        out_shape=[jax.ShapeDtypeStruct((tp, GLA_VW), F32), jax.ShapeDtypeStruct((tp, GLA_VW), BF16),
                   jax.ShapeDtypeStruct((bsz, n_chunks, GLA_HEADS, GLA_DV, GLA_DK), F32)],
        scratch_shapes=[pltpu.VMEM((GLA_HEADS, GLA_DV, GLA_DK), F32)],
        compiler_params=_cp(("parallel", "arbitrary")),
    )(proj, proj, proj, proj, proj, gw_pad, gate_b, gla_norm_g)


def _gla_bwd(proj, gw_pad, gate_b, gla_norm_g, o_raw, s_all, d_ya, dproj, bsz, lp):
    n_chunks = lp // GLA_CHUNK
    n_groups = n_chunks // GLA_GROUP
    tp = bsz * lp

    def body(q_ref, k_ref, v_ref, z_ref, lr_ref, gw_ref, gb_ref, gn_ref, o_ref, s_ref, dya_ref, _,
             dp_ref, dz_ref, dgn_ref, dst_scr):
        dv_ref, dzg_ref = dp_ref.at[:, C_V:C_V + GLA_VW], dp_ref.at[:, C_Z:C_Z + GLA_VW]

        @pl.when(jnp.logical_and(pl.program_id(0) == 0, pl.program_id(1) == 0))
        def _():
            dgn_ref[...] = jnp.zeros_like(dgn_ref)

        @pl.when(pl.program_id(1) == 0)
        def _():
            dst_scr[...] = jnp.zeros_like(dst_scr)

        grp = n_groups - 1 - pl.program_id(1)
        chunks = [slice(j * GLA_CHUNK, (j + 1) * GLA_CHUNK) for j in range(GLA_GROUP)]
        cs = [_gla_gates(q_ref, k_ref, lr_ref, gw_ref, gb_ref, rows, True if j else grp > 0)
              for j, rows in enumerate(chunks)]
        gn = gn_ref[...]
        dgn = jnp.zeros((1, GLA_DV), F32)
        dqe_h, dke_h, dkl_h, dbl_h = ([[None] * GLA_HEADS for _ in chunks] for _ in range(4))
        for h in range(GLA_HEADS):
            ks, vs = slice(h * GLA_DK, (h + 1) * GLA_DK), slice(h * GLA_DV, (h + 1) * GLA_DV)
            dst = dst_scr[h]
            for j in reversed(range(GLA_GROUP)):
                rows, c = chunks[j], cs[j]
                v = v_ref[rows, vs]
                st = s_ref[0, j, h]
                o = o_ref[rows, vs]
                r = lax.rsqrt(jnp.mean(o * o, axis=-1, keepdims=True) + EPS)
                xh = o * r
                zg = z_ref[rows, vs].astype(F32)
                sg = _sigmoid(zg)
                dy = dya_ref[rows, vs].astype(F32)
                dzg_ref[rows, vs] = _bf(dy * (xh * gn) * (sg * (1.0 + zg * (1.0 - sg))))
                t = dy * (zg * sg)
                dgn += jnp.sum(t * xh, axis=0, keepdims=True)
                dxh = t * gn
                do_b = _bf(r * (dxh - xh * jnp.mean(dxh * xh, axis=-1, keepdims=True)))
                qe_b, ke_b, kl_b, dst_b = c["qe_b"][:, ks], c["ke_b"][:, ks], c["kl_b"][:, ks], _bf(dst)
                a = jnp.where(c["tril"], _dot_nt(qe_b, ke_b), 0.0)
                da_b = _bf(jnp.where(c["tril"], _dot_nt(do_b, v), 0.0))
                dqe_h[j][h] = _dot(da_b, ke_b) + _dot(do_b, _bf(st))
                dke_h[j][h] = _dot_tn(da_b, qe_b)
                dkl = _dot(v, dst_b)
                dkl_h[j][h] = dkl
                dv_ref[rows, vs] = _bf(_dot_tn(_bf(a), do_b) + _dot_nt(kl_b, dst_b))
                ddecay = jnp.sum(dst * st, axis=0, keepdims=True)
                dbl_h[j][h] = jnp.sum(dkl * c["kl"][:, ks], axis=0, keepdims=True) + ddecay * c["ebl"][:, ks]
                dst = dst * c["ebl"][:, ks] + _dot_tn(do_b, qe_b)
            dst_scr[h] = dst
        dgn_ref[...] += dgn
        ri = lax.broadcasted_iota(jnp.int32, (GLA_CHUNK, GLA_CHUNK), 0)
        ci = lax.broadcasted_iota(jnp.int32, (GLA_CHUNK, GLA_CHUNK), 1)
        triu = (ci >= ri).astype(F32)
        for j, (rows, c) in enumerate(zip(chunks, cs)):
            dqe, dke, dkl, dbl = (jnp.concatenate(p[j], axis=1) for p in (dqe_h, dke_h, dkl_h, dbl_h))
            db = dqe * c["qe"] - dke * c["ke"] - dkl * c["kl"] + jnp.where(c["row"] == GLA_CHUNK - 1, dbl, 0.0)
            dg = jnp.dot(triu, db, precision=lax.Precision.HIGHEST, preferred_element_type=F32)
            dg = jnp.where(c["live"], dg, 0.0)
            dz_ref[rows, :] = dg * (1.0 / GLA_GATE_NORMALIZER) * _sigmoid(-c["z"])
            dp_ref[rows, C_Q:C_Q + GLA_KW] = _bf(dqe * c["eb"] * (GLA_DK ** -0.5))
            dp_ref[rows, C_K:C_K + GLA_KW] = _bf(dke * c["enb"] + dkl * c["elb"])

    rb, in_specs = _gla_in_specs(n_groups, True)
    wide = pl.BlockSpec((GLA_ROWS, GLA_VW), lambda b, n: (rb(b, n), 0))
    group = C_MZ
    return pl.pallas_call(
        body, name="gla_bwd", grid=(bsz, n_groups),
        in_specs=in_specs + [wide, pl.BlockSpec((1, GLA_GROUP, GLA_HEADS, GLA_DV, GLA_DK),
                                                lambda b, n: (b, n_groups - 1 - n, 0, 0, 0)), wide,
                             pl.BlockSpec(memory_space=pl.ANY)],
        out_specs=[pl.BlockSpec((GLA_ROWS, group), lambda b, n: (rb(b, n), 0)),
                   pl.BlockSpec((GLA_ROWS, GLA_KW), lambda b, n: (rb(b, n), 0)),
                   pl.BlockSpec((1, GLA_DV), lambda b, n: (0, 0))],
        out_shape=[jax.ShapeDtypeStruct((tp, N_EXT), BF16), jax.ShapeDtypeStruct((tp, GLA_KW), F32),
                   jax.ShapeDtypeStruct((1, GLA_DV), F32)],
        input_output_aliases={11: 0},
        scratch_shapes=[pltpu.VMEM((GLA_HEADS, GLA_DV, GLA_DK), F32)],
        compiler_params=_cp(("arbitrary", "arbitrary")),
    )(proj, proj, proj, proj, proj, gw_pad, gate_b, gla_norm_g, o_raw, s_all, d_ya, dproj)


def _gate_bwd(dz, proj, gw_pad):
    tp = dz.shape[0]
    tm = _big_tok(tp)

    def body(dz_ref, lr_ref, gw_ref, dlr_ref, dgw_ref, dgb_ref):
        @pl.when(pl.program_id(0) == 0)
        def _():
            dgw_ref[...] = jnp.zeros_like(dgw_ref)
            dgb_ref[...] = jnp.zeros_like(dgb_ref)

        dz = dz_ref[...]
        dz_b = _bf(dz)
        dlr_ref[...] = _bf(_dot_nt(dz_b, gw_ref[...]))
        dgw_ref[...] += _dot_tn(lr_ref[...], dz_b)
        dgb_ref[...] += jnp.sum(dz, axis=0, keepdims=True)

    return pl.pallas_call(
        body, name="gate_bwd", grid=(tp // tm,),
        in_specs=[pl.BlockSpec((tm, GLA_KW), lambda i: (i, 0)),
                  pl.BlockSpec((tm, LANE), lambda i: (i, C_LR // LANE)),
                  pl.BlockSpec((LANE, GLA_KW), lambda i: (0, 0))],
        out_specs=[pl.BlockSpec((tm, LANE), lambda i: (i, 0)),
                   pl.BlockSpec((LANE, GLA_KW), lambda i: (0, 0)),
                   pl.BlockSpec((1, GLA_KW), lambda i: (0, 0))],
        out_shape=[jax.ShapeDtypeStruct((tp, LANE), BF16), jax.ShapeDtypeStruct((LANE, GLA_KW), F32),
                   jax.ShapeDtypeStruct((1, GLA_KW), F32)],
        compiler_params=_cp(("arbitrary",)),
    )(dz, proj, gw_pad)


def _rms_fwd(x):
    r = lax.rsqrt(jnp.mean(x * x, axis=-1, keepdims=True) + EPS)
    return x * r, r


def _rms_bwd(dy, xh, r, g):
    dxh = dy * g
    dx = r * (dxh - xh * jnp.mean(dxh * xh, axis=-1, keepdims=True))
    return dx, jnp.sum(dy * xh, axis=0, keepdims=True)


def _q_up(proj, q_norm_g, wn, wr, wt, cos_t, sin_t, bsz, lp):
    tp = bsz * lp
    tok = _attn_block(lp)
    nb = lp // tok

    def body(cq_ref, g_ref, wn_ref, wr_ref, wt_ref, cos_ref, sin_ref, q_ref):
        xh, _ = _rms_fwd(cq_ref[...].astype(F32))
        cqn = _bf(xh * g_ref[...])
        nope = _dot(cqn, wn_ref[...])
        rope = _dot(cqn, wr_ref[...])
        rot = _dot(cqn, wt_ref[...])
        cos, sin = cos_ref[...], sin_ref[...]
        one = (lax.broadcasted_iota(jnp.int32, (tok, LANE), 1) == BIAS_LANE).astype(F32)
        for h in range(MLA_HEADS):
            sl = slice(h * LANE, (h + 1) * LANE)
            q_ref[:, h * QKW:h * QKW + LANE] = _bf(nope[:, sl])
            q_ref[:, h * QKW + LANE:(h + 1) * QKW] = _bf(rope[:, sl] * cos + rot[:, sl] * sin + one)

    wspec = pl.BlockSpec((MLA_QR, MLA_HEADS * LANE), lambda b, i: (0, 0))
    tspec = pl.BlockSpec((tok, LANE), lambda b, i: (i, 0))
    return pl.pallas_call(
        body, name="mla_q_up", grid=(bsz, nb),
        in_specs=[pl.BlockSpec((tok, MLA_QR), lambda b, i: (b * nb + i, C_CQ // MLA_QR)),
                  pl.BlockSpec((1, MLA_QR), lambda b, i: (0, 0)), wspec, wspec, wspec, tspec, tspec],
        out_specs=pl.BlockSpec((tok, MLA_HEADS * QKW), lambda b, i: (b * nb + i, 0)),
        out_shape=jax.ShapeDtypeStruct((tp, MLA_HEADS * QKW), BF16),
        compiler_params=_cp(("parallel", "parallel")),
    )(proj, q_norm_g, wn, wr, wt, cos_t, sin_t)


def _kv_up(proj, kv_norm_g, wk, wv, cos_t, sin_t, bsz, lp):
    tp = bsz * lp
    tok = _attn_block(lp)
    nb = lp // tok

    def body(ckv_ref, kr_ref, krot_ref, g_ref, wk_ref, wv_ref, cos_ref, sin_ref, k_ref, v_ref):
        xh, _ = _rms_fwd(ckv_ref[...].astype(F32))
        cn = _bf(xh * g_ref[...])
        kn = _dot(cn, wk_ref[...])
        v_ref[...] = _bf(_dot(cn, wv_ref[...]))
        pos = pl.program_id(1) * tok + lax.broadcasted_iota(jnp.int32, (tok, LANE), 0)
        lane = lax.broadcasted_iota(jnp.int32, (tok, LANE), 1)
        bias = jnp.where(jnp.logical_and(lane == BIAS_LANE, pos < FRONT), KEY_BIAS, 0.0)
        kr = _bf(kr_ref[...].astype(F32) * cos_ref[...] + krot_ref[...].astype(F32) * sin_ref[...] + bias)
        for h in range(MLA_HEADS):
            k_ref[:, h * QKW:h * QKW + LANE] = _bf(kn[:, h * LANE:(h + 1) * LANE])
            k_ref[:, h * QKW + LANE:(h + 1) * QKW] = kr

    wspec = pl.BlockSpec((MLA_KVR, MLA_HEADS * LANE), lambda b, i: (0, 0))
    tspec = pl.BlockSpec((tok, LANE), lambda b, i: (i, 0))
    return pl.pallas_call(
        body, name="mla_kv_up", grid=(bsz, nb),
        in_specs=[pl.BlockSpec((tok, LANE), lambda b, i: (b * nb + i, C_CKV // LANE)),
                  pl.BlockSpec((tok, LANE), lambda b, i: (b * nb + i, C_KR // LANE)),
                  pl.BlockSpec((tok, LANE), lambda b, i: (b * nb + i, C_KROT // LANE)),
                  pl.BlockSpec((1, MLA_KVR), lambda b, i: (0, 0)), wspec, wspec, tspec, tspec],
        out_specs=[pl.BlockSpec((tok, MLA_HEADS * QKW), lambda b, i: (b * nb + i, 0)),
                   pl.BlockSpec((tok, MLA_HEADS * LANE), lambda b, i: (b * nb + i, 0))],
        out_shape=[jax.ShapeDtypeStruct((tp, MLA_HEADS * QKW), BF16),
                   jax.ShapeDtypeStruct((tp, MLA_HEADS * LANE), BF16)],
        compiler_params=_cp(("parallel", "parallel")),
    )(proj, proj, proj, kv_norm_g, wk, wv, cos_t, sin_t)


ATT_SCALE = MLA_QK ** -0.5


KEY_BIAS = -1e30
BIAS_LANE = MLA_ROPE
NEG = 2 * KEY_BIAS
LOG2E = 1.4426950408889634
EXP2_SCALE = ATT_SCALE * LOG2E


def _causal(r0, tq, kmax):
    return (lax.broadcasted_iota(jnp.int32, (tq, kmax), 1)
            <= r0 + lax.broadcasted_iota(jnp.int32, (tq, kmax), 0))


def _attn_fwd(qf, kf, vf, proj, bsz, lp):
    tp = bsz * lp
    tq = _attn_block(lp)

    def body(q_ref, k_ref, v_ref, mz_ref, ob_ref, yb_ref, lse_ref):
        for r0 in range(0, lp, tq):
            rows, kmax = slice(r0, r0 + tq), r0 + tq
            s = jnp.where(_causal(r0, tq, kmax), _dot_nt(q_ref[rows, :], k_ref[0:kmax, :]), NEG)
            m = jnp.max(s, axis=-1, keepdims=True)
            p = jnp.exp2((s - m) * EXP2_SCALE)
            l = jnp.sum(p, axis=-1, keepdims=True)
            o = _dot(_bf(p), v_ref[0:kmax, :]) / l
            ob_ref[rows, :] = _bf(o)
            mz = mz_ref[rows, :].astype(F32)
            yb_ref[rows, :] = _bf(o * (mz * _sigmoid(mz)))
            lse_ref[0, 0, rows, :] = jnp.broadcast_to(m * EXP2_SCALE + jnp.log2(l), (tq, LANE))

    head = lambda off: pl.BlockSpec((lp, MLA_DV), lambda b, h: (b, off + h))
    return pl.pallas_call(
        body, name="mla_attn_fwd", grid=(bsz, MLA_HEADS),
        in_specs=[pl.BlockSpec((lp, QKW), lambda b, h: (b, h)), pl.BlockSpec((lp, QKW), lambda b, h: (b, h)),
                  head(0), head(C_MZ // MLA_DV)],
        out_specs=[head(0), head(0), pl.BlockSpec((1, 1, lp, LANE), lambda b, h: (b, h, 0, 0))],
        out_shape=[jax.ShapeDtypeStruct((tp, MLA_HEADS * MLA_DV), BF16),
                   jax.ShapeDtypeStruct((tp, MLA_HEADS * MLA_DV), BF16),
                   jax.ShapeDtypeStruct((bsz, MLA_HEADS, lp, LANE), F32)],
        compiler_params=_cp(("parallel", "parallel"), 56),
    )(qf, kf, vf, proj)


def _attn_bwd_pre(d_yb, proj, o_b, dproj, bsz, lp):
    tp = bsz * lp
    tok = _attn_block(lp)
    nb = lp // tok
    w = MLA_HEADS * MLA_DV

    def body(dy_ref, mz_ref, o_ref, _, do_ref, dmz_ref, dl_ref):
        dy = dy_ref[...].astype(F32)
        mz = mz_ref[...].astype(F32)
        o = o_ref[...].astype(F32)
        s = _sigmoid(mz)
        do = _bf(dy * (mz * s))
        do_ref[...] = do
        dmz_ref[...] = _bf(dy * o * (s * (1.0 + mz * (1.0 - s))))
        prod = do.astype(F32) * o
        for h in range(MLA_HEADS):
            dl = jnp.sum(prod[:, h * MLA_DV:(h + 1) * MLA_DV], axis=-1, keepdims=True)
            dl_ref[0, h] = jnp.broadcast_to(dl, (tok, LANE))

    return pl.pallas_call(
        body, name="mla_attn_bwd_pre", grid=(bsz, nb),
        in_specs=[pl.BlockSpec((tok, w), lambda b, i: (b * nb + i, 0)),
                  pl.BlockSpec((tok, w), lambda b, i: (b * nb + i, C_MZ // w)),
                  pl.BlockSpec((tok, w), lambda b, i: (b * nb + i, 0)), pl.BlockSpec(memory_space=pl.ANY)],
        out_specs=[pl.BlockSpec((tok, w), lambda b, i: (b * nb + i, 0)),
                   pl.BlockSpec((tok, w), lambda b, i: (b * nb + i, C_MZ // w)),
                   pl.BlockSpec((1, MLA_HEADS, tok, LANE), lambda b, i: (b, 0, i, 0))],
        out_shape=[jax.ShapeDtypeStruct((tp, w), BF16), jax.ShapeDtypeStruct((tp, N_EXT), BF16),
                   jax.ShapeDtypeStruct((bsz, MLA_HEADS, lp, LANE), F32)],
        input_output_aliases={3: 1},
        compiler_params=_cp(("parallel", "parallel")),
    )(d_yb, proj, o_b, dproj)


def _attn_bwd(qf, kf, vf, d_o, lse, delta, bsz, lp):
    tp = bsz * lp
    tq = _attn_block(lp)

    def body(q_ref, k_ref, v_ref, do_ref, lse_ref, dl_ref, dq_ref, dk_ref, dv_ref, dk_acc, dv_acc):
        dk_acc[...] = jnp.zeros_like(dk_acc)
        dv_acc[...] = jnp.zeros_like(dv_acc)
        for r0 in range(0, lp, tq):
            rows, kmax = slice(r0, r0 + tq), r0 + tq
            q, do = q_ref[rows, :], do_ref[rows, :]
            k, v = k_ref[0:kmax, :], v_ref[0:kmax, :]
            p = jnp.exp2(_dot_nt(q, k) * EXP2_SCALE - lse_ref[0, 0, rows, :][:, :1])
            p = jnp.where(_causal(r0, tq, kmax), p, 0.0)
            ds = _bf(p * (_dot_nt(do, v) - dl_ref[0, 0, rows, :][:, :1]))
            dq_ref[rows, :] = _bf(_dot(ds, k) * ATT_SCALE)
            dk_acc[0:kmax, :] += _dot_tn(ds, q)
            dv_acc[0:kmax, :] += _dot_tn(_bf(p), do)
        dk_ref[...] = _bf(dk_acc[...] * ATT_SCALE)
        dv_ref[...] = _bf(dv_acc[...])

    wide = pl.BlockSpec((lp, QKW), lambda b, h: (b, h))
    narrow = pl.BlockSpec((lp, MLA_DV), lambda b, h: (b, h))
    stat = pl.BlockSpec((1, 1, lp, LANE), lambda b, h: (b, h, 0, 0))
    return pl.pallas_call(
        body, name="mla_attn_bwd", grid=(bsz, MLA_HEADS),
        in_specs=[wide, wide, narrow, narrow, stat, stat], out_specs=[wide, wide, narrow],
        out_shape=[jax.ShapeDtypeStruct((tp, MLA_HEADS * QKW), BF16), jax.ShapeDtypeStruct((tp, MLA_HEADS * QKW), BF16),
                   jax.ShapeDtypeStruct((tp, MLA_HEADS * MLA_DV), BF16)],
        scratch_shapes=[pltpu.VMEM((lp, QKW), F32), pltpu.VMEM((lp, MLA_DV), F32)],
        compiler_params=_cp(("parallel", "parallel"), 56),
    )(qf, kf, vf, d_o, lse, delta)


def _q_up_bwd(dqf, proj, q_norm_g, wn, wr, wt, cos_t, sin_t, dproj, bsz, lp):
    tp = bsz * lp
    tok = _attn_block(lp)
    nb = lp // tok
    hw = MLA_HEADS * LANE

    def body(dq_ref, cq_ref, g_ref, wn_ref, wr_ref, wt_ref, cos_ref, sin_ref, _,
             dcq_ref, dwn_ref, dwr_ref, dwt_ref, dg_ref):
        @pl.when(jnp.logical_and(pl.program_id(0) == 0, pl.program_id(1) == 0))
        def _():
            for r in (dwn_ref, dwr_ref, dwt_ref, dg_ref):
                r[...] = jnp.zeros_like(r)

        g = g_ref[...]
        xh, r = _rms_fwd(cq_ref[...].astype(F32))
        cqn = _bf(xh * g)
        cos, sin = cos_ref[...], sin_ref[...]
        dcqn = jnp.zeros((tok, MLA_QR), F32)
        for h in range(MLA_HEADS):
            sl = slice(h * LANE, (h + 1) * LANE)
            dn = dq_ref[:, h * QKW:h * QKW + LANE]
            dr = dq_ref[:, h * QKW + LANE:(h + 1) * QKW].astype(F32)
            dr_c, dr_s = _bf(dr * cos), _bf(dr * sin)
            dcqn += _dot_nt(dn, wn_ref[:, sl]) + _dot_nt(dr_c, wr_ref[:, sl]) + _dot_nt(dr_s, wt_ref[:, sl])
            dwn_ref[:, sl] += _dot_tn(cqn, dn)
            dwr_ref[:, sl] += _dot_tn(cqn, dr_c)
            dwt_ref[:, sl] += _dot_tn(cqn, dr_s)
        dx, dg = _rms_bwd(dcqn, xh, r, g)
        dcq_ref[...] = _bf(dx)
        dg_ref[...] += dg

    aspec = pl.BlockSpec((MLA_QR, hw), lambda b, i: (0, 0))
    tspec = pl.BlockSpec((tok, LANE), lambda b, i: (i, 0))
    return pl.pallas_call(
        body, name="mla_q_up_bwd", grid=(bsz, nb),
        in_specs=[pl.BlockSpec((tok, MLA_HEADS * QKW), lambda b, i: (b * nb + i, 0)),
                  pl.BlockSpec((tok, MLA_QR), lambda b, i: (b * nb + i, C_CQ // MLA_QR)),
                  pl.BlockSpec((1, MLA_QR), lambda b, i: (0, 0)), aspec, aspec, aspec, tspec, tspec,
                  pl.BlockSpec(memory_space=pl.ANY)],
        out_specs=[pl.BlockSpec((tok, MLA_QR), lambda b, i: (b * nb + i, C_CQ // MLA_QR)), aspec, aspec, aspec,
                   pl.BlockSpec((1, MLA_QR), lambda b, i: (0, 0))],
        out_shape=[jax.ShapeDtypeStruct((tp, N_EXT), BF16)] + [jax.ShapeDtypeStruct((MLA_QR, hw), F32)] * 3
        + [jax.ShapeDtypeStruct((1, MLA_QR), F32)],
        input_output_aliases={8: 0},
        compiler_params=_cp(("arbitrary", "arbitrary")),
    )(dqf, proj, q_norm_g, wn, wr, wt, cos_t, sin_t, dproj)


def _kv_up_bwd(dkf, dvf, proj, kv_norm_g, wk, wv, cos_t, sin_t, d_lr, dproj, bsz, lp):
    tp = bsz * lp
    tok = _attn_block(lp)
    nb = lp // tok
    hw = MLA_HEADS * LANE

    def body(dk_ref, dv_ref, ckv_ref, g_ref, wk_ref, wv_ref, cos_ref, sin_ref, dlr_ref, _,
             dp_ref, dwk_ref, dwv_ref, dg_ref):
        dckv_ref, dkr_ref, dkrot_ref = (dp_ref.at[:, j * LANE:(j + 1) * LANE] for j in range(3))
        dp_ref[:, 3 * LANE:] = dlr_ref[...]
        @pl.when(jnp.logical_and(pl.program_id(0) == 0, pl.program_id(1) == 0))
        def _():
            for r in (dwk_ref, dwv_ref, dg_ref):
                r[...] = jnp.zeros_like(r)

        g = g_ref[...]
        xh, r = _rms_fwd(ckv_ref[...].astype(F32))
        cn = _bf(xh * g)
        dv = dv_ref[...]
        dcn = _dot_nt(dv, wv_ref[...])
        dwv_ref[...] += _dot_tn(cn, dv)
        drope = jnp.zeros((tok, LANE), F32)
        for h in range(MLA_HEADS):
            sl = slice(h * LANE, (h + 1) * LANE)
            dn = dk_ref[:, h * QKW:h * QKW + LANE]
            drope += dk_ref[:, h * QKW + LANE:(h + 1) * QKW].astype(F32)
            dcn += _dot_nt(dn, wk_ref[:, sl])
            dwk_ref[:, sl] += _dot_tn(cn, dn)
        dkr_ref[...] = _bf(drope * cos_ref[...])
        dkrot_ref[...] = _bf(drope * sin_ref[...])
        dx, dg = _rms_bwd(dcn, xh, r, g)
        dckv_ref[...] = _bf(dx)
        dg_ref[...] += dg

    aspec = pl.BlockSpec((MLA_KVR, hw), lambda b, i: (0, 0))
    tspec = pl.BlockSpec((tok, LANE), lambda b, i: (i, 0))
    ospec = pl.BlockSpec((tok, LANE), lambda b, i: (b * nb + i, 0))
    return pl.pallas_call(
        body, name="mla_kv_up_bwd", grid=(bsz, nb),
        in_specs=[pl.BlockSpec((tok, MLA_HEADS * QKW), lambda b, i: (b * nb + i, 0)),
                  pl.BlockSpec((tok, hw), lambda b, i: (b * nb + i, 0)),
                  pl.BlockSpec((tok, LANE), lambda b, i: (b * nb + i, C_CKV // LANE)),
                  pl.BlockSpec((1, MLA_KVR), lambda b, i: (0, 0)), aspec, aspec, tspec, tspec, ospec,
                  pl.BlockSpec(memory_space=pl.ANY)],
        out_specs=[pl.BlockSpec((tok, 4 * LANE), lambda b, i: (b * nb + i, C_CKV // (4 * LANE))), aspec, aspec,
                   pl.BlockSpec((1, MLA_KVR), lambda b, i: (0, 0))],
        out_shape=[jax.ShapeDtypeStruct((tp, N_EXT), BF16)] + [jax.ShapeDtypeStruct((MLA_KVR, hw), F32)] * 2
        + [jax.ShapeDtypeStruct((1, MLA_KVR), F32)],
        input_output_aliases={9: 0},
        compiler_params=_cp(("arbitrary", "arbitrary")),
    )(dkf, dvf, proj, kv_norm_g, wk, wv, cos_t, sin_t, d_lr, dproj)


def _mid_fwd(ya_in, yb_in, proj, hp, target, w_gp, w_mp, w_o, final_g, bsz, lp):
    tp = bsz * lp
    tm = _attn_block(lp)
    nb = lp // tm

    def body(ya_ref, yb_ref, gg_ref, gm_ref, h_ref, t_ref, wgp_ref, wmp_ref, wo_ref, fg_ref,
             ya_out, yb_out, dh_ref, loss_ref, dfg_ref):
        @pl.when(jnp.logical_and(pl.program_id(0) == 0, pl.program_id(1) == 0))
        def _():
            loss_ref[...] = jnp.zeros_like(loss_ref)
            dfg_ref[...] = jnp.zeros_like(dfg_ref)

        y_a = _dot(ya_ref[...], wgp_ref[...])
        y_b = _dot(yb_ref[...], wmp_ref[...])
        ya_out[...] = _bf(y_a)
        yb_out[...] = _bf(y_b)
        merged = _sigmoid(gg_ref[...].astype(F32)) * y_a + _sigmoid(gm_ref[...].astype(F32)) * y_b
        h2 = h_ref[...] + _dot(_bf(merged), wo_ref[...])
        fg = fg_ref[...]
        xh, r = _rms_fwd(h2)
        pos = pl.program_id(1) * tm + lax.broadcasted_iota(jnp.int32, (tm, 1), 0)
        err = jnp.where(pos >= X0, xh * fg - t_ref[...], 0.0)
        loss_ref[...] += 0.5 * jnp.sum(jnp.mean(err * err, axis=-1, keepdims=True), axis=0, keepdims=True)
        dy = err * (1.0 / D_MODEL)
        dx, dfg = _rms_bwd(dy, xh, r, fg)
        dh_ref[...] = dx
        dfg_ref[...] += dfg

    tok = lambda c: pl.BlockSpec((tm, D_MODEL), lambda b, i: (b * nb + i, c))
    wspec = pl.BlockSpec((D_MODEL, D_MODEL), lambda b, i: (0, 0))
    return pl.pallas_call(
        body, name="mid_fwd", grid=(bsz, nb),
        in_specs=[tok(0), tok(0), tok(C_GG // D_MODEL), tok(C_GM // D_MODEL), tok(0), tok(0),
                  wspec, wspec, wspec, pl.BlockSpec((1, D_MODEL), lambda b, i: (0, 0))],
        out_specs=[tok(0), tok(0), tok(0), pl.BlockSpec((1, LANE), lambda b, i: (0, 0)),
                   pl.BlockSpec((1, D_MODEL), lambda b, i: (0, 0))],
        out_shape=[jax.ShapeDtypeStruct((tp, D_MODEL), BF16), jax.ShapeDtypeStruct((tp, D_MODEL), BF16),
                   jax.ShapeDtypeStruct((tp, D_MODEL), F32), jax.ShapeDtypeStruct((1, LANE), F32),
                   jax.ShapeDtypeStruct((1, D_MODEL), F32)],
        compiler_params=_cp(("arbitrary", "arbitrary"), 48),
    )(ya_in, yb_in, proj, proj, hp, target, w_gp, w_mp, w_o, final_g)


def _mid_bwd(dh2, y_a, y_b, proj, ya_in, yb_in, w_o, w_gp, w_mp):
    tp = dh2.shape[0]
    tm = _attn_block(tp)
    nsteps = tp // tm

    def body(dh_ref, ya_ref, yb_ref, gg_ref, gm_ref, yai_ref, ybi_ref, wo_ref, wgp_ref, wmp_ref,
             dyai_ref, dybi_ref, dgate_ref, dwo_ref, dwgp_ref, dwmp_ref, a_o, a_gp, a_mp):
        @pl.when(pl.program_id(0) == 0)
        def _():
            for r in (a_o, a_gp, a_mp):
                r[...] = jnp.zeros_like(r)

        dh = _bf(dh_ref[...])
        dm = _dot_nt(dh, wo_ref[...])
        y_a, y_b = ya_ref[...].astype(F32), yb_ref[...].astype(F32)
        sg, sm = _sigmoid(gg_ref[...].astype(F32)), _sigmoid(gm_ref[...].astype(F32))
        d_ya, d_yb = _bf(sg * dm), _bf(sm * dm)
        dgate_ref[:, :D_MODEL] = _bf(dm * y_a * sg * (1.0 - sg))
        dgate_ref[:, D_MODEL:] = _bf(dm * y_b * sm * (1.0 - sm))
        a_o[...] += _dot_tn(_bf(sg * y_a + sm * y_b), dh)
        a_gp[...] += _dot_tn(yai_ref[...], d_ya)
        a_mp[...] += _dot_tn(ybi_ref[...], d_yb)
        dyai_ref[...] = _bf(_dot_nt(d_ya, wgp_ref[...]))
        dybi_ref[...] = _bf(_dot_nt(d_yb, wmp_ref[...]))

        @pl.when(pl.program_id(0) == nsteps - 1)
        def _():
            pltpu.sync_copy(a_o, dwo_ref)
            pltpu.sync_copy(a_gp, dwgp_ref)
            pltpu.sync_copy(a_mp, dwmp_ref)

    tok = lambda c: pl.BlockSpec((tm, D_MODEL), lambda i: (i, c))
    wspec = pl.BlockSpec((D_MODEL, D_MODEL), lambda i: (0, 0))
    anyspec = pl.BlockSpec(memory_space=pl.ANY)
    wshape = jax.ShapeDtypeStruct((D_MODEL, D_MODEL), F32)
    return pl.pallas_call(
        body, name="mid_bwd", grid=(nsteps,),
        in_specs=[tok(0), tok(0), tok(0), tok(C_GG // D_MODEL), tok(C_GM // D_MODEL), tok(0), tok(0),
                  wspec, wspec, wspec],
        out_specs=[tok(0), tok(0), pl.BlockSpec((tm, 2 * D_MODEL), lambda i: (i, C_GG // (2 * D_MODEL))),
                   anyspec, anyspec, anyspec],
        out_shape=[jax.ShapeDtypeStruct((tp, D_MODEL), BF16)] * 2 + [jax.ShapeDtypeStruct((tp, N_EXT), BF16)]
        + [wshape] * 3,
        scratch_shapes=[pltpu.VMEM((D_MODEL, D_MODEL), F32)] * 3,
        compiler_params=_cp(("arbitrary",), 56),
    )(dh2, y_a, y_b, proj, proj, ya_in, yb_in, w_o, w_gp, w_mp)


MESH_ID = pl.DeviceIdType.MESH
EXCHANGE_SEMS = [pltpu.SemaphoreType.DMA((N_DEV - 1,)), pltpu.SemaphoreType.DMA((N_DEV - 1,)), pltpu.SemaphoreType.DMA]


def _my_place():
    return lax.axis_index("x"), lax.axis_index("y"), lax.axis_index("c")


def _exchange(g_ref, recv_ref, send_sems, recv_sems, local_sem, start):
    x, y, c = _my_place()
    me = 4 * x + 2 * y + c
    own = pltpu.make_async_copy(g_ref.at[me], recv_ref.at[me], local_sem)
    sends, lands = [], []
    for d in range(1, N_DEV):
        px = 1 - x if d & 4 else x
        py = 1 - y if d & 2 else y
        pc = 1 - c if d & 1 else c
        peer = 4 * px + 2 * py + pc
        for slot, group in ((me, sends),) if start else ((me, sends), (peer, lands)):
            group.append(pltpu.make_async_remote_copy(
                src_ref=g_ref.at[peer], dst_ref=recv_ref.at[slot], send_sem=send_sems.at[d - 1],
                recv_sem=recv_sems.at[d - 1], device_id=(px, py, pc), device_id_type=MESH_ID))
    if start:
        own.start()
        for cp in sends:
            cp.start()
    else:
        for cp in lands:
            cp.wait_recv()
        for cp in sends:
            cp.wait_send()
        own.wait()


def _dw_in(u, dproj, slabs):
    tp = u.shape[0]
    tm, tn = _big_tok(tp), EXT_BLOCK
    nj, ni = N_EXT // tn, tp // tm

    def body(u_ref, d_ref, g_ref, o_ref, recv_ref, send_sems, recv_sems, local_sem):
        j, i = pl.program_id(0), pl.program_id(1)

        @pl.when(jnp.logical_and(j == 0, i == 0))
        def _():
            _exchange(g_ref, recv_ref, send_sems, recv_sems, local_sem, True)

        @pl.when(i == 0)
        def _():
            o_ref[...] = jnp.zeros_like(o_ref)

        o_ref[...] += _dot_tn(u_ref[...], d_ref[...])

        @pl.when(jnp.logical_and(j == nj - 1, i == ni - 1))
        def _():
            _exchange(g_ref, recv_ref, send_sems, recv_sems, local_sem, False)

    anyspec = pl.BlockSpec(memory_space=pl.ANY)
    return pl.pallas_call(
        body, name="dw_in", grid=(nj, ni),
        in_specs=[pl.BlockSpec((tm, D_MODEL), lambda j, i: (i, 0)), pl.BlockSpec((tm, tn), lambda j, i: (i, j)), anyspec],
        out_specs=[pl.BlockSpec((D_MODEL, tn), lambda j, i: (0, j)), anyspec],
        out_shape=[jax.ShapeDtypeStruct((D_MODEL, N_EXT), F32), jax.ShapeDtypeStruct(slabs.shape, slabs.dtype)],
        scratch_shapes=EXCHANGE_SEMS,
        compiler_params=_cp(("arbitrary", "arbitrary"), 48),
    )(u, dproj, slabs)


def _dx_in(dproj, w_ext, hp, dh2, norm_g, slabs):
    tp = hp.shape[0]
    tm, tk = _big_tok(tp), EXT_BLOCK
    nk = N_EXT // tk
    ni = tp // tm

    def body(d_ref, w_ref, h_ref, dh_ref, g_ref, s_ref, o_ref, dg_ref, recv_ref, acc, send_sems, recv_sems, local_sem):
        k = pl.program_id(1)

        @pl.when(jnp.logical_and(pl.program_id(0) == 0, k == 0))
        def _():
            _exchange(s_ref, recv_ref, send_sems, recv_sems, local_sem, True)

        @pl.when(jnp.logical_and(pl.program_id(0) == 0, k == 0))
        def _():
            dg_ref[...] = jnp.zeros_like(dg_ref)

        @pl.when(k == 0)
        def _():
            acc[...] = jnp.zeros_like(acc)

        acc[...] += _dot_nt(d_ref[...], w_ref[...])

        @pl.when(k == nk - 1)
        def _():
            g = g_ref[...]
            xh, r = _rms_fwd(h_ref[...])
            dx, dg = _rms_bwd(acc[...], xh, r, g)
            o_ref[...] = dh_ref[...] + dx
            dg_ref[...] += dg

        @pl.when(jnp.logical_and(pl.program_id(0) == ni - 1, k == nk - 1))
        def _():
            _exchange(s_ref, recv_ref, send_sems, recv_sems, local_sem, False)

    tok = pl.BlockSpec((tm, D_MODEL), lambda i, k: (i, 0))
    anyspec = pl.BlockSpec(memory_space=pl.ANY)
    return pl.pallas_call(
        body, name="dx_in", grid=(ni, nk),
        in_specs=[pl.BlockSpec((tm, tk), lambda i, k: (i, k)), pl.BlockSpec((D_MODEL, tk), lambda i, k: (0, k)),
                  tok, tok, pl.BlockSpec((1, D_MODEL), lambda i, k: (0, 0)), anyspec],
        out_specs=[tok, pl.BlockSpec((1, D_MODEL), lambda i, k: (0, 0)), anyspec],
        out_shape=[jax.ShapeDtypeStruct((tp, D_MODEL), F32), jax.ShapeDtypeStruct((1, D_MODEL), F32),
                   jax.ShapeDtypeStruct(slabs.shape, slabs.dtype)],
        scratch_shapes=[pltpu.VMEM((tm, D_MODEL), F32)] + EXCHANGE_SEMS,
        compiler_params=_cp(("arbitrary", "arbitrary"), 56),
    )(dproj, w_ext, hp, dh2, norm_g, slabs)


def _meta_grad(dhp3):
    bsz = dhp3.shape[0]

    def body(d_ref, o_ref):
        @pl.when(pl.program_id(0) == 0)
        def _():
            o_ref[...] = jnp.zeros_like(o_ref)

        o_ref[...] += d_ref[0]

    return pl.pallas_call(
        body, name="meta_grad", grid=(bsz,),
        in_specs=[pl.BlockSpec((1, N_META, D_MODEL), lambda b: (b, FRONT // N_META, 0))],
        out_specs=pl.BlockSpec((N_META, D_MODEL), lambda b: (0, 0)),
        out_shape=jax.ShapeDtypeStruct((N_META, D_MODEL), F32),
        compiler_params=_cp(("arbitrary",)),
    )(dhp3)


W_IN_SHARD = N_IN // N_DEV


def _pad_lanes(a, width=LANE):
    return jnp.pad(a, [(0, 0)] * (a.ndim - 1) + [(0, width - a.shape[-1])])


def _rot_cols(w):
    half = w.shape[-1] // 2
    return jnp.concatenate([-w[..., half:], w[..., :half]], axis=-1)


def _unrot_cols(dw):
    half = dw.shape[-1] // 2
    return jnp.concatenate([dw[..., half:], -dw[..., :half]], axis=-1)


def _w_in_cols(shards, lo, hi):
    parts = []
    for k in range(lo // W_IN_SHARD, (hi - 1) // W_IN_SHARD + 1):
        a, b = max(lo, k * W_IN_SHARD), min(hi, (k + 1) * W_IN_SHARD)
        parts.append(shards[k][:, a - k * W_IN_SHARD:b - k * W_IN_SHARD])
    return parts[0] if len(parts) == 1 else jnp.concatenate(parts, axis=1)


def _w_in_ext(shards):
    c = lambda lo, hi: _w_in_cols(shards, lo, hi)
    kr = c(O_KR, O_MZ)
    return jnp.concatenate([
        c(O_V, O_LR), c(O_Z, O_CQ), c(O_Q, O_K), c(O_K, O_V), c(O_MZ, O_GG), c(O_GG, O_GM), c(O_GM, N_IN),
        c(O_CKV, O_KR), _pad_lanes(kr), _pad_lanes(_rot_cols(kr)), _pad_lanes(c(O_LR, O_Z)), c(O_CQ, O_CKV)], axis=1)


def _w_in_grad(dw):
    g = lambda start, width: dw[:, start:start + width]
    kr = g(C_KR, MLA_ROPE) + _unrot_cols(g(C_KROT, MLA_ROPE))
    return jnp.concatenate([
        g(C_Q, GLA_KW), g(C_K, GLA_KW), g(C_V, GLA_VW), g(C_LR, GLA_RANK), g(C_Z, GLA_VW), g(C_CQ, MLA_QR),
        g(C_CKV, MLA_KVR), kr, g(C_MZ, D_MODEL), g(C_GG, D_MODEL), g(C_GM, D_MODEL)], axis=1)


def _rope_tables(lp):
    inv = 1.0 / (ROPE_BASE ** (jnp.arange(0, MLA_ROPE, 2, dtype=F32) / MLA_ROPE))
    ang = (jnp.arange(lp, dtype=F32) - FRONT)[:, None] * inv[None, :]
    cos, sin = jnp.cos(ang), jnp.sin(ang)
    return _pad_lanes(jnp.concatenate([cos, cos], axis=1)), _pad_lanes(jnp.concatenate([sin, sin], axis=1))


def _local_step(x, loss_target, w):
    bsz, seq, _ = x.shape
    lp = X0 + seq
    tp = bsz * lp
    assert lp % TOK == 0 and lp % GLA_CHUNK == 0
    meta = jnp.broadcast_to(w["meta_tokens"][None], (bsz, N_META, D_MODEL))
    hp = jnp.concatenate([jnp.zeros((bsz, FRONT, D_MODEL), F32), meta, x], axis=1).reshape(tp, D_MODEL)
    target = jnp.pad(loss_target, ((0, 0), (X0, 0), (0, 0))).reshape(tp, D_MODEL)
    cos_t, sin_t = _rope_tables(lp)

    w_ext = _w_in_ext(w["w_in"])
    gw_pad = jnp.pad(w["gla_gate_w"], ((0, LANE - GLA_RANK), (0, 0)))
    uq = w["mla_w_uq"].reshape(MLA_QR, MLA_HEADS, MLA_QK)
    rope_w = uq[:, :, MLA_NOPE:]
    hw = MLA_HEADS * LANE
    wn = uq[:, :, :MLA_NOPE].reshape(MLA_QR, hw)
    wr = _pad_lanes(rope_w).reshape(MLA_QR, hw)
    wt = _pad_lanes(_rot_cols(rope_w)).reshape(MLA_QR, hw)
    ukv = w["mla_w_ukv"].reshape(MLA_KVR, MLA_HEADS, MLA_NOPE + MLA_DV)
    wk = ukv[:, :, :MLA_NOPE].reshape(MLA_KVR, hw)
    wv = ukv[:, :, MLA_NOPE:].reshape(MLA_KVR, hw)

    u, proj = _proj_in(hp, w["norm_g"], w_ext)
    o_raw, ya_in, s_all = _gla_fwd(proj, gw_pad, w["gla_gate_b"], w["gla_norm_g"], bsz, lp)
    qf = _q_up(proj, w["mla_q_norm_g"], wn, wr, wt, cos_t, sin_t, bsz, lp)
    kf, vf = _kv_up(proj, w["mla_kv_norm_g"], wk, wv, cos_t, sin_t, bsz, lp)
    o_b, yb_in, lse = _attn_fwd(qf, kf, vf, proj, bsz, lp)
    y_a, y_b, dh2, loss, d_final_g = _mid_fwd(ya_in, yb_in, proj, hp, target, w["gla_proj"], w["mla_proj"],
                                              w["w_out"], w["final_norm_g"], bsz, lp)
    d_ya, d_yb, dproj, d_w_out, d_gla_proj, d_mla_proj = _mid_bwd(
        dh2, y_a, y_b, proj, ya_in, yb_in, w["w_out"], w["gla_proj"], w["mla_proj"])
    dproj, d_gate, d_gla_norm = _gla_bwd(proj, gw_pad, w["gla_gate_b"], w["gla_norm_g"], o_raw, s_all, d_ya, dproj,
                                         bsz, lp)
    d_lr, d_gw_pad, d_gate_b = _gate_bwd(d_gate, proj, gw_pad)
    d_o, dproj, delta = _attn_bwd_pre(d_yb, proj, o_b, dproj, bsz, lp)
    dqf, dkf, dvf = _attn_bwd(qf, kf, vf, d_o, lse, delta, bsz, lp)
    dproj, d_wn, d_wr, d_wt, d_qn = _q_up_bwd(dqf, proj, w["mla_q_norm_g"], wn, wr, wt, cos_t, sin_t, dproj,
                                              bsz, lp)
    dproj, d_wk, d_wv, d_kvn = _kv_up_bwd(dkf, dvf, proj, w["mla_kv_norm_g"], wk, wv, cos_t, sin_t, d_lr, dproj,
                                          bsz, lp)

    d_rope = (d_wr.reshape(MLA_QR, MLA_HEADS, LANE)[:, :, :MLA_ROPE]
              + _unrot_cols(d_wt.reshape(MLA_QR, MLA_HEADS, LANE)[:, :, :MLA_ROPE]))
    d_uq = jnp.concatenate([d_wn.reshape(MLA_QR, MLA_HEADS, LANE), d_rope], axis=-1).reshape(MLA_QR, MLA_HEADS * MLA_QK)
    d_ukv = jnp.concatenate([d_wk.reshape(MLA_KVR, MLA_HEADS, LANE), d_wv.reshape(MLA_KVR, MLA_HEADS, LANE)],
                            axis=-1).reshape(MLA_KVR, MLA_HEADS * (MLA_NOPE + MLA_DV))
    mats = dict(gla_gate_w=d_gw_pad[:GLA_RANK], gla_proj=d_gla_proj, mla_w_uq=d_uq, mla_w_ukv=d_ukv,
                mla_proj=d_mla_proj, w_out=d_w_out)
    packed = _pad_rows(jnp.concatenate([_split8(mats[n], axis).reshape(N_DEV, -1) for n, _, axis in PACKED], axis=1),
                       PACK_ROWS)
    d_w_ext, packed_parts = _dw_in(u, dproj, _bf(packed))
    d_hp, d_norm_g, w_in_parts = _dx_in(dproj, w_ext, hp, dh2, w["norm_g"], _bf(_split8(_w_in_grad(d_w_ext), 1)))
    d_hp3 = d_hp.reshape(bsz, lp, D_MODEL)
    small = dict(meta_tokens=_meta_grad(d_hp3), norm_g=d_norm_g, gla_gate_b=d_gate_b, gla_norm_g=d_gla_norm,
                 mla_q_norm_g=d_qn, mla_kv_norm_g=d_kvn, final_norm_g=d_final_g)
    return loss, d_hp3[:, X0:, :], w_in_parts, packed_parts, small


PACKED = (("gla_gate_w", (GLA_RANK, GLA_KW // N_DEV), 1),
          ("gla_proj", (D_MODEL // N_DEV, D_MODEL), 0), ("mla_w_uq", (MLA_QR, MLA_HEADS * MLA_QK // N_DEV), 1),
          ("mla_w_ukv", (MLA_KVR, MLA_HEADS * (MLA_NOPE + MLA_DV) // N_DEV), 1),
          ("mla_proj", (D_MODEL // N_DEV, D_MODEL), 0), ("w_out", (D_MODEL // N_DEV, D_MODEL), 0))
REPLICATED = (("norm_g", D_MODEL), ("gla_gate_b", GLA_KW), ("gla_norm_g", GLA_DV), ("mla_q_norm_g", MLA_QR),
              ("mla_kv_norm_g", MLA_KVR), ("final_norm_g", D_MODEL))
PACK_ROWS = 3744
PACK_BLOCK = 1248
GATHER_ROWS = 3760
SMALL_ROWS = 48
LOSS_ROW = N_META + 25
W_IN_BLOCK = 128


def _all_gather(shards):
    n_arr = len(shards)

    def body(*refs):
        x_refs, out_refs = refs[:n_arr], refs[n_arr:2 * n_arr]
        send_sems, recv_sems, local_sems = refs[2 * n_arr:]
        x, y, c = _my_place()
        me, sibling = (x, y, c), (x, y, 1 - c)
        chips = [(1 - x, y), (x, 1 - y), (1 - x, 1 - y)]

        def copy(a, k, block, to, from_input=False):
            slab = out_refs[a].at[4 * block[0] + 2 * block[1] + block[2]]
            return pltpu.make_async_remote_copy(
                src_ref=x_refs[a] if from_input else slab, dst_ref=slab,
                send_sem=send_sems.at[7 * a + k], recv_sem=recv_sems.at[7 * a + k], device_id=to,
                device_id_type=MESH_ID)

        arrays = range(n_arr)
        mine = [pltpu.make_async_copy(x_refs[a], out_refs[a].at[4 * x + 2 * y + c], local_sems.at[a]) for a in arrays]
        for cp in mine:
            cp.start()
        first = [copy(a, 0, me, sibling, True) for a in arrays]
        first += [copy(a, 1 + j, me, (*chip, c), True) for j, chip in enumerate(chips) for a in arrays]
        for cp in first:
            cp.start()
        passed = []
        for j, chip in enumerate(chips):
            for a in arrays:
                copy(a, 1 + j, (*chip, c), me).wait_recv()
                passed.append(copy(a, 4 + j, (*chip, c), sibling))
                passed[-1].start()
        for a in arrays:
            copy(a, 0, sibling, me).wait_recv()
        for j, chip in enumerate(chips):
            for a in arrays:
                copy(a, 4 + j, (*chip, 1 - c), me).wait_recv()
        for cp in first + passed:
            cp.wait_send()
        for cp in mine:
            cp.wait()

    anyspec = pl.BlockSpec(memory_space=pl.ANY)
    return pl.pallas_call(
        body, name="weights_all_gather",
        out_shape=[jax.ShapeDtypeStruct((N_DEV,) + s.shape, s.dtype) for s in shards],
        in_specs=[anyspec] * n_arr, out_specs=[anyspec] * n_arr,
        scratch_shapes=[pltpu.SemaphoreType.DMA((7 * n_arr,)), pltpu.SemaphoreType.DMA((7 * n_arr,)),
                        pltpu.SemaphoreType.DMA((n_arr,))],
    )(*shards)


def _small_exchange(slabs):
    def body(g_ref, recv_ref, send_sems, recv_sems, local_sem):
        _exchange(g_ref, recv_ref, send_sems, recv_sems, local_sem, True)
        _exchange(g_ref, recv_ref, send_sems, recv_sems, local_sem, False)

    vmem = pl.BlockSpec(memory_space=pltpu.VMEM)
    return pl.pallas_call(
        body, name="small_exchange", out_shape=jax.ShapeDtypeStruct(slabs.shape, slabs.dtype),
        in_specs=[vmem], out_specs=vmem, scratch_shapes=EXCHANGE_SEMS,
    )(slabs)


def _adamw(parts, w, m, v, block_rows, name):
    rows, cols = w.shape

    def body(p_ref, w_ref, m_ref, v_ref, g_out, d_out, m_out, v_out):
        g = p_ref[0].astype(F32)
        for s in range(1, N_DEV):
            g = g + p_ref[s].astype(F32)
        m_new = ADAM_B1 * m_ref[...] + (1.0 - ADAM_B1) * g
        v_new = ADAM_B2 * v_ref[...] + (1.0 - ADAM_B2) * (g * g)
        m_hat = m_new / (1.0 - ADAM_B1 ** ADAM_STEP)
        v_hat = v_new / (1.0 - ADAM_B2 ** ADAM_STEP)
        g_out[...] = g
        d_out[...] = -ADAM_LR * (m_hat / (jnp.sqrt(v_hat) + ADAM_EPS) + ADAM_WD * w_ref[...])
        m_out[...] = m_new
        v_out[...] = v_new

    spec = pl.BlockSpec((block_rows, cols), lambda i: (i, 0))
    return pl.pallas_call(
        body, name=name, grid=(rows // block_rows,),
        in_specs=[pl.BlockSpec((N_DEV, block_rows, cols), lambda i: (0, i, 0)), spec, spec, spec],
        out_specs=[spec] * 4, out_shape=[jax.ShapeDtypeStruct((rows, cols), F32)] * 4,
        compiler_params=_cp(("parallel",), 48),
    )(parts, w, m, v)


def _pad_rows(flat, rows):
    pad = rows * LANE - flat.shape[-1]
    flat = jnp.pad(flat, [(0, 0)] * (flat.ndim - 1) + [(0, pad)])
    return flat.reshape(flat.shape[:-1] + (rows, LANE))


def _pack_shards(shards):
    return _pad_rows(jnp.concatenate([shards[n].reshape(-1) for n, _, _ in PACKED]), PACK_ROWS)


def _unpack_shards(packed):
    flat, out, off = packed.reshape(-1), {}, 0
    for n, shape, _ in PACKED:
        size = shape[0] * shape[1]
        out[n] = flat[off:off + size].reshape(shape)
        off += size
    return out


def _split8(full, axis):
    r, c = full.shape
    if axis == 0:
        return full.reshape(N_DEV, r // N_DEV, c)
    return full.reshape(r, N_DEV, c // N_DEV).transpose(1, 0, 2)


def _join8(shards, axis):
    _, r, c = shards.shape
    if axis == 0:
        return shards.reshape(N_DEV * r, c)
    return shards.transpose(1, 0, 2).reshape(r, N_DEV * c)


def _pack_small(meta_shard, vals, loss_row):
    rows = jnp.concatenate([vals[n].reshape(-1, LANE) for n, _ in REPLICATED] + [loss_row], axis=0)
    rows = jnp.pad(rows, ((0, SMALL_ROWS - N_META - rows.shape[0]), (0, 0)))
    return jnp.concatenate([meta_shard, jnp.broadcast_to(rows, meta_shard.shape[:-2] + rows.shape)], axis=-2)


def _unpack_small(packed):
    out, off = {"meta_tokens": packed[:N_META]}, N_META
    for n, size in REPLICATED:
        out[n] = packed[off:off + size // LANE].reshape(1, size)
        off += size // LANE
    return out


def kernel(x, meta_tokens, norm_g, w_in, gla_gate_w, gla_gate_b, gla_norm_g, gla_proj, mla_q_norm_g, mla_w_uq, mla_kv_norm_g, mla_w_ukv, mla_proj, w_out, final_norm_g, loss_target, m_meta_tokens, m_norm_g, m_w_in, m_gla_gate_w, m_gla_gate_b, m_gla_norm_g, m_gla_proj, m_mla_q_norm_g, m_mla_w_uq, m_mla_kv_norm_g, m_mla_w_ukv, m_mla_proj, m_w_out, m_final_norm_g, v_meta_tokens, v_norm_g, v_w_in, v_gla_gate_w, v_gla_gate_b, v_gla_norm_g, v_gla_proj, v_mla_q_norm_g, v_mla_w_uq, v_mla_kv_norm_g, v_mla_w_ukv, v_mla_proj, v_w_out, v_final_norm_g):
    given = dict(meta_tokens=meta_tokens, norm_g=norm_g, w_in=w_in, gla_gate_w=gla_gate_w, gla_gate_b=gla_gate_b,
                 gla_norm_g=gla_norm_g, gla_proj=gla_proj, mla_q_norm_g=mla_q_norm_g, mla_w_uq=mla_w_uq,
                 mla_kv_norm_g=mla_kv_norm_g, mla_w_ukv=mla_w_ukv, mla_proj=mla_proj, w_out=w_out,
                 final_norm_g=final_norm_g)
    mom_m = dict(meta_tokens=m_meta_tokens, norm_g=m_norm_g, w_in=m_w_in, gla_gate_w=m_gla_gate_w,
                 gla_gate_b=m_gla_gate_b, gla_norm_g=m_gla_norm_g, gla_proj=m_gla_proj, mla_q_norm_g=m_mla_q_norm_g,
                 mla_w_uq=m_mla_w_uq, mla_kv_norm_g=m_mla_kv_norm_g, mla_w_ukv=m_mla_w_ukv, mla_proj=m_mla_proj,
                 w_out=m_w_out, final_norm_g=m_final_norm_g)
    mom_v = dict(meta_tokens=v_meta_tokens, norm_g=v_norm_g, w_in=v_w_in, gla_gate_w=v_gla_gate_w,
                 gla_gate_b=v_gla_gate_b, gla_norm_g=v_gla_norm_g, gla_proj=v_gla_proj, mla_q_norm_g=v_mla_q_norm_g,
                 mla_w_uq=v_mla_w_uq, mla_kv_norm_g=v_mla_kv_norm_g, mla_w_ukv=v_mla_w_ukv, mla_proj=v_mla_proj,
                 w_out=v_w_out, final_norm_g=v_final_norm_g)
    shapes = {n: a.shape for n, a in given.items()}
    shard2d = {n: s for n, s, _ in PACKED}
    shard2d["w_in"] = (D_MODEL, W_IN_SHARD)
    shard2d["meta_tokens"] = (N_META, LANE)

    def as2d(tree):
        out = {n: tree[n].reshape(shard2d[n]) for n in shard2d}
        out.update({n: tree[n].reshape(1, size) for n, size in REPLICATED})
        return out

    w_loc, m_loc, v_loc = as2d(given), as2d(mom_m), as2d(mom_v)

    meta_bits = lax.bitcast_convert_type(w_loc["meta_tokens"], BF16).reshape(-1)
    flat = jnp.concatenate([w_loc[n].astype(BF16).reshape(-1) for n, _, _ in PACKED] + [meta_bits])
    w_in_all, packed_all = _all_gather([w_loc["w_in"].astype(BF16), _pad_rows(flat, GATHER_ROWS)])
    packed_all = packed_all.reshape(N_DEV, -1)
    full, off = {"w_in": w_in_all}, 0
    for n, shape, axis in PACKED:
        size = shape[0] * shape[1]
        full[n] = _join8(packed_all[:, off:off + size].reshape((N_DEV,) + shape), axis)
        off += size
    meta8 = lax.bitcast_convert_type(packed_all[:, off:off + 2 * N_META * LANE].reshape(N_DEV, N_META, LANE, 2), F32)
    full["meta_tokens"] = _join8(meta8, 1)
    for n, _ in REPLICATED:
        full[n] = w_loc[n]

    loss_part, grad_x, w_in_parts, packed_parts, small = _local_step(x, loss_target, full)
    small_all = _small_exchange(_pack_small(_split8(small["meta_tokens"], 1), small,
                                            jnp.broadcast_to(loss_part[:, :1], (1, LANE))))

    g_w, d_w, m_w, v_w = _adamw(w_in_parts, w_loc["w_in"], m_loc["w_in"], v_loc["w_in"], W_IN_BLOCK, "adamw_w_in")
    g_p, d_p, m_p, v_p = _adamw(packed_parts, _pack_shards(w_loc), _pack_shards(m_loc), _pack_shards(v_loc),
                                PACK_BLOCK, "adamw_packed")
    zero_row = jnp.zeros((1, LANE), F32)
    g_s, d_s, m_s, v_s = _adamw(small_all, *(_pack_small(t["meta_tokens"], t, zero_row) for t in (w_loc, m_loc, v_loc)),
                                SMALL_ROWS, "adamw_small")
    loss = g_s[LOSS_ROW, 0]

    order = ["meta_tokens", "norm_g", "w_in", "gla_gate_w", "gla_gate_b", "gla_norm_g", "gla_proj", "mla_q_norm_g",
             "mla_w_uq", "mla_kv_norm_g", "mla_w_ukv", "mla_proj", "w_out", "final_norm_g"]
    result = [loss, grad_x]
    for w_in_out, packed_sh, packed_sm in ((g_w, g_p, g_s), (d_w, d_p, d_s), (m_w, m_p, m_s), (v_w, v_p, v_s)):
        tree = _unpack_shards(packed_sh)
        tree.update(_unpack_small(packed_sm))
        tree["w_in"] = w_in_out
        result += [tree[n].reshape(shapes[n]) for n in order]
    return tuple(result)
```

```python
import jax
import jax.numpy as jnp
from jax import lax
from jax.experimental import pallas as pl
from jax.experimental.pallas import tpu as pltpu

F32 = jnp.float32
BF16 = jnp.bfloat16

D_MODEL = 1024
N_META = 16
EPS = 1e-6
FRONT = 48
X0 = FRONT + N_META
GLA_HEADS, GLA_DK, GLA_DV, GLA_RANK, GLA_CHUNK = 4, 128, 256, 16, 64
GLA_GATE_NORMALIZER = 16.0
GLA_KW = GLA_HEADS * GLA_DK
GLA_VW = GLA_HEADS * GLA_DV
MLA_HEADS, MLA_NOPE, MLA_ROPE, MLA_DV, MLA_QR, MLA_KVR = 8, 128, 64, 128, 256, 128
MLA_QK = MLA_NOPE + MLA_ROPE
ROPE_BASE = 10000.0
LANE = 128
QKW = 2 * LANE

C_V, C_Z, C_Q, C_K = 0, 1024, 2048, 2560
C_MZ = 3072
C_GG, C_GM = 4096, 5120
C_CKV, C_KR, C_KROT, C_LR = 6144, 6272, 6400, 6528
C_CQ = 6656
N_EXT = 6912
O_Q, O_K, O_V, O_LR, O_Z, O_CQ, O_CKV, O_KR, O_MZ, O_GG, O_GM, N_IN = (
    0, 512, 1024, 2048, 2064, 3088, 3344, 3472, 3536, 4560, 5584, 6608)

ADAM_LR, ADAM_B1, ADAM_B2, ADAM_EPS, ADAM_WD, ADAM_STEP = 0.001, 0.9, 0.999, 1e-08, 0.01, 10

N_DEV = 8
TOK = 192
ATT_BLOCK = 352
EXT_BLOCK = 1152


def _cp(sems=None, vmem_mb=None):
    kw = {}
    if sems is not None:
        kw["dimension_semantics"] = sems
    if vmem_mb is not None:
        kw["vmem_limit_bytes"] = vmem_mb * 1024 * 1024
    return pltpu.CompilerParams(**kw)


def _dot(a, b):
    return jnp.dot(a, b, preferred_element_type=F32)


def _dot_nt(a, b):
    return lax.dot_general(a, b, (((1,), (1,)), ((), ())), preferred_element_type=F32)


def _dot_tn(a, b):
    return lax.dot_general(a, b, (((0,), (0,)), ((), ())), preferred_element_type=F32)


def _sigmoid(x):
    return 1.0 / (1.0 + jnp.exp(-x))


def _bf(x):
    return x.astype(BF16)


def _big_tok(tp):
    return 4 * TOK if tp % (4 * TOK) == 0 else TOK


def _attn_block(lp):
    return ATT_BLOCK if lp % ATT_BLOCK == 0 else TOK


def _proj_in(hp, norm_g, w_ext):
    tp = hp.shape[0]
    tm, tn = _big_tok(tp), EXT_BLOCK

    def body(h_ref, g_ref, w_ref, u_ref, o_ref, u_scr):
        @pl.when(pl.program_id(1) == 0)
        def _():
            x = h_ref[...]
            r = lax.rsqrt(jnp.mean(x * x, axis=-1, keepdims=True) + EPS)
            u = _bf(x * r * g_ref[...])
            u_scr[...] = u
            u_ref[...] = u

        o_ref[...] = _bf(_dot(u_scr[...], w_ref[...]))

    return pl.pallas_call(
        body, name="proj_in", grid=(tp // tm, N_EXT // tn),
        in_specs=[pl.BlockSpec((tm, D_MODEL), lambda i, j: (i, 0)),
                  pl.BlockSpec((1, D_MODEL), lambda i, j: (0, 0)),
                  pl.BlockSpec((D_MODEL, tn), lambda i, j: (0, j))],
        out_specs=[pl.BlockSpec((tm, D_MODEL), lambda i, j: (i, 0)),
                   pl.BlockSpec((tm, tn), lambda i, j: (i, j))],
        out_shape=[jax.ShapeDtypeStruct((tp, D_MODEL), BF16), jax.ShapeDtypeStruct((tp, N_EXT), BF16)],
        scratch_shapes=[pltpu.VMEM((tm, D_MODEL), BF16)],
        compiler_params=_cp(("parallel", "arbitrary"), 48),
    )(hp, norm_g, w_ext)


GLA_GROUP = 3
GLA_ROWS = GLA_GROUP * GLA_CHUNK


def _gla_gates(q_ref, k_ref, lr_ref, gw_ref, gb_ref, rows, not_first):
    z = _dot(lr_ref[rows, :], gw_ref[...]) + gb_ref[...]
    logsig = jnp.minimum(z, 0.0) - jnp.log(1.0 + jnp.exp(-jnp.abs(z)))
    row = lax.broadcasted_iota(jnp.int32, (GLA_CHUNK, GLA_KW), 0)
    live = jnp.logical_or(not_first, row >= FRONT)
    g = jnp.where(live, logsig * (1.0 / GLA_GATE_NORMALIZER), 0.0)
    ri = lax.broadcasted_iota(jnp.int32, (GLA_CHUNK, GLA_CHUNK), 0)
    ci = lax.broadcasted_iota(jnp.int32, (GLA_CHUNK, GLA_CHUNK), 1)
    tril = ci <= ri
    b = jnp.dot(tril.astype(F32), g, precision=lax.Precision.HIGHEST, preferred_element_type=F32)
    bl = jnp.sum(jnp.where(row == GLA_CHUNK - 1, b, 0.0), axis=0, keepdims=True)
    eb, enb, elb, ebl = jnp.exp(b), jnp.exp(-b), jnp.exp(bl - b), jnp.exp(bl)
    q = q_ref[rows, :].astype(F32) * (GLA_DK ** -0.5)
    k = k_ref[rows, :].astype(F32)
    qe, ke, kl = q * eb, k * enb, k * elb
    return dict(z=z, live=live, tril=tril, row=row, eb=eb, enb=enb, elb=elb, ebl=ebl, qe=qe, ke=ke, kl=kl,
                qe_b=_bf(qe), ke_b=_bf(ke), kl_b=_bf(kl))


def _gla_in_specs(n_groups, rev):
    def rb(b, n):
        return b * n_groups + ((n_groups - 1 - n) if rev else n)

    return rb, [pl.BlockSpec((GLA_ROWS, GLA_KW), lambda b, n: (rb(b, n), C_Q // GLA_KW)),
                pl.BlockSpec((GLA_ROWS, GLA_KW), lambda b, n: (rb(b, n), C_K // GLA_KW)),
                pl.BlockSpec((GLA_ROWS, GLA_VW), lambda b, n: (rb(b, n), C_V // GLA_VW)),
                pl.BlockSpec((GLA_ROWS, GLA_VW), lambda b, n: (rb(b, n), C_Z // GLA_VW)),
                pl.BlockSpec((GLA_ROWS, LANE), lambda b, n: (rb(b, n), C_LR // LANE)),
                pl.BlockSpec((LANE, GLA_KW), lambda b, n: (0, 0)),
                pl.BlockSpec((1, GLA_KW), lambda b, n: (0, 0)),
                pl.BlockSpec((1, GLA_DV), lambda b, n: (0, 0))]


def _gla_fwd(proj, gw_pad, gate_b, gla_norm_g, bsz, lp):
    n_chunks = lp // GLA_CHUNK
    n_groups = n_chunks // GLA_GROUP
    tp = bsz * lp

    def body(q_ref, k_ref, v_ref, z_ref, lr_ref, gw_ref, gb_ref, gn_ref, oraw_ref, ya_ref, sall_ref, st_scr):
        grp = pl.program_id(1)

        @pl.when(grp == 0)
        def _():
            st_scr[...] = jnp.zeros_like(st_scr)

        chunks = [slice(j * GLA_CHUNK, (j + 1) * GLA_CHUNK) for j in range(GLA_GROUP)]
        cs = [_gla_gates(q_ref, k_ref, lr_ref, gw_ref, gb_ref, rows, True if j else grp > 0)
              for j, rows in enumerate(chunks)]
        gn = gn_ref[...]
        for h in range(GLA_HEADS):
            ks, vs = slice(h * GLA_DK, (h + 1) * GLA_DK), slice(h * GLA_DV, (h + 1) * GLA_DV)
            st = st_scr[h]
            for j, (rows, c) in enumerate(zip(chunks, cs)):
                sall_ref[0, j, h] = st
                v = v_ref[rows, vs]
                a = jnp.where(c["tril"], _dot_nt(c["qe_b"][:, ks], c["ke_b"][:, ks]), 0.0)
                o = _dot(_bf(a), v) + _dot_nt(c["qe_b"][:, ks], _bf(st))
                st = st * c["ebl"][:, ks] + _dot_tn(v, c["kl_b"][:, ks])
                oraw_ref[rows, vs] = o
                r = lax.rsqrt(jnp.mean(o * o, axis=-1, keepdims=True) + EPS)
                zg = z_ref[rows, vs].astype(F32)
                ya_ref[rows, vs] = _bf((o * r * gn) * (zg * _sigmoid(zg)))
            st_scr[h] = st

    rb, in_specs = _gla_in_specs(n_groups, False)
    return pl.pallas_call(
        body, name="gla_fwd", grid=(bsz, n_groups), in_specs=in_specs,
        out_specs=[pl.BlockSpec((GLA_ROWS, GLA_VW), lambda b, n: (rb(b, n), 0)),
                   pl.BlockSpec((GLA_ROWS, GLA_VW), lambda b, n: (rb(b, n), 0)),
                   pl.BlockSpec((1, GLA_GROUP, GLA_HEADS, GLA_DV, GLA_DK), lambda b, n: (b, n, 0, 0, 0))],
        out_shape=[jax.ShapeDtypeStruct((tp, GLA_VW), F32), jax.ShapeDtypeStruct((tp, GLA_VW), BF16),
                   jax.ShapeDtypeStruct((bsz, n_chunks, GLA_HEADS, GLA_DV, GLA_DK), F32)],
        scratch_shapes=[pltpu.VMEM((GLA_HEADS, GLA_DV, GLA_DK), F32)],
        compiler_params=_cp(("parallel", "arbitrary")),
    )(proj, proj, proj, proj, proj, gw_pad, gate_b, gla_norm_g)


def _gla_bwd(proj, gw_pad, gate_b, gla_norm_g, o_raw, s_all, d_ya, dproj, bsz, lp):
    n_chunks = lp // GLA_CHUNK
    n_groups = n_chunks // GLA_GROUP
    tp = bsz * lp

    def body(q_ref, k_ref, v_ref, z_ref, lr_ref, gw_ref, gb_ref, gn_ref, o_ref, s_ref, dya_ref, _,
             dp_ref, dz_ref, dgn_ref, dst_scr):
        dv_ref, dzg_ref = dp_ref.at[:, C_V:C_V + GLA_VW], dp_ref.at[:, C_Z:C_Z + GLA_VW]

        @pl.when(jnp.logical_and(pl.program_id(0) == 0, pl.program_id(1) == 0))
        def _():
            dgn_ref[...] = jnp.zeros_like(dgn_ref)

        @pl.when(pl.program_id(1) == 0)
        def _():
            dst_scr[...] = jnp.zeros_like(dst_scr)

        grp = n_groups - 1 - pl.program_id(1)
        chunks = [slice(j * GLA_CHUNK, (j + 1) * GLA_CHUNK) for j in range(GLA_GROUP)]
        cs = [_gla_gates(q_ref, k_ref, lr_ref, gw_ref, gb_ref, rows, True if j else grp > 0)
              for j, rows in enumerate(chunks)]
        gn = gn_ref[...]
        dgn = jnp.zeros((1, GLA_DV), F32)
        dqe_h, dke_h, dkl_h, dbl_h = ([[None] * GLA_HEADS for _ in chunks] for _ in range(4))
        for h in range(GLA_HEADS):
            ks, vs = slice(h * GLA_DK, (h + 1) * GLA_DK), slice(h * GLA_DV, (h + 1) * GLA_DV)
            dst = dst_scr[h]
            for j in reversed(range(GLA_GROUP)):
                rows, c = chunks[j], cs[j]
                v = v_ref[rows, vs]
                st = s_ref[0, j, h]
                o = o_ref[rows, vs]
                r = lax.rsqrt(jnp.mean(o * o, axis=-1, keepdims=True) + EPS)
                xh = o * r
                zg = z_ref[rows, vs].astype(F32)
                sg = _sigmoid(zg)
                dy = dya_ref[rows, vs].astype(F32)
                dzg_ref[rows, vs] = _bf(dy * (xh * gn) * (sg * (1.0 + zg * (1.0 - sg))))
                t = dy * (zg * sg)
                dgn += jnp.sum(t * xh, axis=0, keepdims=True)
                dxh = t * gn
                do_b = _bf(r * (dxh - xh * jnp.mean(dxh * xh, axis=-1, keepdims=True)))
                qe_b, ke_b, kl_b, dst_b = c["qe_b"][:, ks], c["ke_b"][:, ks], c["kl_b"][:, ks], _bf(dst)
                a = jnp.where(c["tril"], _dot_nt(qe_b, ke_b), 0.0)
                da_b = _bf(jnp.where(c["tril"], _dot_nt(do_b, v), 0.0))
                dqe_h[j][h] = _dot(da_b, ke_b) + _dot(do_b, _bf(st))
                dke_h[j][h] = _dot_tn(da_b, qe_b)
                dkl = _dot(v, dst_b)
                dkl_h[j][h] = dkl
                dv_ref[rows, vs] = _bf(_dot_tn(_bf(a), do_b) + _dot_nt(kl_b, dst_b))
                ddecay = jnp.sum(dst * st, axis=0, keepdims=True)
                dbl_h[j][h] = jnp.sum(dkl * c["kl"][:, ks], axis=0, keepdims=True) + ddecay * c["ebl"][:, ks]
                dst = dst * c["ebl"][:, ks] + _dot_tn(do_b, qe_b)
            dst_scr[h] = dst
        dgn_ref[...] += dgn
        ri = lax.broadcasted_iota(jnp.int32, (GLA_CHUNK, GLA_CHUNK), 0)
        ci = lax.broadcasted_iota(jnp.int32, (GLA_CHUNK, GLA_CHUNK), 1)
        triu = (ci >= ri).astype(F32)
        for j, (rows, c) in enumerate(zip(chunks, cs)):
            dqe, dke, dkl, dbl = (jnp.concatenate(p[j], axis=1) for p in (dqe_h, dke_h, dkl_h, dbl_h))
            db = dqe * c["qe"] - dke * c["ke"] - dkl * c["kl"] + jnp.where(c["row"] == GLA_CHUNK - 1, dbl, 0.0)
            dg = jnp.dot(triu, db, precision=lax.Precision.HIGHEST, preferred_element_type=F32)
            dg = jnp.where(c["live"], dg, 0.0)
            dz_ref[rows, :] = dg * (1.0 / GLA_GATE_NORMALIZER) * _sigmoid(-c["z"])
            dp_ref[rows, C_Q:C_Q + GLA_KW] = _bf(dqe * c["eb"] * (GLA_DK ** -0.5))
            dp_ref[rows, C_K:C_K + GLA_KW] = _bf(dke * c["enb"] + dkl * c["elb"])

    rb, in_specs = _gla_in_specs(n_groups, True)
    wide = pl.BlockSpec((GLA_ROWS, GLA_VW), lambda b, n: (rb(b, n), 0))
    group = C_MZ
    return pl.pallas_call(
        body, name="gla_bwd", grid=(bsz, n_groups),
        in_specs=in_specs + [wide, pl.BlockSpec((1, GLA_GROUP, GLA_HEADS, GLA_DV, GLA_DK),
                                                lambda b, n: (b, n_groups - 1 - n, 0, 0, 0)), wide,
                             pl.BlockSpec(memory_space=pl.ANY)],
        out_specs=[pl.BlockSpec((GLA_ROWS, group), lambda b, n: (rb(b, n), 0)),
                   pl.BlockSpec((GLA_ROWS, GLA_KW), lambda b, n: (rb(b, n), 0)),
                   pl.BlockSpec((1, GLA_DV), lambda b, n: (0, 0))],
        out_shape=[jax.ShapeDtypeStruct((tp, N_EXT), BF16), jax.ShapeDtypeStruct((tp, GLA_KW), F32),
                   jax.ShapeDtypeStruct((1, GLA_DV), F32)],
        input_output_aliases={11: 0},
        scratch_shapes=[pltpu.VMEM((GLA_HEADS, GLA_DV, GLA_DK), F32)],
        compiler_params=_cp(("arbitrary", "arbitrary")),
    )(proj, proj, proj, proj, proj, gw_pad, gate_b, gla_norm_g, o_raw, s_all, d_ya, dproj)


def _gate_bwd(dz, proj, gw_pad):
    tp = dz.shape[0]
    tm = _big_tok(tp)

    def body(dz_ref, lr_ref, gw_ref, dlr_ref, dgw_ref, dgb_ref):
        @pl.when(pl.program_id(0) == 0)
        def _():
            dgw_ref[...] = jnp.zeros_like(dgw_ref)
            dgb_ref[...] = jnp.zeros_like(dgb_ref)

        dz = dz_ref[...]
        dz_b = _bf(dz)
        dlr_ref[...] = _bf(_dot_nt(dz_b, gw_ref[...]))
        dgw_ref[...] += _dot_tn(lr_ref[...], dz_b)
        dgb_ref[...] += jnp.sum(dz, axis=0, keepdims=True)

    return pl.pallas_call(
        body, name="gate_bwd", grid=(tp // tm,),
        in_specs=[pl.BlockSpec((tm, GLA_KW), lambda i: (i, 0)),
                  pl.BlockSpec((tm, LANE), lambda i: (i, C_LR // LANE)),
                  pl.BlockSpec((LANE, GLA_KW), lambda i: (0, 0))],
        out_specs=[pl.BlockSpec((tm, LANE), lambda i: (i, 0)),
                   pl.BlockSpec((LANE, GLA_KW), lambda i: (0, 0)),
                   pl.BlockSpec((1, GLA_KW), lambda i: (0, 0))],
        out_shape=[jax.ShapeDtypeStruct((tp, LANE), BF16), jax.ShapeDtypeStruct((LANE, GLA_KW), F32),
                   jax.ShapeDtypeStruct((1, GLA_KW), F32)],
        compiler_params=_cp(("arbitrary",)),
    )(dz, proj, gw_pad)


def _rms_fwd(x):
    r = lax.rsqrt(jnp.mean(x * x, axis=-1, keepdims=True) + EPS)
    return x * r, r


def _rms_bwd(dy, xh, r, g):
    dxh = dy * g
    dx = r * (dxh - xh * jnp.mean(dxh * xh, axis=-1, keepdims=True))
    return dx, jnp.sum(dy * xh, axis=0, keepdims=True)


def _q_up(proj, q_norm_g, wn, wr, wt, cos_t, sin_t, bsz, lp):
    tp = bsz * lp
    tok = _attn_block(lp)
    nb = lp // tok

    def body(cq_ref, g_ref, wn_ref, wr_ref, wt_ref, cos_ref, sin_ref, q_ref):
        xh, _ = _rms_fwd(cq_ref[...].astype(F32))
        cqn = _bf(xh * g_ref[...])
        nope = _dot(cqn, wn_ref[...])
        rope = _dot(cqn, wr_ref[...])
        rot = _dot(cqn, wt_ref[...])
        cos, sin = cos_ref[...], sin_ref[...]
        one = (lax.broadcasted_iota(jnp.int32, (tok, LANE), 1) == BIAS_LANE).astype(F32)
        for h in range(MLA_HEADS):
            sl = slice(h * LANE, (h + 1) * LANE)
            q_ref[:, h * QKW:h * QKW + LANE] = _bf(nope[:, sl])
            q_ref[:, h * QKW + LANE:(h + 1) * QKW] = _bf(rope[:, sl] * cos + rot[:, sl] * sin + one)

    wspec = pl.BlockSpec((MLA_QR, MLA_HEADS * LANE), lambda b, i: (0, 0))
    tspec = pl.BlockSpec((tok, LANE), lambda b, i: (i, 0))
    return pl.pallas_call(
        body, name="mla_q_up", grid=(bsz, nb),
        in_specs=[pl.BlockSpec((tok, MLA_QR), lambda b, i: (b * nb + i, C_CQ // MLA_QR)),
                  pl.BlockSpec((1, MLA_QR), lambda b, i: (0, 0)), wspec, wspec, wspec, tspec, tspec],
        out_specs=pl.BlockSpec((tok, MLA_HEADS * QKW), lambda b, i: (b * nb + i, 0)),
        out_shape=jax.ShapeDtypeStruct((tp, MLA_HEADS * QKW), BF16),
        compiler_params=_cp(("parallel", "parallel")),
    )(proj, q_norm_g, wn, wr, wt, cos_t, sin_t)


def _kv_up(proj, kv_norm_g, wk, wv, cos_t, sin_t, bsz, lp):
    tp = bsz * lp
    tok = _attn_block(lp)
    nb = lp // tok

    def body(ckv_ref, kr_ref, krot_ref, g_ref, wk_ref, wv_ref, cos_ref, sin_ref, k_ref, v_ref):
        xh, _ = _rms_fwd(ckv_ref[...].astype(F32))
        cn = _bf(xh * g_ref[...])
        kn = _dot(cn, wk_ref[...])
        v_ref[...] = _bf(_dot(cn, wv_ref[...]))
        pos = pl.program_id(1) * tok + lax.broadcasted_iota(jnp.int32, (tok, LANE), 0)
        lane = lax.broadcasted_iota(jnp.int32, (tok, LANE), 1)
        bias = jnp.where(jnp.logical_and(lane == BIAS_LANE, pos < FRONT), KEY_BIAS, 0.0)
        kr = _bf(kr_ref[...].astype(F32) * cos_ref[...] + krot_ref[...].astype(F32) * sin_ref[...] + bias)
        for h in range(MLA_HEADS):
            k_ref[:, h * QKW:h * QKW + LANE] = _bf(kn[:, h * LANE:(h + 1) * LANE])
            k_ref[:, h * QKW + LANE:(h + 1) * QKW] = kr

    wspec = pl.BlockSpec((MLA_KVR, MLA_HEADS * LANE), lambda b, i: (0, 0))
    tspec = pl.BlockSpec((tok, LANE), lambda b, i: (i, 0))
    return pl.pallas_call(
        body, name="mla_kv_up", grid=(bsz, nb),
        in_specs=[pl.BlockSpec((tok, LANE), lambda b, i: (b * nb + i, C_CKV // LANE)),
                  pl.BlockSpec((tok, LANE), lambda b, i: (b * nb + i, C_KR // LANE)),
                  pl.BlockSpec((tok, LANE), lambda b, i: (b * nb + i, C_KROT // LANE)),
                  pl.BlockSpec((1, MLA_KVR), lambda b, i: (0, 0)), wspec, wspec, tspec, tspec],
        out_specs=[pl.BlockSpec((tok, MLA_HEADS * QKW), lambda b, i: (b * nb + i, 0)),
                   pl.BlockSpec((tok, MLA_HEADS * LANE), lambda b, i: (b * nb + i, 0))],
        out_shape=[jax.ShapeDtypeStruct((tp, MLA_HEADS * QKW), BF16),
                   jax.ShapeDtypeStruct((tp, MLA_HEADS * LANE), BF16)],
        compiler_params=_cp(("parallel", "parallel")),
    )(proj, proj, proj, kv_norm_g, wk, wv, cos_t, sin_t)


ATT_SCALE = MLA_QK ** -0.5


KEY_BIAS = -1e30
BIAS_LANE = MLA_ROPE
NEG = 2 * KEY_BIAS
LOG2E = 1.4426950408889634
EXP2_SCALE = ATT_SCALE * LOG2E


def _causal_fill(s, r0, fill):
    tq, kmax = s.shape
    a = r0 // LANE * LANE
    mask = (a + lax.broadcasted_iota(jnp.int32, (tq, kmax - a), 1)
            <= r0 + lax.broadcasted_iota(jnp.int32, (tq, kmax - a), 0))
    right = jnp.where(mask, s[:, a:], fill)
    return jnp.concatenate([s[:, :a], right], axis=1) if a else right


def _attn_fwd(qf, kf, vf, proj, bsz, lp):
    tp = bsz * lp
    tq = _attn_block(lp)

    def body(q_ref, k_ref, v_ref, mz_ref, ob_ref, yb_ref, lse_ref):
        for r0 in range(0, lp, tq):
            rows, kmax = slice(r0, r0 + tq), r0 + tq
            s = _causal_fill(_dot_nt(q_ref[rows, :], k_ref[0:kmax, :]), r0, NEG)
            m = jnp.max(s, axis=-1, keepdims=True)
            p = jnp.exp2((s - m) * EXP2_SCALE)
            l = jnp.sum(p, axis=-1, keepdims=True)
            o = _dot(_bf(p), v_ref[0:kmax, :]) / l
            ob_ref[rows, :] = _bf(o)
            mz = mz_ref[rows, :].astype(F32)
            yb_ref[rows, :] = _bf(o * (mz * _sigmoid(mz)))
            lse_ref[0, 0, rows, :] = jnp.broadcast_to(m * EXP2_SCALE + jnp.log2(l), (tq, LANE))

    head = lambda off: pl.BlockSpec((lp, MLA_DV), lambda b, h: (b, off + h))
    return pl.pallas_call(
        body, name="mla_attn_fwd", grid=(bsz, MLA_HEADS),
        in_specs=[pl.BlockSpec((lp, QKW), lambda b, h: (b, h)), pl.BlockSpec((lp, QKW), lambda b, h: (b, h)),
                  head(0), head(C_MZ // MLA_DV)],
        out_specs=[head(0), head(0), pl.BlockSpec((1, 1, lp, LANE), lambda b, h: (b, h, 0, 0))],
        out_shape=[jax.ShapeDtypeStruct((tp, MLA_HEADS * MLA_DV), BF16),
                   jax.ShapeDtypeStruct((tp, MLA_HEADS * MLA_DV), BF16),
                   jax.ShapeDtypeStruct((bsz, MLA_HEADS, lp, LANE), F32)],
        compiler_params=_cp(("parallel", "parallel"), 56),
    )(qf, kf, vf, proj)


def _attn_bwd_pre(d_yb, proj, o_b, dproj, bsz, lp):
    tp = bsz * lp
    tok = _attn_block(lp)
    nb = lp // tok
    w = MLA_HEADS * MLA_DV

    def body(dy_ref, mz_ref, o_ref, _, do_ref, dmz_ref, dl_ref):
        dy = dy_ref[...].astype(F32)
        mz = mz_ref[...].astype(F32)
        o = o_ref[...].astype(F32)
        s = _sigmoid(mz)
        do = _bf(dy * (mz * s))
        do_ref[...] = do
        dmz_ref[...] = _bf(dy * o * (s * (1.0 + mz * (1.0 - s))))
        prod = do.astype(F32) * o
        for h in range(MLA_HEADS):
            dl = jnp.sum(prod[:, h * MLA_DV:(h + 1) * MLA_DV], axis=-1, keepdims=True)
            dl_ref[0, h] = jnp.broadcast_to(dl, (tok, LANE))

    return pl.pallas_call(
        body, name="mla_attn_bwd_pre", grid=(bsz, nb),
        in_specs=[pl.BlockSpec((tok, w), lambda b, i: (b * nb + i, 0)),
                  pl.BlockSpec((tok, w), lambda b, i: (b * nb + i, C_MZ // w)),
                  pl.BlockSpec((tok, w), lambda b, i: (b * nb + i, 0)), pl.BlockSpec(memory_space=pl.ANY)],
        out_specs=[pl.BlockSpec((tok, w), lambda b, i: (b * nb + i, 0)),
                   pl.BlockSpec((tok, w), lambda b, i: (b * nb + i, C_MZ // w)),
                   pl.BlockSpec((1, MLA_HEADS, tok, LANE), lambda b, i: (b, 0, i, 0))],
        out_shape=[jax.ShapeDtypeStruct((tp, w), BF16), jax.ShapeDtypeStruct((tp, N_EXT), BF16),
                   jax.ShapeDtypeStruct((bsz, MLA_HEADS, lp, LANE), F32)],
        input_output_aliases={3: 1},
        compiler_params=_cp(("parallel", "parallel")),
    )(d_yb, proj, o_b, dproj)


def _attn_bwd(qf, kf, vf, d_o, lse, delta, bsz, lp):
    tp = bsz * lp
    tq = _attn_block(lp)

    def body(q_ref, k_ref, v_ref, do_ref, lse_ref, dl_ref, dq_ref, dk_ref, dv_ref, dk_acc, dv_acc):
        dk_acc[...] = jnp.zeros_like(dk_acc)
        dv_acc[...] = jnp.zeros_like(dv_acc)
        for r0 in range(0, lp, tq):
            rows, kmax = slice(r0, r0 + tq), r0 + tq
            q, do = q_ref[rows, :], do_ref[rows, :]
            k, v = k_ref[0:kmax, :], v_ref[0:kmax, :]
            p = jnp.exp2(_dot_nt(q, k) * EXP2_SCALE - lse_ref[0, 0, rows, :][:, :1])
            p = _causal_fill(p, r0, 0.0)
            ds = _bf(p * (_dot_nt(do, v) - dl_ref[0, 0, rows, :][:, :1]))
            dq_ref[rows, :] = _bf(_dot(ds, k) * ATT_SCALE)
            dk_acc[0:kmax, :] += _dot_tn(ds, q)
            dv_acc[0:kmax, :] += _dot_tn(_bf(p), do)
        dk_ref[...] = _bf(dk_acc[...] * ATT_SCALE)
        dv_ref[...] = _bf(dv_acc[...])

    wide = pl.BlockSpec((lp, QKW), lambda b, h: (b, h))
    narrow = pl.BlockSpec((lp, MLA_DV), lambda b, h: (b, h))
    stat = pl.BlockSpec((1, 1, lp, LANE), lambda b, h: (b, h, 0, 0))
    return pl.pallas_call(
        body, name="mla_attn_bwd", grid=(bsz, MLA_HEADS),
        in_specs=[wide, wide, narrow, narrow, stat, stat], out_specs=[wide, wide, narrow],
        out_shape=[jax.ShapeDtypeStruct((tp, MLA_HEADS * QKW), BF16), jax.ShapeDtypeStruct((tp, MLA_HEADS * QKW), BF16),
                   jax.ShapeDtypeStruct((tp, MLA_HEADS * MLA_DV), BF16)],
        scratch_shapes=[pltpu.VMEM((lp, QKW), F32), pltpu.VMEM((lp, MLA_DV), F32)],
        compiler_params=_cp(("parallel", "parallel"), 56),
    )(qf, kf, vf, d_o, lse, delta)


def _q_up_bwd(dqf, proj, q_norm_g, wn, wr, wt, cos_t, sin_t, dproj, bsz, lp):
    tp = bsz * lp
    tok = _attn_block(lp)
    nb = lp // tok
    hw = MLA_HEADS * LANE

    def body(dq_ref, cq_ref, g_ref, wn_ref, wr_ref, wt_ref, cos_ref, sin_ref, _,
             dcq_ref, dwn_ref, dwr_ref, dwt_ref, dg_ref):
        @pl.when(jnp.logical_and(pl.program_id(0) == 0, pl.program_id(1) == 0))
        def _():
            for r in (dwn_ref, dwr_ref, dwt_ref, dg_ref):
                r[...] = jnp.zeros_like(r)

        g = g_ref[...]
        xh, r = _rms_fwd(cq_ref[...].astype(F32))
        cqn = _bf(xh * g)
        cos, sin = cos_ref[...], sin_ref[...]
        dcqn = jnp.zeros((tok, MLA_QR), F32)
        for h in range(MLA_HEADS):
            sl = slice(h * LANE, (h + 1) * LANE)
            dn = dq_ref[:, h * QKW:h * QKW + LANE]
            dr = dq_ref[:, h * QKW + LANE:(h + 1) * QKW].astype(F32)
            dr_c, dr_s = _bf(dr * cos), _bf(dr * sin)
            dcqn += _dot_nt(dn, wn_ref[:, sl]) + _dot_nt(dr_c, wr_ref[:, sl]) + _dot_nt(dr_s, wt_ref[:, sl])
            dwn_ref[:, sl] += _dot_tn(cqn, dn)
            dwr_ref[:, sl] += _dot_tn(cqn, dr_c)
            dwt_ref[:, sl] += _dot_tn(cqn, dr_s)
        dx, dg = _rms_bwd(dcqn, xh, r, g)
        dcq_ref[...] = _bf(dx)
        dg_ref[...] += dg

    aspec = pl.BlockSpec((MLA_QR, hw), lambda b, i: (0, 0))
    tspec = pl.BlockSpec((tok, LANE), lambda b, i: (i, 0))
    return pl.pallas_call(
        body, name="mla_q_up_bwd", grid=(bsz, nb),
        in_specs=[pl.BlockSpec((tok, MLA_HEADS * QKW), lambda b, i: (b * nb + i, 0)),
                  pl.BlockSpec((tok, MLA_QR), lambda b, i: (b * nb + i, C_CQ // MLA_QR)),
                  pl.BlockSpec((1, MLA_QR), lambda b, i: (0, 0)), aspec, aspec, aspec, tspec, tspec,
                  pl.BlockSpec(memory_space=pl.ANY)],
        out_specs=[pl.BlockSpec((tok, MLA_QR), lambda b, i: (b * nb + i, C_CQ // MLA_QR)), aspec, aspec, aspec,
                   pl.BlockSpec((1, MLA_QR), lambda b, i: (0, 0))],
        out_shape=[jax.ShapeDtypeStruct((tp, N_EXT), BF16)] + [jax.ShapeDtypeStruct((MLA_QR, hw), F32)] * 3
        + [jax.ShapeDtypeStruct((1, MLA_QR), F32)],
        input_output_aliases={8: 0},
        compiler_params=_cp(("arbitrary", "arbitrary")),
    )(dqf, proj, q_norm_g, wn, wr, wt, cos_t, sin_t, dproj)


def _kv_up_bwd(dkf, dvf, proj, kv_norm_g, wk, wv, cos_t, sin_t, d_lr, dproj, bsz, lp):
    tp = bsz * lp
    tok = _attn_block(lp)
    nb = lp // tok
    hw = MLA_HEADS * LANE

    def body(dk_ref, dv_ref, ckv_ref, g_ref, wk_ref, wv_ref, cos_ref, sin_ref, dlr_ref, _,
             dp_ref, dwk_ref, dwv_ref, dg_ref):
        dckv_ref, dkr_ref, dkrot_ref = (dp_ref.at[:, j * LANE:(j + 1) * LANE] for j in range(3))
        dp_ref[:, 3 * LANE:] = dlr_ref[...]
        @pl.when(jnp.logical_and(pl.program_id(0) == 0, pl.program_id(1) == 0))
        def _():
            for r in (dwk_ref, dwv_ref, dg_ref):
                r[...] = jnp.zeros_like(r)

        g = g_ref[...]
        xh, r = _rms_fwd(ckv_ref[...].astype(F32))
        cn = _bf(xh * g)
        dv = dv_ref[...]
        dcn = _dot_nt(dv, wv_ref[...])
        dwv_ref[...] += _dot_tn(cn, dv)
        drope = jnp.zeros((tok, LANE), F32)
        for h in range(MLA_HEADS):
            sl = slice(h * LANE, (h + 1) * LANE)
            dn = dk_ref[:, h * QKW:h * QKW + LANE]
            drope += dk_ref[:, h * QKW + LANE:(h + 1) * QKW].astype(F32)
            dcn += _dot_nt(dn, wk_ref[:, sl])
            dwk_ref[:, sl] += _dot_tn(cn, dn)
        dkr_ref[...] = _bf(drope * cos_ref[...])
        dkrot_ref[...] = _bf(drope * sin_ref[...])
        dx, dg = _rms_bwd(dcn, xh, r, g)
        dckv_ref[...] = _bf(dx)
        dg_ref[...] += dg

    aspec = pl.BlockSpec((MLA_KVR, hw), lambda b, i: (0, 0))
    tspec = pl.BlockSpec((tok, LANE), lambda b, i: (i, 0))
    ospec = pl.BlockSpec((tok, LANE), lambda b, i: (b * nb + i, 0))
    return pl.pallas_call(
        body, name="mla_kv_up_bwd", grid=(bsz, nb),
        in_specs=[pl.BlockSpec((tok, MLA_HEADS * QKW), lambda b, i: (b * nb + i, 0)),
                  pl.BlockSpec((tok, hw), lambda b, i: (b * nb + i, 0)),
                  pl.BlockSpec((tok, LANE), lambda b, i: (b * nb + i, C_CKV // LANE)),
                  pl.BlockSpec((1, MLA_KVR), lambda b, i: (0, 0)), aspec, aspec, tspec, tspec, ospec,
                  pl.BlockSpec(memory_space=pl.ANY)],
        out_specs=[pl.BlockSpec((tok, 4 * LANE), lambda b, i: (b * nb + i, C_CKV // (4 * LANE))), aspec, aspec,
                   pl.BlockSpec((1, MLA_KVR), lambda b, i: (0, 0))],
        out_shape=[jax.ShapeDtypeStruct((tp, N_EXT), BF16)] + [jax.ShapeDtypeStruct((MLA_KVR, hw), F32)] * 2
        + [jax.ShapeDtypeStruct((1, MLA_KVR), F32)],
        input_output_aliases={9: 0},
        compiler_params=_cp(("arbitrary", "arbitrary")),
    )(dkf, dvf, proj, kv_norm_g, wk, wv, cos_t, sin_t, d_lr, dproj)


def _mid_fwd(ya_in, yb_in, proj, hp, target, w_gp, w_mp, w_o, final_g, bsz, lp):
    tp = bsz * lp
    tm = _attn_block(lp)
    nb = lp // tm

    def body(ya_ref, yb_ref, gg_ref, gm_ref, h_ref, t_ref, wgp_ref, wmp_ref, wo_ref, fg_ref,
             ya_out, yb_out, dh_ref, loss_ref, dfg_ref):
        @pl.when(jnp.logical_and(pl.program_id(0) == 0, pl.program_id(1) == 0))
        def _():
            loss_ref[...] = jnp.zeros_like(loss_ref)
            dfg_ref[...] = jnp.zeros_like(dfg_ref)

        y_a = _dot(ya_ref[...], wgp_ref[...])
        y_b = _dot(yb_ref[...], wmp_ref[...])
        ya_out[...] = _bf(y_a)
        yb_out[...] = _bf(y_b)
        merged = _sigmoid(gg_ref[...].astype(F32)) * y_a + _sigmoid(gm_ref[...].astype(F32)) * y_b
        h2 = h_ref[...] + _dot(_bf(merged), wo_ref[...])
        fg = fg_ref[...]
        xh, r = _rms_fwd(h2)
        pos = pl.program_id(1) * tm + lax.broadcasted_iota(jnp.int32, (tm, 1), 0)
        err = jnp.where(pos >= X0, xh * fg - t_ref[...], 0.0)
        loss_ref[...] += 0.5 * jnp.sum(jnp.mean(err * err, axis=-1, keepdims=True), axis=0, keepdims=True)
        dy = err * (1.0 / D_MODEL)
        dx, dfg = _rms_bwd(dy, xh, r, fg)
        dh_ref[...] = dx
        dfg_ref[...] += dfg

    tok = lambda c: pl.BlockSpec((tm, D_MODEL), lambda b, i: (b * nb + i, c))
    wspec = pl.BlockSpec((D_MODEL, D_MODEL), lambda b, i: (0, 0))
    return pl.pallas_call(
        body, name="mid_fwd", grid=(bsz, nb),
        in_specs=[tok(0), tok(0), tok(C_GG // D_MODEL), tok(C_GM // D_MODEL), tok(0), tok(0),
                  wspec, wspec, wspec, pl.BlockSpec((1, D_MODEL), lambda b, i: (0, 0))],
        out_specs=[tok(0), tok(0), tok(0), pl.BlockSpec((1, LANE), lambda b, i: (0, 0)),
                   pl.BlockSpec((1, D_MODEL), lambda b, i: (0, 0))],
        out_shape=[jax.ShapeDtypeStruct((tp, D_MODEL), BF16), jax.ShapeDtypeStruct((tp, D_MODEL), BF16),
                   jax.ShapeDtypeStruct((tp, D_MODEL), F32), jax.ShapeDtypeStruct((1, LANE), F32),
                   jax.ShapeDtypeStruct((1, D_MODEL), F32)],
        compiler_params=_cp(("arbitrary", "arbitrary"), 48),
    )(ya_in, yb_in, proj, proj, hp, target, w_gp, w_mp, w_o, final_g)


def _mid_bwd(dh2, y_a, y_b, proj, ya_in, yb_in, w_o, w_gp, w_mp):
    tp = dh2.shape[0]
    tm = _attn_block(tp)
    nsteps = tp // tm

    def body(dh_ref, ya_ref, yb_ref, gg_ref, gm_ref, yai_ref, ybi_ref, wo_ref, wgp_ref, wmp_ref,
             dyai_ref, dybi_ref, dgate_ref, dwo_ref, dwgp_ref, dwmp_ref, a_o, a_gp, a_mp):
        @pl.when(pl.program_id(0) == 0)
        def _():
            for r in (a_o, a_gp, a_mp):
                r[...] = jnp.zeros_like(r)

        dh = _bf(dh_ref[...])
        dm = _dot_nt(dh, wo_ref[...])
        y_a, y_b = ya_ref[...].astype(F32), yb_ref[...].astype(F32)
        sg, sm = _sigmoid(gg_ref[...].astype(F32)), _sigmoid(gm_ref[...].astype(F32))
        d_ya, d_yb = _bf(sg * dm), _bf(sm * dm)
        dgate_ref[:, :D_MODEL] = _bf(dm * y_a * sg * (1.0 - sg))
        dgate_ref[:, D_MODEL:] = _bf(dm * y_b * sm * (1.0 - sm))
        a_o[...] += _dot_tn(_bf(sg * y_a + sm * y_b), dh)
        a_gp[...] += _dot_tn(yai_ref[...], d_ya)
        a_mp[...] += _dot_tn(ybi_ref[...], d_yb)
        dyai_ref[...] = _bf(_dot_nt(d_ya, wgp_ref[...]))
        dybi_ref[...] = _bf(_dot_nt(d_yb, wmp_ref[...]))

        @pl.when(pl.program_id(0) == nsteps - 1)
        def _():
            pltpu.sync_copy(a_o, dwo_ref)
            pltpu.sync_copy(a_gp, dwgp_ref)
            pltpu.sync_copy(a_mp, dwmp_ref)

    tok = lambda c: pl.BlockSpec((tm, D_MODEL), lambda i: (i, c))
    wspec = pl.BlockSpec((D_MODEL, D_MODEL), lambda i: (0, 0))
    anyspec = pl.BlockSpec(memory_space=pl.ANY)
    wshape = jax.ShapeDtypeStruct((D_MODEL, D_MODEL), F32)
    return pl.pallas_call(
        body, name="mid_bwd", grid=(nsteps,),
        in_specs=[tok(0), tok(0), tok(0), tok(C_GG // D_MODEL), tok(C_GM // D_MODEL), tok(0), tok(0),
                  wspec, wspec, wspec],
        out_specs=[tok(0), tok(0), pl.BlockSpec((tm, 2 * D_MODEL), lambda i: (i, C_GG // (2 * D_MODEL))),
                   anyspec, anyspec, anyspec],
        out_shape=[jax.ShapeDtypeStruct((tp, D_MODEL), BF16)] * 2 + [jax.ShapeDtypeStruct((tp, N_EXT), BF16)]
        + [wshape] * 3,
        scratch_shapes=[pltpu.VMEM((D_MODEL, D_MODEL), F32)] * 3,
        compiler_params=_cp(("arbitrary",), 56),
    )(dh2, y_a, y_b, proj, proj, ya_in, yb_in, w_o, w_gp, w_mp)


MESH_ID = pl.DeviceIdType.MESH
EXCHANGE_SEMS = [pltpu.SemaphoreType.DMA((N_DEV - 1,)), pltpu.SemaphoreType.DMA((N_DEV - 1,)), pltpu.SemaphoreType.DMA]


def _my_place():
    return lax.axis_index("x"), lax.axis_index("y"), lax.axis_index("c")


def _exchange(g_ref, recv_ref, send_sems, recv_sems, local_sem, start):
    x, y, c = _my_place()
    me = 4 * x + 2 * y + c
    own = pltpu.make_async_copy(g_ref.at[me], recv_ref.at[me], local_sem)
    sends, lands = [], []
    for d in range(1, N_DEV):
        px = 1 - x if d & 4 else x
        py = 1 - y if d & 2 else y
        pc = 1 - c if d & 1 else c
        peer = 4 * px + 2 * py + pc
        for slot, group in ((me, sends),) if start else ((me, sends), (peer, lands)):
            group.append(pltpu.make_async_remote_copy(
                src_ref=g_ref.at[peer], dst_ref=recv_ref.at[slot], send_sem=send_sems.at[d - 1],
                recv_sem=recv_sems.at[d - 1], device_id=(px, py, pc), device_id_type=MESH_ID))
    if start:
        own.start()
        for cp in sends:
            cp.start()
    else:
        for cp in lands:
            cp.wait_recv()
        for cp in sends:
            cp.wait_send()
        own.wait()


def _dw_in(u, dproj, slabs):
    tp = u.shape[0]
    tm, tn = _big_tok(tp), EXT_BLOCK
    nj, ni = N_EXT // tn, tp // tm

    def body(u_ref, d_ref, g_ref, o_ref, recv_ref, send_sems, recv_sems, local_sem):
        j, i = pl.program_id(0), pl.program_id(1)

        @pl.when(jnp.logical_and(j == 0, i == 0))
        def _():
            _exchange(g_ref, recv_ref, send_sems, recv_sems, local_sem, True)

        @pl.when(i == 0)
        def _():
            o_ref[...] = jnp.zeros_like(o_ref)

        o_ref[...] += _dot_tn(u_ref[...], d_ref[...])

        @pl.when(jnp.logical_and(j == nj - 1, i == ni - 1))
        def _():
            _exchange(g_ref, recv_ref, send_sems, recv_sems, local_sem, False)

    anyspec = pl.BlockSpec(memory_space=pl.ANY)
    return pl.pallas_call(
        body, name="dw_in", grid=(nj, ni),
        in_specs=[pl.BlockSpec((tm, D_MODEL), lambda j, i: (i, 0)), pl.BlockSpec((tm, tn), lambda j, i: (i, j)), anyspec],
        out_specs=[pl.BlockSpec((D_MODEL, tn), lambda j, i: (0, j)), anyspec],
        out_shape=[jax.ShapeDtypeStruct((D_MODEL, N_EXT), F32), jax.ShapeDtypeStruct(slabs.shape, slabs.dtype)],
        scratch_shapes=EXCHANGE_SEMS,
        compiler_params=_cp(("arbitrary", "arbitrary"), 48),
    )(u, dproj, slabs)


def _dx_in(dproj, w_ext, hp, dh2, norm_g, slabs):
    tp = hp.shape[0]
    tm, tk = _big_tok(tp), EXT_BLOCK
    nk = N_EXT // tk
    ni = tp // tm

    def body(d_ref, w_ref, h_ref, dh_ref, g_ref, s_ref, o_ref, dg_ref, recv_ref, acc, send_sems, recv_sems, local_sem):
        k = pl.program_id(1)

        @pl.when(jnp.logical_and(pl.program_id(0) == 0, k == 0))
        def _():
            _exchange(s_ref, recv_ref, send_sems, recv_sems, local_sem, True)

        @pl.when(jnp.logical_and(pl.program_id(0) == 0, k == 0))
        def _():
            dg_ref[...] = jnp.zeros_like(dg_ref)

        @pl.when(k == 0)
        def _():
            acc[...] = jnp.zeros_like(acc)

        acc[...] += _dot_nt(d_ref[...], w_ref[...])

        @pl.when(k == nk - 1)
        def _():
            g = g_ref[...]
            xh, r = _rms_fwd(h_ref[...])
            dx, dg = _rms_bwd(acc[...], xh, r, g)
            o_ref[...] = dh_ref[...] + dx
            dg_ref[...] += dg

        @pl.when(jnp.logical_and(pl.program_id(0) == ni - 1, k == nk - 1))
        def _():
            _exchange(s_ref, recv_ref, send_sems, recv_sems, local_sem, False)

    tok = pl.BlockSpec((tm, D_MODEL), lambda i, k: (i, 0))
    anyspec = pl.BlockSpec(memory_space=pl.ANY)
    return pl.pallas_call(
        body, name="dx_in", grid=(ni, nk),
        in_specs=[pl.BlockSpec((tm, tk), lambda i, k: (i, k)), pl.BlockSpec((D_MODEL, tk), lambda i, k: (0, k)),
                  tok, tok, pl.BlockSpec((1, D_MODEL), lambda i, k: (0, 0)), anyspec],
        out_specs=[tok, pl.BlockSpec((1, D_MODEL), lambda i, k: (0, 0)), anyspec],
        out_shape=[jax.ShapeDtypeStruct((tp, D_MODEL), F32), jax.ShapeDtypeStruct((1, D_MODEL), F32),
                   jax.ShapeDtypeStruct(slabs.shape, slabs.dtype)],
        scratch_shapes=[pltpu.VMEM((tm, D_MODEL), F32)] + EXCHANGE_SEMS,
        compiler_params=_cp(("arbitrary", "arbitrary"), 56),
    )(dproj, w_ext, hp, dh2, norm_g, slabs)


def _meta_grad(dhp3):
    bsz = dhp3.shape[0]

    def body(d_ref, o_ref):
        @pl.when(pl.program_id(0) == 0)
        def _():
            o_ref[...] = jnp.zeros_like(o_ref)

        o_ref[...] += d_ref[0]

    return pl.pallas_call(
        body, name="meta_grad", grid=(bsz,),
        in_specs=[pl.BlockSpec((1, N_META, D_MODEL), lambda b: (b, FRONT // N_META, 0))],
        out_specs=pl.BlockSpec((N_META, D_MODEL), lambda b: (0, 0)),
        out_shape=jax.ShapeDtypeStruct((N_META, D_MODEL), F32),
        compiler_params=_cp(("arbitrary",)),
    )(dhp3)


W_IN_SHARD = N_IN // N_DEV


def _pad_lanes(a, width=LANE):
    return jnp.pad(a, [(0, 0)] * (a.ndim - 1) + [(0, width - a.shape[-1])])


def _rot_cols(w):
    half = w.shape[-1] // 2
    return jnp.concatenate([-w[..., half:], w[..., :half]], axis=-1)


def _unrot_cols(dw):
    half = dw.shape[-1] // 2
    return jnp.concatenate([dw[..., half:], -dw[..., :half]], axis=-1)


def _w_in_cols(shards, lo, hi):
    parts = []
    for k in range(lo // W_IN_SHARD, (hi - 1) // W_IN_SHARD + 1):
        a, b = max(lo, k * W_IN_SHARD), min(hi, (k + 1) * W_IN_SHARD)
        parts.append(shards[k][:, a - k * W_IN_SHARD:b - k * W_IN_SHARD])
    return parts[0] if len(parts) == 1 else jnp.concatenate(parts, axis=1)


def _w_in_ext(shards):
    c = lambda lo, hi: _w_in_cols(shards, lo, hi)
    kr = c(O_KR, O_MZ)
    return jnp.concatenate([
        c(O_V, O_LR), c(O_Z, O_CQ), c(O_Q, O_K), c(O_K, O_V), c(O_MZ, O_GG), c(O_GG, O_GM), c(O_GM, N_IN),
        c(O_CKV, O_KR), _pad_lanes(kr), _pad_lanes(_rot_cols(kr)), _pad_lanes(c(O_LR, O_Z)), c(O_CQ, O_CKV)], axis=1)


def _w_in_grad(dw):
    g = lambda start, width: dw[:, start:start + width]
    kr = g(C_KR, MLA_ROPE) + _unrot_cols(g(C_KROT, MLA_ROPE))
    return jnp.concatenate([
        g(C_Q, GLA_KW), g(C_K, GLA_KW), g(C_V, GLA_VW), g(C_LR, GLA_RANK), g(C_Z, GLA_VW), g(C_CQ, MLA_QR),
        g(C_CKV, MLA_KVR), kr, g(C_MZ, D_MODEL), g(C_GG, D_MODEL), g(C_GM, D_MODEL)], axis=1)


def _rope_tables(lp):
    inv = 1.0 / (ROPE_BASE ** (jnp.arange(0, MLA_ROPE, 2, dtype=F32) / MLA_ROPE))
    ang = (jnp.arange(lp, dtype=F32) - FRONT)[:, None] * inv[None, :]
    cos, sin = jnp.cos(ang), jnp.sin(ang)
    return _pad_lanes(jnp.concatenate([cos, cos], axis=1)), _pad_lanes(jnp.concatenate([sin, sin], axis=1))


def _local_step(h3, target3, w):
    bsz, lp, _ = h3.shape
    tp = bsz * lp
    assert lp % TOK == 0 and lp % GLA_ROWS == 0
    hp, target = h3.reshape(tp, D_MODEL), target3.reshape(tp, D_MODEL)
    cos_t, sin_t = _rope_tables(lp)

    w_ext = _w_in_ext(w["w_in"])
    gw_pad = jnp.pad(w["gla_gate_w"], ((0, LANE - GLA_RANK), (0, 0)))
    uq = w["mla_w_uq"].reshape(MLA_QR, MLA_HEADS, MLA_QK)
    rope_w = uq[:, :, MLA_NOPE:]
    hw = MLA_HEADS * LANE
    wn = uq[:, :, :MLA_NOPE].reshape(MLA_QR, hw)
    wr = _pad_lanes(rope_w).reshape(MLA_QR, hw)
    wt = _pad_lanes(_rot_cols(rope_w)).reshape(MLA_QR, hw)
    ukv = w["mla_w_ukv"].reshape(MLA_KVR, MLA_HEADS, MLA_NOPE + MLA_DV)
    wk = ukv[:, :, :MLA_NOPE].reshape(MLA_KVR, hw)
    wv = ukv[:, :, MLA_NOPE:].reshape(MLA_KVR, hw)

    u, proj = _proj_in(hp, w["norm_g"], w_ext)
    o_raw, ya_in, s_all = _gla_fwd(proj, gw_pad, w["gla_gate_b"], w["gla_norm_g"], bsz, lp)
    qf = _q_up(proj, w["mla_q_norm_g"], wn, wr, wt, cos_t, sin_t, bsz, lp)
    kf, vf = _kv_up(proj, w["mla_kv_norm_g"], wk, wv, cos_t, sin_t, bsz, lp)
    o_b, yb_in, lse = _attn_fwd(qf, kf, vf, proj, bsz, lp)
    y_a, y_b, dh2, loss, d_final_g = _mid_fwd(ya_in, yb_in, proj, hp, target, w["gla_proj"], w["mla_proj"],
                                              w["w_out"], w["final_norm_g"], bsz, lp)
    d_ya, d_yb, dproj, d_w_out, d_gla_proj, d_mla_proj = _mid_bwd(
        dh2, y_a, y_b, proj, ya_in, yb_in, w["w_out"], w["gla_proj"], w["mla_proj"])
    dproj, d_gate, d_gla_norm = _gla_bwd(proj, gw_pad, w["gla_gate_b"], w["gla_norm_g"], o_raw, s_all, d_ya, dproj,
                                         bsz, lp)
    d_lr, d_gw_pad, d_gate_b = _gate_bwd(d_gate, proj, gw_pad)
    d_o, dproj, delta = _attn_bwd_pre(d_yb, proj, o_b, dproj, bsz, lp)
    dqf, dkf, dvf = _attn_bwd(qf, kf, vf, d_o, lse, delta, bsz, lp)
    dproj, d_wn, d_wr, d_wt, d_qn = _q_up_bwd(dqf, proj, w["mla_q_norm_g"], wn, wr, wt, cos_t, sin_t, dproj,
                                              bsz, lp)
    dproj, d_wk, d_wv, d_kvn = _kv_up_bwd(dkf, dvf, proj, w["mla_kv_norm_g"], wk, wv, cos_t, sin_t, d_lr, dproj,
                                          bsz, lp)

    d_rope = (d_wr.reshape(MLA_QR, MLA_HEADS, LANE)[:, :, :MLA_ROPE]
              + _unrot_cols(d_wt.reshape(MLA_QR, MLA_HEADS, LANE)[:, :, :MLA_ROPE]))
    d_uq = jnp.concatenate([d_wn.reshape(MLA_QR, MLA_HEADS, LANE), d_rope], axis=-1).reshape(MLA_QR, MLA_HEADS * MLA_QK)
    d_ukv = jnp.concatenate([d_wk.reshape(MLA_KVR, MLA_HEADS, LANE), d_wv.reshape(MLA_KVR, MLA_HEADS, LANE)],
                            axis=-1).reshape(MLA_KVR, MLA_HEADS * (MLA_NOPE + MLA_DV))
    mats = dict(gla_gate_w=d_gw_pad[:GLA_RANK], gla_proj=d_gla_proj, mla_w_uq=d_uq, mla_w_ukv=d_ukv,
                mla_proj=d_mla_proj, w_out=d_w_out)
    packed = _pad_rows(jnp.concatenate([_split8(mats[n], axis).reshape(N_DEV, -1) for n, _, axis in PACKED], axis=1),
                       PACK_ROWS)
    d_w_ext, packed_parts = _dw_in(u, dproj, _bf(packed))
    d_hp, d_norm_g, w_in_parts = _dx_in(dproj, w_ext, hp, dh2, w["norm_g"], _bf(_split8(_w_in_grad(d_w_ext), 1)))
    d_hp3 = d_hp.reshape(bsz, lp, D_MODEL)
    small = dict(meta_tokens=_meta_grad(d_hp3), norm_g=d_norm_g, gla_gate_b=d_gate_b, gla_norm_g=d_gla_norm,
                 mla_q_norm_g=d_qn, mla_kv_norm_g=d_kvn, final_norm_g=d_final_g)
    return loss, d_hp3[:, X0:, :], w_in_parts, packed_parts, small


PACKED = (("gla_gate_w", (GLA_RANK, GLA_KW // N_DEV), 1),
          ("gla_proj", (D_MODEL // N_DEV, D_MODEL), 0), ("mla_w_uq", (MLA_QR, MLA_HEADS * MLA_QK // N_DEV), 1),
          ("mla_w_ukv", (MLA_KVR, MLA_HEADS * (MLA_NOPE + MLA_DV) // N_DEV), 1),
          ("mla_proj", (D_MODEL // N_DEV, D_MODEL), 0), ("w_out", (D_MODEL // N_DEV, D_MODEL), 0))
REPLICATED = (("norm_g", D_MODEL), ("gla_gate_b", GLA_KW), ("gla_norm_g", GLA_DV), ("mla_q_norm_g", MLA_QR),
              ("mla_kv_norm_g", MLA_KVR), ("final_norm_g", D_MODEL))
PACK_ROWS = 3744
PACK_BLOCK = 1248
SMALL_ROWS = 48
LOSS_ROW = N_META + 25
W_IN_BLOCK = 128


def _all_gather(shards, frames, target):
    n_arr = len(shards)
    bsz, seq, _ = frames.shape
    lp = X0 + seq
    zeros = jnp.zeros((X0, D_MODEL), F32)

    def body(*refs):
        x_refs, (f_ref, t_ref, z_ref) = refs[:n_arr], refs[n_arr:n_arr + 3]
        out_refs, (h_ref, tpad_ref) = refs[n_arr + 3:2 * n_arr + 3], refs[2 * n_arr + 3:2 * n_arr + 5]
        send_sems, recv_sems, local_sems, side_sems, meta_sems = refs[2 * n_arr + 5:]
        side = []
        for b in range(bsz):
            for src, dst in ((f_ref.at[b], h_ref.at[b, X0:lp, :]), (t_ref.at[b], tpad_ref.at[b, X0:lp, :]),
                             (z_ref.at[0:FRONT, :], h_ref.at[b, 0:FRONT, :]), (z_ref, tpad_ref.at[b, 0:X0, :])):
                side.append(pltpu.make_async_copy(src, dst, side_sems.at[len(side)]))
                side[-1].start()
        x, y, c = _my_place()
        me, sibling = (x, y, c), (x, y, 1 - c)
        chips = [(1 - x, y), (x, 1 - y), (1 - x, 1 - y)]

        def copy(a, k, block, to, from_input=False):
            slab = out_refs[a].at[4 * block[0] + 2 * block[1] + block[2]]
            return pltpu.make_async_remote_copy(
                src_ref=x_refs[a] if from_input else slab, dst_ref=slab,
                send_sem=send_sems.at[7 * a + k], recv_sem=recv_sems.at[7 * a + k], device_id=to,
                device_id_type=MESH_ID)

        arrays = range(n_arr)
        mine = [pltpu.make_async_copy(x_refs[a], out_refs[a].at[4 * x + 2 * y + c], local_sems.at[a]) for a in arrays]
        for cp in mine:
            cp.start()
        first = [copy(a, 0, me, sibling, True) for a in arrays]
        first += [copy(a, 1 + j, me, (*chip, c), True) for j, chip in enumerate(chips) for a in arrays]
        for cp in first:
            cp.start()
        passed = []
        for j, chip in enumerate(chips):
            for a in arrays:
                copy(a, 1 + j, (*chip, c), me).wait_recv()
                passed.append(copy(a, 4 + j, (*chip, c), sibling))
                passed[-1].start()
        for a in arrays:
            copy(a, 0, sibling, me).wait_recv()
        for j, chip in enumerate(chips):
            for a in arrays:
                copy(a, 4 + j, (*chip, 1 - c), me).wait_recv()
        for cp in first + passed:
            cp.wait_send()
        for cp in mine:
            cp.wait()
        meta = [pltpu.make_async_copy(out_refs[-1].at[k], h_ref.at[b, FRONT:X0, k * LANE:(k + 1) * LANE],
                                      meta_sems.at[N_DEV * b + k]) for b in range(bsz) for k in range(N_DEV)]
        for cp in meta:
            cp.start()
        for cp in side + meta:
            cp.wait()

    anyspec = pl.BlockSpec(memory_space=pl.ANY)
    padded = jax.ShapeDtypeStruct((bsz, lp, D_MODEL), F32)
    out = pl.pallas_call(
        body, name="weights_all_gather",
        out_shape=[jax.ShapeDtypeStruct((N_DEV,) + s.shape, s.dtype) for s in shards] + [padded, padded],
        in_specs=[anyspec] * (n_arr + 3), out_specs=[anyspec] * (n_arr + 2),
        scratch_shapes=[pltpu.SemaphoreType.DMA((7 * n_arr,)), pltpu.SemaphoreType.DMA((7 * n_arr,)),
                        pltpu.SemaphoreType.DMA((n_arr,)), pltpu.SemaphoreType.DMA((4 * bsz,)),
                        pltpu.SemaphoreType.DMA((N_DEV * bsz,))],
    )(*shards, frames, target, zeros)
    return out[:n_arr], out[n_arr], out[n_arr + 1]


def _small_exchange(slabs):
    def body(g_ref, recv_ref, send_sems, recv_sems, local_sem):
        _exchange(g_ref, recv_ref, send_sems, recv_sems, local_sem, True)
        _exchange(g_ref, recv_ref, send_sems, recv_sems, local_sem, False)

    vmem = pl.BlockSpec(memory_space=pltpu.VMEM)
    return pl.pallas_call(
        body, name="small_exchange", out_shape=jax.ShapeDtypeStruct(slabs.shape, slabs.dtype),
        in_specs=[vmem], out_specs=vmem, scratch_shapes=EXCHANGE_SEMS,
    )(slabs)


def _adamw(parts, w, m, v, block_rows, name):
    rows, cols = w.shape

    def body(p_ref, w_ref, m_ref, v_ref, g_out, d_out, m_out, v_out):
        g = p_ref[0].astype(F32)
        for s in range(1, N_DEV):
            g = g + p_ref[s].astype(F32)
        m_new = ADAM_B1 * m_ref[...] + (1.0 - ADAM_B1) * g
        v_new = ADAM_B2 * v_ref[...] + (1.0 - ADAM_B2) * (g * g)
        m_hat = m_new / (1.0 - ADAM_B1 ** ADAM_STEP)
        v_hat = v_new / (1.0 - ADAM_B2 ** ADAM_STEP)
        g_out[...] = g
        d_out[...] = -ADAM_LR * (m_hat / (jnp.sqrt(v_hat) + ADAM_EPS) + ADAM_WD * w_ref[...])
        m_out[...] = m_new
        v_out[...] = v_new

    spec = pl.BlockSpec((block_rows, cols), lambda i: (i, 0))
    return pl.pallas_call(
        body, name=name, grid=(rows // block_rows,),
        in_specs=[pl.BlockSpec((N_DEV, block_rows, cols), lambda i: (0, i, 0)), spec, spec, spec],
        out_specs=[spec] * 4, out_shape=[jax.ShapeDtypeStruct((rows, cols), F32)] * 4,
        compiler_params=_cp(("parallel",), 48),
    )(parts, w, m, v)


def _pad_rows(flat, rows):
    pad = rows * LANE - flat.shape[-1]
    flat = jnp.pad(flat, [(0, 0)] * (flat.ndim - 1) + [(0, pad)])
    return flat.reshape(flat.shape[:-1] + (rows, LANE))


def _pack_shards(shards):
    return _pad_rows(jnp.concatenate([shards[n].reshape(-1) for n, _, _ in PACKED]), PACK_ROWS)


def _unpack_shards(packed):
    flat, out, off = packed.reshape(-1), {}, 0
    for n, shape, _ in PACKED:
        size = shape[0] * shape[1]
        out[n] = flat[off:off + size].reshape(shape)
        off += size
    return out


def _split8(full, axis):
    r, c = full.shape
    if axis == 0:
        return full.reshape(N_DEV, r // N_DEV, c)
    return full.reshape(r, N_DEV, c // N_DEV).transpose(1, 0, 2)


def _join8(shards, axis):
    _, r, c = shards.shape
    if axis == 0:
        return shards.reshape(N_DEV * r, c)
    return shards.transpose(1, 0, 2).reshape(r, N_DEV * c)


def _pack_small(meta_shard, vals, loss_row):
    rows = jnp.concatenate([vals[n].reshape(-1, LANE) for n, _ in REPLICATED] + [loss_row], axis=0)
    rows = jnp.pad(rows, ((0, SMALL_ROWS - N_META - rows.shape[0]), (0, 0)))
    return jnp.concatenate([meta_shard, jnp.broadcast_to(rows, meta_shard.shape[:-2] + rows.shape)], axis=-2)


def _unpack_small(packed):
    out, off = {"meta_tokens": packed[:N_META]}, N_META
    for n, size in REPLICATED:
        out[n] = packed[off:off + size // LANE].reshape(1, size)
        off += size // LANE
    return out


def kernel(x, meta_tokens, norm_g, w_in, gla_gate_w, gla_gate_b, gla_norm_g, gla_proj, mla_q_norm_g, mla_w_uq, mla_kv_norm_g, mla_w_ukv, mla_proj, w_out, final_norm_g, loss_target, m_meta_tokens, m_norm_g, m_w_in, m_gla_gate_w, m_gla_gate_b, m_gla_norm_g, m_gla_proj, m_mla_q_norm_g, m_mla_w_uq, m_mla_kv_norm_g, m_mla_w_ukv, m_mla_proj, m_w_out, m_final_norm_g, v_meta_tokens, v_norm_g, v_w_in, v_gla_gate_w, v_gla_gate_b, v_gla_norm_g, v_gla_proj, v_mla_q_norm_g, v_mla_w_uq, v_mla_kv_norm_g, v_mla_w_ukv, v_mla_proj, v_w_out, v_final_norm_g):
    given = dict(meta_tokens=meta_tokens, norm_g=norm_g, w_in=w_in, gla_gate_w=gla_gate_w, gla_gate_b=gla_gate_b,
                 gla_norm_g=gla_norm_g, gla_proj=gla_proj, mla_q_norm_g=mla_q_norm_g, mla_w_uq=mla_w_uq,
                 mla_kv_norm_g=mla_kv_norm_g, mla_w_ukv=mla_w_ukv, mla_proj=mla_proj, w_out=w_out,
                 final_norm_g=final_norm_g)
    mom_m = dict(meta_tokens=m_meta_tokens, norm_g=m_norm_g, w_in=m_w_in, gla_gate_w=m_gla_gate_w,
                 gla_gate_b=m_gla_gate_b, gla_norm_g=m_gla_norm_g, gla_proj=m_gla_proj, mla_q_norm_g=m_mla_q_norm_g,
                 mla_w_uq=m_mla_w_uq, mla_kv_norm_g=m_mla_kv_norm_g, mla_w_ukv=m_mla_w_ukv, mla_proj=m_mla_proj,
                 w_out=m_w_out, final_norm_g=m_final_norm_g)
    mom_v = dict(meta_tokens=v_meta_tokens, norm_g=v_norm_g, w_in=v_w_in, gla_gate_w=v_gla_gate_w,
                 gla_gate_b=v_gla_gate_b, gla_norm_g=v_gla_norm_g, gla_proj=v_gla_proj, mla_q_norm_g=v_mla_q_norm_g,
                 mla_w_uq=v_mla_w_uq, mla_kv_norm_g=v_mla_kv_norm_g, mla_w_ukv=v_mla_w_ukv, mla_proj=v_mla_proj,
                 w_out=v_w_out, final_norm_g=v_final_norm_g)
    shapes = {n: a.shape for n, a in given.items()}
    shard2d = {n: s for n, s, _ in PACKED}
    shard2d["w_in"] = (D_MODEL, W_IN_SHARD)
    shard2d["meta_tokens"] = (N_META, LANE)

    def as2d(tree):
        out = {n: tree[n].reshape(shard2d[n]) for n in shard2d}
        out.update({n: tree[n].reshape(1, size) for n, size in REPLICATED})
        return out

    w_loc, m_loc, v_loc = as2d(given), as2d(mom_m), as2d(mom_v)

    flat = jnp.concatenate([w_loc[n].astype(BF16).reshape(-1) for n, _, _ in PACKED])
    (w_in_all, packed_all, _), h3, target3 = _all_gather(
        [w_loc["w_in"].astype(BF16), _pad_rows(flat, PACK_ROWS), w_loc["meta_tokens"]], x, loss_target)
    packed_all = packed_all.reshape(N_DEV, -1)
    full, off = {"w_in": w_in_all}, 0
    for n, shape, axis in PACKED:
        size = shape[0] * shape[1]
        full[n] = _join8(packed_all[:, off:off + size].reshape((N_DEV,) + shape), axis)
        off += size
    for n, _ in REPLICATED:
        full[n] = w_loc[n]

    loss_part, grad_x, w_in_parts, packed_parts, small = _local_step(h3, target3, full)
    small_all = _small_exchange(_pack_small(_split8(small["meta_tokens"], 1), small,
                                            jnp.broadcast_to(loss_part[:, :1], (1, LANE))))

    g_w, d_w, m_w, v_w = _adamw(w_in_parts, w_loc["w_in"], m_loc["w_in"], v_loc["w_in"], W_IN_BLOCK, "adamw_w_in")
    g_p, d_p, m_p, v_p = _adamw(packed_parts, _pack_shards(w_loc), _pack_shards(m_loc), _pack_shards(v_loc),
                                PACK_BLOCK, "adamw_packed")
    zero_row = jnp.zeros((1, LANE), F32)
    g_s, d_s, m_s, v_s = _adamw(small_all, *(_pack_small(t["meta_tokens"], t, zero_row) for t in (w_loc, m_loc, v_loc)),
                                SMALL_ROWS, "adamw_small")
    loss = g_s[LOSS_ROW, 0]

    order = ["meta_tokens", "norm_g", "w_in", "gla_gate_w", "gla_gate_b", "gla_norm_g", "gla_proj", "mla_q_norm_g",
             "mla_w_uq", "mla_kv_norm_g", "mla_w_ukv", "mla_proj", "w_out", "final_norm_g"]
    result = [loss, grad_x]
    for w_in_out, packed_sh, packed_sm in ((g_w, g_p, g_s), (d_w, d_p, d_s), (m_w, m_p, m_s), (v_w, v_p, v_s)):
        tree = _unpack_shards(packed_sh)
        tree.update(_unpack_small(packed_sm))
        tree["w_in"] = w_in_out
        result += [tree[n].reshape(shapes[n]) for n in order]
    return tuple(result)
```

```python
import jax
import jax.numpy as jnp
from jax import lax
from jax.experimental import pallas as pl
from jax.experimental.pallas import tpu as pltpu

F32 = jnp.float32
BF16 = jnp.bfloat16

D_MODEL = 1024
N_META = 16
EPS = 1e-6
FRONT = 48
X0 = FRONT + N_META
GLA_HEADS, GLA_DK, GLA_DV, GLA_RANK, GLA_CHUNK = 4, 128, 256, 16, 64
GLA_GATE_NORMALIZER = 16.0
GLA_KW = GLA_HEADS * GLA_DK
GLA_VW = GLA_HEADS * GLA_DV
MLA_HEADS, MLA_NOPE, MLA_ROPE, MLA_DV, MLA_QR, MLA_KVR = 8, 128, 64, 128, 256, 128
MLA_QK = MLA_NOPE + MLA_ROPE
ROPE_BASE = 10000.0
LANE = 128
QKW = 2 * LANE

C_V, C_Z, C_Q, C_K = 0, 1024, 2048, 2560
C_MZ = 3072
C_GG, C_GM = 4096, 5120
C_CKV, C_KR, C_KROT, C_LR = 6144, 6272, 6400, 6528
C_CQ = 6656
N_EXT = 6912
O_Q, O_K, O_V, O_LR, O_Z, O_CQ, O_CKV, O_KR, O_MZ, O_GG, O_GM, N_IN = (
    0, 512, 1024, 2048, 2064, 3088, 3344, 3472, 3536, 4560, 5584, 6608)

ADAM_LR, ADAM_B1, ADAM_B2, ADAM_EPS, ADAM_WD, ADAM_STEP = 0.001, 0.9, 0.999, 1e-08, 0.01, 10

N_DEV = 8
TOK = 192
ATT_BLOCK = 352
EXT_BLOCK = 1152


def _cp(sems=None, vmem_mb=None):
    kw = {}
    if sems is not None:
        kw["dimension_semantics"] = sems
    if vmem_mb is not None:
        kw["vmem_limit_bytes"] = vmem_mb * 1024 * 1024
    return pltpu.CompilerParams(**kw)


def _dot(a, b):
    return jnp.dot(a, b, preferred_element_type=F32)


def _dot_nt(a, b):
    return lax.dot_general(a, b, (((1,), (1,)), ((), ())), preferred_element_type=F32)


def _dot_tn(a, b):
    return lax.dot_general(a, b, (((0,), (0,)), ((), ())), preferred_element_type=F32)


def _sigmoid(x):
    return 1.0 / (1.0 + jnp.exp(-x))


def _bf(x):
    return x.astype(BF16)


def _big_tok(tp):
    return 4 * TOK if tp % (4 * TOK) == 0 else TOK


def _attn_block(lp):
    return ATT_BLOCK if lp % ATT_BLOCK == 0 else TOK


def _proj_in(hp, norm_g, w_ext):
    tp = hp.shape[0]
    tm, tn = _big_tok(tp), EXT_BLOCK

    def body(h_ref, g_ref, w_ref, u_ref, o_ref, u_scr):
        @pl.when(pl.program_id(1) == 0)
        def _():
            x = h_ref[...]
            r = lax.rsqrt(jnp.mean(x * x, axis=-1, keepdims=True) + EPS)
            u = _bf(x * r * g_ref[...])
            u_scr[...] = u
            u_ref[...] = u

        o_ref[...] = _bf(_dot(u_scr[...], w_ref[...]))

    return pl.pallas_call(
        body, name="proj_in", grid=(tp // tm, N_EXT // tn),
        in_specs=[pl.BlockSpec((tm, D_MODEL), lambda i, j: (i, 0)),
                  pl.BlockSpec((1, D_MODEL), lambda i, j: (0, 0)),
                  pl.BlockSpec((D_MODEL, tn), lambda i, j: (0, j))],
        out_specs=[pl.BlockSpec((tm, D_MODEL), lambda i, j: (i, 0)),
                   pl.BlockSpec((tm, tn), lambda i, j: (i, j))],
        out_shape=[jax.ShapeDtypeStruct((tp, D_MODEL), BF16), jax.ShapeDtypeStruct((tp, N_EXT), BF16)],
        scratch_shapes=[pltpu.VMEM((tm, D_MODEL), BF16)],
        compiler_params=_cp(("parallel", "arbitrary"), 48),
    )(hp, norm_g, w_ext)


GLA_GROUP = 3
GLA_ROWS = GLA_GROUP * GLA_CHUNK


def _gla_gates(q_ref, k_ref, lr_ref, gw_ref, gb_ref, rows, not_first):
    z = _dot(lr_ref[rows, :], gw_ref[...]) + gb_ref[...]
    logsig = jnp.minimum(z, 0.0) - jnp.log(1.0 + jnp.exp(-jnp.abs(z)))
    row = lax.broadcasted_iota(jnp.int32, (GLA_CHUNK, GLA_KW), 0)
    live = jnp.logical_or(not_first, row >= FRONT)
    g = jnp.where(live, logsig * (1.0 / GLA_GATE_NORMALIZER), 0.0)
    ri = lax.broadcasted_iota(jnp.int32, (GLA_CHUNK, GLA_CHUNK), 0)
    ci = lax.broadcasted_iota(jnp.int32, (GLA_CHUNK, GLA_CHUNK), 1)
    tril = ci <= ri
    b = jnp.dot(tril.astype(F32), g, precision=lax.Precision.HIGHEST, preferred_element_type=F32)
    bl = jnp.sum(jnp.where(row == GLA_CHUNK - 1, b, 0.0), axis=0, keepdims=True)
    eb, enb, elb, ebl = jnp.exp(b), jnp.exp(-b), jnp.exp(bl - b), jnp.exp(bl)
    q = q_ref[rows, :].astype(F32) * (GLA_DK ** -0.5)
    k = k_ref[rows, :].astype(F32)
    qe, ke, kl = q * eb, k * enb, k * elb
    return dict(z=z, live=live, tril=tril, row=row, eb=eb, enb=enb, elb=elb, ebl=ebl, qe=qe, ke=ke, kl=kl,
                qe_b=_bf(qe), ke_b=_bf(ke), kl_b=_bf(kl))


def _gla_in_specs(n_groups, rev):
    def rb(b, n):
        return b * n_groups + ((n_groups - 1 - n) if rev else n)

    return rb, [pl.BlockSpec((GLA_ROWS, GLA_KW), lambda b, n: (rb(b, n), C_Q // GLA_KW)),
                pl.BlockSpec((GLA_ROWS, GLA_KW), lambda b, n: (rb(b, n), C_K // GLA_KW)),
                pl.BlockSpec((GLA_ROWS, GLA_VW), lambda b, n: (rb(b, n), C_V // GLA_VW)),
                pl.BlockSpec((GLA_ROWS, GLA_VW), lambda b, n: (rb(b, n), C_Z // GLA_VW)),
                pl.BlockSpec((GLA_ROWS, LANE), lambda b, n: (rb(b, n), C_LR // LANE)),
                pl.BlockSpec((LANE, GLA_KW), lambda b, n: (0, 0)),
                pl.BlockSpec((1, GLA_KW), lambda b, n: (0, 0)),
                pl.BlockSpec((1, GLA_DV), lambda b, n: (0, 0))]


def _gla_fwd(proj, gw_pad, gate_b, gla_norm_g, bsz, lp):
    n_chunks = lp // GLA_CHUNK
    n_groups = n_chunks // GLA_GROUP
    tp = bsz * lp

    def body(q_ref, k_ref, v_ref, z_ref, lr_ref, gw_ref, gb_ref, gn_ref, oraw_ref, ya_ref, sall_ref, st_scr):
        grp = pl.program_id(1)

        @pl.when(grp == 0)
        def _():
            st_scr[...] = jnp.zeros_like(st_scr)

        chunks = [slice(j * GLA_CHUNK, (j + 1) * GLA_CHUNK) for j in range(GLA_GROUP)]
        cs = [_gla_gates(q_ref, k_ref, lr_ref, gw_ref, gb_ref, rows, True if j else grp > 0)
              for j, rows in enumerate(chunks)]
        gn = gn_ref[...]
        for h in range(GLA_HEADS):
            ks, vs = slice(h * GLA_DK, (h + 1) * GLA_DK), slice(h * GLA_DV, (h + 1) * GLA_DV)
            st = st_scr[h]
            for j, (rows, c) in enumerate(zip(chunks, cs)):
                sall_ref[0, j, h] = st
                v = v_ref[rows, vs]
                a = jnp.where(c["tril"], _dot_nt(c["qe_b"][:, ks], c["ke_b"][:, ks]), 0.0)
                o = _dot(_bf(a), v) + _dot_nt(c["qe_b"][:, ks], _bf(st))
                st = st * c["ebl"][:, ks] + _dot_tn(v, c["kl_b"][:, ks])
                oraw_ref[rows, vs] = o
                r = lax.rsqrt(jnp.mean(o * o, axis=-1, keepdims=True) + EPS)
                zg = z_ref[rows, vs].astype(F32)
                ya_ref[rows, vs] = _bf((o * r * gn) * (zg * _sigmoid(zg)))
            st_scr[h] = st

    rb, in_specs = _gla_in_specs(n_groups, False)
    return pl.pallas_call(
        body, name="gla_fwd", grid=(bsz, n_groups), in_specs=in_specs,
        out_specs=[pl.BlockSpec((GLA_ROWS, GLA_VW), lambda b, n: (rb(b, n), 0)),
                   pl.BlockSpec((GLA_ROWS, GLA_VW), lambda b, n: (rb(b, n), 0)),
                   pl.BlockSpec((1, GLA_GROUP, GLA_HEADS, GLA_DV, GLA_DK), lambda b, n: (b, n, 0, 0, 0))],
        out_shape=[jax.ShapeDtypeStruct((tp, GLA_VW), F32), jax.ShapeDtypeStruct((tp, GLA_VW), BF16),
                   jax.ShapeDtypeStruct((bsz, n_chunks, GLA_HEADS, GLA_DV, GLA_DK), F32)],
        scratch_shapes=[pltpu.VMEM((GLA_HEADS, GLA_DV, GLA_DK), F32)],
        compiler_params=_cp(("parallel", "arbitrary")),
    )(proj, proj, proj, proj, proj, gw_pad, gate_b, gla_norm_g)


def _gla_bwd(proj, gw_pad, gate_b, gla_norm_g, o_raw, s_all, d_ya, dproj, bsz, lp):
    n_chunks = lp // GLA_CHUNK
    n_groups = n_chunks // GLA_GROUP
    tp = bsz * lp

    def body(q_ref, k_ref, v_ref, z_ref, lr_ref, gw_ref, gb_ref, gn_ref, o_ref, s_ref, dya_ref, _,
             dp_ref, dz_ref, dgn_ref, dst_scr):
        dv_ref, dzg_ref = dp_ref.at[:, C_V:C_V + GLA_VW], dp_ref.at[:, C_Z:C_Z + GLA_VW]

        @pl.when(jnp.logical_and(pl.program_id(0) == 0, pl.program_id(1) == 0))
        def _():
            dgn_ref[...] = jnp.zeros_like(dgn_ref)

        @pl.when(pl.program_id(1) == 0)
        def _():
            dst_scr[...] = jnp.zeros_like(dst_scr)

        grp = n_groups - 1 - pl.program_id(1)
        chunks = [slice(j * GLA_CHUNK, (j + 1) * GLA_CHUNK) for j in range(GLA_GROUP)]
        cs = [_gla_gates(q_ref, k_ref, lr_ref, gw_ref, gb_ref, rows, True if j else grp > 0)
              for j, rows in enumerate(chunks)]
        gn = gn_ref[...]
        dgn = jnp.zeros((1, GLA_DV), F32)
        dqe_h, dke_h, dkl_h, dbl_h = ([[None] * GLA_HEADS for _ in chunks] for _ in range(4))
        for h in range(GLA_HEADS):
            ks, vs = slice(h * GLA_DK, (h + 1) * GLA_DK), slice(h * GLA_DV, (h + 1) * GLA_DV)
            dst = dst_scr[h]
            for j in reversed(range(GLA_GROUP)):
                rows, c = chunks[j], cs[j]
                v = v_ref[rows, vs]
                st = s_ref[0, j, h]
                o = o_ref[rows, vs]
                r = lax.rsqrt(jnp.mean(o * o, axis=-1, keepdims=True) + EPS)
                xh = o * r
                zg = z_ref[rows, vs].astype(F32)
                sg = _sigmoid(zg)
                dy = dya_ref[rows, vs].astype(F32)
                dzg_ref[rows, vs] = _bf(dy * (xh * gn) * (sg * (1.0 + zg * (1.0 - sg))))
                t = dy * (zg * sg)
                dgn += jnp.sum(t * xh, axis=0, keepdims=True)
                dxh = t * gn
                do_b = _bf(r * (dxh - xh * jnp.mean(dxh * xh, axis=-1, keepdims=True)))
                qe_b, ke_b, kl_b, dst_b = c["qe_b"][:, ks], c["ke_b"][:, ks], c["kl_b"][:, ks], _bf(dst)
                a = jnp.where(c["tril"], _dot_nt(qe_b, ke_b), 0.0)
                da_b = _bf(jnp.where(c["tril"], _dot_nt(do_b, v), 0.0))
                dqe_h[j][h] = _dot(da_b, ke_b) + _dot(do_b, _bf(st))
                dke_h[j][h] = _dot_tn(da_b, qe_b)
                dkl = _dot(v, dst_b)
                dkl_h[j][h] = dkl
                dv_ref[rows, vs] = _bf(_dot_tn(_bf(a), do_b) + _dot_nt(kl_b, dst_b))
                ddecay = jnp.sum(dst * st, axis=0, keepdims=True)
                dbl_h[j][h] = jnp.sum(dkl * c["kl"][:, ks], axis=0, keepdims=True) + ddecay * c["ebl"][:, ks]
                dst = dst * c["ebl"][:, ks] + _dot_tn(do_b, qe_b)
            dst_scr[h] = dst
        dgn_ref[...] += dgn
        ri = lax.broadcasted_iota(jnp.int32, (GLA_CHUNK, GLA_CHUNK), 0)
        ci = lax.broadcasted_iota(jnp.int32, (GLA_CHUNK, GLA_CHUNK), 1)
        triu = (ci >= ri).astype(F32)
        for j, (rows, c) in enumerate(zip(chunks, cs)):
            dqe, dke, dkl, dbl = (jnp.concatenate(p[j], axis=1) for p in (dqe_h, dke_h, dkl_h, dbl_h))
            db = dqe * c["qe"] - dke * c["ke"] - dkl * c["kl"] + jnp.where(c["row"] == GLA_CHUNK - 1, dbl, 0.0)
            dg = jnp.dot(triu, db, precision=lax.Precision.HIGHEST, preferred_element_type=F32)
            dg = jnp.where(c["live"], dg, 0.0)
            dz_ref[rows, :] = dg * (1.0 / GLA_GATE_NORMALIZER) * _sigmoid(-c["z"])
            dp_ref[rows, C_Q:C_Q + GLA_KW] = _bf(dqe * c["eb"] * (GLA_DK ** -0.5))
            dp_ref[rows, C_K:C_K + GLA_KW] = _bf(dke * c["enb"] + dkl * c["elb"])

    rb, in_specs = _gla_in_specs(n_groups, True)
    wide = pl.BlockSpec((GLA_ROWS, GLA_VW), lambda b, n: (rb(b, n), 0))
    group = C_MZ
    return pl.pallas_call(
        body, name="gla_bwd", grid=(bsz, n_groups),
        in_specs=in_specs + [wide, pl.BlockSpec((1, GLA_GROUP, GLA_HEADS, GLA_DV, GLA_DK),
                                                lambda b, n: (b, n_groups - 1 - n, 0, 0, 0)), wide,
                             pl.BlockSpec(memory_space=pl.ANY)],
        out_specs=[pl.BlockSpec((GLA_ROWS, group), lambda b, n: (rb(b, n), 0)),
                   pl.BlockSpec((GLA_ROWS, GLA_KW), lambda b, n: (rb(b, n), 0)),
                   pl.BlockSpec((1, GLA_DV), lambda b, n: (0, 0))],
        out_shape=[jax.ShapeDtypeStruct((tp, N_EXT), BF16), jax.ShapeDtypeStruct((tp, GLA_KW), F32),
                   jax.ShapeDtypeStruct((1, GLA_DV), F32)],
        input_output_aliases={11: 0},
        scratch_shapes=[pltpu.VMEM((GLA_HEADS, GLA_DV, GLA_DK), F32)],
        compiler_params=_cp(("arbitrary", "arbitrary")),
    )(proj, proj, proj, proj, proj, gw_pad, gate_b, gla_norm_g, o_raw, s_all, d_ya, dproj)


def _gate_bwd(dz, proj, gw_pad):
    tp = dz.shape[0]
    tm = _big_tok(tp)

    def body(dz_ref, lr_ref, gw_ref, dlr_ref, dgw_ref, dgb_ref):
        @pl.when(pl.program_id(0) == 0)
        def _():
            dgw_ref[...] = jnp.zeros_like(dgw_ref)
            dgb_ref[...] = jnp.zeros_like(dgb_ref)

        dz = dz_ref[...]
        dz_b = _bf(dz)
        dlr_ref[...] = _bf(_dot_nt(dz_b, gw_ref[...]))
        dgw_ref[...] += _dot_tn(lr_ref[...], dz_b)
        dgb_ref[...] += jnp.sum(dz, axis=0, keepdims=True)

    return pl.pallas_call(
        body, name="gate_bwd", grid=(tp // tm,),
        in_specs=[pl.BlockSpec((tm, GLA_KW), lambda i: (i, 0)),
                  pl.BlockSpec((tm, LANE), lambda i: (i, C_LR // LANE)),
                  pl.BlockSpec((LANE, GLA_KW), lambda i: (0, 0))],
        out_specs=[pl.BlockSpec((tm, LANE), lambda i: (i, 0)),
                   pl.BlockSpec((LANE, GLA_KW), lambda i: (0, 0)),
                   pl.BlockSpec((1, GLA_KW), lambda i: (0, 0))],
        out_shape=[jax.ShapeDtypeStruct((tp, LANE), BF16), jax.ShapeDtypeStruct((LANE, GLA_KW), F32),
                   jax.ShapeDtypeStruct((1, GLA_KW), F32)],
        compiler_params=_cp(("arbitrary",)),
    )(dz, proj, gw_pad)


def _rms_fwd(x):
    r = lax.rsqrt(jnp.mean(x * x, axis=-1, keepdims=True) + EPS)
    return x * r, r


def _rms_bwd(dy, xh, r, g):
    dxh = dy * g
    dx = r * (dxh - xh * jnp.mean(dxh * xh, axis=-1, keepdims=True))
    return dx, jnp.sum(dy * xh, axis=0, keepdims=True)


def _q_up(proj, q_norm_g, wn, wr, wt, cos_t, sin_t, bsz, lp):
    tp = bsz * lp
    tok = _attn_block(lp)
    nb = lp // tok

    def body(cq_ref, g_ref, wn_ref, wr_ref, wt_ref, cos_ref, sin_ref, q_ref):
        xh, _ = _rms_fwd(cq_ref[...].astype(F32))
        cqn = _bf(xh * g_ref[...])
        nope = _dot(cqn, wn_ref[...])
        rope = _dot(cqn, wr_ref[...])
        rot = _dot(cqn, wt_ref[...])
        cos, sin = cos_ref[...], sin_ref[...]
        one = (lax.broadcasted_iota(jnp.int32, (tok, LANE), 1) == BIAS_LANE).astype(F32)
        for h in range(MLA_HEADS):
            sl = slice(h * LANE, (h + 1) * LANE)
            q_ref[:, h * QKW:h * QKW + LANE] = _bf(nope[:, sl])
            q_ref[:, h * QKW + LANE:(h + 1) * QKW] = _bf(rope[:, sl] * cos + rot[:, sl] * sin + one)

    wspec = pl.BlockSpec((MLA_QR, MLA_HEADS * LANE), lambda b, i: (0, 0))
    tspec = pl.BlockSpec((tok, LANE), lambda b, i: (i, 0))
    return pl.pallas_call(
        body, name="mla_q_up", grid=(bsz, nb),
        in_specs=[pl.BlockSpec((tok, MLA_QR), lambda b, i: (b * nb + i, C_CQ // MLA_QR)),
                  pl.BlockSpec((1, MLA_QR), lambda b, i: (0, 0)), wspec, wspec, wspec, tspec, tspec],
        out_specs=pl.BlockSpec((tok, MLA_HEADS * QKW), lambda b, i: (b * nb + i, 0)),
        out_shape=jax.ShapeDtypeStruct((tp, MLA_HEADS * QKW), BF16),
        compiler_params=_cp(("parallel", "parallel")),
    )(proj, q_norm_g, wn, wr, wt, cos_t, sin_t)


def _kv_up(proj, kv_norm_g, wk, wv, cos_t, sin_t, bsz, lp):
    tp = bsz * lp
    tok = _attn_block(lp)
    nb = lp // tok

    def body(ckv_ref, kr_ref, krot_ref, g_ref, wk_ref, wv_ref, cos_ref, sin_ref, k_ref, v_ref):
        xh, _ = _rms_fwd(ckv_ref[...].astype(F32))
        cn = _bf(xh * g_ref[...])
        kn = _dot(cn, wk_ref[...])
        v_ref[...] = _bf(_dot(cn, wv_ref[...]))
        pos = pl.program_id(1) * tok + lax.broadcasted_iota(jnp.int32, (tok, LANE), 0)
        lane = lax.broadcasted_iota(jnp.int32, (tok, LANE), 1)
        bias = jnp.where(jnp.logical_and(lane == BIAS_LANE, pos < FRONT), KEY_BIAS, 0.0)
        kr = _bf(kr_ref[...].astype(F32) * cos_ref[...] + krot_ref[...].astype(F32) * sin_ref[...] + bias)
        for h in range(MLA_HEADS):
            k_ref[:, h * QKW:h * QKW + LANE] = _bf(kn[:, h * LANE:(h + 1) * LANE])
            k_ref[:, h * QKW + LANE:(h + 1) * QKW] = kr

    wspec = pl.BlockSpec((MLA_KVR, MLA_HEADS * LANE), lambda b, i: (0, 0))
    tspec = pl.BlockSpec((tok, LANE), lambda b, i: (i, 0))
    return pl.pallas_call(
        body, name="mla_kv_up", grid=(bsz, nb),
        in_specs=[pl.BlockSpec((tok, LANE), lambda b, i: (b * nb + i, C_CKV // LANE)),
                  pl.BlockSpec((tok, LANE), lambda b, i: (b * nb + i, C_KR // LANE)),
                  pl.BlockSpec((tok, LANE), lambda b, i: (b * nb + i, C_KROT // LANE)),
                  pl.BlockSpec((1, MLA_KVR), lambda b, i: (0, 0)), wspec, wspec, tspec, tspec],
        out_specs=[pl.BlockSpec((tok, MLA_HEADS * QKW), lambda b, i: (b * nb + i, 0)),
                   pl.BlockSpec((tok, MLA_HEADS * LANE), lambda b, i: (b * nb + i, 0))],
        out_shape=[jax.ShapeDtypeStruct((tp, MLA_HEADS * QKW), BF16),
                   jax.ShapeDtypeStruct((tp, MLA_HEADS * LANE), BF16)],
        compiler_params=_cp(("parallel", "parallel")),
    )(proj, proj, proj, kv_norm_g, wk, wv, cos_t, sin_t)


ATT_SCALE = MLA_QK ** -0.5


KEY_BIAS = -1e30
BIAS_LANE = MLA_ROPE
NEG = 2 * KEY_BIAS
LOG2E = 1.4426950408889634
EXP2_SCALE = ATT_SCALE * LOG2E


def _causal_fill(s, r0, fill):
    tq, kmax = s.shape
    a = r0 // LANE * LANE
    mask = (a + lax.broadcasted_iota(jnp.int32, (tq, kmax - a), 1)
            <= r0 + lax.broadcasted_iota(jnp.int32, (tq, kmax - a), 0))
    right = jnp.where(mask, s[:, a:], fill)
    return jnp.concatenate([s[:, :a], right], axis=1) if a else right


def _attn_fwd(qf, kf, vf, proj, bsz, lp):
    tp = bsz * lp
    tq = _attn_block(lp)

    def body(q_ref, k_ref, v_ref, mz_ref, ob_ref, yb_ref, lse_ref):
        for r0 in range(0, lp, tq):
            rows, kmax = slice(r0, r0 + tq), r0 + tq
            s = _causal_fill(_dot_nt(q_ref[rows, :], k_ref[0:kmax, :]), r0, NEG)
            m = jnp.max(s, axis=-1, keepdims=True)
            p = jnp.exp2((s - m) * EXP2_SCALE)
            l = jnp.sum(p, axis=-1, keepdims=True)
            o = _dot(_bf(p), v_ref[0:kmax, :]) / l
            ob_ref[rows, :] = _bf(o)
            mz = mz_ref[rows, :].astype(F32)
            yb_ref[rows, :] = _bf(o * (mz * _sigmoid(mz)))
            lse_ref[0, 0, rows, :] = jnp.broadcast_to(m * EXP2_SCALE + jnp.log2(l), (tq, LANE))

    head = lambda off: pl.BlockSpec((lp, MLA_DV), lambda b, h: (b, off + h))
    return pl.pallas_call(
        body, name="mla_attn_fwd", grid=(bsz, MLA_HEADS),
        in_specs=[pl.BlockSpec((lp, QKW), lambda b, h: (b, h)), pl.BlockSpec((lp, QKW), lambda b, h: (b, h)),
                  head(0), head(C_MZ // MLA_DV)],
        out_specs=[head(0), head(0), pl.BlockSpec((1, 1, lp, LANE), lambda b, h: (b, h, 0, 0))],
        out_shape=[jax.ShapeDtypeStruct((tp, MLA_HEADS * MLA_DV), BF16),
                   jax.ShapeDtypeStruct((tp, MLA_HEADS * MLA_DV), BF16),
                   jax.ShapeDtypeStruct((bsz, MLA_HEADS, lp, LANE), F32)],
        compiler_params=_cp(("parallel", "parallel"), 56),
    )(qf, kf, vf, proj)


def _attn_bwd_pre(d_yb, proj, o_b, dproj, bsz, lp):
    tp = bsz * lp
    tok = _attn_block(lp)
    nb = lp // tok
    w = MLA_HEADS * MLA_DV

    def body(dy_ref, mz_ref, o_ref, _, do_ref, dmz_ref, dl_ref):
        dy = dy_ref[...].astype(F32)
        mz = mz_ref[...].astype(F32)
        o = o_ref[...].astype(F32)
        s = _sigmoid(mz)
        do = _bf(dy * (mz * s))
        do_ref[...] = do
        dmz_ref[...] = _bf(dy * o * (s * (1.0 + mz * (1.0 - s))))
        prod = do.astype(F32) * o
        for h in range(MLA_HEADS):
            dl = jnp.sum(prod[:, h * MLA_DV:(h + 1) * MLA_DV], axis=-1, keepdims=True)
            dl_ref[0, h] = jnp.broadcast_to(dl, (tok, LANE))

    return pl.pallas_call(
        body, name="mla_attn_bwd_pre", grid=(bsz, nb),
        in_specs=[pl.BlockSpec((tok, w), lambda b, i: (b * nb + i, 0)),
                  pl.BlockSpec((tok, w), lambda b, i: (b * nb + i, C_MZ // w)),
                  pl.BlockSpec((tok, w), lambda b, i: (b * nb + i, 0)), pl.BlockSpec(memory_space=pl.ANY)],
        out_specs=[pl.BlockSpec((tok, w), lambda b, i: (b * nb + i, 0)),
                   pl.BlockSpec((tok, w), lambda b, i: (b * nb + i, C_MZ // w)),
                   pl.BlockSpec((1, MLA_HEADS, tok, LANE), lambda b, i: (b, 0, i, 0))],
        out_shape=[jax.ShapeDtypeStruct((tp, w), BF16), jax.ShapeDtypeStruct((tp, N_EXT), BF16),
                   jax.ShapeDtypeStruct((bsz, MLA_HEADS, lp, LANE), F32)],
        input_output_aliases={3: 1},
        compiler_params=_cp(("parallel", "parallel")),
    )(d_yb, proj, o_b, dproj)


def _attn_bwd(qf, kf, vf, d_o, lse, delta, bsz, lp):
    tp = bsz * lp
    tq = _attn_block(lp)

    def body(q_ref, k_ref, v_ref, do_ref, lse_ref, dl_ref, dq_ref, dk_ref, dv_ref, dk_acc, dv_acc):
        dk_acc[...] = jnp.zeros_like(dk_acc)
        dv_acc[...] = jnp.zeros_like(dv_acc)
        for r0 in range(0, lp, tq):
            rows, kmax = slice(r0, r0 + tq), r0 + tq
            q, do = q_ref[rows, :], do_ref[rows, :]
            k, v = k_ref[0:kmax, :], v_ref[0:kmax, :]
            p = jnp.exp2(_dot_nt(q, k) * EXP2_SCALE - lse_ref[0, 0, rows, :][:, :1])
            p = _causal_fill(p, r0, 0.0)
            ds = _bf(p * (_dot_nt(do, v) - dl_ref[0, 0, rows, :][:, :1]))
            dq_ref[rows, :] = _bf(_dot(ds, k) * ATT_SCALE)
            dk_acc[0:kmax, :] += _dot_tn(ds, q)
            dv_acc[0:kmax, :] += _dot_tn(_bf(p), do)
        dk_ref[...] = _bf(dk_acc[...] * ATT_SCALE)
        dv_ref[...] = _bf(dv_acc[...])

    wide = pl.BlockSpec((lp, QKW), lambda b, h: (b, h))
    narrow = pl.BlockSpec((lp, MLA_DV), lambda b, h: (b, h))
    stat = pl.BlockSpec((1, 1, lp, LANE), lambda b, h: (b, h, 0, 0))
    return pl.pallas_call(
        body, name="mla_attn_bwd", grid=(bsz, MLA_HEADS),
        in_specs=[wide, wide, narrow, narrow, stat, stat], out_specs=[wide, wide, narrow],
        out_shape=[jax.ShapeDtypeStruct((tp, MLA_HEADS * QKW), BF16), jax.ShapeDtypeStruct((tp, MLA_HEADS * QKW), BF16),
                   jax.ShapeDtypeStruct((tp, MLA_HEADS * MLA_DV), BF16)],
        scratch_shapes=[pltpu.VMEM((lp, QKW), F32), pltpu.VMEM((lp, MLA_DV), F32)],
        compiler_params=_cp(("parallel", "parallel"), 56),
    )(qf, kf, vf, d_o, lse, delta)


def _q_up_bwd(dqf, proj, q_norm_g, wn, wr, wt, cos_t, sin_t, dproj, bsz, lp):
    tp = bsz * lp
    tok = _attn_block(lp)
    nb = lp // tok
    hw = MLA_HEADS * LANE

    def body(dq_ref, cq_ref, g_ref, wn_ref, wr_ref, wt_ref, cos_ref, sin_ref, _,
             dcq_ref, dwn_ref, dwr_ref, dwt_ref, dg_ref):
        @pl.when(jnp.logical_and(pl.program_id(0) == 0, pl.program_id(1) == 0))
        def _():
            for r in (dwn_ref, dwr_ref, dwt_ref, dg_ref):
                r[...] = jnp.zeros_like(r)

        g = g_ref[...]
        xh, r = _rms_fwd(cq_ref[...].astype(F32))
        cqn = _bf(xh * g)
        cos, sin = cos_ref[...], sin_ref[...]
        dcqn = jnp.zeros((tok, MLA_QR), F32)
        for h in range(MLA_HEADS):
            sl = slice(h * LANE, (h + 1) * LANE)
            dn = dq_ref[:, h * QKW:h * QKW + LANE]
            dr = dq_ref[:, h * QKW + LANE:(h + 1) * QKW].astype(F32)
            dr_c, dr_s = _bf(dr * cos), _bf(dr * sin)
            dcqn += _dot_nt(dn, wn_ref[:, sl]) + _dot_nt(dr_c, wr_ref[:, sl]) + _dot_nt(dr_s, wt_ref[:, sl])
            dwn_ref[:, sl] += _dot_tn(cqn, dn)
            dwr_ref[:, sl] += _dot_tn(cqn, dr_c)
            dwt_ref[:, sl] += _dot_tn(cqn, dr_s)
        dx, dg = _rms_bwd(dcqn, xh, r, g)
        dcq_ref[...] = _bf(dx)
        dg_ref[...] += dg

    aspec = pl.BlockSpec((MLA_QR, hw), lambda b, i: (0, 0))
    tspec = pl.BlockSpec((tok, LANE), lambda b, i: (i, 0))
    return pl.pallas_call(
        body, name="mla_q_up_bwd", grid=(bsz, nb),
        in_specs=[pl.BlockSpec((tok, MLA_HEADS * QKW), lambda b, i: (b * nb + i, 0)),
                  pl.BlockSpec((tok, MLA_QR), lambda b, i: (b * nb + i, C_CQ // MLA_QR)),
                  pl.BlockSpec((1, MLA_QR), lambda b, i: (0, 0)), aspec, aspec, aspec, tspec, tspec,
                  pl.BlockSpec(memory_space=pl.ANY)],
        out_specs=[pl.BlockSpec((tok, MLA_QR), lambda b, i: (b * nb + i, C_CQ // MLA_QR)), aspec, aspec, aspec,
                   pl.BlockSpec((1, MLA_QR), lambda b, i: (0, 0))],
        out_shape=[jax.ShapeDtypeStruct((tp, N_EXT), BF16)] + [jax.ShapeDtypeStruct((MLA_QR, hw), F32)] * 3
        + [jax.ShapeDtypeStruct((1, MLA_QR), F32)],
        input_output_aliases={8: 0},
        compiler_params=_cp(("arbitrary", "arbitrary")),
    )(dqf, proj, q_norm_g, wn, wr, wt, cos_t, sin_t, dproj)


def _kv_up_bwd(dkf, dvf, proj, kv_norm_g, wk, wv, cos_t, sin_t, d_lr, dproj, bsz, lp):
    tp = bsz * lp
    tok = _attn_block(lp)
    nb = lp // tok
    hw = MLA_HEADS * LANE

    def body(dk_ref, dv_ref, ckv_ref, g_ref, wk_ref, wv_ref, cos_ref, sin_ref, dlr_ref, _,
             dp_ref, dwk_ref, dwv_ref, dg_ref):
        dckv_ref, dkr_ref, dkrot_ref = (dp_ref.at[:, j * LANE:(j + 1) * LANE] for j in range(3))
        dp_ref[:, 3 * LANE:] = dlr_ref[...]
        @pl.when(jnp.logical_and(pl.program_id(0) == 0, pl.program_id(1) == 0))
        def _():
            for r in (dwk_ref, dwv_ref, dg_ref):
                r[...] = jnp.zeros_like(r)

        g = g_ref[...]
        xh, r = _rms_fwd(ckv_ref[...].astype(F32))
        cn = _bf(xh * g)
        dv = dv_ref[...]
        dcn = _dot_nt(dv, wv_ref[...])
        dwv_ref[...] += _dot_tn(cn, dv)
        drope = jnp.zeros((tok, LANE), F32)
        for h in range(MLA_HEADS):
            sl = slice(h * LANE, (h + 1) * LANE)
            dn = dk_ref[:, h * QKW:h * QKW + LANE]
            drope += dk_ref[:, h * QKW + LANE:(h + 1) * QKW].astype(F32)
            dcn += _dot_nt(dn, wk_ref[:, sl])
            dwk_ref[:, sl] += _dot_tn(cn, dn)
        dkr_ref[...] = _bf(drope * cos_ref[...])
        dkrot_ref[...] = _bf(drope * sin_ref[...])
        dx, dg = _rms_bwd(dcn, xh, r, g)
        dckv_ref[...] = _bf(dx)
        dg_ref[...] += dg

    aspec = pl.BlockSpec((MLA_KVR, hw), lambda b, i: (0, 0))
    tspec = pl.BlockSpec((tok, LANE), lambda b, i: (i, 0))
    ospec = pl.BlockSpec((tok, LANE), lambda b, i: (b * nb + i, 0))
    return pl.pallas_call(
        body, name="mla_kv_up_bwd", grid=(bsz, nb),
        in_specs=[pl.BlockSpec((tok, MLA_HEADS * QKW), lambda b, i: (b * nb + i, 0)),
                  pl.BlockSpec((tok, hw), lambda b, i: (b * nb + i, 0)),
                  pl.BlockSpec((tok, LANE), lambda b, i: (b * nb + i, C_CKV // LANE)),
                  pl.BlockSpec((1, MLA_KVR), lambda b, i: (0, 0)), aspec, aspec, tspec, tspec, ospec,
                  pl.BlockSpec(memory_space=pl.ANY)],
        out_specs=[pl.BlockSpec((tok, 4 * LANE), lambda b, i: (b * nb + i, C_CKV // (4 * LANE))), aspec, aspec,
                   pl.BlockSpec((1, MLA_KVR), lambda b, i: (0, 0))],
        out_shape=[jax.ShapeDtypeStruct((tp, N_EXT), BF16)] + [jax.ShapeDtypeStruct((MLA_KVR, hw), F32)] * 2
        + [jax.ShapeDtypeStruct((1, MLA_KVR), F32)],
        input_output_aliases={9: 0},
        compiler_params=_cp(("arbitrary", "arbitrary")),
    )(dkf, dvf, proj, kv_norm_g, wk, wv, cos_t, sin_t, d_lr, dproj)


def _mid_fwd(ya_in, yb_in, proj, hp, target, w_gp, w_mp, w_o, final_g, bsz, lp):
    tp = bsz * lp
    tm = _attn_block(lp)
    nb = lp // tm

    def body(ya_ref, yb_ref, gg_ref, gm_ref, h_ref, t_ref, wgp_ref, wmp_ref, wo_ref, fg_ref,
             ya_out, yb_out, dh_ref, loss_ref, dfg_ref):
        @pl.when(jnp.logical_and(pl.program_id(0) == 0, pl.program_id(1) == 0))
        def _():
            loss_ref[...] = jnp.zeros_like(loss_ref)
            dfg_ref[...] = jnp.zeros_like(dfg_ref)

        y_a = _dot(ya_ref[...], wgp_ref[...])
        y_b = _dot(yb_ref[...], wmp_ref[...])
        ya_out[...] = _bf(y_a)
        yb_out[...] = _bf(y_b)
        merged = _sigmoid(gg_ref[...].astype(F32)) * y_a + _sigmoid(gm_ref[...].astype(F32)) * y_b
        h2 = h_ref[...] + _dot(_bf(merged), wo_ref[...])
        fg = fg_ref[...]
        xh, r = _rms_fwd(h2)
        pos = pl.program_id(1) * tm + lax.broadcasted_iota(jnp.int32, (tm, 1), 0)
        err = jnp.where(pos >= X0, xh * fg - t_ref[...], 0.0)
        loss_ref[...] += 0.5 * jnp.sum(jnp.mean(err * err, axis=-1, keepdims=True), axis=0, keepdims=True)
        dy = err * (1.0 / D_MODEL)
        dx, dfg = _rms_bwd(dy, xh, r, fg)
        dh_ref[...] = dx
        dfg_ref[...] += dfg

    tok = lambda c: pl.BlockSpec((tm, D_MODEL), lambda b, i: (b * nb + i, c))
    wspec = pl.BlockSpec((D_MODEL, D_MODEL), lambda b, i: (0, 0))
    return pl.pallas_call(
        body, name="mid_fwd", grid=(bsz, nb),
        in_specs=[tok(0), tok(0), tok(C_GG // D_MODEL), tok(C_GM // D_MODEL), tok(0), tok(0),
                  wspec, wspec, wspec, pl.BlockSpec((1, D_MODEL), lambda b, i: (0, 0))],
        out_specs=[tok(0), tok(0), tok(0), pl.BlockSpec((1, LANE), lambda b, i: (0, 0)),
                   pl.BlockSpec((1, D_MODEL), lambda b, i: (0, 0))],
        out_shape=[jax.ShapeDtypeStruct((tp, D_MODEL), BF16), jax.ShapeDtypeStruct((tp, D_MODEL), BF16),
                   jax.ShapeDtypeStruct((tp, D_MODEL), F32), jax.ShapeDtypeStruct((1, LANE), F32),
                   jax.ShapeDtypeStruct((1, D_MODEL), F32)],
        compiler_params=_cp(("arbitrary", "arbitrary"), 48),
    )(ya_in, yb_in, proj, proj, hp, target, w_gp, w_mp, w_o, final_g)


def _mid_bwd(dh2, y_a, y_b, proj, ya_in, yb_in, w_o, w_gp, w_mp):
    tp = dh2.shape[0]
    tm = _attn_block(tp)
    nsteps = tp // tm

    def body(dh_ref, ya_ref, yb_ref, gg_ref, gm_ref, yai_ref, ybi_ref, wo_ref, wgp_ref, wmp_ref,
             dyai_ref, dybi_ref, dgate_ref, dwo_ref, dwgp_ref, dwmp_ref, a_o, a_gp, a_mp):
        @pl.when(pl.program_id(0) == 0)
        def _():
            for r in (a_o, a_gp, a_mp):
                r[...] = jnp.zeros_like(r)

        dh = _bf(dh_ref[...])
        dm = _dot_nt(dh, wo_ref[...])
        y_a, y_b = ya_ref[...].astype(F32), yb_ref[...].astype(F32)
        sg, sm = _sigmoid(gg_ref[...].astype(F32)), _sigmoid(gm_ref[...].astype(F32))
        d_ya, d_yb = _bf(sg * dm), _bf(sm * dm)
        dgate_ref[:, :D_MODEL] = _bf(dm * y_a * sg * (1.0 - sg))
        dgate_ref[:, D_MODEL:] = _bf(dm * y_b * sm * (1.0 - sm))
        a_o[...] += _dot_tn(_bf(sg * y_a + sm * y_b), dh)
        a_gp[...] += _dot_tn(yai_ref[...], d_ya)
        a_mp[...] += _dot_tn(ybi_ref[...], d_yb)
        dyai_ref[...] = _bf(_dot_nt(d_ya, wgp_ref[...]))
        dybi_ref[...] = _bf(_dot_nt(d_yb, wmp_ref[...]))

        @pl.when(pl.program_id(0) == nsteps - 1)
        def _():
            pltpu.sync_copy(a_o, dwo_ref)
            pltpu.sync_copy(a_gp, dwgp_ref)
            pltpu.sync_copy(a_mp, dwmp_ref)

    tok = lambda c: pl.BlockSpec((tm, D_MODEL), lambda i: (i, c))
    wspec = pl.BlockSpec((D_MODEL, D_MODEL), lambda i: (0, 0))
    anyspec = pl.BlockSpec(memory_space=pl.ANY)
    wshape = jax.ShapeDtypeStruct((D_MODEL, D_MODEL), F32)
    return pl.pallas_call(
        body, name="mid_bwd", grid=(nsteps,),
        in_specs=[tok(0), tok(0), tok(0), tok(C_GG // D_MODEL), tok(C_GM // D_MODEL), tok(0), tok(0),
                  wspec, wspec, wspec],
        out_specs=[tok(0), tok(0), pl.BlockSpec((tm, 2 * D_MODEL), lambda i: (i, C_GG // (2 * D_MODEL))),
                   anyspec, anyspec, anyspec],
        out_shape=[jax.ShapeDtypeStruct((tp, D_MODEL), BF16)] * 2 + [jax.ShapeDtypeStruct((tp, N_EXT), BF16)]
        + [wshape] * 3,
        scratch_shapes=[pltpu.VMEM((D_MODEL, D_MODEL), F32)] * 3,
        compiler_params=_cp(("arbitrary",), 56),
    )(dh2, y_a, y_b, proj, proj, ya_in, yb_in, w_o, w_gp, w_mp)


MESH_ID = pl.DeviceIdType.MESH
EXCHANGE_SEMS = [pltpu.SemaphoreType.DMA((N_DEV - 1,)), pltpu.SemaphoreType.DMA((N_DEV - 1,)), pltpu.SemaphoreType.DMA]


def _my_place():
    return lax.axis_index("x"), lax.axis_index("y"), lax.axis_index("c")


def _exchange(g_ref, recv_ref, send_sems, recv_sems, local_sem, start):
    x, y, c = _my_place()
    me = 4 * x + 2 * y + c
    own = pltpu.make_async_copy(g_ref.at[me], recv_ref.at[me], local_sem)
    sends, lands = [], []
    for d in range(1, N_DEV):
        px = 1 - x if d & 4 else x
        py = 1 - y if d & 2 else y
        pc = 1 - c if d & 1 else c
        peer = 4 * px + 2 * py + pc
        for slot, group in ((me, sends),) if start else ((me, sends), (peer, lands)):
            group.append(pltpu.make_async_remote_copy(
                src_ref=g_ref.at[peer], dst_ref=recv_ref.at[slot], send_sem=send_sems.at[d - 1],
                recv_sem=recv_sems.at[d - 1], device_id=(px, py, pc), device_id_type=MESH_ID))
    if start:
        own.start()
        for cp in sends:
            cp.start()
    else:
        for cp in lands:
            cp.wait_recv()
        for cp in sends:
            cp.wait_send()
        own.wait()


def _dw_in(u, dproj, slabs):
    tp = u.shape[0]
    tm, tn = _big_tok(tp), EXT_BLOCK
    nj, ni = N_EXT // tn, tp // tm

    def body(u_ref, d_ref, g_ref, o_ref, recv_ref, send_sems, recv_sems, local_sem):
        j, i = pl.program_id(0), pl.program_id(1)

        @pl.when(jnp.logical_and(j == 0, i == 0))
        def _():
            _exchange(g_ref, recv_ref, send_sems, recv_sems, local_sem, True)

        @pl.when(i == 0)
        def _():
            o_ref[...] = jnp.zeros_like(o_ref)

        o_ref[...] += _dot_tn(d_ref[...], u_ref[...])

        @pl.when(jnp.logical_and(j == nj - 1, i == ni - 1))
        def _():
            _exchange(g_ref, recv_ref, send_sems, recv_sems, local_sem, False)

    anyspec = pl.BlockSpec(memory_space=pl.ANY)
    return pl.pallas_call(
        body, name="dw_in", grid=(nj, ni),
        in_specs=[pl.BlockSpec((tm, D_MODEL), lambda j, i: (i, 0)), pl.BlockSpec((tm, tn), lambda j, i: (i, j)), anyspec],
        out_specs=[pl.BlockSpec((tn, D_MODEL), lambda j, i: (j, 0)), anyspec],
        out_shape=[jax.ShapeDtypeStruct((N_EXT, D_MODEL), F32), jax.ShapeDtypeStruct(slabs.shape, slabs.dtype)],
        scratch_shapes=EXCHANGE_SEMS,
        compiler_params=_cp(("arbitrary", "arbitrary"), 48),
    )(u, dproj, slabs)


def _dx_in(dproj, w_ext, hp, dh2, norm_g, slabs):
    tp = hp.shape[0]
    tm, tk = _big_tok(tp), EXT_BLOCK
    nk = N_EXT // tk
    ni = tp // tm

    def body(d_ref, w_ref, h_ref, dh_ref, g_ref, s_ref, o_ref, dg_ref, recv_ref, acc, send_sems, recv_sems, local_sem):
        k = pl.program_id(1)

        @pl.when(jnp.logical_and(pl.program_id(0) == 0, k == 0))
        def _():
            _exchange(s_ref, recv_ref, send_sems, recv_sems, local_sem, True)

        @pl.when(jnp.logical_and(pl.program_id(0) == 0, k == 0))
        def _():
            dg_ref[...] = jnp.zeros_like(dg_ref)

        @pl.when(k == 0)
        def _():
            acc[...] = jnp.zeros_like(acc)

        acc[...] += _dot_nt(d_ref[...], w_ref[...])

        @pl.when(k == nk - 1)
        def _():
            g = g_ref[...]
            xh, r = _rms_fwd(h_ref[...])
            dx, dg = _rms_bwd(acc[...], xh, r, g)
            o_ref[...] = dh_ref[...] + dx
            dg_ref[...] += dg

        @pl.when(jnp.logical_and(pl.program_id(0) == ni - 1, k == nk - 1))
        def _():
            _exchange(s_ref, recv_ref, send_sems, recv_sems, local_sem, False)

    tok = pl.BlockSpec((tm, D_MODEL), lambda i, k: (i, 0))
    anyspec = pl.BlockSpec(memory_space=pl.ANY)
    return pl.pallas_call(
        body, name="dx_in", grid=(ni, nk),
        in_specs=[pl.BlockSpec((tm, tk), lambda i, k: (i, k)), pl.BlockSpec((D_MODEL, tk), lambda i, k: (0, k)),
                  tok, tok, pl.BlockSpec((1, D_MODEL), lambda i, k: (0, 0)), anyspec],
        out_specs=[tok, pl.BlockSpec((1, D_MODEL), lambda i, k: (0, 0)), anyspec],
        out_shape=[jax.ShapeDtypeStruct((tp, D_MODEL), F32), jax.ShapeDtypeStruct((1, D_MODEL), F32),
                   jax.ShapeDtypeStruct(slabs.shape, slabs.dtype)],
        scratch_shapes=[pltpu.VMEM((tm, D_MODEL), F32)] + EXCHANGE_SEMS,
        compiler_params=_cp(("arbitrary", "arbitrary"), 56),
    )(dproj, w_ext, hp, dh2, norm_g, slabs)


def _meta_grad(dhp3):
    bsz = dhp3.shape[0]

    def body(d_ref, o_ref):
        @pl.when(pl.program_id(0) == 0)
        def _():
            o_ref[...] = jnp.zeros_like(o_ref)

        o_ref[...] += d_ref[0]

    return pl.pallas_call(
        body, name="meta_grad", grid=(bsz,),
        in_specs=[pl.BlockSpec((1, N_META, D_MODEL), lambda b: (b, FRONT // N_META, 0))],
        out_specs=pl.BlockSpec((N_META, D_MODEL), lambda b: (0, 0)),
        out_shape=jax.ShapeDtypeStruct((N_META, D_MODEL), F32),
        compiler_params=_cp(("arbitrary",)),
    )(dhp3)


W_IN_SHARD = N_IN // N_DEV


def _pad_lanes(a, width=LANE):
    return jnp.pad(a, [(0, 0)] * (a.ndim - 1) + [(0, width - a.shape[-1])])


def _rot_cols(w):
    half = w.shape[-1] // 2
    return jnp.concatenate([-w[..., half:], w[..., :half]], axis=-1)


def _unrot_cols(dw):
    half = dw.shape[-1] // 2
    return jnp.concatenate([dw[..., half:], -dw[..., :half]], axis=-1)


def _w_in_cols(shards, lo, hi):
    parts = []
    for k in range(lo // W_IN_SHARD, (hi - 1) // W_IN_SHARD + 1):
        a, b = max(lo, k * W_IN_SHARD), min(hi, (k + 1) * W_IN_SHARD)
        parts.append(shards[k][:, a - k * W_IN_SHARD:b - k * W_IN_SHARD])
    return parts[0] if len(parts) == 1 else jnp.concatenate(parts, axis=1)


def _w_in_ext(shards):
    c = lambda lo, hi: _w_in_cols(shards, lo, hi)
    kr = c(O_KR, O_MZ)
    return jnp.concatenate([
        c(O_V, O_LR), c(O_Z, O_CQ), c(O_Q, O_K), c(O_K, O_V), c(O_MZ, O_GG), c(O_GG, O_GM), c(O_GM, N_IN),
        c(O_CKV, O_KR), _pad_lanes(kr), _pad_lanes(_rot_cols(kr)), _pad_lanes(c(O_LR, O_Z)), c(O_CQ, O_CKV)], axis=1)


def _w_in_grad_t(dwt):
    g = lambda start, width: dwt[start:start + width]
    half = MLA_ROPE // 2
    krot = g(C_KROT, MLA_ROPE)
    kr = g(C_KR, MLA_ROPE) + jnp.concatenate([krot[half:], -krot[:half]], axis=0)
    return jnp.concatenate([
        g(C_Q, GLA_KW), g(C_K, GLA_KW), g(C_V, GLA_VW), g(C_LR, GLA_RANK), g(C_Z, GLA_VW), g(C_CQ, MLA_QR),
        g(C_CKV, MLA_KVR), kr, g(C_MZ, D_MODEL), g(C_GG, D_MODEL), g(C_GM, D_MODEL)], axis=0)


def _rope_tables(lp):
    inv = 1.0 / (ROPE_BASE ** (jnp.arange(0, MLA_ROPE, 2, dtype=F32) / MLA_ROPE))
    ang = (jnp.arange(lp, dtype=F32) - FRONT)[:, None] * inv[None, :]
    cos, sin = jnp.cos(ang), jnp.sin(ang)
    return _pad_lanes(jnp.concatenate([cos, cos], axis=1)), _pad_lanes(jnp.concatenate([sin, sin], axis=1))


def _local_step(x, loss_target, w):
    bsz, seq, _ = x.shape
    lp = X0 + seq
    tp = bsz * lp
    assert lp % TOK == 0 and lp % GLA_ROWS == 0
    meta = jnp.broadcast_to(w["meta_tokens"][None], (bsz, N_META, D_MODEL))
    hp = jnp.concatenate([jnp.zeros((bsz, FRONT, D_MODEL), F32), meta, x], axis=1).reshape(tp, D_MODEL)
    target = jnp.pad(loss_target, ((0, 0), (X0, 0), (0, 0))).reshape(tp, D_MODEL)
    cos_t, sin_t = _rope_tables(lp)

    w_ext = _w_in_ext(w["w_in"])
    gw_pad = jnp.pad(w["gla_gate_w"], ((0, LANE - GLA_RANK), (0, 0)))
    uq = w["mla_w_uq"].reshape(MLA_QR, MLA_HEADS, MLA_QK)
    rope_w = uq[:, :, MLA_NOPE:]
    hw = MLA_HEADS * LANE
    wn = uq[:, :, :MLA_NOPE].reshape(MLA_QR, hw)
    wr = _pad_lanes(rope_w).reshape(MLA_QR, hw)
    wt = _pad_lanes(_rot_cols(rope_w)).reshape(MLA_QR, hw)
    ukv = w["mla_w_ukv"].reshape(MLA_KVR, MLA_HEADS, MLA_NOPE + MLA_DV)
    wk = ukv[:, :, :MLA_NOPE].reshape(MLA_KVR, hw)
    wv = ukv[:, :, MLA_NOPE:].reshape(MLA_KVR, hw)

    u, proj = _proj_in(hp, w["norm_g"], w_ext)
    o_raw, ya_in, s_all = _gla_fwd(proj, gw_pad, w["gla_gate_b"], w["gla_norm_g"], bsz, lp)
    qf = _q_up(proj, w["mla_q_norm_g"], wn, wr, wt, cos_t, sin_t, bsz, lp)
    kf, vf = _kv_up(proj, w["mla_kv_norm_g"], wk, wv, cos_t, sin_t, bsz, lp)
    o_b, yb_in, lse = _attn_fwd(qf, kf, vf, proj, bsz, lp)
    y_a, y_b, dh2, loss, d_final_g = _mid_fwd(ya_in, yb_in, proj, hp, target, w["gla_proj"], w["mla_proj"],
                                              w["w_out"], w["final_norm_g"], bsz, lp)
    d_ya, d_yb, dproj, d_w_out, d_gla_proj, d_mla_proj = _mid_bwd(
        dh2, y_a, y_b, proj, ya_in, yb_in, w["w_out"], w["gla_proj"], w["mla_proj"])
    dproj, d_gate, d_gla_norm = _gla_bwd(proj, gw_pad, w["gla_gate_b"], w["gla_norm_g"], o_raw, s_all, d_ya, dproj,
                                         bsz, lp)
    d_lr, d_gw_pad, d_gate_b = _gate_bwd(d_gate, proj, gw_pad)
    d_o, dproj, delta = _attn_bwd_pre(d_yb, proj, o_b, dproj, bsz, lp)
    dqf, dkf, dvf = _attn_bwd(qf, kf, vf, d_o, lse, delta, bsz, lp)
    dproj, d_wn, d_wr, d_wt, d_qn = _q_up_bwd(dqf, proj, w["mla_q_norm_g"], wn, wr, wt, cos_t, sin_t, dproj,
                                              bsz, lp)
    dproj, d_wk, d_wv, d_kvn = _kv_up_bwd(dkf, dvf, proj, w["mla_kv_norm_g"], wk, wv, cos_t, sin_t, d_lr, dproj,
                                          bsz, lp)

    d_rope = (d_wr.reshape(MLA_QR, MLA_HEADS, LANE)[:, :, :MLA_ROPE]
              + _unrot_cols(d_wt.reshape(MLA_QR, MLA_HEADS, LANE)[:, :, :MLA_ROPE]))
    d_uq = jnp.concatenate([d_wn.reshape(MLA_QR, MLA_HEADS, LANE), d_rope], axis=-1).reshape(MLA_QR, MLA_HEADS * MLA_QK)
    d_ukv = jnp.concatenate([d_wk.reshape(MLA_KVR, MLA_HEADS, LANE), d_wv.reshape(MLA_KVR, MLA_HEADS, LANE)],
                            axis=-1).reshape(MLA_KVR, MLA_HEADS * (MLA_NOPE + MLA_DV))
    mats = dict(gla_gate_w=d_gw_pad[:GLA_RANK], gla_proj=d_gla_proj, mla_w_uq=d_uq, mla_w_ukv=d_ukv,
                mla_proj=d_mla_proj, w_out=d_w_out)
    packed = _pad_rows(jnp.concatenate([_split8(mats[n], axis).reshape(N_DEV, -1) for n, _, axis in PACKED], axis=1),
                       PACK_ROWS)
    d_w_ext_t, packed_parts = _dw_in(u, dproj, _bf(packed))
    w_in_slabs = _bf(_w_in_grad_t(d_w_ext_t).reshape(N_DEV, W_IN_SHARD, D_MODEL))
    d_hp, d_norm_g, w_in_parts = _dx_in(dproj, w_ext, hp, dh2, w["norm_g"], w_in_slabs)
    d_hp3 = d_hp.reshape(bsz, lp, D_MODEL)
    small = dict(meta_tokens=_meta_grad(d_hp3), norm_g=d_norm_g, gla_gate_b=d_gate_b, gla_norm_g=d_gla_norm,
                 mla_q_norm_g=d_qn, mla_kv_norm_g=d_kvn, final_norm_g=d_final_g)
    return loss, d_hp3[:, X0:, :], w_in_parts, packed_parts, small


PACKED = (("gla_gate_w", (GLA_RANK, GLA_KW // N_DEV), 1),
          ("gla_proj", (D_MODEL // N_DEV, D_MODEL), 0), ("mla_w_uq", (MLA_QR, MLA_HEADS * MLA_QK // N_DEV), 1),
          ("mla_w_ukv", (MLA_KVR, MLA_HEADS * (MLA_NOPE + MLA_DV) // N_DEV), 1),
          ("mla_proj", (D_MODEL // N_DEV, D_MODEL), 0), ("w_out", (D_MODEL // N_DEV, D_MODEL), 0))
REPLICATED = (("norm_g", D_MODEL), ("gla_gate_b", GLA_KW), ("gla_norm_g", GLA_DV), ("mla_q_norm_g", MLA_QR),
              ("mla_kv_norm_g", MLA_KVR), ("final_norm_g", D_MODEL))
PACK_ROWS = 3744
PACK_BLOCK = 1248
SMALL_ROWS = 48
LOSS_ROW = N_META + 25
W_IN_BLOCK = 128


def _all_gather(shards):
    n_arr = len(shards)

    def body(*refs):
        x_refs, out_refs = refs[:n_arr], refs[n_arr:2 * n_arr]
        send_sems, recv_sems, local_sems = refs[2 * n_arr:]
        x, y, c = _my_place()
        me, sibling = (x, y, c), (x, y, 1 - c)
        chips = [(1 - x, y), (x, 1 - y), (1 - x, 1 - y)]

        def copy(a, k, block, to, from_input=False):
            slab = out_refs[a].at[4 * block[0] + 2 * block[1] + block[2]]
            return pltpu.make_async_remote_copy(
                src_ref=x_refs[a] if from_input else slab, dst_ref=slab,
                send_sem=send_sems.at[7 * a + k], recv_sem=recv_sems.at[7 * a + k], device_id=to,
                device_id_type=MESH_ID)

        arrays = range(n_arr)
        mine = [pltpu.make_async_copy(x_refs[a], out_refs[a].at[4 * x + 2 * y + c], local_sems.at[a]) for a in arrays]
        for cp in mine:
            cp.start()
        first = [copy(a, 0, me, sibling, True) for a in arrays]
        first += [copy(a, 1 + j, me, (*chip, c), True) for j, chip in enumerate(chips) for a in arrays]
        for cp in first:
            cp.start()
        passed = []
        for j, chip in enumerate(chips):
            for a in arrays:
                copy(a, 1 + j, (*chip, c), me).wait_recv()
                passed.append(copy(a, 4 + j, (*chip, c), sibling))
                passed[-1].start()
        for a in arrays:
            copy(a, 0, sibling, me).wait_recv()
        for j, chip in enumerate(chips):
            for a in arrays:
                copy(a, 4 + j, (*chip, 1 - c), me).wait_recv()
        for cp in first + passed:
            cp.wait_send()
        for cp in mine:
            cp.wait()

    anyspec = pl.BlockSpec(memory_space=pl.ANY)
    return pl.pallas_call(
        body, name="weights_all_gather",
        out_shape=[jax.ShapeDtypeStruct((N_DEV,) + s.shape, s.dtype) for s in shards],
        in_specs=[anyspec] * n_arr, out_specs=[anyspec] * n_arr,
        scratch_shapes=[pltpu.SemaphoreType.DMA((7 * n_arr,)), pltpu.SemaphoreType.DMA((7 * n_arr,)),
                        pltpu.SemaphoreType.DMA((n_arr,))],
    )(*shards)


def _small_exchange(slabs):
    def body(g_ref, recv_ref, send_sems, recv_sems, local_sem):
        _exchange(g_ref, recv_ref, send_sems, recv_sems, local_sem, True)
        _exchange(g_ref, recv_ref, send_sems, recv_sems, local_sem, False)

    vmem = pl.BlockSpec(memory_space=pltpu.VMEM)
    return pl.pallas_call(
        body, name="small_exchange", out_shape=jax.ShapeDtypeStruct(slabs.shape, slabs.dtype),
        in_specs=[vmem], out_specs=vmem, scratch_shapes=EXCHANGE_SEMS,
    )(slabs)


def _adamw(parts, w, m, v, block_rows, name):
    rows, cols = w.shape

    def body(p_ref, w_ref, m_ref, v_ref, g_out, d_out, m_out, v_out):
        g = p_ref[0].astype(F32)
        for s in range(1, N_DEV):
            g = g + p_ref[s].astype(F32)
        m_new = ADAM_B1 * m_ref[...] + (1.0 - ADAM_B1) * g
        v_new = ADAM_B2 * v_ref[...] + (1.0 - ADAM_B2) * (g * g)
        m_hat = m_new / (1.0 - ADAM_B1 ** ADAM_STEP)
        v_hat = v_new / (1.0 - ADAM_B2 ** ADAM_STEP)
        g_out[...] = g
        d_out[...] = -ADAM_LR * (m_hat / (jnp.sqrt(v_hat) + ADAM_EPS) + ADAM_WD * w_ref[...])
        m_out[...] = m_new
        v_out[...] = v_new

    spec = pl.BlockSpec((block_rows, cols), lambda i: (i, 0))
    return pl.pallas_call(
        body, name=name, grid=(pl.cdiv(rows, block_rows),),
        in_specs=[pl.BlockSpec((N_DEV, block_rows, cols), lambda i: (0, i, 0)), spec, spec, spec],
        out_specs=[spec] * 4, out_shape=[jax.ShapeDtypeStruct((rows, cols), F32)] * 4,
        compiler_params=_cp(("parallel",), 48),
    )(parts, w, m, v)


def _pad_rows(flat, rows):
    pad = rows * LANE - flat.shape[-1]
    flat = jnp.pad(flat, [(0, 0)] * (flat.ndim - 1) + [(0, pad)])
    return flat.reshape(flat.shape[:-1] + (rows, LANE))


def _pack_shards(shards):
    return _pad_rows(jnp.concatenate([shards[n].reshape(-1) for n, _, _ in PACKED]), PACK_ROWS)


def _unpack_shards(packed):
    flat, out, off = packed.reshape(-1), {}, 0
    for n, shape, _ in PACKED:
        size = shape[0] * shape[1]
        out[n] = flat[off:off + size].reshape(shape)
        off += size
    return out


def _split8(full, axis):
    r, c = full.shape
    if axis == 0:
        return full.reshape(N_DEV, r // N_DEV, c)
    return full.reshape(r, N_DEV, c // N_DEV).transpose(1, 0, 2)


def _join8(shards, axis):
    _, r, c = shards.shape
    if axis == 0:
        return shards.reshape(N_DEV * r, c)
    return shards.transpose(1, 0, 2).reshape(r, N_DEV * c)


def _pack_small(meta_shard, vals, loss_row):
    rows = jnp.concatenate([vals[n].reshape(-1, LANE) for n, _ in REPLICATED] + [loss_row], axis=0)
    rows = jnp.pad(rows, ((0, SMALL_ROWS - N_META - rows.shape[0]), (0, 0)))
    return jnp.concatenate([meta_shard, jnp.broadcast_to(rows, meta_shard.shape[:-2] + rows.shape)], axis=-2)


def _unpack_small(packed):
    out, off = {"meta_tokens": packed[:N_META]}, N_META
    for n, size in REPLICATED:
        out[n] = packed[off:off + size // LANE].reshape(1, size)
        off += size // LANE
    return out


def kernel(x, meta_tokens, norm_g, w_in, gla_gate_w, gla_gate_b, gla_norm_g, gla_proj, mla_q_norm_g, mla_w_uq, mla_kv_norm_g, mla_w_ukv, mla_proj, w_out, final_norm_g, loss_target, m_meta_tokens, m_norm_g, m_w_in, m_gla_gate_w, m_gla_gate_b, m_gla_norm_g, m_gla_proj, m_mla_q_norm_g, m_mla_w_uq, m_mla_kv_norm_g, m_mla_w_ukv, m_mla_proj, m_w_out, m_final_norm_g, v_meta_tokens, v_norm_g, v_w_in, v_gla_gate_w, v_gla_gate_b, v_gla_norm_g, v_gla_proj, v_mla_q_norm_g, v_mla_w_uq, v_mla_kv_norm_g, v_mla_w_ukv, v_mla_proj, v_w_out, v_final_norm_g):
    given = dict(meta_tokens=meta_tokens, norm_g=norm_g, w_in=w_in, gla_gate_w=gla_gate_w, gla_gate_b=gla_gate_b,
                 gla_norm_g=gla_norm_g, gla_proj=gla_proj, mla_q_norm_g=mla_q_norm_g, mla_w_uq=mla_w_uq,
                 mla_kv_norm_g=mla_kv_norm_g, mla_w_ukv=mla_w_ukv, mla_proj=mla_proj, w_out=w_out,
                 final_norm_g=final_norm_g)
    mom_m = dict(meta_tokens=m_meta_tokens, norm_g=m_norm_g, w_in=m_w_in, gla_gate_w=m_gla_gate_w,
                 gla_gate_b=m_gla_gate_b, gla_norm_g=m_gla_norm_g, gla_proj=m_gla_proj, mla_q_norm_g=m_mla_q_norm_g,
                 mla_w_uq=m_mla_w_uq, mla_kv_norm_g=m_mla_kv_norm_g, mla_w_ukv=m_mla_w_ukv, mla_proj=m_mla_proj,
                 w_out=m_w_out, final_norm_g=m_final_norm_g)
    mom_v = dict(meta_tokens=v_meta_tokens, norm_g=v_norm_g, w_in=v_w_in, gla_gate_w=v_gla_gate_w,
                 gla_gate_b=v_gla_gate_b, gla_norm_g=v_gla_norm_g, gla_proj=v_gla_proj, mla_q_norm_g=v_mla_q_norm_g,
                 mla_w_uq=v_mla_w_uq, mla_kv_norm_g=v_mla_kv_norm_g, mla_w_ukv=v_mla_w_ukv, mla_proj=v_mla_proj,
                 w_out=v_w_out, final_norm_g=v_final_norm_g)
    shapes = {n: a.shape for n, a in given.items()}
    shard2d = {n: s for n, s, _ in PACKED}
    shard2d["w_in"] = (D_MODEL, W_IN_SHARD)
    shard2d["meta_tokens"] = (N_META, LANE)

    def as2d(tree):
        out = {n: tree[n].reshape(shard2d[n]) for n in shard2d}
        out.update({n: tree[n].reshape(1, size) for n, size in REPLICATED})
        return out

    w_loc, m_loc, v_loc = as2d(given), as2d(mom_m), as2d(mom_v)

    flat = jnp.concatenate([w_loc[n].astype(BF16).reshape(-1) for n, _, _ in PACKED])
    w_in_all, packed_all, meta_all = _all_gather(
        [w_loc["w_in"].astype(BF16), _pad_rows(flat, PACK_ROWS), w_loc["meta_tokens"]])
    packed_all = packed_all.reshape(N_DEV, -1)
    full, off = {"w_in": w_in_all, "meta_tokens": _join8(meta_all, 1)}, 0
    for n, shape, axis in PACKED:
        size = shape[0] * shape[1]
        full[n] = _join8(packed_all[:, off:off + size].reshape((N_DEV,) + shape), axis)
        off += size
    for n, _ in REPLICATED:
        full[n] = w_loc[n]

    loss_part, grad_x, w_in_parts, packed_parts, small = _local_step(x, loss_target, full)
    small_all = _small_exchange(_pack_small(_split8(small["meta_tokens"], 1), small,
                                            jnp.broadcast_to(loss_part[:, :1], (1, LANE))))

    w_in_t = [t["w_in"].T for t in (w_loc, m_loc, v_loc)]
    g_w, d_w, m_w, v_w = (o.T for o in _adamw(w_in_parts, *w_in_t, W_IN_BLOCK, "adamw_w_in"))
    g_p, d_p, m_p, v_p = _adamw(packed_parts, _pack_shards(w_loc), _pack_shards(m_loc), _pack_shards(v_loc),
                                PACK_BLOCK, "adamw_packed")
    zero_row = jnp.zeros((1, LANE), F32)
    g_s, d_s, m_s, v_s = _adamw(small_all, *(_pack_small(t["meta_tokens"], t, zero_row) for t in (w_loc, m_loc, v_loc)),
                                SMALL_ROWS, "adamw_small")
    loss = g_s[LOSS_ROW, 0]

    order = ["meta_tokens", "norm_g", "w_in", "gla_gate_w", "gla_gate_b", "gla_norm_g", "gla_proj", "mla_q_norm_g",
             "mla_w_uq", "mla_kv_norm_g", "mla_w_ukv", "mla_proj", "w_out", "final_norm_g"]
    result = [loss, grad_x]
    for w_in_out, packed_sh, packed_sm in ((g_w, g_p, g_s), (d_w, d_p, d_s), (m_w, m_p, m_s), (v_w, v_p, v_s)):
        tree = _unpack_shards(packed_sh)
        tree.update(_unpack_small(packed_sm))
        tree["w_in"] = w_in_out
        result += [tree[n].reshape(shapes[n]) for n in order]
    return tuple(result)
```

```python
import jax
import jax.numpy as jnp
from jax import lax
from jax.experimental import pallas as pl
from jax.experimental.pallas import tpu as pltpu

F32 = jnp.float32
BF16 = jnp.bfloat16

D_MODEL = 1024
N_META = 16
EPS = 1e-6
FRONT = 48
X0 = FRONT + N_META
GLA_HEADS, GLA_DK, GLA_DV, GLA_RANK, GLA_CHUNK = 4, 128, 256, 16, 64
GLA_GATE_NORMALIZER = 16.0
GLA_KW = GLA_HEADS * GLA_DK
GLA_VW = GLA_HEADS * GLA_DV
MLA_HEADS, MLA_NOPE, MLA_ROPE, MLA_DV, MLA_QR, MLA_KVR = 8, 128, 64, 128, 256, 128
MLA_QK = MLA_NOPE + MLA_ROPE
ROPE_BASE = 10000.0
LANE = 128
QKW = 2 * LANE

C_V, C_Z, C_Q, C_K = 0, 1024, 2048, 2560
C_MZ, C_GG, C_GM = 3072, 4096, 5120
C_CKV, C_KR, C_KROT, C_LR = 6144, 6272, 6400, 6528
C_CQ = 6656
N_EXT = 6912
O_Q, O_K, O_V, O_LR, O_Z, O_CQ, O_CKV, O_KR, O_MZ, O_GG, O_GM, N_IN = (
    0, 512, 1024, 2048, 2064, 3088, 3344, 3472, 3536, 4560, 5584, 6608)

ADAM_LR, ADAM_B1, ADAM_B2, ADAM_EPS, ADAM_WD, ADAM_STEP = 0.001, 0.9, 0.999, 1e-08, 0.01, 10

N_DEV = 8
TOK = 192
ATT_BLOCK = 352
EXT_BLOCK = 1152


def _cp(sems=None, vmem_mb=None):
    kw = {}
    if sems is not None:
        kw["dimension_semantics"] = sems
    if vmem_mb is not None:
        kw["vmem_limit_bytes"] = vmem_mb * 1024 * 1024
    return pltpu.CompilerParams(**kw)


def _dot(a, b):
    return jnp.dot(a, b, preferred_element_type=F32)


def _dot_nt(a, b):
    return lax.dot_general(a, b, (((1,), (1,)), ((), ())), preferred_element_type=F32)


def _dot_tn(a, b):
    return lax.dot_general(a, b, (((0,), (0,)), ((), ())), preferred_element_type=F32)


def _sigmoid(x):
    return 1.0 / (1.0 + jnp.exp(-x))


def _bf(x):
    return x.astype(BF16)


def _big_tok(tp):
    return 4 * TOK if tp % (4 * TOK) == 0 else TOK


def _attn_block(lp):
    return ATT_BLOCK if lp % ATT_BLOCK == 0 else TOK


def _proj_in(hp, norm_g, w_ext):
    tp = hp.shape[0]
    tm, tn = _big_tok(tp), EXT_BLOCK

    def body(h_ref, g_ref, w_ref, u_ref, o_ref, u_scr):
        @pl.when(pl.program_id(1) == 0)
        def _():
            x = h_ref[...]
            r = lax.rsqrt(jnp.mean(x * x, axis=-1, keepdims=True) + EPS)
            u = _bf(x * r * g_ref[...])
            u_scr[...] = u
            u_ref[...] = u

        o_ref[...] = _bf(_dot(u_scr[...], w_ref[...]))

    return pl.pallas_call(
        body, name="proj_in", grid=(tp // tm, N_EXT // tn),
        in_specs=[pl.BlockSpec((tm, D_MODEL), lambda i, j: (i, 0)),
                  pl.BlockSpec((1, D_MODEL), lambda i, j: (0, 0)),
                  pl.BlockSpec((D_MODEL, tn), lambda i, j: (0, j))],
        out_specs=[pl.BlockSpec((tm, D_MODEL), lambda i, j: (i, 0)),
                   pl.BlockSpec((tm, tn), lambda i, j: (i, j))],
        out_shape=[jax.ShapeDtypeStruct((tp, D_MODEL), BF16), jax.ShapeDtypeStruct((tp, N_EXT), BF16)],
        scratch_shapes=[pltpu.VMEM((tm, D_MODEL), BF16)],
        compiler_params=_cp(("parallel", "arbitrary"), 48),
    )(hp, norm_g, w_ext)


GLA_GROUP = 3
GLA_ROWS = GLA_GROUP * GLA_CHUNK


def _tri_dot(tri, x):
    hi = _bf(x)
    rest = x - hi.astype(F32)
    mid = _bf(rest)
    return _dot(tri, hi) + _dot(tri, mid) + _dot(tri, _bf(rest - mid.astype(F32)))


def _gla_gates(q_ref, k_ref, lr_ref, gw_ref, gb_ref, rows, not_first):
    z = _dot(lr_ref[rows, :], gw_ref[...]) + gb_ref[...]
    logsig = jnp.minimum(z, 0.0) - jnp.log(1.0 + jnp.exp(-jnp.abs(z)))
    row = lax.broadcasted_iota(jnp.int32, (GLA_CHUNK, GLA_KW), 0)
    live = jnp.logical_or(not_first, row >= FRONT)
    g = jnp.where(live, logsig * (1.0 / GLA_GATE_NORMALIZER), 0.0)
    ri = lax.broadcasted_iota(jnp.int32, (GLA_CHUNK, GLA_CHUNK), 0)
    ci = lax.broadcasted_iota(jnp.int32, (GLA_CHUNK, GLA_CHUNK), 1)
    tril = ci <= ri
    b = _tri_dot(_bf(tril.astype(F32)), g)
    bl = jnp.sum(jnp.where(row == GLA_CHUNK - 1, b, 0.0), axis=0, keepdims=True)
    eb, enb, elb, ebl = jnp.exp(b), jnp.exp(-b), jnp.exp(bl - b), jnp.exp(bl)
    q = q_ref[rows, :].astype(F32) * (GLA_DK ** -0.5)
    k = k_ref[rows, :].astype(F32)
    qe, ke, kl = q * eb, k * enb, k * elb
    return dict(z=z, live=live, tril=tril, row=row, eb=eb, enb=enb, elb=elb, ebl=ebl, qe=qe, ke=ke, kl=kl,
                qe_b=_bf(qe), ke_b=_bf(ke), kl_b=_bf(kl))


def _gla_in_specs(n_groups, rev):
    def rb(b, n):
        return b * n_groups + ((n_groups - 1 - n) if rev else n)

    return rb, [pl.BlockSpec((GLA_ROWS, GLA_KW), lambda b, n: (rb(b, n), C_Q // GLA_KW)),
                pl.BlockSpec((GLA_ROWS, GLA_KW), lambda b, n: (rb(b, n), C_K // GLA_KW)),
                pl.BlockSpec((GLA_ROWS, GLA_VW), lambda b, n: (rb(b, n), C_V // GLA_VW)),
                pl.BlockSpec((GLA_ROWS, GLA_VW), lambda b, n: (rb(b, n), C_Z // GLA_VW)),
                pl.BlockSpec((GLA_ROWS, LANE), lambda b, n: (rb(b, n), C_LR // LANE)),
                pl.BlockSpec((LANE, GLA_KW), lambda b, n: (0, 0)),
                pl.BlockSpec((1, GLA_KW), lambda b, n: (0, 0)),
                pl.BlockSpec((1, GLA_DV), lambda b, n: (0, 0))]


def _gla_fwd(proj, gw_pad, gate_b, gla_norm_g, bsz, lp):
    n_chunks = lp // GLA_CHUNK
    n_groups = n_chunks // GLA_GROUP
    tp = bsz * lp

    def body(q_ref, k_ref, v_ref, z_ref, lr_ref, gw_ref, gb_ref, gn_ref, oraw_ref, ya_ref, sall_ref, st_scr):
        grp = pl.program_id(1)

        @pl.when(grp == 0)
        def _():
            st_scr[...] = jnp.zeros_like(st_scr)

        chunks = [slice(j * GLA_CHUNK, (j + 1) * GLA_CHUNK) for j in range(GLA_GROUP)]
        cs = [_gla_gates(q_ref, k_ref, lr_ref, gw_ref, gb_ref, rows, True if j else grp > 0)
              for j, rows in enumerate(chunks)]
        gn = gn_ref[...]
        for h in range(GLA_HEADS):
            ks, vs = slice(h * GLA_DK, (h + 1) * GLA_DK), slice(h * GLA_DV, (h + 1) * GLA_DV)
            st = st_scr[h]
            for j, (rows, c) in enumerate(zip(chunks, cs)):
                sall_ref[0, j, h] = st
                v = v_ref[rows, vs]
                a = jnp.where(c["tril"], _dot_nt(c["qe_b"][:, ks], c["ke_b"][:, ks]), 0.0)
                o = _dot(_bf(a), v) + _dot_nt(c["qe_b"][:, ks], _bf(st))
                st = st * c["ebl"][:, ks] + _dot_tn(v, c["kl_b"][:, ks])
                oraw_ref[rows, vs] = o
                r = lax.rsqrt(jnp.mean(o * o, axis=-1, keepdims=True) + EPS)
                zg = z_ref[rows, vs].astype(F32)
                ya_ref[rows, vs] = _bf((o * r * gn) * (zg * _sigmoid(zg)))
            st_scr[h] = st

    rb, in_specs = _gla_in_specs(n_groups, False)
    return pl.pallas_call(
        body, name="gla_fwd", grid=(bsz, n_groups), in_specs=in_specs,
        out_specs=[pl.BlockSpec((GLA_ROWS, GLA_VW), lambda b, n: (rb(b, n), 0)),
                   pl.BlockSpec((GLA_ROWS, GLA_VW), lambda b, n: (rb(b, n), 0)),
                   pl.BlockSpec((1, GLA_GROUP, GLA_HEADS, GLA_DV, GLA_DK), lambda b, n: (b, n, 0, 0, 0))],
        out_shape=[jax.ShapeDtypeStruct((tp, GLA_VW), F32), jax.ShapeDtypeStruct((tp, GLA_VW), BF16),
                   jax.ShapeDtypeStruct((bsz, n_chunks, GLA_HEADS, GLA_DV, GLA_DK), F32)],
        scratch_shapes=[pltpu.VMEM((GLA_HEADS, GLA_DV, GLA_DK), F32)],
        compiler_params=_cp(("parallel", "arbitrary")),
    )(proj, proj, proj, proj, proj, gw_pad, gate_b, gla_norm_g)


def _gla_bwd(proj, gw_pad, gate_b, gla_norm_g, o_raw, s_all, d_ya, dproj, bsz, lp):
    n_chunks = lp // GLA_CHUNK
    n_groups = n_chunks // GLA_GROUP
    tp = bsz * lp

    def body(q_ref, k_ref, v_ref, z_ref, lr_ref, gw_ref, gb_ref, gn_ref, o_ref, s_ref, dya_ref, _,
             dp_ref, dz_ref, dgn_ref, dst_scr):
        dv_ref, dzg_ref = dp_ref.at[:, C_V:C_V + GLA_VW], dp_ref.at[:, C_Z:C_Z + GLA_VW]

        @pl.when(jnp.logical_and(pl.program_id(0) == 0, pl.program_id(1) == 0))
        def _():
            dgn_ref[...] = jnp.zeros_like(dgn_ref)

        @pl.when(pl.program_id(1) == 0)
        def _():
            dst_scr[...] = jnp.zeros_like(dst_scr)

        grp = n_groups - 1 - pl.program_id(1)
        chunks = [slice(j * GLA_CHUNK, (j + 1) * GLA_CHUNK) for j in range(GLA_GROUP)]
        cs = [_gla_gates(q_ref, k_ref, lr_ref, gw_ref, gb_ref, rows, True if j else grp > 0)
              for j, rows in enumerate(chunks)]
        gn = gn_ref[...]
        dgn = jnp.zeros((1, GLA_DV), F32)
        dqe_h, dke_h, dkl_h, dbl_h = ([[None] * GLA_HEADS for _ in chunks] for _ in range(4))
        for h in range(GLA_HEADS):
            ks, vs = slice(h * GLA_DK, (h + 1) * GLA_DK), slice(h * GLA_DV, (h + 1) * GLA_DV)
            dst = dst_scr[h]
            for j in reversed(range(GLA_GROUP)):
                rows, c = chunks[j], cs[j]
                v = v_ref[rows, vs]
                st = s_ref[0, j, h]
                o = o_ref[rows, vs]
                r = lax.rsqrt(jnp.mean(o * o, axis=-1, keepdims=True) + EPS)
                xh = o * r
                zg = z_ref[rows, vs].astype(F32)
                sg = _sigmoid(zg)
                dy = dya_ref[rows, vs].astype(F32)
                dzg_ref[rows, vs] = _bf(dy * (xh * gn) * (sg * (1.0 + zg * (1.0 - sg))))
                t = dy * (zg * sg)
                dgn += jnp.sum(t * xh, axis=0, keepdims=True)
                dxh = t * gn
                do_b = _bf(r * (dxh - xh * jnp.mean(dxh * xh, axis=-1, keepdims=True)))
                qe_b, ke_b, kl_b, dst_b = c["qe_b"][:, ks], c["ke_b"][:, ks], c["kl_b"][:, ks], _bf(dst)
                a = jnp.where(c["tril"], _dot_nt(qe_b, ke_b), 0.0)
                da_b = _bf(jnp.where(c["tril"], _dot_nt(do_b, v), 0.0))
                dqe_h[j][h] = _dot(da_b, ke_b) + _dot(do_b, _bf(st))
                dke_h[j][h] = _dot_tn(da_b, qe_b)
                dkl = _dot(v, dst_b)
                dkl_h[j][h] = dkl
                dv_ref[rows, vs] = _bf(_dot_tn(_bf(a), do_b) + _dot_nt(kl_b, dst_b))
                ddecay = jnp.sum(dst * st, axis=0, keepdims=True)
                dbl_h[j][h] = jnp.sum(dkl * c["kl"][:, ks], axis=0, keepdims=True) + ddecay * c["ebl"][:, ks]
                dst = dst * c["ebl"][:, ks] + _dot_tn(do_b, qe_b)
            dst_scr[h] = dst
        dgn_ref[...] += dgn
        ri = lax.broadcasted_iota(jnp.int32, (GLA_CHUNK, GLA_CHUNK), 0)
        ci = lax.broadcasted_iota(jnp.int32, (GLA_CHUNK, GLA_CHUNK), 1)
        triu = _bf((ci >= ri).astype(F32))
        for j, (rows, c) in enumerate(zip(chunks, cs)):
            dqe, dke, dkl, dbl = (jnp.concatenate(p[j], axis=1) for p in (dqe_h, dke_h, dkl_h, dbl_h))
            db = dqe * c["qe"] - dke * c["ke"] - dkl * c["kl"] + jnp.where(c["row"] == GLA_CHUNK - 1, dbl, 0.0)
            dg = _tri_dot(triu, db)
            dg = jnp.where(c["live"], dg, 0.0)
            dz_ref[rows, :] = dg * (1.0 / GLA_GATE_NORMALIZER) * _sigmoid(-c["z"])
            dp_ref[rows, C_Q:C_Q + GLA_KW] = _bf(dqe * c["eb"] * (GLA_DK ** -0.5))
            dp_ref[rows, C_K:C_K + GLA_KW] = _bf(dke * c["enb"] + dkl * c["elb"])

    rb, in_specs = _gla_in_specs(n_groups, True)
    wide = pl.BlockSpec((GLA_ROWS, GLA_VW), lambda b, n: (rb(b, n), 0))
    group = C_MZ
    return pl.pallas_call(
        body, name="gla_bwd", grid=(bsz, n_groups),
        in_specs=in_specs + [wide, pl.BlockSpec((1, GLA_GROUP, GLA_HEADS, GLA_DV, GLA_DK),
                                                lambda b, n: (b, n_groups - 1 - n, 0, 0, 0)), wide,
                             pl.BlockSpec(memory_space=pl.ANY)],
        out_specs=[pl.BlockSpec((GLA_ROWS, group), lambda b, n: (rb(b, n), 0)),
                   pl.BlockSpec((GLA_ROWS, GLA_KW), lambda b, n: (rb(b, n), 0)),
                   pl.BlockSpec((1, GLA_DV), lambda b, n: (0, 0))],
        out_shape=[jax.ShapeDtypeStruct((tp, N_EXT), BF16), jax.ShapeDtypeStruct((tp, GLA_KW), F32),
                   jax.ShapeDtypeStruct((1, GLA_DV), F32)],
        input_output_aliases={11: 0},
        scratch_shapes=[pltpu.VMEM((GLA_HEADS, GLA_DV, GLA_DK), F32)],
        compiler_params=_cp(("arbitrary", "arbitrary")),
    )(proj, proj, proj, proj, proj, gw_pad, gate_b, gla_norm_g, o_raw, s_all, d_ya, dproj)


def _gate_bwd(dz, proj, gw_pad):
    tp = dz.shape[0]
    tm = _big_tok(tp)

    def body(dz_ref, lr_ref, gw_ref, dlr_ref, dgw_ref, dgb_ref):
        @pl.when(pl.program_id(0) == 0)
        def _():
            dgw_ref[...] = jnp.zeros_like(dgw_ref)
            dgb_ref[...] = jnp.zeros_like(dgb_ref)

        dz = dz_ref[...]
        dz_b = _bf(dz)
        dlr_ref[...] = _bf(_dot_nt(dz_b, gw_ref[...]))
        dgw_ref[...] += _dot_tn(lr_ref[...], dz_b)
        dgb_ref[...] += jnp.sum(dz, axis=0, keepdims=True)

    return pl.pallas_call(
        body, name="gate_bwd", grid=(tp // tm,),
        in_specs=[pl.BlockSpec((tm, GLA_KW), lambda i: (i, 0)),
                  pl.BlockSpec((tm, LANE), lambda i: (i, C_LR // LANE)),
                  pl.BlockSpec((LANE, GLA_KW), lambda i: (0, 0))],
        out_specs=[pl.BlockSpec((tm, LANE), lambda i: (i, 0)),
                   pl.BlockSpec((LANE, GLA_KW), lambda i: (0, 0)),
                   pl.BlockSpec((1, GLA_KW), lambda i: (0, 0))],
        out_shape=[jax.ShapeDtypeStruct((tp, LANE), BF16), jax.ShapeDtypeStruct((LANE, GLA_KW), F32),
                   jax.ShapeDtypeStruct((1, GLA_KW), F32)],
        compiler_params=_cp(("arbitrary",)),
    )(dz, proj, gw_pad)


def _rms_fwd(x):
    r = lax.rsqrt(jnp.mean(x * x, axis=-1, keepdims=True) + EPS)
    return x * r, r


def _rms_bwd(dy, xh, r, g):
    dxh = dy * g
    dx = r * (dxh - xh * jnp.mean(dxh * xh, axis=-1, keepdims=True))
    return dx, jnp.sum(dy * xh, axis=0, keepdims=True)


def _q_up(proj, q_norm_g, wn, wr, wt, cos_t, sin_t, bsz, lp):
    tp = bsz * lp
    tok = _attn_block(lp)
    nb = lp // tok

    def body(cq_ref, g_ref, wn_ref, wr_ref, wt_ref, cos_ref, sin_ref, q_ref):
        xh, _ = _rms_fwd(cq_ref[...].astype(F32))
        cqn = _bf(xh * g_ref[...])
        nope = _dot(cqn, wn_ref[...])
        rope = _dot(cqn, wr_ref[...])
        rot = _dot(cqn, wt_ref[...])
        cos, sin = cos_ref[...], sin_ref[...]
        one = (lax.broadcasted_iota(jnp.int32, (tok, LANE), 1) == BIAS_LANE).astype(F32)
        for h in range(MLA_HEADS):
            sl = slice(h * LANE, (h + 1) * LANE)
            q_ref[:, h * QKW:h * QKW + LANE] = _bf(nope[:, sl])
            q_ref[:, h * QKW + LANE:(h + 1) * QKW] = _bf(rope[:, sl] * cos + rot[:, sl] * sin + one)

    wspec = pl.BlockSpec((MLA_QR, MLA_HEADS * LANE), lambda b, i: (0, 0))
    tspec = pl.BlockSpec((tok, LANE), lambda b, i: (i, 0))
    return pl.pallas_call(
        body, name="mla_q_up", grid=(bsz, nb),
        in_specs=[pl.BlockSpec((tok, MLA_QR), lambda b, i: (b * nb + i, C_CQ // MLA_QR)),
                  pl.BlockSpec((1, MLA_QR), lambda b, i: (0, 0)), wspec, wspec, wspec, tspec, tspec],
        out_specs=pl.BlockSpec((tok, MLA_HEADS * QKW), lambda b, i: (b * nb + i, 0)),
        out_shape=jax.ShapeDtypeStruct((tp, MLA_HEADS * QKW), BF16),
        compiler_params=_cp(("parallel", "parallel")),
    )(proj, q_norm_g, wn, wr, wt, cos_t, sin_t)


def _kv_up(proj, kv_norm_g, wk, wv, cos_t, sin_t, bsz, lp):
    tp = bsz * lp
    tok = _attn_block(lp)
    nb = lp // tok

    def body(ckv_ref, kr_ref, krot_ref, g_ref, wk_ref, wv_ref, cos_ref, sin_ref, k_ref, v_ref):
        xh, _ = _rms_fwd(ckv_ref[...].astype(F32))
        cn = _bf(xh * g_ref[...])
        kn = _dot(cn, wk_ref[...])
        v_ref[...] = _bf(_dot(cn, wv_ref[...]))
        pos = pl.program_id(1) * tok + lax.broadcasted_iota(jnp.int32, (tok, LANE), 0)
        lane = lax.broadcasted_iota(jnp.int32, (tok, LANE), 1)
        bias = jnp.where(jnp.logical_and(lane == BIAS_LANE, pos < FRONT), KEY_BIAS, 0.0)
        kr = _bf(kr_ref[...].astype(F32) * cos_ref[...] + krot_ref[...].astype(F32) * sin_ref[...] + bias)
        for h in range(MLA_HEADS):
            k_ref[:, h * QKW:h * QKW + LANE] = _bf(kn[:, h * LANE:(h + 1) * LANE])
            k_ref[:, h * QKW + LANE:(h + 1) * QKW] = kr

    wspec = pl.BlockSpec((MLA_KVR, MLA_HEADS * LANE), lambda b, i: (0, 0))
    tspec = pl.BlockSpec((tok, LANE), lambda b, i: (i, 0))
    return pl.pallas_call(
        body, name="mla_kv_up", grid=(bsz, nb),
        in_specs=[pl.BlockSpec((tok, LANE), lambda b, i: (b * nb + i, C_CKV // LANE)),
                  pl.BlockSpec((tok, LANE), lambda b, i: (b * nb + i, C_KR // LANE)),
                  pl.BlockSpec((tok, LANE), lambda b, i: (b * nb + i, C_KROT // LANE)),
                  pl.BlockSpec((1, MLA_KVR), lambda b, i: (0, 0)), wspec, wspec, tspec, tspec],
        out_specs=[pl.BlockSpec((tok, MLA_HEADS * QKW), lambda b, i: (b * nb + i, 0)),
                   pl.BlockSpec((tok, MLA_HEADS * LANE), lambda b, i: (b * nb + i, 0))],
        out_shape=[jax.ShapeDtypeStruct((tp, MLA_HEADS * QKW), BF16),
                   jax.ShapeDtypeStruct((tp, MLA_HEADS * LANE), BF16)],
        compiler_params=_cp(("parallel", "parallel")),
    )(proj, proj, proj, kv_norm_g, wk, wv, cos_t, sin_t)


ATT_SCALE = MLA_QK ** -0.5


KEY_BIAS = -1e30
BIAS_LANE = MLA_ROPE
NEG = 2 * KEY_BIAS
LOG2E = 1.4426950408889634
EXP2_SCALE = ATT_SCALE * LOG2E


def _causal_fill(s, r0, fill):
    tq, kmax = s.shape
    a = r0 // LANE * LANE
    mask = (a + lax.broadcasted_iota(jnp.int32, (tq, kmax - a), 1)
            <= r0 + lax.broadcasted_iota(jnp.int32, (tq, kmax - a), 0))
    right = jnp.where(mask, s[:, a:], fill)
    return jnp.concatenate([s[:, :a], right], axis=1) if a else right


def _attn_fwd(qf, kf, vf, proj, bsz, lp):
    tp = bsz * lp
    tq = _attn_block(lp)

    def body(q_ref, k_ref, v_ref, mz_ref, ob_ref, yb_ref, lse_ref):
        for r0 in range(0, lp, tq):
            rows, kmax = slice(r0, r0 + tq), r0 + tq
            s = _causal_fill(_dot_nt(q_ref[rows, :], k_ref[0:kmax, :]), r0, NEG)
            m = jnp.max(s, axis=-1, keepdims=True)
            p = jnp.exp2((s - m) * EXP2_SCALE)
            l = jnp.sum(p, axis=-1, keepdims=True)
            o = _dot(_bf(p), v_ref[0:kmax, :]) / l
            ob_ref[rows, :] = _bf(o)
            mz = mz_ref[rows, :].astype(F32)
            yb_ref[rows, :] = _bf(o * (mz * _sigmoid(mz)))
            lse_ref[0, 0, rows, :] = jnp.broadcast_to(m * EXP2_SCALE + jnp.log2(l), (tq, LANE))

    head = lambda off: pl.BlockSpec((lp, MLA_DV), lambda b, h: (b, off + h))
    return pl.pallas_call(
        body, name="mla_attn_fwd", grid=(bsz, MLA_HEADS),
        in_specs=[pl.BlockSpec((lp, QKW), lambda b, h: (b, h)), pl.BlockSpec((lp, QKW), lambda b, h: (b, h)),
                  head(0), head(C_MZ // MLA_DV)],
        out_specs=[head(0), head(0), pl.BlockSpec((1, 1, lp, LANE), lambda b, h: (b, h, 0, 0))],
        out_shape=[jax.ShapeDtypeStruct((tp, MLA_HEADS * MLA_DV), BF16),
                   jax.ShapeDtypeStruct((tp, MLA_HEADS * MLA_DV), BF16),
                   jax.ShapeDtypeStruct((bsz, MLA_HEADS, lp, LANE), F32)],
        compiler_params=_cp(("parallel", "parallel"), 56),
    )(qf, kf, vf, proj)


def _attn_bwd(qf, kf, vf, d_o, lse, delta, bsz, lp):
    tp = bsz * lp
    tq = _attn_block(lp)

    def body(q_ref, k_ref, v_ref, do_ref, lse_ref, dl_ref, dq_ref, dk_ref, dv_ref, dk_acc, dv_acc):
        dk_acc[...] = jnp.zeros_like(dk_acc)
        dv_acc[...] = jnp.zeros_like(dv_acc)
        for r0 in range(0, lp, tq):
            rows, kmax = slice(r0, r0 + tq), r0 + tq
            q, do = q_ref[rows, :], do_ref[rows, :]
            k, v = k_ref[0:kmax, :], v_ref[0:kmax, :]
            p = jnp.exp2(_dot_nt(q, k) * EXP2_SCALE - lse_ref[0, 0, rows, :][:, :1])
            p = _causal_fill(p, r0, 0.0)
            ds = _bf(p * (_dot_nt(do, v) - dl_ref[0, 0, rows, :][:, :1]))
            dq_ref[rows, :] = _bf(_dot(ds, k) * ATT_SCALE)
            dk_acc[0:kmax, :] += _dot_tn(ds, q)
            dv_acc[0:kmax, :] += _dot_tn(_bf(p), do)
        dk_ref[...] = _bf(dk_acc[...] * ATT_SCALE)
        dv_ref[...] = _bf(dv_acc[...])

    wide = pl.BlockSpec((lp, QKW), lambda b, h: (b, h))
    narrow = pl.BlockSpec((lp, MLA_DV), lambda b, h: (b, h))
    stat = pl.BlockSpec((1, 1, lp, LANE), lambda b, h: (b, h, 0, 0))
    return pl.pallas_call(
        body, name="mla_attn_bwd", grid=(bsz, MLA_HEADS),
        in_specs=[wide, wide, narrow, narrow, stat, stat], out_specs=[wide, wide, narrow],
        out_shape=[jax.ShapeDtypeStruct((tp, MLA_HEADS * QKW), BF16), jax.ShapeDtypeStruct((tp, MLA_HEADS * QKW), BF16),
                   jax.ShapeDtypeStruct((tp, MLA_HEADS * MLA_DV), BF16)],
        scratch_shapes=[pltpu.VMEM((lp, QKW), F32), pltpu.VMEM((lp, MLA_DV), F32)],
        compiler_params=_cp(("parallel", "parallel"), 56),
    )(qf, kf, vf, d_o, lse, delta)


def _q_up_bwd(dqf, proj, q_norm_g, wn, wr, wt, cos_t, sin_t, dproj, bsz, lp):
    tp = bsz * lp
    tok = _attn_block(lp)
    nb = lp // tok
    hw = MLA_HEADS * LANE

    def body(dq_ref, cq_ref, g_ref, wn_ref, wr_ref, wt_ref, cos_ref, sin_ref, _,
             dcq_ref, dwn_ref, dwr_ref, dwt_ref, dg_ref):
        @pl.when(jnp.logical_and(pl.program_id(0) == 0, pl.program_id(1) == 0))
        def _():
            for r in (dwn_ref, dwr_ref, dwt_ref, dg_ref):
                r[...] = jnp.zeros_like(r)

        g = g_ref[...]
        xh, r = _rms_fwd(cq_ref[...].astype(F32))
        cqn = _bf(xh * g)
        cos, sin = cos_ref[...], sin_ref[...]
        dcqn = jnp.zeros((tok, MLA_QR), F32)
        for h in range(MLA_HEADS):
            sl = slice(h * LANE, (h + 1) * LANE)
            dn = dq_ref[:, h * QKW:h * QKW + LANE]
            dr = dq_ref[:, h * QKW + LANE:(h + 1) * QKW].astype(F32)
            dr_c, dr_s = _bf(dr * cos), _bf(dr * sin)
            dcqn += _dot_nt(dn, wn_ref[:, sl]) + _dot_nt(dr_c, wr_ref[:, sl]) + _dot_nt(dr_s, wt_ref[:, sl])
            dwn_ref[:, sl] += _dot_tn(cqn, dn)
            dwr_ref[:, sl] += _dot_tn(cqn, dr_c)
            dwt_ref[:, sl] += _dot_tn(cqn, dr_s)
        dx, dg = _rms_bwd(dcqn, xh, r, g)
        dcq_ref[...] = _bf(dx)
        dg_ref[...] += dg

    aspec = pl.BlockSpec((MLA_QR, hw), lambda b, i: (0, 0))
    tspec = pl.BlockSpec((tok, LANE), lambda b, i: (i, 0))
    return pl.pallas_call(
        body, name="mla_q_up_bwd", grid=(bsz, nb),
        in_specs=[pl.BlockSpec((tok, MLA_HEADS * QKW), lambda b, i: (b * nb + i, 0)),
                  pl.BlockSpec((tok, MLA_QR), lambda b, i: (b * nb + i, C_CQ // MLA_QR)),
                  pl.BlockSpec((1, MLA_QR), lambda b, i: (0, 0)), aspec, aspec, aspec, tspec, tspec,
                  pl.BlockSpec(memory_space=pl.ANY)],
        out_specs=[pl.BlockSpec((tok, MLA_QR), lambda b, i: (b * nb + i, C_CQ // MLA_QR)), aspec, aspec, aspec,
                   pl.BlockSpec((1, MLA_QR), lambda b, i: (0, 0))],
        out_shape=[jax.ShapeDtypeStruct((tp, N_EXT), BF16)] + [jax.ShapeDtypeStruct((MLA_QR, hw), F32)] * 3
        + [jax.ShapeDtypeStruct((1, MLA_QR), F32)],
        input_output_aliases={8: 0},
        compiler_params=_cp(("arbitrary", "arbitrary")),
    )(dqf, proj, q_norm_g, wn, wr, wt, cos_t, sin_t, dproj)


def _kv_up_bwd(dkf, dvf, proj, kv_norm_g, wk, wv, cos_t, sin_t, d_lr, dproj, bsz, lp):
    tp = bsz * lp
    tok = _attn_block(lp)
    nb = lp // tok
    hw = MLA_HEADS * LANE

    def body(dk_ref, dv_ref, ckv_ref, g_ref, wk_ref, wv_ref, cos_ref, sin_ref, dlr_ref, _,
             dp_ref, dwk_ref, dwv_ref, dg_ref):
        dckv_ref, dkr_ref, dkrot_ref = (dp_ref.at[:, j * LANE:(j + 1) * LANE] for j in range(3))
        dp_ref[:, 3 * LANE:] = dlr_ref[...]
        @pl.when(jnp.logical_and(pl.program_id(0) == 0, pl.program_id(1) == 0))
        def _():
            for r in (dwk_ref, dwv_ref, dg_ref):
                r[...] = jnp.zeros_like(r)

        g = g_ref[...]
        xh, r = _rms_fwd(ckv_ref[...].astype(F32))
        cn = _bf(xh * g)
        dv = dv_ref[...]
        dcn = _dot_nt(dv, wv_ref[...])
        dwv_ref[...] += _dot_tn(cn, dv)
        drope = jnp.zeros((tok, LANE), F32)
        for h in range(MLA_HEADS):
            sl = slice(h * LANE, (h + 1) * LANE)
            dn = dk_ref[:, h * QKW:h * QKW + LANE]
            drope += dk_ref[:, h * QKW + LANE:(h + 1) * QKW].astype(F32)
            dcn += _dot_nt(dn, wk_ref[:, sl])
            dwk_ref[:, sl] += _dot_tn(cn, dn)
        dkr_ref[...] = _bf(drope * cos_ref[...])
        dkrot_ref[...] = _bf(drope * sin_ref[...])
        dx, dg = _rms_bwd(dcn, xh, r, g)
        dckv_ref[...] = _bf(dx)
        dg_ref[...] += dg

    aspec = pl.BlockSpec((MLA_KVR, hw), lambda b, i: (0, 0))
    tspec = pl.BlockSpec((tok, LANE), lambda b, i: (i, 0))
    ospec = pl.BlockSpec((tok, LANE), lambda b, i: (b * nb + i, 0))
    return pl.pallas_call(
        body, name="mla_kv_up_bwd", grid=(bsz, nb),
        in_specs=[pl.BlockSpec((tok, MLA_HEADS * QKW), lambda b, i: (b * nb + i, 0)),
                  pl.BlockSpec((tok, hw), lambda b, i: (b * nb + i, 0)),
                  pl.BlockSpec((tok, LANE), lambda b, i: (b * nb + i, C_CKV // LANE)),
                  pl.BlockSpec((1, MLA_KVR), lambda b, i: (0, 0)), aspec, aspec, tspec, tspec, ospec,
                  pl.BlockSpec(memory_space=pl.ANY)],
        out_specs=[pl.BlockSpec((tok, 4 * LANE), lambda b, i: (b * nb + i, C_CKV // (4 * LANE))), aspec, aspec,
                   pl.BlockSpec((1, MLA_KVR), lambda b, i: (0, 0))],
        out_shape=[jax.ShapeDtypeStruct((tp, N_EXT), BF16)] + [jax.ShapeDtypeStruct((MLA_KVR, hw), F32)] * 2
        + [jax.ShapeDtypeStruct((1, MLA_KVR), F32)],
        input_output_aliases={9: 0},
        compiler_params=_cp(("arbitrary", "arbitrary")),
    )(dkf, dvf, proj, kv_norm_g, wk, wv, cos_t, sin_t, d_lr, dproj)


def _mid_fwd(ya_in, yb_in, proj, hp, target, w_gp, w_mp, w_o, final_g, bsz, lp):
    tp = bsz * lp
    tm = _attn_block(lp)
    nb = lp // tm

    def body(ya_ref, yb_ref, gg_ref, gm_ref, h_ref, t_ref, wgp_ref, wmp_ref, wo_ref, fg_ref,
             ya_out, yb_out, dh_ref, loss_ref, dfg_ref):
        @pl.when(jnp.logical_and(pl.program_id(0) == 0, pl.program_id(1) == 0))
        def _():
            loss_ref[...] = jnp.zeros_like(loss_ref)
            dfg_ref[...] = jnp.zeros_like(dfg_ref)

        y_a = _dot(ya_ref[...], wgp_ref[...])
        y_b = _dot(yb_ref[...], wmp_ref[...])
        ya_out[...] = _bf(y_a)
        yb_out[...] = _bf(y_b)
        merged = _sigmoid(gg_ref[...].astype(F32)) * y_a + _sigmoid(gm_ref[...].astype(F32)) * y_b
        h2 = h_ref[...] + _dot(_bf(merged), wo_ref[...])
        fg = fg_ref[...]
        xh, r = _rms_fwd(h2)
        pos = pl.program_id(1) * tm + lax.broadcasted_iota(jnp.int32, (tm, 1), 0)
        err = jnp.where(pos >= X0, xh * fg - t_ref[...], 0.0)
        loss_ref[...] += 0.5 * jnp.sum(jnp.mean(err * err, axis=-1, keepdims=True), axis=0, keepdims=True)
        dy = err * (1.0 / D_MODEL)
        dx, dfg = _rms_bwd(dy, xh, r, fg)
        dh_ref[...] = dx
        dfg_ref[...] += dfg

    tok = lambda c: pl.BlockSpec((tm, D_MODEL), lambda b, i: (b * nb + i, c))
    wspec = pl.BlockSpec((D_MODEL, D_MODEL), lambda b, i: (0, 0))
    return pl.pallas_call(
        body, name="mid_fwd", grid=(bsz, nb),
        in_specs=[tok(0), tok(0), tok(C_GG // D_MODEL), tok(C_GM // D_MODEL), tok(0), tok(0),
                  wspec, wspec, wspec, pl.BlockSpec((1, D_MODEL), lambda b, i: (0, 0))],
        out_specs=[tok(0), tok(0), tok(0), pl.BlockSpec((1, LANE), lambda b, i: (0, 0)),
                   pl.BlockSpec((1, D_MODEL), lambda b, i: (0, 0))],
        out_shape=[jax.ShapeDtypeStruct((tp, D_MODEL), BF16), jax.ShapeDtypeStruct((tp, D_MODEL), BF16),
                   jax.ShapeDtypeStruct((tp, D_MODEL), F32), jax.ShapeDtypeStruct((1, LANE), F32),
                   jax.ShapeDtypeStruct((1, D_MODEL), F32)],
        compiler_params=_cp(("arbitrary", "arbitrary"), 48),
    )(ya_in, yb_in, proj, proj, hp, target, w_gp, w_mp, w_o, final_g)


def _mid_bwd(dh2, y_a, y_b, proj, ya_in, yb_in, o_b, w_o, w_gp, w_mp, bsz, lp):
    tp = bsz * lp
    tm = _attn_block(lp)
    nb = lp // tm
    nsteps = tp // tm
    group = 3 * D_MODEL

    def body(dh_ref, ya_ref, yb_ref, mz_ref, gg_ref, gm_ref, yai_ref, ybi_ref, ob_ref, wo_ref, wgp_ref, wmp_ref,
             dyai_ref, do_ref, dp_ref, dl_ref, dwo_ref, dwgp_ref, dwmp_ref, a_o, a_gp, a_mp):
        @pl.when(pl.program_id(0) == 0)
        def _():
            for r in (a_o, a_gp, a_mp):
                r[...] = jnp.zeros_like(r)

        dh = _bf(dh_ref[...])
        dm = _dot_nt(dh, wo_ref[...])
        y_a, y_b = ya_ref[...].astype(F32), yb_ref[...].astype(F32)
        sg, sm = _sigmoid(gg_ref[...].astype(F32)), _sigmoid(gm_ref[...].astype(F32))
        d_ya, d_yb = _bf(sg * dm), _bf(sm * dm)
        dp_ref[:, D_MODEL:2 * D_MODEL] = _bf(dm * y_a * sg * (1.0 - sg))
        dp_ref[:, 2 * D_MODEL:] = _bf(dm * y_b * sm * (1.0 - sm))
        a_o[...] += _dot_tn(_bf(sg * y_a + sm * y_b), dh)
        a_gp[...] += _dot_tn(yai_ref[...], d_ya)
        a_mp[...] += _dot_tn(ybi_ref[...], d_yb)
        dyai_ref[...] = _bf(_dot_nt(d_ya, wgp_ref[...]))
        dy = _dot_nt(d_yb, wmp_ref[...])
        mz, o = mz_ref[...].astype(F32), ob_ref[...].astype(F32)
        s = _sigmoid(mz)
        do = _bf(dy * (mz * s))
        do_ref[...] = do
        dp_ref[:, :D_MODEL] = _bf(dy * o * (s * (1.0 + mz * (1.0 - s))))
        prod = do.astype(F32) * o
        for h in range(MLA_HEADS):
            dl = jnp.sum(prod[:, h * MLA_DV:(h + 1) * MLA_DV], axis=-1, keepdims=True)
            dl_ref[0, h] = jnp.broadcast_to(dl, (tm, LANE))

        @pl.when(pl.program_id(0) == nsteps - 1)
        def _():
            pltpu.sync_copy(a_o, dwo_ref)
            pltpu.sync_copy(a_gp, dwgp_ref)
            pltpu.sync_copy(a_mp, dwmp_ref)

    tok = lambda c: pl.BlockSpec((tm, D_MODEL), lambda i: (i, c))
    wspec = pl.BlockSpec((D_MODEL, D_MODEL), lambda i: (0, 0))
    anyspec = pl.BlockSpec(memory_space=pl.ANY)
    wshape = jax.ShapeDtypeStruct((D_MODEL, D_MODEL), F32)
    return pl.pallas_call(
        body, name="mid_bwd", grid=(nsteps,),
        in_specs=[tok(0), tok(0), tok(0), tok(C_MZ // D_MODEL), tok(C_GG // D_MODEL), tok(C_GM // D_MODEL),
                  tok(0), tok(0), tok(0), wspec, wspec, wspec],
        out_specs=[tok(0), tok(0), pl.BlockSpec((tm, group), lambda i: (i, C_MZ // group)),
                   pl.BlockSpec((1, MLA_HEADS, tm, LANE), lambda i: (i // nb, 0, i % nb, 0)),
                   anyspec, anyspec, anyspec],
        out_shape=[jax.ShapeDtypeStruct((tp, D_MODEL), BF16)] * 2 + [jax.ShapeDtypeStruct((tp, N_EXT), BF16),
                   jax.ShapeDtypeStruct((bsz, MLA_HEADS, lp, LANE), F32)] + [wshape] * 3,
        scratch_shapes=[pltpu.VMEM((D_MODEL, D_MODEL), F32)] * 3,
        compiler_params=_cp(("arbitrary",), 56),
    )(dh2, y_a, y_b, proj, proj, proj, ya_in, yb_in, o_b, w_o, w_gp, w_mp)


MESH_ID = pl.DeviceIdType.MESH
EXCHANGE_SEMS = [pltpu.SemaphoreType.DMA((N_DEV - 1,)), pltpu.SemaphoreType.DMA((N_DEV - 1,)), pltpu.SemaphoreType.DMA]


def _my_place():
    return lax.axis_index("x"), lax.axis_index("y"), lax.axis_index("c")


def _exchange(g_ref, recv_ref, send_sems, recv_sems, local_sem, start):
    x, y, c = _my_place()
    me = 4 * x + 2 * y + c
    own = pltpu.make_async_copy(g_ref.at[me], recv_ref.at[me], local_sem)
    sends, lands = [], []
    for d in range(1, N_DEV):
        px = 1 - x if d & 4 else x
        py = 1 - y if d & 2 else y
        pc = 1 - c if d & 1 else c
        peer = 4 * px + 2 * py + pc
        for slot, group in ((me, sends),) if start else ((me, sends), (peer, lands)):
            group.append(pltpu.make_async_remote_copy(
                src_ref=g_ref.at[peer], dst_ref=recv_ref.at[slot], send_sem=send_sems.at[d - 1],
                recv_sem=recv_sems.at[d - 1], device_id=(px, py, pc), device_id_type=MESH_ID))
    if start:
        own.start()
        for cp in sends:
            cp.start()
    else:
        for cp in lands:
            cp.wait_recv()
        for cp in sends:
            cp.wait_send()
        own.wait()


def _dw_in(u, dproj, slabs):
    tp = u.shape[0]
    tm, tn = _big_tok(tp), EXT_BLOCK
    nj, ni = N_EXT // tn, tp // tm

    def body(u_ref, d_ref, g_ref, o_ref, recv_ref, send_sems, recv_sems, local_sem):
        j, i = pl.program_id(0), pl.program_id(1)

        @pl.when(jnp.logical_and(j == 0, i == 0))
        def _():
            _exchange(g_ref, recv_ref, send_sems, recv_sems, local_sem, True)

        @pl.when(i == 0)
        def _():
            o_ref[...] = jnp.zeros_like(o_ref)

        o_ref[...] += _dot_tn(d_ref[...], u_ref[...])

        @pl.when(jnp.logical_and(j == nj - 1, i == ni - 1))
        def _():
            _exchange(g_ref, recv_ref, send_sems, recv_sems, local_sem, False)

    anyspec = pl.BlockSpec(memory_space=pl.ANY)
    return pl.pallas_call(
        body, name="dw_in", grid=(nj, ni),
        in_specs=[pl.BlockSpec((tm, D_MODEL), lambda j, i: (i, 0)), pl.BlockSpec((tm, tn), lambda j, i: (i, j)), anyspec],
        out_specs=[pl.BlockSpec((tn, D_MODEL), lambda j, i: (j, 0)), anyspec],
        out_shape=[jax.ShapeDtypeStruct((N_EXT, D_MODEL), F32), jax.ShapeDtypeStruct(slabs.shape, slabs.dtype)],
        scratch_shapes=EXCHANGE_SEMS,
        compiler_params=_cp(("arbitrary", "arbitrary"), 48),
    )(u, dproj, slabs)


def _dx_in(dproj, w_ext, hp, dh2, norm_g, slabs):
    tp = hp.shape[0]
    tm, tk = _big_tok(tp), EXT_BLOCK
    nk = N_EXT // tk
    ni = tp // tm

    def body(d_ref, w_ref, h_ref, dh_ref, g_ref, s_ref, o_ref, dg_ref, recv_ref, acc, send_sems, recv_sems, local_sem):
        k = pl.program_id(1)

        @pl.when(jnp.logical_and(pl.program_id(0) == 0, k == 0))
        def _():
            _exchange(s_ref, recv_ref, send_sems, recv_sems, local_sem, True)

        @pl.when(jnp.logical_and(pl.program_id(0) == 0, k == 0))
        def _():
            dg_ref[...] = jnp.zeros_like(dg_ref)

        @pl.when(k == 0)
        def _():
            acc[...] = jnp.zeros_like(acc)

        acc[...] += _dot_nt(d_ref[...], w_ref[...])

        @pl.when(k == nk - 1)
        def _():
            g = g_ref[...]
            xh, r = _rms_fwd(h_ref[...])
            dx, dg = _rms_bwd(acc[...], xh, r, g)
            o_ref[...] = dh_ref[...] + dx
            dg_ref[...] += dg

        @pl.when(jnp.logical_and(pl.program_id(0) == ni - 1, k == nk - 1))
        def _():
            _exchange(s_ref, recv_ref, send_sems, recv_sems, local_sem, False)

    tok = pl.BlockSpec((tm, D_MODEL), lambda i, k: (i, 0))
    anyspec = pl.BlockSpec(memory_space=pl.ANY)
    return pl.pallas_call(
        body, name="dx_in", grid=(ni, nk),
        in_specs=[pl.BlockSpec((tm, tk), lambda i, k: (i, k)), pl.BlockSpec((D_MODEL, tk), lambda i, k: (0, k)),
                  tok, tok, pl.BlockSpec((1, D_MODEL), lambda i, k: (0, 0)), anyspec],
        out_specs=[tok, pl.BlockSpec((1, D_MODEL), lambda i, k: (0, 0)), anyspec],
        out_shape=[jax.ShapeDtypeStruct((tp, D_MODEL), F32), jax.ShapeDtypeStruct((1, D_MODEL), F32),
                   jax.ShapeDtypeStruct(slabs.shape, slabs.dtype)],
        scratch_shapes=[pltpu.VMEM((tm, D_MODEL), F32)] + EXCHANGE_SEMS,
        compiler_params=_cp(("arbitrary", "arbitrary"), 56),
    )(dproj, w_ext, hp, dh2, norm_g, slabs)


def _meta_grad(dhp3):
    bsz = dhp3.shape[0]

    def body(d_ref, o_ref):
        @pl.when(pl.program_id(0) == 0)
        def _():
            o_ref[...] = jnp.zeros_like(o_ref)

        o_ref[...] += d_ref[0]

    return pl.pallas_call(
        body, name="meta_grad", grid=(bsz,),
        in_specs=[pl.BlockSpec((1, N_META, D_MODEL), lambda b: (b, FRONT // N_META, 0))],
        out_specs=pl.BlockSpec((N_META, D_MODEL), lambda b: (0, 0)),
        out_shape=jax.ShapeDtypeStruct((N_META, D_MODEL), F32),
        compiler_params=_cp(("arbitrary",)),
    )(dhp3)


W_IN_SHARD = N_IN // N_DEV


def _pad_lanes(a, width=LANE):
    return jnp.pad(a, [(0, 0)] * (a.ndim - 1) + [(0, width - a.shape[-1])])


def _rot_cols(w):
    half = w.shape[-1] // 2
    return jnp.concatenate([-w[..., half:], w[..., :half]], axis=-1)


def _unrot_cols(dw):
    half = dw.shape[-1] // 2
    return jnp.concatenate([dw[..., half:], -dw[..., :half]], axis=-1)


def _w_in_cols(shards, lo, hi):
    parts = []
    for k in range(lo // W_IN_SHARD, (hi - 1) // W_IN_SHARD + 1):
        a, b = max(lo, k * W_IN_SHARD), min(hi, (k + 1) * W_IN_SHARD)
        parts.append(shards[k][:, a - k * W_IN_SHARD:b - k * W_IN_SHARD])
    return parts[0] if len(parts) == 1 else jnp.concatenate(parts, axis=1)


def _w_in_ext(shards):
    c = lambda lo, hi: _w_in_cols(shards, lo, hi)
    kr = c(O_KR, O_MZ)
    return jnp.concatenate([
        c(O_V, O_LR), c(O_Z, O_CQ), c(O_Q, O_K), c(O_K, O_V), c(O_MZ, O_GG), c(O_GG, O_GM), c(O_GM, N_IN),
        c(O_CKV, O_KR), _pad_lanes(kr), _pad_lanes(_rot_cols(kr)), _pad_lanes(c(O_LR, O_Z)), c(O_CQ, O_CKV)], axis=1)


def _w_in_grad_t(dwt):
    g = lambda start, width: dwt[start:start + width]
    half = MLA_ROPE // 2
    krot = g(C_KROT, MLA_ROPE)
    kr = g(C_KR, MLA_ROPE) + jnp.concatenate([krot[half:], -krot[:half]], axis=0)
    return jnp.concatenate([
        g(C_Q, GLA_KW), g(C_K, GLA_KW), g(C_V, GLA_VW), g(C_LR, GLA_RANK), g(C_Z, GLA_VW), g(C_CQ, MLA_QR),
        g(C_CKV, MLA_KVR), kr, g(C_MZ, D_MODEL), g(C_GG, D_MODEL), g(C_GM, D_MODEL)], axis=0)


def _rope_tables(lp):
    inv = 1.0 / (ROPE_BASE ** (jnp.arange(0, MLA_ROPE, 2, dtype=F32) / MLA_ROPE))
    ang = (jnp.arange(lp, dtype=F32) - FRONT)[:, None] * inv[None, :]
    cos, sin = jnp.cos(ang), jnp.sin(ang)
    return _pad_lanes(jnp.concatenate([cos, cos], axis=1)), _pad_lanes(jnp.concatenate([sin, sin], axis=1))


def _local_step(x, loss_target, w):
    bsz, seq, _ = x.shape
    lp = X0 + seq
    tp = bsz * lp
    assert lp % TOK == 0 and lp % GLA_ROWS == 0
    meta = jnp.broadcast_to(w["meta_tokens"][None], (bsz, N_META, D_MODEL))
    hp = jnp.concatenate([jnp.zeros((bsz, FRONT, D_MODEL), F32), meta, x], axis=1).reshape(tp, D_MODEL)
    target = jnp.pad(loss_target, ((0, 0), (X0, 0), (0, 0))).reshape(tp, D_MODEL)
    cos_t, sin_t = _rope_tables(lp)

    w_ext = _w_in_ext(w["w_in"])
    gw_pad = jnp.pad(w["gla_gate_w"], ((0, LANE - GLA_RANK), (0, 0)))
    uq = w["mla_w_uq"].reshape(MLA_QR, MLA_HEADS, MLA_QK)
    rope_w = uq[:, :, MLA_NOPE:]
    hw = MLA_HEADS * LANE
    wn = uq[:, :, :MLA_NOPE].reshape(MLA_QR, hw)
    wr = _pad_lanes(rope_w).reshape(MLA_QR, hw)
    wt = _pad_lanes(_rot_cols(rope_w)).reshape(MLA_QR, hw)
    ukv = w["mla_w_ukv"].reshape(MLA_KVR, MLA_HEADS, MLA_NOPE + MLA_DV)
    wk = ukv[:, :, :MLA_NOPE].reshape(MLA_KVR, hw)
    wv = ukv[:, :, MLA_NOPE:].reshape(MLA_KVR, hw)

    u, proj = _proj_in(hp, w["norm_g"], w_ext)
    o_raw, ya_in, s_all = _gla_fwd(proj, gw_pad, w["gla_gate_b"], w["gla_norm_g"], bsz, lp)
    qf = _q_up(proj, w["mla_q_norm_g"], wn, wr, wt, cos_t, sin_t, bsz, lp)
    kf, vf = _kv_up(proj, w["mla_kv_norm_g"], wk, wv, cos_t, sin_t, bsz, lp)
    o_b, yb_in, lse = _attn_fwd(qf, kf, vf, proj, bsz, lp)
    y_a, y_b, dh2, loss, d_final_g = _mid_fwd(ya_in, yb_in, proj, hp, target, w["gla_proj"], w["mla_proj"],
                                              w["w_out"], w["final_norm_g"], bsz, lp)
    d_ya, d_o, dproj, delta, d_w_out, d_gla_proj, d_mla_proj = _mid_bwd(
        dh2, y_a, y_b, proj, ya_in, yb_in, o_b, w["w_out"], w["gla_proj"], w["mla_proj"], bsz, lp)
    dproj, d_gate, d_gla_norm = _gla_bwd(proj, gw_pad, w["gla_gate_b"], w["gla_norm_g"], o_raw, s_all, d_ya, dproj,
                                         bsz, lp)
    d_lr, d_gw_pad, d_gate_b = _gate_bwd(d_gate, proj, gw_pad)
    dqf, dkf, dvf = _attn_bwd(qf, kf, vf, d_o, lse, delta, bsz, lp)
    dproj, d_wn, d_wr, d_wt, d_qn = _q_up_bwd(dqf, proj, w["mla_q_norm_g"], wn, wr, wt, cos_t, sin_t, dproj,
                                              bsz, lp)
    dproj, d_wk, d_wv, d_kvn = _kv_up_bwd(dkf, dvf, proj, w["mla_kv_norm_g"], wk, wv, cos_t, sin_t, d_lr, dproj,
                                          bsz, lp)

    d_rope = (d_wr.reshape(MLA_QR, MLA_HEADS, LANE)[:, :, :MLA_ROPE]
              + _unrot_cols(d_wt.reshape(MLA_QR, MLA_HEADS, LANE)[:, :, :MLA_ROPE]))
    d_uq = jnp.concatenate([d_wn.reshape(MLA_QR, MLA_HEADS, LANE), d_rope], axis=-1).reshape(MLA_QR, MLA_HEADS * MLA_QK)
    d_ukv = jnp.concatenate([d_wk.reshape(MLA_KVR, MLA_HEADS, LANE), d_wv.reshape(MLA_KVR, MLA_HEADS, LANE)],
                            axis=-1).reshape(MLA_KVR, MLA_HEADS * (MLA_NOPE + MLA_DV))
    mats = dict(gla_gate_w=d_gw_pad[:GLA_RANK], gla_proj=d_gla_proj, mla_w_uq=d_uq, mla_w_ukv=d_ukv,
                mla_proj=d_mla_proj, w_out=d_w_out)
    packed = _pad_rows(jnp.concatenate([_split8(mats[n], axis).reshape(N_DEV, -1) for n, _, axis in PACKED], axis=1),
                       PACK_ROWS)
    d_w_ext_t, packed_parts = _dw_in(u, dproj, _bf(packed))
    w_in_slabs = _bf(_w_in_grad_t(d_w_ext_t).reshape(N_DEV, W_IN_SHARD, D_MODEL))
    d_hp, d_norm_g, w_in_parts = _dx_in(dproj, w_ext, hp, dh2, w["norm_g"], w_in_slabs)
    d_hp3 = d_hp.reshape(bsz, lp, D_MODEL)
    small = dict(meta_tokens=_meta_grad(d_hp3), norm_g=d_norm_g, gla_gate_b=d_gate_b, gla_norm_g=d_gla_norm,
                 mla_q_norm_g=d_qn, mla_kv_norm_g=d_kvn, final_norm_g=d_final_g)
    return loss, d_hp3[:, X0:, :], w_in_parts, packed_parts, small


PACKED = (("gla_gate_w", (GLA_RANK, GLA_KW // N_DEV), 1),
          ("gla_proj", (D_MODEL // N_DEV, D_MODEL), 0), ("mla_w_uq", (MLA_QR, MLA_HEADS * MLA_QK // N_DEV), 1),
          ("mla_w_ukv", (MLA_KVR, MLA_HEADS * (MLA_NOPE + MLA_DV) // N_DEV), 1),
          ("mla_proj", (D_MODEL // N_DEV, D_MODEL), 0), ("w_out", (D_MODEL // N_DEV, D_MODEL), 0))
REPLICATED = (("norm_g", D_MODEL), ("gla_gate_b", GLA_KW), ("gla_norm_g", GLA_DV), ("mla_q_norm_g", MLA_QR),
              ("mla_kv_norm_g", MLA_KVR), ("final_norm_g", D_MODEL))
PACK_ROWS = 3744
PACK_BLOCK = 1248
SMALL_ROWS = 48
LOSS_ROW = N_META + 25
W_IN_BLOCK = 128


def _all_gather(shards):
    n_arr = len(shards)

    def body(*refs):
        x_refs, out_refs = refs[:n_arr], refs[n_arr:2 * n_arr]
        send_sems, recv_sems, local_sems = refs[2 * n_arr:]
        x, y, c = _my_place()
        me, sibling = (x, y, c), (x, y, 1 - c)
        chips = [(1 - x, y), (x, 1 - y), (1 - x, 1 - y)]

        def copy(a, k, block, to, from_input=False):
            slab = out_refs[a].at[4 * block[0] + 2 * block[1] + block[2]]
            return pltpu.make_async_remote_copy(
                src_ref=x_refs[a] if from_input else slab, dst_ref=slab,
                send_sem=send_sems.at[7 * a + k], recv_sem=recv_sems.at[7 * a + k], device_id=to,
                device_id_type=MESH_ID)

        arrays = range(n_arr)
        mine = [pltpu.make_async_copy(x_refs[a], out_refs[a].at[4 * x + 2 * y + c], local_sems.at[a]) for a in arrays]
        for cp in mine:
            cp.start()
        first = [copy(a, 0, me, sibling, True) for a in arrays]
        first += [copy(a, 1 + j, me, (*chip, c), True) for j, chip in enumerate(chips) for a in arrays]
        for cp in first:
            cp.start()
        passed = []
        for j, chip in enumerate(chips):
            for a in arrays:
                copy(a, 1 + j, (*chip, c), me).wait_recv()
                passed.append(copy(a, 4 + j, (*chip, c), sibling))
                passed[-1].start()
        for a in arrays:
            copy(a, 0, sibling, me).wait_recv()
        for j, chip in enumerate(chips):
            for a in arrays:
                copy(a, 4 + j, (*chip, 1 - c), me).wait_recv()
        for cp in first + passed:
            cp.wait_send()
        for cp in mine:
            cp.wait()

    anyspec = pl.BlockSpec(memory_space=pl.ANY)
    return pl.pallas_call(
        body, name="weights_all_gather",
        out_shape=[jax.ShapeDtypeStruct((N_DEV,) + s.shape, s.dtype) for s in shards],
        in_specs=[anyspec] * n_arr, out_specs=[anyspec] * n_arr,
        scratch_shapes=[pltpu.SemaphoreType.DMA((7 * n_arr,)), pltpu.SemaphoreType.DMA((7 * n_arr,)),
                        pltpu.SemaphoreType.DMA((n_arr,))],
    )(*shards)


def _small_exchange(slabs):
    def body(g_ref, recv_ref, send_sems, recv_sems, local_sem):
        _exchange(g_ref, recv_ref, send_sems, recv_sems, local_sem, True)
        _exchange(g_ref, recv_ref, send_sems, recv_sems, local_sem, False)

    vmem = pl.BlockSpec(memory_space=pltpu.VMEM)
    return pl.pallas_call(
        body, name="small_exchange", out_shape=jax.ShapeDtypeStruct(slabs.shape, slabs.dtype),
        in_specs=[vmem], out_specs=vmem, scratch_shapes=EXCHANGE_SEMS,
    )(slabs)


def _adamw(parts, w, m, v, block_rows, name):
    rows, cols = w.shape

    def body(p_ref, w_ref, m_ref, v_ref, g_out, d_out, m_out, v_out):
        g = p_ref[0].astype(F32)
        for s in range(1, N_DEV):
            g = g + p_ref[s].astype(F32)
        m_new = ADAM_B1 * m_ref[...] + (1.0 - ADAM_B1) * g
        v_new = ADAM_B2 * v_ref[...] + (1.0 - ADAM_B2) * (g * g)
        m_hat = m_new / (1.0 - ADAM_B1 ** ADAM_STEP)
        v_hat = v_new / (1.0 - ADAM_B2 ** ADAM_STEP)
        g_out[...] = g
        d_out[...] = -ADAM_LR * (m_hat / (jnp.sqrt(v_hat) + ADAM_EPS) + ADAM_WD * w_ref[...])
        m_out[...] = m_new
        v_out[...] = v_new

    spec = pl.BlockSpec((block_rows, cols), lambda i: (i, 0))
    return pl.pallas_call(
        body, name=name, grid=(pl.cdiv(rows, block_rows),),
        in_specs=[pl.BlockSpec((N_DEV, block_rows, cols), lambda i: (0, i, 0)), spec, spec, spec],
        out_specs=[spec] * 4, out_shape=[jax.ShapeDtypeStruct((rows, cols), F32)] * 4,
        compiler_params=_cp(("parallel",), 48),
    )(parts, w, m, v)


def _pad_rows(flat, rows):
    pad = rows * LANE - flat.shape[-1]
    flat = jnp.pad(flat, [(0, 0)] * (flat.ndim - 1) + [(0, pad)])
    return flat.reshape(flat.shape[:-1] + (rows, LANE))


def _pack_shards(shards):
    return _pad_rows(jnp.concatenate([shards[n].reshape(-1) for n, _, _ in PACKED]), PACK_ROWS)


def _unpack_shards(packed):
    flat, out, off = packed.reshape(-1), {}, 0
    for n, shape, _ in PACKED:
        size = shape[0] * shape[1]
        out[n] = flat[off:off + size].reshape(shape)
        off += size
    return out


def _split8(full, axis):
    r, c = full.shape
    if axis == 0:
        return full.reshape(N_DEV, r // N_DEV, c)
    return full.reshape(r, N_DEV, c // N_DEV).transpose(1, 0, 2)


def _join8(shards, axis):
    _, r, c = shards.shape
    if axis == 0:
        return shards.reshape(N_DEV * r, c)
    return shards.transpose(1, 0, 2).reshape(r, N_DEV * c)


def _pack_small(meta_shard, vals, loss_row):
    rows = jnp.concatenate([vals[n].reshape(-1, LANE) for n, _ in REPLICATED] + [loss_row], axis=0)
    rows = jnp.pad(rows, ((0, SMALL_ROWS - N_META - rows.shape[0]), (0, 0)))
    return jnp.concatenate([meta_shard, jnp.broadcast_to(rows, meta_shard.shape[:-2] + rows.shape)], axis=-2)


def _unpack_small(packed):
    out, off = {"meta_tokens": packed[:N_META]}, N_META
    for n, size in REPLICATED:
        out[n] = packed[off:off + size // LANE].reshape(1, size)
        off += size // LANE
    return out


def kernel(x, meta_tokens, norm_g, w_in, gla_gate_w, gla_gate_b, gla_norm_g, gla_proj, mla_q_norm_g, mla_w_uq, mla_kv_norm_g, mla_w_ukv, mla_proj, w_out, final_norm_g, loss_target, m_meta_tokens, m_norm_g, m_w_in, m_gla_gate_w, m_gla_gate_b, m_gla_norm_g, m_gla_proj, m_mla_q_norm_g, m_mla_w_uq, m_mla_kv_norm_g, m_mla_w_ukv, m_mla_proj, m_w_out, m_final_norm_g, v_meta_tokens, v_norm_g, v_w_in, v_gla_gate_w, v_gla_gate_b, v_gla_norm_g, v_gla_proj, v_mla_q_norm_g, v_mla_w_uq, v_mla_kv_norm_g, v_mla_w_ukv, v_mla_proj, v_w_out, v_final_norm_g):
    given = dict(meta_tokens=meta_tokens, norm_g=norm_g, w_in=w_in, gla_gate_w=gla_gate_w, gla_gate_b=gla_gate_b,
                 gla_norm_g=gla_norm_g, gla_proj=gla_proj, mla_q_norm_g=mla_q_norm_g, mla_w_uq=mla_w_uq,
                 mla_kv_norm_g=mla_kv_norm_g, mla_w_ukv=mla_w_ukv, mla_proj=mla_proj, w_out=w_out,
                 final_norm_g=final_norm_g)
    mom_m = dict(meta_tokens=m_meta_tokens, norm_g=m_norm_g, w_in=m_w_in, gla_gate_w=m_gla_gate_w,
                 gla_gate_b=m_gla_gate_b, gla_norm_g=m_gla_norm_g, gla_proj=m_gla_proj, mla_q_norm_g=m_mla_q_norm_g,
                 mla_w_uq=m_mla_w_uq, mla_kv_norm_g=m_mla_kv_norm_g, mla_w_ukv=m_mla_w_ukv, mla_proj=m_mla_proj,
                 w_out=m_w_out, final_norm_g=m_final_norm_g)
    mom_v = dict(meta_tokens=v_meta_tokens, norm_g=v_norm_g, w_in=v_w_in, gla_gate_w=v_gla_gate_w,
                 gla_gate_b=v_gla_gate_b, gla_norm_g=v_gla_norm_g, gla_proj=v_gla_proj, mla_q_norm_g=v_mla_q_norm_g,
                 mla_w_uq=v_mla_w_uq, mla_kv_norm_g=v_mla_kv_norm_g, mla_w_ukv=v_mla_w_ukv, mla_proj=v_mla_proj,
                 w_out=v_w_out, final_norm_g=v_final_norm_g)
    shapes = {n: a.shape for n, a in given.items()}
    shard2d = {n: s for n, s, _ in PACKED}
    shard2d["w_in"] = (D_MODEL, W_IN_SHARD)
    shard2d["meta_tokens"] = (N_META, LANE)

    def as2d(tree):
        out = {n: tree[n].reshape(shard2d[n]) for n in shard2d}
        out.update({n: tree[n].reshape(1, size) for n, size in REPLICATED})
        return out

    w_loc, m_loc, v_loc = as2d(given), as2d(mom_m), as2d(mom_v)

    flat = jnp.concatenate([w_loc[n].astype(BF16).reshape(-1) for n, _, _ in PACKED])
    w_in_all, packed_all, meta_all = _all_gather(
        [w_loc["w_in"].astype(BF16), _pad_rows(flat, PACK_ROWS), w_loc["meta_tokens"]])
    packed_all = packed_all.reshape(N_DEV, -1)
    full, off = {"w_in": w_in_all, "meta_tokens": _join8(meta_all, 1)}, 0
    for n, shape, axis in PACKED:
        size = shape[0] * shape[1]
        full[n] = _join8(packed_all[:, off:off + size].reshape((N_DEV,) + shape), axis)
        off += size
    for n, _ in REPLICATED:
        full[n] = w_loc[n]

    loss_part, grad_x, w_in_parts, packed_parts, small = _local_step(x, loss_target, full)
    small_all = _small_exchange(_pack_small(_split8(small["meta_tokens"], 1), small,
                                            jnp.broadcast_to(loss_part[:, :1], (1, LANE))))

    w_in_t = [t["w_in"].T for t in (w_loc, m_loc, v_loc)]
    g_w, d_w, m_w, v_w = (o.T for o in _adamw(w_in_parts, *w_in_t, W_IN_BLOCK, "adamw_w_in"))
    g_p, d_p, m_p, v_p = _adamw(packed_parts, _pack_shards(w_loc), _pack_shards(m_loc), _pack_shards(v_loc),
                                PACK_BLOCK, "adamw_packed")
    zero_row = jnp.zeros((1, LANE), F32)
    g_s, d_s, m_s, v_s = _adamw(small_all, *(_pack_small(t["meta_tokens"], t, zero_row) for t in (w_loc, m_loc, v_loc)),
                                SMALL_ROWS, "adamw_small")
    loss = g_s[LOSS_ROW, 0]

    order = ["meta_tokens", "norm_g", "w_in", "gla_gate_w", "gla_gate_b", "gla_norm_g", "gla_proj", "mla_q_norm_g",
             "mla_w_uq", "mla_kv_norm_g", "mla_w_ukv", "mla_proj", "w_out", "final_norm_g"]
    result = [loss, grad_x]
    for w_in_out, packed_sh, packed_sm in ((g_w, g_p, g_s), (d_w, d_p, d_s), (m_w, m_p, m_s), (v_w, v_p, v_s)):
        tree = _unpack_shards(packed_sh)
        tree.update(_unpack_small(packed_sm))
        tree["w_in"] = w_in_out
        result += [tree[n].reshape(shapes[n]) for n in order]
    return tuple(result)
```

```python
import jax
import jax.numpy as jnp
from jax import lax
from jax.experimental import pallas as pl
from jax.experimental.pallas import tpu as pltpu

F32 = jnp.float32
BF16 = jnp.bfloat16

D_MODEL = 1024
N_META = 16
EPS = 1e-6
FRONT = 48
X0 = FRONT + N_META
GLA_HEADS, GLA_DK, GLA_DV, GLA_RANK, GLA_CHUNK = 4, 128, 256, 16, 64
GLA_GATE_NORMALIZER = 16.0
GLA_KW = GLA_HEADS * GLA_DK
GLA_VW = GLA_HEADS * GLA_DV
MLA_HEADS, MLA_NOPE, MLA_ROPE, MLA_DV, MLA_QR, MLA_KVR = 8, 128, 64, 128, 256, 128
MLA_QK = MLA_NOPE + MLA_ROPE
ROPE_BASE = 10000.0
LANE = 128
QKW = 2 * LANE

C_V, C_Z, C_Q, C_K = 0, 1024, 2048, 2560
C_MZ, C_GG, C_GM = 3072, 4096, 5120
C_CKV, C_KR, C_KROT, C_LR = 6144, 6272, 6400, 6528
C_CQ = 6656
N_EXT = 6912
O_Q, O_K, O_V, O_LR, O_Z, O_CQ, O_CKV, O_KR, O_MZ, O_GG, O_GM, N_IN = (
    0, 512, 1024, 2048, 2064, 3088, 3344, 3472, 3536, 4560, 5584, 6608)

ADAM_LR, ADAM_B1, ADAM_B2, ADAM_EPS, ADAM_WD, ADAM_STEP = 0.001, 0.9, 0.999, 1e-08, 0.01, 10

N_DEV = 8
TOK = 192
ATT_BLOCK = 352
EXT_BLOCK = 1152


def _cp(sems=None, vmem_mb=None):
    kw = {}
    if sems is not None:
        kw["dimension_semantics"] = sems
    if vmem_mb is not None:
        kw["vmem_limit_bytes"] = vmem_mb * 1024 * 1024
    return pltpu.CompilerParams(**kw)


def _dot(a, b):
    return jnp.dot(a, b, preferred_element_type=F32)


def _dot_nt(a, b):
    return lax.dot_general(a, b, (((1,), (1,)), ((), ())), preferred_element_type=F32)


def _dot_tn(a, b):
    return lax.dot_general(a, b, (((0,), (0,)), ((), ())), preferred_element_type=F32)


def _sigmoid(x):
    return 1.0 / (1.0 + jnp.exp(-x))


def _bf(x):
    return x.astype(BF16)


def _big_tok(tp):
    return 4 * TOK if tp % (4 * TOK) == 0 else TOK


def _attn_block(lp):
    return ATT_BLOCK if lp % ATT_BLOCK == 0 else TOK


def _proj_in(hp, norm_g, w_ext, packed):
    tp = hp.shape[0]
    tm, tn = _big_tok(tp), EXT_BLOCK
    ni, nj = tp // tm, N_EXT // tn

    def body(h_ref, g_ref, w_ref, p_ref, u_ref, o_ref, pall_ref, u_scr, send_sems, recv_sems, local_sem):
        i, j = pl.program_id(0), pl.program_id(1)

        @pl.when(jnp.logical_and(i == 0, j == 0))
        def _():
            _exchange(p_ref, pall_ref, send_sems, recv_sems, local_sem, True, same=True)

        @pl.when(j == 0)
        def _():
            x = h_ref[...]
            r = lax.rsqrt(jnp.mean(x * x, axis=-1, keepdims=True) + EPS)
            u = _bf(x * r * g_ref[...])
            u_scr[...] = u
            u_ref[...] = u

        o_ref[...] = _bf(_dot(u_scr[...], w_ref[...]))

        @pl.when(jnp.logical_and(i == ni - 1, j == nj - 1))
        def _():
            _exchange(p_ref, pall_ref, send_sems, recv_sems, local_sem, False, same=True)

    anyspec = pl.BlockSpec(memory_space=pl.ANY)
    return pl.pallas_call(
        body, name="proj_in", grid=(ni, nj),
        in_specs=[pl.BlockSpec((tm, D_MODEL), lambda i, j: (i, 0)),
                  pl.BlockSpec((1, D_MODEL), lambda i, j: (0, 0)),
                  pl.BlockSpec((D_MODEL, tn), lambda i, j: (0, j)), anyspec],
        out_specs=[pl.BlockSpec((tm, D_MODEL), lambda i, j: (i, 0)),
                   pl.BlockSpec((tm, tn), lambda i, j: (i, j)), anyspec],
        out_shape=[jax.ShapeDtypeStruct((tp, D_MODEL), BF16), jax.ShapeDtypeStruct((tp, N_EXT), BF16),
                   jax.ShapeDtypeStruct((N_DEV,) + packed.shape, packed.dtype)],
        scratch_shapes=[pltpu.VMEM((tm, D_MODEL), BF16)] + EXCHANGE_SEMS,
        compiler_params=_cp(("arbitrary", "arbitrary"), 48),
    )(hp, norm_g, w_ext, packed)


GLA_GROUP = 3
GLA_ROWS = GLA_GROUP * GLA_CHUNK


def _tri_dot(tri, x):
    hi = _bf(x)
    rest = x - hi.astype(F32)
    mid = _bf(rest)
    return _dot(tri, hi) + _dot(tri, mid) + _dot(tri, _bf(rest - mid.astype(F32)))


def _gla_gates(q_ref, k_ref, lr_ref, gw_ref, gb_ref, rows, not_first):
    z = _dot(lr_ref[rows, :], gw_ref[...]) + gb_ref[...]
    logsig = jnp.minimum(z, 0.0) - jnp.log(1.0 + jnp.exp(-jnp.abs(z)))
    row = lax.broadcasted_iota(jnp.int32, (GLA_CHUNK, GLA_KW), 0)
    live = jnp.logical_or(not_first, row >= FRONT)
    g = jnp.where(live, logsig * (1.0 / GLA_GATE_NORMALIZER), 0.0)
    ri = lax.broadcasted_iota(jnp.int32, (GLA_CHUNK, GLA_CHUNK), 0)
    ci = lax.broadcasted_iota(jnp.int32, (GLA_CHUNK, GLA_CHUNK), 1)
    tril = ci <= ri
    b = _tri_dot(_bf(tril.astype(F32)), g)
    bl = jnp.sum(jnp.where(row == GLA_CHUNK - 1, b, 0.0), axis=0, keepdims=True)
    eb, enb, elb, ebl = jnp.exp(b), jnp.exp(-b), jnp.exp(bl - b), jnp.exp(bl)
    q = q_ref[rows, :].astype(F32) * (GLA_DK ** -0.5)
    k = k_ref[rows, :].astype(F32)
    qe, ke, kl = q * eb, k * enb, k * elb
    return dict(z=z, live=live, tril=tril, row=row, eb=eb, enb=enb, elb=elb, ebl=ebl, qe=qe, ke=ke, kl=kl,
                qe_b=_bf(qe), ke_b=_bf(ke), kl_b=_bf(kl))


def _gla_in_specs(n_groups, rev):
    def rb(b, n):
        return b * n_groups + ((n_groups - 1 - n) if rev else n)

    return rb, [pl.BlockSpec((GLA_ROWS, GLA_KW), lambda b, n: (rb(b, n), C_Q // GLA_KW)),
                pl.BlockSpec((GLA_ROWS, GLA_KW), lambda b, n: (rb(b, n), C_K // GLA_KW)),
                pl.BlockSpec((GLA_ROWS, GLA_VW), lambda b, n: (rb(b, n), C_V // GLA_VW)),
                pl.BlockSpec((GLA_ROWS, GLA_VW), lambda b, n: (rb(b, n), C_Z // GLA_VW)),
                pl.BlockSpec((GLA_ROWS, LANE), lambda b, n: (rb(b, n), C_LR // LANE)),
                pl.BlockSpec((LANE, GLA_KW), lambda b, n: (0, 0)),
                pl.BlockSpec((1, GLA_KW), lambda b, n: (0, 0)),
                pl.BlockSpec((1, GLA_DV), lambda b, n: (0, 0))]


def _gla_fwd(proj, gw_pad, gate_b, gla_norm_g, bsz, lp):
    n_chunks = lp // GLA_CHUNK
    n_groups = n_chunks // GLA_GROUP
    tp = bsz * lp

    def body(q_ref, k_ref, v_ref, z_ref, lr_ref, gw_ref, gb_ref, gn_ref, oraw_ref, ya_ref, sall_ref, st_scr):
        grp = pl.program_id(1)

        @pl.when(grp == 0)
        def _():
            st_scr[...] = jnp.zeros_like(st_scr)

        chunks = [slice(j * GLA_CHUNK, (j + 1) * GLA_CHUNK) for j in range(GLA_GROUP)]
        cs = [_gla_gates(q_ref, k_ref, lr_ref, gw_ref, gb_ref, rows, True if j else grp > 0)
              for j, rows in enumerate(chunks)]
        gn = gn_ref[...]
        for h in range(GLA_HEADS):
            ks, vs = slice(h * GLA_DK, (h + 1) * GLA_DK), slice(h * GLA_DV, (h + 1) * GLA_DV)
            st = st_scr[h]
            for j, (rows, c) in enumerate(zip(chunks, cs)):
                sall_ref[0, j, h] = st
                v = v_ref[rows, vs]
                a = jnp.where(c["tril"], _dot_nt(c["qe_b"][:, ks], c["ke_b"][:, ks]), 0.0)
                o = _dot(_bf(a), v) + _dot_nt(c["qe_b"][:, ks], _bf(st))
                st = st * c["ebl"][:, ks] + _dot_tn(v, c["kl_b"][:, ks])
                oraw_ref[rows, vs] = o
                r = lax.rsqrt(jnp.mean(o * o, axis=-1, keepdims=True) + EPS)
                zg = z_ref[rows, vs].astype(F32)
                ya_ref[rows, vs] = _bf((o * r * gn) * (zg * _sigmoid(zg)))
            st_scr[h] = st

    rb, in_specs = _gla_in_specs(n_groups, False)
    return pl.pallas_call(
        body, name="gla_fwd", grid=(bsz, n_groups), in_specs=in_specs,
        out_specs=[pl.BlockSpec((GLA_ROWS, GLA_VW), lambda b, n: (rb(b, n), 0)),
                   pl.BlockSpec((GLA_ROWS, GLA_VW), lambda b, n: (rb(b, n), 0)),
                   pl.BlockSpec((1, GLA_GROUP, GLA_HEADS, GLA_DV, GLA_DK), lambda b, n: (b, n, 0, 0, 0))],
        out_shape=[jax.ShapeDtypeStruct((tp, GLA_VW), F32), jax.ShapeDtypeStruct((tp, GLA_VW), BF16),
                   jax.ShapeDtypeStruct((bsz, n_chunks, GLA_HEADS, GLA_DV, GLA_DK), F32)],
        scratch_shapes=[pltpu.VMEM((GLA_HEADS, GLA_DV, GLA_DK), F32)],
        compiler_params=_cp(("parallel", "arbitrary")),
    )(proj, proj, proj, proj, proj, gw_pad, gate_b, gla_norm_g)


def _gla_bwd(proj, gw_pad, gate_b, gla_norm_g, o_raw, s_all, d_ya, dproj, bsz, lp):
    n_chunks = lp // GLA_CHUNK
    n_groups = n_chunks // GLA_GROUP
    tp = bsz * lp

    def body(q_ref, k_ref, v_ref, z_ref, lr_ref, gw_ref, gb_ref, gn_ref, o_ref, s_ref, dya_ref, _,
             dp_ref, dz_ref, dgn_ref, dst_scr):
        dv_ref, dzg_ref = dp_ref.at[:, C_V:C_V + GLA_VW], dp_ref.at[:, C_Z:C_Z + GLA_VW]

        @pl.when(jnp.logical_and(pl.program_id(0) == 0, pl.program_id(1) == 0))
        def _():
            dgn_ref[...] = jnp.zeros_like(dgn_ref)

        @pl.when(pl.program_id(1) == 0)
        def _():
            dst_scr[...] = jnp.zeros_like(dst_scr)

        grp = n_groups - 1 - pl.program_id(1)
        chunks = [slice(j * GLA_CHUNK, (j + 1) * GLA_CHUNK) for j in range(GLA_GROUP)]
        cs = [_gla_gates(q_ref, k_ref, lr_ref, gw_ref, gb_ref, rows, True if j else grp > 0)
              for j, rows in enumerate(chunks)]
        gn = gn_ref[...]
        dgn = jnp.zeros((1, GLA_DV), F32)
        dqe_h, dke_h, dkl_h, dbl_h = ([[None] * GLA_HEADS for _ in chunks] for _ in range(4))
        for h in range(GLA_HEADS):
            ks, vs = slice(h * GLA_DK, (h + 1) * GLA_DK), slice(h * GLA_DV, (h + 1) * GLA_DV)
            dst = dst_scr[h]
            for j in reversed(range(GLA_GROUP)):
                rows, c = chunks[j], cs[j]
                v = v_ref[rows, vs]
                st = s_ref[0, j, h]
                o = o_ref[rows, vs]
                r = lax.rsqrt(jnp.mean(o * o, axis=-1, keepdims=True) + EPS)
                xh = o * r
                zg = z_ref[rows, vs].astype(F32)
                sg = _sigmoid(zg)
                dy = dya_ref[rows, vs].astype(F32)
                dzg_ref[rows, vs] = _bf(dy * (xh * gn) * (sg * (1.0 + zg * (1.0 - sg))))
                t = dy * (zg * sg)
                dgn += jnp.sum(t * xh, axis=0, keepdims=True)
                dxh = t * gn
                do_b = _bf(r * (dxh - xh * jnp.mean(dxh * xh, axis=-1, keepdims=True)))
                qe_b, ke_b, kl_b, dst_b = c["qe_b"][:, ks], c["ke_b"][:, ks], c["kl_b"][:, ks], _bf(dst)
                a = jnp.where(c["tril"], _dot_nt(qe_b, ke_b), 0.0)
                da_b = _bf(jnp.where(c["tril"], _dot_nt(do_b, v), 0.0))
                dqe_h[j][h] = _dot(da_b, ke_b) + _dot(do_b, _bf(st))
                dke_h[j][h] = _dot_tn(da_b, qe_b)
                dkl = _dot(v, dst_b)
                dkl_h[j][h] = dkl
                dv_ref[rows, vs] = _bf(_dot_tn(_bf(a), do_b) + _dot_nt(kl_b, dst_b))
                ddecay = jnp.sum(dst * st, axis=0, keepdims=True)
                dbl_h[j][h] = jnp.sum(dkl * c["kl"][:, ks], axis=0, keepdims=True) + ddecay * c["ebl"][:, ks]
                dst = dst * c["ebl"][:, ks] + _dot_tn(do_b, qe_b)
            dst_scr[h] = dst
        dgn_ref[...] += dgn
        ri = lax.broadcasted_iota(jnp.int32, (GLA_CHUNK, GLA_CHUNK), 0)
        ci = lax.broadcasted_iota(jnp.int32, (GLA_CHUNK, GLA_CHUNK), 1)
        triu = _bf((ci >= ri).astype(F32))
        for j, (rows, c) in enumerate(zip(chunks, cs)):
            dqe, dke, dkl, dbl = (jnp.concatenate(p[j], axis=1) for p in (dqe_h, dke_h, dkl_h, dbl_h))
            db = dqe * c["qe"] - dke * c["ke"] - dkl * c["kl"] + jnp.where(c["row"] == GLA_CHUNK - 1, dbl, 0.0)
            dg = _tri_dot(triu, db)
            dg = jnp.where(c["live"], dg, 0.0)
            dz_ref[rows, :] = dg * (1.0 / GLA_GATE_NORMALIZER) * _sigmoid(-c["z"])
            dp_ref[rows, C_Q:C_Q + GLA_KW] = _bf(dqe * c["eb"] * (GLA_DK ** -0.5))
            dp_ref[rows, C_K:C_K + GLA_KW] = _bf(dke * c["enb"] + dkl * c["elb"])

    rb, in_specs = _gla_in_specs(n_groups, True)
    wide = pl.BlockSpec((GLA_ROWS, GLA_VW), lambda b, n: (rb(b, n), 0))
    group = C_MZ
    return pl.pallas_call(
        body, name="gla_bwd", grid=(bsz, n_groups),
        in_specs=in_specs + [wide, pl.BlockSpec((1, GLA_GROUP, GLA_HEADS, GLA_DV, GLA_DK),
                                                lambda b, n: (b, n_groups - 1 - n, 0, 0, 0)), wide,
                             pl.BlockSpec(memory_space=pl.ANY)],
        out_specs=[pl.BlockSpec((GLA_ROWS, group), lambda b, n: (rb(b, n), 0)),
                   pl.BlockSpec((GLA_ROWS, GLA_KW), lambda b, n: (rb(b, n), 0)),
                   pl.BlockSpec((1, GLA_DV), lambda b, n: (0, 0))],
        out_shape=[jax.ShapeDtypeStruct((tp, N_EXT), BF16), jax.ShapeDtypeStruct((tp, GLA_KW), F32),
                   jax.ShapeDtypeStruct((1, GLA_DV), F32)],
        input_output_aliases={11: 0},
        scratch_shapes=[pltpu.VMEM((GLA_HEADS, GLA_DV, GLA_DK), F32)],
        compiler_params=_cp(("arbitrary", "arbitrary")),
    )(proj, proj, proj, proj, proj, gw_pad, gate_b, gla_norm_g, o_raw, s_all, d_ya, dproj)


def _gate_bwd(dz, proj, gw_pad):
    tp = dz.shape[0]
    tm = _big_tok(tp)

    def body(dz_ref, lr_ref, gw_ref, dlr_ref, dgw_ref, dgb_ref):
        @pl.when(pl.program_id(0) == 0)
        def _():
            dgw_ref[...] = jnp.zeros_like(dgw_ref)
            dgb_ref[...] = jnp.zeros_like(dgb_ref)

        dz = dz_ref[...]
        dz_b = _bf(dz)
        dlr_ref[...] = _bf(_dot_nt(dz_b, gw_ref[...]))
        dgw_ref[...] += _dot_tn(lr_ref[...], dz_b)
        dgb_ref[...] += jnp.sum(dz, axis=0, keepdims=True)

    return pl.pallas_call(
        body, name="gate_bwd", grid=(tp // tm,),
        in_specs=[pl.BlockSpec((tm, GLA_KW), lambda i: (i, 0)),
                  pl.BlockSpec((tm, LANE), lambda i: (i, C_LR // LANE)),
                  pl.BlockSpec((LANE, GLA_KW), lambda i: (0, 0))],
        out_specs=[pl.BlockSpec((tm, LANE), lambda i: (i, 0)),
                   pl.BlockSpec((LANE, GLA_KW), lambda i: (0, 0)),
                   pl.BlockSpec((1, GLA_KW), lambda i: (0, 0))],
        out_shape=[jax.ShapeDtypeStruct((tp, LANE), BF16), jax.ShapeDtypeStruct((LANE, GLA_KW), F32),
                   jax.ShapeDtypeStruct((1, GLA_KW), F32)],
        compiler_params=_cp(("arbitrary",)),
    )(dz, proj, gw_pad)


def _rms_fwd(x):
    r = lax.rsqrt(jnp.mean(x * x, axis=-1, keepdims=True) + EPS)
    return x * r, r


def _rms_bwd(dy, xh, r, g):
    dxh = dy * g
    dx = r * (dxh - xh * jnp.mean(dxh * xh, axis=-1, keepdims=True))
    return dx, jnp.sum(dy * xh, axis=0, keepdims=True)


def _q_up(proj, q_norm_g, wn, wr, wt, cos_t, sin_t, bsz, lp):
    tp = bsz * lp
    tok = _attn_block(lp)
    nb = lp // tok

    def body(cq_ref, g_ref, wn_ref, wr_ref, wt_ref, cos_ref, sin_ref, q_ref):
        xh, _ = _rms_fwd(cq_ref[...].astype(F32))
        cqn = _bf(xh * g_ref[...])
        nope = _dot(cqn, wn_ref[...])
        rope = _dot(cqn, wr_ref[...])
        rot = _dot(cqn, wt_ref[...])
        cos, sin = cos_ref[...], sin_ref[...]
        one = (lax.broadcasted_iota(jnp.int32, (tok, LANE), 1) == BIAS_LANE).astype(F32)
        for h in range(MLA_HEADS):
            sl = slice(h * LANE, (h + 1) * LANE)
            q_ref[:, h * QKW:h * QKW + LANE] = _bf(nope[:, sl])
            q_ref[:, h * QKW + LANE:(h + 1) * QKW] = _bf(rope[:, sl] * cos + rot[:, sl] * sin + one)

    wspec = pl.BlockSpec((MLA_QR, MLA_HEADS * LANE), lambda b, i: (0, 0))
    tspec = pl.BlockSpec((tok, LANE), lambda b, i: (i, 0))
    return pl.pallas_call(
        body, name="mla_q_up", grid=(bsz, nb),
        in_specs=[pl.BlockSpec((tok, MLA_QR), lambda b, i: (b * nb + i, C_CQ // MLA_QR)),
                  pl.BlockSpec((1, MLA_QR), lambda b, i: (0, 0)), wspec, wspec, wspec, tspec, tspec],
        out_specs=pl.BlockSpec((tok, MLA_HEADS * QKW), lambda b, i: (b * nb + i, 0)),
        out_shape=jax.ShapeDtypeStruct((tp, MLA_HEADS * QKW), BF16),
        compiler_params=_cp(("parallel", "parallel")),
    )(proj, q_norm_g, wn, wr, wt, cos_t, sin_t)


def _kv_up(proj, kv_norm_g, wk, wv, cos_t, sin_t, bsz, lp):
    tp = bsz * lp
    tok = _attn_block(lp)
    nb = lp // tok

    def body(ckv_ref, kr_ref, krot_ref, g_ref, wk_ref, wv_ref, cos_ref, sin_ref, k_ref, v_ref):
        xh, _ = _rms_fwd(ckv_ref[...].astype(F32))
        cn = _bf(xh * g_ref[...])
        kn = _dot(cn, wk_ref[...])
        v_ref[...] = _bf(_dot(cn, wv_ref[...]))
        pos = pl.program_id(1) * tok + lax.broadcasted_iota(jnp.int32, (tok, LANE), 0)
        lane = lax.broadcasted_iota(jnp.int32, (tok, LANE), 1)
        bias = jnp.where(jnp.logical_and(lane == BIAS_LANE, pos < FRONT), KEY_BIAS, 0.0)
        kr = _bf(kr_ref[...].astype(F32) * cos_ref[...] + krot_ref[...].astype(F32) * sin_ref[...] + bias)
        for h in range(MLA_HEADS):
            k_ref[:, h * QKW:h * QKW + LANE] = _bf(kn[:, h * LANE:(h + 1) * LANE])
            k_ref[:, h * QKW + LANE:(h + 1) * QKW] = kr

    wspec = pl.BlockSpec((MLA_KVR, MLA_HEADS * LANE), lambda b, i: (0, 0))
    tspec = pl.BlockSpec((tok, LANE), lambda b, i: (i, 0))
    return pl.pallas_call(
        body, name="mla_kv_up", grid=(bsz, nb),
        in_specs=[pl.BlockSpec((tok, LANE), lambda b, i: (b * nb + i, C_CKV // LANE)),
                  pl.BlockSpec((tok, LANE), lambda b, i: (b * nb + i, C_KR // LANE)),
                  pl.BlockSpec((tok, LANE), lambda b, i: (b * nb + i, C_KROT // LANE)),
                  pl.BlockSpec((1, MLA_KVR), lambda b, i: (0, 0)), wspec, wspec, tspec, tspec],
        out_specs=[pl.BlockSpec((tok, MLA_HEADS * QKW), lambda b, i: (b * nb + i, 0)),
                   pl.BlockSpec((tok, MLA_HEADS * LANE), lambda b, i: (b * nb + i, 0))],
        out_shape=[jax.ShapeDtypeStruct((tp, MLA_HEADS * QKW), BF16),
                   jax.ShapeDtypeStruct((tp, MLA_HEADS * LANE), BF16)],
        compiler_params=_cp(("parallel", "parallel")),
    )(proj, proj, proj, kv_norm_g, wk, wv, cos_t, sin_t)


ATT_SCALE = MLA_QK ** -0.5


KEY_BIAS = -1e30
BIAS_LANE = MLA_ROPE
NEG = 2 * KEY_BIAS
LOG2E = 1.4426950408889634
EXP2_SCALE = ATT_SCALE * LOG2E


def _causal_fill(s, r0, fill):
    tq, kmax = s.shape
    a = r0 // LANE * LANE
    mask = (a + lax.broadcasted_iota(jnp.int32, (tq, kmax - a), 1)
            <= r0 + lax.broadcasted_iota(jnp.int32, (tq, kmax - a), 0))
    right = jnp.where(mask, s[:, a:], fill)
    return jnp.concatenate([s[:, :a], right], axis=1) if a else right


def _attn_fwd(qf, kf, vf, proj, bsz, lp):
    tp = bsz * lp
    tq = _attn_block(lp)

    def body(q_ref, k_ref, v_ref, mz_ref, ob_ref, yb_ref, lse_ref):
        for r0 in range(0, lp, tq):
            rows, kmax = slice(r0, r0 + tq), r0 + tq
            s = _causal_fill(_dot_nt(q_ref[rows, :], k_ref[0:kmax, :]), r0, NEG)
            m = jnp.max(s, axis=-1, keepdims=True)
            p = jnp.exp2((s - m) * EXP2_SCALE)
            l = jnp.sum(p, axis=-1, keepdims=True)
            o = _dot(_bf(p), v_ref[0:kmax, :]) / l
            ob_ref[rows, :] = _bf(o)
            mz = mz_ref[rows, :].astype(F32)
            yb_ref[rows, :] = _bf(o * (mz * _sigmoid(mz)))
            lse_ref[0, 0, rows, :] = jnp.broadcast_to(m * EXP2_SCALE + jnp.log2(l), (tq, LANE))

    head = lambda off: pl.BlockSpec((lp, MLA_DV), lambda b, h: (b, off + h))
    return pl.pallas_call(
        body, name="mla_attn_fwd", grid=(bsz, MLA_HEADS),
        in_specs=[pl.BlockSpec((lp, QKW), lambda b, h: (b, h)), pl.BlockSpec((lp, QKW), lambda b, h: (b, h)),
                  head(0), head(C_MZ // MLA_DV)],
        out_specs=[head(0), head(0), pl.BlockSpec((1, 1, lp, LANE), lambda b, h: (b, h, 0, 0))],
        out_shape=[jax.ShapeDtypeStruct((tp, MLA_HEADS * MLA_DV), BF16),
                   jax.ShapeDtypeStruct((tp, MLA_HEADS * MLA_DV), BF16),
                   jax.ShapeDtypeStruct((bsz, MLA_HEADS, lp, LANE), F32)],
        compiler_params=_cp(("parallel", "parallel"), 56),
    )(qf, kf, vf, proj)


def _attn_bwd(qf, kf, vf, d_o, lse, delta, bsz, lp):
    tp = bsz * lp
    tq = _attn_block(lp)

    def body(q_ref, k_ref, v_ref, do_ref, lse_ref, dl_ref, dq_ref, dk_ref, dv_ref, dk_acc, dv_acc):
        dk_acc[...] = jnp.zeros_like(dk_acc)
        dv_acc[...] = jnp.zeros_like(dv_acc)
        for r0 in range(0, lp, tq):
            rows, kmax = slice(r0, r0 + tq), r0 + tq
            q, do = q_ref[rows, :], do_ref[rows, :]
            k, v = k_ref[0:kmax, :], v_ref[0:kmax, :]
            p = jnp.exp2(_dot_nt(q, k) * EXP2_SCALE - lse_ref[0, 0, rows, :][:, :1])
            p = _causal_fill(p, r0, 0.0)
            ds = _bf(p * (_dot_nt(do, v) - dl_ref[0, 0, rows, :][:, :1]))
            dq_ref[rows, :] = _bf(_dot(ds, k) * ATT_SCALE)
            dk_acc[0:kmax, :] += _dot_tn(ds, q)
            dv_acc[0:kmax, :] += _dot_tn(_bf(p), do)
        dk_ref[...] = _bf(dk_acc[...] * ATT_SCALE)
        dv_ref[...] = _bf(dv_acc[...])

    wide = pl.BlockSpec((lp, QKW), lambda b, h: (b, h))
    narrow = pl.BlockSpec((lp, MLA_DV), lambda b, h: (b, h))
    stat = pl.BlockSpec((1, 1, lp, LANE), lambda b, h: (b, h, 0, 0))
    return pl.pallas_call(
        body, name="mla_attn_bwd", grid=(bsz, MLA_HEADS),
        in_specs=[wide, wide, narrow, narrow, stat, stat], out_specs=[wide, wide, narrow],
        out_shape=[jax.ShapeDtypeStruct((tp, MLA_HEADS * QKW), BF16), jax.ShapeDtypeStruct((tp, MLA_HEADS * QKW), BF16),
                   jax.ShapeDtypeStruct((tp, MLA_HEADS * MLA_DV), BF16)],
        scratch_shapes=[pltpu.VMEM((lp, QKW), F32), pltpu.VMEM((lp, MLA_DV), F32)],
        compiler_params=_cp(("parallel", "parallel"), 56),
    )(qf, kf, vf, d_o, lse, delta)


def _q_up_bwd(dqf, proj, q_norm_g, wn, wr, wt, cos_t, sin_t, dproj, bsz, lp):
    tp = bsz * lp
    tok = _attn_block(lp)
    nb = lp // tok
    hw = MLA_HEADS * LANE

    def body(dq_ref, cq_ref, g_ref, wn_ref, wr_ref, wt_ref, cos_ref, sin_ref, _,
             dcq_ref, dwn_ref, dwr_ref, dwt_ref, dg_ref):
        @pl.when(jnp.logical_and(pl.program_id(0) == 0, pl.program_id(1) == 0))
        def _():
            for r in (dwn_ref, dwr_ref, dwt_ref, dg_ref):
                r[...] = jnp.zeros_like(r)

        g = g_ref[...]
        xh, r = _rms_fwd(cq_ref[...].astype(F32))
        cqn = _bf(xh * g)
        cos, sin = cos_ref[...], sin_ref[...]
        dcqn = jnp.zeros((tok, MLA_QR), F32)
        for h in range(MLA_HEADS):
            sl = slice(h * LANE, (h + 1) * LANE)
            dn = dq_ref[:, h * QKW:h * QKW + LANE]
            dr = dq_ref[:, h * QKW + LANE:(h + 1) * QKW].astype(F32)
            dr_c, dr_s = _bf(dr * cos), _bf(dr * sin)
            dcqn += _dot_nt(dn, wn_ref[:, sl]) + _dot_nt(dr_c, wr_ref[:, sl]) + _dot_nt(dr_s, wt_ref[:, sl])
            dwn_ref[:, sl] += _dot_tn(cqn, dn)
            dwr_ref[:, sl] += _dot_tn(cqn, dr_c)
            dwt_ref[:, sl] += _dot_tn(cqn, dr_s)
        dx, dg = _rms_bwd(dcqn, xh, r, g)
        dcq_ref[...] = _bf(dx)
        dg_ref[...] += dg

    aspec = pl.BlockSpec((MLA_QR, hw), lambda b, i: (0, 0))
    tspec = pl.BlockSpec((tok, LANE), lambda b, i: (i, 0))
    return pl.pallas_call(
        body, name="mla_q_up_bwd", grid=(bsz, nb),
        in_specs=[pl.BlockSpec((tok, MLA_HEADS * QKW), lambda b, i: (b * nb + i, 0)),
                  pl.BlockSpec((tok, MLA_QR), lambda b, i: (b * nb + i, C_CQ // MLA_QR)),
                  pl.BlockSpec((1, MLA_QR), lambda b, i: (0, 0)), aspec, aspec, aspec, tspec, tspec,
                  pl.BlockSpec(memory_space=pl.ANY)],
        out_specs=[pl.BlockSpec((tok, MLA_QR), lambda b, i: (b * nb + i, C_CQ // MLA_QR)), aspec, aspec, aspec,
                   pl.BlockSpec((1, MLA_QR), lambda b, i: (0, 0))],
        out_shape=[jax.ShapeDtypeStruct((tp, N_EXT), BF16)] + [jax.ShapeDtypeStruct((MLA_QR, hw), F32)] * 3
        + [jax.ShapeDtypeStruct((1, MLA_QR), F32)],
        input_output_aliases={8: 0},
        compiler_params=_cp(("arbitrary", "arbitrary")),
    )(dqf, proj, q_norm_g, wn, wr, wt, cos_t, sin_t, dproj)


def _kv_up_bwd(dkf, dvf, proj, kv_norm_g, wk, wv, cos_t, sin_t, d_lr, dproj, bsz, lp):
    tp = bsz * lp
    tok = _attn_block(lp)
    nb = lp // tok
    hw = MLA_HEADS * LANE

    def body(dk_ref, dv_ref, ckv_ref, g_ref, wk_ref, wv_ref, cos_ref, sin_ref, dlr_ref, _,
             dp_ref, dwk_ref, dwv_ref, dg_ref):
        dckv_ref, dkr_ref, dkrot_ref = (dp_ref.at[:, j * LANE:(j + 1) * LANE] for j in range(3))
        dp_ref[:, 3 * LANE:] = dlr_ref[...]
        @pl.when(jnp.logical_and(pl.program_id(0) == 0, pl.program_id(1) == 0))
        def _():
            for r in (dwk_ref, dwv_ref, dg_ref):
                r[...] = jnp.zeros_like(r)

        g = g_ref[...]
        xh, r = _rms_fwd(ckv_ref[...].astype(F32))
        cn = _bf(xh * g)
        dv = dv_ref[...]
        dcn = _dot_nt(dv, wv_ref[...])
        dwv_ref[...] += _dot_tn(cn, dv)
        drope = jnp.zeros((tok, LANE), F32)
        for h in range(MLA_HEADS):
            sl = slice(h * LANE, (h + 1) * LANE)
            dn = dk_ref[:, h * QKW:h * QKW + LANE]
            drope += dk_ref[:, h * QKW + LANE:(h + 1) * QKW].astype(F32)
            dcn += _dot_nt(dn, wk_ref[:, sl])
            dwk_ref[:, sl] += _dot_tn(cn, dn)
        dkr_ref[...] = _bf(drope * cos_ref[...])
        dkrot_ref[...] = _bf(drope * sin_ref[...])
        dx, dg = _rms_bwd(dcn, xh, r, g)
        dckv_ref[...] = _bf(dx)
        dg_ref[...] += dg

    aspec = pl.BlockSpec((MLA_KVR, hw), lambda b, i: (0, 0))
    tspec = pl.BlockSpec((tok, LANE), lambda b, i: (i, 0))
    ospec = pl.BlockSpec((tok, LANE), lambda b, i: (b * nb + i, 0))
    return pl.pallas_call(
        body, name="mla_kv_up_bwd", grid=(bsz, nb),
        in_specs=[pl.BlockSpec((tok, MLA_HEADS * QKW), lambda b, i: (b * nb + i, 0)),
                  pl.BlockSpec((tok, hw), lambda b, i: (b * nb + i, 0)),
                  pl.BlockSpec((tok, LANE), lambda b, i: (b * nb + i, C_CKV // LANE)),
                  pl.BlockSpec((1, MLA_KVR), lambda b, i: (0, 0)), aspec, aspec, tspec, tspec, ospec,
                  pl.BlockSpec(memory_space=pl.ANY)],
        out_specs=[pl.BlockSpec((tok, 4 * LANE), lambda b, i: (b * nb + i, C_CKV // (4 * LANE))), aspec, aspec,
                   pl.BlockSpec((1, MLA_KVR), lambda b, i: (0, 0))],
        out_shape=[jax.ShapeDtypeStruct((tp, N_EXT), BF16)] + [jax.ShapeDtypeStruct((MLA_KVR, hw), F32)] * 2
        + [jax.ShapeDtypeStruct((1, MLA_KVR), F32)],
        input_output_aliases={9: 0},
        compiler_params=_cp(("arbitrary", "arbitrary")),
    )(dkf, dvf, proj, kv_norm_g, wk, wv, cos_t, sin_t, d_lr, dproj)


def _mid_fwd(ya_in, yb_in, proj, hp, target, w_gp, w_mp, w_o, final_g, bsz, lp):
    tp = bsz * lp
    tm = _attn_block(lp)
    nb = lp // tm
    last = pl.cdiv(lp - X0, tm) - 1

    def body(ya_ref, yb_ref, gg_ref, gm_ref, h_ref, ta_ref, tb_ref, wgp_ref, wmp_ref, wo_ref, fg_ref,
             ya_out, yb_out, dh_ref, loss_ref, dfg_ref):
        @pl.when(jnp.logical_and(pl.program_id(0) == 0, pl.program_id(1) == 0))
        def _():
            loss_ref[...] = jnp.zeros_like(loss_ref)
            dfg_ref[...] = jnp.zeros_like(dfg_ref)

        y_a = _dot(ya_ref[...], wgp_ref[...])
        y_b = _dot(yb_ref[...], wmp_ref[...])
        ya_out[...] = _bf(y_a)
        yb_out[...] = _bf(y_b)
        merged = _sigmoid(gg_ref[...].astype(F32)) * y_a + _sigmoid(gm_ref[...].astype(F32)) * y_b
        h2 = h_ref[...] + _dot(_bf(merged), wo_ref[...])
        fg = fg_ref[...]
        xh, r = _rms_fwd(h2)
        pos = pl.program_id(1) * tm + lax.broadcasted_iota(jnp.int32, (tm, 1), 0)
        t = jnp.concatenate([ta_ref[0, tm - X0:, :], tb_ref[0, :tm - X0, :]], axis=0)
        err = jnp.where(pos >= X0, xh * fg - t, 0.0)
        loss_ref[...] += 0.5 * jnp.sum(jnp.mean(err * err, axis=-1, keepdims=True), axis=0, keepdims=True)
        dy = err * (1.0 / D_MODEL)
        dx, dfg = _rms_bwd(dy, xh, r, fg)
        dh_ref[...] = dx
        dfg_ref[...] += dfg

    tok = lambda c: pl.BlockSpec((tm, D_MODEL), lambda b, i: (b * nb + i, c))
    wspec = pl.BlockSpec((D_MODEL, D_MODEL), lambda b, i: (0, 0))
    return pl.pallas_call(
        body, name="mid_fwd", grid=(bsz, nb),
        in_specs=[tok(0), tok(0), tok(C_GG // D_MODEL), tok(C_GM // D_MODEL), tok(0),
                  pl.BlockSpec((1, tm, D_MODEL), lambda b, i: (b, jnp.maximum(i - 1, 0), 0)),
                  pl.BlockSpec((1, tm, D_MODEL), lambda b, i: (b, jnp.minimum(i, last), 0)),
                  wspec, wspec, wspec, pl.BlockSpec((1, D_MODEL), lambda b, i: (0, 0))],
        out_specs=[tok(0), tok(0), tok(0), pl.BlockSpec((1, LANE), lambda b, i: (0, 0)),
                   pl.BlockSpec((1, D_MODEL), lambda b, i: (0, 0))],
        out_shape=[jax.ShapeDtypeStruct((tp, D_MODEL), BF16), jax.ShapeDtypeStruct((tp, D_MODEL), BF16),
                   jax.ShapeDtypeStruct((tp, D_MODEL), F32), jax.ShapeDtypeStruct((1, LANE), F32),
                   jax.ShapeDtypeStruct((1, D_MODEL), F32)],
        compiler_params=_cp(("arbitrary", "arbitrary"), 48),
    )(ya_in, yb_in, proj, proj, hp, target, target, w_gp, w_mp, w_o, final_g)


def _mid_bwd(dh2, y_a, y_b, proj, ya_in, yb_in, o_b, w_o, w_gp, w_mp, bsz, lp):
    tp = bsz * lp
    tm = _attn_block(lp)
    nb = lp // tm
    nsteps = tp // tm
    group = 3 * D_MODEL

    def body(dh_ref, ya_ref, yb_ref, mz_ref, gg_ref, gm_ref, yai_ref, ybi_ref, ob_ref, wo_ref, wgp_ref, wmp_ref,
             dyai_ref, do_ref, dp_ref, dl_ref, dwo_ref, dwgp_ref, dwmp_ref, a_o, a_gp, a_mp):
        @pl.when(pl.program_id(0) == 0)
        def _():
            for r in (a_o, a_gp, a_mp):
                r[...] = jnp.zeros_like(r)

        dh = _bf(dh_ref[...])
        dm = _dot_nt(dh, wo_ref[...])
        y_a, y_b = ya_ref[...].astype(F32), yb_ref[...].astype(F32)
        sg, sm = _sigmoid(gg_ref[...].astype(F32)), _sigmoid(gm_ref[...].astype(F32))
        d_ya, d_yb = _bf(sg * dm), _bf(sm * dm)
        dp_ref[:, D_MODEL:2 * D_MODEL] = _bf(dm * y_a * sg * (1.0 - sg))
        dp_ref[:, 2 * D_MODEL:] = _bf(dm * y_b * sm * (1.0 - sm))
        a_o[...] += _dot_tn(_bf(sg * y_a + sm * y_b), dh)
        a_gp[...] += _dot_tn(yai_ref[...], d_ya)
        a_mp[...] += _dot_tn(ybi_ref[...], d_yb)
        dyai_ref[...] = _bf(_dot_nt(d_ya, wgp_ref[...]))
        dy = _dot_nt(d_yb, wmp_ref[...])
        mz, o = mz_ref[...].astype(F32), ob_ref[...].astype(F32)
        s = _sigmoid(mz)
        do = _bf(dy * (mz * s))
        do_ref[...] = do
        dp_ref[:, :D_MODEL] = _bf(dy * o * (s * (1.0 + mz * (1.0 - s))))
        prod = do.astype(F32) * o
        for h in range(MLA_HEADS):
            dl = jnp.sum(prod[:, h * MLA_DV:(h + 1) * MLA_DV], axis=-1, keepdims=True)
            dl_ref[0, h] = jnp.broadcast_to(dl, (tm, LANE))

        @pl.when(pl.program_id(0) == nsteps - 1)
        def _():
            pltpu.sync_copy(a_o, dwo_ref)
            pltpu.sync_copy(a_gp, dwgp_ref)
            pltpu.sync_copy(a_mp, dwmp_ref)

    tok = lambda c: pl.BlockSpec((tm, D_MODEL), lambda i: (i, c))
    wspec = pl.BlockSpec((D_MODEL, D_MODEL), lambda i: (0, 0))
    anyspec = pl.BlockSpec(memory_space=pl.ANY)
    wshape = jax.ShapeDtypeStruct((D_MODEL, D_MODEL), F32)
    return pl.pallas_call(
        body, name="mid_bwd", grid=(nsteps,),
        in_specs=[tok(0), tok(0), tok(0), tok(C_MZ // D_MODEL), tok(C_GG // D_MODEL), tok(C_GM // D_MODEL),
                  tok(0), tok(0), tok(0), wspec, wspec, wspec],
        out_specs=[tok(0), tok(0), pl.BlockSpec((tm, group), lambda i: (i, C_MZ // group)),
                   pl.BlockSpec((1, MLA_HEADS, tm, LANE), lambda i: (i // nb, 0, i % nb, 0)),
                   anyspec, anyspec, anyspec],
        out_shape=[jax.ShapeDtypeStruct((tp, D_MODEL), BF16)] * 2 + [jax.ShapeDtypeStruct((tp, N_EXT), BF16),
                   jax.ShapeDtypeStruct((bsz, MLA_HEADS, lp, LANE), F32)] + [wshape] * 3,
        scratch_shapes=[pltpu.VMEM((D_MODEL, D_MODEL), F32)] * 3,
        compiler_params=_cp(("arbitrary",), 56),
    )(dh2, y_a, y_b, proj, proj, proj, ya_in, yb_in, o_b, w_o, w_gp, w_mp)


MESH_ID = pl.DeviceIdType.MESH
EXCHANGE_SEMS = [pltpu.SemaphoreType.DMA((N_DEV - 1,)), pltpu.SemaphoreType.DMA((N_DEV - 1,)), pltpu.SemaphoreType.DMA]


def _my_place():
    return lax.axis_index("x"), lax.axis_index("y"), lax.axis_index("c")


def _exchange(g_ref, recv_ref, send_sems, recv_sems, local_sem, start, same=False):
    x, y, c = _my_place()
    me = 4 * x + 2 * y + c
    own = pltpu.make_async_copy(g_ref if same else g_ref.at[me], recv_ref.at[me], local_sem)
    sends, lands = [], []
    for d in range(1, N_DEV):
        px = 1 - x if d & 4 else x
        py = 1 - y if d & 2 else y
        pc = 1 - c if d & 1 else c
        peer = 4 * px + 2 * py + pc
        for slot, group in ((me, sends),) if start else ((me, sends), (peer, lands)):
            group.append(pltpu.make_async_remote_copy(
                src_ref=g_ref if same else g_ref.at[peer], dst_ref=recv_ref.at[slot], send_sem=send_sems.at[d - 1],
                recv_sem=recv_sems.at[d - 1], device_id=(px, py, pc), device_id_type=MESH_ID))
    if start:
        own.start()
        for cp in sends:
            cp.start()
    else:
        for cp in lands:
            cp.wait_recv()
        for cp in sends:
            cp.wait_send()
        own.wait()


def _dw_in(u, dproj, slabs):
    tp = u.shape[0]
    tm, tn = _big_tok(tp), EXT_BLOCK
    nj, ni = N_EXT // tn, tp // tm

    def body(u_ref, d_ref, g_ref, o_ref, recv_ref, send_sems, recv_sems, local_sem):
        j, i = pl.program_id(0), pl.program_id(1)

        @pl.when(jnp.logical_and(j == 0, i == 0))
        def _():
            _exchange(g_ref, recv_ref, send_sems, recv_sems, local_sem, True)

        @pl.when(i == 0)
        def _():
            o_ref[...] = jnp.zeros_like(o_ref)

        o_ref[...] += _dot_tn(d_ref[...], u_ref[...])

        @pl.when(jnp.logical_and(j == nj - 1, i == ni - 1))
        def _():
            _exchange(g_ref, recv_ref, send_sems, recv_sems, local_sem, False)

    anyspec = pl.BlockSpec(memory_space=pl.ANY)
    return pl.pallas_call(
        body, name="dw_in", grid=(nj, ni),
        in_specs=[pl.BlockSpec((tm, D_MODEL), lambda j, i: (i, 0)), pl.BlockSpec((tm, tn), lambda j, i: (i, j)), anyspec],
        out_specs=[pl.BlockSpec((tn, D_MODEL), lambda j, i: (j, 0)), anyspec],
        out_shape=[jax.ShapeDtypeStruct((N_EXT, D_MODEL), F32), jax.ShapeDtypeStruct(slabs.shape, slabs.dtype)],
        scratch_shapes=EXCHANGE_SEMS,
        compiler_params=_cp(("arbitrary", "arbitrary"), 48),
    )(u, dproj, slabs)


def _dx_in(dproj, w_ext, hp, dh2, norm_g, slabs):
    tp = hp.shape[0]
    tm, tk = _big_tok(tp), EXT_BLOCK
    nk = N_EXT // tk
    ni = tp // tm

    def body(d_ref, w_ref, h_ref, dh_ref, g_ref, s_ref, o_ref, dg_ref, recv_ref, acc, send_sems, recv_sems, local_sem):
        k = pl.program_id(1)

        @pl.when(jnp.logical_and(pl.program_id(0) == 0, k == 0))
        def _():
            _exchange(s_ref, recv_ref, send_sems, recv_sems, local_sem, True)

        @pl.when(jnp.logical_and(pl.program_id(0) == 0, k == 0))
        def _():
            dg_ref[...] = jnp.zeros_like(dg_ref)

        @pl.when(k == 0)
        def _():
            acc[...] = jnp.zeros_like(acc)

        acc[...] += _dot_nt(d_ref[...], w_ref[...])

        @pl.when(k == nk - 1)
        def _():
            g = g_ref[...]
            xh, r = _rms_fwd(h_ref[...])
            dx, dg = _rms_bwd(acc[...], xh, r, g)
            o_ref[...] = dh_ref[...] + dx
            dg_ref[...] += dg

        @pl.when(jnp.logical_and(pl.program_id(0) == ni - 1, k == nk - 1))
        def _():
            _exchange(s_ref, recv_ref, send_sems, recv_sems, local_sem, False)

    tok = pl.BlockSpec((tm, D_MODEL), lambda i, k: (i, 0))
    anyspec = pl.BlockSpec(memory_space=pl.ANY)
    return pl.pallas_call(
        body, name="dx_in", grid=(ni, nk),
        in_specs=[pl.BlockSpec((tm, tk), lambda i, k: (i, k)), pl.BlockSpec((D_MODEL, tk), lambda i, k: (0, k)),
                  tok, tok, pl.BlockSpec((1, D_MODEL), lambda i, k: (0, 0)), anyspec],
        out_specs=[tok, pl.BlockSpec((1, D_MODEL), lambda i, k: (0, 0)), anyspec],
        out_shape=[jax.ShapeDtypeStruct((tp, D_MODEL), F32), jax.ShapeDtypeStruct((1, D_MODEL), F32),
                   jax.ShapeDtypeStruct(slabs.shape, slabs.dtype)],
        scratch_shapes=[pltpu.VMEM((tm, D_MODEL), F32)] + EXCHANGE_SEMS,
        compiler_params=_cp(("arbitrary", "arbitrary"), 56),
    )(dproj, w_ext, hp, dh2, norm_g, slabs)


def _meta_grad(dhp3):
    bsz = dhp3.shape[0]

    def body(d_ref, o_ref):
        @pl.when(pl.program_id(0) == 0)
        def _():
            o_ref[...] = jnp.zeros_like(o_ref)

        o_ref[...] += d_ref[0]

    return pl.pallas_call(
        body, name="meta_grad", grid=(bsz,),
        in_specs=[pl.BlockSpec((1, N_META, D_MODEL), lambda b: (b, FRONT // N_META, 0))],
        out_specs=pl.BlockSpec((N_META, D_MODEL), lambda b: (0, 0)),
        out_shape=jax.ShapeDtypeStruct((N_META, D_MODEL), F32),
        compiler_params=_cp(("arbitrary",)),
    )(dhp3)


W_IN_SHARD = N_IN // N_DEV


def _pad_lanes(a, width=LANE):
    return jnp.pad(a, [(0, 0)] * (a.ndim - 1) + [(0, width - a.shape[-1])])


def _rot_cols(w):
    half = w.shape[-1] // 2
    return jnp.concatenate([-w[..., half:], w[..., :half]], axis=-1)


def _unrot_cols(dw):
    half = dw.shape[-1] // 2
    return jnp.concatenate([dw[..., half:], -dw[..., :half]], axis=-1)


def _w_in_cols(shards, lo, hi):
    parts = []
    for k in range(lo // W_IN_SHARD, (hi - 1) // W_IN_SHARD + 1):
        a, b = max(lo, k * W_IN_SHARD), min(hi, (k + 1) * W_IN_SHARD)
        parts.append(shards[k][:, a - k * W_IN_SHARD:b - k * W_IN_SHARD])
    return parts[0] if len(parts) == 1 else jnp.concatenate(parts, axis=1)


def _w_in_ext(shards):
    c = lambda lo, hi: _w_in_cols(shards, lo, hi)
    kr = c(O_KR, O_MZ)
    return jnp.concatenate([
        c(O_V, O_LR), c(O_Z, O_CQ), c(O_Q, O_K), c(O_K, O_V), c(O_MZ, O_GG), c(O_GG, O_GM), c(O_GM, N_IN),
        c(O_CKV, O_KR), _pad_lanes(kr), _pad_lanes(_rot_cols(kr)), _pad_lanes(c(O_LR, O_Z)), c(O_CQ, O_CKV)], axis=1)


def _w_in_grad_t(dwt):
    g = lambda start, width: dwt[start:start + width]
    half = MLA_ROPE // 2
    krot = g(C_KROT, MLA_ROPE)
    kr = g(C_KR, MLA_ROPE) + jnp.concatenate([krot[half:], -krot[:half]], axis=0)
    return jnp.concatenate([
        g(C_Q, GLA_KW), g(C_K, GLA_KW), g(C_V, GLA_VW), g(C_LR, GLA_RANK), g(C_Z, GLA_VW), g(C_CQ, MLA_QR),
        g(C_CKV, MLA_KVR), kr, g(C_MZ, D_MODEL), g(C_GG, D_MODEL), g(C_GM, D_MODEL)], axis=0)


def _rope_tables(lp):
    inv = 1.0 / (ROPE_BASE ** (jnp.arange(0, MLA_ROPE, 2, dtype=F32) / MLA_ROPE))
    ang = (jnp.arange(lp, dtype=F32) - FRONT)[:, None] * inv[None, :]
    cos, sin = jnp.cos(ang), jnp.sin(ang)
    return _pad_lanes(jnp.concatenate([cos, cos], axis=1)), _pad_lanes(jnp.concatenate([sin, sin], axis=1))


def _local_step(x, loss_target, w):
    bsz, seq, _ = x.shape
    lp = X0 + seq
    tp = bsz * lp
    assert lp % TOK == 0 and lp % GLA_ROWS == 0
    meta = jnp.broadcast_to(w["meta_tokens"][None], (bsz, N_META, D_MODEL))
    hp = jnp.concatenate([jnp.zeros((bsz, FRONT, D_MODEL), F32), meta, x], axis=1).reshape(tp, D_MODEL)
    cos_t, sin_t = _rope_tables(lp)

    w_ext = _w_in_ext(w["w_in"])
    u, proj, packed_all = _proj_in(hp, w["norm_g"], w_ext, w["packed"])
    packed_all, off = packed_all.reshape(N_DEV, -1), 0
    for n, shape, axis in PACKED:
        size = shape[0] * shape[1]
        w[n] = _join8(packed_all[:, off:off + size].reshape((N_DEV,) + shape), axis)
        off += size
    gw_pad = jnp.pad(w["gla_gate_w"], ((0, LANE - GLA_RANK), (0, 0)))
    uq = w["mla_w_uq"].reshape(MLA_QR, MLA_HEADS, MLA_QK)
    rope_w = uq[:, :, MLA_NOPE:]
    hw = MLA_HEADS * LANE
    wn = uq[:, :, :MLA_NOPE].reshape(MLA_QR, hw)
    wr = _pad_lanes(rope_w).reshape(MLA_QR, hw)
    wt = _pad_lanes(_rot_cols(rope_w)).reshape(MLA_QR, hw)
    ukv = w["mla_w_ukv"].reshape(MLA_KVR, MLA_HEADS, MLA_NOPE + MLA_DV)
    wk = ukv[:, :, :MLA_NOPE].reshape(MLA_KVR, hw)
    wv = ukv[:, :, MLA_NOPE:].reshape(MLA_KVR, hw)

    o_raw, ya_in, s_all = _gla_fwd(proj, gw_pad, w["gla_gate_b"], w["gla_norm_g"], bsz, lp)
    qf = _q_up(proj, w["mla_q_norm_g"], wn, wr, wt, cos_t, sin_t, bsz, lp)
    kf, vf = _kv_up(proj, w["mla_kv_norm_g"], wk, wv, cos_t, sin_t, bsz, lp)
    o_b, yb_in, lse = _attn_fwd(qf, kf, vf, proj, bsz, lp)
    y_a, y_b, dh2, loss, d_final_g = _mid_fwd(ya_in, yb_in, proj, hp, loss_target, w["gla_proj"], w["mla_proj"],
                                              w["w_out"], w["final_norm_g"], bsz, lp)
    d_ya, d_o, dproj, delta, d_w_out, d_gla_proj, d_mla_proj = _mid_bwd(
        dh2, y_a, y_b, proj, ya_in, yb_in, o_b, w["w_out"], w["gla_proj"], w["mla_proj"], bsz, lp)
    dproj, d_gate, d_gla_norm = _gla_bwd(proj, gw_pad, w["gla_gate_b"], w["gla_norm_g"], o_raw, s_all, d_ya, dproj,
                                         bsz, lp)
    d_lr, d_gw_pad, d_gate_b = _gate_bwd(d_gate, proj, gw_pad)
    dqf, dkf, dvf = _attn_bwd(qf, kf, vf, d_o, lse, delta, bsz, lp)
    dproj, d_wn, d_wr, d_wt, d_qn = _q_up_bwd(dqf, proj, w["mla_q_norm_g"], wn, wr, wt, cos_t, sin_t, dproj,
                                              bsz, lp)
    dproj, d_wk, d_wv, d_kvn = _kv_up_bwd(dkf, dvf, proj, w["mla_kv_norm_g"], wk, wv, cos_t, sin_t, d_lr, dproj,
                                          bsz, lp)

    d_rope = (d_wr.reshape(MLA_QR, MLA_HEADS, LANE)[:, :, :MLA_ROPE]
              + _unrot_cols(d_wt.reshape(MLA_QR, MLA_HEADS, LANE)[:, :, :MLA_ROPE]))
    d_uq = jnp.concatenate([d_wn.reshape(MLA_QR, MLA_HEADS, LANE), d_rope], axis=-1).reshape(MLA_QR, MLA_HEADS * MLA_QK)
    d_ukv = jnp.concatenate([d_wk.reshape(MLA_KVR, MLA_HEADS, LANE), d_wv.reshape(MLA_KVR, MLA_HEADS, LANE)],
                            axis=-1).reshape(MLA_KVR, MLA_HEADS * (MLA_NOPE + MLA_DV))
    mats = dict(gla_gate_w=d_gw_pad[:GLA_RANK], gla_proj=d_gla_proj, mla_w_uq=d_uq, mla_w_ukv=d_ukv,
                mla_proj=d_mla_proj, w_out=d_w_out)
    packed = _pad_rows(jnp.concatenate([_split8(mats[n], axis).reshape(N_DEV, -1) for n, _, axis in PACKED], axis=1),
                       PACK_ROWS)
    d_w_ext_t, packed_parts = _dw_in(u, dproj, _bf(packed))
    w_in_slabs = _bf(_w_in_grad_t(d_w_ext_t).reshape(N_DEV, W_IN_SHARD, D_MODEL))
    d_hp, d_norm_g, w_in_parts = _dx_in(dproj, w_ext, hp, dh2, w["norm_g"], w_in_slabs)
    d_hp3 = d_hp.reshape(bsz, lp, D_MODEL)
    small = dict(meta_tokens=_meta_grad(d_hp3), norm_g=d_norm_g, gla_gate_b=d_gate_b, gla_norm_g=d_gla_norm,
                 mla_q_norm_g=d_qn, mla_kv_norm_g=d_kvn, final_norm_g=d_final_g)
    return loss, d_hp3[:, X0:, :], w_in_parts, packed_parts, small


PACKED = (("gla_gate_w", (GLA_RANK, GLA_KW // N_DEV), 1),
          ("gla_proj", (D_MODEL // N_DEV, D_MODEL), 0), ("mla_w_uq", (MLA_QR, MLA_HEADS * MLA_QK // N_DEV), 1),
          ("mla_w_ukv", (MLA_KVR, MLA_HEADS * (MLA_NOPE + MLA_DV) // N_DEV), 1),
          ("mla_proj", (D_MODEL // N_DEV, D_MODEL), 0), ("w_out", (D_MODEL // N_DEV, D_MODEL), 0))
REPLICATED = (("norm_g", D_MODEL), ("gla_gate_b", GLA_KW), ("gla_norm_g", GLA_DV), ("mla_q_norm_g", MLA_QR),
              ("mla_kv_norm_g", MLA_KVR), ("final_norm_g", D_MODEL))
PACK_ROWS = 3744
PACK_BLOCK = 1248
SMALL_ROWS = 48
LOSS_ROW = N_META + 25
W_IN_BLOCK = 128


def _all_gather(shards):
    n_arr = len(shards)

    def body(*refs):
        x_refs, out_refs = refs[:n_arr], refs[n_arr:2 * n_arr]
        send_sems, recv_sems, local_sems = refs[2 * n_arr:]
        x, y, c = _my_place()
        me, sibling = (x, y, c), (x, y, 1 - c)
        chips = [(1 - x, y), (x, 1 - y), (1 - x, 1 - y)]

        def copy(a, k, block, to, from_input=False):
            slab = out_refs[a].at[4 * block[0] + 2 * block[1] + block[2]]
            return pltpu.make_async_remote_copy(
                src_ref=x_refs[a] if from_input else slab, dst_ref=slab,
                send_sem=send_sems.at[7 * a + k], recv_sem=recv_sems.at[7 * a + k], device_id=to,
                device_id_type=MESH_ID)

        arrays = range(n_arr)
        mine = [pltpu.make_async_copy(x_refs[a], out_refs[a].at[4 * x + 2 * y + c], local_sems.at[a]) for a in arrays]
        for cp in mine:
            cp.start()
        first = [copy(a, 0, me, sibling, True) for a in arrays]
        first += [copy(a, 1 + j, me, (*chip, c), True) for j, chip in enumerate(chips) for a in arrays]
        for cp in first:
            cp.start()
        passed = []
        for j, chip in enumerate(chips):
            for a in arrays:
                copy(a, 1 + j, (*chip, c), me).wait_recv()
                passed.append(copy(a, 4 + j, (*chip, c), sibling))
                passed[-1].start()
        for a in arrays:
            copy(a, 0, sibling, me).wait_recv()
        for j, chip in enumerate(chips):
            for a in arrays:
                copy(a, 4 + j, (*chip, 1 - c), me).wait_recv()
        for cp in first + passed:
            cp.wait_send()
        for cp in mine:
            cp.wait()

    anyspec = pl.BlockSpec(memory_space=pl.ANY)
    return pl.pallas_call(
        body, name="weights_all_gather",
        out_shape=[jax.ShapeDtypeStruct((N_DEV,) + s.shape, s.dtype) for s in shards],
        in_specs=[anyspec] * n_arr, out_specs=[anyspec] * n_arr,
        scratch_shapes=[pltpu.SemaphoreType.DMA((7 * n_arr,)), pltpu.SemaphoreType.DMA((7 * n_arr,)),
                        pltpu.SemaphoreType.DMA((n_arr,))],
    )(*shards)


def _small_exchange(slabs):
    def body(g_ref, recv_ref, send_sems, recv_sems, local_sem):
        _exchange(g_ref, recv_ref, send_sems, recv_sems, local_sem, True)
        _exchange(g_ref, recv_ref, send_sems, recv_sems, local_sem, False)

    vmem = pl.BlockSpec(memory_space=pltpu.VMEM)
    return pl.pallas_call(
        body, name="small_exchange", out_shape=jax.ShapeDtypeStruct(slabs.shape, slabs.dtype),
        in_specs=[vmem], out_specs=vmem, scratch_shapes=EXCHANGE_SEMS,
    )(slabs)


def _adamw(parts, w, m, v, block_rows, name):
    rows, cols = w.shape

    def body(p_ref, w_ref, m_ref, v_ref, g_out, d_out, m_out, v_out):
        g = p_ref[0].astype(F32)
        for s in range(1, N_DEV):
            g = g + p_ref[s].astype(F32)
        m_new = ADAM_B1 * m_ref[...] + (1.0 - ADAM_B1) * g
        v_new = ADAM_B2 * v_ref[...] + (1.0 - ADAM_B2) * (g * g)
        m_hat = m_new / (1.0 - ADAM_B1 ** ADAM_STEP)
        v_hat = v_new / (1.0 - ADAM_B2 ** ADAM_STEP)
        g_out[...] = g
        d_out[...] = -ADAM_LR * (m_hat / (jnp.sqrt(v_hat) + ADAM_EPS) + ADAM_WD * w_ref[...])
        m_out[...] = m_new
        v_out[...] = v_new

    spec = pl.BlockSpec((block_rows, cols), lambda i: (i, 0))
    return pl.pallas_call(
        body, name=name, grid=(pl.cdiv(rows, block_rows),),
        in_specs=[pl.BlockSpec((N_DEV, block_rows, cols), lambda i: (0, i, 0)), spec, spec, spec],
        out_specs=[spec] * 4, out_shape=[jax.ShapeDtypeStruct((rows, cols), F32)] * 4,
        compiler_params=_cp(("parallel",), 48),
    )(parts, w, m, v)


def _pad_rows(flat, rows):
    pad = rows * LANE - flat.shape[-1]
    flat = jnp.pad(flat, [(0, 0)] * (flat.ndim - 1) + [(0, pad)])
    return flat.reshape(flat.shape[:-1] + (rows, LANE))


def _pack_shards(shards):
    return _pad_rows(jnp.concatenate([shards[n].reshape(-1) for n, _, _ in PACKED]), PACK_ROWS)


def _unpack_shards(packed):
    flat, out, off = packed.reshape(-1), {}, 0
    for n, shape, _ in PACKED:
        size = shape[0] * shape[1]
        out[n] = flat[off:off + size].reshape(shape)
        off += size
    return out


def _split8(full, axis):
    r, c = full.shape
    if axis == 0:
        return full.reshape(N_DEV, r // N_DEV, c)
    return full.reshape(r, N_DEV, c // N_DEV).transpose(1, 0, 2)


def _join8(shards, axis):
    _, r, c = shards.shape
    if axis == 0:
        return shards.reshape(N_DEV * r, c)
    return shards.transpose(1, 0, 2).reshape(r, N_DEV * c)


def _pack_small(meta_shard, vals, loss_row):
    rows = jnp.concatenate([vals[n].reshape(-1, LANE) for n, _ in REPLICATED] + [loss_row], axis=0)
    rows = jnp.pad(rows, ((0, SMALL_ROWS - N_META - rows.shape[0]), (0, 0)))
    return jnp.concatenate([meta_shard, jnp.broadcast_to(rows, meta_shard.shape[:-2] + rows.shape)], axis=-2)


def _unpack_small(packed):
    out, off = {"meta_tokens": packed[:N_META]}, N_META
    for n, size in REPLICATED:
        out[n] = packed[off:off + size // LANE].reshape(1, size)
        off += size // LANE
    return out


def kernel(x, meta_tokens, norm_g, w_in, gla_gate_w, gla_gate_b, gla_norm_g, gla_proj, mla_q_norm_g, mla_w_uq, mla_kv_norm_g, mla_w_ukv, mla_proj, w_out, final_norm_g, loss_target, m_meta_tokens, m_norm_g, m_w_in, m_gla_gate_w, m_gla_gate_b, m_gla_norm_g, m_gla_proj, m_mla_q_norm_g, m_mla_w_uq, m_mla_kv_norm_g, m_mla_w_ukv, m_mla_proj, m_w_out, m_final_norm_g, v_meta_tokens, v_norm_g, v_w_in, v_gla_gate_w, v_gla_gate_b, v_gla_norm_g, v_gla_proj, v_mla_q_norm_g, v_mla_w_uq, v_mla_kv_norm_g, v_mla_w_ukv, v_mla_proj, v_w_out, v_final_norm_g):
    given = dict(meta_tokens=meta_tokens, norm_g=norm_g, w_in=w_in, gla_gate_w=gla_gate_w, gla_gate_b=gla_gate_b,
                 gla_norm_g=gla_norm_g, gla_proj=gla_proj, mla_q_norm_g=mla_q_norm_g, mla_w_uq=mla_w_uq,
                 mla_kv_norm_g=mla_kv_norm_g, mla_w_ukv=mla_w_ukv, mla_proj=mla_proj, w_out=w_out,
                 final_norm_g=final_norm_g)
    mom_m = dict(meta_tokens=m_meta_tokens, norm_g=m_norm_g, w_in=m_w_in, gla_gate_w=m_gla_gate_w,
                 gla_gate_b=m_gla_gate_b, gla_norm_g=m_gla_norm_g, gla_proj=m_gla_proj, mla_q_norm_g=m_mla_q_norm_g,
                 mla_w_uq=m_mla_w_uq, mla_kv_norm_g=m_mla_kv_norm_g, mla_w_ukv=m_mla_w_ukv, mla_proj=m_mla_proj,
                 w_out=m_w_out, final_norm_g=m_final_norm_g)
    mom_v = dict(meta_tokens=v_meta_tokens, norm_g=v_norm_g, w_in=v_w_in, gla_gate_w=v_gla_gate_w,
                 gla_gate_b=v_gla_gate_b, gla_norm_g=v_gla_norm_g, gla_proj=v_gla_proj, mla_q_norm_g=v_mla_q_norm_g,
                 mla_w_uq=v_mla_w_uq, mla_kv_norm_g=v_mla_kv_norm_g, mla_w_ukv=v_mla_w_ukv, mla_proj=v_mla_proj,
                 w_out=v_w_out, final_norm_g=v_final_norm_g)
    shapes = {n: a.shape for n, a in given.items()}
    shard2d = {n: s for n, s, _ in PACKED}
    shard2d["w_in"] = (D_MODEL, W_IN_SHARD)
    shard2d["meta_tokens"] = (N_META, LANE)

    def as2d(tree):
        out = {n: tree[n].reshape(shard2d[n]) for n in shard2d}
        out.update({n: tree[n].reshape(1, size) for n, size in REPLICATED})
        return out

    w_loc, m_loc, v_loc = as2d(given), as2d(mom_m), as2d(mom_v)

    w_in_all, meta_all = _all_gather([w_loc["w_in"].astype(BF16), w_loc["meta_tokens"]])
    flat = jnp.concatenate([w_loc[n].astype(BF16).reshape(-1) for n, _, _ in PACKED])
    full = {"w_in": w_in_all, "meta_tokens": _join8(meta_all, 1), "packed": _pad_rows(flat, PACK_ROWS)}
    for n, _ in REPLICATED:
        full[n] = w_loc[n]

    loss_part, grad_x, w_in_parts, packed_parts, small = _local_step(x, loss_target, full)
    small_all = _small_exchange(_pack_small(_split8(small["meta_tokens"], 1), small,
                                            jnp.broadcast_to(loss_part[:, :1], (1, LANE))))

    w_in_t = [t["w_in"].T for t in (w_loc, m_loc, v_loc)]
    g_w, d_w, m_w, v_w = (o.T for o in _adamw(w_in_parts, *w_in_t, W_IN_BLOCK, "adamw_w_in"))
    g_p, d_p, m_p, v_p = _adamw(packed_parts, _pack_shards(w_loc), _pack_shards(m_loc), _pack_shards(v_loc),
                                PACK_BLOCK, "adamw_packed")
    zero_row = jnp.zeros((1, LANE), F32)
    g_s, d_s, m_s, v_s = _adamw(small_all, *(_pack_small(t["meta_tokens"], t, zero_row) for t in (w_loc, m_loc, v_loc)),
                                SMALL_ROWS, "adamw_small")
    loss = g_s[LOSS_ROW, 0]

    order = ["meta_tokens", "norm_g", "w_in", "gla_gate_w", "gla_gate_b", "gla_norm_g", "gla_proj", "mla_q_norm_g",
             "mla_w_uq", "mla_kv_norm_g", "mla_w_ukv", "mla_proj", "w_out", "final_norm_g"]
    result = [loss, grad_x]
    for w_in_out, packed_sh, packed_sm in ((g_w, g_p, g_s), (d_w, d_p, d_s), (m_w, m_p, m_s), (v_w, v_p, v_s)):
        tree = _unpack_shards(packed_sh)
        tree.update(_unpack_small(packed_sm))
        tree["w_in"] = w_in_out
        result += [tree[n].reshape(shapes[n]) for n in order]
    return tuple(result)
```

```python
import jax
import jax.numpy as jnp
from jax import lax
from jax.experimental import pallas as pl
from jax.experimental.pallas import tpu as pltpu

F32 = jnp.float32
BF16 = jnp.bfloat16

D_MODEL = 1024
N_META = 16
EPS = 1e-6
FRONT = 48
X0 = FRONT + N_META
GLA_HEADS, GLA_DK, GLA_DV, GLA_RANK, GLA_CHUNK = 4, 128, 256, 16, 64
GLA_GATE_NORMALIZER = 16.0
GLA_KW = GLA_HEADS * GLA_DK
GLA_VW = GLA_HEADS * GLA_DV
MLA_HEADS, MLA_NOPE, MLA_ROPE, MLA_DV, MLA_QR, MLA_KVR = 8, 128, 64, 128, 256, 128
MLA_QK = MLA_NOPE + MLA_ROPE
ROPE_BASE = 10000.0
LANE = 128
QKW = 2 * LANE

C_V, C_Z, C_Q, C_K = 0, 1024, 2048, 2560
C_MZ, C_GG, C_GM = 3072, 4096, 5120
C_CKV, C_KR, C_KROT, C_LR = 6144, 6272, 6400, 6528
C_CQ = 6656
N_EXT = 6912
O_Q, O_K, O_V, O_LR, O_Z, O_CQ, O_CKV, O_KR, O_MZ, O_GG, O_GM, N_IN = (
    0, 512, 1024, 2048, 2064, 3088, 3344, 3472, 3536, 4560, 5584, 6608)

ADAM_LR, ADAM_B1, ADAM_B2, ADAM_EPS, ADAM_WD, ADAM_STEP = 0.001, 0.9, 0.999, 1e-08, 0.01, 10

N_DEV = 8
TOK = 192
ATT_BLOCK = 352
EXT_BLOCK = 1152
MXU_DEPTH = 256


def _cp(sems=None, vmem_mb=None):
    kw = {}
    if sems is not None:
        kw["dimension_semantics"] = sems
    if vmem_mb is not None:
        kw["vmem_limit_bytes"] = vmem_mb * 1024 * 1024
    return pltpu.CompilerParams(**kw)


def _dot(a, b):
    return jnp.dot(a, b, preferred_element_type=F32)


def _dot_nt(a, b):
    return lax.dot_general(a, b, (((1,), (1,)), ((), ())), preferred_element_type=F32)


def _dot_tn(a, b):
    return lax.dot_general(a, b, (((0,), (0,)), ((), ())), preferred_element_type=F32)


def _sigmoid(x):
    return 1.0 / (1.0 + jnp.exp(-x))


def _bf(x):
    return x.astype(BF16)


def _big_tok(tp):
    return 4 * TOK if tp % (4 * TOK) == 0 else TOK


def _attn_block(lp):
    return ATT_BLOCK if lp % ATT_BLOCK == 0 else TOK


def _proj_in(hp, norm_g, w_ext, packed):
    tp = hp.shape[0]
    tm, tn = _big_tok(tp), EXT_BLOCK
    ni, nj = tp // tm, N_EXT // tn

    def body(h_ref, g_ref, w_ref, p_ref, u_ref, o_ref, pall_ref, u_scr, send_sems, recv_sems, local_sem):
        i, j = pl.program_id(0), pl.program_id(1)

        @pl.when(jnp.logical_and(i == 0, j == 0))
        def _():
            _exchange(p_ref, pall_ref, send_sems, recv_sems, local_sem, True, same=True)

        @pl.when(j == 0)
        def _():
            x = h_ref[...]
            r = lax.rsqrt(jnp.mean(x * x, axis=-1, keepdims=True) + EPS)
            u = _bf(x * r * g_ref[...])
            u_scr[...] = u
            u_ref[...] = u

        o_ref[...] = _bf(_dot(u_scr[...], w_ref[...]))

        @pl.when(jnp.logical_and(i == ni - 1, j == nj - 1))
        def _():
            _exchange(p_ref, pall_ref, send_sems, recv_sems, local_sem, False, same=True)

    anyspec = pl.BlockSpec(memory_space=pl.ANY)
    return pl.pallas_call(
        body, name="proj_in", grid=(ni, nj),
        in_specs=[pl.BlockSpec((tm, D_MODEL), lambda i, j: (i, 0)),
                  pl.BlockSpec((1, D_MODEL), lambda i, j: (0, 0)),
                  pl.BlockSpec((D_MODEL, tn), lambda i, j: (0, j)), anyspec],
        out_specs=[pl.BlockSpec((tm, D_MODEL), lambda i, j: (i, 0)),
                   pl.BlockSpec((tm, tn), lambda i, j: (i, j)), anyspec],
        out_shape=[jax.ShapeDtypeStruct((tp, D_MODEL), BF16), jax.ShapeDtypeStruct((tp, N_EXT), BF16),
                   jax.ShapeDtypeStruct((N_DEV,) + packed.shape, packed.dtype)],
        scratch_shapes=[pltpu.VMEM((tm, D_MODEL), BF16)] + EXCHANGE_SEMS,
        compiler_params=_cp(("arbitrary", "arbitrary"), 48),
    )(hp, norm_g, w_ext, packed)


GLA_GROUP = 3
GLA_ROWS = GLA_GROUP * GLA_CHUNK


def _tri_dot(tri, x):
    hi = _bf(x)
    rest = x - hi.astype(F32)
    mid = _bf(rest)
    return _dot(tri, hi) + _dot(tri, mid) + _dot(tri, _bf(rest - mid.astype(F32)))


def _gla_gates(q_ref, k_ref, lr_ref, gw_ref, gb_ref, rows, not_first):
    z = _dot(lr_ref[rows, :], gw_ref[...]) + gb_ref[...]
    logsig = jnp.minimum(z, 0.0) - jnp.log(1.0 + jnp.exp(-jnp.abs(z)))
    row = lax.broadcasted_iota(jnp.int32, (GLA_CHUNK, GLA_KW), 0)
    live = jnp.logical_or(not_first, row >= FRONT)
    g = jnp.where(live, logsig * (1.0 / GLA_GATE_NORMALIZER), 0.0)
    ri = lax.broadcasted_iota(jnp.int32, (GLA_CHUNK, GLA_CHUNK), 0)
    ci = lax.broadcasted_iota(jnp.int32, (GLA_CHUNK, GLA_CHUNK), 1)
    tril = ci <= ri
    b = _tri_dot(_bf(tril.astype(F32)), g)
    bl = jnp.sum(jnp.where(row == GLA_CHUNK - 1, b, 0.0), axis=0, keepdims=True)
    eb, enb, elb, ebl = jnp.exp(b), jnp.exp(-b), jnp.exp(bl - b), jnp.exp(bl)
    q = q_ref[rows, :].astype(F32) * (GLA_DK ** -0.5)
    k = k_ref[rows, :].astype(F32)
    qe, ke, kl = q * eb, k * enb, k * elb
    return dict(z=z, live=live, tril=tril, row=row, eb=eb, enb=enb, elb=elb, ebl=ebl, qe=qe, ke=ke, kl=kl,
                qe_b=_bf(qe), ke_b=_bf(ke), kl_b=_bf(kl))


def _gla_in_specs(n_groups, rev):
    def rb(b, n):
        return b * n_groups + ((n_groups - 1 - n) if rev else n)

    return rb, [pl.BlockSpec((GLA_ROWS, GLA_KW), lambda b, n: (rb(b, n), C_Q // GLA_KW)),
                pl.BlockSpec((GLA_ROWS, GLA_KW), lambda b, n: (rb(b, n), C_K // GLA_KW)),
                pl.BlockSpec((GLA_ROWS, GLA_VW), lambda b, n: (rb(b, n), C_V // GLA_VW)),
                pl.BlockSpec((GLA_ROWS, GLA_VW), lambda b, n: (rb(b, n), C_Z // GLA_VW)),
                pl.BlockSpec((GLA_ROWS, LANE), lambda b, n: (rb(b, n), C_LR // LANE)),
                pl.BlockSpec((LANE, GLA_KW), lambda b, n: (0, 0)),
                pl.BlockSpec((1, GLA_KW), lambda b, n: (0, 0)),
                pl.BlockSpec((1, GLA_DV), lambda b, n: (0, 0))]


def _gla_fwd(proj, gw_pad, gate_b, gla_norm_g, bsz, lp):
    n_chunks = lp // GLA_CHUNK
    n_groups = n_chunks // GLA_GROUP
    tp = bsz * lp

    def body(q_ref, k_ref, v_ref, z_ref, lr_ref, gw_ref, gb_ref, gn_ref, oraw_ref, ya_ref, sall_ref, st_scr):
        grp = pl.program_id(1)

        @pl.when(grp == 0)
        def _():
            st_scr[...] = jnp.zeros_like(st_scr)

        chunks = [slice(j * GLA_CHUNK, (j + 1) * GLA_CHUNK) for j in range(GLA_GROUP)]
        cs = [_gla_gates(q_ref, k_ref, lr_ref, gw_ref, gb_ref, rows, True if j else grp > 0)
              for j, rows in enumerate(chunks)]
        gn = gn_ref[...]
        for h in range(GLA_HEADS):
            ks, vs = slice(h * GLA_DK, (h + 1) * GLA_DK), slice(h * GLA_DV, (h + 1) * GLA_DV)
            st = st_scr[h]
            for j, (rows, c) in enumerate(zip(chunks, cs)):
                sall_ref[0, j, h] = st
                v = v_ref[rows, vs]
                a = jnp.where(c["tril"], _dot_nt(c["qe_b"][:, ks], c["ke_b"][:, ks]), 0.0)
                o = _dot(_bf(a), v) + _dot_nt(c["qe_b"][:, ks], _bf(st))
                st = st * c["ebl"][:, ks] + _dot_tn(v, c["kl_b"][:, ks])
                oraw_ref[rows, vs] = o
                r = lax.rsqrt(jnp.mean(o * o, axis=-1, keepdims=True) + EPS)
                zg = z_ref[rows, vs].astype(F32)
                ya_ref[rows, vs] = _bf((o * r * gn) * (zg * _sigmoid(zg)))
            st_scr[h] = st

    rb, in_specs = _gla_in_specs(n_groups, False)
    return pl.pallas_call(
        body, name="gla_fwd", grid=(bsz, n_groups), in_specs=in_specs,
        out_specs=[pl.BlockSpec((GLA_ROWS, GLA_VW), lambda b, n: (rb(b, n), 0)),
                   pl.BlockSpec((GLA_ROWS, GLA_VW), lambda b, n: (rb(b, n), 0)),
                   pl.BlockSpec((1, GLA_GROUP, GLA_HEADS, GLA_DV, GLA_DK), lambda b, n: (b, n, 0, 0, 0))],
        out_shape=[jax.ShapeDtypeStruct((tp, GLA_VW), F32), jax.ShapeDtypeStruct((tp, GLA_VW), BF16),
                   jax.ShapeDtypeStruct((bsz, n_chunks, GLA_HEADS, GLA_DV, GLA_DK), F32)],
        scratch_shapes=[pltpu.VMEM((GLA_HEADS, GLA_DV, GLA_DK), F32)],
        compiler_params=_cp(("parallel", "arbitrary")),
    )(proj, proj, proj, proj, proj, gw_pad, gate_b, gla_norm_g)


def _gla_bwd(proj, gw_pad, gate_b, gla_norm_g, o_raw, s_all, d_ya, dproj, bsz, lp):
    n_chunks = lp // GLA_CHUNK
    n_groups = n_chunks // GLA_GROUP
    tp = bsz * lp

    def body(q_ref, k_ref, v_ref, z_ref, lr_ref, gw_ref, gb_ref, gn_ref, o_ref, s_ref, dya_ref, _,
             dp_ref, dz_ref, dgn_ref, dst_scr):
        dv_ref, dzg_ref = dp_ref.at[:, C_V:C_V + GLA_VW], dp_ref.at[:, C_Z:C_Z + GLA_VW]

        @pl.when(jnp.logical_and(pl.program_id(0) == 0, pl.program_id(1) == 0))
        def _():
            dgn_ref[...] = jnp.zeros_like(dgn_ref)

        @pl.when(pl.program_id(1) == 0)
        def _():
            dst_scr[...] = jnp.zeros_like(dst_scr)

        grp = n_groups - 1 - pl.program_id(1)
        chunks = [slice(j * GLA_CHUNK, (j + 1) * GLA_CHUNK) for j in range(GLA_GROUP)]
        cs = [_gla_gates(q_ref, k_ref, lr_ref, gw_ref, gb_ref, rows, True if j else grp > 0)
              for j, rows in enumerate(chunks)]
        gn = gn_ref[...]
        dgn = jnp.zeros((1, GLA_DV), F32)
        dqe_h, dke_h, dkl_h, dbl_h = ([[None] * GLA_HEADS for _ in chunks] for _ in range(4))
        for h in range(GLA_HEADS):
            ks, vs = slice(h * GLA_DK, (h + 1) * GLA_DK), slice(h * GLA_DV, (h + 1) * GLA_DV)
            dst = dst_scr[h]
            for j in reversed(range(GLA_GROUP)):
                rows, c = chunks[j], cs[j]
                v = v_ref[rows, vs]
                st = s_ref[0, j, h]
                o = o_ref[rows, vs]
                r = lax.rsqrt(jnp.mean(o * o, axis=-1, keepdims=True) + EPS)
                xh = o * r
                zg = z_ref[rows, vs].astype(F32)
                sg = _sigmoid(zg)
                dy = dya_ref[rows, vs].astype(F32)
                dzg_ref[rows, vs] = _bf(dy * (xh * gn) * (sg * (1.0 + zg * (1.0 - sg))))
                t = dy * (zg * sg)
                dgn += jnp.sum(t * xh, axis=0, keepdims=True)
                dxh = t * gn
                do_b = _bf(r * (dxh - xh * jnp.mean(dxh * xh, axis=-1, keepdims=True)))
                qe_b, ke_b, kl_b, dst_b = c["qe_b"][:, ks], c["ke_b"][:, ks], c["kl_b"][:, ks], _bf(dst)
                a = jnp.where(c["tril"], _dot_nt(qe_b, ke_b), 0.0)
                da_b = _bf(jnp.where(c["tril"], _dot_nt(do_b, v), 0.0))
                dqe_h[j][h] = _dot(da_b, ke_b) + _dot(do_b, _bf(st))
                dke_h[j][h] = _dot_tn(da_b, qe_b)
                dkl = _dot(v, dst_b)
                dkl_h[j][h] = dkl
                dv_ref[rows, vs] = _bf(_dot_tn(_bf(a), do_b) + _dot_nt(kl_b, dst_b))
                ddecay = jnp.sum(dst * st, axis=0, keepdims=True)
                dbl_h[j][h] = jnp.sum(dkl * c["kl"][:, ks], axis=0, keepdims=True) + ddecay * c["ebl"][:, ks]
                dst = dst * c["ebl"][:, ks] + _dot_tn(do_b, qe_b)
            dst_scr[h] = dst
        dgn_ref[...] += dgn
        ri = lax.broadcasted_iota(jnp.int32, (GLA_CHUNK, GLA_CHUNK), 0)
        ci = lax.broadcasted_iota(jnp.int32, (GLA_CHUNK, GLA_CHUNK), 1)
        triu = _bf((ci >= ri).astype(F32))
        for j, (rows, c) in enumerate(zip(chunks, cs)):
            dqe, dke, dkl, dbl = (jnp.concatenate(p[j], axis=1) for p in (dqe_h, dke_h, dkl_h, dbl_h))
            db = dqe * c["qe"] - dke * c["ke"] - dkl * c["kl"] + jnp.where(c["row"] == GLA_CHUNK - 1, dbl, 0.0)
            dg = _tri_dot(triu, db)
            dg = jnp.where(c["live"], dg, 0.0)
            dz_ref[rows, :] = dg * (1.0 / GLA_GATE_NORMALIZER) * _sigmoid(-c["z"])
            dp_ref[rows, C_Q:C_Q + GLA_KW] = _bf(dqe * c["eb"] * (GLA_DK ** -0.5))
            dp_ref[rows, C_K:C_K + GLA_KW] = _bf(dke * c["enb"] + dkl * c["elb"])

    rb, in_specs = _gla_in_specs(n_groups, True)
    wide = pl.BlockSpec((GLA_ROWS, GLA_VW), lambda b, n: (rb(b, n), 0))
    group = C_MZ
    return pl.pallas_call(
        body, name="gla_bwd", grid=(bsz, n_groups),
        in_specs=in_specs + [wide, pl.BlockSpec((1, GLA_GROUP, GLA_HEADS, GLA_DV, GLA_DK),
                                                lambda b, n: (b, n_groups - 1 - n, 0, 0, 0)), wide,
                             pl.BlockSpec(memory_space=pl.ANY)],
        out_specs=[pl.BlockSpec((GLA_ROWS, group), lambda b, n: (rb(b, n), 0)),
                   pl.BlockSpec((GLA_ROWS, GLA_KW), lambda b, n: (rb(b, n), 0)),
                   pl.BlockSpec((1, GLA_DV), lambda b, n: (0, 0))],
        out_shape=[jax.ShapeDtypeStruct((tp, N_EXT), BF16), jax.ShapeDtypeStruct((tp, GLA_KW), F32),
                   jax.ShapeDtypeStruct((1, GLA_DV), F32)],
        input_output_aliases={11: 0},
        scratch_shapes=[pltpu.VMEM((GLA_HEADS, GLA_DV, GLA_DK), F32)],
        compiler_params=_cp(("arbitrary", "arbitrary")),
    )(proj, proj, proj, proj, proj, gw_pad, gate_b, gla_norm_g, o_raw, s_all, d_ya, dproj)


def _gate_bwd(dz, proj, gw_pad):
    tp = dz.shape[0]
    tm = _big_tok(tp)

    def body(dz_ref, lr_ref, gw_ref, dlr_ref, dgw_ref, dgb_ref):
        @pl.when(pl.program_id(0) == 0)
        def _():
            dgw_ref[...] = jnp.zeros_like(dgw_ref)
            dgb_ref[...] = jnp.zeros_like(dgb_ref)

        dz = dz_ref[...]
        dz_b = _bf(dz)
        dlr_ref[...] = _bf(_dot_nt(dz_b, gw_ref[...]))
        dgw_ref[...] += _dot_tn(lr_ref[...], dz_b)
        dgb_ref[...] += jnp.sum(dz, axis=0, keepdims=True)

    return pl.pallas_call(
        body, name="gate_bwd", grid=(tp // tm,),
        in_specs=[pl.BlockSpec((tm, GLA_KW), lambda i: (i, 0)),
                  pl.BlockSpec((tm, LANE), lambda i: (i, C_LR // LANE)),
                  pl.BlockSpec((LANE, GLA_KW), lambda i: (0, 0))],
        out_specs=[pl.BlockSpec((tm, LANE), lambda i: (i, 0)),
                   pl.BlockSpec((LANE, GLA_KW), lambda i: (0, 0)),
                   pl.BlockSpec((1, GLA_KW), lambda i: (0, 0))],
        out_shape=[jax.ShapeDtypeStruct((tp, LANE), BF16), jax.ShapeDtypeStruct((LANE, GLA_KW), F32),
                   jax.ShapeDtypeStruct((1, GLA_KW), F32)],
        compiler_params=_cp(("arbitrary",)),
    )(dz, proj, gw_pad)


def _rms_fwd(x):
    r = lax.rsqrt(jnp.mean(x * x, axis=-1, keepdims=True) + EPS)
    return x * r, r


def _rms_bwd(dy, xh, r, g):
    dxh = dy * g
    dx = r * (dxh - xh * jnp.mean(dxh * xh, axis=-1, keepdims=True))
    return dx, jnp.sum(dy * xh, axis=0, keepdims=True)


def _q_up(proj, q_norm_g, wn, wr, wt, cos_t, sin_t, bsz, lp):
    tp = bsz * lp
    tok = _attn_block(lp)
    nb = lp // tok

    def body(cq_ref, g_ref, wn_ref, wr_ref, wt_ref, cos_ref, sin_ref, q_ref):
        xh, _ = _rms_fwd(cq_ref[...].astype(F32))
        cqn = _bf(xh * g_ref[...])
        nope = _dot(cqn, wn_ref[...])
        rope = _dot(cqn, wr_ref[...])
        rot = _dot(cqn, wt_ref[...])
        cos, sin = cos_ref[...], sin_ref[...]
        one = (lax.broadcasted_iota(jnp.int32, (tok, LANE), 1) == BIAS_LANE).astype(F32)
        for h in range(MLA_HEADS):
            sl = slice(h * LANE, (h + 1) * LANE)
            q_ref[:, h * QKW:h * QKW + LANE] = _bf(nope[:, sl])
            q_ref[:, h * QKW + LANE:(h + 1) * QKW] = _bf(rope[:, sl] * cos + rot[:, sl] * sin + one)

    wspec = pl.BlockSpec((MLA_QR, MLA_HEADS * LANE), lambda b, i: (0, 0))
    tspec = pl.BlockSpec((tok, LANE), lambda b, i: (i, 0))
    return pl.pallas_call(
        body, name="mla_q_up", grid=(bsz, nb),
        in_specs=[pl.BlockSpec((tok, MLA_QR), lambda b, i: (b * nb + i, C_CQ // MLA_QR)),
                  pl.BlockSpec((1, MLA_QR), lambda b, i: (0, 0)), wspec, wspec, wspec, tspec, tspec],
        out_specs=pl.BlockSpec((tok, MLA_HEADS * QKW), lambda b, i: (b * nb + i, 0)),
        out_shape=jax.ShapeDtypeStruct((tp, MLA_HEADS * QKW), BF16),
        compiler_params=_cp(("parallel", "parallel")),
    )(proj, q_norm_g, wn, wr, wt, cos_t, sin_t)


def _kv_up(proj, kv_norm_g, wk, wv, cos_t, sin_t, bsz, lp):
    tp = bsz * lp
    tok = _attn_block(lp)
    nb = lp // tok

    def body(ckv_ref, kr_ref, krot_ref, g_ref, wk_ref, wv_ref, cos_ref, sin_ref, k_ref, v_ref):
        xh, _ = _rms_fwd(ckv_ref[...].astype(F32))
        cn = _bf(xh * g_ref[...])
        kn = _dot(cn, wk_ref[...])
        v_ref[...] = _bf(_dot(cn, wv_ref[...]))
        pos = pl.program_id(1) * tok + lax.broadcasted_iota(jnp.int32, (tok, LANE), 0)
        lane = lax.broadcasted_iota(jnp.int32, (tok, LANE), 1)
        bias = jnp.where(jnp.logical_and(lane == BIAS_LANE, pos < FRONT), KEY_BIAS, 0.0)
        kr = _bf(kr_ref[...].astype(F32) * cos_ref[...] + krot_ref[...].astype(F32) * sin_ref[...] + bias)
        for h in range(MLA_HEADS):
            k_ref[:, h * QKW:h * QKW + LANE] = _bf(kn[:, h * LANE:(h + 1) * LANE])
            k_ref[:, h * QKW + LANE:(h + 1) * QKW] = kr

    wspec = pl.BlockSpec((MLA_KVR, MLA_HEADS * LANE), lambda b, i: (0, 0))
    tspec = pl.BlockSpec((tok, LANE), lambda b, i: (i, 0))
    return pl.pallas_call(
        body, name="mla_kv_up", grid=(bsz, nb),
        in_specs=[pl.BlockSpec((tok, LANE), lambda b, i: (b * nb + i, C_CKV // LANE)),
                  pl.BlockSpec((tok, LANE), lambda b, i: (b * nb + i, C_KR // LANE)),
                  pl.BlockSpec((tok, LANE), lambda b, i: (b * nb + i, C_KROT // LANE)),
                  pl.BlockSpec((1, MLA_KVR), lambda b, i: (0, 0)), wspec, wspec, tspec, tspec],
        out_specs=[pl.BlockSpec((tok, MLA_HEADS * QKW), lambda b, i: (b * nb + i, 0)),
                   pl.BlockSpec((tok, MLA_HEADS * LANE), lambda b, i: (b * nb + i, 0))],
        out_shape=[jax.ShapeDtypeStruct((tp, MLA_HEADS * QKW), BF16),
                   jax.ShapeDtypeStruct((tp, MLA_HEADS * LANE), BF16)],
        compiler_params=_cp(("parallel", "parallel")),
    )(proj, proj, proj, kv_norm_g, wk, wv, cos_t, sin_t)


ATT_SCALE = MLA_QK ** -0.5


KEY_BIAS = -1e30
BIAS_LANE = MLA_ROPE
NEG = 2 * KEY_BIAS
LOG2E = 1.4426950408889634
EXP2_SCALE = ATT_SCALE * LOG2E


def _causal_fill(s, r0, fill):
    tq, kmax = s.shape
    a = r0 // LANE * LANE
    mask = (a + lax.broadcasted_iota(jnp.int32, (tq, kmax - a), 1)
            <= r0 + lax.broadcasted_iota(jnp.int32, (tq, kmax - a), 0))
    right = jnp.where(mask, s[:, a:], fill)
    return jnp.concatenate([s[:, :a], right], axis=1) if a else right


def _attn_fwd(qf, kf, vf, proj, bsz, lp):
    tp = bsz * lp
    tq = _attn_block(lp)

    def body(q_ref, k_ref, v_ref, mz_ref, ob_ref, yb_ref, lse_ref):
        for r0 in range(0, lp, tq):
            rows, kmax = slice(r0, r0 + tq), r0 + tq
            s = _causal_fill(_dot_nt(q_ref[rows, :], k_ref[0:kmax, :]), r0, NEG)
            m = jnp.max(s, axis=-1, keepdims=True)
            p = jnp.exp2((s - m) * EXP2_SCALE)
            l = jnp.sum(p, axis=-1, keepdims=True)
            o = _dot(_bf(p), v_ref[0:kmax, :]) / l
            ob_ref[rows, :] = _bf(o)
            mz = mz_ref[rows, :].astype(F32)
            yb_ref[rows, :] = _bf(o * (mz * _sigmoid(mz)))
            lse_ref[0, 0, rows, :] = jnp.broadcast_to(m * EXP2_SCALE + jnp.log2(l), (tq, LANE))

    head = lambda off: pl.BlockSpec((lp, MLA_DV), lambda b, h: (b, off + h))
    return pl.pallas_call(
        body, name="mla_attn_fwd", grid=(bsz, MLA_HEADS),
        in_specs=[pl.BlockSpec((lp, QKW), lambda b, h: (b, h)), pl.BlockSpec((lp, QKW), lambda b, h: (b, h)),
                  head(0), head(C_MZ // MLA_DV)],
        out_specs=[head(0), head(0), pl.BlockSpec((1, 1, lp, LANE), lambda b, h: (b, h, 0, 0))],
        out_shape=[jax.ShapeDtypeStruct((tp, MLA_HEADS * MLA_DV), BF16),
                   jax.ShapeDtypeStruct((tp, MLA_HEADS * MLA_DV), BF16),
                   jax.ShapeDtypeStruct((bsz, MLA_HEADS, lp, LANE), F32)],
        compiler_params=_cp(("parallel", "parallel"), 56),
    )(qf, kf, vf, proj)


def _attn_bwd(qf, kf, vf, d_o, lse, delta, bsz, lp):
    tp = bsz * lp
    tq = _attn_block(lp)

    def body(q_ref, k_ref, v_ref, do_ref, lse_ref, dl_ref, dq_ref, dk_ref, dv_ref, dk_acc, dv_acc):
        dk_acc[...] = jnp.zeros_like(dk_acc)
        dv_acc[...] = jnp.zeros_like(dv_acc)
        for r0 in range(0, lp, tq):
            rows, kmax = slice(r0, r0 + tq), r0 + tq
            q, do = q_ref[rows, :], do_ref[rows, :]
            k, v = k_ref[0:kmax, :], v_ref[0:kmax, :]
            p = jnp.exp2(_dot_nt(q, k) * EXP2_SCALE - lse_ref[0, 0, rows, :][:, :1])
            p = _causal_fill(p, r0, 0.0)
            ds = _bf(p * (_dot_nt(do, v) - dl_ref[0, rows, :][:, :1]))
            dq_ref[rows, :] = _bf(_dot(ds, k) * ATT_SCALE)
            dk_acc[0:kmax, :] += _dot_tn(ds, q)
            dv_acc[0:kmax, :] += _dot_tn(_bf(p), do)
        dk_ref[...] = _bf(dk_acc[...] * ATT_SCALE)
        dv_ref[...] = _bf(dv_acc[...])

    wide = pl.BlockSpec((lp, QKW), lambda b, h: (b, h))
    narrow = pl.BlockSpec((lp, MLA_DV), lambda b, h: (b, h))
    stat = pl.BlockSpec((1, 1, lp, LANE), lambda b, h: (b, h, 0, 0))
    return pl.pallas_call(
        body, name="mla_attn_bwd", grid=(bsz, MLA_HEADS),
        in_specs=[wide, wide, narrow, narrow, stat, pl.BlockSpec((1, lp, LANE), lambda b, h: (h, b, 0))],
        out_specs=[wide, wide, narrow],
        out_shape=[jax.ShapeDtypeStruct((tp, MLA_HEADS * QKW), BF16), jax.ShapeDtypeStruct((tp, MLA_HEADS * QKW), BF16),
                   jax.ShapeDtypeStruct((tp, MLA_HEADS * MLA_DV), BF16)],
        scratch_shapes=[pltpu.VMEM((lp, QKW), F32), pltpu.VMEM((lp, MLA_DV), F32)],
        compiler_params=_cp(("parallel", "parallel"), 56),
    )(qf, kf, vf, d_o, lse, delta)


def _q_up_bwd(dqf, proj, q_norm_g, wn, wr, wt, cos_t, sin_t, dproj, bsz, lp):
    tp = bsz * lp
    tok = _attn_block(lp)
    nb = lp // tok
    hw = MLA_HEADS * LANE

    def body(dq_ref, cq_ref, g_ref, wn_ref, wr_ref, wt_ref, cos_ref, sin_ref, _,
             dcq_ref, dwn_ref, dwr_ref, dwt_ref, dg_ref):
        @pl.when(jnp.logical_and(pl.program_id(0) == 0, pl.program_id(1) == 0))
        def _():
            for r in (dwn_ref, dwr_ref, dwt_ref, dg_ref):
                r[...] = jnp.zeros_like(r)

        g = g_ref[...]
        xh, r = _rms_fwd(cq_ref[...].astype(F32))
        cqn = _bf(xh * g)
        cos, sin = cos_ref[...], sin_ref[...]
        dcqn = jnp.zeros((tok, MLA_QR), F32)
        for h in range(MLA_HEADS):
            sl = slice(h * LANE, (h + 1) * LANE)
            dn = dq_ref[:, h * QKW:h * QKW + LANE]
            dr = dq_ref[:, h * QKW + LANE:(h + 1) * QKW].astype(F32)
            dr_c, dr_s = _bf(dr * cos), _bf(dr * sin)
            dcqn += _dot_nt(dn, wn_ref[:, sl]) + _dot_nt(dr_c, wr_ref[:, sl]) + _dot_nt(dr_s, wt_ref[:, sl])
            dwn_ref[:, sl] += _dot_tn(cqn, dn)
            dwr_ref[:, sl] += _dot_tn(cqn, dr_c)
            dwt_ref[:, sl] += _dot_tn(cqn, dr_s)
        dx, dg = _rms_bwd(dcqn, xh, r, g)
        dcq_ref[...] = _bf(dx)
        dg_ref[...] += dg

    aspec = pl.BlockSpec((MLA_QR, hw), lambda b, i: (0, 0))
    tspec = pl.BlockSpec((tok, LANE), lambda b, i: (i, 0))
    return pl.pallas_call(
        body, name="mla_q_up_bwd", grid=(bsz, nb),
        in_specs=[pl.BlockSpec((tok, MLA_HEADS * QKW), lambda b, i: (b * nb + i, 0)),
                  pl.BlockSpec((tok, MLA_QR), lambda b, i: (b * nb + i, C_CQ // MLA_QR)),
                  pl.BlockSpec((1, MLA_QR), lambda b, i: (0, 0)), aspec, aspec, aspec, tspec, tspec,
                  pl.BlockSpec(memory_space=pl.ANY)],
        out_specs=[pl.BlockSpec((tok, MLA_QR), lambda b, i: (b * nb + i, C_CQ // MLA_QR)), aspec, aspec, aspec,
                   pl.BlockSpec((1, MLA_QR), lambda b, i: (0, 0))],
        out_shape=[jax.ShapeDtypeStruct((tp, N_EXT), BF16)] + [jax.ShapeDtypeStruct((MLA_QR, hw), F32)] * 3
        + [jax.ShapeDtypeStruct((1, MLA_QR), F32)],
        input_output_aliases={8: 0},
        compiler_params=_cp(("arbitrary", "arbitrary")),
    )(dqf, proj, q_norm_g, wn, wr, wt, cos_t, sin_t, dproj)


def _kv_up_bwd(dkf, dvf, proj, kv_norm_g, wk, wv, cos_t, sin_t, d_lr, dproj, bsz, lp):
    tp = bsz * lp
    tok = _attn_block(lp)
    nb = lp // tok
    hw = MLA_HEADS * LANE

    def body(dk_ref, dv_ref, ckv_ref, g_ref, wk_ref, wv_ref, cos_ref, sin_ref, dlr_ref, _,
             dp_ref, dwk_ref, dwv_ref, dg_ref):
        dckv_ref, dkr_ref, dkrot_ref = (dp_ref.at[:, j * LANE:(j + 1) * LANE] for j in range(3))
        dp_ref[:, 3 * LANE:] = dlr_ref[...]
        @pl.when(jnp.logical_and(pl.program_id(0) == 0, pl.program_id(1) == 0))
        def _():
            for r in (dwk_ref, dwv_ref, dg_ref):
                r[...] = jnp.zeros_like(r)

        g = g_ref[...]
        xh, r = _rms_fwd(ckv_ref[...].astype(F32))
        cn = _bf(xh * g)
        dv = dv_ref[...]
        dcn = _dot_nt(dv, wv_ref[...])
        dwv_ref[...] += _dot_tn(cn, dv)
        drope = jnp.zeros((tok, LANE), F32)
        for h in range(MLA_HEADS):
            sl = slice(h * LANE, (h + 1) * LANE)
            dn = dk_ref[:, h * QKW:h * QKW + LANE]
            drope += dk_ref[:, h * QKW + LANE:(h + 1) * QKW].astype(F32)
            dcn += _dot_nt(dn, wk_ref[:, sl])
            dwk_ref[:, sl] += _dot_tn(cn, dn)
        dkr_ref[...] = _bf(drope * cos_ref[...])
        dkrot_ref[...] = _bf(drope * sin_ref[...])
        dx, dg = _rms_bwd(dcn, xh, r, g)
        dckv_ref[...] = _bf(dx)
        dg_ref[...] += dg

    aspec = pl.BlockSpec((MLA_KVR, hw), lambda b, i: (0, 0))
    tspec = pl.BlockSpec((tok, LANE), lambda b, i: (i, 0))
    ospec = pl.BlockSpec((tok, LANE), lambda b, i: (b * nb + i, 0))
    return pl.pallas_call(
        body, name="mla_kv_up_bwd", grid=(bsz, nb),
        in_specs=[pl.BlockSpec((tok, MLA_HEADS * QKW), lambda b, i: (b * nb + i, 0)),
                  pl.BlockSpec((tok, hw), lambda b, i: (b * nb + i, 0)),
                  pl.BlockSpec((tok, LANE), lambda b, i: (b * nb + i, C_CKV // LANE)),
                  pl.BlockSpec((1, MLA_KVR), lambda b, i: (0, 0)), aspec, aspec, tspec, tspec, ospec,
                  pl.BlockSpec(memory_space=pl.ANY)],
        out_specs=[pl.BlockSpec((tok, 4 * LANE), lambda b, i: (b * nb + i, C_CKV // (4 * LANE))), aspec, aspec,
                   pl.BlockSpec((1, MLA_KVR), lambda b, i: (0, 0))],
        out_shape=[jax.ShapeDtypeStruct((tp, N_EXT), BF16)] + [jax.ShapeDtypeStruct((MLA_KVR, hw), F32)] * 2
        + [jax.ShapeDtypeStruct((1, MLA_KVR), F32)],
        input_output_aliases={9: 0},
        compiler_params=_cp(("arbitrary", "arbitrary")),
    )(dkf, dvf, proj, kv_norm_g, wk, wv, cos_t, sin_t, d_lr, dproj)


def _mid_fwd(ya_in, yb_in, proj, hp, target, w_gp, w_mp, w_o, final_g, bsz, lp):
    tp = bsz * lp
    tm = _attn_block(lp)
    nb = lp // tm
    last = pl.cdiv(lp - X0, tm) - 1

    def body(ya_ref, yb_ref, gg_ref, gm_ref, h_ref, ta_ref, tb_ref, wgp_ref, wmp_ref, wo_ref, fg_ref,
             ya_out, yb_out, dh_ref, loss_ref, dfg_ref):
        @pl.when(jnp.logical_and(pl.program_id(0) == 0, pl.program_id(1) == 0))
        def _():
            loss_ref[...] = jnp.zeros_like(loss_ref)
            dfg_ref[...] = jnp.zeros_like(dfg_ref)

        y_a = _dot(ya_ref[...], wgp_ref[...])
        y_b = _dot(yb_ref[...], wmp_ref[...])
        ya_out[...] = _bf(y_a)
        yb_out[...] = _bf(y_b)
        merged = _sigmoid(gg_ref[...].astype(F32)) * y_a + _sigmoid(gm_ref[...].astype(F32)) * y_b
        h2 = h_ref[...] + _dot(_bf(merged), wo_ref[...])
        fg = fg_ref[...]
        xh, r = _rms_fwd(h2)
        pos = pl.program_id(1) * tm + lax.broadcasted_iota(jnp.int32, (tm, 1), 0)
        t = jnp.concatenate([ta_ref[0, tm - X0:, :], tb_ref[0, :tm - X0, :]], axis=0)
        err = jnp.where(pos >= X0, xh * fg - t, 0.0)
        loss_ref[...] += 0.5 * jnp.sum(jnp.mean(err * err, axis=-1, keepdims=True), axis=0, keepdims=True)
        dy = err * (1.0 / D_MODEL)
        dx, dfg = _rms_bwd(dy, xh, r, fg)
        dh_ref[...] = dx
        dfg_ref[...] += dfg

    tok = lambda c: pl.BlockSpec((tm, D_MODEL), lambda b, i: (b * nb + i, c))
    wspec = pl.BlockSpec((D_MODEL, D_MODEL), lambda b, i: (0, 0))
    return pl.pallas_call(
        body, name="mid_fwd", grid=(bsz, nb),
        in_specs=[tok(0), tok(0), tok(C_GG // D_MODEL), tok(C_GM // D_MODEL), tok(0),
                  pl.BlockSpec((1, tm, D_MODEL), lambda b, i: (b, jnp.maximum(i - 1, 0), 0)),
                  pl.BlockSpec((1, tm, D_MODEL), lambda b, i: (b, jnp.minimum(i, last), 0)),
                  wspec, wspec, wspec, pl.BlockSpec((1, D_MODEL), lambda b, i: (0, 0))],
        out_specs=[tok(0), tok(0), tok(0), pl.BlockSpec((1, LANE), lambda b, i: (0, 0)),
                   pl.BlockSpec((1, D_MODEL), lambda b, i: (0, 0))],
        out_shape=[jax.ShapeDtypeStruct((tp, D_MODEL), BF16), jax.ShapeDtypeStruct((tp, D_MODEL), BF16),
                   jax.ShapeDtypeStruct((tp, D_MODEL), F32), jax.ShapeDtypeStruct((1, LANE), F32),
                   jax.ShapeDtypeStruct((1, D_MODEL), F32)],
        compiler_params=_cp(("arbitrary", "arbitrary"), 48),
    )(ya_in, yb_in, proj, proj, hp, target, target, w_gp, w_mp, w_o, final_g)


def _mid_bwd(dh2, y_a, y_b, proj, ya_in, yb_in, o_b, w_o, w_gp, w_mp, bsz, lp):
    tp = bsz * lp
    tm = MXU_DEPTH if tp % MXU_DEPTH == 0 else _attn_block(lp)
    nsteps = tp // tm
    group = 3 * D_MODEL

    def body(dh_ref, ya_ref, yb_ref, mz_ref, gg_ref, gm_ref, yai_ref, ybi_ref, ob_ref, wo_ref, wgp_ref, wmp_ref,
             dyai_ref, do_ref, dp_ref, dl_ref, dwo_ref, dwgp_ref, dwmp_ref, a_o, a_gp, a_mp):
        @pl.when(pl.program_id(0) == 0)
        def _():
            for r in (a_o, a_gp, a_mp):
                r[...] = jnp.zeros_like(r)

        dh = _bf(dh_ref[...])
        dm = _dot_nt(dh, wo_ref[...])
        y_a, y_b = ya_ref[...].astype(F32), yb_ref[...].astype(F32)
        sg, sm = _sigmoid(gg_ref[...].astype(F32)), _sigmoid(gm_ref[...].astype(F32))
        d_ya, d_yb = _bf(sg * dm), _bf(sm * dm)
        dp_ref[:, D_MODEL:2 * D_MODEL] = _bf(dm * y_a * sg * (1.0 - sg))
        dp_ref[:, 2 * D_MODEL:] = _bf(dm * y_b * sm * (1.0 - sm))
        a_o[...] += _dot_tn(_bf(sg * y_a + sm * y_b), dh)
        a_gp[...] += _dot_tn(yai_ref[...], d_ya)
        a_mp[...] += _dot_tn(ybi_ref[...], d_yb)
        dyai_ref[...] = _bf(_dot_nt(d_ya, wgp_ref[...]))
        dy = _dot_nt(d_yb, wmp_ref[...])
        mz, o = mz_ref[...].astype(F32), ob_ref[...].astype(F32)
        s = _sigmoid(mz)
        do = _bf(dy * (mz * s))
        do_ref[...] = do
        dp_ref[:, :D_MODEL] = _bf(dy * o * (s * (1.0 + mz * (1.0 - s))))
        prod = do.astype(F32) * o
        for h in range(MLA_HEADS):
            dl = jnp.sum(prod[:, h * MLA_DV:(h + 1) * MLA_DV], axis=-1, keepdims=True)
            dl_ref[h] = jnp.broadcast_to(dl, (tm, LANE))

        @pl.when(pl.program_id(0) == nsteps - 1)
        def _():
            pltpu.sync_copy(a_o, dwo_ref)
            pltpu.sync_copy(a_gp, dwgp_ref)
            pltpu.sync_copy(a_mp, dwmp_ref)

    tok = lambda c: pl.BlockSpec((tm, D_MODEL), lambda i: (i, c))
    wspec = pl.BlockSpec((D_MODEL, D_MODEL), lambda i: (0, 0))
    anyspec = pl.BlockSpec(memory_space=pl.ANY)
    wshape = jax.ShapeDtypeStruct((D_MODEL, D_MODEL), F32)
    return pl.pallas_call(
        body, name="mid_bwd", grid=(nsteps,),
        in_specs=[tok(0), tok(0), tok(0), tok(C_MZ // D_MODEL), tok(C_GG // D_MODEL), tok(C_GM // D_MODEL),
                  tok(0), tok(0), tok(0), wspec, wspec, wspec],
        out_specs=[tok(0), tok(0), pl.BlockSpec((tm, group), lambda i: (i, C_MZ // group)),
                   pl.BlockSpec((MLA_HEADS, tm, LANE), lambda i: (0, i, 0)), anyspec, anyspec, anyspec],
        out_shape=[jax.ShapeDtypeStruct((tp, D_MODEL), BF16)] * 2 + [jax.ShapeDtypeStruct((tp, N_EXT), BF16),
                   jax.ShapeDtypeStruct((MLA_HEADS, tp, LANE), F32)] + [wshape] * 3,
        scratch_shapes=[pltpu.VMEM((D_MODEL, D_MODEL), F32)] * 3,
        compiler_params=_cp(("arbitrary",), 56),
    )(dh2, y_a, y_b, proj, proj, proj, ya_in, yb_in, o_b, w_o, w_gp, w_mp)


MESH_ID = pl.DeviceIdType.MESH
EXCHANGE_SEMS = [pltpu.SemaphoreType.DMA((N_DEV - 1,)), pltpu.SemaphoreType.DMA((N_DEV - 1,)), pltpu.SemaphoreType.DMA]


def _my_place():
    return lax.axis_index("x"), lax.axis_index("y"), lax.axis_index("c")


def _exchange(g_ref, recv_ref, send_sems, recv_sems, local_sem, start, same=False):
    x, y, c = _my_place()
    me = 4 * x + 2 * y + c
    own = pltpu.make_async_copy(g_ref if same else g_ref.at[me], recv_ref.at[me], local_sem)
    sends, lands = [], []
    for d in range(1, N_DEV):
        px = 1 - x if d & 4 else x
        py = 1 - y if d & 2 else y
        pc = 1 - c if d & 1 else c
        peer = 4 * px + 2 * py + pc
        for slot, group in ((me, sends),) if start else ((me, sends), (peer, lands)):
            group.append(pltpu.make_async_remote_copy(
                src_ref=g_ref if same else g_ref.at[peer], dst_ref=recv_ref.at[slot], send_sem=send_sems.at[d - 1],
                recv_sem=recv_sems.at[d - 1], device_id=(px, py, pc), device_id_type=MESH_ID))
    if start:
        own.start()
        for cp in sends:
            cp.start()
    else:
        for cp in lands:
            cp.wait_recv()
        for cp in sends:
            cp.wait_send()
        own.wait()


def _dw_in(u, dproj, slabs):
    tp = u.shape[0]
    tm, tn = _big_tok(tp), EXT_BLOCK
    nj, ni = N_EXT // tn, tp // tm

    def body(u_ref, d_ref, g_ref, o_ref, recv_ref, send_sems, recv_sems, local_sem):
        j, i = pl.program_id(0), pl.program_id(1)

        @pl.when(jnp.logical_and(j == 0, i == 0))
        def _():
            _exchange(g_ref, recv_ref, send_sems, recv_sems, local_sem, True)

        @pl.when(i == 0)
        def _():
            o_ref[...] = jnp.zeros_like(o_ref)

        o_ref[...] += _dot_tn(d_ref[...], u_ref[...])

        @pl.when(jnp.logical_and(j == nj - 1, i == ni - 1))
        def _():
            _exchange(g_ref, recv_ref, send_sems, recv_sems, local_sem, False)

    anyspec = pl.BlockSpec(memory_space=pl.ANY)
    return pl.pallas_call(
        body, name="dw_in", grid=(nj, ni),
        in_specs=[pl.BlockSpec((tm, D_MODEL), lambda j, i: (i, 0)), pl.BlockSpec((tm, tn), lambda j, i: (i, j)), anyspec],
        out_specs=[pl.BlockSpec((tn, D_MODEL), lambda j, i: (j, 0)), anyspec],
        out_shape=[jax.ShapeDtypeStruct((N_EXT, D_MODEL), F32), jax.ShapeDtypeStruct(slabs.shape, slabs.dtype)],
        scratch_shapes=EXCHANGE_SEMS,
        compiler_params=_cp(("arbitrary", "arbitrary"), 48),
    )(u, dproj, slabs)


def _dx_in(dproj, w_ext, hp, dh2, norm_g, slabs):
    tp = hp.shape[0]
    tm, tk = _big_tok(tp), EXT_BLOCK
    nk = N_EXT // tk
    ni = tp // tm

    def body(d_ref, w_ref, h_ref, dh_ref, g_ref, s_ref, o_ref, dg_ref, recv_ref, acc, send_sems, recv_sems, local_sem):
        k = pl.program_id(1)

        @pl.when(jnp.logical_and(pl.program_id(0) == 0, k == 0))
        def _():
            _exchange(s_ref, recv_ref, send_sems, recv_sems, local_sem, True)

        @pl.when(jnp.logical_and(pl.program_id(0) == 0, k == 0))
        def _():
            dg_ref[...] = jnp.zeros_like(dg_ref)

        @pl.when(k == 0)
        def _():
            acc[...] = jnp.zeros_like(acc)

        acc[...] += _dot_nt(d_ref[...], w_ref[...])

        @pl.when(k == nk - 1)
        def _():
            g = g_ref[...]
            xh, r = _rms_fwd(h_ref[...])
            dx, dg = _rms_bwd(acc[...], xh, r, g)
            o_ref[...] = dh_ref[...] + dx
            dg_ref[...] += dg

        @pl.when(jnp.logical_and(pl.program_id(0) == ni - 1, k == nk - 1))
        def _():
            _exchange(s_ref, recv_ref, send_sems, recv_sems, local_sem, False)

    tok = pl.BlockSpec((tm, D_MODEL), lambda i, k: (i, 0))
    anyspec = pl.BlockSpec(memory_space=pl.ANY)
    return pl.pallas_call(
        body, name="dx_in", grid=(ni, nk),
        in_specs=[pl.BlockSpec((tm, tk), lambda i, k: (i, k)), pl.BlockSpec((D_MODEL, tk), lambda i, k: (0, k)),
                  tok, tok, pl.BlockSpec((1, D_MODEL), lambda i, k: (0, 0)), anyspec],
        out_specs=[tok, pl.BlockSpec((1, D_MODEL), lambda i, k: (0, 0)), anyspec],
        out_shape=[jax.ShapeDtypeStruct((tp, D_MODEL), F32), jax.ShapeDtypeStruct((1, D_MODEL), F32),
                   jax.ShapeDtypeStruct(slabs.shape, slabs.dtype)],
        scratch_shapes=[pltpu.VMEM((tm, D_MODEL), F32)] + EXCHANGE_SEMS,
        compiler_params=_cp(("arbitrary", "arbitrary"), 56),
    )(dproj, w_ext, hp, dh2, norm_g, slabs)


def _meta_grad(dhp3):
    bsz = dhp3.shape[0]

    def body(d_ref, o_ref):
        @pl.when(pl.program_id(0) == 0)
        def _():
            o_ref[...] = jnp.zeros_like(o_ref)

        o_ref[...] += d_ref[0]

    return pl.pallas_call(
        body, name="meta_grad", grid=(bsz,),
        in_specs=[pl.BlockSpec((1, N_META, D_MODEL), lambda b: (b, FRONT // N_META, 0))],
        out_specs=pl.BlockSpec((N_META, D_MODEL), lambda b: (0, 0)),
        out_shape=jax.ShapeDtypeStruct((N_META, D_MODEL), F32),
        compiler_params=_cp(("arbitrary",)),
    )(dhp3)


W_IN_SHARD = N_IN // N_DEV


def _pad_lanes(a, width=LANE):
    return jnp.pad(a, [(0, 0)] * (a.ndim - 1) + [(0, width - a.shape[-1])])


def _rot_cols(w):
    half = w.shape[-1] // 2
    return jnp.concatenate([-w[..., half:], w[..., :half]], axis=-1)


def _unrot_cols(dw):
    half = dw.shape[-1] // 2
    return jnp.concatenate([dw[..., half:], -dw[..., :half]], axis=-1)


def _w_in_cols(shards, lo, hi):
    parts = []
    for k in range(lo // W_IN_SHARD, (hi - 1) // W_IN_SHARD + 1):
        a, b = max(lo, k * W_IN_SHARD), min(hi, (k + 1) * W_IN_SHARD)
        parts.append(shards[k][:, a - k * W_IN_SHARD:b - k * W_IN_SHARD])
    return parts[0] if len(parts) == 1 else jnp.concatenate(parts, axis=1)


def _w_in_ext(shards):
    c = lambda lo, hi: _w_in_cols(shards, lo, hi)
    kr = c(O_KR, O_MZ)
    return jnp.concatenate([
        c(O_V, O_LR), c(O_Z, O_CQ), c(O_Q, O_K), c(O_K, O_V), c(O_MZ, O_GG), c(O_GG, O_GM), c(O_GM, N_IN),
        c(O_CKV, O_KR), _pad_lanes(kr), _pad_lanes(_rot_cols(kr)), _pad_lanes(c(O_LR, O_Z)), c(O_CQ, O_CKV)], axis=1)


def _w_in_grad_t(dwt):
    g = lambda start, width: dwt[start:start + width]
    half = MLA_ROPE // 2
    krot = g(C_KROT, MLA_ROPE)
    kr = g(C_KR, MLA_ROPE) + jnp.concatenate([krot[half:], -krot[:half]], axis=0)
    return jnp.concatenate([
        g(C_Q, GLA_KW), g(C_K, GLA_KW), g(C_V, GLA_VW), g(C_LR, GLA_RANK), g(C_Z, GLA_VW), g(C_CQ, MLA_QR),
        g(C_CKV, MLA_KVR), kr, g(C_MZ, D_MODEL), g(C_GG, D_MODEL), g(C_GM, D_MODEL)], axis=0)


def _rope_tables(lp):
    inv = 1.0 / (ROPE_BASE ** (jnp.arange(0, MLA_ROPE, 2, dtype=F32) / MLA_ROPE))
    ang = (jnp.arange(lp, dtype=F32) - FRONT)[:, None] * inv[None, :]
    cos, sin = jnp.cos(ang), jnp.sin(ang)
    return _pad_lanes(jnp.concatenate([cos, cos], axis=1)), _pad_lanes(jnp.concatenate([sin, sin], axis=1))


def _local_step(x, loss_target, w):
    bsz, seq, _ = x.shape
    lp = X0 + seq
    tp = bsz * lp
    assert lp % TOK == 0 and lp % GLA_ROWS == 0
    meta = jnp.broadcast_to(w["meta_tokens"][None], (bsz, N_META, D_MODEL))
    hp = jnp.concatenate([jnp.zeros((bsz, FRONT, D_MODEL), F32), meta, x], axis=1).reshape(tp, D_MODEL)
    cos_t, sin_t = _rope_tables(lp)

    w_ext = _w_in_ext(w["w_in"])
    u, proj, packed_all = _proj_in(hp, w["norm_g"], w_ext, w["packed"])
    packed_all, off = packed_all.reshape(N_DEV, -1), 0
    for n, shape, axis in PACKED:
        size = shape[0] * shape[1]
        w[n] = _join8(packed_all[:, off:off + size].reshape((N_DEV,) + shape), axis)
        off += size
    gw_pad = jnp.pad(w["gla_gate_w"], ((0, LANE - GLA_RANK), (0, 0)))
    uq = w["mla_w_uq"].reshape(MLA_QR, MLA_HEADS, MLA_QK)
    rope_w = uq[:, :, MLA_NOPE:]
    hw = MLA_HEADS * LANE
    wn = uq[:, :, :MLA_NOPE].reshape(MLA_QR, hw)
    wr = _pad_lanes(rope_w).reshape(MLA_QR, hw)
    wt = _pad_lanes(_rot_cols(rope_w)).reshape(MLA_QR, hw)
    ukv = w["mla_w_ukv"].reshape(MLA_KVR, MLA_HEADS, MLA_NOPE + MLA_DV)
    wk = ukv[:, :, :MLA_NOPE].reshape(MLA_KVR, hw)
    wv = ukv[:, :, MLA_NOPE:].reshape(MLA_KVR, hw)

    o_raw, ya_in, s_all = _gla_fwd(proj, gw_pad, w["gla_gate_b"], w["gla_norm_g"], bsz, lp)
    qf = _q_up(proj, w["mla_q_norm_g"], wn, wr, wt, cos_t, sin_t, bsz, lp)
    kf, vf = _kv_up(proj, w["mla_kv_norm_g"], wk, wv, cos_t, sin_t, bsz, lp)
    o_b, yb_in, lse = _attn_fwd(qf, kf, vf, proj, bsz, lp)
    y_a, y_b, dh2, loss, d_final_g = _mid_fwd(ya_in, yb_in, proj, hp, loss_target, w["gla_proj"], w["mla_proj"],
                                              w["w_out"], w["final_norm_g"], bsz, lp)
    d_ya, d_o, dproj, delta, d_w_out, d_gla_proj, d_mla_proj = _mid_bwd(
        dh2, y_a, y_b, proj, ya_in, yb_in, o_b, w["w_out"], w["gla_proj"], w["mla_proj"], bsz, lp)
    dproj, d_gate, d_gla_norm = _gla_bwd(proj, gw_pad, w["gla_gate_b"], w["gla_norm_g"], o_raw, s_all, d_ya, dproj,
                                         bsz, lp)
    d_lr, d_gw_pad, d_gate_b = _gate_bwd(d_gate, proj, gw_pad)
    dqf, dkf, dvf = _attn_bwd(qf, kf, vf, d_o, lse, delta, bsz, lp)
    dproj, d_wn, d_wr, d_wt, d_qn = _q_up_bwd(dqf, proj, w["mla_q_norm_g"], wn, wr, wt, cos_t, sin_t, dproj,
                                              bsz, lp)
    dproj, d_wk, d_wv, d_kvn = _kv_up_bwd(dkf, dvf, proj, w["mla_kv_norm_g"], wk, wv, cos_t, sin_t, d_lr, dproj,
                                          bsz, lp)

    d_rope = (d_wr.reshape(MLA_QR, MLA_HEADS, LANE)[:, :, :MLA_ROPE]
              + _unrot_cols(d_wt.reshape(MLA_QR, MLA_HEADS, LANE)[:, :, :MLA_ROPE]))
    d_uq = jnp.concatenate([d_wn.reshape(MLA_QR, MLA_HEADS, LANE), d_rope], axis=-1).reshape(MLA_QR, MLA_HEADS * MLA_QK)
    d_ukv = jnp.concatenate([d_wk.reshape(MLA_KVR, MLA_HEADS, LANE), d_wv.reshape(MLA_KVR, MLA_HEADS, LANE)],
                            axis=-1).reshape(MLA_KVR, MLA_HEADS * (MLA_NOPE + MLA_DV))
    mats = dict(gla_gate_w=d_gw_pad[:GLA_RANK], gla_proj=d_gla_proj, mla_w_uq=d_uq, mla_w_ukv=d_ukv,
                mla_proj=d_mla_proj, w_out=d_w_out)
    packed = _pad_rows(jnp.concatenate([_split8(mats[n], axis).reshape(N_DEV, -1) for n, _, axis in PACKED], axis=1),
                       PACK_ROWS)
    d_w_ext_t, packed_parts = _dw_in(u, dproj, _bf(packed))
    w_in_slabs = _bf(_w_in_grad_t(d_w_ext_t).reshape(N_DEV, W_IN_SHARD, D_MODEL))
    d_hp, d_norm_g, w_in_parts = _dx_in(dproj, w_ext, hp, dh2, w["norm_g"], w_in_slabs)
    d_hp3 = d_hp.reshape(bsz, lp, D_MODEL)
    small = dict(meta_tokens=_meta_grad(d_hp3), norm_g=d_norm_g, gla_gate_b=d_gate_b, gla_norm_g=d_gla_norm,
                 mla_q_norm_g=d_qn, mla_kv_norm_g=d_kvn, final_norm_g=d_final_g)
    return loss, d_hp3[:, X0:, :], w_in_parts, packed_parts, small


PACKED = (("gla_gate_w", (GLA_RANK, GLA_KW // N_DEV), 1),
          ("gla_proj", (D_MODEL // N_DEV, D_MODEL), 0), ("mla_w_uq", (MLA_QR, MLA_HEADS * MLA_QK // N_DEV), 1),
          ("mla_w_ukv", (MLA_KVR, MLA_HEADS * (MLA_NOPE + MLA_DV) // N_DEV), 1),
          ("mla_proj", (D_MODEL // N_DEV, D_MODEL), 0), ("w_out", (D_MODEL // N_DEV, D_MODEL), 0))
REPLICATED = (("norm_g", D_MODEL), ("gla_gate_b", GLA_KW), ("gla_norm_g", GLA_DV), ("mla_q_norm_g", MLA_QR),
              ("mla_kv_norm_g", MLA_KVR), ("final_norm_g", D_MODEL))
PACK_ROWS = 3744
PACK_BLOCK = 1248
SMALL_ROWS = 48
LOSS_ROW = N_META + 25
W_IN_BLOCK = 128


def _all_gather(shards):
    n_arr = len(shards)

    def body(*refs):
        x_refs, out_refs = refs[:n_arr], refs[n_arr:2 * n_arr]
        send_sems, recv_sems, local_sems = refs[2 * n_arr:]
        x, y, c = _my_place()
        me, sibling = (x, y, c), (x, y, 1 - c)
        chips = [(1 - x, y), (x, 1 - y), (1 - x, 1 - y)]

        def copy(a, k, block, to, from_input=False):
            slab = out_refs[a].at[4 * block[0] + 2 * block[1] + block[2]]
            return pltpu.make_async_remote_copy(
                src_ref=x_refs[a] if from_input else slab, dst_ref=slab,
                send_sem=send_sems.at[7 * a + k], recv_sem=recv_sems.at[7 * a + k], device_id=to,
                device_id_type=MESH_ID)

        arrays = range(n_arr)
        mine = [pltpu.make_async_copy(x_refs[a], out_refs[a].at[4 * x + 2 * y + c], local_sems.at[a]) for a in arrays]
        for cp in mine:
            cp.start()
        first = [copy(a, 0, me, sibling, True) for a in arrays]
        first += [copy(a, 1 + j, me, (*chip, c), True) for j, chip in enumerate(chips) for a in arrays]
        for cp in first:
            cp.start()
        passed = []
        for j, chip in enumerate(chips):
            for a in arrays:
                copy(a, 1 + j, (*chip, c), me).wait_recv()
                passed.append(copy(a, 4 + j, (*chip, c), sibling))
                passed[-1].start()
        for a in arrays:
            copy(a, 0, sibling, me).wait_recv()
        for j, chip in enumerate(chips):
            for a in arrays:
                copy(a, 4 + j, (*chip, 1 - c), me).wait_recv()
        for cp in first + passed:
            cp.wait_send()
        for cp in mine:
            cp.wait()

    anyspec = pl.BlockSpec(memory_space=pl.ANY)
    return pl.pallas_call(
        body, name="weights_all_gather",
        out_shape=[jax.ShapeDtypeStruct((N_DEV,) + s.shape, s.dtype) for s in shards],
        in_specs=[anyspec] * n_arr, out_specs=[anyspec] * n_arr,
        scratch_shapes=[pltpu.SemaphoreType.DMA((7 * n_arr,)), pltpu.SemaphoreType.DMA((7 * n_arr,)),
                        pltpu.SemaphoreType.DMA((n_arr,))],
    )(*shards)


def _small_exchange(slabs):
    def body(g_ref, recv_ref, send_sems, recv_sems, local_sem):
        _exchange(g_ref, recv_ref, send_sems, recv_sems, local_sem, True)
        _exchange(g_ref, recv_ref, send_sems, recv_sems, local_sem, False)

    vmem = pl.BlockSpec(memory_space=pltpu.VMEM)
    return pl.pallas_call(
        body, name="small_exchange", out_shape=jax.ShapeDtypeStruct(slabs.shape, slabs.dtype),
        in_specs=[vmem], out_specs=vmem, scratch_shapes=EXCHANGE_SEMS,
    )(slabs)


def _adamw(parts, w, m, v, block_rows, name):
    rows, cols = w.shape

    def body(p_ref, w_ref, m_ref, v_ref, g_out, d_out, m_out, v_out):
        g = p_ref[0].astype(F32)
        for s in range(1, N_DEV):
            g = g + p_ref[s].astype(F32)
        m_new = ADAM_B1 * m_ref[...] + (1.0 - ADAM_B1) * g
        v_new = ADAM_B2 * v_ref[...] + (1.0 - ADAM_B2) * (g * g)
        m_hat = m_new / (1.0 - ADAM_B1 ** ADAM_STEP)
        v_hat = v_new / (1.0 - ADAM_B2 ** ADAM_STEP)
        g_out[...] = g
        d_out[...] = -ADAM_LR * (m_hat / (jnp.sqrt(v_hat) + ADAM_EPS) + ADAM_WD * w_ref[...])
        m_out[...] = m_new
        v_out[...] = v_new

    spec = pl.BlockSpec((block_rows, cols), lambda i: (i, 0))
    return pl.pallas_call(
        body, name=name, grid=(pl.cdiv(rows, block_rows),),
        in_specs=[pl.BlockSpec((N_DEV, block_rows, cols), lambda i: (0, i, 0)), spec, spec, spec],
        out_specs=[spec] * 4, out_shape=[jax.ShapeDtypeStruct((rows, cols), F32)] * 4,
        compiler_params=_cp(("parallel",), 48),
    )(parts, w, m, v)


def _pad_rows(flat, rows):
    pad = rows * LANE - flat.shape[-1]
    flat = jnp.pad(flat, [(0, 0)] * (flat.ndim - 1) + [(0, pad)])
    return flat.reshape(flat.shape[:-1] + (rows, LANE))


def _pack_shards(shards):
    return _pad_rows(jnp.concatenate([shards[n].reshape(-1) for n, _, _ in PACKED]), PACK_ROWS)


def _unpack_shards(packed):
    flat, out, off = packed.reshape(-1), {}, 0
    for n, shape, _ in PACKED:
        size = shape[0] * shape[1]
        out[n] = flat[off:off + size].reshape(shape)
        off += size
    return out


def _split8(full, axis):
    r, c = full.shape
    if axis == 0:
        return full.reshape(N_DEV, r // N_DEV, c)
    return full.reshape(r, N_DEV, c // N_DEV).transpose(1, 0, 2)


def _join8(shards, axis):
    _, r, c = shards.shape
    if axis == 0:
        return shards.reshape(N_DEV * r, c)
    return shards.transpose(1, 0, 2).reshape(r, N_DEV * c)


def _pack_small(meta_shard, vals, loss_row):
    rows = jnp.concatenate([vals[n].reshape(-1, LANE) for n, _ in REPLICATED] + [loss_row], axis=0)
    rows = jnp.pad(rows, ((0, SMALL_ROWS - N_META - rows.shape[0]), (0, 0)))
    return jnp.concatenate([meta_shard, jnp.broadcast_to(rows, meta_shard.shape[:-2] + rows.shape)], axis=-2)


def _unpack_small(packed):
    out, off = {"meta_tokens": packed[:N_META]}, N_META
    for n, size in REPLICATED:
        out[n] = packed[off:off + size // LANE].reshape(1, size)
        off += size // LANE
    return out


def kernel(x, meta_tokens, norm_g, w_in, gla_gate_w, gla_gate_b, gla_norm_g, gla_proj, mla_q_norm_g, mla_w_uq, mla_kv_norm_g, mla_w_ukv, mla_proj, w_out, final_norm_g, loss_target, m_meta_tokens, m_norm_g, m_w_in, m_gla_gate_w, m_gla_gate_b, m_gla_norm_g, m_gla_proj, m_mla_q_norm_g, m_mla_w_uq, m_mla_kv_norm_g, m_mla_w_ukv, m_mla_proj, m_w_out, m_final_norm_g, v_meta_tokens, v_norm_g, v_w_in, v_gla_gate_w, v_gla_gate_b, v_gla_norm_g, v_gla_proj, v_mla_q_norm_g, v_mla_w_uq, v_mla_kv_norm_g, v_mla_w_ukv, v_mla_proj, v_w_out, v_final_norm_g):
    given = dict(meta_tokens=meta_tokens, norm_g=norm_g, w_in=w_in, gla_gate_w=gla_gate_w, gla_gate_b=gla_gate_b,
                 gla_norm_g=gla_norm_g, gla_proj=gla_proj, mla_q_norm_g=mla_q_norm_g, mla_w_uq=mla_w_uq,
                 mla_kv_norm_g=mla_kv_norm_g, mla_w_ukv=mla_w_ukv, mla_proj=mla_proj, w_out=w_out,
                 final_norm_g=final_norm_g)
    mom_m = dict(meta_tokens=m_meta_tokens, norm_g=m_norm_g, w_in=m_w_in, gla_gate_w=m_gla_gate_w,
                 gla_gate_b=m_gla_gate_b, gla_norm_g=m_gla_norm_g, gla_proj=m_gla_proj, mla_q_norm_g=m_mla_q_norm_g,
                 mla_w_uq=m_mla_w_uq, mla_kv_norm_g=m_mla_kv_norm_g, mla_w_ukv=m_mla_w_ukv, mla_proj=m_mla_proj,
                 w_out=m_w_out, final_norm_g=m_final_norm_g)
    mom_v = dict(meta_tokens=v_meta_tokens, norm_g=v_norm_g, w_in=v_w_in, gla_gate_w=v_gla_gate_w,
                 gla_gate_b=v_gla_gate_b, gla_norm_g=v_gla_norm_g, gla_proj=v_gla_proj, mla_q_norm_g=v_mla_q_norm_g,
                 mla_w_uq=v_mla_w_uq, mla_kv_norm_g=v_mla_kv_norm_g, mla_w_ukv=v_mla_w_ukv, mla_proj=v_mla_proj,
                 w_out=v_w_out, final_norm_g=v_final_norm_g)
    shapes = {n: a.shape for n, a in given.items()}
    shard2d = {n: s for n, s, _ in PACKED}
    shard2d["w_in"] = (D_MODEL, W_IN_SHARD)
    shard2d["meta_tokens"] = (N_META, LANE)

    def as2d(tree):
        out = {n: tree[n].reshape(shard2d[n]) for n in shard2d}
        out.update({n: tree[n].reshape(1, size) for n, size in REPLICATED})
        return out

    w_loc, m_loc, v_loc = as2d(given), as2d(mom_m), as2d(mom_v)

    w_in_all, meta_all = _all_gather([w_loc["w_in"].astype(BF16), w_loc["meta_tokens"]])
    flat = jnp.concatenate([w_loc[n].astype(BF16).reshape(-1) for n, _, _ in PACKED])
    full = {"w_in": w_in_all, "meta_tokens": _join8(meta_all, 1), "packed": _pad_rows(flat, PACK_ROWS)}
    for n, _ in REPLICATED:
        full[n] = w_loc[n]

    loss_part, grad_x, w_in_parts, packed_parts, small = _local_step(x, loss_target, full)
    small_all = _small_exchange(_pack_small(_split8(small["meta_tokens"], 1), small,
                                            jnp.broadcast_to(loss_part[:, :1], (1, LANE))))

    w_in_t = [t["w_in"].T for t in (w_loc, m_loc, v_loc)]
    g_w, d_w, m_w, v_w = (o.T for o in _adamw(w_in_parts, *w_in_t, W_IN_BLOCK, "adamw_w_in"))
    g_p, d_p, m_p, v_p = _adamw(packed_parts, _pack_shards(w_loc), _pack_shards(m_loc), _pack_shards(v_loc),
                                PACK_BLOCK, "adamw_packed")
    zero_row = jnp.zeros((1, LANE), F32)
    g_s, d_s, m_s, v_s = _adamw(small_all, *(_pack_small(t["meta_tokens"], t, zero_row) for t in (w_loc, m_loc, v_loc)),
                                SMALL_ROWS, "adamw_small")
    loss = g_s[LOSS_ROW, 0]

    order = ["meta_tokens", "norm_g", "w_in", "gla_gate_w", "gla_gate_b", "gla_norm_g", "gla_proj", "mla_q_norm_g",
             "mla_w_uq", "mla_kv_norm_g", "mla_w_ukv", "mla_proj", "w_out", "final_norm_g"]
    result = [loss, grad_x]
    for w_in_out, packed_sh, packed_sm in ((g_w, g_p, g_s), (d_w, d_p, d_s), (m_w, m_p, m_s), (v_w, v_p, v_s)):
        tree = _unpack_shards(packed_sh)
        tree.update(_unpack_small(packed_sm))
        tree["w_in"] = w_in_out
        result += [tree[n].reshape(shapes[n]) for n in order]
    return tuple(result)
```

```python
import jax
import jax.numpy as jnp
from jax import lax
from jax.experimental import pallas as pl
from jax.experimental.pallas import tpu as pltpu

F32 = jnp.float32
BF16 = jnp.bfloat16

D_MODEL = 1024
N_META = 16
EPS = 1e-6
FRONT = 48
X0 = FRONT + N_META
GLA_HEADS, GLA_DK, GLA_DV, GLA_RANK, GLA_CHUNK = 4, 128, 256, 16, 64
GLA_GATE_NORMALIZER = 16.0
GLA_KW = GLA_HEADS * GLA_DK
GLA_VW = GLA_HEADS * GLA_DV
MLA_HEADS, MLA_NOPE, MLA_ROPE, MLA_DV, MLA_QR, MLA_KVR = 8, 128, 64, 128, 256, 128
MLA_QK = MLA_NOPE + MLA_ROPE
ROPE_BASE = 10000.0
LANE = 128
QKW = 2 * LANE

C_V, C_Z, C_Q, C_K = 0, 1024, 2048, 2560
C_MZ, C_GG, C_GM = 3072, 4096, 5120
C_CKV, C_KR, C_KROT, C_LR = 6144, 6272, 6400, 6528
C_CQ = 6656
N_EXT = 6912
O_Q, O_K, O_V, O_LR, O_Z, O_CQ, O_CKV, O_KR, O_MZ, O_GG, O_GM, N_IN = (
    0, 512, 1024, 2048, 2064, 3088, 3344, 3472, 3536, 4560, 5584, 6608)

ADAM_LR, ADAM_B1, ADAM_B2, ADAM_EPS, ADAM_WD, ADAM_STEP = 0.001, 0.9, 0.999, 1e-08, 0.01, 10

N_DEV = 8
TOK = 192
ATT_BLOCK = 352
EXT_BLOCK = 1152
MXU_DEPTH = 256


def _cp(sems=None, vmem_mb=None):
    kw = {}
    if sems is not None:
        kw["dimension_semantics"] = sems
    if vmem_mb is not None:
        kw["vmem_limit_bytes"] = vmem_mb * 1024 * 1024
    return pltpu.CompilerParams(**kw)


def _dot(a, b):
    return jnp.dot(a, b, preferred_element_type=F32)


def _dot_nt(a, b):
    return lax.dot_general(a, b, (((1,), (1,)), ((), ())), preferred_element_type=F32)


def _dot_tn(a, b):
    return lax.dot_general(a, b, (((0,), (0,)), ((), ())), preferred_element_type=F32)


def _sigmoid(x):
    return 1.0 / (1.0 + jnp.exp(-x))


def _bf(x):
    return x.astype(BF16)


def _big_tok(tp):
    return 4 * TOK if tp % (4 * TOK) == 0 else TOK


def _attn_block(lp):
    return ATT_BLOCK if lp % ATT_BLOCK == 0 else TOK


def _proj_in(hp, norm_g, w_ext, packed):
    tp = hp.shape[0]
    tm, tn = _big_tok(tp), EXT_BLOCK
    ni, nj = tp // tm, N_EXT // tn

    def body(h_ref, g_ref, w_ref, p_ref, u_ref, o_ref, pall_ref, u_scr, send_sems, recv_sems, local_sem):
        i, j = pl.program_id(0), pl.program_id(1)

        @pl.when(jnp.logical_and(i == 0, j == 0))
        def _():
            _exchange(p_ref, pall_ref, send_sems, recv_sems, local_sem, True, same=True)

        @pl.when(j == 0)
        def _():
            x = h_ref[...]
            r = lax.rsqrt(jnp.mean(x * x, axis=-1, keepdims=True) + EPS)
            u = _bf(x * r * g_ref[...])
            u_scr[...] = u
            u_ref[...] = u

        o_ref[...] = _bf(_dot(u_scr[...], w_ref[...]))

        @pl.when(jnp.logical_and(i == ni - 1, j == nj - 1))
        def _():
            _exchange(p_ref, pall_ref, send_sems, recv_sems, local_sem, False, same=True)

    anyspec = pl.BlockSpec(memory_space=pl.ANY)
    return pl.pallas_call(
        body, name="proj_in", grid=(ni, nj),
        in_specs=[pl.BlockSpec((tm, D_MODEL), lambda i, j: (i, 0)),
                  pl.BlockSpec((1, D_MODEL), lambda i, j: (0, 0)),
                  pl.BlockSpec((D_MODEL, tn), lambda i, j: (0, j)), anyspec],
        out_specs=[pl.BlockSpec((tm, D_MODEL), lambda i, j: (i, 0)),
                   pl.BlockSpec((tm, tn), lambda i, j: (i, j)), anyspec],
        out_shape=[jax.ShapeDtypeStruct((tp, D_MODEL), BF16), jax.ShapeDtypeStruct((tp, N_EXT), BF16),
                   jax.ShapeDtypeStruct((N_DEV,) + packed.shape, packed.dtype)],
        scratch_shapes=[pltpu.VMEM((tm, D_MODEL), BF16)] + EXCHANGE_SEMS,
        compiler_params=_cp(("arbitrary", "arbitrary"), 48),
    )(hp, norm_g, w_ext, packed)


GLA_GROUP = 3
GLA_ROWS = GLA_GROUP * GLA_CHUNK


def _tri_dot(tri, x):
    hi = _bf(x)
    rest = x - hi.astype(F32)
    mid = _bf(rest)
    return _dot(tri, hi) + _dot(tri, mid) + _dot(tri, _bf(rest - mid.astype(F32)))


def _gla_gates(q_ref, k_ref, lr_ref, gw_ref, gb_ref, rows, not_first):
    z = _dot(lr_ref[rows, :], gw_ref[...]) + gb_ref[...]
    logsig = jnp.minimum(z, 0.0) - jnp.log(1.0 + jnp.exp(-jnp.abs(z)))
    row = lax.broadcasted_iota(jnp.int32, (GLA_CHUNK, GLA_KW), 0)
    live = jnp.logical_or(not_first, row >= FRONT)
    g = jnp.where(live, logsig * (1.0 / GLA_GATE_NORMALIZER), 0.0)
    ri = lax.broadcasted_iota(jnp.int32, (GLA_CHUNK, GLA_CHUNK), 0)
    ci = lax.broadcasted_iota(jnp.int32, (GLA_CHUNK, GLA_CHUNK), 1)
    tril = ci <= ri
    b = _tri_dot(_bf(tril.astype(F32)), g)
    bl = jnp.sum(jnp.where(row == GLA_CHUNK - 1, b, 0.0), axis=0, keepdims=True)
    eb, enb, elb, ebl = jnp.exp(b), jnp.exp(-b), jnp.exp(bl - b), jnp.exp(bl)
    q = q_ref[rows, :].astype(F32) * (GLA_DK ** -0.5)
    k = k_ref[rows, :].astype(F32)
    qe, ke, kl = q * eb, k * enb, k * elb
    return dict(z=z, live=live, tril=tril, row=row, eb=eb, enb=enb, elb=elb, ebl=ebl, qe=qe, ke=ke, kl=kl,
                qe_b=_bf(qe), ke_b=_bf(ke), kl_b=_bf(kl))


def _gla_in_specs(n_groups, rev):
    def rb(b, n):
        return b * n_groups + ((n_groups - 1 - n) if rev else n)

    return rb, [pl.BlockSpec((GLA_ROWS, GLA_KW), lambda b, n: (rb(b, n), C_Q // GLA_KW)),
                pl.BlockSpec((GLA_ROWS, GLA_KW), lambda b, n: (rb(b, n), C_K // GLA_KW)),
                pl.BlockSpec((GLA_ROWS, GLA_VW), lambda b, n: (rb(b, n), C_V // GLA_VW)),
                pl.BlockSpec((GLA_ROWS, GLA_VW), lambda b, n: (rb(b, n), C_Z // GLA_VW)),
                pl.BlockSpec((GLA_ROWS, LANE), lambda b, n: (rb(b, n), C_LR // LANE)),
                pl.BlockSpec((LANE, GLA_KW), lambda b, n: (0, 0)),
                pl.BlockSpec((1, GLA_KW), lambda b, n: (0, 0)),
                pl.BlockSpec((1, GLA_DV), lambda b, n: (0, 0))]


def _gla_fwd(proj, gw_pad, gate_b, gla_norm_g, bsz, lp):
    n_chunks = lp // GLA_CHUNK
    n_groups = n_chunks // GLA_GROUP
    tp = bsz * lp

    def body(q_ref, k_ref, v_ref, z_ref, lr_ref, gw_ref, gb_ref, gn_ref, oraw_ref, ya_ref, sall_ref, st_scr):
        grp = pl.program_id(1)

        @pl.when(grp == 0)
        def _():
            st_scr[...] = jnp.zeros_like(st_scr)

        chunks = [slice(j * GLA_CHUNK, (j + 1) * GLA_CHUNK) for j in range(GLA_GROUP)]
        cs = [_gla_gates(q_ref, k_ref, lr_ref, gw_ref, gb_ref, rows, True if j else grp > 0)
              for j, rows in enumerate(chunks)]
        gn = gn_ref[...]
        sts = [st_scr[h] for h in range(GLA_HEADS)]
        for j, (rows, c) in enumerate(zip(chunks, cs)):
            for h in range(GLA_HEADS):
                ks, vs = slice(h * GLA_DK, (h + 1) * GLA_DK), slice(h * GLA_DV, (h + 1) * GLA_DV)
                st = sts[h]
                sall_ref[0, j, h] = st
                v = v_ref[rows, vs]
                a = jnp.where(c["tril"], _dot_nt(c["qe_b"][:, ks], c["ke_b"][:, ks]), 0.0)
                o = _dot(_bf(a), v) + _dot_nt(c["qe_b"][:, ks], _bf(st))
                sts[h] = st * c["ebl"][:, ks] + _dot_tn(v, c["kl_b"][:, ks])
                oraw_ref[rows, vs] = o
                r = lax.rsqrt(jnp.mean(o * o, axis=-1, keepdims=True) + EPS)
                zg = z_ref[rows, vs].astype(F32)
                ya_ref[rows, vs] = _bf((o * r * gn) * (zg * _sigmoid(zg)))
        for h in range(GLA_HEADS):
            st_scr[h] = sts[h]

    rb, in_specs = _gla_in_specs(n_groups, False)
    return pl.pallas_call(
        body, name="gla_fwd", grid=(bsz, n_groups), in_specs=in_specs,
        out_specs=[pl.BlockSpec((GLA_ROWS, GLA_VW), lambda b, n: (rb(b, n), 0)),
                   pl.BlockSpec((GLA_ROWS, GLA_VW), lambda b, n: (rb(b, n), 0)),
                   pl.BlockSpec((1, GLA_GROUP, GLA_HEADS, GLA_DV, GLA_DK), lambda b, n: (b, n, 0, 0, 0))],
        out_shape=[jax.ShapeDtypeStruct((tp, GLA_VW), F32), jax.ShapeDtypeStruct((tp, GLA_VW), BF16),
                   jax.ShapeDtypeStruct((bsz, n_chunks, GLA_HEADS, GLA_DV, GLA_DK), F32)],
        scratch_shapes=[pltpu.VMEM((GLA_HEADS, GLA_DV, GLA_DK), F32)],
        compiler_params=_cp(("parallel", "arbitrary")),
    )(proj, proj, proj, proj, proj, gw_pad, gate_b, gla_norm_g)


def _gla_bwd(proj, gw_pad, gate_b, gla_norm_g, o_raw, s_all, d_ya, dproj, bsz, lp):
    n_chunks = lp // GLA_CHUNK
    n_groups = n_chunks // GLA_GROUP
    tp = bsz * lp

    def body(q_ref, k_ref, v_ref, z_ref, lr_ref, gw_ref, gb_ref, gn_ref, o_ref, s_ref, dya_ref, _,
             dp_ref, dz_ref, dgn_ref, dst_scr):
        dv_ref, dzg_ref = dp_ref.at[:, C_V:C_V + GLA_VW], dp_ref.at[:, C_Z:C_Z + GLA_VW]

        @pl.when(jnp.logical_and(pl.program_id(0) == 0, pl.program_id(1) == 0))
        def _():
            dgn_ref[...] = jnp.zeros_like(dgn_ref)

        @pl.when(pl.program_id(1) == 0)
        def _():
            dst_scr[...] = jnp.zeros_like(dst_scr)

        grp = n_groups - 1 - pl.program_id(1)
        chunks = [slice(j * GLA_CHUNK, (j + 1) * GLA_CHUNK) for j in range(GLA_GROUP)]
        cs = [_gla_gates(q_ref, k_ref, lr_ref, gw_ref, gb_ref, rows, True if j else grp > 0)
              for j, rows in enumerate(chunks)]
        gn = gn_ref[...]
        dgn = jnp.zeros((1, GLA_DV), F32)
        dqe_h, dke_h, dkl_h, dbl_h = ([[None] * GLA_HEADS for _ in chunks] for _ in range(4))
        dsts = [dst_scr[h] for h in range(GLA_HEADS)]
        for j in reversed(range(GLA_GROUP)):
            rows, c = chunks[j], cs[j]
            for h in range(GLA_HEADS):
                ks, vs = slice(h * GLA_DK, (h + 1) * GLA_DK), slice(h * GLA_DV, (h + 1) * GLA_DV)
                dst = dsts[h]
                v = v_ref[rows, vs]
                st = s_ref[0, j, h]
                o = o_ref[rows, vs]
                r = lax.rsqrt(jnp.mean(o * o, axis=-1, keepdims=True) + EPS)
                xh = o * r
                zg = z_ref[rows, vs].astype(F32)
                sg = _sigmoid(zg)
                dy = dya_ref[rows, vs].astype(F32)
                dzg_ref[rows, vs] = _bf(dy * (xh * gn) * (sg * (1.0 + zg * (1.0 - sg))))
                t = dy * (zg * sg)
                dgn += jnp.sum(t * xh, axis=0, keepdims=True)
                dxh = t * gn
                do_b = _bf(r * (dxh - xh * jnp.mean(dxh * xh, axis=-1, keepdims=True)))
                qe_b, ke_b, kl_b, dst_b = c["qe_b"][:, ks], c["ke_b"][:, ks], c["kl_b"][:, ks], _bf(dst)
                a = jnp.where(c["tril"], _dot_nt(qe_b, ke_b), 0.0)
                da_b = _bf(jnp.where(c["tril"], _dot_nt(do_b, v), 0.0))
                dqe_h[j][h] = _dot(da_b, ke_b) + _dot(do_b, _bf(st))
                dke_h[j][h] = _dot_tn(da_b, qe_b)
                dkl = _dot(v, dst_b)
                dkl_h[j][h] = dkl
                dv_ref[rows, vs] = _bf(_dot_tn(_bf(a), do_b) + _dot_nt(kl_b, dst_b))
                ddecay = jnp.sum(dst * st, axis=0, keepdims=True)
                dbl_h[j][h] = jnp.sum(dkl * c["kl"][:, ks], axis=0, keepdims=True) + ddecay * c["ebl"][:, ks]
                dsts[h] = dst * c["ebl"][:, ks] + _dot_tn(do_b, qe_b)
        for h in range(GLA_HEADS):
            dst_scr[h] = dsts[h]
        dgn_ref[...] += dgn
        ri = lax.broadcasted_iota(jnp.int32, (GLA_CHUNK, GLA_CHUNK), 0)
        ci = lax.broadcasted_iota(jnp.int32, (GLA_CHUNK, GLA_CHUNK), 1)
        triu = _bf((ci >= ri).astype(F32))
        for j, (rows, c) in enumerate(zip(chunks, cs)):
            dqe, dke, dkl, dbl = (jnp.concatenate(p[j], axis=1) for p in (dqe_h, dke_h, dkl_h, dbl_h))
            db = dqe * c["qe"] - dke * c["ke"] - dkl * c["kl"] + jnp.where(c["row"] == GLA_CHUNK - 1, dbl, 0.0)
            dg = _tri_dot(triu, db)
            dg = jnp.where(c["live"], dg, 0.0)
            dz_ref[rows, :] = dg * (1.0 / GLA_GATE_NORMALIZER) * _sigmoid(-c["z"])
            dp_ref[rows, C_Q:C_Q + GLA_KW] = _bf(dqe * c["eb"] * (GLA_DK ** -0.5))
            dp_ref[rows, C_K:C_K + GLA_KW] = _bf(dke * c["enb"] + dkl * c["elb"])

    rb, in_specs = _gla_in_specs(n_groups, True)
    wide = pl.BlockSpec((GLA_ROWS, GLA_VW), lambda b, n: (rb(b, n), 0))
    group = C_MZ
    return pl.pallas_call(
        body, name="gla_bwd", grid=(bsz, n_groups),
        in_specs=in_specs + [wide, pl.BlockSpec((1, GLA_GROUP, GLA_HEADS, GLA_DV, GLA_DK),
                                                lambda b, n: (b, n_groups - 1 - n, 0, 0, 0)), wide,
                             pl.BlockSpec(memory_space=pl.ANY)],
        out_specs=[pl.BlockSpec((GLA_ROWS, group), lambda b, n: (rb(b, n), 0)),
                   pl.BlockSpec((GLA_ROWS, GLA_KW), lambda b, n: (rb(b, n), 0)),
                   pl.BlockSpec((1, GLA_DV), lambda b, n: (0, 0))],
        out_shape=[jax.ShapeDtypeStruct((tp, N_EXT), BF16), jax.ShapeDtypeStruct((tp, GLA_KW), F32),
                   jax.ShapeDtypeStruct((1, GLA_DV), F32)],
        input_output_aliases={11: 0},
        scratch_shapes=[pltpu.VMEM((GLA_HEADS, GLA_DV, GLA_DK), F32)],
        compiler_params=_cp(("arbitrary", "arbitrary")),
    )(proj, proj, proj, proj, proj, gw_pad, gate_b, gla_norm_g, o_raw, s_all, d_ya, dproj)


def _gate_bwd(dz, proj, gw_pad):
    tp = dz.shape[0]
    tm = _big_tok(tp)

    def body(dz_ref, lr_ref, gw_ref, dlr_ref, dgw_ref, dgb_ref):
        @pl.when(pl.program_id(0) == 0)
        def _():
            dgw_ref[...] = jnp.zeros_like(dgw_ref)
            dgb_ref[...] = jnp.zeros_like(dgb_ref)

        dz = dz_ref[...]
        dz_b = _bf(dz)
        dlr_ref[...] = _bf(_dot_nt(dz_b, gw_ref[...]))
        dgw_ref[...] += _dot_tn(lr_ref[...], dz_b)
        dgb_ref[...] += jnp.sum(dz, axis=0, keepdims=True)

    return pl.pallas_call(
        body, name="gate_bwd", grid=(tp // tm,),
        in_specs=[pl.BlockSpec((tm, GLA_KW), lambda i: (i, 0)),
                  pl.BlockSpec((tm, LANE), lambda i: (i, C_LR // LANE)),
                  pl.BlockSpec((LANE, GLA_KW), lambda i: (0, 0))],
        out_specs=[pl.BlockSpec((tm, LANE), lambda i: (i, 0)),
                   pl.BlockSpec((LANE, GLA_KW), lambda i: (0, 0)),
                   pl.BlockSpec((1, GLA_KW), lambda i: (0, 0))],
        out_shape=[jax.ShapeDtypeStruct((tp, LANE), BF16), jax.ShapeDtypeStruct((LANE, GLA_KW), F32),
                   jax.ShapeDtypeStruct((1, GLA_KW), F32)],
        compiler_params=_cp(("arbitrary",)),
    )(dz, proj, gw_pad)


def _rms_fwd(x):
    r = lax.rsqrt(jnp.mean(x * x, axis=-1, keepdims=True) + EPS)
    return x * r, r


def _rms_bwd(dy, xh, r, g):
    dxh = dy * g
    dx = r * (dxh - xh * jnp.mean(dxh * xh, axis=-1, keepdims=True))
    return dx, jnp.sum(dy * xh, axis=0, keepdims=True)


def _q_up(proj, q_norm_g, wn, wr, wt, cos_t, sin_t, bsz, lp):
    tp = bsz * lp
    tok = _attn_block(lp)
    nb = lp // tok

    def body(cq_ref, g_ref, wn_ref, wr_ref, wt_ref, cos_ref, sin_ref, q_ref):
        xh, _ = _rms_fwd(cq_ref[...].astype(F32))
        cqn = _bf(xh * g_ref[...])
        nope = _dot(cqn, wn_ref[...])
        rope = _dot(cqn, wr_ref[...])
        rot = _dot(cqn, wt_ref[...])
        cos, sin = cos_ref[...], sin_ref[...]
        one = (lax.broadcasted_iota(jnp.int32, (tok, LANE), 1) == BIAS_LANE).astype(F32)
        for h in range(MLA_HEADS):
            sl = slice(h * LANE, (h + 1) * LANE)
            q_ref[:, h * QKW:h * QKW + LANE] = _bf(nope[:, sl])
            q_ref[:, h * QKW + LANE:(h + 1) * QKW] = _bf(rope[:, sl] * cos + rot[:, sl] * sin + one)

    wspec = pl.BlockSpec((MLA_QR, MLA_HEADS * LANE), lambda b, i: (0, 0))
    tspec = pl.BlockSpec((tok, LANE), lambda b, i: (i, 0))
    return pl.pallas_call(
        body, name="mla_q_up", grid=(bsz, nb),
        in_specs=[pl.BlockSpec((tok, MLA_QR), lambda b, i: (b * nb + i, C_CQ // MLA_QR)),
                  pl.BlockSpec((1, MLA_QR), lambda b, i: (0, 0)), wspec, wspec, wspec, tspec, tspec],
        out_specs=pl.BlockSpec((tok, MLA_HEADS * QKW), lambda b, i: (b * nb + i, 0)),
        out_shape=jax.ShapeDtypeStruct((tp, MLA_HEADS * QKW), BF16),
        compiler_params=_cp(("parallel", "parallel")),
    )(proj, q_norm_g, wn, wr, wt, cos_t, sin_t)


def _kv_up(proj, kv_norm_g, wk, wv, cos_t, sin_t, bsz, lp):
    tp = bsz * lp
    tok = _attn_block(lp)
    nb = lp // tok

    def body(ckv_ref, kr_ref, krot_ref, g_ref, wk_ref, wv_ref, cos_ref, sin_ref, k_ref, v_ref):
        xh, _ = _rms_fwd(ckv_ref[...].astype(F32))
        cn = _bf(xh * g_ref[...])
        kn = _dot(cn, wk_ref[...])
        v_ref[...] = _bf(_dot(cn, wv_ref[...]))
        pos = pl.program_id(1) * tok + lax.broadcasted_iota(jnp.int32, (tok, LANE), 0)
        lane = lax.broadcasted_iota(jnp.int32, (tok, LANE), 1)
        bias = jnp.where(jnp.logical_and(lane == BIAS_LANE, pos < FRONT), KEY_BIAS, 0.0)
        kr = _bf(kr_ref[...].astype(F32) * cos_ref[...] + krot_ref[...].astype(F32) * sin_ref[...] + bias)
        for h in range(MLA_HEADS):
            k_ref[:, h * QKW:h * QKW + LANE] = _bf(kn[:, h * LANE:(h + 1) * LANE])
            k_ref[:, h * QKW + LANE:(h + 1) * QKW] = kr

    wspec = pl.BlockSpec((MLA_KVR, MLA_HEADS * LANE), lambda b, i: (0, 0))
    tspec = pl.BlockSpec((tok, LANE), lambda b, i: (i, 0))
    return pl.pallas_call(
        body, name="mla_kv_up", grid=(bsz, nb),
        in_specs=[pl.BlockSpec((tok, LANE), lambda b, i: (b * nb + i, C_CKV // LANE)),
                  pl.BlockSpec((tok, LANE), lambda b, i: (b * nb + i, C_KR // LANE)),
                  pl.BlockSpec((tok, LANE), lambda b, i: (b * nb + i, C_KROT // LANE)),
                  pl.BlockSpec((1, MLA_KVR), lambda b, i: (0, 0)), wspec, wspec, tspec, tspec],
        out_specs=[pl.BlockSpec((tok, MLA_HEADS * QKW), lambda b, i: (b * nb + i, 0)),
                   pl.BlockSpec((tok, MLA_HEADS * LANE), lambda b, i: (b * nb + i, 0))],
        out_shape=[jax.ShapeDtypeStruct((tp, MLA_HEADS * QKW), BF16),
                   jax.ShapeDtypeStruct((tp, MLA_HEADS * LANE), BF16)],
        compiler_params=_cp(("parallel", "parallel")),
    )(proj, proj, proj, kv_norm_g, wk, wv, cos_t, sin_t)


ATT_SCALE = MLA_QK ** -0.5


KEY_BIAS = -1e30
BIAS_LANE = MLA_ROPE
NEG = 2 * KEY_BIAS
LOG2E = 1.4426950408889634
EXP2_SCALE = ATT_SCALE * LOG2E


def _causal_fill(s, r0, fill):
    tq, kmax = s.shape
    a = r0 // LANE * LANE
    mask = (a + lax.broadcasted_iota(jnp.int32, (tq, kmax - a), 1)
            <= r0 + lax.broadcasted_iota(jnp.int32, (tq, kmax - a), 0))
    right = jnp.where(mask, s[:, a:], fill)
    return jnp.concatenate([s[:, :a], right], axis=1) if a else right


def _attn_fwd(qf, kf, vf, proj, bsz, lp):
    tp = bsz * lp
    tq = _attn_block(lp)

    def body(q_ref, k_ref, v_ref, mz_ref, ob_ref, yb_ref, lse_ref):
        starts = list(range(0, lp, tq))
        for pair in (starts[i:i + 2] for i in range(0, len(starts), 2)):
            ss = [_causal_fill(_dot_nt(q_ref[r0:r0 + tq, :], k_ref[0:r0 + tq, :]), r0, NEG) for r0 in pair]
            ms = [jnp.max(s, axis=-1, keepdims=True) for s in ss]
            ps = [jnp.exp2((s - m) * EXP2_SCALE) for s, m in zip(ss, ms)]
            ls = [jnp.sum(p, axis=-1, keepdims=True) for p in ps]
            for r0, p, m, l in zip(pair, ps, ms, ls):
                rows = slice(r0, r0 + tq)
                o = _dot(_bf(p), v_ref[0:r0 + tq, :]) / l
                ob_ref[rows, :] = _bf(o)
                mz = mz_ref[rows, :].astype(F32)
                yb_ref[rows, :] = _bf(o * (mz * _sigmoid(mz)))
                lse_ref[0, 0, rows, :] = jnp.broadcast_to(m * EXP2_SCALE + jnp.log2(l), (tq, LANE))

    head = lambda off: pl.BlockSpec((lp, MLA_DV), lambda b, h: (b, off + h))
    return pl.pallas_call(
        body, name="mla_attn_fwd", grid=(bsz, MLA_HEADS),
        in_specs=[pl.BlockSpec((lp, QKW), lambda b, h: (b, h)), pl.BlockSpec((lp, QKW), lambda b, h: (b, h)),
                  head(0), head(C_MZ // MLA_DV)],
        out_specs=[head(0), head(0), pl.BlockSpec((1, 1, lp, LANE), lambda b, h: (b, h, 0, 0))],
        out_shape=[jax.ShapeDtypeStruct((tp, MLA_HEADS * MLA_DV), BF16),
                   jax.ShapeDtypeStruct((tp, MLA_HEADS * MLA_DV), BF16),
                   jax.ShapeDtypeStruct((bsz, MLA_HEADS, lp, LANE), F32)],
        compiler_params=_cp(("parallel", "parallel"), 56),
    )(qf, kf, vf, proj)


def _attn_bwd(qf, kf, vf, d_o, lse, delta, bsz, lp):
    tp = bsz * lp
    tq = _attn_block(lp)

    def body(q_ref, k_ref, v_ref, do_ref, lse_ref, dl_ref, dq_ref, dk_ref, dv_ref, dk_acc, dv_acc):
        dk_acc[...] = jnp.zeros_like(dk_acc)
        dv_acc[...] = jnp.zeros_like(dv_acc)
        starts = list(range(0, lp, tq))
        for pair in (starts[i:i + 2] for i in range(0, len(starts), 2)):
            ps = [_causal_fill(jnp.exp2(_dot_nt(q_ref[r0:r0 + tq, :], k_ref[0:r0 + tq, :]) * EXP2_SCALE
                                        - lse_ref[0, 0, r0:r0 + tq, :][:, :1]), r0, 0.0) for r0 in pair]
            dss = [_bf(p * (_dot_nt(do_ref[r0:r0 + tq, :], v_ref[0:r0 + tq, :]) - dl_ref[0, r0:r0 + tq, :][:, :1]))
                   for r0, p in zip(pair, ps)]
            for r0, p, ds in zip(pair, ps, dss):
                rows, kmax = slice(r0, r0 + tq), r0 + tq
                dq_ref[rows, :] = _bf(_dot(ds, k_ref[0:kmax, :]) * ATT_SCALE)
                dk_acc[0:kmax, :] += _dot_tn(ds, q_ref[rows, :])
                dv_acc[0:kmax, :] += _dot_tn(_bf(p), do_ref[rows, :])
        dk_ref[...] = _bf(dk_acc[...] * ATT_SCALE)
        dv_ref[...] = _bf(dv_acc[...])

    wide = pl.BlockSpec((lp, QKW), lambda b, h: (b, h))
    narrow = pl.BlockSpec((lp, MLA_DV), lambda b, h: (b, h))
    stat = pl.BlockSpec((1, 1, lp, LANE), lambda b, h: (b, h, 0, 0))
    return pl.pallas_call(
        body, name="mla_attn_bwd", grid=(bsz, MLA_HEADS),
        in_specs=[wide, wide, narrow, narrow, stat, pl.BlockSpec((1, lp, LANE), lambda b, h: (h, b, 0))],
        out_specs=[wide, wide, narrow],
        out_shape=[jax.ShapeDtypeStruct((tp, MLA_HEADS * QKW), BF16), jax.ShapeDtypeStruct((tp, MLA_HEADS * QKW), BF16),
                   jax.ShapeDtypeStruct((tp, MLA_HEADS * MLA_DV), BF16)],
        scratch_shapes=[pltpu.VMEM((lp, QKW), F32), pltpu.VMEM((lp, MLA_DV), F32)],
        compiler_params=_cp(("parallel", "parallel"), 56),
    )(qf, kf, vf, d_o, lse, delta)


def _q_up_bwd(dqf, proj, q_norm_g, wn, wr, wt, cos_t, sin_t, dproj, bsz, lp):
    tp = bsz * lp
    tok = _attn_block(lp)
    nb = lp // tok
    hw = MLA_HEADS * LANE

    def body(dq_ref, cq_ref, g_ref, wn_ref, wr_ref, wt_ref, cos_ref, sin_ref, _,
             dcq_ref, dwn_ref, dwr_ref, dwt_ref, dg_ref):
        @pl.when(jnp.logical_and(pl.program_id(0) == 0, pl.program_id(1) == 0))
        def _():
            for r in (dwn_ref, dwr_ref, dwt_ref, dg_ref):
                r[...] = jnp.zeros_like(r)

        g = g_ref[...]
        xh, r = _rms_fwd(cq_ref[...].astype(F32))
        cqn = _bf(xh * g)
        cos, sin = cos_ref[...], sin_ref[...]
        dcqn = jnp.zeros((tok, MLA_QR), F32)
        for h in range(MLA_HEADS):
            sl = slice(h * LANE, (h + 1) * LANE)
            dn = dq_ref[:, h * QKW:h * QKW + LANE]
            dr = dq_ref[:, h * QKW + LANE:(h + 1) * QKW].astype(F32)
            dr_c, dr_s = _bf(dr * cos), _bf(dr * sin)
            dcqn += _dot_nt(dn, wn_ref[:, sl]) + _dot_nt(dr_c, wr_ref[:, sl]) + _dot_nt(dr_s, wt_ref[:, sl])
            dwn_ref[:, sl] += _dot_tn(cqn, dn)
            dwr_ref[:, sl] += _dot_tn(cqn, dr_c)
            dwt_ref[:, sl] += _dot_tn(cqn, dr_s)
        dx, dg = _rms_bwd(dcqn, xh, r, g)
        dcq_ref[...] = _bf(dx)
        dg_ref[...] += dg

    aspec = pl.BlockSpec((MLA_QR, hw), lambda b, i: (0, 0))
    tspec = pl.BlockSpec((tok, LANE), lambda b, i: (i, 0))
    return pl.pallas_call(
        body, name="mla_q_up_bwd", grid=(bsz, nb),
        in_specs=[pl.BlockSpec((tok, MLA_HEADS * QKW), lambda b, i: (b * nb + i, 0)),
                  pl.BlockSpec((tok, MLA_QR), lambda b, i: (b * nb + i, C_CQ // MLA_QR)),
                  pl.BlockSpec((1, MLA_QR), lambda b, i: (0, 0)), aspec, aspec, aspec, tspec, tspec,
                  pl.BlockSpec(memory_space=pl.ANY)],
        out_specs=[pl.BlockSpec((tok, MLA_QR), lambda b, i: (b * nb + i, C_CQ // MLA_QR)), aspec, aspec, aspec,
                   pl.BlockSpec((1, MLA_QR), lambda b, i: (0, 0))],
        out_shape=[jax.ShapeDtypeStruct((tp, N_EXT), BF16)] + [jax.ShapeDtypeStruct((MLA_QR, hw), F32)] * 3
        + [jax.ShapeDtypeStruct((1, MLA_QR), F32)],
        input_output_aliases={8: 0},
        compiler_params=_cp(("arbitrary", "arbitrary")),
    )(dqf, proj, q_norm_g, wn, wr, wt, cos_t, sin_t, dproj)


def _kv_up_bwd(dkf, dvf, proj, kv_norm_g, wk, wv, cos_t, sin_t, d_lr, dproj, bsz, lp):
    tp = bsz * lp
    tok = _attn_block(lp)
    nb = lp // tok
    hw = MLA_HEADS * LANE

    def body(dk_ref, dv_ref, ckv_ref, g_ref, wk_ref, wv_ref, cos_ref, sin_ref, dlr_ref, _,
             dp_ref, dwk_ref, dwv_ref, dg_ref):
        dckv_ref, dkr_ref, dkrot_ref = (dp_ref.at[:, j * LANE:(j + 1) * LANE] for j in range(3))
        dp_ref[:, 3 * LANE:] = dlr_ref[...]
        @pl.when(jnp.logical_and(pl.program_id(0) == 0, pl.program_id(1) == 0))
        def _():
            for r in (dwk_ref, dwv_ref, dg_ref):
                r[...] = jnp.zeros_like(r)

        g = g_ref[...]
        xh, r = _rms_fwd(ckv_ref[...].astype(F32))
        cn = _bf(xh * g)
        dv = dv_ref[...]
        dcn = _dot_nt(dv, wv_ref[...])
        dwv_ref[...] += _dot_tn(cn, dv)
        drope = jnp.zeros((tok, LANE), F32)
        for h in range(MLA_HEADS):
            sl = slice(h * LANE, (h + 1) * LANE)
            dn = dk_ref[:, h * QKW:h * QKW + LANE]
            drope += dk_ref[:, h * QKW + LANE:(h + 1) * QKW].astype(F32)
            dcn += _dot_nt(dn, wk_ref[:, sl])
            dwk_ref[:, sl] += _dot_tn(cn, dn)
        dkr_ref[...] = _bf(drope * cos_ref[...])
        dkrot_ref[...] = _bf(drope * sin_ref[...])
        dx, dg = _rms_bwd(dcn, xh, r, g)
        dckv_ref[...] = _bf(dx)
        dg_ref[...] += dg

    aspec = pl.BlockSpec((MLA_KVR, hw), lambda b, i: (0, 0))
    tspec = pl.BlockSpec((tok, LANE), lambda b, i: (i, 0))
    ospec = pl.BlockSpec((tok, LANE), lambda b, i: (b * nb + i, 0))
    return pl.pallas_call(
        body, name="mla_kv_up_bwd", grid=(bsz, nb),
        in_specs=[pl.BlockSpec((tok, MLA_HEADS * QKW), lambda b, i: (b * nb + i, 0)),
                  pl.BlockSpec((tok, hw), lambda b, i: (b * nb + i, 0)),
                  pl.BlockSpec((tok, LANE), lambda b, i: (b * nb + i, C_CKV // LANE)),
                  pl.BlockSpec((1, MLA_KVR), lambda b, i: (0, 0)), aspec, aspec, tspec, tspec, ospec,
                  pl.BlockSpec(memory_space=pl.ANY)],
        out_specs=[pl.BlockSpec((tok, 4 * LANE), lambda b, i: (b * nb + i, C_CKV // (4 * LANE))), aspec, aspec,
                   pl.BlockSpec((1, MLA_KVR), lambda b, i: (0, 0))],
        out_shape=[jax.ShapeDtypeStruct((tp, N_EXT), BF16)] + [jax.ShapeDtypeStruct((MLA_KVR, hw), F32)] * 2
        + [jax.ShapeDtypeStruct((1, MLA_KVR), F32)],
        input_output_aliases={9: 0},
        compiler_params=_cp(("arbitrary", "arbitrary")),
    )(dkf, dvf, proj, kv_norm_g, wk, wv, cos_t, sin_t, d_lr, dproj)


def _mid_fwd(ya_in, yb_in, proj, hp, target, w_gp, w_mp, w_o, final_g, bsz, lp):
    tp = bsz * lp
    tm = _attn_block(lp)
    nb = lp // tm
    last = pl.cdiv(lp - X0, tm) - 1

    def body(ya_ref, yb_ref, gg_ref, gm_ref, h_ref, ta_ref, tb_ref, wgp_ref, wmp_ref, wo_ref, fg_ref,
             ya_out, yb_out, dh_ref, loss_ref, dfg_ref):
        @pl.when(jnp.logical_and(pl.program_id(0) == 0, pl.program_id(1) == 0))
        def _():
            loss_ref[...] = jnp.zeros_like(loss_ref)
            dfg_ref[...] = jnp.zeros_like(dfg_ref)

        y_a = _dot(ya_ref[...], wgp_ref[...])
        y_b = _dot(yb_ref[...], wmp_ref[...])
        ya_out[...] = _bf(y_a)
        yb_out[...] = _bf(y_b)
        merged = _sigmoid(gg_ref[...].astype(F32)) * y_a + _sigmoid(gm_ref[...].astype(F32)) * y_b
        h2 = h_ref[...] + _dot(_bf(merged), wo_ref[...])
        fg = fg_ref[...]
        xh, r = _rms_fwd(h2)
        pos = pl.program_id(1) * tm + lax.broadcasted_iota(jnp.int32, (tm, 1), 0)
        t = jnp.concatenate([ta_ref[0, tm - X0:, :], tb_ref[0, :tm - X0, :]], axis=0)
        err = jnp.where(pos >= X0, xh * fg - t, 0.0)
        loss_ref[...] += 0.5 * jnp.sum(jnp.mean(err * err, axis=-1, keepdims=True), axis=0, keepdims=True)
        dy = err * (1.0 / D_MODEL)
        dx, dfg = _rms_bwd(dy, xh, r, fg)
        dh_ref[...] = dx
        dfg_ref[...] += dfg

    tok = lambda c: pl.BlockSpec((tm, D_MODEL), lambda b, i: (b * nb + i, c))
    wspec = pl.BlockSpec((D_MODEL, D_MODEL), lambda b, i: (0, 0))
    return pl.pallas_call(
        body, name="mid_fwd", grid=(bsz, nb),
        in_specs=[tok(0), tok(0), tok(C_GG // D_MODEL), tok(C_GM // D_MODEL), tok(0),
                  pl.BlockSpec((1, tm, D_MODEL), lambda b, i: (b, jnp.maximum(i - 1, 0), 0)),
                  pl.BlockSpec((1, tm, D_MODEL), lambda b, i: (b, jnp.minimum(i, last), 0)),
                  wspec, wspec, wspec, pl.BlockSpec((1, D_MODEL), lambda b, i: (0, 0))],
        out_specs=[tok(0), tok(0), tok(0), pl.BlockSpec((1, LANE), lambda b, i: (0, 0)),
                   pl.BlockSpec((1, D_MODEL), lambda b, i: (0, 0))],
        out_shape=[jax.ShapeDtypeStruct((tp, D_MODEL), BF16), jax.ShapeDtypeStruct((tp, D_MODEL), BF16),
                   jax.ShapeDtypeStruct((tp, D_MODEL), F32), jax.ShapeDtypeStruct((1, LANE), F32),
                   jax.ShapeDtypeStruct((1, D_MODEL), F32)],
        compiler_params=_cp(("arbitrary", "arbitrary"), 48),
    )(ya_in, yb_in, proj, proj, hp, target, target, w_gp, w_mp, w_o, final_g)


def _mid_bwd(dh2, y_a, y_b, proj, ya_in, yb_in, o_b, w_o, w_gp, w_mp, bsz, lp):
    tp = bsz * lp
    tm = MXU_DEPTH if tp % MXU_DEPTH == 0 else _attn_block(lp)
    nsteps = tp // tm
    group = 3 * D_MODEL

    def body(dh_ref, ya_ref, yb_ref, mz_ref, gg_ref, gm_ref, yai_ref, ybi_ref, ob_ref, wo_ref, wgp_ref, wmp_ref,
             dyai_ref, do_ref, dp_ref, dl_ref, dwo_ref, dwgp_ref, dwmp_ref, a_o, a_gp, a_mp):
        @pl.when(pl.program_id(0) == 0)
        def _():
            for r in (a_o, a_gp, a_mp):
                r[...] = jnp.zeros_like(r)

        dh = _bf(dh_ref[...])
        dm = _dot_nt(dh, wo_ref[...])
        y_a, y_b = ya_ref[...].astype(F32), yb_ref[...].astype(F32)
        sg, sm = _sigmoid(gg_ref[...].astype(F32)), _sigmoid(gm_ref[...].astype(F32))
        d_ya, d_yb = _bf(sg * dm), _bf(sm * dm)
        dp_ref[:, D_MODEL:2 * D_MODEL] = _bf(dm * y_a * sg * (1.0 - sg))
        dp_ref[:, 2 * D_MODEL:] = _bf(dm * y_b * sm * (1.0 - sm))
        a_o[...] += _dot_tn(_bf(sg * y_a + sm * y_b), dh)
        a_gp[...] += _dot_tn(yai_ref[...], d_ya)
        a_mp[...] += _dot_tn(ybi_ref[...], d_yb)
        dyai_ref[...] = _bf(_dot_nt(d_ya, wgp_ref[...]))
        dy = _dot_nt(d_yb, wmp_ref[...])
        mz, o = mz_ref[...].astype(F32), ob_ref[...].astype(F32)
        s = _sigmoid(mz)
        do = _bf(dy * (mz * s))
        do_ref[...] = do
        dp_ref[:, :D_MODEL] = _bf(dy * o * (s * (1.0 + mz * (1.0 - s))))
        prod = do.astype(F32) * o
        for h in range(MLA_HEADS):
            dl = jnp.sum(prod[:, h * MLA_DV:(h + 1) * MLA_DV], axis=-1, keepdims=True)
            dl_ref[h] = jnp.broadcast_to(dl, (tm, LANE))

        @pl.when(pl.program_id(0) == nsteps - 1)
        def _():
            pltpu.sync_copy(a_o, dwo_ref)
            pltpu.sync_copy(a_gp, dwgp_ref)
            pltpu.sync_copy(a_mp, dwmp_ref)

    tok = lambda c: pl.BlockSpec((tm, D_MODEL), lambda i: (i, c))
    wspec = pl.BlockSpec((D_MODEL, D_MODEL), lambda i: (0, 0))
    anyspec = pl.BlockSpec(memory_space=pl.ANY)
    wshape = jax.ShapeDtypeStruct((D_MODEL, D_MODEL), F32)
    return pl.pallas_call(
        body, name="mid_bwd", grid=(nsteps,),
        in_specs=[tok(0), tok(0), tok(0), tok(C_MZ // D_MODEL), tok(C_GG // D_MODEL), tok(C_GM // D_MODEL),
                  tok(0), tok(0), tok(0), wspec, wspec, wspec],
        out_specs=[tok(0), tok(0), pl.BlockSpec((tm, group), lambda i: (i, C_MZ // group)),
                   pl.BlockSpec((MLA_HEADS, tm, LANE), lambda i: (0, i, 0)), anyspec, anyspec, anyspec],
        out_shape=[jax.ShapeDtypeStruct((tp, D_MODEL), BF16)] * 2 + [jax.ShapeDtypeStruct((tp, N_EXT), BF16),
                   jax.ShapeDtypeStruct((MLA_HEADS, tp, LANE), F32)] + [wshape] * 3,
        scratch_shapes=[pltpu.VMEM((D_MODEL, D_MODEL), F32)] * 3,
        compiler_params=_cp(("arbitrary",), 56),
    )(dh2, y_a, y_b, proj, proj, proj, ya_in, yb_in, o_b, w_o, w_gp, w_mp)


MESH_ID = pl.DeviceIdType.MESH
EXCHANGE_SEMS = [pltpu.SemaphoreType.DMA((N_DEV - 1,)), pltpu.SemaphoreType.DMA((N_DEV - 1,)), pltpu.SemaphoreType.DMA]


def _my_place():
    return lax.axis_index("x"), lax.axis_index("y"), lax.axis_index("c")


def _exchange(g_ref, recv_ref, send_sems, recv_sems, local_sem, start, same=False):
    x, y, c = _my_place()
    me = 4 * x + 2 * y + c
    own = pltpu.make_async_copy(g_ref if same else g_ref.at[me], recv_ref.at[me], local_sem)
    sends, lands = [], []
    for d in range(1, N_DEV):
        px = 1 - x if d & 4 else x
        py = 1 - y if d & 2 else y
        pc = 1 - c if d & 1 else c
        peer = 4 * px + 2 * py + pc
        for slot, group in ((me, sends),) if start else ((me, sends), (peer, lands)):
            group.append(pltpu.make_async_remote_copy(
                src_ref=g_ref if same else g_ref.at[peer], dst_ref=recv_ref.at[slot], send_sem=send_sems.at[d - 1],
                recv_sem=recv_sems.at[d - 1], device_id=(px, py, pc), device_id_type=MESH_ID))
    if start:
        own.start()
        for cp in sends:
            cp.start()
    else:
        for cp in lands:
            cp.wait_recv()
        for cp in sends:
            cp.wait_send()
        own.wait()


def _dw_in(u, dproj, slabs):
    tp = u.shape[0]
    tm, tn = _big_tok(tp), EXT_BLOCK
    nj, ni = N_EXT // tn, tp // tm

    def body(u_ref, d_ref, g_ref, o_ref, recv_ref, send_sems, recv_sems, local_sem):
        j, i = pl.program_id(0), pl.program_id(1)

        @pl.when(jnp.logical_and(j == 0, i == 0))
        def _():
            _exchange(g_ref, recv_ref, send_sems, recv_sems, local_sem, True)

        @pl.when(i == 0)
        def _():
            o_ref[...] = jnp.zeros_like(o_ref)

        o_ref[...] += _dot_tn(d_ref[...], u_ref[...])

        @pl.when(jnp.logical_and(j == nj - 1, i == ni - 1))
        def _():
            _exchange(g_ref, recv_ref, send_sems, recv_sems, local_sem, False)

    anyspec = pl.BlockSpec(memory_space=pl.ANY)
    return pl.pallas_call(
        body, name="dw_in", grid=(nj, ni),
        in_specs=[pl.BlockSpec((tm, D_MODEL), lambda j, i: (i, 0)), pl.BlockSpec((tm, tn), lambda j, i: (i, j)), anyspec],
        out_specs=[pl.BlockSpec((tn, D_MODEL), lambda j, i: (j, 0)), anyspec],
        out_shape=[jax.ShapeDtypeStruct((N_EXT, D_MODEL), F32), jax.ShapeDtypeStruct(slabs.shape, slabs.dtype)],
        scratch_shapes=EXCHANGE_SEMS,
        compiler_params=_cp(("arbitrary", "arbitrary"), 48),
    )(u, dproj, slabs)


def _dx_in(dproj, w_ext, hp, dh2, norm_g, slabs):
    tp = hp.shape[0]
    tm, tk = _big_tok(tp), EXT_BLOCK
    nk = N_EXT // tk
    ni = tp // tm

    def body(d_ref, w_ref, h_ref, dh_ref, g_ref, s_ref, o_ref, dg_ref, recv_ref, acc, send_sems, recv_sems, local_sem):
        k = pl.program_id(1)

        @pl.when(jnp.logical_and(pl.program_id(0) == 0, k == 0))
        def _():
            _exchange(s_ref, recv_ref, send_sems, recv_sems, local_sem, True)

        @pl.when(jnp.logical_and(pl.program_id(0) == 0, k == 0))
        def _():
            dg_ref[...] = jnp.zeros_like(dg_ref)

        @pl.when(k == 0)
        def _():
            acc[...] = jnp.zeros_like(acc)

        acc[...] += _dot_nt(d_ref[...], w_ref[...])

        @pl.when(k == nk - 1)
        def _():
            g = g_ref[...]
            xh, r = _rms_fwd(h_ref[...])
            dx, dg = _rms_bwd(acc[...], xh, r, g)
            o_ref[...] = dh_ref[...] + dx
            dg_ref[...] += dg

        @pl.when(jnp.logical_and(pl.program_id(0) == ni - 1, k == nk - 1))
        def _():
            _exchange(s_ref, recv_ref, send_sems, recv_sems, local_sem, False)

    tok = pl.BlockSpec((tm, D_MODEL), lambda i, k: (i, 0))
    anyspec = pl.BlockSpec(memory_space=pl.ANY)
    return pl.pallas_call(
        body, name="dx_in", grid=(ni, nk),
        in_specs=[pl.BlockSpec((tm, tk), lambda i, k: (i, k)), pl.BlockSpec((D_MODEL, tk), lambda i, k: (0, k)),
                  tok, tok, pl.BlockSpec((1, D_MODEL), lambda i, k: (0, 0)), anyspec],
        out_specs=[tok, pl.BlockSpec((1, D_MODEL), lambda i, k: (0, 0)), anyspec],
        out_shape=[jax.ShapeDtypeStruct((tp, D_MODEL), F32), jax.ShapeDtypeStruct((1, D_MODEL), F32),
                   jax.ShapeDtypeStruct(slabs.shape, slabs.dtype)],
        scratch_shapes=[pltpu.VMEM((tm, D_MODEL), F32)] + EXCHANGE_SEMS,
        compiler_params=_cp(("arbitrary", "arbitrary"), 56),
    )(dproj, w_ext, hp, dh2, norm_g, slabs)


def _meta_grad(dhp3):
    bsz = dhp3.shape[0]

    def body(d_ref, o_ref):
        @pl.when(pl.program_id(0) == 0)
        def _():
            o_ref[...] = jnp.zeros_like(o_ref)

        o_ref[...] += d_ref[0]

    return pl.pallas_call(
        body, name="meta_grad", grid=(bsz,),
        in_specs=[pl.BlockSpec((1, N_META, D_MODEL), lambda b: (b, FRONT // N_META, 0))],
        out_specs=pl.BlockSpec((N_META, D_MODEL), lambda b: (0, 0)),
        out_shape=jax.ShapeDtypeStruct((N_META, D_MODEL), F32),
        compiler_params=_cp(("arbitrary",)),
    )(dhp3)


W_IN_SHARD = N_IN // N_DEV


def _pad_lanes(a, width=LANE):
    return jnp.pad(a, [(0, 0)] * (a.ndim - 1) + [(0, width - a.shape[-1])])


def _rot_cols(w):
    half = w.shape[-1] // 2
    return jnp.concatenate([-w[..., half:], w[..., :half]], axis=-1)


def _unrot_cols(dw):
    half = dw.shape[-1] // 2
    return jnp.concatenate([dw[..., half:], -dw[..., :half]], axis=-1)


def _w_in_cols(shards, lo, hi):
    parts = []
    for k in range(lo // W_IN_SHARD, (hi - 1) // W_IN_SHARD + 1):
        a, b = max(lo, k * W_IN_SHARD), min(hi, (k + 1) * W_IN_SHARD)
        parts.append(shards[k][:, a - k * W_IN_SHARD:b - k * W_IN_SHARD])
    return parts[0] if len(parts) == 1 else jnp.concatenate(parts, axis=1)


def _w_in_ext(shards):
    c = lambda lo, hi: _w_in_cols(shards, lo, hi)
    kr = c(O_KR, O_MZ)
    return jnp.concatenate([
        c(O_V, O_LR), c(O_Z, O_CQ), c(O_Q, O_K), c(O_K, O_V), c(O_MZ, O_GG), c(O_GG, O_GM), c(O_GM, N_IN),
        c(O_CKV, O_KR), _pad_lanes(kr), _pad_lanes(_rot_cols(kr)), _pad_lanes(c(O_LR, O_Z)), c(O_CQ, O_CKV)], axis=1)


def _w_in_grad_t(dwt):
    g = lambda start, width: dwt[start:start + width]
    half = MLA_ROPE // 2
    krot = g(C_KROT, MLA_ROPE)
    kr = g(C_KR, MLA_ROPE) + jnp.concatenate([krot[half:], -krot[:half]], axis=0)
    return jnp.concatenate([
        g(C_Q, GLA_KW), g(C_K, GLA_KW), g(C_V, GLA_VW), g(C_LR, GLA_RANK), g(C_Z, GLA_VW), g(C_CQ, MLA_QR),
        g(C_CKV, MLA_KVR), kr, g(C_MZ, D_MODEL), g(C_GG, D_MODEL), g(C_GM, D_MODEL)], axis=0)


def _rope_tables(lp):
    inv = 1.0 / (ROPE_BASE ** (jnp.arange(0, MLA_ROPE, 2, dtype=F32) / MLA_ROPE))
    ang = (jnp.arange(lp, dtype=F32) - FRONT)[:, None] * inv[None, :]
    cos, sin = jnp.cos(ang), jnp.sin(ang)
    return _pad_lanes(jnp.concatenate([cos, cos], axis=1)), _pad_lanes(jnp.concatenate([sin, sin], axis=1))


def _local_step(x, loss_target, w):
    bsz, seq, _ = x.shape
    lp = X0 + seq
    tp = bsz * lp
    assert lp % TOK == 0 and lp % GLA_ROWS == 0
    meta = jnp.broadcast_to(w["meta_tokens"][None], (bsz, N_META, D_MODEL))
    hp = jnp.concatenate([jnp.zeros((bsz, FRONT, D_MODEL), F32), meta, x], axis=1).reshape(tp, D_MODEL)
    cos_t, sin_t = _rope_tables(lp)

    w_ext = _w_in_ext(w["w_in"])
    u, proj, packed_all = _proj_in(hp, w["norm_g"], w_ext, w["packed"])
    packed_all, off = packed_all.reshape(N_DEV, -1), 0
    for n, shape, axis in PACKED:
        size = shape[0] * shape[1]
        w[n] = _join8(packed_all[:, off:off + size].reshape((N_DEV,) + shape), axis)
        off += size
    gw_pad = jnp.pad(w["gla_gate_w"], ((0, LANE - GLA_RANK), (0, 0)))
    uq = w["mla_w_uq"].reshape(MLA_QR, MLA_HEADS, MLA_QK)
    rope_w = uq[:, :, MLA_NOPE:]
    hw = MLA_HEADS * LANE
    wn = uq[:, :, :MLA_NOPE].reshape(MLA_QR, hw)
    wr = _pad_lanes(rope_w).reshape(MLA_QR, hw)
    wt = _pad_lanes(_rot_cols(rope_w)).reshape(MLA_QR, hw)
    ukv = w["mla_w_ukv"].reshape(MLA_KVR, MLA_HEADS, MLA_NOPE + MLA_DV)
    wk = ukv[:, :, :MLA_NOPE].reshape(MLA_KVR, hw)
    wv = ukv[:, :, MLA_NOPE:].reshape(MLA_KVR, hw)

    o_raw, ya_in, s_all = _gla_fwd(proj, gw_pad, w["gla_gate_b"], w["gla_norm_g"], bsz, lp)
    qf = _q_up(proj, w["mla_q_norm_g"], wn, wr, wt, cos_t, sin_t, bsz, lp)
    kf, vf = _kv_up(proj, w["mla_kv_norm_g"], wk, wv, cos_t, sin_t, bsz, lp)
    o_b, yb_in, lse = _attn_fwd(qf, kf, vf, proj, bsz, lp)
    y_a, y_b, dh2, loss, d_final_g = _mid_fwd(ya_in, yb_in, proj, hp, loss_target, w["gla_proj"], w["mla_proj"],
                                              w["w_out"], w["final_norm_g"], bsz, lp)
    d_ya, d_o, dproj, delta, d_w_out, d_gla_proj, d_mla_proj = _mid_bwd(
        dh2, y_a, y_b, proj, ya_in, yb_in, o_b, w["w_out"], w["gla_proj"], w["mla_proj"], bsz, lp)
    dproj, d_gate, d_gla_norm = _gla_bwd(proj, gw_pad, w["gla_gate_b"], w["gla_norm_g"], o_raw, s_all, d_ya, dproj,
                                         bsz, lp)
    d_lr, d_gw_pad, d_gate_b = _gate_bwd(d_gate, proj, gw_pad)
    dqf, dkf, dvf = _attn_bwd(qf, kf, vf, d_o, lse, delta, bsz, lp)
    dproj, d_wn, d_wr, d_wt, d_qn = _q_up_bwd(dqf, proj, w["mla_q_norm_g"], wn, wr, wt, cos_t, sin_t, dproj,
                                              bsz, lp)
    dproj, d_wk, d_wv, d_kvn = _kv_up_bwd(dkf, dvf, proj, w["mla_kv_norm_g"], wk, wv, cos_t, sin_t, d_lr, dproj,
                                          bsz, lp)

    d_rope = (d_wr.reshape(MLA_QR, MLA_HEADS, LANE)[:, :, :MLA_ROPE]
              + _unrot_cols(d_wt.reshape(MLA_QR, MLA_HEADS, LANE)[:, :, :MLA_ROPE]))
    d_uq = jnp.concatenate([d_wn.reshape(MLA_QR, MLA_HEADS, LANE), d_rope], axis=-1).reshape(MLA_QR, MLA_HEADS * MLA_QK)
    d_ukv = jnp.concatenate([d_wk.reshape(MLA_KVR, MLA_HEADS, LANE), d_wv.reshape(MLA_KVR, MLA_HEADS, LANE)],
                            axis=-1).reshape(MLA_KVR, MLA_HEADS * (MLA_NOPE + MLA_DV))
    mats = dict(gla_gate_w=d_gw_pad[:GLA_RANK], gla_proj=d_gla_proj, mla_w_uq=d_uq, mla_w_ukv=d_ukv,
                mla_proj=d_mla_proj, w_out=d_w_out)
    packed = _pad_rows(jnp.concatenate([_split8(mats[n], axis).reshape(N_DEV, -1) for n, _, axis in PACKED], axis=1),
                       PACK_ROWS)
    d_w_ext_t, packed_parts = _dw_in(u, dproj, _bf(packed))
    w_in_slabs = _bf(_w_in_grad_t(d_w_ext_t).reshape(N_DEV, W_IN_SHARD, D_MODEL))
    d_hp, d_norm_g, w_in_parts = _dx_in(dproj, w_ext, hp, dh2, w["norm_g"], w_in_slabs)
    d_hp3 = d_hp.reshape(bsz, lp, D_MODEL)
    small = dict(meta_tokens=_meta_grad(d_hp3), norm_g=d_norm_g, gla_gate_b=d_gate_b, gla_norm_g=d_gla_norm,
                 mla_q_norm_g=d_qn, mla_kv_norm_g=d_kvn, final_norm_g=d_final_g)
    return loss, d_hp3[:, X0:, :], w_in_parts, packed_parts, small


PACKED = (("gla_gate_w", (GLA_RANK, GLA_KW // N_DEV), 1),
          ("gla_proj", (D_MODEL // N_DEV, D_MODEL), 0), ("mla_w_uq", (MLA_QR, MLA_HEADS * MLA_QK // N_DEV), 1),
          ("mla_w_ukv", (MLA_KVR, MLA_HEADS * (MLA_NOPE + MLA_DV) // N_DEV), 1),
          ("mla_proj", (D_MODEL // N_DEV, D_MODEL), 0), ("w_out", (D_MODEL // N_DEV, D_MODEL), 0))
REPLICATED = (("norm_g", D_MODEL), ("gla_gate_b", GLA_KW), ("gla_norm_g", GLA_DV), ("mla_q_norm_g", MLA_QR),
              ("mla_kv_norm_g", MLA_KVR), ("final_norm_g", D_MODEL))
PACK_ROWS = 3744
PACK_BLOCK = 1248
SMALL_ROWS = 48
LOSS_ROW = N_META + 25
W_IN_BLOCK = 128


def _all_gather(shards):
    n_arr = len(shards)

    def body(*refs):
        x_refs, out_refs = refs[:n_arr], refs[n_arr:2 * n_arr]
        send_sems, recv_sems, local_sems = refs[2 * n_arr:]
        x, y, c = _my_place()
        me, sibling = (x, y, c), (x, y, 1 - c)
        chips = [(1 - x, y), (x, 1 - y), (1 - x, 1 - y)]

        def copy(a, k, block, to, from_input=False):
            slab = out_refs[a].at[4 * block[0] + 2 * block[1] + block[2]]
            return pltpu.make_async_remote_copy(
                src_ref=x_refs[a] if from_input else slab, dst_ref=slab,
                send_sem=send_sems.at[7 * a + k], recv_sem=recv_sems.at[7 * a + k], device_id=to,
                device_id_type=MESH_ID)

        arrays = range(n_arr)
        mine = [pltpu.make_async_copy(x_refs[a], out_refs[a].at[4 * x + 2 * y + c], local_sems.at[a]) for a in arrays]
        for cp in mine:
            cp.start()
        first = [copy(a, 0, me, sibling, True) for a in arrays]
        first += [copy(a, 1 + j, me, (*chip, c), True) for j, chip in enumerate(chips) for a in arrays]
        for cp in first:
            cp.start()
        passed = []
        for j, chip in enumerate(chips):
            for a in arrays:
                copy(a, 1 + j, (*chip, c), me).wait_recv()
                passed.append(copy(a, 4 + j, (*chip, c), sibling))
                passed[-1].start()
        for a in arrays:
            copy(a, 0, sibling, me).wait_recv()
        for j, chip in enumerate(chips):
            for a in arrays:
                copy(a, 4 + j, (*chip, 1 - c), me).wait_recv()
        for cp in first + passed:
            cp.wait_send()
        for cp in mine:
            cp.wait()

    anyspec = pl.BlockSpec(memory_space=pl.ANY)
    return pl.pallas_call(
        body, name="weights_all_gather",
        out_shape=[jax.ShapeDtypeStruct((N_DEV,) + s.shape, s.dtype) for s in shards],
        in_specs=[anyspec] * n_arr, out_specs=[anyspec] * n_arr,
        scratch_shapes=[pltpu.SemaphoreType.DMA((7 * n_arr,)), pltpu.SemaphoreType.DMA((7 * n_arr,)),
                        pltpu.SemaphoreType.DMA((n_arr,))],
    )(*shards)


def _small_exchange(slabs):
    def body(g_ref, recv_ref, send_sems, recv_sems, local_sem):
        _exchange(g_ref, recv_ref, send_sems, recv_sems, local_sem, True)
        _exchange(g_ref, recv_ref, send_sems, recv_sems, local_sem, False)

    vmem = pl.BlockSpec(memory_space=pltpu.VMEM)
    return pl.pallas_call(
        body, name="small_exchange", out_shape=jax.ShapeDtypeStruct(slabs.shape, slabs.dtype),
        in_specs=[vmem], out_specs=vmem, scratch_shapes=EXCHANGE_SEMS,
    )(slabs)


def _adamw(parts, w, m, v, block_rows, name):
    rows, cols = w.shape

    def body(p_ref, w_ref, m_ref, v_ref, g_out, d_out, m_out, v_out):
        g = p_ref[0].astype(F32)
        for s in range(1, N_DEV):
            g = g + p_ref[s].astype(F32)
        m_new = ADAM_B1 * m_ref[...] + (1.0 - ADAM_B1) * g
        v_new = ADAM_B2 * v_ref[...] + (1.0 - ADAM_B2) * (g * g)
        m_hat = m_new / (1.0 - ADAM_B1 ** ADAM_STEP)
        v_hat = v_new / (1.0 - ADAM_B2 ** ADAM_STEP)
        g_out[...] = g
        d_out[...] = -ADAM_LR * (m_hat / (jnp.sqrt(v_hat) + ADAM_EPS) + ADAM_WD * w_ref[...])
        m_out[...] = m_new
        v_out[...] = v_new

    spec = pl.BlockSpec((block_rows, cols), lambda i: (i, 0))
    return pl.pallas_call(
        body, name=name, grid=(pl.cdiv(rows, block_rows),),
        in_specs=[pl.BlockSpec((N_DEV, block_rows, cols), lambda i: (0, i, 0)), spec, spec, spec],
        out_specs=[spec] * 4, out_shape=[jax.ShapeDtypeStruct((rows, cols), F32)] * 4,
        compiler_params=_cp(("parallel",), 48),
    )(parts, w, m, v)


def _pad_rows(flat, rows):
    pad = rows * LANE - flat.shape[-1]
    flat = jnp.pad(flat, [(0, 0)] * (flat.ndim - 1) + [(0, pad)])
    return flat.reshape(flat.shape[:-1] + (rows, LANE))


def _pack_shards(shards):
    return _pad_rows(jnp.concatenate([shards[n].reshape(-1) for n, _, _ in PACKED]), PACK_ROWS)


def _unpack_shards(packed):
    flat, out, off = packed.reshape(-1), {}, 0
    for n, shape, _ in PACKED:
        size = shape[0] * shape[1]
        out[n] = flat[off:off + size].reshape(shape)
        off += size
    return out


def _split8(full, axis):
    r, c = full.shape
    if axis == 0:
        return full.reshape(N_DEV, r // N_DEV, c)
    return full.reshape(r, N_DEV, c // N_DEV).transpose(1, 0, 2)


def _join8(shards, axis):
    _, r, c = shards.shape
    if axis == 0:
        return shards.reshape(N_DEV * r, c)
    return shards.transpose(1, 0, 2).reshape(r, N_DEV * c)


def _pack_small(meta_shard, vals, loss_row):
    rows = jnp.concatenate([vals[n].reshape(-1, LANE) for n, _ in REPLICATED] + [loss_row], axis=0)
    rows = jnp.pad(rows, ((0, SMALL_ROWS - N_META - rows.shape[0]), (0, 0)))
    return jnp.concatenate([meta_shard, jnp.broadcast_to(rows, meta_shard.shape[:-2] + rows.shape)], axis=-2)


def _unpack_small(packed):
    out, off = {"meta_tokens": packed[:N_META]}, N_META
    for n, size in REPLICATED:
        out[n] = packed[off:off + size // LANE].reshape(1, size)
        off += size // LANE
    return out


def kernel(x, meta_tokens, norm_g, w_in, gla_gate_w, gla_gate_b, gla_norm_g, gla_proj, mla_q_norm_g, mla_w_uq, mla_kv_norm_g, mla_w_ukv, mla_proj, w_out, final_norm_g, loss_target, m_meta_tokens, m_norm_g, m_w_in, m_gla_gate_w, m_gla_gate_b, m_gla_norm_g, m_gla_proj, m_mla_q_norm_g, m_mla_w_uq, m_mla_kv_norm_g, m_mla_w_ukv, m_mla_proj, m_w_out, m_final_norm_g, v_meta_tokens, v_norm_g, v_w_in, v_gla_gate_w, v_gla_gate_b, v_gla_norm_g, v_gla_proj, v_mla_q_norm_g, v_mla_w_uq, v_mla_kv_norm_g, v_mla_w_ukv, v_mla_proj, v_w_out, v_final_norm_g):
    given = dict(meta_tokens=meta_tokens, norm_g=norm_g, w_in=w_in, gla_gate_w=gla_gate_w, gla_gate_b=gla_gate_b,
                 gla_norm_g=gla_norm_g, gla_proj=gla_proj, mla_q_norm_g=mla_q_norm_g, mla_w_uq=mla_w_uq,
                 mla_kv_norm_g=mla_kv_norm_g, mla_w_ukv=mla_w_ukv, mla_proj=mla_proj, w_out=w_out,
                 final_norm_g=final_norm_g)
    mom_m = dict(meta_tokens=m_meta_tokens, norm_g=m_norm_g, w_in=m_w_in, gla_gate_w=m_gla_gate_w,
                 gla_gate_b=m_gla_gate_b, gla_norm_g=m_gla_norm_g, gla_proj=m_gla_proj, mla_q_norm_g=m_mla_q_norm_g,
                 mla_w_uq=m_mla_w_uq, mla_kv_norm_g=m_mla_kv_norm_g, mla_w_ukv=m_mla_w_ukv, mla_proj=m_mla_proj,
                 w_out=m_w_out, final_norm_g=m_final_norm_g)
    mom_v = dict(meta_tokens=v_meta_tokens, norm_g=v_norm_g, w_in=v_w_in, gla_gate_w=v_gla_gate_w,
                 gla_gate_b=v_gla_gate_b, gla_norm_g=v_gla_norm_g, gla_proj=v_gla_proj, mla_q_norm_g=v_mla_q_norm_g,
                 mla_w_uq=v_mla_w_uq, mla_kv_norm_g=v_mla_kv_norm_g, mla_w_ukv=v_mla_w_ukv, mla_proj=v_mla_proj,
                 w_out=v_w_out, final_norm_g=v_final_norm_g)
    shapes = {n: a.shape for n, a in given.items()}
    shard2d = {n: s for n, s, _ in PACKED}
    shard2d["w_in"] = (D_MODEL, W_IN_SHARD)
    shard2d["meta_tokens"] = (N_META, LANE)

    def as2d(tree):
        out = {n: tree[n].reshape(shard2d[n]) for n in shard2d}
        out.update({n: tree[n].reshape(1, size) for n, size in REPLICATED})
        return out

    w_loc, m_loc, v_loc = as2d(given), as2d(mom_m), as2d(mom_v)

    w_in_all, meta_all = _all_gather([w_loc["w_in"].astype(BF16), w_loc["meta_tokens"]])
    flat = jnp.concatenate([w_loc[n].astype(BF16).reshape(-1) for n, _, _ in PACKED])
    full = {"w_in": w_in_all, "meta_tokens": _join8(meta_all, 1), "packed": _pad_rows(flat, PACK_ROWS)}
    for n, _ in REPLICATED:
        full[n] = w_loc[n]

    loss_part, grad_x, w_in_parts, packed_parts, small = _local_step(x, loss_target, full)
    small_all = _small_exchange(_pack_small(_split8(small["meta_tokens"], 1), small,
                                            jnp.broadcast_to(loss_part[:, :1], (1, LANE))))

    w_in_t = [t["w_in"].T for t in (w_loc, m_loc, v_loc)]
    g_w, d_w, m_w, v_w = (o.T for o in _adamw(w_in_parts, *w_in_t, W_IN_BLOCK, "adamw_w_in"))
    g_p, d_p, m_p, v_p = _adamw(packed_parts, _pack_shards(w_loc), _pack_shards(m_loc), _pack_shards(v_loc),
                                PACK_BLOCK, "adamw_packed")
    zero_row = jnp.zeros((1, LANE), F32)
    g_s, d_s, m_s, v_s = _adamw(small_all, *(_pack_small(t["meta_tokens"], t, zero_row) for t in (w_loc, m_loc, v_loc)),
                                SMALL_ROWS, "adamw_small")
    loss = g_s[LOSS_ROW, 0]

    order = ["meta_tokens", "norm_g", "w_in", "gla_gate_w", "gla_gate_b", "gla_norm_g", "gla_proj", "mla_q_norm_g",
             "mla_w_uq", "mla_kv_norm_g", "mla_w_ukv", "mla_proj", "w_out", "final_norm_g"]
    result = [loss, grad_x]
    for w_in_out, packed_sh, packed_sm in ((g_w, g_p, g_s), (d_w, d_p, d_s), (m_w, m_p, m_s), (v_w, v_p, v_s)):
        tree = _unpack_shards(packed_sh)
        tree.update(_unpack_small(packed_sm))
        tree["w_in"] = w_in_out
        result += [tree[n].reshape(shapes[n]) for n in order]
    return tuple(result)
```

```python
import jax
import jax.numpy as jnp
from jax import lax
from jax.experimental import pallas as pl
from jax.experimental.pallas import tpu as pltpu

F32 = jnp.float32
BF16 = jnp.bfloat16

D_MODEL = 1024
N_META = 16
EPS = 1e-6
FRONT = 48
X0 = FRONT + N_META
GLA_HEADS, GLA_DK, GLA_DV, GLA_RANK, GLA_CHUNK = 4, 128, 256, 16, 64
GLA_GATE_NORMALIZER = 16.0
GLA_KW = GLA_HEADS * GLA_DK
GLA_VW = GLA_HEADS * GLA_DV
MLA_HEADS, MLA_NOPE, MLA_ROPE, MLA_DV, MLA_QR, MLA_KVR = 8, 128, 64, 128, 256, 128
MLA_QK = MLA_NOPE + MLA_ROPE
ROPE_BASE = 10000.0
LANE = 128
QKW = 2 * LANE

C_V, C_Z, C_Q, C_K = 0, 1024, 2048, 2560
C_MZ, C_GG, C_GM = 3072, 4096, 5120
C_CKV, C_KR, C_KROT, C_LR = 6144, 6272, 6400, 6528
C_CQ = 6656
N_EXT = 6912
O_Q, O_K, O_V, O_LR, O_Z, O_CQ, O_CKV, O_KR, O_MZ, O_GG, O_GM, N_IN = (
    0, 512, 1024, 2048, 2064, 3088, 3344, 3472, 3536, 4560, 5584, 6608)

ADAM_LR, ADAM_B1, ADAM_B2, ADAM_EPS, ADAM_WD, ADAM_STEP = 0.001, 0.9, 0.999, 1e-08, 0.01, 10

N_DEV = 8
TOK = 192
ATT_BLOCK = 352
EXT_BLOCK = 1152
MXU_DEPTH = 256


def _cp(sems=None, vmem_mb=None):
    kw = {}
    if sems is not None:
        kw["dimension_semantics"] = sems
    if vmem_mb is not None:
        kw["vmem_limit_bytes"] = vmem_mb * 1024 * 1024
    return pltpu.CompilerParams(**kw)


def _dot(a, b):
    return jnp.dot(a, b, preferred_element_type=F32)


def _dot_nt(a, b):
    return lax.dot_general(a, b, (((1,), (1,)), ((), ())), preferred_element_type=F32)


def _dot_tn(a, b):
    return lax.dot_general(a, b, (((0,), (0,)), ((), ())), preferred_element_type=F32)


def _sigmoid(x):
    return 1.0 / (1.0 + jnp.exp(-x))


def _bf(x):
    return x.astype(BF16)


def _big_tok(tp):
    return 4 * TOK if tp % (4 * TOK) == 0 else TOK


def _attn_block(lp):
    return ATT_BLOCK if lp % ATT_BLOCK == 0 else TOK


def _proj_in(hp, norm_g, w_ext, packed):
    tp = hp.shape[0]
    tm, tn = _big_tok(tp), EXT_BLOCK
    ni, nj = tp // tm, N_EXT // tn

    def body(h_ref, g_ref, w_ref, p_ref, u_ref, o_ref, pall_ref, u_scr, send_sems, recv_sems, local_sem):
        i, j = pl.program_id(0), pl.program_id(1)

        @pl.when(jnp.logical_and(i == 0, j == 0))
        def _():
            _exchange(p_ref, pall_ref, send_sems, recv_sems, local_sem, True, same=True)

        @pl.when(j == 0)
        def _():
            x = h_ref[...]
            r = lax.rsqrt(jnp.mean(x * x, axis=-1, keepdims=True) + EPS)
            u = _bf(x * r * g_ref[...])
            u_scr[...] = u
            u_ref[...] = u

        o_ref[...] = _bf(_dot(u_scr[...], w_ref[...]))

        @pl.when(jnp.logical_and(i == ni - 1, j == nj - 1))
        def _():
            _exchange(p_ref, pall_ref, send_sems, recv_sems, local_sem, False, same=True)

    anyspec = pl.BlockSpec(memory_space=pl.ANY)
    return pl.pallas_call(
        body, name="proj_in", grid=(ni, nj),
        in_specs=[pl.BlockSpec((tm, D_MODEL), lambda i, j: (i, 0)),
                  pl.BlockSpec((1, D_MODEL), lambda i, j: (0, 0)),
                  pl.BlockSpec((D_MODEL, tn), lambda i, j: (0, j)), anyspec],
        out_specs=[pl.BlockSpec((tm, D_MODEL), lambda i, j: (i, 0)),
                   pl.BlockSpec((tm, tn), lambda i, j: (i, j)), anyspec],
        out_shape=[jax.ShapeDtypeStruct((tp, D_MODEL), BF16), jax.ShapeDtypeStruct((tp, N_EXT), BF16),
                   jax.ShapeDtypeStruct((N_DEV,) + packed.shape, packed.dtype)],
        scratch_shapes=[pltpu.VMEM((tm, D_MODEL), BF16)] + EXCHANGE_SEMS,
        compiler_params=_cp(("arbitrary", "arbitrary"), 48),
    )(hp, norm_g, w_ext, packed)


GLA_GROUP = 3
GLA_ROWS = GLA_GROUP * GLA_CHUNK


def _tri_dot(tri, x):
    hi = _bf(x)
    rest = x - hi.astype(F32)
    mid = _bf(rest)
    return _dot(tri, hi) + _dot(tri, mid) + _dot(tri, _bf(rest - mid.astype(F32)))


def _gla_gates(q_ref, k_ref, lr_ref, gw_ref, gb_ref, rows, not_first):
    z = _dot(lr_ref[rows, :], gw_ref[...]) + gb_ref[...]
    logsig = jnp.minimum(z, 0.0) - jnp.log(1.0 + jnp.exp(-jnp.abs(z)))
    row = lax.broadcasted_iota(jnp.int32, (GLA_CHUNK, GLA_KW), 0)
    live = jnp.logical_or(not_first, row >= FRONT)
    g = jnp.where(live, logsig * (1.0 / GLA_GATE_NORMALIZER), 0.0)
    ri = lax.broadcasted_iota(jnp.int32, (GLA_CHUNK, GLA_CHUNK), 0)
    ci = lax.broadcasted_iota(jnp.int32, (GLA_CHUNK, GLA_CHUNK), 1)
    tril = ci <= ri
    b = _tri_dot(_bf(tril.astype(F32)), g)
    bl = jnp.sum(jnp.where(row == GLA_CHUNK - 1, b, 0.0), axis=0, keepdims=True)
    eb, enb, elb, ebl = jnp.exp(b), jnp.exp(-b), jnp.exp(bl - b), jnp.exp(bl)
    q = q_ref[rows, :].astype(F32) * (GLA_DK ** -0.5)
    k = k_ref[rows, :].astype(F32)
    qe, ke, kl = q * eb, k * enb, k * elb
    return dict(z=z, live=live, tril=tril, row=row, eb=eb, enb=enb, elb=elb, ebl=ebl, qe=qe, ke=ke, kl=kl,
                qe_b=_bf(qe), ke_b=_bf(ke), kl_b=_bf(kl))


def _gla_in_specs(n_groups, rev):
    def rb(b, n):
        return b * n_groups + ((n_groups - 1 - n) if rev else n)

    return rb, [pl.BlockSpec((GLA_ROWS, GLA_KW), lambda b, n: (rb(b, n), C_Q // GLA_KW)),
                pl.BlockSpec((GLA_ROWS, GLA_KW), lambda b, n: (rb(b, n), C_K // GLA_KW)),
                pl.BlockSpec((GLA_ROWS, GLA_VW), lambda b, n: (rb(b, n), C_V // GLA_VW)),
                pl.BlockSpec((GLA_ROWS, GLA_VW), lambda b, n: (rb(b, n), C_Z // GLA_VW)),
                pl.BlockSpec((GLA_ROWS, LANE), lambda b, n: (rb(b, n), C_LR // LANE)),
                pl.BlockSpec((LANE, GLA_KW), lambda b, n: (0, 0)),
                pl.BlockSpec((1, GLA_KW), lambda b, n: (0, 0)),
                pl.BlockSpec((1, GLA_DV), lambda b, n: (0, 0))]


def _gla_fwd(proj, gw_pad, gate_b, gla_norm_g, bsz, lp):
    n_chunks = lp // GLA_CHUNK
    n_groups = n_chunks // GLA_GROUP
    tp = bsz * lp

    def body(q_ref, k_ref, v_ref, z_ref, lr_ref, gw_ref, gb_ref, gn_ref, oraw_ref, ya_ref, sall_ref, st_scr):
        grp = pl.program_id(1)

        @pl.when(grp == 0)
        def _():
            st_scr[...] = jnp.zeros_like(st_scr)

        chunks = [slice(j * GLA_CHUNK, (j + 1) * GLA_CHUNK) for j in range(GLA_GROUP)]
        cs = [_gla_gates(q_ref, k_ref, lr_ref, gw_ref, gb_ref, rows, True if j else grp > 0)
              for j, rows in enumerate(chunks)]
        gn = gn_ref[...]
        sts = [st_scr[h] for h in range(GLA_HEADS)]
        for j, (rows, c) in enumerate(zip(chunks, cs)):
            for h in range(GLA_HEADS):
                ks, vs = slice(h * GLA_DK, (h + 1) * GLA_DK), slice(h * GLA_DV, (h + 1) * GLA_DV)
                st = sts[h]
                sall_ref[0, j, h] = st
                v = v_ref[rows, vs]
                a = jnp.where(c["tril"], _dot_nt(c["qe_b"][:, ks], c["ke_b"][:, ks]), 0.0)
                o = _dot(_bf(a), v) + _dot_nt(c["qe_b"][:, ks], _bf(st))
                sts[h] = st * c["ebl"][:, ks] + _dot_tn(v, c["kl_b"][:, ks])
                oraw_ref[rows, vs] = o
                r = lax.rsqrt(jnp.mean(o * o, axis=-1, keepdims=True) + EPS)
                zg = z_ref[rows, vs].astype(F32)
                ya_ref[rows, vs] = _bf((o * r * gn) * (zg * _sigmoid(zg)))
        for h in range(GLA_HEADS):
            st_scr[h] = sts[h]

    rb, in_specs = _gla_in_specs(n_groups, False)
    return pl.pallas_call(
        body, name="gla_fwd", grid=(bsz, n_groups), in_specs=in_specs,
        out_specs=[pl.BlockSpec((GLA_ROWS, GLA_VW), lambda b, n: (rb(b, n), 0)),
                   pl.BlockSpec((GLA_ROWS, GLA_VW), lambda b, n: (rb(b, n), 0)),
                   pl.BlockSpec((1, GLA_GROUP, GLA_HEADS, GLA_DV, GLA_DK), lambda b, n: (b, n, 0, 0, 0))],
        out_shape=[jax.ShapeDtypeStruct((tp, GLA_VW), F32), jax.ShapeDtypeStruct((tp, GLA_VW), BF16),
                   jax.ShapeDtypeStruct((bsz, n_chunks, GLA_HEADS, GLA_DV, GLA_DK), F32)],
        scratch_shapes=[pltpu.VMEM((GLA_HEADS, GLA_DV, GLA_DK), F32)],
        compiler_params=_cp(("parallel", "arbitrary")),
    )(proj, proj, proj, proj, proj, gw_pad, gate_b, gla_norm_g)


def _gla_bwd(proj, gw_pad, gate_b, gla_norm_g, o_raw, s_all, d_ya, dproj, bsz, lp):
    n_chunks = lp // GLA_CHUNK
    n_groups = n_chunks // GLA_GROUP
    tp = bsz * lp

    def body(q_ref, k_ref, v_ref, z_ref, lr_ref, gw_ref, gb_ref, gn_ref, o_ref, s_ref, dya_ref, _,
             dp_ref, dz_ref, dgn_ref, dst_scr):
        dv_ref, dzg_ref = dp_ref.at[:, C_V:C_V + GLA_VW], dp_ref.at[:, C_Z:C_Z + GLA_VW]

        @pl.when(jnp.logical_and(pl.program_id(0) == 0, pl.program_id(1) == 0))
        def _():
            dgn_ref[...] = jnp.zeros_like(dgn_ref)

        @pl.when(pl.program_id(1) == 0)
        def _():
            dst_scr[...] = jnp.zeros_like(dst_scr)

        grp = n_groups - 1 - pl.program_id(1)
        chunks = [slice(j * GLA_CHUNK, (j + 1) * GLA_CHUNK) for j in range(GLA_GROUP)]
        cs = [_gla_gates(q_ref, k_ref, lr_ref, gw_ref, gb_ref, rows, True if j else grp > 0)
              for j, rows in enumerate(chunks)]
        gn = gn_ref[...]
        dgn = jnp.zeros((1, GLA_DV), F32)
        dqe_h, dke_h, dkl_h, dbl_h = ([[None] * GLA_HEADS for _ in chunks] for _ in range(4))
        dsts = [dst_scr[h] for h in range(GLA_HEADS)]
        for j in reversed(range(GLA_GROUP)):
            rows, c = chunks[j], cs[j]
            for h in range(GLA_HEADS):
                ks, vs = slice(h * GLA_DK, (h + 1) * GLA_DK), slice(h * GLA_DV, (h + 1) * GLA_DV)
                dst = dsts[h]
                v = v_ref[rows, vs]
                st = s_ref[0, j, h]
                o = o_ref[rows, vs]
                r = lax.rsqrt(jnp.mean(o * o, axis=-1, keepdims=True) + EPS)
                xh = o * r
                zg = z_ref[rows, vs].astype(F32)
                sg = _sigmoid(zg)
                dy = dya_ref[rows, vs].astype(F32)
                dzg_ref[rows, vs] = _bf(dy * (xh * gn) * (sg * (1.0 + zg * (1.0 - sg))))
                t = dy * (zg * sg)
                dgn += jnp.sum(t * xh, axis=0, keepdims=True)
                dxh = t * gn
                do_b = _bf(r * (dxh - xh * jnp.mean(dxh * xh, axis=-1, keepdims=True)))
                qe_b, ke_b, kl_b, dst_b = c["qe_b"][:, ks], c["ke_b"][:, ks], c["kl_b"][:, ks], _bf(dst)
                a = jnp.where(c["tril"], _dot_nt(qe_b, ke_b), 0.0)
                da_b = _bf(jnp.where(c["tril"], _dot_nt(do_b, v), 0.0))
                dqe_h[j][h] = _dot(da_b, ke_b) + _dot(do_b, _bf(st))
                dke_h[j][h] = _dot_tn(da_b, qe_b)
                dkl = _dot(v, dst_b)
                dkl_h[j][h] = dkl
                dv_ref[rows, vs] = _bf(_dot_tn(_bf(a), do_b) + _dot_nt(kl_b, dst_b))
                ddecay = jnp.sum(dst * st, axis=0, keepdims=True)
                dbl_h[j][h] = jnp.sum(dkl * c["kl"][:, ks], axis=0, keepdims=True) + ddecay * c["ebl"][:, ks]
                dsts[h] = dst * c["ebl"][:, ks] + _dot_tn(do_b, qe_b)
        for h in range(GLA_HEADS):
            dst_scr[h] = dsts[h]
        dgn_ref[...] += dgn
        ri = lax.broadcasted_iota(jnp.int32, (GLA_CHUNK, GLA_CHUNK), 0)
        ci = lax.broadcasted_iota(jnp.int32, (GLA_CHUNK, GLA_CHUNK), 1)
        triu = _bf((ci >= ri).astype(F32))
        for j, (rows, c) in enumerate(zip(chunks, cs)):
            dqe, dke, dkl, dbl = (jnp.concatenate(p[j], axis=1) for p in (dqe_h, dke_h, dkl_h, dbl_h))
            db = dqe * c["qe"] - dke * c["ke"] - dkl * c["kl"] + jnp.where(c["row"] == GLA_CHUNK - 1, dbl, 0.0)
            dg = _tri_dot(triu, db)
            dg = jnp.where(c["live"], dg, 0.0)
            dz_ref[rows, :] = dg * (1.0 / GLA_GATE_NORMALIZER) * _sigmoid(-c["z"])
            dp_ref[rows, C_Q:C_Q + GLA_KW] = _bf(dqe * c["eb"] * (GLA_DK ** -0.5))
            dp_ref[rows, C_K:C_K + GLA_KW] = _bf(dke * c["enb"] + dkl * c["elb"])

    rb, in_specs = _gla_in_specs(n_groups, True)
    wide = pl.BlockSpec((GLA_ROWS, GLA_VW), lambda b, n: (rb(b, n), 0))
    group = C_MZ
    return pl.pallas_call(
        body, name="gla_bwd", grid=(bsz, n_groups),
        in_specs=in_specs + [wide, pl.BlockSpec((1, GLA_GROUP, GLA_HEADS, GLA_DV, GLA_DK),
                                                lambda b, n: (b, n_groups - 1 - n, 0, 0, 0)), wide,
                             pl.BlockSpec(memory_space=pl.ANY)],
        out_specs=[pl.BlockSpec((GLA_ROWS, group), lambda b, n: (rb(b, n), 0)),
                   pl.BlockSpec((GLA_ROWS, GLA_KW), lambda b, n: (rb(b, n), 0)),
                   pl.BlockSpec((1, GLA_DV), lambda b, n: (0, 0))],
        out_shape=[jax.ShapeDtypeStruct((tp, N_EXT), BF16), jax.ShapeDtypeStruct((tp, GLA_KW), F32),
                   jax.ShapeDtypeStruct((1, GLA_DV), F32)],
        input_output_aliases={11: 0},
        scratch_shapes=[pltpu.VMEM((GLA_HEADS, GLA_DV, GLA_DK), F32)],
        compiler_params=_cp(("arbitrary", "arbitrary")),
    )(proj, proj, proj, proj, proj, gw_pad, gate_b, gla_norm_g, o_raw, s_all, d_ya, dproj)


def _gate_bwd(dz, proj, gw_pad):
    tp = dz.shape[0]
    tm = _big_tok(tp)

    def body(dz_ref, lr_ref, gw_ref, dlr_ref, dgw_ref, dgb_ref):
        @pl.when(pl.program_id(0) == 0)
        def _():
            dgw_ref[...] = jnp.zeros_like(dgw_ref)
            dgb_ref[...] = jnp.zeros_like(dgb_ref)

        dz = dz_ref[...]
        dz_b = _bf(dz)
        dlr_ref[...] = _bf(_dot_nt(dz_b, gw_ref[...]))
        dgw_ref[...] += _dot_tn(lr_ref[...], dz_b)
        dgb_ref[...] += jnp.sum(dz, axis=0, keepdims=True)

    return pl.pallas_call(
        body, name="gate_bwd", grid=(tp // tm,),
        in_specs=[pl.BlockSpec((tm, GLA_KW), lambda i: (i, 0)),
                  pl.BlockSpec((tm, LANE), lambda i: (i, C_LR // LANE)),
                  pl.BlockSpec((LANE, GLA_KW), lambda i: (0, 0))],
        out_specs=[pl.BlockSpec((tm, LANE), lambda i: (i, 0)),
                   pl.BlockSpec((LANE, GLA_KW), lambda i: (0, 0)),
                   pl.BlockSpec((1, GLA_KW), lambda i: (0, 0))],
        out_shape=[jax.ShapeDtypeStruct((tp, LANE), BF16), jax.ShapeDtypeStruct((LANE, GLA_KW), F32),
                   jax.ShapeDtypeStruct((1, GLA_KW), F32)],
        compiler_params=_cp(("arbitrary",)),
    )(dz, proj, gw_pad)


def _rms_fwd(x):
    r = lax.rsqrt(jnp.mean(x * x, axis=-1, keepdims=True) + EPS)
    return x * r, r


def _rms_bwd(dy, xh, r, g):
    dxh = dy * g
    dx = r * (dxh - xh * jnp.mean(dxh * xh, axis=-1, keepdims=True))
    return dx, jnp.sum(dy * xh, axis=0, keepdims=True)


def _q_up(proj, q_norm_g, wn, wr, wt, cos_t, sin_t, bsz, lp):
    tp = bsz * lp
    tok = _attn_block(lp)
    nb = lp // tok

    def body(cq_ref, g_ref, wn_ref, wr_ref, wt_ref, cos_ref, sin_ref, q_ref):
        xh, _ = _rms_fwd(cq_ref[...].astype(F32))
        cqn = _bf(xh * g_ref[...])
        nope = _dot(cqn, wn_ref[...])
        rope = _dot(cqn, wr_ref[...])
        rot = _dot(cqn, wt_ref[...])
        cos, sin = cos_ref[...], sin_ref[...]
        one = (lax.broadcasted_iota(jnp.int32, (tok, LANE), 1) == BIAS_LANE).astype(F32)
        for h in range(MLA_HEADS):
            sl = slice(h * LANE, (h + 1) * LANE)
            q_ref[:, h * QKW:h * QKW + LANE] = _bf(nope[:, sl])
            q_ref[:, h * QKW + LANE:(h + 1) * QKW] = _bf(rope[:, sl] * cos + rot[:, sl] * sin + one)

    wspec = pl.BlockSpec((MLA_QR, MLA_HEADS * LANE), lambda b, i: (0, 0))
    tspec = pl.BlockSpec((tok, LANE), lambda b, i: (i, 0))
    return pl.pallas_call(
        body, name="mla_q_up", grid=(bsz, nb),
        in_specs=[pl.BlockSpec((tok, MLA_QR), lambda b, i: (b * nb + i, C_CQ // MLA_QR)),
                  pl.BlockSpec((1, MLA_QR), lambda b, i: (0, 0)), wspec, wspec, wspec, tspec, tspec],
        out_specs=pl.BlockSpec((tok, MLA_HEADS * QKW), lambda b, i: (b * nb + i, 0)),
        out_shape=jax.ShapeDtypeStruct((tp, MLA_HEADS * QKW), BF16),
        compiler_params=_cp(("parallel", "parallel")),
    )(proj, q_norm_g, wn, wr, wt, cos_t, sin_t)


def _kv_up(proj, kv_norm_g, wk, wv, cos_t, sin_t, bsz, lp):
    tp = bsz * lp
    tok = _attn_block(lp)
    nb = lp // tok

    def body(ckv_ref, kr_ref, krot_ref, g_ref, wk_ref, wv_ref, cos_ref, sin_ref, k_ref, v_ref):
        xh, _ = _rms_fwd(ckv_ref[...].astype(F32))
        cn = _bf(xh * g_ref[...])
        kn = _dot(cn, wk_ref[...])
        v_ref[...] = _bf(_dot(cn, wv_ref[...]))
        pos = pl.program_id(1) * tok + lax.broadcasted_iota(jnp.int32, (tok, LANE), 0)
        lane = lax.broadcasted_iota(jnp.int32, (tok, LANE), 1)
        bias = jnp.where(jnp.logical_and(lane == BIAS_LANE, pos < FRONT), KEY_BIAS, 0.0)
        kr = _bf(kr_ref[...].astype(F32) * cos_ref[...] + krot_ref[...].astype(F32) * sin_ref[...] + bias)
        for h in range(MLA_HEADS):
            k_ref[:, h * QKW:h * QKW + LANE] = _bf(kn[:, h * LANE:(h + 1) * LANE])
            k_ref[:, h * QKW + LANE:(h + 1) * QKW] = kr

    wspec = pl.BlockSpec((MLA_KVR, MLA_HEADS * LANE), lambda b, i: (0, 0))
    tspec = pl.BlockSpec((tok, LANE), lambda b, i: (i, 0))
    return pl.pallas_call(
        body, name="mla_kv_up", grid=(bsz, nb),
        in_specs=[pl.BlockSpec((tok, LANE), lambda b, i: (b * nb + i, C_CKV // LANE)),
                  pl.BlockSpec((tok, LANE), lambda b, i: (b * nb + i, C_KR // LANE)),
                  pl.BlockSpec((tok, LANE), lambda b, i: (b * nb + i, C_KROT // LANE)),
                  pl.BlockSpec((1, MLA_KVR), lambda b, i: (0, 0)), wspec, wspec, tspec, tspec],
        out_specs=[pl.BlockSpec((tok, MLA_HEADS * QKW), lambda b, i: (b * nb + i, 0)),
                   pl.BlockSpec((tok, MLA_HEADS * LANE), lambda b, i: (b * nb + i, 0))],
        out_shape=[jax.ShapeDtypeStruct((tp, MLA_HEADS * QKW), BF16),
                   jax.ShapeDtypeStruct((tp, MLA_HEADS * LANE), BF16)],
        compiler_params=_cp(("parallel", "parallel")),
    )(proj, proj, proj, kv_norm_g, wk, wv, cos_t, sin_t)


ATT_SCALE = MLA_QK ** -0.5


KEY_BIAS = -1e30
BIAS_LANE = MLA_ROPE
NEG = 2 * KEY_BIAS
LOG2E = 1.4426950408889634
EXP2_SCALE = ATT_SCALE * LOG2E


def _causal_fill(s, r0, fill):
    tq, kmax = s.shape
    a = r0 // LANE * LANE
    mask = (a + lax.broadcasted_iota(jnp.int32, (tq, kmax - a), 1)
            <= r0 + lax.broadcasted_iota(jnp.int32, (tq, kmax - a), 0))
    right = jnp.where(mask, s[:, a:], fill)
    return jnp.concatenate([s[:, :a], right], axis=1) if a else right


def _attn_fwd(qf, kf, vf, proj, bsz, lp):
    tp = bsz * lp
    tq = _attn_block(lp)
    nh = 2

    def body(q_ref, k_ref, v_ref, mz_ref, ob_ref, yb_ref, lse_ref):
        starts = list(range(0, lp, tq))
        for pair in (starts[i:i + 2] for i in range(0, len(starts), 2)):
            work = [(r0, h) for r0 in pair for h in range(nh)]
            ss = [_causal_fill(_dot_nt(q_ref[r0:r0 + tq, h * QKW:(h + 1) * QKW],
                                       k_ref[0:r0 + tq, h * QKW:(h + 1) * QKW]), r0, NEG) for r0, h in work]
            ms = [jnp.max(s, axis=-1, keepdims=True) for s in ss]
            ps = [jnp.exp2((s - m) * EXP2_SCALE) for s, m in zip(ss, ms)]
            ls = [jnp.sum(p, axis=-1, keepdims=True) for p in ps]
            for (r0, h), p, m, l in zip(work, ps, ms, ls):
                rows, cols = slice(r0, r0 + tq), slice(h * MLA_DV, (h + 1) * MLA_DV)
                o = _dot(_bf(p), v_ref[0:r0 + tq, cols]) / l
                ob_ref[rows, cols] = _bf(o)
                mz = mz_ref[rows, cols].astype(F32)
                yb_ref[rows, cols] = _bf(o * (mz * _sigmoid(mz)))
                lse_ref[0, h, rows, :] = jnp.broadcast_to(m * EXP2_SCALE + jnp.log2(l), (tq, LANE))

    head = lambda off: pl.BlockSpec((lp, nh * MLA_DV), lambda b, h: (b, off + h))
    wide = pl.BlockSpec((lp, nh * QKW), lambda b, h: (b, h))
    return pl.pallas_call(
        body, name="mla_attn_fwd", grid=(bsz, MLA_HEADS // nh),
        in_specs=[wide, wide, head(0), head(C_MZ // (nh * MLA_DV))],
        out_specs=[head(0), head(0), pl.BlockSpec((1, nh, lp, LANE), lambda b, h: (b, h, 0, 0))],
        out_shape=[jax.ShapeDtypeStruct((tp, MLA_HEADS * MLA_DV), BF16),
                   jax.ShapeDtypeStruct((tp, MLA_HEADS * MLA_DV), BF16),
                   jax.ShapeDtypeStruct((bsz, MLA_HEADS, lp, LANE), F32)],
        compiler_params=_cp(("parallel", "parallel"), 56),
    )(qf, kf, vf, proj)


def _attn_bwd(qf, kf, vf, d_o, lse, delta, bsz, lp):
    tp = bsz * lp
    tq = _attn_block(lp)

    def body(q_ref, k_ref, v_ref, do_ref, lse_ref, dl_ref, dq_ref, dk_ref, dv_ref, dk_acc, dv_acc):
        dk_acc[...] = jnp.zeros_like(dk_acc)
        dv_acc[...] = jnp.zeros_like(dv_acc)
        for r0 in range(0, lp, tq):
            rows, kmax = slice(r0, r0 + tq), r0 + tq
            q, do = q_ref[rows, :], do_ref[rows, :]
            k, v = k_ref[0:kmax, :], v_ref[0:kmax, :]
            p = jnp.exp2(_dot_nt(q, k) * EXP2_SCALE - lse_ref[0, 0, rows, :][:, :1])
            p = _causal_fill(p, r0, 0.0)
            ds = _bf(p * (_dot_nt(do, v) - dl_ref[0, rows, :][:, :1]))
            dq_ref[rows, :] = _bf(_dot(ds, k) * ATT_SCALE)
            dk_acc[0:kmax, :] += _dot_tn(ds, q)
            dv_acc[0:kmax, :] += _dot_tn(_bf(p), do)
        dk_ref[...] = _bf(dk_acc[...] * ATT_SCALE)
        dv_ref[...] = _bf(dv_acc[...])

    wide = pl.BlockSpec((lp, QKW), lambda b, h: (b, h))
    narrow = pl.BlockSpec((lp, MLA_DV), lambda b, h: (b, h))
    stat = pl.BlockSpec((1, 1, lp, LANE), lambda b, h: (b, h, 0, 0))
    return pl.pallas_call(
        body, name="mla_attn_bwd", grid=(bsz, MLA_HEADS),
        in_specs=[wide, wide, narrow, narrow, stat, pl.BlockSpec((1, lp, LANE), lambda b, h: (h, b, 0))],
        out_specs=[wide, wide, narrow],
        out_shape=[jax.ShapeDtypeStruct((tp, MLA_HEADS * QKW), BF16), jax.ShapeDtypeStruct((tp, MLA_HEADS * QKW), BF16),
                   jax.ShapeDtypeStruct((tp, MLA_HEADS * MLA_DV), BF16)],
        scratch_shapes=[pltpu.VMEM((lp, QKW), F32), pltpu.VMEM((lp, MLA_DV), F32)],
        compiler_params=_cp(("parallel", "parallel"), 56),
    )(qf, kf, vf, d_o, lse, delta)


def _q_up_bwd(dqf, proj, q_norm_g, wn, wr, wt, cos_t, sin_t, dproj, bsz, lp):
    tp = bsz * lp
    tok = _attn_block(lp)
    nb = lp // tok
    hw = MLA_HEADS * LANE

    def body(dq_ref, cq_ref, g_ref, wn_ref, wr_ref, wt_ref, cos_ref, sin_ref, _,
             dcq_ref, dwn_ref, dwr_ref, dwt_ref, dg_ref):
        @pl.when(jnp.logical_and(pl.program_id(0) == 0, pl.program_id(1) == 0))
        def _():
            for r in (dwn_ref, dwr_ref, dwt_ref, dg_ref):
                r[...] = jnp.zeros_like(r)

        g = g_ref[...]
        xh, r = _rms_fwd(cq_ref[...].astype(F32))
        cqn = _bf(xh * g)
        dn = jnp.concatenate([dq_ref[:, h * QKW:h * QKW + LANE] for h in range(MLA_HEADS)], axis=1)
        dr = jnp.concatenate([dq_ref[:, h * QKW + LANE:(h + 1) * QKW] for h in range(MLA_HEADS)], axis=1).astype(F32)
        dr_c = _bf(dr * jnp.tile(cos_ref[...], (1, MLA_HEADS)))
        dr_s = _bf(dr * jnp.tile(sin_ref[...], (1, MLA_HEADS)))
        dcqn = _dot_nt(dn, wn_ref[...]) + _dot_nt(dr_c, wr_ref[...]) + _dot_nt(dr_s, wt_ref[...])
        dwn_ref[...] += _dot_tn(cqn, dn)
        dwr_ref[...] += _dot_tn(cqn, dr_c)
        dwt_ref[...] += _dot_tn(cqn, dr_s)
        dx, dg = _rms_bwd(dcqn, xh, r, g)
        dcq_ref[...] = _bf(dx)
        dg_ref[...] += dg

    aspec = pl.BlockSpec((MLA_QR, hw), lambda b, i: (0, 0))
    tspec = pl.BlockSpec((tok, LANE), lambda b, i: (i, 0))
    return pl.pallas_call(
        body, name="mla_q_up_bwd", grid=(bsz, nb),
        in_specs=[pl.BlockSpec((tok, MLA_HEADS * QKW), lambda b, i: (b * nb + i, 0)),
                  pl.BlockSpec((tok, MLA_QR), lambda b, i: (b * nb + i, C_CQ // MLA_QR)),
                  pl.BlockSpec((1, MLA_QR), lambda b, i: (0, 0)), aspec, aspec, aspec, tspec, tspec,
                  pl.BlockSpec(memory_space=pl.ANY)],
        out_specs=[pl.BlockSpec((tok, MLA_QR), lambda b, i: (b * nb + i, C_CQ // MLA_QR)), aspec, aspec, aspec,
                   pl.BlockSpec((1, MLA_QR), lambda b, i: (0, 0))],
        out_shape=[jax.ShapeDtypeStruct((tp, N_EXT), BF16)] + [jax.ShapeDtypeStruct((MLA_QR, hw), F32)] * 3
        + [jax.ShapeDtypeStruct((1, MLA_QR), F32)],
        input_output_aliases={8: 0},
        compiler_params=_cp(("arbitrary", "arbitrary")),
    )(dqf, proj, q_norm_g, wn, wr, wt, cos_t, sin_t, dproj)


def _kv_up_bwd(dkf, dvf, proj, kv_norm_g, wk, wv, cos_t, sin_t, d_lr, dproj, bsz, lp):
    tp = bsz * lp
    tok = _attn_block(lp)
    nb = lp // tok
    hw = MLA_HEADS * LANE

    def body(dk_ref, dv_ref, ckv_ref, g_ref, wk_ref, wv_ref, cos_ref, sin_ref, dlr_ref, _,
             dp_ref, dwk_ref, dwv_ref, dg_ref):
        dckv_ref, dkr_ref, dkrot_ref = (dp_ref.at[:, j * LANE:(j + 1) * LANE] for j in range(3))
        dp_ref[:, 3 * LANE:] = dlr_ref[...]
        @pl.when(jnp.logical_and(pl.program_id(0) == 0, pl.program_id(1) == 0))
        def _():
            for r in (dwk_ref, dwv_ref, dg_ref):
                r[...] = jnp.zeros_like(r)

        g = g_ref[...]
        xh, r = _rms_fwd(ckv_ref[...].astype(F32))
        cn = _bf(xh * g)
        dv = dv_ref[...]
        dn = jnp.concatenate([dk_ref[:, h * QKW:h * QKW + LANE] for h in range(MLA_HEADS)], axis=1)
        dcn = _dot_nt(dv, wv_ref[...]) + _dot_nt(dn, wk_ref[...])
        dwv_ref[...] += _dot_tn(cn, dv)
        dwk_ref[...] += _dot_tn(cn, dn)
        drope = jnp.zeros((tok, LANE), F32)
        for h in range(MLA_HEADS):
            drope += dk_ref[:, h * QKW + LANE:(h + 1) * QKW].astype(F32)
        dkr_ref[...] = _bf(drope * cos_ref[...])
        dkrot_ref[...] = _bf(drope * sin_ref[...])
        dx, dg = _rms_bwd(dcn, xh, r, g)
        dckv_ref[...] = _bf(dx)
        dg_ref[...] += dg

    aspec = pl.BlockSpec((MLA_KVR, hw), lambda b, i: (0, 0))
    tspec = pl.BlockSpec((tok, LANE), lambda b, i: (i, 0))
    ospec = pl.BlockSpec((tok, LANE), lambda b, i: (b * nb + i, 0))
    return pl.pallas_call(
        body, name="mla_kv_up_bwd", grid=(bsz, nb),
        in_specs=[pl.BlockSpec((tok, MLA_HEADS * QKW), lambda b, i: (b * nb + i, 0)),
                  pl.BlockSpec((tok, hw), lambda b, i: (b * nb + i, 0)),
                  pl.BlockSpec((tok, LANE), lambda b, i: (b * nb + i, C_CKV // LANE)),
                  pl.BlockSpec((1, MLA_KVR), lambda b, i: (0, 0)), aspec, aspec, tspec, tspec, ospec,
                  pl.BlockSpec(memory_space=pl.ANY)],
        out_specs=[pl.BlockSpec((tok, 4 * LANE), lambda b, i: (b * nb + i, C_CKV // (4 * LANE))), aspec, aspec,
                   pl.BlockSpec((1, MLA_KVR), lambda b, i: (0, 0))],
        out_shape=[jax.ShapeDtypeStruct((tp, N_EXT), BF16)] + [jax.ShapeDtypeStruct((MLA_KVR, hw), F32)] * 2
        + [jax.ShapeDtypeStruct((1, MLA_KVR), F32)],
        input_output_aliases={9: 0},
        compiler_params=_cp(("arbitrary", "arbitrary")),
    )(dkf, dvf, proj, kv_norm_g, wk, wv, cos_t, sin_t, d_lr, dproj)


def _mid_fwd(ya_in, yb_in, proj, hp, target, w_gp, w_mp, w_o, final_g, bsz, lp):
    tp = bsz * lp
    tm = _attn_block(lp)
    nb = lp // tm
    last = pl.cdiv(lp - X0, tm) - 1

    def body(ya_ref, yb_ref, gg_ref, gm_ref, h_ref, ta_ref, tb_ref, wgp_ref, wmp_ref, wo_ref, fg_ref,
             ya_out, yb_out, dh_ref, loss_ref, dfg_ref):
        @pl.when(jnp.logical_and(pl.program_id(0) == 0, pl.program_id(1) == 0))
        def _():
            loss_ref[...] = jnp.zeros_like(loss_ref)
            dfg_ref[...] = jnp.zeros_like(dfg_ref)

        y_a = _dot(ya_ref[...], wgp_ref[...])
        y_b = _dot(yb_ref[...], wmp_ref[...])
        ya_out[...] = _bf(y_a)
        yb_out[...] = _bf(y_b)
        merged = _sigmoid(gg_ref[...].astype(F32)) * y_a + _sigmoid(gm_ref[...].astype(F32)) * y_b
        h2 = h_ref[...] + _dot(_bf(merged), wo_ref[...])
        fg = fg_ref[...]
        xh, r = _rms_fwd(h2)
        pos = pl.program_id(1) * tm + lax.broadcasted_iota(jnp.int32, (tm, 1), 0)
        t = jnp.concatenate([ta_ref[0, tm - X0:, :], tb_ref[0, :tm - X0, :]], axis=0)
        err = jnp.where(pos >= X0, xh * fg - t, 0.0)
        loss_ref[...] += 0.5 * jnp.sum(jnp.mean(err * err, axis=-1, keepdims=True), axis=0, keepdims=True)
        dy = err * (1.0 / D_MODEL)
        dx, dfg = _rms_bwd(dy, xh, r, fg)
        dh_ref[...] = dx
        dfg_ref[...] += dfg

    tok = lambda c: pl.BlockSpec((tm, D_MODEL), lambda b, i: (b * nb + i, c))
    wspec = pl.BlockSpec((D_MODEL, D_MODEL), lambda b, i: (0, 0))
    return pl.pallas_call(
        body, name="mid_fwd", grid=(bsz, nb),
        in_specs=[tok(0), tok(0), tok(C_GG // D_MODEL), tok(C_GM // D_MODEL), tok(0),
                  pl.BlockSpec((1, tm, D_MODEL), lambda b, i: (b, jnp.maximum(i - 1, 0), 0)),
                  pl.BlockSpec((1, tm, D_MODEL), lambda b, i: (b, jnp.minimum(i, last), 0)),
                  wspec, wspec, wspec, pl.BlockSpec((1, D_MODEL), lambda b, i: (0, 0))],
        out_specs=[tok(0), tok(0), tok(0), pl.BlockSpec((1, LANE), lambda b, i: (0, 0)),
                   pl.BlockSpec((1, D_MODEL), lambda b, i: (0, 0))],
        out_shape=[jax.ShapeDtypeStruct((tp, D_MODEL), BF16), jax.ShapeDtypeStruct((tp, D_MODEL), BF16),
                   jax.ShapeDtypeStruct((tp, D_MODEL), F32), jax.ShapeDtypeStruct((1, LANE), F32),
                   jax.ShapeDtypeStruct((1, D_MODEL), F32)],
        compiler_params=_cp(("arbitrary", "arbitrary"), 48),
    )(ya_in, yb_in, proj, proj, hp, target, target, w_gp, w_mp, w_o, final_g)


def _mid_bwd(dh2, y_a, y_b, proj, ya_in, yb_in, o_b, w_o, w_gp, w_mp, bsz, lp):
    tp = bsz * lp
    tm = MXU_DEPTH if tp % MXU_DEPTH == 0 else _attn_block(lp)
    nsteps = tp // tm
    group = 3 * D_MODEL

    def body(dh_ref, ya_ref, yb_ref, mz_ref, gg_ref, gm_ref, yai_ref, ybi_ref, ob_ref, wo_ref, wgp_ref, wmp_ref,
             dyai_ref, do_ref, dp_ref, dl_ref, dwo_ref, dwgp_ref, dwmp_ref, a_o, a_gp, a_mp):
        @pl.when(pl.program_id(0) == 0)
        def _():
            for r in (a_o, a_gp, a_mp):
                r[...] = jnp.zeros_like(r)

        dh = _bf(dh_ref[...])
        dm = _dot_nt(dh, wo_ref[...])
        y_a, y_b = ya_ref[...].astype(F32), yb_ref[...].astype(F32)
        sg, sm = _sigmoid(gg_ref[...].astype(F32)), _sigmoid(gm_ref[...].astype(F32))
        d_ya, d_yb = _bf(sg * dm), _bf(sm * dm)
        dp_ref[:, D_MODEL:2 * D_MODEL] = _bf(dm * y_a * sg * (1.0 - sg))
        dp_ref[:, 2 * D_MODEL:] = _bf(dm * y_b * sm * (1.0 - sm))
        a_o[...] += _dot_tn(_bf(sg * y_a + sm * y_b), dh)
        a_gp[...] += _dot_tn(yai_ref[...], d_ya)
        a_mp[...] += _dot_tn(ybi_ref[...], d_yb)
        dyai_ref[...] = _bf(_dot_nt(d_ya, wgp_ref[...]))
        dy = _dot_nt(d_yb, wmp_ref[...])
        mz, o = mz_ref[...].astype(F32), ob_ref[...].astype(F32)
        s = _sigmoid(mz)
        do = _bf(dy * (mz * s))
        do_ref[...] = do
        dp_ref[:, :D_MODEL] = _bf(dy * o * (s * (1.0 + mz * (1.0 - s))))
        prod = do.astype(F32) * o
        for h in range(MLA_HEADS):
            dl = jnp.sum(prod[:, h * MLA_DV:(h + 1) * MLA_DV], axis=-1, keepdims=True)
            dl_ref[h] = jnp.broadcast_to(dl, (tm, LANE))

        @pl.when(pl.program_id(0) == nsteps - 1)
        def _():
            pltpu.sync_copy(a_o, dwo_ref)
            pltpu.sync_copy(a_gp, dwgp_ref)
            pltpu.sync_copy(a_mp, dwmp_ref)

    tok = lambda c: pl.BlockSpec((tm, D_MODEL), lambda i: (i, c))
    wspec = pl.BlockSpec((D_MODEL, D_MODEL), lambda i: (0, 0))
    anyspec = pl.BlockSpec(memory_space=pl.ANY)
    wshape = jax.ShapeDtypeStruct((D_MODEL, D_MODEL), F32)
    return pl.pallas_call(
        body, name="mid_bwd", grid=(nsteps,),
        in_specs=[tok(0), tok(0), tok(0), tok(C_MZ // D_MODEL), tok(C_GG // D_MODEL), tok(C_GM // D_MODEL),
                  tok(0), tok(0), tok(0), wspec, wspec, wspec],
        out_specs=[tok(0), tok(0), pl.BlockSpec((tm, group), lambda i: (i, C_MZ // group)),
                   pl.BlockSpec((MLA_HEADS, tm, LANE), lambda i: (0, i, 0)), anyspec, anyspec, anyspec],
        out_shape=[jax.ShapeDtypeStruct((tp, D_MODEL), BF16)] * 2 + [jax.ShapeDtypeStruct((tp, N_EXT), BF16),
                   jax.ShapeDtypeStruct((MLA_HEADS, tp, LANE), F32)] + [wshape] * 3,
        scratch_shapes=[pltpu.VMEM((D_MODEL, D_MODEL), F32)] * 3,
        compiler_params=_cp(("arbitrary",), 56),
    )(dh2, y_a, y_b, proj, proj, proj, ya_in, yb_in, o_b, w_o, w_gp, w_mp)


MESH_ID = pl.DeviceIdType.MESH
EXCHANGE_SEMS = [pltpu.SemaphoreType.DMA((N_DEV - 1,)), pltpu.SemaphoreType.DMA((N_DEV - 1,)), pltpu.SemaphoreType.DMA]


def _my_place():
    return lax.axis_index("x"), lax.axis_index("y"), lax.axis_index("c")


def _exchange(g_ref, recv_ref, send_sems, recv_sems, local_sem, start, same=False):
    x, y, c = _my_place()
    me = 4 * x + 2 * y + c
    own = pltpu.make_async_copy(g_ref if same else g_ref.at[me], recv_ref.at[me], local_sem)
    sends, lands = [], []
    for d in range(1, N_DEV):
        px = 1 - x if d & 4 else x
        py = 1 - y if d & 2 else y
        pc = 1 - c if d & 1 else c
        peer = 4 * px + 2 * py + pc
        for slot, group in ((me, sends),) if start else ((me, sends), (peer, lands)):
            group.append(pltpu.make_async_remote_copy(
                src_ref=g_ref if same else g_ref.at[peer], dst_ref=recv_ref.at[slot], send_sem=send_sems.at[d - 1],
                recv_sem=recv_sems.at[d - 1], device_id=(px, py, pc), device_id_type=MESH_ID))
    if start:
        own.start()
        for cp in sends:
            cp.start()
    else:
        for cp in lands:
            cp.wait_recv()
        for cp in sends:
            cp.wait_send()
        own.wait()


def _dw_in(u, dproj, slabs):
    tp = u.shape[0]
    tm, tn = _big_tok(tp), EXT_BLOCK
    nj, ni = N_EXT // tn, tp // tm

    def body(u_ref, d_ref, g_ref, o_ref, recv_ref, send_sems, recv_sems, local_sem):
        j, i = pl.program_id(0), pl.program_id(1)

        @pl.when(jnp.logical_and(j == 0, i == 0))
        def _():
            _exchange(g_ref, recv_ref, send_sems, recv_sems, local_sem, True)

        @pl.when(i == 0)
        def _():
            o_ref[...] = jnp.zeros_like(o_ref)

        o_ref[...] += _dot_tn(d_ref[...], u_ref[...])

        @pl.when(jnp.logical_and(j == nj - 1, i == ni - 1))
        def _():
            _exchange(g_ref, recv_ref, send_sems, recv_sems, local_sem, False)

    anyspec = pl.BlockSpec(memory_space=pl.ANY)
    return pl.pallas_call(
        body, name="dw_in", grid=(nj, ni),
        in_specs=[pl.BlockSpec((tm, D_MODEL), lambda j, i: (i, 0)), pl.BlockSpec((tm, tn), lambda j, i: (i, j)), anyspec],
        out_specs=[pl.BlockSpec((tn, D_MODEL), lambda j, i: (j, 0)), anyspec],
        out_shape=[jax.ShapeDtypeStruct((N_EXT, D_MODEL), F32), jax.ShapeDtypeStruct(slabs.shape, slabs.dtype)],
        scratch_shapes=EXCHANGE_SEMS,
        compiler_params=_cp(("arbitrary", "arbitrary"), 48),
    )(u, dproj, slabs)


def _dx_in(dproj, w_ext, hp, dh2, norm_g, slabs):
    tp = hp.shape[0]
    tm, tk = _big_tok(tp), EXT_BLOCK
    nk = N_EXT // tk
    ni = tp // tm

    def body(d_ref, w_ref, h_ref, dh_ref, g_ref, s_ref, o_ref, dg_ref, recv_ref, acc, send_sems, recv_sems, local_sem):
        k = pl.program_id(1)

        @pl.when(jnp.logical_and(pl.program_id(0) == 0, k == 0))
        def _():
            _exchange(s_ref, recv_ref, send_sems, recv_sems, local_sem, True)

        @pl.when(jnp.logical_and(pl.program_id(0) == 0, k == 0))
        def _():
            dg_ref[...] = jnp.zeros_like(dg_ref)

        @pl.when(k == 0)
        def _():
            acc[...] = jnp.zeros_like(acc)

        acc[...] += _dot_nt(d_ref[...], w_ref[...])

        @pl.when(k == nk - 1)
        def _():
            g = g_ref[...]
            xh, r = _rms_fwd(h_ref[...])
            dx, dg = _rms_bwd(acc[...], xh, r, g)
            o_ref[...] = dh_ref[...] + dx
            dg_ref[...] += dg

        @pl.when(jnp.logical_and(pl.program_id(0) == ni - 1, k == nk - 1))
        def _():
            _exchange(s_ref, recv_ref, send_sems, recv_sems, local_sem, False)

    tok = pl.BlockSpec((tm, D_MODEL), lambda i, k: (i, 0))
    anyspec = pl.BlockSpec(memory_space=pl.ANY)
    return pl.pallas_call(
        body, name="dx_in", grid=(ni, nk),
        in_specs=[pl.BlockSpec((tm, tk), lambda i, k: (i, k)), pl.BlockSpec((D_MODEL, tk), lambda i, k: (0, k)),
                  tok, tok, pl.BlockSpec((1, D_MODEL), lambda i, k: (0, 0)), anyspec],
        out_specs=[tok, pl.BlockSpec((1, D_MODEL), lambda i, k: (0, 0)), anyspec],
        out_shape=[jax.ShapeDtypeStruct((tp, D_MODEL), F32), jax.ShapeDtypeStruct((1, D_MODEL), F32),
                   jax.ShapeDtypeStruct(slabs.shape, slabs.dtype)],
        scratch_shapes=[pltpu.VMEM((tm, D_MODEL), F32)] + EXCHANGE_SEMS,
        compiler_params=_cp(("arbitrary", "arbitrary"), 56),
    )(dproj, w_ext, hp, dh2, norm_g, slabs)


def _meta_grad(dhp3):
    bsz = dhp3.shape[0]

    def body(d_ref, o_ref):
        @pl.when(pl.program_id(0) == 0)
        def _():
            o_ref[...] = jnp.zeros_like(o_ref)

        o_ref[...] += d_ref[0]

    return pl.pallas_call(
        body, name="meta_grad", grid=(bsz,),
        in_specs=[pl.BlockSpec((1, N_META, D_MODEL), lambda b: (b, FRONT // N_META, 0))],
        out_specs=pl.BlockSpec((N_META, D_MODEL), lambda b: (0, 0)),
        out_shape=jax.ShapeDtypeStruct((N_META, D_MODEL), F32),
        compiler_params=_cp(("arbitrary",)),
    )(dhp3)


W_IN_SHARD = N_IN // N_DEV


def _pad_lanes(a, width=LANE):
    return jnp.pad(a, [(0, 0)] * (a.ndim - 1) + [(0, width - a.shape[-1])])


def _rot_cols(w):
    half = w.shape[-1] // 2
    return jnp.concatenate([-w[..., half:], w[..., :half]], axis=-1)


def _unrot_cols(dw):
    half = dw.shape[-1] // 2
    return jnp.concatenate([dw[..., half:], -dw[..., :half]], axis=-1)


def _w_in_cols(shards, lo, hi):
    parts = []
    for k in range(lo // W_IN_SHARD, (hi - 1) // W_IN_SHARD + 1):
        a, b = max(lo, k * W_IN_SHARD), min(hi, (k + 1) * W_IN_SHARD)
        parts.append(shards[k][:, a - k * W_IN_SHARD:b - k * W_IN_SHARD])
    return parts[0] if len(parts) == 1 else jnp.concatenate(parts, axis=1)


def _w_in_ext(shards):
    c = lambda lo, hi: _w_in_cols(shards, lo, hi)
    kr = c(O_KR, O_MZ)
    return jnp.concatenate([
        c(O_V, O_LR), c(O_Z, O_CQ), c(O_Q, O_K), c(O_K, O_V), c(O_MZ, O_GG), c(O_GG, O_GM), c(O_GM, N_IN),
        c(O_CKV, O_KR), _pad_lanes(kr), _pad_lanes(_rot_cols(kr)), _pad_lanes(c(O_LR, O_Z)), c(O_CQ, O_CKV)], axis=1)


def _w_in_grad_t(dwt):
    g = lambda start, width: dwt[start:start + width]
    half = MLA_ROPE // 2
    krot = g(C_KROT, MLA_ROPE)
    kr = g(C_KR, MLA_ROPE) + jnp.concatenate([krot[half:], -krot[:half]], axis=0)
    return jnp.concatenate([
        g(C_Q, GLA_KW), g(C_K, GLA_KW), g(C_V, GLA_VW), g(C_LR, GLA_RANK), g(C_Z, GLA_VW), g(C_CQ, MLA_QR),
        g(C_CKV, MLA_KVR), kr, g(C_MZ, D_MODEL), g(C_GG, D_MODEL), g(C_GM, D_MODEL)], axis=0)


def _rope_tables(lp):
    inv = 1.0 / (ROPE_BASE ** (jnp.arange(0, MLA_ROPE, 2, dtype=F32) / MLA_ROPE))
    ang = (jnp.arange(lp, dtype=F32) - FRONT)[:, None] * inv[None, :]
    cos, sin = jnp.cos(ang), jnp.sin(ang)
    return _pad_lanes(jnp.concatenate([cos, cos], axis=1)), _pad_lanes(jnp.concatenate([sin, sin], axis=1))


def _local_step(x, loss_target, w):
    bsz, seq, _ = x.shape
    lp = X0 + seq
    tp = bsz * lp
    assert lp % TOK == 0 and lp % GLA_ROWS == 0
    meta = jnp.broadcast_to(w["meta_tokens"][None], (bsz, N_META, D_MODEL))
    hp = jnp.concatenate([jnp.zeros((bsz, FRONT, D_MODEL), F32), meta, x], axis=1).reshape(tp, D_MODEL)
    cos_t, sin_t = _rope_tables(lp)

    w_ext = _w_in_ext(w["w_in"])
    u, proj, packed_all = _proj_in(hp, w["norm_g"], w_ext, w["packed"])
    packed_all, off = packed_all.reshape(N_DEV, -1), 0
    for n, shape, axis in PACKED:
        size = shape[0] * shape[1]
        w[n] = _join8(packed_all[:, off:off + size].reshape((N_DEV,) + shape), axis)
        off += size
    gw_pad = jnp.pad(w["gla_gate_w"], ((0, LANE - GLA_RANK), (0, 0)))
    uq = w["mla_w_uq"].reshape(MLA_QR, MLA_HEADS, MLA_QK)
    rope_w = uq[:, :, MLA_NOPE:]
    hw = MLA_HEADS * LANE
    wn = uq[:, :, :MLA_NOPE].reshape(MLA_QR, hw)
    wr = _pad_lanes(rope_w).reshape(MLA_QR, hw)
    wt = _pad_lanes(_rot_cols(rope_w)).reshape(MLA_QR, hw)
    ukv = w["mla_w_ukv"].reshape(MLA_KVR, MLA_HEADS, MLA_NOPE + MLA_DV)
    wk = ukv[:, :, :MLA_NOPE].reshape(MLA_KVR, hw)
    wv = ukv[:, :, MLA_NOPE:].reshape(MLA_KVR, hw)

    o_raw, ya_in, s_all = _gla_fwd(proj, gw_pad, w["gla_gate_b"], w["gla_norm_g"], bsz, lp)
    qf = _q_up(proj, w["mla_q_norm_g"], wn, wr, wt, cos_t, sin_t, bsz, lp)
    kf, vf = _kv_up(proj, w["mla_kv_norm_g"], wk, wv, cos_t, sin_t, bsz, lp)
    o_b, yb_in, lse = _attn_fwd(qf, kf, vf, proj, bsz, lp)
    y_a, y_b, dh2, loss, d_final_g = _mid_fwd(ya_in, yb_in, proj, hp, loss_target, w["gla_proj"], w["mla_proj"],
                                              w["w_out"], w["final_norm_g"], bsz, lp)
    d_ya, d_o, dproj, delta, d_w_out, d_gla_proj, d_mla_proj = _mid_bwd(
        dh2, y_a, y_b, proj, ya_in, yb_in, o_b, w["w_out"], w["gla_proj"], w["mla_proj"], bsz, lp)
    dproj, d_gate, d_gla_norm = _gla_bwd(proj, gw_pad, w["gla_gate_b"], w["gla_norm_g"], o_raw, s_all, d_ya, dproj,
                                         bsz, lp)
    d_lr, d_gw_pad, d_gate_b = _gate_bwd(d_gate, proj, gw_pad)
    dqf, dkf, dvf = _attn_bwd(qf, kf, vf, d_o, lse, delta, bsz, lp)
    dproj, d_wn, d_wr, d_wt, d_qn = _q_up_bwd(dqf, proj, w["mla_q_norm_g"], wn, wr, wt, cos_t, sin_t, dproj,
                                              bsz, lp)
    dproj, d_wk, d_wv, d_kvn = _kv_up_bwd(dkf, dvf, proj, w["mla_kv_norm_g"], wk, wv, cos_t, sin_t, d_lr, dproj,
                                          bsz, lp)

    d_rope = (d_wr.reshape(MLA_QR, MLA_HEADS, LANE)[:, :, :MLA_ROPE]
              + _unrot_cols(d_wt.reshape(MLA_QR, MLA_HEADS, LANE)[:, :, :MLA_ROPE]))
    d_uq = jnp.concatenate([d_wn.reshape(MLA_QR, MLA_HEADS, LANE), d_rope], axis=-1).reshape(MLA_QR, MLA_HEADS * MLA_QK)
    d_ukv = jnp.concatenate([d_wk.reshape(MLA_KVR, MLA_HEADS, LANE), d_wv.reshape(MLA_KVR, MLA_HEADS, LANE)],
                            axis=-1).reshape(MLA_KVR, MLA_HEADS * (MLA_NOPE + MLA_DV))
    mats = dict(gla_gate_w=d_gw_pad[:GLA_RANK], gla_proj=d_gla_proj, mla_w_uq=d_uq, mla_w_ukv=d_ukv,
                mla_proj=d_mla_proj, w_out=d_w_out)
    packed = _pad_rows(jnp.concatenate([_split8(mats[n], axis).reshape(N_DEV, -1) for n, _, axis in PACKED], axis=1),
                       PACK_ROWS)
    d_w_ext_t, packed_parts = _dw_in(u, dproj, _bf(packed))
    w_in_slabs = _bf(_w_in_grad_t(d_w_ext_t).reshape(N_DEV, W_IN_SHARD, D_MODEL))
    d_hp, d_norm_g, w_in_parts = _dx_in(dproj, w_ext, hp, dh2, w["norm_g"], w_in_slabs)
    d_hp3 = d_hp.reshape(bsz, lp, D_MODEL)
    small = dict(meta_tokens=_meta_grad(d_hp3), norm_g=d_norm_g, gla_gate_b=d_gate_b, gla_norm_g=d_gla_norm,
                 mla_q_norm_g=d_qn, mla_kv_norm_g=d_kvn, final_norm_g=d_final_g)
    return loss, d_hp3[:, X0:, :], w_in_parts, packed_parts, small


PACKED = (("gla_gate_w", (GLA_RANK, GLA_KW // N_DEV), 1),
          ("gla_proj", (D_MODEL // N_DEV, D_MODEL), 0), ("mla_w_uq", (MLA_QR, MLA_HEADS * MLA_QK // N_DEV), 1),
          ("mla_w_ukv", (MLA_KVR, MLA_HEADS * (MLA_NOPE + MLA_DV) // N_DEV), 1),
          ("mla_proj", (D_MODEL // N_DEV, D_MODEL), 0), ("w_out", (D_MODEL // N_DEV, D_MODEL), 0))
REPLICATED = (("norm_g", D_MODEL), ("gla_gate_b", GLA_KW), ("gla_norm_g", GLA_DV), ("mla_q_norm_g", MLA_QR),
              ("mla_kv_norm_g", MLA_KVR), ("final_norm_g", D_MODEL))
PACK_ROWS = 3744
PACK_BLOCK = 1248
SMALL_ROWS = 48
LOSS_ROW = N_META + 25
W_IN_BLOCK = 128


def _all_gather(shards):
    n_arr = len(shards)

    def body(*refs):
        x_refs, out_refs = refs[:n_arr], refs[n_arr:2 * n_arr]
        send_sems, recv_sems, local_sems = refs[2 * n_arr:]
        x, y, c = _my_place()
        me, sibling = (x, y, c), (x, y, 1 - c)
        chips = [(1 - x, y), (x, 1 - y), (1 - x, 1 - y)]

        def copy(a, k, block, to, from_input=False):
            slab = out_refs[a].at[4 * block[0] + 2 * block[1] + block[2]]
            return pltpu.make_async_remote_copy(
                src_ref=x_refs[a] if from_input else slab, dst_ref=slab,
                send_sem=send_sems.at[7 * a + k], recv_sem=recv_sems.at[7 * a + k], device_id=to,
                device_id_type=MESH_ID)

        arrays = range(n_arr)
        mine = [pltpu.make_async_copy(x_refs[a], out_refs[a].at[4 * x + 2 * y + c], local_sems.at[a]) for a in arrays]
        for cp in mine:
            cp.start()
        first = [copy(a, 0, me, sibling, True) for a in arrays]
        first += [copy(a, 1 + j, me, (*chip, c), True) for j, chip in enumerate(chips) for a in arrays]
        for cp in first:
            cp.start()
        passed = []
        for j, chip in enumerate(chips):
            for a in arrays:
                copy(a, 1 + j, (*chip, c), me).wait_recv()
                passed.append(copy(a, 4 + j, (*chip, c), sibling))
                passed[-1].start()
        for a in arrays:
            copy(a, 0, sibling, me).wait_recv()
        for j, chip in enumerate(chips):
            for a in arrays:
                copy(a, 4 + j, (*chip, 1 - c), me).wait_recv()
        for cp in first + passed:
            cp.wait_send()
        for cp in mine:
            cp.wait()

    anyspec = pl.BlockSpec(memory_space=pl.ANY)
    return pl.pallas_call(
        body, name="weights_all_gather",
        out_shape=[jax.ShapeDtypeStruct((N_DEV,) + s.shape, s.dtype) for s in shards],
        in_specs=[anyspec] * n_arr, out_specs=[anyspec] * n_arr,
        scratch_shapes=[pltpu.SemaphoreType.DMA((7 * n_arr,)), pltpu.SemaphoreType.DMA((7 * n_arr,)),
                        pltpu.SemaphoreType.DMA((n_arr,))],
    )(*shards)


def _small_exchange(slabs):
    def body(g_ref, recv_ref, send_sems, recv_sems, local_sem):
        _exchange(g_ref, recv_ref, send_sems, recv_sems, local_sem, True)
        _exchange(g_ref, recv_ref, send_sems, recv_sems, local_sem, False)

    vmem = pl.BlockSpec(memory_space=pltpu.VMEM)
    return pl.pallas_call(
        body, name="small_exchange", out_shape=jax.ShapeDtypeStruct(slabs.shape, slabs.dtype),
        in_specs=[vmem], out_specs=vmem, scratch_shapes=EXCHANGE_SEMS,
    )(slabs)


def _adamw(parts, w, m, v, block_rows, name):
    rows, cols = w.shape

    def body(p_ref, w_ref, m_ref, v_ref, g_out, d_out, m_out, v_out):
        g = p_ref[0].astype(F32)
        for s in range(1, N_DEV):
            g = g + p_ref[s].astype(F32)
        m_new = ADAM_B1 * m_ref[...] + (1.0 - ADAM_B1) * g
        v_new = ADAM_B2 * v_ref[...] + (1.0 - ADAM_B2) * (g * g)
        m_hat = m_new / (1.0 - ADAM_B1 ** ADAM_STEP)
        v_hat = v_new / (1.0 - ADAM_B2 ** ADAM_STEP)
        g_out[...] = g
        d_out[...] = -ADAM_LR * (m_hat / (jnp.sqrt(v_hat) + ADAM_EPS) + ADAM_WD * w_ref[...])
        m_out[...] = m_new
        v_out[...] = v_new

    spec = pl.BlockSpec((block_rows, cols), lambda i: (i, 0))
    return pl.pallas_call(
        body, name=name, grid=(pl.cdiv(rows, block_rows),),
        in_specs=[pl.BlockSpec((N_DEV, block_rows, cols), lambda i: (0, i, 0)), spec, spec, spec],
        out_specs=[spec] * 4, out_shape=[jax.ShapeDtypeStruct((rows, cols), F32)] * 4,
        compiler_params=_cp(("parallel",), 48),
    )(parts, w, m, v)


def _pad_rows(flat, rows):
    pad = rows * LANE - flat.shape[-1]
    flat = jnp.pad(flat, [(0, 0)] * (flat.ndim - 1) + [(0, pad)])
    return flat.reshape(flat.shape[:-1] + (rows, LANE))


def _pack_shards(shards):
    return _pad_rows(jnp.concatenate([shards[n].reshape(-1) for n, _, _ in PACKED]), PACK_ROWS)


def _unpack_shards(packed):
    flat, out, off = packed.reshape(-1), {}, 0
    for n, shape, _ in PACKED:
        size = shape[0] * shape[1]
        out[n] = flat[off:off + size].reshape(shape)
        off += size
    return out


def _split8(full, axis):
    r, c = full.shape
    if axis == 0:
        return full.reshape(N_DEV, r // N_DEV, c)
    return full.reshape(r, N_DEV, c // N_DEV).transpose(1, 0, 2)


def _join8(shards, axis):
    _, r, c = shards.shape
    if axis == 0:
        return shards.reshape(N_DEV * r, c)
    return shards.transpose(1, 0, 2).reshape(r, N_DEV * c)


def _pack_small(meta_shard, vals, loss_row):
    rows = jnp.concatenate([vals[n].reshape(-1, LANE) for n, _ in REPLICATED] + [loss_row], axis=0)
    rows = jnp.pad(rows, ((0, SMALL_ROWS - N_META - rows.shape[0]), (0, 0)))
    return jnp.concatenate([meta_shard, jnp.broadcast_to(rows, meta_shard.shape[:-2] + rows.shape)], axis=-2)


def _unpack_small(packed):
    out, off = {"meta_tokens": packed[:N_META]}, N_META
    for n, size in REPLICATED:
        out[n] = packed[off:off + size // LANE].reshape(1, size)
        off += size // LANE
    return out


def kernel(x, meta_tokens, norm_g, w_in, gla_gate_w, gla_gate_b, gla_norm_g, gla_proj, mla_q_norm_g, mla_w_uq, mla_kv_norm_g, mla_w_ukv, mla_proj, w_out, final_norm_g, loss_target, m_meta_tokens, m_norm_g, m_w_in, m_gla_gate_w, m_gla_gate_b, m_gla_norm_g, m_gla_proj, m_mla_q_norm_g, m_mla_w_uq, m_mla_kv_norm_g, m_mla_w_ukv, m_mla_proj, m_w_out, m_final_norm_g, v_meta_tokens, v_norm_g, v_w_in, v_gla_gate_w, v_gla_gate_b, v_gla_norm_g, v_gla_proj, v_mla_q_norm_g, v_mla_w_uq, v_mla_kv_norm_g, v_mla_w_ukv, v_mla_proj, v_w_out, v_final_norm_g):
    given = dict(meta_tokens=meta_tokens, norm_g=norm_g, w_in=w_in, gla_gate_w=gla_gate_w, gla_gate_b=gla_gate_b,
                 gla_norm_g=gla_norm_g, gla_proj=gla_proj, mla_q_norm_g=mla_q_norm_g, mla_w_uq=mla_w_uq,
                 mla_kv_norm_g=mla_kv_norm_g, mla_w_ukv=mla_w_ukv, mla_proj=mla_proj, w_out=w_out,
                 final_norm_g=final_norm_g)
    mom_m = dict(meta_tokens=m_meta_tokens, norm_g=m_norm_g, w_in=m_w_in, gla_gate_w=m_gla_gate_w,
                 gla_gate_b=m_gla_gate_b, gla_norm_g=m_gla_norm_g, gla_proj=m_gla_proj, mla_q_norm_g=m_mla_q_norm_g,
                 mla_w_uq=m_mla_w_uq, mla_kv_norm_g=m_mla_kv_norm_g, mla_w_ukv=m_mla_w_ukv, mla_proj=m_mla_proj,
                 w_out=m_w_out, final_norm_g=m_final_norm_g)
    mom_v = dict(meta_tokens=v_meta_tokens, norm_g=v_norm_g, w_in=v_w_in, gla_gate_w=v_gla_gate_w,
                 gla_gate_b=v_gla_gate_b, gla_norm_g=v_gla_norm_g, gla_proj=v_gla_proj, mla_q_norm_g=v_mla_q_norm_g,
                 mla_w_uq=v_mla_w_uq, mla_kv_norm_g=v_mla_kv_norm_g, mla_w_ukv=v_mla_w_ukv, mla_proj=v_mla_proj,
                 w_out=v_w_out, final_norm_g=v_final_norm_g)
    shapes = {n: a.shape for n, a in given.items()}
    shard2d = {n: s for n, s, _ in PACKED}
    shard2d["w_in"] = (D_MODEL, W_IN_SHARD)
    shard2d["meta_tokens"] = (N_META, LANE)

    def as2d(tree):
        out = {n: tree[n].reshape(shard2d[n]) for n in shard2d}
        out.update({n: tree[n].reshape(1, size) for n, size in REPLICATED})
        return out

    w_loc, m_loc, v_loc = as2d(given), as2d(mom_m), as2d(mom_v)

    w_in_all, meta_all = _all_gather([w_loc["w_in"].astype(BF16), w_loc["meta_tokens"]])
    flat = jnp.concatenate([w_loc[n].astype(BF16).reshape(-1) for n, _, _ in PACKED])
    full = {"w_in": w_in_all, "meta_tokens": _join8(meta_all, 1), "packed": _pad_rows(flat, PACK_ROWS)}
    for n, _ in REPLICATED:
        full[n] = w_loc[n]

    loss_part, grad_x, w_in_parts, packed_parts, small = _local_step(x, loss_target, full)
    small_all = _small_exchange(_pack_small(_split8(small["meta_tokens"], 1), small,
                                            jnp.broadcast_to(loss_part[:, :1], (1, LANE))))

    w_in_t = [t["w_in"].T for t in (w_loc, m_loc, v_loc)]
    g_w, d_w, m_w, v_w = (o.T for o in _adamw(w_in_parts, *w_in_t, W_IN_BLOCK, "adamw_w_in"))
    g_p, d_p, m_p, v_p = _adamw(packed_parts, _pack_shards(w_loc), _pack_shards(m_loc), _pack_shards(v_loc),
                                PACK_BLOCK, "adamw_packed")
    zero_row = jnp.zeros((1, LANE), F32)
    g_s, d_s, m_s, v_s = _adamw(small_all, *(_pack_small(t["meta_tokens"], t, zero_row) for t in (w_loc, m_loc, v_loc)),
                                SMALL_ROWS, "adamw_small")
    loss = g_s[LOSS_ROW, 0]

    order = ["meta_tokens", "norm_g", "w_in", "gla_gate_w", "gla_gate_b", "gla_norm_g", "gla_proj", "mla_q_norm_g",
             "mla_w_uq", "mla_kv_norm_g", "mla_w_ukv", "mla_proj", "w_out", "final_norm_g"]
    result = [loss, grad_x]
    for w_in_out, packed_sh, packed_sm in ((g_w, g_p, g_s), (d_w, d_p, d_s), (m_w, m_p, m_s), (v_w, v_p, v_s)):
        tree = _unpack_shards(packed_sh)
        tree.update(_unpack_small(packed_sm))
        tree["w_in"] = w_in_out
        result += [tree[n].reshape(shapes[n]) for n in order]
    return tuple(result)
```

```python
import jax
import jax.numpy as jnp
from jax import lax
from jax.experimental import pallas as pl
from jax.experimental.pallas import tpu as pltpu

F32 = jnp.float32
BF16 = jnp.bfloat16

D_MODEL = 1024
N_META = 16
EPS = 1e-6
FRONT = 48
X0 = FRONT + N_META
GLA_HEADS, GLA_DK, GLA_DV, GLA_RANK, GLA_CHUNK = 4, 128, 256, 16, 64
GLA_GATE_NORMALIZER = 16.0
GLA_KW = GLA_HEADS * GLA_DK
GLA_VW = GLA_HEADS * GLA_DV
MLA_HEADS, MLA_NOPE, MLA_ROPE, MLA_DV, MLA_QR, MLA_KVR = 8, 128, 64, 128, 256, 128
MLA_QK = MLA_NOPE + MLA_ROPE
ROPE_BASE = 10000.0
LANE = 128
QKW = 2 * LANE

C_V, C_Z, C_Q, C_K = 0, 1024, 2048, 2560
C_MZ, C_GG, C_GM = 3072, 4096, 5120
C_CKV, C_KR, C_KROT, C_LR = 6144, 6272, 6400, 6528
C_CQ = 6656
N_EXT = 6912
O_Q, O_K, O_V, O_LR, O_Z, O_CQ, O_CKV, O_KR, O_MZ, O_GG, O_GM, N_IN = (
    0, 512, 1024, 2048, 2064, 3088, 3344, 3472, 3536, 4560, 5584, 6608)

ADAM_LR, ADAM_B1, ADAM_B2, ADAM_EPS, ADAM_WD, ADAM_STEP = 0.001, 0.9, 0.999, 1e-08, 0.01, 10

N_DEV = 8
TOK = 192
ATT_BLOCK = 352
EXT_BLOCK = 1152
MXU_DEPTH = 256


def _cp(sems=None, vmem_mb=None):
    kw = {}
    if sems is not None:
        kw["dimension_semantics"] = sems
    if vmem_mb is not None:
        kw["vmem_limit_bytes"] = vmem_mb * 1024 * 1024
    return pltpu.CompilerParams(**kw)


def _dot(a, b):
    return jnp.dot(a, b, preferred_element_type=F32)


def _dot_nt(a, b):
    return lax.dot_general(a, b, (((1,), (1,)), ((), ())), preferred_element_type=F32)


def _dot_tn(a, b):
    return lax.dot_general(a, b, (((0,), (0,)), ((), ())), preferred_element_type=F32)


def _sigmoid(x):
    return 1.0 / (1.0 + jnp.exp(-x))


def _bf(x):
    return x.astype(BF16)


def _big_tok(tp):
    return 4 * TOK if tp % (4 * TOK) == 0 else TOK


def _attn_block(lp):
    return ATT_BLOCK if lp % ATT_BLOCK == 0 else TOK


def _proj_in(hp, norm_g, w_ext, packed):
    tp = hp.shape[0]
    tm = 2 * TOK
    ni = tp // tm

    def body(h_ref, g_ref, w_ref, p_ref, u_ref, o_ref, pall_ref, send_sems, recv_sems, local_sem):
        i = pl.program_id(0)

        @pl.when(i == 0)
        def _():
            _exchange(p_ref, pall_ref, send_sems, recv_sems, local_sem, True, same=True)

        x = h_ref[...]
        r = lax.rsqrt(jnp.mean(x * x, axis=-1, keepdims=True) + EPS)
        u = _bf(x * r * g_ref[...])
        u_ref[...] = u
        o_ref[...] = _bf(_dot(u, w_ref[...]))

        @pl.when(i == ni - 1)
        def _():
            _exchange(p_ref, pall_ref, send_sems, recv_sems, local_sem, False, same=True)

    anyspec = pl.BlockSpec(memory_space=pl.ANY)
    return pl.pallas_call(
        body, name="proj_in", grid=(ni,),
        in_specs=[pl.BlockSpec((tm, D_MODEL), lambda i: (i, 0)),
                  pl.BlockSpec((1, D_MODEL), lambda i: (0, 0)),
                  pl.BlockSpec((D_MODEL, N_EXT), lambda i: (0, 0), pipeline_mode=pl.Buffered(1)), anyspec],
        out_specs=[pl.BlockSpec((tm, D_MODEL), lambda i: (i, 0)),
                   pl.BlockSpec((tm, N_EXT), lambda i: (i, 0)), anyspec],
        out_shape=[jax.ShapeDtypeStruct((tp, D_MODEL), BF16), jax.ShapeDtypeStruct((tp, N_EXT), BF16),
                   jax.ShapeDtypeStruct((N_DEV,) + packed.shape, packed.dtype)],
        scratch_shapes=EXCHANGE_SEMS,
        compiler_params=_cp(("arbitrary",), 56),
    )(hp, norm_g, w_ext, packed)


GLA_GROUP = 3
GLA_ROWS = GLA_GROUP * GLA_CHUNK


def _tri_dot(tri, x):
    hi = _bf(x)
    rest = x - hi.astype(F32)
    mid = _bf(rest)
    return _dot(tri, hi) + _dot(tri, mid) + _dot(tri, _bf(rest - mid.astype(F32)))


def _gla_gates(q_ref, k_ref, lr_ref, gw_ref, gb_ref, rows, not_first):
    z = _dot(lr_ref[rows, :], gw_ref[...]) + gb_ref[...]
    logsig = jnp.minimum(z, 0.0) - jnp.log(1.0 + jnp.exp(-jnp.abs(z)))
    row = lax.broadcasted_iota(jnp.int32, (GLA_CHUNK, GLA_KW), 0)
    live = jnp.logical_or(not_first, row >= FRONT)
    g = jnp.where(live, logsig * (1.0 / GLA_GATE_NORMALIZER), 0.0)
    ri = lax.broadcasted_iota(jnp.int32, (GLA_CHUNK, GLA_CHUNK), 0)
    ci = lax.broadcasted_iota(jnp.int32, (GLA_CHUNK, GLA_CHUNK), 1)
    tril = ci <= ri
    b = _tri_dot(_bf(tril.astype(F32)), g)
    bl = jnp.sum(jnp.where(row == GLA_CHUNK - 1, b, 0.0), axis=0, keepdims=True)
    eb, enb, elb, ebl = jnp.exp(b), jnp.exp(-b), jnp.exp(bl - b), jnp.exp(bl)
    q = q_ref[rows, :].astype(F32) * (GLA_DK ** -0.5)
    k = k_ref[rows, :].astype(F32)
    qe, ke, kl = q * eb, k * enb, k * elb
    return dict(z=z, live=live, tril=tril, row=row, eb=eb, enb=enb, elb=elb, ebl=ebl, qe=qe, ke=ke, kl=kl,
                qe_b=_bf(qe), ke_b=_bf(ke), kl_b=_bf(kl))


def _gla_in_specs(n_groups, rev):
    def rb(b, n):
        return b * n_groups + ((n_groups - 1 - n) if rev else n)

    return rb, [pl.BlockSpec((GLA_ROWS, GLA_KW), lambda b, n: (rb(b, n), C_Q // GLA_KW)),
                pl.BlockSpec((GLA_ROWS, GLA_KW), lambda b, n: (rb(b, n), C_K // GLA_KW)),
                pl.BlockSpec((GLA_ROWS, GLA_VW), lambda b, n: (rb(b, n), C_V // GLA_VW)),
                pl.BlockSpec((GLA_ROWS, GLA_VW), lambda b, n: (rb(b, n), C_Z // GLA_VW)),
                pl.BlockSpec((GLA_ROWS, LANE), lambda b, n: (rb(b, n), C_LR // LANE)),
                pl.BlockSpec((LANE, GLA_KW), lambda b, n: (0, 0)),
                pl.BlockSpec((1, GLA_KW), lambda b, n: (0, 0)),
                pl.BlockSpec((1, GLA_DV), lambda b, n: (0, 0))]


def _gla_fwd(proj, gw_pad, gate_b, gla_norm_g, bsz, lp):
    n_chunks = lp // GLA_CHUNK
    n_groups = n_chunks // GLA_GROUP
    tp = bsz * lp

    def body(q_ref, k_ref, v_ref, z_ref, lr_ref, gw_ref, gb_ref, gn_ref, oraw_ref, ya_ref, sall_ref, st_scr):
        grp = pl.program_id(1)

        @pl.when(grp == 0)
        def _():
            st_scr[...] = jnp.zeros_like(st_scr)

        chunks = [slice(j * GLA_CHUNK, (j + 1) * GLA_CHUNK) for j in range(GLA_GROUP)]
        cs = [_gla_gates(q_ref, k_ref, lr_ref, gw_ref, gb_ref, rows, True if j else grp > 0)
              for j, rows in enumerate(chunks)]
        gn = gn_ref[...]
        sts = [st_scr[h] for h in range(GLA_HEADS)]
        for j, (rows, c) in enumerate(zip(chunks, cs)):
            for h in range(GLA_HEADS):
                ks, vs = slice(h * GLA_DK, (h + 1) * GLA_DK), slice(h * GLA_DV, (h + 1) * GLA_DV)
                st = sts[h]
                sall_ref[0, j, h] = st
                v = v_ref[rows, vs]
                a = jnp.where(c["tril"], _dot_nt(c["qe_b"][:, ks], c["ke_b"][:, ks]), 0.0)
                o = _dot(_bf(a), v) + _dot_nt(c["qe_b"][:, ks], _bf(st))
                sts[h] = st * c["ebl"][:, ks] + _dot_tn(v, c["kl_b"][:, ks])
                oraw_ref[rows, vs] = o
                r = lax.rsqrt(jnp.mean(o * o, axis=-1, keepdims=True) + EPS)
                zg = z_ref[rows, vs].astype(F32)
                ya_ref[rows, vs] = _bf((o * r * gn) * (zg * _sigmoid(zg)))
        for h in range(GLA_HEADS):
            st_scr[h] = sts[h]

    rb, in_specs = _gla_in_specs(n_groups, False)
    return pl.pallas_call(
        body, name="gla_fwd", grid=(bsz, n_groups), in_specs=in_specs,
        out_specs=[pl.BlockSpec((GLA_ROWS, GLA_VW), lambda b, n: (rb(b, n), 0)),
                   pl.BlockSpec((GLA_ROWS, GLA_VW), lambda b, n: (rb(b, n), 0)),
                   pl.BlockSpec((1, GLA_GROUP, GLA_HEADS, GLA_DV, GLA_DK), lambda b, n: (b, n, 0, 0, 0))],
        out_shape=[jax.ShapeDtypeStruct((tp, GLA_VW), F32), jax.ShapeDtypeStruct((tp, GLA_VW), BF16),
                   jax.ShapeDtypeStruct((bsz, n_chunks, GLA_HEADS, GLA_DV, GLA_DK), F32)],
        scratch_shapes=[pltpu.VMEM((GLA_HEADS, GLA_DV, GLA_DK), F32)],
        compiler_params=_cp(("parallel", "arbitrary")),
    )(proj, proj, proj, proj, proj, gw_pad, gate_b, gla_norm_g)


def _gla_bwd(proj, gw_pad, gate_b, gla_norm_g, o_raw, s_all, d_ya, dproj, bsz, lp):
    n_chunks = lp // GLA_CHUNK
    n_groups = n_chunks // GLA_GROUP
    tp = bsz * lp

    def body(q_ref, k_ref, v_ref, z_ref, lr_ref, gw_ref, gb_ref, gn_ref, o_ref, s_ref, dya_ref, _,
             dp_ref, dz_ref, dgn_ref, dst_scr):
        dv_ref, dzg_ref = dp_ref.at[:, C_V:C_V + GLA_VW], dp_ref.at[:, C_Z:C_Z + GLA_VW]

        @pl.when(jnp.logical_and(pl.program_id(0) == 0, pl.program_id(1) == 0))
        def _():
            dgn_ref[...] = jnp.zeros_like(dgn_ref)

        @pl.when(pl.program_id(1) == 0)
        def _():
            dst_scr[...] = jnp.zeros_like(dst_scr)

        grp = n_groups - 1 - pl.program_id(1)
        chunks = [slice(j * GLA_CHUNK, (j + 1) * GLA_CHUNK) for j in range(GLA_GROUP)]
        cs = [_gla_gates(q_ref, k_ref, lr_ref, gw_ref, gb_ref, rows, True if j else grp > 0)
              for j, rows in enumerate(chunks)]
        gn = gn_ref[...]
        dgn = jnp.zeros((1, GLA_DV), F32)
        dqe_h, dke_h, dkl_h, dbl_h = ([[None] * GLA_HEADS for _ in chunks] for _ in range(4))
        dsts = [dst_scr[h] for h in range(GLA_HEADS)]
        for j in reversed(range(GLA_GROUP)):
            rows, c = chunks[j], cs[j]
            for h in range(GLA_HEADS):
                ks, vs = slice(h * GLA_DK, (h + 1) * GLA_DK), slice(h * GLA_DV, (h + 1) * GLA_DV)
                dst = dsts[h]
                v = v_ref[rows, vs]
                st = s_ref[0, j, h]
                o = o_ref[rows, vs]
                r = lax.rsqrt(jnp.mean(o * o, axis=-1, keepdims=True) + EPS)
                xh = o * r
                zg = z_ref[rows, vs].astype(F32)
                sg = _sigmoid(zg)
                dy = dya_ref[rows, vs].astype(F32)
                dzg_ref[rows, vs] = _bf(dy * (xh * gn) * (sg * (1.0 + zg * (1.0 - sg))))
                t = dy * (zg * sg)
                dgn += jnp.sum(t * xh, axis=0, keepdims=True)
                dxh = t * gn
                do_b = _bf(r * (dxh - xh * jnp.mean(dxh * xh, axis=-1, keepdims=True)))
                qe_b, ke_b, kl_b, dst_b = c["qe_b"][:, ks], c["ke_b"][:, ks], c["kl_b"][:, ks], _bf(dst)
                a = jnp.where(c["tril"], _dot_nt(qe_b, ke_b), 0.0)
                da_b = _bf(jnp.where(c["tril"], _dot_nt(do_b, v), 0.0))
                dqe_h[j][h] = _dot(da_b, ke_b) + _dot(do_b, _bf(st))
                dke_h[j][h] = _dot_tn(da_b, qe_b)
                dkl = _dot(v, dst_b)
                dkl_h[j][h] = dkl
                dv_ref[rows, vs] = _bf(_dot_tn(_bf(a), do_b) + _dot_nt(kl_b, dst_b))
                ddecay = jnp.sum(dst * st, axis=0, keepdims=True)
                dbl_h[j][h] = jnp.sum(dkl * c["kl"][:, ks], axis=0, keepdims=True) + ddecay * c["ebl"][:, ks]
                dsts[h] = dst * c["ebl"][:, ks] + _dot_tn(do_b, qe_b)
        for h in range(GLA_HEADS):
            dst_scr[h] = dsts[h]
        dgn_ref[...] += dgn
        ri = lax.broadcasted_iota(jnp.int32, (GLA_CHUNK, GLA_CHUNK), 0)
        ci = lax.broadcasted_iota(jnp.int32, (GLA_CHUNK, GLA_CHUNK), 1)
        triu = _bf((ci >= ri).astype(F32))
        for j, (rows, c) in enumerate(zip(chunks, cs)):
            dqe, dke, dkl, dbl = (jnp.concatenate(p[j], axis=1) for p in (dqe_h, dke_h, dkl_h, dbl_h))
            db = dqe * c["qe"] - dke * c["ke"] - dkl * c["kl"] + jnp.where(c["row"] == GLA_CHUNK - 1, dbl, 0.0)
            dg = _tri_dot(triu, db)
            dg = jnp.where(c["live"], dg, 0.0)
            dz_ref[rows, :] = dg * (1.0 / GLA_GATE_NORMALIZER) * _sigmoid(-c["z"])
            dp_ref[rows, C_Q:C_Q + GLA_KW] = _bf(dqe * c["eb"] * (GLA_DK ** -0.5))
            dp_ref[rows, C_K:C_K + GLA_KW] = _bf(dke * c["enb"] + dkl * c["elb"])

    rb, in_specs = _gla_in_specs(n_groups, True)
    wide = pl.BlockSpec((GLA_ROWS, GLA_VW), lambda b, n: (rb(b, n), 0))
    group = C_MZ
    return pl.pallas_call(
        body, name="gla_bwd", grid=(bsz, n_groups),
        in_specs=in_specs + [wide, pl.BlockSpec((1, GLA_GROUP, GLA_HEADS, GLA_DV, GLA_DK),
                                                lambda b, n: (b, n_groups - 1 - n, 0, 0, 0)), wide,
                             pl.BlockSpec(memory_space=pl.ANY)],
        out_specs=[pl.BlockSpec((GLA_ROWS, group), lambda b, n: (rb(b, n), 0)),
                   pl.BlockSpec((GLA_ROWS, GLA_KW), lambda b, n: (rb(b, n), 0)),
                   pl.BlockSpec((1, GLA_DV), lambda b, n: (0, 0))],
        out_shape=[jax.ShapeDtypeStruct((tp, N_EXT), BF16), jax.ShapeDtypeStruct((tp, GLA_KW), F32),
                   jax.ShapeDtypeStruct((1, GLA_DV), F32)],
        input_output_aliases={11: 0},
        scratch_shapes=[pltpu.VMEM((GLA_HEADS, GLA_DV, GLA_DK), F32)],
        compiler_params=_cp(("arbitrary", "arbitrary")),
    )(proj, proj, proj, proj, proj, gw_pad, gate_b, gla_norm_g, o_raw, s_all, d_ya, dproj)


def _gate_bwd(dz, proj, gw_pad):
    tp = dz.shape[0]
    tm = _big_tok(tp)

    def body(dz_ref, lr_ref, gw_ref, dlr_ref, dgw_ref, dgb_ref):
        @pl.when(pl.program_id(0) == 0)
        def _():
            dgw_ref[...] = jnp.zeros_like(dgw_ref)
            dgb_ref[...] = jnp.zeros_like(dgb_ref)

        dz = dz_ref[...]
        dz_b = _bf(dz)
        dlr_ref[...] = _bf(_dot_nt(dz_b, gw_ref[...]))
        dgw_ref[...] += _dot_tn(lr_ref[...], dz_b)
        dgb_ref[...] += jnp.sum(dz, axis=0, keepdims=True)

    return pl.pallas_call(
        body, name="gate_bwd", grid=(tp // tm,),
        in_specs=[pl.BlockSpec((tm, GLA_KW), lambda i: (i, 0)),
                  pl.BlockSpec((tm, LANE), lambda i: (i, C_LR // LANE)),
                  pl.BlockSpec((LANE, GLA_KW), lambda i: (0, 0))],
        out_specs=[pl.BlockSpec((tm, LANE), lambda i: (i, 0)),
                   pl.BlockSpec((LANE, GLA_KW), lambda i: (0, 0)),
                   pl.BlockSpec((1, GLA_KW), lambda i: (0, 0))],
        out_shape=[jax.ShapeDtypeStruct((tp, LANE), BF16), jax.ShapeDtypeStruct((LANE, GLA_KW), F32),
                   jax.ShapeDtypeStruct((1, GLA_KW), F32)],
        compiler_params=_cp(("arbitrary",)),
    )(dz, proj, gw_pad)


def _rms_fwd(x):
    r = lax.rsqrt(jnp.mean(x * x, axis=-1, keepdims=True) + EPS)
    return x * r, r


def _rms_bwd(dy, xh, r, g):
    dxh = dy * g
    dx = r * (dxh - xh * jnp.mean(dxh * xh, axis=-1, keepdims=True))
    return dx, jnp.sum(dy * xh, axis=0, keepdims=True)


def _q_up(proj, q_norm_g, wn, wr, wt, cos_t, sin_t, bsz, lp):
    tp = bsz * lp
    tok = _attn_block(lp)
    nb = lp // tok

    def body(cq_ref, g_ref, wn_ref, wr_ref, wt_ref, cos_ref, sin_ref, q_ref):
        xh, _ = _rms_fwd(cq_ref[...].astype(F32))
        cqn = _bf(xh * g_ref[...])
        nope = _dot(cqn, wn_ref[...])
        rope = _dot(cqn, wr_ref[...])
        rot = _dot(cqn, wt_ref[...])
        cos, sin = cos_ref[...], sin_ref[...]
        one = (lax.broadcasted_iota(jnp.int32, (tok, LANE), 1) == BIAS_LANE).astype(F32)
        for h in range(MLA_HEADS):
            sl = slice(h * LANE, (h + 1) * LANE)
            q_ref[:, h * QKW:h * QKW + LANE] = _bf(nope[:, sl])
            q_ref[:, h * QKW + LANE:(h + 1) * QKW] = _bf(rope[:, sl] * cos + rot[:, sl] * sin + one)

    wspec = pl.BlockSpec((MLA_QR, MLA_HEADS * LANE), lambda b, i: (0, 0))
    tspec = pl.BlockSpec((tok, LANE), lambda b, i: (i, 0))
    return pl.pallas_call(
        body, name="mla_q_up", grid=(bsz, nb),
        in_specs=[pl.BlockSpec((tok, MLA_QR), lambda b, i: (b * nb + i, C_CQ // MLA_QR)),
                  pl.BlockSpec((1, MLA_QR), lambda b, i: (0, 0)), wspec, wspec, wspec, tspec, tspec],
        out_specs=pl.BlockSpec((tok, MLA_HEADS * QKW), lambda b, i: (b * nb + i, 0)),
        out_shape=jax.ShapeDtypeStruct((tp, MLA_HEADS * QKW), BF16),
        compiler_params=_cp(("parallel", "parallel")),
    )(proj, q_norm_g, wn, wr, wt, cos_t, sin_t)


def _kv_up(proj, kv_norm_g, wk, wv, cos_t, sin_t, bsz, lp):
    tp = bsz * lp
    tok = _attn_block(lp)
    nb = lp // tok

    def body(ckv_ref, kr_ref, krot_ref, g_ref, wk_ref, wv_ref, cos_ref, sin_ref, k_ref, v_ref):
        xh, _ = _rms_fwd(ckv_ref[...].astype(F32))
        cn = _bf(xh * g_ref[...])
        kn = _dot(cn, wk_ref[...])
        v_ref[...] = _bf(_dot(cn, wv_ref[...]))
        pos = pl.program_id(1) * tok + lax.broadcasted_iota(jnp.int32, (tok, LANE), 0)
        lane = lax.broadcasted_iota(jnp.int32, (tok, LANE), 1)
        bias = jnp.where(jnp.logical_and(lane == BIAS_LANE, pos < FRONT), KEY_BIAS, 0.0)
        kr = _bf(kr_ref[...].astype(F32) * cos_ref[...] + krot_ref[...].astype(F32) * sin_ref[...] + bias)
        for h in range(MLA_HEADS):
            k_ref[:, h * QKW:h * QKW + LANE] = _bf(kn[:, h * LANE:(h + 1) * LANE])
            k_ref[:, h * QKW + LANE:(h + 1) * QKW] = kr

    wspec = pl.BlockSpec((MLA_KVR, MLA_HEADS * LANE), lambda b, i: (0, 0))
    tspec = pl.BlockSpec((tok, LANE), lambda b, i: (i, 0))
    return pl.pallas_call(
        body, name="mla_kv_up", grid=(bsz, nb),
        in_specs=[pl.BlockSpec((tok, LANE), lambda b, i: (b * nb + i, C_CKV // LANE)),
                  pl.BlockSpec((tok, LANE), lambda b, i: (b * nb + i, C_KR // LANE)),
                  pl.BlockSpec((tok, LANE), lambda b, i: (b * nb + i, C_KROT // LANE)),
                  pl.BlockSpec((1, MLA_KVR), lambda b, i: (0, 0)), wspec, wspec, tspec, tspec],
        out_specs=[pl.BlockSpec((tok, MLA_HEADS * QKW), lambda b, i: (b * nb + i, 0)),
                   pl.BlockSpec((tok, MLA_HEADS * LANE), lambda b, i: (b * nb + i, 0))],
        out_shape=[jax.ShapeDtypeStruct((tp, MLA_HEADS * QKW), BF16),
                   jax.ShapeDtypeStruct((tp, MLA_HEADS * LANE), BF16)],
        compiler_params=_cp(("parallel", "parallel")),
    )(proj, proj, proj, kv_norm_g, wk, wv, cos_t, sin_t)


ATT_SCALE = MLA_QK ** -0.5


KEY_BIAS = -1e30
BIAS_LANE = MLA_ROPE
NEG = 2 * KEY_BIAS
LOG2E = 1.4426950408889634
EXP2_SCALE = ATT_SCALE * LOG2E


def _causal_fill(s, r0, fill):
    tq, kmax = s.shape
    a = r0 // LANE * LANE
    mask = (a + lax.broadcasted_iota(jnp.int32, (tq, kmax - a), 1)
            <= r0 + lax.broadcasted_iota(jnp.int32, (tq, kmax - a), 0))
    right = jnp.where(mask, s[:, a:], fill)
    return jnp.concatenate([s[:, :a], right], axis=1) if a else right


def _attn_fwd(qf, kf, vf, proj, bsz, lp):
    tp = bsz * lp
    tq = _attn_block(lp)
    nh = 2

    def body(q_ref, k_ref, v_ref, mz_ref, ob_ref, yb_ref, lse_ref):
        starts = list(range(0, lp, tq))
        for pair in (starts[i:i + 2] for i in range(0, len(starts), 2)):
            work = [(r0, h) for r0 in pair for h in range(nh)]
            ss = [_causal_fill(_dot_nt(q_ref[r0:r0 + tq, h * QKW:(h + 1) * QKW],
                                       k_ref[0:r0 + tq, h * QKW:(h + 1) * QKW]), r0, NEG) for r0, h in work]
            ms = [jnp.max(s, axis=-1, keepdims=True) for s in ss]
            ps = [jnp.exp2((s - m) * EXP2_SCALE) for s, m in zip(ss, ms)]
            ls = [jnp.sum(p, axis=-1, keepdims=True) for p in ps]
            for (r0, h), p, m, l in zip(work, ps, ms, ls):
                rows, cols = slice(r0, r0 + tq), slice(h * MLA_DV, (h + 1) * MLA_DV)
                o = _dot(_bf(p), v_ref[0:r0 + tq, cols]) / l
                ob_ref[rows, cols] = _bf(o)
                mz = mz_ref[rows, cols].astype(F32)
                yb_ref[rows, cols] = _bf(o * (mz * _sigmoid(mz)))
                lse_ref[0, h, rows, :] = jnp.broadcast_to(m * EXP2_SCALE + jnp.log2(l), (tq, LANE))

    head = lambda off: pl.BlockSpec((lp, nh * MLA_DV), lambda b, h: (b, off + h))
    wide = pl.BlockSpec((lp, nh * QKW), lambda b, h: (b, h))
    return pl.pallas_call(
        body, name="mla_attn_fwd", grid=(bsz, MLA_HEADS // nh),
        in_specs=[wide, wide, head(0), head(C_MZ // (nh * MLA_DV))],
        out_specs=[head(0), head(0), pl.BlockSpec((1, nh, lp, LANE), lambda b, h: (b, h, 0, 0))],
        out_shape=[jax.ShapeDtypeStruct((tp, MLA_HEADS * MLA_DV), BF16),
                   jax.ShapeDtypeStruct((tp, MLA_HEADS * MLA_DV), BF16),
                   jax.ShapeDtypeStruct((bsz, MLA_HEADS, lp, LANE), F32)],
        compiler_params=_cp(("parallel", "parallel"), 56),
    )(qf, kf, vf, proj)


def _attn_bwd(qf, kf, vf, d_o, lse, delta, bsz, lp):
    tp = bsz * lp
    tq = _attn_block(lp)

    def body(q_ref, k_ref, v_ref, do_ref, lse_ref, dl_ref, dq_ref, dk_ref, dv_ref, dk_acc, dv_acc):
        dk_acc[...] = jnp.zeros_like(dk_acc)
        dv_acc[...] = jnp.zeros_like(dv_acc)
        for r0 in range(0, lp, tq):
            rows, kmax = slice(r0, r0 + tq), r0 + tq
            q, do = q_ref[rows, :], do_ref[rows, :]
            k, v = k_ref[0:kmax, :], v_ref[0:kmax, :]
            p = jnp.exp2(_dot_nt(q, k) * EXP2_SCALE - lse_ref[0, 0, rows, :][:, :1])
            p = _causal_fill(p, r0, 0.0)
            ds = _bf(p * (_dot_nt(do, v) - dl_ref[0, rows, :][:, :1]))
            dq_ref[rows, :] = _bf(_dot(ds, k) * ATT_SCALE)
            dk_acc[0:kmax, :] += _dot_tn(ds, q)
            dv_acc[0:kmax, :] += _dot_tn(_bf(p), do)
        dk_ref[...] = _bf(dk_acc[...] * ATT_SCALE)
        dv_ref[...] = _bf(dv_acc[...])

    wide = pl.BlockSpec((lp, QKW), lambda b, h: (b, h))
    narrow = pl.BlockSpec((lp, MLA_DV), lambda b, h: (b, h))
    stat = pl.BlockSpec((1, 1, lp, LANE), lambda b, h: (b, h, 0, 0))
    return pl.pallas_call(
        body, name="mla_attn_bwd", grid=(bsz, MLA_HEADS),
        in_specs=[wide, wide, narrow, narrow, stat, pl.BlockSpec((1, lp, LANE), lambda b, h: (h, b, 0))],
        out_specs=[wide, wide, narrow],
        out_shape=[jax.ShapeDtypeStruct((tp, MLA_HEADS * QKW), BF16), jax.ShapeDtypeStruct((tp, MLA_HEADS * QKW), BF16),
                   jax.ShapeDtypeStruct((tp, MLA_HEADS * MLA_DV), BF16)],
        scratch_shapes=[pltpu.VMEM((lp, QKW), F32), pltpu.VMEM((lp, MLA_DV), F32)],
        compiler_params=_cp(("parallel", "parallel"), 56),
    )(qf, kf, vf, d_o, lse, delta)


def _q_up_bwd(dqf, proj, q_norm_g, wn, wr, wt, cos_t, sin_t, dproj, bsz, lp):
    tp = bsz * lp
    tok = _attn_block(lp)
    nb = lp // tok
    hw = MLA_HEADS * LANE

    def body(dq_ref, cq_ref, g_ref, wn_ref, wr_ref, wt_ref, cos_ref, sin_ref, _,
             dcq_ref, dwn_ref, dwr_ref, dwt_ref, dg_ref):
        @pl.when(jnp.logical_and(pl.program_id(0) == 0, pl.program_id(1) == 0))
        def _():
            for r in (dwn_ref, dwr_ref, dwt_ref, dg_ref):
                r[...] = jnp.zeros_like(r)

        g = g_ref[...]
        xh, r = _rms_fwd(cq_ref[...].astype(F32))
        cqn = _bf(xh * g)
        dn = jnp.concatenate([dq_ref[:, h * QKW:h * QKW + LANE] for h in range(MLA_HEADS)], axis=1)
        dr = jnp.concatenate([dq_ref[:, h * QKW + LANE:(h + 1) * QKW] for h in range(MLA_HEADS)], axis=1).astype(F32)
        dr_c = _bf(dr * jnp.tile(cos_ref[...], (1, MLA_HEADS)))
        dr_s = _bf(dr * jnp.tile(sin_ref[...], (1, MLA_HEADS)))
        dcqn = _dot_nt(dn, wn_ref[...]) + _dot_nt(dr_c, wr_ref[...]) + _dot_nt(dr_s, wt_ref[...])
        dwn_ref[...] += _dot_tn(cqn, dn)
        dwr_ref[...] += _dot_tn(cqn, dr_c)
        dwt_ref[...] += _dot_tn(cqn, dr_s)
        dx, dg = _rms_bwd(dcqn, xh, r, g)
        dcq_ref[...] = _bf(dx)
        dg_ref[...] += dg

    aspec = pl.BlockSpec((MLA_QR, hw), lambda b, i: (0, 0))
    tspec = pl.BlockSpec((tok, LANE), lambda b, i: (i, 0))
    return pl.pallas_call(
        body, name="mla_q_up_bwd", grid=(bsz, nb),
        in_specs=[pl.BlockSpec((tok, MLA_HEADS * QKW), lambda b, i: (b * nb + i, 0)),
                  pl.BlockSpec((tok, MLA_QR), lambda b, i: (b * nb + i, C_CQ // MLA_QR)),
                  pl.BlockSpec((1, MLA_QR), lambda b, i: (0, 0)), aspec, aspec, aspec, tspec, tspec,
                  pl.BlockSpec(memory_space=pl.ANY)],
        out_specs=[pl.BlockSpec((tok, MLA_QR), lambda b, i: (b * nb + i, C_CQ // MLA_QR)), aspec, aspec, aspec,
                   pl.BlockSpec((1, MLA_QR), lambda b, i: (0, 0))],
        out_shape=[jax.ShapeDtypeStruct((tp, N_EXT), BF16)] + [jax.ShapeDtypeStruct((MLA_QR, hw), F32)] * 3
        + [jax.ShapeDtypeStruct((1, MLA_QR), F32)],
        input_output_aliases={8: 0},
        compiler_params=_cp(("arbitrary", "arbitrary")),
    )(dqf, proj, q_norm_g, wn, wr, wt, cos_t, sin_t, dproj)


def _kv_up_bwd(dkf, dvf, proj, kv_norm_g, wk, wv, cos_t, sin_t, d_lr, dproj, bsz, lp):
    tp = bsz * lp
    tok = _attn_block(lp)
    nb = lp // tok
    hw = MLA_HEADS * LANE

    def body(dk_ref, dv_ref, ckv_ref, g_ref, wk_ref, wv_ref, cos_ref, sin_ref, dlr_ref, _,
             dp_ref, dwk_ref, dwv_ref, dg_ref):
        dckv_ref, dkr_ref, dkrot_ref = (dp_ref.at[:, j * LANE:(j + 1) * LANE] for j in range(3))
        dp_ref[:, 3 * LANE:] = dlr_ref[...]
        @pl.when(jnp.logical_and(pl.program_id(0) == 0, pl.program_id(1) == 0))
        def _():
            for r in (dwk_ref, dwv_ref, dg_ref):
                r[...] = jnp.zeros_like(r)

        g = g_ref[...]
        xh, r = _rms_fwd(ckv_ref[...].astype(F32))
        cn = _bf(xh * g)
        dv = dv_ref[...]
        dn = jnp.concatenate([dk_ref[:, h * QKW:h * QKW + LANE] for h in range(MLA_HEADS)], axis=1)
        dcn = _dot_nt(dv, wv_ref[...]) + _dot_nt(dn, wk_ref[...])
        dwv_ref[...] += _dot_tn(cn, dv)
        dwk_ref[...] += _dot_tn(cn, dn)
        drope = jnp.zeros((tok, LANE), F32)
        for h in range(MLA_HEADS):
            drope += dk_ref[:, h * QKW + LANE:(h + 1) * QKW].astype(F32)
        dkr_ref[...] = _bf(drope * cos_ref[...])
        dkrot_ref[...] = _bf(drope * sin_ref[...])
        dx, dg = _rms_bwd(dcn, xh, r, g)
        dckv_ref[...] = _bf(dx)
        dg_ref[...] += dg

    aspec = pl.BlockSpec((MLA_KVR, hw), lambda b, i: (0, 0))
    tspec = pl.BlockSpec((tok, LANE), lambda b, i: (i, 0))
    ospec = pl.BlockSpec((tok, LANE), lambda b, i: (b * nb + i, 0))
    return pl.pallas_call(
        body, name="mla_kv_up_bwd", grid=(bsz, nb),
        in_specs=[pl.BlockSpec((tok, MLA_HEADS * QKW), lambda b, i: (b * nb + i, 0)),
                  pl.BlockSpec((tok, hw), lambda b, i: (b * nb + i, 0)),
                  pl.BlockSpec((tok, LANE), lambda b, i: (b * nb + i, C_CKV // LANE)),
                  pl.BlockSpec((1, MLA_KVR), lambda b, i: (0, 0)), aspec, aspec, tspec, tspec, ospec,
                  pl.BlockSpec(memory_space=pl.ANY)],
        out_specs=[pl.BlockSpec((tok, 4 * LANE), lambda b, i: (b * nb + i, C_CKV // (4 * LANE))), aspec, aspec,
                   pl.BlockSpec((1, MLA_KVR), lambda b, i: (0, 0))],
        out_shape=[jax.ShapeDtypeStruct((tp, N_EXT), BF16)] + [jax.ShapeDtypeStruct((MLA_KVR, hw), F32)] * 2
        + [jax.ShapeDtypeStruct((1, MLA_KVR), F32)],
        input_output_aliases={9: 0},
        compiler_params=_cp(("arbitrary", "arbitrary")),
    )(dkf, dvf, proj, kv_norm_g, wk, wv, cos_t, sin_t, d_lr, dproj)


def _mid_fwd(ya_in, yb_in, proj, hp, target, w_gp, w_mp, w_o, final_g, bsz, lp):
    tp = bsz * lp
    tm = _attn_block(lp)
    nb = lp // tm
    last = pl.cdiv(lp - X0, tm) - 1

    def body(ya_ref, yb_ref, gg_ref, gm_ref, h_ref, ta_ref, tb_ref, wgp_ref, wmp_ref, wo_ref, fg_ref,
             ya_out, yb_out, dh_ref, loss_ref, dfg_ref):
        @pl.when(jnp.logical_and(pl.program_id(0) == 0, pl.program_id(1) == 0))
        def _():
            loss_ref[...] = jnp.zeros_like(loss_ref)
            dfg_ref[...] = jnp.zeros_like(dfg_ref)

        fg = fg_ref[...]
        half = tm // 2
        halves = [slice(0, half), slice(half, tm)]
        y_as = [_dot(ya_ref[rows, :], wgp_ref[...]) for rows in halves]
        y_bs = [_dot(yb_ref[rows, :], wmp_ref[...]) for rows in halves]
        mergeds = []
        for rows, y_a, y_b in zip(halves, y_as, y_bs):
            ya_out[rows, :] = _bf(y_a)
            yb_out[rows, :] = _bf(y_b)
            mergeds.append(_bf(_sigmoid(gg_ref[rows, :].astype(F32)) * y_a + _sigmoid(gm_ref[rows, :].astype(F32)) * y_b))
        h2s = [h_ref[rows, :] + _dot(merged, wo_ref[...]) for rows, merged in zip(halves, mergeds)]
        t = jnp.concatenate([ta_ref[0, tm - X0:, :], tb_ref[0, :tm - X0, :]], axis=0)
        for rows, h2 in zip(halves, h2s):
            xh, r = _rms_fwd(h2)
            pos = pl.program_id(1) * tm + rows.start + lax.broadcasted_iota(jnp.int32, (half, 1), 0)
            err = jnp.where(pos >= X0, xh * fg - t[rows], 0.0)
            loss_ref[...] += 0.5 * jnp.sum(jnp.mean(err * err, axis=-1, keepdims=True), axis=0, keepdims=True)
            dy = err * (1.0 / D_MODEL)
            dx, dfg = _rms_bwd(dy, xh, r, fg)
            dh_ref[rows, :] = dx
            dfg_ref[...] += dfg

    tok = lambda c: pl.BlockSpec((tm, D_MODEL), lambda b, i: (b * nb + i, c))
    wspec = pl.BlockSpec((D_MODEL, D_MODEL), lambda b, i: (0, 0))
    return pl.pallas_call(
        body, name="mid_fwd", grid=(bsz, nb),
        in_specs=[tok(0), tok(0), tok(C_GG // D_MODEL), tok(C_GM // D_MODEL), tok(0),
                  pl.BlockSpec((1, tm, D_MODEL), lambda b, i: (b, jnp.maximum(i - 1, 0), 0)),
                  pl.BlockSpec((1, tm, D_MODEL), lambda b, i: (b, jnp.minimum(i, last), 0)),
                  wspec, wspec, wspec, pl.BlockSpec((1, D_MODEL), lambda b, i: (0, 0))],
        out_specs=[tok(0), tok(0), tok(0), pl.BlockSpec((1, LANE), lambda b, i: (0, 0)),
                   pl.BlockSpec((1, D_MODEL), lambda b, i: (0, 0))],
        out_shape=[jax.ShapeDtypeStruct((tp, D_MODEL), BF16), jax.ShapeDtypeStruct((tp, D_MODEL), BF16),
                   jax.ShapeDtypeStruct((tp, D_MODEL), F32), jax.ShapeDtypeStruct((1, LANE), F32),
                   jax.ShapeDtypeStruct((1, D_MODEL), F32)],
        compiler_params=_cp(("arbitrary", "arbitrary"), 48),
    )(ya_in, yb_in, proj, proj, hp, target, target, w_gp, w_mp, w_o, final_g)


def _mid_bwd(dh2, y_a, y_b, proj, ya_in, yb_in, o_b, w_o, w_gp, w_mp, bsz, lp):
    tp = bsz * lp
    tm = MXU_DEPTH if tp % MXU_DEPTH == 0 else _attn_block(lp)
    nsteps = tp // tm
    group = 3 * D_MODEL

    def body(dh_ref, ya_ref, yb_ref, mz_ref, gg_ref, gm_ref, yai_ref, ybi_ref, ob_ref, wo_ref, wgp_ref, wmp_ref,
             dyai_ref, do_ref, dp_ref, dl_ref, dwo_ref, dwgp_ref, dwmp_ref, a_o, a_gp, a_mp):
        @pl.when(pl.program_id(0) == 0)
        def _():
            for r in (a_o, a_gp, a_mp):
                r[...] = jnp.zeros_like(r)

        dh = _bf(dh_ref[...])
        dm = _dot_nt(dh, wo_ref[...])
        y_a, y_b = ya_ref[...].astype(F32), yb_ref[...].astype(F32)
        sg, sm = _sigmoid(gg_ref[...].astype(F32)), _sigmoid(gm_ref[...].astype(F32))
        d_ya, d_yb = _bf(sg * dm), _bf(sm * dm)
        dp_ref[:, D_MODEL:2 * D_MODEL] = _bf(dm * y_a * sg * (1.0 - sg))
        dp_ref[:, 2 * D_MODEL:] = _bf(dm * y_b * sm * (1.0 - sm))
        a_o[...] += _dot_tn(_bf(sg * y_a + sm * y_b), dh)
        a_gp[...] += _dot_tn(yai_ref[...], d_ya)
        a_mp[...] += _dot_tn(ybi_ref[...], d_yb)
        dyai_ref[...] = _bf(_dot_nt(d_ya, wgp_ref[...]))
        dy = _dot_nt(d_yb, wmp_ref[...])
        mz, o = mz_ref[...].astype(F32), ob_ref[...].astype(F32)
        s = _sigmoid(mz)
        do = _bf(dy * (mz * s))
        do_ref[...] = do
        dp_ref[:, :D_MODEL] = _bf(dy * o * (s * (1.0 + mz * (1.0 - s))))
        prod = do.astype(F32) * o
        for h in range(MLA_HEADS):
            dl = jnp.sum(prod[:, h * MLA_DV:(h + 1) * MLA_DV], axis=-1, keepdims=True)
            dl_ref[h] = jnp.broadcast_to(dl, (tm, LANE))

        @pl.when(pl.program_id(0) == nsteps - 1)
        def _():
            pltpu.sync_copy(a_o, dwo_ref)
            pltpu.sync_copy(a_gp, dwgp_ref)
            pltpu.sync_copy(a_mp, dwmp_ref)

    tok = lambda c: pl.BlockSpec((tm, D_MODEL), lambda i: (i, c))
    wspec = pl.BlockSpec((D_MODEL, D_MODEL), lambda i: (0, 0))
    anyspec = pl.BlockSpec(memory_space=pl.ANY)
    wshape = jax.ShapeDtypeStruct((D_MODEL, D_MODEL), F32)
    return pl.pallas_call(
        body, name="mid_bwd", grid=(nsteps,),
        in_specs=[tok(0), tok(0), tok(0), tok(C_MZ // D_MODEL), tok(C_GG // D_MODEL), tok(C_GM // D_MODEL),
                  tok(0), tok(0), tok(0), wspec, wspec, wspec],
        out_specs=[tok(0), tok(0), pl.BlockSpec((tm, group), lambda i: (i, C_MZ // group)),
                   pl.BlockSpec((MLA_HEADS, tm, LANE), lambda i: (0, i, 0)), anyspec, anyspec, anyspec],
        out_shape=[jax.ShapeDtypeStruct((tp, D_MODEL), BF16)] * 2 + [jax.ShapeDtypeStruct((tp, N_EXT), BF16),
                   jax.ShapeDtypeStruct((MLA_HEADS, tp, LANE), F32)] + [wshape] * 3,
        scratch_shapes=[pltpu.VMEM((D_MODEL, D_MODEL), F32)] * 3,
        compiler_params=_cp(("arbitrary",), 56),
    )(dh2, y_a, y_b, proj, proj, proj, ya_in, yb_in, o_b, w_o, w_gp, w_mp)


MESH_ID = pl.DeviceIdType.MESH
EXCHANGE_SEMS = [pltpu.SemaphoreType.DMA((N_DEV - 1,)), pltpu.SemaphoreType.DMA((N_DEV - 1,)), pltpu.SemaphoreType.DMA]


def _my_place():
    return lax.axis_index("x"), lax.axis_index("y"), lax.axis_index("c")


def _exchange(g_ref, recv_ref, send_sems, recv_sems, local_sem, start, same=False):
    x, y, c = _my_place()
    me = 4 * x + 2 * y + c
    own = pltpu.make_async_copy(g_ref if same else g_ref.at[me], recv_ref.at[me], local_sem)
    sends, lands = [], []
    for d in range(1, N_DEV):
        px = 1 - x if d & 4 else x
        py = 1 - y if d & 2 else y
        pc = 1 - c if d & 1 else c
        peer = 4 * px + 2 * py + pc
        for slot, group in ((me, sends),) if start else ((me, sends), (peer, lands)):
            group.append(pltpu.make_async_remote_copy(
                src_ref=g_ref if same else g_ref.at[peer], dst_ref=recv_ref.at[slot], send_sem=send_sems.at[d - 1],
                recv_sem=recv_sems.at[d - 1], device_id=(px, py, pc), device_id_type=MESH_ID))
    if start:
        own.start()
        for cp in sends:
            cp.start()
    else:
        for cp in lands:
            cp.wait_recv()
        for cp in sends:
            cp.wait_send()
        own.wait()


def _dw_in(u, dproj, slabs):
    tp = u.shape[0]
    tm, tn = _big_tok(tp), EXT_BLOCK
    nj, ni = N_EXT // tn, tp // tm

    def body(u_ref, d_ref, g_ref, o_ref, recv_ref, send_sems, recv_sems, local_sem):
        j, i = pl.program_id(0), pl.program_id(1)

        @pl.when(jnp.logical_and(j == 0, i == 0))
        def _():
            _exchange(g_ref, recv_ref, send_sems, recv_sems, local_sem, True)

        @pl.when(i == 0)
        def _():
            o_ref[...] = jnp.zeros_like(o_ref)

        o_ref[...] += _dot_tn(d_ref[...], u_ref[...])

        @pl.when(jnp.logical_and(j == nj - 1, i == ni - 1))
        def _():
            _exchange(g_ref, recv_ref, send_sems, recv_sems, local_sem, False)

    anyspec = pl.BlockSpec(memory_space=pl.ANY)
    return pl.pallas_call(
        body, name="dw_in", grid=(nj, ni),
        in_specs=[pl.BlockSpec((tm, D_MODEL), lambda j, i: (i, 0)), pl.BlockSpec((tm, tn), lambda j, i: (i, j)), anyspec],
        out_specs=[pl.BlockSpec((tn, D_MODEL), lambda j, i: (j, 0)), anyspec],
        out_shape=[jax.ShapeDtypeStruct((N_EXT, D_MODEL), F32), jax.ShapeDtypeStruct(slabs.shape, slabs.dtype)],
        scratch_shapes=EXCHANGE_SEMS,
        compiler_params=_cp(("arbitrary", "arbitrary"), 48),
    )(u, dproj, slabs)


def _dx_in(dproj, w_ext, hp, dh2, norm_g, slabs):
    tp = hp.shape[0]
    tm = 2 * TOK
    ni = tp // tm

    def body(d_ref, w_ref, h_ref, dh_ref, g_ref, s_ref, o_ref, dg_ref, recv_ref, send_sems, recv_sems, local_sem):
        i = pl.program_id(0)

        @pl.when(i == 0)
        def _():
            _exchange(s_ref, recv_ref, send_sems, recv_sems, local_sem, True)
            dg_ref[...] = jnp.zeros_like(dg_ref)

        du = _dot_nt(d_ref[...], w_ref[...])
        g = g_ref[...]
        xh, r = _rms_fwd(h_ref[...])
        dx, dg = _rms_bwd(du, xh, r, g)
        o_ref[...] = dh_ref[...] + dx
        dg_ref[...] += dg

        @pl.when(i == ni - 1)
        def _():
            _exchange(s_ref, recv_ref, send_sems, recv_sems, local_sem, False)

    tok = pl.BlockSpec((tm, D_MODEL), lambda i: (i, 0))
    anyspec = pl.BlockSpec(memory_space=pl.ANY)
    return pl.pallas_call(
        body, name="dx_in", grid=(ni,),
        in_specs=[pl.BlockSpec((tm, N_EXT), lambda i: (i, 0)),
                  pl.BlockSpec((D_MODEL, N_EXT), lambda i: (0, 0), pipeline_mode=pl.Buffered(1)),
                  tok, tok, pl.BlockSpec((1, D_MODEL), lambda i: (0, 0)), anyspec],
        out_specs=[tok, pl.BlockSpec((1, D_MODEL), lambda i: (0, 0)), anyspec],
        out_shape=[jax.ShapeDtypeStruct((tp, D_MODEL), F32), jax.ShapeDtypeStruct((1, D_MODEL), F32),
                   jax.ShapeDtypeStruct(slabs.shape, slabs.dtype)],
        scratch_shapes=EXCHANGE_SEMS,
        compiler_params=_cp(("arbitrary",), 56),
    )(dproj, w_ext, hp, dh2, norm_g, slabs)


def _meta_grad(dhp3):
    bsz = dhp3.shape[0]

    def body(d_ref, o_ref):
        @pl.when(pl.program_id(0) == 0)
        def _():
            o_ref[...] = jnp.zeros_like(o_ref)

        o_ref[...] += d_ref[0]

    return pl.pallas_call(
        body, name="meta_grad", grid=(bsz,),
        in_specs=[pl.BlockSpec((1, N_META, D_MODEL), lambda b: (b, FRONT // N_META, 0))],
        out_specs=pl.BlockSpec((N_META, D_MODEL), lambda b: (0, 0)),
        out_shape=jax.ShapeDtypeStruct((N_META, D_MODEL), F32),
        compiler_params=_cp(("arbitrary",)),
    )(dhp3)


W_IN_SHARD = N_IN // N_DEV


def _pad_lanes(a, width=LANE):
    return jnp.pad(a, [(0, 0)] * (a.ndim - 1) + [(0, width - a.shape[-1])])


def _rot_cols(w):
    half = w.shape[-1] // 2
    return jnp.concatenate([-w[..., half:], w[..., :half]], axis=-1)


def _unrot_cols(dw):
    half = dw.shape[-1] // 2
    return jnp.concatenate([dw[..., half:], -dw[..., :half]], axis=-1)


def _w_in_cols(shards, lo, hi):
    parts = []
    for k in range(lo // W_IN_SHARD, (hi - 1) // W_IN_SHARD + 1):
        a, b = max(lo, k * W_IN_SHARD), min(hi, (k + 1) * W_IN_SHARD)
        parts.append(shards[k][:, a - k * W_IN_SHARD:b - k * W_IN_SHARD])
    return parts[0] if len(parts) == 1 else jnp.concatenate(parts, axis=1)


def _w_in_ext(shards):
    c = lambda lo, hi: _w_in_cols(shards, lo, hi)
    kr = c(O_KR, O_MZ)
    return jnp.concatenate([
        c(O_V, O_LR), c(O_Z, O_CQ), c(O_Q, O_K), c(O_K, O_V), c(O_MZ, O_GG), c(O_GG, O_GM), c(O_GM, N_IN),
        c(O_CKV, O_KR), _pad_lanes(kr), _pad_lanes(_rot_cols(kr)), _pad_lanes(c(O_LR, O_Z)), c(O_CQ, O_CKV)], axis=1)


def _w_in_grad_t(dwt):
    g = lambda start, width: dwt[start:start + width]
    half = MLA_ROPE // 2
    krot = g(C_KROT, MLA_ROPE)
    kr = g(C_KR, MLA_ROPE) + jnp.concatenate([krot[half:], -krot[:half]], axis=0)
    return jnp.concatenate([
        g(C_Q, GLA_KW), g(C_K, GLA_KW), g(C_V, GLA_VW), g(C_LR, GLA_RANK), g(C_Z, GLA_VW), g(C_CQ, MLA_QR),
        g(C_CKV, MLA_KVR), kr, g(C_MZ, D_MODEL), g(C_GG, D_MODEL), g(C_GM, D_MODEL)], axis=0)


def _rope_tables(lp):
    inv = 1.0 / (ROPE_BASE ** (jnp.arange(0, MLA_ROPE, 2, dtype=F32) / MLA_ROPE))
    ang = (jnp.arange(lp, dtype=F32) - FRONT)[:, None] * inv[None, :]
    cos, sin = jnp.cos(ang), jnp.sin(ang)
    return _pad_lanes(jnp.concatenate([cos, cos], axis=1)), _pad_lanes(jnp.concatenate([sin, sin], axis=1))


def _local_step(x, loss_target, w):
    bsz, seq, _ = x.shape
    lp = X0 + seq
    tp = bsz * lp
    assert lp % TOK == 0 and lp % GLA_ROWS == 0
    meta = jnp.broadcast_to(w["meta_tokens"][None], (bsz, N_META, D_MODEL))
    hp = jnp.concatenate([jnp.zeros((bsz, FRONT, D_MODEL), F32), meta, x], axis=1).reshape(tp, D_MODEL)
    cos_t, sin_t = _rope_tables(lp)

    w_ext = _w_in_ext(w["w_in"])
    u, proj, packed_all = _proj_in(hp, w["norm_g"], w_ext, w["packed"])
    packed_all, off = packed_all.reshape(N_DEV, -1), 0
    for n, shape, axis in PACKED:
        size = shape[0] * shape[1]
        w[n] = _join8(packed_all[:, off:off + size].reshape((N_DEV,) + shape), axis)
        off += size
    gw_pad = jnp.pad(w["gla_gate_w"], ((0, LANE - GLA_RANK), (0, 0)))
    uq = w["mla_w_uq"].reshape(MLA_QR, MLA_HEADS, MLA_QK)
    rope_w = uq[:, :, MLA_NOPE:]
    hw = MLA_HEADS * LANE
    wn = uq[:, :, :MLA_NOPE].reshape(MLA_QR, hw)
    wr = _pad_lanes(rope_w).reshape(MLA_QR, hw)
    wt = _pad_lanes(_rot_cols(rope_w)).reshape(MLA_QR, hw)
    ukv = w["mla_w_ukv"].reshape(MLA_KVR, MLA_HEADS, MLA_NOPE + MLA_DV)
    wk = ukv[:, :, :MLA_NOPE].reshape(MLA_KVR, hw)
    wv = ukv[:, :, MLA_NOPE:].reshape(MLA_KVR, hw)

    o_raw, ya_in, s_all = _gla_fwd(proj, gw_pad, w["gla_gate_b"], w["gla_norm_g"], bsz, lp)
    qf = _q_up(proj, w["mla_q_norm_g"], wn, wr, wt, cos_t, sin_t, bsz, lp)
    kf, vf = _kv_up(proj, w["mla_kv_norm_g"], wk, wv, cos_t, sin_t, bsz, lp)
    o_b, yb_in, lse = _attn_fwd(qf, kf, vf, proj, bsz, lp)
    y_a, y_b, dh2, loss, d_final_g = _mid_fwd(ya_in, yb_in, proj, hp, loss_target, w["gla_proj"], w["mla_proj"],
                                              w["w_out"], w["final_norm_g"], bsz, lp)
    d_ya, d_o, dproj, delta, d_w_out, d_gla_proj, d_mla_proj = _mid_bwd(
        dh2, y_a, y_b, proj, ya_in, yb_in, o_b, w["w_out"], w["gla_proj"], w["mla_proj"], bsz, lp)
    dproj, d_gate, d_gla_norm = _gla_bwd(proj, gw_pad, w["gla_gate_b"], w["gla_norm_g"], o_raw, s_all, d_ya, dproj,
                                         bsz, lp)
    d_lr, d_gw_pad, d_gate_b = _gate_bwd(d_gate, proj, gw_pad)
    dqf, dkf, dvf = _attn_bwd(qf, kf, vf, d_o, lse, delta, bsz, lp)
    dproj, d_wn, d_wr, d_wt, d_qn = _q_up_bwd(dqf, proj, w["mla_q_norm_g"], wn, wr, wt, cos_t, sin_t, dproj,
                                              bsz, lp)
    dproj, d_wk, d_wv, d_kvn = _kv_up_bwd(dkf, dvf, proj, w["mla_kv_norm_g"], wk, wv, cos_t, sin_t, d_lr, dproj,
                                          bsz, lp)

    d_rope = (d_wr.reshape(MLA_QR, MLA_HEADS, LANE)[:, :, :MLA_ROPE]
              + _unrot_cols(d_wt.reshape(MLA_QR, MLA_HEADS, LANE)[:, :, :MLA_ROPE]))
    d_uq = jnp.concatenate([d_wn.reshape(MLA_QR, MLA_HEADS, LANE), d_rope], axis=-1).reshape(MLA_QR, MLA_HEADS * MLA_QK)
    d_ukv = jnp.concatenate([d_wk.reshape(MLA_KVR, MLA_HEADS, LANE), d_wv.reshape(MLA_KVR, MLA_HEADS, LANE)],
                            axis=-1).reshape(MLA_KVR, MLA_HEADS * (MLA_NOPE + MLA_DV))
    mats = dict(gla_gate_w=d_gw_pad[:GLA_RANK], gla_proj=d_gla_proj, mla_w_uq=d_uq, mla_w_ukv=d_ukv,
                mla_proj=d_mla_proj, w_out=d_w_out)
    packed = _pad_rows(jnp.concatenate([_split8(mats[n], axis).reshape(N_DEV, -1) for n, _, axis in PACKED], axis=1),
                       PACK_ROWS)
    d_w_ext_t, packed_parts = _dw_in(u, dproj, _bf(packed))
    w_in_slabs = _bf(_w_in_grad_t(d_w_ext_t).reshape(N_DEV, W_IN_SHARD, D_MODEL))
    d_hp, d_norm_g, w_in_parts = _dx_in(dproj, w_ext, hp, dh2, w["norm_g"], w_in_slabs)
    d_hp3 = d_hp.reshape(bsz, lp, D_MODEL)
    small = dict(meta_tokens=_meta_grad(d_hp3), norm_g=d_norm_g, gla_gate_b=d_gate_b, gla_norm_g=d_gla_norm,
                 mla_q_norm_g=d_qn, mla_kv_norm_g=d_kvn, final_norm_g=d_final_g)
    return loss, d_hp3[:, X0:, :], w_in_parts, packed_parts, small


PACKED = (("gla_gate_w", (GLA_RANK, GLA_KW // N_DEV), 1),
          ("gla_proj", (D_MODEL // N_DEV, D_MODEL), 0), ("mla_w_uq", (MLA_QR, MLA_HEADS * MLA_QK // N_DEV), 1),
          ("mla_w_ukv", (MLA_KVR, MLA_HEADS * (MLA_NOPE + MLA_DV) // N_DEV), 1),
          ("mla_proj", (D_MODEL // N_DEV, D_MODEL), 0), ("w_out", (D_MODEL // N_DEV, D_MODEL), 0))
REPLICATED = (("norm_g", D_MODEL), ("gla_gate_b", GLA_KW), ("gla_norm_g", GLA_DV), ("mla_q_norm_g", MLA_QR),
              ("mla_kv_norm_g", MLA_KVR), ("final_norm_g", D_MODEL))
PACK_ROWS = 3744
PACK_BLOCK = 1248
SMALL_ROWS = 48
LOSS_ROW = N_META + 25
W_IN_BLOCK = 128


def _all_gather(shards):
    n_arr = len(shards)

    def body(*refs):
        x_refs, out_refs = refs[:n_arr], refs[n_arr:2 * n_arr]
        send_sems, recv_sems, local_sems = refs[2 * n_arr:]
        x, y, c = _my_place()
        me, sibling = (x, y, c), (x, y, 1 - c)
        chips = [(1 - x, y), (x, 1 - y), (1 - x, 1 - y)]

        def copy(a, k, block, to, from_input=False):
            slab = out_refs[a].at[4 * block[0] + 2 * block[1] + block[2]]
            return pltpu.make_async_remote_copy(
                src_ref=x_refs[a] if from_input else slab, dst_ref=slab,
                send_sem=send_sems.at[7 * a + k], recv_sem=recv_sems.at[7 * a + k], device_id=to,
                device_id_type=MESH_ID)

        arrays = range(n_arr)
        mine = [pltpu.make_async_copy(x_refs[a], out_refs[a].at[4 * x + 2 * y + c], local_sems.at[a]) for a in arrays]
        for cp in mine:
            cp.start()
        first = [copy(a, 0, me, sibling, True) for a in arrays]
        first += [copy(a, 1 + j, me, (*chip, c), True) for j, chip in enumerate(chips) for a in arrays]
        for cp in first:
            cp.start()
        passed = []
        for j, chip in enumerate(chips):
            for a in arrays:
                copy(a, 1 + j, (*chip, c), me).wait_recv()
                passed.append(copy(a, 4 + j, (*chip, c), sibling))
                passed[-1].start()
        for a in arrays:
            copy(a, 0, sibling, me).wait_recv()
        for j, chip in enumerate(chips):
            for a in arrays:
                copy(a, 4 + j, (*chip, 1 - c), me).wait_recv()
        for cp in first + passed:
            cp.wait_send()
        for cp in mine:
            cp.wait()

    anyspec = pl.BlockSpec(memory_space=pl.ANY)
    return pl.pallas_call(
        body, name="weights_all_gather",
        out_shape=[jax.ShapeDtypeStruct((N_DEV,) + s.shape, s.dtype) for s in shards],
        in_specs=[anyspec] * n_arr, out_specs=[anyspec] * n_arr,
        scratch_shapes=[pltpu.SemaphoreType.DMA((7 * n_arr,)), pltpu.SemaphoreType.DMA((7 * n_arr,)),
                        pltpu.SemaphoreType.DMA((n_arr,))],
    )(*shards)


def _small_exchange(slabs):
    def body(g_ref, recv_ref, send_sems, recv_sems, local_sem):
        _exchange(g_ref, recv_ref, send_sems, recv_sems, local_sem, True)
        _exchange(g_ref, recv_ref, send_sems, recv_sems, local_sem, False)

    vmem = pl.BlockSpec(memory_space=pltpu.VMEM)
    return pl.pallas_call(
        body, name="small_exchange", out_shape=jax.ShapeDtypeStruct(slabs.shape, slabs.dtype),
        in_specs=[vmem], out_specs=vmem, scratch_shapes=EXCHANGE_SEMS,
    )(slabs)


def _adamw(parts, w, m, v, block_rows, name):
    rows, cols = w.shape

    def body(p_ref, w_ref, m_ref, v_ref, g_out, d_out, m_out, v_out):
        g = p_ref[0].astype(F32)
        for s in range(1, N_DEV):
            g = g + p_ref[s].astype(F32)
        m_new = ADAM_B1 * m_ref[...] + (1.0 - ADAM_B1) * g
        v_new = ADAM_B2 * v_ref[...] + (1.0 - ADAM_B2) * (g * g)
        m_hat = m_new / (1.0 - ADAM_B1 ** ADAM_STEP)
        v_hat = v_new / (1.0 - ADAM_B2 ** ADAM_STEP)
        g_out[...] = g
        d_out[...] = -ADAM_LR * (m_hat / (jnp.sqrt(v_hat) + ADAM_EPS) + ADAM_WD * w_ref[...])
        m_out[...] = m_new
        v_out[...] = v_new

    spec = pl.BlockSpec((block_rows, cols), lambda i: (i, 0))
    return pl.pallas_call(
        body, name=name, grid=(pl.cdiv(rows, block_rows),),
        in_specs=[pl.BlockSpec((N_DEV, block_rows, cols), lambda i: (0, i, 0)), spec, spec, spec],
        out_specs=[spec] * 4, out_shape=[jax.ShapeDtypeStruct((rows, cols), F32)] * 4,
        compiler_params=_cp(("parallel",), 48),
    )(parts, w, m, v)


def _pad_rows(flat, rows):
    pad = rows * LANE - flat.shape[-1]
    flat = jnp.pad(flat, [(0, 0)] * (flat.ndim - 1) + [(0, pad)])
    return flat.reshape(flat.shape[:-1] + (rows, LANE))


def _pack_shards(shards):
    return _pad_rows(jnp.concatenate([shards[n].reshape(-1) for n, _, _ in PACKED]), PACK_ROWS)


def _unpack_shards(packed):
    flat, out, off = packed.reshape(-1), {}, 0
    for n, shape, _ in PACKED:
        size = shape[0] * shape[1]
        out[n] = flat[off:off + size].reshape(shape)
        off += size
    return out


def _split8(full, axis):
    r, c = full.shape
    if axis == 0:
        return full.reshape(N_DEV, r // N_DEV, c)
    return full.reshape(r, N_DEV, c // N_DEV).transpose(1, 0, 2)


def _join8(shards, axis):
    _, r, c = shards.shape
    if axis == 0:
        return shards.reshape(N_DEV * r, c)
    return shards.transpose(1, 0, 2).reshape(r, N_DEV * c)


def _pack_small(meta_shard, vals, loss_row):
    rows = jnp.concatenate([vals[n].reshape(-1, LANE) for n, _ in REPLICATED] + [loss_row], axis=0)
    rows = jnp.pad(rows, ((0, SMALL_ROWS - N_META - rows.shape[0]), (0, 0)))
    return jnp.concatenate([meta_shard, jnp.broadcast_to(rows, meta_shard.shape[:-2] + rows.shape)], axis=-2)


def _unpack_small(packed):
    out, off = {"meta_tokens": packed[:N_META]}, N_META
    for n, size in REPLICATED:
        out[n] = packed[off:off + size // LANE].reshape(1, size)
        off += size // LANE
    return out


def kernel(x, meta_tokens, norm_g, w_in, gla_gate_w, gla_gate_b, gla_norm_g, gla_proj, mla_q_norm_g, mla_w_uq, mla_kv_norm_g, mla_w_ukv, mla_proj, w_out, final_norm_g, loss_target, m_meta_tokens, m_norm_g, m_w_in, m_gla_gate_w, m_gla_gate_b, m_gla_norm_g, m_gla_proj, m_mla_q_norm_g, m_mla_w_uq, m_mla_kv_norm_g, m_mla_w_ukv, m_mla_proj, m_w_out, m_final_norm_g, v_meta_tokens, v_norm_g, v_w_in, v_gla_gate_w, v_gla_gate_b, v_gla_norm_g, v_gla_proj, v_mla_q_norm_g, v_mla_w_uq, v_mla_kv_norm_g, v_mla_w_ukv, v_mla_proj, v_w_out, v_final_norm_g):
    given = dict(meta_tokens=meta_tokens, norm_g=norm_g, w_in=w_in, gla_gate_w=gla_gate_w, gla_gate_b=gla_gate_b,
                 gla_norm_g=gla_norm_g, gla_proj=gla_proj, mla_q_norm_g=mla_q_norm_g, mla_w_uq=mla_w_uq,
                 mla_kv_norm_g=mla_kv_norm_g, mla_w_ukv=mla_w_ukv, mla_proj=mla_proj, w_out=w_out,
                 final_norm_g=final_norm_g)
    mom_m = dict(meta_tokens=m_meta_tokens, norm_g=m_norm_g, w_in=m_w_in, gla_gate_w=m_gla_gate_w,
                 gla_gate_b=m_gla_gate_b, gla_norm_g=m_gla_norm_g, gla_proj=m_gla_proj, mla_q_norm_g=m_mla_q_norm_g,
                 mla_w_uq=m_mla_w_uq, mla_kv_norm_g=m_mla_kv_norm_g, mla_w_ukv=m_mla_w_ukv, mla_proj=m_mla_proj,
                 w_out=m_w_out, final_norm_g=m_final_norm_g)
    mom_v = dict(meta_tokens=v_meta_tokens, norm_g=v_norm_g, w_in=v_w_in, gla_gate_w=v_gla_gate_w,
                 gla_gate_b=v_gla_gate_b, gla_norm_g=v_gla_norm_g, gla_proj=v_gla_proj, mla_q_norm_g=v_mla_q_norm_g,
                 mla_w_uq=v_mla_w_uq, mla_kv_norm_g=v_mla_kv_norm_g, mla_w_ukv=v_mla_w_ukv, mla_proj=v_mla_proj,
                 w_out=v_w_out, final_norm_g=v_final_norm_g)
    shapes = {n: a.shape for n, a in given.items()}
    shard2d = {n: s for n, s, _ in PACKED}
    shard2d["w_in"] = (D_MODEL, W_IN_SHARD)
    shard2d["meta_tokens"] = (N_META, LANE)

    def as2d(tree):
        out = {n: tree[n].reshape(shard2d[n]) for n in shard2d}
        out.update({n: tree[n].reshape(1, size) for n, size in REPLICATED})
        return out

    w_loc, m_loc, v_loc = as2d(given), as2d(mom_m), as2d(mom_v)

    w_in_all, meta_all = _all_gather([w_loc["w_in"].astype(BF16), w_loc["meta_tokens"]])
    flat = jnp.concatenate([w_loc[n].astype(BF16).reshape(-1) for n, _, _ in PACKED])
    full = {"w_in": w_in_all, "meta_tokens": _join8(meta_all, 1), "packed": _pad_rows(flat, PACK_ROWS)}
    for n, _ in REPLICATED:
        full[n] = w_loc[n]

    loss_part, grad_x, w_in_parts, packed_parts, small = _local_step(x, loss_target, full)
    small_all = _small_exchange(_pack_small(_split8(small["meta_tokens"], 1), small,
                                            jnp.broadcast_to(loss_part[:, :1], (1, LANE))))

    w_in_t = [t["w_in"].T for t in (w_loc, m_loc, v_loc)]
    g_w, d_w, m_w, v_w = (o.T for o in _adamw(w_in_parts, *w_in_t, W_IN_BLOCK, "adamw_w_in"))
    g_p, d_p, m_p, v_p = _adamw(packed_parts, _pack_shards(w_loc), _pack_shards(m_loc), _pack_shards(v_loc),
                                PACK_BLOCK, "adamw_packed")
    zero_row = jnp.zeros((1, LANE), F32)
    g_s, d_s, m_s, v_s = _adamw(small_all, *(_pack_small(t["meta_tokens"], t, zero_row) for t in (w_loc, m_loc, v_loc)),
                                SMALL_ROWS, "adamw_small")
    loss = g_s[LOSS_ROW, 0]

    order = ["meta_tokens", "norm_g", "w_in", "gla_gate_w", "gla_gate_b", "gla_norm_g", "gla_proj", "mla_q_norm_g",
             "mla_w_uq", "mla_kv_norm_g", "mla_w_ukv", "mla_proj", "w_out", "final_norm_g"]
    result = [loss, grad_x]
    for w_in_out, packed_sh, packed_sm in ((g_w, g_p, g_s), (d_w, d_p, d_s), (m_w, m_p, m_s), (v_w, v_p, v_s)):
        tree = _unpack_shards(packed_sh)
        tree.update(_unpack_small(packed_sm))
        tree["w_in"] = w_in_out
        result += [tree[n].reshape(shapes[n]) for n in order]
    return tuple(result)
```

```python
import jax
import jax.numpy as jnp
from jax import lax
from jax.experimental import pallas as pl
from jax.experimental.pallas import tpu as pltpu

F32 = jnp.float32
BF16 = jnp.bfloat16

D_MODEL = 1024
N_META = 16
EPS = 1e-6
FRONT = 48
X0 = FRONT + N_META
GLA_HEADS, GLA_DK, GLA_DV, GLA_RANK, GLA_CHUNK = 4, 128, 256, 16, 64
GLA_GATE_NORMALIZER = 16.0
GLA_KW = GLA_HEADS * GLA_DK
GLA_VW = GLA_HEADS * GLA_DV
MLA_HEADS, MLA_NOPE, MLA_ROPE, MLA_DV, MLA_QR, MLA_KVR = 8, 128, 64, 128, 256, 128
MLA_QK = MLA_NOPE + MLA_ROPE
ROPE_BASE = 10000.0
LANE = 128
QKW = 2 * LANE

C_V, C_Z, C_Q, C_K = 0, 1024, 2048, 2560
C_MZ, C_GG, C_GM = 3072, 4096, 5120
C_CKV, C_KR, C_KROT, C_LR = 6144, 6272, 6400, 6528
C_CQ = 6656
N_EXT = 6912
O_Q, O_K, O_V, O_LR, O_Z, O_CQ, O_CKV, O_KR, O_MZ, O_GG, O_GM, N_IN = (
    0, 512, 1024, 2048, 2064, 3088, 3344, 3472, 3536, 4560, 5584, 6608)

ADAM_LR, ADAM_B1, ADAM_B2, ADAM_EPS, ADAM_WD, ADAM_STEP = 0.001, 0.9, 0.999, 1e-08, 0.01, 10

N_DEV = 8
TOK = 192
ATT_BLOCK = 352
EXT_BLOCK = 1152
MXU_DEPTH = 256


def _cp(sems=None, vmem_mb=None):
    kw = {}
    if sems is not None:
        kw["dimension_semantics"] = sems
    if vmem_mb is not None:
        kw["vmem_limit_bytes"] = vmem_mb * 1024 * 1024
    return pltpu.CompilerParams(**kw)


def _dot(a, b):
    return jnp.dot(a, b, preferred_element_type=F32)


def _dot_nt(a, b):
    return lax.dot_general(a, b, (((1,), (1,)), ((), ())), preferred_element_type=F32)


def _dot_tn(a, b):
    return lax.dot_general(a, b, (((0,), (0,)), ((), ())), preferred_element_type=F32)


def _sigmoid(x):
    return 1.0 / (1.0 + jnp.exp(-x))


def _bf(x):
    return x.astype(BF16)


def _big_tok(tp):
    return 4 * TOK if tp % (4 * TOK) == 0 else TOK


def _attn_block(lp):
    return ATT_BLOCK if lp % ATT_BLOCK == 0 else TOK


def _proj_in(hp, norm_g, w_ext, packed):
    tp = hp.shape[0]
    tm = 2 * TOK
    ni = tp // tm

    def body(h_ref, g_ref, w_ref, p_ref, u_ref, o_ref, pall_ref, send_sems, recv_sems, local_sem):
        i = pl.program_id(0)

        @pl.when(i == 0)
        def _():
            _exchange(p_ref, pall_ref, send_sems, recv_sems, local_sem, True, same=True)

        x = h_ref[...]
        r = lax.rsqrt(jnp.mean(x * x, axis=-1, keepdims=True) + EPS)
        u = _bf(x * r * g_ref[...])
        u_ref[...] = u
        o_ref[...] = _bf(_dot(u, w_ref[...]))

        @pl.when(i == ni - 1)
        def _():
            _exchange(p_ref, pall_ref, send_sems, recv_sems, local_sem, False, same=True)

    anyspec = pl.BlockSpec(memory_space=pl.ANY)
    return pl.pallas_call(
        body, name="proj_in", grid=(ni,),
        in_specs=[pl.BlockSpec((tm, D_MODEL), lambda i: (i, 0)),
                  pl.BlockSpec((1, D_MODEL), lambda i: (0, 0)),
                  pl.BlockSpec((D_MODEL, N_EXT), lambda i: (0, 0), pipeline_mode=pl.Buffered(1)), anyspec],
        out_specs=[pl.BlockSpec((tm, D_MODEL), lambda i: (i, 0)),
                   pl.BlockSpec((tm, N_EXT), lambda i: (i, 0)), anyspec],
        out_shape=[jax.ShapeDtypeStruct((tp, D_MODEL), BF16), jax.ShapeDtypeStruct((tp, N_EXT), BF16),
                   jax.ShapeDtypeStruct((N_DEV,) + packed.shape, packed.dtype)],
        scratch_shapes=EXCHANGE_SEMS,
        compiler_params=_cp(("arbitrary",), 56),
    )(hp, norm_g, w_ext, packed)


GLA_GROUP = 3
GLA_ROWS = GLA_GROUP * GLA_CHUNK


def _tri_dot(tri, x):
    hi = _bf(x)
    rest = x - hi.astype(F32)
    mid = _bf(rest)
    return _dot(tri, hi) + _dot(tri, mid) + _dot(tri, _bf(rest - mid.astype(F32)))


def _gla_gates(q_ref, k_ref, lr_ref, gw_ref, gb_ref, rows, not_first):
    z = _dot(lr_ref[rows, :], gw_ref[...]) + gb_ref[...]
    logsig = jnp.minimum(z, 0.0) - jnp.log(1.0 + jnp.exp(-jnp.abs(z)))
    row = lax.broadcasted_iota(jnp.int32, (GLA_CHUNK, GLA_KW), 0)
    live = jnp.logical_or(not_first, row >= FRONT)
    g = jnp.where(live, logsig * (1.0 / GLA_GATE_NORMALIZER), 0.0)
    ri = lax.broadcasted_iota(jnp.int32, (GLA_CHUNK, GLA_CHUNK), 0)
    ci = lax.broadcasted_iota(jnp.int32, (GLA_CHUNK, GLA_CHUNK), 1)
    tril = ci <= ri
    b = _tri_dot(_bf(tril.astype(F32)), g)
    bl = jnp.sum(jnp.where(row == GLA_CHUNK - 1, b, 0.0), axis=0, keepdims=True)
    eb, enb, elb, ebl = jnp.exp(b), jnp.exp(-b), jnp.exp(bl - b), jnp.exp(bl)
    q = q_ref[rows, :].astype(F32) * (GLA_DK ** -0.5)
    k = k_ref[rows, :].astype(F32)
    qe, ke, kl = q * eb, k * enb, k * elb
    return dict(z=z, live=live, tril=tril, row=row, eb=eb, enb=enb, elb=elb, ebl=ebl, qe=qe, ke=ke, kl=kl,
                qe_b=_bf(qe), ke_b=_bf(ke), kl_b=_bf(kl))


def _gla_in_specs(n_groups, rev):
    def rb(b, n):
        return b * n_groups + ((n_groups - 1 - n) if rev else n)

    return rb, [pl.BlockSpec((GLA_ROWS, GLA_KW), lambda b, n: (rb(b, n), C_Q // GLA_KW)),
                pl.BlockSpec((GLA_ROWS, GLA_KW), lambda b, n: (rb(b, n), C_K // GLA_KW)),
                pl.BlockSpec((GLA_ROWS, GLA_VW), lambda b, n: (rb(b, n), C_V // GLA_VW)),
                pl.BlockSpec((GLA_ROWS, GLA_VW), lambda b, n: (rb(b, n), C_Z // GLA_VW)),
                pl.BlockSpec((GLA_ROWS, LANE), lambda b, n: (rb(b, n), C_LR // LANE)),
                pl.BlockSpec((LANE, GLA_KW), lambda b, n: (0, 0)),
                pl.BlockSpec((1, GLA_KW), lambda b, n: (0, 0)),
                pl.BlockSpec((1, GLA_DV), lambda b, n: (0, 0))]


def _gla_fwd(proj, gw_pad, gate_b, gla_norm_g, bsz, lp):
    n_chunks = lp // GLA_CHUNK
    n_groups = n_chunks // GLA_GROUP
    tp = bsz * lp

    def body(q_ref, k_ref, v_ref, z_ref, lr_ref, gw_ref, gb_ref, gn_ref, oraw_ref, ya_ref, sall_ref, st_scr):
        grp = pl.program_id(1)

        @pl.when(grp == 0)
        def _():
            st_scr[...] = jnp.zeros_like(st_scr)

        chunks = [slice(j * GLA_CHUNK, (j + 1) * GLA_CHUNK) for j in range(GLA_GROUP)]
        cs = [_gla_gates(q_ref, k_ref, lr_ref, gw_ref, gb_ref, rows, True if j else grp > 0)
              for j, rows in enumerate(chunks)]
        gn = gn_ref[...]
        sts = [st_scr[h] for h in range(GLA_HEADS)]
        for j, (rows, c) in enumerate(zip(chunks, cs)):
            for h in range(GLA_HEADS):
                ks, vs = slice(h * GLA_DK, (h + 1) * GLA_DK), slice(h * GLA_DV, (h + 1) * GLA_DV)
                st = sts[h]
                sall_ref[0, j, h] = st
                v = v_ref[rows, vs]
                a = jnp.where(c["tril"], _dot_nt(c["qe_b"][:, ks], c["ke_b"][:, ks]), 0.0)
                o = _dot(_bf(a), v) + _dot_nt(c["qe_b"][:, ks], _bf(st))
                sts[h] = st * c["ebl"][:, ks] + _dot_tn(v, c["kl_b"][:, ks])
                oraw_ref[rows, vs] = o
                r = lax.rsqrt(jnp.mean(o * o, axis=-1, keepdims=True) + EPS)
                zg = z_ref[rows, vs].astype(F32)
                ya_ref[rows, vs] = _bf((o * r * gn) * (zg * _sigmoid(zg)))
        for h in range(GLA_HEADS):
            st_scr[h] = sts[h]

    rb, in_specs = _gla_in_specs(n_groups, False)
    return pl.pallas_call(
        body, name="gla_fwd", grid=(bsz, n_groups), in_specs=in_specs,
        out_specs=[pl.BlockSpec((GLA_ROWS, GLA_VW), lambda b, n: (rb(b, n), 0)),
                   pl.BlockSpec((GLA_ROWS, GLA_VW), lambda b, n: (rb(b, n), 0)),
                   pl.BlockSpec((1, GLA_GROUP, GLA_HEADS, GLA_DV, GLA_DK), lambda b, n: (b, n, 0, 0, 0))],
        out_shape=[jax.ShapeDtypeStruct((tp, GLA_VW), F32), jax.ShapeDtypeStruct((tp, GLA_VW), BF16),
                   jax.ShapeDtypeStruct((bsz, n_chunks, GLA_HEADS, GLA_DV, GLA_DK), F32)],
        scratch_shapes=[pltpu.VMEM((GLA_HEADS, GLA_DV, GLA_DK), F32)],
        compiler_params=_cp(("parallel", "arbitrary")),
    )(proj, proj, proj, proj, proj, gw_pad, gate_b, gla_norm_g)


def _gla_bwd(proj, gw_pad, gate_b, gla_norm_g, o_raw, s_all, d_ya, dproj, bsz, lp):
    n_chunks = lp // GLA_CHUNK
    n_groups = n_chunks // GLA_GROUP
    tp = bsz * lp

    def body(q_ref, k_ref, v_ref, z_ref, lr_ref, gw_ref, gb_ref, gn_ref, o_ref, s_ref, dya_ref, _,
             dp_ref, dz_ref, dgn_ref, dst_scr):
        dv_ref, dzg_ref = dp_ref.at[:, C_V:C_V + GLA_VW], dp_ref.at[:, C_Z:C_Z + GLA_VW]

        @pl.when(jnp.logical_and(pl.program_id(0) == 0, pl.program_id(1) == 0))
        def _():
            dgn_ref[...] = jnp.zeros_like(dgn_ref)

        @pl.when(pl.program_id(1) == 0)
        def _():
            dst_scr[...] = jnp.zeros_like(dst_scr)

        grp = n_groups - 1 - pl.program_id(1)
        chunks = [slice(j * GLA_CHUNK, (j + 1) * GLA_CHUNK) for j in range(GLA_GROUP)]
        cs = [_gla_gates(q_ref, k_ref, lr_ref, gw_ref, gb_ref, rows, True if j else grp > 0)
              for j, rows in enumerate(chunks)]
        gn = gn_ref[...]
        dgn = jnp.zeros((1, GLA_DV), F32)
        dqe_h, dke_h, dkl_h, dbl_h = ([[None] * GLA_HEADS for _ in chunks] for _ in range(4))
        dsts = [dst_scr[h] for h in range(GLA_HEADS)]
        for j in reversed(range(GLA_GROUP)):
            rows, c = chunks[j], cs[j]
            for h in range(GLA_HEADS):
                ks, vs = slice(h * GLA_DK, (h + 1) * GLA_DK), slice(h * GLA_DV, (h + 1) * GLA_DV)
                dst = dsts[h]
                v = v_ref[rows, vs]
                st = s_ref[0, j, h]
                o = o_ref[rows, vs]
                r = lax.rsqrt(jnp.mean(o * o, axis=-1, keepdims=True) + EPS)
                xh = o * r
                zg = z_ref[rows, vs].astype(F32)
                sg = _sigmoid(zg)
                dy = dya_ref[rows, vs].astype(F32)
                dzg_ref[rows, vs] = _bf(dy * (xh * gn) * (sg * (1.0 + zg * (1.0 - sg))))
                t = dy * (zg * sg)
                dgn += jnp.sum(t * xh, axis=0, keepdims=True)
                dxh = t * gn
                do_b = _bf(r * (dxh - xh * jnp.mean(dxh * xh, axis=-1, keepdims=True)))
                qe_b, ke_b, kl_b, dst_b = c["qe_b"][:, ks], c["ke_b"][:, ks], c["kl_b"][:, ks], _bf(dst)
                a = jnp.where(c["tril"], _dot_nt(qe_b, ke_b), 0.0)
                da_b = _bf(jnp.where(c["tril"], _dot_nt(do_b, v), 0.0))
                dqe_h[j][h] = _dot(da_b, ke_b) + _dot(do_b, _bf(st))
                dke_h[j][h] = _dot_tn(da_b, qe_b)
                dkl = _dot(v, dst_b)
                dkl_h[j][h] = dkl
                dv_ref[rows, vs] = _bf(_dot_tn(_bf(a), do_b) + _dot_nt(kl_b, dst_b))
                ddecay = jnp.sum(dst * st, axis=0, keepdims=True)
                dbl_h[j][h] = jnp.sum(dkl * c["kl"][:, ks], axis=0, keepdims=True) + ddecay * c["ebl"][:, ks]
                dsts[h] = dst * c["ebl"][:, ks] + _dot_tn(do_b, qe_b)
        for h in range(GLA_HEADS):
            dst_scr[h] = dsts[h]
        dgn_ref[...] += dgn
        ri = lax.broadcasted_iota(jnp.int32, (GLA_CHUNK, GLA_CHUNK), 0)
        ci = lax.broadcasted_iota(jnp.int32, (GLA_CHUNK, GLA_CHUNK), 1)
        triu = _bf((ci >= ri).astype(F32))
        for j, (rows, c) in enumerate(zip(chunks, cs)):
            dqe, dke, dkl, dbl = (jnp.concatenate(p[j], axis=1) for p in (dqe_h, dke_h, dkl_h, dbl_h))
            db = dqe * c["qe"] - dke * c["ke"] - dkl * c["kl"] + jnp.where(c["row"] == GLA_CHUNK - 1, dbl, 0.0)
            dg = _tri_dot(triu, db)
            dg = jnp.where(c["live"], dg, 0.0)
            dz_ref[rows, :] = dg * (1.0 / GLA_GATE_NORMALIZER) * _sigmoid(-c["z"])
            dp_ref[rows, C_Q:C_Q + GLA_KW] = _bf(dqe * c["eb"] * (GLA_DK ** -0.5))
            dp_ref[rows, C_K:C_K + GLA_KW] = _bf(dke * c["enb"] + dkl * c["elb"])

    rb, in_specs = _gla_in_specs(n_groups, True)
    wide = pl.BlockSpec((GLA_ROWS, GLA_VW), lambda b, n: (rb(b, n), 0))
    group = C_MZ
    return pl.pallas_call(
        body, name="gla_bwd", grid=(bsz, n_groups),
        in_specs=in_specs + [wide, pl.BlockSpec((1, GLA_GROUP, GLA_HEADS, GLA_DV, GLA_DK),
                                                lambda b, n: (b, n_groups - 1 - n, 0, 0, 0)), wide,
                             pl.BlockSpec(memory_space=pl.ANY)],
        out_specs=[pl.BlockSpec((GLA_ROWS, group), lambda b, n: (rb(b, n), 0)),
                   pl.BlockSpec((GLA_ROWS, GLA_KW), lambda b, n: (rb(b, n), 0)),
                   pl.BlockSpec((1, GLA_DV), lambda b, n: (0, 0))],
        out_shape=[jax.ShapeDtypeStruct((tp, N_EXT), BF16), jax.ShapeDtypeStruct((tp, GLA_KW), F32),
                   jax.ShapeDtypeStruct((1, GLA_DV), F32)],
        input_output_aliases={11: 0},
        scratch_shapes=[pltpu.VMEM((GLA_HEADS, GLA_DV, GLA_DK), F32)],
        compiler_params=_cp(("arbitrary", "arbitrary")),
    )(proj, proj, proj, proj, proj, gw_pad, gate_b, gla_norm_g, o_raw, s_all, d_ya, dproj)


def _gate_bwd(dz, proj, gw_pad):
    tp = dz.shape[0]
    tm = _big_tok(tp)

    def body(dz_ref, lr_ref, gw_ref, dlr_ref, dgw_ref, dgb_ref):
        @pl.when(pl.program_id(0) == 0)
        def _():
            dgw_ref[...] = jnp.zeros_like(dgw_ref)
            dgb_ref[...] = jnp.zeros_like(dgb_ref)

        dz = dz_ref[...]
        dz_b = _bf(dz)
        dlr_ref[...] = _bf(_dot_nt(dz_b, gw_ref[...]))
        dgw_ref[...] += _dot_tn(lr_ref[...], dz_b)
        dgb_ref[...] += jnp.sum(dz, axis=0, keepdims=True)

    return pl.pallas_call(
        body, name="gate_bwd", grid=(tp // tm,),
        in_specs=[pl.BlockSpec((tm, GLA_KW), lambda i: (i, 0)),
                  pl.BlockSpec((tm, LANE), lambda i: (i, C_LR // LANE)),
                  pl.BlockSpec((LANE, GLA_KW), lambda i: (0, 0))],
        out_specs=[pl.BlockSpec((tm, LANE), lambda i: (i, 0)),
                   pl.BlockSpec((LANE, GLA_KW), lambda i: (0, 0)),
                   pl.BlockSpec((1, GLA_KW), lambda i: (0, 0))],
        out_shape=[jax.ShapeDtypeStruct((tp, LANE), BF16), jax.ShapeDtypeStruct((LANE, GLA_KW), F32),
                   jax.ShapeDtypeStruct((1, GLA_KW), F32)],
        compiler_params=_cp(("arbitrary",)),
    )(dz, proj, gw_pad)


def _rms_fwd(x):
    r = lax.rsqrt(jnp.mean(x * x, axis=-1, keepdims=True) + EPS)
    return x * r, r


def _rms_bwd(dy, xh, r, g):
    dxh = dy * g
    dx = r * (dxh - xh * jnp.mean(dxh * xh, axis=-1, keepdims=True))
    return dx, jnp.sum(dy * xh, axis=0, keepdims=True)


def _q_up(proj, q_norm_g, wn, wr, wt, cos_t, sin_t, bsz, lp):
    tp = bsz * lp
    tok = _attn_block(lp)
    nb = lp // tok

    def body(cq_ref, g_ref, wn_ref, wr_ref, wt_ref, cos_ref, sin_ref, q_ref):
        xh, _ = _rms_fwd(cq_ref[...].astype(F32))
        cqn = _bf(xh * g_ref[...])
        nope = _dot(cqn, wn_ref[...])
        rope = _dot(cqn, wr_ref[...])
        rot = _dot(cqn, wt_ref[...])
        cos, sin = cos_ref[...], sin_ref[...]
        one = (lax.broadcasted_iota(jnp.int32, (tok, LANE), 1) == BIAS_LANE).astype(F32)
        for h in range(MLA_HEADS):
            sl = slice(h * LANE, (h + 1) * LANE)
            q_ref[:, h * QKW:h * QKW + LANE] = _bf(nope[:, sl])
            q_ref[:, h * QKW + LANE:(h + 1) * QKW] = _bf(rope[:, sl] * cos + rot[:, sl] * sin + one)

    wspec = pl.BlockSpec((MLA_QR, MLA_HEADS * LANE), lambda b, i: (0, 0))
    tspec = pl.BlockSpec((tok, LANE), lambda b, i: (i, 0))
    return pl.pallas_call(
        body, name="mla_q_up", grid=(bsz, nb),
        in_specs=[pl.BlockSpec((tok, MLA_QR), lambda b, i: (b * nb + i, C_CQ // MLA_QR)),
                  pl.BlockSpec((1, MLA_QR), lambda b, i: (0, 0)), wspec, wspec, wspec, tspec, tspec],
        out_specs=pl.BlockSpec((tok, MLA_HEADS * QKW), lambda b, i: (b * nb + i, 0)),
        out_shape=jax.ShapeDtypeStruct((tp, MLA_HEADS * QKW), BF16),
        compiler_params=_cp(("parallel", "parallel")),
    )(proj, q_norm_g, wn, wr, wt, cos_t, sin_t)


def _kv_up(proj, kv_norm_g, wk, wv, cos_t, sin_t, bsz, lp):
    tp = bsz * lp
    tok = _attn_block(lp)
    nb = lp // tok

    def body(ckv_ref, kr_ref, krot_ref, g_ref, wk_ref, wv_ref, cos_ref, sin_ref, k_ref, v_ref):
        xh, _ = _rms_fwd(ckv_ref[...].astype(F32))
        cn = _bf(xh * g_ref[...])
        kn = _dot(cn, wk_ref[...])
        v_ref[...] = _bf(_dot(cn, wv_ref[...]))
        pos = pl.program_id(1) * tok + lax.broadcasted_iota(jnp.int32, (tok, LANE), 0)
        lane = lax.broadcasted_iota(jnp.int32, (tok, LANE), 1)
        bias = jnp.where(jnp.logical_and(lane == BIAS_LANE, pos < FRONT), KEY_BIAS, 0.0)
        kr = _bf(kr_ref[...].astype(F32) * cos_ref[...] + krot_ref[...].astype(F32) * sin_ref[...] + bias)
        for h in range(MLA_HEADS):
            k_ref[:, h * QKW:h * QKW + LANE] = _bf(kn[:, h * LANE:(h + 1) * LANE])
            k_ref[:, h * QKW + LANE:(h + 1) * QKW] = kr

    wspec = pl.BlockSpec((MLA_KVR, MLA_HEADS * LANE), lambda b, i: (0, 0))
    tspec = pl.BlockSpec((tok, LANE), lambda b, i: (i, 0))
    return pl.pallas_call(
        body, name="mla_kv_up", grid=(bsz, nb),
        in_specs=[pl.BlockSpec((tok, LANE), lambda b, i: (b * nb + i, C_CKV // LANE)),
                  pl.BlockSpec((tok, LANE), lambda b, i: (b * nb + i, C_KR // LANE)),
                  pl.BlockSpec((tok, LANE), lambda b, i: (b * nb + i, C_KROT // LANE)),
                  pl.BlockSpec((1, MLA_KVR), lambda b, i: (0, 0)), wspec, wspec, tspec, tspec],
        out_specs=[pl.BlockSpec((tok, MLA_HEADS * QKW), lambda b, i: (b * nb + i, 0)),
                   pl.BlockSpec((tok, MLA_HEADS * LANE), lambda b, i: (b * nb + i, 0))],
        out_shape=[jax.ShapeDtypeStruct((tp, MLA_HEADS * QKW), BF16),
                   jax.ShapeDtypeStruct((tp, MLA_HEADS * LANE), BF16)],
        compiler_params=_cp(("parallel", "parallel")),
    )(proj, proj, proj, kv_norm_g, wk, wv, cos_t, sin_t)


ATT_SCALE = MLA_QK ** -0.5


KEY_BIAS = -1e30
BIAS_LANE = MLA_ROPE
NEG = 2 * KEY_BIAS
LOG2E = 1.4426950408889634
EXP2_SCALE = ATT_SCALE * LOG2E


def _causal_fill(s, r0, fill):
    tq, kmax = s.shape
    a = r0 // LANE * LANE
    mask = (a + lax.broadcasted_iota(jnp.int32, (tq, kmax - a), 1)
            <= r0 + lax.broadcasted_iota(jnp.int32, (tq, kmax - a), 0))
    right = jnp.where(mask, s[:, a:], fill)
    return jnp.concatenate([s[:, :a], right], axis=1) if a else right


def _attn_fwd(qf, kf, vf, proj, bsz, lp):
    tp = bsz * lp
    tq = _attn_block(lp)
    nh = 2

    def body(q_ref, k_ref, v_ref, mz_ref, ob_ref, yb_ref, lse_ref):
        starts = list(range(0, lp, tq))
        for pair in (starts[i:i + 2] for i in range(0, len(starts), 2)):
            work = [(r0, h) for r0 in pair for h in range(nh)]
            ss = [_causal_fill(_dot_nt(q_ref[r0:r0 + tq, h * QKW:(h + 1) * QKW],
                                       k_ref[0:r0 + tq, h * QKW:(h + 1) * QKW]), r0, NEG) for r0, h in work]
            ms = [jnp.max(s, axis=-1, keepdims=True) for s in ss]
            ps = [jnp.exp2((s - m) * EXP2_SCALE) for s, m in zip(ss, ms)]
            ls = [jnp.sum(p, axis=-1, keepdims=True) for p in ps]
            for (r0, h), p, m, l in zip(work, ps, ms, ls):
                rows, cols = slice(r0, r0 + tq), slice(h * MLA_DV, (h + 1) * MLA_DV)
                o = _dot(_bf(p), v_ref[0:r0 + tq, cols]) / l
                ob_ref[rows, cols] = _bf(o)
                mz = mz_ref[rows, cols].astype(F32)
                yb_ref[rows, cols] = _bf(o * (mz * _sigmoid(mz)))
                lse_ref[0, h, rows, :] = jnp.broadcast_to(m * EXP2_SCALE + jnp.log2(l), (tq, LANE))

    head = lambda off: pl.BlockSpec((lp, nh * MLA_DV), lambda b, h: (b, off + h))
    wide = pl.BlockSpec((lp, nh * QKW), lambda b, h: (b, h))
    return pl.pallas_call(
        body, name="mla_attn_fwd", grid=(bsz, MLA_HEADS // nh),
        in_specs=[wide, wide, head(0), head(C_MZ // (nh * MLA_DV))],
        out_specs=[head(0), head(0), pl.BlockSpec((1, nh, lp, LANE), lambda b, h: (b, h, 0, 0))],
        out_shape=[jax.ShapeDtypeStruct((tp, MLA_HEADS * MLA_DV), BF16),
                   jax.ShapeDtypeStruct((tp, MLA_HEADS * MLA_DV), BF16),
                   jax.ShapeDtypeStruct((bsz, MLA_HEADS, lp, LANE), F32)],
        compiler_params=_cp(("parallel", "parallel"), 56),
    )(qf, kf, vf, proj)


def _attn_bwd(qf, kf, vf, d_o, lse, delta, bsz, lp):
    tp = bsz * lp
    tq = _attn_block(lp)

    def body(q_ref, k_ref, v_ref, do_ref, lse_ref, dl_ref, dq_ref, dk_ref, dv_ref, dk_acc, dv_acc):
        dk_acc[...] = jnp.zeros_like(dk_acc)
        dv_acc[...] = jnp.zeros_like(dv_acc)
        for r0 in range(0, lp, tq):
            rows, kmax = slice(r0, r0 + tq), r0 + tq
            q, do = q_ref[rows, :], do_ref[rows, :]
            k, v = k_ref[0:kmax, :], v_ref[0:kmax, :]
            p = jnp.exp2(_dot_nt(q, k) * EXP2_SCALE - lse_ref[0, 0, rows, :][:, :1])
            p = _causal_fill(p, r0, 0.0)
            ds = _bf(p * (_dot_nt(do, v) - dl_ref[0, rows, :][:, :1]))
            dq_ref[rows, :] = _bf(_dot(ds, k) * ATT_SCALE)
            dk_acc[0:kmax, :] += _dot_tn(ds, q)
            dv_acc[0:kmax, :] += _dot_tn(_bf(p), do)
        dk_ref[...] = _bf(dk_acc[...] * ATT_SCALE)
        dv_ref[...] = _bf(dv_acc[...])

    wide = pl.BlockSpec((lp, QKW), lambda b, h: (b, h))
    narrow = pl.BlockSpec((lp, MLA_DV), lambda b, h: (b, h))
    stat = pl.BlockSpec((1, 1, lp, LANE), lambda b, h: (b, h, 0, 0))
    return pl.pallas_call(
        body, name="mla_attn_bwd", grid=(bsz, MLA_HEADS),
        in_specs=[wide, wide, narrow, narrow, stat, pl.BlockSpec((1, lp, LANE), lambda b, h: (h, b, 0))],
        out_specs=[wide, wide, narrow],
        out_shape=[jax.ShapeDtypeStruct((tp, MLA_HEADS * QKW), BF16), jax.ShapeDtypeStruct((tp, MLA_HEADS * QKW), BF16),
                   jax.ShapeDtypeStruct((tp, MLA_HEADS * MLA_DV), BF16)],
        scratch_shapes=[pltpu.VMEM((lp, QKW), F32), pltpu.VMEM((lp, MLA_DV), F32)],
        compiler_params=_cp(("parallel", "parallel"), 56),
    )(qf, kf, vf, d_o, lse, delta)


def _q_up_bwd(dqf, proj, q_norm_g, wn, wr, wt, cos_t, sin_t, dproj, bsz, lp):
    tp = bsz * lp
    tok = _attn_block(lp)
    nb = lp // tok
    hw = MLA_HEADS * LANE

    def body(dq_ref, cq_ref, g_ref, wn_ref, wr_ref, wt_ref, cos_ref, sin_ref, _,
             dcq_ref, dwn_ref, dwr_ref, dwt_ref, dg_ref):
        @pl.when(jnp.logical_and(pl.program_id(0) == 0, pl.program_id(1) == 0))
        def _():
            for r in (dwn_ref, dwr_ref, dwt_ref, dg_ref):
                r[...] = jnp.zeros_like(r)

        g = g_ref[...]
        xh, r = _rms_fwd(cq_ref[...].astype(F32))
        cqn = _bf(xh * g)
        dn = jnp.concatenate([dq_ref[:, h * QKW:h * QKW + LANE] for h in range(MLA_HEADS)], axis=1)
        dr = jnp.concatenate([dq_ref[:, h * QKW + LANE:(h + 1) * QKW] for h in range(MLA_HEADS)], axis=1).astype(F32)
        dr_c = _bf(dr * jnp.tile(cos_ref[...], (1, MLA_HEADS)))
        dr_s = _bf(dr * jnp.tile(sin_ref[...], (1, MLA_HEADS)))
        dcqn = _dot_nt(dn, wn_ref[...]) + _dot_nt(dr_c, wr_ref[...]) + _dot_nt(dr_s, wt_ref[...])
        dwn_ref[...] += _dot_tn(cqn, dn)
        dwr_ref[...] += _dot_tn(cqn, dr_c)
        dwt_ref[...] += _dot_tn(cqn, dr_s)
        dx, dg = _rms_bwd(dcqn, xh, r, g)
        dcq_ref[...] = _bf(dx)
        dg_ref[...] += dg

    aspec = pl.BlockSpec((MLA_QR, hw), lambda b, i: (0, 0))
    tspec = pl.BlockSpec((tok, LANE), lambda b, i: (i, 0))
    return pl.pallas_call(
        body, name="mla_q_up_bwd", grid=(bsz, nb),
        in_specs=[pl.BlockSpec((tok, MLA_HEADS * QKW), lambda b, i: (b * nb + i, 0)),
                  pl.BlockSpec((tok, MLA_QR), lambda b, i: (b * nb + i, C_CQ // MLA_QR)),
                  pl.BlockSpec((1, MLA_QR), lambda b, i: (0, 0)), aspec, aspec, aspec, tspec, tspec,
                  pl.BlockSpec(memory_space=pl.ANY)],
        out_specs=[pl.BlockSpec((tok, MLA_QR), lambda b, i: (b * nb + i, C_CQ // MLA_QR)), aspec, aspec, aspec,
                   pl.BlockSpec((1, MLA_QR), lambda b, i: (0, 0))],
        out_shape=[jax.ShapeDtypeStruct((tp, N_EXT), BF16)] + [jax.ShapeDtypeStruct((MLA_QR, hw), F32)] * 3
        + [jax.ShapeDtypeStruct((1, MLA_QR), F32)],
        input_output_aliases={8: 0},
        compiler_params=_cp(("arbitrary", "arbitrary")),
    )(dqf, proj, q_norm_g, wn, wr, wt, cos_t, sin_t, dproj)


def _kv_up_bwd(dkf, dvf, proj, kv_norm_g, wk, wv, cos_t, sin_t, d_lr, dproj, bsz, lp):
    tp = bsz * lp
    tok = _attn_block(lp)
    nb = lp // tok
    hw = MLA_HEADS * LANE

    def body(dk_ref, dv_ref, ckv_ref, g_ref, wk_ref, wv_ref, cos_ref, sin_ref, dlr_ref, _,
             dp_ref, dwk_ref, dwv_ref, dg_ref):
        dckv_ref, dkr_ref, dkrot_ref = (dp_ref.at[:, j * LANE:(j + 1) * LANE] for j in range(3))
        dp_ref[:, 3 * LANE:] = dlr_ref[...]
        @pl.when(jnp.logical_and(pl.program_id(0) == 0, pl.program_id(1) == 0))
        def _():
            for r in (dwk_ref, dwv_ref, dg_ref):
                r[...] = jnp.zeros_like(r)

        g = g_ref[...]
        xh, r = _rms_fwd(ckv_ref[...].astype(F32))
        cn = _bf(xh * g)
        dv = dv_ref[...]
        dn = jnp.concatenate([dk_ref[:, h * QKW:h * QKW + LANE] for h in range(MLA_HEADS)], axis=1)
        dcn = _dot_nt(dv, wv_ref[...]) + _dot_nt(dn, wk_ref[...])
        dwv_ref[...] += _dot_tn(cn, dv)
        dwk_ref[...] += _dot_tn(cn, dn)
        drope = jnp.zeros((tok, LANE), F32)
        for h in range(MLA_HEADS):
            drope += dk_ref[:, h * QKW + LANE:(h + 1) * QKW].astype(F32)
        dkr_ref[...] = _bf(drope * cos_ref[...])
        dkrot_ref[...] = _bf(drope * sin_ref[...])
        dx, dg = _rms_bwd(dcn, xh, r, g)
        dckv_ref[...] = _bf(dx)
        dg_ref[...] += dg

    aspec = pl.BlockSpec((MLA_KVR, hw), lambda b, i: (0, 0))
    tspec = pl.BlockSpec((tok, LANE), lambda b, i: (i, 0))
    ospec = pl.BlockSpec((tok, LANE), lambda b, i: (b * nb + i, 0))
    return pl.pallas_call(
        body, name="mla_kv_up_bwd", grid=(bsz, nb),
        in_specs=[pl.BlockSpec((tok, MLA_HEADS * QKW), lambda b, i: (b * nb + i, 0)),
                  pl.BlockSpec((tok, hw), lambda b, i: (b * nb + i, 0)),
                  pl.BlockSpec((tok, LANE), lambda b, i: (b * nb + i, C_CKV // LANE)),
                  pl.BlockSpec((1, MLA_KVR), lambda b, i: (0, 0)), aspec, aspec, tspec, tspec, ospec,
                  pl.BlockSpec(memory_space=pl.ANY)],
        out_specs=[pl.BlockSpec((tok, 4 * LANE), lambda b, i: (b * nb + i, C_CKV // (4 * LANE))), aspec, aspec,
                   pl.BlockSpec((1, MLA_KVR), lambda b, i: (0, 0))],
        out_shape=[jax.ShapeDtypeStruct((tp, N_EXT), BF16)] + [jax.ShapeDtypeStruct((MLA_KVR, hw), F32)] * 2
        + [jax.ShapeDtypeStruct((1, MLA_KVR), F32)],
        input_output_aliases={9: 0},
        compiler_params=_cp(("arbitrary", "arbitrary")),
    )(dkf, dvf, proj, kv_norm_g, wk, wv, cos_t, sin_t, d_lr, dproj)


def _mid_fwd(ya_in, yb_in, proj, hp, target, w_gp, w_mp, w_o, final_g, bsz, lp):
    tp = bsz * lp
    tm = _attn_block(lp)
    nb = lp // tm
    last = pl.cdiv(lp - X0, tm) - 1

    def body(ya_ref, yb_ref, gg_ref, gm_ref, h_ref, ta_ref, tb_ref, wgp_ref, wmp_ref, wo_ref, fg_ref,
             ya_out, yb_out, dh_ref, loss_ref, dfg_ref):
        @pl.when(jnp.logical_and(pl.program_id(0) == 0, pl.program_id(1) == 0))
        def _():
            loss_ref[...] = jnp.zeros_like(loss_ref)
            dfg_ref[...] = jnp.zeros_like(dfg_ref)

        y_a = _dot(ya_ref[...], wgp_ref[...])
        y_b = _dot(yb_ref[...], wmp_ref[...])
        ya_out[...] = _bf(y_a)
        yb_out[...] = _bf(y_b)
        merged = _sigmoid(gg_ref[...].astype(F32)) * y_a + _sigmoid(gm_ref[...].astype(F32)) * y_b
        h2 = h_ref[...] + _dot(_bf(merged), wo_ref[...])
        fg = fg_ref[...]
        xh, r = _rms_fwd(h2)
        pos = pl.program_id(1) * tm + lax.broadcasted_iota(jnp.int32, (tm, 1), 0)
        t = jnp.concatenate([ta_ref[0, tm - X0:, :], tb_ref[0, :tm - X0, :]], axis=0)
        err = jnp.where(pos >= X0, xh * fg - t, 0.0)
        loss_ref[...] += 0.5 * jnp.sum(jnp.mean(err * err, axis=-1, keepdims=True), axis=0, keepdims=True)
        dy = err * (1.0 / D_MODEL)
        dx, dfg = _rms_bwd(dy, xh, r, fg)
        dh_ref[...] = dx
        dfg_ref[...] += dfg

    tok = lambda c: pl.BlockSpec((tm, D_MODEL), lambda b, i: (b * nb + i, c))
    wspec = pl.BlockSpec((D_MODEL, D_MODEL), lambda b, i: (0, 0))
    return pl.pallas_call(
        body, name="mid_fwd", grid=(bsz, nb),
        in_specs=[tok(0), tok(0), tok(C_GG // D_MODEL), tok(C_GM // D_MODEL), tok(0),
                  pl.BlockSpec((1, tm, D_MODEL), lambda b, i: (b, jnp.maximum(i - 1, 0), 0)),
                  pl.BlockSpec((1, tm, D_MODEL), lambda b, i: (b, jnp.minimum(i, last), 0)),
                  wspec, wspec, wspec, pl.BlockSpec((1, D_MODEL), lambda b, i: (0, 0))],
        out_specs=[tok(0), tok(0), tok(0), pl.BlockSpec((1, LANE), lambda b, i: (0, 0)),
                   pl.BlockSpec((1, D_MODEL), lambda b, i: (0, 0))],
        out_shape=[jax.ShapeDtypeStruct((tp, D_MODEL), BF16), jax.ShapeDtypeStruct((tp, D_MODEL), BF16),
                   jax.ShapeDtypeStruct((tp, D_MODEL), F32), jax.ShapeDtypeStruct((1, LANE), F32),
                   jax.ShapeDtypeStruct((1, D_MODEL), F32)],
        compiler_params=_cp(("arbitrary", "arbitrary"), 48),
    )(ya_in, yb_in, proj, proj, hp, target, target, w_gp, w_mp, w_o, final_g)


def _mid_bwd(dh2, y_a, y_b, proj, ya_in, yb_in, o_b, w_o, w_gp, w_mp, bsz, lp):
    tp = bsz * lp
    tm = MXU_DEPTH if tp % MXU_DEPTH == 0 else _attn_block(lp)
    nsteps = tp // tm
    group = 3 * D_MODEL

    def body(dh_ref, ya_ref, yb_ref, mz_ref, gg_ref, gm_ref, yai_ref, ybi_ref, ob_ref, wo_ref, wgp_ref, wmp_ref,
             dyai_ref, do_ref, dp_ref, dl_ref, dwo_ref, dwgp_ref, dwmp_ref, a_o, a_gp, a_mp):
        @pl.when(pl.program_id(0) == 0)
        def _():
            for r in (a_o, a_gp, a_mp):
                r[...] = jnp.zeros_like(r)

        dh = _bf(dh_ref[...])
        dm = _dot_nt(dh, wo_ref[...])
        y_a, y_b = ya_ref[...].astype(F32), yb_ref[...].astype(F32)
        sg, sm = _sigmoid(gg_ref[...].astype(F32)), _sigmoid(gm_ref[...].astype(F32))
        d_ya, d_yb = _bf(sg * dm), _bf(sm * dm)
        dp_ref[:, D_MODEL:2 * D_MODEL] = _bf(dm * y_a * sg * (1.0 - sg))
        dp_ref[:, 2 * D_MODEL:] = _bf(dm * y_b * sm * (1.0 - sm))
        a_o[...] += _dot_tn(_bf(sg * y_a + sm * y_b), dh)
        a_gp[...] += _dot_tn(yai_ref[...], d_ya)
        a_mp[...] += _dot_tn(ybi_ref[...], d_yb)
        dyai_ref[...] = _bf(_dot_nt(d_ya, wgp_ref[...]))
        dy = _dot_nt(d_yb, wmp_ref[...])
        mz, o = mz_ref[...].astype(F32), ob_ref[...].astype(F32)
        s = _sigmoid(mz)
        do = _bf(dy * (mz * s))
        do_ref[...] = do
        dp_ref[:, :D_MODEL] = _bf(dy * o * (s * (1.0 + mz * (1.0 - s))))
        prod = do.astype(F32) * o
        for h in range(MLA_HEADS):
            dl = jnp.sum(prod[:, h * MLA_DV:(h + 1) * MLA_DV], axis=-1, keepdims=True)
            dl_ref[h] = jnp.broadcast_to(dl, (tm, LANE))

        @pl.when(pl.program_id(0) == nsteps - 1)
        def _():
            pltpu.sync_copy(a_o, dwo_ref)
            pltpu.sync_copy(a_gp, dwgp_ref)
            pltpu.sync_copy(a_mp, dwmp_ref)

    tok = lambda c: pl.BlockSpec((tm, D_MODEL), lambda i: (i, c))
    wspec = pl.BlockSpec((D_MODEL, D_MODEL), lambda i: (0, 0))
    anyspec = pl.BlockSpec(memory_space=pl.ANY)
    wshape = jax.ShapeDtypeStruct((D_MODEL, D_MODEL), F32)
    return pl.pallas_call(
        body, name="mid_bwd", grid=(nsteps,),
        in_specs=[tok(0), tok(0), tok(0), tok(C_MZ // D_MODEL), tok(C_GG // D_MODEL), tok(C_GM // D_MODEL),
                  tok(0), tok(0), tok(0), wspec, wspec, wspec],
        out_specs=[tok(0), tok(0), pl.BlockSpec((tm, group), lambda i: (i, C_MZ // group)),
                   pl.BlockSpec((MLA_HEADS, tm, LANE), lambda i: (0, i, 0)), anyspec, anyspec, anyspec],
        out_shape=[jax.ShapeDtypeStruct((tp, D_MODEL), BF16)] * 2 + [jax.ShapeDtypeStruct((tp, N_EXT), BF16),
                   jax.ShapeDtypeStruct((MLA_HEADS, tp, LANE), F32)] + [wshape] * 3,
        scratch_shapes=[pltpu.VMEM((D_MODEL, D_MODEL), F32)] * 3,
        compiler_params=_cp(("arbitrary",), 56),
    )(dh2, y_a, y_b, proj, proj, proj, ya_in, yb_in, o_b, w_o, w_gp, w_mp)


MESH_ID = pl.DeviceIdType.MESH
EXCHANGE_SEMS = [pltpu.SemaphoreType.DMA((N_DEV - 1,)), pltpu.SemaphoreType.DMA((N_DEV - 1,)), pltpu.SemaphoreType.DMA]


def _my_place():
    return lax.axis_index("x"), lax.axis_index("y"), lax.axis_index("c")


def _exchange(g_ref, recv_ref, send_sems, recv_sems, local_sem, start, same=False):
    x, y, c = _my_place()
    me = 4 * x + 2 * y + c
    own = pltpu.make_async_copy(g_ref if same else g_ref.at[me], recv_ref.at[me], local_sem)
    sends, lands = [], []
    for d in range(1, N_DEV):
        px = 1 - x if d & 4 else x
        py = 1 - y if d & 2 else y
        pc = 1 - c if d & 1 else c
        peer = 4 * px + 2 * py + pc
        for slot, group in ((me, sends),) if start else ((me, sends), (peer, lands)):
            group.append(pltpu.make_async_remote_copy(
                src_ref=g_ref if same else g_ref.at[peer], dst_ref=recv_ref.at[slot], send_sem=send_sems.at[d - 1],
                recv_sem=recv_sems.at[d - 1], device_id=(px, py, pc), device_id_type=MESH_ID))
    if start:
        own.start()
        for cp in sends:
            cp.start()
    else:
        for cp in lands:
            cp.wait_recv()
        for cp in sends:
            cp.wait_send()
        own.wait()


def _dw_in(u, dproj, slabs):
    tp = u.shape[0]
    tn = 3 * LANE
    nj = N_EXT // tn

    def body(u_ref, d_ref, g_ref, o_ref, recv_ref, send_sems, recv_sems, local_sem):
        j = pl.program_id(0)

        @pl.when(j == 0)
        def _():
            _exchange(g_ref, recv_ref, send_sems, recv_sems, local_sem, True)

        o_ref[...] = _dot_tn(d_ref[...], u_ref[...])

        @pl.when(j == nj - 1)
        def _():
            _exchange(g_ref, recv_ref, send_sems, recv_sems, local_sem, False)

    anyspec = pl.BlockSpec(memory_space=pl.ANY)
    return pl.pallas_call(
        body, name="dw_in", grid=(nj,),
        in_specs=[pl.BlockSpec((tp, D_MODEL), lambda j: (0, 0), pipeline_mode=pl.Buffered(1)),
                  pl.BlockSpec((tp, tn), lambda j: (0, j)), anyspec],
        out_specs=[pl.BlockSpec((tn, D_MODEL), lambda j: (j, 0)), anyspec],
        out_shape=[jax.ShapeDtypeStruct((N_EXT, D_MODEL), F32), jax.ShapeDtypeStruct(slabs.shape, slabs.dtype)],
        scratch_shapes=EXCHANGE_SEMS,
        compiler_params=_cp(("arbitrary",), 56),
    )(u, dproj, slabs)


def _dx_in(dproj, w_ext, hp, dh2, norm_g, slabs):
    tp = hp.shape[0]
    tm = 2 * TOK
    ni = tp // tm

    def body(d_ref, w_ref, h_ref, dh_ref, g_ref, s_ref, o_ref, dg_ref, recv_ref, send_sems, recv_sems, local_sem):
        i = pl.program_id(0)

        @pl.when(i == 0)
        def _():
            _exchange(s_ref, recv_ref, send_sems, recv_sems, local_sem, True)
            dg_ref[...] = jnp.zeros_like(dg_ref)

        du = _dot_nt(d_ref[...], w_ref[...])
        g = g_ref[...]
        xh, r = _rms_fwd(h_ref[...])
        dx, dg = _rms_bwd(du, xh, r, g)
        o_ref[...] = dh_ref[...] + dx
        dg_ref[...] += dg

        @pl.when(i == ni - 1)
        def _():
            _exchange(s_ref, recv_ref, send_sems, recv_sems, local_sem, False)

    tok = pl.BlockSpec((tm, D_MODEL), lambda i: (i, 0))
    anyspec = pl.BlockSpec(memory_space=pl.ANY)
    return pl.pallas_call(
        body, name="dx_in", grid=(ni,),
        in_specs=[pl.BlockSpec((tm, N_EXT), lambda i: (i, 0)),
                  pl.BlockSpec((D_MODEL, N_EXT), lambda i: (0, 0), pipeline_mode=pl.Buffered(1)),
                  tok, tok, pl.BlockSpec((1, D_MODEL), lambda i: (0, 0)), anyspec],
        out_specs=[tok, pl.BlockSpec((1, D_MODEL), lambda i: (0, 0)), anyspec],
        out_shape=[jax.ShapeDtypeStruct((tp, D_MODEL), F32), jax.ShapeDtypeStruct((1, D_MODEL), F32),
                   jax.ShapeDtypeStruct(slabs.shape, slabs.dtype)],
        scratch_shapes=EXCHANGE_SEMS,
        compiler_params=_cp(("arbitrary",), 56),
    )(dproj, w_ext, hp, dh2, norm_g, slabs)


def _meta_grad(dhp3):
    bsz = dhp3.shape[0]

    def body(d_ref, o_ref):
        @pl.when(pl.program_id(0) == 0)
        def _():
            o_ref[...] = jnp.zeros_like(o_ref)

        o_ref[...] += d_ref[0]

    return pl.pallas_call(
        body, name="meta_grad", grid=(bsz,),
        in_specs=[pl.BlockSpec((1, N_META, D_MODEL), lambda b: (b, FRONT // N_META, 0))],
        out_specs=pl.BlockSpec((N_META, D_MODEL), lambda b: (0, 0)),
        out_shape=jax.ShapeDtypeStruct((N_META, D_MODEL), F32),
        compiler_params=_cp(("arbitrary",)),
    )(dhp3)


W_IN_SHARD = N_IN // N_DEV


def _pad_lanes(a, width=LANE):
    return jnp.pad(a, [(0, 0)] * (a.ndim - 1) + [(0, width - a.shape[-1])])


def _rot_cols(w):
    half = w.shape[-1] // 2
    return jnp.concatenate([-w[..., half:], w[..., :half]], axis=-1)


def _unrot_cols(dw):
    half = dw.shape[-1] // 2
    return jnp.concatenate([dw[..., half:], -dw[..., :half]], axis=-1)


def _w_in_cols(shards, lo, hi):
    parts = []
    for k in range(lo // W_IN_SHARD, (hi - 1) // W_IN_SHARD + 1):
        a, b = max(lo, k * W_IN_SHARD), min(hi, (k + 1) * W_IN_SHARD)
        parts.append(shards[k][:, a - k * W_IN_SHARD:b - k * W_IN_SHARD])
    return parts[0] if len(parts) == 1 else jnp.concatenate(parts, axis=1)


def _w_in_ext(shards):
    c = lambda lo, hi: _w_in_cols(shards, lo, hi)
    kr = c(O_KR, O_MZ)
    return jnp.concatenate([
        c(O_V, O_LR), c(O_Z, O_CQ), c(O_Q, O_K), c(O_K, O_V), c(O_MZ, O_GG), c(O_GG, O_GM), c(O_GM, N_IN),
        c(O_CKV, O_KR), _pad_lanes(kr), _pad_lanes(_rot_cols(kr)), _pad_lanes(c(O_LR, O_Z)), c(O_CQ, O_CKV)], axis=1)


def _w_in_grad_t(dwt):
    g = lambda start, width: dwt[start:start + width]
    half = MLA_ROPE // 2
    krot = g(C_KROT, MLA_ROPE)
    kr = g(C_KR, MLA_ROPE) + jnp.concatenate([krot[half:], -krot[:half]], axis=0)
    return jnp.concatenate([
        g(C_Q, GLA_KW), g(C_K, GLA_KW), g(C_V, GLA_VW), g(C_LR, GLA_RANK), g(C_Z, GLA_VW), g(C_CQ, MLA_QR),
        g(C_CKV, MLA_KVR), kr, g(C_MZ, D_MODEL), g(C_GG, D_MODEL), g(C_GM, D_MODEL)], axis=0)


def _rope_tables(lp):
    inv = 1.0 / (ROPE_BASE ** (jnp.arange(0, MLA_ROPE, 2, dtype=F32) / MLA_ROPE))
    ang = (jnp.arange(lp, dtype=F32) - FRONT)[:, None] * inv[None, :]
    cos, sin = jnp.cos(ang), jnp.sin(ang)
    return _pad_lanes(jnp.concatenate([cos, cos], axis=1)), _pad_lanes(jnp.concatenate([sin, sin], axis=1))


def _local_step(x, loss_target, w):
    bsz, seq, _ = x.shape
    lp = X0 + seq
    tp = bsz * lp
    assert lp % TOK == 0 and lp % GLA_ROWS == 0
    meta = jnp.broadcast_to(w["meta_tokens"][None], (bsz, N_META, D_MODEL))
    hp = jnp.concatenate([jnp.zeros((bsz, FRONT, D_MODEL), F32), meta, x], axis=1).reshape(tp, D_MODEL)
    cos_t, sin_t = _rope_tables(lp)

    w_ext = _w_in_ext(w["w_in"])
    u, proj, packed_all = _proj_in(hp, w["norm_g"], w_ext, w["packed"])
    packed_all, off = packed_all.reshape(N_DEV, -1), 0
    for n, shape, axis in PACKED:
        size = shape[0] * shape[1]
        w[n] = _join8(packed_all[:, off:off + size].reshape((N_DEV,) + shape), axis)
        off += size
    gw_pad = jnp.pad(w["gla_gate_w"], ((0, LANE - GLA_RANK), (0, 0)))
    uq = w["mla_w_uq"].reshape(MLA_QR, MLA_HEADS, MLA_QK)
    rope_w = uq[:, :, MLA_NOPE:]
    hw = MLA_HEADS * LANE
    wn = uq[:, :, :MLA_NOPE].reshape(MLA_QR, hw)
    wr = _pad_lanes(rope_w).reshape(MLA_QR, hw)
    wt = _pad_lanes(_rot_cols(rope_w)).reshape(MLA_QR, hw)
    ukv = w["mla_w_ukv"].reshape(MLA_KVR, MLA_HEADS, MLA_NOPE + MLA_DV)
    wk = ukv[:, :, :MLA_NOPE].reshape(MLA_KVR, hw)
    wv = ukv[:, :, MLA_NOPE:].reshape(MLA_KVR, hw)

    o_raw, ya_in, s_all = _gla_fwd(proj, gw_pad, w["gla_gate_b"], w["gla_norm_g"], bsz, lp)
    qf = _q_up(proj, w["mla_q_norm_g"], wn, wr, wt, cos_t, sin_t, bsz, lp)
    kf, vf = _kv_up(proj, w["mla_kv_norm_g"], wk, wv, cos_t, sin_t, bsz, lp)
    o_b, yb_in, lse = _attn_fwd(qf, kf, vf, proj, bsz, lp)
    y_a, y_b, dh2, loss, d_final_g = _mid_fwd(ya_in, yb_in, proj, hp, loss_target, w["gla_proj"], w["mla_proj"],
                                              w["w_out"], w["final_norm_g"], bsz, lp)
    d_ya, d_o, dproj, delta, d_w_out, d_gla_proj, d_mla_proj = _mid_bwd(
        dh2, y_a, y_b, proj, ya_in, yb_in, o_b, w["w_out"], w["gla_proj"], w["mla_proj"], bsz, lp)
    dproj, d_gate, d_gla_norm = _gla_bwd(proj, gw_pad, w["gla_gate_b"], w["gla_norm_g"], o_raw, s_all, d_ya, dproj,
                                         bsz, lp)
    d_lr, d_gw_pad, d_gate_b = _gate_bwd(d_gate, proj, gw_pad)
    dqf, dkf, dvf = _attn_bwd(qf, kf, vf, d_o, lse, delta, bsz, lp)
    dproj, d_wn, d_wr, d_wt, d_qn = _q_up_bwd(dqf, proj, w["mla_q_norm_g"], wn, wr, wt, cos_t, sin_t, dproj,
                                              bsz, lp)
    dproj, d_wk, d_wv, d_kvn = _kv_up_bwd(dkf, dvf, proj, w["mla_kv_norm_g"], wk, wv, cos_t, sin_t, d_lr, dproj,
                                          bsz, lp)

    d_rope = (d_wr.reshape(MLA_QR, MLA_HEADS, LANE)[:, :, :MLA_ROPE]
              + _unrot_cols(d_wt.reshape(MLA_QR, MLA_HEADS, LANE)[:, :, :MLA_ROPE]))
    d_uq = jnp.concatenate([d_wn.reshape(MLA_QR, MLA_HEADS, LANE), d_rope], axis=-1).reshape(MLA_QR, MLA_HEADS * MLA_QK)
    d_ukv = jnp.concatenate([d_wk.reshape(MLA_KVR, MLA_HEADS, LANE), d_wv.reshape(MLA_KVR, MLA_HEADS, LANE)],
                            axis=-1).reshape(MLA_KVR, MLA_HEADS * (MLA_NOPE + MLA_DV))
    mats = dict(gla_gate_w=d_gw_pad[:GLA_RANK], gla_proj=d_gla_proj, mla_w_uq=d_uq, mla_w_ukv=d_ukv,
                mla_proj=d_mla_proj, w_out=d_w_out)
    packed = _pad_rows(jnp.concatenate([_split8(mats[n], axis).reshape(N_DEV, -1) for n, _, axis in PACKED], axis=1),
                       PACK_ROWS)
    d_w_ext_t, packed_parts = _dw_in(u, dproj, _bf(packed))
    w_in_slabs = _bf(_w_in_grad_t(d_w_ext_t).reshape(N_DEV, W_IN_SHARD, D_MODEL))
    d_hp, d_norm_g, w_in_parts = _dx_in(dproj, w_ext, hp, dh2, w["norm_g"], w_in_slabs)
    d_hp3 = d_hp.reshape(bsz, lp, D_MODEL)
    small = dict(meta_tokens=_meta_grad(d_hp3), norm_g=d_norm_g, gla_gate_b=d_gate_b, gla_norm_g=d_gla_norm,
                 mla_q_norm_g=d_qn, mla_kv_norm_g=d_kvn, final_norm_g=d_final_g)
    return loss, d_hp3[:, X0:, :], w_in_parts, packed_parts, small


PACKED = (("gla_gate_w", (GLA_RANK, GLA_KW // N_DEV), 1),
          ("gla_proj", (D_MODEL // N_DEV, D_MODEL), 0), ("mla_w_uq", (MLA_QR, MLA_HEADS * MLA_QK // N_DEV), 1),
          ("mla_w_ukv", (MLA_KVR, MLA_HEADS * (MLA_NOPE + MLA_DV) // N_DEV), 1),
          ("mla_proj", (D_MODEL // N_DEV, D_MODEL), 0), ("w_out", (D_MODEL // N_DEV, D_MODEL), 0))
REPLICATED = (("norm_g", D_MODEL), ("gla_gate_b", GLA_KW), ("gla_norm_g", GLA_DV), ("mla_q_norm_g", MLA_QR),
              ("mla_kv_norm_g", MLA_KVR), ("final_norm_g", D_MODEL))
PACK_ROWS = 3744
PACK_BLOCK = 1248
SMALL_ROWS = 48
LOSS_ROW = N_META + 25
W_IN_BLOCK = 128


def _all_gather(shards):
    n_arr = len(shards)

    def body(*refs):
        x_refs, out_refs = refs[:n_arr], refs[n_arr:2 * n_arr]
        send_sems, recv_sems, local_sems = refs[2 * n_arr:]
        x, y, c = _my_place()
        me, sibling = (x, y, c), (x, y, 1 - c)
        chips = [(1 - x, y), (x, 1 - y), (1 - x, 1 - y)]

        def copy(a, k, block, to, from_input=False):
            slab = out_refs[a].at[4 * block[0] + 2 * block[1] + block[2]]
            return pltpu.make_async_remote_copy(
                src_ref=x_refs[a] if from_input else slab, dst_ref=slab,
                send_sem=send_sems.at[7 * a + k], recv_sem=recv_sems.at[7 * a + k], device_id=to,
                device_id_type=MESH_ID)

        arrays = range(n_arr)
        mine = [pltpu.make_async_copy(x_refs[a], out_refs[a].at[4 * x + 2 * y + c], local_sems.at[a]) for a in arrays]
        for cp in mine:
            cp.start()
        first = [copy(a, 0, me, sibling, True) for a in arrays]
        first += [copy(a, 1 + j, me, (*chip, c), True) for j, chip in enumerate(chips) for a in arrays]
        for cp in first:
            cp.start()
        passed = []
        for j, chip in enumerate(chips):
            for a in arrays:
                copy(a, 1 + j, (*chip, c), me).wait_recv()
                passed.append(copy(a, 4 + j, (*chip, c), sibling))
                passed[-1].start()
        for a in arrays:
            copy(a, 0, sibling, me).wait_recv()
        for j, chip in enumerate(chips):
            for a in arrays:
                copy(a, 4 + j, (*chip, 1 - c), me).wait_recv()
        for cp in first + passed:
            cp.wait_send()
        for cp in mine:
            cp.wait()

    anyspec = pl.BlockSpec(memory_space=pl.ANY)
    return pl.pallas_call(
        body, name="weights_all_gather",
        out_shape=[jax.ShapeDtypeStruct((N_DEV,) + s.shape, s.dtype) for s in shards],
        in_specs=[anyspec] * n_arr, out_specs=[anyspec] * n_arr,
        scratch_shapes=[pltpu.SemaphoreType.DMA((7 * n_arr,)), pltpu.SemaphoreType.DMA((7 * n_arr,)),
                        pltpu.SemaphoreType.DMA((n_arr,))],
    )(*shards)


def _small_exchange(slabs):
    def body(g_ref, recv_ref, send_sems, recv_sems, local_sem):
        _exchange(g_ref, recv_ref, send_sems, recv_sems, local_sem, True)
        _exchange(g_ref, recv_ref, send_sems, recv_sems, local_sem, False)

    vmem = pl.BlockSpec(memory_space=pltpu.VMEM)
    return pl.pallas_call(
        body, name="small_exchange", out_shape=jax.ShapeDtypeStruct(slabs.shape, slabs.dtype),
        in_specs=[vmem], out_specs=vmem, scratch_shapes=EXCHANGE_SEMS,
    )(slabs)


def _adamw(parts, w, m, v, block_rows, name):
    rows, cols = w.shape

    def body(p_ref, w_ref, m_ref, v_ref, g_out, d_out, m_out, v_out):
        g = p_ref[0].astype(F32)
        for s in range(1, N_DEV):
            g = g + p_ref[s].astype(F32)
        m_new = ADAM_B1 * m_ref[...] + (1.0 - ADAM_B1) * g
        v_new = ADAM_B2 * v_ref[...] + (1.0 - ADAM_B2) * (g * g)
        m_hat = m_new / (1.0 - ADAM_B1 ** ADAM_STEP)
        v_hat = v_new / (1.0 - ADAM_B2 ** ADAM_STEP)
        g_out[...] = g
        d_out[...] = -ADAM_LR * (m_hat / (jnp.sqrt(v_hat) + ADAM_EPS) + ADAM_WD * w_ref[...])
        m_out[...] = m_new
        v_out[...] = v_new

    spec = pl.BlockSpec((block_rows, cols), lambda i: (i, 0))
    return pl.pallas_call(
        body, name=name, grid=(pl.cdiv(rows, block_rows),),
        in_specs=[pl.BlockSpec((N_DEV, block_rows, cols), lambda i: (0, i, 0)), spec, spec, spec],
        out_specs=[spec] * 4, out_shape=[jax.ShapeDtypeStruct((rows, cols), F32)] * 4,
        compiler_params=_cp(("parallel",), 48),
    )(parts, w, m, v)


def _pad_rows(flat, rows):
    pad = rows * LANE - flat.shape[-1]
    flat = jnp.pad(flat, [(0, 0)] * (flat.ndim - 1) + [(0, pad)])
    return flat.reshape(flat.shape[:-1] + (rows, LANE))


def _pack_shards(shards):
    return _pad_rows(jnp.concatenate([shards[n].reshape(-1) for n, _, _ in PACKED]), PACK_ROWS)


def _unpack_shards(packed):
    flat, out, off = packed.reshape(-1), {}, 0
    for n, shape, _ in PACKED:
        size = shape[0] * shape[1]
        out[n] = flat[off:off + size].reshape(shape)
        off += size
    return out


def _split8(full, axis):
    r, c = full.shape
    if axis == 0:
        return full.reshape(N_DEV, r // N_DEV, c)
    return full.reshape(r, N_DEV, c // N_DEV).transpose(1, 0, 2)


def _join8(shards, axis):
    _, r, c = shards.shape
    if axis == 0:
        return shards.reshape(N_DEV * r, c)
    return shards.transpose(1, 0, 2).reshape(r, N_DEV * c)


def _pack_small(meta_shard, vals, loss_row):
    rows = jnp.concatenate([vals[n].reshape(-1, LANE) for n, _ in REPLICATED] + [loss_row], axis=0)
    rows = jnp.pad(rows, ((0, SMALL_ROWS - N_META - rows.shape[0]), (0, 0)))
    return jnp.concatenate([meta_shard, jnp.broadcast_to(rows, meta_shard.shape[:-2] + rows.shape)], axis=-2)


def _unpack_small(packed):
    out, off = {"meta_tokens": packed[:N_META]}, N_META
    for n, size in REPLICATED:
        out[n] = packed[off:off + size // LANE].reshape(1, size)
        off += size // LANE
    return out


def kernel(x, meta_tokens, norm_g, w_in, gla_gate_w, gla_gate_b, gla_norm_g, gla_proj, mla_q_norm_g, mla_w_uq, mla_kv_norm_g, mla_w_ukv, mla_proj, w_out, final_norm_g, loss_target, m_meta_tokens, m_norm_g, m_w_in, m_gla_gate_w, m_gla_gate_b, m_gla_norm_g, m_gla_proj, m_mla_q_norm_g, m_mla_w_uq, m_mla_kv_norm_g, m_mla_w_ukv, m_mla_proj, m_w_out, m_final_norm_g, v_meta_tokens, v_norm_g, v_w_in, v_gla_gate_w, v_gla_gate_b, v_gla_norm_g, v_gla_proj, v_mla_q_norm_g, v_mla_w_uq, v_mla_kv_norm_g, v_mla_w_ukv, v_mla_proj, v_w_out, v_final_norm_g):
    given = dict(meta_tokens=meta_tokens, norm_g=norm_g, w_in=w_in, gla_gate_w=gla_gate_w, gla_gate_b=gla_gate_b,
                 gla_norm_g=gla_norm_g, gla_proj=gla_proj, mla_q_norm_g=mla_q_norm_g, mla_w_uq=mla_w_uq,
                 mla_kv_norm_g=mla_kv_norm_g, mla_w_ukv=mla_w_ukv, mla_proj=mla_proj, w_out=w_out,
                 final_norm_g=final_norm_g)
    mom_m = dict(meta_tokens=m_meta_tokens, norm_g=m_norm_g, w_in=m_w_in, gla_gate_w=m_gla_gate_w,
                 gla_gate_b=m_gla_gate_b, gla_norm_g=m_gla_norm_g, gla_proj=m_gla_proj, mla_q_norm_g=m_mla_q_norm_g,
                 mla_w_uq=m_mla_w_uq, mla_kv_norm_g=m_mla_kv_norm_g, mla_w_ukv=m_mla_w_ukv, mla_proj=m_mla_proj,
                 w_out=m_w_out, final_norm_g=m_final_norm_g)
    mom_v = dict(meta_tokens=v_meta_tokens, norm_g=v_norm_g, w_in=v_w_in, gla_gate_w=v_gla_gate_w,
                 gla_gate_b=v_gla_gate_b, gla_norm_g=v_gla_norm_g, gla_proj=v_gla_proj, mla_q_norm_g=v_mla_q_norm_g,
                 mla_w_uq=v_mla_w_uq, mla_kv_norm_g=v_mla_kv_norm_g, mla_w_ukv=v_mla_w_ukv, mla_proj=v_mla_proj,
                 w_out=v_w_out, final_norm_g=v_final_norm_g)
    shapes = {n: a.shape for n, a in given.items()}
    shard2d = {n: s for n, s, _ in PACKED}
    shard2d["w_in"] = (D_MODEL, W_IN_SHARD)
    shard2d["meta_tokens"] = (N_META, LANE)

    def as2d(tree):
        out = {n: tree[n].reshape(shard2d[n]) for n in shard2d}
        out.update({n: tree[n].reshape(1, size) for n, size in REPLICATED})
        return out

    w_loc, m_loc, v_loc = as2d(given), as2d(mom_m), as2d(mom_v)

    w_in_all, meta_all = _all_gather([w_loc["w_in"].astype(BF16), w_loc["meta_tokens"]])
    flat = jnp.concatenate([w_loc[n].astype(BF16).reshape(-1) for n, _, _ in PACKED])
    full = {"w_in": w_in_all, "meta_tokens": _join8(meta_all, 1), "packed": _pad_rows(flat, PACK_ROWS)}
    for n, _ in REPLICATED:
        full[n] = w_loc[n]

    loss_part, grad_x, w_in_parts, packed_parts, small = _local_step(x, loss_target, full)
    small_all = _small_exchange(_pack_small(_split8(small["meta_tokens"], 1), small,
                                            jnp.broadcast_to(loss_part[:, :1], (1, LANE))))

    w_in_t = [t["w_in"].T for t in (w_loc, m_loc, v_loc)]
    g_w, d_w, m_w, v_w = (o.T for o in _adamw(w_in_parts, *w_in_t, W_IN_BLOCK, "adamw_w_in"))
    g_p, d_p, m_p, v_p = _adamw(packed_parts, _pack_shards(w_loc), _pack_shards(m_loc), _pack_shards(v_loc),
                                PACK_BLOCK, "adamw_packed")
    zero_row = jnp.zeros((1, LANE), F32)
    g_s, d_s, m_s, v_s = _adamw(small_all, *(_pack_small(t["meta_tokens"], t, zero_row) for t in (w_loc, m_loc, v_loc)),
                                SMALL_ROWS, "adamw_small")
    loss = g_s[LOSS_ROW, 0]

    order = ["meta_tokens", "norm_g", "w_in", "gla_gate_w", "gla_gate_b", "gla_norm_g", "gla_proj", "mla_q_norm_g",
             "mla_w_uq", "mla_kv_norm_g", "mla_w_ukv", "mla_proj", "w_out", "final_norm_g"]
    result = [loss, grad_x]
    for w_in_out, packed_sh, packed_sm in ((g_w, g_p, g_s), (d_w, d_p, d_s), (m_w, m_p, m_s), (v_w, v_p, v_s)):
        tree = _unpack_shards(packed_sh)
        tree.update(_unpack_small(packed_sm))
        tree["w_in"] = w_in_out
        result += [tree[n].reshape(shapes[n]) for n in order]
    return tuple(result)
```

```python
import jax
import jax.numpy as jnp
from jax import lax
from jax.experimental import pallas as pl
from jax.experimental.pallas import tpu as pltpu

F32 = jnp.float32
BF16 = jnp.bfloat16

D_MODEL = 1024
N_META = 16
EPS = 1e-6
FRONT = 48
X0 = FRONT + N_META
GLA_HEADS, GLA_DK, GLA_DV, GLA_RANK, GLA_CHUNK = 4, 128, 256, 16, 64
GLA_GATE_NORMALIZER = 16.0
GLA_KW = GLA_HEADS * GLA_DK
GLA_VW = GLA_HEADS * GLA_DV
MLA_HEADS, MLA_NOPE, MLA_ROPE, MLA_DV, MLA_QR, MLA_KVR = 8, 128, 64, 128, 256, 128
MLA_QK = MLA_NOPE + MLA_ROPE
ROPE_BASE = 10000.0
LANE = 128
QKW = 2 * LANE

C_V, C_Z, C_Q, C_K = 0, 1024, 2048, 2560
C_MZ, C_GG, C_GM = 3072, 4096, 5120
C_CKV, C_KR, C_KROT, C_LR = 6144, 6272, 6400, 6528
C_CQ = 6656
N_EXT = 6912
O_Q, O_K, O_V, O_LR, O_Z, O_CQ, O_CKV, O_KR, O_MZ, O_GG, O_GM, N_IN = (
    0, 512, 1024, 2048, 2064, 3088, 3344, 3472, 3536, 4560, 5584, 6608)

ADAM_LR, ADAM_B1, ADAM_B2, ADAM_EPS, ADAM_WD, ADAM_STEP = 0.001, 0.9, 0.999, 1e-08, 0.01, 10

N_DEV = 8
TOK = 192
ATT_BLOCK = 352
EXT_BLOCK = 1152
MXU_DEPTH = 256


def _cp(sems=None, vmem_mb=None):
    kw = {}
    if sems is not None:
        kw["dimension_semantics"] = sems
    if vmem_mb is not None:
        kw["vmem_limit_bytes"] = vmem_mb * 1024 * 1024
    return pltpu.CompilerParams(**kw)


def _dot(a, b):
    return jnp.dot(a, b, preferred_element_type=F32)


def _dot_nt(a, b):
    return lax.dot_general(a, b, (((1,), (1,)), ((), ())), preferred_element_type=F32)


def _dot_tn(a, b):
    return lax.dot_general(a, b, (((0,), (0,)), ((), ())), preferred_element_type=F32)


def _sigmoid(x):
    return 1.0 / (1.0 + jnp.exp(-x))


def _bf(x):
    return x.astype(BF16)


def _big_tok(tp):
    return 4 * TOK if tp % (4 * TOK) == 0 else TOK


def _attn_block(lp):
    return ATT_BLOCK if lp % ATT_BLOCK == 0 else TOK


def _proj_in(hp, norm_g, w_ext, packed):
    tp = hp.shape[0]
    tm = 2 * TOK
    ni = tp // tm

    def body(h_ref, g_ref, w_ref, p_ref, u_ref, o_ref, pall_ref, send_sems, recv_sems, local_sem):
        i = pl.program_id(0)

        @pl.when(i == 0)
        def _():
            _exchange(p_ref, pall_ref, send_sems, recv_sems, local_sem, True, same=True)

        x = h_ref[...]
        r = lax.rsqrt(jnp.mean(x * x, axis=-1, keepdims=True) + EPS)
        u = _bf(x * r * g_ref[...])
        u_ref[...] = u
        o_ref[...] = _bf(_dot(u, w_ref[...]))

        @pl.when(i == ni - 1)
        def _():
            _exchange(p_ref, pall_ref, send_sems, recv_sems, local_sem, False, same=True)

    anyspec = pl.BlockSpec(memory_space=pl.ANY)
    return pl.pallas_call(
        body, name="proj_in", grid=(ni,),
        in_specs=[pl.BlockSpec((tm, D_MODEL), lambda i: (i, 0)),
                  pl.BlockSpec((1, D_MODEL), lambda i: (0, 0)),
                  pl.BlockSpec((D_MODEL, N_EXT), lambda i: (0, 0), pipeline_mode=pl.Buffered(1)), anyspec],
        out_specs=[pl.BlockSpec((tm, D_MODEL), lambda i: (i, 0)),
                   pl.BlockSpec((tm, N_EXT), lambda i: (i, 0)), anyspec],
        out_shape=[jax.ShapeDtypeStruct((tp, D_MODEL), BF16), jax.ShapeDtypeStruct((tp, N_EXT), BF16),
                   jax.ShapeDtypeStruct((N_DEV,) + packed.shape, packed.dtype)],
        scratch_shapes=EXCHANGE_SEMS,
        compiler_params=_cp(("arbitrary",), 56),
    )(hp, norm_g, w_ext, packed)


GLA_GROUP = 3
GLA_ROWS = GLA_GROUP * GLA_CHUNK


def _tri_dot(tri, x):
    hi = _bf(x)
    rest = x - hi.astype(F32)
    mid = _bf(rest)
    return _dot(tri, hi) + _dot(tri, mid) + _dot(tri, _bf(rest - mid.astype(F32)))


def _gla_gates(q_ref, k_ref, lr_ref, gw_ref, gb_ref, rows, not_first):
    z = _dot(lr_ref[rows, :], gw_ref[...]) + gb_ref[...]
    logsig = jnp.minimum(z, 0.0) - jnp.log(1.0 + jnp.exp(-jnp.abs(z)))
    row = lax.broadcasted_iota(jnp.int32, (GLA_CHUNK, GLA_KW), 0)
    live = jnp.logical_or(not_first, row >= FRONT)
    g = jnp.where(live, logsig * (1.0 / GLA_GATE_NORMALIZER), 0.0)
    ri = lax.broadcasted_iota(jnp.int32, (GLA_CHUNK, GLA_CHUNK), 0)
    ci = lax.broadcasted_iota(jnp.int32, (GLA_CHUNK, GLA_CHUNK), 1)
    tril = ci <= ri
    b = _tri_dot(_bf(tril.astype(F32)), g)
    bl = jnp.sum(jnp.where(row == GLA_CHUNK - 1, b, 0.0), axis=0, keepdims=True)
    eb, enb, elb, ebl = jnp.exp(b), jnp.exp(-b), jnp.exp(bl - b), jnp.exp(bl)
    q = q_ref[rows, :].astype(F32) * (GLA_DK ** -0.5)
    k = k_ref[rows, :].astype(F32)
    qe, ke, kl = q * eb, k * enb, k * elb
    return dict(z=z, live=live, tril=tril, row=row, eb=eb, enb=enb, elb=elb, ebl=ebl, qe=qe, ke=ke, kl=kl,
                qe_b=_bf(qe), ke_b=_bf(ke), kl_b=_bf(kl))


def _gla_in_specs(n_groups, rev):
    def rb(b, n):
        return b * n_groups + ((n_groups - 1 - n) if rev else n)

    return rb, [pl.BlockSpec((GLA_ROWS, GLA_KW), lambda b, n: (rb(b, n), C_Q // GLA_KW)),
                pl.BlockSpec((GLA_ROWS, GLA_KW), lambda b, n: (rb(b, n), C_K // GLA_KW)),
                pl.BlockSpec((GLA_ROWS, GLA_VW), lambda b, n: (rb(b, n), C_V // GLA_VW)),
                pl.BlockSpec((GLA_ROWS, GLA_VW), lambda b, n: (rb(b, n), C_Z // GLA_VW)),
                pl.BlockSpec((GLA_ROWS, LANE), lambda b, n: (rb(b, n), C_LR // LANE)),
                pl.BlockSpec((LANE, GLA_KW), lambda b, n: (0, 0)),
                pl.BlockSpec((1, GLA_KW), lambda b, n: (0, 0)),
                pl.BlockSpec((1, GLA_DV), lambda b, n: (0, 0))]


def _gla_fwd(proj, gw_pad, gate_b, gla_norm_g, bsz, lp):
    n_chunks = lp // GLA_CHUNK
    n_groups = n_chunks // GLA_GROUP
    tp = bsz * lp

    def body(q_ref, k_ref, v_ref, z_ref, lr_ref, gw_ref, gb_ref, gn_ref, oraw_ref, ya_ref, sall_ref, st_scr):
        grp = pl.program_id(1)

        @pl.when(grp == 0)
        def _():
            st_scr[...] = jnp.zeros_like(st_scr)

        chunks = [slice(j * GLA_CHUNK, (j + 1) * GLA_CHUNK) for j in range(GLA_GROUP)]
        cs = [_gla_gates(q_ref, k_ref, lr_ref, gw_ref, gb_ref, rows, True if j else grp > 0)
              for j, rows in enumerate(chunks)]
        gn = gn_ref[...]
        sts = [st_scr[h] for h in range(GLA_HEADS)]
        for j, (rows, c) in enumerate(zip(chunks, cs)):
            for h in range(GLA_HEADS):
                ks, vs = slice(h * GLA_DK, (h + 1) * GLA_DK), slice(h * GLA_DV, (h + 1) * GLA_DV)
                st = sts[h]
                sall_ref[0, j, h] = st
                v = v_ref[rows, vs]
                a = jnp.where(c["tril"], _dot_nt(c["qe_b"][:, ks], c["ke_b"][:, ks]), 0.0)
                o = _dot(_bf(a), v) + _dot_nt(c["qe_b"][:, ks], _bf(st))
                sts[h] = st * c["ebl"][:, ks] + _dot_tn(v, c["kl_b"][:, ks])
                oraw_ref[rows, vs] = o
                r = lax.rsqrt(jnp.mean(o * o, axis=-1, keepdims=True) + EPS)
                zg = z_ref[rows, vs].astype(F32)
                ya_ref[rows, vs] = _bf((o * r * gn) * (zg * _sigmoid(zg)))
        for h in range(GLA_HEADS):
            st_scr[h] = sts[h]

    rb, in_specs = _gla_in_specs(n_groups, False)
    return pl.pallas_call(
        body, name="gla_fwd", grid=(bsz, n_groups), in_specs=in_specs,
        out_specs=[pl.BlockSpec((GLA_ROWS, GLA_VW), lambda b, n: (rb(b, n), 0)),
                   pl.BlockSpec((GLA_ROWS, GLA_VW), lambda b, n: (rb(b, n), 0)),
                   pl.BlockSpec((1, GLA_GROUP, GLA_HEADS, GLA_DV, GLA_DK), lambda b, n: (b, n, 0, 0, 0))],
        out_shape=[jax.ShapeDtypeStruct((tp, GLA_VW), F32), jax.ShapeDtypeStruct((tp, GLA_VW), BF16),
                   jax.ShapeDtypeStruct((bsz, n_chunks, GLA_HEADS, GLA_DV, GLA_DK), F32)],
        scratch_shapes=[pltpu.VMEM((GLA_HEADS, GLA_DV, GLA_DK), F32)],
        compiler_params=_cp(("parallel", "arbitrary")),
    )(proj, proj, proj, proj, proj, gw_pad, gate_b, gla_norm_g)


def _gla_bwd(proj, gw_pad, gate_b, gla_norm_g, o_raw, s_all, d_ya, dproj, bsz, lp):
    n_chunks = lp // GLA_CHUNK
    n_groups = n_chunks // GLA_GROUP
    tp = bsz * lp

    def body(q_ref, k_ref, v_ref, z_ref, lr_ref, gw_ref, gb_ref, gn_ref, o_ref, s_ref, dya_ref, _,
             dp_ref, dz_ref, dgn_ref, dst_scr):
        dv_ref, dzg_ref = dp_ref.at[:, C_V:C_V + GLA_VW], dp_ref.at[:, C_Z:C_Z + GLA_VW]

        @pl.when(jnp.logical_and(pl.program_id(0) == 0, pl.program_id(1) == 0))
        def _():
            dgn_ref[...] = jnp.zeros_like(dgn_ref)

        @pl.when(pl.program_id(1) == 0)
        def _():
            dst_scr[...] = jnp.zeros_like(dst_scr)

        grp = n_groups - 1 - pl.program_id(1)
        chunks = [slice(j * GLA_CHUNK, (j + 1) * GLA_CHUNK) for j in range(GLA_GROUP)]
        cs = [_gla_gates(q_ref, k_ref, lr_ref, gw_ref, gb_ref, rows, True if j else grp > 0)
              for j, rows in enumerate(chunks)]
        gn = gn_ref[...]
        dgn = jnp.zeros((1, GLA_DV), F32)
        dqe_h, dke_h, dkl_h, dbl_h = ([[None] * GLA_HEADS for _ in chunks] for _ in range(4))
        dsts = [dst_scr[h] for h in range(GLA_HEADS)]
        for j in reversed(range(GLA_GROUP)):
            rows, c = chunks[j], cs[j]
            for h in range(GLA_HEADS):
                ks, vs = slice(h * GLA_DK, (h + 1) * GLA_DK), slice(h * GLA_DV, (h + 1) * GLA_DV)
                dst = dsts[h]
                v = v_ref[rows, vs]
                st = s_ref[0, j, h]
                o = o_ref[rows, vs]
                r = lax.rsqrt(jnp.mean(o * o, axis=-1, keepdims=True) + EPS)
                xh = o * r
                zg = z_ref[rows, vs].astype(F32)
                sg = _sigmoid(zg)
                dy = dya_ref[rows, vs].astype(F32)
                dzg_ref[rows, vs] = _bf(dy * (xh * gn) * (sg * (1.0 + zg * (1.0 - sg))))
                t = dy * (zg * sg)
                dgn += jnp.sum(t * xh, axis=0, keepdims=True)
                dxh = t * gn
                do_b = _bf(r * (dxh - xh * jnp.mean(dxh * xh, axis=-1, keepdims=True)))
                qe_b, ke_b, kl_b, dst_b = c["qe_b"][:, ks], c["ke_b"][:, ks], c["kl_b"][:, ks], _bf(dst)
                a = jnp.where(c["tril"], _dot_nt(qe_b, ke_b), 0.0)
                da_b = _bf(jnp.where(c["tril"], _dot_nt(do_b, v), 0.0))
                dqe_h[j][h] = _dot(da_b, ke_b) + _dot(do_b, _bf(st))
                dke_h[j][h] = _dot_tn(da_b, qe_b)
                dkl = _dot(v, dst_b)
                dkl_h[j][h] = dkl
                dv_ref[rows, vs] = _bf(_dot_tn(_bf(a), do_b) + _dot_nt(kl_b, dst_b))
                ddecay = jnp.sum(dst * st, axis=0, keepdims=True)
                dbl_h[j][h] = jnp.sum(dkl * c["kl"][:, ks], axis=0, keepdims=True) + ddecay * c["ebl"][:, ks]
                dsts[h] = dst * c["ebl"][:, ks] + _dot_tn(do_b, qe_b)
        for h in range(GLA_HEADS):
            dst_scr[h] = dsts[h]
        dgn_ref[...] += dgn
        ri = lax.broadcasted_iota(jnp.int32, (GLA_CHUNK, GLA_CHUNK), 0)
        ci = lax.broadcasted_iota(jnp.int32, (GLA_CHUNK, GLA_CHUNK), 1)
        triu = _bf((ci >= ri).astype(F32))
        for j, (rows, c) in enumerate(zip(chunks, cs)):
            dqe, dke, dkl, dbl = (jnp.concatenate(p[j], axis=1) for p in (dqe_h, dke_h, dkl_h, dbl_h))
            db = dqe * c["qe"] - dke * c["ke"] - dkl * c["kl"] + jnp.where(c["row"] == GLA_CHUNK - 1, dbl, 0.0)
            dg = _tri_dot(triu, db)
            dg = jnp.where(c["live"], dg, 0.0)
            dz_ref[rows, :] = dg * (1.0 / GLA_GATE_NORMALIZER) * _sigmoid(-c["z"])
            dp_ref[rows, C_Q:C_Q + GLA_KW] = _bf(dqe * c["eb"] * (GLA_DK ** -0.5))
            dp_ref[rows, C_K:C_K + GLA_KW] = _bf(dke * c["enb"] + dkl * c["elb"])

    rb, in_specs = _gla_in_specs(n_groups, True)
    wide = pl.BlockSpec((GLA_ROWS, GLA_VW), lambda b, n: (rb(b, n), 0))
    group = C_MZ
    return pl.pallas_call(
        body, name="gla_bwd", grid=(bsz, n_groups),
        in_specs=in_specs + [wide, pl.BlockSpec((1, GLA_GROUP, GLA_HEADS, GLA_DV, GLA_DK),
                                                lambda b, n: (b, n_groups - 1 - n, 0, 0, 0)), wide,
                             pl.BlockSpec(memory_space=pl.ANY)],
        out_specs=[pl.BlockSpec((GLA_ROWS, group), lambda b, n: (rb(b, n), 0)),
                   pl.BlockSpec((GLA_ROWS, GLA_KW), lambda b, n: (rb(b, n), 0)),
                   pl.BlockSpec((1, GLA_DV), lambda b, n: (0, 0))],
        out_shape=[jax.ShapeDtypeStruct((tp, N_EXT), BF16), jax.ShapeDtypeStruct((tp, GLA_KW), F32),
                   jax.ShapeDtypeStruct((1, GLA_DV), F32)],
        input_output_aliases={11: 0},
        scratch_shapes=[pltpu.VMEM((GLA_HEADS, GLA_DV, GLA_DK), F32)],
        compiler_params=_cp(("arbitrary", "arbitrary")),
    )(proj, proj, proj, proj, proj, gw_pad, gate_b, gla_norm_g, o_raw, s_all, d_ya, dproj)


def _gate_bwd(dz, proj, gw_pad):
    tp = dz.shape[0]
    tm = _big_tok(tp)

    def body(dz_ref, lr_ref, gw_ref, dlr_ref, dgw_ref, dgb_ref):
        @pl.when(pl.program_id(0) == 0)
        def _():
            dgw_ref[...] = jnp.zeros_like(dgw_ref)
            dgb_ref[...] = jnp.zeros_like(dgb_ref)

        dz = dz_ref[...]
        dz_b = _bf(dz)
        dlr_ref[...] = _bf(_dot_nt(dz_b, gw_ref[...]))
        dgw_ref[...] += _dot_tn(lr_ref[...], dz_b)
        dgb_ref[...] += jnp.sum(dz, axis=0, keepdims=True)

    return pl.pallas_call(
        body, name="gate_bwd", grid=(tp // tm,),
        in_specs=[pl.BlockSpec((tm, GLA_KW), lambda i: (i, 0)),
                  pl.BlockSpec((tm, LANE), lambda i: (i, C_LR // LANE)),
                  pl.BlockSpec((LANE, GLA_KW), lambda i: (0, 0))],
        out_specs=[pl.BlockSpec((tm, LANE), lambda i: (i, 0)),
                   pl.BlockSpec((LANE, GLA_KW), lambda i: (0, 0)),
                   pl.BlockSpec((1, GLA_KW), lambda i: (0, 0))],
        out_shape=[jax.ShapeDtypeStruct((tp, LANE), BF16), jax.ShapeDtypeStruct((LANE, GLA_KW), F32),
                   jax.ShapeDtypeStruct((1, GLA_KW), F32)],
        compiler_params=_cp(("arbitrary",)),
    )(dz, proj, gw_pad)


def _rms_fwd(x):
    r = lax.rsqrt(jnp.mean(x * x, axis=-1, keepdims=True) + EPS)
    return x * r, r


def _rms_bwd(dy, xh, r, g):
    dxh = dy * g
    dx = r * (dxh - xh * jnp.mean(dxh * xh, axis=-1, keepdims=True))
    return dx, jnp.sum(dy * xh, axis=0, keepdims=True)


def _q_up(proj, q_norm_g, wn, wr, wt, cos_t, sin_t, bsz, lp):
    tp = bsz * lp
    tok = _attn_block(lp)
    nb = lp // tok

    def body(cq_ref, g_ref, wn_ref, wr_ref, wt_ref, cos_ref, sin_ref, q_ref):
        xh, _ = _rms_fwd(cq_ref[...].astype(F32))
        cqn = _bf(xh * g_ref[...])
        nope = _dot(cqn, wn_ref[...])
        rope = _dot(cqn, wr_ref[...])
        rot = _dot(cqn, wt_ref[...])
        cos, sin = cos_ref[...], sin_ref[...]
        one = (lax.broadcasted_iota(jnp.int32, (tok, LANE), 1) == BIAS_LANE).astype(F32)
        for h in range(MLA_HEADS):
            sl = slice(h * LANE, (h + 1) * LANE)
            q_ref[:, h * QKW:h * QKW + LANE] = _bf(nope[:, sl])
            q_ref[:, h * QKW + LANE:(h + 1) * QKW] = _bf(rope[:, sl] * cos + rot[:, sl] * sin + one)

    wspec = pl.BlockSpec((MLA_QR, MLA_HEADS * LANE), lambda b, i: (0, 0))
    tspec = pl.BlockSpec((tok, LANE), lambda b, i: (i, 0))
    return pl.pallas_call(
        body, name="mla_q_up", grid=(bsz, nb),
        in_specs=[pl.BlockSpec((tok, MLA_QR), lambda b, i: (b * nb + i, C_CQ // MLA_QR)),
                  pl.BlockSpec((1, MLA_QR), lambda b, i: (0, 0)), wspec, wspec, wspec, tspec, tspec],
        out_specs=pl.BlockSpec((tok, MLA_HEADS * QKW), lambda b, i: (b * nb + i, 0)),
        out_shape=jax.ShapeDtypeStruct((tp, MLA_HEADS * QKW), BF16),
        compiler_params=_cp(("parallel", "parallel")),
    )(proj, q_norm_g, wn, wr, wt, cos_t, sin_t)


def _kv_up(proj, kv_norm_g, wk, wv, cos_t, sin_t, bsz, lp):
    tp = bsz * lp
    tok = _attn_block(lp)
    nb = lp // tok

    def body(ckv_ref, kr_ref, krot_ref, g_ref, wk_ref, wv_ref, cos_ref, sin_ref, k_ref, v_ref):
        xh, _ = _rms_fwd(ckv_ref[...].astype(F32))
        cn = _bf(xh * g_ref[...])
        kn = _dot(cn, wk_ref[...])
        v_ref[...] = _bf(_dot(cn, wv_ref[...]))
        pos = pl.program_id(1) * tok + lax.broadcasted_iota(jnp.int32, (tok, LANE), 0)
        lane = lax.broadcasted_iota(jnp.int32, (tok, LANE), 1)
        bias = jnp.where(jnp.logical_and(lane == BIAS_LANE, pos < FRONT), KEY_BIAS, 0.0)
        kr = _bf(kr_ref[...].astype(F32) * cos_ref[...] + krot_ref[...].astype(F32) * sin_ref[...] + bias)
        for h in range(MLA_HEADS):
            k_ref[:, h * QKW:h * QKW + LANE] = _bf(kn[:, h * LANE:(h + 1) * LANE])
            k_ref[:, h * QKW + LANE:(h + 1) * QKW] = kr

    wspec = pl.BlockSpec((MLA_KVR, MLA_HEADS * LANE), lambda b, i: (0, 0))
    tspec = pl.BlockSpec((tok, LANE), lambda b, i: (i, 0))
    return pl.pallas_call(
        body, name="mla_kv_up", grid=(bsz, nb),
        in_specs=[pl.BlockSpec((tok, LANE), lambda b, i: (b * nb + i, C_CKV // LANE)),
                  pl.BlockSpec((tok, LANE), lambda b, i: (b * nb + i, C_KR // LANE)),
                  pl.BlockSpec((tok, LANE), lambda b, i: (b * nb + i, C_KROT // LANE)),
                  pl.BlockSpec((1, MLA_KVR), lambda b, i: (0, 0)), wspec, wspec, tspec, tspec],
        out_specs=[pl.BlockSpec((tok, MLA_HEADS * QKW), lambda b, i: (b * nb + i, 0)),
                   pl.BlockSpec((tok, MLA_HEADS * LANE), lambda b, i: (b * nb + i, 0))],
        out_shape=[jax.ShapeDtypeStruct((tp, MLA_HEADS * QKW), BF16),
                   jax.ShapeDtypeStruct((tp, MLA_HEADS * LANE), BF16)],
        compiler_params=_cp(("parallel", "parallel")),
    )(proj, proj, proj, kv_norm_g, wk, wv, cos_t, sin_t)


ATT_SCALE = MLA_QK ** -0.5


KEY_BIAS = -1e30
BIAS_LANE = MLA_ROPE
NEG = 2 * KEY_BIAS
LOG2E = 1.4426950408889634
EXP2_SCALE = ATT_SCALE * LOG2E


def _causal_fill(s, r0, fill):
    tq, kmax = s.shape
    a = r0 // LANE * LANE
    mask = (a + lax.broadcasted_iota(jnp.int32, (tq, kmax - a), 1)
            <= r0 + lax.broadcasted_iota(jnp.int32, (tq, kmax - a), 0))
    right = jnp.where(mask, s[:, a:], fill)
    return jnp.concatenate([s[:, :a], right], axis=1) if a else right


def _attn_fwd(qf, kf, vf, proj, bsz, lp):
    tp = bsz * lp
    tq = _attn_block(lp)
    nh = 2

    def body(q_ref, k_ref, v_ref, mz_ref, ob_ref, yb_ref, lse_ref):
        starts = list(range(0, lp, tq))
        for pair in (starts[i:i + 2] for i in range(0, len(starts), 2)):
            work = [(r0, h) for r0 in pair for h in range(nh)]
            ss = [_causal_fill(_dot_nt(q_ref[r0:r0 + tq, h * QKW:(h + 1) * QKW],
                                       k_ref[0:r0 + tq, h * QKW:(h + 1) * QKW]), r0, NEG) for r0, h in work]
            ms = [jnp.max(s, axis=-1, keepdims=True) for s in ss]
            ps = [jnp.exp2((s - m) * EXP2_SCALE) for s, m in zip(ss, ms)]
            ls = [jnp.sum(p, axis=-1, keepdims=True) for p in ps]
            for (r0, h), p, m, l in zip(work, ps, ms, ls):
                rows, cols = slice(r0, r0 + tq), slice(h * MLA_DV, (h + 1) * MLA_DV)
                o = _dot(_bf(p), v_ref[0:r0 + tq, cols]) / l
                ob_ref[rows, cols] = _bf(o)
                mz = mz_ref[rows, cols].astype(F32)
                yb_ref[rows, cols] = _bf(o * (mz * _sigmoid(mz)))
                lse_ref[0, h, rows, :] = jnp.broadcast_to(m * EXP2_SCALE + jnp.log2(l), (tq, LANE))

    head = lambda off: pl.BlockSpec((lp, nh * MLA_DV), lambda b, h: (b, off + h))
    wide = pl.BlockSpec((lp, nh * QKW), lambda b, h: (b, h))
    return pl.pallas_call(
        body, name="mla_attn_fwd", grid=(bsz, MLA_HEADS // nh),
        in_specs=[wide, wide, head(0), head(C_MZ // (nh * MLA_DV))],
        out_specs=[head(0), head(0), pl.BlockSpec((1, nh, lp, LANE), lambda b, h: (b, h, 0, 0))],
        out_shape=[jax.ShapeDtypeStruct((tp, MLA_HEADS * MLA_DV), BF16),
                   jax.ShapeDtypeStruct((tp, MLA_HEADS * MLA_DV), BF16),
                   jax.ShapeDtypeStruct((bsz, MLA_HEADS, lp, LANE), F32)],
        compiler_params=_cp(("parallel", "parallel"), 56),
    )(qf, kf, vf, proj)


def _attn_bwd_blocks(lp):
    return [(0, X0)] + [(r0, min(MXU_DEPTH, lp - r0)) for r0 in range(X0, lp, MXU_DEPTH)]


def _attn_bwd(qf, kf, vf, d_o, lse, delta, bsz, lp):
    tp = bsz * lp

    def body(q_ref, k_ref, v_ref, do_ref, lse_ref, dl_ref, dq_ref, dk_ref, dv_ref, dk_acc, dv_acc):
        dk_acc[...] = jnp.zeros_like(dk_acc)
        dv_acc[...] = jnp.zeros_like(dv_acc)
        for r0, tq in _attn_bwd_blocks(lp):
            rows, kmax = slice(r0, r0 + tq), r0 + tq
            q, do = q_ref[rows, :], do_ref[rows, :]
            k, v = k_ref[0:kmax, :], v_ref[0:kmax, :]
            p = jnp.exp2(_dot_nt(q, k) * EXP2_SCALE - lse_ref[0, 0, rows, :][:, :1])
            p = _causal_fill(p, r0, 0.0)
            ds = _bf(p * (_dot_nt(do, v) - dl_ref[0, rows, :][:, :1]))
            dq_ref[rows, :] = _bf(_dot(ds, k) * ATT_SCALE)
            dk_acc[0:kmax, :] += _dot_tn(ds, q)
            dv_acc[0:kmax, :] += _dot_tn(_bf(p), do)
        dk_ref[...] = _bf(dk_acc[...] * ATT_SCALE)
        dv_ref[...] = _bf(dv_acc[...])

    wide = pl.BlockSpec((lp, QKW), lambda b, h: (b, h))
    narrow = pl.BlockSpec((lp, MLA_DV), lambda b, h: (b, h))
    stat = pl.BlockSpec((1, 1, lp, LANE), lambda b, h: (b, h, 0, 0))
    return pl.pallas_call(
        body, name="mla_attn_bwd", grid=(bsz, MLA_HEADS),
        in_specs=[wide, wide, narrow, narrow, stat, pl.BlockSpec((1, lp, LANE), lambda b, h: (h, b, 0))],
        out_specs=[wide, wide, narrow],
        out_shape=[jax.ShapeDtypeStruct((tp, MLA_HEADS * QKW), BF16), jax.ShapeDtypeStruct((tp, MLA_HEADS * QKW), BF16),
                   jax.ShapeDtypeStruct((tp, MLA_HEADS * MLA_DV), BF16)],
        scratch_shapes=[pltpu.VMEM((lp, QKW), F32), pltpu.VMEM((lp, MLA_DV), F32)],
        compiler_params=_cp(("parallel", "parallel"), 56),
    )(qf, kf, vf, d_o, lse, delta)


def _q_up_bwd(dqf, proj, q_norm_g, wn, wr, wt, cos_t, sin_t, dproj, bsz, lp):
    tp = bsz * lp
    tok = _attn_block(lp)
    nb = lp // tok
    hw = MLA_HEADS * LANE

    def body(dq_ref, cq_ref, g_ref, wn_ref, wr_ref, wt_ref, cos_ref, sin_ref, _,
             dcq_ref, dwn_ref, dwr_ref, dwt_ref, dg_ref):
        @pl.when(jnp.logical_and(pl.program_id(0) == 0, pl.program_id(1) == 0))
        def _():
            for r in (dwn_ref, dwr_ref, dwt_ref, dg_ref):
                r[...] = jnp.zeros_like(r)

        g = g_ref[...]
        xh, r = _rms_fwd(cq_ref[...].astype(F32))
        cqn = _bf(xh * g)
        dn = jnp.concatenate([dq_ref[:, h * QKW:h * QKW + LANE] for h in range(MLA_HEADS)], axis=1)
        dr = jnp.concatenate([dq_ref[:, h * QKW + LANE:(h + 1) * QKW] for h in range(MLA_HEADS)], axis=1).astype(F32)
        dr_c = _bf(dr * jnp.tile(cos_ref[...], (1, MLA_HEADS)))
        dr_s = _bf(dr * jnp.tile(sin_ref[...], (1, MLA_HEADS)))
        dcqn = _dot_nt(dn, wn_ref[...]) + _dot_nt(dr_c, wr_ref[...]) + _dot_nt(dr_s, wt_ref[...])
        dwn_ref[...] += _dot_tn(cqn, dn)
        dwr_ref[...] += _dot_tn(cqn, dr_c)
        dwt_ref[...] += _dot_tn(cqn, dr_s)
        dx, dg = _rms_bwd(dcqn, xh, r, g)
        dcq_ref[...] = _bf(dx)
        dg_ref[...] += dg

    aspec = pl.BlockSpec((MLA_QR, hw), lambda b, i: (0, 0))
    tspec = pl.BlockSpec((tok, LANE), lambda b, i: (i, 0))
    return pl.pallas_call(
        body, name="mla_q_up_bwd", grid=(bsz, nb),
        in_specs=[pl.BlockSpec((tok, MLA_HEADS * QKW), lambda b, i: (b * nb + i, 0)),
                  pl.BlockSpec((tok, MLA_QR), lambda b, i: (b * nb + i, C_CQ // MLA_QR)),
                  pl.BlockSpec((1, MLA_QR), lambda b, i: (0, 0)), aspec, aspec, aspec, tspec, tspec,
                  pl.BlockSpec(memory_space=pl.ANY)],
        out_specs=[pl.BlockSpec((tok, MLA_QR), lambda b, i: (b * nb + i, C_CQ // MLA_QR)), aspec, aspec, aspec,
                   pl.BlockSpec((1, MLA_QR), lambda b, i: (0, 0))],
        out_shape=[jax.ShapeDtypeStruct((tp, N_EXT), BF16)] + [jax.ShapeDtypeStruct((MLA_QR, hw), F32)] * 3
        + [jax.ShapeDtypeStruct((1, MLA_QR), F32)],
        input_output_aliases={8: 0},
        compiler_params=_cp(("arbitrary", "arbitrary")),
    )(dqf, proj, q_norm_g, wn, wr, wt, cos_t, sin_t, dproj)


def _kv_up_bwd(dkf, dvf, proj, kv_norm_g, wk, wv, cos_t, sin_t, d_lr, dproj, bsz, lp):
    tp = bsz * lp
    tok = _attn_block(lp)
    nb = lp // tok
    hw = MLA_HEADS * LANE

    def body(dk_ref, dv_ref, ckv_ref, g_ref, wk_ref, wv_ref, cos_ref, sin_ref, dlr_ref, _,
             dp_ref, dwk_ref, dwv_ref, dg_ref):
        dckv_ref, dkr_ref, dkrot_ref = (dp_ref.at[:, j * LANE:(j + 1) * LANE] for j in range(3))
        dp_ref[:, 3 * LANE:] = dlr_ref[...]
        @pl.when(jnp.logical_and(pl.program_id(0) == 0, pl.program_id(1) == 0))
        def _():
            for r in (dwk_ref, dwv_ref, dg_ref):
                r[...] = jnp.zeros_like(r)

        g = g_ref[...]
        xh, r = _rms_fwd(ckv_ref[...].astype(F32))
        cn = _bf(xh * g)
        dv = dv_ref[...]
        dn = jnp.concatenate([dk_ref[:, h * QKW:h * QKW + LANE] for h in range(MLA_HEADS)], axis=1)
        dcn = _dot_nt(dv, wv_ref[...]) + _dot_nt(dn, wk_ref[...])
        dwv_ref[...] += _dot_tn(cn, dv)
        dwk_ref[...] += _dot_tn(cn, dn)
        drope = jnp.zeros((tok, LANE), F32)
        for h in range(MLA_HEADS):
            drope += dk_ref[:, h * QKW + LANE:(h + 1) * QKW].astype(F32)
        dkr_ref[...] = _bf(drope * cos_ref[...])
        dkrot_ref[...] = _bf(drope * sin_ref[...])
        dx, dg = _rms_bwd(dcn, xh, r, g)
        dckv_ref[...] = _bf(dx)
        dg_ref[...] += dg

    aspec = pl.BlockSpec((MLA_KVR, hw), lambda b, i: (0, 0))
    tspec = pl.BlockSpec((tok, LANE), lambda b, i: (i, 0))
    ospec = pl.BlockSpec((tok, LANE), lambda b, i: (b * nb + i, 0))
    return pl.pallas_call(
        body, name="mla_kv_up_bwd", grid=(bsz, nb),
        in_specs=[pl.BlockSpec((tok, MLA_HEADS * QKW), lambda b, i: (b * nb + i, 0)),
                  pl.BlockSpec((tok, hw), lambda b, i: (b * nb + i, 0)),
                  pl.BlockSpec((tok, LANE), lambda b, i: (b * nb + i, C_CKV // LANE)),
                  pl.BlockSpec((1, MLA_KVR), lambda b, i: (0, 0)), aspec, aspec, tspec, tspec, ospec,
                  pl.BlockSpec(memory_space=pl.ANY)],
        out_specs=[pl.BlockSpec((tok, 4 * LANE), lambda b, i: (b * nb + i, C_CKV // (4 * LANE))), aspec, aspec,
                   pl.BlockSpec((1, MLA_KVR), lambda b, i: (0, 0))],
        out_shape=[jax.ShapeDtypeStruct((tp, N_EXT), BF16)] + [jax.ShapeDtypeStruct((MLA_KVR, hw), F32)] * 2
        + [jax.ShapeDtypeStruct((1, MLA_KVR), F32)],
        input_output_aliases={9: 0},
        compiler_params=_cp(("arbitrary", "arbitrary")),
    )(dkf, dvf, proj, kv_norm_g, wk, wv, cos_t, sin_t, d_lr, dproj)


def _mid_fwd(ya_in, yb_in, proj, hp, target, w_gp, w_mp, w_o, final_g, bsz, lp):
    tp = bsz * lp
    tm = _attn_block(lp)
    nb = lp // tm
    last = pl.cdiv(lp - X0, tm) - 1

    def body(ya_ref, yb_ref, gg_ref, gm_ref, h_ref, ta_ref, tb_ref, wgp_ref, wmp_ref, wo_ref, fg_ref,
             ya_out, yb_out, dh_ref, loss_ref, dfg_ref):
        @pl.when(jnp.logical_and(pl.program_id(0) == 0, pl.program_id(1) == 0))
        def _():
            loss_ref[...] = jnp.zeros_like(loss_ref)
            dfg_ref[...] = jnp.zeros_like(dfg_ref)

        y_a = _dot(ya_ref[...], wgp_ref[...])
        y_b = _dot(yb_ref[...], wmp_ref[...])
        ya_out[...] = _bf(y_a)
        yb_out[...] = _bf(y_b)
        merged = _sigmoid(gg_ref[...].astype(F32)) * y_a + _sigmoid(gm_ref[...].astype(F32)) * y_b
        h2 = h_ref[...] + _dot(_bf(merged), wo_ref[...])
        fg = fg_ref[...]
        xh, r = _rms_fwd(h2)
        pos = pl.program_id(1) * tm + lax.broadcasted_iota(jnp.int32, (tm, 1), 0)
        t = jnp.concatenate([ta_ref[0, tm - X0:, :], tb_ref[0, :tm - X0, :]], axis=0)
        err = jnp.where(pos >= X0, xh * fg - t, 0.0)
        loss_ref[...] += 0.5 * jnp.sum(jnp.mean(err * err, axis=-1, keepdims=True), axis=0, keepdims=True)
        dy = err * (1.0 / D_MODEL)
        dx, dfg = _rms_bwd(dy, xh, r, fg)
        dh_ref[...] = dx
        dfg_ref[...] += dfg

    tok = lambda c: pl.BlockSpec((tm, D_MODEL), lambda b, i: (b * nb + i, c))
    wspec = pl.BlockSpec((D_MODEL, D_MODEL), lambda b, i: (0, 0))
    return pl.pallas_call(
        body, name="mid_fwd", grid=(bsz, nb),
        in_specs=[tok(0), tok(0), tok(C_GG // D_MODEL), tok(C_GM // D_MODEL), tok(0),
                  pl.BlockSpec((1, tm, D_MODEL), lambda b, i: (b, jnp.maximum(i - 1, 0), 0)),
                  pl.BlockSpec((1, tm, D_MODEL), lambda b, i: (b, jnp.minimum(i, last), 0)),
                  wspec, wspec, wspec, pl.BlockSpec((1, D_MODEL), lambda b, i: (0, 0))],
        out_specs=[tok(0), tok(0), tok(0), pl.BlockSpec((1, LANE), lambda b, i: (0, 0)),
                   pl.BlockSpec((1, D_MODEL), lambda b, i: (0, 0))],
        out_shape=[jax.ShapeDtypeStruct((tp, D_MODEL), BF16), jax.ShapeDtypeStruct((tp, D_MODEL), BF16),
                   jax.ShapeDtypeStruct((tp, D_MODEL), F32), jax.ShapeDtypeStruct((1, LANE), F32),
                   jax.ShapeDtypeStruct((1, D_MODEL), F32)],
        compiler_params=_cp(("arbitrary", "arbitrary"), 48),
    )(ya_in, yb_in, proj, proj, hp, target, target, w_gp, w_mp, w_o, final_g)


def _mid_bwd(dh2, y_a, y_b, proj, ya_in, yb_in, o_b, w_o, w_gp, w_mp, bsz, lp):
    tp = bsz * lp
    tm = MXU_DEPTH if tp % MXU_DEPTH == 0 else _attn_block(lp)
    nsteps = tp // tm
    group = 3 * D_MODEL

    def body(dh_ref, ya_ref, yb_ref, mz_ref, gg_ref, gm_ref, yai_ref, ybi_ref, ob_ref, wo_ref, wgp_ref, wmp_ref,
             dyai_ref, do_ref, dp_ref, dl_ref, dwo_ref, dwgp_ref, dwmp_ref, a_o, a_gp, a_mp):
        @pl.when(pl.program_id(0) == 0)
        def _():
            for r in (a_o, a_gp, a_mp):
                r[...] = jnp.zeros_like(r)

        dh = _bf(dh_ref[...])
        dm = _dot_nt(dh, wo_ref[...])
        y_a, y_b = ya_ref[...].astype(F32), yb_ref[...].astype(F32)
        sg, sm = _sigmoid(gg_ref[...].astype(F32)), _sigmoid(gm_ref[...].astype(F32))
        d_ya, d_yb = _bf(sg * dm), _bf(sm * dm)
        dp_ref[:, D_MODEL:2 * D_MODEL] = _bf(dm * y_a * sg * (1.0 - sg))
        dp_ref[:, 2 * D_MODEL:] = _bf(dm * y_b * sm * (1.0 - sm))
        a_o[...] += _dot_tn(_bf(sg * y_a + sm * y_b), dh)
        a_gp[...] += _dot_tn(yai_ref[...], d_ya)
        a_mp[...] += _dot_tn(ybi_ref[...], d_yb)
        dyai_ref[...] = _bf(_dot_nt(d_ya, wgp_ref[...]))
        dy = _dot_nt(d_yb, wmp_ref[...])
        mz, o = mz_ref[...].astype(F32), ob_ref[...].astype(F32)
        s = _sigmoid(mz)
        do = _bf(dy * (mz * s))
        do_ref[...] = do
        dp_ref[:, :D_MODEL] = _bf(dy * o * (s * (1.0 + mz * (1.0 - s))))
        prod = do.astype(F32) * o
        for h in range(MLA_HEADS):
            dl = jnp.sum(prod[:, h * MLA_DV:(h + 1) * MLA_DV], axis=-1, keepdims=True)
            dl_ref[h] = jnp.broadcast_to(dl, (tm, LANE))

        @pl.when(pl.program_id(0) == nsteps - 1)
        def _():
            pltpu.sync_copy(a_o, dwo_ref)
            pltpu.sync_copy(a_gp, dwgp_ref)
            pltpu.sync_copy(a_mp, dwmp_ref)

    tok = lambda c: pl.BlockSpec((tm, D_MODEL), lambda i: (i, c))
    wspec = pl.BlockSpec((D_MODEL, D_MODEL), lambda i: (0, 0))
    anyspec = pl.BlockSpec(memory_space=pl.ANY)
    wshape = jax.ShapeDtypeStruct((D_MODEL, D_MODEL), F32)
    return pl.pallas_call(
        body, name="mid_bwd", grid=(nsteps,),
        in_specs=[tok(0), tok(0), tok(0), tok(C_MZ // D_MODEL), tok(C_GG // D_MODEL), tok(C_GM // D_MODEL),
                  tok(0), tok(0), tok(0), wspec, wspec, wspec],
        out_specs=[tok(0), tok(0), pl.BlockSpec((tm, group), lambda i: (i, C_MZ // group)),
                   pl.BlockSpec((MLA_HEADS, tm, LANE), lambda i: (0, i, 0)), anyspec, anyspec, anyspec],
        out_shape=[jax.ShapeDtypeStruct((tp, D_MODEL), BF16)] * 2 + [jax.ShapeDtypeStruct((tp, N_EXT), BF16),
                   jax.ShapeDtypeStruct((MLA_HEADS, tp, LANE), F32)] + [wshape] * 3,
        scratch_shapes=[pltpu.VMEM((D_MODEL, D_MODEL), F32)] * 3,
        compiler_params=_cp(("arbitrary",), 56),
    )(dh2, y_a, y_b, proj, proj, proj, ya_in, yb_in, o_b, w_o, w_gp, w_mp)


MESH_ID = pl.DeviceIdType.MESH
EXCHANGE_SEMS = [pltpu.SemaphoreType.DMA((N_DEV - 1,)), pltpu.SemaphoreType.DMA((N_DEV - 1,)), pltpu.SemaphoreType.DMA]


def _my_place():
    return lax.axis_index("x"), lax.axis_index("y"), lax.axis_index("c")


def _exchange(g_ref, recv_ref, send_sems, recv_sems, local_sem, start, same=False):
    x, y, c = _my_place()
    me = 4 * x + 2 * y + c
    own = pltpu.make_async_copy(g_ref if same else g_ref.at[me], recv_ref.at[me], local_sem)
    sends, lands = [], []
    for d in range(1, N_DEV):
        px = 1 - x if d & 4 else x
        py = 1 - y if d & 2 else y
        pc = 1 - c if d & 1 else c
        peer = 4 * px + 2 * py + pc
        for slot, group in ((me, sends),) if start else ((me, sends), (peer, lands)):
            group.append(pltpu.make_async_remote_copy(
                src_ref=g_ref if same else g_ref.at[peer], dst_ref=recv_ref.at[slot], send_sem=send_sems.at[d - 1],
                recv_sem=recv_sems.at[d - 1], device_id=(px, py, pc), device_id_type=MESH_ID))
    if start:
        own.start()
        for cp in sends:
            cp.start()
    else:
        for cp in lands:
            cp.wait_recv()
        for cp in sends:
            cp.wait_send()
        own.wait()


def _dw_in(u, dproj, slabs):
    tp = u.shape[0]
    tn = 3 * LANE
    nj = N_EXT // tn

    def body(u_ref, d_ref, g_ref, o_ref, recv_ref, send_sems, recv_sems, local_sem):
        j = pl.program_id(0)

        @pl.when(j == 0)
        def _():
            _exchange(g_ref, recv_ref, send_sems, recv_sems, local_sem, True)

        o_ref[...] = _dot_tn(d_ref[...], u_ref[...])

        @pl.when(j == nj - 1)
        def _():
            _exchange(g_ref, recv_ref, send_sems, recv_sems, local_sem, False)

    anyspec = pl.BlockSpec(memory_space=pl.ANY)
    return pl.pallas_call(
        body, name="dw_in", grid=(nj,),
        in_specs=[pl.BlockSpec((tp, D_MODEL), lambda j: (0, 0), pipeline_mode=pl.Buffered(1)),
                  pl.BlockSpec((tp, tn), lambda j: (0, j)), anyspec],
        out_specs=[pl.BlockSpec((tn, D_MODEL), lambda j: (j, 0)), anyspec],
        out_shape=[jax.ShapeDtypeStruct((N_EXT, D_MODEL), F32), jax.ShapeDtypeStruct(slabs.shape, slabs.dtype)],
        scratch_shapes=EXCHANGE_SEMS,
        compiler_params=_cp(("arbitrary",), 56),
    )(u, dproj, slabs)


def _dx_in(dproj, w_ext, hp, dh2, norm_g, slabs):
    tp = hp.shape[0]
    tm = 2 * TOK
    ni = tp // tm

    def body(d_ref, w_ref, h_ref, dh_ref, g_ref, s_ref, o_ref, dg_ref, recv_ref, send_sems, recv_sems, local_sem):
        i = pl.program_id(0)

        @pl.when(i == 0)
        def _():
            _exchange(s_ref, recv_ref, send_sems, recv_sems, local_sem, True)
            dg_ref[...] = jnp.zeros_like(dg_ref)

        du = _dot_nt(d_ref[...], w_ref[...])
        g = g_ref[...]
        xh, r = _rms_fwd(h_ref[...])
        dx, dg = _rms_bwd(du, xh, r, g)
        o_ref[...] = dh_ref[...] + dx
        dg_ref[...] += dg

        @pl.when(i == ni - 1)
        def _():
            _exchange(s_ref, recv_ref, send_sems, recv_sems, local_sem, False)

    tok = pl.BlockSpec((tm, D_MODEL), lambda i: (i, 0))
    anyspec = pl.BlockSpec(memory_space=pl.ANY)
    return pl.pallas_call(
        body, name="dx_in", grid=(ni,),
        in_specs=[pl.BlockSpec((tm, N_EXT), lambda i: (i, 0)),
                  pl.BlockSpec((D_MODEL, N_EXT), lambda i: (0, 0), pipeline_mode=pl.Buffered(1)),
                  tok, tok, pl.BlockSpec((1, D_MODEL), lambda i: (0, 0)), anyspec],
        out_specs=[tok, pl.BlockSpec((1, D_MODEL), lambda i: (0, 0)), anyspec],
        out_shape=[jax.ShapeDtypeStruct((tp, D_MODEL), F32), jax.ShapeDtypeStruct((1, D_MODEL), F32),
                   jax.ShapeDtypeStruct(slabs.shape, slabs.dtype)],
        scratch_shapes=EXCHANGE_SEMS,
        compiler_params=_cp(("arbitrary",), 56),
    )(dproj, w_ext, hp, dh2, norm_g, slabs)


def _meta_grad(dhp3):
    bsz = dhp3.shape[0]

    def body(d_ref, o_ref):
        @pl.when(pl.program_id(0) == 0)
        def _():
            o_ref[...] = jnp.zeros_like(o_ref)

        o_ref[...] += d_ref[0]

    return pl.pallas_call(
        body, name="meta_grad", grid=(bsz,),
        in_specs=[pl.BlockSpec((1, N_META, D_MODEL), lambda b: (b, FRONT // N_META, 0))],
        out_specs=pl.BlockSpec((N_META, D_MODEL), lambda b: (0, 0)),
        out_shape=jax.ShapeDtypeStruct((N_META, D_MODEL), F32),
        compiler_params=_cp(("arbitrary",)),
    )(dhp3)


W_IN_SHARD = N_IN // N_DEV


def _pad_lanes(a, width=LANE):
    return jnp.pad(a, [(0, 0)] * (a.ndim - 1) + [(0, width - a.shape[-1])])


def _rot_cols(w):
    half = w.shape[-1] // 2
    return jnp.concatenate([-w[..., half:], w[..., :half]], axis=-1)


def _unrot_cols(dw):
    half = dw.shape[-1] // 2
    return jnp.concatenate([dw[..., half:], -dw[..., :half]], axis=-1)


def _w_in_cols(shards, lo, hi):
    parts = []
    for k in range(lo // W_IN_SHARD, (hi - 1) // W_IN_SHARD + 1):
        a, b = max(lo, k * W_IN_SHARD), min(hi, (k + 1) * W_IN_SHARD)
        parts.append(shards[k][:, a - k * W_IN_SHARD:b - k * W_IN_SHARD])
    return parts[0] if len(parts) == 1 else jnp.concatenate(parts, axis=1)


def _w_in_ext(shards):
    c = lambda lo, hi: _w_in_cols(shards, lo, hi)
    kr = c(O_KR, O_MZ)
    return jnp.concatenate([
        c(O_V, O_LR), c(O_Z, O_CQ), c(O_Q, O_K), c(O_K, O_V), c(O_MZ, O_GG), c(O_GG, O_GM), c(O_GM, N_IN),
        c(O_CKV, O_KR), _pad_lanes(kr), _pad_lanes(_rot_cols(kr)), _pad_lanes(c(O_LR, O_Z)), c(O_CQ, O_CKV)], axis=1)


def _w_in_grad_t(dwt):
    g = lambda start, width: dwt[start:start + width]
    half = MLA_ROPE // 2
    krot = g(C_KROT, MLA_ROPE)
    kr = g(C_KR, MLA_ROPE) + jnp.concatenate([krot[half:], -krot[:half]], axis=0)
    return jnp.concatenate([
        g(C_Q, GLA_KW), g(C_K, GLA_KW), g(C_V, GLA_VW), g(C_LR, GLA_RANK), g(C_Z, GLA_VW), g(C_CQ, MLA_QR),
        g(C_CKV, MLA_KVR), kr, g(C_MZ, D_MODEL), g(C_GG, D_MODEL), g(C_GM, D_MODEL)], axis=0)


def _rope_tables(lp):
    inv = 1.0 / (ROPE_BASE ** (jnp.arange(0, MLA_ROPE, 2, dtype=F32) / MLA_ROPE))
    ang = (jnp.arange(lp, dtype=F32) - FRONT)[:, None] * inv[None, :]
    cos, sin = jnp.cos(ang), jnp.sin(ang)
    return _pad_lanes(jnp.concatenate([cos, cos], axis=1)), _pad_lanes(jnp.concatenate([sin, sin], axis=1))


def _local_step(x, loss_target, w):
    bsz, seq, _ = x.shape
    lp = X0 + seq
    tp = bsz * lp
    assert lp % TOK == 0 and lp % GLA_ROWS == 0
    meta = jnp.broadcast_to(w["meta_tokens"][None], (bsz, N_META, D_MODEL))
    hp = jnp.concatenate([jnp.zeros((bsz, FRONT, D_MODEL), F32), meta, x], axis=1).reshape(tp, D_MODEL)
    cos_t, sin_t = _rope_tables(lp)

    w_ext = _w_in_ext(w["w_in"])
    u, proj, packed_all = _proj_in(hp, w["norm_g"], w_ext, w["packed"])
    packed_all, off = packed_all.reshape(N_DEV, -1), 0
    for n, shape, axis in PACKED:
        size = shape[0] * shape[1]
        w[n] = _join8(packed_all[:, off:off + size].reshape((N_DEV,) + shape), axis)
        off += size
    gw_pad = jnp.pad(w["gla_gate_w"], ((0, LANE - GLA_RANK), (0, 0)))
    uq = w["mla_w_uq"].reshape(MLA_QR, MLA_HEADS, MLA_QK)
    rope_w = uq[:, :, MLA_NOPE:]
    hw = MLA_HEADS * LANE
    wn = uq[:, :, :MLA_NOPE].reshape(MLA_QR, hw)
    wr = _pad_lanes(rope_w).reshape(MLA_QR, hw)
    wt = _pad_lanes(_rot_cols(rope_w)).reshape(MLA_QR, hw)
    ukv = w["mla_w_ukv"].reshape(MLA_KVR, MLA_HEADS, MLA_NOPE + MLA_DV)
    wk = ukv[:, :, :MLA_NOPE].reshape(MLA_KVR, hw)
    wv = ukv[:, :, MLA_NOPE:].reshape(MLA_KVR, hw)

    o_raw, ya_in, s_all = _gla_fwd(proj, gw_pad, w["gla_gate_b"], w["gla_norm_g"], bsz, lp)
    qf = _q_up(proj, w["mla_q_norm_g"], wn, wr, wt, cos_t, sin_t, bsz, lp)
    kf, vf = _kv_up(proj, w["mla_kv_norm_g"], wk, wv, cos_t, sin_t, bsz, lp)
    o_b, yb_in, lse = _attn_fwd(qf, kf, vf, proj, bsz, lp)
    y_a, y_b, dh2, loss, d_final_g = _mid_fwd(ya_in, yb_in, proj, hp, loss_target, w["gla_proj"], w["mla_proj"],
                                              w["w_out"], w["final_norm_g"], bsz, lp)
    d_ya, d_o, dproj, delta, d_w_out, d_gla_proj, d_mla_proj = _mid_bwd(
        dh2, y_a, y_b, proj, ya_in, yb_in, o_b, w["w_out"], w["gla_proj"], w["mla_proj"], bsz, lp)
    dproj, d_gate, d_gla_norm = _gla_bwd(proj, gw_pad, w["gla_gate_b"], w["gla_norm_g"], o_raw, s_all, d_ya, dproj,
                                         bsz, lp)
    d_lr, d_gw_pad, d_gate_b = _gate_bwd(d_gate, proj, gw_pad)
    dqf, dkf, dvf = _attn_bwd(qf, kf, vf, d_o, lse, delta, bsz, lp)
    dproj, d_wn, d_wr, d_wt, d_qn = _q_up_bwd(dqf, proj, w["mla_q_norm_g"], wn, wr, wt, cos_t, sin_t, dproj,
                                              bsz, lp)
    dproj, d_wk, d_wv, d_kvn = _kv_up_bwd(dkf, dvf, proj, w["mla_kv_norm_g"], wk, wv, cos_t, sin_t, d_lr, dproj,
                                          bsz, lp)

    d_rope = (d_wr.reshape(MLA_QR, MLA_HEADS, LANE)[:, :, :MLA_ROPE]
              + _unrot_cols(d_wt.reshape(MLA_QR, MLA_HEADS, LANE)[:, :, :MLA_ROPE]))
    d_uq = jnp.concatenate([d_wn.reshape(MLA_QR, MLA_HEADS, LANE), d_rope], axis=-1).reshape(MLA_QR, MLA_HEADS * MLA_QK)
    d_ukv = jnp.concatenate([d_wk.reshape(MLA_KVR, MLA_HEADS, LANE), d_wv.reshape(MLA_KVR, MLA_HEADS, LANE)],
                            axis=-1).reshape(MLA_KVR, MLA_HEADS * (MLA_NOPE + MLA_DV))
    mats = dict(gla_gate_w=d_gw_pad[:GLA_RANK], gla_proj=d_gla_proj, mla_w_uq=d_uq, mla_w_ukv=d_ukv,
                mla_proj=d_mla_proj, w_out=d_w_out)
    packed = _pad_rows(jnp.concatenate([_split8(mats[n], axis).reshape(N_DEV, -1) for n, _, axis in PACKED], axis=1),
                       PACK_ROWS)
    d_w_ext_t, packed_parts = _dw_in(u, dproj, _bf(packed))
    w_in_slabs = _bf(_w_in_grad_t(d_w_ext_t).reshape(N_DEV, W_IN_SHARD, D_MODEL))
    d_hp, d_norm_g, w_in_parts = _dx_in(dproj, w_ext, hp, dh2, w["norm_g"], w_in_slabs)
    d_hp3 = d_hp.reshape(bsz, lp, D_MODEL)
    small = dict(meta_tokens=_meta_grad(d_hp3), norm_g=d_norm_g, gla_gate_b=d_gate_b, gla_norm_g=d_gla_norm,
                 mla_q_norm_g=d_qn, mla_kv_norm_g=d_kvn, final_norm_g=d_final_g)
    return loss, d_hp3[:, X0:, :], w_in_parts, packed_parts, small


PACKED = (("gla_gate_w", (GLA_RANK, GLA_KW // N_DEV), 1),
          ("gla_proj", (D_MODEL // N_DEV, D_MODEL), 0), ("mla_w_uq", (MLA_QR, MLA_HEADS * MLA_QK // N_DEV), 1),
          ("mla_w_ukv", (MLA_KVR, MLA_HEADS * (MLA_NOPE + MLA_DV) // N_DEV), 1),
          ("mla_proj", (D_MODEL // N_DEV, D_MODEL), 0), ("w_out", (D_MODEL // N_DEV, D_MODEL), 0))
REPLICATED = (("norm_g", D_MODEL), ("gla_gate_b", GLA_KW), ("gla_norm_g", GLA_DV), ("mla_q_norm_g", MLA_QR),
              ("mla_kv_norm_g", MLA_KVR), ("final_norm_g", D_MODEL))
PACK_ROWS = 3744
PACK_BLOCK = 1248
SMALL_ROWS = 48
LOSS_ROW = N_META + 25
W_IN_BLOCK = 128


def _all_gather(shards):
    n_arr = len(shards)

    def body(*refs):
        x_refs, out_refs = refs[:n_arr], refs[n_arr:2 * n_arr]
        send_sems, recv_sems, local_sems = refs[2 * n_arr:]
        x, y, c = _my_place()
        me, sibling = (x, y, c), (x, y, 1 - c)
        chips = [(1 - x, y), (x, 1 - y), (1 - x, 1 - y)]

        def copy(a, k, block, to, from_input=False):
            slab = out_refs[a].at[4 * block[0] + 2 * block[1] + block[2]]
            return pltpu.make_async_remote_copy(
                src_ref=x_refs[a] if from_input else slab, dst_ref=slab,
                send_sem=send_sems.at[7 * a + k], recv_sem=recv_sems.at[7 * a + k], device_id=to,
                device_id_type=MESH_ID)

        arrays = range(n_arr)
        mine = [pltpu.make_async_copy(x_refs[a], out_refs[a].at[4 * x + 2 * y + c], local_sems.at[a]) for a in arrays]
        for cp in mine:
            cp.start()
        first = [copy(a, 0, me, sibling, True) for a in arrays]
        first += [copy(a, 1 + j, me, (*chip, c), True) for j, chip in enumerate(chips) for a in arrays]
        for cp in first:
            cp.start()
        passed = []
        for j, chip in enumerate(chips):
            for a in arrays:
                copy(a, 1 + j, (*chip, c), me).wait_recv()
                passed.append(copy(a, 4 + j, (*chip, c), sibling))
                passed[-1].start()
        for a in arrays:
            copy(a, 0, sibling, me).wait_recv()
        for j, chip in enumerate(chips):
            for a in arrays:
                copy(a, 4 + j, (*chip, 1 - c), me).wait_recv()
        for cp in first + passed:
            cp.wait_send()
        for cp in mine:
            cp.wait()

    anyspec = pl.BlockSpec(memory_space=pl.ANY)
    return pl.pallas_call(
        body, name="weights_all_gather",
        out_shape=[jax.ShapeDtypeStruct((N_DEV,) + s.shape, s.dtype) for s in shards],
        in_specs=[anyspec] * n_arr, out_specs=[anyspec] * n_arr,
        scratch_shapes=[pltpu.SemaphoreType.DMA((7 * n_arr,)), pltpu.SemaphoreType.DMA((7 * n_arr,)),
                        pltpu.SemaphoreType.DMA((n_arr,))],
    )(*shards)


def _small_exchange(slabs):
    def body(g_ref, recv_ref, send_sems, recv_sems, local_sem):
        _exchange(g_ref, recv_ref, send_sems, recv_sems, local_sem, True)
        _exchange(g_ref, recv_ref, send_sems, recv_sems, local_sem, False)

    vmem = pl.BlockSpec(memory_space=pltpu.VMEM)
    return pl.pallas_call(
        body, name="small_exchange", out_shape=jax.ShapeDtypeStruct(slabs.shape, slabs.dtype),
        in_specs=[vmem], out_specs=vmem, scratch_shapes=EXCHANGE_SEMS,
    )(slabs)


def _adamw(parts, w, m, v, block_rows, name):
    rows, cols = w.shape

    def body(p_ref, w_ref, m_ref, v_ref, g_out, d_out, m_out, v_out):
        g = p_ref[0].astype(F32)
        for s in range(1, N_DEV):
            g = g + p_ref[s].astype(F32)
        m_new = ADAM_B1 * m_ref[...] + (1.0 - ADAM_B1) * g
        v_new = ADAM_B2 * v_ref[...] + (1.0 - ADAM_B2) * (g * g)
        m_hat = m_new / (1.0 - ADAM_B1 ** ADAM_STEP)
        v_hat = v_new / (1.0 - ADAM_B2 ** ADAM_STEP)
        g_out[...] = g
        d_out[...] = -ADAM_LR * (m_hat / (jnp.sqrt(v_hat) + ADAM_EPS) + ADAM_WD * w_ref[...])
        m_out[...] = m_new
        v_out[...] = v_new

    spec = pl.BlockSpec((block_rows, cols), lambda i: (i, 0))
    return pl.pallas_call(
        body, name=name, grid=(pl.cdiv(rows, block_rows),),
        in_specs=[pl.BlockSpec((N_DEV, block_rows, cols), lambda i: (0, i, 0)), spec, spec, spec],
        out_specs=[spec] * 4, out_shape=[jax.ShapeDtypeStruct((rows, cols), F32)] * 4,
        compiler_params=_cp(("parallel",), 48),
    )(parts, w, m, v)


def _pad_rows(flat, rows):
    pad = rows * LANE - flat.shape[-1]
    flat = jnp.pad(flat, [(0, 0)] * (flat.ndim - 1) + [(0, pad)])
    return flat.reshape(flat.shape[:-1] + (rows, LANE))


def _pack_shards(shards):
    return _pad_rows(jnp.concatenate([shards[n].reshape(-1) for n, _, _ in PACKED]), PACK_ROWS)


def _unpack_shards(packed):
    flat, out, off = packed.reshape(-1), {}, 0
    for n, shape, _ in PACKED:
        size = shape[0] * shape[1]
        out[n] = flat[off:off + size].reshape(shape)
        off += size
    return out


def _split8(full, axis):
    r, c = full.shape
    if axis == 0:
        return full.reshape(N_DEV, r // N_DEV, c)
    return full.reshape(r, N_DEV, c // N_DEV).transpose(1, 0, 2)


def _join8(shards, axis):
    _, r, c = shards.shape
    if axis == 0:
        return shards.reshape(N_DEV * r, c)
    return shards.transpose(1, 0, 2).reshape(r, N_DEV * c)


def _pack_small(meta_shard, vals, loss_row):
    rows = jnp.concatenate([vals[n].reshape(-1, LANE) for n, _ in REPLICATED] + [loss_row], axis=0)
    rows = jnp.pad(rows, ((0, SMALL_ROWS - N_META - rows.shape[0]), (0, 0)))
    return jnp.concatenate([meta_shard, jnp.broadcast_to(rows, meta_shard.shape[:-2] + rows.shape)], axis=-2)


def _unpack_small(packed):
    out, off = {"meta_tokens": packed[:N_META]}, N_META
    for n, size in REPLICATED:
        out[n] = packed[off:off + size // LANE].reshape(1, size)
        off += size // LANE
    return out


def kernel(x, meta_tokens, norm_g, w_in, gla_gate_w, gla_gate_b, gla_norm_g, gla_proj, mla_q_norm_g, mla_w_uq, mla_kv_norm_g, mla_w_ukv, mla_proj, w_out, final_norm_g, loss_target, m_meta_tokens, m_norm_g, m_w_in, m_gla_gate_w, m_gla_gate_b, m_gla_norm_g, m_gla_proj, m_mla_q_norm_g, m_mla_w_uq, m_mla_kv_norm_g, m_mla_w_ukv, m_mla_proj, m_w_out, m_final_norm_g, v_meta_tokens, v_norm_g, v_w_in, v_gla_gate_w, v_gla_gate_b, v_gla_norm_g, v_gla_proj, v_mla_q_norm_g, v_mla_w_uq, v_mla_kv_norm_g, v_mla_w_ukv, v_mla_proj, v_w_out, v_final_norm_g):
    given = dict(meta_tokens=meta_tokens, norm_g=norm_g, w_in=w_in, gla_gate_w=gla_gate_w, gla_gate_b=gla_gate_b,
                 gla_norm_g=gla_norm_g, gla_proj=gla_proj, mla_q_norm_g=mla_q_norm_g, mla_w_uq=mla_w_uq,
                 mla_kv_norm_g=mla_kv_norm_g, mla_w_ukv=mla_w_ukv, mla_proj=mla_proj, w_out=w_out,
                 final_norm_g=final_norm_g)
    mom_m = dict(meta_tokens=m_meta_tokens, norm_g=m_norm_g, w_in=m_w_in, gla_gate_w=m_gla_gate_w,
                 gla_gate_b=m_gla_gate_b, gla_norm_g=m_gla_norm_g, gla_proj=m_gla_proj, mla_q_norm_g=m_mla_q_norm_g,
                 mla_w_uq=m_mla_w_uq, mla_kv_norm_g=m_mla_kv_norm_g, mla_w_ukv=m_mla_w_ukv, mla_proj=m_mla_proj,
                 w_out=m_w_out, final_norm_g=m_final_norm_g)
    mom_v = dict(meta_tokens=v_meta_tokens, norm_g=v_norm_g, w_in=v_w_in, gla_gate_w=v_gla_gate_w,
                 gla_gate_b=v_gla_gate_b, gla_norm_g=v_gla_norm_g, gla_proj=v_gla_proj, mla_q_norm_g=v_mla_q_norm_g,
                 mla_w_uq=v_mla_w_uq, mla_kv_norm_g=v_mla_kv_norm_g, mla_w_ukv=v_mla_w_ukv, mla_proj=v_mla_proj,
                 w_out=v_w_out, final_norm_g=v_final_norm_g)
    shapes = {n: a.shape for n, a in given.items()}
    shard2d = {n: s for n, s, _ in PACKED}
    shard2d["w_in"] = (D_MODEL, W_IN_SHARD)
    shard2d["meta_tokens"] = (N_META, LANE)

    def as2d(tree):
        out = {n: tree[n].reshape(shard2d[n]) for n in shard2d}
        out.update({n: tree[n].reshape(1, size) for n, size in REPLICATED})
        return out

    w_loc, m_loc, v_loc = as2d(given), as2d(mom_m), as2d(mom_v)

    w_in_all, meta_all = _all_gather([w_loc["w_in"].astype(BF16), w_loc["meta_tokens"]])
    flat = jnp.concatenate([w_loc[n].astype(BF16).reshape(-1) for n, _, _ in PACKED])
    full = {"w_in": w_in_all, "meta_tokens": _join8(meta_all, 1), "packed": _pad_rows(flat, PACK_ROWS)}
    for n, _ in REPLICATED:
        full[n] = w_loc[n]

    loss_part, grad_x, w_in_parts, packed_parts, small = _local_step(x, loss_target, full)
    small_all = _small_exchange(_pack_small(_split8(small["meta_tokens"], 1), small,
                                            jnp.broadcast_to(loss_part[:, :1], (1, LANE))))

    w_in_t = [t["w_in"].T for t in (w_loc, m_loc, v_loc)]
    g_w, d_w, m_w, v_w = (o.T for o in _adamw(w_in_parts, *w_in_t, W_IN_BLOCK, "adamw_w_in"))
    g_p, d_p, m_p, v_p = _adamw(packed_parts, _pack_shards(w_loc), _pack_shards(m_loc), _pack_shards(v_loc),
                                PACK_BLOCK, "adamw_packed")
    zero_row = jnp.zeros((1, LANE), F32)
    g_s, d_s, m_s, v_s = _adamw(small_all, *(_pack_small(t["meta_tokens"], t, zero_row) for t in (w_loc, m_loc, v_loc)),
                                SMALL_ROWS, "adamw_small")
    loss = g_s[LOSS_ROW, 0]

    order = ["meta_tokens", "norm_g", "w_in", "gla_gate_w", "gla_gate_b", "gla_norm_g", "gla_proj", "mla_q_norm_g",
             "mla_w_uq", "mla_kv_norm_g", "mla_w_ukv", "mla_proj", "w_out", "final_norm_g"]
    result = [loss, grad_x]
    for w_in_out, packed_sh, packed_sm in ((g_w, g_p, g_s), (d_w, d_p, d_s), (m_w, m_p, m_s), (v_w, v_p, v_s)):
        tree = _unpack_shards(packed_sh)
        tree.update(_unpack_small(packed_sm))
        tree["w_in"] = w_in_out
        result += [tree[n].reshape(shapes[n]) for n in order]
    return tuple(result)
```

```python
import jax
import jax.numpy as jnp
from jax import lax
from jax.experimental import pallas as pl
from jax.experimental.pallas import tpu as pltpu

F32 = jnp.float32
BF16 = jnp.bfloat16

D_MODEL = 1024
N_META = 16
EPS = 1e-6
FRONT = 48
X0 = FRONT + N_META
GLA_HEADS, GLA_DK, GLA_DV, GLA_RANK, GLA_CHUNK = 4, 128, 256, 16, 64
GLA_GATE_NORMALIZER = 16.0
GLA_KW = GLA_HEADS * GLA_DK
GLA_VW = GLA_HEADS * GLA_DV
MLA_HEADS, MLA_NOPE, MLA_ROPE, MLA_DV, MLA_QR, MLA_KVR = 8, 128, 64, 128, 256, 128
MLA_QK = MLA_NOPE + MLA_ROPE
ROPE_BASE = 10000.0
LANE = 128
QKW = 2 * LANE

C_V, C_Z, C_Q, C_K = 0, 1024, 2048, 2560
C_MZ, C_GG, C_GM = 3072, 4096, 5120
C_CKV, C_KR, C_KROT, C_LR = 6144, 6272, 6400, 6528
C_CQ = 6656
N_EXT = 6912
O_Q, O_K, O_V, O_LR, O_Z, O_CQ, O_CKV, O_KR, O_MZ, O_GG, O_GM, N_IN = (
    0, 512, 1024, 2048, 2064, 3088, 3344, 3472, 3536, 4560, 5584, 6608)

ADAM_LR, ADAM_B1, ADAM_B2, ADAM_EPS, ADAM_WD, ADAM_STEP = 0.001, 0.9, 0.999, 1e-08, 0.01, 10

N_DEV = 8
TOK = 192
ATT_BLOCK = 352
EXT_BLOCK = 1152
MXU_DEPTH = 256


def _cp(sems=None, vmem_mb=None):
    kw = {}
    if sems is not None:
        kw["dimension_semantics"] = sems
    if vmem_mb is not None:
        kw["vmem_limit_bytes"] = vmem_mb * 1024 * 1024
    return pltpu.CompilerParams(**kw)


def _dot(a, b):
    return jnp.dot(a, b, preferred_element_type=F32)


def _dot_nt(a, b):
    return lax.dot_general(a, b, (((1,), (1,)), ((), ())), preferred_element_type=F32)


def _dot_tn(a, b):
    return lax.dot_general(a, b, (((0,), (0,)), ((), ())), preferred_element_type=F32)


def _sigmoid(x):
    return 1.0 / (1.0 + jnp.exp(-x))


def _bf(x):
    return x.astype(BF16)


def _big_tok(tp):
    return 4 * TOK if tp % (4 * TOK) == 0 else TOK


def _attn_block(lp):
    return ATT_BLOCK if lp % ATT_BLOCK == 0 else TOK


def _proj_in(hp, norm_g, w_ext, packed):
    tp = hp.shape[0]
    tm = 2 * TOK
    ni = tp // tm

    def body(h_ref, g_ref, w_ref, p_ref, u_ref, o_ref, pall_ref, send_sems, recv_sems, local_sem):
        i = pl.program_id(0)

        @pl.when(i == 0)
        def _():
            _exchange(p_ref, pall_ref, send_sems, recv_sems, local_sem, True, same=True)

        x = h_ref[...]
        r = lax.rsqrt(jnp.mean(x * x, axis=-1, keepdims=True) + EPS)
        u = _bf(x * r * g_ref[...])
        u_ref[...] = u
        o_ref[...] = _bf(_dot(u, w_ref[...]))

        @pl.when(i == ni - 1)
        def _():
            _exchange(p_ref, pall_ref, send_sems, recv_sems, local_sem, False, same=True)

    anyspec = pl.BlockSpec(memory_space=pl.ANY)
    return pl.pallas_call(
        body, name="proj_in", grid=(ni,),
        in_specs=[pl.BlockSpec((tm, D_MODEL), lambda i: (i, 0)),
                  pl.BlockSpec((1, D_MODEL), lambda i: (0, 0)),
                  pl.BlockSpec((D_MODEL, N_EXT), lambda i: (0, 0), pipeline_mode=pl.Buffered(1)), anyspec],
        out_specs=[pl.BlockSpec((tm, D_MODEL), lambda i: (i, 0)),
                   pl.BlockSpec((tm, N_EXT), lambda i: (i, 0)), anyspec],
        out_shape=[jax.ShapeDtypeStruct((tp, D_MODEL), BF16), jax.ShapeDtypeStruct((tp, N_EXT), BF16),
                   jax.ShapeDtypeStruct((N_DEV,) + packed.shape, packed.dtype)],
        scratch_shapes=EXCHANGE_SEMS,
        compiler_params=_cp(("arbitrary",), 56),
    )(hp, norm_g, w_ext, packed)


def _gla_group(n_chunks):
    return 11 if n_chunks % 11 == 0 else 3


def _tri_dot(tri, x):
    hi = _bf(x)
    rest = x - hi.astype(F32)
    mid = _bf(rest)
    return _dot(tri, hi) + _dot(tri, mid) + _dot(tri, _bf(rest - mid.astype(F32)))


def _gla_gates(q_ref, k_ref, lr_ref, gw_ref, gb_ref, rows, not_first):
    z = _dot(lr_ref[rows, :], gw_ref[...]) + gb_ref[...]
    logsig = jnp.minimum(z, 0.0) - jnp.log(1.0 + jnp.exp(-jnp.abs(z)))
    row = lax.broadcasted_iota(jnp.int32, (GLA_CHUNK, GLA_KW), 0)
    live = jnp.logical_or(not_first, row >= FRONT)
    g = jnp.where(live, logsig * (1.0 / GLA_GATE_NORMALIZER), 0.0)
    ri = lax.broadcasted_iota(jnp.int32, (GLA_CHUNK, GLA_CHUNK), 0)
    ci = lax.broadcasted_iota(jnp.int32, (GLA_CHUNK, GLA_CHUNK), 1)
    tril = ci <= ri
    b = _tri_dot(_bf(tril.astype(F32)), g)
    bl = jnp.sum(jnp.where(row == GLA_CHUNK - 1, b, 0.0), axis=0, keepdims=True)
    eb, enb, elb, ebl = jnp.exp(b), jnp.exp(-b), jnp.exp(bl - b), jnp.exp(bl)
    q = q_ref[rows, :].astype(F32) * (GLA_DK ** -0.5)
    k = k_ref[rows, :].astype(F32)
    qe, ke, kl = q * eb, k * enb, k * elb
    return dict(z=z, live=live, tril=tril, row=row, eb=eb, enb=enb, elb=elb, ebl=ebl, qe=qe, ke=ke, kl=kl,
                qe_b=_bf(qe), ke_b=_bf(ke), kl_b=_bf(kl))


def _gla_in_specs(n_groups, gla_rows, rev):
    def rb(b, n):
        return b * n_groups + ((n_groups - 1 - n) if rev else n)

    return rb, [pl.BlockSpec((gla_rows, GLA_KW), lambda b, n: (rb(b, n), C_Q // GLA_KW)),
                pl.BlockSpec((gla_rows, GLA_KW), lambda b, n: (rb(b, n), C_K // GLA_KW)),
                pl.BlockSpec((gla_rows, GLA_VW), lambda b, n: (rb(b, n), C_V // GLA_VW)),
                pl.BlockSpec((gla_rows, GLA_VW), lambda b, n: (rb(b, n), C_Z // GLA_VW)),
                pl.BlockSpec((gla_rows, LANE), lambda b, n: (rb(b, n), C_LR // LANE)),
                pl.BlockSpec((LANE, GLA_KW), lambda b, n: (0, 0)),
                pl.BlockSpec((1, GLA_KW), lambda b, n: (0, 0)),
                pl.BlockSpec((1, GLA_DV), lambda b, n: (0, 0))]


def _gla_fwd(proj, gw_pad, gate_b, gla_norm_g, bsz, lp):
    n_chunks = lp // GLA_CHUNK
    gla_group = _gla_group(n_chunks)
    gla_rows = gla_group * GLA_CHUNK
    n_groups = n_chunks // gla_group
    tp = bsz * lp

    def body(q_ref, k_ref, v_ref, z_ref, lr_ref, gw_ref, gb_ref, gn_ref, oraw_ref, ya_ref, sall_ref, st_scr):
        grp = pl.program_id(1)

        @pl.when(grp == 0)
        def _():
            st_scr[...] = jnp.zeros_like(st_scr)

        chunks = [slice(j * GLA_CHUNK, (j + 1) * GLA_CHUNK) for j in range(gla_group)]
        cs = [_gla_gates(q_ref, k_ref, lr_ref, gw_ref, gb_ref, rows, True if j else grp > 0)
              for j, rows in enumerate(chunks)]
        gn = gn_ref[...]
        sts = [st_scr[h] for h in range(GLA_HEADS)]
        for j, (rows, c) in enumerate(zip(chunks, cs)):
            for h in range(GLA_HEADS):
                ks, vs = slice(h * GLA_DK, (h + 1) * GLA_DK), slice(h * GLA_DV, (h + 1) * GLA_DV)
                st = sts[h]
                sall_ref[0, j, h] = st
                v = v_ref[rows, vs]
                a = jnp.where(c["tril"], _dot_nt(c["qe_b"][:, ks], c["ke_b"][:, ks]), 0.0)
                o = _dot(_bf(a), v) + _dot_nt(c["qe_b"][:, ks], _bf(st))
                sts[h] = st * c["ebl"][:, ks] + _dot_tn(v, c["kl_b"][:, ks])
                oraw_ref[rows, vs] = o
                r = lax.rsqrt(jnp.mean(o * o, axis=-1, keepdims=True) + EPS)
                zg = z_ref[rows, vs].astype(F32)
                ya_ref[rows, vs] = _bf((o * r * gn) * (zg * _sigmoid(zg)))
        for h in range(GLA_HEADS):
            st_scr[h] = sts[h]

    rb, in_specs = _gla_in_specs(n_groups, gla_rows, False)
    return pl.pallas_call(
        body, name="gla_fwd", grid=(bsz, n_groups), in_specs=in_specs,
        out_specs=[pl.BlockSpec((gla_rows, GLA_VW), lambda b, n: (rb(b, n), 0)),
                   pl.BlockSpec((gla_rows, GLA_VW), lambda b, n: (rb(b, n), 0)),
                   pl.BlockSpec((1, gla_group, GLA_HEADS, GLA_DV, GLA_DK), lambda b, n: (b, n, 0, 0, 0))],
        out_shape=[jax.ShapeDtypeStruct((tp, GLA_VW), F32), jax.ShapeDtypeStruct((tp, GLA_VW), BF16),
                   jax.ShapeDtypeStruct((bsz, n_chunks, GLA_HEADS, GLA_DV, GLA_DK), F32)],
        scratch_shapes=[pltpu.VMEM((GLA_HEADS, GLA_DV, GLA_DK), F32)],
        compiler_params=_cp(("parallel", "arbitrary"), 56),
    )(proj, proj, proj, proj, proj, gw_pad, gate_b, gla_norm_g)


def _gla_bwd(proj, gw_pad, gate_b, gla_norm_g, o_raw, s_all, d_ya, dproj, bsz, lp):
    n_chunks = lp // GLA_CHUNK
    gla_group = _gla_group(n_chunks)
    gla_rows = gla_group * GLA_CHUNK
    n_groups = n_chunks // gla_group
    tp = bsz * lp

    def body(q_ref, k_ref, v_ref, z_ref, lr_ref, gw_ref, gb_ref, gn_ref, o_ref, s_ref, dya_ref, _,
             dp_ref, dz_ref, dgn_ref, dst_scr):
        dv_ref, dzg_ref = dp_ref.at[:, C_V:C_V + GLA_VW], dp_ref.at[:, C_Z:C_Z + GLA_VW]

        @pl.when(jnp.logical_and(pl.program_id(0) == 0, pl.program_id(1) == 0))
        def _():
            dgn_ref[...] = jnp.zeros_like(dgn_ref)

        @pl.when(pl.program_id(1) == 0)
        def _():
            dst_scr[...] = jnp.zeros_like(dst_scr)

        grp = n_groups - 1 - pl.program_id(1)
        chunks = [slice(j * GLA_CHUNK, (j + 1) * GLA_CHUNK) for j in range(gla_group)]
        cs = [_gla_gates(q_ref, k_ref, lr_ref, gw_ref, gb_ref, rows, True if j else grp > 0)
              for j, rows in enumerate(chunks)]
        gn = gn_ref[...]
        dgn = jnp.zeros((1, GLA_DV), F32)
        dqe_h, dke_h, dkl_h, dbl_h = ([[None] * GLA_HEADS for _ in chunks] for _ in range(4))
        dsts = [dst_scr[h] for h in range(GLA_HEADS)]
        for j in reversed(range(gla_group)):
            rows, c = chunks[j], cs[j]
            for h in range(GLA_HEADS):
                ks, vs = slice(h * GLA_DK, (h + 1) * GLA_DK), slice(h * GLA_DV, (h + 1) * GLA_DV)
                dst = dsts[h]
                v = v_ref[rows, vs]
                st = s_ref[0, j, h]
                o = o_ref[rows, vs]
                r = lax.rsqrt(jnp.mean(o * o, axis=-1, keepdims=True) + EPS)
                xh = o * r
                zg = z_ref[rows, vs].astype(F32)
                sg = _sigmoid(zg)
                dy = dya_ref[rows, vs].astype(F32)
                dzg_ref[rows, vs] = _bf(dy * (xh * gn) * (sg * (1.0 + zg * (1.0 - sg))))
                t = dy * (zg * sg)
                dgn += jnp.sum(t * xh, axis=0, keepdims=True)
                dxh = t * gn
                do_b = _bf(r * (dxh - xh * jnp.mean(dxh * xh, axis=-1, keepdims=True)))
                qe_b, ke_b, kl_b, dst_b = c["qe_b"][:, ks], c["ke_b"][:, ks], c["kl_b"][:, ks], _bf(dst)
                a = jnp.where(c["tril"], _dot_nt(qe_b, ke_b), 0.0)
                da_b = _bf(jnp.where(c["tril"], _dot_nt(do_b, v), 0.0))
                dqe_h[j][h] = _dot(da_b, ke_b) + _dot(do_b, _bf(st))
                dke_h[j][h] = _dot_tn(da_b, qe_b)
                dkl = _dot(v, dst_b)
                dkl_h[j][h] = dkl
                dv_ref[rows, vs] = _bf(_dot_tn(_bf(a), do_b) + _dot_nt(kl_b, dst_b))
                ddecay = jnp.sum(dst * st, axis=0, keepdims=True)
                dbl_h[j][h] = jnp.sum(dkl * c["kl"][:, ks], axis=0, keepdims=True) + ddecay * c["ebl"][:, ks]
                dsts[h] = dst * c["ebl"][:, ks] + _dot_tn(do_b, qe_b)
        for h in range(GLA_HEADS):
            dst_scr[h] = dsts[h]
        dgn_ref[...] += dgn
        ri = lax.broadcasted_iota(jnp.int32, (GLA_CHUNK, GLA_CHUNK), 0)
        ci = lax.broadcasted_iota(jnp.int32, (GLA_CHUNK, GLA_CHUNK), 1)
        triu = _bf((ci >= ri).astype(F32))
        for j, (rows, c) in enumerate(zip(chunks, cs)):
            dqe, dke, dkl, dbl = (jnp.concatenate(p[j], axis=1) for p in (dqe_h, dke_h, dkl_h, dbl_h))
            db = dqe * c["qe"] - dke * c["ke"] - dkl * c["kl"] + jnp.where(c["row"] == GLA_CHUNK - 1, dbl, 0.0)
            dg = _tri_dot(triu, db)
            dg = jnp.where(c["live"], dg, 0.0)
            dz_ref[rows, :] = dg * (1.0 / GLA_GATE_NORMALIZER) * _sigmoid(-c["z"])
            dp_ref[rows, C_Q:C_Q + GLA_KW] = _bf(dqe * c["eb"] * (GLA_DK ** -0.5))
            dp_ref[rows, C_K:C_K + GLA_KW] = _bf(dke * c["enb"] + dkl * c["elb"])

    rb, in_specs = _gla_in_specs(n_groups, gla_rows, True)
    wide = pl.BlockSpec((gla_rows, GLA_VW), lambda b, n: (rb(b, n), 0))
    group = C_MZ
    return pl.pallas_call(
        body, name="gla_bwd", grid=(bsz, n_groups),
        in_specs=in_specs + [wide, pl.BlockSpec((1, gla_group, GLA_HEADS, GLA_DV, GLA_DK),
                                                lambda b, n: (b, n_groups - 1 - n, 0, 0, 0)), wide,
                             pl.BlockSpec(memory_space=pl.ANY)],
        out_specs=[pl.BlockSpec((gla_rows, group), lambda b, n: (rb(b, n), 0)),
                   pl.BlockSpec((gla_rows, GLA_KW), lambda b, n: (rb(b, n), 0)),
                   pl.BlockSpec((1, GLA_DV), lambda b, n: (0, 0))],
        out_shape=[jax.ShapeDtypeStruct((tp, N_EXT), BF16), jax.ShapeDtypeStruct((tp, GLA_KW), F32),
                   jax.ShapeDtypeStruct((1, GLA_DV), F32)],
        input_output_aliases={11: 0},
        scratch_shapes=[pltpu.VMEM((GLA_HEADS, GLA_DV, GLA_DK), F32)],
        compiler_params=_cp(("arbitrary", "arbitrary"), 56),
    )(proj, proj, proj, proj, proj, gw_pad, gate_b, gla_norm_g, o_raw, s_all, d_ya, dproj)


def _gate_bwd(dz, proj, gw_pad):
    tp = dz.shape[0]
    tm = _big_tok(tp)

    def body(dz_ref, lr_ref, gw_ref, dlr_ref, dgw_ref, dgb_ref):
        @pl.when(pl.program_id(0) == 0)
        def _():
            dgw_ref[...] = jnp.zeros_like(dgw_ref)
            dgb_ref[...] = jnp.zeros_like(dgb_ref)

        dz = dz_ref[...]
        dz_b = _bf(dz)
        dlr_ref[...] = _bf(_dot_nt(dz_b, gw_ref[...]))
        dgw_ref[...] += _dot_tn(lr_ref[...], dz_b)
        dgb_ref[...] += jnp.sum(dz, axis=0, keepdims=True)

    return pl.pallas_call(
        body, name="gate_bwd", grid=(tp // tm,),
        in_specs=[pl.BlockSpec((tm, GLA_KW), lambda i: (i, 0)),
                  pl.BlockSpec((tm, LANE), lambda i: (i, C_LR // LANE)),
                  pl.BlockSpec((LANE, GLA_KW), lambda i: (0, 0))],
        out_specs=[pl.BlockSpec((tm, LANE), lambda i: (i, 0)),
                   pl.BlockSpec((LANE, GLA_KW), lambda i: (0, 0)),
                   pl.BlockSpec((1, GLA_KW), lambda i: (0, 0))],
        out_shape=[jax.ShapeDtypeStruct((tp, LANE), BF16), jax.ShapeDtypeStruct((LANE, GLA_KW), F32),
                   jax.ShapeDtypeStruct((1, GLA_KW), F32)],
        compiler_params=_cp(("arbitrary",)),
    )(dz, proj, gw_pad)


def _rms_fwd(x):
    r = lax.rsqrt(jnp.mean(x * x, axis=-1, keepdims=True) + EPS)
    return x * r, r


def _rms_bwd(dy, xh, r, g):
    dxh = dy * g
    dx = r * (dxh - xh * jnp.mean(dxh * xh, axis=-1, keepdims=True))
    return dx, jnp.sum(dy * xh, axis=0, keepdims=True)


def _q_up(proj, q_norm_g, wn, wr, wt, cos_t, sin_t, bsz, lp):
    tp = bsz * lp
    tok = _attn_block(lp)
    nb = lp // tok

    def body(cq_ref, g_ref, wn_ref, wr_ref, wt_ref, cos_ref, sin_ref, q_ref):
        xh, _ = _rms_fwd(cq_ref[...].astype(F32))
        cqn = _bf(xh * g_ref[...])
        nope = _dot(cqn, wn_ref[...])
        rope = _dot(cqn, wr_ref[...])
        rot = _dot(cqn, wt_ref[...])
        cos, sin = cos_ref[...], sin_ref[...]
        one = (lax.broadcasted_iota(jnp.int32, (tok, LANE), 1) == BIAS_LANE).astype(F32)
        for h in range(MLA_HEADS):
            sl = slice(h * LANE, (h + 1) * LANE)
            q_ref[:, h * QKW:h * QKW + LANE] = _bf(nope[:, sl])
            q_ref[:, h * QKW + LANE:(h + 1) * QKW] = _bf(rope[:, sl] * cos + rot[:, sl] * sin + one)

    wspec = pl.BlockSpec((MLA_QR, MLA_HEADS * LANE), lambda b, i: (0, 0))
    tspec = pl.BlockSpec((tok, LANE), lambda b, i: (i, 0))
    return pl.pallas_call(
        body, name="mla_q_up", grid=(bsz, nb),
        in_specs=[pl.BlockSpec((tok, MLA_QR), lambda b, i: (b * nb + i, C_CQ // MLA_QR)),
                  pl.BlockSpec((1, MLA_QR), lambda b, i: (0, 0)), wspec, wspec, wspec, tspec, tspec],
        out_specs=pl.BlockSpec((tok, MLA_HEADS * QKW), lambda b, i: (b * nb + i, 0)),
        out_shape=jax.ShapeDtypeStruct((tp, MLA_HEADS * QKW), BF16),
        compiler_params=_cp(("parallel", "parallel")),
    )(proj, q_norm_g, wn, wr, wt, cos_t, sin_t)


def _kv_up(proj, kv_norm_g, wk, wv, cos_t, sin_t, bsz, lp):
    tp = bsz * lp
    tok = _attn_block(lp)
    nb = lp // tok

    def body(ckv_ref, kr_ref, krot_ref, g_ref, wk_ref, wv_ref, cos_ref, sin_ref, k_ref, v_ref):
        xh, _ = _rms_fwd(ckv_ref[...].astype(F32))
        cn = _bf(xh * g_ref[...])
        kn = _dot(cn, wk_ref[...])
        v_ref[...] = _bf(_dot(cn, wv_ref[...]))
        pos = pl.program_id(1) * tok + lax.broadcasted_iota(jnp.int32, (tok, LANE), 0)
        lane = lax.broadcasted_iota(jnp.int32, (tok, LANE), 1)
        bias = jnp.where(jnp.logical_and(lane == BIAS_LANE, pos < FRONT), KEY_BIAS, 0.0)
        kr = _bf(kr_ref[...].astype(F32) * cos_ref[...] + krot_ref[...].astype(F32) * sin_ref[...] + bias)
        for h in range(MLA_HEADS):
            k_ref[:, h * QKW:h * QKW + LANE] = _bf(kn[:, h * LANE:(h + 1) * LANE])
            k_ref[:, h * QKW + LANE:(h + 1) * QKW] = kr

    wspec = pl.BlockSpec((MLA_KVR, MLA_HEADS * LANE), lambda b, i: (0, 0))
    tspec = pl.BlockSpec((tok, LANE), lambda b, i: (i, 0))
    return pl.pallas_call(
        body, name="mla_kv_up", grid=(bsz, nb),
        in_specs=[pl.BlockSpec((tok, LANE), lambda b, i: (b * nb + i, C_CKV // LANE)),
                  pl.BlockSpec((tok, LANE), lambda b, i: (b * nb + i, C_KR // LANE)),
                  pl.BlockSpec((tok, LANE), lambda b, i: (b * nb + i, C_KROT // LANE)),
                  pl.BlockSpec((1, MLA_KVR), lambda b, i: (0, 0)), wspec, wspec, tspec, tspec],
        out_specs=[pl.BlockSpec((tok, MLA_HEADS * QKW), lambda b, i: (b * nb + i, 0)),
                   pl.BlockSpec((tok, MLA_HEADS * LANE), lambda b, i: (b * nb + i, 0))],
        out_shape=[jax.ShapeDtypeStruct((tp, MLA_HEADS * QKW), BF16),
                   jax.ShapeDtypeStruct((tp, MLA_HEADS * LANE), BF16)],
        compiler_params=_cp(("parallel", "parallel")),
    )(proj, proj, proj, kv_norm_g, wk, wv, cos_t, sin_t)


ATT_SCALE = MLA_QK ** -0.5


KEY_BIAS = -1e30
BIAS_LANE = MLA_ROPE
NEG = 2 * KEY_BIAS
LOG2E = 1.4426950408889634
EXP2_SCALE = ATT_SCALE * LOG2E


def _causal_fill(s, r0, fill):
    tq, kmax = s.shape
    a = r0 // LANE * LANE
    mask = (a + lax.broadcasted_iota(jnp.int32, (tq, kmax - a), 1)
            <= r0 + lax.broadcasted_iota(jnp.int32, (tq, kmax - a), 0))
    right = jnp.where(mask, s[:, a:], fill)
    return jnp.concatenate([s[:, :a], right], axis=1) if a else right


def _attn_fwd(qf, kf, vf, proj, bsz, lp):
    tp = bsz * lp
    tq = _attn_block(lp)
    nh = 2

    def body(q_ref, k_ref, v_ref, mz_ref, ob_ref, yb_ref, lse_ref):
        starts = list(range(0, lp, tq))
        for pair in (starts[i:i + 2] for i in range(0, len(starts), 2)):
            work = [(r0, h) for r0 in pair for h in range(nh)]
            ss = [_causal_fill(_dot_nt(q_ref[r0:r0 + tq, h * QKW:(h + 1) * QKW],
                                       k_ref[0:r0 + tq, h * QKW:(h + 1) * QKW]), r0, NEG) for r0, h in work]
            ms = [jnp.max(s, axis=-1, keepdims=True) for s in ss]
            ps = [jnp.exp2((s - m) * EXP2_SCALE) for s, m in zip(ss, ms)]
            ls = [jnp.sum(p, axis=-1, keepdims=True) for p in ps]
            for (r0, h), p, m, l in zip(work, ps, ms, ls):
                rows, cols = slice(r0, r0 + tq), slice(h * MLA_DV, (h + 1) * MLA_DV)
                o = _dot(_bf(p), v_ref[0:r0 + tq, cols]) / l
                ob_ref[rows, cols] = _bf(o)
                mz = mz_ref[rows, cols].astype(F32)
                yb_ref[rows, cols] = _bf(o * (mz * _sigmoid(mz)))
                lse_ref[0, h, rows, :] = jnp.broadcast_to(m * EXP2_SCALE + jnp.log2(l), (tq, LANE))

    head = lambda off: pl.BlockSpec((lp, nh * MLA_DV), lambda b, h: (b, off + h))
    wide = pl.BlockSpec((lp, nh * QKW), lambda b, h: (b, h))
    return pl.pallas_call(
        body, name="mla_attn_fwd", grid=(bsz, MLA_HEADS // nh),
        in_specs=[wide, wide, head(0), head(C_MZ // (nh * MLA_DV))],
        out_specs=[head(0), head(0), pl.BlockSpec((1, nh, lp, LANE), lambda b, h: (b, h, 0, 0))],
        out_shape=[jax.ShapeDtypeStruct((tp, MLA_HEADS * MLA_DV), BF16),
                   jax.ShapeDtypeStruct((tp, MLA_HEADS * MLA_DV), BF16),
                   jax.ShapeDtypeStruct((bsz, MLA_HEADS, lp, LANE), F32)],
        compiler_params=_cp(("parallel", "parallel"), 56),
    )(qf, kf, vf, proj)


def _attn_bwd_blocks(lp):
    return [(0, X0)] + [(r0, min(MXU_DEPTH, lp - r0)) for r0 in range(X0, lp, MXU_DEPTH)]


def _attn_bwd(qf, kf, vf, d_o, lse, delta, bsz, lp):
    tp = bsz * lp

    def body(q_ref, k_ref, v_ref, do_ref, lse_ref, dl_ref, dq_ref, dk_ref, dv_ref, dk_acc, dv_acc):
        dk_acc[...] = jnp.zeros_like(dk_acc)
        dv_acc[...] = jnp.zeros_like(dv_acc)
        for r0, tq in _attn_bwd_blocks(lp):
            rows, kmax = slice(r0, r0 + tq), r0 + tq
            q, do = q_ref[rows, :], do_ref[rows, :]
            k, v = k_ref[0:kmax, :], v_ref[0:kmax, :]
            p = jnp.exp2(_dot_nt(q, k) * EXP2_SCALE - lse_ref[0, 0, rows, :][:, :1])
            p = _causal_fill(p, r0, 0.0)
            ds = _bf(p * (_dot_nt(do, v) - dl_ref[0, rows, :][:, :1]))
            dq_ref[rows, :] = _bf(_dot(ds, k) * ATT_SCALE)
            dk_acc[0:kmax, :] += _dot_tn(ds, q)
            dv_acc[0:kmax, :] += _dot_tn(_bf(p), do)
        dk_ref[...] = _bf(dk_acc[...] * ATT_SCALE)
        dv_ref[...] = _bf(dv_acc[...])

    wide = pl.BlockSpec((lp, QKW), lambda b, h: (b, h))
    narrow = pl.BlockSpec((lp, MLA_DV), lambda b, h: (b, h))
    stat = pl.BlockSpec((1, 1, lp, LANE), lambda b, h: (b, h, 0, 0))
    return pl.pallas_call(
        body, name="mla_attn_bwd", grid=(bsz, MLA_HEADS),
        in_specs=[wide, wide, narrow, narrow, stat, pl.BlockSpec((1, lp, LANE), lambda b, h: (h, b, 0))],
        out_specs=[wide, wide, narrow],
        out_shape=[jax.ShapeDtypeStruct((tp, MLA_HEADS * QKW), BF16), jax.ShapeDtypeStruct((tp, MLA_HEADS * QKW), BF16),
                   jax.ShapeDtypeStruct((tp, MLA_HEADS * MLA_DV), BF16)],
        scratch_shapes=[pltpu.VMEM((lp, QKW), F32), pltpu.VMEM((lp, MLA_DV), F32)],
        compiler_params=_cp(("parallel", "parallel"), 56),
    )(qf, kf, vf, d_o, lse, delta)


def _q_up_bwd(dqf, proj, q_norm_g, wn, wr, wt, cos_t, sin_t, dproj, bsz, lp):
    tp = bsz * lp
    tok = _attn_block(lp)
    nb = lp // tok
    hw = MLA_HEADS * LANE

    def body(dq_ref, cq_ref, g_ref, wn_ref, wr_ref, wt_ref, cos_ref, sin_ref, _,
             dcq_ref, dwn_ref, dwr_ref, dwt_ref, dg_ref):
        @pl.when(jnp.logical_and(pl.program_id(0) == 0, pl.program_id(1) == 0))
        def _():
            for r in (dwn_ref, dwr_ref, dwt_ref, dg_ref):
                r[...] = jnp.zeros_like(r)

        g = g_ref[...]
        xh, r = _rms_fwd(cq_ref[...].astype(F32))
        cqn = _bf(xh * g)
        dn = jnp.concatenate([dq_ref[:, h * QKW:h * QKW + LANE] for h in range(MLA_HEADS)], axis=1)
        dr = jnp.concatenate([dq_ref[:, h * QKW + LANE:(h + 1) * QKW] for h in range(MLA_HEADS)], axis=1).astype(F32)
        dr_c = _bf(dr * jnp.tile(cos_ref[...], (1, MLA_HEADS)))
        dr_s = _bf(dr * jnp.tile(sin_ref[...], (1, MLA_HEADS)))
        dcqn = _dot_nt(dn, wn_ref[...]) + _dot_nt(dr_c, wr_ref[...]) + _dot_nt(dr_s, wt_ref[...])
        dwn_ref[...] += _dot_tn(cqn, dn)
        dwr_ref[...] += _dot_tn(cqn, dr_c)
        dwt_ref[...] += _dot_tn(cqn, dr_s)
        dx, dg = _rms_bwd(dcqn, xh, r, g)
        dcq_ref[...] = _bf(dx)
        dg_ref[...] += dg

    aspec = pl.BlockSpec((MLA_QR, hw), lambda b, i: (0, 0))
    tspec = pl.BlockSpec((tok, LANE), lambda b, i: (i, 0))
    return pl.pallas_call(
        body, name="mla_q_up_bwd", grid=(bsz, nb),
        in_specs=[pl.BlockSpec((tok, MLA_HEADS * QKW), lambda b, i: (b * nb + i, 0)),
                  pl.BlockSpec((tok, MLA_QR), lambda b, i: (b * nb + i, C_CQ // MLA_QR)),
                  pl.BlockSpec((1, MLA_QR), lambda b, i: (0, 0)), aspec, aspec, aspec, tspec, tspec,
                  pl.BlockSpec(memory_space=pl.ANY)],
        out_specs=[pl.BlockSpec((tok, MLA_QR), lambda b, i: (b * nb + i, C_CQ // MLA_QR)), aspec, aspec, aspec,
                   pl.BlockSpec((1, MLA_QR), lambda b, i: (0, 0))],
        out_shape=[jax.ShapeDtypeStruct((tp, N_EXT), BF16)] + [jax.ShapeDtypeStruct((MLA_QR, hw), F32)] * 3
        + [jax.ShapeDtypeStruct((1, MLA_QR), F32)],
        input_output_aliases={8: 0},
        compiler_params=_cp(("arbitrary", "arbitrary")),
    )(dqf, proj, q_norm_g, wn, wr, wt, cos_t, sin_t, dproj)


def _kv_up_bwd(dkf, dvf, proj, kv_norm_g, wk, wv, cos_t, sin_t, d_lr, dproj, bsz, lp):
    tp = bsz * lp
    tok = _attn_block(lp)
    nb = lp // tok
    hw = MLA_HEADS * LANE

    def body(dk_ref, dv_ref, ckv_ref, g_ref, wk_ref, wv_ref, cos_ref, sin_ref, dlr_ref, _,
             dp_ref, dwk_ref, dwv_ref, dg_ref):
        dckv_ref, dkr_ref, dkrot_ref = (dp_ref.at[:, j * LANE:(j + 1) * LANE] for j in range(3))
        dp_ref[:, 3 * LANE:] = dlr_ref[...]
        @pl.when(jnp.logical_and(pl.program_id(0) == 0, pl.program_id(1) == 0))
        def _():
            for r in (dwk_ref, dwv_ref, dg_ref):
                r[...] = jnp.zeros_like(r)

        g = g_ref[...]
        xh, r = _rms_fwd(ckv_ref[...].astype(F32))
        cn = _bf(xh * g)
        dv = dv_ref[...]
        dn = jnp.concatenate([dk_ref[:, h * QKW:h * QKW + LANE] for h in range(MLA_HEADS)], axis=1)
        dcn = _dot_nt(dv, wv_ref[...]) + _dot_nt(dn, wk_ref[...])
        dwv_ref[...] += _dot_tn(cn, dv)
        dwk_ref[...] += _dot_tn(cn, dn)
        drope = jnp.zeros((tok, LANE), F32)
        for h in range(MLA_HEADS):
            drope += dk_ref[:, h * QKW + LANE:(h + 1) * QKW].astype(F32)
        dkr_ref[...] = _bf(drope * cos_ref[...])
        dkrot_ref[...] = _bf(drope * sin_ref[...])
        dx, dg = _rms_bwd(dcn, xh, r, g)
        dckv_ref[...] = _bf(dx)
        dg_ref[...] += dg

    aspec = pl.BlockSpec((MLA_KVR, hw), lambda b, i: (0, 0))
    tspec = pl.BlockSpec((tok, LANE), lambda b, i: (i, 0))
    ospec = pl.BlockSpec((tok, LANE), lambda b, i: (b * nb + i, 0))
    return pl.pallas_call(
        body, name="mla_kv_up_bwd", grid=(bsz, nb),
        in_specs=[pl.BlockSpec((tok, MLA_HEADS * QKW), lambda b, i: (b * nb + i, 0)),
                  pl.BlockSpec((tok, hw), lambda b, i: (b * nb + i, 0)),
                  pl.BlockSpec((tok, LANE), lambda b, i: (b * nb + i, C_CKV // LANE)),
                  pl.BlockSpec((1, MLA_KVR), lambda b, i: (0, 0)), aspec, aspec, tspec, tspec, ospec,
                  pl.BlockSpec(memory_space=pl.ANY)],
        out_specs=[pl.BlockSpec((tok, 4 * LANE), lambda b, i: (b * nb + i, C_CKV // (4 * LANE))), aspec, aspec,
                   pl.BlockSpec((1, MLA_KVR), lambda b, i: (0, 0))],
        out_shape=[jax.ShapeDtypeStruct((tp, N_EXT), BF16)] + [jax.ShapeDtypeStruct((MLA_KVR, hw), F32)] * 2
        + [jax.ShapeDtypeStruct((1, MLA_KVR), F32)],
        input_output_aliases={9: 0},
        compiler_params=_cp(("arbitrary", "arbitrary")),
    )(dkf, dvf, proj, kv_norm_g, wk, wv, cos_t, sin_t, d_lr, dproj)


def _mid_fwd(ya_in, yb_in, proj, hp, target, w_gp, w_mp, w_o, final_g, bsz, lp):
    tp = bsz * lp
    tm = _attn_block(lp)
    nb = lp // tm
    last = pl.cdiv(lp - X0, tm) - 1

    def body(ya_ref, yb_ref, gg_ref, gm_ref, h_ref, ta_ref, tb_ref, wgp_ref, wmp_ref, wo_ref, fg_ref,
             ya_out, yb_out, dh_ref, loss_ref, dfg_ref):
        @pl.when(jnp.logical_and(pl.program_id(0) == 0, pl.program_id(1) == 0))
        def _():
            loss_ref[...] = jnp.zeros_like(loss_ref)
            dfg_ref[...] = jnp.zeros_like(dfg_ref)

        y_a = _dot(ya_ref[...], wgp_ref[...])
        y_b = _dot(yb_ref[...], wmp_ref[...])
        ya_out[...] = _bf(y_a)
        yb_out[...] = _bf(y_b)
        merged = _sigmoid(gg_ref[...].astype(F32)) * y_a + _sigmoid(gm_ref[...].astype(F32)) * y_b
        h2 = h_ref[...] + _dot(_bf(merged), wo_ref[...])
        fg = fg_ref[...]
        xh, r = _rms_fwd(h2)
        pos = pl.program_id(1) * tm + lax.broadcasted_iota(jnp.int32, (tm, 1), 0)
        t = jnp.concatenate([ta_ref[0, tm - X0:, :], tb_ref[0, :tm - X0, :]], axis=0)
        err = jnp.where(pos >= X0, xh * fg - t, 0.0)
        loss_ref[...] += 0.5 * jnp.sum(jnp.mean(err * err, axis=-1, keepdims=True), axis=0, keepdims=True)
        dy = err * (1.0 / D_MODEL)
        dx, dfg = _rms_bwd(dy, xh, r, fg)
        dh_ref[...] = dx
        dfg_ref[...] += dfg

    tok = lambda c: pl.BlockSpec((tm, D_MODEL), lambda b, i: (b * nb + i, c))
    wspec = pl.BlockSpec((D_MODEL, D_MODEL), lambda b, i: (0, 0))
    return pl.pallas_call(
        body, name="mid_fwd", grid=(bsz, nb),
        in_specs=[tok(0), tok(0), tok(C_GG // D_MODEL), tok(C_GM // D_MODEL), tok(0),
                  pl.BlockSpec((1, tm, D_MODEL), lambda b, i: (b, jnp.maximum(i - 1, 0), 0)),
                  pl.BlockSpec((1, tm, D_MODEL), lambda b, i: (b, jnp.minimum(i, last), 0)),
                  wspec, wspec, wspec, pl.BlockSpec((1, D_MODEL), lambda b, i: (0, 0))],
        out_specs=[tok(0), tok(0), tok(0), pl.BlockSpec((1, LANE), lambda b, i: (0, 0)),
                   pl.BlockSpec((1, D_MODEL), lambda b, i: (0, 0))],
        out_shape=[jax.ShapeDtypeStruct((tp, D_MODEL), BF16), jax.ShapeDtypeStruct((tp, D_MODEL), BF16),
                   jax.ShapeDtypeStruct((tp, D_MODEL), F32), jax.ShapeDtypeStruct((1, LANE), F32),
                   jax.ShapeDtypeStruct((1, D_MODEL), F32)],
        compiler_params=_cp(("arbitrary", "arbitrary"), 48),
    )(ya_in, yb_in, proj, proj, hp, target, target, w_gp, w_mp, w_o, final_g)


def _mid_bwd(dh2, y_a, y_b, proj, ya_in, yb_in, o_b, w_o, w_gp, w_mp, bsz, lp):
    tp = bsz * lp
    tm = MXU_DEPTH if tp % MXU_DEPTH == 0 else _attn_block(lp)
    nsteps = tp // tm
    group = 3 * D_MODEL

    def body(dh_ref, ya_ref, yb_ref, mz_ref, gg_ref, gm_ref, yai_ref, ybi_ref, ob_ref, wo_ref, wgp_ref, wmp_ref,
             dyai_ref, do_ref, dp_ref, dl_ref, dwo_ref, dwgp_ref, dwmp_ref, a_o, a_gp, a_mp):
        @pl.when(pl.program_id(0) == 0)
        def _():
            for r in (a_o, a_gp, a_mp):
                r[...] = jnp.zeros_like(r)

        dh = _bf(dh_ref[...])
        dm = _dot_nt(dh, wo_ref[...])
        y_a, y_b = ya_ref[...].astype(F32), yb_ref[...].astype(F32)
        sg, sm = _sigmoid(gg_ref[...].astype(F32)), _sigmoid(gm_ref[...].astype(F32))
        d_ya, d_yb = _bf(sg * dm), _bf(sm * dm)
        dp_ref[:, D_MODEL:2 * D_MODEL] = _bf(dm * y_a * sg * (1.0 - sg))
        dp_ref[:, 2 * D_MODEL:] = _bf(dm * y_b * sm * (1.0 - sm))
        a_o[...] += _dot_tn(_bf(sg * y_a + sm * y_b), dh)
        a_gp[...] += _dot_tn(yai_ref[...], d_ya)
        a_mp[...] += _dot_tn(ybi_ref[...], d_yb)
        dyai_ref[...] = _bf(_dot_nt(d_ya, wgp_ref[...]))
        dy = _dot_nt(d_yb, wmp_ref[...])
        mz, o = mz_ref[...].astype(F32), ob_ref[...].astype(F32)
        s = _sigmoid(mz)
        do = _bf(dy * (mz * s))
        do_ref[...] = do
        dp_ref[:, :D_MODEL] = _bf(dy * o * (s * (1.0 + mz * (1.0 - s))))
        prod = do.astype(F32) * o
        for h in range(MLA_HEADS):
            dl = jnp.sum(prod[:, h * MLA_DV:(h + 1) * MLA_DV], axis=-1, keepdims=True)
            dl_ref[h] = jnp.broadcast_to(dl, (tm, LANE))

        @pl.when(pl.program_id(0) == nsteps - 1)
        def _():
            pltpu.sync_copy(a_o, dwo_ref)
            pltpu.sync_copy(a_gp, dwgp_ref)
            pltpu.sync_copy(a_mp, dwmp_ref)

    tok = lambda c: pl.BlockSpec((tm, D_MODEL), lambda i: (i, c))
    wspec = pl.BlockSpec((D_MODEL, D_MODEL), lambda i: (0, 0))
    anyspec = pl.BlockSpec(memory_space=pl.ANY)
    wshape = jax.ShapeDtypeStruct((D_MODEL, D_MODEL), F32)
    return pl.pallas_call(
        body, name="mid_bwd", grid=(nsteps,),
        in_specs=[tok(0), tok(0), tok(0), tok(C_MZ // D_MODEL), tok(C_GG // D_MODEL), tok(C_GM // D_MODEL),
                  tok(0), tok(0), tok(0), wspec, wspec, wspec],
        out_specs=[tok(0), tok(0), pl.BlockSpec((tm, group), lambda i: (i, C_MZ // group)),
                   pl.BlockSpec((MLA_HEADS, tm, LANE), lambda i: (0, i, 0)), anyspec, anyspec, anyspec],
        out_shape=[jax.ShapeDtypeStruct((tp, D_MODEL), BF16)] * 2 + [jax.ShapeDtypeStruct((tp, N_EXT), BF16),
                   jax.ShapeDtypeStruct((MLA_HEADS, tp, LANE), F32)] + [wshape] * 3,
        scratch_shapes=[pltpu.VMEM((D_MODEL, D_MODEL), F32)] * 3,
        compiler_params=_cp(("arbitrary",), 56),
    )(dh2, y_a, y_b, proj, proj, proj, ya_in, yb_in, o_b, w_o, w_gp, w_mp)


MESH_ID = pl.DeviceIdType.MESH
EXCHANGE_SEMS = [pltpu.SemaphoreType.DMA((N_DEV - 1,)), pltpu.SemaphoreType.DMA((N_DEV - 1,)), pltpu.SemaphoreType.DMA]


def _my_place():
    return lax.axis_index("x"), lax.axis_index("y"), lax.axis_index("c")


def _exchange(g_ref, recv_ref, send_sems, recv_sems, local_sem, start, same=False):
    x, y, c = _my_place()
    me = 4 * x + 2 * y + c
    own = pltpu.make_async_copy(g_ref if same else g_ref.at[me], recv_ref.at[me], local_sem)
    sends, lands = [], []
    for d in range(1, N_DEV):
        px = 1 - x if d & 4 else x
        py = 1 - y if d & 2 else y
        pc = 1 - c if d & 1 else c
        peer = 4 * px + 2 * py + pc
        for slot, group in ((me, sends),) if start else ((me, sends), (peer, lands)):
            group.append(pltpu.make_async_remote_copy(
                src_ref=g_ref if same else g_ref.at[peer], dst_ref=recv_ref.at[slot], send_sem=send_sems.at[d - 1],
                recv_sem=recv_sems.at[d - 1], device_id=(px, py, pc), device_id_type=MESH_ID))
    if start:
        own.start()
        for cp in sends:
            cp.start()
    else:
        for cp in lands:
            cp.wait_recv()
        for cp in sends:
            cp.wait_send()
        own.wait()


def _dw_in(u, dproj, slabs):
    tp = u.shape[0]
    tn = 3 * LANE
    nj = N_EXT // tn

    def body(u_ref, d_ref, g_ref, o_ref, recv_ref, send_sems, recv_sems, local_sem):
        j = pl.program_id(0)

        @pl.when(j == 0)
        def _():
            _exchange(g_ref, recv_ref, send_sems, recv_sems, local_sem, True)

        o_ref[...] = _dot_tn(d_ref[...], u_ref[...])

        @pl.when(j == nj - 1)
        def _():
            _exchange(g_ref, recv_ref, send_sems, recv_sems, local_sem, False)

    anyspec = pl.BlockSpec(memory_space=pl.ANY)
    return pl.pallas_call(
        body, name="dw_in", grid=(nj,),
        in_specs=[pl.BlockSpec((tp, D_MODEL), lambda j: (0, 0), pipeline_mode=pl.Buffered(1)),
                  pl.BlockSpec((tp, tn), lambda j: (0, j)), anyspec],
        out_specs=[pl.BlockSpec((tn, D_MODEL), lambda j: (j, 0)), anyspec],
        out_shape=[jax.ShapeDtypeStruct((N_EXT, D_MODEL), F32), jax.ShapeDtypeStruct(slabs.shape, slabs.dtype)],
        scratch_shapes=EXCHANGE_SEMS,
        compiler_params=_cp(("arbitrary",), 56),
    )(u, dproj, slabs)


def _dx_in(dproj, w_ext, hp, dh2, norm_g, slabs):
    tp = hp.shape[0]
    tm = 2 * TOK
    ni = tp // tm

    def body(d_ref, w_ref, h_ref, dh_ref, g_ref, s_ref, o_ref, dg_ref, recv_ref, send_sems, recv_sems, local_sem):
        i = pl.program_id(0)

        @pl.when(i == 0)
        def _():
            _exchange(s_ref, recv_ref, send_sems, recv_sems, local_sem, True)
            dg_ref[...] = jnp.zeros_like(dg_ref)

        du = _dot_nt(d_ref[...], w_ref[...])
        g = g_ref[...]
        xh, r = _rms_fwd(h_ref[...])
        dx, dg = _rms_bwd(du, xh, r, g)
        o_ref[...] = dh_ref[...] + dx
        dg_ref[...] += dg

        @pl.when(i == ni - 1)
        def _():
            _exchange(s_ref, recv_ref, send_sems, recv_sems, local_sem, False)

    tok = pl.BlockSpec((tm, D_MODEL), lambda i: (i, 0))
    anyspec = pl.BlockSpec(memory_space=pl.ANY)
    return pl.pallas_call(
        body, name="dx_in", grid=(ni,),
        in_specs=[pl.BlockSpec((tm, N_EXT), lambda i: (i, 0)),
                  pl.BlockSpec((D_MODEL, N_EXT), lambda i: (0, 0), pipeline_mode=pl.Buffered(1)),
                  tok, tok, pl.BlockSpec((1, D_MODEL), lambda i: (0, 0)), anyspec],
        out_specs=[tok, pl.BlockSpec((1, D_MODEL), lambda i: (0, 0)), anyspec],
        out_shape=[jax.ShapeDtypeStruct((tp, D_MODEL), F32), jax.ShapeDtypeStruct((1, D_MODEL), F32),
                   jax.ShapeDtypeStruct(slabs.shape, slabs.dtype)],
        scratch_shapes=EXCHANGE_SEMS,
        compiler_params=_cp(("arbitrary",), 56),
    )(dproj, w_ext, hp, dh2, norm_g, slabs)


def _meta_grad(dhp3):
    bsz = dhp3.shape[0]

    def body(d_ref, o_ref):
        @pl.when(pl.program_id(0) == 0)
        def _():
            o_ref[...] = jnp.zeros_like(o_ref)

        o_ref[...] += d_ref[0]

    return pl.pallas_call(
        body, name="meta_grad", grid=(bsz,),
        in_specs=[pl.BlockSpec((1, N_META, D_MODEL), lambda b: (b, FRONT // N_META, 0))],
        out_specs=pl.BlockSpec((N_META, D_MODEL), lambda b: (0, 0)),
        out_shape=jax.ShapeDtypeStruct((N_META, D_MODEL), F32),
        compiler_params=_cp(("arbitrary",)),
    )(dhp3)


W_IN_SHARD = N_IN // N_DEV


def _pad_lanes(a, width=LANE):
    return jnp.pad(a, [(0, 0)] * (a.ndim - 1) + [(0, width - a.shape[-1])])


def _rot_cols(w):
    half = w.shape[-1] // 2
    return jnp.concatenate([-w[..., half:], w[..., :half]], axis=-1)


def _unrot_cols(dw):
    half = dw.shape[-1] // 2
    return jnp.concatenate([dw[..., half:], -dw[..., :half]], axis=-1)


def _w_in_cols(shards, lo, hi):
    parts = []
    for k in range(lo // W_IN_SHARD, (hi - 1) // W_IN_SHARD + 1):
        a, b = max(lo, k * W_IN_SHARD), min(hi, (k + 1) * W_IN_SHARD)
        parts.append(shards[k][:, a - k * W_IN_SHARD:b - k * W_IN_SHARD])
    return parts[0] if len(parts) == 1 else jnp.concatenate(parts, axis=1)


def _w_in_ext(shards):
    c = lambda lo, hi: _w_in_cols(shards, lo, hi)
    kr = c(O_KR, O_MZ)
    return jnp.concatenate([
        c(O_V, O_LR), c(O_Z, O_CQ), c(O_Q, O_K), c(O_K, O_V), c(O_MZ, O_GG), c(O_GG, O_GM), c(O_GM, N_IN),
        c(O_CKV, O_KR), _pad_lanes(kr), _pad_lanes(_rot_cols(kr)), _pad_lanes(c(O_LR, O_Z)), c(O_CQ, O_CKV)], axis=1)


def _w_in_grad_t(dwt):
    g = lambda start, width: dwt[start:start + width]
    half = MLA_ROPE // 2
    krot = g(C_KROT, MLA_ROPE)
    kr = g(C_KR, MLA_ROPE) + jnp.concatenate([krot[half:], -krot[:half]], axis=0)
    return jnp.concatenate([
        g(C_Q, GLA_KW), g(C_K, GLA_KW), g(C_V, GLA_VW), g(C_LR, GLA_RANK), g(C_Z, GLA_VW), g(C_CQ, MLA_QR),
        g(C_CKV, MLA_KVR), kr, g(C_MZ, D_MODEL), g(C_GG, D_MODEL), g(C_GM, D_MODEL)], axis=0)


def _rope_tables(lp):
    inv = 1.0 / (ROPE_BASE ** (jnp.arange(0, MLA_ROPE, 2, dtype=F32) / MLA_ROPE))
    ang = (jnp.arange(lp, dtype=F32) - FRONT)[:, None] * inv[None, :]
    cos, sin = jnp.cos(ang), jnp.sin(ang)
    return _pad_lanes(jnp.concatenate([cos, cos], axis=1)), _pad_lanes(jnp.concatenate([sin, sin], axis=1))


def _local_step(x, loss_target, w):
    bsz, seq, _ = x.shape
    lp = X0 + seq
    tp = bsz * lp
    assert lp % TOK == 0 and (lp // GLA_CHUNK) % _gla_group(lp // GLA_CHUNK) == 0
    meta = jnp.broadcast_to(w["meta_tokens"][None], (bsz, N_META, D_MODEL))
    hp = jnp.concatenate([jnp.zeros((bsz, FRONT, D_MODEL), F32), meta, x], axis=1).reshape(tp, D_MODEL)
    cos_t, sin_t = _rope_tables(lp)

    w_ext = _w_in_ext(w["w_in"])
    u, proj, packed_all = _proj_in(hp, w["norm_g"], w_ext, w["packed"])
    packed_all, off = packed_all.reshape(N_DEV, -1), 0
    for n, shape, axis in PACKED:
        size = shape[0] * shape[1]
        w[n] = _join8(packed_all[:, off:off + size].reshape((N_DEV,) + shape), axis)
        off += size
    gw_pad = jnp.pad(w["gla_gate_w"], ((0, LANE - GLA_RANK), (0, 0)))
    uq = w["mla_w_uq"].reshape(MLA_QR, MLA_HEADS, MLA_QK)
    rope_w = uq[:, :, MLA_NOPE:]
    hw = MLA_HEADS * LANE
    wn = uq[:, :, :MLA_NOPE].reshape(MLA_QR, hw)
    wr = _pad_lanes(rope_w).reshape(MLA_QR, hw)
    wt = _pad_lanes(_rot_cols(rope_w)).reshape(MLA_QR, hw)
    ukv = w["mla_w_ukv"].reshape(MLA_KVR, MLA_HEADS, MLA_NOPE + MLA_DV)
    wk = ukv[:, :, :MLA_NOPE].reshape(MLA_KVR, hw)
    wv = ukv[:, :, MLA_NOPE:].reshape(MLA_KVR, hw)

    o_raw, ya_in, s_all = _gla_fwd(proj, gw_pad, w["gla_gate_b"], w["gla_norm_g"], bsz, lp)
    qf = _q_up(proj, w["mla_q_norm_g"], wn, wr, wt, cos_t, sin_t, bsz, lp)
    kf, vf = _kv_up(proj, w["mla_kv_norm_g"], wk, wv, cos_t, sin_t, bsz, lp)
    o_b, yb_in, lse = _attn_fwd(qf, kf, vf, proj, bsz, lp)
    y_a, y_b, dh2, loss, d_final_g = _mid_fwd(ya_in, yb_in, proj, hp, loss_target, w["gla_proj"], w["mla_proj"],
                                              w["w_out"], w["final_norm_g"], bsz, lp)
    d_ya, d_o, dproj, delta, d_w_out, d_gla_proj, d_mla_proj = _mid_bwd(
        dh2, y_a, y_b, proj, ya_in, yb_in, o_b, w["w_out"], w["gla_proj"], w["mla_proj"], bsz, lp)
    dproj, d_gate, d_gla_norm = _gla_bwd(proj, gw_pad, w["gla_gate_b"], w["gla_norm_g"], o_raw, s_all, d_ya, dproj,
                                         bsz, lp)
    d_lr, d_gw_pad, d_gate_b = _gate_bwd(d_gate, proj, gw_pad)
    dqf, dkf, dvf = _attn_bwd(qf, kf, vf, d_o, lse, delta, bsz, lp)
    dproj, d_wn, d_wr, d_wt, d_qn = _q_up_bwd(dqf, proj, w["mla_q_norm_g"], wn, wr, wt, cos_t, sin_t, dproj,
                                              bsz, lp)
    dproj, d_wk, d_wv, d_kvn = _kv_up_bwd(dkf, dvf, proj, w["mla_kv_norm_g"], wk, wv, cos_t, sin_t, d_lr, dproj,
                                          bsz, lp)

    d_rope = (d_wr.reshape(MLA_QR, MLA_HEADS, LANE)[:, :, :MLA_ROPE]
              + _unrot_cols(d_wt.reshape(MLA_QR, MLA_HEADS, LANE)[:, :, :MLA_ROPE]))
    d_uq = jnp.concatenate([d_wn.reshape(MLA_QR, MLA_HEADS, LANE), d_rope], axis=-1).reshape(MLA_QR, MLA_HEADS * MLA_QK)
    d_ukv = jnp.concatenate([d_wk.reshape(MLA_KVR, MLA_HEADS, LANE), d_wv.reshape(MLA_KVR, MLA_HEADS, LANE)],
                            axis=-1).reshape(MLA_KVR, MLA_HEADS * (MLA_NOPE + MLA_DV))
    mats = dict(gla_gate_w=d_gw_pad[:GLA_RANK], gla_proj=d_gla_proj, mla_w_uq=d_uq, mla_w_ukv=d_ukv,
                mla_proj=d_mla_proj, w_out=d_w_out)
    packed = _pad_rows(jnp.concatenate([_split8(mats[n], axis).reshape(N_DEV, -1) for n, _, axis in PACKED], axis=1),
                       PACK_ROWS)
    d_w_ext_t, packed_parts = _dw_in(u, dproj, _bf(packed))
    w_in_slabs = _bf(_w_in_grad_t(d_w_ext_t).reshape(N_DEV, W_IN_SHARD, D_MODEL))
    d_hp, d_norm_g, w_in_parts = _dx_in(dproj, w_ext, hp, dh2, w["norm_g"], w_in_slabs)
    d_hp3 = d_hp.reshape(bsz, lp, D_MODEL)
    small = dict(meta_tokens=_meta_grad(d_hp3), norm_g=d_norm_g, gla_gate_b=d_gate_b, gla_norm_g=d_gla_norm,
                 mla_q_norm_g=d_qn, mla_kv_norm_g=d_kvn, final_norm_g=d_final_g)
    return loss, d_hp3[:, X0:, :], w_in_parts, packed_parts, small


PACKED = (("gla_gate_w", (GLA_RANK, GLA_KW // N_DEV), 1),
          ("gla_proj", (D_MODEL // N_DEV, D_MODEL), 0), ("mla_w_uq", (MLA_QR, MLA_HEADS * MLA_QK // N_DEV), 1),
          ("mla_w_ukv", (MLA_KVR, MLA_HEADS * (MLA_NOPE + MLA_DV) // N_DEV), 1),
          ("mla_proj", (D_MODEL // N_DEV, D_MODEL), 0), ("w_out", (D_MODEL // N_DEV, D_MODEL), 0))
REPLICATED = (("norm_g", D_MODEL), ("gla_gate_b", GLA_KW), ("gla_norm_g", GLA_DV), ("mla_q_norm_g", MLA_QR),
              ("mla_kv_norm_g", MLA_KVR), ("final_norm_g", D_MODEL))
PACK_ROWS = 3744
PACK_BLOCK = 1248
SMALL_ROWS = 48
LOSS_ROW = N_META + 25
W_IN_BLOCK = 128


def _all_gather(shards):
    n_arr = len(shards)

    def body(*refs):
        x_refs, out_refs = refs[:n_arr], refs[n_arr:2 * n_arr]
        send_sems, recv_sems, local_sems = refs[2 * n_arr:]
        x, y, c = _my_place()
        me, sibling = (x, y, c), (x, y, 1 - c)
        chips = [(1 - x, y), (x, 1 - y), (1 - x, 1 - y)]

        def copy(a, k, block, to, from_input=False):
            slab = out_refs[a].at[4 * block[0] + 2 * block[1] + block[2]]
            return pltpu.make_async_remote_copy(
                src_ref=x_refs[a] if from_input else slab, dst_ref=slab,
                send_sem=send_sems.at[7 * a + k], recv_sem=recv_sems.at[7 * a + k], device_id=to,
                device_id_type=MESH_ID)

        arrays = range(n_arr)
        mine = [pltpu.make_async_copy(x_refs[a], out_refs[a].at[4 * x + 2 * y + c], local_sems.at[a]) for a in arrays]
        for cp in mine:
            cp.start()
        first = [copy(a, 0, me, sibling, True) for a in arrays]
        first += [copy(a, 1 + j, me, (*chip, c), True) for j, chip in enumerate(chips) for a in arrays]
        for cp in first:
            cp.start()
        passed = []
        for j, chip in enumerate(chips):
            for a in arrays:
                copy(a, 1 + j, (*chip, c), me).wait_recv()
                passed.append(copy(a, 4 + j, (*chip, c), sibling))
                passed[-1].start()
        for a in arrays:
            copy(a, 0, sibling, me).wait_recv()
        for j, chip in enumerate(chips):
            for a in arrays:
                copy(a, 4 + j, (*chip, 1 - c), me).wait_recv()
        for cp in first + passed:
            cp.wait_send()
        for cp in mine:
            cp.wait()

    anyspec = pl.BlockSpec(memory_space=pl.ANY)
    return pl.pallas_call(
        body, name="weights_all_gather",
        out_shape=[jax.ShapeDtypeStruct((N_DEV,) + s.shape, s.dtype) for s in shards],
        in_specs=[anyspec] * n_arr, out_specs=[anyspec] * n_arr,
        scratch_shapes=[pltpu.SemaphoreType.DMA((7 * n_arr,)), pltpu.SemaphoreType.DMA((7 * n_arr,)),
                        pltpu.SemaphoreType.DMA((n_arr,))],
    )(*shards)


def _small_exchange(slabs):
    def body(g_ref, recv_ref, send_sems, recv_sems, local_sem):
        _exchange(g_ref, recv_ref, send_sems, recv_sems, local_sem, True)
        _exchange(g_ref, recv_ref, send_sems, recv_sems, local_sem, False)

    vmem = pl.BlockSpec(memory_space=pltpu.VMEM)
    return pl.pallas_call(
        body, name="small_exchange", out_shape=jax.ShapeDtypeStruct(slabs.shape, slabs.dtype),
        in_specs=[vmem], out_specs=vmem, scratch_shapes=EXCHANGE_SEMS,
    )(slabs)


def _adamw(parts, w, m, v, block_rows, name):
    rows, cols = w.shape

    def body(p_ref, w_ref, m_ref, v_ref, g_out, d_out, m_out, v_out):
        g = p_ref[0].astype(F32)
        for s in range(1, N_DEV):
            g = g + p_ref[s].astype(F32)
        m_new = ADAM_B1 * m_ref[...] + (1.0 - ADAM_B1) * g
        v_new = ADAM_B2 * v_ref[...] + (1.0 - ADAM_B2) * (g * g)
        m_hat = m_new / (1.0 - ADAM_B1 ** ADAM_STEP)
        v_hat = v_new / (1.0 - ADAM_B2 ** ADAM_STEP)
        g_out[...] = g
        d_out[...] = -ADAM_LR * (m_hat / (jnp.sqrt(v_hat) + ADAM_EPS) + ADAM_WD * w_ref[...])
        m_out[...] = m_new
        v_out[...] = v_new

    spec = pl.BlockSpec((block_rows, cols), lambda i: (i, 0))
    return pl.pallas_call(
        body, name=name, grid=(pl.cdiv(rows, block_rows),),
        in_specs=[pl.BlockSpec((N_DEV, block_rows, cols), lambda i: (0, i, 0)), spec, spec, spec],
        out_specs=[spec] * 4, out_shape=[jax.ShapeDtypeStruct((rows, cols), F32)] * 4,
        compiler_params=_cp(("parallel",), 48),
    )(parts, w, m, v)


def _pad_rows(flat, rows):
    pad = rows * LANE - flat.shape[-1]
    flat = jnp.pad(flat, [(0, 0)] * (flat.ndim - 1) + [(0, pad)])
    return flat.reshape(flat.shape[:-1] + (rows, LANE))


def _pack_shards(shards):
    return _pad_rows(jnp.concatenate([shards[n].reshape(-1) for n, _, _ in PACKED]), PACK_ROWS)


def _unpack_shards(packed):
    flat, out, off = packed.reshape(-1), {}, 0
    for n, shape, _ in PACKED:
        size = shape[0] * shape[1]
        out[n] = flat[off:off + size].reshape(shape)
        off += size
    return out


def _split8(full, axis):
    r, c = full.shape
    if axis == 0:
        return full.reshape(N_DEV, r // N_DEV, c)
    return full.reshape(r, N_DEV, c // N_DEV).transpose(1, 0, 2)


def _join8(shards, axis):
    _, r, c = shards.shape
    if axis == 0:
        return shards.reshape(N_DEV * r, c)
    return shards.transpose(1, 0, 2).reshape(r, N_DEV * c)


def _pack_small(meta_shard, vals, loss_row):
    rows = jnp.concatenate([vals[n].reshape(-1, LANE) for n, _ in REPLICATED] + [loss_row], axis=0)
    rows = jnp.pad(rows, ((0, SMALL_ROWS - N_META - rows.shape[0]), (0, 0)))
    return jnp.concatenate([meta_shard, jnp.broadcast_to(rows, meta_shard.shape[:-2] + rows.shape)], axis=-2)


def _unpack_small(packed):
    out, off = {"meta_tokens": packed[:N_META]}, N_META
    for n, size in REPLICATED:
        out[n] = packed[off:off + size // LANE].reshape(1, size)
        off += size // LANE
    return out


def kernel(x, meta_tokens, norm_g, w_in, gla_gate_w, gla_gate_b, gla_norm_g, gla_proj, mla_q_norm_g, mla_w_uq, mla_kv_norm_g, mla_w_ukv, mla_proj, w_out, final_norm_g, loss_target, m_meta_tokens, m_norm_g, m_w_in, m_gla_gate_w, m_gla_gate_b, m_gla_norm_g, m_gla_proj, m_mla_q_norm_g, m_mla_w_uq, m_mla_kv_norm_g, m_mla_w_ukv, m_mla_proj, m_w_out, m_final_norm_g, v_meta_tokens, v_norm_g, v_w_in, v_gla_gate_w, v_gla_gate_b, v_gla_norm_g, v_gla_proj, v_mla_q_norm_g, v_mla_w_uq, v_mla_kv_norm_g, v_mla_w_ukv, v_mla_proj, v_w_out, v_final_norm_g):
    given = dict(meta_tokens=meta_tokens, norm_g=norm_g, w_in=w_in, gla_gate_w=gla_gate_w, gla_gate_b=gla_gate_b,
                 gla_norm_g=gla_norm_g, gla_proj=gla_proj, mla_q_norm_g=mla_q_norm_g, mla_w_uq=mla_w_uq,
                 mla_kv_norm_g=mla_kv_norm_g, mla_w_ukv=mla_w_ukv, mla_proj=mla_proj, w_out=w_out,
                 final_norm_g=final_norm_g)
    mom_m = dict(meta_tokens=m_meta_tokens, norm_g=m_norm_g, w_in=m_w_in, gla_gate_w=m_gla_gate_w,
                 gla_gate_b=m_gla_gate_b, gla_norm_g=m_gla_norm_g, gla_proj=m_gla_proj, mla_q_norm_g=m_mla_q_norm_g,
                 mla_w_uq=m_mla_w_uq, mla_kv_norm_g=m_mla_kv_norm_g, mla_w_ukv=m_mla_w_ukv, mla_proj=m_mla_proj,
                 w_out=m_w_out, final_norm_g=m_final_norm_g)
    mom_v = dict(meta_tokens=v_meta_tokens, norm_g=v_norm_g, w_in=v_w_in, gla_gate_w=v_gla_gate_w,
                 gla_gate_b=v_gla_gate_b, gla_norm_g=v_gla_norm_g, gla_proj=v_gla_proj, mla_q_norm_g=v_mla_q_norm_g,
                 mla_w_uq=v_mla_w_uq, mla_kv_norm_g=v_mla_kv_norm_g, mla_w_ukv=v_mla_w_ukv, mla_proj=v_mla_proj,
                 w_out=v_w_out, final_norm_g=v_final_norm_g)
    shapes = {n: a.shape for n, a in given.items()}
    shard2d = {n: s for n, s, _ in PACKED}
    shard2d["w_in"] = (D_MODEL, W_IN_SHARD)
    shard2d["meta_tokens"] = (N_META, LANE)

    def as2d(tree):
        out = {n: tree[n].reshape(shard2d[n]) for n in shard2d}
        out.update({n: tree[n].reshape(1, size) for n, size in REPLICATED})
        return out

    w_loc, m_loc, v_loc = as2d(given), as2d(mom_m), as2d(mom_v)

    w_in_all, meta_all = _all_gather([w_loc["w_in"].astype(BF16), w_loc["meta_tokens"]])
    flat = jnp.concatenate([w_loc[n].astype(BF16).reshape(-1) for n, _, _ in PACKED])
    full = {"w_in": w_in_all, "meta_tokens": _join8(meta_all, 1), "packed": _pad_rows(flat, PACK_ROWS)}
    for n, _ in REPLICATED:
        full[n] = w_loc[n]

    loss_part, grad_x, w_in_parts, packed_parts, small = _local_step(x, loss_target, full)
    small_all = _small_exchange(_pack_small(_split8(small["meta_tokens"], 1), small,
                                            jnp.broadcast_to(loss_part[:, :1], (1, LANE))))

    w_in_t = [t["w_in"].T for t in (w_loc, m_loc, v_loc)]
    g_w, d_w, m_w, v_w = (o.T for o in _adamw(w_in_parts, *w_in_t, W_IN_BLOCK, "adamw_w_in"))
    g_p, d_p, m_p, v_p = _adamw(packed_parts, _pack_shards(w_loc), _pack_shards(m_loc), _pack_shards(v_loc),
                                PACK_BLOCK, "adamw_packed")
    zero_row = jnp.zeros((1, LANE), F32)
    g_s, d_s, m_s, v_s = _adamw(small_all, *(_pack_small(t["meta_tokens"], t, zero_row) for t in (w_loc, m_loc, v_loc)),
                                SMALL_ROWS, "adamw_small")
    loss = g_s[LOSS_ROW, 0]

    order = ["meta_tokens", "norm_g", "w_in", "gla_gate_w", "gla_gate_b", "gla_norm_g", "gla_proj", "mla_q_norm_g",
             "mla_w_uq", "mla_kv_norm_g", "mla_w_ukv", "mla_proj", "w_out", "final_norm_g"]
    result = [loss, grad_x]
    for w_in_out, packed_sh, packed_sm in ((g_w, g_p, g_s), (d_w, d_p, d_s), (m_w, m_p, m_s), (v_w, v_p, v_s)):
        tree = _unpack_shards(packed_sh)
        tree.update(_unpack_small(packed_sm))
        tree["w_in"] = w_in_out
        result += [tree[n].reshape(shapes[n]) for n in order]
    return tuple(result)
```

```python
import jax
import jax.numpy as jnp
from jax import lax
from jax.experimental import pallas as pl
from jax.experimental.pallas import tpu as pltpu

F32 = jnp.float32
BF16 = jnp.bfloat16

D_MODEL = 1024
N_META = 16
EPS = 1e-6
FRONT = 48
X0 = FRONT + N_META
GLA_HEADS, GLA_DK, GLA_DV, GLA_RANK, GLA_CHUNK = 4, 128, 256, 16, 64
GLA_GATE_NORMALIZER = 16.0
GLA_KW = GLA_HEADS * GLA_DK
GLA_VW = GLA_HEADS * GLA_DV
MLA_HEADS, MLA_NOPE, MLA_ROPE, MLA_DV, MLA_QR, MLA_KVR = 8, 128, 64, 128, 256, 128
MLA_QK = MLA_NOPE + MLA_ROPE
ROPE_BASE = 10000.0
LANE = 128
QKW = 2 * LANE

C_V, C_Z, C_Q, C_K = 0, 1024, 2048, 2560
C_MZ, C_GG, C_GM = 3072, 4096, 5120
C_CKV, C_KR, C_KROT, C_LR = 6144, 6272, 6400, 6528
C_CQ = 6656
N_EXT = 6912
O_Q, O_K, O_V, O_LR, O_Z, O_CQ, O_CKV, O_KR, O_MZ, O_GG, O_GM, N_IN = (
    0, 512, 1024, 2048, 2064, 3088, 3344, 3472, 3536, 4560, 5584, 6608)

ADAM_LR, ADAM_B1, ADAM_B2, ADAM_EPS, ADAM_WD, ADAM_STEP = 0.001, 0.9, 0.999, 1e-08, 0.01, 10

N_DEV = 8
TOK = 192
ATT_BLOCK = 352
EXT_BLOCK = 1152
MXU_DEPTH = 256


def _cp(sems=None, vmem_mb=None):
    kw = {}
    if sems is not None:
        kw["dimension_semantics"] = sems
    if vmem_mb is not None:
        kw["vmem_limit_bytes"] = vmem_mb * 1024 * 1024
    return pltpu.CompilerParams(**kw)


def _dot(a, b):
    return jnp.dot(a, b, preferred_element_type=F32)


def _dot_nt(a, b):
    return lax.dot_general(a, b, (((1,), (1,)), ((), ())), preferred_element_type=F32)


def _dot_tn(a, b):
    return lax.dot_general(a, b, (((0,), (0,)), ((), ())), preferred_element_type=F32)


def _sigmoid(x):
    return 1.0 / (1.0 + jnp.exp(-x))


def _bf(x):
    return x.astype(BF16)


def _big_tok(tp):
    return 4 * TOK if tp % (4 * TOK) == 0 else TOK


def _attn_block(lp):
    return ATT_BLOCK if lp % ATT_BLOCK == 0 else TOK


def _proj_in(x, head, norm_g, w_ext, packed):
    bsz, seq, _ = x.shape
    lp = X0 + seq
    tp = bsz * lp
    tm = _attn_block(lp)
    nb = lp // tm
    last = pl.cdiv(seq, tm) - 1

    def body(xa_ref, xb_ref, hd_ref, g_ref, w_ref, p_ref, h_ref, u_ref, o_ref, pall_ref, send_sems, recv_sems, local_sem):
        first = jnp.logical_and(pl.program_id(0) == 0, pl.program_id(1) == 0)

        @pl.when(first)
        def _():
            _exchange(p_ref, pall_ref, send_sems, recv_sems, local_sem, True, same=True)

        front = jnp.where(pl.program_id(1) == 0, hd_ref[...], xa_ref[0, tm - X0:, :])
        h = jnp.concatenate([front, xb_ref[0, :tm - X0, :]], axis=0)
        h_ref[...] = h
        r = lax.rsqrt(jnp.mean(h * h, axis=-1, keepdims=True) + EPS)
        u = _bf(h * r * g_ref[...])
        u_ref[...] = u
        o_ref[...] = _bf(_dot(u, w_ref[...]))

        @pl.when(jnp.logical_and(pl.program_id(0) == bsz - 1, pl.program_id(1) == nb - 1))
        def _():
            _exchange(p_ref, pall_ref, send_sems, recv_sems, local_sem, False, same=True)

    anyspec = pl.BlockSpec(memory_space=pl.ANY)
    tok = lambda width: pl.BlockSpec((tm, width), lambda b, i: (b * nb + i, 0))
    return pl.pallas_call(
        body, name="proj_in", grid=(bsz, nb),
        in_specs=[pl.BlockSpec((1, tm, D_MODEL), lambda b, i: (b, jnp.maximum(i - 1, 0), 0)),
                  pl.BlockSpec((1, tm, D_MODEL), lambda b, i: (b, jnp.minimum(i, last), 0)),
                  pl.BlockSpec((X0, D_MODEL), lambda b, i: (0, 0)),
                  pl.BlockSpec((1, D_MODEL), lambda b, i: (0, 0)),
                  pl.BlockSpec((D_MODEL, N_EXT), lambda b, i: (0, 0), pipeline_mode=pl.Buffered(1)), anyspec],
        out_specs=[tok(D_MODEL), tok(D_MODEL), tok(N_EXT), anyspec],
        out_shape=[jax.ShapeDtypeStruct((tp, D_MODEL), F32), jax.ShapeDtypeStruct((tp, D_MODEL), BF16),
                   jax.ShapeDtypeStruct((tp, N_EXT), BF16),
                   jax.ShapeDtypeStruct((N_DEV,) + packed.shape, packed.dtype)],
        scratch_shapes=EXCHANGE_SEMS,
        compiler_params=_cp(("arbitrary", "arbitrary"), 56),
    )(x, x, head, norm_g, w_ext, packed)


def _gla_group(n_chunks):
    return 11 if n_chunks % 11 == 0 else 3


def _tri_dot(tri, x):
    hi = _bf(x)
    rest = x - hi.astype(F32)
    mid = _bf(rest)
    return _dot(tri, hi) + _dot(tri, mid) + _dot(tri, _bf(rest - mid.astype(F32)))


def _gla_gates(q_ref, k_ref, lr_ref, gw_ref, gb_ref, rows, not_first):
    z = _dot(lr_ref[rows, :], gw_ref[...]) + gb_ref[...]
    logsig = jnp.minimum(z, 0.0) - jnp.log(1.0 + jnp.exp(-jnp.abs(z)))
    row = lax.broadcasted_iota(jnp.int32, (GLA_CHUNK, GLA_KW), 0)
    live = jnp.logical_or(not_first, row >= FRONT)
    g = jnp.where(live, logsig * (1.0 / GLA_GATE_NORMALIZER), 0.0)
    ri = lax.broadcasted_iota(jnp.int32, (GLA_CHUNK, GLA_CHUNK), 0)
    ci = lax.broadcasted_iota(jnp.int32, (GLA_CHUNK, GLA_CHUNK), 1)
    tril = ci <= ri
    b = _tri_dot(_bf(tril.astype(F32)), g)
    bl = jnp.sum(jnp.where(row == GLA_CHUNK - 1, b, 0.0), axis=0, keepdims=True)
    eb, enb, elb, ebl = jnp.exp(b), jnp.exp(-b), jnp.exp(bl - b), jnp.exp(bl)
    q = q_ref[rows, :].astype(F32) * (GLA_DK ** -0.5)
    k = k_ref[rows, :].astype(F32)
    qe, ke, kl = q * eb, k * enb, k * elb
    return dict(z=z, live=live, tril=tril, row=row, eb=eb, enb=enb, elb=elb, ebl=ebl, qe=qe, ke=ke, kl=kl,
                qe_b=_bf(qe), ke_b=_bf(ke), kl_b=_bf(kl))


def _gla_in_specs(n_groups, gla_rows, rev):
    def rb(b, n):
        return b * n_groups + ((n_groups - 1 - n) if rev else n)

    return rb, [pl.BlockSpec((gla_rows, GLA_KW), lambda b, n: (rb(b, n), C_Q // GLA_KW)),
                pl.BlockSpec((gla_rows, GLA_KW), lambda b, n: (rb(b, n), C_K // GLA_KW)),
                pl.BlockSpec((gla_rows, GLA_VW), lambda b, n: (rb(b, n), C_V // GLA_VW)),
                pl.BlockSpec((gla_rows, GLA_VW), lambda b, n: (rb(b, n), C_Z // GLA_VW)),
                pl.BlockSpec((gla_rows, LANE), lambda b, n: (rb(b, n), C_LR // LANE)),
                pl.BlockSpec((LANE, GLA_KW), lambda b, n: (0, 0)),
                pl.BlockSpec((1, GLA_KW), lambda b, n: (0, 0)),
                pl.BlockSpec((1, GLA_DV), lambda b, n: (0, 0))]


def _gla_fwd(proj, gw_pad, gate_b, gla_norm_g, bsz, lp):
    n_chunks = lp // GLA_CHUNK
    gla_group = _gla_group(n_chunks)
    gla_rows = gla_group * GLA_CHUNK
    n_groups = n_chunks // gla_group
    tp = bsz * lp

    def body(q_ref, k_ref, v_ref, z_ref, lr_ref, gw_ref, gb_ref, gn_ref, oraw_ref, ya_ref, sall_ref, st_scr):
        grp = pl.program_id(1)

        @pl.when(grp == 0)
        def _():
            st_scr[...] = jnp.zeros_like(st_scr)

        chunks = [slice(j * GLA_CHUNK, (j + 1) * GLA_CHUNK) for j in range(gla_group)]
        cs = [_gla_gates(q_ref, k_ref, lr_ref, gw_ref, gb_ref, rows, True if j else grp > 0)
              for j, rows in enumerate(chunks)]
        gn = gn_ref[...]
        sts = [st_scr[h] for h in range(GLA_HEADS)]
        for j, (rows, c) in enumerate(zip(chunks, cs)):
            for h in range(GLA_HEADS):
                ks, vs = slice(h * GLA_DK, (h + 1) * GLA_DK), slice(h * GLA_DV, (h + 1) * GLA_DV)
                st = sts[h]
                sall_ref[0, j, h] = st
                v = v_ref[rows, vs]
                a = jnp.where(c["tril"], _dot_nt(c["qe_b"][:, ks], c["ke_b"][:, ks]), 0.0)
                o = _dot(_bf(a), v) + _dot_nt(c["qe_b"][:, ks], _bf(st))
                sts[h] = st * c["ebl"][:, ks] + _dot_tn(v, c["kl_b"][:, ks])
                oraw_ref[rows, vs] = o
                r = lax.rsqrt(jnp.mean(o * o, axis=-1, keepdims=True) + EPS)
                zg = z_ref[rows, vs].astype(F32)
                ya_ref[rows, vs] = _bf((o * r * gn) * (zg * _sigmoid(zg)))
        for h in range(GLA_HEADS):
            st_scr[h] = sts[h]

    rb, in_specs = _gla_in_specs(n_groups, gla_rows, False)
    return pl.pallas_call(
        body, name="gla_fwd", grid=(bsz, n_groups), in_specs=in_specs,
        out_specs=[pl.BlockSpec((gla_rows, GLA_VW), lambda b, n: (rb(b, n), 0)),
                   pl.BlockSpec((gla_rows, GLA_VW), lambda b, n: (rb(b, n), 0)),
                   pl.BlockSpec((1, gla_group, GLA_HEADS, GLA_DV, GLA_DK), lambda b, n: (b, n, 0, 0, 0))],
        out_shape=[jax.ShapeDtypeStruct((tp, GLA_VW), F32), jax.ShapeDtypeStruct((tp, GLA_VW), BF16),
                   jax.ShapeDtypeStruct((bsz, n_chunks, GLA_HEADS, GLA_DV, GLA_DK), F32)],
        scratch_shapes=[pltpu.VMEM((GLA_HEADS, GLA_DV, GLA_DK), F32)],
        compiler_params=_cp(("parallel", "arbitrary"), 56),
    )(proj, proj, proj, proj, proj, gw_pad, gate_b, gla_norm_g)


def _gla_bwd(proj, gw_pad, gate_b, gla_norm_g, o_raw, s_all, d_ya, dproj, bsz, lp):
    n_chunks = lp // GLA_CHUNK
    gla_group = _gla_group(n_chunks)
    gla_rows = gla_group * GLA_CHUNK
    n_groups = n_chunks // gla_group
    tp = bsz * lp

    def body(q_ref, k_ref, v_ref, z_ref, lr_ref, gw_ref, gb_ref, gn_ref, o_ref, s_ref, dya_ref, _,
             dp_ref, dz_ref, dgn_ref, dst_scr):
        dv_ref, dzg_ref = dp_ref.at[:, C_V:C_V + GLA_VW], dp_ref.at[:, C_Z:C_Z + GLA_VW]

        @pl.when(jnp.logical_and(pl.program_id(0) == 0, pl.program_id(1) == 0))
        def _():
            dgn_ref[...] = jnp.zeros_like(dgn_ref)

        @pl.when(pl.program_id(1) == 0)
        def _():
            dst_scr[...] = jnp.zeros_like(dst_scr)

        grp = n_groups - 1 - pl.program_id(1)
        chunks = [slice(j * GLA_CHUNK, (j + 1) * GLA_CHUNK) for j in range(gla_group)]
        cs = [_gla_gates(q_ref, k_ref, lr_ref, gw_ref, gb_ref, rows, True if j else grp > 0)
              for j, rows in enumerate(chunks)]
        gn = gn_ref[...]
        dgn = jnp.zeros((1, GLA_DV), F32)
        dqe_h, dke_h, dkl_h, dbl_h = ([[None] * GLA_HEADS for _ in chunks] for _ in range(4))
        dsts = [dst_scr[h] for h in range(GLA_HEADS)]
        for j in reversed(range(gla_group)):
            rows, c = chunks[j], cs[j]
            for h in range(GLA_HEADS):
                ks, vs = slice(h * GLA_DK, (h + 1) * GLA_DK), slice(h * GLA_DV, (h + 1) * GLA_DV)
                dst = dsts[h]
                v = v_ref[rows, vs]
                st = s_ref[0, j, h]
                o = o_ref[rows, vs]
                r = lax.rsqrt(jnp.mean(o * o, axis=-1, keepdims=True) + EPS)
                xh = o * r
                zg = z_ref[rows, vs].astype(F32)
                sg = _sigmoid(zg)
                dy = dya_ref[rows, vs].astype(F32)
                dzg_ref[rows, vs] = _bf(dy * (xh * gn) * (sg * (1.0 + zg * (1.0 - sg))))
                t = dy * (zg * sg)
                dgn += jnp.sum(t * xh, axis=0, keepdims=True)
                dxh = t * gn
                do_b = _bf(r * (dxh - xh * jnp.mean(dxh * xh, axis=-1, keepdims=True)))
                qe_b, ke_b, kl_b, dst_b = c["qe_b"][:, ks], c["ke_b"][:, ks], c["kl_b"][:, ks], _bf(dst)
                a = jnp.where(c["tril"], _dot_nt(qe_b, ke_b), 0.0)
                da_b = _bf(jnp.where(c["tril"], _dot_nt(do_b, v), 0.0))
                dqe_h[j][h] = _dot(da_b, ke_b) + _dot(do_b, _bf(st))
                dke_h[j][h] = _dot_tn(da_b, qe_b)
                dkl = _dot(v, dst_b)
                dkl_h[j][h] = dkl
                dv_ref[rows, vs] = _bf(_dot_tn(_bf(a), do_b) + _dot_nt(kl_b, dst_b))
                ddecay = jnp.sum(dst * st, axis=0, keepdims=True)
                dbl_h[j][h] = jnp.sum(dkl * c["kl"][:, ks], axis=0, keepdims=True) + ddecay * c["ebl"][:, ks]
                dsts[h] = dst * c["ebl"][:, ks] + _dot_tn(do_b, qe_b)
        for h in range(GLA_HEADS):
            dst_scr[h] = dsts[h]
        dgn_ref[...] += dgn
        ri = lax.broadcasted_iota(jnp.int32, (GLA_CHUNK, GLA_CHUNK), 0)
        ci = lax.broadcasted_iota(jnp.int32, (GLA_CHUNK, GLA_CHUNK), 1)
        triu = _bf((ci >= ri).astype(F32))
        for j, (rows, c) in enumerate(zip(chunks, cs)):
            dqe, dke, dkl, dbl = (jnp.concatenate(p[j], axis=1) for p in (dqe_h, dke_h, dkl_h, dbl_h))
            db = dqe * c["qe"] - dke * c["ke"] - dkl * c["kl"] + jnp.where(c["row"] == GLA_CHUNK - 1, dbl, 0.0)
            dg = _tri_dot(triu, db)
            dg = jnp.where(c["live"], dg, 0.0)
            dz_ref[rows, :] = dg * (1.0 / GLA_GATE_NORMALIZER) * _sigmoid(-c["z"])
            dp_ref[rows, C_Q:C_Q + GLA_KW] = _bf(dqe * c["eb"] * (GLA_DK ** -0.5))
            dp_ref[rows, C_K:C_K + GLA_KW] = _bf(dke * c["enb"] + dkl * c["elb"])

    rb, in_specs = _gla_in_specs(n_groups, gla_rows, True)
    wide = pl.BlockSpec((gla_rows, GLA_VW), lambda b, n: (rb(b, n), 0))
    group = C_MZ
    return pl.pallas_call(
        body, name="gla_bwd", grid=(bsz, n_groups),
        in_specs=in_specs + [wide, pl.BlockSpec((1, gla_group, GLA_HEADS, GLA_DV, GLA_DK),
                                                lambda b, n: (b, n_groups - 1 - n, 0, 0, 0)), wide,
                             pl.BlockSpec(memory_space=pl.ANY)],
        out_specs=[pl.BlockSpec((gla_rows, group), lambda b, n: (rb(b, n), 0)),
                   pl.BlockSpec((gla_rows, GLA_KW), lambda b, n: (rb(b, n), 0)),
                   pl.BlockSpec((1, GLA_DV), lambda b, n: (0, 0))],
        out_shape=[jax.ShapeDtypeStruct((tp, N_EXT), BF16), jax.ShapeDtypeStruct((tp, GLA_KW), F32),
                   jax.ShapeDtypeStruct((1, GLA_DV), F32)],
        input_output_aliases={11: 0},
        scratch_shapes=[pltpu.VMEM((GLA_HEADS, GLA_DV, GLA_DK), F32)],
        compiler_params=_cp(("arbitrary", "arbitrary"), 56),
    )(proj, proj, proj, proj, proj, gw_pad, gate_b, gla_norm_g, o_raw, s_all, d_ya, dproj)


def _gate_bwd(dz, proj, gw_pad):
    tp = dz.shape[0]
    tm = _big_tok(tp)

    def body(dz_ref, lr_ref, gw_ref, dlr_ref, dgw_ref, dgb_ref):
        @pl.when(pl.program_id(0) == 0)
        def _():
            dgw_ref[...] = jnp.zeros_like(dgw_ref)
            dgb_ref[...] = jnp.zeros_like(dgb_ref)

        dz = dz_ref[...]
        dz_b = _bf(dz)
        dlr_ref[...] = _bf(_dot_nt(dz_b, gw_ref[...]))
        dgw_ref[...] += _dot_tn(lr_ref[...], dz_b)
        dgb_ref[...] += jnp.sum(dz, axis=0, keepdims=True)

    return pl.pallas_call(
        body, name="gate_bwd", grid=(tp // tm,),
        in_specs=[pl.BlockSpec((tm, GLA_KW), lambda i: (i, 0)),
                  pl.BlockSpec((tm, LANE), lambda i: (i, C_LR // LANE)),
                  pl.BlockSpec((LANE, GLA_KW), lambda i: (0, 0))],
        out_specs=[pl.BlockSpec((tm, LANE), lambda i: (i, 0)),
                   pl.BlockSpec((LANE, GLA_KW), lambda i: (0, 0)),
                   pl.BlockSpec((1, GLA_KW), lambda i: (0, 0))],
        out_shape=[jax.ShapeDtypeStruct((tp, LANE), BF16), jax.ShapeDtypeStruct((LANE, GLA_KW), F32),
                   jax.ShapeDtypeStruct((1, GLA_KW), F32)],
        compiler_params=_cp(("arbitrary",)),
    )(dz, proj, gw_pad)


def _rms_fwd(x):
    r = lax.rsqrt(jnp.mean(x * x, axis=-1, keepdims=True) + EPS)
    return x * r, r


def _rms_bwd(dy, xh, r, g):
    dxh = dy * g
    dx = r * (dxh - xh * jnp.mean(dxh * xh, axis=-1, keepdims=True))
    return dx, jnp.sum(dy * xh, axis=0, keepdims=True)


def _q_up(proj, q_norm_g, wn, wr, wt, cos_t, sin_t, bsz, lp):
    tp = bsz * lp
    tok = _attn_block(lp)
    nb = lp // tok

    def body(cq_ref, g_ref, wn_ref, wr_ref, wt_ref, cos_ref, sin_ref, q_ref):
        xh, _ = _rms_fwd(cq_ref[...].astype(F32))
        cqn = _bf(xh * g_ref[...])
        nope = _dot(cqn, wn_ref[...])
        rope = _dot(cqn, wr_ref[...])
        rot = _dot(cqn, wt_ref[...])
        cos, sin = cos_ref[...], sin_ref[...]
        one = (lax.broadcasted_iota(jnp.int32, (tok, LANE), 1) == BIAS_LANE).astype(F32)
        for h in range(MLA_HEADS):
            sl = slice(h * LANE, (h + 1) * LANE)
            q_ref[:, h * QKW:h * QKW + LANE] = _bf(nope[:, sl])
            q_ref[:, h * QKW + LANE:(h + 1) * QKW] = _bf(rope[:, sl] * cos + rot[:, sl] * sin + one)

    wspec = pl.BlockSpec((MLA_QR, MLA_HEADS * LANE), lambda b, i: (0, 0))
    tspec = pl.BlockSpec((tok, LANE), lambda b, i: (i, 0))
    return pl.pallas_call(
        body, name="mla_q_up", grid=(bsz, nb),
        in_specs=[pl.BlockSpec((tok, MLA_QR), lambda b, i: (b * nb + i, C_CQ // MLA_QR)),
                  pl.BlockSpec((1, MLA_QR), lambda b, i: (0, 0)), wspec, wspec, wspec, tspec, tspec],
        out_specs=pl.BlockSpec((tok, MLA_HEADS * QKW), lambda b, i: (b * nb + i, 0)),
        out_shape=jax.ShapeDtypeStruct((tp, MLA_HEADS * QKW), BF16),
        compiler_params=_cp(("parallel", "parallel")),
    )(proj, q_norm_g, wn, wr, wt, cos_t, sin_t)


def _kv_up(proj, kv_norm_g, wk, wv, cos_t, sin_t, bsz, lp):
    tp = bsz * lp
    tok = _attn_block(lp)
    nb = lp // tok

    def body(ckv_ref, kr_ref, krot_ref, g_ref, wk_ref, wv_ref, cos_ref, sin_ref, k_ref, v_ref):
        xh, _ = _rms_fwd(ckv_ref[...].astype(F32))
        cn = _bf(xh * g_ref[...])
        kn = _dot(cn, wk_ref[...])
        v_ref[...] = _bf(_dot(cn, wv_ref[...]))
        pos = pl.program_id(1) * tok + lax.broadcasted_iota(jnp.int32, (tok, LANE), 0)
        lane = lax.broadcasted_iota(jnp.int32, (tok, LANE), 1)
        bias = jnp.where(jnp.logical_and(lane == BIAS_LANE, pos < FRONT), KEY_BIAS, 0.0)
        kr = _bf(kr_ref[...].astype(F32) * cos_ref[...] + krot_ref[...].astype(F32) * sin_ref[...] + bias)
        for h in range(MLA_HEADS):
            k_ref[:, h * QKW:h * QKW + LANE] = _bf(kn[:, h * LANE:(h + 1) * LANE])
            k_ref[:, h * QKW + LANE:(h + 1) * QKW] = kr

    wspec = pl.BlockSpec((MLA_KVR, MLA_HEADS * LANE), lambda b, i: (0, 0))
    tspec = pl.BlockSpec((tok, LANE), lambda b, i: (i, 0))
    return pl.pallas_call(
        body, name="mla_kv_up", grid=(bsz, nb),
        in_specs=[pl.BlockSpec((tok, LANE), lambda b, i: (b * nb + i, C_CKV // LANE)),
                  pl.BlockSpec((tok, LANE), lambda b, i: (b * nb + i, C_KR // LANE)),
                  pl.BlockSpec((tok, LANE), lambda b, i: (b * nb + i, C_KROT // LANE)),
                  pl.BlockSpec((1, MLA_KVR), lambda b, i: (0, 0)), wspec, wspec, tspec, tspec],
        out_specs=[pl.BlockSpec((tok, MLA_HEADS * QKW), lambda b, i: (b * nb + i, 0)),
                   pl.BlockSpec((tok, MLA_HEADS * LANE), lambda b, i: (b * nb + i, 0))],
        out_shape=[jax.ShapeDtypeStruct((tp, MLA_HEADS * QKW), BF16),
                   jax.ShapeDtypeStruct((tp, MLA_HEADS * LANE), BF16)],
        compiler_params=_cp(("parallel", "parallel")),
    )(proj, proj, proj, kv_norm_g, wk, wv, cos_t, sin_t)


ATT_SCALE = MLA_QK ** -0.5


KEY_BIAS = -1e30
BIAS_LANE = MLA_ROPE
NEG = 2 * KEY_BIAS
LOG2E = 1.4426950408889634
EXP2_SCALE = ATT_SCALE * LOG2E


def _causal_fill(s, r0, fill):
    tq, kmax = s.shape
    a = r0 // LANE * LANE
    mask = (a + lax.broadcasted_iota(jnp.int32, (tq, kmax - a), 1)
            <= r0 + lax.broadcasted_iota(jnp.int32, (tq, kmax - a), 0))
    right = jnp.where(mask, s[:, a:], fill)
    return jnp.concatenate([s[:, :a], right], axis=1) if a else right


def _attn_fwd(qf, kf, vf, proj, bsz, lp):
    tp = bsz * lp
    tq = _attn_block(lp)
    nh = 2

    def body(q_ref, k_ref, v_ref, mz_ref, ob_ref, yb_ref, lse_ref):
        starts = list(range(0, lp, tq))
        for pair in (starts[i:i + 2] for i in range(0, len(starts), 2)):
            work = [(r0, h) for r0 in pair for h in range(nh)]
            ss = [_causal_fill(_dot_nt(q_ref[r0:r0 + tq, h * QKW:(h + 1) * QKW],
                                       k_ref[0:r0 + tq, h * QKW:(h + 1) * QKW]), r0, NEG) for r0, h in work]
            ms = [jnp.max(s, axis=-1, keepdims=True) for s in ss]
            ps = [jnp.exp2((s - m) * EXP2_SCALE) for s, m in zip(ss, ms)]
            ls = [jnp.sum(p, axis=-1, keepdims=True) for p in ps]
            for (r0, h), p, m, l in zip(work, ps, ms, ls):
                rows, cols = slice(r0, r0 + tq), slice(h * MLA_DV, (h + 1) * MLA_DV)
                o = _dot(_bf(p), v_ref[0:r0 + tq, cols]) / l
                ob_ref[rows, cols] = _bf(o)
                mz = mz_ref[rows, cols].astype(F32)
                yb_ref[rows, cols] = _bf(o * (mz * _sigmoid(mz)))
                lse_ref[0, h, rows, :] = jnp.broadcast_to(m * EXP2_SCALE + jnp.log2(l), (tq, LANE))

    head = lambda off: pl.BlockSpec((lp, nh * MLA_DV), lambda b, h: (b, off + h))
    wide = pl.BlockSpec((lp, nh * QKW), lambda b, h: (b, h))
    return pl.pallas_call(
        body, name="mla_attn_fwd", grid=(bsz, MLA_HEADS // nh),
        in_specs=[wide, wide, head(0), head(C_MZ // (nh * MLA_DV))],
        out_specs=[head(0), head(0), pl.BlockSpec((1, nh, lp, LANE), lambda b, h: (b, h, 0, 0))],
        out_shape=[jax.ShapeDtypeStruct((tp, MLA_HEADS * MLA_DV), BF16),
                   jax.ShapeDtypeStruct((tp, MLA_HEADS * MLA_DV), BF16),
                   jax.ShapeDtypeStruct((bsz, MLA_HEADS, lp, LANE), F32)],
        compiler_params=_cp(("parallel", "parallel"), 56),
    )(qf, kf, vf, proj)


def _attn_bwd_blocks(lp):
    return [(0, X0)] + [(r0, min(MXU_DEPTH, lp - r0)) for r0 in range(X0, lp, MXU_DEPTH)]


def _attn_bwd(qf, kf, vf, d_o, lse, delta, bsz, lp):
    tp = bsz * lp

    def body(q_ref, k_ref, v_ref, do_ref, lse_ref, dl_ref, dq_ref, dk_ref, dv_ref, dk_acc, dv_acc):
        dk_acc[...] = jnp.zeros_like(dk_acc)
        dv_acc[...] = jnp.zeros_like(dv_acc)
        for r0, tq in _attn_bwd_blocks(lp):
            rows, kmax = slice(r0, r0 + tq), r0 + tq
            q, do = q_ref[rows, :], do_ref[rows, :]
            k, v = k_ref[0:kmax, :], v_ref[0:kmax, :]
            p = jnp.exp2(_dot_nt(q, k) * EXP2_SCALE - lse_ref[0, 0, rows, :][:, :1])
            p = _causal_fill(p, r0, 0.0)
            ds = _bf(p * (_dot_nt(do, v) - dl_ref[0, rows, :][:, :1]))
            dq_ref[rows, :] = _bf(_dot(ds, k) * ATT_SCALE)
            dk_acc[0:kmax, :] += _dot_tn(ds, q)
            dv_acc[0:kmax, :] += _dot_tn(_bf(p), do)
        dk_ref[...] = _bf(dk_acc[...] * ATT_SCALE)
        dv_ref[...] = _bf(dv_acc[...])

    wide = pl.BlockSpec((lp, QKW), lambda b, h: (b, h))
    narrow = pl.BlockSpec((lp, MLA_DV), lambda b, h: (b, h))
    stat = pl.BlockSpec((1, 1, lp, LANE), lambda b, h: (b, h, 0, 0))
    return pl.pallas_call(
        body, name="mla_attn_bwd", grid=(bsz, MLA_HEADS),
        in_specs=[wide, wide, narrow, narrow, stat, pl.BlockSpec((1, lp, LANE), lambda b, h: (h, b, 0))],
        out_specs=[wide, wide, narrow],
        out_shape=[jax.ShapeDtypeStruct((tp, MLA_HEADS * QKW), BF16), jax.ShapeDtypeStruct((tp, MLA_HEADS * QKW), BF16),
                   jax.ShapeDtypeStruct((tp, MLA_HEADS * MLA_DV), BF16)],
        scratch_shapes=[pltpu.VMEM((lp, QKW), F32), pltpu.VMEM((lp, MLA_DV), F32)],
        compiler_params=_cp(("parallel", "parallel"), 56),
    )(qf, kf, vf, d_o, lse, delta)


def _q_up_bwd(dqf, proj, q_norm_g, wn, wr, wt, cos_t, sin_t, dproj, bsz, lp):
    tp = bsz * lp
    tok = _attn_block(lp)
    nb = lp // tok
    hw = MLA_HEADS * LANE

    def body(dq_ref, cq_ref, g_ref, wn_ref, wr_ref, wt_ref, cos_ref, sin_ref, _,
             dcq_ref, dwn_ref, dwr_ref, dwt_ref, dg_ref):
        @pl.when(jnp.logical_and(pl.program_id(0) == 0, pl.program_id(1) == 0))
        def _():
            for r in (dwn_ref, dwr_ref, dwt_ref, dg_ref):
                r[...] = jnp.zeros_like(r)

        g = g_ref[...]
        xh, r = _rms_fwd(cq_ref[...].astype(F32))
        cqn = _bf(xh * g)
        dn = jnp.concatenate([dq_ref[:, h * QKW:h * QKW + LANE] for h in range(MLA_HEADS)], axis=1)
        dr = jnp.concatenate([dq_ref[:, h * QKW + LANE:(h + 1) * QKW] for h in range(MLA_HEADS)], axis=1).astype(F32)
        dr_c = _bf(dr * jnp.tile(cos_ref[...], (1, MLA_HEADS)))
        dr_s = _bf(dr * jnp.tile(sin_ref[...], (1, MLA_HEADS)))
        dcqn = _dot_nt(dn, wn_ref[...]) + _dot_nt(dr_c, wr_ref[...]) + _dot_nt(dr_s, wt_ref[...])
        dwn_ref[...] += _dot_tn(cqn, dn)
        dwr_ref[...] += _dot_tn(cqn, dr_c)
        dwt_ref[...] += _dot_tn(cqn, dr_s)
        dx, dg = _rms_bwd(dcqn, xh, r, g)
        dcq_ref[...] = _bf(dx)
        dg_ref[...] += dg

    aspec = pl.BlockSpec((MLA_QR, hw), lambda b, i: (0, 0))
    tspec = pl.BlockSpec((tok, LANE), lambda b, i: (i, 0))
    return pl.pallas_call(
        body, name="mla_q_up_bwd", grid=(bsz, nb),
        in_specs=[pl.BlockSpec((tok, MLA_HEADS * QKW), lambda b, i: (b * nb + i, 0)),
                  pl.BlockSpec((tok, MLA_QR), lambda b, i: (b * nb + i, C_CQ // MLA_QR)),
                  pl.BlockSpec((1, MLA_QR), lambda b, i: (0, 0)), aspec, aspec, aspec, tspec, tspec,
                  pl.BlockSpec(memory_space=pl.ANY)],
        out_specs=[pl.BlockSpec((tok, MLA_QR), lambda b, i: (b * nb + i, C_CQ // MLA_QR)), aspec, aspec, aspec,
                   pl.BlockSpec((1, MLA_QR), lambda b, i: (0, 0))],
        out_shape=[jax.ShapeDtypeStruct((tp, N_EXT), BF16)] + [jax.ShapeDtypeStruct((MLA_QR, hw), F32)] * 3
        + [jax.ShapeDtypeStruct((1, MLA_QR), F32)],
        input_output_aliases={8: 0},
        compiler_params=_cp(("arbitrary", "arbitrary")),
    )(dqf, proj, q_norm_g, wn, wr, wt, cos_t, sin_t, dproj)


def _kv_up_bwd(dkf, dvf, proj, kv_norm_g, wk, wv, cos_t, sin_t, d_lr, dproj, bsz, lp):
    tp = bsz * lp
    tok = _attn_block(lp)
    nb = lp // tok
    hw = MLA_HEADS * LANE

    def body(dk_ref, dv_ref, ckv_ref, g_ref, wk_ref, wv_ref, cos_ref, sin_ref, dlr_ref, _,
             dp_ref, dwk_ref, dwv_ref, dg_ref):
        dckv_ref, dkr_ref, dkrot_ref = (dp_ref.at[:, j * LANE:(j + 1) * LANE] for j in range(3))
        dp_ref[:, 3 * LANE:] = dlr_ref[...]
        @pl.when(jnp.logical_and(pl.program_id(0) == 0, pl.program_id(1) == 0))
        def _():
            for r in (dwk_ref, dwv_ref, dg_ref):
                r[...] = jnp.zeros_like(r)

        g = g_ref[...]
        xh, r = _rms_fwd(ckv_ref[...].astype(F32))
        cn = _bf(xh * g)
        dv = dv_ref[...]
        dn = jnp.concatenate([dk_ref[:, h * QKW:h * QKW + LANE] for h in range(MLA_HEADS)], axis=1)
        dcn = _dot_nt(dv, wv_ref[...]) + _dot_nt(dn, wk_ref[...])
        dwv_ref[...] += _dot_tn(cn, dv)
        dwk_ref[...] += _dot_tn(cn, dn)
        drope = jnp.zeros((tok, LANE), F32)
        for h in range(MLA_HEADS):
            drope += dk_ref[:, h * QKW + LANE:(h + 1) * QKW].astype(F32)
        dkr_ref[...] = _bf(drope * cos_ref[...])
        dkrot_ref[...] = _bf(drope * sin_ref[...])
        dx, dg = _rms_bwd(dcn, xh, r, g)
        dckv_ref[...] = _bf(dx)
        dg_ref[...] += dg

    aspec = pl.BlockSpec((MLA_KVR, hw), lambda b, i: (0, 0))
    tspec = pl.BlockSpec((tok, LANE), lambda b, i: (i, 0))
    ospec = pl.BlockSpec((tok, LANE), lambda b, i: (b * nb + i, 0))
    return pl.pallas_call(
        body, name="mla_kv_up_bwd", grid=(bsz, nb),
        in_specs=[pl.BlockSpec((tok, MLA_HEADS * QKW), lambda b, i: (b * nb + i, 0)),
                  pl.BlockSpec((tok, hw), lambda b, i: (b * nb + i, 0)),
                  pl.BlockSpec((tok, LANE), lambda b, i: (b * nb + i, C_CKV // LANE)),
                  pl.BlockSpec((1, MLA_KVR), lambda b, i: (0, 0)), aspec, aspec, tspec, tspec, ospec,
                  pl.BlockSpec(memory_space=pl.ANY)],
        out_specs=[pl.BlockSpec((tok, 4 * LANE), lambda b, i: (b * nb + i, C_CKV // (4 * LANE))), aspec, aspec,
                   pl.BlockSpec((1, MLA_KVR), lambda b, i: (0, 0))],
        out_shape=[jax.ShapeDtypeStruct((tp, N_EXT), BF16)] + [jax.ShapeDtypeStruct((MLA_KVR, hw), F32)] * 2
        + [jax.ShapeDtypeStruct((1, MLA_KVR), F32)],
        input_output_aliases={9: 0},
        compiler_params=_cp(("arbitrary", "arbitrary")),
    )(dkf, dvf, proj, kv_norm_g, wk, wv, cos_t, sin_t, d_lr, dproj)


def _mid_fwd(ya_in, yb_in, proj, hp, target, w_gp, w_mp, w_o, final_g, bsz, lp):
    tp = bsz * lp
    tm = _attn_block(lp)
    nb = lp // tm
    last = pl.cdiv(lp - X0, tm) - 1

    def body(ya_ref, yb_ref, gg_ref, gm_ref, h_ref, ta_ref, tb_ref, wgp_ref, wmp_ref, wo_ref, fg_ref,
             ya_out, yb_out, dh_ref, loss_ref, dfg_ref):
        @pl.when(jnp.logical_and(pl.program_id(0) == 0, pl.program_id(1) == 0))
        def _():
            loss_ref[...] = jnp.zeros_like(loss_ref)
            dfg_ref[...] = jnp.zeros_like(dfg_ref)

        y_a = _dot(ya_ref[...], wgp_ref[...])
        y_b = _dot(yb_ref[...], wmp_ref[...])
        ya_out[...] = _bf(y_a)
        yb_out[...] = _bf(y_b)
        merged = _sigmoid(gg_ref[...].astype(F32)) * y_a + _sigmoid(gm_ref[...].astype(F32)) * y_b
        h2 = h_ref[...] + _dot(_bf(merged), wo_ref[...])
        fg = fg_ref[...]
        xh, r = _rms_fwd(h2)
        pos = pl.program_id(1) * tm + lax.broadcasted_iota(jnp.int32, (tm, 1), 0)
        t = jnp.concatenate([ta_ref[0, tm - X0:, :], tb_ref[0, :tm - X0, :]], axis=0)
        err = jnp.where(pos >= X0, xh * fg - t, 0.0)
        loss_ref[...] += 0.5 * jnp.sum(jnp.mean(err * err, axis=-1, keepdims=True), axis=0, keepdims=True)
        dy = err * (1.0 / D_MODEL)
        dx, dfg = _rms_bwd(dy, xh, r, fg)
        dh_ref[...] = dx
        dfg_ref[...] += dfg

    tok = lambda c: pl.BlockSpec((tm, D_MODEL), lambda b, i: (b * nb + i, c))
    wspec = pl.BlockSpec((D_MODEL, D_MODEL), lambda b, i: (0, 0))
    return pl.pallas_call(
        body, name="mid_fwd", grid=(bsz, nb),
        in_specs=[tok(0), tok(0), tok(C_GG // D_MODEL), tok(C_GM // D_MODEL), tok(0),
                  pl.BlockSpec((1, tm, D_MODEL), lambda b, i: (b, jnp.maximum(i - 1, 0), 0)),
                  pl.BlockSpec((1, tm, D_MODEL), lambda b, i: (b, jnp.minimum(i, last), 0)),
                  wspec, wspec, wspec, pl.BlockSpec((1, D_MODEL), lambda b, i: (0, 0))],
        out_specs=[tok(0), tok(0), tok(0), pl.BlockSpec((1, LANE), lambda b, i: (0, 0)),
                   pl.BlockSpec((1, D_MODEL), lambda b, i: (0, 0))],
        out_shape=[jax.ShapeDtypeStruct((tp, D_MODEL), BF16), jax.ShapeDtypeStruct((tp, D_MODEL), BF16),
                   jax.ShapeDtypeStruct((tp, D_MODEL), F32), jax.ShapeDtypeStruct((1, LANE), F32),
                   jax.ShapeDtypeStruct((1, D_MODEL), F32)],
        compiler_params=_cp(("arbitrary", "arbitrary"), 48),
    )(ya_in, yb_in, proj, proj, hp, target, target, w_gp, w_mp, w_o, final_g)


def _mid_bwd(dh2, y_a, y_b, proj, ya_in, yb_in, o_b, w_o, w_gp, w_mp, bsz, lp):
    tp = bsz * lp
    tm = MXU_DEPTH if tp % MXU_DEPTH == 0 else _attn_block(lp)
    nsteps = tp // tm
    group = 3 * D_MODEL

    def body(dh_ref, ya_ref, yb_ref, mz_ref, gg_ref, gm_ref, yai_ref, ybi_ref, ob_ref, wo_ref, wgp_ref, wmp_ref,
             dyai_ref, do_ref, dp_ref, dl_ref, dwo_ref, dwgp_ref, dwmp_ref, a_o, a_gp, a_mp):
        @pl.when(pl.program_id(0) == 0)
        def _():
            for r in (a_o, a_gp, a_mp):
                r[...] = jnp.zeros_like(r)

        dh = _bf(dh_ref[...])
        dm = _dot_nt(dh, wo_ref[...])
        y_a, y_b = ya_ref[...].astype(F32), yb_ref[...].astype(F32)
        sg, sm = _sigmoid(gg_ref[...].astype(F32)), _sigmoid(gm_ref[...].astype(F32))
        d_ya, d_yb = _bf(sg * dm), _bf(sm * dm)
        dp_ref[:, D_MODEL:2 * D_MODEL] = _bf(dm * y_a * sg * (1.0 - sg))
        dp_ref[:, 2 * D_MODEL:] = _bf(dm * y_b * sm * (1.0 - sm))
        merged = _bf(sg * y_a + sm * y_b)
        dy = _dot_nt(d_yb, wmp_ref[...])
        dyai_ref[...] = _bf(_dot_nt(d_ya, wgp_ref[...]))
        a_o[...] += _dot_tn(merged, dh)
        a_gp[...] += _dot_tn(yai_ref[...], d_ya)
        a_mp[...] += _dot_tn(ybi_ref[...], d_yb)
        mz, o = mz_ref[...].astype(F32), ob_ref[...].astype(F32)
        s = _sigmoid(mz)
        do = _bf(dy * (mz * s))
        do_ref[...] = do
        dp_ref[:, :D_MODEL] = _bf(dy * o * (s * (1.0 + mz * (1.0 - s))))
        prod = do.astype(F32) * o
        for h in range(MLA_HEADS):
            dl = jnp.sum(prod[:, h * MLA_DV:(h + 1) * MLA_DV], axis=-1, keepdims=True)
            dl_ref[h] = jnp.broadcast_to(dl, (tm, LANE))

        @pl.when(pl.program_id(0) == nsteps - 1)
        def _():
            pltpu.sync_copy(a_o, dwo_ref)
            pltpu.sync_copy(a_gp, dwgp_ref)
            pltpu.sync_copy(a_mp, dwmp_ref)

    tok = lambda c: pl.BlockSpec((tm, D_MODEL), lambda i: (i, c))
    wspec = pl.BlockSpec((D_MODEL, D_MODEL), lambda i: (0, 0))
    anyspec = pl.BlockSpec(memory_space=pl.ANY)
    wshape = jax.ShapeDtypeStruct((D_MODEL, D_MODEL), F32)
    return pl.pallas_call(
        body, name="mid_bwd", grid=(nsteps,),
        in_specs=[tok(0), tok(0), tok(0), tok(C_MZ // D_MODEL), tok(C_GG // D_MODEL), tok(C_GM // D_MODEL),
                  tok(0), tok(0), tok(0), wspec, wspec, wspec],
        out_specs=[tok(0), tok(0), pl.BlockSpec((tm, group), lambda i: (i, C_MZ // group)),
                   pl.BlockSpec((MLA_HEADS, tm, LANE), lambda i: (0, i, 0)), anyspec, anyspec, anyspec],
        out_shape=[jax.ShapeDtypeStruct((tp, D_MODEL), BF16)] * 2 + [jax.ShapeDtypeStruct((tp, N_EXT), BF16),
                   jax.ShapeDtypeStruct((MLA_HEADS, tp, LANE), F32)] + [wshape] * 3,
        scratch_shapes=[pltpu.VMEM((D_MODEL, D_MODEL), F32)] * 3,
        compiler_params=_cp(("arbitrary",), 56),
    )(dh2, y_a, y_b, proj, proj, proj, ya_in, yb_in, o_b, w_o, w_gp, w_mp)


MESH_ID = pl.DeviceIdType.MESH
EXCHANGE_SEMS = [pltpu.SemaphoreType.DMA((N_DEV - 1,)), pltpu.SemaphoreType.DMA((N_DEV - 1,)), pltpu.SemaphoreType.DMA]


def _my_place():
    return lax.axis_index("x"), lax.axis_index("y"), lax.axis_index("c")


def _exchange(g_ref, recv_ref, send_sems, recv_sems, local_sem, start, same=False):
    x, y, c = _my_place()
    me = 4 * x + 2 * y + c
    own = pltpu.make_async_copy(g_ref if same else g_ref.at[me], recv_ref.at[me], local_sem)
    sends, lands = [], []
    for d in range(1, N_DEV):
        px = 1 - x if d & 4 else x
        py = 1 - y if d & 2 else y
        pc = 1 - c if d & 1 else c
        peer = 4 * px + 2 * py + pc
        for slot, group in ((me, sends),) if start else ((me, sends), (peer, lands)):
            group.append(pltpu.make_async_remote_copy(
                src_ref=g_ref if same else g_ref.at[peer], dst_ref=recv_ref.at[slot], send_sem=send_sems.at[d - 1],
                recv_sem=recv_sems.at[d - 1], device_id=(px, py, pc), device_id_type=MESH_ID))
    if start:
        own.start()
        for cp in sends:
            cp.start()
    else:
        for cp in lands:
            cp.wait_recv()
        for cp in sends:
            cp.wait_send()
        own.wait()


def _dw_in(u, dproj, slabs):
    tp = u.shape[0]
    tn = 3 * LANE
    nj = N_EXT // tn

    def body(u_ref, d_ref, g_ref, o_ref, recv_ref, send_sems, recv_sems, local_sem):
        j = pl.program_id(0)

        @pl.when(j == 0)
        def _():
            _exchange(g_ref, recv_ref, send_sems, recv_sems, local_sem, True)

        o_ref[...] = _dot_tn(d_ref[...], u_ref[...])

        @pl.when(j == nj - 1)
        def _():
            _exchange(g_ref, recv_ref, send_sems, recv_sems, local_sem, False)

    anyspec = pl.BlockSpec(memory_space=pl.ANY)
    return pl.pallas_call(
        body, name="dw_in", grid=(nj,),
        in_specs=[pl.BlockSpec((tp, D_MODEL), lambda j: (0, 0), pipeline_mode=pl.Buffered(1)),
                  pl.BlockSpec((tp, tn), lambda j: (0, j)), anyspec],
        out_specs=[pl.BlockSpec((tn, D_MODEL), lambda j: (j, 0)), anyspec],
        out_shape=[jax.ShapeDtypeStruct((N_EXT, D_MODEL), F32), jax.ShapeDtypeStruct(slabs.shape, slabs.dtype)],
        scratch_shapes=EXCHANGE_SEMS,
        compiler_params=_cp(("arbitrary",), 56),
    )(u, dproj, slabs)


def _dx_in(dproj, w_ext, hp, dh2, norm_g, slabs):
    tp = hp.shape[0]
    tm = 2 * TOK
    ni = tp // tm

    def body(d_ref, w_ref, h_ref, dh_ref, g_ref, s_ref, o_ref, dg_ref, recv_ref, send_sems, recv_sems, local_sem):
        i = pl.program_id(0)

        @pl.when(i == 0)
        def _():
            _exchange(s_ref, recv_ref, send_sems, recv_sems, local_sem, True)
            dg_ref[...] = jnp.zeros_like(dg_ref)

        du = _dot_nt(d_ref[...], w_ref[...])
        g = g_ref[...]
        xh, r = _rms_fwd(h_ref[...])
        dx, dg = _rms_bwd(du, xh, r, g)
        o_ref[...] = dh_ref[...] + dx
        dg_ref[...] += dg

        @pl.when(i == ni - 1)
        def _():
            _exchange(s_ref, recv_ref, send_sems, recv_sems, local_sem, False)

    tok = pl.BlockSpec((tm, D_MODEL), lambda i: (i, 0))
    anyspec = pl.BlockSpec(memory_space=pl.ANY)
    return pl.pallas_call(
        body, name="dx_in", grid=(ni,),
        in_specs=[pl.BlockSpec((tm, N_EXT), lambda i: (i, 0)),
                  pl.BlockSpec((D_MODEL, N_EXT), lambda i: (0, 0), pipeline_mode=pl.Buffered(1)),
                  tok, tok, pl.BlockSpec((1, D_MODEL), lambda i: (0, 0)), anyspec],
        out_specs=[tok, pl.BlockSpec((1, D_MODEL), lambda i: (0, 0)), anyspec],
        out_shape=[jax.ShapeDtypeStruct((tp, D_MODEL), F32), jax.ShapeDtypeStruct((1, D_MODEL), F32),
                   jax.ShapeDtypeStruct(slabs.shape, slabs.dtype)],
        scratch_shapes=EXCHANGE_SEMS,
        compiler_params=_cp(("arbitrary",), 56),
    )(dproj, w_ext, hp, dh2, norm_g, slabs)


def _meta_grad(dhp3):
    bsz = dhp3.shape[0]

    def body(d_ref, o_ref):
        @pl.when(pl.program_id(0) == 0)
        def _():
            o_ref[...] = jnp.zeros_like(o_ref)

        o_ref[...] += d_ref[0]

    return pl.pallas_call(
        body, name="meta_grad", grid=(bsz,),
        in_specs=[pl.BlockSpec((1, N_META, D_MODEL), lambda b: (b, FRONT // N_META, 0))],
        out_specs=pl.BlockSpec((N_META, D_MODEL), lambda b: (0, 0)),
        out_shape=jax.ShapeDtypeStruct((N_META, D_MODEL), F32),
        compiler_params=_cp(("arbitrary",)),
    )(dhp3)


W_IN_SHARD = N_IN // N_DEV


def _pad_lanes(a, width=LANE):
    return jnp.pad(a, [(0, 0)] * (a.ndim - 1) + [(0, width - a.shape[-1])])


def _rot_cols(w):
    half = w.shape[-1] // 2
    return jnp.concatenate([-w[..., half:], w[..., :half]], axis=-1)


def _unrot_cols(dw):
    half = dw.shape[-1] // 2
    return jnp.concatenate([dw[..., half:], -dw[..., :half]], axis=-1)


def _w_in_cols(shards, lo, hi):
    parts = []
    for k in range(lo // W_IN_SHARD, (hi - 1) // W_IN_SHARD + 1):
        a, b = max(lo, k * W_IN_SHARD), min(hi, (k + 1) * W_IN_SHARD)
        parts.append(shards[k][:, a - k * W_IN_SHARD:b - k * W_IN_SHARD])
    return parts[0] if len(parts) == 1 else jnp.concatenate(parts, axis=1)


def _w_in_ext(shards):
    c = lambda lo, hi: _w_in_cols(shards, lo, hi)
    kr = c(O_KR, O_MZ)
    return jnp.concatenate([
        c(O_V, O_LR), c(O_Z, O_CQ), c(O_Q, O_K), c(O_K, O_V), c(O_MZ, O_GG), c(O_GG, O_GM), c(O_GM, N_IN),
        c(O_CKV, O_KR), _pad_lanes(kr), _pad_lanes(_rot_cols(kr)), _pad_lanes(c(O_LR, O_Z)), c(O_CQ, O_CKV)], axis=1)


def _w_in_grad_t(dwt):
    g = lambda start, width: dwt[start:start + width]
    half = MLA_ROPE // 2
    krot = g(C_KROT, MLA_ROPE)
    kr = g(C_KR, MLA_ROPE) + jnp.concatenate([krot[half:], -krot[:half]], axis=0)
    return jnp.concatenate([
        g(C_Q, GLA_KW), g(C_K, GLA_KW), g(C_V, GLA_VW), g(C_LR, GLA_RANK), g(C_Z, GLA_VW), g(C_CQ, MLA_QR),
        g(C_CKV, MLA_KVR), kr, g(C_MZ, D_MODEL), g(C_GG, D_MODEL), g(C_GM, D_MODEL)], axis=0)


def _rope_tables(lp):
    inv = 1.0 / (ROPE_BASE ** (jnp.arange(0, MLA_ROPE, 2, dtype=F32) / MLA_ROPE))
    ang = (jnp.arange(lp, dtype=F32) - FRONT)[:, None] * inv[None, :]
    cos, sin = jnp.cos(ang), jnp.sin(ang)
    return _pad_lanes(jnp.concatenate([cos, cos], axis=1)), _pad_lanes(jnp.concatenate([sin, sin], axis=1))


def _local_step(x, loss_target, w):
    bsz, seq, _ = x.shape
    lp = X0 + seq
    tp = bsz * lp
    assert lp % TOK == 0 and (lp // GLA_CHUNK) % _gla_group(lp // GLA_CHUNK) == 0
    head = jnp.concatenate([jnp.zeros((FRONT, D_MODEL), F32), w["meta_tokens"]], axis=0)
    cos_t, sin_t = _rope_tables(lp)

    w_ext = _w_in_ext(w["w_in"])
    hp, u, proj, packed_all = _proj_in(x, head, w["norm_g"], w_ext, w["packed"])
    packed_all, off = packed_all.reshape(N_DEV, -1), 0
    for n, shape, axis in PACKED:
        size = shape[0] * shape[1]
        w[n] = _join8(packed_all[:, off:off + size].reshape((N_DEV,) + shape), axis)
        off += size
    gw_pad = jnp.pad(w["gla_gate_w"], ((0, LANE - GLA_RANK), (0, 0)))
    uq = w["mla_w_uq"].reshape(MLA_QR, MLA_HEADS, MLA_QK)
    rope_w = uq[:, :, MLA_NOPE:]
    hw = MLA_HEADS * LANE
    wn = uq[:, :, :MLA_NOPE].reshape(MLA_QR, hw)
    wr = _pad_lanes(rope_w).reshape(MLA_QR, hw)
    wt = _pad_lanes(_rot_cols(rope_w)).reshape(MLA_QR, hw)
    ukv = w["mla_w_ukv"].reshape(MLA_KVR, MLA_HEADS, MLA_NOPE + MLA_DV)
    wk = ukv[:, :, :MLA_NOPE].reshape(MLA_KVR, hw)
    wv = ukv[:, :, MLA_NOPE:].reshape(MLA_KVR, hw)

    o_raw, ya_in, s_all = _gla_fwd(proj, gw_pad, w["gla_gate_b"], w["gla_norm_g"], bsz, lp)
    qf = _q_up(proj, w["mla_q_norm_g"], wn, wr, wt, cos_t, sin_t, bsz, lp)
    kf, vf = _kv_up(proj, w["mla_kv_norm_g"], wk, wv, cos_t, sin_t, bsz, lp)
    o_b, yb_in, lse = _attn_fwd(qf, kf, vf, proj, bsz, lp)
    y_a, y_b, dh2, loss, d_final_g = _mid_fwd(ya_in, yb_in, proj, hp, loss_target, w["gla_proj"], w["mla_proj"],
                                              w["w_out"], w["final_norm_g"], bsz, lp)
    d_ya, d_o, dproj, delta, d_w_out, d_gla_proj, d_mla_proj = _mid_bwd(
        dh2, y_a, y_b, proj, ya_in, yb_in, o_b, w["w_out"], w["gla_proj"], w["mla_proj"], bsz, lp)
    dproj, d_gate, d_gla_norm = _gla_bwd(proj, gw_pad, w["gla_gate_b"], w["gla_norm_g"], o_raw, s_all, d_ya, dproj,
                                         bsz, lp)
    d_lr, d_gw_pad, d_gate_b = _gate_bwd(d_gate, proj, gw_pad)
    dqf, dkf, dvf = _attn_bwd(qf, kf, vf, d_o, lse, delta, bsz, lp)
    dproj, d_wn, d_wr, d_wt, d_qn = _q_up_bwd(dqf, proj, w["mla_q_norm_g"], wn, wr, wt, cos_t, sin_t, dproj,
                                              bsz, lp)
    dproj, d_wk, d_wv, d_kvn = _kv_up_bwd(dkf, dvf, proj, w["mla_kv_norm_g"], wk, wv, cos_t, sin_t, d_lr, dproj,
                                          bsz, lp)

    d_rope = (d_wr.reshape(MLA_QR, MLA_HEADS, LANE)[:, :, :MLA_ROPE]
              + _unrot_cols(d_wt.reshape(MLA_QR, MLA_HEADS, LANE)[:, :, :MLA_ROPE]))
    d_uq = jnp.concatenate([d_wn.reshape(MLA_QR, MLA_HEADS, LANE), d_rope], axis=-1).reshape(MLA_QR, MLA_HEADS * MLA_QK)
    d_ukv = jnp.concatenate([d_wk.reshape(MLA_KVR, MLA_HEADS, LANE), d_wv.reshape(MLA_KVR, MLA_HEADS, LANE)],
                            axis=-1).reshape(MLA_KVR, MLA_HEADS * (MLA_NOPE + MLA_DV))
    mats = dict(gla_gate_w=d_gw_pad[:GLA_RANK], gla_proj=d_gla_proj, mla_w_uq=d_uq, mla_w_ukv=d_ukv,
                mla_proj=d_mla_proj, w_out=d_w_out)
    packed = _pad_rows(jnp.concatenate([_split8(mats[n], axis).reshape(N_DEV, -1) for n, _, axis in PACKED], axis=1),
                       PACK_ROWS)
    d_w_ext_t, packed_parts = _dw_in(u, dproj, _bf(packed))
    w_in_slabs = _bf(_w_in_grad_t(d_w_ext_t).reshape(N_DEV, W_IN_SHARD, D_MODEL))
    d_hp, d_norm_g, w_in_parts = _dx_in(dproj, w_ext, hp, dh2, w["norm_g"], w_in_slabs)
    d_hp3 = d_hp.reshape(bsz, lp, D_MODEL)
    small = dict(meta_tokens=_meta_grad(d_hp3), norm_g=d_norm_g, gla_gate_b=d_gate_b, gla_norm_g=d_gla_norm,
                 mla_q_norm_g=d_qn, mla_kv_norm_g=d_kvn, final_norm_g=d_final_g)
    return loss, d_hp3[:, X0:, :], w_in_parts, packed_parts, small


PACKED = (("gla_gate_w", (GLA_RANK, GLA_KW // N_DEV), 1),
          ("gla_proj", (D_MODEL // N_DEV, D_MODEL), 0), ("mla_w_uq", (MLA_QR, MLA_HEADS * MLA_QK // N_DEV), 1),
          ("mla_w_ukv", (MLA_KVR, MLA_HEADS * (MLA_NOPE + MLA_DV) // N_DEV), 1),
          ("mla_proj", (D_MODEL // N_DEV, D_MODEL), 0), ("w_out", (D_MODEL // N_DEV, D_MODEL), 0))
REPLICATED = (("norm_g", D_MODEL), ("gla_gate_b", GLA_KW), ("gla_norm_g", GLA_DV), ("mla_q_norm_g", MLA_QR),
              ("mla_kv_norm_g", MLA_KVR), ("final_norm_g", D_MODEL))
PACK_ROWS = 3744
PACK_BLOCK = 1248
SMALL_ROWS = 48
LOSS_ROW = N_META + 25
W_IN_BLOCK = 128


def _all_gather(shards):
    n_arr = len(shards)

    def body(*refs):
        x_refs, out_refs = refs[:n_arr], refs[n_arr:2 * n_arr]
        send_sems, recv_sems, local_sems = refs[2 * n_arr:]
        x, y, c = _my_place()
        me, sibling = (x, y, c), (x, y, 1 - c)
        chips = [(1 - x, y), (x, 1 - y), (1 - x, 1 - y)]

        def copy(a, k, block, to, from_input=False):
            slab = out_refs[a].at[4 * block[0] + 2 * block[1] + block[2]]
            return pltpu.make_async_remote_copy(
                src_ref=x_refs[a] if from_input else slab, dst_ref=slab,
                send_sem=send_sems.at[7 * a + k], recv_sem=recv_sems.at[7 * a + k], device_id=to,
                device_id_type=MESH_ID)

        arrays = range(n_arr)
        mine = [pltpu.make_async_copy(x_refs[a], out_refs[a].at[4 * x + 2 * y + c], local_sems.at[a]) for a in arrays]
        for cp in mine:
            cp.start()
        first = [copy(a, 0, me, sibling, True) for a in arrays]
        first += [copy(a, 1 + j, me, (*chip, c), True) for j, chip in enumerate(chips) for a in arrays]
        for cp in first:
            cp.start()
        passed = []
        for j, chip in enumerate(chips):
            for a in arrays:
                copy(a, 1 + j, (*chip, c), me).wait_recv()
                passed.append(copy(a, 4 + j, (*chip, c), sibling))
                passed[-1].start()
        for a in arrays:
            copy(a, 0, sibling, me).wait_recv()
        for j, chip in enumerate(chips):
            for a in arrays:
                copy(a, 4 + j, (*chip, 1 - c), me).wait_recv()
        for cp in first + passed:
            cp.wait_send()
        for cp in mine:
            cp.wait()

    anyspec = pl.BlockSpec(memory_space=pl.ANY)
    return pl.pallas_call(
        body, name="weights_all_gather",
        out_shape=[jax.ShapeDtypeStruct((N_DEV,) + s.shape, s.dtype) for s in shards],
        in_specs=[anyspec] * n_arr, out_specs=[anyspec] * n_arr,
        scratch_shapes=[pltpu.SemaphoreType.DMA((7 * n_arr,)), pltpu.SemaphoreType.DMA((7 * n_arr,)),
                        pltpu.SemaphoreType.DMA((n_arr,))],
    )(*shards)


def _small_exchange(slabs):
    def body(g_ref, recv_ref, send_sems, recv_sems, local_sem):
        _exchange(g_ref, recv_ref, send_sems, recv_sems, local_sem, True)
        _exchange(g_ref, recv_ref, send_sems, recv_sems, local_sem, False)

    vmem = pl.BlockSpec(memory_space=pltpu.VMEM)
    return pl.pallas_call(
        body, name="small_exchange", out_shape=jax.ShapeDtypeStruct(slabs.shape, slabs.dtype),
        in_specs=[vmem], out_specs=vmem, scratch_shapes=EXCHANGE_SEMS,
    )(slabs)


def _adamw(parts, w, m, v, block_rows, name):
    rows, cols = w.shape

    def body(p_ref, w_ref, m_ref, v_ref, g_out, d_out, m_out, v_out):
        g = p_ref[0].astype(F32)
        for s in range(1, N_DEV):
            g = g + p_ref[s].astype(F32)
        m_new = ADAM_B1 * m_ref[...] + (1.0 - ADAM_B1) * g
        v_new = ADAM_B2 * v_ref[...] + (1.0 - ADAM_B2) * (g * g)
        m_hat = m_new / (1.0 - ADAM_B1 ** ADAM_STEP)
        v_hat = v_new / (1.0 - ADAM_B2 ** ADAM_STEP)
        g_out[...] = g
        d_out[...] = -ADAM_LR * (m_hat / (jnp.sqrt(v_hat) + ADAM_EPS) + ADAM_WD * w_ref[...])
        m_out[...] = m_new
        v_out[...] = v_new

    spec = pl.BlockSpec((block_rows, cols), lambda i: (i, 0))
    return pl.pallas_call(
        body, name=name, grid=(pl.cdiv(rows, block_rows),),
        in_specs=[pl.BlockSpec((N_DEV, block_rows, cols), lambda i: (0, i, 0)), spec, spec, spec],
        out_specs=[spec] * 4, out_shape=[jax.ShapeDtypeStruct((rows, cols), F32)] * 4,
        compiler_params=_cp(("parallel",), 48),
    )(parts, w, m, v)


def _pad_rows(flat, rows):
    pad = rows * LANE - flat.shape[-1]
    flat = jnp.pad(flat, [(0, 0)] * (flat.ndim - 1) + [(0, pad)])
    return flat.reshape(flat.shape[:-1] + (rows, LANE))


def _pack_shards(shards):
    return _pad_rows(jnp.concatenate([shards[n].reshape(-1) for n, _, _ in PACKED]), PACK_ROWS)


def _unpack_shards(packed):
    flat, out, off = packed.reshape(-1), {}, 0
    for n, shape, _ in PACKED:
        size = shape[0] * shape[1]
        out[n] = flat[off:off + size].reshape(shape)
        off += size
    return out


def _split8(full, axis):
    r, c = full.shape
    if axis == 0:
        return full.reshape(N_DEV, r // N_DEV, c)
    return full.reshape(r, N_DEV, c // N_DEV).transpose(1, 0, 2)


def _join8(shards, axis):
    _, r, c = shards.shape
    if axis == 0:
        return shards.reshape(N_DEV * r, c)
    return shards.transpose(1, 0, 2).reshape(r, N_DEV * c)


def _pack_small(meta_shard, vals, loss_row):
    rows = jnp.concatenate([vals[n].reshape(-1, LANE) for n, _ in REPLICATED] + [loss_row], axis=0)
    rows = jnp.pad(rows, ((0, SMALL_ROWS - N_META - rows.shape[0]), (0, 0)))
    return jnp.concatenate([meta_shard, jnp.broadcast_to(rows, meta_shard.shape[:-2] + rows.shape)], axis=-2)


def _unpack_small(packed):
    out, off = {"meta_tokens": packed[:N_META]}, N_META
    for n, size in REPLICATED:
        out[n] = packed[off:off + size // LANE].reshape(1, size)
        off += size // LANE
    return out


def kernel(x, meta_tokens, norm_g, w_in, gla_gate_w, gla_gate_b, gla_norm_g, gla_proj, mla_q_norm_g, mla_w_uq, mla_kv_norm_g, mla_w_ukv, mla_proj, w_out, final_norm_g, loss_target, m_meta_tokens, m_norm_g, m_w_in, m_gla_gate_w, m_gla_gate_b, m_gla_norm_g, m_gla_proj, m_mla_q_norm_g, m_mla_w_uq, m_mla_kv_norm_g, m_mla_w_ukv, m_mla_proj, m_w_out, m_final_norm_g, v_meta_tokens, v_norm_g, v_w_in, v_gla_gate_w, v_gla_gate_b, v_gla_norm_g, v_gla_proj, v_mla_q_norm_g, v_mla_w_uq, v_mla_kv_norm_g, v_mla_w_ukv, v_mla_proj, v_w_out, v_final_norm_g):
    given = dict(meta_tokens=meta_tokens, norm_g=norm_g, w_in=w_in, gla_gate_w=gla_gate_w, gla_gate_b=gla_gate_b,
                 gla_norm_g=gla_norm_g, gla_proj=gla_proj, mla_q_norm_g=mla_q_norm_g, mla_w_uq=mla_w_uq,
                 mla_kv_norm_g=mla_kv_norm_g, mla_w_ukv=mla_w_ukv, mla_proj=mla_proj, w_out=w_out,
                 final_norm_g=final_norm_g)
    mom_m = dict(meta_tokens=m_meta_tokens, norm_g=m_norm_g, w_in=m_w_in, gla_gate_w=m_gla_gate_w,
                 gla_gate_b=m_gla_gate_b, gla_norm_g=m_gla_norm_g, gla_proj=m_gla_proj, mla_q_norm_g=m_mla_q_norm_g,
                 mla_w_uq=m_mla_w_uq, mla_kv_norm_g=m_mla_kv_norm_g, mla_w_ukv=m_mla_w_ukv, mla_proj=m_mla_proj,
                 w_out=m_w_out, final_norm_g=m_final_norm_g)
    mom_v = dict(meta_tokens=v_meta_tokens, norm_g=v_norm_g, w_in=v_w_in, gla_gate_w=v_gla_gate_w,
                 gla_gate_b=v_gla_gate_b, gla_norm_g=v_gla_norm_g, gla_proj=v_gla_proj, mla_q_norm_g=v_mla_q_norm_g,
                 mla_w_uq=v_mla_w_uq, mla_kv_norm_g=v_mla_kv_norm_g, mla_w_ukv=v_mla_w_ukv, mla_proj=v_mla_proj,
                 w_out=v_w_out, final_norm_g=v_final_norm_g)
    shapes = {n: a.shape for n, a in given.items()}
    shard2d = {n: s for n, s, _ in PACKED}
    shard2d["w_in"] = (D_MODEL, W_IN_SHARD)
    shard2d["meta_tokens"] = (N_META, LANE)

    def as2d(tree):
        out = {n: tree[n].reshape(shard2d[n]) for n in shard2d}
        out.update({n: tree[n].reshape(1, size) for n, size in REPLICATED})
        return out

    w_loc, m_loc, v_loc = as2d(given), as2d(mom_m), as2d(mom_v)

    w_in_all, meta_all = _all_gather([w_loc["w_in"].astype(BF16), w_loc["meta_tokens"]])
    flat = jnp.concatenate([w_loc[n].astype(BF16).reshape(-1) for n, _, _ in PACKED])
    full = {"w_in": w_in_all, "meta_tokens": _join8(meta_all, 1), "packed": _pad_rows(flat, PACK_ROWS)}
    for n, _ in REPLICATED:
        full[n] = w_loc[n]

    loss_part, grad_x, w_in_parts, packed_parts, small = _local_step(x, loss_target, full)
    small_all = _small_exchange(_pack_small(_split8(small["meta_tokens"], 1), small,
                                            jnp.broadcast_to(loss_part[:, :1], (1, LANE))))

    w_in_t = [t["w_in"].T for t in (w_loc, m_loc, v_loc)]
    g_w, d_w, m_w, v_w = (o.T for o in _adamw(w_in_parts, *w_in_t, W_IN_BLOCK, "adamw_w_in"))
    g_p, d_p, m_p, v_p = _adamw(packed_parts, _pack_shards(w_loc), _pack_shards(m_loc), _pack_shards(v_loc),
                                PACK_BLOCK, "adamw_packed")
    zero_row = jnp.zeros((1, LANE), F32)
    g_s, d_s, m_s, v_s = _adamw(small_all, *(_pack_small(t["meta_tokens"], t, zero_row) for t in (w_loc, m_loc, v_loc)),
                                SMALL_ROWS, "adamw_small")
    loss = g_s[LOSS_ROW, 0]

    order = ["meta_tokens", "norm_g", "w_in", "gla_gate_w", "gla_gate_b", "gla_norm_g", "gla_proj", "mla_q_norm_g",
             "mla_w_uq", "mla_kv_norm_g", "mla_w_ukv", "mla_proj", "w_out", "final_norm_g"]
    result = [loss, grad_x]
    for w_in_out, packed_sh, packed_sm in ((g_w, g_p, g_s), (d_w, d_p, d_s), (m_w, m_p, m_s), (v_w, v_p, v_s)):
        tree = _unpack_shards(packed_sh)
        tree.update(_unpack_small(packed_sm))
        tree["w_in"] = w_in_out
        result += [tree[n].reshape(shapes[n]) for n in order]
    return tuple(result)
```

```python
import jax
import jax.numpy as jnp
from jax import lax
from jax.experimental import pallas as pl
from jax.experimental.pallas import tpu as pltpu

F32 = jnp.float32
BF16 = jnp.bfloat16

D_MODEL = 1024
N_META = 16
EPS = 1e-6
FRONT = 48
X0 = FRONT + N_META
GLA_HEADS, GLA_DK, GLA_DV, GLA_RANK, GLA_CHUNK = 4, 128, 256, 16, 64
GLA_GATE_NORMALIZER = 16.0
GLA_KW = GLA_HEADS * GLA_DK
GLA_VW = GLA_HEADS * GLA_DV
MLA_HEADS, MLA_NOPE, MLA_ROPE, MLA_DV, MLA_QR, MLA_KVR = 8, 128, 64, 128, 256, 128
MLA_QK = MLA_NOPE + MLA_ROPE
ROPE_BASE = 10000.0
LANE = 128
QKW = 2 * LANE

C_V, C_Z, C_Q, C_K = 0, 1024, 2048, 2560
C_MZ, C_GG, C_GM = 3072, 4096, 5120
C_CKV, C_KR, C_KROT, C_LR = 6144, 6272, 6400, 6528
C_CQ = 6656
N_EXT = 6912
O_Q, O_K, O_V, O_LR, O_Z, O_CQ, O_CKV, O_KR, O_MZ, O_GG, O_GM, N_IN = (
    0, 512, 1024, 2048, 2064, 3088, 3344, 3472, 3536, 4560, 5584, 6608)

ADAM_LR, ADAM_B1, ADAM_B2, ADAM_EPS, ADAM_WD, ADAM_STEP = 0.001, 0.9, 0.999, 1e-08, 0.01, 10

N_DEV = 8
TOK = 192
ATT_BLOCK = 352
EXT_BLOCK = 1152
MXU_DEPTH = 256


def _cp(sems=None, vmem_mb=None):
    kw = {}
    if sems is not None:
        kw["dimension_semantics"] = sems
    if vmem_mb is not None:
        kw["vmem_limit_bytes"] = vmem_mb * 1024 * 1024
    return pltpu.CompilerParams(**kw)


def _dot(a, b):
    return jnp.dot(a, b, preferred_element_type=F32)


def _dot_nt(a, b):
    return lax.dot_general(a, b, (((1,), (1,)), ((), ())), preferred_element_type=F32)


def _dot_tn(a, b):
    return lax.dot_general(a, b, (((0,), (0,)), ((), ())), preferred_element_type=F32)


def _sigmoid(x):
    return 1.0 / (1.0 + jnp.exp(-x))


def _bf(x):
    return x.astype(BF16)


def _big_tok(tp):
    return 4 * TOK if tp % (4 * TOK) == 0 else TOK


def _attn_block(lp):
    return ATT_BLOCK if lp % ATT_BLOCK == 0 else TOK


def _wide_block(lp):
    return 2 * ATT_BLOCK if lp % (2 * ATT_BLOCK) == 0 else _attn_block(lp)


def _proj_in(x, head, norm_g, w_ext, packed):
    bsz, seq, _ = x.shape
    lp = X0 + seq
    tp = bsz * lp
    tm = _attn_block(lp)
    nb = lp // tm
    last = pl.cdiv(seq, tm) - 1

    def body(xa_ref, xb_ref, hd_ref, g_ref, w_ref, p_ref, h_ref, u_ref, o_ref, pall_ref, send_sems, recv_sems, local_sem):
        first = jnp.logical_and(pl.program_id(0) == 0, pl.program_id(1) == 0)

        @pl.when(first)
        def _():
            _exchange(p_ref, pall_ref, send_sems, recv_sems, local_sem, True, same=True)

        front = jnp.where(pl.program_id(1) == 0, hd_ref[...], xa_ref[0, tm - X0:, :])
        h = jnp.concatenate([front, xb_ref[0, :tm - X0, :]], axis=0)
        h_ref[...] = h
        r = lax.rsqrt(jnp.mean(h * h, axis=-1, keepdims=True) + EPS)
        u = _bf(h * r * g_ref[...])
        u_ref[...] = u
        o_ref[...] = _bf(_dot(u, w_ref[...]))

        @pl.when(jnp.logical_and(pl.program_id(0) == bsz - 1, pl.program_id(1) == nb - 1))
        def _():
            _exchange(p_ref, pall_ref, send_sems, recv_sems, local_sem, False, same=True)

    anyspec = pl.BlockSpec(memory_space=pl.ANY)
    tok = lambda width: pl.BlockSpec((tm, width), lambda b, i: (b * nb + i, 0))
    return pl.pallas_call(
        body, name="proj_in", grid=(bsz, nb),
        in_specs=[pl.BlockSpec((1, tm, D_MODEL), lambda b, i: (b, jnp.maximum(i - 1, 0), 0)),
                  pl.BlockSpec((1, tm, D_MODEL), lambda b, i: (b, jnp.minimum(i, last), 0)),
                  pl.BlockSpec((X0, D_MODEL), lambda b, i: (0, 0)),
                  pl.BlockSpec((1, D_MODEL), lambda b, i: (0, 0)),
                  pl.BlockSpec((D_MODEL, N_EXT), lambda b, i: (0, 0), pipeline_mode=pl.Buffered(1)), anyspec],
        out_specs=[tok(D_MODEL), tok(D_MODEL), tok(N_EXT), anyspec],
        out_shape=[jax.ShapeDtypeStruct((tp, D_MODEL), F32), jax.ShapeDtypeStruct((tp, D_MODEL), BF16),
                   jax.ShapeDtypeStruct((tp, N_EXT), BF16),
                   jax.ShapeDtypeStruct((N_DEV,) + packed.shape, packed.dtype)],
        scratch_shapes=EXCHANGE_SEMS,
        compiler_params=_cp(("arbitrary", "arbitrary"), 56),
    )(x, x, head, norm_g, w_ext, packed)


def _gla_group(n_chunks):
    return 11 if n_chunks % 11 == 0 else 3


def _tri_dot(tri, x):
    hi = _bf(x)
    rest = x - hi.astype(F32)
    mid = _bf(rest)
    return _dot(tri, hi) + _dot(tri, mid) + _dot(tri, _bf(rest - mid.astype(F32)))


def _gla_gates(q_ref, k_ref, lr_ref, gw_ref, gb_ref, rows, not_first):
    z = _dot(lr_ref[rows, :], gw_ref[...]) + gb_ref[...]
    logsig = jnp.minimum(z, 0.0) - jnp.log(1.0 + jnp.exp(-jnp.abs(z)))
    row = lax.broadcasted_iota(jnp.int32, (GLA_CHUNK, GLA_KW), 0)
    live = jnp.logical_or(not_first, row >= FRONT)
    g = jnp.where(live, logsig * (1.0 / GLA_GATE_NORMALIZER), 0.0)
    ri = lax.broadcasted_iota(jnp.int32, (GLA_CHUNK, GLA_CHUNK), 0)
    ci = lax.broadcasted_iota(jnp.int32, (GLA_CHUNK, GLA_CHUNK), 1)
    tril = ci <= ri
    b = _tri_dot(_bf(tril.astype(F32)), g)
    bl = jnp.sum(jnp.where(row == GLA_CHUNK - 1, b, 0.0), axis=0, keepdims=True)
    eb, enb, elb, ebl = jnp.exp(b), jnp.exp(-b), jnp.exp(bl - b), jnp.exp(bl)
    q = q_ref[rows, :].astype(F32) * (GLA_DK ** -0.5)
    k = k_ref[rows, :].astype(F32)
    qe, ke, kl = q * eb, k * enb, k * elb
    return dict(z=z, live=live, tril=tril, row=row, eb=eb, enb=enb, elb=elb, ebl=ebl, qe=qe, ke=ke, kl=kl,
                qe_b=_bf(qe), ke_b=_bf(ke), kl_b=_bf(kl))


def _gla_in_specs(n_groups, gla_rows, rev):
    def rb(b, n):
        return b * n_groups + ((n_groups - 1 - n) if rev else n)

    return rb, [pl.BlockSpec((gla_rows, GLA_KW), lambda b, n: (rb(b, n), C_Q // GLA_KW)),
                pl.BlockSpec((gla_rows, GLA_KW), lambda b, n: (rb(b, n), C_K // GLA_KW)),
                pl.BlockSpec((gla_rows, GLA_VW), lambda b, n: (rb(b, n), C_V // GLA_VW)),
                pl.BlockSpec((gla_rows, GLA_VW), lambda b, n: (rb(b, n), C_Z // GLA_VW)),
                pl.BlockSpec((gla_rows, LANE), lambda b, n: (rb(b, n), C_LR // LANE)),
                pl.BlockSpec((LANE, GLA_KW), lambda b, n: (0, 0)),
                pl.BlockSpec((1, GLA_KW), lambda b, n: (0, 0)),
                pl.BlockSpec((1, GLA_DV), lambda b, n: (0, 0))]


def _gla_fwd(proj, gw_pad, gate_b, gla_norm_g, bsz, lp):
    n_chunks = lp // GLA_CHUNK
    gla_group = _gla_group(n_chunks)
    gla_rows = gla_group * GLA_CHUNK
    n_groups = n_chunks // gla_group
    tp = bsz * lp

    def body(q_ref, k_ref, v_ref, z_ref, lr_ref, gw_ref, gb_ref, gn_ref, oraw_ref, ya_ref, sall_ref, st_scr):
        grp = pl.program_id(1)

        @pl.when(grp == 0)
        def _():
            st_scr[...] = jnp.zeros_like(st_scr)

        chunks = [slice(j * GLA_CHUNK, (j + 1) * GLA_CHUNK) for j in range(gla_group)]
        cs = [_gla_gates(q_ref, k_ref, lr_ref, gw_ref, gb_ref, rows, True if j else grp > 0)
              for j, rows in enumerate(chunks)]
        gn = gn_ref[...]
        sts = [st_scr[h] for h in range(GLA_HEADS)]
        for j, (rows, c) in enumerate(zip(chunks, cs)):
            for h in range(GLA_HEADS):
                ks, vs = slice(h * GLA_DK, (h + 1) * GLA_DK), slice(h * GLA_DV, (h + 1) * GLA_DV)
                st = sts[h]
                sall_ref[0, j, h] = st
                v = v_ref[rows, vs]
                a = jnp.where(c["tril"], _dot_nt(c["qe_b"][:, ks], c["ke_b"][:, ks]), 0.0)
                o = _dot(_bf(a), v) + _dot_nt(c["qe_b"][:, ks], _bf(st))
                sts[h] = st * c["ebl"][:, ks] + _dot_tn(v, c["kl_b"][:, ks])
                oraw_ref[rows, vs] = o
                r = lax.rsqrt(jnp.mean(o * o, axis=-1, keepdims=True) + EPS)
                zg = z_ref[rows, vs].astype(F32)
                ya_ref[rows, vs] = _bf((o * r * gn) * (zg * _sigmoid(zg)))
        for h in range(GLA_HEADS):
            st_scr[h] = sts[h]

    rb, in_specs = _gla_in_specs(n_groups, gla_rows, False)
    return pl.pallas_call(
        body, name="gla_fwd", grid=(bsz, n_groups), in_specs=in_specs,
        out_specs=[pl.BlockSpec((gla_rows, GLA_VW), lambda b, n: (rb(b, n), 0)),
                   pl.BlockSpec((gla_rows, GLA_VW), lambda b, n: (rb(b, n), 0)),
                   pl.BlockSpec((1, gla_group, GLA_HEADS, GLA_DV, GLA_DK), lambda b, n: (b, n, 0, 0, 0))],
        out_shape=[jax.ShapeDtypeStruct((tp, GLA_VW), F32), jax.ShapeDtypeStruct((tp, GLA_VW), BF16),
                   jax.ShapeDtypeStruct((bsz, n_chunks, GLA_HEADS, GLA_DV, GLA_DK), F32)],
        scratch_shapes=[pltpu.VMEM((GLA_HEADS, GLA_DV, GLA_DK), F32)],
        compiler_params=_cp(("parallel", "arbitrary"), 56),
    )(proj, proj, proj, proj, proj, gw_pad, gate_b, gla_norm_g)


def _gla_bwd(proj, gw_pad, gate_b, gla_norm_g, o_raw, s_all, d_ya, dproj, bsz, lp):
    n_chunks = lp // GLA_CHUNK
    gla_group = _gla_group(n_chunks)
    gla_rows = gla_group * GLA_CHUNK
    n_groups = n_chunks // gla_group
    tp = bsz * lp

    def body(q_ref, k_ref, v_ref, z_ref, lr_ref, gw_ref, gb_ref, gn_ref, o_ref, s_ref, dya_ref, _,
             dp_ref, dz_ref, dgn_ref, dst_scr):
        dv_ref, dzg_ref = dp_ref.at[:, C_V:C_V + GLA_VW], dp_ref.at[:, C_Z:C_Z + GLA_VW]

        @pl.when(jnp.logical_and(pl.program_id(0) == 0, pl.program_id(1) == 0))
        def _():
            dgn_ref[...] = jnp.zeros_like(dgn_ref)

        @pl.when(pl.program_id(1) == 0)
        def _():
            dst_scr[...] = jnp.zeros_like(dst_scr)

        grp = n_groups - 1 - pl.program_id(1)
        chunks = [slice(j * GLA_CHUNK, (j + 1) * GLA_CHUNK) for j in range(gla_group)]
        cs = [_gla_gates(q_ref, k_ref, lr_ref, gw_ref, gb_ref, rows, True if j else grp > 0)
              for j, rows in enumerate(chunks)]
        gn = gn_ref[...]
        dgn = jnp.zeros((1, GLA_DV), F32)
        dqe_h, dke_h, dkl_h, dbl_h = ([[None] * GLA_HEADS for _ in chunks] for _ in range(4))
        dsts = [dst_scr[h] for h in range(GLA_HEADS)]
        for j in reversed(range(gla_group)):
            rows, c = chunks[j], cs[j]
            for h in range(GLA_HEADS):
                ks, vs = slice(h * GLA_DK, (h + 1) * GLA_DK), slice(h * GLA_DV, (h + 1) * GLA_DV)
                dst = dsts[h]
                v = v_ref[rows, vs]
                st = s_ref[0, j, h]
                o = o_ref[rows, vs]
                r = lax.rsqrt(jnp.mean(o * o, axis=-1, keepdims=True) + EPS)
                xh = o * r
                zg = z_ref[rows, vs].astype(F32)
                sg = _sigmoid(zg)
                dy = dya_ref[rows, vs].astype(F32)
                dzg_ref[rows, vs] = _bf(dy * (xh * gn) * (sg * (1.0 + zg * (1.0 - sg))))
                t = dy * (zg * sg)
                dgn += jnp.sum(t * xh, axis=0, keepdims=True)
                dxh = t * gn
                do_b = _bf(r * (dxh - xh * jnp.mean(dxh * xh, axis=-1, keepdims=True)))
                qe_b, ke_b, kl_b, dst_b = c["qe_b"][:, ks], c["ke_b"][:, ks], c["kl_b"][:, ks], _bf(dst)
                a = jnp.where(c["tril"], _dot_nt(qe_b, ke_b), 0.0)
                da_b = _bf(jnp.where(c["tril"], _dot_nt(do_b, v), 0.0))
                dqe_h[j][h] = _dot(da_b, ke_b) + _dot(do_b, _bf(st))
                dke_h[j][h] = _dot_tn(da_b, qe_b)
                dkl = _dot(v, dst_b)
                dkl_h[j][h] = dkl
                dv_ref[rows, vs] = _bf(_dot_tn(_bf(a), do_b) + _dot_nt(kl_b, dst_b))
                ddecay = jnp.sum(dst * st, axis=0, keepdims=True)
                dbl_h[j][h] = jnp.sum(dkl * c["kl"][:, ks], axis=0, keepdims=True) + ddecay * c["ebl"][:, ks]
                dsts[h] = dst * c["ebl"][:, ks] + _dot_tn(do_b, qe_b)
        for h in range(GLA_HEADS):
            dst_scr[h] = dsts[h]
        dgn_ref[...] += dgn
        ri = lax.broadcasted_iota(jnp.int32, (GLA_CHUNK, GLA_CHUNK), 0)
        ci = lax.broadcasted_iota(jnp.int32, (GLA_CHUNK, GLA_CHUNK), 1)
        triu = _bf((ci >= ri).astype(F32))
        for j, (rows, c) in enumerate(zip(chunks, cs)):
            dqe, dke, dkl, dbl = (jnp.concatenate(p[j], axis=1) for p in (dqe_h, dke_h, dkl_h, dbl_h))
            db = dqe * c["qe"] - dke * c["ke"] - dkl * c["kl"] + jnp.where(c["row"] == GLA_CHUNK - 1, dbl, 0.0)
            dg = _tri_dot(triu, db)
            dg = jnp.where(c["live"], dg, 0.0)
            dz_ref[rows, :] = dg * (1.0 / GLA_GATE_NORMALIZER) * _sigmoid(-c["z"])
            dp_ref[rows, C_Q:C_Q + GLA_KW] = _bf(dqe * c["eb"] * (GLA_DK ** -0.5))
            dp_ref[rows, C_K:C_K + GLA_KW] = _bf(dke * c["enb"] + dkl * c["elb"])

    rb, in_specs = _gla_in_specs(n_groups, gla_rows, True)
    wide = pl.BlockSpec((gla_rows, GLA_VW), lambda b, n: (rb(b, n), 0))
    group = C_MZ
    return pl.pallas_call(
        body, name="gla_bwd", grid=(bsz, n_groups),
        in_specs=in_specs + [wide, pl.BlockSpec((1, gla_group, GLA_HEADS, GLA_DV, GLA_DK),
                                                lambda b, n: (b, n_groups - 1 - n, 0, 0, 0)), wide,
                             pl.BlockSpec(memory_space=pl.ANY)],
        out_specs=[pl.BlockSpec((gla_rows, group), lambda b, n: (rb(b, n), 0)),
                   pl.BlockSpec((gla_rows, GLA_KW), lambda b, n: (rb(b, n), 0)),
                   pl.BlockSpec((1, GLA_DV), lambda b, n: (0, 0))],
        out_shape=[jax.ShapeDtypeStruct((tp, N_EXT), BF16), jax.ShapeDtypeStruct((tp, GLA_KW), F32),
                   jax.ShapeDtypeStruct((1, GLA_DV), F32)],
        input_output_aliases={11: 0},
        scratch_shapes=[pltpu.VMEM((GLA_HEADS, GLA_DV, GLA_DK), F32)],
        compiler_params=_cp(("arbitrary", "arbitrary"), 56),
    )(proj, proj, proj, proj, proj, gw_pad, gate_b, gla_norm_g, o_raw, s_all, d_ya, dproj)


def _gate_bwd(dz, proj, gw_pad):
    tp = dz.shape[0]
    tm = _big_tok(tp)

    def body(dz_ref, lr_ref, gw_ref, dlr_ref, dgw_ref, dgb_ref):
        @pl.when(pl.program_id(0) == 0)
        def _():
            dgw_ref[...] = jnp.zeros_like(dgw_ref)
            dgb_ref[...] = jnp.zeros_like(dgb_ref)

        dz = dz_ref[...]
        dz_b = _bf(dz)
        dlr_ref[...] = _bf(_dot_nt(dz_b, gw_ref[...]))
        dgw_ref[...] += _dot_tn(lr_ref[...], dz_b)
        dgb_ref[...] += jnp.sum(dz, axis=0, keepdims=True)

    return pl.pallas_call(
        body, name="gate_bwd", grid=(tp // tm,),
        in_specs=[pl.BlockSpec((tm, GLA_KW), lambda i: (i, 0)),
                  pl.BlockSpec((tm, LANE), lambda i: (i, C_LR // LANE)),
                  pl.BlockSpec((LANE, GLA_KW), lambda i: (0, 0))],
        out_specs=[pl.BlockSpec((tm, LANE), lambda i: (i, 0)),
                   pl.BlockSpec((LANE, GLA_KW), lambda i: (0, 0)),
                   pl.BlockSpec((1, GLA_KW), lambda i: (0, 0))],
        out_shape=[jax.ShapeDtypeStruct((tp, LANE), BF16), jax.ShapeDtypeStruct((LANE, GLA_KW), F32),
                   jax.ShapeDtypeStruct((1, GLA_KW), F32)],
        compiler_params=_cp(("arbitrary",)),
    )(dz, proj, gw_pad)


def _rms_fwd(x):
    r = lax.rsqrt(jnp.mean(x * x, axis=-1, keepdims=True) + EPS)
    return x * r, r


def _rms_bwd(dy, xh, r, g):
    dxh = dy * g
    dx = r * (dxh - xh * jnp.mean(dxh * xh, axis=-1, keepdims=True))
    return dx, jnp.sum(dy * xh, axis=0, keepdims=True)


def _q_up(proj, q_norm_g, wn, wr, wt, cos_t, sin_t, bsz, lp):
    tp = bsz * lp
    tok = _wide_block(lp)
    nb = lp // tok

    def body(cq_ref, g_ref, wn_ref, wr_ref, wt_ref, cos_ref, sin_ref, q_ref):
        xh, _ = _rms_fwd(cq_ref[...].astype(F32))
        cqn = _bf(xh * g_ref[...])
        nope = _dot(cqn, wn_ref[...])
        rope = _dot(cqn, wr_ref[...])
        rot = _dot(cqn, wt_ref[...])
        cos, sin = cos_ref[...], sin_ref[...]
        one = (lax.broadcasted_iota(jnp.int32, (tok, LANE), 1) == BIAS_LANE).astype(F32)
        for h in range(MLA_HEADS):
            sl = slice(h * LANE, (h + 1) * LANE)
            q_ref[:, h * QKW:h * QKW + LANE] = _bf(nope[:, sl])
            q_ref[:, h * QKW + LANE:(h + 1) * QKW] = _bf(rope[:, sl] * cos + rot[:, sl] * sin + one)

    wspec = pl.BlockSpec((MLA_QR, MLA_HEADS * LANE), lambda b, i: (0, 0))
    tspec = pl.BlockSpec((tok, LANE), lambda b, i: (i, 0))
    return pl.pallas_call(
        body, name="mla_q_up", grid=(bsz, nb),
        in_specs=[pl.BlockSpec((tok, MLA_QR), lambda b, i: (b * nb + i, C_CQ // MLA_QR)),
                  pl.BlockSpec((1, MLA_QR), lambda b, i: (0, 0)), wspec, wspec, wspec, tspec, tspec],
        out_specs=pl.BlockSpec((tok, MLA_HEADS * QKW), lambda b, i: (b * nb + i, 0)),
        out_shape=jax.ShapeDtypeStruct((tp, MLA_HEADS * QKW), BF16),
        compiler_params=_cp(("parallel", "parallel")),
    )(proj, q_norm_g, wn, wr, wt, cos_t, sin_t)


def _kv_up(proj, kv_norm_g, wk, wv, cos_t, sin_t, bsz, lp):
    tp = bsz * lp
    tok = _wide_block(lp)
    nb = lp // tok

    def body(ckv_ref, kr_ref, krot_ref, g_ref, wk_ref, wv_ref, cos_ref, sin_ref, k_ref, v_ref):
        xh, _ = _rms_fwd(ckv_ref[...].astype(F32))
        cn = _bf(xh * g_ref[...])
        kn = _dot(cn, wk_ref[...])
        v_ref[...] = _bf(_dot(cn, wv_ref[...]))
        pos = pl.program_id(1) * tok + lax.broadcasted_iota(jnp.int32, (tok, LANE), 0)
        lane = lax.broadcasted_iota(jnp.int32, (tok, LANE), 1)
        bias = jnp.where(jnp.logical_and(lane == BIAS_LANE, pos < FRONT), KEY_BIAS, 0.0)
        kr = _bf(kr_ref[...].astype(F32) * cos_ref[...] + krot_ref[...].astype(F32) * sin_ref[...] + bias)
        for h in range(MLA_HEADS):
            k_ref[:, h * QKW:h * QKW + LANE] = _bf(kn[:, h * LANE:(h + 1) * LANE])
            k_ref[:, h * QKW + LANE:(h + 1) * QKW] = kr

    wspec = pl.BlockSpec((MLA_KVR, MLA_HEADS * LANE), lambda b, i: (0, 0))
    tspec = pl.BlockSpec((tok, LANE), lambda b, i: (i, 0))
    return pl.pallas_call(
        body, name="mla_kv_up", grid=(bsz, nb),
        in_specs=[pl.BlockSpec((tok, LANE), lambda b, i: (b * nb + i, C_CKV // LANE)),
                  pl.BlockSpec((tok, LANE), lambda b, i: (b * nb + i, C_KR // LANE)),
                  pl.BlockSpec((tok, LANE), lambda b, i: (b * nb + i, C_KROT // LANE)),
                  pl.BlockSpec((1, MLA_KVR), lambda b, i: (0, 0)), wspec, wspec, tspec, tspec],
        out_specs=[pl.BlockSpec((tok, MLA_HEADS * QKW), lambda b, i: (b * nb + i, 0)),
                   pl.BlockSpec((tok, MLA_HEADS * LANE), lambda b, i: (b * nb + i, 0))],
        out_shape=[jax.ShapeDtypeStruct((tp, MLA_HEADS * QKW), BF16),
                   jax.ShapeDtypeStruct((tp, MLA_HEADS * LANE), BF16)],
        compiler_params=_cp(("parallel", "parallel")),
    )(proj, proj, proj, kv_norm_g, wk, wv, cos_t, sin_t)


ATT_SCALE = MLA_QK ** -0.5


KEY_BIAS = -1e30
BIAS_LANE = MLA_ROPE
NEG = 2 * KEY_BIAS
LOG2E = 1.4426950408889634
EXP2_SCALE = ATT_SCALE * LOG2E


def _causal_fill(s, r0, fill):
    tq, kmax = s.shape
    a = r0 // LANE * LANE
    mask = (a + lax.broadcasted_iota(jnp.int32, (tq, kmax - a), 1)
            <= r0 + lax.broadcasted_iota(jnp.int32, (tq, kmax - a), 0))
    right = jnp.where(mask, s[:, a:], fill)
    return jnp.concatenate([s[:, :a], right], axis=1) if a else right


def _attn_fwd(qf, kf, vf, proj, bsz, lp):
    tp = bsz * lp
    tq = _attn_block(lp)
    nh = 2

    def body(q_ref, k_ref, v_ref, mz_ref, ob_ref, yb_ref, lse_ref):
        starts = list(range(0, lp, tq))
        for pair in (starts[i:i + 2] for i in range(0, len(starts), 2)):
            work = [(r0, h) for r0 in pair for h in range(nh)]
            ss = [_causal_fill(_dot_nt(q_ref[r0:r0 + tq, h * QKW:(h + 1) * QKW],
                                       k_ref[0:r0 + tq, h * QKW:(h + 1) * QKW]), r0, NEG) for r0, h in work]
            ms = [jnp.max(s, axis=-1, keepdims=True) for s in ss]
            ps = [jnp.exp2((s - m) * EXP2_SCALE) for s, m in zip(ss, ms)]
            ls = [jnp.sum(p, axis=-1, keepdims=True) for p in ps]
            for (r0, h), p, m, l in zip(work, ps, ms, ls):
                rows, cols = slice(r0, r0 + tq), slice(h * MLA_DV, (h + 1) * MLA_DV)
                o = _dot(_bf(p), v_ref[0:r0 + tq, cols]) / l
                ob_ref[rows, cols] = _bf(o)
                mz = mz_ref[rows, cols].astype(F32)
                yb_ref[rows, cols] = _bf(o * (mz * _sigmoid(mz)))
                lse_ref[0, h, rows, :] = jnp.broadcast_to(m * EXP2_SCALE + jnp.log2(l), (tq, LANE))

    head = lambda off: pl.BlockSpec((lp, nh * MLA_DV), lambda b, h: (b, off + h))
    wide = pl.BlockSpec((lp, nh * QKW), lambda b, h: (b, h))
    return pl.pallas_call(
        body, name="mla_attn_fwd", grid=(bsz, MLA_HEADS // nh),
        in_specs=[wide, wide, head(0), head(C_MZ // (nh * MLA_DV))],
        out_specs=[head(0), head(0), pl.BlockSpec((1, nh, lp, LANE), lambda b, h: (b, h, 0, 0))],
        out_shape=[jax.ShapeDtypeStruct((tp, MLA_HEADS * MLA_DV), BF16),
                   jax.ShapeDtypeStruct((tp, MLA_HEADS * MLA_DV), BF16),
                   jax.ShapeDtypeStruct((bsz, MLA_HEADS, lp, LANE), F32)],
        compiler_params=_cp(("parallel", "parallel"), 56),
    )(qf, kf, vf, proj)


def _attn_bwd_blocks(lp):
    return [(0, X0)] + [(r0, min(MXU_DEPTH, lp - r0)) for r0 in range(X0, lp, MXU_DEPTH)]


def _attn_bwd(qf, kf, vf, d_o, lse, delta, bsz, lp):
    tp = bsz * lp

    def body(q_ref, k_ref, v_ref, do_ref, lse_ref, dl_ref, dq_ref, dk_ref, dv_ref, dk_acc, dv_acc):
        dk_acc[...] = jnp.zeros_like(dk_acc)
        dv_acc[...] = jnp.zeros_like(dv_acc)
        for r0, tq in _attn_bwd_blocks(lp):
            rows, kmax = slice(r0, r0 + tq), r0 + tq
            q, do = q_ref[rows, :], do_ref[rows, :]
            k, v = k_ref[0:kmax, :], v_ref[0:kmax, :]
            p = jnp.exp2(_dot_nt(q, k) * EXP2_SCALE - lse_ref[0, 0, rows, :][:, :1])
            p = _causal_fill(p, r0, 0.0)
            ds = _bf(p * (_dot_nt(do, v) - dl_ref[0, rows, :][:, :1]))
            dq_ref[rows, :] = _bf(_dot(ds, k) * ATT_SCALE)
            dk_acc[0:kmax, :] += _dot_tn(ds, q)
            dv_acc[0:kmax, :] += _dot_tn(_bf(p), do)
        dk_ref[...] = _bf(dk_acc[...] * ATT_SCALE)
        dv_ref[...] = _bf(dv_acc[...])

    wide = pl.BlockSpec((lp, QKW), lambda b, h: (b, h))
    narrow = pl.BlockSpec((lp, MLA_DV), lambda b, h: (b, h))
    stat = pl.BlockSpec((1, 1, lp, LANE), lambda b, h: (b, h, 0, 0))
    return pl.pallas_call(
        body, name="mla_attn_bwd", grid=(bsz, MLA_HEADS),
        in_specs=[wide, wide, narrow, narrow, stat, pl.BlockSpec((1, lp, LANE), lambda b, h: (h, b, 0))],
        out_specs=[wide, wide, narrow],
        out_shape=[jax.ShapeDtypeStruct((tp, MLA_HEADS * QKW), BF16), jax.ShapeDtypeStruct((tp, MLA_HEADS * QKW), BF16),
                   jax.ShapeDtypeStruct((tp, MLA_HEADS * MLA_DV), BF16)],
        scratch_shapes=[pltpu.VMEM((lp, QKW), F32), pltpu.VMEM((lp, MLA_DV), F32)],
        compiler_params=_cp(("parallel", "parallel"), 56),
    )(qf, kf, vf, d_o, lse, delta)


def _q_up_bwd(dqf, proj, q_norm_g, wn, wr, wt, cos_t, sin_t, dproj, bsz, lp):
    tp = bsz * lp
    tok = _wide_block(lp)
    nb = lp // tok
    hw = MLA_HEADS * LANE

    def body(dq_ref, cq_ref, g_ref, wn_ref, wr_ref, wt_ref, cos_ref, sin_ref, _,
             dcq_ref, dwn_ref, dwr_ref, dwt_ref, dg_ref):
        @pl.when(jnp.logical_and(pl.program_id(0) == 0, pl.program_id(1) == 0))
        def _():
            for r in (dwn_ref, dwr_ref, dwt_ref, dg_ref):
                r[...] = jnp.zeros_like(r)

        g = g_ref[...]
        xh, r = _rms_fwd(cq_ref[...].astype(F32))
        cqn = _bf(xh * g)
        dn = jnp.concatenate([dq_ref[:, h * QKW:h * QKW + LANE] for h in range(MLA_HEADS)], axis=1)
        dr = jnp.concatenate([dq_ref[:, h * QKW + LANE:(h + 1) * QKW] for h in range(MLA_HEADS)], axis=1).astype(F32)
        dr_c = _bf(dr * jnp.tile(cos_ref[...], (1, MLA_HEADS)))
        dr_s = _bf(dr * jnp.tile(sin_ref[...], (1, MLA_HEADS)))
        dcqn = _dot_nt(dn, wn_ref[...]) + _dot_nt(dr_c, wr_ref[...]) + _dot_nt(dr_s, wt_ref[...])
        dwn_ref[...] += _dot_tn(cqn, dn)
        dwr_ref[...] += _dot_tn(cqn, dr_c)
        dwt_ref[...] += _dot_tn(cqn, dr_s)
        dx, dg = _rms_bwd(dcqn, xh, r, g)
        dcq_ref[...] = _bf(dx)
        dg_ref[...] += dg

    aspec = pl.BlockSpec((MLA_QR, hw), lambda b, i: (0, 0))
    tspec = pl.BlockSpec((tok, LANE), lambda b, i: (i, 0))
    return pl.pallas_call(
        body, name="mla_q_up_bwd", grid=(bsz, nb),
        in_specs=[pl.BlockSpec((tok, MLA_HEADS * QKW), lambda b, i: (b * nb + i, 0)),
                  pl.BlockSpec((tok, MLA_QR), lambda b, i: (b * nb + i, C_CQ // MLA_QR)),
                  pl.BlockSpec((1, MLA_QR), lambda b, i: (0, 0)), aspec, aspec, aspec, tspec, tspec,
                  pl.BlockSpec(memory_space=pl.ANY)],
        out_specs=[pl.BlockSpec((tok, MLA_QR), lambda b, i: (b * nb + i, C_CQ // MLA_QR)), aspec, aspec, aspec,
                   pl.BlockSpec((1, MLA_QR), lambda b, i: (0, 0))],
        out_shape=[jax.ShapeDtypeStruct((tp, N_EXT), BF16)] + [jax.ShapeDtypeStruct((MLA_QR, hw), F32)] * 3
        + [jax.ShapeDtypeStruct((1, MLA_QR), F32)],
        input_output_aliases={8: 0},
        compiler_params=_cp(("arbitrary", "arbitrary")),
    )(dqf, proj, q_norm_g, wn, wr, wt, cos_t, sin_t, dproj)


def _kv_up_bwd(dkf, dvf, proj, kv_norm_g, wk, wv, cos_t, sin_t, d_lr, dproj, bsz, lp):
    tp = bsz * lp
    tok = _wide_block(lp)
    nb = lp // tok
    hw = MLA_HEADS * LANE

    def body(dk_ref, dv_ref, ckv_ref, g_ref, wk_ref, wv_ref, cos_ref, sin_ref, dlr_ref, _,
             dp_ref, dwk_ref, dwv_ref, dg_ref):
        dckv_ref, dkr_ref, dkrot_ref = (dp_ref.at[:, j * LANE:(j + 1) * LANE] for j in range(3))
        dp_ref[:, 3 * LANE:] = dlr_ref[...]
        @pl.when(jnp.logical_and(pl.program_id(0) == 0, pl.program_id(1) == 0))
        def _():
            for r in (dwk_ref, dwv_ref, dg_ref):
                r[...] = jnp.zeros_like(r)

        g = g_ref[...]
        xh, r = _rms_fwd(ckv_ref[...].astype(F32))
        cn = _bf(xh * g)
        dv = dv_ref[...]
        dn = jnp.concatenate([dk_ref[:, h * QKW:h * QKW + LANE] for h in range(MLA_HEADS)], axis=1)
        dcn = _dot_nt(dv, wv_ref[...]) + _dot_nt(dn, wk_ref[...])
        dwv_ref[...] += _dot_tn(cn, dv)
        dwk_ref[...] += _dot_tn(cn, dn)
        drope = jnp.zeros((tok, LANE), F32)
        for h in range(MLA_HEADS):
            drope += dk_ref[:, h * QKW + LANE:(h + 1) * QKW].astype(F32)
        dkr_ref[...] = _bf(drope * cos_ref[...])
        dkrot_ref[...] = _bf(drope * sin_ref[...])
        dx, dg = _rms_bwd(dcn, xh, r, g)
        dckv_ref[...] = _bf(dx)
        dg_ref[...] += dg

    aspec = pl.BlockSpec((MLA_KVR, hw), lambda b, i: (0, 0))
    tspec = pl.BlockSpec((tok, LANE), lambda b, i: (i, 0))
    ospec = pl.BlockSpec((tok, LANE), lambda b, i: (b * nb + i, 0))
    return pl.pallas_call(
        body, name="mla_kv_up_bwd", grid=(bsz, nb),
        in_specs=[pl.BlockSpec((tok, MLA_HEADS * QKW), lambda b, i: (b * nb + i, 0)),
                  pl.BlockSpec((tok, hw), lambda b, i: (b * nb + i, 0)),
                  pl.BlockSpec((tok, LANE), lambda b, i: (b * nb + i, C_CKV // LANE)),
                  pl.BlockSpec((1, MLA_KVR), lambda b, i: (0, 0)), aspec, aspec, tspec, tspec, ospec,
                  pl.BlockSpec(memory_space=pl.ANY)],
        out_specs=[pl.BlockSpec((tok, 4 * LANE), lambda b, i: (b * nb + i, C_CKV // (4 * LANE))), aspec, aspec,
                   pl.BlockSpec((1, MLA_KVR), lambda b, i: (0, 0))],
        out_shape=[jax.ShapeDtypeStruct((tp, N_EXT), BF16)] + [jax.ShapeDtypeStruct((MLA_KVR, hw), F32)] * 2
        + [jax.ShapeDtypeStruct((1, MLA_KVR), F32)],
        input_output_aliases={9: 0},
        compiler_params=_cp(("arbitrary", "arbitrary")),
    )(dkf, dvf, proj, kv_norm_g, wk, wv, cos_t, sin_t, d_lr, dproj)


def _mid_fwd(ya_in, yb_in, proj, hp, target, w_gp, w_mp, w_o, final_g, bsz, lp):
    tp = bsz * lp
    tm = _attn_block(lp)
    nb = lp // tm
    last = pl.cdiv(lp - X0, tm) - 1

    def body(ya_ref, yb_ref, gg_ref, gm_ref, h_ref, ta_ref, tb_ref, wgp_ref, wmp_ref, wo_ref, fg_ref,
             ya_out, yb_out, dh_ref, loss_ref, dfg_ref):
        @pl.when(jnp.logical_and(pl.program_id(0) == 0, pl.program_id(1) == 0))
        def _():
            loss_ref[...] = jnp.zeros_like(loss_ref)
            dfg_ref[...] = jnp.zeros_like(dfg_ref)

        y_a = _dot(ya_ref[...], wgp_ref[...])
        y_b = _dot(yb_ref[...], wmp_ref[...])
        ya_out[...] = _bf(y_a)
        yb_out[...] = _bf(y_b)
        merged = _sigmoid(gg_ref[...].astype(F32)) * y_a + _sigmoid(gm_ref[...].astype(F32)) * y_b
        h2 = h_ref[...] + _dot(_bf(merged), wo_ref[...])
        fg = fg_ref[...]
        xh, r = _rms_fwd(h2)
        pos = pl.program_id(1) * tm + lax.broadcasted_iota(jnp.int32, (tm, 1), 0)
        t = jnp.concatenate([ta_ref[0, tm - X0:, :], tb_ref[0, :tm - X0, :]], axis=0)
        err = jnp.where(pos >= X0, xh * fg - t, 0.0)
        loss_ref[...] += 0.5 * jnp.sum(jnp.mean(err * err, axis=-1, keepdims=True), axis=0, keepdims=True)
        dy = err * (1.0 / D_MODEL)
        dx, dfg = _rms_bwd(dy, xh, r, fg)
        dh_ref[...] = dx
        dfg_ref[...] += dfg

    tok = lambda c: pl.BlockSpec((tm, D_MODEL), lambda b, i: (b * nb + i, c))
    wspec = pl.BlockSpec((D_MODEL, D_MODEL), lambda b, i: (0, 0))
    return pl.pallas_call(
        body, name="mid_fwd", grid=(bsz, nb),
        in_specs=[tok(0), tok(0), tok(C_GG // D_MODEL), tok(C_GM // D_MODEL), tok(0),
                  pl.BlockSpec((1, tm, D_MODEL), lambda b, i: (b, jnp.maximum(i - 1, 0), 0)),
                  pl.BlockSpec((1, tm, D_MODEL), lambda b, i: (b, jnp.minimum(i, last), 0)),
                  wspec, wspec, wspec, pl.BlockSpec((1, D_MODEL), lambda b, i: (0, 0))],
        out_specs=[tok(0), tok(0), tok(0), pl.BlockSpec((1, LANE), lambda b, i: (0, 0)),
                   pl.BlockSpec((1, D_MODEL), lambda b, i: (0, 0))],
        out_shape=[jax.ShapeDtypeStruct((tp, D_MODEL), BF16), jax.ShapeDtypeStruct((tp, D_MODEL), BF16),
                   jax.ShapeDtypeStruct((tp, D_MODEL), F32), jax.ShapeDtypeStruct((1, LANE), F32),
                   jax.ShapeDtypeStruct((1, D_MODEL), F32)],
        compiler_params=_cp(("arbitrary", "arbitrary"), 48),
    )(ya_in, yb_in, proj, proj, hp, target, target, w_gp, w_mp, w_o, final_g)


def _mid_bwd(dh2, y_a, y_b, proj, ya_in, yb_in, o_b, w_o, w_gp, w_mp, bsz, lp):
    tp = bsz * lp
    tm = MXU_DEPTH if tp % MXU_DEPTH == 0 else _attn_block(lp)
    nsteps = tp // tm
    group = 3 * D_MODEL

    def body(dh_ref, ya_ref, yb_ref, mz_ref, gg_ref, gm_ref, yai_ref, ybi_ref, ob_ref, wo_ref, wgp_ref, wmp_ref,
             dyai_ref, do_ref, dp_ref, dl_ref, dwo_ref, dwgp_ref, dwmp_ref, a_o, a_gp, a_mp):
        @pl.when(pl.program_id(0) == 0)
        def _():
            for r in (a_o, a_gp, a_mp):
                r[...] = jnp.zeros_like(r)

        dh = _bf(dh_ref[...])
        dm = _dot_nt(dh, wo_ref[...])
        y_a, y_b = ya_ref[...].astype(F32), yb_ref[...].astype(F32)
        sg, sm = _sigmoid(gg_ref[...].astype(F32)), _sigmoid(gm_ref[...].astype(F32))
        d_ya, d_yb = _bf(sg * dm), _bf(sm * dm)
        dp_ref[:, D_MODEL:2 * D_MODEL] = _bf(dm * y_a * sg * (1.0 - sg))
        dp_ref[:, 2 * D_MODEL:] = _bf(dm * y_b * sm * (1.0 - sm))
        merged = _bf(sg * y_a + sm * y_b)
        dy = _dot_nt(d_yb, wmp_ref[...])
        dyai_ref[...] = _bf(_dot_nt(d_ya, wgp_ref[...]))
        a_o[...] += _dot_tn(merged, dh)
        a_gp[...] += _dot_tn(yai_ref[...], d_ya)
        a_mp[...] += _dot_tn(ybi_ref[...], d_yb)
        mz, o = mz_ref[...].astype(F32), ob_ref[...].astype(F32)
        s = _sigmoid(mz)
        do = _bf(dy * (mz * s))
        do_ref[...] = do
        dp_ref[:, :D_MODEL] = _bf(dy * o * (s * (1.0 + mz * (1.0 - s))))
        prod = do.astype(F32) * o
        for h in range(MLA_HEADS):
            dl = jnp.sum(prod[:, h * MLA_DV:(h + 1) * MLA_DV], axis=-1, keepdims=True)
            dl_ref[h] = jnp.broadcast_to(dl, (tm, LANE))

        @pl.when(pl.program_id(0) == nsteps - 1)
        def _():
            pltpu.sync_copy(a_o, dwo_ref)
            pltpu.sync_copy(a_gp, dwgp_ref)
            pltpu.sync_copy(a_mp, dwmp_ref)

    tok = lambda c: pl.BlockSpec((tm, D_MODEL), lambda i: (i, c))
    wspec = pl.BlockSpec((D_MODEL, D_MODEL), lambda i: (0, 0))
    anyspec = pl.BlockSpec(memory_space=pl.ANY)
    wshape = jax.ShapeDtypeStruct((D_MODEL, D_MODEL), F32)
    return pl.pallas_call(
        body, name="mid_bwd", grid=(nsteps,),
        in_specs=[tok(0), tok(0), tok(0), tok(C_MZ // D_MODEL), tok(C_GG // D_MODEL), tok(C_GM // D_MODEL),
                  tok(0), tok(0), tok(0), wspec, wspec, wspec],
        out_specs=[tok(0), tok(0), pl.BlockSpec((tm, group), lambda i: (i, C_MZ // group)),
                   pl.BlockSpec((MLA_HEADS, tm, LANE), lambda i: (0, i, 0)), anyspec, anyspec, anyspec],
        out_shape=[jax.ShapeDtypeStruct((tp, D_MODEL), BF16)] * 2 + [jax.ShapeDtypeStruct((tp, N_EXT), BF16),
                   jax.ShapeDtypeStruct((MLA_HEADS, tp, LANE), F32)] + [wshape] * 3,
        scratch_shapes=[pltpu.VMEM((D_MODEL, D_MODEL), F32)] * 3,
        compiler_params=_cp(("arbitrary",), 56),
    )(dh2, y_a, y_b, proj, proj, proj, ya_in, yb_in, o_b, w_o, w_gp, w_mp)


MESH_ID = pl.DeviceIdType.MESH
EXCHANGE_SEMS = [pltpu.SemaphoreType.DMA((N_DEV - 1,)), pltpu.SemaphoreType.DMA((N_DEV - 1,)), pltpu.SemaphoreType.DMA]


def _my_place():
    return lax.axis_index("x"), lax.axis_index("y"), lax.axis_index("c")


def _exchange(g_ref, recv_ref, send_sems, recv_sems, local_sem, start, same=False):
    x, y, c = _my_place()
    me = 4 * x + 2 * y + c
    own = pltpu.make_async_copy(g_ref if same else g_ref.at[me], recv_ref.at[me], local_sem)
    sends, lands = [], []
    for d in range(1, N_DEV):
        px = 1 - x if d & 4 else x
        py = 1 - y if d & 2 else y
        pc = 1 - c if d & 1 else c
        peer = 4 * px + 2 * py + pc
        for slot, group in ((me, sends),) if start else ((me, sends), (peer, lands)):
            group.append(pltpu.make_async_remote_copy(
                src_ref=g_ref if same else g_ref.at[peer], dst_ref=recv_ref.at[slot], send_sem=send_sems.at[d - 1],
                recv_sem=recv_sems.at[d - 1], device_id=(px, py, pc), device_id_type=MESH_ID))
    if start:
        own.start()
        for cp in sends:
            cp.start()
    else:
        for cp in lands:
            cp.wait_recv()
        for cp in sends:
            cp.wait_send()
        own.wait()


def _dw_in(u, dproj, slabs):
    tp = u.shape[0]
    tn = 3 * LANE
    nj = N_EXT // tn

    def body(u_ref, d_ref, g_ref, o_ref, recv_ref, send_sems, recv_sems, local_sem):
        j = pl.program_id(0)

        @pl.when(j == 0)
        def _():
            _exchange(g_ref, recv_ref, send_sems, recv_sems, local_sem, True)

        o_ref[...] = _dot_tn(d_ref[...], u_ref[...])

        @pl.when(j == nj - 1)
        def _():
            _exchange(g_ref, recv_ref, send_sems, recv_sems, local_sem, False)

    anyspec = pl.BlockSpec(memory_space=pl.ANY)
    return pl.pallas_call(
        body, name="dw_in", grid=(nj,),
        in_specs=[pl.BlockSpec((tp, D_MODEL), lambda j: (0, 0), pipeline_mode=pl.Buffered(1)),
                  pl.BlockSpec((tp, tn), lambda j: (0, j)), anyspec],
        out_specs=[pl.BlockSpec((tn, D_MODEL), lambda j: (j, 0)), anyspec],
        out_shape=[jax.ShapeDtypeStruct((N_EXT, D_MODEL), F32), jax.ShapeDtypeStruct(slabs.shape, slabs.dtype)],
        scratch_shapes=EXCHANGE_SEMS,
        compiler_params=_cp(("arbitrary",), 56),
    )(u, dproj, slabs)


def _dx_in(dproj, w_ext, hp, dh2, norm_g, slabs):
    tp = hp.shape[0]
    tm = 2 * TOK
    ni = tp // tm

    def body(d_ref, w_ref, h_ref, dh_ref, g_ref, s_ref, o_ref, dg_ref, recv_ref, send_sems, recv_sems, local_sem):
        i = pl.program_id(0)

        @pl.when(i == 0)
        def _():
            _exchange(s_ref, recv_ref, send_sems, recv_sems, local_sem, True)
            dg_ref[...] = jnp.zeros_like(dg_ref)

        du = _dot_nt(d_ref[...], w_ref[...])
        g = g_ref[...]
        xh, r = _rms_fwd(h_ref[...])
        dx, dg = _rms_bwd(du, xh, r, g)
        o_ref[...] = dh_ref[...] + dx
        dg_ref[...] += dg

        @pl.when(i == ni - 1)
        def _():
            _exchange(s_ref, recv_ref, send_sems, recv_sems, local_sem, False)

    tok = pl.BlockSpec((tm, D_MODEL), lambda i: (i, 0))
    anyspec = pl.BlockSpec(memory_space=pl.ANY)
    return pl.pallas_call(
        body, name="dx_in", grid=(ni,),
        in_specs=[pl.BlockSpec((tm, N_EXT), lambda i: (i, 0)),
                  pl.BlockSpec((D_MODEL, N_EXT), lambda i: (0, 0), pipeline_mode=pl.Buffered(1)),
                  tok, tok, pl.BlockSpec((1, D_MODEL), lambda i: (0, 0)), anyspec],
        out_specs=[tok, pl.BlockSpec((1, D_MODEL), lambda i: (0, 0)), anyspec],
        out_shape=[jax.ShapeDtypeStruct((tp, D_MODEL), F32), jax.ShapeDtypeStruct((1, D_MODEL), F32),
                   jax.ShapeDtypeStruct(slabs.shape, slabs.dtype)],
        scratch_shapes=EXCHANGE_SEMS,
        compiler_params=_cp(("arbitrary",), 56),
    )(dproj, w_ext, hp, dh2, norm_g, slabs)


def _meta_grad(dhp3):
    bsz = dhp3.shape[0]

    def body(d_ref, o_ref):
        @pl.when(pl.program_id(0) == 0)
        def _():
            o_ref[...] = jnp.zeros_like(o_ref)

        o_ref[...] += d_ref[0]

    return pl.pallas_call(
        body, name="meta_grad", grid=(bsz,),
        in_specs=[pl.BlockSpec((1, N_META, D_MODEL), lambda b: (b, FRONT // N_META, 0))],
        out_specs=pl.BlockSpec((N_META, D_MODEL), lambda b: (0, 0)),
        out_shape=jax.ShapeDtypeStruct((N_META, D_MODEL), F32),
        compiler_params=_cp(("arbitrary",)),
    )(dhp3)


W_IN_SHARD = N_IN // N_DEV


def _pad_lanes(a, width=LANE):
    return jnp.pad(a, [(0, 0)] * (a.ndim - 1) + [(0, width - a.shape[-1])])


def _rot_cols(w):
    half = w.shape[-1] // 2
    return jnp.concatenate([-w[..., half:], w[..., :half]], axis=-1)


def _unrot_cols(dw):
    half = dw.shape[-1] // 2
    return jnp.concatenate([dw[..., half:], -dw[..., :half]], axis=-1)


def _w_in_cols(shards, lo, hi):
    parts = []
    for k in range(lo // W_IN_SHARD, (hi - 1) // W_IN_SHARD + 1):
        a, b = max(lo, k * W_IN_SHARD), min(hi, (k + 1) * W_IN_SHARD)
        parts.append(shards[k][:, a - k * W_IN_SHARD:b - k * W_IN_SHARD])
    return parts[0] if len(parts) == 1 else jnp.concatenate(parts, axis=1)


def _w_in_ext(shards):
    c = lambda lo, hi: _w_in_cols(shards, lo, hi)
    kr = c(O_KR, O_MZ)
    return jnp.concatenate([
        c(O_V, O_LR), c(O_Z, O_CQ), c(O_Q, O_K), c(O_K, O_V), c(O_MZ, O_GG), c(O_GG, O_GM), c(O_GM, N_IN),
        c(O_CKV, O_KR), _pad_lanes(kr), _pad_lanes(_rot_cols(kr)), _pad_lanes(c(O_LR, O_Z)), c(O_CQ, O_CKV)], axis=1)


def _w_in_grad_t(dwt):
    g = lambda start, width: dwt[start:start + width]
    half = MLA_ROPE // 2
    krot = g(C_KROT, MLA_ROPE)
    kr = g(C_KR, MLA_ROPE) + jnp.concatenate([krot[half:], -krot[:half]], axis=0)
    return jnp.concatenate([
        g(C_Q, GLA_KW), g(C_K, GLA_KW), g(C_V, GLA_VW), g(C_LR, GLA_RANK), g(C_Z, GLA_VW), g(C_CQ, MLA_QR),
        g(C_CKV, MLA_KVR), kr, g(C_MZ, D_MODEL), g(C_GG, D_MODEL), g(C_GM, D_MODEL)], axis=0)


def _rope_tables(lp):
    inv = 1.0 / (ROPE_BASE ** (jnp.arange(0, MLA_ROPE, 2, dtype=F32) / MLA_ROPE))
    ang = (jnp.arange(lp, dtype=F32) - FRONT)[:, None] * inv[None, :]
    cos, sin = jnp.cos(ang), jnp.sin(ang)
    return _pad_lanes(jnp.concatenate([cos, cos], axis=1)), _pad_lanes(jnp.concatenate([sin, sin], axis=1))


def _local_step(x, loss_target, w):
    bsz, seq, _ = x.shape
    lp = X0 + seq
    tp = bsz * lp
    assert lp % TOK == 0 and (lp // GLA_CHUNK) % _gla_group(lp // GLA_CHUNK) == 0
    head = jnp.concatenate([jnp.zeros((FRONT, D_MODEL), F32), w["meta_tokens"]], axis=0)
    cos_t, sin_t = _rope_tables(lp)

    w_ext = _w_in_ext(w["w_in"])
    hp, u, proj, packed_all = _proj_in(x, head, w["norm_g"], w_ext, w["packed"])
    packed_all, off = packed_all.reshape(N_DEV, -1), 0
    for n, shape, axis in PACKED:
        size = shape[0] * shape[1]
        w[n] = _join8(packed_all[:, off:off + size].reshape((N_DEV,) + shape), axis)
        off += size
    gw_pad = jnp.pad(w["gla_gate_w"], ((0, LANE - GLA_RANK), (0, 0)))
    uq = w["mla_w_uq"].reshape(MLA_QR, MLA_HEADS, MLA_QK)
    rope_w = uq[:, :, MLA_NOPE:]
    hw = MLA_HEADS * LANE
    wn = uq[:, :, :MLA_NOPE].reshape(MLA_QR, hw)
    wr = _pad_lanes(rope_w).reshape(MLA_QR, hw)
    wt = _pad_lanes(_rot_cols(rope_w)).reshape(MLA_QR, hw)
    ukv = w["mla_w_ukv"].reshape(MLA_KVR, MLA_HEADS, MLA_NOPE + MLA_DV)
    wk = ukv[:, :, :MLA_NOPE].reshape(MLA_KVR, hw)
    wv = ukv[:, :, MLA_NOPE:].reshape(MLA_KVR, hw)

    o_raw, ya_in, s_all = _gla_fwd(proj, gw_pad, w["gla_gate_b"], w["gla_norm_g"], bsz, lp)
    qf = _q_up(proj, w["mla_q_norm_g"], wn, wr, wt, cos_t, sin_t, bsz, lp)
    kf, vf = _kv_up(proj, w["mla_kv_norm_g"], wk, wv, cos_t, sin_t, bsz, lp)
    o_b, yb_in, lse = _attn_fwd(qf, kf, vf, proj, bsz, lp)
    y_a, y_b, dh2, loss, d_final_g = _mid_fwd(ya_in, yb_in, proj, hp, loss_target, w["gla_proj"], w["mla_proj"],
                                              w["w_out"], w["final_norm_g"], bsz, lp)
    d_ya, d_o, dproj, delta, d_w_out, d_gla_proj, d_mla_proj = _mid_bwd(
        dh2, y_a, y_b, proj, ya_in, yb_in, o_b, w["w_out"], w["gla_proj"], w["mla_proj"], bsz, lp)
    dproj, d_gate, d_gla_norm = _gla_bwd(proj, gw_pad, w["gla_gate_b"], w["gla_norm_g"], o_raw, s_all, d_ya, dproj,
                                         bsz, lp)
    d_lr, d_gw_pad, d_gate_b = _gate_bwd(d_gate, proj, gw_pad)
    dqf, dkf, dvf = _attn_bwd(qf, kf, vf, d_o, lse, delta, bsz, lp)
    dproj, d_wn, d_wr, d_wt, d_qn = _q_up_bwd(dqf, proj, w["mla_q_norm_g"], wn, wr, wt, cos_t, sin_t, dproj,
                                              bsz, lp)
    dproj, d_wk, d_wv, d_kvn = _kv_up_bwd(dkf, dvf, proj, w["mla_kv_norm_g"], wk, wv, cos_t, sin_t, d_lr, dproj,
                                          bsz, lp)

    d_rope = (d_wr.reshape(MLA_QR, MLA_HEADS, LANE)[:, :, :MLA_ROPE]
              + _unrot_cols(d_wt.reshape(MLA_QR, MLA_HEADS, LANE)[:, :, :MLA_ROPE]))
    d_uq = jnp.concatenate([d_wn.reshape(MLA_QR, MLA_HEADS, LANE), d_rope], axis=-1).reshape(MLA_QR, MLA_HEADS * MLA_QK)
    d_ukv = jnp.concatenate([d_wk.reshape(MLA_KVR, MLA_HEADS, LANE), d_wv.reshape(MLA_KVR, MLA_HEADS, LANE)],
                            axis=-1).reshape(MLA_KVR, MLA_HEADS * (MLA_NOPE + MLA_DV))
    mats = dict(gla_gate_w=d_gw_pad[:GLA_RANK], gla_proj=d_gla_proj, mla_w_uq=d_uq, mla_w_ukv=d_ukv,
                mla_proj=d_mla_proj, w_out=d_w_out)
    packed = _pad_rows(jnp.concatenate([_split8(mats[n], axis).reshape(N_DEV, -1) for n, _, axis in PACKED], axis=1),
                       PACK_ROWS)
    d_w_ext_t, packed_parts = _dw_in(u, dproj, _bf(packed))
    w_in_slabs = _bf(_w_in_grad_t(d_w_ext_t).reshape(N_DEV, W_IN_SHARD, D_MODEL))
    d_hp, d_norm_g, w_in_parts = _dx_in(dproj, w_ext, hp, dh2, w["norm_g"], w_in_slabs)
    d_hp3 = d_hp.reshape(bsz, lp, D_MODEL)
    small = dict(meta_tokens=_meta_grad(d_hp3), norm_g=d_norm_g, gla_gate_b=d_gate_b, gla_norm_g=d_gla_norm,
                 mla_q_norm_g=d_qn, mla_kv_norm_g=d_kvn, final_norm_g=d_final_g)
    return loss, d_hp3[:, X0:, :], w_in_parts, packed_parts, small


PACKED = (("gla_gate_w", (GLA_RANK, GLA_KW // N_DEV), 1),
          ("gla_proj", (D_MODEL // N_DEV, D_MODEL), 0), ("mla_w_uq", (MLA_QR, MLA_HEADS * MLA_QK // N_DEV), 1),
          ("mla_w_ukv", (MLA_KVR, MLA_HEADS * (MLA_NOPE + MLA_DV) // N_DEV), 1),
          ("mla_proj", (D_MODEL // N_DEV, D_MODEL), 0), ("w_out", (D_MODEL // N_DEV, D_MODEL), 0))
REPLICATED = (("norm_g", D_MODEL), ("gla_gate_b", GLA_KW), ("gla_norm_g", GLA_DV), ("mla_q_norm_g", MLA_QR),
              ("mla_kv_norm_g", MLA_KVR), ("final_norm_g", D_MODEL))
PACK_ROWS = 3744
PACK_BLOCK = 1248
SMALL_ROWS = 48
LOSS_ROW = N_META + 25
W_IN_BLOCK = 128


def _all_gather(shards):
    n_arr = len(shards)

    def body(*refs):
        x_refs, out_refs = refs[:n_arr], refs[n_arr:2 * n_arr]
        send_sems, recv_sems, local_sems = refs[2 * n_arr:]
        x, y, c = _my_place()
        me, sibling = (x, y, c), (x, y, 1 - c)
        chips = [(1 - x, y), (x, 1 - y), (1 - x, 1 - y)]

        def copy(a, k, block, to, from_input=False):
            slab = out_refs[a].at[4 * block[0] + 2 * block[1] + block[2]]
            return pltpu.make_async_remote_copy(
                src_ref=x_refs[a] if from_input else slab, dst_ref=slab,
                send_sem=send_sems.at[7 * a + k], recv_sem=recv_sems.at[7 * a + k], device_id=to,
                device_id_type=MESH_ID)

        arrays = range(n_arr)
        mine = [pltpu.make_async_copy(x_refs[a], out_refs[a].at[4 * x + 2 * y + c], local_sems.at[a]) for a in arrays]
        for cp in mine:
            cp.start()
        first = [copy(a, 0, me, sibling, True) for a in arrays]
        first += [copy(a, 1 + j, me, (*chip, c), True) for j, chip in enumerate(chips) for a in arrays]
        for cp in first:
            cp.start()
        passed = []
        for j, chip in enumerate(chips):
            for a in arrays:
                copy(a, 1 + j, (*chip, c), me).wait_recv()
                passed.append(copy(a, 4 + j, (*chip, c), sibling))
                passed[-1].start()
        for a in arrays:
            copy(a, 0, sibling, me).wait_recv()
        for j, chip in enumerate(chips):
            for a in arrays:
                copy(a, 4 + j, (*chip, 1 - c), me).wait_recv()
        for cp in first + passed:
            cp.wait_send()
        for cp in mine:
            cp.wait()

    anyspec = pl.BlockSpec(memory_space=pl.ANY)
    return pl.pallas_call(
        body, name="weights_all_gather",
        out_shape=[jax.ShapeDtypeStruct((N_DEV,) + s.shape, s.dtype) for s in shards],
        in_specs=[anyspec] * n_arr, out_specs=[anyspec] * n_arr,
        scratch_shapes=[pltpu.SemaphoreType.DMA((7 * n_arr,)), pltpu.SemaphoreType.DMA((7 * n_arr,)),
                        pltpu.SemaphoreType.DMA((n_arr,))],
    )(*shards)


def _small_exchange(slabs):
    def body(g_ref, recv_ref, send_sems, recv_sems, local_sem):
        _exchange(g_ref, recv_ref, send_sems, recv_sems, local_sem, True)
        _exchange(g_ref, recv_ref, send_sems, recv_sems, local_sem, False)

    vmem = pl.BlockSpec(memory_space=pltpu.VMEM)
    return pl.pallas_call(
        body, name="small_exchange", out_shape=jax.ShapeDtypeStruct(slabs.shape, slabs.dtype),
        in_specs=[vmem], out_specs=vmem, scratch_shapes=EXCHANGE_SEMS,
    )(slabs)


def _adamw(parts, w, m, v, block_rows, name):
    rows, cols = w.shape

    def body(p_ref, w_ref, m_ref, v_ref, g_out, d_out, m_out, v_out):
        g = p_ref[0].astype(F32)
        for s in range(1, N_DEV):
            g = g + p_ref[s].astype(F32)
        m_new = ADAM_B1 * m_ref[...] + (1.0 - ADAM_B1) * g
        v_new = ADAM_B2 * v_ref[...] + (1.0 - ADAM_B2) * (g * g)
        m_hat = m_new / (1.0 - ADAM_B1 ** ADAM_STEP)
        v_hat = v_new / (1.0 - ADAM_B2 ** ADAM_STEP)
        g_out[...] = g
        d_out[...] = -ADAM_LR * (m_hat / (jnp.sqrt(v_hat) + ADAM_EPS) + ADAM_WD * w_ref[...])
        m_out[...] = m_new
        v_out[...] = v_new

    spec = pl.BlockSpec((block_rows, cols), lambda i: (i, 0))
    return pl.pallas_call(
        body, name=name, grid=(pl.cdiv(rows, block_rows),),
        in_specs=[pl.BlockSpec((N_DEV, block_rows, cols), lambda i: (0, i, 0)), spec, spec, spec],
        out_specs=[spec] * 4, out_shape=[jax.ShapeDtypeStruct((rows, cols), F32)] * 4,
        compiler_params=_cp(("parallel",), 48),
    )(parts, w, m, v)


def _pad_rows(flat, rows):
    pad = rows * LANE - flat.shape[-1]
    flat = jnp.pad(flat, [(0, 0)] * (flat.ndim - 1) + [(0, pad)])
    return flat.reshape(flat.shape[:-1] + (rows, LANE))


def _pack_shards(shards):
    return _pad_rows(jnp.concatenate([shards[n].reshape(-1) for n, _, _ in PACKED]), PACK_ROWS)


def _unpack_shards(packed):
    flat, out, off = packed.reshape(-1), {}, 0
    for n, shape, _ in PACKED:
        size = shape[0] * shape[1]
        out[n] = flat[off:off + size].reshape(shape)
        off += size
    return out


def _split8(full, axis):
    r, c = full.shape
    if axis == 0:
        return full.reshape(N_DEV, r // N_DEV, c)
    return full.reshape(r, N_DEV, c // N_DEV).transpose(1, 0, 2)


def _join8(shards, axis):
    _, r, c = shards.shape
    if axis == 0:
        return shards.reshape(N_DEV * r, c)
    return shards.transpose(1, 0, 2).reshape(r, N_DEV * c)


def _pack_small(meta_shard, vals, loss_row):
    rows = jnp.concatenate([vals[n].reshape(-1, LANE) for n, _ in REPLICATED] + [loss_row], axis=0)
    rows = jnp.pad(rows, ((0, SMALL_ROWS - N_META - rows.shape[0]), (0, 0)))
    return jnp.concatenate([meta_shard, jnp.broadcast_to(rows, meta_shard.shape[:-2] + rows.shape)], axis=-2)


def _unpack_small(packed):
    out, off = {"meta_tokens": packed[:N_META]}, N_META
    for n, size in REPLICATED:
        out[n] = packed[off:off + size // LANE].reshape(1, size)
        off += size // LANE
    return out


def kernel(x, meta_tokens, norm_g, w_in, gla_gate_w, gla_gate_b, gla_norm_g, gla_proj, mla_q_norm_g, mla_w_uq, mla_kv_norm_g, mla_w_ukv, mla_proj, w_out, final_norm_g, loss_target, m_meta_tokens, m_norm_g, m_w_in, m_gla_gate_w, m_gla_gate_b, m_gla_norm_g, m_gla_proj, m_mla_q_norm_g, m_mla_w_uq, m_mla_kv_norm_g, m_mla_w_ukv, m_mla_proj, m_w_out, m_final_norm_g, v_meta_tokens, v_norm_g, v_w_in, v_gla_gate_w, v_gla_gate_b, v_gla_norm_g, v_gla_proj, v_mla_q_norm_g, v_mla_w_uq, v_mla_kv_norm_g, v_mla_w_ukv, v_mla_proj, v_w_out, v_final_norm_g):
    given = dict(meta_tokens=meta_tokens, norm_g=norm_g, w_in=w_in, gla_gate_w=gla_gate_w, gla_gate_b=gla_gate_b,
                 gla_norm_g=gla_norm_g, gla_proj=gla_proj, mla_q_norm_g=mla_q_norm_g, mla_w_uq=mla_w_uq,
                 mla_kv_norm_g=mla_kv_norm_g, mla_w_ukv=mla_w_ukv, mla_proj=mla_proj, w_out=w_out,
                 final_norm_g=final_norm_g)
    mom_m = dict(meta_tokens=m_meta_tokens, norm_g=m_norm_g, w_in=m_w_in, gla_gate_w=m_gla_gate_w,
                 gla_gate_b=m_gla_gate_b, gla_norm_g=m_gla_norm_g, gla_proj=m_gla_proj, mla_q_norm_g=m_mla_q_norm_g,
                 mla_w_uq=m_mla_w_uq, mla_kv_norm_g=m_mla_kv_norm_g, mla_w_ukv=m_mla_w_ukv, mla_proj=m_mla_proj,
                 w_out=m_w_out, final_norm_g=m_final_norm_g)
    mom_v = dict(meta_tokens=v_meta_tokens, norm_g=v_norm_g, w_in=v_w_in, gla_gate_w=v_gla_gate_w,
                 gla_gate_b=v_gla_gate_b, gla_norm_g=v_gla_norm_g, gla_proj=v_gla_proj, mla_q_norm_g=v_mla_q_norm_g,
                 mla_w_uq=v_mla_w_uq, mla_kv_norm_g=v_mla_kv_norm_g, mla_w_ukv=v_mla_w_ukv, mla_proj=v_mla_proj,
                 w_out=v_w_out, final_norm_g=v_final_norm_g)
    shapes = {n: a.shape for n, a in given.items()}
    shard2d = {n: s for n, s, _ in PACKED}
    shard2d["w_in"] = (D_MODEL, W_IN_SHARD)
    shard2d["meta_tokens"] = (N_META, LANE)

    def as2d(tree):
        out = {n: tree[n].reshape(shard2d[n]) for n in shard2d}
        out.update({n: tree[n].reshape(1, size) for n, size in REPLICATED})
        return out

    w_loc, m_loc, v_loc = as2d(given), as2d(mom_m), as2d(mom_v)

    w_in_all, meta_all = _all_gather([w_loc["w_in"].astype(BF16), w_loc["meta_tokens"]])
    flat = jnp.concatenate([w_loc[n].astype(BF16).reshape(-1) for n, _, _ in PACKED])
    full = {"w_in": w_in_all, "meta_tokens": _join8(meta_all, 1), "packed": _pad_rows(flat, PACK_ROWS)}
    for n, _ in REPLICATED:
        full[n] = w_loc[n]

    loss_part, grad_x, w_in_parts, packed_parts, small = _local_step(x, loss_target, full)
    small_all = _small_exchange(_pack_small(_split8(small["meta_tokens"], 1), small,
                                            jnp.broadcast_to(loss_part[:, :1], (1, LANE))))

    w_in_t = [t["w_in"].T for t in (w_loc, m_loc, v_loc)]
    g_w, d_w, m_w, v_w = (o.T for o in _adamw(w_in_parts, *w_in_t, W_IN_BLOCK, "adamw_w_in"))
    g_p, d_p, m_p, v_p = _adamw(packed_parts, _pack_shards(w_loc), _pack_shards(m_loc), _pack_shards(v_loc),
                                PACK_BLOCK, "adamw_packed")
    zero_row = jnp.zeros((1, LANE), F32)
    g_s, d_s, m_s, v_s = _adamw(small_all, *(_pack_small(t["meta_tokens"], t, zero_row) for t in (w_loc, m_loc, v_loc)),
                                SMALL_ROWS, "adamw_small")
    loss = g_s[LOSS_ROW, 0]

    order = ["meta_tokens", "norm_g", "w_in", "gla_gate_w", "gla_gate_b", "gla_norm_g", "gla_proj", "mla_q_norm_g",
             "mla_w_uq", "mla_kv_norm_g", "mla_w_ukv", "mla_proj", "w_out", "final_norm_g"]
    result = [loss, grad_x]
    for w_in_out, packed_sh, packed_sm in ((g_w, g_p, g_s), (d_w, d_p, d_s), (m_w, m_p, m_s), (v_w, v_p, v_s)):
        tree = _unpack_shards(packed_sh)
        tree.update(_unpack_small(packed_sm))
        tree["w_in"] = w_in_out
        result += [tree[n].reshape(shapes[n]) for n in order]
    return tuple(result)
```

```python
import jax
import jax.numpy as jnp
from jax import lax
from jax.experimental import pallas as pl
from jax.experimental.pallas import tpu as pltpu

F32 = jnp.float32
BF16 = jnp.bfloat16

D_MODEL = 1024
N_META = 16
EPS = 1e-6
FRONT = 48
X0 = FRONT + N_META
GLA_HEADS, GLA_DK, GLA_DV, GLA_RANK, GLA_CHUNK = 4, 128, 256, 16, 64
GLA_GATE_NORMALIZER = 16.0
GLA_KW = GLA_HEADS * GLA_DK
GLA_VW = GLA_HEADS * GLA_DV
MLA_HEADS, MLA_NOPE, MLA_ROPE, MLA_DV, MLA_QR, MLA_KVR = 8, 128, 64, 128, 256, 128
MLA_QK = MLA_NOPE + MLA_ROPE
ROPE_BASE = 10000.0
LANE = 128
QKW = 2 * LANE

C_V, C_Z, C_Q, C_K = 0, 1024, 2048, 2560
C_MZ, C_GG, C_GM = 3072, 4096, 5120
C_CKV, C_KR, C_KROT, C_LR = 6144, 6272, 6400, 6528
C_CQ = 6656
N_EXT = 6912
O_Q, O_K, O_V, O_LR, O_Z, O_CQ, O_CKV, O_KR, O_MZ, O_GG, O_GM, N_IN = (
    0, 512, 1024, 2048, 2064, 3088, 3344, 3472, 3536, 4560, 5584, 6608)

ADAM_LR, ADAM_B1, ADAM_B2, ADAM_EPS, ADAM_WD, ADAM_STEP = 0.001, 0.9, 0.999, 1e-08, 0.01, 10

N_DEV = 8
TOK = 192
ATT_BLOCK = 352
EXT_BLOCK = 1152
MXU_DEPTH = 256


def _cp(sems=None, vmem_mb=None):
    kw = {}
    if sems is not None:
        kw["dimension_semantics"] = sems
    if vmem_mb is not None:
        kw["vmem_limit_bytes"] = vmem_mb * 1024 * 1024
    return pltpu.CompilerParams(**kw)


def _dot(a, b):
    return jnp.dot(a, b, preferred_element_type=F32)


def _dot_nt(a, b):
    return lax.dot_general(a, b, (((1,), (1,)), ((), ())), preferred_element_type=F32)


def _dot_tn(a, b):
    return lax.dot_general(a, b, (((0,), (0,)), ((), ())), preferred_element_type=F32)


def _sigmoid(x):
    return 1.0 / (1.0 + jnp.exp(-x))


def _bf(x):
    return x.astype(BF16)


def _big_tok(tp):
    return 4 * TOK if tp % (4 * TOK) == 0 else TOK


def _attn_block(lp):
    return ATT_BLOCK if lp % ATT_BLOCK == 0 else TOK


def _wide_block(lp):
    return 2 * ATT_BLOCK if lp % (2 * ATT_BLOCK) == 0 else _attn_block(lp)


def _proj_in(x, head, norm_g, w_ext, packed):
    bsz, seq, _ = x.shape
    lp = X0 + seq
    tp = bsz * lp
    tm = _attn_block(lp)
    nb = lp // tm
    last = pl.cdiv(seq, tm) - 1

    def body(xa_ref, xb_ref, hd_ref, g_ref, w_ref, p_ref, h_ref, u_ref, o_ref, pall_ref, send_sems, recv_sems, local_sem):
        first = jnp.logical_and(pl.program_id(0) == 0, pl.program_id(1) == 0)

        @pl.when(first)
        def _():
            _exchange(p_ref, pall_ref, send_sems, recv_sems, local_sem, True, same=True)

        front = jnp.where(pl.program_id(1) == 0, hd_ref[...], xa_ref[0, tm - X0:, :])
        h = jnp.concatenate([front, xb_ref[0, :tm - X0, :]], axis=0)
        h_ref[...] = h
        r = lax.rsqrt(jnp.mean(h * h, axis=-1, keepdims=True) + EPS)
        u = _bf(h * r * g_ref[...])
        u_ref[...] = u
        o_ref[...] = _bf(_dot(u, w_ref[...]))

        @pl.when(jnp.logical_and(pl.program_id(0) == bsz - 1, pl.program_id(1) == nb - 1))
        def _():
            _exchange(p_ref, pall_ref, send_sems, recv_sems, local_sem, False, same=True)

    anyspec = pl.BlockSpec(memory_space=pl.ANY)
    tok = lambda width: pl.BlockSpec((tm, width), lambda b, i: (b * nb + i, 0))
    return pl.pallas_call(
        body, name="proj_in", grid=(bsz, nb),
        in_specs=[pl.BlockSpec((1, tm, D_MODEL), lambda b, i: (b, jnp.maximum(i - 1, 0), 0)),
                  pl.BlockSpec((1, tm, D_MODEL), lambda b, i: (b, jnp.minimum(i, last), 0)),
                  pl.BlockSpec((X0, D_MODEL), lambda b, i: (0, 0)),
                  pl.BlockSpec((1, D_MODEL), lambda b, i: (0, 0)),
                  pl.BlockSpec((D_MODEL, N_EXT), lambda b, i: (0, 0), pipeline_mode=pl.Buffered(1)), anyspec],
        out_specs=[tok(D_MODEL), tok(D_MODEL), tok(N_EXT), anyspec],
        out_shape=[jax.ShapeDtypeStruct((tp, D_MODEL), F32), jax.ShapeDtypeStruct((tp, D_MODEL), BF16),
                   jax.ShapeDtypeStruct((tp, N_EXT), BF16),
                   jax.ShapeDtypeStruct((N_DEV,) + packed.shape, packed.dtype)],
        scratch_shapes=EXCHANGE_SEMS,
        compiler_params=_cp(("arbitrary", "arbitrary"), 56),
    )(x, x, head, norm_g, w_ext, packed)


def _gla_group(n_chunks):
    return 11 if n_chunks % 11 == 0 else 3


def _tri_dot(tri, x):
    hi = _bf(x)
    rest = x - hi.astype(F32)
    mid = _bf(rest)
    return _dot(tri, hi) + _dot(tri, mid) + _dot(tri, _bf(rest - mid.astype(F32)))


def _gla_gates(q_ref, k_ref, lr_ref, gw_ref, gb_ref, rows, not_first):
    z = _dot(lr_ref[rows, :], gw_ref[...]) + gb_ref[...]
    logsig = jnp.minimum(z, 0.0) - jnp.log(1.0 + jnp.exp(-jnp.abs(z)))
    row = lax.broadcasted_iota(jnp.int32, (GLA_CHUNK, GLA_KW), 0)
    live = jnp.logical_or(not_first, row >= FRONT)
    g = jnp.where(live, logsig * (1.0 / GLA_GATE_NORMALIZER), 0.0)
    ri = lax.broadcasted_iota(jnp.int32, (GLA_CHUNK, GLA_CHUNK), 0)
    ci = lax.broadcasted_iota(jnp.int32, (GLA_CHUNK, GLA_CHUNK), 1)
    tril = ci <= ri
    b = _tri_dot(_bf(tril.astype(F32)), g)
    bl = jnp.sum(jnp.where(row == GLA_CHUNK - 1, b, 0.0), axis=0, keepdims=True)
    eb, enb, elb, ebl = jnp.exp(b), jnp.exp(-b), jnp.exp(bl - b), jnp.exp(bl)
    q = q_ref[rows, :].astype(F32) * (GLA_DK ** -0.5)
    k = k_ref[rows, :].astype(F32)
    qe, ke, kl = q * eb, k * enb, k * elb
    return dict(z=z, live=live, tril=tril, row=row, eb=eb, enb=enb, elb=elb, ebl=ebl, qe=qe, ke=ke, kl=kl,
                qe_b=_bf(qe), ke_b=_bf(ke), kl_b=_bf(kl))


def _gla_in_specs(n_groups, gla_rows, rev):
    def rb(b, n):
        return b * n_groups + ((n_groups - 1 - n) if rev else n)

    return rb, [pl.BlockSpec((gla_rows, GLA_KW), lambda b, n: (rb(b, n), C_Q // GLA_KW)),
                pl.BlockSpec((gla_rows, GLA_KW), lambda b, n: (rb(b, n), C_K // GLA_KW)),
                pl.BlockSpec((gla_rows, GLA_VW), lambda b, n: (rb(b, n), C_V // GLA_VW)),
                pl.BlockSpec((gla_rows, GLA_VW), lambda b, n: (rb(b, n), C_Z // GLA_VW)),
                pl.BlockSpec((gla_rows, LANE), lambda b, n: (rb(b, n), C_LR // LANE)),
                pl.BlockSpec((LANE, GLA_KW), lambda b, n: (0, 0)),
                pl.BlockSpec((1, GLA_KW), lambda b, n: (0, 0)),
                pl.BlockSpec((1, GLA_DV), lambda b, n: (0, 0))]


def _gla_fwd(proj, gw_pad, gate_b, gla_norm_g, bsz, lp):
    n_chunks = lp // GLA_CHUNK
    gla_group = _gla_group(n_chunks)
    gla_rows = gla_group * GLA_CHUNK
    n_groups = n_chunks // gla_group
    tp = bsz * lp

    def body(q_ref, k_ref, v_ref, z_ref, lr_ref, gw_ref, gb_ref, gn_ref, oraw_ref, ya_ref, sall_ref, st_scr):
        grp = pl.program_id(1)

        @pl.when(grp == 0)
        def _():
            st_scr[...] = jnp.zeros_like(st_scr)

        chunks = [slice(j * GLA_CHUNK, (j + 1) * GLA_CHUNK) for j in range(gla_group)]
        cs = [_gla_gates(q_ref, k_ref, lr_ref, gw_ref, gb_ref, rows, True if j else grp > 0)
              for j, rows in enumerate(chunks)]
        gn = gn_ref[...]
        sts = [st_scr[h] for h in range(GLA_HEADS)]
        heads = [(slice(h * GLA_DK, (h + 1) * GLA_DK), slice(h * GLA_DV, (h + 1) * GLA_DV)) for h in range(GLA_HEADS)]
        a_all = [[_bf(jnp.where(c["tril"], _dot_nt(c["qe_b"][:, ks], c["ke_b"][:, ks]), 0.0)) for ks, _ in heads]
                 for c in cs]
        u_all = [[_dot_tn(v_ref[rows, vs], c["kl_b"][:, ks]) for ks, vs in heads] for rows, c in zip(chunks, cs)]
        for j, (rows, c) in enumerate(zip(chunks, cs)):
            for h, (ks, vs) in enumerate(heads):
                st = sts[h]
                sall_ref[0, j, h] = st
                o = _dot(a_all[j][h], v_ref[rows, vs]) + _dot_nt(c["qe_b"][:, ks], _bf(st))
                sts[h] = st * c["ebl"][:, ks] + u_all[j][h]
                oraw_ref[rows, vs] = o
                r = lax.rsqrt(jnp.mean(o * o, axis=-1, keepdims=True) + EPS)
                zg = z_ref[rows, vs].astype(F32)
                ya_ref[rows, vs] = _bf((o * r * gn) * (zg * _sigmoid(zg)))
        for h in range(GLA_HEADS):
            st_scr[h] = sts[h]

    rb, in_specs = _gla_in_specs(n_groups, gla_rows, False)
    return pl.pallas_call(
        body, name="gla_fwd", grid=(bsz, n_groups), in_specs=in_specs,
        out_specs=[pl.BlockSpec((gla_rows, GLA_VW), lambda b, n: (rb(b, n), 0)),
                   pl.BlockSpec((gla_rows, GLA_VW), lambda b, n: (rb(b, n), 0)),
                   pl.BlockSpec((1, gla_group, GLA_HEADS, GLA_DV, GLA_DK), lambda b, n: (b, n, 0, 0, 0))],
        out_shape=[jax.ShapeDtypeStruct((tp, GLA_VW), F32), jax.ShapeDtypeStruct((tp, GLA_VW), BF16),
                   jax.ShapeDtypeStruct((bsz, n_chunks, GLA_HEADS, GLA_DV, GLA_DK), F32)],
        scratch_shapes=[pltpu.VMEM((GLA_HEADS, GLA_DV, GLA_DK), F32)],
        compiler_params=_cp(("parallel", "arbitrary"), 56),
    )(proj, proj, proj, proj, proj, gw_pad, gate_b, gla_norm_g)


def _gla_bwd(proj, gw_pad, gate_b, gla_norm_g, o_raw, s_all, d_ya, dproj, bsz, lp):
    n_chunks = lp // GLA_CHUNK
    gla_group = _gla_group(n_chunks)
    gla_rows = gla_group * GLA_CHUNK
    n_groups = n_chunks // gla_group
    tp = bsz * lp

    def body(q_ref, k_ref, v_ref, z_ref, lr_ref, gw_ref, gb_ref, gn_ref, o_ref, s_ref, dya_ref, _,
             dp_ref, dz_ref, dgn_ref, dst_scr):
        dv_ref, dzg_ref = dp_ref.at[:, C_V:C_V + GLA_VW], dp_ref.at[:, C_Z:C_Z + GLA_VW]

        @pl.when(jnp.logical_and(pl.program_id(0) == 0, pl.program_id(1) == 0))
        def _():
            dgn_ref[...] = jnp.zeros_like(dgn_ref)

        @pl.when(pl.program_id(1) == 0)
        def _():
            dst_scr[...] = jnp.zeros_like(dst_scr)

        grp = n_groups - 1 - pl.program_id(1)
        chunks = [slice(j * GLA_CHUNK, (j + 1) * GLA_CHUNK) for j in range(gla_group)]
        cs = [_gla_gates(q_ref, k_ref, lr_ref, gw_ref, gb_ref, rows, True if j else grp > 0)
              for j, rows in enumerate(chunks)]
        gn = gn_ref[...]
        dgn = jnp.zeros((1, GLA_DV), F32)
        dqe_h, dke_h, dkl_h, dbl_h = ([[None] * GLA_HEADS for _ in chunks] for _ in range(4))
        dsts = [dst_scr[h] for h in range(GLA_HEADS)]
        for j in reversed(range(gla_group)):
            rows, c = chunks[j], cs[j]
            for h in range(GLA_HEADS):
                ks, vs = slice(h * GLA_DK, (h + 1) * GLA_DK), slice(h * GLA_DV, (h + 1) * GLA_DV)
                dst = dsts[h]
                v = v_ref[rows, vs]
                st = s_ref[0, j, h]
                o = o_ref[rows, vs]
                r = lax.rsqrt(jnp.mean(o * o, axis=-1, keepdims=True) + EPS)
                xh = o * r
                zg = z_ref[rows, vs].astype(F32)
                sg = _sigmoid(zg)
                dy = dya_ref[rows, vs].astype(F32)
                dzg_ref[rows, vs] = _bf(dy * (xh * gn) * (sg * (1.0 + zg * (1.0 - sg))))
                t = dy * (zg * sg)
                dgn += jnp.sum(t * xh, axis=0, keepdims=True)
                dxh = t * gn
                do_b = _bf(r * (dxh - xh * jnp.mean(dxh * xh, axis=-1, keepdims=True)))
                qe_b, ke_b, kl_b, dst_b = c["qe_b"][:, ks], c["ke_b"][:, ks], c["kl_b"][:, ks], _bf(dst)
                a = jnp.where(c["tril"], _dot_nt(qe_b, ke_b), 0.0)
                da_b = _bf(jnp.where(c["tril"], _dot_nt(do_b, v), 0.0))
                dqe_h[j][h] = _dot(da_b, ke_b) + _dot(do_b, _bf(st))
                dke_h[j][h] = _dot_tn(da_b, qe_b)
                dkl = _dot(v, dst_b)
                dkl_h[j][h] = dkl
                dv_ref[rows, vs] = _bf(_dot_tn(_bf(a), do_b) + _dot_nt(kl_b, dst_b))
                ddecay = jnp.sum(dst * st, axis=0, keepdims=True)
                dbl_h[j][h] = jnp.sum(dkl * c["kl"][:, ks], axis=0, keepdims=True) + ddecay * c["ebl"][:, ks]
                dsts[h] = dst * c["ebl"][:, ks] + _dot_tn(do_b, qe_b)
        for h in range(GLA_HEADS):
            dst_scr[h] = dsts[h]
        dgn_ref[...] += dgn
        ri = lax.broadcasted_iota(jnp.int32, (GLA_CHUNK, GLA_CHUNK), 0)
        ci = lax.broadcasted_iota(jnp.int32, (GLA_CHUNK, GLA_CHUNK), 1)
        triu = _bf((ci >= ri).astype(F32))
        for j, (rows, c) in enumerate(zip(chunks, cs)):
            dqe, dke, dkl, dbl = (jnp.concatenate(p[j], axis=1) for p in (dqe_h, dke_h, dkl_h, dbl_h))
            db = dqe * c["qe"] - dke * c["ke"] - dkl * c["kl"] + jnp.where(c["row"] == GLA_CHUNK - 1, dbl, 0.0)
            dg = _tri_dot(triu, db)
            dg = jnp.where(c["live"], dg, 0.0)
            dz_ref[rows, :] = dg * (1.0 / GLA_GATE_NORMALIZER) * _sigmoid(-c["z"])
            dp_ref[rows, C_Q:C_Q + GLA_KW] = _bf(dqe * c["eb"] * (GLA_DK ** -0.5))
            dp_ref[rows, C_K:C_K + GLA_KW] = _bf(dke * c["enb"] + dkl * c["elb"])

    rb, in_specs = _gla_in_specs(n_groups, gla_rows, True)
    wide = pl.BlockSpec((gla_rows, GLA_VW), lambda b, n: (rb(b, n), 0))
    group = C_MZ
    return pl.pallas_call(
        body, name="gla_bwd", grid=(bsz, n_groups),
        in_specs=in_specs + [wide, pl.BlockSpec((1, gla_group, GLA_HEADS, GLA_DV, GLA_DK),
                                                lambda b, n: (b, n_groups - 1 - n, 0, 0, 0)), wide,
                             pl.BlockSpec(memory_space=pl.ANY)],
        out_specs=[pl.BlockSpec((gla_rows, group), lambda b, n: (rb(b, n), 0)),
                   pl.BlockSpec((gla_rows, GLA_KW), lambda b, n: (rb(b, n), 0)),
                   pl.BlockSpec((1, GLA_DV), lambda b, n: (0, 0))],
        out_shape=[jax.ShapeDtypeStruct((tp, N_EXT), BF16), jax.ShapeDtypeStruct((tp, GLA_KW), F32),
                   jax.ShapeDtypeStruct((1, GLA_DV), F32)],
        input_output_aliases={11: 0},
        scratch_shapes=[pltpu.VMEM((GLA_HEADS, GLA_DV, GLA_DK), F32)],
        compiler_params=_cp(("arbitrary", "arbitrary"), 56),
    )(proj, proj, proj, proj, proj, gw_pad, gate_b, gla_norm_g, o_raw, s_all, d_ya, dproj)


def _gate_bwd(dz, proj, gw_pad):
    tp = dz.shape[0]
    tm = _big_tok(tp)

    def body(dz_ref, lr_ref, gw_ref, dlr_ref, dgw_ref, dgb_ref):
        @pl.when(pl.program_id(0) == 0)
        def _():
            dgw_ref[...] = jnp.zeros_like(dgw_ref)
            dgb_ref[...] = jnp.zeros_like(dgb_ref)

        dz = dz_ref[...]
        dz_b = _bf(dz)
        dlr_ref[...] = _bf(_dot_nt(dz_b, gw_ref[...]))
        dgw_ref[...] += _dot_tn(lr_ref[...], dz_b)
        dgb_ref[...] += jnp.sum(dz, axis=0, keepdims=True)

    return pl.pallas_call(
        body, name="gate_bwd", grid=(tp // tm,),
        in_specs=[pl.BlockSpec((tm, GLA_KW), lambda i: (i, 0)),
                  pl.BlockSpec((tm, LANE), lambda i: (i, C_LR // LANE)),
                  pl.BlockSpec((LANE, GLA_KW), lambda i: (0, 0))],
        out_specs=[pl.BlockSpec((tm, LANE), lambda i: (i, 0)),
                   pl.BlockSpec((LANE, GLA_KW), lambda i: (0, 0)),
                   pl.BlockSpec((1, GLA_KW), lambda i: (0, 0))],
        out_shape=[jax.ShapeDtypeStruct((tp, LANE), BF16), jax.ShapeDtypeStruct((LANE, GLA_KW), F32),
                   jax.ShapeDtypeStruct((1, GLA_KW), F32)],
        compiler_params=_cp(("arbitrary",)),
    )(dz, proj, gw_pad)


def _rms_fwd(x):
    r = lax.rsqrt(jnp.mean(x * x, axis=-1, keepdims=True) + EPS)
    return x * r, r


def _rms_bwd(dy, xh, r, g):
    dxh = dy * g
    dx = r * (dxh - xh * jnp.mean(dxh * xh, axis=-1, keepdims=True))
    return dx, jnp.sum(dy * xh, axis=0, keepdims=True)


def _q_up(proj, q_norm_g, wn, wr, wt, cos_t, sin_t, bsz, lp):
    tp = bsz * lp
    tok = _wide_block(lp)
    nb = lp // tok

    def body(cq_ref, g_ref, wn_ref, wr_ref, wt_ref, cos_ref, sin_ref, q_ref):
        xh, _ = _rms_fwd(cq_ref[...].astype(F32))
        cqn = _bf(xh * g_ref[...])
        nope = _dot(cqn, wn_ref[...])
        rope = _dot(cqn, wr_ref[...])
        rot = _dot(cqn, wt_ref[...])
        cos, sin = cos_ref[...], sin_ref[...]
        one = (lax.broadcasted_iota(jnp.int32, (tok, LANE), 1) == BIAS_LANE).astype(F32)
        for h in range(MLA_HEADS):
            sl = slice(h * LANE, (h + 1) * LANE)
            q_ref[:, h * QKW:h * QKW + LANE] = _bf(nope[:, sl])
            q_ref[:, h * QKW + LANE:(h + 1) * QKW] = _bf(rope[:, sl] * cos + rot[:, sl] * sin + one)

    wspec = pl.BlockSpec((MLA_QR, MLA_HEADS * LANE), lambda b, i: (0, 0))
    tspec = pl.BlockSpec((tok, LANE), lambda b, i: (i, 0))
    return pl.pallas_call(
        body, name="mla_q_up", grid=(bsz, nb),
        in_specs=[pl.BlockSpec((tok, MLA_QR), lambda b, i: (b * nb + i, C_CQ // MLA_QR)),
                  pl.BlockSpec((1, MLA_QR), lambda b, i: (0, 0)), wspec, wspec, wspec, tspec, tspec],
        out_specs=pl.BlockSpec((tok, MLA_HEADS * QKW), lambda b, i: (b * nb + i, 0)),
        out_shape=jax.ShapeDtypeStruct((tp, MLA_HEADS * QKW), BF16),
        compiler_params=_cp(("parallel", "parallel")),
    )(proj, q_norm_g, wn, wr, wt, cos_t, sin_t)


def _kv_up(proj, kv_norm_g, wk, wv, cos_t, sin_t, bsz, lp):
    tp = bsz * lp
    tok = _wide_block(lp)
    nb = lp // tok

    def body(ckv_ref, kr_ref, krot_ref, g_ref, wk_ref, wv_ref, cos_ref, sin_ref, k_ref, v_ref):
        xh, _ = _rms_fwd(ckv_ref[...].astype(F32))
        cn = _bf(xh * g_ref[...])
        kn = _dot(cn, wk_ref[...])
        v_ref[...] = _bf(_dot(cn, wv_ref[...]))
        pos = pl.program_id(1) * tok + lax.broadcasted_iota(jnp.int32, (tok, LANE), 0)
        lane = lax.broadcasted_iota(jnp.int32, (tok, LANE), 1)
        bias = jnp.where(jnp.logical_and(lane == BIAS_LANE, pos < FRONT), KEY_BIAS, 0.0)
        kr = _bf(kr_ref[...].astype(F32) * cos_ref[...] + krot_ref[...].astype(F32) * sin_ref[...] + bias)
        for h in range(MLA_HEADS):
            k_ref[:, h * QKW:h * QKW + LANE] = _bf(kn[:, h * LANE:(h + 1) * LANE])
            k_ref[:, h * QKW + LANE:(h + 1) * QKW] = kr

    wspec = pl.BlockSpec((MLA_KVR, MLA_HEADS * LANE), lambda b, i: (0, 0))
    tspec = pl.BlockSpec((tok, LANE), lambda b, i: (i, 0))
    return pl.pallas_call(
        body, name="mla_kv_up", grid=(bsz, nb),
        in_specs=[pl.BlockSpec((tok, LANE), lambda b, i: (b * nb + i, C_CKV // LANE)),
                  pl.BlockSpec((tok, LANE), lambda b, i: (b * nb + i, C_KR // LANE)),
                  pl.BlockSpec((tok, LANE), lambda b, i: (b * nb + i, C_KROT // LANE)),
                  pl.BlockSpec((1, MLA_KVR), lambda b, i: (0, 0)), wspec, wspec, tspec, tspec],
        out_specs=[pl.BlockSpec((tok, MLA_HEADS * QKW), lambda b, i: (b * nb + i, 0)),
                   pl.BlockSpec((tok, MLA_HEADS * LANE), lambda b, i: (b * nb + i, 0))],
        out_shape=[jax.ShapeDtypeStruct((tp, MLA_HEADS * QKW), BF16),
                   jax.ShapeDtypeStruct((tp, MLA_HEADS * LANE), BF16)],
        compiler_params=_cp(("parallel", "parallel")),
    )(proj, proj, proj, kv_norm_g, wk, wv, cos_t, sin_t)


ATT_SCALE = MLA_QK ** -0.5


KEY_BIAS = -1e30
BIAS_LANE = MLA_ROPE
NEG = 2 * KEY_BIAS
LOG2E = 1.4426950408889634
EXP2_SCALE = ATT_SCALE * LOG2E


def _causal_fill(s, r0, fill):
    tq, kmax = s.shape
    a = r0 // LANE * LANE
    mask = (a + lax.broadcasted_iota(jnp.int32, (tq, kmax - a), 1)
            <= r0 + lax.broadcasted_iota(jnp.int32, (tq, kmax - a), 0))
    right = jnp.where(mask, s[:, a:], fill)
    return jnp.concatenate([s[:, :a], right], axis=1) if a else right


def _attn_fwd(qf, kf, vf, proj, bsz, lp):
    tp = bsz * lp
    tq = _attn_block(lp)
    nh = 2

    def body(q_ref, k_ref, v_ref, mz_ref, ob_ref, yb_ref, lse_ref):
        starts = list(range(0, lp, tq))
        for pair in (starts[i:i + 2] for i in range(0, len(starts), 2)):
            work = [(r0, h) for r0 in pair for h in range(nh)]
            ss = [_causal_fill(_dot_nt(q_ref[r0:r0 + tq, h * QKW:(h + 1) * QKW],
                                       k_ref[0:r0 + tq, h * QKW:(h + 1) * QKW]), r0, NEG) for r0, h in work]
            ms = [jnp.max(s, axis=-1, keepdims=True) for s in ss]
            ps = [jnp.exp2((s - m) * EXP2_SCALE) for s, m in zip(ss, ms)]
            ls = [jnp.sum(p, axis=-1, keepdims=True) for p in ps]
            for (r0, h), p, m, l in zip(work, ps, ms, ls):
                rows, cols = slice(r0, r0 + tq), slice(h * MLA_DV, (h + 1) * MLA_DV)
                o = _dot(_bf(p), v_ref[0:r0 + tq, cols]) / l
                ob_ref[rows, cols] = _bf(o)
                mz = mz_ref[rows, cols].astype(F32)
                yb_ref[rows, cols] = _bf(o * (mz * _sigmoid(mz)))
                lse_ref[0, h, rows, :] = jnp.broadcast_to(m * EXP2_SCALE + jnp.log2(l), (tq, LANE))

    head = lambda off: pl.BlockSpec((lp, nh * MLA_DV), lambda b, h: (b, off + h))
    wide = pl.BlockSpec((lp, nh * QKW), lambda b, h: (b, h))
    return pl.pallas_call(
        body, name="mla_attn_fwd", grid=(bsz, MLA_HEADS // nh),
        in_specs=[wide, wide, head(0), head(C_MZ // (nh * MLA_DV))],
        out_specs=[head(0), head(0), pl.BlockSpec((1, nh, lp, LANE), lambda b, h: (b, h, 0, 0))],
        out_shape=[jax.ShapeDtypeStruct((tp, MLA_HEADS * MLA_DV), BF16),
                   jax.ShapeDtypeStruct((tp, MLA_HEADS * MLA_DV), BF16),
                   jax.ShapeDtypeStruct((bsz, MLA_HEADS, lp, LANE), F32)],
        compiler_params=_cp(("parallel", "parallel"), 56),
    )(qf, kf, vf, proj)


def _attn_bwd_blocks(lp):
    return [(0, X0)] + [(r0, min(MXU_DEPTH, lp - r0)) for r0 in range(X0, lp, MXU_DEPTH)]


def _attn_bwd(qf, kf, vf, d_o, lse, delta, bsz, lp):
    tp = bsz * lp

    def body(q_ref, k_ref, v_ref, do_ref, lse_ref, dl_ref, dq_ref, dk_ref, dv_ref, dk_acc, dv_acc):
        dk_acc[...] = jnp.zeros_like(dk_acc)
        dv_acc[...] = jnp.zeros_like(dv_acc)
        for r0, tq in _attn_bwd_blocks(lp):
            rows, kmax = slice(r0, r0 + tq), r0 + tq
            q, do = q_ref[rows, :], do_ref[rows, :]
            k, v = k_ref[0:kmax, :], v_ref[0:kmax, :]
            p = jnp.exp2(_dot_nt(q, k) * EXP2_SCALE - lse_ref[0, 0, rows, :][:, :1])
            p = _causal_fill(p, r0, 0.0)
            ds = _bf(p * (_dot_nt(do, v) - dl_ref[0, rows, :][:, :1]))
            dq_ref[rows, :] = _bf(_dot(ds, k) * ATT_SCALE)
            dk_acc[0:kmax, :] += _dot_tn(ds, q)
            dv_acc[0:kmax, :] += _dot_tn(_bf(p), do)
        dk_ref[...] = _bf(dk_acc[...] * ATT_SCALE)
        dv_ref[...] = _bf(dv_acc[...])

    wide = pl.BlockSpec((lp, QKW), lambda b, h: (b, h))
    narrow = pl.BlockSpec((lp, MLA_DV), lambda b, h: (b, h))
    stat = pl.BlockSpec((1, 1, lp, LANE), lambda b, h: (b, h, 0, 0))
    return pl.pallas_call(
        body, name="mla_attn_bwd", grid=(bsz, MLA_HEADS),
        in_specs=[wide, wide, narrow, narrow, stat, pl.BlockSpec((1, lp, LANE), lambda b, h: (h, b, 0))],
        out_specs=[wide, wide, narrow],
        out_shape=[jax.ShapeDtypeStruct((tp, MLA_HEADS * QKW), BF16), jax.ShapeDtypeStruct((tp, MLA_HEADS * QKW), BF16),
                   jax.ShapeDtypeStruct((tp, MLA_HEADS * MLA_DV), BF16)],
        scratch_shapes=[pltpu.VMEM((lp, QKW), F32), pltpu.VMEM((lp, MLA_DV), F32)],
        compiler_params=_cp(("parallel", "parallel"), 56),
    )(qf, kf, vf, d_o, lse, delta)


def _q_up_bwd(dqf, proj, q_norm_g, wn, wr, wt, cos_t, sin_t, dproj, bsz, lp):
    tp = bsz * lp
    tok = _wide_block(lp)
    nb = lp // tok
    hw = MLA_HEADS * LANE

    def body(dq_ref, cq_ref, g_ref, wn_ref, wr_ref, wt_ref, cos_ref, sin_ref, _,
             dcq_ref, dwn_ref, dwr_ref, dwt_ref, dg_ref):
        @pl.when(jnp.logical_and(pl.program_id(0) == 0, pl.program_id(1) == 0))
        def _():
            for r in (dwn_ref, dwr_ref, dwt_ref, dg_ref):
                r[...] = jnp.zeros_like(r)

        g = g_ref[...]
        xh, r = _rms_fwd(cq_ref[...].astype(F32))
        cqn = _bf(xh * g)
        dn = jnp.concatenate([dq_ref[:, h * QKW:h * QKW + LANE] for h in range(MLA_HEADS)], axis=1)
        dr = jnp.concatenate([dq_ref[:, h * QKW + LANE:(h + 1) * QKW] for h in range(MLA_HEADS)], axis=1).astype(F32)
        dr_c = _bf(dr * jnp.tile(cos_ref[...], (1, MLA_HEADS)))
        dr_s = _bf(dr * jnp.tile(sin_ref[...], (1, MLA_HEADS)))
        dcqn = _dot_nt(dn, wn_ref[...]) + _dot_nt(dr_c, wr_ref[...]) + _dot_nt(dr_s, wt_ref[...])
        dwn_ref[...] += _dot_tn(cqn, dn)
        dwr_ref[...] += _dot_tn(cqn, dr_c)
        dwt_ref[...] += _dot_tn(cqn, dr_s)
        dx, dg = _rms_bwd(dcqn, xh, r, g)
        dcq_ref[...] = _bf(dx)
        dg_ref[...] += dg

    aspec = pl.BlockSpec((MLA_QR, hw), lambda b, i: (0, 0))
    tspec = pl.BlockSpec((tok, LANE), lambda b, i: (i, 0))
    return pl.pallas_call(
        body, name="mla_q_up_bwd", grid=(bsz, nb),
        in_specs=[pl.BlockSpec((tok, MLA_HEADS * QKW), lambda b, i: (b * nb + i, 0)),
                  pl.BlockSpec((tok, MLA_QR), lambda b, i: (b * nb + i, C_CQ // MLA_QR)),
                  pl.BlockSpec((1, MLA_QR), lambda b, i: (0, 0)), aspec, aspec, aspec, tspec, tspec,
                  pl.BlockSpec(memory_space=pl.ANY)],
        out_specs=[pl.BlockSpec((tok, MLA_QR), lambda b, i: (b * nb + i, C_CQ // MLA_QR)), aspec, aspec, aspec,
                   pl.BlockSpec((1, MLA_QR), lambda b, i: (0, 0))],
        out_shape=[jax.ShapeDtypeStruct((tp, N_EXT), BF16)] + [jax.ShapeDtypeStruct((MLA_QR, hw), F32)] * 3
        + [jax.ShapeDtypeStruct((1, MLA_QR), F32)],
        input_output_aliases={8: 0},
        compiler_params=_cp(("arbitrary", "arbitrary")),
    )(dqf, proj, q_norm_g, wn, wr, wt, cos_t, sin_t, dproj)


def _kv_up_bwd(dkf, dvf, proj, kv_norm_g, wk, wv, cos_t, sin_t, d_lr, dproj, bsz, lp):
    tp = bsz * lp
    tok = _wide_block(lp)
    nb = lp // tok
    hw = MLA_HEADS * LANE

    def body(dk_ref, dv_ref, ckv_ref, g_ref, wk_ref, wv_ref, cos_ref, sin_ref, dlr_ref, _,
             dp_ref, dwk_ref, dwv_ref, dg_ref):
        dckv_ref, dkr_ref, dkrot_ref = (dp_ref.at[:, j * LANE:(j + 1) * LANE] for j in range(3))
        dp_ref[:, 3 * LANE:] = dlr_ref[...]
        @pl.when(jnp.logical_and(pl.program_id(0) == 0, pl.program_id(1) == 0))
        def _():
            for r in (dwk_ref, dwv_ref, dg_ref):
                r[...] = jnp.zeros_like(r)

        g = g_ref[...]
        xh, r = _rms_fwd(ckv_ref[...].astype(F32))
        cn = _bf(xh * g)
        dv = dv_ref[...]
        dn = jnp.concatenate([dk_ref[:, h * QKW:h * QKW + LANE] for h in range(MLA_HEADS)], axis=1)
        dcn = _dot_nt(dv, wv_ref[...]) + _dot_nt(dn, wk_ref[...])
        dwv_ref[...] += _dot_tn(cn, dv)
        dwk_ref[...] += _dot_tn(cn, dn)
        drope = jnp.zeros((tok, LANE), F32)
        for h in range(MLA_HEADS):
            drope += dk_ref[:, h * QKW + LANE:(h + 1) * QKW].astype(F32)
        dkr_ref[...] = _bf(drope * cos_ref[...])
        dkrot_ref[...] = _bf(drope * sin_ref[...])
        dx, dg = _rms_bwd(dcn, xh, r, g)
        dckv_ref[...] = _bf(dx)
        dg_ref[...] += dg

    aspec = pl.BlockSpec((MLA_KVR, hw), lambda b, i: (0, 0))
    tspec = pl.BlockSpec((tok, LANE), lambda b, i: (i, 0))
    ospec = pl.BlockSpec((tok, LANE), lambda b, i: (b * nb + i, 0))
    return pl.pallas_call(
        body, name="mla_kv_up_bwd", grid=(bsz, nb),
        in_specs=[pl.BlockSpec((tok, MLA_HEADS * QKW), lambda b, i: (b * nb + i, 0)),
                  pl.BlockSpec((tok, hw), lambda b, i: (b * nb + i, 0)),
                  pl.BlockSpec((tok, LANE), lambda b, i: (b * nb + i, C_CKV // LANE)),
                  pl.BlockSpec((1, MLA_KVR), lambda b, i: (0, 0)), aspec, aspec, tspec, tspec, ospec,
                  pl.BlockSpec(memory_space=pl.ANY)],
        out_specs=[pl.BlockSpec((tok, 4 * LANE), lambda b, i: (b * nb + i, C_CKV // (4 * LANE))), aspec, aspec,
                   pl.BlockSpec((1, MLA_KVR), lambda b, i: (0, 0))],
        out_shape=[jax.ShapeDtypeStruct((tp, N_EXT), BF16)] + [jax.ShapeDtypeStruct((MLA_KVR, hw), F32)] * 2
        + [jax.ShapeDtypeStruct((1, MLA_KVR), F32)],
        input_output_aliases={9: 0},
        compiler_params=_cp(("arbitrary", "arbitrary")),
    )(dkf, dvf, proj, kv_norm_g, wk, wv, cos_t, sin_t, d_lr, dproj)


def _mid_fwd(ya_in, yb_in, proj, hp, target, w_gp, w_mp, w_o, final_g, bsz, lp):
    tp = bsz * lp
    tm = _wide_block(lp)
    nb = lp // tm
    last = pl.cdiv(lp - X0, tm) - 1

    def body(ya_ref, yb_ref, gg_ref, gm_ref, h_ref, ta_ref, tb_ref, wgp_ref, wmp_ref, wo_ref, fg_ref,
             ya_out, yb_out, dh_ref, loss_ref, dfg_ref):
        @pl.when(jnp.logical_and(pl.program_id(0) == 0, pl.program_id(1) == 0))
        def _():
            loss_ref[...] = jnp.zeros_like(loss_ref)
            dfg_ref[...] = jnp.zeros_like(dfg_ref)

        y_a = _dot(ya_ref[...], wgp_ref[...])
        y_b = _dot(yb_ref[...], wmp_ref[...])
        ya_out[...] = _bf(y_a)
        yb_out[...] = _bf(y_b)
        merged = _sigmoid(gg_ref[...].astype(F32)) * y_a + _sigmoid(gm_ref[...].astype(F32)) * y_b
        h2 = h_ref[...] + _dot(_bf(merged), wo_ref[...])
        fg = fg_ref[...]
        xh, r = _rms_fwd(h2)
        pos = pl.program_id(1) * tm + lax.broadcasted_iota(jnp.int32, (tm, 1), 0)
        t = jnp.concatenate([ta_ref[0, tm - X0:, :], tb_ref[0, :tm - X0, :]], axis=0)
        err = jnp.where(pos >= X0, xh * fg - t, 0.0)
        loss_ref[...] += 0.5 * jnp.sum(jnp.mean(err * err, axis=-1, keepdims=True), axis=0, keepdims=True)
        dy = err * (1.0 / D_MODEL)
        dx, dfg = _rms_bwd(dy, xh, r, fg)
        dh_ref[...] = dx
        dfg_ref[...] += dfg

    tok = lambda c: pl.BlockSpec((tm, D_MODEL), lambda b, i: (b * nb + i, c))
    wspec = pl.BlockSpec((D_MODEL, D_MODEL), lambda b, i: (0, 0), pipeline_mode=pl.Buffered(1))
    return pl.pallas_call(
        body, name="mid_fwd", grid=(bsz, nb),
        in_specs=[tok(0), tok(0), tok(C_GG // D_MODEL), tok(C_GM // D_MODEL), tok(0),
                  pl.BlockSpec((1, tm, D_MODEL), lambda b, i: (b, jnp.maximum(i - 1, 0), 0)),
                  pl.BlockSpec((1, tm, D_MODEL), lambda b, i: (b, jnp.minimum(i, last), 0)),
                  wspec, wspec, wspec, pl.BlockSpec((1, D_MODEL), lambda b, i: (0, 0))],
        out_specs=[tok(0), tok(0), tok(0), pl.BlockSpec((1, LANE), lambda b, i: (0, 0)),
                   pl.BlockSpec((1, D_MODEL), lambda b, i: (0, 0))],
        out_shape=[jax.ShapeDtypeStruct((tp, D_MODEL), BF16), jax.ShapeDtypeStruct((tp, D_MODEL), BF16),
                   jax.ShapeDtypeStruct((tp, D_MODEL), F32), jax.ShapeDtypeStruct((1, LANE), F32),
                   jax.ShapeDtypeStruct((1, D_MODEL), F32)],
        compiler_params=_cp(("arbitrary", "arbitrary"), 56),
    )(ya_in, yb_in, proj, proj, hp, target, target, w_gp, w_mp, w_o, final_g)


def _mid_bwd(dh2, y_a, y_b, proj, ya_in, yb_in, o_b, w_o, w_gp, w_mp, bsz, lp):
    tp = bsz * lp
    tm = MXU_DEPTH if tp % MXU_DEPTH == 0 else _attn_block(lp)
    nsteps = tp // tm
    group = 3 * D_MODEL

    def body(dh_ref, ya_ref, yb_ref, mz_ref, gg_ref, gm_ref, yai_ref, ybi_ref, ob_ref, wo_ref, wgp_ref, wmp_ref,
             dyai_ref, do_ref, dp_ref, dl_ref, dwo_ref, dwgp_ref, dwmp_ref, a_o, a_gp, a_mp):
        @pl.when(pl.program_id(0) == 0)
        def _():
            for r in (a_o, a_gp, a_mp):
                r[...] = jnp.zeros_like(r)

        dh = _bf(dh_ref[...])
        dm = _dot_nt(dh, wo_ref[...])
        y_a, y_b = ya_ref[...].astype(F32), yb_ref[...].astype(F32)
        sg, sm = _sigmoid(gg_ref[...].astype(F32)), _sigmoid(gm_ref[...].astype(F32))
        d_ya, d_yb = _bf(sg * dm), _bf(sm * dm)
        dp_ref[:, D_MODEL:2 * D_MODEL] = _bf(dm * y_a * sg * (1.0 - sg))
        dp_ref[:, 2 * D_MODEL:] = _bf(dm * y_b * sm * (1.0 - sm))
        merged = _bf(sg * y_a + sm * y_b)
        dy = _dot_nt(d_yb, wmp_ref[...])
        dyai_ref[...] = _bf(_dot_nt(d_ya, wgp_ref[...]))
        a_o[...] += _dot_tn(merged, dh)
        a_gp[...] += _dot_tn(yai_ref[...], d_ya)
        a_mp[...] += _dot_tn(ybi_ref[...], d_yb)
        mz, o = mz_ref[...].astype(F32), ob_ref[...].astype(F32)
        s = _sigmoid(mz)
        do = _bf(dy * (mz * s))
        do_ref[...] = do
        dp_ref[:, :D_MODEL] = _bf(dy * o * (s * (1.0 + mz * (1.0 - s))))
        prod = do.astype(F32) * o
        for h in range(MLA_HEADS):
            dl = jnp.sum(prod[:, h * MLA_DV:(h + 1) * MLA_DV], axis=-1, keepdims=True)
            dl_ref[h] = jnp.broadcast_to(dl, (tm, LANE))

        @pl.when(pl.program_id(0) == nsteps - 1)
        def _():
            pltpu.sync_copy(a_o, dwo_ref)
            pltpu.sync_copy(a_gp, dwgp_ref)
            pltpu.sync_copy(a_mp, dwmp_ref)

    tok = lambda c: pl.BlockSpec((tm, D_MODEL), lambda i: (i, c))
    wspec = pl.BlockSpec((D_MODEL, D_MODEL), lambda i: (0, 0))
    anyspec = pl.BlockSpec(memory_space=pl.ANY)
    wshape = jax.ShapeDtypeStruct((D_MODEL, D_MODEL), F32)
    return pl.pallas_call(
        body, name="mid_bwd", grid=(nsteps,),
        in_specs=[tok(0), tok(0), tok(0), tok(C_MZ // D_MODEL), tok(C_GG // D_MODEL), tok(C_GM // D_MODEL),
                  tok(0), tok(0), tok(0), wspec, wspec, wspec],
        out_specs=[tok(0), tok(0), pl.BlockSpec((tm, group), lambda i: (i, C_MZ // group)),
                   pl.BlockSpec((MLA_HEADS, tm, LANE), lambda i: (0, i, 0)), anyspec, anyspec, anyspec],
        out_shape=[jax.ShapeDtypeStruct((tp, D_MODEL), BF16)] * 2 + [jax.ShapeDtypeStruct((tp, N_EXT), BF16),
                   jax.ShapeDtypeStruct((MLA_HEADS, tp, LANE), F32)] + [wshape] * 3,
        scratch_shapes=[pltpu.VMEM((D_MODEL, D_MODEL), F32)] * 3,
        compiler_params=_cp(("arbitrary",), 56),
    )(dh2, y_a, y_b, proj, proj, proj, ya_in, yb_in, o_b, w_o, w_gp, w_mp)


MESH_ID = pl.DeviceIdType.MESH
EXCHANGE_SEMS = [pltpu.SemaphoreType.DMA((N_DEV - 1,)), pltpu.SemaphoreType.DMA((N_DEV - 1,)), pltpu.SemaphoreType.DMA]


def _my_place():
    return lax.axis_index("x"), lax.axis_index("y"), lax.axis_index("c")


def _exchange(g_ref, recv_ref, send_sems, recv_sems, local_sem, start, same=False):
    x, y, c = _my_place()
    me = 4 * x + 2 * y + c
    own = pltpu.make_async_copy(g_ref if same else g_ref.at[me], recv_ref.at[me], local_sem)
    sends, lands = [], []
    for d in range(1, N_DEV):
        px = 1 - x if d & 4 else x
        py = 1 - y if d & 2 else y
        pc = 1 - c if d & 1 else c
        peer = 4 * px + 2 * py + pc
        for slot, group in ((me, sends),) if start else ((me, sends), (peer, lands)):
            group.append(pltpu.make_async_remote_copy(
                src_ref=g_ref if same else g_ref.at[peer], dst_ref=recv_ref.at[slot], send_sem=send_sems.at[d - 1],
                recv_sem=recv_sems.at[d - 1], device_id=(px, py, pc), device_id_type=MESH_ID))
    if start:
        own.start()
        for cp in sends:
            cp.start()
    else:
        for cp in lands:
            cp.wait_recv()
        for cp in sends:
            cp.wait_send()
        own.wait()


def _dw_in(u, dproj, slabs):
    tp = u.shape[0]
    tn = 3 * LANE
    nj = N_EXT // tn

    def body(u_ref, d_ref, g_ref, o_ref, recv_ref, send_sems, recv_sems, local_sem):
        j = pl.program_id(0)

        @pl.when(j == 0)
        def _():
            _exchange(g_ref, recv_ref, send_sems, recv_sems, local_sem, True)

        o_ref[...] = _dot_tn(d_ref[...], u_ref[...])

        @pl.when(j == nj - 1)
        def _():
            _exchange(g_ref, recv_ref, send_sems, recv_sems, local_sem, False)

    anyspec = pl.BlockSpec(memory_space=pl.ANY)
    return pl.pallas_call(
        body, name="dw_in", grid=(nj,),
        in_specs=[pl.BlockSpec((tp, D_MODEL), lambda j: (0, 0), pipeline_mode=pl.Buffered(1)),
                  pl.BlockSpec((tp, tn), lambda j: (0, j)), anyspec],
        out_specs=[pl.BlockSpec((tn, D_MODEL), lambda j: (j, 0)), anyspec],
        out_shape=[jax.ShapeDtypeStruct((N_EXT, D_MODEL), F32), jax.ShapeDtypeStruct(slabs.shape, slabs.dtype)],
        scratch_shapes=EXCHANGE_SEMS,
        compiler_params=_cp(("arbitrary",), 56),
    )(u, dproj, slabs)


def _dx_in(dproj, w_ext, hp, dh2, norm_g, slabs):
    tp = hp.shape[0]
    tm = 2 * TOK
    ni = tp // tm

    def body(d_ref, w_ref, h_ref, dh_ref, g_ref, s_ref, o_ref, dg_ref, recv_ref, send_sems, recv_sems, local_sem):
        i = pl.program_id(0)

        @pl.when(i == 0)
        def _():
            _exchange(s_ref, recv_ref, send_sems, recv_sems, local_sem, True)
            dg_ref[...] = jnp.zeros_like(dg_ref)

        du = _dot_nt(d_ref[...], w_ref[...])
        g = g_ref[...]
        xh, r = _rms_fwd(h_ref[...])
        dx, dg = _rms_bwd(du, xh, r, g)
        o_ref[...] = dh_ref[...] + dx
        dg_ref[...] += dg

        @pl.when(i == ni - 1)
        def _():
            _exchange(s_ref, recv_ref, send_sems, recv_sems, local_sem, False)

    tok = pl.BlockSpec((tm, D_MODEL), lambda i: (i, 0))
    anyspec = pl.BlockSpec(memory_space=pl.ANY)
    return pl.pallas_call(
        body, name="dx_in", grid=(ni,),
        in_specs=[pl.BlockSpec((tm, N_EXT), lambda i: (i, 0)),
                  pl.BlockSpec((D_MODEL, N_EXT), lambda i: (0, 0), pipeline_mode=pl.Buffered(1)),
                  tok, tok, pl.BlockSpec((1, D_MODEL), lambda i: (0, 0)), anyspec],
        out_specs=[tok, pl.BlockSpec((1, D_MODEL), lambda i: (0, 0)), anyspec],
        out_shape=[jax.ShapeDtypeStruct((tp, D_MODEL), F32), jax.ShapeDtypeStruct((1, D_MODEL), F32),
                   jax.ShapeDtypeStruct(slabs.shape, slabs.dtype)],
        scratch_shapes=EXCHANGE_SEMS,
        compiler_params=_cp(("arbitrary",), 56),
    )(dproj, w_ext, hp, dh2, norm_g, slabs)


def _meta_grad(dhp3):
    bsz = dhp3.shape[0]

    def body(d_ref, o_ref):
        @pl.when(pl.program_id(0) == 0)
        def _():
            o_ref[...] = jnp.zeros_like(o_ref)

        o_ref[...] += d_ref[0]

    return pl.pallas_call(
        body, name="meta_grad", grid=(bsz,),
        in_specs=[pl.BlockSpec((1, N_META, D_MODEL), lambda b: (b, FRONT // N_META, 0))],
        out_specs=pl.BlockSpec((N_META, D_MODEL), lambda b: (0, 0)),
        out_shape=jax.ShapeDtypeStruct((N_META, D_MODEL), F32),
        compiler_params=_cp(("arbitrary",)),
    )(dhp3)


W_IN_SHARD = N_IN // N_DEV


def _pad_lanes(a, width=LANE):
    return jnp.pad(a, [(0, 0)] * (a.ndim - 1) + [(0, width - a.shape[-1])])


def _rot_cols(w):
    half = w.shape[-1] // 2
    return jnp.concatenate([-w[..., half:], w[..., :half]], axis=-1)


def _unrot_cols(dw):
    half = dw.shape[-1] // 2
    return jnp.concatenate([dw[..., half:], -dw[..., :half]], axis=-1)


def _w_in_cols(shards, lo, hi):
    parts = []
    for k in range(lo // W_IN_SHARD, (hi - 1) // W_IN_SHARD + 1):
        a, b = max(lo, k * W_IN_SHARD), min(hi, (k + 1) * W_IN_SHARD)
        parts.append(shards[k][:, a - k * W_IN_SHARD:b - k * W_IN_SHARD])
    return parts[0] if len(parts) == 1 else jnp.concatenate(parts, axis=1)


def _w_in_ext(shards):
    c = lambda lo, hi: _w_in_cols(shards, lo, hi)
    kr = c(O_KR, O_MZ)
    return jnp.concatenate([
        c(O_V, O_LR), c(O_Z, O_CQ), c(O_Q, O_K), c(O_K, O_V), c(O_MZ, O_GG), c(O_GG, O_GM), c(O_GM, N_IN),
        c(O_CKV, O_KR), _pad_lanes(kr), _pad_lanes(_rot_cols(kr)), _pad_lanes(c(O_LR, O_Z)), c(O_CQ, O_CKV)], axis=1)


def _w_in_grad_t(dwt):
    g = lambda start, width: dwt[start:start + width]
    half = MLA_ROPE // 2
    krot = g(C_KROT, MLA_ROPE)
    kr = g(C_KR, MLA_ROPE) + jnp.concatenate([krot[half:], -krot[:half]], axis=0)
    return jnp.concatenate([
        g(C_Q, GLA_KW), g(C_K, GLA_KW), g(C_V, GLA_VW), g(C_LR, GLA_RANK), g(C_Z, GLA_VW), g(C_CQ, MLA_QR),
        g(C_CKV, MLA_KVR), kr, g(C_MZ, D_MODEL), g(C_GG, D_MODEL), g(C_GM, D_MODEL)], axis=0)


def _rope_tables(lp):
    inv = 1.0 / (ROPE_BASE ** (jnp.arange(0, MLA_ROPE, 2, dtype=F32) / MLA_ROPE))
    ang = (jnp.arange(lp, dtype=F32) - FRONT)[:, None] * inv[None, :]
    cos, sin = jnp.cos(ang), jnp.sin(ang)
    return _pad_lanes(jnp.concatenate([cos, cos], axis=1)), _pad_lanes(jnp.concatenate([sin, sin], axis=1))


def _local_step(x, loss_target, w):
    bsz, seq, _ = x.shape
    lp = X0 + seq
    tp = bsz * lp
    assert lp % TOK == 0 and (lp // GLA_CHUNK) % _gla_group(lp // GLA_CHUNK) == 0
    head = jnp.concatenate([jnp.zeros((FRONT, D_MODEL), F32), w["meta_tokens"]], axis=0)
    cos_t, sin_t = _rope_tables(lp)

    w_ext = _w_in_ext(w["w_in"])
    hp, u, proj, packed_all = _proj_in(x, head, w["norm_g"], w_ext, w["packed"])
    packed_all, off = packed_all.reshape(N_DEV, -1), 0
    for n, shape, axis in PACKED:
        size = shape[0] * shape[1]
        w[n] = _join8(packed_all[:, off:off + size].reshape((N_DEV,) + shape), axis)
        off += size
    gw_pad = jnp.pad(w["gla_gate_w"], ((0, LANE - GLA_RANK), (0, 0)))
    uq = w["mla_w_uq"].reshape(MLA_QR, MLA_HEADS, MLA_QK)
    rope_w = uq[:, :, MLA_NOPE:]
    hw = MLA_HEADS * LANE
    wn = uq[:, :, :MLA_NOPE].reshape(MLA_QR, hw)
    wr = _pad_lanes(rope_w).reshape(MLA_QR, hw)
    wt = _pad_lanes(_rot_cols(rope_w)).reshape(MLA_QR, hw)
    ukv = w["mla_w_ukv"].reshape(MLA_KVR, MLA_HEADS, MLA_NOPE + MLA_DV)
    wk = ukv[:, :, :MLA_NOPE].reshape(MLA_KVR, hw)
    wv = ukv[:, :, MLA_NOPE:].reshape(MLA_KVR, hw)

    o_raw, ya_in, s_all = _gla_fwd(proj, gw_pad, w["gla_gate_b"], w["gla_norm_g"], bsz, lp)
    qf = _q_up(proj, w["mla_q_norm_g"], wn, wr, wt, cos_t, sin_t, bsz, lp)
    kf, vf = _kv_up(proj, w["mla_kv_norm_g"], wk, wv, cos_t, sin_t, bsz, lp)
    o_b, yb_in, lse = _attn_fwd(qf, kf, vf, proj, bsz, lp)
    y_a, y_b, dh2, loss, d_final_g = _mid_fwd(ya_in, yb_in, proj, hp, loss_target, w["gla_proj"], w["mla_proj"],
                                              w["w_out"], w["final_norm_g"], bsz, lp)
    d_ya, d_o, dproj, delta, d_w_out, d_gla_proj, d_mla_proj = _mid_bwd(
        dh2, y_a, y_b, proj, ya_in, yb_in, o_b, w["w_out"], w["gla_proj"], w["mla_proj"], bsz, lp)
    dproj, d_gate, d_gla_norm = _gla_bwd(proj, gw_pad, w["gla_gate_b"], w["gla_norm_g"], o_raw, s_all, d_ya, dproj,
                                         bsz, lp)
    d_lr, d_gw_pad, d_gate_b = _gate_bwd(d_gate, proj, gw_pad)
    dqf, dkf, dvf = _attn_bwd(qf, kf, vf, d_o, lse, delta, bsz, lp)
    dproj, d_wn, d_wr, d_wt, d_qn = _q_up_bwd(dqf, proj, w["mla_q_norm_g"], wn, wr, wt, cos_t, sin_t, dproj,
                                              bsz, lp)
    dproj, d_wk, d_wv, d_kvn = _kv_up_bwd(dkf, dvf, proj, w["mla_kv_norm_g"], wk, wv, cos_t, sin_t, d_lr, dproj,
                                          bsz, lp)

    d_rope = (d_wr.reshape(MLA_QR, MLA_HEADS, LANE)[:, :, :MLA_ROPE]
              + _unrot_cols(d_wt.reshape(MLA_QR, MLA_HEADS, LANE)[:, :, :MLA_ROPE]))
    d_uq = jnp.concatenate([d_wn.reshape(MLA_QR, MLA_HEADS, LANE), d_rope], axis=-1).reshape(MLA_QR, MLA_HEADS * MLA_QK)
    d_ukv = jnp.concatenate([d_wk.reshape(MLA_KVR, MLA_HEADS, LANE), d_wv.reshape(MLA_KVR, MLA_HEADS, LANE)],
                            axis=-1).reshape(MLA_KVR, MLA_HEADS * (MLA_NOPE + MLA_DV))
    mats = dict(gla_gate_w=d_gw_pad[:GLA_RANK], gla_proj=d_gla_proj, mla_w_uq=d_uq, mla_w_ukv=d_ukv,
                mla_proj=d_mla_proj, w_out=d_w_out)
    packed = _pad_rows(jnp.concatenate([_split8(mats[n], axis).reshape(N_DEV, -1) for n, _, axis in PACKED], axis=1),
                       PACK_ROWS)
    d_w_ext_t, packed_parts = _dw_in(u, dproj, _bf(packed))
    w_in_slabs = _bf(_w_in_grad_t(d_w_ext_t).reshape(N_DEV, W_IN_SHARD, D_MODEL))
    d_hp, d_norm_g, w_in_parts = _dx_in(dproj, w_ext, hp, dh2, w["norm_g"], w_in_slabs)
    d_hp3 = d_hp.reshape(bsz, lp, D_MODEL)
    small = dict(meta_tokens=_meta_grad(d_hp3), norm_g=d_norm_g, gla_gate_b=d_gate_b, gla_norm_g=d_gla_norm,
                 mla_q_norm_g=d_qn, mla_kv_norm_g=d_kvn, final_norm_g=d_final_g)
    return loss, d_hp3[:, X0:, :], w_in_parts, packed_parts, small


PACKED = (("gla_gate_w", (GLA_RANK, GLA_KW // N_DEV), 1),
          ("gla_proj", (D_MODEL // N_DEV, D_MODEL), 0), ("mla_w_uq", (MLA_QR, MLA_HEADS * MLA_QK // N_DEV), 1),
          ("mla_w_ukv", (MLA_KVR, MLA_HEADS * (MLA_NOPE + MLA_DV) // N_DEV), 1),
          ("mla_proj", (D_MODEL // N_DEV, D_MODEL), 0), ("w_out", (D_MODEL // N_DEV, D_MODEL), 0))
REPLICATED = (("norm_g", D_MODEL), ("gla_gate_b", GLA_KW), ("gla_norm_g", GLA_DV), ("mla_q_norm_g", MLA_QR),
              ("mla_kv_norm_g", MLA_KVR), ("final_norm_g", D_MODEL))
PACK_ROWS = 3744
PACK_BLOCK = 1248
SMALL_ROWS = 48
LOSS_ROW = N_META + 25
W_IN_BLOCK = 128


def _all_gather(shards):
    n_arr = len(shards)

    def body(*refs):
        x_refs, out_refs = refs[:n_arr], refs[n_arr:2 * n_arr]
        send_sems, recv_sems, local_sems = refs[2 * n_arr:]
        x, y, c = _my_place()
        me, sibling = (x, y, c), (x, y, 1 - c)
        chips = [(1 - x, y), (x, 1 - y), (1 - x, 1 - y)]

        def copy(a, k, block, to, from_input=False):
            slab = out_refs[a].at[4 * block[0] + 2 * block[1] + block[2]]
            return pltpu.make_async_remote_copy(
                src_ref=x_refs[a] if from_input else slab, dst_ref=slab,
                send_sem=send_sems.at[7 * a + k], recv_sem=recv_sems.at[7 * a + k], device_id=to,
                device_id_type=MESH_ID)

        arrays = range(n_arr)
        mine = [pltpu.make_async_copy(x_refs[a], out_refs[a].at[4 * x + 2 * y + c], local_sems.at[a]) for a in arrays]
        for cp in mine:
            cp.start()
        first = [copy(a, 0, me, sibling, True) for a in arrays]
        first += [copy(a, 1 + j, me, (*chip, c), True) for j, chip in enumerate(chips) for a in arrays]
        for cp in first:
            cp.start()
        passed = []
        for j, chip in enumerate(chips):
            for a in arrays:
                copy(a, 1 + j, (*chip, c), me).wait_recv()
                passed.append(copy(a, 4 + j, (*chip, c), sibling))
                passed[-1].start()
        for a in arrays:
            copy(a, 0, sibling, me).wait_recv()
        for j, chip in enumerate(chips):
            for a in arrays:
                copy(a, 4 + j, (*chip, 1 - c), me).wait_recv()
        for cp in first + passed:
            cp.wait_send()
        for cp in mine:
            cp.wait()

    anyspec = pl.BlockSpec(memory_space=pl.ANY)
    return pl.pallas_call(
        body, name="weights_all_gather",
        out_shape=[jax.ShapeDtypeStruct((N_DEV,) + s.shape, s.dtype) for s in shards],
        in_specs=[anyspec] * n_arr, out_specs=[anyspec] * n_arr,
        scratch_shapes=[pltpu.SemaphoreType.DMA((7 * n_arr,)), pltpu.SemaphoreType.DMA((7 * n_arr,)),
                        pltpu.SemaphoreType.DMA((n_arr,))],
    )(*shards)


def _small_exchange(slabs):
    def body(g_ref, recv_ref, send_sems, recv_sems, local_sem):
        _exchange(g_ref, recv_ref, send_sems, recv_sems, local_sem, True)
        _exchange(g_ref, recv_ref, send_sems, recv_sems, local_sem, False)

    vmem = pl.BlockSpec(memory_space=pltpu.VMEM)
    return pl.pallas_call(
        body, name="small_exchange", out_shape=jax.ShapeDtypeStruct(slabs.shape, slabs.dtype),
        in_specs=[vmem], out_specs=vmem, scratch_shapes=EXCHANGE_SEMS,
    )(slabs)


def _adamw(parts, w, m, v, block_rows, name):
    rows, cols = w.shape

    def body(p_ref, w_ref, m_ref, v_ref, g_out, d_out, m_out, v_out):
        g = p_ref[0].astype(F32)
        for s in range(1, N_DEV):
            g = g + p_ref[s].astype(F32)
        m_new = ADAM_B1 * m_ref[...] + (1.0 - ADAM_B1) * g
        v_new = ADAM_B2 * v_ref[...] + (1.0 - ADAM_B2) * (g * g)
        m_hat = m_new / (1.0 - ADAM_B1 ** ADAM_STEP)
        v_hat = v_new / (1.0 - ADAM_B2 ** ADAM_STEP)
        g_out[...] = g
        d_out[...] = -ADAM_LR * (m_hat / (jnp.sqrt(v_hat) + ADAM_EPS) + ADAM_WD * w_ref[...])
        m_out[...] = m_new
        v_out[...] = v_new

    spec = pl.BlockSpec((block_rows, cols), lambda i: (i, 0))
    return pl.pallas_call(
        body, name=name, grid=(pl.cdiv(rows, block_rows),),
        in_specs=[pl.BlockSpec((N_DEV, block_rows, cols), lambda i: (0, i, 0)), spec, spec, spec],
        out_specs=[spec] * 4, out_shape=[jax.ShapeDtypeStruct((rows, cols), F32)] * 4,
        compiler_params=_cp(("parallel",), 48),
    )(parts, w, m, v)


def _pad_rows(flat, rows):
    pad = rows * LANE - flat.shape[-1]
    flat = jnp.pad(flat, [(0, 0)] * (flat.ndim - 1) + [(0, pad)])
    return flat.reshape(flat.shape[:-1] + (rows, LANE))


def _pack_shards(shards):
    return _pad_rows(jnp.concatenate([shards[n].reshape(-1) for n, _, _ in PACKED]), PACK_ROWS)


def _unpack_shards(packed):
    flat, out, off = packed.reshape(-1), {}, 0
    for n, shape, _ in PACKED:
        size = shape[0] * shape[1]
        out[n] = flat[off:off + size].reshape(shape)
        off += size
    return out


def _split8(full, axis):
    r, c = full.shape
    if axis == 0:
        return full.reshape(N_DEV, r // N_DEV, c)
    return full.reshape(r, N_DEV, c // N_DEV).transpose(1, 0, 2)


def _join8(shards, axis):
    _, r, c = shards.shape
    if axis == 0:
        return shards.reshape(N_DEV * r, c)
    return shards.transpose(1, 0, 2).reshape(r, N_DEV * c)


def _pack_small(meta_shard, vals, loss_row):
    rows = jnp.concatenate([vals[n].reshape(-1, LANE) for n, _ in REPLICATED] + [loss_row], axis=0)
    rows = jnp.pad(rows, ((0, SMALL_ROWS - N_META - rows.shape[0]), (0, 0)))
    return jnp.concatenate([meta_shard, jnp.broadcast_to(rows, meta_shard.shape[:-2] + rows.shape)], axis=-2)


def _unpack_small(packed):
    out, off = {"meta_tokens": packed[:N_META]}, N_META
    for n, size in REPLICATED:
        out[n] = packed[off:off + size // LANE].reshape(1, size)
        off += size // LANE
    return out


def kernel(x, meta_tokens, norm_g, w_in, gla_gate_w, gla_gate_b, gla_norm_g, gla_proj, mla_q_norm_g, mla_w_uq, mla_kv_norm_g, mla_w_ukv, mla_proj, w_out, final_norm_g, loss_target, m_meta_tokens, m_norm_g, m_w_in, m_gla_gate_w, m_gla_gate_b, m_gla_norm_g, m_gla_proj, m_mla_q_norm_g, m_mla_w_uq, m_mla_kv_norm_g, m_mla_w_ukv, m_mla_proj, m_w_out, m_final_norm_g, v_meta_tokens, v_norm_g, v_w_in, v_gla_gate_w, v_gla_gate_b, v_gla_norm_g, v_gla_proj, v_mla_q_norm_g, v_mla_w_uq, v_mla_kv_norm_g, v_mla_w_ukv, v_mla_proj, v_w_out, v_final_norm_g):
    given = dict(meta_tokens=meta_tokens, norm_g=norm_g, w_in=w_in, gla_gate_w=gla_gate_w, gla_gate_b=gla_gate_b,
                 gla_norm_g=gla_norm_g, gla_proj=gla_proj, mla_q_norm_g=mla_q_norm_g, mla_w_uq=mla_w_uq,
                 mla_kv_norm_g=mla_kv_norm_g, mla_w_ukv=mla_w_ukv, mla_proj=mla_proj, w_out=w_out,
                 final_norm_g=final_norm_g)
    mom_m = dict(meta_tokens=m_meta_tokens, norm_g=m_norm_g, w_in=m_w_in, gla_gate_w=m_gla_gate_w,
                 gla_gate_b=m_gla_gate_b, gla_norm_g=m_gla_norm_g, gla_proj=m_gla_proj, mla_q_norm_g=m_mla_q_norm_g,
                 mla_w_uq=m_mla_w_uq, mla_kv_norm_g=m_mla_kv_norm_g, mla_w_ukv=m_mla_w_ukv, mla_proj=m_mla_proj,
                 w_out=m_w_out, final_norm_g=m_final_norm_g)
    mom_v = dict(meta_tokens=v_meta_tokens, norm_g=v_norm_g, w_in=v_w_in, gla_gate_w=v_gla_gate_w,
                 gla_gate_b=v_gla_gate_b, gla_norm_g=v_gla_norm_g, gla_proj=v_gla_proj, mla_q_norm_g=v_mla_q_norm_g,
                 mla_w_uq=v_mla_w_uq, mla_kv_norm_g=v_mla_kv_norm_g, mla_w_ukv=v_mla_w_ukv, mla_proj=v_mla_proj,
                 w_out=v_w_out, final_norm_g=v_final_norm_g)
    shapes = {n: a.shape for n, a in given.items()}
    shard2d = {n: s for n, s, _ in PACKED}
    shard2d["w_in"] = (D_MODEL, W_IN_SHARD)
    shard2d["meta_tokens"] = (N_META, LANE)

    def as2d(tree):
        out = {n: tree[n].reshape(shard2d[n]) for n in shard2d}
        out.update({n: tree[n].reshape(1, size) for n, size in REPLICATED})
        return out

    w_loc, m_loc, v_loc = as2d(given), as2d(mom_m), as2d(mom_v)

    w_in_all, meta_all = _all_gather([w_loc["w_in"].astype(BF16), w_loc["meta_tokens"]])
    flat = jnp.concatenate([w_loc[n].astype(BF16).reshape(-1) for n, _, _ in PACKED])
    full = {"w_in": w_in_all, "meta_tokens": _join8(meta_all, 1), "packed": _pad_rows(flat, PACK_ROWS)}
    for n, _ in REPLICATED:
        full[n] = w_loc[n]

    loss_part, grad_x, w_in_parts, packed_parts, small = _local_step(x, loss_target, full)
    small_all = _small_exchange(_pack_small(_split8(small["meta_tokens"], 1), small,
                                            jnp.broadcast_to(loss_part[:, :1], (1, LANE))))

    w_in_t = [t["w_in"].T for t in (w_loc, m_loc, v_loc)]
    g_w, d_w, m_w, v_w = (o.T for o in _adamw(w_in_parts, *w_in_t, W_IN_BLOCK, "adamw_w_in"))
    g_p, d_p, m_p, v_p = _adamw(packed_parts, _pack_shards(w_loc), _pack_shards(m_loc), _pack_shards(v_loc),
                                PACK_BLOCK, "adamw_packed")
    zero_row = jnp.zeros((1, LANE), F32)
    g_s, d_s, m_s, v_s = _adamw(small_all, *(_pack_small(t["meta_tokens"], t, zero_row) for t in (w_loc, m_loc, v_loc)),
                                SMALL_ROWS, "adamw_small")
    loss = g_s[LOSS_ROW, 0]

    order = ["meta_tokens", "norm_g", "w_in", "gla_gate_w", "gla_gate_b", "gla_norm_g", "gla_proj", "mla_q_norm_g",
             "mla_w_uq", "mla_kv_norm_g", "mla_w_ukv", "mla_proj", "w_out", "final_norm_g"]
    result = [loss, grad_x]
    for w_in_out, packed_sh, packed_sm in ((g_w, g_p, g_s), (d_w, d_p, d_s), (m_w, m_p, m_s), (v_w, v_p, v_s)):
        tree = _unpack_shards(packed_sh)
        tree.update(_unpack_small(packed_sm))
        tree["w_in"] = w_in_out
        result += [tree[n].reshape(shapes[n]) for n in order]
    return tuple(result)
```

```python
import jax
import jax.numpy as jnp
from jax import lax
from jax.experimental import pallas as pl
from jax.experimental.pallas import tpu as pltpu

F32 = jnp.float32
BF16 = jnp.bfloat16

D_MODEL = 1024
N_META = 16
EPS = 1e-6
FRONT = 48
X0 = FRONT + N_META
GLA_HEADS, GLA_DK, GLA_DV, GLA_RANK, GLA_CHUNK = 4, 128, 256, 16, 64
GLA_GATE_NORMALIZER = 16.0
GLA_KW = GLA_HEADS * GLA_DK
GLA_VW = GLA_HEADS * GLA_DV
MLA_HEADS, MLA_NOPE, MLA_ROPE, MLA_DV, MLA_QR, MLA_KVR = 8, 128, 64, 128, 256, 128
MLA_QK = MLA_NOPE + MLA_ROPE
ROPE_BASE = 10000.0
LANE = 128
QKW = 2 * LANE

C_V, C_Z, C_Q, C_K = 0, 1024, 2048, 2560
C_MZ, C_GG, C_GM = 3072, 4096, 5120
C_CKV, C_KR, C_KROT, C_LR = 6144, 6272, 6400, 6528
C_CQ = 6656
N_EXT = 6912
O_Q, O_K, O_V, O_LR, O_Z, O_CQ, O_CKV, O_KR, O_MZ, O_GG, O_GM, N_IN = (
    0, 512, 1024, 2048, 2064, 3088, 3344, 3472, 3536, 4560, 5584, 6608)

ADAM_LR, ADAM_B1, ADAM_B2, ADAM_EPS, ADAM_WD, ADAM_STEP = 0.001, 0.9, 0.999, 1e-08, 0.01, 10

N_DEV = 8
TOK = 192
ATT_BLOCK = 352
EXT_BLOCK = 1152
MXU_DEPTH = 256


def _cp(sems=None, vmem_mb=None):
    kw = {}
    if sems is not None:
        kw["dimension_semantics"] = sems
    if vmem_mb is not None:
        kw["vmem_limit_bytes"] = vmem_mb * 1024 * 1024
    return pltpu.CompilerParams(**kw)


def _dot(a, b):
    return jnp.dot(a, b, preferred_element_type=F32)


def _dot_nt(a, b):
    return lax.dot_general(a, b, (((1,), (1,)), ((), ())), preferred_element_type=F32)


def _dot_tn(a, b):
    return lax.dot_general(a, b, (((0,), (0,)), ((), ())), preferred_element_type=F32)


def _sigmoid(x):
    return 1.0 / (1.0 + jnp.exp(-x))


def _bf(x):
    return x.astype(BF16)


def _big_tok(tp):
    return 4 * TOK if tp % (4 * TOK) == 0 else TOK


def _attn_block(lp):
    return ATT_BLOCK if lp % ATT_BLOCK == 0 else TOK


def _wide_block(lp):
    return 2 * ATT_BLOCK if lp % (2 * ATT_BLOCK) == 0 else _attn_block(lp)


def _proj_in(x, head, norm_g, w_ext, packed):
    bsz, seq, _ = x.shape
    lp = X0 + seq
    tp = bsz * lp
    tm = _attn_block(lp)
    nb = lp // tm
    last = pl.cdiv(seq, tm) - 1

    def body(xa_ref, xb_ref, hd_ref, g_ref, w_ref, p_ref, h_ref, u_ref, o_ref, pall_ref, send_sems, recv_sems, local_sem):
        first = jnp.logical_and(pl.program_id(0) == 0, pl.program_id(1) == 0)

        @pl.when(first)
        def _():
            _exchange(p_ref, pall_ref, send_sems, recv_sems, local_sem, True, same=True)

        front = jnp.where(pl.program_id(1) == 0, hd_ref[...], xa_ref[0, tm - X0:, :])
        h = jnp.concatenate([front, xb_ref[0, :tm - X0, :]], axis=0)
        h_ref[...] = h
        r = lax.rsqrt(jnp.mean(h * h, axis=-1, keepdims=True) + EPS)
        u = _bf(h * r * g_ref[...])
        u_ref[...] = u
        o_ref[...] = _bf(_dot(u, w_ref[...]))

        @pl.when(jnp.logical_and(pl.program_id(0) == bsz - 1, pl.program_id(1) == nb - 1))
        def _():
            _exchange(p_ref, pall_ref, send_sems, recv_sems, local_sem, False, same=True)

    anyspec = pl.BlockSpec(memory_space=pl.ANY)
    tok = lambda width: pl.BlockSpec((tm, width), lambda b, i: (b * nb + i, 0))
    return pl.pallas_call(
        body, name="proj_in", grid=(bsz, nb),
        in_specs=[pl.BlockSpec((1, tm, D_MODEL), lambda b, i: (b, jnp.maximum(i - 1, 0), 0)),
                  pl.BlockSpec((1, tm, D_MODEL), lambda b, i: (b, jnp.minimum(i, last), 0)),
                  pl.BlockSpec((X0, D_MODEL), lambda b, i: (0, 0)),
                  pl.BlockSpec((1, D_MODEL), lambda b, i: (0, 0)),
                  pl.BlockSpec((D_MODEL, N_EXT), lambda b, i: (0, 0), pipeline_mode=pl.Buffered(1)), anyspec],
        out_specs=[tok(D_MODEL), tok(D_MODEL), tok(N_EXT), anyspec],
        out_shape=[jax.ShapeDtypeStruct((tp, D_MODEL), F32), jax.ShapeDtypeStruct((tp, D_MODEL), BF16),
                   jax.ShapeDtypeStruct((tp, N_EXT), BF16),
                   jax.ShapeDtypeStruct((N_DEV,) + packed.shape, packed.dtype)],
        scratch_shapes=EXCHANGE_SEMS,
        compiler_params=_cp(("arbitrary", "arbitrary"), 56),
    )(x, x, head, norm_g, w_ext, packed)


def _gla_group(n_chunks):
    return 11 if n_chunks % 11 == 0 else 3


def _tri_dot(tri, x):
    hi = _bf(x)
    rest = x - hi.astype(F32)
    mid = _bf(rest)
    return _dot(tri, hi) + _dot(tri, mid) + _dot(tri, _bf(rest - mid.astype(F32)))


def _gla_gates(q_ref, k_ref, lr_ref, gw_ref, gb_ref, rows, not_first):
    z = _dot(lr_ref[rows, :], gw_ref[...]) + gb_ref[...]
    logsig = jnp.minimum(z, 0.0) - jnp.log(1.0 + jnp.exp(-jnp.abs(z)))
    row = lax.broadcasted_iota(jnp.int32, (GLA_CHUNK, GLA_KW), 0)
    live = jnp.logical_or(not_first, row >= FRONT)
    g = jnp.where(live, logsig * (1.0 / GLA_GATE_NORMALIZER), 0.0)
    ri = lax.broadcasted_iota(jnp.int32, (GLA_CHUNK, GLA_CHUNK), 0)
    ci = lax.broadcasted_iota(jnp.int32, (GLA_CHUNK, GLA_CHUNK), 1)
    tril = ci <= ri
    b = _tri_dot(_bf(tril.astype(F32)), g)
    bl = jnp.sum(jnp.where(row == GLA_CHUNK - 1, b, 0.0), axis=0, keepdims=True)
    eb, enb, elb, ebl = jnp.exp(b), jnp.exp(-b), jnp.exp(bl - b), jnp.exp(bl)
    q = q_ref[rows, :].astype(F32) * (GLA_DK ** -0.5)
    k = k_ref[rows, :].astype(F32)
    qe, ke, kl = q * eb, k * enb, k * elb
    return dict(z=z, live=live, tril=tril, row=row, eb=eb, enb=enb, elb=elb, ebl=ebl, qe=qe, ke=ke, kl=kl,
                qe_b=_bf(qe), ke_b=_bf(ke), kl_b=_bf(kl))


def _gla_in_specs(n_groups, gla_rows, rev):
    def rb(b, n):
        return b * n_groups + ((n_groups - 1 - n) if rev else n)

    return rb, [pl.BlockSpec((gla_rows, GLA_KW), lambda b, n: (rb(b, n), C_Q // GLA_KW)),
                pl.BlockSpec((gla_rows, GLA_KW), lambda b, n: (rb(b, n), C_K // GLA_KW)),
                pl.BlockSpec((gla_rows, GLA_VW), lambda b, n: (rb(b, n), C_V // GLA_VW)),
                pl.BlockSpec((gla_rows, GLA_VW), lambda b, n: (rb(b, n), C_Z // GLA_VW)),
                pl.BlockSpec((gla_rows, LANE), lambda b, n: (rb(b, n), C_LR // LANE)),
                pl.BlockSpec((LANE, GLA_KW), lambda b, n: (0, 0)),
                pl.BlockSpec((1, GLA_KW), lambda b, n: (0, 0)),
                pl.BlockSpec((1, GLA_DV), lambda b, n: (0, 0))]


def _gla_fwd(proj, gw_pad, gate_b, gla_norm_g, bsz, lp):
    n_chunks = lp // GLA_CHUNK
    gla_group = _gla_group(n_chunks)
    gla_rows = gla_group * GLA_CHUNK
    n_groups = n_chunks // gla_group
    tp = bsz * lp

    def body(q_ref, k_ref, v_ref, z_ref, lr_ref, gw_ref, gb_ref, gn_ref, oraw_ref, ya_ref, sall_ref, st_scr):
        grp = pl.program_id(1)

        @pl.when(grp == 0)
        def _():
            st_scr[...] = jnp.zeros_like(st_scr)

        chunks = [slice(j * GLA_CHUNK, (j + 1) * GLA_CHUNK) for j in range(gla_group)]
        cs = [_gla_gates(q_ref, k_ref, lr_ref, gw_ref, gb_ref, rows, True if j else grp > 0)
              for j, rows in enumerate(chunks)]
        gn = gn_ref[...]
        sts = [st_scr[h] for h in range(GLA_HEADS)]
        heads = [(slice(h * GLA_DK, (h + 1) * GLA_DK), slice(h * GLA_DV, (h + 1) * GLA_DV)) for h in range(GLA_HEADS)]
        a_all = [[_bf(jnp.where(c["tril"], _dot_nt(c["qe_b"][:, ks], c["ke_b"][:, ks]), 0.0)) for ks, _ in heads]
                 for c in cs]
        u_all = [[_dot_tn(v_ref[rows, vs], c["kl_b"][:, ks]) for ks, vs in heads] for rows, c in zip(chunks, cs)]
        for j, (rows, c) in enumerate(zip(chunks, cs)):
            for h, (ks, vs) in enumerate(heads):
                st = sts[h]
                sall_ref[0, j, h] = st
                o = _dot(a_all[j][h], v_ref[rows, vs]) + _dot_nt(c["qe_b"][:, ks], _bf(st))
                sts[h] = st * c["ebl"][:, ks] + u_all[j][h]
                oraw_ref[rows, vs] = o
                r = lax.rsqrt(jnp.mean(o * o, axis=-1, keepdims=True) + EPS)
                zg = z_ref[rows, vs].astype(F32)
                ya_ref[rows, vs] = _bf((o * r * gn) * (zg * _sigmoid(zg)))
        for h in range(GLA_HEADS):
            st_scr[h] = sts[h]

    rb, in_specs = _gla_in_specs(n_groups, gla_rows, False)
    return pl.pallas_call(
        body, name="gla_fwd", grid=(bsz, n_groups), in_specs=in_specs,
        out_specs=[pl.BlockSpec((gla_rows, GLA_VW), lambda b, n: (rb(b, n), 0)),
                   pl.BlockSpec((gla_rows, GLA_VW), lambda b, n: (rb(b, n), 0)),
                   pl.BlockSpec((1, gla_group, GLA_HEADS, GLA_DV, GLA_DK), lambda b, n: (b, n, 0, 0, 0))],
        out_shape=[jax.ShapeDtypeStruct((tp, GLA_VW), F32), jax.ShapeDtypeStruct((tp, GLA_VW), BF16),
                   jax.ShapeDtypeStruct((bsz, n_chunks, GLA_HEADS, GLA_DV, GLA_DK), F32)],
        scratch_shapes=[pltpu.VMEM((GLA_HEADS, GLA_DV, GLA_DK), F32)],
        compiler_params=_cp(("parallel", "arbitrary"), 56),
    )(proj, proj, proj, proj, proj, gw_pad, gate_b, gla_norm_g)


def _gla_bwd(proj, gw_pad, gate_b, gla_norm_g, o_raw, s_all, d_ya, dproj, bsz, lp):
    n_chunks = lp // GLA_CHUNK
    gla_group = _gla_group(n_chunks)
    gla_rows = gla_group * GLA_CHUNK
    n_groups = n_chunks // gla_group
    tp = bsz * lp

    def body(q_ref, k_ref, v_ref, z_ref, lr_ref, gw_ref, gb_ref, gn_ref, o_ref, s_ref, dya_ref, _,
             dp_ref, dz_ref, dgn_ref, dst_scr):
        dv_ref, dzg_ref = dp_ref.at[:, C_V:C_V + GLA_VW], dp_ref.at[:, C_Z:C_Z + GLA_VW]

        @pl.when(jnp.logical_and(pl.program_id(0) == 0, pl.program_id(1) == 0))
        def _():
            dgn_ref[...] = jnp.zeros_like(dgn_ref)

        @pl.when(pl.program_id(1) == 0)
        def _():
            dst_scr[...] = jnp.zeros_like(dst_scr)

        grp = n_groups - 1 - pl.program_id(1)
        chunks = [slice(j * GLA_CHUNK, (j + 1) * GLA_CHUNK) for j in range(gla_group)]
        cs = [_gla_gates(q_ref, k_ref, lr_ref, gw_ref, gb_ref, rows, True if j else grp > 0)
              for j, rows in enumerate(chunks)]
        gn = gn_ref[...]
        dgn = jnp.zeros((1, GLA_DV), F32)
        dqe_h, dke_h, dkl_h, dbl_h = ([[None] * GLA_HEADS for _ in chunks] for _ in range(4))
        dsts = [dst_scr[h] for h in range(GLA_HEADS)]
        for j in reversed(range(gla_group)):
            rows, c = chunks[j], cs[j]
            for h in range(GLA_HEADS):
                ks, vs = slice(h * GLA_DK, (h + 1) * GLA_DK), slice(h * GLA_DV, (h + 1) * GLA_DV)
                dst = dsts[h]
                v = v_ref[rows, vs]
                st = s_ref[0, j, h]
                o = o_ref[rows, vs]
                r = lax.rsqrt(jnp.mean(o * o, axis=-1, keepdims=True) + EPS)
                xh = o * r
                zg = z_ref[rows, vs].astype(F32)
                sg = _sigmoid(zg)
                dy = dya_ref[rows, vs].astype(F32)
                dzg_ref[rows, vs] = _bf(dy * (xh * gn) * (sg * (1.0 + zg * (1.0 - sg))))
                t = dy * (zg * sg)
                dgn += jnp.sum(t * xh, axis=0, keepdims=True)
                dxh = t * gn
                do_b = _bf(r * (dxh - xh * jnp.mean(dxh * xh, axis=-1, keepdims=True)))
                qe_b, ke_b, kl_b, dst_b = c["qe_b"][:, ks], c["ke_b"][:, ks], c["kl_b"][:, ks], _bf(dst)
                a = jnp.where(c["tril"], _dot_nt(qe_b, ke_b), 0.0)
                da_b = _bf(jnp.where(c["tril"], _dot_nt(do_b, v), 0.0))
                dqe_h[j][h] = _dot(da_b, ke_b) + _dot(do_b, _bf(st))
                dke_h[j][h] = _dot_tn(da_b, qe_b)
                dkl = _dot(v, dst_b)
                dkl_h[j][h] = dkl
                dv_ref[rows, vs] = _bf(_dot_tn(_bf(a), do_b) + _dot_nt(kl_b, dst_b))
                ddecay = jnp.sum(dst * st, axis=0, keepdims=True)
                dbl_h[j][h] = jnp.sum(dkl * c["kl"][:, ks], axis=0, keepdims=True) + ddecay * c["ebl"][:, ks]
                dsts[h] = dst * c["ebl"][:, ks] + _dot_tn(do_b, qe_b)
        for h in range(GLA_HEADS):
            dst_scr[h] = dsts[h]
        dgn_ref[...] += dgn
        ri = lax.broadcasted_iota(jnp.int32, (GLA_CHUNK, GLA_CHUNK), 0)
        ci = lax.broadcasted_iota(jnp.int32, (GLA_CHUNK, GLA_CHUNK), 1)
        triu = _bf((ci >= ri).astype(F32))
        for j, (rows, c) in enumerate(zip(chunks, cs)):
            dqe, dke, dkl, dbl = (jnp.concatenate(p[j], axis=1) for p in (dqe_h, dke_h, dkl_h, dbl_h))
            db = dqe * c["qe"] - dke * c["ke"] - dkl * c["kl"] + jnp.where(c["row"] == GLA_CHUNK - 1, dbl, 0.0)
            dg = _tri_dot(triu, db)
            dg = jnp.where(c["live"], dg, 0.0)
            dz_ref[rows, :] = dg * (1.0 / GLA_GATE_NORMALIZER) * _sigmoid(-c["z"])
            dp_ref[rows, C_Q:C_Q + GLA_KW] = _bf(dqe * c["eb"] * (GLA_DK ** -0.5))
            dp_ref[rows, C_K:C_K + GLA_KW] = _bf(dke * c["enb"] + dkl * c["elb"])

    rb, in_specs = _gla_in_specs(n_groups, gla_rows, True)
    wide = pl.BlockSpec((gla_rows, GLA_VW), lambda b, n: (rb(b, n), 0))
    group = C_MZ
    return pl.pallas_call(
        body, name="gla_bwd", grid=(bsz, n_groups),
        in_specs=in_specs + [wide, pl.BlockSpec((1, gla_group, GLA_HEADS, GLA_DV, GLA_DK),
                                                lambda b, n: (b, n_groups - 1 - n, 0, 0, 0)), wide,
                             pl.BlockSpec(memory_space=pl.ANY)],
        out_specs=[pl.BlockSpec((gla_rows, group), lambda b, n: (rb(b, n), 0)),
                   pl.BlockSpec((gla_rows, GLA_KW), lambda b, n: (rb(b, n), 0)),
                   pl.BlockSpec((1, GLA_DV), lambda b, n: (0, 0))],
        out_shape=[jax.ShapeDtypeStruct((tp, N_EXT), BF16), jax.ShapeDtypeStruct((tp, GLA_KW), F32),
                   jax.ShapeDtypeStruct((1, GLA_DV), F32)],
        input_output_aliases={11: 0},
        scratch_shapes=[pltpu.VMEM((GLA_HEADS, GLA_DV, GLA_DK), F32)],
        compiler_params=_cp(("arbitrary", "arbitrary"), 56),
    )(proj, proj, proj, proj, proj, gw_pad, gate_b, gla_norm_g, o_raw, s_all, d_ya, dproj)


def _gate_bwd(dz, proj, gw_pad):
    tp = dz.shape[0]
    tm = _big_tok(tp)

    def body(dz_ref, lr_ref, gw_ref, dlr_ref, dgw_ref, dgb_ref):
        @pl.when(pl.program_id(0) == 0)
        def _():
            dgw_ref[...] = jnp.zeros_like(dgw_ref)
            dgb_ref[...] = jnp.zeros_like(dgb_ref)

        dz = dz_ref[...]
        dz_b = _bf(dz)
        dlr_ref[...] = _bf(_dot_nt(dz_b, gw_ref[...]))
        dgw_ref[...] += _dot_tn(lr_ref[...], dz_b)
        dgb_ref[...] += jnp.sum(dz, axis=0, keepdims=True)

    return pl.pallas_call(
        body, name="gate_bwd", grid=(tp // tm,),
        in_specs=[pl.BlockSpec((tm, GLA_KW), lambda i: (i, 0)),
                  pl.BlockSpec((tm, LANE), lambda i: (i, C_LR // LANE)),
                  pl.BlockSpec((LANE, GLA_KW), lambda i: (0, 0))],
        out_specs=[pl.BlockSpec((tm, LANE), lambda i: (i, 0)),
                   pl.BlockSpec((LANE, GLA_KW), lambda i: (0, 0)),
                   pl.BlockSpec((1, GLA_KW), lambda i: (0, 0))],
        out_shape=[jax.ShapeDtypeStruct((tp, LANE), BF16), jax.ShapeDtypeStruct((LANE, GLA_KW), F32),
                   jax.ShapeDtypeStruct((1, GLA_KW), F32)],
        compiler_params=_cp(("arbitrary",)),
    )(dz, proj, gw_pad)


def _rms_fwd(x):
    r = lax.rsqrt(jnp.mean(x * x, axis=-1, keepdims=True) + EPS)
    return x * r, r


def _rms_bwd(dy, xh, r, g):
    dxh = dy * g
    dx = r * (dxh - xh * jnp.mean(dxh * xh, axis=-1, keepdims=True))
    return dx, jnp.sum(dy * xh, axis=0, keepdims=True)


def _q_up(proj, q_norm_g, wn, wr, wt, cos_t, sin_t, bsz, lp):
    tp = bsz * lp
    tok = _wide_block(lp)
    nb = lp // tok

    def body(cq_ref, g_ref, wn_ref, wr_ref, wt_ref, cos_ref, sin_ref, q_ref):
        xh, _ = _rms_fwd(cq_ref[...].astype(F32))
        cqn = _bf(xh * g_ref[...])
        nope = _dot(cqn, wn_ref[...])
        rope = _dot(cqn, wr_ref[...])
        rot = _dot(cqn, wt_ref[...])
        cos, sin = cos_ref[...], sin_ref[...]
        one = (lax.broadcasted_iota(jnp.int32, (tok, LANE), 1) == BIAS_LANE).astype(F32)
        for h in range(MLA_HEADS):
            sl = slice(h * LANE, (h + 1) * LANE)
            q_ref[:, h * QKW:h * QKW + LANE] = _bf(nope[:, sl])
            q_ref[:, h * QKW + LANE:(h + 1) * QKW] = _bf(rope[:, sl] * cos + rot[:, sl] * sin + one)

    wspec = pl.BlockSpec((MLA_QR, MLA_HEADS * LANE), lambda b, i: (0, 0))
    tspec = pl.BlockSpec((tok, LANE), lambda b, i: (i, 0))
    return pl.pallas_call(
        body, name="mla_q_up", grid=(bsz, nb),
        in_specs=[pl.BlockSpec((tok, MLA_QR), lambda b, i: (b * nb + i, C_CQ // MLA_QR)),
                  pl.BlockSpec((1, MLA_QR), lambda b, i: (0, 0)), wspec, wspec, wspec, tspec, tspec],
        out_specs=pl.BlockSpec((tok, MLA_HEADS * QKW), lambda b, i: (b * nb + i, 0)),
        out_shape=jax.ShapeDtypeStruct((tp, MLA_HEADS * QKW), BF16),
        compiler_params=_cp(("parallel", "parallel")),
    )(proj, q_norm_g, wn, wr, wt, cos_t, sin_t)


def _kv_up(proj, kv_norm_g, wk, wv, cos_t, sin_t, bsz, lp):
    tp = bsz * lp
    tok = _wide_block(lp)
    nb = lp // tok

    def body(ckv_ref, kr_ref, krot_ref, g_ref, wk_ref, wv_ref, cos_ref, sin_ref, k_ref, v_ref):
        xh, _ = _rms_fwd(ckv_ref[...].astype(F32))
        cn = _bf(xh * g_ref[...])
        kn = _dot(cn, wk_ref[...])
        v_ref[...] = _bf(_dot(cn, wv_ref[...]))
        pos = pl.program_id(1) * tok + lax.broadcasted_iota(jnp.int32, (tok, LANE), 0)
        lane = lax.broadcasted_iota(jnp.int32, (tok, LANE), 1)
        bias = jnp.where(jnp.logical_and(lane == BIAS_LANE, pos < FRONT), KEY_BIAS, 0.0)
        kr = _bf(kr_ref[...].astype(F32) * cos_ref[...] + krot_ref[...].astype(F32) * sin_ref[...] + bias)
        for h in range(MLA_HEADS):
            k_ref[:, h * QKW:h * QKW + LANE] = _bf(kn[:, h * LANE:(h + 1) * LANE])
            k_ref[:, h * QKW + LANE:(h + 1) * QKW] = kr

    wspec = pl.BlockSpec((MLA_KVR, MLA_HEADS * LANE), lambda b, i: (0, 0))
    tspec = pl.BlockSpec((tok, LANE), lambda b, i: (i, 0))
    return pl.pallas_call(
        body, name="mla_kv_up", grid=(bsz, nb),
        in_specs=[pl.BlockSpec((tok, LANE), lambda b, i: (b * nb + i, C_CKV // LANE)),
                  pl.BlockSpec((tok, LANE), lambda b, i: (b * nb + i, C_KR // LANE)),
                  pl.BlockSpec((tok, LANE), lambda b, i: (b * nb + i, C_KROT // LANE)),
                  pl.BlockSpec((1, MLA_KVR), lambda b, i: (0, 0)), wspec, wspec, tspec, tspec],
        out_specs=[pl.BlockSpec((tok, MLA_HEADS * QKW), lambda b, i: (b * nb + i, 0)),
                   pl.BlockSpec((tok, MLA_HEADS * LANE), lambda b, i: (b * nb + i, 0))],
        out_shape=[jax.ShapeDtypeStruct((tp, MLA_HEADS * QKW), BF16),
                   jax.ShapeDtypeStruct((tp, MLA_HEADS * LANE), BF16)],
        compiler_params=_cp(("parallel", "parallel")),
    )(proj, proj, proj, kv_norm_g, wk, wv, cos_t, sin_t)


ATT_SCALE = MLA_QK ** -0.5


KEY_BIAS = -1e30
BIAS_LANE = MLA_ROPE
NEG = 2 * KEY_BIAS
LOG2E = 1.4426950408889634
EXP2_SCALE = ATT_SCALE * LOG2E


def _causal_fill(s, r0, fill):
    tq, kmax = s.shape
    a = r0 // LANE * LANE
    mask = (a + lax.broadcasted_iota(jnp.int32, (tq, kmax - a), 1)
            <= r0 + lax.broadcasted_iota(jnp.int32, (tq, kmax - a), 0))
    right = jnp.where(mask, s[:, a:], fill)
    return jnp.concatenate([s[:, :a], right], axis=1) if a else right


def _attn_fwd(qf, kf, vf, proj, bsz, lp):
    tp = bsz * lp
    tq = _attn_block(lp)
    nh = 2

    def body(q_ref, k_ref, v_ref, mz_ref, ob_ref, yb_ref, lse_ref):
        starts = list(range(0, lp, tq))
        for pair in (starts[i:i + 2] for i in range(0, len(starts), 2)):
            work = [(r0, h) for r0 in pair for h in range(nh)]
            ss = [_causal_fill(_dot_nt(q_ref[r0:r0 + tq, h * QKW:(h + 1) * QKW],
                                       k_ref[0:r0 + tq, h * QKW:(h + 1) * QKW]), r0, NEG) for r0, h in work]
            ms = [jnp.max(s, axis=-1, keepdims=True) for s in ss]
            ps = [jnp.exp2((s - m) * EXP2_SCALE) for s, m in zip(ss, ms)]
            ls = [jnp.sum(p, axis=-1, keepdims=True) for p in ps]
            for (r0, h), p, m, l in zip(work, ps, ms, ls):
                rows, cols = slice(r0, r0 + tq), slice(h * MLA_DV, (h + 1) * MLA_DV)
                o = _dot(_bf(p), v_ref[0:r0 + tq, cols]) / l
                ob_ref[rows, cols] = _bf(o)
                mz = mz_ref[rows, cols].astype(F32)
                yb_ref[rows, cols] = _bf(o * (mz * _sigmoid(mz)))
                lse_ref[0, h, rows, :] = jnp.broadcast_to(m * EXP2_SCALE + jnp.log2(l), (tq, LANE))

    head = lambda off: pl.BlockSpec((lp, nh * MLA_DV), lambda b, h: (b, off + h))
    wide = pl.BlockSpec((lp, nh * QKW), lambda b, h: (b, h))
    return pl.pallas_call(
        body, name="mla_attn_fwd", grid=(bsz, MLA_HEADS // nh),
        in_specs=[wide, wide, head(0), head(C_MZ // (nh * MLA_DV))],
        out_specs=[head(0), head(0), pl.BlockSpec((1, nh, lp, LANE), lambda b, h: (b, h, 0, 0))],
        out_shape=[jax.ShapeDtypeStruct((tp, MLA_HEADS * MLA_DV), BF16),
                   jax.ShapeDtypeStruct((tp, MLA_HEADS * MLA_DV), BF16),
                   jax.ShapeDtypeStruct((bsz, MLA_HEADS, lp, LANE), F32)],
        compiler_params=_cp(("parallel", "parallel"), 56),
    )(qf, kf, vf, proj)


def _attn_bwd_blocks(lp):
    return [(0, X0)] + [(r0, min(MXU_DEPTH, lp - r0)) for r0 in range(X0, lp, MXU_DEPTH)]


def _attn_bwd(qf, kf, vf, d_o, lse, delta, bsz, lp):
    tp = bsz * lp

    def body(q_ref, k_ref, v_ref, do_ref, lse_ref, dl_ref, dq_ref, dk_ref, dv_ref, dk_acc, dv_acc):
        dk_acc[...] = jnp.zeros_like(dk_acc)
        dv_acc[...] = jnp.zeros_like(dv_acc)
        for r0, tq in _attn_bwd_blocks(lp):
            rows, kmax = slice(r0, r0 + tq), r0 + tq
            q, do = q_ref[rows, :], do_ref[rows, :]
            k, v = k_ref[0:kmax, :], v_ref[0:kmax, :]
            p = jnp.exp2(_dot_nt(q, k) * EXP2_SCALE - lse_ref[0, 0, rows, :][:, :1])
            p = _causal_fill(p, r0, 0.0)
            ds = _bf(p * (_dot_nt(do, v) - dl_ref[0, rows, :][:, :1]))
            dq_ref[rows, :] = _bf(_dot(ds, k) * ATT_SCALE)
            dk_acc[0:kmax, :] += _dot_tn(ds, q)
            dv_acc[0:kmax, :] += _dot_tn(_bf(p), do)
        dk_ref[...] = _bf(dk_acc[...] * ATT_SCALE)
        dv_ref[...] = _bf(dv_acc[...])

    wide = pl.BlockSpec((lp, QKW), lambda b, h: (b, h))
    narrow = pl.BlockSpec((lp, MLA_DV), lambda b, h: (b, h))
    stat = pl.BlockSpec((1, 1, lp, LANE), lambda b, h: (b, h, 0, 0))
    return pl.pallas_call(
        body, name="mla_attn_bwd", grid=(bsz, MLA_HEADS),
        in_specs=[wide, wide, narrow, narrow, stat, pl.BlockSpec((1, lp, LANE), lambda b, h: (h, b, 0))],
        out_specs=[wide, wide, narrow],
        out_shape=[jax.ShapeDtypeStruct((tp, MLA_HEADS * QKW), BF16), jax.ShapeDtypeStruct((tp, MLA_HEADS * QKW), BF16),
                   jax.ShapeDtypeStruct((tp, MLA_HEADS * MLA_DV), BF16)],
        scratch_shapes=[pltpu.VMEM((lp, QKW), F32), pltpu.VMEM((lp, MLA_DV), F32)],
        compiler_params=_cp(("parallel", "parallel"), 56),
    )(qf, kf, vf, d_o, lse, delta)


def _q_up_bwd(dqf, proj, q_norm_g, wn, wr, wt, cos_t, sin_t, dproj, bsz, lp):
    tp = bsz * lp
    tok = _wide_block(lp)
    nb = lp // tok
    hw = MLA_HEADS * LANE

    def body(dq_ref, cq_ref, g_ref, wn_ref, wr_ref, wt_ref, cos_ref, sin_ref, _,
             dcq_ref, dwn_ref, dwr_ref, dwt_ref, dg_ref):
        @pl.when(jnp.logical_and(pl.program_id(0) == 0, pl.program_id(1) == 0))
        def _():
            for r in (dwn_ref, dwr_ref, dwt_ref, dg_ref):
                r[...] = jnp.zeros_like(r)

        g = g_ref[...]
        xh, r = _rms_fwd(cq_ref[...].astype(F32))
        cqn = _bf(xh * g)
        dn = jnp.concatenate([dq_ref[:, h * QKW:h * QKW + LANE] for h in range(MLA_HEADS)], axis=1)
        dr = jnp.concatenate([dq_ref[:, h * QKW + LANE:(h + 1) * QKW] for h in range(MLA_HEADS)], axis=1).astype(F32)
        dr_c = _bf(dr * jnp.tile(cos_ref[...], (1, MLA_HEADS)))
        dr_s = _bf(dr * jnp.tile(sin_ref[...], (1, MLA_HEADS)))
        dcqn = _dot_nt(dn, wn_ref[...]) + _dot_nt(dr_c, wr_ref[...]) + _dot_nt(dr_s, wt_ref[...])
        dwn_ref[...] += _dot_tn(cqn, dn)
        dwr_ref[...] += _dot_tn(cqn, dr_c)
        dwt_ref[...] += _dot_tn(cqn, dr_s)
        dx, dg = _rms_bwd(dcqn, xh, r, g)
        dcq_ref[...] = _bf(dx)
        dg_ref[...] += dg

    aspec = pl.BlockSpec((MLA_QR, hw), lambda b, i: (0, 0))
    tspec = pl.BlockSpec((tok, LANE), lambda b, i: (i, 0))
    return pl.pallas_call(
        body, name="mla_q_up_bwd", grid=(bsz, nb),
        in_specs=[pl.BlockSpec((tok, MLA_HEADS * QKW), lambda b, i: (b * nb + i, 0)),
                  pl.BlockSpec((tok, MLA_QR), lambda b, i: (b * nb + i, C_CQ // MLA_QR)),
                  pl.BlockSpec((1, MLA_QR), lambda b, i: (0, 0)), aspec, aspec, aspec, tspec, tspec,
                  pl.BlockSpec(memory_space=pl.ANY)],
        out_specs=[pl.BlockSpec((tok, MLA_QR), lambda b, i: (b * nb + i, C_CQ // MLA_QR)), aspec, aspec, aspec,
                   pl.BlockSpec((1, MLA_QR), lambda b, i: (0, 0))],
        out_shape=[jax.ShapeDtypeStruct((tp, N_EXT), BF16)] + [jax.ShapeDtypeStruct((MLA_QR, hw), F32)] * 3
        + [jax.ShapeDtypeStruct((1, MLA_QR), F32)],
        input_output_aliases={8: 0},
        compiler_params=_cp(("arbitrary", "arbitrary")),
    )(dqf, proj, q_norm_g, wn, wr, wt, cos_t, sin_t, dproj)


def _kv_up_bwd(dkf, dvf, proj, kv_norm_g, wk, wv, cos_t, sin_t, d_lr, dproj, bsz, lp):
    tp = bsz * lp
    tok = _wide_block(lp)
    nb = lp // tok
    hw = MLA_HEADS * LANE

    def body(dk_ref, dv_ref, ckv_ref, g_ref, wk_ref, wv_ref, cos_ref, sin_ref, dlr_ref, _,
             dp_ref, dwk_ref, dwv_ref, dg_ref):
        dckv_ref, dkr_ref, dkrot_ref = (dp_ref.at[:, j * LANE:(j + 1) * LANE] for j in range(3))
        dp_ref[:, 3 * LANE:] = dlr_ref[...]
        @pl.when(jnp.logical_and(pl.program_id(0) == 0, pl.program_id(1) == 0))
        def _():
            for r in (dwk_ref, dwv_ref, dg_ref):
                r[...] = jnp.zeros_like(r)

        g = g_ref[...]
        xh, r = _rms_fwd(ckv_ref[...].astype(F32))
        cn = _bf(xh * g)
        dv = dv_ref[...]
        dn = jnp.concatenate([dk_ref[:, h * QKW:h * QKW + LANE] for h in range(MLA_HEADS)], axis=1)
        dcn = _dot_nt(dv, wv_ref[...]) + _dot_nt(dn, wk_ref[...])
        dwv_ref[...] += _dot_tn(cn, dv)
        dwk_ref[...] += _dot_tn(cn, dn)
        drope = jnp.zeros((tok, LANE), F32)
        for h in range(MLA_HEADS):
            drope += dk_ref[:, h * QKW + LANE:(h + 1) * QKW].astype(F32)
        dkr_ref[...] = _bf(drope * cos_ref[...])
        dkrot_ref[...] = _bf(drope * sin_ref[...])
        dx, dg = _rms_bwd(dcn, xh, r, g)
        dckv_ref[...] = _bf(dx)
        dg_ref[...] += dg

    aspec = pl.BlockSpec((MLA_KVR, hw), lambda b, i: (0, 0))
    tspec = pl.BlockSpec((tok, LANE), lambda b, i: (i, 0))
    ospec = pl.BlockSpec((tok, LANE), lambda b, i: (b * nb + i, 0))
    return pl.pallas_call(
        body, name="mla_kv_up_bwd", grid=(bsz, nb),
        in_specs=[pl.BlockSpec((tok, MLA_HEADS * QKW), lambda b, i: (b * nb + i, 0)),
                  pl.BlockSpec((tok, hw), lambda b, i: (b * nb + i, 0)),
                  pl.BlockSpec((tok, LANE), lambda b, i: (b * nb + i, C_CKV // LANE)),
                  pl.BlockSpec((1, MLA_KVR), lambda b, i: (0, 0)), aspec, aspec, tspec, tspec, ospec,
                  pl.BlockSpec(memory_space=pl.ANY)],
        out_specs=[pl.BlockSpec((tok, 4 * LANE), lambda b, i: (b * nb + i, C_CKV // (4 * LANE))), aspec, aspec,
                   pl.BlockSpec((1, MLA_KVR), lambda b, i: (0, 0))],
        out_shape=[jax.ShapeDtypeStruct((tp, N_EXT), BF16)] + [jax.ShapeDtypeStruct((MLA_KVR, hw), F32)] * 2
        + [jax.ShapeDtypeStruct((1, MLA_KVR), F32)],
        input_output_aliases={9: 0},
        compiler_params=_cp(("arbitrary", "arbitrary")),
    )(dkf, dvf, proj, kv_norm_g, wk, wv, cos_t, sin_t, d_lr, dproj)


def _mid_fwd(ya_in, yb_in, proj, hp, target, w_gp, w_mp, w_o, final_g, bsz, lp):
    tp = bsz * lp
    tm = _wide_block(lp)
    nb = lp // tm
    last = pl.cdiv(lp - X0, tm) - 1

    def body(ya_ref, yb_ref, gg_ref, gm_ref, h_ref, ta_ref, tb_ref, wgp_ref, wmp_ref, wo_ref, fg_ref,
             ya_out, yb_out, dh_ref, loss_ref, dfg_ref):
        @pl.when(jnp.logical_and(pl.program_id(0) == 0, pl.program_id(1) == 0))
        def _():
            loss_ref[...] = jnp.zeros_like(loss_ref)
            dfg_ref[...] = jnp.zeros_like(dfg_ref)

        y_a = _dot(ya_ref[...], wgp_ref[...])
        y_b = _dot(yb_ref[...], wmp_ref[...])
        ya_out[...] = _bf(y_a)
        yb_out[...] = _bf(y_b)
        merged = _sigmoid(gg_ref[...].astype(F32)) * y_a + _sigmoid(gm_ref[...].astype(F32)) * y_b
        h2 = h_ref[...] + _dot(_bf(merged), wo_ref[...])
        fg = fg_ref[...]
        xh, r = _rms_fwd(h2)
        pos = pl.program_id(1) * tm + lax.broadcasted_iota(jnp.int32, (tm, 1), 0)
        t = jnp.concatenate([ta_ref[0, tm - X0:, :], tb_ref[0, :tm - X0, :]], axis=0)
        err = jnp.where(pos >= X0, xh * fg - t, 0.0)
        loss_ref[...] += 0.5 * jnp.sum(jnp.mean(err * err, axis=-1, keepdims=True), axis=0, keepdims=True)
        dy = err * (1.0 / D_MODEL)
        dx, dfg = _rms_bwd(dy, xh, r, fg)
        dh_ref[...] = dx
        dfg_ref[...] += dfg

    tok = lambda c: pl.BlockSpec((tm, D_MODEL), lambda b, i: (b * nb + i, c))
    wspec = pl.BlockSpec((D_MODEL, D_MODEL), lambda b, i: (0, 0), pipeline_mode=pl.Buffered(1))
    return pl.pallas_call(
        body, name="mid_fwd", grid=(bsz, nb),
        in_specs=[tok(0), tok(0), tok(C_GG // D_MODEL), tok(C_GM // D_MODEL), tok(0),
                  pl.BlockSpec((1, tm, D_MODEL), lambda b, i: (b, jnp.maximum(i - 1, 0), 0)),
                  pl.BlockSpec((1, tm, D_MODEL), lambda b, i: (b, jnp.minimum(i, last), 0)),
                  wspec, wspec, wspec, pl.BlockSpec((1, D_MODEL), lambda b, i: (0, 0))],
        out_specs=[tok(0), tok(0), tok(0), pl.BlockSpec((1, LANE), lambda b, i: (0, 0)),
                   pl.BlockSpec((1, D_MODEL), lambda b, i: (0, 0))],
        out_shape=[jax.ShapeDtypeStruct((tp, D_MODEL), BF16), jax.ShapeDtypeStruct((tp, D_MODEL), BF16),
                   jax.ShapeDtypeStruct((tp, D_MODEL), F32), jax.ShapeDtypeStruct((1, LANE), F32),
                   jax.ShapeDtypeStruct((1, D_MODEL), F32)],
        compiler_params=_cp(("arbitrary", "arbitrary"), 56),
    )(ya_in, yb_in, proj, proj, hp, target, target, w_gp, w_mp, w_o, final_g)


def _mid_bwd(dh2, y_a, y_b, proj, ya_in, yb_in, o_b, w_o, w_gp, w_mp, bsz, lp):
    tp = bsz * lp
    tm = MXU_DEPTH if tp % MXU_DEPTH == 0 else _attn_block(lp)
    nsteps = tp // tm
    group = 3 * D_MODEL

    def body(dh_ref, ya_ref, yb_ref, mz_ref, gg_ref, gm_ref, yai_ref, ybi_ref, ob_ref, wo_ref, wgp_ref, wmp_ref,
             dyai_ref, do_ref, dp_ref, dl_ref, dwo_ref, dwgp_ref, dwmp_ref, a_o, a_gp, a_mp):
        @pl.when(pl.program_id(0) == 0)
        def _():
            for r in (a_o, a_gp, a_mp):
                r[...] = jnp.zeros_like(r)

        dh = _bf(dh_ref[...])
        dm = _dot_nt(dh, wo_ref[...])
        y_a, y_b = ya_ref[...].astype(F32), yb_ref[...].astype(F32)
        sg, sm = _sigmoid(gg_ref[...].astype(F32)), _sigmoid(gm_ref[...].astype(F32))
        d_ya, d_yb = _bf(sg * dm), _bf(sm * dm)
        dp_ref[:, D_MODEL:2 * D_MODEL] = _bf(dm * y_a * sg * (1.0 - sg))
        dp_ref[:, 2 * D_MODEL:] = _bf(dm * y_b * sm * (1.0 - sm))
        merged = _bf(sg * y_a + sm * y_b)
        dy = _dot_nt(d_yb, wmp_ref[...])
        dyai_ref[...] = _bf(_dot_nt(d_ya, wgp_ref[...]))
        a_o[...] += _dot_tn(merged, dh)
        a_gp[...] += _dot_tn(yai_ref[...], d_ya)
        a_mp[...] += _dot_tn(ybi_ref[...], d_yb)
        mz, o = mz_ref[...].astype(F32), ob_ref[...].astype(F32)
        s = _sigmoid(mz)
        do = _bf(dy * (mz * s))
        do_ref[...] = do
        dp_ref[:, :D_MODEL] = _bf(dy * o * (s * (1.0 + mz * (1.0 - s))))
        prod = do.astype(F32) * o
        for h in range(MLA_HEADS):
            dl = jnp.sum(prod[:, h * MLA_DV:(h + 1) * MLA_DV], axis=-1, keepdims=True)
            dl_ref[h] = jnp.broadcast_to(dl, (tm, LANE))

        @pl.when(pl.program_id(0) == nsteps - 1)
        def _():
            pltpu.sync_copy(a_o, dwo_ref)
            pltpu.sync_copy(a_gp, dwgp_ref)
            pltpu.sync_copy(a_mp, dwmp_ref)

    tok = lambda c: pl.BlockSpec((tm, D_MODEL), lambda i: (i, c))
    wspec = pl.BlockSpec((D_MODEL, D_MODEL), lambda i: (0, 0))
    anyspec = pl.BlockSpec(memory_space=pl.ANY)
    wshape = jax.ShapeDtypeStruct((D_MODEL, D_MODEL), F32)
    return pl.pallas_call(
        body, name="mid_bwd", grid=(nsteps,),
        in_specs=[tok(0), tok(0), tok(0), tok(C_MZ // D_MODEL), tok(C_GG // D_MODEL), tok(C_GM // D_MODEL),
                  tok(0), tok(0), tok(0), wspec, wspec, wspec],
        out_specs=[tok(0), tok(0), pl.BlockSpec((tm, group), lambda i: (i, C_MZ // group)),
                   pl.BlockSpec((MLA_HEADS, tm, LANE), lambda i: (0, i, 0)), anyspec, anyspec, anyspec],
        out_shape=[jax.ShapeDtypeStruct((tp, D_MODEL), BF16)] * 2 + [jax.ShapeDtypeStruct((tp, N_EXT), BF16),
                   jax.ShapeDtypeStruct((MLA_HEADS, tp, LANE), F32)] + [wshape] * 3,
        scratch_shapes=[pltpu.VMEM((D_MODEL, D_MODEL), F32)] * 3,
        compiler_params=_cp(("arbitrary",), 56),
    )(dh2, y_a, y_b, proj, proj, proj, ya_in, yb_in, o_b, w_o, w_gp, w_mp)


MESH_ID = pl.DeviceIdType.MESH
EXCHANGE_SEMS = [pltpu.SemaphoreType.DMA((N_DEV - 1,)), pltpu.SemaphoreType.DMA((N_DEV - 1,)), pltpu.SemaphoreType.DMA]


def _my_place():
    return lax.axis_index("x"), lax.axis_index("y"), lax.axis_index("c")


def _exchange(g_ref, recv_ref, send_sems, recv_sems, local_sem, start, same=False):
    x, y, c = _my_place()
    me = 4 * x + 2 * y + c
    own = pltpu.make_async_copy(g_ref if same else g_ref.at[me], recv_ref.at[me], local_sem)
    sends, lands = [], []
    for d in range(1, N_DEV):
        px = 1 - x if d & 4 else x
        py = 1 - y if d & 2 else y
        pc = 1 - c if d & 1 else c
        peer = 4 * px + 2 * py + pc
        for slot, group in ((me, sends),) if start else ((me, sends), (peer, lands)):
            group.append(pltpu.make_async_remote_copy(
                src_ref=g_ref if same else g_ref.at[peer], dst_ref=recv_ref.at[slot], send_sem=send_sems.at[d - 1],
                recv_sem=recv_sems.at[d - 1], device_id=(px, py, pc), device_id_type=MESH_ID))
    if start:
        own.start()
        for cp in sends:
            cp.start()
    else:
        for cp in lands:
            cp.wait_recv()
        for cp in sends:
            cp.wait_send()
        own.wait()


def _dw_in(u, dproj, slabs):
    tp = u.shape[0]
    tn = 3 * LANE
    nj = N_EXT // tn

    def body(u_ref, d_ref, g_ref, o_ref, recv_ref, send_sems, recv_sems, local_sem):
        j = pl.program_id(0)

        @pl.when(j == 0)
        def _():
            _exchange(g_ref, recv_ref, send_sems, recv_sems, local_sem, True)

        o_ref[...] = _bf(_dot_tn(d_ref[...], u_ref[...]))

        @pl.when(j == nj - 1)
        def _():
            _exchange(g_ref, recv_ref, send_sems, recv_sems, local_sem, False)

    anyspec = pl.BlockSpec(memory_space=pl.ANY)
    return pl.pallas_call(
        body, name="dw_in", grid=(nj,),
        in_specs=[pl.BlockSpec((tp, D_MODEL), lambda j: (0, 0), pipeline_mode=pl.Buffered(1)),
                  pl.BlockSpec((tp, tn), lambda j: (0, j)), anyspec],
        out_specs=[pl.BlockSpec((tn, D_MODEL), lambda j: (j, 0)), anyspec],
        out_shape=[jax.ShapeDtypeStruct((N_EXT, D_MODEL), BF16), jax.ShapeDtypeStruct(slabs.shape, slabs.dtype)],
        scratch_shapes=EXCHANGE_SEMS,
        compiler_params=_cp(("arbitrary",), 56),
    )(u, dproj, slabs)


def _dx_in(dproj, w_ext, hp, dh2, norm_g, slabs):
    tp = hp.shape[0]
    tm = 2 * TOK
    ni = tp // tm

    def body(d_ref, w_ref, h_ref, dh_ref, g_ref, s_ref, o_ref, dg_ref, recv_ref, send_sems, recv_sems, local_sem):
        i = pl.program_id(0)

        @pl.when(i == 0)
        def _():
            _exchange(s_ref, recv_ref, send_sems, recv_sems, local_sem, True)
            dg_ref[...] = jnp.zeros_like(dg_ref)

        du = _dot_nt(d_ref[...], w_ref[...])
        g = g_ref[...]
        xh, r = _rms_fwd(h_ref[...])
        dx, dg = _rms_bwd(du, xh, r, g)
        o_ref[...] = dh_ref[...] + dx
        dg_ref[...] += dg

        @pl.when(i == ni - 1)
        def _():
            _exchange(s_ref, recv_ref, send_sems, recv_sems, local_sem, False)

    tok = pl.BlockSpec((tm, D_MODEL), lambda i: (i, 0))
    anyspec = pl.BlockSpec(memory_space=pl.ANY)
    return pl.pallas_call(
        body, name="dx_in", grid=(ni,),
        in_specs=[pl.BlockSpec((tm, N_EXT), lambda i: (i, 0)),
                  pl.BlockSpec((D_MODEL, N_EXT), lambda i: (0, 0), pipeline_mode=pl.Buffered(1)),
                  tok, tok, pl.BlockSpec((1, D_MODEL), lambda i: (0, 0)), anyspec],
        out_specs=[tok, pl.BlockSpec((1, D_MODEL), lambda i: (0, 0)), anyspec],
        out_shape=[jax.ShapeDtypeStruct((tp, D_MODEL), F32), jax.ShapeDtypeStruct((1, D_MODEL), F32),
                   jax.ShapeDtypeStruct(slabs.shape, slabs.dtype)],
        scratch_shapes=EXCHANGE_SEMS,
        compiler_params=_cp(("arbitrary",), 56),
    )(dproj, w_ext, hp, dh2, norm_g, slabs)


def _meta_grad(dhp3):
    bsz = dhp3.shape[0]

    def body(d_ref, o_ref):
        @pl.when(pl.program_id(0) == 0)
        def _():
            o_ref[...] = jnp.zeros_like(o_ref)

        o_ref[...] += d_ref[0]

    return pl.pallas_call(
        body, name="meta_grad", grid=(bsz,),
        in_specs=[pl.BlockSpec((1, N_META, D_MODEL), lambda b: (b, FRONT // N_META, 0))],
        out_specs=pl.BlockSpec((N_META, D_MODEL), lambda b: (0, 0)),
        out_shape=jax.ShapeDtypeStruct((N_META, D_MODEL), F32),
        compiler_params=_cp(("arbitrary",)),
    )(dhp3)


W_IN_SHARD = N_IN // N_DEV


def _pad_lanes(a, width=LANE):
    return jnp.pad(a, [(0, 0)] * (a.ndim - 1) + [(0, width - a.shape[-1])])


def _rot_cols(w):
    half = w.shape[-1] // 2
    return jnp.concatenate([-w[..., half:], w[..., :half]], axis=-1)


def _unrot_cols(dw):
    half = dw.shape[-1] // 2
    return jnp.concatenate([dw[..., half:], -dw[..., :half]], axis=-1)


def _w_in_cols(shards, lo, hi):
    parts = []
    for k in range(lo // W_IN_SHARD, (hi - 1) // W_IN_SHARD + 1):
        a, b = max(lo, k * W_IN_SHARD), min(hi, (k + 1) * W_IN_SHARD)
        parts.append(shards[k][:, a - k * W_IN_SHARD:b - k * W_IN_SHARD])
    return parts[0] if len(parts) == 1 else jnp.concatenate(parts, axis=1)


def _w_in_ext(shards):
    c = lambda lo, hi: _w_in_cols(shards, lo, hi)
    kr = c(O_KR, O_MZ)
    return jnp.concatenate([
        c(O_V, O_LR), c(O_Z, O_CQ), c(O_Q, O_K), c(O_K, O_V), c(O_MZ, O_GG), c(O_GG, O_GM), c(O_GM, N_IN),
        c(O_CKV, O_KR), _pad_lanes(kr), _pad_lanes(_rot_cols(kr)), _pad_lanes(c(O_LR, O_Z)), c(O_CQ, O_CKV)], axis=1)


def _w_in_grad_t(dwt):
    g = lambda start, width: dwt[start:start + width]
    half = MLA_ROPE // 2
    krot = g(C_KROT, MLA_ROPE)
    kr = g(C_KR, MLA_ROPE) + jnp.concatenate([krot[half:], -krot[:half]], axis=0)
    return jnp.concatenate([
        g(C_Q, GLA_KW), g(C_K, GLA_KW), g(C_V, GLA_VW), g(C_LR, GLA_RANK), g(C_Z, GLA_VW), g(C_CQ, MLA_QR),
        g(C_CKV, MLA_KVR), kr, g(C_MZ, D_MODEL), g(C_GG, D_MODEL), g(C_GM, D_MODEL)], axis=0)


def _rope_tables(lp):
    inv = 1.0 / (ROPE_BASE ** (jnp.arange(0, MLA_ROPE, 2, dtype=F32) / MLA_ROPE))
    ang = (jnp.arange(lp, dtype=F32) - FRONT)[:, None] * inv[None, :]
    cos, sin = jnp.cos(ang), jnp.sin(ang)
    return _pad_lanes(jnp.concatenate([cos, cos], axis=1)), _pad_lanes(jnp.concatenate([sin, sin], axis=1))


def _local_step(x, loss_target, w):
    bsz, seq, _ = x.shape
    lp = X0 + seq
    tp = bsz * lp
    assert lp % TOK == 0 and (lp // GLA_CHUNK) % _gla_group(lp // GLA_CHUNK) == 0
    head = jnp.concatenate([jnp.zeros((FRONT, D_MODEL), F32), w["meta_tokens"]], axis=0)
    cos_t, sin_t = _rope_tables(lp)

    w_ext = _w_in_ext(w["w_in"])
    hp, u, proj, packed_all = _proj_in(x, head, w["norm_g"], w_ext, w["packed"])
    packed_all, off = packed_all.reshape(N_DEV, -1), 0
    for n, shape, axis in PACKED:
        size = shape[0] * shape[1]
        w[n] = _join8(packed_all[:, off:off + size].reshape((N_DEV,) + shape), axis)
        off += size
    gw_pad = jnp.pad(w["gla_gate_w"], ((0, LANE - GLA_RANK), (0, 0)))
    uq = w["mla_w_uq"].reshape(MLA_QR, MLA_HEADS, MLA_QK)
    rope_w = uq[:, :, MLA_NOPE:]
    hw = MLA_HEADS * LANE
    wn = uq[:, :, :MLA_NOPE].reshape(MLA_QR, hw)
    wr = _pad_lanes(rope_w).reshape(MLA_QR, hw)
    wt = _pad_lanes(_rot_cols(rope_w)).reshape(MLA_QR, hw)
    ukv = w["mla_w_ukv"].reshape(MLA_KVR, MLA_HEADS, MLA_NOPE + MLA_DV)
    wk = ukv[:, :, :MLA_NOPE].reshape(MLA_KVR, hw)
    wv = ukv[:, :, MLA_NOPE:].reshape(MLA_KVR, hw)

    o_raw, ya_in, s_all = _gla_fwd(proj, gw_pad, w["gla_gate_b"], w["gla_norm_g"], bsz, lp)
    qf = _q_up(proj, w["mla_q_norm_g"], wn, wr, wt, cos_t, sin_t, bsz, lp)
    kf, vf = _kv_up(proj, w["mla_kv_norm_g"], wk, wv, cos_t, sin_t, bsz, lp)
    o_b, yb_in, lse = _attn_fwd(qf, kf, vf, proj, bsz, lp)
    y_a, y_b, dh2, loss, d_final_g = _mid_fwd(ya_in, yb_in, proj, hp, loss_target, w["gla_proj"], w["mla_proj"],
                                              w["w_out"], w["final_norm_g"], bsz, lp)
    d_ya, d_o, dproj, delta, d_w_out, d_gla_proj, d_mla_proj = _mid_bwd(
        dh2, y_a, y_b, proj, ya_in, yb_in, o_b, w["w_out"], w["gla_proj"], w["mla_proj"], bsz, lp)
    dproj, d_gate, d_gla_norm = _gla_bwd(proj, gw_pad, w["gla_gate_b"], w["gla_norm_g"], o_raw, s_all, d_ya, dproj,
                                         bsz, lp)
    d_lr, d_gw_pad, d_gate_b = _gate_bwd(d_gate, proj, gw_pad)
    dqf, dkf, dvf = _attn_bwd(qf, kf, vf, d_o, lse, delta, bsz, lp)
    dproj, d_wn, d_wr, d_wt, d_qn = _q_up_bwd(dqf, proj, w["mla_q_norm_g"], wn, wr, wt, cos_t, sin_t, dproj,
                                              bsz, lp)
    dproj, d_wk, d_wv, d_kvn = _kv_up_bwd(dkf, dvf, proj, w["mla_kv_norm_g"], wk, wv, cos_t, sin_t, d_lr, dproj,
                                          bsz, lp)

    d_rope = (d_wr.reshape(MLA_QR, MLA_HEADS, LANE)[:, :, :MLA_ROPE]
              + _unrot_cols(d_wt.reshape(MLA_QR, MLA_HEADS, LANE)[:, :, :MLA_ROPE]))
    d_uq = jnp.concatenate([d_wn.reshape(MLA_QR, MLA_HEADS, LANE), d_rope], axis=-1).reshape(MLA_QR, MLA_HEADS * MLA_QK)
    d_ukv = jnp.concatenate([d_wk.reshape(MLA_KVR, MLA_HEADS, LANE), d_wv.reshape(MLA_KVR, MLA_HEADS, LANE)],
                            axis=-1).reshape(MLA_KVR, MLA_HEADS * (MLA_NOPE + MLA_DV))
    mats = dict(gla_gate_w=d_gw_pad[:GLA_RANK], gla_proj=d_gla_proj, mla_w_uq=d_uq, mla_w_ukv=d_ukv,
                mla_proj=d_mla_proj, w_out=d_w_out)
    packed = _pad_rows(jnp.concatenate([_split8(mats[n], axis).reshape(N_DEV, -1) for n, _, axis in PACKED], axis=1),
                       PACK_ROWS)
    d_w_ext_t, packed_parts = _dw_in(u, dproj, _bf(packed))
    w_in_slabs = _w_in_grad_t(d_w_ext_t).reshape(N_DEV, W_IN_SHARD, D_MODEL)
    d_hp, d_norm_g, w_in_parts = _dx_in(dproj, w_ext, hp, dh2, w["norm_g"], w_in_slabs)
    d_hp3 = d_hp.reshape(bsz, lp, D_MODEL)
    small = dict(meta_tokens=_meta_grad(d_hp3), norm_g=d_norm_g, gla_gate_b=d_gate_b, gla_norm_g=d_gla_norm,
                 mla_q_norm_g=d_qn, mla_kv_norm_g=d_kvn, final_norm_g=d_final_g)
    return loss, d_hp3[:, X0:, :], w_in_parts, packed_parts, small


PACKED = (("gla_gate_w", (GLA_RANK, GLA_KW // N_DEV), 1),
          ("gla_proj", (D_MODEL // N_DEV, D_MODEL), 0), ("mla_w_uq", (MLA_QR, MLA_HEADS * MLA_QK // N_DEV), 1),
          ("mla_w_ukv", (MLA_KVR, MLA_HEADS * (MLA_NOPE + MLA_DV) // N_DEV), 1),
          ("mla_proj", (D_MODEL // N_DEV, D_MODEL), 0), ("w_out", (D_MODEL // N_DEV, D_MODEL), 0))
REPLICATED = (("norm_g", D_MODEL), ("gla_gate_b", GLA_KW), ("gla_norm_g", GLA_DV), ("mla_q_norm_g", MLA_QR),
              ("mla_kv_norm_g", MLA_KVR), ("final_norm_g", D_MODEL))
PACK_ROWS = 3744
PACK_BLOCK = 1248
SMALL_ROWS = 48
LOSS_ROW = N_META + 25
W_IN_BLOCK = 128


def _all_gather(shards):
    n_arr = len(shards)
    pieces = []
    for a, s in enumerate(shards):
        step = s.shape[0] // 4 if s.shape[0] >= 4 * LANE else s.shape[0]
        pieces += [(a, slice(r, r + step)) for r in range(0, s.shape[0], step)]
    n_pc = len(pieces)

    def body(*refs):
        x_refs, out_refs = refs[:n_arr], refs[n_arr:2 * n_arr]
        send_sems, recv_sems, local_sems = refs[2 * n_arr:]
        x, y, c = _my_place()
        me, sibling = (x, y, c), (x, y, 1 - c)
        chips = [(1 - x, y), (x, 1 - y), (1 - x, 1 - y)]

        def copy(u, k, block, to, from_input=False):
            a, rows = pieces[u]
            slab = out_refs[a].at[4 * block[0] + 2 * block[1] + block[2], rows]
            return pltpu.make_async_remote_copy(
                src_ref=x_refs[a].at[rows] if from_input else slab, dst_ref=slab,
                send_sem=send_sems.at[7 * u + k], recv_sem=recv_sems.at[7 * u + k], device_id=to,
                device_id_type=MESH_ID)

        arrays = range(n_pc)
        mine = [pltpu.make_async_copy(x_refs[a], out_refs[a].at[4 * x + 2 * y + c], local_sems.at[a])
                for a in range(n_arr)]
        for cp in mine:
            cp.start()
        first = [copy(a, 0, me, sibling, True) for a in arrays]
        first += [copy(a, 1 + j, me, (*chip, c), True) for j, chip in enumerate(chips) for a in arrays]
        for cp in first:
            cp.start()
        passed = []
        for j, chip in enumerate(chips):
            for a in arrays:
                copy(a, 1 + j, (*chip, c), me).wait_recv()
                passed.append(copy(a, 4 + j, (*chip, c), sibling))
                passed[-1].start()
        for a in arrays:
            copy(a, 0, sibling, me).wait_recv()
        for j, chip in enumerate(chips):
            for a in arrays:
                copy(a, 4 + j, (*chip, 1 - c), me).wait_recv()
        for cp in first + passed:
            cp.wait_send()
        for cp in mine:
            cp.wait()

    anyspec = pl.BlockSpec(memory_space=pl.ANY)
    return pl.pallas_call(
        body, name="weights_all_gather",
        out_shape=[jax.ShapeDtypeStruct((N_DEV,) + s.shape, s.dtype) for s in shards],
        in_specs=[anyspec] * n_arr, out_specs=[anyspec] * n_arr,
        scratch_shapes=[pltpu.SemaphoreType.DMA((7 * n_pc,)), pltpu.SemaphoreType.DMA((7 * n_pc,)),
                        pltpu.SemaphoreType.DMA((n_arr,))],
    )(*shards)


def _small_exchange(slabs):
    def body(g_ref, recv_ref, send_sems, recv_sems, local_sem):
        _exchange(g_ref, recv_ref, send_sems, recv_sems, local_sem, True)
        _exchange(g_ref, recv_ref, send_sems, recv_sems, local_sem, False)

    vmem = pl.BlockSpec(memory_space=pltpu.VMEM)
    return pl.pallas_call(
        body, name="small_exchange", out_shape=jax.ShapeDtypeStruct(slabs.shape, slabs.dtype),
        in_specs=[vmem], out_specs=vmem, scratch_shapes=EXCHANGE_SEMS,
    )(slabs)


def _adamw(parts, w, m, v, block_rows, name):
    rows, cols = w.shape

    def body(p_ref, w_ref, m_ref, v_ref, g_out, d_out, m_out, v_out):
        g = p_ref[0].astype(F32)
        for s in range(1, N_DEV):
            g = g + p_ref[s].astype(F32)
        m_new = ADAM_B1 * m_ref[...] + (1.0 - ADAM_B1) * g
        v_new = ADAM_B2 * v_ref[...] + (1.0 - ADAM_B2) * (g * g)
        m_hat = m_new / (1.0 - ADAM_B1 ** ADAM_STEP)
        v_hat = v_new / (1.0 - ADAM_B2 ** ADAM_STEP)
        g_out[...] = g
        d_out[...] = -ADAM_LR * (m_hat / (jnp.sqrt(v_hat) + ADAM_EPS) + ADAM_WD * w_ref[...])
        m_out[...] = m_new
        v_out[...] = v_new

    spec = pl.BlockSpec((block_rows, cols), lambda i: (i, 0))
    return pl.pallas_call(
        body, name=name, grid=(pl.cdiv(rows, block_rows),),
        in_specs=[pl.BlockSpec((N_DEV, block_rows, cols), lambda i: (0, i, 0)), spec, spec, spec],
        out_specs=[spec] * 4, out_shape=[jax.ShapeDtypeStruct((rows, cols), F32)] * 4,
        compiler_params=_cp(("parallel",), 48),
    )(parts, w, m, v)


def _pad_rows(flat, rows):
    pad = rows * LANE - flat.shape[-1]
    flat = jnp.pad(flat, [(0, 0)] * (flat.ndim - 1) + [(0, pad)])
    return flat.reshape(flat.shape[:-1] + (rows, LANE))


def _pack_shards(shards):
    return _pad_rows(jnp.concatenate([shards[n].reshape(-1) for n, _, _ in PACKED]), PACK_ROWS)


def _unpack_shards(packed):
    flat, out, off = packed.reshape(-1), {}, 0
    for n, shape, _ in PACKED:
        size = shape[0] * shape[1]
        out[n] = flat[off:off + size].reshape(shape)
        off += size
    return out


def _split8(full, axis):
    r, c = full.shape
    if axis == 0:
        return full.reshape(N_DEV, r // N_DEV, c)
    return full.reshape(r, N_DEV, c // N_DEV).transpose(1, 0, 2)


def _join8(shards, axis):
    _, r, c = shards.shape
    if axis == 0:
        return shards.reshape(N_DEV * r, c)
    return shards.transpose(1, 0, 2).reshape(r, N_DEV * c)


def _pack_small(meta_shard, vals, loss_row):
    rows = jnp.concatenate([vals[n].reshape(-1, LANE) for n, _ in REPLICATED] + [loss_row], axis=0)
    rows = jnp.pad(rows, ((0, SMALL_ROWS - N_META - rows.shape[0]), (0, 0)))
    return jnp.concatenate([meta_shard, jnp.broadcast_to(rows, meta_shard.shape[:-2] + rows.shape)], axis=-2)


def _unpack_small(packed):
    out, off = {"meta_tokens": packed[:N_META]}, N_META
    for n, size in REPLICATED:
        out[n] = packed[off:off + size // LANE].reshape(1, size)
        off += size // LANE
    return out


def kernel(x, meta_tokens, norm_g, w_in, gla_gate_w, gla_gate_b, gla_norm_g, gla_proj, mla_q_norm_g, mla_w_uq, mla_kv_norm_g, mla_w_ukv, mla_proj, w_out, final_norm_g, loss_target, m_meta_tokens, m_norm_g, m_w_in, m_gla_gate_w, m_gla_gate_b, m_gla_norm_g, m_gla_proj, m_mla_q_norm_g, m_mla_w_uq, m_mla_kv_norm_g, m_mla_w_ukv, m_mla_proj, m_w_out, m_final_norm_g, v_meta_tokens, v_norm_g, v_w_in, v_gla_gate_w, v_gla_gate_b, v_gla_norm_g, v_gla_proj, v_mla_q_norm_g, v_mla_w_uq, v_mla_kv_norm_g, v_mla_w_ukv, v_mla_proj, v_w_out, v_final_norm_g):
    given = dict(meta_tokens=meta_tokens, norm_g=norm_g, w_in=w_in, gla_gate_w=gla_gate_w, gla_gate_b=gla_gate_b,
                 gla_norm_g=gla_norm_g, gla_proj=gla_proj, mla_q_norm_g=mla_q_norm_g, mla_w_uq=mla_w_uq,
                 mla_kv_norm_g=mla_kv_norm_g, mla_w_ukv=mla_w_ukv, mla_proj=mla_proj, w_out=w_out,
                 final_norm_g=final_norm_g)
    mom_m = dict(meta_tokens=m_meta_tokens, norm_g=m_norm_g, w_in=m_w_in, gla_gate_w=m_gla_gate_w,
                 gla_gate_b=m_gla_gate_b, gla_norm_g=m_gla_norm_g, gla_proj=m_gla_proj, mla_q_norm_g=m_mla_q_norm_g,
                 mla_w_uq=m_mla_w_uq, mla_kv_norm_g=m_mla_kv_norm_g, mla_w_ukv=m_mla_w_ukv, mla_proj=m_mla_proj,
                 w_out=m_w_out, final_norm_g=m_final_norm_g)
    mom_v = dict(meta_tokens=v_meta_tokens, norm_g=v_norm_g, w_in=v_w_in, gla_gate_w=v_gla_gate_w,
                 gla_gate_b=v_gla_gate_b, gla_norm_g=v_gla_norm_g, gla_proj=v_gla_proj, mla_q_norm_g=v_mla_q_norm_g,
                 mla_w_uq=v_mla_w_uq, mla_kv_norm_g=v_mla_kv_norm_g, mla_w_ukv=v_mla_w_ukv, mla_proj=v_mla_proj,
                 w_out=v_w_out, final_norm_g=v_final_norm_g)
    shapes = {n: a.shape for n, a in given.items()}
    shard2d = {n: s for n, s, _ in PACKED}
    shard2d["w_in"] = (D_MODEL, W_IN_SHARD)
    shard2d["meta_tokens"] = (N_META, LANE)

    def as2d(tree):
        out = {n: tree[n].reshape(shard2d[n]) for n in shard2d}
        out.update({n: tree[n].reshape(1, size) for n, size in REPLICATED})
        return out

    w_loc, m_loc, v_loc = as2d(given), as2d(mom_m), as2d(mom_v)

    w_in_all, meta_all = _all_gather([w_loc["w_in"].astype(BF16), w_loc["meta_tokens"]])
    flat = jnp.concatenate([w_loc[n].astype(BF16).reshape(-1) for n, _, _ in PACKED])
    full = {"w_in": w_in_all, "meta_tokens": _join8(meta_all, 1), "packed": _pad_rows(flat, PACK_ROWS)}
    for n, _ in REPLICATED:
        full[n] = w_loc[n]

    loss_part, grad_x, w_in_parts, packed_parts, small = _local_step(x, loss_target, full)
    small_all = _small_exchange(_pack_small(_split8(small["meta_tokens"], 1), small,
                                            jnp.broadcast_to(loss_part[:, :1], (1, LANE))))

    w_in_t = [t["w_in"].T for t in (w_loc, m_loc, v_loc)]
    g_w, d_w, m_w, v_w = (o.T for o in _adamw(w_in_parts, *w_in_t, W_IN_BLOCK, "adamw_w_in"))
    g_p, d_p, m_p, v_p = _adamw(packed_parts, _pack_shards(w_loc), _pack_shards(m_loc), _pack_shards(v_loc),
                                PACK_BLOCK, "adamw_packed")
    zero_row = jnp.zeros((1, LANE), F32)
    g_s, d_s, m_s, v_s = _adamw(small_all, *(_pack_small(t["meta_tokens"], t, zero_row) for t in (w_loc, m_loc, v_loc)),
                                SMALL_ROWS, "adamw_small")
    loss = g_s[LOSS_ROW, 0]

    order = ["meta_tokens", "norm_g", "w_in", "gla_gate_w", "gla_gate_b", "gla_norm_g", "gla_proj", "mla_q_norm_g",
             "mla_w_uq", "mla_kv_norm_g", "mla_w_ukv", "mla_proj", "w_out", "final_norm_g"]
    result = [loss, grad_x]
    for w_in_out, packed_sh, packed_sm in ((g_w, g_p, g_s), (d_w, d_p, d_s), (m_w, m_p, m_s), (v_w, v_p, v_s)):
        tree = _unpack_shards(packed_sh)
        tree.update(_unpack_small(packed_sm))
        tree["w_in"] = w_in_out
        result += [tree[n].reshape(shapes[n]) for n in order]
    return tuple(result)
```

```python
import jax
import jax.numpy as jnp
from jax import lax
from jax.experimental import pallas as pl
from jax.experimental.pallas import tpu as pltpu

F32 = jnp.float32
BF16 = jnp.bfloat16

D_MODEL = 1024
N_META = 16
EPS = 1e-6
FRONT = 48
X0 = FRONT + N_META
GLA_HEADS, GLA_DK, GLA_DV, GLA_RANK, GLA_CHUNK = 4, 128, 256, 16, 64
GLA_GATE_NORMALIZER = 16.0
GLA_KW = GLA_HEADS * GLA_DK
GLA_VW = GLA_HEADS * GLA_DV
MLA_HEADS, MLA_NOPE, MLA_ROPE, MLA_DV, MLA_QR, MLA_KVR = 8, 128, 64, 128, 256, 128
MLA_QK = MLA_NOPE + MLA_ROPE
ROPE_BASE = 10000.0
LANE = 128
QKW = 2 * LANE

C_V, C_Z, C_Q, C_K = 0, 1024, 2048, 2560
C_MZ, C_GG, C_GM = 3072, 4096, 5120
C_CKV, C_KR, C_KROT, C_LR = 6144, 6272, 6400, 6528
C_CQ = 6656
N_EXT = 6912
O_Q, O_K, O_V, O_LR, O_Z, O_CQ, O_CKV, O_KR, O_MZ, O_GG, O_GM, N_IN = (
    0, 512, 1024, 2048, 2064, 3088, 3344, 3472, 3536, 4560, 5584, 6608)

ADAM_LR, ADAM_B1, ADAM_B2, ADAM_EPS, ADAM_WD, ADAM_STEP = 0.001, 0.9, 0.999, 1e-08, 0.01, 10

N_DEV = 8
TOK = 192
ATT_BLOCK = 352
MXU_DEPTH = 256


def _cp(sems=None, vmem_mb=None):
    kw = {}
    if sems is not None:
        kw["dimension_semantics"] = sems
    if vmem_mb is not None:
        kw["vmem_limit_bytes"] = vmem_mb * 1024 * 1024
    return pltpu.CompilerParams(**kw)


def _dot(a, b):
    return jnp.dot(a, b, preferred_element_type=F32)


def _dot_nt(a, b):
    return lax.dot_general(a, b, (((1,), (1,)), ((), ())), preferred_element_type=F32)


def _dot_tn(a, b):
    return lax.dot_general(a, b, (((0,), (0,)), ((), ())), preferred_element_type=F32)


def _sigmoid(x):
    return 1.0 / (1.0 + jnp.exp(-x))


def _bf(x):
    return x.astype(BF16)


def _attn_block(lp):
    return ATT_BLOCK if lp % ATT_BLOCK == 0 else TOK


def _wide_block(lp):
    return 2 * ATT_BLOCK if lp % (2 * ATT_BLOCK) == 0 else _attn_block(lp)


def _proj_in(x, head, norm_g, w_ext, packed):
    bsz, seq, _ = x.shape
    lp = X0 + seq
    tp = bsz * lp
    tm = _attn_block(lp)
    nb = lp // tm
    last = pl.cdiv(seq, tm) - 1

    def body(xa_ref, xb_ref, hd_ref, g_ref, w_ref, p_ref, h_ref, u_ref, o_ref, pall_ref, send_sems, recv_sems, local_sem):
        first = jnp.logical_and(pl.program_id(0) == 0, pl.program_id(1) == 0)

        @pl.when(first)
        def _():
            _exchange(p_ref, pall_ref, send_sems, recv_sems, local_sem, True, same=True)

        front = jnp.where(pl.program_id(1) == 0, hd_ref[...], xa_ref[0, tm - X0:, :])
        h = jnp.concatenate([front, xb_ref[0, :tm - X0, :]], axis=0)
        h_ref[...] = h
        r = lax.rsqrt(jnp.mean(h * h, axis=-1, keepdims=True) + EPS)
        u = _bf(h * r * g_ref[...])
        u_ref[...] = u
        o_ref[...] = _bf(_dot(u, w_ref[...]))

        @pl.when(jnp.logical_and(pl.program_id(0) == bsz - 1, pl.program_id(1) == nb - 1))
        def _():
            _exchange(p_ref, pall_ref, send_sems, recv_sems, local_sem, False, same=True)

    anyspec = pl.BlockSpec(memory_space=pl.ANY)
    tok = lambda width: pl.BlockSpec((tm, width), lambda b, i: (b * nb + i, 0))
    return pl.pallas_call(
        body, name="proj_in", grid=(bsz, nb),
        in_specs=[pl.BlockSpec((1, tm, D_MODEL), lambda b, i: (b, jnp.maximum(i - 1, 0), 0)),
                  pl.BlockSpec((1, tm, D_MODEL), lambda b, i: (b, jnp.minimum(i, last), 0)),
                  pl.BlockSpec((X0, D_MODEL), lambda b, i: (0, 0)),
                  pl.BlockSpec((1, D_MODEL), lambda b, i: (0, 0)),
                  pl.BlockSpec((D_MODEL, N_EXT), lambda b, i: (0, 0), pipeline_mode=pl.Buffered(1)), anyspec],
        out_specs=[tok(D_MODEL), tok(D_MODEL), tok(N_EXT), anyspec],
        out_shape=[jax.ShapeDtypeStruct((tp, D_MODEL), F32), jax.ShapeDtypeStruct((tp, D_MODEL), BF16),
                   jax.ShapeDtypeStruct((tp, N_EXT), BF16),
                   jax.ShapeDtypeStruct((N_DEV,) + packed.shape, packed.dtype)],
        scratch_shapes=EXCHANGE_SEMS,
        compiler_params=_cp(("arbitrary", "arbitrary"), 56),
    )(x, x, head, norm_g, w_ext, packed)


def _gla_group(n_chunks):
    return 11 if n_chunks % 11 == 0 else 3


def _tri_dot(tri, x):
    hi = _bf(x)
    rest = x - hi.astype(F32)
    mid = _bf(rest)
    return _dot(tri, hi) + _dot(tri, mid) + _dot(tri, _bf(rest - mid.astype(F32)))


def _gla_gates(q_ref, k_ref, lr_ref, gw_ref, gb_ref, rows, not_first):
    z = _dot(lr_ref[rows, :], gw_ref[...]) + gb_ref[...]
    logsig = jnp.minimum(z, 0.0) - jnp.log(1.0 + jnp.exp(-jnp.abs(z)))
    row = lax.broadcasted_iota(jnp.int32, (GLA_CHUNK, GLA_KW), 0)
    live = jnp.logical_or(not_first, row >= FRONT)
    g = jnp.where(live, logsig * (1.0 / GLA_GATE_NORMALIZER), 0.0)
    ri = lax.broadcasted_iota(jnp.int32, (GLA_CHUNK, GLA_CHUNK), 0)
    ci = lax.broadcasted_iota(jnp.int32, (GLA_CHUNK, GLA_CHUNK), 1)
    tril = ci <= ri
    b = _tri_dot(_bf(tril.astype(F32)), g)
    bl = jnp.sum(jnp.where(row == GLA_CHUNK - 1, b, 0.0), axis=0, keepdims=True)
    eb, enb, elb, ebl = jnp.exp(b), jnp.exp(-b), jnp.exp(bl - b), jnp.exp(bl)
    q = q_ref[rows, :].astype(F32) * (GLA_DK ** -0.5)
    k = k_ref[rows, :].astype(F32)
    qe, ke, kl = q * eb, k * enb, k * elb
    return dict(z=z, live=live, tril=tril, row=row, eb=eb, enb=enb, elb=elb, ebl=ebl, qe=qe, ke=ke, kl=kl,
                qe_b=_bf(qe), ke_b=_bf(ke), kl_b=_bf(kl))


def _gla_in_specs(n_groups, gla_rows, rev):
    def rb(b, n):
        return b * n_groups + ((n_groups - 1 - n) if rev else n)

    return rb, [pl.BlockSpec((gla_rows, GLA_KW), lambda b, n: (rb(b, n), C_Q // GLA_KW)),
                pl.BlockSpec((gla_rows, GLA_KW), lambda b, n: (rb(b, n), C_K // GLA_KW)),
                pl.BlockSpec((gla_rows, GLA_VW), lambda b, n: (rb(b, n), C_V // GLA_VW)),
                pl.BlockSpec((gla_rows, GLA_VW), lambda b, n: (rb(b, n), C_Z // GLA_VW)),
                pl.BlockSpec((gla_rows, LANE), lambda b, n: (rb(b, n), C_LR // LANE)),
                pl.BlockSpec((LANE, GLA_KW), lambda b, n: (0, 0)),
                pl.BlockSpec((1, GLA_KW), lambda b, n: (0, 0)),
                pl.BlockSpec((1, GLA_DV), lambda b, n: (0, 0))]


def _gla_fwd(proj, gw_pad, gate_b, gla_norm_g, bsz, lp):
    n_chunks = lp // GLA_CHUNK
    gla_group = _gla_group(n_chunks)
    gla_rows = gla_group * GLA_CHUNK
    n_groups = n_chunks // gla_group
    tp = bsz * lp

    def body(q_ref, k_ref, v_ref, z_ref, lr_ref, gw_ref, gb_ref, gn_ref, oraw_ref, ya_ref, sall_ref, st_scr):
        grp = pl.program_id(1)

        @pl.when(grp == 0)
        def _():
            st_scr[...] = jnp.zeros_like(st_scr)

        chunks = [slice(j * GLA_CHUNK, (j + 1) * GLA_CHUNK) for j in range(gla_group)]
        cs = [_gla_gates(q_ref, k_ref, lr_ref, gw_ref, gb_ref, rows, True if j else grp > 0)
              for j, rows in enumerate(chunks)]
        gn = gn_ref[...]
        sts = [st_scr[h] for h in range(GLA_HEADS)]
        heads = [(slice(h * GLA_DK, (h + 1) * GLA_DK), slice(h * GLA_DV, (h + 1) * GLA_DV)) for h in range(GLA_HEADS)]
        a_all = [[_bf(jnp.where(c["tril"], _dot_nt(c["qe_b"][:, ks], c["ke_b"][:, ks]), 0.0)) for ks, _ in heads]
                 for c in cs]
        u_all = [[_dot_tn(v_ref[rows, vs], c["kl_b"][:, ks]) for ks, vs in heads] for rows, c in zip(chunks, cs)]
        for j, (rows, c) in enumerate(zip(chunks, cs)):
            for h, (ks, vs) in enumerate(heads):
                st = sts[h]
                sall_ref[0, j, h] = st
                o = _dot(a_all[j][h], v_ref[rows, vs]) + _dot_nt(c["qe_b"][:, ks], _bf(st))
                sts[h] = st * c["ebl"][:, ks] + u_all[j][h]
                oraw_ref[rows, vs] = o
                r = lax.rsqrt(jnp.mean(o * o, axis=-1, keepdims=True) + EPS)
                zg = z_ref[rows, vs].astype(F32)
                ya_ref[rows, vs] = _bf((o * r * gn) * (zg * _sigmoid(zg)))
        for h in range(GLA_HEADS):
            st_scr[h] = sts[h]

    rb, in_specs = _gla_in_specs(n_groups, gla_rows, False)
    return pl.pallas_call(
        body, name="gla_fwd", grid=(bsz, n_groups), in_specs=in_specs,
        out_specs=[pl.BlockSpec((gla_rows, GLA_VW), lambda b, n: (rb(b, n), 0)),
                   pl.BlockSpec((gla_rows, GLA_VW), lambda b, n: (rb(b, n), 0)),
                   pl.BlockSpec((1, gla_group, GLA_HEADS, GLA_DV, GLA_DK), lambda b, n: (b, n, 0, 0, 0))],
        out_shape=[jax.ShapeDtypeStruct((tp, GLA_VW), F32), jax.ShapeDtypeStruct((tp, GLA_VW), BF16),
                   jax.ShapeDtypeStruct((bsz, n_chunks, GLA_HEADS, GLA_DV, GLA_DK), F32)],
        scratch_shapes=[pltpu.VMEM((GLA_HEADS, GLA_DV, GLA_DK), F32)],
        compiler_params=_cp(("parallel", "arbitrary"), 56),
    )(proj, proj, proj, proj, proj, gw_pad, gate_b, gla_norm_g)


def _gla_bwd(proj, gw_pad, gate_b, gla_norm_g, o_raw, s_all, d_ya, dproj, bsz, lp):
    n_chunks = lp // GLA_CHUNK
    gla_group = _gla_group(n_chunks)
    gla_rows = gla_group * GLA_CHUNK
    n_groups = n_chunks // gla_group
    tp = bsz * lp

    def body(q_ref, k_ref, v_ref, z_ref, lr_ref, gw_ref, gb_ref, gn_ref, o_ref, s_ref, dya_ref, _,
             dp_ref, dlr_ref, dgw_ref, dgb_ref, dgn_ref, dst_scr):
        dv_ref, dzg_ref = dp_ref.at[:, C_V:C_V + GLA_VW], dp_ref.at[:, C_Z:C_Z + GLA_VW]

        @pl.when(jnp.logical_and(pl.program_id(0) == 0, pl.program_id(1) == 0))
        def _():
            for r in (dgw_ref, dgb_ref, dgn_ref):
                r[...] = jnp.zeros_like(r)

        @pl.when(pl.program_id(1) == 0)
        def _():
            dst_scr[...] = jnp.zeros_like(dst_scr)

        grp = n_groups - 1 - pl.program_id(1)
        chunks = [slice(j * GLA_CHUNK, (j + 1) * GLA_CHUNK) for j in range(gla_group)]
        cs = [_gla_gates(q_ref, k_ref, lr_ref, gw_ref, gb_ref, rows, True if j else grp > 0)
              for j, rows in enumerate(chunks)]
        gn = gn_ref[...]
        dgn = jnp.zeros((1, GLA_DV), F32)
        dqe_h, dke_h, dkl_h, dbl_h = ([[None] * GLA_HEADS for _ in chunks] for _ in range(4))
        dsts = [dst_scr[h] for h in range(GLA_HEADS)]
        for j in reversed(range(gla_group)):
            rows, c = chunks[j], cs[j]
            for h in range(GLA_HEADS):
                ks, vs = slice(h * GLA_DK, (h + 1) * GLA_DK), slice(h * GLA_DV, (h + 1) * GLA_DV)
                dst = dsts[h]
                v = v_ref[rows, vs]
                st = s_ref[0, j, h]
                o = o_ref[rows, vs]
                r = lax.rsqrt(jnp.mean(o * o, axis=-1, keepdims=True) + EPS)
                xh = o * r
                zg = z_ref[rows, vs].astype(F32)
                sg = _sigmoid(zg)
                dy = dya_ref[rows, vs].astype(F32)
                dzg_ref[rows, vs] = _bf(dy * (xh * gn) * (sg * (1.0 + zg * (1.0 - sg))))
                t = dy * (zg * sg)
                dgn += jnp.sum(t * xh, axis=0, keepdims=True)
                dxh = t * gn
                do_b = _bf(r * (dxh - xh * jnp.mean(dxh * xh, axis=-1, keepdims=True)))
                qe_b, ke_b, kl_b, dst_b = c["qe_b"][:, ks], c["ke_b"][:, ks], c["kl_b"][:, ks], _bf(dst)
                a = jnp.where(c["tril"], _dot_nt(qe_b, ke_b), 0.0)
                da_b = _bf(jnp.where(c["tril"], _dot_nt(do_b, v), 0.0))
                dqe_h[j][h] = _dot(da_b, ke_b) + _dot(do_b, _bf(st))
                dke_h[j][h] = _dot_tn(da_b, qe_b)
                dkl = _dot(v, dst_b)
                dkl_h[j][h] = dkl
                dv_ref[rows, vs] = _bf(_dot_tn(_bf(a), do_b) + _dot_nt(kl_b, dst_b))
                ddecay = jnp.sum(dst * st, axis=0, keepdims=True)
                dbl_h[j][h] = jnp.sum(dkl * c["kl"][:, ks], axis=0, keepdims=True) + ddecay * c["ebl"][:, ks]
                dsts[h] = dst * c["ebl"][:, ks] + _dot_tn(do_b, qe_b)
        for h in range(GLA_HEADS):
            dst_scr[h] = dsts[h]
        dgn_ref[...] += dgn
        ri = lax.broadcasted_iota(jnp.int32, (GLA_CHUNK, GLA_CHUNK), 0)
        ci = lax.broadcasted_iota(jnp.int32, (GLA_CHUNK, GLA_CHUNK), 1)
        triu = _bf((ci >= ri).astype(F32))
        for j, (rows, c) in enumerate(zip(chunks, cs)):
            dqe, dke, dkl, dbl = (jnp.concatenate(p[j], axis=1) for p in (dqe_h, dke_h, dkl_h, dbl_h))
            db = dqe * c["qe"] - dke * c["ke"] - dkl * c["kl"] + jnp.where(c["row"] == GLA_CHUNK - 1, dbl, 0.0)
            dg = _tri_dot(triu, db)
            dg = jnp.where(c["live"], dg, 0.0)
            dz = dg * (1.0 / GLA_GATE_NORMALIZER) * _sigmoid(-c["z"])
            dz_b = _bf(dz)
            dlr_ref[rows, :] = _bf(_dot_nt(dz_b, gw_ref[...]))
            dgw_ref[...] += _dot_tn(lr_ref[rows, :], dz_b)
            dgb_ref[...] += jnp.sum(dz, axis=0, keepdims=True)
            dp_ref[rows, C_Q:C_Q + GLA_KW] = _bf(dqe * c["eb"] * (GLA_DK ** -0.5))
            dp_ref[rows, C_K:C_K + GLA_KW] = _bf(dke * c["enb"] + dkl * c["elb"])

    rb, in_specs = _gla_in_specs(n_groups, gla_rows, True)
    wide = pl.BlockSpec((gla_rows, GLA_VW), lambda b, n: (rb(b, n), 0))
    group = C_MZ
    return pl.pallas_call(
        body, name="gla_bwd", grid=(bsz, n_groups),
        in_specs=in_specs + [wide, pl.BlockSpec((1, gla_group, GLA_HEADS, GLA_DV, GLA_DK),
                                                lambda b, n: (b, n_groups - 1 - n, 0, 0, 0)), wide,
                             pl.BlockSpec(memory_space=pl.ANY)],
        out_specs=[pl.BlockSpec((gla_rows, group), lambda b, n: (rb(b, n), 0)),
                   pl.BlockSpec((gla_rows, LANE), lambda b, n: (rb(b, n), 0)),
                   pl.BlockSpec((LANE, GLA_KW), lambda b, n: (0, 0)),
                   pl.BlockSpec((1, GLA_KW), lambda b, n: (0, 0)),
                   pl.BlockSpec((1, GLA_DV), lambda b, n: (0, 0))],
        out_shape=[jax.ShapeDtypeStruct((tp, N_EXT), BF16), jax.ShapeDtypeStruct((tp, LANE), BF16),
                   jax.ShapeDtypeStruct((LANE, GLA_KW), F32), jax.ShapeDtypeStruct((1, GLA_KW), F32),
                   jax.ShapeDtypeStruct((1, GLA_DV), F32)],
        input_output_aliases={11: 0},
        scratch_shapes=[pltpu.VMEM((GLA_HEADS, GLA_DV, GLA_DK), F32)],
        compiler_params=_cp(("arbitrary", "arbitrary"), 56),
    )(proj, proj, proj, proj, proj, gw_pad, gate_b, gla_norm_g, o_raw, s_all, d_ya, dproj)


def _rms_fwd(x):
    r = lax.rsqrt(jnp.mean(x * x, axis=-1, keepdims=True) + EPS)
    return x * r, r


def _rms_bwd(dy, xh, r, g):
    dxh = dy * g
    dx = r * (dxh - xh * jnp.mean(dxh * xh, axis=-1, keepdims=True))
    return dx, jnp.sum(dy * xh, axis=0, keepdims=True)


def _q_up(proj, q_norm_g, wn, wr, wt, cos_t, sin_t, bsz, lp):
    tp = bsz * lp
    tok = _wide_block(lp)
    nb = lp // tok

    def body(cq_ref, g_ref, wn_ref, wr_ref, wt_ref, cos_ref, sin_ref, q_ref):
        xh, _ = _rms_fwd(cq_ref[...].astype(F32))
        cqn = _bf(xh * g_ref[...])
        nope = _dot(cqn, wn_ref[...])
        rope = _dot(cqn, wr_ref[...])
        rot = _dot(cqn, wt_ref[...])
        cos, sin = cos_ref[...], sin_ref[...]
        one = (lax.broadcasted_iota(jnp.int32, (tok, LANE), 1) == BIAS_LANE).astype(F32)
        for h in range(MLA_HEADS):
            sl = slice(h * LANE, (h + 1) * LANE)
            q_ref[:, h * QKW:h * QKW + LANE] = _bf(nope[:, sl])
            q_ref[:, h * QKW + LANE:(h + 1) * QKW] = _bf(rope[:, sl] * cos + rot[:, sl] * sin + one)

    wspec = pl.BlockSpec((MLA_QR, MLA_HEADS * LANE), lambda b, i: (0, 0))
    tspec = pl.BlockSpec((tok, LANE), lambda b, i: (i, 0))
    return pl.pallas_call(
        body, name="mla_q_up", grid=(bsz, nb),
        in_specs=[pl.BlockSpec((tok, MLA_QR), lambda b, i: (b * nb + i, C_CQ // MLA_QR)),
                  pl.BlockSpec((1, MLA_QR), lambda b, i: (0, 0)), wspec, wspec, wspec, tspec, tspec],
        out_specs=pl.BlockSpec((tok, MLA_HEADS * QKW), lambda b, i: (b * nb + i, 0)),
        out_shape=jax.ShapeDtypeStruct((tp, MLA_HEADS * QKW), BF16),
        compiler_params=_cp(("parallel", "parallel")),
    )(proj, q_norm_g, wn, wr, wt, cos_t, sin_t)


def _kv_up(proj, kv_norm_g, wk, wv, cos_t, sin_t, bsz, lp):
    tp = bsz * lp
    tok = _wide_block(lp)
    nb = lp // tok

    def body(ckv_ref, kr_ref, krot_ref, g_ref, wk_ref, wv_ref, cos_ref, sin_ref, k_ref, v_ref):
        xh, _ = _rms_fwd(ckv_ref[...].astype(F32))
        cn = _bf(xh * g_ref[...])
        kn = _dot(cn, wk_ref[...])
        v_ref[...] = _bf(_dot(cn, wv_ref[...]))
        pos = pl.program_id(1) * tok + lax.broadcasted_iota(jnp.int32, (tok, LANE), 0)
        lane = lax.broadcasted_iota(jnp.int32, (tok, LANE), 1)
        bias = jnp.where(jnp.logical_and(lane == BIAS_LANE, pos < FRONT), KEY_BIAS, 0.0)
        kr = _bf(kr_ref[...].astype(F32) * cos_ref[...] + krot_ref[...].astype(F32) * sin_ref[...] + bias)
        for h in range(MLA_HEADS):
            k_ref[:, h * QKW:h * QKW + LANE] = _bf(kn[:, h * LANE:(h + 1) * LANE])
            k_ref[:, h * QKW + LANE:(h + 1) * QKW] = kr

    wspec = pl.BlockSpec((MLA_KVR, MLA_HEADS * LANE), lambda b, i: (0, 0))
    tspec = pl.BlockSpec((tok, LANE), lambda b, i: (i, 0))
    return pl.pallas_call(
        body, name="mla_kv_up", grid=(bsz, nb),
        in_specs=[pl.BlockSpec((tok, LANE), lambda b, i: (b * nb + i, C_CKV // LANE)),
                  pl.BlockSpec((tok, LANE), lambda b, i: (b * nb + i, C_KR // LANE)),
                  pl.BlockSpec((tok, LANE), lambda b, i: (b * nb + i, C_KROT // LANE)),
                  pl.BlockSpec((1, MLA_KVR), lambda b, i: (0, 0)), wspec, wspec, tspec, tspec],
        out_specs=[pl.BlockSpec((tok, MLA_HEADS * QKW), lambda b, i: (b * nb + i, 0)),
                   pl.BlockSpec((tok, MLA_HEADS * LANE), lambda b, i: (b * nb + i, 0))],
        out_shape=[jax.ShapeDtypeStruct((tp, MLA_HEADS * QKW), BF16),
                   jax.ShapeDtypeStruct((tp, MLA_HEADS * LANE), BF16)],
        compiler_params=_cp(("parallel", "parallel")),
    )(proj, proj, proj, kv_norm_g, wk, wv, cos_t, sin_t)


ATT_SCALE = MLA_QK ** -0.5


KEY_BIAS = -1e30
BIAS_LANE = MLA_ROPE
NEG = 2 * KEY_BIAS
LOG2E = 1.4426950408889634
EXP2_SCALE = ATT_SCALE * LOG2E


def _causal_fill(s, r0, fill):
    tq, kmax = s.shape
    a = r0 // LANE * LANE
    mask = (a + lax.broadcasted_iota(jnp.int32, (tq, kmax - a), 1)
            <= r0 + lax.broadcasted_iota(jnp.int32, (tq, kmax - a), 0))
    right = jnp.where(mask, s[:, a:], fill)
    return jnp.concatenate([s[:, :a], right], axis=1) if a else right


def _attn_fwd(qf, kf, vf, proj, bsz, lp):
    tp = bsz * lp
    tq = _attn_block(lp)
    nh = 2

    def body(q_ref, k_ref, v_ref, mz_ref, ob_ref, yb_ref, lse_ref):
        starts = list(range(0, lp, tq))
        for pair in (starts[i:i + 2] for i in range(0, len(starts), 2)):
            work = [(r0, h) for r0 in pair for h in range(nh)]
            ss = [_causal_fill(_dot_nt(q_ref[r0:r0 + tq, h * QKW:(h + 1) * QKW],
                                       k_ref[0:r0 + tq, h * QKW:(h + 1) * QKW]), r0, NEG) for r0, h in work]
            ms = [jnp.max(s, axis=-1, keepdims=True) for s in ss]
            ps = [jnp.exp2((s - m) * EXP2_SCALE) for s, m in zip(ss, ms)]
            ls = [jnp.sum(p, axis=-1, keepdims=True) for p in ps]
            for (r0, h), p, m, l in zip(work, ps, ms, ls):
                rows, cols = slice(r0, r0 + tq), slice(h * MLA_DV, (h + 1) * MLA_DV)
                o = _dot(_bf(p), v_ref[0:r0 + tq, cols]) / l
                ob_ref[rows, cols] = _bf(o)
                mz = mz_ref[rows, cols].astype(F32)
                yb_ref[rows, cols] = _bf(o * (mz * _sigmoid(mz)))
                lse_ref[0, h, rows, :] = jnp.broadcast_to(m * EXP2_SCALE + jnp.log2(l), (tq, LANE))

    head = lambda off: pl.BlockSpec((lp, nh * MLA_DV), lambda b, h: (b, off + h))
    wide = pl.BlockSpec((lp, nh * QKW), lambda b, h: (b, h))
    return pl.pallas_call(
        body, name="mla_attn_fwd", grid=(bsz, MLA_HEADS // nh),
        in_specs=[wide, wide, head(0), head(C_MZ // (nh * MLA_DV))],
        out_specs=[head(0), head(0), pl.BlockSpec((1, nh, lp, LANE), lambda b, h: (b, h, 0, 0))],
        out_shape=[jax.ShapeDtypeStruct((tp, MLA_HEADS * MLA_DV), BF16),
                   jax.ShapeDtypeStruct((tp, MLA_HEADS * MLA_DV), BF16),
                   jax.ShapeDtypeStruct((bsz, MLA_HEADS, lp, LANE), F32)],
        compiler_params=_cp(("parallel", "parallel"), 56),
    )(qf, kf, vf, proj)


def _attn_bwd_blocks(lp):
    return [(0, X0)] + [(r0, min(MXU_DEPTH, lp - r0)) for r0 in range(X0, lp, MXU_DEPTH)]


def _attn_bwd(qf, kf, vf, d_o, lse, delta, bsz, lp):
    tp = bsz * lp

    def body(q_ref, k_ref, v_ref, do_ref, lse_ref, dl_ref, dq_ref, dk_ref, dv_ref, dk_acc, dv_acc):
        dk_acc[...] = jnp.zeros_like(dk_acc)
        dv_acc[...] = jnp.zeros_like(dv_acc)
        for r0, tq in _attn_bwd_blocks(lp):
            rows, kmax = slice(r0, r0 + tq), r0 + tq
            q, do = q_ref[rows, :], do_ref[rows, :]
            k, v = k_ref[0:kmax, :], v_ref[0:kmax, :]
            p = jnp.exp2(_dot_nt(q, k) * EXP2_SCALE - lse_ref[0, 0, rows, :][:, :1])
            p = _causal_fill(p, r0, 0.0)
            ds = _bf(p * (_dot_nt(do, v) - dl_ref[0, rows, :][:, :1]))
            dq_ref[rows, :] = _bf(_dot(ds, k) * ATT_SCALE)
            dk_acc[0:kmax, :] += _dot_tn(ds, q)
            dv_acc[0:kmax, :] += _dot_tn(_bf(p), do)
        dk_ref[...] = _bf(dk_acc[...] * ATT_SCALE)
        dv_ref[...] = _bf(dv_acc[...])

    wide = pl.BlockSpec((lp, QKW), lambda b, h: (b, h))
    narrow = pl.BlockSpec((lp, MLA_DV), lambda b, h: (b, h))
    stat = pl.BlockSpec((1, 1, lp, LANE), lambda b, h: (b, h, 0, 0))
    return pl.pallas_call(
        body, name="mla_attn_bwd", grid=(bsz, MLA_HEADS),
        in_specs=[wide, wide, narrow, narrow, stat, pl.BlockSpec((1, lp, LANE), lambda b, h: (h, b, 0))],
        out_specs=[wide, wide, narrow],
        out_shape=[jax.ShapeDtypeStruct((tp, MLA_HEADS * QKW), BF16), jax.ShapeDtypeStruct((tp, MLA_HEADS * QKW), BF16),
                   jax.ShapeDtypeStruct((tp, MLA_HEADS * MLA_DV), BF16)],
        scratch_shapes=[pltpu.VMEM((lp, QKW), F32), pltpu.VMEM((lp, MLA_DV), F32)],
        compiler_params=_cp(("parallel", "parallel"), 56),
    )(qf, kf, vf, d_o, lse, delta)


def _q_up_bwd(dqf, proj, q_norm_g, wn, wr, wt, cos_t, sin_t, dproj, bsz, lp):
    tp = bsz * lp
    tok = _wide_block(lp)
    nb = lp // tok
    hw = MLA_HEADS * LANE

    def body(dq_ref, cq_ref, g_ref, wn_ref, wr_ref, wt_ref, cos_ref, sin_ref, _,
             dcq_ref, dwn_ref, dwr_ref, dwt_ref, dg_ref):
        @pl.when(jnp.logical_and(pl.program_id(0) == 0, pl.program_id(1) == 0))
        def _():
            for r in (dwn_ref, dwr_ref, dwt_ref, dg_ref):
                r[...] = jnp.zeros_like(r)

        g = g_ref[...]
        xh, r = _rms_fwd(cq_ref[...].astype(F32))
        cqn = _bf(xh * g)
        dn = jnp.concatenate([dq_ref[:, h * QKW:h * QKW + LANE] for h in range(MLA_HEADS)], axis=1)
        dr = jnp.concatenate([dq_ref[:, h * QKW + LANE:(h + 1) * QKW] for h in range(MLA_HEADS)], axis=1).astype(F32)
        dr_c = _bf(dr * jnp.tile(cos_ref[...], (1, MLA_HEADS)))
        dr_s = _bf(dr * jnp.tile(sin_ref[...], (1, MLA_HEADS)))
        dcqn = _dot_nt(dn, wn_ref[...]) + _dot_nt(dr_c, wr_ref[...]) + _dot_nt(dr_s, wt_ref[...])
        dwn_ref[...] += _dot_tn(cqn, dn)
        dwr_ref[...] += _dot_tn(cqn, dr_c)
        dwt_ref[...] += _dot_tn(cqn, dr_s)
        dx, dg = _rms_bwd(dcqn, xh, r, g)
        dcq_ref[...] = _bf(dx)
        dg_ref[...] += dg

    aspec = pl.BlockSpec((MLA_QR, hw), lambda b, i: (0, 0))
    tspec = pl.BlockSpec((tok, LANE), lambda b, i: (i, 0))
    return pl.pallas_call(
        body, name="mla_q_up_bwd", grid=(bsz, nb),
        in_specs=[pl.BlockSpec((tok, MLA_HEADS * QKW), lambda b, i: (b * nb + i, 0)),
                  pl.BlockSpec((tok, MLA_QR), lambda b, i: (b * nb + i, C_CQ // MLA_QR)),
                  pl.BlockSpec((1, MLA_QR), lambda b, i: (0, 0)), aspec, aspec, aspec, tspec, tspec,
                  pl.BlockSpec(memory_space=pl.ANY)],
        out_specs=[pl.BlockSpec((tok, MLA_QR), lambda b, i: (b * nb + i, C_CQ // MLA_QR)), aspec, aspec, aspec,
                   pl.BlockSpec((1, MLA_QR), lambda b, i: (0, 0))],
        out_shape=[jax.ShapeDtypeStruct((tp, N_EXT), BF16)] + [jax.ShapeDtypeStruct((MLA_QR, hw), F32)] * 3
        + [jax.ShapeDtypeStruct((1, MLA_QR), F32)],
        input_output_aliases={8: 0},
        compiler_params=_cp(("arbitrary", "arbitrary")),
    )(dqf, proj, q_norm_g, wn, wr, wt, cos_t, sin_t, dproj)


def _kv_up_bwd(dkf, dvf, proj, kv_norm_g, wk, wv, cos_t, sin_t, d_lr, dproj, bsz, lp):
    tp = bsz * lp
    tok = _wide_block(lp)
    nb = lp // tok
    hw = MLA_HEADS * LANE

    def body(dk_ref, dv_ref, ckv_ref, g_ref, wk_ref, wv_ref, cos_ref, sin_ref, dlr_ref, _,
             dp_ref, dwk_ref, dwv_ref, dg_ref):
        dckv_ref, dkr_ref, dkrot_ref = (dp_ref.at[:, j * LANE:(j + 1) * LANE] for j in range(3))
        dp_ref[:, 3 * LANE:] = dlr_ref[...]
        @pl.when(jnp.logical_and(pl.program_id(0) == 0, pl.program_id(1) == 0))
        def _():
            for r in (dwk_ref, dwv_ref, dg_ref):
                r[...] = jnp.zeros_like(r)

        g = g_ref[...]
        xh, r = _rms_fwd(ckv_ref[...].astype(F32))
        cn = _bf(xh * g)
        dv = dv_ref[...]
        dn = jnp.concatenate([dk_ref[:, h * QKW:h * QKW + LANE] for h in range(MLA_HEADS)], axis=1)
        dcn = _dot_nt(dv, wv_ref[...]) + _dot_nt(dn, wk_ref[...])
        dwv_ref[...] += _dot_tn(cn, dv)
        dwk_ref[...] += _dot_tn(cn, dn)
        drope = jnp.zeros((tok, LANE), F32)
        for h in range(MLA_HEADS):
            drope += dk_ref[:, h * QKW + LANE:(h + 1) * QKW].astype(F32)
        dkr_ref[...] = _bf(drope * cos_ref[...])
        dkrot_ref[...] = _bf(drope * sin_ref[...])
        dx, dg = _rms_bwd(dcn, xh, r, g)
        dckv_ref[...] = _bf(dx)
        dg_ref[...] += dg

    aspec = pl.BlockSpec((MLA_KVR, hw), lambda b, i: (0, 0))
    tspec = pl.BlockSpec((tok, LANE), lambda b, i: (i, 0))
    ospec = pl.BlockSpec((tok, LANE), lambda b, i: (b * nb + i, 0))
    return pl.pallas_call(
        body, name="mla_kv_up_bwd", grid=(bsz, nb),
        in_specs=[pl.BlockSpec((tok, MLA_HEADS * QKW), lambda b, i: (b * nb + i, 0)),
                  pl.BlockSpec((tok, hw), lambda b, i: (b * nb + i, 0)),
                  pl.BlockSpec((tok, LANE), lambda b, i: (b * nb + i, C_CKV // LANE)),
                  pl.BlockSpec((1, MLA_KVR), lambda b, i: (0, 0)), aspec, aspec, tspec, tspec, ospec,
                  pl.BlockSpec(memory_space=pl.ANY)],
        out_specs=[pl.BlockSpec((tok, 4 * LANE), lambda b, i: (b * nb + i, C_CKV // (4 * LANE))), aspec, aspec,
                   pl.BlockSpec((1, MLA_KVR), lambda b, i: (0, 0))],
        out_shape=[jax.ShapeDtypeStruct((tp, N_EXT), BF16)] + [jax.ShapeDtypeStruct((MLA_KVR, hw), F32)] * 2
        + [jax.ShapeDtypeStruct((1, MLA_KVR), F32)],
        input_output_aliases={9: 0},
        compiler_params=_cp(("arbitrary", "arbitrary")),
    )(dkf, dvf, proj, kv_norm_g, wk, wv, cos_t, sin_t, d_lr, dproj)


def _mid_fwd(ya_in, yb_in, proj, hp, target, w_gp, w_mp, w_o, final_g, bsz, lp):
    tp = bsz * lp
    tm = _wide_block(lp)
    nb = lp // tm
    last = pl.cdiv(lp - X0, tm) - 1

    def body(ya_ref, yb_ref, gg_ref, gm_ref, h_ref, ta_ref, tb_ref, wgp_ref, wmp_ref, wo_ref, fg_ref,
             ya_out, yb_out, dh_ref, loss_ref, dfg_ref):
        @pl.when(jnp.logical_and(pl.program_id(0) == 0, pl.program_id(1) == 0))
        def _():
            loss_ref[...] = jnp.zeros_like(loss_ref)
            dfg_ref[...] = jnp.zeros_like(dfg_ref)

        y_a = _dot(ya_ref[...], wgp_ref[...])
        y_b = _dot(yb_ref[...], wmp_ref[...])
        ya_out[...] = _bf(y_a)
        yb_out[...] = _bf(y_b)
        merged = _sigmoid(gg_ref[...].astype(F32)) * y_a + _sigmoid(gm_ref[...].astype(F32)) * y_b
        h2 = h_ref[...] + _dot(_bf(merged), wo_ref[...])
        fg = fg_ref[...]
        xh, r = _rms_fwd(h2)
        pos = pl.program_id(1) * tm + lax.broadcasted_iota(jnp.int32, (tm, 1), 0)
        t = jnp.concatenate([ta_ref[0, tm - X0:, :], tb_ref[0, :tm - X0, :]], axis=0)
        err = jnp.where(pos >= X0, xh * fg - t, 0.0)
        loss_ref[...] += 0.5 * jnp.sum(jnp.mean(err * err, axis=-1, keepdims=True), axis=0, keepdims=True)
        dy = err * (1.0 / D_MODEL)
        dx, dfg = _rms_bwd(dy, xh, r, fg)
        dh_ref[...] = dx
        dfg_ref[...] += dfg

    tok = lambda c: pl.BlockSpec((tm, D_MODEL), lambda b, i: (b * nb + i, c))
    wspec = pl.BlockSpec((D_MODEL, D_MODEL), lambda b, i: (0, 0), pipeline_mode=pl.Buffered(1))
    return pl.pallas_call(
        body, name="mid_fwd", grid=(bsz, nb),
        in_specs=[tok(0), tok(0), tok(C_GG // D_MODEL), tok(C_GM // D_MODEL), tok(0),
                  pl.BlockSpec((1, tm, D_MODEL), lambda b, i: (b, jnp.maximum(i - 1, 0), 0)),
                  pl.BlockSpec((1, tm, D_MODEL), lambda b, i: (b, jnp.minimum(i, last), 0)),
                  wspec, wspec, wspec, pl.BlockSpec((1, D_MODEL), lambda b, i: (0, 0))],
        out_specs=[tok(0), tok(0), tok(0), pl.BlockSpec((1, LANE), lambda b, i: (0, 0)),
                   pl.BlockSpec((1, D_MODEL), lambda b, i: (0, 0))],
        out_shape=[jax.ShapeDtypeStruct((tp, D_MODEL), BF16), jax.ShapeDtypeStruct((tp, D_MODEL), BF16),
                   jax.ShapeDtypeStruct((tp, D_MODEL), F32), jax.ShapeDtypeStruct((1, LANE), F32),
                   jax.ShapeDtypeStruct((1, D_MODEL), F32)],
        compiler_params=_cp(("arbitrary", "arbitrary"), 56),
    )(ya_in, yb_in, proj, proj, hp, target, target, w_gp, w_mp, w_o, final_g)


def _mid_bwd(dh2, y_a, y_b, proj, ya_in, yb_in, o_b, w_o, w_gp, w_mp, bsz, lp):
    tp = bsz * lp
    tm = MXU_DEPTH if tp % MXU_DEPTH == 0 else _attn_block(lp)
    nsteps = tp // tm
    group = 3 * D_MODEL

    def body(dh_ref, ya_ref, yb_ref, mz_ref, gg_ref, gm_ref, yai_ref, ybi_ref, ob_ref, wo_ref, wgp_ref, wmp_ref,
             dyai_ref, do_ref, dp_ref, dl_ref, dwo_ref, dwgp_ref, dwmp_ref, a_o, a_gp, a_mp):
        @pl.when(pl.program_id(0) == 0)
        def _():
            for r in (a_o, a_gp, a_mp):
                r[...] = jnp.zeros_like(r)

        dh = _bf(dh_ref[...])
        dm = _dot_nt(dh, wo_ref[...])
        y_a, y_b = ya_ref[...].astype(F32), yb_ref[...].astype(F32)
        sg, sm = _sigmoid(gg_ref[...].astype(F32)), _sigmoid(gm_ref[...].astype(F32))
        d_ya, d_yb = _bf(sg * dm), _bf(sm * dm)
        dp_ref[:, D_MODEL:2 * D_MODEL] = _bf(dm * y_a * sg * (1.0 - sg))
        dp_ref[:, 2 * D_MODEL:] = _bf(dm * y_b * sm * (1.0 - sm))
        merged = _bf(sg * y_a + sm * y_b)
        dy = _dot_nt(d_yb, wmp_ref[...])
        dyai_ref[...] = _bf(_dot_nt(d_ya, wgp_ref[...]))
        a_o[...] += _dot_tn(merged, dh)
        a_gp[...] += _dot_tn(yai_ref[...], d_ya)
        a_mp[...] += _dot_tn(ybi_ref[...], d_yb)
        mz, o = mz_ref[...].astype(F32), ob_ref[...].astype(F32)
        s = _sigmoid(mz)
        do = _bf(dy * (mz * s))
        do_ref[...] = do
        dp_ref[:, :D_MODEL] = _bf(dy * o * (s * (1.0 + mz * (1.0 - s))))
        prod = do.astype(F32) * o
        for h in range(MLA_HEADS):
            dl = jnp.sum(prod[:, h * MLA_DV:(h + 1) * MLA_DV], axis=-1, keepdims=True)
            dl_ref[h] = jnp.broadcast_to(dl, (tm, LANE))

        @pl.when(pl.program_id(0) == nsteps - 1)
        def _():
            pltpu.sync_copy(a_o, dwo_ref)
            pltpu.sync_copy(a_gp, dwgp_ref)
            pltpu.sync_copy(a_mp, dwmp_ref)

    tok = lambda c: pl.BlockSpec((tm, D_MODEL), lambda i: (i, c))
    wspec = pl.BlockSpec((D_MODEL, D_MODEL), lambda i: (0, 0))
    anyspec = pl.BlockSpec(memory_space=pl.ANY)
    wshape = jax.ShapeDtypeStruct((D_MODEL, D_MODEL), F32)
    return pl.pallas_call(
        body, name="mid_bwd", grid=(nsteps,),
        in_specs=[tok(0), tok(0), tok(0), tok(C_MZ // D_MODEL), tok(C_GG // D_MODEL), tok(C_GM // D_MODEL),
                  tok(0), tok(0), tok(0), wspec, wspec, wspec],
        out_specs=[tok(0), tok(0), pl.BlockSpec((tm, group), lambda i: (i, C_MZ // group)),
                   pl.BlockSpec((MLA_HEADS, tm, LANE), lambda i: (0, i, 0)), anyspec, anyspec, anyspec],
        out_shape=[jax.ShapeDtypeStruct((tp, D_MODEL), BF16)] * 2 + [jax.ShapeDtypeStruct((tp, N_EXT), BF16),
                   jax.ShapeDtypeStruct((MLA_HEADS, tp, LANE), F32)] + [wshape] * 3,
        scratch_shapes=[pltpu.VMEM((D_MODEL, D_MODEL), F32)] * 3,
        compiler_params=_cp(("arbitrary",), 56),
    )(dh2, y_a, y_b, proj, proj, proj, ya_in, yb_in, o_b, w_o, w_gp, w_mp)


MESH_ID = pl.DeviceIdType.MESH
EXCHANGE_SEMS = [pltpu.SemaphoreType.DMA((N_DEV - 1,)), pltpu.SemaphoreType.DMA((N_DEV - 1,)), pltpu.SemaphoreType.DMA]


def _my_place():
    return lax.axis_index("x"), lax.axis_index("y"), lax.axis_index("c")


def _exchange(g_ref, recv_ref, send_sems, recv_sems, local_sem, start, same=False):
    x, y, c = _my_place()
    me = 4 * x + 2 * y + c
    own = pltpu.make_async_copy(g_ref if same else g_ref.at[me], recv_ref.at[me], local_sem)
    sends, lands = [], []
    for d in range(1, N_DEV):
        px = 1 - x if d & 4 else x
        py = 1 - y if d & 2 else y
        pc = 1 - c if d & 1 else c
        peer = 4 * px + 2 * py + pc
        for slot, group in ((me, sends),) if start else ((me, sends), (peer, lands)):
            group.append(pltpu.make_async_remote_copy(
                src_ref=g_ref if same else g_ref.at[peer], dst_ref=recv_ref.at[slot], send_sem=send_sems.at[d - 1],
                recv_sem=recv_sems.at[d - 1], device_id=(px, py, pc), device_id_type=MESH_ID))
    if start:
        own.start()
        for cp in sends:
            cp.start()
    else:
        for cp in lands:
            cp.wait_recv()
        for cp in sends:
            cp.wait_send()
        own.wait()


def _dw_in(u, dproj, slabs):
    tp = u.shape[0]
    tn = 3 * LANE
    nj = N_EXT // tn

    def body(u_ref, d_ref, g_ref, o_ref, recv_ref, send_sems, recv_sems, local_sem):
        j = pl.program_id(0)

        @pl.when(j == 0)
        def _():
            _exchange(g_ref, recv_ref, send_sems, recv_sems, local_sem, True)

        o_ref[...] = _bf(_dot_tn(d_ref[...], u_ref[...]))

        @pl.when(j == nj - 1)
        def _():
            _exchange(g_ref, recv_ref, send_sems, recv_sems, local_sem, False)

    anyspec = pl.BlockSpec(memory_space=pl.ANY)
    return pl.pallas_call(
        body, name="dw_in", grid=(nj,),
        in_specs=[pl.BlockSpec((tp, D_MODEL), lambda j: (0, 0), pipeline_mode=pl.Buffered(1)),
                  pl.BlockSpec((tp, tn), lambda j: (0, j)), anyspec],
        out_specs=[pl.BlockSpec((tn, D_MODEL), lambda j: (j, 0)), anyspec],
        out_shape=[jax.ShapeDtypeStruct((N_EXT, D_MODEL), BF16), jax.ShapeDtypeStruct(slabs.shape, slabs.dtype)],
        scratch_shapes=EXCHANGE_SEMS,
        compiler_params=_cp(("arbitrary",), 56),
    )(u, dproj, slabs)


def _dx_in(dproj, w_ext, hp, dh2, norm_g, slabs):
    tp = hp.shape[0]
    tm = 2 * TOK
    ni = tp // tm

    def body(d_ref, w_ref, h_ref, dh_ref, g_ref, s_ref, o_ref, dg_ref, recv_ref, send_sems, recv_sems, local_sem):
        i = pl.program_id(0)

        @pl.when(i == 0)
        def _():
            _exchange(s_ref, recv_ref, send_sems, recv_sems, local_sem, True)
            dg_ref[...] = jnp.zeros_like(dg_ref)

        du = _dot_nt(d_ref[...], w_ref[...])
        g = g_ref[...]
        xh, r = _rms_fwd(h_ref[...])
        dx, dg = _rms_bwd(du, xh, r, g)
        o_ref[...] = dh_ref[...] + dx
        dg_ref[...] += dg

        @pl.when(i == ni - 1)
        def _():
            _exchange(s_ref, recv_ref, send_sems, recv_sems, local_sem, False)

    tok = pl.BlockSpec((tm, D_MODEL), lambda i: (i, 0))
    anyspec = pl.BlockSpec(memory_space=pl.ANY)
    return pl.pallas_call(
        body, name="dx_in", grid=(ni,),
        in_specs=[pl.BlockSpec((tm, N_EXT), lambda i: (i, 0)),
                  pl.BlockSpec((D_MODEL, N_EXT), lambda i: (0, 0), pipeline_mode=pl.Buffered(1)),
                  tok, tok, pl.BlockSpec((1, D_MODEL), lambda i: (0, 0)), anyspec],
        out_specs=[tok, pl.BlockSpec((1, D_MODEL), lambda i: (0, 0)), anyspec],
        out_shape=[jax.ShapeDtypeStruct((tp, D_MODEL), F32), jax.ShapeDtypeStruct((1, D_MODEL), F32),
                   jax.ShapeDtypeStruct(slabs.shape, slabs.dtype)],
        scratch_shapes=EXCHANGE_SEMS,
        compiler_params=_cp(("arbitrary",), 56),
    )(dproj, w_ext, hp, dh2, norm_g, slabs)


def _meta_grad(dhp3):
    bsz = dhp3.shape[0]

    def body(d_ref, o_ref):
        @pl.when(pl.program_id(0) == 0)
        def _():
            o_ref[...] = jnp.zeros_like(o_ref)

        o_ref[...] += d_ref[0]

    return pl.pallas_call(
        body, name="meta_grad", grid=(bsz,),
        in_specs=[pl.BlockSpec((1, N_META, D_MODEL), lambda b: (b, FRONT // N_META, 0))],
        out_specs=pl.BlockSpec((N_META, D_MODEL), lambda b: (0, 0)),
        out_shape=jax.ShapeDtypeStruct((N_META, D_MODEL), F32),
        compiler_params=_cp(("arbitrary",)),
    )(dhp3)


W_IN_SHARD = N_IN // N_DEV


def _pad_lanes(a, width=LANE):
    return jnp.pad(a, [(0, 0)] * (a.ndim - 1) + [(0, width - a.shape[-1])])


def _rot_cols(w):
    half = w.shape[-1] // 2
    return jnp.concatenate([-w[..., half:], w[..., :half]], axis=-1)


def _unrot_cols(dw):
    half = dw.shape[-1] // 2
    return jnp.concatenate([dw[..., half:], -dw[..., :half]], axis=-1)


def _w_in_cols(shards, lo, hi):
    parts = []
    for k in range(lo // W_IN_SHARD, (hi - 1) // W_IN_SHARD + 1):
        a, b = max(lo, k * W_IN_SHARD), min(hi, (k + 1) * W_IN_SHARD)
        parts.append(shards[k][:, a - k * W_IN_SHARD:b - k * W_IN_SHARD])
    return parts[0] if len(parts) == 1 else jnp.concatenate(parts, axis=1)


def _w_in_ext(shards):
    c = lambda lo, hi: _w_in_cols(shards, lo, hi)
    kr = c(O_KR, O_MZ)
    return jnp.concatenate([
        c(O_V, O_LR), c(O_Z, O_CQ), c(O_Q, O_K), c(O_K, O_V), c(O_MZ, O_GG), c(O_GG, O_GM), c(O_GM, N_IN),
        c(O_CKV, O_KR), _pad_lanes(kr), _pad_lanes(_rot_cols(kr)), _pad_lanes(c(O_LR, O_Z)), c(O_CQ, O_CKV)], axis=1)


def _w_in_grad_t(dwt):
    g = lambda start, width: dwt[start:start + width]
    half = MLA_ROPE // 2
    krot = g(C_KROT, MLA_ROPE)
    kr = g(C_KR, MLA_ROPE) + jnp.concatenate([krot[half:], -krot[:half]], axis=0)
    return jnp.concatenate([
        g(C_Q, GLA_KW), g(C_K, GLA_KW), g(C_V, GLA_VW), g(C_LR, GLA_RANK), g(C_Z, GLA_VW), g(C_CQ, MLA_QR),
        g(C_CKV, MLA_KVR), kr, g(C_MZ, D_MODEL), g(C_GG, D_MODEL), g(C_GM, D_MODEL)], axis=0)


def _rope_tables(lp):
    inv = 1.0 / (ROPE_BASE ** (jnp.arange(0, MLA_ROPE, 2, dtype=F32) / MLA_ROPE))
    ang = (jnp.arange(lp, dtype=F32) - FRONT)[:, None] * inv[None, :]
    cos, sin = jnp.cos(ang), jnp.sin(ang)
    return _pad_lanes(jnp.concatenate([cos, cos], axis=1)), _pad_lanes(jnp.concatenate([sin, sin], axis=1))


def _local_step(x, loss_target, w):
    bsz, seq, _ = x.shape
    lp = X0 + seq
    tp = bsz * lp
    assert lp % TOK == 0 and (lp // GLA_CHUNK) % _gla_group(lp // GLA_CHUNK) == 0
    head = jnp.concatenate([jnp.zeros((FRONT, D_MODEL), F32), w["meta_tokens"]], axis=0)
    cos_t, sin_t = _rope_tables(lp)

    w_ext = _w_in_ext(w["w_in"])
    hp, u, proj, packed_all = _proj_in(x, head, w["norm_g"], w_ext, w["packed"])
    packed_all, off = packed_all.reshape(N_DEV, -1), 0
    for n, shape, axis in PACKED:
        size = shape[0] * shape[1]
        w[n] = _join8(packed_all[:, off:off + size].reshape((N_DEV,) + shape), axis)
        off += size
    gw_pad = jnp.pad(w["gla_gate_w"], ((0, LANE - GLA_RANK), (0, 0)))
    uq = w["mla_w_uq"].reshape(MLA_QR, MLA_HEADS, MLA_QK)
    rope_w = uq[:, :, MLA_NOPE:]
    hw = MLA_HEADS * LANE
    wn = uq[:, :, :MLA_NOPE].reshape(MLA_QR, hw)
    wr = _pad_lanes(rope_w).reshape(MLA_QR, hw)
    wt = _pad_lanes(_rot_cols(rope_w)).reshape(MLA_QR, hw)
    ukv = w["mla_w_ukv"].reshape(MLA_KVR, MLA_HEADS, MLA_NOPE + MLA_DV)
    wk = ukv[:, :, :MLA_NOPE].reshape(MLA_KVR, hw)
    wv = ukv[:, :, MLA_NOPE:].reshape(MLA_KVR, hw)

    o_raw, ya_in, s_all = _gla_fwd(proj, gw_pad, w["gla_gate_b"], w["gla_norm_g"], bsz, lp)
    qf = _q_up(proj, w["mla_q_norm_g"], wn, wr, wt, cos_t, sin_t, bsz, lp)
    kf, vf = _kv_up(proj, w["mla_kv_norm_g"], wk, wv, cos_t, sin_t, bsz, lp)
    o_b, yb_in, lse = _attn_fwd(qf, kf, vf, proj, bsz, lp)
    y_a, y_b, dh2, loss, d_final_g = _mid_fwd(ya_in, yb_in, proj, hp, loss_target, w["gla_proj"], w["mla_proj"],
                                              w["w_out"], w["final_norm_g"], bsz, lp)
    d_ya, d_o, dproj, delta, d_w_out, d_gla_proj, d_mla_proj = _mid_bwd(
        dh2, y_a, y_b, proj, ya_in, yb_in, o_b, w["w_out"], w["gla_proj"], w["mla_proj"], bsz, lp)
    dproj, d_lr, d_gw_pad, d_gate_b, d_gla_norm = _gla_bwd(proj, gw_pad, w["gla_gate_b"], w["gla_norm_g"], o_raw, s_all,
                                                           d_ya, dproj, bsz, lp)
    dqf, dkf, dvf = _attn_bwd(qf, kf, vf, d_o, lse, delta, bsz, lp)
    dproj, d_wn, d_wr, d_wt, d_qn = _q_up_bwd(dqf, proj, w["mla_q_norm_g"], wn, wr, wt, cos_t, sin_t, dproj,
                                              bsz, lp)
    dproj, d_wk, d_wv, d_kvn = _kv_up_bwd(dkf, dvf, proj, w["mla_kv_norm_g"], wk, wv, cos_t, sin_t, d_lr, dproj,
                                          bsz, lp)

    d_rope = (d_wr.reshape(MLA_QR, MLA_HEADS, LANE)[:, :, :MLA_ROPE]
              + _unrot_cols(d_wt.reshape(MLA_QR, MLA_HEADS, LANE)[:, :, :MLA_ROPE]))
    d_uq = jnp.concatenate([d_wn.reshape(MLA_QR, MLA_HEADS, LANE), d_rope], axis=-1).reshape(MLA_QR, MLA_HEADS * MLA_QK)
    d_ukv = jnp.concatenate([d_wk.reshape(MLA_KVR, MLA_HEADS, LANE), d_wv.reshape(MLA_KVR, MLA_HEADS, LANE)],
                            axis=-1).reshape(MLA_KVR, MLA_HEADS * (MLA_NOPE + MLA_DV))
    mats = dict(gla_gate_w=d_gw_pad[:GLA_RANK], gla_proj=d_gla_proj, mla_w_uq=d_uq, mla_w_ukv=d_ukv,
                mla_proj=d_mla_proj, w_out=d_w_out)
    packed = _pad_rows(jnp.concatenate([_split8(mats[n], axis).reshape(N_DEV, -1) for n, _, axis in PACKED], axis=1),
                       PACK_ROWS)
    d_w_ext_t, packed_parts = _dw_in(u, dproj, _bf(packed))
    w_in_slabs = _w_in_grad_t(d_w_ext_t).reshape(N_DEV, W_IN_SHARD, D_MODEL)
    d_hp, d_norm_g, w_in_parts = _dx_in(dproj, w_ext, hp, dh2, w["norm_g"], w_in_slabs)
    d_hp3 = d_hp.reshape(bsz, lp, D_MODEL)
    small = dict(meta_tokens=_meta_grad(d_hp3), norm_g=d_norm_g, gla_gate_b=d_gate_b, gla_norm_g=d_gla_norm,
                 mla_q_norm_g=d_qn, mla_kv_norm_g=d_kvn, final_norm_g=d_final_g)
    return loss, d_hp3[:, X0:, :], w_in_parts, packed_parts, small


PACKED = (("gla_gate_w", (GLA_RANK, GLA_KW // N_DEV), 1),
          ("gla_proj", (D_MODEL // N_DEV, D_MODEL), 0), ("mla_w_uq", (MLA_QR, MLA_HEADS * MLA_QK // N_DEV), 1),
          ("mla_w_ukv", (MLA_KVR, MLA_HEADS * (MLA_NOPE + MLA_DV) // N_DEV), 1),
          ("mla_proj", (D_MODEL // N_DEV, D_MODEL), 0), ("w_out", (D_MODEL // N_DEV, D_MODEL), 0))
REPLICATED = (("norm_g", D_MODEL), ("gla_gate_b", GLA_KW), ("gla_norm_g", GLA_DV), ("mla_q_norm_g", MLA_QR),
              ("mla_kv_norm_g", MLA_KVR), ("final_norm_g", D_MODEL))
PACK_ROWS = 3744
PACK_BLOCK = 1248
SMALL_ROWS = 48
LOSS_ROW = N_META + 25
W_IN_BLOCK = 128


def _all_gather(shards):
    n_arr = len(shards)
    pieces = []
    for a, s in enumerate(shards):
        step = s.shape[0] // 4 if s.shape[0] >= 4 * LANE else s.shape[0]
        pieces += [(a, slice(r, r + step)) for r in range(0, s.shape[0], step)]
    n_pc = len(pieces)

    def body(*refs):
        x_refs, out_refs = refs[:n_arr], refs[n_arr:2 * n_arr]
        send_sems, recv_sems, local_sems = refs[2 * n_arr:]
        x, y, c = _my_place()
        me, sibling = (x, y, c), (x, y, 1 - c)
        chips = [(1 - x, y), (x, 1 - y), (1 - x, 1 - y)]

        def copy(u, k, block, to, from_input=False):
            a, rows = pieces[u]
            slab = out_refs[a].at[4 * block[0] + 2 * block[1] + block[2], rows]
            return pltpu.make_async_remote_copy(
                src_ref=x_refs[a].at[rows] if from_input else slab, dst_ref=slab,
                send_sem=send_sems.at[7 * u + k], recv_sem=recv_sems.at[7 * u + k], device_id=to,
                device_id_type=MESH_ID)

        arrays = range(n_pc)
        mine = [pltpu.make_async_copy(x_refs[a], out_refs[a].at[4 * x + 2 * y + c], local_sems.at[a])
                for a in range(n_arr)]
        for cp in mine:
            cp.start()
        first = [copy(a, 0, me, sibling, True) for a in arrays]
        first += [copy(a, 1 + j, me, (*chip, c), True) for j, chip in enumerate(chips) for a in arrays]
        for cp in first:
            cp.start()
        passed = []
        for j, chip in enumerate(chips):
            for a in arrays:
                copy(a, 1 + j, (*chip, c), me).wait_recv()
                passed.append(copy(a, 4 + j, (*chip, c), sibling))
                passed[-1].start()
        for a in arrays:
            copy(a, 0, sibling, me).wait_recv()
        for j, chip in enumerate(chips):
            for a in arrays:
                copy(a, 4 + j, (*chip, 1 - c), me).wait_recv()
        for cp in first + passed:
            cp.wait_send()
        for cp in mine:
            cp.wait()

    anyspec = pl.BlockSpec(memory_space=pl.ANY)
    return pl.pallas_call(
        body, name="weights_all_gather",
        out_shape=[jax.ShapeDtypeStruct((N_DEV,) + s.shape, s.dtype) for s in shards],
        in_specs=[anyspec] * n_arr, out_specs=[anyspec] * n_arr,
        scratch_shapes=[pltpu.SemaphoreType.DMA((7 * n_pc,)), pltpu.SemaphoreType.DMA((7 * n_pc,)),
                        pltpu.SemaphoreType.DMA((n_arr,))],
    )(*shards)


def _small_exchange(slabs):
    def body(g_ref, recv_ref, send_sems, recv_sems, local_sem):
        _exchange(g_ref, recv_ref, send_sems, recv_sems, local_sem, True)
        _exchange(g_ref, recv_ref, send_sems, recv_sems, local_sem, False)

    vmem = pl.BlockSpec(memory_space=pltpu.VMEM)
    return pl.pallas_call(
        body, name="small_exchange", out_shape=jax.ShapeDtypeStruct(slabs.shape, slabs.dtype),
        in_specs=[vmem], out_specs=vmem, scratch_shapes=EXCHANGE_SEMS,
    )(slabs)


def _adamw(parts, w, m, v, block_rows, name):
    rows, cols = w.shape

    def body(p_ref, w_ref, m_ref, v_ref, g_out, d_out, m_out, v_out):
        g = p_ref[0].astype(F32)
        for s in range(1, N_DEV):
            g = g + p_ref[s].astype(F32)
        m_new = ADAM_B1 * m_ref[...] + (1.0 - ADAM_B1) * g
        v_new = ADAM_B2 * v_ref[...] + (1.0 - ADAM_B2) * (g * g)
        m_hat = m_new / (1.0 - ADAM_B1 ** ADAM_STEP)
        v_hat = v_new / (1.0 - ADAM_B2 ** ADAM_STEP)
        g_out[...] = g
        d_out[...] = -ADAM_LR * (m_hat / (jnp.sqrt(v_hat) + ADAM_EPS) + ADAM_WD * w_ref[...])
        m_out[...] = m_new
        v_out[...] = v_new

    spec = pl.BlockSpec((block_rows, cols), lambda i: (i, 0))
    return pl.pallas_call(
        body, name=name, grid=(pl.cdiv(rows, block_rows),),
        in_specs=[pl.BlockSpec((N_DEV, block_rows, cols), lambda i: (0, i, 0)), spec, spec, spec],
        out_specs=[spec] * 4, out_shape=[jax.ShapeDtypeStruct((rows, cols), F32)] * 4,
        compiler_params=_cp(("parallel",), 48),
    )(parts, w, m, v)


def _pad_rows(flat, rows):
    pad = rows * LANE - flat.shape[-1]
    flat = jnp.pad(flat, [(0, 0)] * (flat.ndim - 1) + [(0, pad)])
    return flat.reshape(flat.shape[:-1] + (rows, LANE))


def _pack_shards(shards):
    return _pad_rows(jnp.concatenate([shards[n].reshape(-1) for n, _, _ in PACKED]), PACK_ROWS)


def _unpack_shards(packed):
    flat, out, off = packed.reshape(-1), {}, 0
    for n, shape, _ in PACKED:
        size = shape[0] * shape[1]
        out[n] = flat[off:off + size].reshape(shape)
        off += size
    return out


def _split8(full, axis):
    r, c = full.shape
    if axis == 0:
        return full.reshape(N_DEV, r // N_DEV, c)
    return full.reshape(r, N_DEV, c // N_DEV).transpose(1, 0, 2)


def _join8(shards, axis):
    _, r, c = shards.shape
    if axis == 0:
        return shards.reshape(N_DEV * r, c)
    return shards.transpose(1, 0, 2).reshape(r, N_DEV * c)


def _pack_small(meta_shard, vals, loss_row):
    rows = jnp.concatenate([vals[n].reshape(-1, LANE) for n, _ in REPLICATED] + [loss_row], axis=0)
    rows = jnp.pad(rows, ((0, SMALL_ROWS - N_META - rows.shape[0]), (0, 0)))
    return jnp.concatenate([meta_shard, jnp.broadcast_to(rows, meta_shard.shape[:-2] + rows.shape)], axis=-2)


def _unpack_small(packed):
    out, off = {"meta_tokens": packed[:N_META]}, N_META
    for n, size in REPLICATED:
        out[n] = packed[off:off + size // LANE].reshape(1, size)
        off += size // LANE
    return out


def kernel(x, meta_tokens, norm_g, w_in, gla_gate_w, gla_gate_b, gla_norm_g, gla_proj, mla_q_norm_g, mla_w_uq, mla_kv_norm_g, mla_w_ukv, mla_proj, w_out, final_norm_g, loss_target, m_meta_tokens, m_norm_g, m_w_in, m_gla_gate_w, m_gla_gate_b, m_gla_norm_g, m_gla_proj, m_mla_q_norm_g, m_mla_w_uq, m_mla_kv_norm_g, m_mla_w_ukv, m_mla_proj, m_w_out, m_final_norm_g, v_meta_tokens, v_norm_g, v_w_in, v_gla_gate_w, v_gla_gate_b, v_gla_norm_g, v_gla_proj, v_mla_q_norm_g, v_mla_w_uq, v_mla_kv_norm_g, v_mla_w_ukv, v_mla_proj, v_w_out, v_final_norm_g):
    given = dict(meta_tokens=meta_tokens, norm_g=norm_g, w_in=w_in, gla_gate_w=gla_gate_w, gla_gate_b=gla_gate_b,
                 gla_norm_g=gla_norm_g, gla_proj=gla_proj, mla_q_norm_g=mla_q_norm_g, mla_w_uq=mla_w_uq,
                 mla_kv_norm_g=mla_kv_norm_g, mla_w_ukv=mla_w_ukv, mla_proj=mla_proj, w_out=w_out,
                 final_norm_g=final_norm_g)
    mom_m = dict(meta_tokens=m_meta_tokens, norm_g=m_norm_g, w_in=m_w_in, gla_gate_w=m_gla_gate_w,
                 gla_gate_b=m_gla_gate_b, gla_norm_g=m_gla_norm_g, gla_proj=m_gla_proj, mla_q_norm_g=m_mla_q_norm_g,
                 mla_w_uq=m_mla_w_uq, mla_kv_norm_g=m_mla_kv_norm_g, mla_w_ukv=m_mla_w_ukv, mla_proj=m_mla_proj,
                 w_out=m_w_out, final_norm_g=m_final_norm_g)
    mom_v = dict(meta_tokens=v_meta_tokens, norm_g=v_norm_g, w_in=v_w_in, gla_gate_w=v_gla_gate_w,
                 gla_gate_b=v_gla_gate_b, gla_norm_g=v_gla_norm_g, gla_proj=v_gla_proj, mla_q_norm_g=v_mla_q_norm_g,
                 mla_w_uq=v_mla_w_uq, mla_kv_norm_g=v_mla_kv_norm_g, mla_w_ukv=v_mla_w_ukv, mla_proj=v_mla_proj,
                 w_out=v_w_out, final_norm_g=v_final_norm_g)
    shapes = {n: a.shape for n, a in given.items()}
    shard2d = {n: s for n, s, _ in PACKED}
    shard2d["w_in"] = (D_MODEL, W_IN_SHARD)
    shard2d["meta_tokens"] = (N_META, LANE)

    def as2d(tree):
        out = {n: tree[n].reshape(shard2d[n]) for n in shard2d}
        out.update({n: tree[n].reshape(1, size) for n, size in REPLICATED})
        return out

    w_loc, m_loc, v_loc = as2d(given), as2d(mom_m), as2d(mom_v)

    w_in_all, meta_all = _all_gather([w_loc["w_in"].astype(BF16), w_loc["meta_tokens"]])
    flat = jnp.concatenate([w_loc[n].astype(BF16).reshape(-1) for n, _, _ in PACKED])
    full = {"w_in": w_in_all, "meta_tokens": _join8(meta_all, 1), "packed": _pad_rows(flat, PACK_ROWS)}
    for n, _ in REPLICATED:
        full[n] = w_loc[n]

    loss_part, grad_x, w_in_parts, packed_parts, small = _local_step(x, loss_target, full)
    small_all = _small_exchange(_pack_small(_split8(small["meta_tokens"], 1), small,
                                            jnp.broadcast_to(loss_part[:, :1], (1, LANE))))

    w_in_t = [t["w_in"].T for t in (w_loc, m_loc, v_loc)]
    g_w, d_w, m_w, v_w = (o.T for o in _adamw(w_in_parts, *w_in_t, W_IN_BLOCK, "adamw_w_in"))
    g_p, d_p, m_p, v_p = _adamw(packed_parts, _pack_shards(w_loc), _pack_shards(m_loc), _pack_shards(v_loc),
                                PACK_BLOCK, "adamw_packed")
    zero_row = jnp.zeros((1, LANE), F32)
    g_s, d_s, m_s, v_s = _adamw(small_all, *(_pack_small(t["meta_tokens"], t, zero_row) for t in (w_loc, m_loc, v_loc)),
                                SMALL_ROWS, "adamw_small")
    loss = g_s[LOSS_ROW, 0]

    order = ["meta_tokens", "norm_g", "w_in", "gla_gate_w", "gla_gate_b", "gla_norm_g", "gla_proj", "mla_q_norm_g",
             "mla_w_uq", "mla_kv_norm_g", "mla_w_ukv", "mla_proj", "w_out", "final_norm_g"]
    result = [loss, grad_x]
    for w_in_out, packed_sh, packed_sm in ((g_w, g_p, g_s), (d_w, d_p, d_s), (m_w, m_p, m_s), (v_w, v_p, v_s)):
        tree = _unpack_shards(packed_sh)
        tree.update(_unpack_small(packed_sm))
        tree["w_in"] = w_in_out
        result += [tree[n].reshape(shapes[n]) for n in order]
    return tuple(result)
```

```python
import jax
import jax.numpy as jnp
from jax import lax
from jax.experimental import pallas as pl
from jax.experimental.pallas import tpu as pltpu

F32 = jnp.float32
BF16 = jnp.bfloat16

D_MODEL = 1024
N_META = 16
EPS = 1e-6
FRONT = 48
X0 = FRONT + N_META
GLA_HEADS, GLA_DK, GLA_DV, GLA_RANK, GLA_CHUNK = 4, 128, 256, 16, 64
GLA_GATE_NORMALIZER = 16.0
GLA_KW = GLA_HEADS * GLA_DK
GLA_VW = GLA_HEADS * GLA_DV
MLA_HEADS, MLA_NOPE, MLA_ROPE, MLA_DV, MLA_QR, MLA_KVR = 8, 128, 64, 128, 256, 128
MLA_QK = MLA_NOPE + MLA_ROPE
ROPE_BASE = 10000.0
LANE = 128
QKW = 2 * LANE

C_V, C_Z, C_Q, C_K = 0, 1024, 2048, 2560
C_MZ, C_GG, C_GM = 3072, 4096, 5120
C_CKV, C_KR, C_KROT, C_LR = 6144, 6272, 6400, 6528
C_CQ = 6656
N_EXT = 6912
O_Q, O_K, O_V, O_LR, O_Z, O_CQ, O_CKV, O_KR, O_MZ, O_GG, O_GM, N_IN = (
    0, 512, 1024, 2048, 2064, 3088, 3344, 3472, 3536, 4560, 5584, 6608)

ADAM_LR, ADAM_B1, ADAM_B2, ADAM_EPS, ADAM_WD, ADAM_STEP = 0.001, 0.9, 0.999, 1e-08, 0.01, 10

N_DEV = 8
TOK = 192
ATT_BLOCK = 352
MXU_DEPTH = 256


def _cp(sems=None, vmem_mb=None):
    kw = {}
    if sems is not None:
        kw["dimension_semantics"] = sems
    if vmem_mb is not None:
        kw["vmem_limit_bytes"] = vmem_mb * 1024 * 1024
    return pltpu.CompilerParams(**kw)


def _dot(a, b):
    return jnp.dot(a, b, preferred_element_type=F32)


def _dot_nt(a, b):
    return lax.dot_general(a, b, (((1,), (1,)), ((), ())), preferred_element_type=F32)


def _dot_tn(a, b):
    return lax.dot_general(a, b, (((0,), (0,)), ((), ())), preferred_element_type=F32)


def _sigmoid(x):
    return 1.0 / (1.0 + jnp.exp(-x))


def _bf(x):
    return x.astype(BF16)


def _big_tok(tp):
    return 4 * TOK if tp % (4 * TOK) == 0 else TOK


def _attn_block(lp):
    return ATT_BLOCK if lp % ATT_BLOCK == 0 else TOK


def _wide_block(lp):
    return 2 * ATT_BLOCK if lp % (2 * ATT_BLOCK) == 0 else _attn_block(lp)


def _proj_in(x, head, norm_g, w_ext, packed):
    bsz, seq, _ = x.shape
    lp = X0 + seq
    tp = bsz * lp
    tm = _attn_block(lp)
    nb = lp // tm
    last = pl.cdiv(seq, tm) - 1

    def body(xa_ref, xb_ref, hd_ref, g_ref, w_ref, p_ref, h_ref, u_ref, o_ref, pall_ref, send_sems, recv_sems, local_sem):
        first = jnp.logical_and(pl.program_id(0) == 0, pl.program_id(1) == 0)

        @pl.when(first)
        def _():
            _exchange(p_ref, pall_ref, send_sems, recv_sems, local_sem, True, same=True)

        front = jnp.where(pl.program_id(1) == 0, hd_ref[...], xa_ref[0, tm - X0:, :])
        h = jnp.concatenate([front, xb_ref[0, :tm - X0, :]], axis=0)
        h_ref[...] = h
        r = lax.rsqrt(jnp.mean(h * h, axis=-1, keepdims=True) + EPS)
        u = _bf(h * r * g_ref[...])
        u_ref[...] = u
        o_ref[...] = _bf(_dot(u, w_ref[...]))

        @pl.when(jnp.logical_and(pl.program_id(0) == bsz - 1, pl.program_id(1) == nb - 1))
        def _():
            _exchange(p_ref, pall_ref, send_sems, recv_sems, local_sem, False, same=True)

    anyspec = pl.BlockSpec(memory_space=pl.ANY)
    tok = lambda width: pl.BlockSpec((tm, width), lambda b, i: (b * nb + i, 0))
    return pl.pallas_call(
        body, name="proj_in", grid=(bsz, nb),
        in_specs=[pl.BlockSpec((1, tm, D_MODEL), lambda b, i: (b, jnp.maximum(i - 1, 0), 0)),
                  pl.BlockSpec((1, tm, D_MODEL), lambda b, i: (b, jnp.minimum(i, last), 0)),
                  pl.BlockSpec((X0, D_MODEL), lambda b, i: (0, 0)),
                  pl.BlockSpec((1, D_MODEL), lambda b, i: (0, 0)),
                  pl.BlockSpec((D_MODEL, N_EXT), lambda b, i: (0, 0), pipeline_mode=pl.Buffered(1)), anyspec],
        out_specs=[tok(D_MODEL), tok(D_MODEL), tok(N_EXT), anyspec],
        out_shape=[jax.ShapeDtypeStruct((tp, D_MODEL), F32), jax.ShapeDtypeStruct((tp, D_MODEL), BF16),
                   jax.ShapeDtypeStruct((tp, N_EXT), BF16),
                   jax.ShapeDtypeStruct((N_DEV,) + packed.shape, packed.dtype)],
        scratch_shapes=EXCHANGE_SEMS,
        compiler_params=_cp(("arbitrary", "arbitrary"), 56),
    )(x, x, head, norm_g, w_ext, packed)


def _gla_group(n_chunks):
    return 11 if n_chunks % 11 == 0 else 3


def _tri_dot(tri, x):
    hi = _bf(x)
    rest = x - hi.astype(F32)
    mid = _bf(rest)
    return _dot(tri, hi) + _dot(tri, mid) + _dot(tri, _bf(rest - mid.astype(F32)))


def _gla_gates(q_ref, k_ref, lr_ref, gw_ref, gb_ref, rows, not_first):
    z = _dot(lr_ref[rows, :], gw_ref[...]) + gb_ref[...]
    logsig = jnp.minimum(z, 0.0) - jnp.log(1.0 + jnp.exp(-jnp.abs(z)))
    row = lax.broadcasted_iota(jnp.int32, (GLA_CHUNK, GLA_KW), 0)
    live = jnp.logical_or(not_first, row >= FRONT)
    g = jnp.where(live, logsig * (1.0 / GLA_GATE_NORMALIZER), 0.0)
    ri = lax.broadcasted_iota(jnp.int32, (GLA_CHUNK, GLA_CHUNK), 0)
    ci = lax.broadcasted_iota(jnp.int32, (GLA_CHUNK, GLA_CHUNK), 1)
    tril = ci <= ri
    b = _tri_dot(_bf(tril.astype(F32)), g)
    bl = jnp.sum(jnp.where(row == GLA_CHUNK - 1, b, 0.0), axis=0, keepdims=True)
    eb, enb, elb, ebl = jnp.exp(b), jnp.exp(-b), jnp.exp(bl - b), jnp.exp(bl)
    q = q_ref[rows, :].astype(F32) * (GLA_DK ** -0.5)
    k = k_ref[rows, :].astype(F32)
    qe, ke, kl = q * eb, k * enb, k * elb
    return dict(z=z, live=live, tril=tril, row=row, eb=eb, enb=enb, elb=elb, ebl=ebl, qe=qe, ke=ke, kl=kl,
                qe_b=_bf(qe), ke_b=_bf(ke), kl_b=_bf(kl))


def _gla_in_specs(n_groups, gla_rows, rev):
    def rb(b, n):
        return b * n_groups + ((n_groups - 1 - n) if rev else n)

    return rb, [pl.BlockSpec((gla_rows, GLA_KW), lambda b, n: (rb(b, n), C_Q // GLA_KW)),
                pl.BlockSpec((gla_rows, GLA_KW), lambda b, n: (rb(b, n), C_K // GLA_KW)),
                pl.BlockSpec((gla_rows, GLA_VW), lambda b, n: (rb(b, n), C_V // GLA_VW)),
                pl.BlockSpec((gla_rows, GLA_VW), lambda b, n: (rb(b, n), C_Z // GLA_VW)),
                pl.BlockSpec((gla_rows, LANE), lambda b, n: (rb(b, n), C_LR // LANE)),
                pl.BlockSpec((LANE, GLA_KW), lambda b, n: (0, 0)),
                pl.BlockSpec((1, GLA_KW), lambda b, n: (0, 0)),
                pl.BlockSpec((1, GLA_DV), lambda b, n: (0, 0))]


def _gla_fwd(proj, gw_pad, gate_b, gla_norm_g, bsz, lp):
    n_chunks = lp // GLA_CHUNK
    gla_group = _gla_group(n_chunks)
    gla_rows = gla_group * GLA_CHUNK
    n_groups = n_chunks // gla_group
    tp = bsz * lp

    def body(q_ref, k_ref, v_ref, z_ref, lr_ref, gw_ref, gb_ref, gn_ref, oraw_ref, ya_ref, sall_ref, st_scr):
        grp = pl.program_id(1)

        @pl.when(grp == 0)
        def _():
            st_scr[...] = jnp.zeros_like(st_scr)

        chunks = [slice(j * GLA_CHUNK, (j + 1) * GLA_CHUNK) for j in range(gla_group)]
        cs = [_gla_gates(q_ref, k_ref, lr_ref, gw_ref, gb_ref, rows, True if j else grp > 0)
              for j, rows in enumerate(chunks)]
        gn = gn_ref[...]
        sts = [st_scr[h] for h in range(GLA_HEADS)]
        heads = [(slice(h * GLA_DK, (h + 1) * GLA_DK), slice(h * GLA_DV, (h + 1) * GLA_DV)) for h in range(GLA_HEADS)]
        a_all = [[_bf(jnp.where(c["tril"], _dot_nt(c["qe_b"][:, ks], c["ke_b"][:, ks]), 0.0)) for ks, _ in heads]
                 for c in cs]
        u_all = [[_dot_tn(v_ref[rows, vs], c["kl_b"][:, ks]) for ks, vs in heads] for rows, c in zip(chunks, cs)]
        for j, (rows, c) in enumerate(zip(chunks, cs)):
            for h, (ks, vs) in enumerate(heads):
                st = sts[h]
                sall_ref[0, j, h] = st
                o = _dot(a_all[j][h], v_ref[rows, vs]) + _dot_nt(c["qe_b"][:, ks], _bf(st))
                sts[h] = st * c["ebl"][:, ks] + u_all[j][h]
                oraw_ref[rows, vs] = o
                r = lax.rsqrt(jnp.mean(o * o, axis=-1, keepdims=True) + EPS)
                zg = z_ref[rows, vs].astype(F32)
                ya_ref[rows, vs] = _bf((o * r * gn) * (zg * _sigmoid(zg)))
        for h in range(GLA_HEADS):
            st_scr[h] = sts[h]

    rb, in_specs = _gla_in_specs(n_groups, gla_rows, False)
    return pl.pallas_call(
        body, name="gla_fwd", grid=(bsz, n_groups), in_specs=in_specs,
        out_specs=[pl.BlockSpec((gla_rows, GLA_VW), lambda b, n: (rb(b, n), 0)),
                   pl.BlockSpec((gla_rows, GLA_VW), lambda b, n: (rb(b, n), 0)),
                   pl.BlockSpec((1, gla_group, GLA_HEADS, GLA_DV, GLA_DK), lambda b, n: (b, n, 0, 0, 0))],
        out_shape=[jax.ShapeDtypeStruct((tp, GLA_VW), F32), jax.ShapeDtypeStruct((tp, GLA_VW), BF16),
                   jax.ShapeDtypeStruct((bsz, n_chunks, GLA_HEADS, GLA_DV, GLA_DK), F32)],
        scratch_shapes=[pltpu.VMEM((GLA_HEADS, GLA_DV, GLA_DK), F32)],
        compiler_params=_cp(("parallel", "arbitrary"), 56),
    )(proj, proj, proj, proj, proj, gw_pad, gate_b, gla_norm_g)


def _gla_bwd(proj, gw_pad, gate_b, gla_norm_g, o_raw, s_all, d_ya, dproj, bsz, lp):
    n_chunks = lp // GLA_CHUNK
    gla_group = _gla_group(n_chunks)
    gla_rows = gla_group * GLA_CHUNK
    n_groups = n_chunks // gla_group
    tp = bsz * lp

    def body(q_ref, k_ref, v_ref, z_ref, lr_ref, gw_ref, gb_ref, gn_ref, o_ref, s_ref, dya_ref, _,
             dp_ref, dz_ref, dgn_ref, dst_scr):
        dv_ref, dzg_ref = dp_ref.at[:, C_V:C_V + GLA_VW], dp_ref.at[:, C_Z:C_Z + GLA_VW]

        @pl.when(jnp.logical_and(pl.program_id(0) == 0, pl.program_id(1) == 0))
        def _():
            dgn_ref[...] = jnp.zeros_like(dgn_ref)

        @pl.when(pl.program_id(1) == 0)
        def _():
            dst_scr[...] = jnp.zeros_like(dst_scr)

        grp = n_groups - 1 - pl.program_id(1)
        chunks = [slice(j * GLA_CHUNK, (j + 1) * GLA_CHUNK) for j in range(gla_group)]
        cs = [_gla_gates(q_ref, k_ref, lr_ref, gw_ref, gb_ref, rows, True if j else grp > 0)
              for j, rows in enumerate(chunks)]
        gn = gn_ref[...]
        dgn = jnp.zeros((1, GLA_DV), F32)
        dqe_h, dke_h, dkl_h, dbl_h = ([[None] * GLA_HEADS for _ in chunks] for _ in range(4))
        dsts = [dst_scr[h] for h in range(GLA_HEADS)]
        for j in reversed(range(gla_group)):
            rows, c = chunks[j], cs[j]
            for h in range(GLA_HEADS):
                ks, vs = slice(h * GLA_DK, (h + 1) * GLA_DK), slice(h * GLA_DV, (h + 1) * GLA_DV)
                dst = dsts[h]
                v = v_ref[rows, vs]
                st = s_ref[0, j, h]
                o = o_ref[rows, vs]
                r = lax.rsqrt(jnp.mean(o * o, axis=-1, keepdims=True) + EPS)
                xh = o * r
                zg = z_ref[rows, vs].astype(F32)
                sg = _sigmoid(zg)
                dy = dya_ref[rows, vs].astype(F32)
                dzg_ref[rows, vs] = _bf(dy * (xh * gn) * (sg * (1.0 + zg * (1.0 - sg))))
                t = dy * (zg * sg)
                dgn += jnp.sum(t * xh, axis=0, keepdims=True)
                dxh = t * gn
                do_b = _bf(r * (dxh - xh * jnp.mean(dxh * xh, axis=-1, keepdims=True)))
                qe_b, ke_b, kl_b, dst_b = c["qe_b"][:, ks], c["ke_b"][:, ks], c["kl_b"][:, ks], _bf(dst)
                a = jnp.where(c["tril"], _dot_nt(qe_b, ke_b), 0.0)
                da_b = _bf(jnp.where(c["tril"], _dot_nt(do_b, v), 0.0))
                dqe_h[j][h] = _dot(da_b, ke_b) + _dot(do_b, _bf(st))
                dke_h[j][h] = _dot_tn(da_b, qe_b)
                dkl = _dot(v, dst_b)
                dkl_h[j][h] = dkl
                dv_ref[rows, vs] = _bf(_dot_tn(_bf(a), do_b) + _dot_nt(kl_b, dst_b))
                ddecay = jnp.sum(dst * st, axis=0, keepdims=True)
                dbl_h[j][h] = jnp.sum(dkl * c["kl"][:, ks], axis=0, keepdims=True) + ddecay * c["ebl"][:, ks]
                dsts[h] = dst * c["ebl"][:, ks] + _dot_tn(do_b, qe_b)
        for h in range(GLA_HEADS):
            dst_scr[h] = dsts[h]
        dgn_ref[...] += dgn
        ri = lax.broadcasted_iota(jnp.int32, (GLA_CHUNK, GLA_CHUNK), 0)
        ci = lax.broadcasted_iota(jnp.int32, (GLA_CHUNK, GLA_CHUNK), 1)
        triu = _bf((ci >= ri).astype(F32))
        for j, (rows, c) in enumerate(zip(chunks, cs)):
            dqe, dke, dkl, dbl = (jnp.concatenate(p[j], axis=1) for p in (dqe_h, dke_h, dkl_h, dbl_h))
            db = dqe * c["qe"] - dke * c["ke"] - dkl * c["kl"] + jnp.where(c["row"] == GLA_CHUNK - 1, dbl, 0.0)
            dg = _tri_dot(triu, db)
            dg = jnp.where(c["live"], dg, 0.0)
            dz_ref[rows, :] = dg * (1.0 / GLA_GATE_NORMALIZER) * _sigmoid(-c["z"])
            dp_ref[rows, C_Q:C_Q + GLA_KW] = _bf(dqe * c["eb"] * (GLA_DK ** -0.5))
            dp_ref[rows, C_K:C_K + GLA_KW] = _bf(dke * c["enb"] + dkl * c["elb"])

    rb, in_specs = _gla_in_specs(n_groups, gla_rows, True)
    wide = pl.BlockSpec((gla_rows, GLA_VW), lambda b, n: (rb(b, n), 0))
    group = C_MZ
    return pl.pallas_call(
        body, name="gla_bwd", grid=(bsz, n_groups),
        in_specs=in_specs + [wide, pl.BlockSpec((1, gla_group, GLA_HEADS, GLA_DV, GLA_DK),
                                                lambda b, n: (b, n_groups - 1 - n, 0, 0, 0)), wide,
                             pl.BlockSpec(memory_space=pl.ANY)],
        out_specs=[pl.BlockSpec((gla_rows, group), lambda b, n: (rb(b, n), 0)),
                   pl.BlockSpec((gla_rows, GLA_KW), lambda b, n: (rb(b, n), 0)),
                   pl.BlockSpec((1, GLA_DV), lambda b, n: (0, 0))],
        out_shape=[jax.ShapeDtypeStruct((tp, N_EXT), BF16), jax.ShapeDtypeStruct((tp, GLA_KW), F32),
                   jax.ShapeDtypeStruct((1, GLA_DV), F32)],
        input_output_aliases={11: 0},
        scratch_shapes=[pltpu.VMEM((GLA_HEADS, GLA_DV, GLA_DK), F32)],
        compiler_params=_cp(("arbitrary", "arbitrary"), 56),
    )(proj, proj, proj, proj, proj, gw_pad, gate_b, gla_norm_g, o_raw, s_all, d_ya, dproj)


def _gate_bwd(dz, proj, gw_pad):
    tp = dz.shape[0]
    tm = _big_tok(tp)

    def body(dz_ref, lr_ref, gw_ref, dlr_ref, dgw_ref, dgb_ref):
        @pl.when(pl.program_id(0) == 0)
        def _():
            dgw_ref[...] = jnp.zeros_like(dgw_ref)
            dgb_ref[...] = jnp.zeros_like(dgb_ref)

        dz = dz_ref[...]
        dz_b = _bf(dz)
        dlr_ref[...] = _bf(_dot_nt(dz_b, gw_ref[...]))
        dgw_ref[...] += _dot_tn(lr_ref[...], dz_b)
        dgb_ref[...] += jnp.sum(dz, axis=0, keepdims=True)

    return pl.pallas_call(
        body, name="gate_bwd", grid=(tp // tm,),
        in_specs=[pl.BlockSpec((tm, GLA_KW), lambda i: (i, 0)),
                  pl.BlockSpec((tm, LANE), lambda i: (i, C_LR // LANE)),
                  pl.BlockSpec((LANE, GLA_KW), lambda i: (0, 0))],
        out_specs=[pl.BlockSpec((tm, LANE), lambda i: (i, 0)),
                   pl.BlockSpec((LANE, GLA_KW), lambda i: (0, 0)),
                   pl.BlockSpec((1, GLA_KW), lambda i: (0, 0))],
        out_shape=[jax.ShapeDtypeStruct((tp, LANE), BF16), jax.ShapeDtypeStruct((LANE, GLA_KW), F32),
                   jax.ShapeDtypeStruct((1, GLA_KW), F32)],
        compiler_params=_cp(("arbitrary",)),
    )(dz, proj, gw_pad)


def _rms_fwd(x):
    r = lax.rsqrt(jnp.mean(x * x, axis=-1, keepdims=True) + EPS)
    return x * r, r


def _rms_bwd(dy, xh, r, g):
    dxh = dy * g
    dx = r * (dxh - xh * jnp.mean(dxh * xh, axis=-1, keepdims=True))
    return dx, jnp.sum(dy * xh, axis=0, keepdims=True)


def _q_up(proj, q_norm_g, wn, wr, wt, cos_t, sin_t, bsz, lp):
    tp = bsz * lp
    tok = _wide_block(lp)
    nb = lp // tok

    def body(cq_ref, g_ref, wn_ref, wr_ref, wt_ref, cos_ref, sin_ref, q_ref):
        xh, _ = _rms_fwd(cq_ref[...].astype(F32))
        cqn = _bf(xh * g_ref[...])
        nope = _dot(cqn, wn_ref[...])
        rope = _dot(cqn, wr_ref[...])
        rot = _dot(cqn, wt_ref[...])
        cos, sin = cos_ref[...], sin_ref[...]
        one = (lax.broadcasted_iota(jnp.int32, (tok, LANE), 1) == BIAS_LANE).astype(F32)
        for h in range(MLA_HEADS):
            sl = slice(h * LANE, (h + 1) * LANE)
            q_ref[:, h * QKW:h * QKW + LANE] = _bf(nope[:, sl])
            q_ref[:, h * QKW + LANE:(h + 1) * QKW] = _bf(rope[:, sl] * cos + rot[:, sl] * sin + one)

    wspec = pl.BlockSpec((MLA_QR, MLA_HEADS * LANE), lambda b, i: (0, 0))
    tspec = pl.BlockSpec((tok, LANE), lambda b, i: (i, 0))
    return pl.pallas_call(
        body, name="mla_q_up", grid=(bsz, nb),
        in_specs=[pl.BlockSpec((tok, MLA_QR), lambda b, i: (b * nb + i, C_CQ // MLA_QR)),
                  pl.BlockSpec((1, MLA_QR), lambda b, i: (0, 0)), wspec, wspec, wspec, tspec, tspec],
        out_specs=pl.BlockSpec((tok, MLA_HEADS * QKW), lambda b, i: (b * nb + i, 0)),
        out_shape=jax.ShapeDtypeStruct((tp, MLA_HEADS * QKW), BF16),
        compiler_params=_cp(("parallel", "parallel")),
    )(proj, q_norm_g, wn, wr, wt, cos_t, sin_t)


def _kv_up(proj, kv_norm_g, wk, wv, cos_t, sin_t, bsz, lp):
    tp = bsz * lp
    tok = _wide_block(lp)
    nb = lp // tok

    def body(ckv_ref, kr_ref, krot_ref, g_ref, wk_ref, wv_ref, cos_ref, sin_ref, k_ref, v_ref):
        xh, _ = _rms_fwd(ckv_ref[...].astype(F32))
        cn = _bf(xh * g_ref[...])
        kn = _dot(cn, wk_ref[...])
        v_ref[...] = _bf(_dot(cn, wv_ref[...]))
        pos = pl.program_id(1) * tok + lax.broadcasted_iota(jnp.int32, (tok, LANE), 0)
        lane = lax.broadcasted_iota(jnp.int32, (tok, LANE), 1)
        bias = jnp.where(jnp.logical_and(lane == BIAS_LANE, pos < FRONT), KEY_BIAS, 0.0)
        kr = _bf(kr_ref[...].astype(F32) * cos_ref[...] + krot_ref[...].astype(F32) * sin_ref[...] + bias)
        for h in range(MLA_HEADS):
            k_ref[:, h * QKW:h * QKW + LANE] = _bf(kn[:, h * LANE:(h + 1) * LANE])
            k_ref[:, h * QKW + LANE:(h + 1) * QKW] = kr

    wspec = pl.BlockSpec((MLA_KVR, MLA_HEADS * LANE), lambda b, i: (0, 0))
    tspec = pl.BlockSpec((tok, LANE), lambda b, i: (i, 0))
    return pl.pallas_call(
        body, name="mla_kv_up", grid=(bsz, nb),
        in_specs=[pl.BlockSpec((tok, LANE), lambda b, i: (b * nb + i, C_CKV // LANE)),
                  pl.BlockSpec((tok, LANE), lambda b, i: (b * nb + i, C_KR // LANE)),
                  pl.BlockSpec((tok, LANE), lambda b, i: (b * nb + i, C_KROT // LANE)),
                  pl.BlockSpec((1, MLA_KVR), lambda b, i: (0, 0)), wspec, wspec, tspec, tspec],
        out_specs=[pl.BlockSpec((tok, MLA_HEADS * QKW), lambda b, i: (b * nb + i, 0)),
                   pl.BlockSpec((tok, MLA_HEADS * LANE), lambda b, i: (b * nb + i, 0))],
        out_shape=[jax.ShapeDtypeStruct((tp, MLA_HEADS * QKW), BF16),
                   jax.ShapeDtypeStruct((tp, MLA_HEADS * LANE), BF16)],
        compiler_params=_cp(("parallel", "parallel")),
    )(proj, proj, proj, kv_norm_g, wk, wv, cos_t, sin_t)


ATT_SCALE = MLA_QK ** -0.5


KEY_BIAS = -1e30
BIAS_LANE = MLA_ROPE
NEG = 2 * KEY_BIAS
LOG2E = 1.4426950408889634
EXP2_SCALE = ATT_SCALE * LOG2E


def _causal_fill(s, r0, fill):
    tq, kmax = s.shape
    a = r0 // LANE * LANE
    mask = (a + lax.broadcasted_iota(jnp.int32, (tq, kmax - a), 1)
            <= r0 + lax.broadcasted_iota(jnp.int32, (tq, kmax - a), 0))
    right = jnp.where(mask, s[:, a:], fill)
    return jnp.concatenate([s[:, :a], right], axis=1) if a else right


def _attn_fwd(qf, kf, vf, proj, bsz, lp):
    tp = bsz * lp
    tq = _attn_block(lp)
    nh = 2

    def body(q_ref, k_ref, v_ref, mz_ref, ob_ref, yb_ref, lse_ref):
        starts = list(range(0, lp, tq))
        for pair in (starts[i:i + 2] for i in range(0, len(starts), 2)):
            work = [(r0, h) for r0 in pair for h in range(nh)]
            ss = [_causal_fill(_dot_nt(q_ref[r0:r0 + tq, h * QKW:(h + 1) * QKW],
                                       k_ref[0:r0 + tq, h * QKW:(h + 1) * QKW]), r0, NEG) for r0, h in work]
            ms = [jnp.max(s, axis=-1, keepdims=True) for s in ss]
            ps = [jnp.exp2((s - m) * EXP2_SCALE) for s, m in zip(ss, ms)]
            ls = [jnp.sum(p, axis=-1, keepdims=True) for p in ps]
            for (r0, h), p, m, l in zip(work, ps, ms, ls):
                rows, cols = slice(r0, r0 + tq), slice(h * MLA_DV, (h + 1) * MLA_DV)
                o = _dot(_bf(p), v_ref[0:r0 + tq, cols]) / l
                ob_ref[rows, cols] = _bf(o)
                mz = mz_ref[rows, cols].astype(F32)
                yb_ref[rows, cols] = _bf(o * (mz * _sigmoid(mz)))
                lse_ref[0, h, rows, :] = jnp.broadcast_to(m * EXP2_SCALE + jnp.log2(l), (tq, LANE))

    head = lambda off: pl.BlockSpec((lp, nh * MLA_DV), lambda b, h: (b, off + h))
    wide = pl.BlockSpec((lp, nh * QKW), lambda b, h: (b, h))
    return pl.pallas_call(
        body, name="mla_attn_fwd", grid=(bsz, MLA_HEADS // nh),
        in_specs=[wide, wide, head(0), head(C_MZ // (nh * MLA_DV))],
        out_specs=[head(0), head(0), pl.BlockSpec((1, nh, lp, LANE), lambda b, h: (b, h, 0, 0))],
        out_shape=[jax.ShapeDtypeStruct((tp, MLA_HEADS * MLA_DV), BF16),
                   jax.ShapeDtypeStruct((tp, MLA_HEADS * MLA_DV), BF16),
                   jax.ShapeDtypeStruct((bsz, MLA_HEADS, lp, LANE), F32)],
        compiler_params=_cp(("parallel", "parallel"), 56),
    )(qf, kf, vf, proj)


def _attn_bwd_blocks(lp):
    return [(0, X0)] + [(r0, min(MXU_DEPTH, lp - r0)) for r0 in range(X0, lp, MXU_DEPTH)]


def _attn_bwd(qf, kf, vf, d_o, lse, delta, bsz, lp):
    tp = bsz * lp

    def body(q_ref, k_ref, v_ref, do_ref, lse_ref, dl_ref, dq_ref, dk_ref, dv_ref, dk_acc, dv_acc):
        dk_acc[...] = jnp.zeros_like(dk_acc)
        dv_acc[...] = jnp.zeros_like(dv_acc)
        for r0, tq in _attn_bwd_blocks(lp):
            rows, kmax = slice(r0, r0 + tq), r0 + tq
            q, do = q_ref[rows, :], do_ref[rows, :]
            k, v = k_ref[0:kmax, :], v_ref[0:kmax, :]
            p = jnp.exp2(_dot_nt(q, k) * EXP2_SCALE - lse_ref[0, 0, rows, :][:, :1])
            p = _causal_fill(p, r0, 0.0)
            ds = _bf(p * (_dot_nt(do, v) - dl_ref[0, rows, :][:, :1]))
            dq_ref[rows, :] = _bf(_dot(ds, k) * ATT_SCALE)
            dk_acc[0:kmax, :] += _dot_tn(ds, q)
            dv_acc[0:kmax, :] += _dot_tn(_bf(p), do)
        dk_ref[...] = _bf(dk_acc[...] * ATT_SCALE)
        dv_ref[...] = _bf(dv_acc[...])

    wide = pl.BlockSpec((lp, QKW), lambda b, h: (b, h))
    narrow = pl.BlockSpec((lp, MLA_DV), lambda b, h: (b, h))
    stat = pl.BlockSpec((1, 1, lp, LANE), lambda b, h: (b, h, 0, 0))
    return pl.pallas_call(
        body, name="mla_attn_bwd", grid=(bsz, MLA_HEADS),
        in_specs=[wide, wide, narrow, narrow, stat, pl.BlockSpec((1, lp, LANE), lambda b, h: (h, b, 0))],
        out_specs=[wide, wide, narrow],
        out_shape=[jax.ShapeDtypeStruct((tp, MLA_HEADS * QKW), BF16), jax.ShapeDtypeStruct((tp, MLA_HEADS * QKW), BF16),
                   jax.ShapeDtypeStruct((tp, MLA_HEADS * MLA_DV), BF16)],
        scratch_shapes=[pltpu.VMEM((lp, QKW), F32), pltpu.VMEM((lp, MLA_DV), F32)],
        compiler_params=_cp(("parallel", "parallel"), 56),
    )(qf, kf, vf, d_o, lse, delta)


def _q_up_bwd(dqf, proj, q_norm_g, wn, wr, wt, cos_t, sin_t, dproj, bsz, lp):
    tp = bsz * lp
    tok = _wide_block(lp)
    nb = lp // tok
    hw = MLA_HEADS * LANE

    def body(dq_ref, cq_ref, g_ref, wn_ref, wr_ref, wt_ref, cos_ref, sin_ref, _,
             dcq_ref, dwn_ref, dwr_ref, dwt_ref, dg_ref):
        @pl.when(jnp.logical_and(pl.program_id(0) == 0, pl.program_id(1) == 0))
        def _():
            for r in (dwn_ref, dwr_ref, dwt_ref, dg_ref):
                r[...] = jnp.zeros_like(r)

        g = g_ref[...]
        xh, r = _rms_fwd(cq_ref[...].astype(F32))
        cqn = _bf(xh * g)
        dn = jnp.concatenate([dq_ref[:, h * QKW:h * QKW + LANE] for h in range(MLA_HEADS)], axis=1)
        dr = jnp.concatenate([dq_ref[:, h * QKW + LANE:(h + 1) * QKW] for h in range(MLA_HEADS)], axis=1).astype(F32)
        dr_c = _bf(dr * jnp.tile(cos_ref[...], (1, MLA_HEADS)))
        dr_s = _bf(dr * jnp.tile(sin_ref[...], (1, MLA_HEADS)))
        dcqn = _dot_nt(dn, wn_ref[...]) + _dot_nt(dr_c, wr_ref[...]) + _dot_nt(dr_s, wt_ref[...])
        dwn_ref[...] += _dot_tn(cqn, dn)
        dwr_ref[...] += _dot_tn(cqn, dr_c)
        dwt_ref[...] += _dot_tn(cqn, dr_s)
        dx, dg = _rms_bwd(dcqn, xh, r, g)
        dcq_ref[...] = _bf(dx)
        dg_ref[...] += dg

    aspec = pl.BlockSpec((MLA_QR, hw), lambda b, i: (0, 0))
    tspec = pl.BlockSpec((tok, LANE), lambda b, i: (i, 0))
    return pl.pallas_call(
        body, name="mla_q_up_bwd", grid=(bsz, nb),
        in_specs=[pl.BlockSpec((tok, MLA_HEADS * QKW), lambda b, i: (b * nb + i, 0)),
                  pl.BlockSpec((tok, MLA_QR), lambda b, i: (b * nb + i, C_CQ // MLA_QR)),
                  pl.BlockSpec((1, MLA_QR), lambda b, i: (0, 0)), aspec, aspec, aspec, tspec, tspec,
                  pl.BlockSpec(memory_space=pl.ANY)],
        out_specs=[pl.BlockSpec((tok, MLA_QR), lambda b, i: (b * nb + i, C_CQ // MLA_QR)), aspec, aspec, aspec,
                   pl.BlockSpec((1, MLA_QR), lambda b, i: (0, 0))],
        out_shape=[jax.ShapeDtypeStruct((tp, N_EXT), BF16)] + [jax.ShapeDtypeStruct((MLA_QR, hw), F32)] * 3
        + [jax.ShapeDtypeStruct((1, MLA_QR), F32)],
        input_output_aliases={8: 0},
        compiler_params=_cp(("arbitrary", "arbitrary")),
    )(dqf, proj, q_norm_g, wn, wr, wt, cos_t, sin_t, dproj)


def _kv_up_bwd(dkf, dvf, proj, kv_norm_g, wk, wv, cos_t, sin_t, d_lr, dproj, bsz, lp):
    tp = bsz * lp
    tok = _wide_block(lp)
    nb = lp // tok
    hw = MLA_HEADS * LANE

    def body(dk_ref, dv_ref, ckv_ref, g_ref, wk_ref, wv_ref, cos_ref, sin_ref, dlr_ref, _,
             dp_ref, dwk_ref, dwv_ref, dg_ref):
        dckv_ref, dkr_ref, dkrot_ref = (dp_ref.at[:, j * LANE:(j + 1) * LANE] for j in range(3))
        dp_ref[:, 3 * LANE:] = dlr_ref[...]
        @pl.when(jnp.logical_and(pl.program_id(0) == 0, pl.program_id(1) == 0))
        def _():
            for r in (dwk_ref, dwv_ref, dg_ref):
                r[...] = jnp.zeros_like(r)

        g = g_ref[...]
        xh, r = _rms_fwd(ckv_ref[...].astype(F32))
        cn = _bf(xh * g)
        dv = dv_ref[...]
        dn = jnp.concatenate([dk_ref[:, h * QKW:h * QKW + LANE] for h in range(MLA_HEADS)], axis=1)
        dcn = _dot_nt(dv, wv_ref[...]) + _dot_nt(dn, wk_ref[...])
        dwv_ref[...] += _dot_tn(cn, dv)
        dwk_ref[...] += _dot_tn(cn, dn)
        drope = jnp.zeros((tok, LANE), F32)
        for h in range(MLA_HEADS):
            drope += dk_ref[:, h * QKW + LANE:(h + 1) * QKW].astype(F32)
        dkr_ref[...] = _bf(drope * cos_ref[...])
        dkrot_ref[...] = _bf(drope * sin_ref[...])
        dx, dg = _rms_bwd(dcn, xh, r, g)
        dckv_ref[...] = _bf(dx)
        dg_ref[...] += dg

    aspec = pl.BlockSpec((MLA_KVR, hw), lambda b, i: (0, 0))
    tspec = pl.BlockSpec((tok, LANE), lambda b, i: (i, 0))
    ospec = pl.BlockSpec((tok, LANE), lambda b, i: (b * nb + i, 0))
    return pl.pallas_call(
        body, name="mla_kv_up_bwd", grid=(bsz, nb),
        in_specs=[pl.BlockSpec((tok, MLA_HEADS * QKW), lambda b, i: (b * nb + i, 0)),
                  pl.BlockSpec((tok, hw), lambda b, i: (b * nb + i, 0)),
                  pl.BlockSpec((tok, LANE), lambda b, i: (b * nb + i, C_CKV // LANE)),
                  pl.BlockSpec((1, MLA_KVR), lambda b, i: (0, 0)), aspec, aspec, tspec, tspec, ospec,
                  pl.BlockSpec(memory_space=pl.ANY)],
        out_specs=[pl.BlockSpec((tok, 4 * LANE), lambda b, i: (b * nb + i, C_CKV // (4 * LANE))), aspec, aspec,
                   pl.BlockSpec((1, MLA_KVR), lambda b, i: (0, 0))],
        out_shape=[jax.ShapeDtypeStruct((tp, N_EXT), BF16)] + [jax.ShapeDtypeStruct((MLA_KVR, hw), F32)] * 2
        + [jax.ShapeDtypeStruct((1, MLA_KVR), F32)],
        input_output_aliases={9: 0},
        compiler_params=_cp(("arbitrary", "arbitrary")),
    )(dkf, dvf, proj, kv_norm_g, wk, wv, cos_t, sin_t, d_lr, dproj)


def _mid_fwd(ya_in, yb_in, proj, hp, target, w_gp, w_mp, w_o, final_g, bsz, lp):
    tp = bsz * lp
    tm = _wide_block(lp)
    nb = lp // tm
    last = pl.cdiv(lp - X0, tm) - 1

    def body(ya_ref, yb_ref, gg_ref, gm_ref, h_ref, ta_ref, tb_ref, wgp_ref, wmp_ref, wo_ref, fg_ref,
             ya_out, yb_out, dh_ref, loss_ref, dfg_ref):
        @pl.when(jnp.logical_and(pl.program_id(0) == 0, pl.program_id(1) == 0))
        def _():
            loss_ref[...] = jnp.zeros_like(loss_ref)
            dfg_ref[...] = jnp.zeros_like(dfg_ref)

        y_a = _dot(ya_ref[...], wgp_ref[...])
        y_b = _dot(yb_ref[...], wmp_ref[...])
        ya_out[...] = _bf(y_a)
        yb_out[...] = _bf(y_b)
        merged = _sigmoid(gg_ref[...].astype(F32)) * y_a + _sigmoid(gm_ref[...].astype(F32)) * y_b
        h2 = h_ref[...] + _dot(_bf(merged), wo_ref[...])
        fg = fg_ref[...]
        xh, r = _rms_fwd(h2)
        pos = pl.program_id(1) * tm + lax.broadcasted_iota(jnp.int32, (tm, 1), 0)
        t = jnp.concatenate([ta_ref[0, tm - X0:, :], tb_ref[0, :tm - X0, :]], axis=0)
        err = jnp.where(pos >= X0, xh * fg - t, 0.0)
        loss_ref[...] += 0.5 * jnp.sum(jnp.mean(err * err, axis=-1, keepdims=True), axis=0, keepdims=True)
        dy = err * (1.0 / D_MODEL)
        dx, dfg = _rms_bwd(dy, xh, r, fg)
        dh_ref[...] = dx
        dfg_ref[...] += dfg

    tok = lambda c: pl.BlockSpec((tm, D_MODEL), lambda b, i: (b * nb + i, c))
    wspec = pl.BlockSpec((D_MODEL, D_MODEL), lambda b, i: (0, 0), pipeline_mode=pl.Buffered(1))
    return pl.pallas_call(
        body, name="mid_fwd", grid=(bsz, nb),
        in_specs=[tok(0), tok(0), tok(C_GG // D_MODEL), tok(C_GM // D_MODEL), tok(0),
                  pl.BlockSpec((1, tm, D_MODEL), lambda b, i: (b, jnp.maximum(i - 1, 0), 0)),
                  pl.BlockSpec((1, tm, D_MODEL), lambda b, i: (b, jnp.minimum(i, last), 0)),
                  wspec, wspec, wspec, pl.BlockSpec((1, D_MODEL), lambda b, i: (0, 0))],
        out_specs=[tok(0), tok(0), tok(0), pl.BlockSpec((1, LANE), lambda b, i: (0, 0)),
                   pl.BlockSpec((1, D_MODEL), lambda b, i: (0, 0))],
        out_shape=[jax.ShapeDtypeStruct((tp, D_MODEL), BF16), jax.ShapeDtypeStruct((tp, D_MODEL), BF16),
                   jax.ShapeDtypeStruct((tp, D_MODEL), F32), jax.ShapeDtypeStruct((1, LANE), F32),
                   jax.ShapeDtypeStruct((1, D_MODEL), F32)],
        compiler_params=_cp(("arbitrary", "arbitrary"), 56),
    )(ya_in, yb_in, proj, proj, hp, target, target, w_gp, w_mp, w_o, final_g)


def _mid_bwd(dh2, y_a, y_b, proj, ya_in, yb_in, o_b, w_o, w_gp, w_mp, bsz, lp):
    tp = bsz * lp
    tm = MXU_DEPTH if tp % MXU_DEPTH == 0 else _attn_block(lp)
    nsteps = tp // tm
    group = 3 * D_MODEL

    def body(dh_ref, ya_ref, yb_ref, mz_ref, gg_ref, gm_ref, yai_ref, ybi_ref, ob_ref, wo_ref, wgp_ref, wmp_ref,
             dyai_ref, do_ref, dp_ref, dl_ref, dwo_ref, dwgp_ref, dwmp_ref, a_o, a_gp, a_mp):
        @pl.when(pl.program_id(0) == 0)
        def _():
            for r in (a_o, a_gp, a_mp):
                r[...] = jnp.zeros_like(r)

        dh = _bf(dh_ref[...])
        dm = _dot_nt(dh, wo_ref[...])
        y_a, y_b = ya_ref[...].astype(F32), yb_ref[...].astype(F32)
        sg, sm = _sigmoid(gg_ref[...].astype(F32)), _sigmoid(gm_ref[...].astype(F32))
        d_ya, d_yb = _bf(sg * dm), _bf(sm * dm)
        dp_ref[:, D_MODEL:2 * D_MODEL] = _bf(dm * y_a * sg * (1.0 - sg))
        dp_ref[:, 2 * D_MODEL:] = _bf(dm * y_b * sm * (1.0 - sm))
        merged = _bf(sg * y_a + sm * y_b)
        dy = _dot_nt(d_yb, wmp_ref[...])
        dyai_ref[...] = _bf(_dot_nt(d_ya, wgp_ref[...]))
        a_o[...] += _dot_tn(merged, dh)
        a_gp[...] += _dot_tn(yai_ref[...], d_ya)
        a_mp[...] += _dot_tn(ybi_ref[...], d_yb)
        mz, o = mz_ref[...].astype(F32), ob_ref[...].astype(F32)
        s = _sigmoid(mz)
        do = _bf(dy * (mz * s))
        do_ref[...] = do
        dp_ref[:, :D_MODEL] = _bf(dy * o * (s * (1.0 + mz * (1.0 - s))))
        prod = do.astype(F32) * o
        for h in range(MLA_HEADS):
            dl = jnp.sum(prod[:, h * MLA_DV:(h + 1) * MLA_DV], axis=-1, keepdims=True)
            dl_ref[h] = jnp.broadcast_to(dl, (tm, LANE))

        @pl.when(pl.program_id(0) == nsteps - 1)
        def _():
            pltpu.sync_copy(a_o, dwo_ref)
            pltpu.sync_copy(a_gp, dwgp_ref)
            pltpu.sync_copy(a_mp, dwmp_ref)

    tok = lambda c: pl.BlockSpec((tm, D_MODEL), lambda i: (i, c))
    wspec = pl.BlockSpec((D_MODEL, D_MODEL), lambda i: (0, 0))
    anyspec = pl.BlockSpec(memory_space=pl.ANY)
    wshape = jax.ShapeDtypeStruct((D_MODEL, D_MODEL), F32)
    return pl.pallas_call(
        body, name="mid_bwd", grid=(nsteps,),
        in_specs=[tok(0), tok(0), tok(0), tok(C_MZ // D_MODEL), tok(C_GG // D_MODEL), tok(C_GM // D_MODEL),
                  tok(0), tok(0), tok(0), wspec, wspec, wspec],
        out_specs=[tok(0), tok(0), pl.BlockSpec((tm, group), lambda i: (i, C_MZ // group)),
                   pl.BlockSpec((MLA_HEADS, tm, LANE), lambda i: (0, i, 0)), anyspec, anyspec, anyspec],
        out_shape=[jax.ShapeDtypeStruct((tp, D_MODEL), BF16)] * 2 + [jax.ShapeDtypeStruct((tp, N_EXT), BF16),
                   jax.ShapeDtypeStruct((MLA_HEADS, tp, LANE), F32)] + [wshape] * 3,
        scratch_shapes=[pltpu.VMEM((D_MODEL, D_MODEL), F32)] * 3,
        compiler_params=_cp(("arbitrary",), 56),
    )(dh2, y_a, y_b, proj, proj, proj, ya_in, yb_in, o_b, w_o, w_gp, w_mp)


MESH_ID = pl.DeviceIdType.MESH
EXCHANGE_SEMS = [pltpu.SemaphoreType.DMA((N_DEV - 1,)), pltpu.SemaphoreType.DMA((N_DEV - 1,)), pltpu.SemaphoreType.DMA]


def _my_place():
    return lax.axis_index("x"), lax.axis_index("y"), lax.axis_index("c")


def _exchange(g_ref, recv_ref, send_sems, recv_sems, local_sem, start, same=False):
    x, y, c = _my_place()
    me = 4 * x + 2 * y + c
    own = pltpu.make_async_copy(g_ref if same else g_ref.at[me], recv_ref.at[me], local_sem)
    sends, lands = [], []
    for d in range(1, N_DEV):
        px = 1 - x if d & 4 else x
        py = 1 - y if d & 2 else y
        pc = 1 - c if d & 1 else c
        peer = 4 * px + 2 * py + pc
        for slot, group in ((me, sends),) if start else ((me, sends), (peer, lands)):
            group.append(pltpu.make_async_remote_copy(
                src_ref=g_ref if same else g_ref.at[peer], dst_ref=recv_ref.at[slot], send_sem=send_sems.at[d - 1],
                recv_sem=recv_sems.at[d - 1], device_id=(px, py, pc), device_id_type=MESH_ID))
    if start:
        own.start()
        for cp in sends:
            cp.start()
    else:
        for cp in lands:
            cp.wait_recv()
        for cp in sends:
            cp.wait_send()
        own.wait()


def _dw_in(u, dproj, slabs):
    tp = u.shape[0]
    tn = 3 * LANE
    nj = N_EXT // tn

    def body(u_ref, d_ref, g_ref, o_ref, recv_ref, send_sems, recv_sems, local_sem):
        j = pl.program_id(0)

        @pl.when(j == 0)
        def _():
            _exchange(g_ref, recv_ref, send_sems, recv_sems, local_sem, True)

        o_ref[...] = _bf(_dot_tn(d_ref[...], u_ref[...]))

        @pl.when(j == nj - 1)
        def _():
            _exchange(g_ref, recv_ref, send_sems, recv_sems, local_sem, False)

    anyspec = pl.BlockSpec(memory_space=pl.ANY)
    return pl.pallas_call(
        body, name="dw_in", grid=(nj,),
        in_specs=[pl.BlockSpec((tp, D_MODEL), lambda j: (0, 0), pipeline_mode=pl.Buffered(1)),
                  pl.BlockSpec((tp, tn), lambda j: (0, j)), anyspec],
        out_specs=[pl.BlockSpec((tn, D_MODEL), lambda j: (j, 0)), anyspec],
        out_shape=[jax.ShapeDtypeStruct((N_EXT, D_MODEL), BF16), jax.ShapeDtypeStruct(slabs.shape, slabs.dtype)],
        scratch_shapes=EXCHANGE_SEMS,
        compiler_params=_cp(("arbitrary",), 56),
    )(u, dproj, slabs)


def _dx_in(dproj, w_ext, hp, dh2, norm_g, slabs):
    tp = hp.shape[0]
    tm = 2 * TOK
    ni = tp // tm

    def body(d_ref, w_ref, h_ref, dh_ref, g_ref, s_ref, o_ref, dg_ref, recv_ref, send_sems, recv_sems, local_sem):
        i = pl.program_id(0)

        @pl.when(i == 0)
        def _():
            _exchange(s_ref, recv_ref, send_sems, recv_sems, local_sem, True)
            dg_ref[...] = jnp.zeros_like(dg_ref)

        du = _dot_nt(d_ref[...], w_ref[...])
        g = g_ref[...]
        xh, r = _rms_fwd(h_ref[...])
        dx, dg = _rms_bwd(du, xh, r, g)
        o_ref[...] = dh_ref[...] + dx
        dg_ref[...] += dg

        @pl.when(i == ni - 1)
        def _():
            _exchange(s_ref, recv_ref, send_sems, recv_sems, local_sem, False)

    tok = pl.BlockSpec((tm, D_MODEL), lambda i: (i, 0))
    anyspec = pl.BlockSpec(memory_space=pl.ANY)
    return pl.pallas_call(
        body, name="dx_in", grid=(ni,),
        in_specs=[pl.BlockSpec((tm, N_EXT), lambda i: (i, 0)),
                  pl.BlockSpec((D_MODEL, N_EXT), lambda i: (0, 0), pipeline_mode=pl.Buffered(1)),
                  tok, tok, pl.BlockSpec((1, D_MODEL), lambda i: (0, 0)), anyspec],
        out_specs=[tok, pl.BlockSpec((1, D_MODEL), lambda i: (0, 0)), anyspec],
        out_shape=[jax.ShapeDtypeStruct((tp, D_MODEL), F32), jax.ShapeDtypeStruct((1, D_MODEL), F32),
                   jax.ShapeDtypeStruct(slabs.shape, slabs.dtype)],
        scratch_shapes=EXCHANGE_SEMS,
        compiler_params=_cp(("arbitrary",), 56),
    )(dproj, w_ext, hp, dh2, norm_g, slabs)


def _meta_grad(dhp3):
    bsz = dhp3.shape[0]

    def body(d_ref, o_ref):
        @pl.when(pl.program_id(0) == 0)
        def _():
            o_ref[...] = jnp.zeros_like(o_ref)

        o_ref[...] += d_ref[0]

    return pl.pallas_call(
        body, name="meta_grad", grid=(bsz,),
        in_specs=[pl.BlockSpec((1, N_META, D_MODEL), lambda b: (b, FRONT // N_META, 0))],
        out_specs=pl.BlockSpec((N_META, D_MODEL), lambda b: (0, 0)),
        out_shape=jax.ShapeDtypeStruct((N_META, D_MODEL), F32),
        compiler_params=_cp(("arbitrary",)),
    )(dhp3)


W_IN_SHARD = N_IN // N_DEV


def _pad_lanes(a, width=LANE):
    return jnp.pad(a, [(0, 0)] * (a.ndim - 1) + [(0, width - a.shape[-1])])


def _rot_cols(w):
    half = w.shape[-1] // 2
    return jnp.concatenate([-w[..., half:], w[..., :half]], axis=-1)


def _unrot_cols(dw):
    half = dw.shape[-1] // 2
    return jnp.concatenate([dw[..., half:], -dw[..., :half]], axis=-1)


def _w_in_cols(shards, lo, hi):
    parts = []
    for k in range(lo // W_IN_SHARD, (hi - 1) // W_IN_SHARD + 1):
        a, b = max(lo, k * W_IN_SHARD), min(hi, (k + 1) * W_IN_SHARD)
        parts.append(shards[k][:, a - k * W_IN_SHARD:b - k * W_IN_SHARD])
    return parts[0] if len(parts) == 1 else jnp.concatenate(parts, axis=1)


def _w_in_ext(shards):
    c = lambda lo, hi: _w_in_cols(shards, lo, hi)
    kr = c(O_KR, O_MZ)
    return jnp.concatenate([
        c(O_V, O_LR), c(O_Z, O_CQ), c(O_Q, O_K), c(O_K, O_V), c(O_MZ, O_GG), c(O_GG, O_GM), c(O_GM, N_IN),
        c(O_CKV, O_KR), _pad_lanes(kr), _pad_lanes(_rot_cols(kr)), _pad_lanes(c(O_LR, O_Z)), c(O_CQ, O_CKV)], axis=1)


def _w_in_grad_t(dwt):
    g = lambda start, width: dwt[start:start + width]
    half = MLA_ROPE // 2
    krot = g(C_KROT, MLA_ROPE)
    kr = g(C_KR, MLA_ROPE) + jnp.concatenate([krot[half:], -krot[:half]], axis=0)
    return jnp.concatenate([
        g(C_Q, GLA_KW), g(C_K, GLA_KW), g(C_V, GLA_VW), g(C_LR, GLA_RANK), g(C_Z, GLA_VW), g(C_CQ, MLA_QR),
        g(C_CKV, MLA_KVR), kr, g(C_MZ, D_MODEL), g(C_GG, D_MODEL), g(C_GM, D_MODEL)], axis=0)


def _rope_tables(lp):
    inv = 1.0 / (ROPE_BASE ** (jnp.arange(0, MLA_ROPE, 2, dtype=F32) / MLA_ROPE))
    ang = (jnp.arange(lp, dtype=F32) - FRONT)[:, None] * inv[None, :]
    cos, sin = jnp.cos(ang), jnp.sin(ang)
    return _pad_lanes(jnp.concatenate([cos, cos], axis=1)), _pad_lanes(jnp.concatenate([sin, sin], axis=1))


def _local_step(x, loss_target, w):
    bsz, seq, _ = x.shape
    lp = X0 + seq
    tp = bsz * lp
    assert lp % TOK == 0 and (lp // GLA_CHUNK) % _gla_group(lp // GLA_CHUNK) == 0
    head = jnp.concatenate([jnp.zeros((FRONT, D_MODEL), F32), w["meta_tokens"]], axis=0)
    cos_t, sin_t = _rope_tables(lp)

    w_ext = _w_in_ext(w["w_in"])
    hp, u, proj, packed_all = _proj_in(x, head, w["norm_g"], w_ext, w["packed"])
    gathered = _unpack_shards(packed_all)
    for n, _, axis in PACKED:
        w[n] = _join8(gathered[n], axis)
    gw_pad = jnp.pad(w["gla_gate_w"], ((0, LANE - GLA_RANK), (0, 0)))
    uq = w["mla_w_uq"].reshape(MLA_QR, MLA_HEADS, MLA_QK)
    rope_w = uq[:, :, MLA_NOPE:]
    hw = MLA_HEADS * LANE
    wn = uq[:, :, :MLA_NOPE].reshape(MLA_QR, hw)
    wr = _pad_lanes(rope_w).reshape(MLA_QR, hw)
    wt = _pad_lanes(_rot_cols(rope_w)).reshape(MLA_QR, hw)
    ukv = w["mla_w_ukv"].reshape(MLA_KVR, MLA_HEADS, MLA_NOPE + MLA_DV)
    wk = ukv[:, :, :MLA_NOPE].reshape(MLA_KVR, hw)
    wv = ukv[:, :, MLA_NOPE:].reshape(MLA_KVR, hw)

    o_raw, ya_in, s_all = _gla_fwd(proj, gw_pad, w["gla_gate_b"], w["gla_norm_g"], bsz, lp)
    qf = _q_up(proj, w["mla_q_norm_g"], wn, wr, wt, cos_t, sin_t, bsz, lp)
    kf, vf = _kv_up(proj, w["mla_kv_norm_g"], wk, wv, cos_t, sin_t, bsz, lp)
    o_b, yb_in, lse = _attn_fwd(qf, kf, vf, proj, bsz, lp)
    y_a, y_b, dh2, loss, d_final_g = _mid_fwd(ya_in, yb_in, proj, hp, loss_target, w["gla_proj"], w["mla_proj"],
                                              w["w_out"], w["final_norm_g"], bsz, lp)
    d_ya, d_o, dproj, delta, d_w_out, d_gla_proj, d_mla_proj = _mid_bwd(
        dh2, y_a, y_b, proj, ya_in, yb_in, o_b, w["w_out"], w["gla_proj"], w["mla_proj"], bsz, lp)
    dproj, d_gate, d_gla_norm = _gla_bwd(proj, gw_pad, w["gla_gate_b"], w["gla_norm_g"], o_raw, s_all, d_ya, dproj,
                                         bsz, lp)
    d_lr, d_gw_pad, d_gate_b = _gate_bwd(d_gate, proj, gw_pad)
    dqf, dkf, dvf = _attn_bwd(qf, kf, vf, d_o, lse, delta, bsz, lp)
    dproj, d_wn, d_wr, d_wt, d_qn = _q_up_bwd(dqf, proj, w["mla_q_norm_g"], wn, wr, wt, cos_t, sin_t, dproj,
                                              bsz, lp)
    dproj, d_wk, d_wv, d_kvn = _kv_up_bwd(dkf, dvf, proj, w["mla_kv_norm_g"], wk, wv, cos_t, sin_t, d_lr, dproj,
                                          bsz, lp)

    d_rope = (d_wr.reshape(MLA_QR, MLA_HEADS, LANE)[:, :, :MLA_ROPE]
              + _unrot_cols(d_wt.reshape(MLA_QR, MLA_HEADS, LANE)[:, :, :MLA_ROPE]))
    d_uq = jnp.concatenate([d_wn.reshape(MLA_QR, MLA_HEADS, LANE), d_rope], axis=-1).reshape(MLA_QR, MLA_HEADS * MLA_QK)
    d_ukv = jnp.concatenate([d_wk.reshape(MLA_KVR, MLA_HEADS, LANE), d_wv.reshape(MLA_KVR, MLA_HEADS, LANE)],
                            axis=-1).reshape(MLA_KVR, MLA_HEADS * (MLA_NOPE + MLA_DV))
    mats = dict(gla_gate_w=d_gw_pad[:GLA_RANK], gla_proj=d_gla_proj, mla_w_uq=d_uq, mla_w_ukv=d_ukv,
                mla_proj=d_mla_proj, w_out=d_w_out)
    packed = _pack_shards({n: _bf(_split8(mats[n], axis)) for n, _, axis in PACKED})
    d_w_ext_t, packed_parts = _dw_in(u, dproj, packed)
    w_in_slabs = _w_in_grad_t(d_w_ext_t).reshape(N_DEV, W_IN_SHARD, D_MODEL)
    d_hp, d_norm_g, w_in_parts = _dx_in(dproj, w_ext, hp, dh2, w["norm_g"], w_in_slabs)
    d_hp3 = d_hp.reshape(bsz, lp, D_MODEL)
    small = dict(meta_tokens=_meta_grad(d_hp3), norm_g=d_norm_g, gla_gate_b=d_gate_b, gla_norm_g=d_gla_norm,
                 mla_q_norm_g=d_qn, mla_kv_norm_g=d_kvn, final_norm_g=d_final_g)
    return loss, d_hp3[:, X0:, :], w_in_parts, packed_parts, small


PACKED = (("gla_gate_w", (GLA_RANK, GLA_KW // N_DEV), 1),
          ("gla_proj", (D_MODEL // N_DEV, D_MODEL), 0), ("mla_w_uq", (MLA_QR, MLA_HEADS * MLA_QK // N_DEV), 1),
          ("mla_w_ukv", (MLA_KVR, MLA_HEADS * (MLA_NOPE + MLA_DV) // N_DEV), 1),
          ("mla_proj", (D_MODEL // N_DEV, D_MODEL), 0), ("w_out", (D_MODEL // N_DEV, D_MODEL), 0))
REPLICATED = (("norm_g", D_MODEL), ("gla_gate_b", GLA_KW), ("gla_norm_g", GLA_DV), ("mla_q_norm_g", MLA_QR),
              ("mla_kv_norm_g", MLA_KVR), ("final_norm_g", D_MODEL))
PACK_ROWS = 480
PACK_BLOCK = 160
SMALL_ROWS = 48
LOSS_ROW = N_META + 25
W_IN_BLOCK = 128


def _all_gather(shards):
    n_arr = len(shards)
    pieces = []
    for a, s in enumerate(shards):
        step = s.shape[0] // 4 if s.shape[0] >= 4 * LANE else s.shape[0]
        pieces += [(a, slice(r, r + step)) for r in range(0, s.shape[0], step)]
    n_pc = len(pieces)

    def body(*refs):
        x_refs, out_refs = refs[:n_arr], refs[n_arr:2 * n_arr]
        send_sems, recv_sems, local_sems = refs[2 * n_arr:]
        x, y, c = _my_place()
        me, sibling = (x, y, c), (x, y, 1 - c)
        chips = [(1 - x, y), (x, 1 - y), (1 - x, 1 - y)]

        def copy(u, k, block, to, from_input=False):
            a, rows = pieces[u]
            slab = out_refs[a].at[4 * block[0] + 2 * block[1] + block[2], rows]
            return pltpu.make_async_remote_copy(
                src_ref=x_refs[a].at[rows] if from_input else slab, dst_ref=slab,
                send_sem=send_sems.at[7 * u + k], recv_sem=recv_sems.at[7 * u + k], device_id=to,
                device_id_type=MESH_ID)

        arrays = range(n_pc)
        mine = [pltpu.make_async_copy(x_refs[a], out_refs[a].at[4 * x + 2 * y + c], local_sems.at[a])
                for a in range(n_arr)]
        for cp in mine:
            cp.start()
        first = [copy(a, 0, me, sibling, True) for a in arrays]
        first += [copy(a, 1 + j, me, (*chip, c), True) for j, chip in enumerate(chips) for a in arrays]
        for cp in first:
            cp.start()
        passed = []
        for j, chip in enumerate(chips):
            for a in arrays:
                copy(a, 1 + j, (*chip, c), me).wait_recv()
                passed.append(copy(a, 4 + j, (*chip, c), sibling))
                passed[-1].start()
        for a in arrays:
            copy(a, 0, sibling, me).wait_recv()
        for j, chip in enumerate(chips):
            for a in arrays:
                copy(a, 4 + j, (*chip, 1 - c), me).wait_recv()
        for cp in first + passed:
            cp.wait_send()
        for cp in mine:
            cp.wait()

    anyspec = pl.BlockSpec(memory_space=pl.ANY)
    return pl.pallas_call(
        body, name="weights_all_gather",
        out_shape=[jax.ShapeDtypeStruct((N_DEV,) + s.shape, s.dtype) for s in shards],
        in_specs=[anyspec] * n_arr, out_specs=[anyspec] * n_arr,
        scratch_shapes=[pltpu.SemaphoreType.DMA((7 * n_pc,)), pltpu.SemaphoreType.DMA((7 * n_pc,)),
                        pltpu.SemaphoreType.DMA((n_arr,))],
    )(*shards)


def _small_exchange(slabs):
    def body(g_ref, recv_ref, send_sems, recv_sems, local_sem):
        _exchange(g_ref, recv_ref, send_sems, recv_sems, local_sem, True)
        _exchange(g_ref, recv_ref, send_sems, recv_sems, local_sem, False)

    vmem = pl.BlockSpec(memory_space=pltpu.VMEM)
    return pl.pallas_call(
        body, name="small_exchange", out_shape=jax.ShapeDtypeStruct(slabs.shape, slabs.dtype),
        in_specs=[vmem], out_specs=vmem, scratch_shapes=EXCHANGE_SEMS,
    )(slabs)


def _adamw(parts, w, m, v, block_rows, name):
    rows, cols = w.shape

    def body(p_ref, w_ref, m_ref, v_ref, g_out, d_out, m_out, v_out):
        g = p_ref[0].astype(F32)
        for s in range(1, N_DEV):
            g = g + p_ref[s].astype(F32)
        m_new = ADAM_B1 * m_ref[...] + (1.0 - ADAM_B1) * g
        v_new = ADAM_B2 * v_ref[...] + (1.0 - ADAM_B2) * (g * g)
        m_hat = m_new / (1.0 - ADAM_B1 ** ADAM_STEP)
        v_hat = v_new / (1.0 - ADAM_B2 ** ADAM_STEP)
        g_out[...] = g
        d_out[...] = -ADAM_LR * (m_hat / (jnp.sqrt(v_hat) + ADAM_EPS) + ADAM_WD * w_ref[...])
        m_out[...] = m_new
        v_out[...] = v_new

    spec = pl.BlockSpec((block_rows, cols), lambda i: (i, 0))
    return pl.pallas_call(
        body, name=name, grid=(pl.cdiv(rows, block_rows),),
        in_specs=[pl.BlockSpec((N_DEV, block_rows, cols), lambda i: (0, i, 0)), spec, spec, spec],
        out_specs=[spec] * 4, out_shape=[jax.ShapeDtypeStruct((rows, cols), F32)] * 4,
        compiler_params=_cp(("parallel",), 48),
    )(parts, w, m, v)


def _pack_rows_of(shape):
    rows = shape[0] * shape[1] // D_MODEL
    return -(-rows // 16) * 16


def _pack_shards(shards):
    parts = []
    for n, shape, _ in PACKED:
        a = shards[n]
        lead = a.shape[:-2]
        if shape[1] != D_MODEL:
            a = a.reshape(lead + (shape[0] * shape[1] // D_MODEL, D_MODEL))
        pad = _pack_rows_of(shape) - a.shape[-2]
        parts.append(jnp.pad(a, [(0, 0)] * len(lead) + [(0, pad), (0, 0)]) if pad else a)
    return jnp.concatenate(parts, axis=-2)


def _unpack_shards(packed):
    lead, out, off = packed.shape[:-2], {}, 0
    for n, shape, _ in PACKED:
        rows = shape[0] * shape[1] // D_MODEL
        out[n] = packed[..., off:off + rows, :].reshape(lead + shape)
        off += _pack_rows_of(shape)
    return out


def _split8(full, axis):
    r, c = full.shape
    if axis == 0:
        return full.reshape(N_DEV, r // N_DEV, c)
    return full.reshape(r, N_DEV, c // N_DEV).transpose(1, 0, 2)


def _join8(shards, axis):
    _, r, c = shards.shape
    if axis == 0:
        return shards.reshape(N_DEV * r, c)
    return shards.transpose(1, 0, 2).reshape(r, N_DEV * c)


def _pack_small(meta_shard, vals, loss_row):
    rows = jnp.concatenate([vals[n].reshape(-1, LANE) for n, _ in REPLICATED] + [loss_row], axis=0)
    rows = jnp.pad(rows, ((0, SMALL_ROWS - N_META - rows.shape[0]), (0, 0)))
    return jnp.concatenate([meta_shard, jnp.broadcast_to(rows, meta_shard.shape[:-2] + rows.shape)], axis=-2)


def _unpack_small(packed):
    out, off = {"meta_tokens": packed[:N_META]}, N_META
    for n, size in REPLICATED:
        out[n] = packed[off:off + size // LANE].reshape(1, size)
        off += size // LANE
    return out


def kernel(x, meta_tokens, norm_g, w_in, gla_gate_w, gla_gate_b, gla_norm_g, gla_proj, mla_q_norm_g, mla_w_uq, mla_kv_norm_g, mla_w_ukv, mla_proj, w_out, final_norm_g, loss_target, m_meta_tokens, m_norm_g, m_w_in, m_gla_gate_w, m_gla_gate_b, m_gla_norm_g, m_gla_proj, m_mla_q_norm_g, m_mla_w_uq, m_mla_kv_norm_g, m_mla_w_ukv, m_mla_proj, m_w_out, m_final_norm_g, v_meta_tokens, v_norm_g, v_w_in, v_gla_gate_w, v_gla_gate_b, v_gla_norm_g, v_gla_proj, v_mla_q_norm_g, v_mla_w_uq, v_mla_kv_norm_g, v_mla_w_ukv, v_mla_proj, v_w_out, v_final_norm_g):
    given = dict(meta_tokens=meta_tokens, norm_g=norm_g, w_in=w_in, gla_gate_w=gla_gate_w, gla_gate_b=gla_gate_b,
                 gla_norm_g=gla_norm_g, gla_proj=gla_proj, mla_q_norm_g=mla_q_norm_g, mla_w_uq=mla_w_uq,
                 mla_kv_norm_g=mla_kv_norm_g, mla_w_ukv=mla_w_ukv, mla_proj=mla_proj, w_out=w_out,
                 final_norm_g=final_norm_g)
    mom_m = dict(meta_tokens=m_meta_tokens, norm_g=m_norm_g, w_in=m_w_in, gla_gate_w=m_gla_gate_w,
                 gla_gate_b=m_gla_gate_b, gla_norm_g=m_gla_norm_g, gla_proj=m_gla_proj, mla_q_norm_g=m_mla_q_norm_g,
                 mla_w_uq=m_mla_w_uq, mla_kv_norm_g=m_mla_kv_norm_g, mla_w_ukv=m_mla_w_ukv, mla_proj=m_mla_proj,
                 w_out=m_w_out, final_norm_g=m_final_norm_g)
    mom_v = dict(meta_tokens=v_meta_tokens, norm_g=v_norm_g, w_in=v_w_in, gla_gate_w=v_gla_gate_w,
                 gla_gate_b=v_gla_gate_b, gla_norm_g=v_gla_norm_g, gla_proj=v_gla_proj, mla_q_norm_g=v_mla_q_norm_g,
                 mla_w_uq=v_mla_w_uq, mla_kv_norm_g=v_mla_kv_norm_g, mla_w_ukv=v_mla_w_ukv, mla_proj=v_mla_proj,
                 w_out=v_w_out, final_norm_g=v_final_norm_g)
    shapes = {n: a.shape for n, a in given.items()}
    shard2d = {n: s for n, s, _ in PACKED}
    shard2d["w_in"] = (D_MODEL, W_IN_SHARD)
    shard2d["meta_tokens"] = (N_META, LANE)

    def as2d(tree):
        out = {n: tree[n].reshape(shard2d[n]) for n in shard2d}
        out.update({n: tree[n].reshape(1, size) for n, size in REPLICATED})
        return out

    w_loc, m_loc, v_loc = as2d(given), as2d(mom_m), as2d(mom_v)

    w_in_all, meta_all = _all_gather([w_loc["w_in"].astype(BF16), w_loc["meta_tokens"]])
    packed = _pack_shards({n: w_loc[n].astype(BF16) for n, _, _ in PACKED})
    full = {"w_in": w_in_all, "meta_tokens": _join8(meta_all, 1), "packed": packed}
    for n, _ in REPLICATED:
        full[n] = w_loc[n]

    loss_part, grad_x, w_in_parts, packed_parts, small = _local_step(x, loss_target, full)
    small_all = _small_exchange(_pack_small(_split8(small["meta_tokens"], 1), small,
                                            jnp.broadcast_to(loss_part[:, :1], (1, LANE))))

    w_in_t = [t["w_in"].T for t in (w_loc, m_loc, v_loc)]
    g_w, d_w, m_w, v_w = (o.T for o in _adamw(w_in_parts, *w_in_t, W_IN_BLOCK, "adamw_w_in"))
    g_p, d_p, m_p, v_p = _adamw(packed_parts, _pack_shards(w_loc), _pack_shards(m_loc), _pack_shards(v_loc),
                                PACK_BLOCK, "adamw_packed")
    zero_row = jnp.zeros((1, LANE), F32)
    g_s, d_s, m_s, v_s = _adamw(small_all, *(_pack_small(t["meta_tokens"], t, zero_row) for t in (w_loc, m_loc, v_loc)),
                                SMALL_ROWS, "adamw_small")
    loss = g_s[LOSS_ROW, 0]

    order = ["meta_tokens", "norm_g", "w_in", "gla_gate_w", "gla_gate_b", "gla_norm_g", "gla_proj", "mla_q_norm_g",
             "mla_w_uq", "mla_kv_norm_g", "mla_w_ukv", "mla_proj", "w_out", "final_norm_g"]
    result = [loss, grad_x]
    for w_in_out, packed_sh, packed_sm in ((g_w, g_p, g_s), (d_w, d_p, d_s), (m_w, m_p, m_s), (v_w, v_p, v_s)):
        tree = _unpack_shards(packed_sh)
        tree.update(_unpack_small(packed_sm))
        tree["w_in"] = w_in_out
        result += [tree[n].reshape(shapes[n]) for n in order]
    return tuple(result)
```

```python
import jax
import jax.numpy as jnp
from jax import lax
from jax.experimental import pallas as pl
from jax.experimental.pallas import tpu as pltpu

F32 = jnp.float32
BF16 = jnp.bfloat16

D_MODEL = 1024
N_META = 16
EPS = 1e-6
FRONT = 48
X0 = FRONT + N_META
GLA_HEADS, GLA_DK, GLA_DV, GLA_RANK, GLA_CHUNK = 4, 128, 256, 16, 64
GLA_GATE_NORMALIZER = 16.0
GLA_KW = GLA_HEADS * GLA_DK
GLA_VW = GLA_HEADS * GLA_DV
MLA_HEADS, MLA_NOPE, MLA_ROPE, MLA_DV, MLA_QR, MLA_KVR = 8, 128, 64, 128, 256, 128
MLA_QK = MLA_NOPE + MLA_ROPE
ROPE_BASE = 10000.0
LANE = 128
QKW = 2 * LANE

C_V, C_Z, C_Q, C_K = 0, 1024, 2048, 2560
C_MZ, C_GG, C_GM = 3072, 4096, 5120
C_CKV, C_KR, C_KROT, C_LR = 6144, 6272, 6400, 6528
C_CQ = 6656
N_EXT = 6912
O_Q, O_K, O_V, O_LR, O_Z, O_CQ, O_CKV, O_KR, O_MZ, O_GG, O_GM, N_IN = (
    0, 512, 1024, 2048, 2064, 3088, 3344, 3472, 3536, 4560, 5584, 6608)

ADAM_LR, ADAM_B1, ADAM_B2, ADAM_EPS, ADAM_WD, ADAM_STEP = 0.001, 0.9, 0.999, 1e-08, 0.01, 10

N_DEV = 8
TOK = 192
ATT_BLOCK = 352
MXU_DEPTH = 256


def _cp(sems=None, vmem_mb=None):
    kw = {}
    if sems is not None:
        kw["dimension_semantics"] = sems
    if vmem_mb is not None:
        kw["vmem_limit_bytes"] = vmem_mb * 1024 * 1024
    return pltpu.CompilerParams(**kw)


def _dot(a, b):
    return jnp.dot(a, b, preferred_element_type=F32)


def _dot_nt(a, b):
    return lax.dot_general(a, b, (((1,), (1,)), ((), ())), preferred_element_type=F32)


def _dot_tn(a, b):
    return lax.dot_general(a, b, (((0,), (0,)), ((), ())), preferred_element_type=F32)


def _sigmoid(x):
    return 1.0 / (1.0 + jnp.exp(-x))


def _bf(x):
    return x.astype(BF16)


def _big_tok(tp):
    return 4 * TOK if tp % (4 * TOK) == 0 else TOK


def _attn_block(lp):
    return ATT_BLOCK if lp % ATT_BLOCK == 0 else TOK


def _wide_block(lp):
    return 2 * ATT_BLOCK if lp % (2 * ATT_BLOCK) == 0 else _attn_block(lp)


def _proj_in(x, head, norm_g, w_ext, packed):
    bsz, seq, _ = x.shape
    lp = X0 + seq
    tp = bsz * lp
    tm = _attn_block(lp)
    nb = lp // tm
    last = pl.cdiv(seq, tm) - 1

    def body(xa_ref, xb_ref, hd_ref, g_ref, w_ref, p_ref, h_ref, u_ref, o_ref, pall_ref, send_sems, recv_sems, local_sem):
        first = jnp.logical_and(pl.program_id(0) == 0, pl.program_id(1) == 0)

        @pl.when(first)
        def _():
            _exchange(p_ref, pall_ref, send_sems, recv_sems, local_sem, True, same=True)

        front = jnp.where(pl.program_id(1) == 0, hd_ref[...], xa_ref[0, tm - X0:, :])
        h = jnp.concatenate([front, xb_ref[0, :tm - X0, :]], axis=0)
        h_ref[...] = h
        r = lax.rsqrt(jnp.mean(h * h, axis=-1, keepdims=True) + EPS)
        u = _bf(h * r * g_ref[...])
        u_ref[...] = u
        o_ref[...] = _bf(_dot(u, w_ref[...]))

        @pl.when(jnp.logical_and(pl.program_id(0) == bsz - 1, pl.program_id(1) == nb - 1))
        def _():
            _exchange(p_ref, pall_ref, send_sems, recv_sems, local_sem, False, same=True)

    anyspec = pl.BlockSpec(memory_space=pl.ANY)
    tok = lambda width: pl.BlockSpec((tm, width), lambda b, i: (b * nb + i, 0))
    return pl.pallas_call(
        body, name="proj_in", grid=(bsz, nb),
        in_specs=[pl.BlockSpec((1, tm, D_MODEL), lambda b, i: (b, jnp.maximum(i - 1, 0), 0)),
                  pl.BlockSpec((1, tm, D_MODEL), lambda b, i: (b, jnp.minimum(i, last), 0)),
                  pl.BlockSpec((X0, D_MODEL), lambda b, i: (0, 0)),
                  pl.BlockSpec((1, D_MODEL), lambda b, i: (0, 0)),
                  pl.BlockSpec((D_MODEL, N_EXT), lambda b, i: (0, 0), pipeline_mode=pl.Buffered(1)), anyspec],
        out_specs=[tok(D_MODEL), tok(D_MODEL), tok(N_EXT), anyspec],
        out_shape=[jax.ShapeDtypeStruct((tp, D_MODEL), F32), jax.ShapeDtypeStruct((tp, D_MODEL), BF16),
                   jax.ShapeDtypeStruct((tp, N_EXT), BF16),
                   jax.ShapeDtypeStruct((N_DEV,) + packed.shape, packed.dtype)],
        scratch_shapes=EXCHANGE_SEMS,
        compiler_params=_cp(("arbitrary", "arbitrary"), 56),
    )(x, x, head, norm_g, w_ext, packed)


def _gla_group(n_chunks):
    return 11 if n_chunks % 11 == 0 else 3


def _tri_dot(tri, x):
    hi = _bf(x)
    rest = x - hi.astype(F32)
    mid = _bf(rest)
    return _dot(tri, hi) + _dot(tri, mid) + _dot(tri, _bf(rest - mid.astype(F32)))


def _gla_gates(q_ref, k_ref, lr_ref, gw_ref, gb_ref, rows, not_first):
    z = _dot(lr_ref[rows, :], gw_ref[...]) + gb_ref[...]
    logsig = jnp.minimum(z, 0.0) - jnp.log(1.0 + jnp.exp(-jnp.abs(z)))
    row = lax.broadcasted_iota(jnp.int32, (GLA_CHUNK, GLA_KW), 0)
    live = jnp.logical_or(not_first, row >= FRONT)
    g = jnp.where(live, logsig * (1.0 / GLA_GATE_NORMALIZER), 0.0)
    ri = lax.broadcasted_iota(jnp.int32, (GLA_CHUNK, GLA_CHUNK), 0)
    ci = lax.broadcasted_iota(jnp.int32, (GLA_CHUNK, GLA_CHUNK), 1)
    tril = ci <= ri
    b = _tri_dot(_bf(tril.astype(F32)), g)
    bl = jnp.sum(jnp.where(row == GLA_CHUNK - 1, b, 0.0), axis=0, keepdims=True)
    eb, enb, elb, ebl = jnp.exp(b), jnp.exp(-b), jnp.exp(bl - b), jnp.exp(bl)
    q = q_ref[rows, :].astype(F32) * (GLA_DK ** -0.5)
    k = k_ref[rows, :].astype(F32)
    qe, ke, kl = q * eb, k * enb, k * elb
    return dict(z=z, live=live, tril=tril, row=row, eb=eb, enb=enb, elb=elb, ebl=ebl, qe=qe, ke=ke, kl=kl,
                qe_b=_bf(qe), ke_b=_bf(ke), kl_b=_bf(kl))


def _gla_in_specs(n_groups, gla_rows, rev):
    def rb(b, n):
        return b * n_groups + ((n_groups - 1 - n) if rev else n)

    return rb, [pl.BlockSpec((gla_rows, GLA_KW), lambda b, n: (rb(b, n), C_Q // GLA_KW)),
                pl.BlockSpec((gla_rows, GLA_KW), lambda b, n: (rb(b, n), C_K // GLA_KW)),
                pl.BlockSpec((gla_rows, GLA_VW), lambda b, n: (rb(b, n), C_V // GLA_VW)),
                pl.BlockSpec((gla_rows, GLA_VW), lambda b, n: (rb(b, n), C_Z // GLA_VW)),
                pl.BlockSpec((gla_rows, LANE), lambda b, n: (rb(b, n), C_LR // LANE)),
                pl.BlockSpec((LANE, GLA_KW), lambda b, n: (0, 0)),
                pl.BlockSpec((1, GLA_KW), lambda b, n: (0, 0)),
                pl.BlockSpec((1, GLA_DV), lambda b, n: (0, 0))]


def _gla_fwd(proj, gw_pad, gate_b, gla_norm_g, bsz, lp):
    n_chunks = lp // GLA_CHUNK
    gla_group = _gla_group(n_chunks)
    gla_rows = gla_group * GLA_CHUNK
    n_groups = n_chunks // gla_group
    tp = bsz * lp

    def body(q_ref, k_ref, v_ref, z_ref, lr_ref, gw_ref, gb_ref, gn_ref, oraw_ref, ya_ref, sall_ref, st_scr):
        grp = pl.program_id(1)

        @pl.when(grp == 0)
        def _():
            st_scr[...] = jnp.zeros_like(st_scr)

        chunks = [slice(j * GLA_CHUNK, (j + 1) * GLA_CHUNK) for j in range(gla_group)]
        cs = [_gla_gates(q_ref, k_ref, lr_ref, gw_ref, gb_ref, rows, True if j else grp > 0)
              for j, rows in enumerate(chunks)]
        gn = gn_ref[...]
        sts = [st_scr[h] for h in range(GLA_HEADS)]
        heads = [(slice(h * GLA_DK, (h + 1) * GLA_DK), slice(h * GLA_DV, (h + 1) * GLA_DV)) for h in range(GLA_HEADS)]
        a_all = [[_bf(jnp.where(c["tril"], _dot_nt(c["qe_b"][:, ks], c["ke_b"][:, ks]), 0.0)) for ks, _ in heads]
                 for c in cs]
        u_all = [[_dot_tn(v_ref[rows, vs], c["kl_b"][:, ks]) for ks, vs in heads] for rows, c in zip(chunks, cs)]
        for j, (rows, c) in enumerate(zip(chunks, cs)):
            for h, (ks, vs) in enumerate(heads):
                st = sts[h]
                sall_ref[0, j, h] = st
                o = _dot(a_all[j][h], v_ref[rows, vs]) + _dot_nt(c["qe_b"][:, ks], _bf(st))
                sts[h] = st * c["ebl"][:, ks] + u_all[j][h]
                oraw_ref[rows, vs] = o
                r = lax.rsqrt(jnp.mean(o * o, axis=-1, keepdims=True) + EPS)
                zg = z_ref[rows, vs].astype(F32)
                ya_ref[rows, vs] = _bf((o * r * gn) * (zg * _sigmoid(zg)))
        for h in range(GLA_HEADS):
            st_scr[h] = sts[h]

    rb, in_specs = _gla_in_specs(n_groups, gla_rows, False)
    return pl.pallas_call(
        body, name="gla_fwd", grid=(bsz, n_groups), in_specs=in_specs,
        out_specs=[pl.BlockSpec((gla_rows, GLA_VW), lambda b, n: (rb(b, n), 0)),
                   pl.BlockSpec((gla_rows, GLA_VW), lambda b, n: (rb(b, n), 0)),
                   pl.BlockSpec((1, gla_group, GLA_HEADS, GLA_DV, GLA_DK), lambda b, n: (b, n, 0, 0, 0))],
        out_shape=[jax.ShapeDtypeStruct((tp, GLA_VW), F32), jax.ShapeDtypeStruct((tp, GLA_VW), BF16),
                   jax.ShapeDtypeStruct((bsz, n_chunks, GLA_HEADS, GLA_DV, GLA_DK), F32)],
        scratch_shapes=[pltpu.VMEM((GLA_HEADS, GLA_DV, GLA_DK), F32)],
        compiler_params=_cp(("parallel", "arbitrary"), 56),
    )(proj, proj, proj, proj, proj, gw_pad, gate_b, gla_norm_g)


def _gla_bwd(proj, gw_pad, gate_b, gla_norm_g, o_raw, s_all, d_ya, dproj, bsz, lp):
    n_chunks = lp // GLA_CHUNK
    gla_group = _gla_group(n_chunks)
    gla_rows = gla_group * GLA_CHUNK
    n_groups = n_chunks // gla_group
    tp = bsz * lp

    def body(q_ref, k_ref, v_ref, z_ref, lr_ref, gw_ref, gb_ref, gn_ref, o_ref, s_ref, dya_ref, _,
             dp_ref, dz_ref, dgn_ref, dst_scr):
        dv_ref, dzg_ref = dp_ref.at[:, C_V:C_V + GLA_VW], dp_ref.at[:, C_Z:C_Z + GLA_VW]

        @pl.when(jnp.logical_and(pl.program_id(0) == 0, pl.program_id(1) == 0))
        def _():
            dgn_ref[...] = jnp.zeros_like(dgn_ref)

        @pl.when(pl.program_id(1) == 0)
        def _():
            dst_scr[...] = jnp.zeros_like(dst_scr)

        grp = n_groups - 1 - pl.program_id(1)
        chunks = [slice(j * GLA_CHUNK, (j + 1) * GLA_CHUNK) for j in range(gla_group)]
        cs = [_gla_gates(q_ref, k_ref, lr_ref, gw_ref, gb_ref, rows, True if j else grp > 0)
              for j, rows in enumerate(chunks)]
        gn = gn_ref[...]
        dgn = jnp.zeros((1, GLA_DV), F32)
        dqe_h, dke_h, dkl_h, dbl_h = ([[None] * GLA_HEADS for _ in chunks] for _ in range(4))
        dsts = [dst_scr[h] for h in range(GLA_HEADS)]
        for j in reversed(range(gla_group)):
            rows, c = chunks[j], cs[j]
            for h in range(GLA_HEADS):
                ks, vs = slice(h * GLA_DK, (h + 1) * GLA_DK), slice(h * GLA_DV, (h + 1) * GLA_DV)
                dst = dsts[h]
                v = v_ref[rows, vs]
                st = s_ref[0, j, h]
                o = o_ref[rows, vs]
                r = lax.rsqrt(jnp.mean(o * o, axis=-1, keepdims=True) + EPS)
                xh = o * r
                zg = z_ref[rows, vs].astype(F32)
                sg = _sigmoid(zg)
                dy = dya_ref[rows, vs].astype(F32)
                dzg_ref[rows, vs] = _bf(dy * (xh * gn) * (sg * (1.0 + zg * (1.0 - sg))))
                t = dy * (zg * sg)
                dgn += jnp.sum(t * xh, axis=0, keepdims=True)
                dxh = t * gn
                do_b = _bf(r * (dxh - xh * jnp.mean(dxh * xh, axis=-1, keepdims=True)))
                qe_b, ke_b, kl_b, dst_b = c["qe_b"][:, ks], c["ke_b"][:, ks], c["kl_b"][:, ks], _bf(dst)
                a = jnp.where(c["tril"], _dot_nt(qe_b, ke_b), 0.0)
                da_b = _bf(jnp.where(c["tril"], _dot_nt(do_b, v), 0.0))
                dqe_h[j][h] = _dot(da_b, ke_b) + _dot(do_b, _bf(st))
                dke_h[j][h] = _dot_tn(da_b, qe_b)
                dkl = _dot(v, dst_b)
                dkl_h[j][h] = dkl
                dv_ref[rows, vs] = _bf(_dot_tn(_bf(a), do_b) + _dot_nt(kl_b, dst_b))
                ddecay = jnp.sum(dst * st, axis=0, keepdims=True)
                dbl_h[j][h] = jnp.sum(dkl * c["kl"][:, ks], axis=0, keepdims=True) + ddecay * c["ebl"][:, ks]
                dsts[h] = dst * c["ebl"][:, ks] + _dot_tn(do_b, qe_b)
        for h in range(GLA_HEADS):
            dst_scr[h] = dsts[h]
        dgn_ref[...] += dgn
        ri = lax.broadcasted_iota(jnp.int32, (GLA_CHUNK, GLA_CHUNK), 0)
        ci = lax.broadcasted_iota(jnp.int32, (GLA_CHUNK, GLA_CHUNK), 1)
        triu = _bf((ci >= ri).astype(F32))
        for j, (rows, c) in enumerate(zip(chunks, cs)):
            dqe, dke, dkl, dbl = (jnp.concatenate(p[j], axis=1) for p in (dqe_h, dke_h, dkl_h, dbl_h))
            db = dqe * c["qe"] - dke * c["ke"] - dkl * c["kl"] + jnp.where(c["row"] == GLA_CHUNK - 1, dbl, 0.0)
            dg = _tri_dot(triu, db)
            dg = jnp.where(c["live"], dg, 0.0)
            dz_ref[rows, :] = dg * (1.0 / GLA_GATE_NORMALIZER) * _sigmoid(-c["z"])
            dp_ref[rows, C_Q:C_Q + GLA_KW] = _bf(dqe * c["eb"] * (GLA_DK ** -0.5))
            dp_ref[rows, C_K:C_K + GLA_KW] = _bf(dke * c["enb"] + dkl * c["elb"])

    rb, in_specs = _gla_in_specs(n_groups, gla_rows, True)
    wide = pl.BlockSpec((gla_rows, GLA_VW), lambda b, n: (rb(b, n), 0))
    group = C_MZ
    return pl.pallas_call(
        body, name="gla_bwd", grid=(bsz, n_groups),
        in_specs=in_specs + [wide, pl.BlockSpec((1, gla_group, GLA_HEADS, GLA_DV, GLA_DK),
                                                lambda b, n: (b, n_groups - 1 - n, 0, 0, 0)), wide,
                             pl.BlockSpec(memory_space=pl.ANY)],
        out_specs=[pl.BlockSpec((gla_rows, group), lambda b, n: (rb(b, n), 0)),
                   pl.BlockSpec((gla_rows, GLA_KW), lambda b, n: (rb(b, n), 0)),
                   pl.BlockSpec((1, GLA_DV), lambda b, n: (0, 0))],
        out_shape=[jax.ShapeDtypeStruct((tp, N_EXT), BF16), jax.ShapeDtypeStruct((tp, GLA_KW), F32),
                   jax.ShapeDtypeStruct((1, GLA_DV), F32)],
        input_output_aliases={11: 0},
        scratch_shapes=[pltpu.VMEM((GLA_HEADS, GLA_DV, GLA_DK), F32)],
        compiler_params=_cp(("arbitrary", "arbitrary"), 56),
    )(proj, proj, proj, proj, proj, gw_pad, gate_b, gla_norm_g, o_raw, s_all, d_ya, dproj)


def _gate_bwd(dz, proj, gw_pad):
    tp = dz.shape[0]
    tm = _big_tok(tp)

    def body(dz_ref, lr_ref, gw_ref, dlr_ref, dgw_ref, dgb_ref):
        @pl.when(pl.program_id(0) == 0)
        def _():
            dgw_ref[...] = jnp.zeros_like(dgw_ref)
            dgb_ref[...] = jnp.zeros_like(dgb_ref)

        dz = dz_ref[...]
        dz_b = _bf(dz)
        dlr_ref[...] = _bf(_dot_nt(dz_b, gw_ref[...]))
        dgw_ref[...] += _dot_tn(lr_ref[...], dz_b)
        dgb_ref[...] += jnp.sum(dz, axis=0, keepdims=True)

    return pl.pallas_call(
        body, name="gate_bwd", grid=(tp // tm,),
        in_specs=[pl.BlockSpec((tm, GLA_KW), lambda i: (i, 0)),
                  pl.BlockSpec((tm, LANE), lambda i: (i, C_LR // LANE)),
                  pl.BlockSpec((LANE, GLA_KW), lambda i: (0, 0))],
        out_specs=[pl.BlockSpec((tm, LANE), lambda i: (i, 0)),
                   pl.BlockSpec((LANE, GLA_KW), lambda i: (0, 0)),
                   pl.BlockSpec((1, GLA_KW), lambda i: (0, 0))],
        out_shape=[jax.ShapeDtypeStruct((tp, LANE), BF16), jax.ShapeDtypeStruct((LANE, GLA_KW), F32),
                   jax.ShapeDtypeStruct((1, GLA_KW), F32)],
        compiler_params=_cp(("arbitrary",)),
    )(dz, proj, gw_pad)


def _rms_fwd(x):
    r = lax.rsqrt(jnp.mean(x * x, axis=-1, keepdims=True) + EPS)
    return x * r, r


def _rms_bwd(dy, xh, r, g):
    dxh = dy * g
    dx = r * (dxh - xh * jnp.mean(dxh * xh, axis=-1, keepdims=True))
    return dx, jnp.sum(dy * xh, axis=0, keepdims=True)


def _q_up(proj, q_norm_g, wn, wr, wt, cos_t, sin_t, bsz, lp):
    tp = bsz * lp
    tok = _wide_block(lp)
    nb = lp // tok

    def body(cq_ref, g_ref, wn_ref, wr_ref, wt_ref, cos_ref, sin_ref, q_ref):
        xh, _ = _rms_fwd(cq_ref[...].astype(F32))
        cqn = _bf(xh * g_ref[...])
        nope = _dot(cqn, wn_ref[...])
        rope = _dot(cqn, wr_ref[...])
        rot = _dot(cqn, wt_ref[...])
        cos, sin = cos_ref[...], sin_ref[...]
        one = (lax.broadcasted_iota(jnp.int32, (tok, LANE), 1) == BIAS_LANE).astype(F32)
        for h in range(MLA_HEADS):
            sl = slice(h * LANE, (h + 1) * LANE)
            q_ref[:, h * QKW:h * QKW + LANE] = _bf(nope[:, sl])
            q_ref[:, h * QKW + LANE:(h + 1) * QKW] = _bf(rope[:, sl] * cos + rot[:, sl] * sin + one)

    wspec = pl.BlockSpec((MLA_QR, MLA_HEADS * LANE), lambda b, i: (0, 0))
    tspec = pl.BlockSpec((tok, LANE), lambda b, i: (i, 0))
    return pl.pallas_call(
        body, name="mla_q_up", grid=(bsz, nb),
        in_specs=[pl.BlockSpec((tok, MLA_QR), lambda b, i: (b * nb + i, C_CQ // MLA_QR)),
                  pl.BlockSpec((1, MLA_QR), lambda b, i: (0, 0)), wspec, wspec, wspec, tspec, tspec],
        out_specs=pl.BlockSpec((tok, MLA_HEADS * QKW), lambda b, i: (b * nb + i, 0)),
        out_shape=jax.ShapeDtypeStruct((tp, MLA_HEADS * QKW), BF16),
        compiler_params=_cp(("parallel", "parallel")),
    )(proj, q_norm_g, wn, wr, wt, cos_t, sin_t)


def _kv_up(proj, kv_norm_g, wk, wv, cos_t, sin_t, bsz, lp):
    tp = bsz * lp
    tok = _wide_block(lp)
    nb = lp // tok

    def body(ckv_ref, kr_ref, krot_ref, g_ref, wk_ref, wv_ref, cos_ref, sin_ref, k_ref, v_ref):
        xh, _ = _rms_fwd(ckv_ref[...].astype(F32))
        cn = _bf(xh * g_ref[...])
        kn = _dot(cn, wk_ref[...])
        v_ref[...] = _bf(_dot(cn, wv_ref[...]))
        pos = pl.program_id(1) * tok + lax.broadcasted_iota(jnp.int32, (tok, LANE), 0)
        lane = lax.broadcasted_iota(jnp.int32, (tok, LANE), 1)
        bias = jnp.where(jnp.logical_and(lane == BIAS_LANE, pos < FRONT), KEY_BIAS, 0.0)
        kr = _bf(kr_ref[...].astype(F32) * cos_ref[...] + krot_ref[...].astype(F32) * sin_ref[...] + bias)
        for h in range(MLA_HEADS):
            k_ref[:, h * QKW:h * QKW + LANE] = _bf(kn[:, h * LANE:(h + 1) * LANE])
            k_ref[:, h * QKW + LANE:(h + 1) * QKW] = kr

    wspec = pl.BlockSpec((MLA_KVR, MLA_HEADS * LANE), lambda b, i: (0, 0))
    tspec = pl.BlockSpec((tok, LANE), lambda b, i: (i, 0))
    return pl.pallas_call(
        body, name="mla_kv_up", grid=(bsz, nb),
        in_specs=[pl.BlockSpec((tok, LANE), lambda b, i: (b * nb + i, C_CKV // LANE)),
                  pl.BlockSpec((tok, LANE), lambda b, i: (b * nb + i, C_KR // LANE)),
                  pl.BlockSpec((tok, LANE), lambda b, i: (b * nb + i, C_KROT // LANE)),
                  pl.BlockSpec((1, MLA_KVR), lambda b, i: (0, 0)), wspec, wspec, tspec, tspec],
        out_specs=[pl.BlockSpec((tok, MLA_HEADS * QKW), lambda b, i: (b * nb + i, 0)),
                   pl.BlockSpec((tok, MLA_HEADS * LANE), lambda b, i: (b * nb + i, 0))],
        out_shape=[jax.ShapeDtypeStruct((tp, MLA_HEADS * QKW), BF16),
                   jax.ShapeDtypeStruct((tp, MLA_HEADS * LANE), BF16)],
        compiler_params=_cp(("parallel", "parallel")),
    )(proj, proj, proj, kv_norm_g, wk, wv, cos_t, sin_t)


ATT_SCALE = MLA_QK ** -0.5


KEY_BIAS = -1e30
BIAS_LANE = MLA_ROPE
NEG = 2 * KEY_BIAS
LOG2E = 1.4426950408889634
EXP2_SCALE = ATT_SCALE * LOG2E


def _causal_fill(s, r0, fill):
    tq, kmax = s.shape
    a = r0 // LANE * LANE
    mask = (a + lax.broadcasted_iota(jnp.int32, (tq, kmax - a), 1)
            <= r0 + lax.broadcasted_iota(jnp.int32, (tq, kmax - a), 0))
    right = jnp.where(mask, s[:, a:], fill)
    return jnp.concatenate([s[:, :a], right], axis=1) if a else right


def _attn_fwd(qf, kf, vf, proj, bsz, lp):
    tp = bsz * lp
    tq = _attn_block(lp)
    nh = 2

    def body(q_ref, k_ref, v_ref, mz_ref, ob_ref, yb_ref, lse_ref):
        starts = list(range(0, lp, tq))
        for pair in (starts[i:i + 2] for i in range(0, len(starts), 2)):
            work = [(r0, h) for r0 in pair for h in range(nh)]
            ss = [_causal_fill(_dot_nt(q_ref[r0:r0 + tq, h * QKW:(h + 1) * QKW],
                                       k_ref[0:r0 + tq, h * QKW:(h + 1) * QKW]), r0, NEG) for r0, h in work]
            ms = [jnp.max(s, axis=-1, keepdims=True) for s in ss]
            ps = [jnp.exp2((s - m) * EXP2_SCALE) for s, m in zip(ss, ms)]
            ls = [jnp.sum(p, axis=-1, keepdims=True) for p in ps]
            for (r0, h), p, m, l in zip(work, ps, ms, ls):
                rows, cols = slice(r0, r0 + tq), slice(h * MLA_DV, (h + 1) * MLA_DV)
                o = _dot(_bf(p), v_ref[0:r0 + tq, cols]) / l
                ob_ref[rows, cols] = _bf(o)
                mz = mz_ref[rows, cols].astype(F32)
                yb_ref[rows, cols] = _bf(o * (mz * _sigmoid(mz)))
                lse_ref[0, h, rows, :] = jnp.broadcast_to(m * EXP2_SCALE + jnp.log2(l), (tq, LANE))

    head = lambda off: pl.BlockSpec((lp, nh * MLA_DV), lambda b, h: (b, off + h))
    wide = pl.BlockSpec((lp, nh * QKW), lambda b, h: (b, h))
    return pl.pallas_call(
        body, name="mla_attn_fwd", grid=(bsz, MLA_HEADS // nh),
        in_specs=[wide, wide, head(0), head(C_MZ // (nh * MLA_DV))],
        out_specs=[head(0), head(0), pl.BlockSpec((1, nh, lp, LANE), lambda b, h: (b, h, 0, 0))],
        out_shape=[jax.ShapeDtypeStruct((tp, MLA_HEADS * MLA_DV), BF16),
                   jax.ShapeDtypeStruct((tp, MLA_HEADS * MLA_DV), BF16),
                   jax.ShapeDtypeStruct((bsz, MLA_HEADS, lp, LANE), F32)],
        compiler_params=_cp(("parallel", "parallel"), 56),
    )(qf, kf, vf, proj)


def _attn_bwd_blocks(lp):
    return [(0, X0)] + [(r0, min(MXU_DEPTH, lp - r0)) for r0 in range(X0, lp, MXU_DEPTH)]


def _attn_bwd(qf, kf, vf, d_o, lse, delta, bsz, lp):
    tp = bsz * lp

    def body(q_ref, k_ref, v_ref, do_ref, lse_ref, dl_ref, dq_ref, dk_ref, dv_ref, dk_acc, dv_acc):
        dk_acc[...] = jnp.zeros_like(dk_acc)
        dv_acc[...] = jnp.zeros_like(dv_acc)
        for r0, tq in _attn_bwd_blocks(lp):
            rows, kmax = slice(r0, r0 + tq), r0 + tq
            q, do = q_ref[rows, :], do_ref[rows, :]
            k, v = k_ref[0:kmax, :], v_ref[0:kmax, :]
            p = jnp.exp2(_dot_nt(q, k) * EXP2_SCALE - lse_ref[0, 0, rows, :][:, :1])
            p = _causal_fill(p, r0, 0.0)
            ds = _bf(p * (_dot_nt(do, v) - dl_ref[0, rows, :][:, :1]))
            dq_ref[rows, :] = _bf(_dot(ds, k) * ATT_SCALE)
            dk_acc[0:kmax, :] += _dot_tn(ds, q)
            dv_acc[0:kmax, :] += _dot_tn(_bf(p), do)
        dk_ref[...] = _bf(dk_acc[...] * ATT_SCALE)
        dv_ref[...] = _bf(dv_acc[...])

    wide = pl.BlockSpec((lp, QKW), lambda b, h: (b, h))
    narrow = pl.BlockSpec((lp, MLA_DV), lambda b, h: (b, h))
    stat = pl.BlockSpec((1, 1, lp, LANE), lambda b, h: (b, h, 0, 0))
    return pl.pallas_call(
        body, name="mla_attn_bwd", grid=(bsz, MLA_HEADS),
        in_specs=[wide, wide, narrow, narrow, stat, pl.BlockSpec((1, lp, LANE), lambda b, h: (h, b, 0))],
        out_specs=[wide, wide, narrow],
        out_shape=[jax.ShapeDtypeStruct((tp, MLA_HEADS * QKW), BF16), jax.ShapeDtypeStruct((tp, MLA_HEADS * QKW), BF16),
                   jax.ShapeDtypeStruct((tp, MLA_HEADS * MLA_DV), BF16)],
        scratch_shapes=[pltpu.VMEM((lp, QKW), F32), pltpu.VMEM((lp, MLA_DV), F32)],
        compiler_params=_cp(("parallel", "parallel"), 56),
    )(qf, kf, vf, d_o, lse, delta)


def _q_up_bwd(dqf, proj, q_norm_g, wn, wr, wt, cos_t, sin_t, dproj, bsz, lp):
    tp = bsz * lp
    tok = _wide_block(lp)
    nb = lp // tok
    hw = MLA_HEADS * LANE

    def body(dq_ref, cq_ref, g_ref, wn_ref, wr_ref, wt_ref, cos_ref, sin_ref, _,
             dcq_ref, dwn_ref, dwr_ref, dwt_ref, dg_ref):
        @pl.when(jnp.logical_and(pl.program_id(0) == 0, pl.program_id(1) == 0))
        def _():
            for r in (dwn_ref, dwr_ref, dwt_ref, dg_ref):
                r[...] = jnp.zeros_like(r)

        g = g_ref[...]
        xh, r = _rms_fwd(cq_ref[...].astype(F32))
        cqn = _bf(xh * g)
        dn = jnp.concatenate([dq_ref[:, h * QKW:h * QKW + LANE] for h in range(MLA_HEADS)], axis=1)
        dr = jnp.concatenate([dq_ref[:, h * QKW + LANE:(h + 1) * QKW] for h in range(MLA_HEADS)], axis=1).astype(F32)
        dr_c = _bf(dr * jnp.tile(cos_ref[...], (1, MLA_HEADS)))
        dr_s = _bf(dr * jnp.tile(sin_ref[...], (1, MLA_HEADS)))
        dcqn = _dot_nt(dn, wn_ref[...]) + _dot_nt(dr_c, wr_ref[...]) + _dot_nt(dr_s, wt_ref[...])
        dwn_ref[...] += _dot_tn(cqn, dn)
        dwr_ref[...] += _dot_tn(cqn, dr_c)
        dwt_ref[...] += _dot_tn(cqn, dr_s)
        dx, dg = _rms_bwd(dcqn, xh, r, g)
        dcq_ref[...] = _bf(dx)
        dg_ref[...] += dg

    aspec = pl.BlockSpec((MLA_QR, hw), lambda b, i: (0, 0))
    tspec = pl.BlockSpec((tok, LANE), lambda b, i: (i, 0))
    return pl.pallas_call(
        body, name="mla_q_up_bwd", grid=(bsz, nb),
        in_specs=[pl.BlockSpec((tok, MLA_HEADS * QKW), lambda b, i: (b * nb + i, 0)),
                  pl.BlockSpec((tok, MLA_QR), lambda b, i: (b * nb + i, C_CQ // MLA_QR)),
                  pl.BlockSpec((1, MLA_QR), lambda b, i: (0, 0)), aspec, aspec, aspec, tspec, tspec,
                  pl.BlockSpec(memory_space=pl.ANY)],
        out_specs=[pl.BlockSpec((tok, MLA_QR), lambda b, i: (b * nb + i, C_CQ // MLA_QR)), aspec, aspec, aspec,
                   pl.BlockSpec((1, MLA_QR), lambda b, i: (0, 0))],
        out_shape=[jax.ShapeDtypeStruct((tp, N_EXT), BF16)] + [jax.ShapeDtypeStruct((MLA_QR, hw), F32)] * 3
        + [jax.ShapeDtypeStruct((1, MLA_QR), F32)],
        input_output_aliases={8: 0},
        compiler_params=_cp(("arbitrary", "arbitrary")),
    )(dqf, proj, q_norm_g, wn, wr, wt, cos_t, sin_t, dproj)


def _kv_up_bwd(dkf, dvf, proj, kv_norm_g, wk, wv, cos_t, sin_t, d_lr, dproj, bsz, lp):
    tp = bsz * lp
    tok = _wide_block(lp)
    nb = lp // tok
    hw = MLA_HEADS * LANE

    def body(dk_ref, dv_ref, ckv_ref, g_ref, wk_ref, wv_ref, cos_ref, sin_ref, dlr_ref, _,
             dp_ref, dwk_ref, dwv_ref, dg_ref):
        dckv_ref, dkr_ref, dkrot_ref = (dp_ref.at[:, j * LANE:(j + 1) * LANE] for j in range(3))
        dp_ref[:, 3 * LANE:] = dlr_ref[...]
        @pl.when(jnp.logical_and(pl.program_id(0) == 0, pl.program_id(1) == 0))
        def _():
            for r in (dwk_ref, dwv_ref, dg_ref):
                r[...] = jnp.zeros_like(r)

        g = g_ref[...]
        xh, r = _rms_fwd(ckv_ref[...].astype(F32))
        cn = _bf(xh * g)
        dv = dv_ref[...]
        dn = jnp.concatenate([dk_ref[:, h * QKW:h * QKW + LANE] for h in range(MLA_HEADS)], axis=1)
        dcn = _dot_nt(dv, wv_ref[...]) + _dot_nt(dn, wk_ref[...])
        dwv_ref[...] += _dot_tn(cn, dv)
        dwk_ref[...] += _dot_tn(cn, dn)
        drope = jnp.zeros((tok, LANE), F32)
        for h in range(MLA_HEADS):
            drope += dk_ref[:, h * QKW + LANE:(h + 1) * QKW].astype(F32)
        dkr_ref[...] = _bf(drope * cos_ref[...])
        dkrot_ref[...] = _bf(drope * sin_ref[...])
        dx, dg = _rms_bwd(dcn, xh, r, g)
        dckv_ref[...] = _bf(dx)
        dg_ref[...] += dg

    aspec = pl.BlockSpec((MLA_KVR, hw), lambda b, i: (0, 0))
    tspec = pl.BlockSpec((tok, LANE), lambda b, i: (i, 0))
    ospec = pl.BlockSpec((tok, LANE), lambda b, i: (b * nb + i, 0))
    return pl.pallas_call(
        body, name="mla_kv_up_bwd", grid=(bsz, nb),
        in_specs=[pl.BlockSpec((tok, MLA_HEADS * QKW), lambda b, i: (b * nb + i, 0)),
                  pl.BlockSpec((tok, hw), lambda b, i: (b * nb + i, 0)),
                  pl.BlockSpec((tok, LANE), lambda b, i: (b * nb + i, C_CKV // LANE)),
                  pl.BlockSpec((1, MLA_KVR), lambda b, i: (0, 0)), aspec, aspec, tspec, tspec, ospec,
                  pl.BlockSpec(memory_space=pl.ANY)],
        out_specs=[pl.BlockSpec((tok, 4 * LANE), lambda b, i: (b * nb + i, C_CKV // (4 * LANE))), aspec, aspec,
                   pl.BlockSpec((1, MLA_KVR), lambda b, i: (0, 0))],
        out_shape=[jax.ShapeDtypeStruct((tp, N_EXT), BF16)] + [jax.ShapeDtypeStruct((MLA_KVR, hw), F32)] * 2
        + [jax.ShapeDtypeStruct((1, MLA_KVR), F32)],
        input_output_aliases={9: 0},
        compiler_params=_cp(("arbitrary", "arbitrary")),
    )(dkf, dvf, proj, kv_norm_g, wk, wv, cos_t, sin_t, d_lr, dproj)


def _mid_fwd(ya_in, yb_in, proj, hp, target, w_gp, w_mp, w_o, final_g, bsz, lp):
    tp = bsz * lp
    tm = _wide_block(lp)
    nb = lp // tm
    last = pl.cdiv(lp - X0, tm) - 1

    def body(ya_ref, yb_ref, gg_ref, gm_ref, h_ref, ta_ref, tb_ref, wgp_ref, wmp_ref, wo_ref, fg_ref,
             ya_out, yb_out, dh_ref, loss_ref, dfg_ref):
        @pl.when(jnp.logical_and(pl.program_id(0) == 0, pl.program_id(1) == 0))
        def _():
            loss_ref[...] = jnp.zeros_like(loss_ref)
            dfg_ref[...] = jnp.zeros_like(dfg_ref)

        y_a = _dot(ya_ref[...], wgp_ref[...])
        y_b = _dot(yb_ref[...], wmp_ref[...])
        ya_out[...] = _bf(y_a)
        yb_out[...] = _bf(y_b)
        merged = _sigmoid(gg_ref[...].astype(F32)) * y_a + _sigmoid(gm_ref[...].astype(F32)) * y_b
        h2 = h_ref[...] + _dot(_bf(merged), wo_ref[...])
        fg = fg_ref[...]
        xh, r = _rms_fwd(h2)
        pos = pl.program_id(1) * tm + lax.broadcasted_iota(jnp.int32, (tm, 1), 0)
        t = jnp.concatenate([ta_ref[0, tm - X0:, :], tb_ref[0, :tm - X0, :]], axis=0)
        err = jnp.where(pos >= X0, xh * fg - t, 0.0)
        loss_ref[...] += 0.5 * jnp.sum(jnp.mean(err * err, axis=-1, keepdims=True), axis=0, keepdims=True)
        dy = err * (1.0 / D_MODEL)
        dx, dfg = _rms_bwd(dy, xh, r, fg)
        dh_ref[...] = dx
        dfg_ref[...] += dfg

    tok = lambda c: pl.BlockSpec((tm, D_MODEL), lambda b, i: (b * nb + i, c))
    wspec = pl.BlockSpec((D_MODEL, D_MODEL), lambda b, i: (0, 0), pipeline_mode=pl.Buffered(1))
    return pl.pallas_call(
        body, name="mid_fwd", grid=(bsz, nb),
        in_specs=[tok(0), tok(0), tok(C_GG // D_MODEL), tok(C_GM // D_MODEL), tok(0),
                  pl.BlockSpec((1, tm, D_MODEL), lambda b, i: (b, jnp.maximum(i - 1, 0), 0)),
                  pl.BlockSpec((1, tm, D_MODEL), lambda b, i: (b, jnp.minimum(i, last), 0)),
                  wspec, wspec, wspec, pl.BlockSpec((1, D_MODEL), lambda b, i: (0, 0))],
        out_specs=[tok(0), tok(0), tok(0), pl.BlockSpec((1, LANE), lambda b, i: (0, 0)),
                   pl.BlockSpec((1, D_MODEL), lambda b, i: (0, 0))],
        out_shape=[jax.ShapeDtypeStruct((tp, D_MODEL), BF16), jax.ShapeDtypeStruct((tp, D_MODEL), BF16),
                   jax.ShapeDtypeStruct((tp, D_MODEL), F32), jax.ShapeDtypeStruct((1, LANE), F32),
                   jax.ShapeDtypeStruct((1, D_MODEL), F32)],
        compiler_params=_cp(("arbitrary", "arbitrary"), 56),
    )(ya_in, yb_in, proj, proj, hp, target, target, w_gp, w_mp, w_o, final_g)


def _mid_bwd(dh2, y_a, y_b, proj, ya_in, yb_in, o_b, w_o, w_gp, w_mp, bsz, lp):
    tp = bsz * lp
    tm = MXU_DEPTH if tp % MXU_DEPTH == 0 else _attn_block(lp)
    nsteps = tp // tm
    group = 3 * D_MODEL

    def body(dh_ref, ya_ref, yb_ref, mz_ref, gg_ref, gm_ref, yai_ref, ybi_ref, ob_ref, wo_ref, wgp_ref, wmp_ref,
             dyai_ref, do_ref, dp_ref, dl_ref, dwo_ref, dwgp_ref, dwmp_ref, a_o, a_gp, a_mp):
        @pl.when(pl.program_id(0) == 0)
        def _():
            for r in (a_o, a_gp, a_mp):
                r[...] = jnp.zeros_like(r)

        dh = _bf(dh_ref[...])
        dm = _dot_nt(dh, wo_ref[...])
        y_a, y_b = ya_ref[...].astype(F32), yb_ref[...].astype(F32)
        sg, sm = _sigmoid(gg_ref[...].astype(F32)), _sigmoid(gm_ref[...].astype(F32))
        d_ya, d_yb = _bf(sg * dm), _bf(sm * dm)
        dp_ref[:, D_MODEL:2 * D_MODEL] = _bf(dm * y_a * sg * (1.0 - sg))
        dp_ref[:, 2 * D_MODEL:] = _bf(dm * y_b * sm * (1.0 - sm))
        merged = _bf(sg * y_a + sm * y_b)
        dy = _dot_nt(d_yb, wmp_ref[...])
        dyai_ref[...] = _bf(_dot_nt(d_ya, wgp_ref[...]))
        a_o[...] += _dot_tn(merged, dh)
        a_gp[...] += _dot_tn(yai_ref[...], d_ya)
        a_mp[...] += _dot_tn(ybi_ref[...], d_yb)
        mz, o = mz_ref[...].astype(F32), ob_ref[...].astype(F32)
        s = _sigmoid(mz)
        do = _bf(dy * (mz * s))
        do_ref[...] = do
        dp_ref[:, :D_MODEL] = _bf(dy * o * (s * (1.0 + mz * (1.0 - s))))
        prod = do.astype(F32) * o
        for h in range(MLA_HEADS):
            dl = jnp.sum(prod[:, h * MLA_DV:(h + 1) * MLA_DV], axis=-1, keepdims=True)
            dl_ref[h] = jnp.broadcast_to(dl, (tm, LANE))

        @pl.when(pl.program_id(0) == nsteps - 1)
        def _():
            pltpu.sync_copy(a_o, dwo_ref)
            pltpu.sync_copy(a_gp, dwgp_ref)
            pltpu.sync_copy(a_mp, dwmp_ref)

    tok = lambda c: pl.BlockSpec((tm, D_MODEL), lambda i: (i, c))
    wspec = pl.BlockSpec((D_MODEL, D_MODEL), lambda i: (0, 0))
    anyspec = pl.BlockSpec(memory_space=pl.ANY)
    wshape = jax.ShapeDtypeStruct((D_MODEL, D_MODEL), F32)
    return pl.pallas_call(
        body, name="mid_bwd", grid=(nsteps,),
        in_specs=[tok(0), tok(0), tok(0), tok(C_MZ // D_MODEL), tok(C_GG // D_MODEL), tok(C_GM // D_MODEL),
                  tok(0), tok(0), tok(0), wspec, wspec, wspec],
        out_specs=[tok(0), tok(0), pl.BlockSpec((tm, group), lambda i: (i, C_MZ // group)),
                   pl.BlockSpec((MLA_HEADS, tm, LANE), lambda i: (0, i, 0)), anyspec, anyspec, anyspec],
        out_shape=[jax.ShapeDtypeStruct((tp, D_MODEL), BF16)] * 2 + [jax.ShapeDtypeStruct((tp, N_EXT), BF16),
                   jax.ShapeDtypeStruct((MLA_HEADS, tp, LANE), F32)] + [wshape] * 3,
        scratch_shapes=[pltpu.VMEM((D_MODEL, D_MODEL), F32)] * 3,
        compiler_params=_cp(("arbitrary",), 56),
    )(dh2, y_a, y_b, proj, proj, proj, ya_in, yb_in, o_b, w_o, w_gp, w_mp)


MESH_ID = pl.DeviceIdType.MESH
EXCHANGE_SEMS = [pltpu.SemaphoreType.DMA((N_DEV - 1,)), pltpu.SemaphoreType.DMA((N_DEV - 1,)), pltpu.SemaphoreType.DMA]


def _my_place():
    return lax.axis_index("x"), lax.axis_index("y"), lax.axis_index("c")


def _exchange(g_ref, recv_ref, send_sems, recv_sems, local_sem, start, same=False):
    x, y, c = _my_place()
    me = 4 * x + 2 * y + c
    own = pltpu.make_async_copy(g_ref if same else g_ref.at[me], recv_ref.at[me], local_sem)
    sends, lands = [], []
    for d in range(1, N_DEV):
        px = 1 - x if d & 4 else x
        py = 1 - y if d & 2 else y
        pc = 1 - c if d & 1 else c
        peer = 4 * px + 2 * py + pc
        for slot, group in ((me, sends),) if start else ((me, sends), (peer, lands)):
            group.append(pltpu.make_async_remote_copy(
                src_ref=g_ref if same else g_ref.at[peer], dst_ref=recv_ref.at[slot], send_sem=send_sems.at[d - 1],
                recv_sem=recv_sems.at[d - 1], device_id=(px, py, pc), device_id_type=MESH_ID))
    if start:
        own.start()
        for cp in sends:
            cp.start()
    else:
        for cp in lands:
            cp.wait_recv()
        for cp in sends:
            cp.wait_send()
        own.wait()


def _dw_in(u, dproj, slabs):
    tp = u.shape[0]
    tn = 3 * LANE
    nj = N_EXT // tn

    def body(u_ref, d_ref, g_ref, o_ref, recv_ref, send_sems, recv_sems, local_sem):
        j = pl.program_id(0)

        @pl.when(j == 0)
        def _():
            _exchange(g_ref, recv_ref, send_sems, recv_sems, local_sem, True)

        o_ref[...] = _bf(_dot_tn(d_ref[...], u_ref[...]))

        @pl.when(j == nj - 1)
        def _():
            _exchange(g_ref, recv_ref, send_sems, recv_sems, local_sem, False)

    anyspec = pl.BlockSpec(memory_space=pl.ANY)
    return pl.pallas_call(
        body, name="dw_in", grid=(nj,),
        in_specs=[pl.BlockSpec((tp, D_MODEL), lambda j: (0, 0), pipeline_mode=pl.Buffered(1)),
                  pl.BlockSpec((tp, tn), lambda j: (0, j)), anyspec],
        out_specs=[pl.BlockSpec((tn, D_MODEL), lambda j: (j, 0)), anyspec],
        out_shape=[jax.ShapeDtypeStruct((N_EXT, D_MODEL), BF16), jax.ShapeDtypeStruct(slabs.shape, slabs.dtype)],
        scratch_shapes=EXCHANGE_SEMS,
        compiler_params=_cp(("arbitrary",), 56),
    )(u, dproj, slabs)


def _dx_in(dproj, w_ext, hp, dh2, norm_g, slabs):
    tp = hp.shape[0]
    tm = 2 * TOK
    ni = tp // tm

    def body(d_ref, w_ref, h_ref, dh_ref, g_ref, s_ref, o_ref, dg_ref, recv_ref, send_sems, recv_sems, local_sem):
        i = pl.program_id(0)

        @pl.when(i == 0)
        def _():
            _exchange(s_ref, recv_ref, send_sems, recv_sems, local_sem, True)
            dg_ref[...] = jnp.zeros_like(dg_ref)

        du = _dot_nt(d_ref[...], w_ref[...])
        g = g_ref[...]
        xh, r = _rms_fwd(h_ref[...])
        dx, dg = _rms_bwd(du, xh, r, g)
        o_ref[...] = dh_ref[...] + dx
        dg_ref[...] += dg

        @pl.when(i == ni - 1)
        def _():
            _exchange(s_ref, recv_ref, send_sems, recv_sems, local_sem, False)

    tok = pl.BlockSpec((tm, D_MODEL), lambda i: (i, 0))
    anyspec = pl.BlockSpec(memory_space=pl.ANY)
    return pl.pallas_call(
        body, name="dx_in", grid=(ni,),
        in_specs=[pl.BlockSpec((tm, N_EXT), lambda i: (i, 0)),
                  pl.BlockSpec((D_MODEL, N_EXT), lambda i: (0, 0), pipeline_mode=pl.Buffered(1)),
                  tok, tok, pl.BlockSpec((1, D_MODEL), lambda i: (0, 0)), anyspec],
        out_specs=[tok, pl.BlockSpec((1, D_MODEL), lambda i: (0, 0)), anyspec],
        out_shape=[jax.ShapeDtypeStruct((tp, D_MODEL), F32), jax.ShapeDtypeStruct((1, D_MODEL), F32),
                   jax.ShapeDtypeStruct(slabs.shape, slabs.dtype)],
        scratch_shapes=EXCHANGE_SEMS,
        compiler_params=_cp(("arbitrary",), 56),
    )(dproj, w_ext, hp, dh2, norm_g, slabs)


def _meta_grad(dhp3):
    bsz = dhp3.shape[0]

    def body(d_ref, o_ref):
        @pl.when(pl.program_id(0) == 0)
        def _():
            o_ref[...] = jnp.zeros_like(o_ref)

        o_ref[...] += d_ref[0]

    return pl.pallas_call(
        body, name="meta_grad", grid=(bsz,),
        in_specs=[pl.BlockSpec((1, N_META, D_MODEL), lambda b: (b, FRONT // N_META, 0))],
        out_specs=pl.BlockSpec((N_META, D_MODEL), lambda b: (0, 0)),
        out_shape=jax.ShapeDtypeStruct((N_META, D_MODEL), F32),
        compiler_params=_cp(("arbitrary",)),
    )(dhp3)


W_IN_SHARD = N_IN // N_DEV


def _pad_lanes(a, width=LANE):
    return jnp.pad(a, [(0, 0)] * (a.ndim - 1) + [(0, width - a.shape[-1])])


def _rot_cols(w):
    half = w.shape[-1] // 2
    return jnp.concatenate([-w[..., half:], w[..., :half]], axis=-1)


def _unrot_cols(dw):
    half = dw.shape[-1] // 2
    return jnp.concatenate([dw[..., half:], -dw[..., :half]], axis=-1)


def _w_in_cols(shards, lo, hi):
    parts = []
    for k in range(lo // W_IN_SHARD, (hi - 1) // W_IN_SHARD + 1):
        a, b = max(lo, k * W_IN_SHARD), min(hi, (k + 1) * W_IN_SHARD)
        parts.append(shards[k][:, a - k * W_IN_SHARD:b - k * W_IN_SHARD])
    return parts[0] if len(parts) == 1 else jnp.concatenate(parts, axis=1)


def _w_in_ext(shards):
    c = lambda lo, hi: _w_in_cols(shards, lo, hi)
    kr = c(O_KR, O_MZ)
    return jnp.concatenate([
        c(O_V, O_LR), c(O_Z, O_CQ), c(O_Q, O_K), c(O_K, O_V), c(O_MZ, O_GG), c(O_GG, O_GM), c(O_GM, N_IN),
        c(O_CKV, O_KR), _pad_lanes(kr), _pad_lanes(_rot_cols(kr)), _pad_lanes(c(O_LR, O_Z)), c(O_CQ, O_CKV)], axis=1)


def _w_in_slabs(dwt):
    half = MLA_ROPE // 2
    krot = dwt[C_KROT:C_KROT + MLA_ROPE]
    kr = dwt[C_KR:C_KR + MLA_ROPE] + jnp.concatenate([krot[half:], -krot[:half]], axis=0)
    groups = ((O_Q, GLA_KW, C_Q), (O_K, GLA_KW, C_K), (O_V, GLA_VW, C_V), (O_LR, GLA_RANK, C_LR), (O_Z, GLA_VW, C_Z),
              (O_CQ, MLA_QR, C_CQ), (O_CKV, MLA_KVR, C_CKV), (O_KR, MLA_ROPE, None), (O_MZ, D_MODEL, C_MZ),
              (O_GG, D_MODEL, C_GG), (O_GM, D_MODEL, C_GM))
    slabs = []
    for k in range(N_DEV):
        lo, hi = k * W_IN_SHARD, (k + 1) * W_IN_SHARD
        parts = []
        for first, width, row in groups:
            a, b = max(lo, first), min(hi, first + width)
            if a < b:
                parts.append(kr[a - first:b - first] if row is None else dwt[row + a - first:row + b - first])
        slabs.append(jnp.concatenate(parts, axis=0))
    return jnp.stack(slabs)


def _rope_tables(lp):
    inv = 1.0 / (ROPE_BASE ** (jnp.arange(0, MLA_ROPE, 2, dtype=F32) / MLA_ROPE))
    ang = (jnp.arange(lp, dtype=F32) - FRONT)[:, None] * inv[None, :]
    cos, sin = jnp.cos(ang), jnp.sin(ang)
    return _pad_lanes(jnp.concatenate([cos, cos], axis=1)), _pad_lanes(jnp.concatenate([sin, sin], axis=1))


def _local_step(x, loss_target, w):
    bsz, seq, _ = x.shape
    lp = X0 + seq
    tp = bsz * lp
    assert lp % TOK == 0 and (lp // GLA_CHUNK) % _gla_group(lp // GLA_CHUNK) == 0
    head = jnp.concatenate([jnp.zeros((FRONT, D_MODEL), F32), w["meta_tokens"]], axis=0)
    cos_t, sin_t = _rope_tables(lp)

    w_ext = _w_in_ext(w["w_in"])
    hp, u, proj, packed_all = _proj_in(x, head, w["norm_g"], w_ext, w["packed"])
    gathered = _unpack_shards(packed_all)
    for n, _, axis in PACKED:
        w[n] = _join8(gathered[n], axis)
    gw_pad = jnp.pad(w["gla_gate_w"], ((0, LANE - GLA_RANK), (0, 0)))
    uq = w["mla_w_uq"].reshape(MLA_QR, MLA_HEADS, MLA_QK)
    rope_w = uq[:, :, MLA_NOPE:]
    hw = MLA_HEADS * LANE
    wn = uq[:, :, :MLA_NOPE].reshape(MLA_QR, hw)
    wr = _pad_lanes(rope_w).reshape(MLA_QR, hw)
    wt = _pad_lanes(_rot_cols(rope_w)).reshape(MLA_QR, hw)
    ukv = w["mla_w_ukv"].reshape(MLA_KVR, MLA_HEADS, MLA_NOPE + MLA_DV)
    wk = ukv[:, :, :MLA_NOPE].reshape(MLA_KVR, hw)
    wv = ukv[:, :, MLA_NOPE:].reshape(MLA_KVR, hw)

    o_raw, ya_in, s_all = _gla_fwd(proj, gw_pad, w["gla_gate_b"], w["gla_norm_g"], bsz, lp)
    qf = _q_up(proj, w["mla_q_norm_g"], wn, wr, wt, cos_t, sin_t, bsz, lp)
    kf, vf = _kv_up(proj, w["mla_kv_norm_g"], wk, wv, cos_t, sin_t, bsz, lp)
    o_b, yb_in, lse = _attn_fwd(qf, kf, vf, proj, bsz, lp)
    y_a, y_b, dh2, loss, d_final_g = _mid_fwd(ya_in, yb_in, proj, hp, loss_target, w["gla_proj"], w["mla_proj"],
                                              w["w_out"], w["final_norm_g"], bsz, lp)
    d_ya, d_o, dproj, delta, d_w_out, d_gla_proj, d_mla_proj = _mid_bwd(
        dh2, y_a, y_b, proj, ya_in, yb_in, o_b, w["w_out"], w["gla_proj"], w["mla_proj"], bsz, lp)
    dproj, d_gate, d_gla_norm = _gla_bwd(proj, gw_pad, w["gla_gate_b"], w["gla_norm_g"], o_raw, s_all, d_ya, dproj,
                                         bsz, lp)
    d_lr, d_gw_pad, d_gate_b = _gate_bwd(d_gate, proj, gw_pad)
    dqf, dkf, dvf = _attn_bwd(qf, kf, vf, d_o, lse, delta, bsz, lp)
    dproj, d_wn, d_wr, d_wt, d_qn = _q_up_bwd(dqf, proj, w["mla_q_norm_g"], wn, wr, wt, cos_t, sin_t, dproj,
                                              bsz, lp)
    dproj, d_wk, d_wv, d_kvn = _kv_up_bwd(dkf, dvf, proj, w["mla_kv_norm_g"], wk, wv, cos_t, sin_t, d_lr, dproj,
                                          bsz, lp)

    d_rope = (d_wr.reshape(MLA_QR, MLA_HEADS, LANE)[:, :, :MLA_ROPE]
              + _unrot_cols(d_wt.reshape(MLA_QR, MLA_HEADS, LANE)[:, :, :MLA_ROPE]))
    d_uq = jnp.concatenate([d_wn.reshape(MLA_QR, MLA_HEADS, LANE), d_rope], axis=-1).reshape(MLA_QR, MLA_HEADS * MLA_QK)
    d_ukv = jnp.concatenate([d_wk.reshape(MLA_KVR, MLA_HEADS, LANE), d_wv.reshape(MLA_KVR, MLA_HEADS, LANE)],
                            axis=-1).reshape(MLA_KVR, MLA_HEADS * (MLA_NOPE + MLA_DV))
    mats = dict(gla_gate_w=d_gw_pad[:GLA_RANK], gla_proj=d_gla_proj, mla_w_uq=d_uq, mla_w_ukv=d_ukv,
                mla_proj=d_mla_proj, w_out=d_w_out)
    packed = _pack_shards({n: _bf(_split8(mats[n], axis)) for n, _, axis in PACKED})
    d_w_ext_t, packed_parts = _dw_in(u, dproj, packed)
    w_in_slabs = _w_in_slabs(d_w_ext_t)
    d_hp, d_norm_g, w_in_parts = _dx_in(dproj, w_ext, hp, dh2, w["norm_g"], w_in_slabs)
    d_hp3 = d_hp.reshape(bsz, lp, D_MODEL)
    small = dict(meta_tokens=_meta_grad(d_hp3), norm_g=d_norm_g, gla_gate_b=d_gate_b, gla_norm_g=d_gla_norm,
                 mla_q_norm_g=d_qn, mla_kv_norm_g=d_kvn, final_norm_g=d_final_g)
    return loss, d_hp3[:, X0:, :], w_in_parts, packed_parts, small


PACKED = (("gla_gate_w", (GLA_RANK, GLA_KW // N_DEV), 1),
          ("gla_proj", (D_MODEL // N_DEV, D_MODEL), 0), ("mla_w_uq", (MLA_QR, MLA_HEADS * MLA_QK // N_DEV), 1),
          ("mla_w_ukv", (MLA_KVR, MLA_HEADS * (MLA_NOPE + MLA_DV) // N_DEV), 1),
          ("mla_proj", (D_MODEL // N_DEV, D_MODEL), 0), ("w_out", (D_MODEL // N_DEV, D_MODEL), 0))
REPLICATED = (("norm_g", D_MODEL), ("gla_gate_b", GLA_KW), ("gla_norm_g", GLA_DV), ("mla_q_norm_g", MLA_QR),
              ("mla_kv_norm_g", MLA_KVR), ("final_norm_g", D_MODEL))
PACK_ROWS = 480
PACK_BLOCK = 160
SMALL_ROWS = 48
LOSS_ROW = N_META + 25
W_IN_BLOCK = 128


def _all_gather(shards):
    n_arr = len(shards)
    pieces = []
    for a, s in enumerate(shards):
        step = s.shape[0] // 4 if s.shape[0] >= 4 * LANE else s.shape[0]
        pieces += [(a, slice(r, r + step)) for r in range(0, s.shape[0], step)]
    n_pc = len(pieces)

    def body(*refs):
        x_refs, out_refs = refs[:n_arr], refs[n_arr:2 * n_arr]
        send_sems, recv_sems, local_sems = refs[2 * n_arr:]
        x, y, c = _my_place()
        me, sibling = (x, y, c), (x, y, 1 - c)
        chips = [(1 - x, y), (x, 1 - y), (1 - x, 1 - y)]

        def copy(u, k, block, to, from_input=False):
            a, rows = pieces[u]
            slab = out_refs[a].at[4 * block[0] + 2 * block[1] + block[2], rows]
            return pltpu.make_async_remote_copy(
                src_ref=x_refs[a].at[rows] if from_input else slab, dst_ref=slab,
                send_sem=send_sems.at[7 * u + k], recv_sem=recv_sems.at[7 * u + k], device_id=to,
                device_id_type=MESH_ID)

        arrays = range(n_pc)
        mine = [pltpu.make_async_copy(x_refs[a], out_refs[a].at[4 * x + 2 * y + c], local_sems.at[a])
                for a in range(n_arr)]
        for cp in mine:
            cp.start()
        first = [copy(a, 0, me, sibling, True) for a in arrays]
        first += [copy(a, 1 + j, me, (*chip, c), True) for j, chip in enumerate(chips) for a in arrays]
        for cp in first:
            cp.start()
        passed = []
        for j, chip in enumerate(chips):
            for a in arrays:
                copy(a, 1 + j, (*chip, c), me).wait_recv()
                passed.append(copy(a, 4 + j, (*chip, c), sibling))
                passed[-1].start()
        for a in arrays:
            copy(a, 0, sibling, me).wait_recv()
        for j, chip in enumerate(chips):
            for a in arrays:
                copy(a, 4 + j, (*chip, 1 - c), me).wait_recv()
        for cp in first + passed:
            cp.wait_send()
        for cp in mine:
            cp.wait()

    anyspec = pl.BlockSpec(memory_space=pl.ANY)
    return pl.pallas_call(
        body, name="weights_all_gather",
        out_shape=[jax.ShapeDtypeStruct((N_DEV,) + s.shape, s.dtype) for s in shards],
        in_specs=[anyspec] * n_arr, out_specs=[anyspec] * n_arr,
        scratch_shapes=[pltpu.SemaphoreType.DMA((7 * n_pc,)), pltpu.SemaphoreType.DMA((7 * n_pc,)),
                        pltpu.SemaphoreType.DMA((n_arr,))],
    )(*shards)


def _small_exchange(slabs):
    def body(g_ref, recv_ref, send_sems, recv_sems, local_sem):
        _exchange(g_ref, recv_ref, send_sems, recv_sems, local_sem, True)
        _exchange(g_ref, recv_ref, send_sems, recv_sems, local_sem, False)

    vmem = pl.BlockSpec(memory_space=pltpu.VMEM)
    return pl.pallas_call(
        body, name="small_exchange", out_shape=jax.ShapeDtypeStruct(slabs.shape, slabs.dtype),
        in_specs=[vmem], out_specs=vmem, scratch_shapes=EXCHANGE_SEMS,
    )(slabs)


def _adamw(parts, w, m, v, block_rows, name):
    rows, cols = w.shape

    def body(p_ref, w_ref, m_ref, v_ref, g_out, d_out, m_out, v_out):
        g = p_ref[0].astype(F32)
        for s in range(1, N_DEV):
            g = g + p_ref[s].astype(F32)
        m_new = ADAM_B1 * m_ref[...] + (1.0 - ADAM_B1) * g
        v_new = ADAM_B2 * v_ref[...] + (1.0 - ADAM_B2) * (g * g)
        m_hat = m_new / (1.0 - ADAM_B1 ** ADAM_STEP)
        v_hat = v_new / (1.0 - ADAM_B2 ** ADAM_STEP)
        g_out[...] = g
        d_out[...] = -ADAM_LR * (m_hat / (jnp.sqrt(v_hat) + ADAM_EPS) + ADAM_WD * w_ref[...])
        m_out[...] = m_new
        v_out[...] = v_new

    spec = pl.BlockSpec((block_rows, cols), lambda i: (i, 0))
    return pl.pallas_call(
        body, name=name, grid=(pl.cdiv(rows, block_rows),),
        in_specs=[pl.BlockSpec((N_DEV, block_rows, cols), lambda i: (0, i, 0)), spec, spec, spec],
        out_specs=[spec] * 4, out_shape=[jax.ShapeDtypeStruct((rows, cols), F32)] * 4,
        compiler_params=_cp(("parallel",), 48),
    )(parts, w, m, v)


def _pack_rows_of(shape):
    rows = shape[0] * shape[1] // D_MODEL
    return -(-rows // 16) * 16


def _pack_shards(shards):
    parts = []
    for n, shape, _ in PACKED:
        a = shards[n]
        lead = a.shape[:-2]
        if shape[1] != D_MODEL:
            a = a.reshape(lead + (shape[0] * shape[1] // D_MODEL, D_MODEL))
        pad = _pack_rows_of(shape) - a.shape[-2]
        parts.append(jnp.pad(a, [(0, 0)] * len(lead) + [(0, pad), (0, 0)]) if pad else a)
    return jnp.concatenate(parts, axis=-2)


def _unpack_shards(packed):
    lead, out, off = packed.shape[:-2], {}, 0
    for n, shape, _ in PACKED:
        rows = shape[0] * shape[1] // D_MODEL
        out[n] = packed[..., off:off + rows, :].reshape(lead + shape)
        off += _pack_rows_of(shape)
    return out


def _split8(full, axis):
    r, c = full.shape
    if axis == 0:
        return full.reshape(N_DEV, r // N_DEV, c)
    return full.reshape(r, N_DEV, c // N_DEV).transpose(1, 0, 2)


def _join8(shards, axis):
    _, r, c = shards.shape
    if axis == 0:
        return shards.reshape(N_DEV * r, c)
    return shards.transpose(1, 0, 2).reshape(r, N_DEV * c)


def _pack_small(meta_shard, vals, loss_row):
    rows = jnp.concatenate([vals[n].reshape(-1, LANE) for n, _ in REPLICATED] + [loss_row], axis=0)
    rows = jnp.pad(rows, ((0, SMALL_ROWS - N_META - rows.shape[0]), (0, 0)))
    return jnp.concatenate([meta_shard, jnp.broadcast_to(rows, meta_shard.shape[:-2] + rows.shape)], axis=-2)


def _unpack_small(packed):
    out, off = {"meta_tokens": packed[:N_META]}, N_META
    for n, size in REPLICATED:
        out[n] = packed[off:off + size // LANE].reshape(1, size)
        off += size // LANE
    return out


def kernel(x, meta_tokens, norm_g, w_in, gla_gate_w, gla_gate_b, gla_norm_g, gla_proj, mla_q_norm_g, mla_w_uq, mla_kv_norm_g, mla_w_ukv, mla_proj, w_out, final_norm_g, loss_target, m_meta_tokens, m_norm_g, m_w_in, m_gla_gate_w, m_gla_gate_b, m_gla_norm_g, m_gla_proj, m_mla_q_norm_g, m_mla_w_uq, m_mla_kv_norm_g, m_mla_w_ukv, m_mla_proj, m_w_out, m_final_norm_g, v_meta_tokens, v_norm_g, v_w_in, v_gla_gate_w, v_gla_gate_b, v_gla_norm_g, v_gla_proj, v_mla_q_norm_g, v_mla_w_uq, v_mla_kv_norm_g, v_mla_w_ukv, v_mla_proj, v_w_out, v_final_norm_g):
    given = dict(meta_tokens=meta_tokens, norm_g=norm_g, w_in=w_in, gla_gate_w=gla_gate_w, gla_gate_b=gla_gate_b,
                 gla_norm_g=gla_norm_g, gla_proj=gla_proj, mla_q_norm_g=mla_q_norm_g, mla_w_uq=mla_w_uq,
                 mla_kv_norm_g=mla_kv_norm_g, mla_w_ukv=mla_w_ukv, mla_proj=mla_proj, w_out=w_out,
                 final_norm_g=final_norm_g)
    mom_m = dict(meta_tokens=m_meta_tokens, norm_g=m_norm_g, w_in=m_w_in, gla_gate_w=m_gla_gate_w,
                 gla_gate_b=m_gla_gate_b, gla_norm_g=m_gla_norm_g, gla_proj=m_gla_proj, mla_q_norm_g=m_mla_q_norm_g,
                 mla_w_uq=m_mla_w_uq, mla_kv_norm_g=m_mla_kv_norm_g, mla_w_ukv=m_mla_w_ukv, mla_proj=m_mla_proj,
                 w_out=m_w_out, final_norm_g=m_final_norm_g)
    mom_v = dict(meta_tokens=v_meta_tokens, norm_g=v_norm_g, w_in=v_w_in, gla_gate_w=v_gla_gate_w,
                 gla_gate_b=v_gla_gate_b, gla_norm_g=v_gla_norm_g, gla_proj=v_gla_proj, mla_q_norm_g=v_mla_q_norm_g,
                 mla_w_uq=v_mla_w_uq, mla_kv_norm_g=v_mla_kv_norm_g, mla_w_ukv=v_mla_w_ukv, mla_proj=v_mla_proj,
                 w_out=v_w_out, final_norm_g=v_final_norm_g)
    shapes = {n: a.shape for n, a in given.items()}
    shard2d = {n: s for n, s, _ in PACKED}
    shard2d["w_in"] = (D_MODEL, W_IN_SHARD)
    shard2d["meta_tokens"] = (N_META, LANE)

    def as2d(tree):
        out = {n: tree[n].reshape(shard2d[n]) for n in shard2d}
        out.update({n: tree[n].reshape(1, size) for n, size in REPLICATED})
        return out

    w_loc, m_loc, v_loc = as2d(given), as2d(mom_m), as2d(mom_v)

    w_in_all, meta_all = _all_gather([w_loc["w_in"].astype(BF16), w_loc["meta_tokens"]])
    packed = _pack_shards({n: w_loc[n].astype(BF16) for n, _, _ in PACKED})
    full = {"w_in": w_in_all, "meta_tokens": _join8(meta_all, 1), "packed": packed}
    for n, _ in REPLICATED:
        full[n] = w_loc[n]

    loss_part, grad_x, w_in_parts, packed_parts, small = _local_step(x, loss_target, full)
    small_all = _small_exchange(_pack_small(_split8(small["meta_tokens"], 1), small,
                                            jnp.broadcast_to(loss_part[:, :1], (1, LANE))))

    w_in_t = [t["w_in"].T for t in (w_loc, m_loc, v_loc)]
    g_w, d_w, m_w, v_w = (o.T for o in _adamw(w_in_parts, *w_in_t, W_IN_BLOCK, "adamw_w_in"))
    g_p, d_p, m_p, v_p = _adamw(packed_parts, _pack_shards(w_loc), _pack_shards(m_loc), _pack_shards(v_loc),
                                PACK_BLOCK, "adamw_packed")
    zero_row = jnp.zeros((1, LANE), F32)
    g_s, d_s, m_s, v_s = _adamw(small_all, *(_pack_small(t["meta_tokens"], t, zero_row) for t in (w_loc, m_loc, v_loc)),
                                SMALL_ROWS, "adamw_small")
    loss = g_s[LOSS_ROW, 0]

    order = ["meta_tokens", "norm_g", "w_in", "gla_gate_w", "gla_gate_b", "gla_norm_g", "gla_proj", "mla_q_norm_g",
             "mla_w_uq", "mla_kv_norm_g", "mla_w_ukv", "mla_proj", "w_out", "final_norm_g"]
    result = [loss, grad_x]
    for w_in_out, packed_sh, packed_sm in ((g_w, g_p, g_s), (d_w, d_p, d_s), (m_w, m_p, m_s), (v_w, v_p, v_s)):
        tree = _unpack_shards(packed_sh)
        tree.update(_unpack_small(packed_sm))
        tree["w_in"] = w_in_out
        result += [tree[n].reshape(shapes[n]) for n in order]
    return tuple(result)
```

```python
import jax
import jax.numpy as jnp
from jax import lax
from jax.experimental import pallas as pl
from jax.experimental.pallas import tpu as pltpu

F32 = jnp.float32
BF16 = jnp.bfloat16

D_MODEL = 1024
N_META = 16
EPS = 1e-6
FRONT = 48
X0 = FRONT + N_META
GLA_HEADS, GLA_DK, GLA_DV, GLA_RANK, GLA_CHUNK = 4, 128, 256, 16, 64
GLA_GATE_NORMALIZER = 16.0
GLA_KW = GLA_HEADS * GLA_DK
GLA_VW = GLA_HEADS * GLA_DV
MLA_HEADS, MLA_NOPE, MLA_ROPE, MLA_DV, MLA_QR, MLA_KVR = 8, 128, 64, 128, 256, 128
MLA_QK = MLA_NOPE + MLA_ROPE
ROPE_BASE = 10000.0
LANE = 128
QKW = 2 * LANE

C_V, C_Z, C_Q, C_K = 0, 1024, 2048, 2560
C_MZ, C_GG, C_GM = 3072, 4096, 5120
C_CKV, C_KR, C_KROT, C_LR = 6144, 6272, 6400, 6528
C_CQ = 6656
N_EXT = 6912
O_Q, O_K, O_V, O_LR, O_Z, O_CQ, O_CKV, O_KR, O_MZ, O_GG, O_GM, N_IN = (
    0, 512, 1024, 2048, 2064, 3088, 3344, 3472, 3536, 4560, 5584, 6608)

ADAM_LR, ADAM_B1, ADAM_B2, ADAM_EPS, ADAM_WD, ADAM_STEP = 0.001, 0.9, 0.999, 1e-08, 0.01, 10

N_DEV = 8
TOK = 192
ATT_BLOCK = 352
MXU_DEPTH = 256


def _cp(sems=None, vmem_mb=None):
    kw = {}
    if sems is not None:
        kw["dimension_semantics"] = sems
    if vmem_mb is not None:
        kw["vmem_limit_bytes"] = vmem_mb * 1024 * 1024
    return pltpu.CompilerParams(**kw)


def _dot(a, b):
    return jnp.dot(a, b, preferred_element_type=F32)


def _dot_nt(a, b):
    return lax.dot_general(a, b, (((1,), (1,)), ((), ())), preferred_element_type=F32)


def _dot_tn(a, b):
    return lax.dot_general(a, b, (((0,), (0,)), ((), ())), preferred_element_type=F32)


def _sigmoid(x):
    return 1.0 / (1.0 + jnp.exp(-x))


def _bf(x):
    return x.astype(BF16)


def _big_tok(tp):
    return 4 * TOK if tp % (4 * TOK) == 0 else TOK


def _attn_block(lp):
    return ATT_BLOCK if lp % ATT_BLOCK == 0 else TOK


def _wide_block(lp):
    return 2 * ATT_BLOCK if lp % (2 * ATT_BLOCK) == 0 else _attn_block(lp)


def _proj_in(x, head, norm_g, w_ext, packed):
    bsz, seq, _ = x.shape
    lp = X0 + seq
    tp = bsz * lp
    tm = _attn_block(lp)
    nb = lp // tm
    last = pl.cdiv(seq, tm) - 1

    def body(xa_ref, xb_ref, hd_ref, g_ref, w_ref, p_ref, h_ref, u_ref, o_ref, pall_ref, send_sems, recv_sems, local_sem):
        first = jnp.logical_and(pl.program_id(0) == 0, pl.program_id(1) == 0)

        @pl.when(first)
        def _():
            _exchange(p_ref, pall_ref, send_sems, recv_sems, local_sem, True, same=True)

        front = jnp.where(pl.program_id(1) == 0, hd_ref[...], xa_ref[0, tm - X0:, :])
        h = jnp.concatenate([front, xb_ref[0, :tm - X0, :]], axis=0)
        h_ref[...] = h
        r = lax.rsqrt(jnp.mean(h * h, axis=-1, keepdims=True) + EPS)
        u = _bf(h * r * g_ref[...])
        u_ref[...] = u
        o_ref[...] = _bf(_dot_nt(u, w_ref[...]))

        @pl.when(jnp.logical_and(pl.program_id(0) == bsz - 1, pl.program_id(1) == nb - 1))
        def _():
            _exchange(p_ref, pall_ref, send_sems, recv_sems, local_sem, False, same=True)

    anyspec = pl.BlockSpec(memory_space=pl.ANY)
    tok = lambda width: pl.BlockSpec((tm, width), lambda b, i: (b * nb + i, 0))
    return pl.pallas_call(
        body, name="proj_in", grid=(bsz, nb),
        in_specs=[pl.BlockSpec((1, tm, D_MODEL), lambda b, i: (b, jnp.maximum(i - 1, 0), 0)),
                  pl.BlockSpec((1, tm, D_MODEL), lambda b, i: (b, jnp.minimum(i, last), 0)),
                  pl.BlockSpec((X0, D_MODEL), lambda b, i: (0, 0)),
                  pl.BlockSpec((1, D_MODEL), lambda b, i: (0, 0)),
                  pl.BlockSpec((N_EXT, D_MODEL), lambda b, i: (0, 0), pipeline_mode=pl.Buffered(1)), anyspec],
        out_specs=[tok(D_MODEL), tok(D_MODEL), tok(N_EXT), anyspec],
        out_shape=[jax.ShapeDtypeStruct((tp, D_MODEL), F32), jax.ShapeDtypeStruct((tp, D_MODEL), BF16),
                   jax.ShapeDtypeStruct((tp, N_EXT), BF16),
                   jax.ShapeDtypeStruct((N_DEV,) + packed.shape, packed.dtype)],
        scratch_shapes=EXCHANGE_SEMS,
        compiler_params=_cp(("arbitrary", "arbitrary"), 56),
    )(x, x, head, norm_g, w_ext, packed)


def _gla_group(n_chunks):
    return 11 if n_chunks % 11 == 0 else 3


def _tri_dot(tri, x):
    hi = _bf(x)
    rest = x - hi.astype(F32)
    mid = _bf(rest)
    return _dot(tri, hi) + _dot(tri, mid) + _dot(tri, _bf(rest - mid.astype(F32)))


def _gla_gates(q_ref, k_ref, lr_ref, gw_ref, gb_ref, rows, not_first):
    z = _dot(lr_ref[rows, :], gw_ref[...]) + gb_ref[...]
    logsig = jnp.minimum(z, 0.0) - jnp.log(1.0 + jnp.exp(-jnp.abs(z)))
    row = lax.broadcasted_iota(jnp.int32, (GLA_CHUNK, GLA_KW), 0)
    live = jnp.logical_or(not_first, row >= FRONT)
    g = jnp.where(live, logsig * (1.0 / GLA_GATE_NORMALIZER), 0.0)
    ri = lax.broadcasted_iota(jnp.int32, (GLA_CHUNK, GLA_CHUNK), 0)
    ci = lax.broadcasted_iota(jnp.int32, (GLA_CHUNK, GLA_CHUNK), 1)
    tril = ci <= ri
    b = _tri_dot(_bf(tril.astype(F32)), g)
    bl = jnp.sum(jnp.where(row == GLA_CHUNK - 1, b, 0.0), axis=0, keepdims=True)
    eb, enb, elb, ebl = jnp.exp(b), jnp.exp(-b), jnp.exp(bl - b), jnp.exp(bl)
    q = q_ref[rows, :].astype(F32) * (GLA_DK ** -0.5)
    k = k_ref[rows, :].astype(F32)
    qe, ke, kl = q * eb, k * enb, k * elb
    return dict(z=z, live=live, tril=tril, row=row, eb=eb, enb=enb, elb=elb, ebl=ebl, qe=qe, ke=ke, kl=kl,
                qe_b=_bf(qe), ke_b=_bf(ke), kl_b=_bf(kl))


def _gla_in_specs(n_groups, gla_rows, rev):
    def rb(b, n):
        return b * n_groups + ((n_groups - 1 - n) if rev else n)

    return rb, [pl.BlockSpec((gla_rows, GLA_KW), lambda b, n: (rb(b, n), C_Q // GLA_KW)),
                pl.BlockSpec((gla_rows, GLA_KW), lambda b, n: (rb(b, n), C_K // GLA_KW)),
                pl.BlockSpec((gla_rows, GLA_VW), lambda b, n: (rb(b, n), C_V // GLA_VW)),
                pl.BlockSpec((gla_rows, GLA_VW), lambda b, n: (rb(b, n), C_Z // GLA_VW)),
                pl.BlockSpec((gla_rows, LANE), lambda b, n: (rb(b, n), C_LR // LANE)),
                pl.BlockSpec((LANE, GLA_KW), lambda b, n: (0, 0)),
                pl.BlockSpec((1, GLA_KW), lambda b, n: (0, 0)),
                pl.BlockSpec((1, GLA_DV), lambda b, n: (0, 0))]


def _gla_fwd(proj, gw_pad, gate_b, gla_norm_g, bsz, lp):
    n_chunks = lp // GLA_CHUNK
    gla_group = _gla_group(n_chunks)
    gla_rows = gla_group * GLA_CHUNK
    n_groups = n_chunks // gla_group
    tp = bsz * lp

    def body(q_ref, k_ref, v_ref, z_ref, lr_ref, gw_ref, gb_ref, gn_ref, oraw_ref, ya_ref, sall_ref, st_scr):
        grp = pl.program_id(1)

        @pl.when(grp == 0)
        def _():
            st_scr[...] = jnp.zeros_like(st_scr)

        chunks = [slice(j * GLA_CHUNK, (j + 1) * GLA_CHUNK) for j in range(gla_group)]
        cs = [_gla_gates(q_ref, k_ref, lr_ref, gw_ref, gb_ref, rows, True if j else grp > 0)
              for j, rows in enumerate(chunks)]
        gn = gn_ref[...]
        sts = [st_scr[h] for h in range(GLA_HEADS)]
        heads = [(slice(h * GLA_DK, (h + 1) * GLA_DK), slice(h * GLA_DV, (h + 1) * GLA_DV)) for h in range(GLA_HEADS)]
        a_all = [[_bf(jnp.where(c["tril"], _dot_nt(c["qe_b"][:, ks], c["ke_b"][:, ks]), 0.0)) for ks, _ in heads]
                 for c in cs]
        u_all = [[_dot_tn(v_ref[rows, vs], c["kl_b"][:, ks]) for ks, vs in heads] for rows, c in zip(chunks, cs)]
        for j, (rows, c) in enumerate(zip(chunks, cs)):
            for h, (ks, vs) in enumerate(heads):
                st = sts[h]
                sall_ref[0, j, h] = st
                o = _dot(a_all[j][h], v_ref[rows, vs]) + _dot_nt(c["qe_b"][:, ks], _bf(st))
                sts[h] = st * c["ebl"][:, ks] + u_all[j][h]
                oraw_ref[rows, vs] = o
                r = lax.rsqrt(jnp.mean(o * o, axis=-1, keepdims=True) + EPS)
                zg = z_ref[rows, vs].astype(F32)
                ya_ref[rows, vs] = _bf((o * r * gn) * (zg * _sigmoid(zg)))
        for h in range(GLA_HEADS):
            st_scr[h] = sts[h]

    rb, in_specs = _gla_in_specs(n_groups, gla_rows, False)
    return pl.pallas_call(
        body, name="gla_fwd", grid=(bsz, n_groups), in_specs=in_specs,
        out_specs=[pl.BlockSpec((gla_rows, GLA_VW), lambda b, n: (rb(b, n), 0)),
                   pl.BlockSpec((gla_rows, GLA_VW), lambda b, n: (rb(b, n), 0)),
                   pl.BlockSpec((1, gla_group, GLA_HEADS, GLA_DV, GLA_DK), lambda b, n: (b, n, 0, 0, 0))],
        out_shape=[jax.ShapeDtypeStruct((tp, GLA_VW), F32), jax.ShapeDtypeStruct((tp, GLA_VW), BF16),
                   jax.ShapeDtypeStruct((bsz, n_chunks, GLA_HEADS, GLA_DV, GLA_DK), F32)],
        scratch_shapes=[pltpu.VMEM((GLA_HEADS, GLA_DV, GLA_DK), F32)],
        compiler_params=_cp(("parallel", "arbitrary"), 56),
    )(proj, proj, proj, proj, proj, gw_pad, gate_b, gla_norm_g)


def _gla_bwd(proj, gw_pad, gate_b, gla_norm_g, o_raw, s_all, d_ya, dproj, bsz, lp):
    n_chunks = lp // GLA_CHUNK
    gla_group = _gla_group(n_chunks)
    gla_rows = gla_group * GLA_CHUNK
    n_groups = n_chunks // gla_group
    tp = bsz * lp

    def body(q_ref, k_ref, v_ref, z_ref, lr_ref, gw_ref, gb_ref, gn_ref, o_ref, s_ref, dya_ref, _,
             dp_ref, dz_ref, dgn_ref, dst_scr):
        dv_ref, dzg_ref = dp_ref.at[:, C_V:C_V + GLA_VW], dp_ref.at[:, C_Z:C_Z + GLA_VW]

        @pl.when(jnp.logical_and(pl.program_id(0) == 0, pl.program_id(1) == 0))
        def _():
            dgn_ref[...] = jnp.zeros_like(dgn_ref)

        @pl.when(pl.program_id(1) == 0)
        def _():
            dst_scr[...] = jnp.zeros_like(dst_scr)

        grp = n_groups - 1 - pl.program_id(1)
        chunks = [slice(j * GLA_CHUNK, (j + 1) * GLA_CHUNK) for j in range(gla_group)]
        cs = [_gla_gates(q_ref, k_ref, lr_ref, gw_ref, gb_ref, rows, True if j else grp > 0)
              for j, rows in enumerate(chunks)]
        gn = gn_ref[...]
        dgn = jnp.zeros((1, GLA_DV), F32)
        dqe_h, dke_h, dkl_h, dbl_h = ([[None] * GLA_HEADS for _ in chunks] for _ in range(4))
        dsts = [dst_scr[h] for h in range(GLA_HEADS)]
        for j in reversed(range(gla_group)):
            rows, c = chunks[j], cs[j]
            for h in range(GLA_HEADS):
                ks, vs = slice(h * GLA_DK, (h + 1) * GLA_DK), slice(h * GLA_DV, (h + 1) * GLA_DV)
                dst = dsts[h]
                v = v_ref[rows, vs]
                st = s_ref[0, j, h]
                o = o_ref[rows, vs]
                r = lax.rsqrt(jnp.mean(o * o, axis=-1, keepdims=True) + EPS)
                xh = o * r
                zg = z_ref[rows, vs].astype(F32)
                sg = _sigmoid(zg)
                dy = dya_ref[rows, vs].astype(F32)
                dzg_ref[rows, vs] = _bf(dy * (xh * gn) * (sg * (1.0 + zg * (1.0 - sg))))
                t = dy * (zg * sg)
                dgn += jnp.sum(t * xh, axis=0, keepdims=True)
                dxh = t * gn
                do_b = _bf(r * (dxh - xh * jnp.mean(dxh * xh, axis=-1, keepdims=True)))
                qe_b, ke_b, kl_b, dst_b = c["qe_b"][:, ks], c["ke_b"][:, ks], c["kl_b"][:, ks], _bf(dst)
                a = jnp.where(c["tril"], _dot_nt(qe_b, ke_b), 0.0)
                da_b = _bf(jnp.where(c["tril"], _dot_nt(do_b, v), 0.0))
                dqe_h[j][h] = _dot(da_b, ke_b) + _dot(do_b, _bf(st))
                dke_h[j][h] = _dot_tn(da_b, qe_b)
                dkl = _dot(v, dst_b)
                dkl_h[j][h] = dkl
                dv_ref[rows, vs] = _bf(_dot_tn(_bf(a), do_b) + _dot_nt(kl_b, dst_b))
                ddecay = jnp.sum(dst * st, axis=0, keepdims=True)
                dbl_h[j][h] = jnp.sum(dkl * c["kl"][:, ks], axis=0, keepdims=True) + ddecay * c["ebl"][:, ks]
                dsts[h] = dst * c["ebl"][:, ks] + _dot_tn(do_b, qe_b)
        for h in range(GLA_HEADS):
            dst_scr[h] = dsts[h]
        dgn_ref[...] += dgn
        ri = lax.broadcasted_iota(jnp.int32, (GLA_CHUNK, GLA_CHUNK), 0)
        ci = lax.broadcasted_iota(jnp.int32, (GLA_CHUNK, GLA_CHUNK), 1)
        triu = _bf((ci >= ri).astype(F32))
        for j, (rows, c) in enumerate(zip(chunks, cs)):
            dqe, dke, dkl, dbl = (jnp.concatenate(p[j], axis=1) for p in (dqe_h, dke_h, dkl_h, dbl_h))
            db = dqe * c["qe"] - dke * c["ke"] - dkl * c["kl"] + jnp.where(c["row"] == GLA_CHUNK - 1, dbl, 0.0)
            dg = _tri_dot(triu, db)
            dg = jnp.where(c["live"], dg, 0.0)
            dz_ref[rows, :] = dg * (1.0 / GLA_GATE_NORMALIZER) * _sigmoid(-c["z"])
            dp_ref[rows, C_Q:C_Q + GLA_KW] = _bf(dqe * c["eb"] * (GLA_DK ** -0.5))
            dp_ref[rows, C_K:C_K + GLA_KW] = _bf(dke * c["enb"] + dkl * c["elb"])

    rb, in_specs = _gla_in_specs(n_groups, gla_rows, True)
    wide = pl.BlockSpec((gla_rows, GLA_VW), lambda b, n: (rb(b, n), 0))
    group = C_MZ
    return pl.pallas_call(
        body, name="gla_bwd", grid=(bsz, n_groups),
        in_specs=in_specs + [wide, pl.BlockSpec((1, gla_group, GLA_HEADS, GLA_DV, GLA_DK),
                                                lambda b, n: (b, n_groups - 1 - n, 0, 0, 0)), wide,
                             pl.BlockSpec(memory_space=pl.ANY)],
        out_specs=[pl.BlockSpec((gla_rows, group), lambda b, n: (rb(b, n), 0)),
                   pl.BlockSpec((gla_rows, GLA_KW), lambda b, n: (rb(b, n), 0)),
                   pl.BlockSpec((1, GLA_DV), lambda b, n: (0, 0))],
        out_shape=[jax.ShapeDtypeStruct((tp, N_EXT), BF16), jax.ShapeDtypeStruct((tp, GLA_KW), F32),
                   jax.ShapeDtypeStruct((1, GLA_DV), F32)],
        input_output_aliases={11: 0},
        scratch_shapes=[pltpu.VMEM((GLA_HEADS, GLA_DV, GLA_DK), F32)],
        compiler_params=_cp(("arbitrary", "arbitrary"), 56),
    )(proj, proj, proj, proj, proj, gw_pad, gate_b, gla_norm_g, o_raw, s_all, d_ya, dproj)


def _gate_bwd(dz, proj, gw_pad):
    tp = dz.shape[0]
    tm = _big_tok(tp)

    def body(dz_ref, lr_ref, gw_ref, dlr_ref, dgw_ref, dgb_ref):
        @pl.when(pl.program_id(0) == 0)
        def _():
            dgw_ref[...] = jnp.zeros_like(dgw_ref)
            dgb_ref[...] = jnp.zeros_like(dgb_ref)

        dz = dz_ref[...]
        dz_b = _bf(dz)
        dlr_ref[...] = _bf(_dot_nt(dz_b, gw_ref[...]))
        dgw_ref[...] += _dot_tn(lr_ref[...], dz_b)
        dgb_ref[...] += jnp.sum(dz, axis=0, keepdims=True)

    return pl.pallas_call(
        body, name="gate_bwd", grid=(tp // tm,),
        in_specs=[pl.BlockSpec((tm, GLA_KW), lambda i: (i, 0)),
                  pl.BlockSpec((tm, LANE), lambda i: (i, C_LR // LANE)),
                  pl.BlockSpec((LANE, GLA_KW), lambda i: (0, 0))],
        out_specs=[pl.BlockSpec((tm, LANE), lambda i: (i, 0)),
                   pl.BlockSpec((LANE, GLA_KW), lambda i: (0, 0)),
                   pl.BlockSpec((1, GLA_KW), lambda i: (0, 0))],
        out_shape=[jax.ShapeDtypeStruct((tp, LANE), BF16), jax.ShapeDtypeStruct((LANE, GLA_KW), F32),
                   jax.ShapeDtypeStruct((1, GLA_KW), F32)],
        compiler_params=_cp(("arbitrary",)),
    )(dz, proj, gw_pad)


def _rms_fwd(x):
    r = lax.rsqrt(jnp.mean(x * x, axis=-1, keepdims=True) + EPS)
    return x * r, r


def _rms_bwd(dy, xh, r, g):
    dxh = dy * g
    dx = r * (dxh - xh * jnp.mean(dxh * xh, axis=-1, keepdims=True))
    return dx, jnp.sum(dy * xh, axis=0, keepdims=True)


def _q_up(proj, q_norm_g, wn, wr, wt, cos_t, sin_t, bsz, lp):
    tp = bsz * lp
    tok = _wide_block(lp)
    nb = lp // tok

    def body(cq_ref, g_ref, wn_ref, wr_ref, wt_ref, cos_ref, sin_ref, q_ref):
        xh, _ = _rms_fwd(cq_ref[...].astype(F32))
        cqn = _bf(xh * g_ref[...])
        nope = _dot(cqn, wn_ref[...])
        rope = _dot(cqn, wr_ref[...])
        rot = _dot(cqn, wt_ref[...])
        cos, sin = cos_ref[...], sin_ref[...]
        one = (lax.broadcasted_iota(jnp.int32, (tok, LANE), 1) == BIAS_LANE).astype(F32)
        for h in range(MLA_HEADS):
            sl = slice(h * LANE, (h + 1) * LANE)
            q_ref[:, h * QKW:h * QKW + LANE] = _bf(nope[:, sl])
            q_ref[:, h * QKW + LANE:(h + 1) * QKW] = _bf(rope[:, sl] * cos + rot[:, sl] * sin + one)

    wspec = pl.BlockSpec((MLA_QR, MLA_HEADS * LANE), lambda b, i: (0, 0))
    tspec = pl.BlockSpec((tok, LANE), lambda b, i: (i, 0))
    return pl.pallas_call(
        body, name="mla_q_up", grid=(bsz, nb),
        in_specs=[pl.BlockSpec((tok, MLA_QR), lambda b, i: (b * nb + i, C_CQ // MLA_QR)),
                  pl.BlockSpec((1, MLA_QR), lambda b, i: (0, 0)), wspec, wspec, wspec, tspec, tspec],
        out_specs=pl.BlockSpec((tok, MLA_HEADS * QKW), lambda b, i: (b * nb + i, 0)),
        out_shape=jax.ShapeDtypeStruct((tp, MLA_HEADS * QKW), BF16),
        compiler_params=_cp(("parallel", "parallel")),
    )(proj, q_norm_g, wn, wr, wt, cos_t, sin_t)


def _kv_up(proj, kv_norm_g, wk, wv, cos_t, sin_t, bsz, lp):
    tp = bsz * lp
    tok = _wide_block(lp)
    nb = lp // tok

    def body(ckv_ref, kr_ref, krot_ref, g_ref, wk_ref, wv_ref, cos_ref, sin_ref, k_ref, v_ref):
        xh, _ = _rms_fwd(ckv_ref[...].astype(F32))
        cn = _bf(xh * g_ref[...])
        kn = _dot(cn, wk_ref[...])
        v_ref[...] = _bf(_dot(cn, wv_ref[...]))
        pos = pl.program_id(1) * tok + lax.broadcasted_iota(jnp.int32, (tok, LANE), 0)
        lane = lax.broadcasted_iota(jnp.int32, (tok, LANE), 1)
        bias = jnp.where(jnp.logical_and(lane == BIAS_LANE, pos < FRONT), KEY_BIAS, 0.0)
        kr = _bf(kr_ref[...].astype(F32) * cos_ref[...] + krot_ref[...].astype(F32) * sin_ref[...] + bias)
        for h in range(MLA_HEADS):
            k_ref[:, h * QKW:h * QKW + LANE] = _bf(kn[:, h * LANE:(h + 1) * LANE])
            k_ref[:, h * QKW + LANE:(h + 1) * QKW] = kr

    wspec = pl.BlockSpec((MLA_KVR, MLA_HEADS * LANE), lambda b, i: (0, 0))
    tspec = pl.BlockSpec((tok, LANE), lambda b, i: (i, 0))
    return pl.pallas_call(
        body, name="mla_kv_up", grid=(bsz, nb),
        in_specs=[pl.BlockSpec((tok, LANE), lambda b, i: (b * nb + i, C_CKV // LANE)),
                  pl.BlockSpec((tok, LANE), lambda b, i: (b * nb + i, C_KR // LANE)),
                  pl.BlockSpec((tok, LANE), lambda b, i: (b * nb + i, C_KROT // LANE)),
                  pl.BlockSpec((1, MLA_KVR), lambda b, i: (0, 0)), wspec, wspec, tspec, tspec],
        out_specs=[pl.BlockSpec((tok, MLA_HEADS * QKW), lambda b, i: (b * nb + i, 0)),
                   pl.BlockSpec((tok, MLA_HEADS * LANE), lambda b, i: (b * nb + i, 0))],
        out_shape=[jax.ShapeDtypeStruct((tp, MLA_HEADS * QKW), BF16),
                   jax.ShapeDtypeStruct((tp, MLA_HEADS * LANE), BF16)],
        compiler_params=_cp(("parallel", "parallel")),
    )(proj, proj, proj, kv_norm_g, wk, wv, cos_t, sin_t)


ATT_SCALE = MLA_QK ** -0.5


KEY_BIAS = -1e30
BIAS_LANE = MLA_ROPE
NEG = 2 * KEY_BIAS
LOG2E = 1.4426950408889634
EXP2_SCALE = ATT_SCALE * LOG2E


def _causal_fill(s, r0, fill):
    tq, kmax = s.shape
    a = r0 // LANE * LANE
    mask = (a + lax.broadcasted_iota(jnp.int32, (tq, kmax - a), 1)
            <= r0 + lax.broadcasted_iota(jnp.int32, (tq, kmax - a), 0))
    right = jnp.where(mask, s[:, a:], fill)
    return jnp.concatenate([s[:, :a], right], axis=1) if a else right


def _attn_fwd(qf, kf, vf, proj, bsz, lp):
    tp = bsz * lp
    tq = _attn_block(lp)
    nh = 2

    def body(q_ref, k_ref, v_ref, mz_ref, ob_ref, yb_ref, lse_ref):
        starts = list(range(0, lp, tq))
        for pair in (starts[i:i + 2] for i in range(0, len(starts), 2)):
            work = [(r0, h) for r0 in pair for h in range(nh)]
            ss = [_causal_fill(_dot_nt(q_ref[r0:r0 + tq, h * QKW:(h + 1) * QKW],
                                       k_ref[0:r0 + tq, h * QKW:(h + 1) * QKW]), r0, NEG) for r0, h in work]
            ms = [jnp.max(s, axis=-1, keepdims=True) for s in ss]
            ps = [jnp.exp2((s - m) * EXP2_SCALE) for s, m in zip(ss, ms)]
            ls = [jnp.sum(p, axis=-1, keepdims=True) for p in ps]
            for (r0, h), p, m, l in zip(work, ps, ms, ls):
                rows, cols = slice(r0, r0 + tq), slice(h * MLA_DV, (h + 1) * MLA_DV)
                o = _dot(_bf(p), v_ref[0:r0 + tq, cols]) / l
                ob_ref[rows, cols] = _bf(o)
                mz = mz_ref[rows, cols].astype(F32)
                yb_ref[rows, cols] = _bf(o * (mz * _sigmoid(mz)))
                lse_ref[0, h, rows, :] = jnp.broadcast_to(m * EXP2_SCALE + jnp.log2(l), (tq, LANE))

    head = lambda off: pl.BlockSpec((lp, nh * MLA_DV), lambda b, h: (b, off + h))
    wide = pl.BlockSpec((lp, nh * QKW), lambda b, h: (b, h))
    return pl.pallas_call(
        body, name="mla_attn_fwd", grid=(bsz, MLA_HEADS // nh),
        in_specs=[wide, wide, head(0), head(C_MZ // (nh * MLA_DV))],
        out_specs=[head(0), head(0), pl.BlockSpec((1, nh, lp, LANE), lambda b, h: (b, h, 0, 0))],
        out_shape=[jax.ShapeDtypeStruct((tp, MLA_HEADS * MLA_DV), BF16),
                   jax.ShapeDtypeStruct((tp, MLA_HEADS * MLA_DV), BF16),
                   jax.ShapeDtypeStruct((bsz, MLA_HEADS, lp, LANE), F32)],
        compiler_params=_cp(("parallel", "parallel"), 56),
    )(qf, kf, vf, proj)


def _attn_bwd_blocks(lp):
    return [(0, X0)] + [(r0, min(MXU_DEPTH, lp - r0)) for r0 in range(X0, lp, MXU_DEPTH)]


def _attn_bwd(qf, kf, vf, d_o, lse, delta, bsz, lp):
    tp = bsz * lp

    def body(q_ref, k_ref, v_ref, do_ref, lse_ref, dl_ref, dq_ref, dk_ref, dv_ref, dk_acc, dv_acc):
        dk_acc[...] = jnp.zeros_like(dk_acc)
        dv_acc[...] = jnp.zeros_like(dv_acc)
        for r0, tq in _attn_bwd_blocks(lp):
            rows, kmax = slice(r0, r0 + tq), r0 + tq
            q, do = q_ref[rows, :], do_ref[rows, :]
            k, v = k_ref[0:kmax, :], v_ref[0:kmax, :]
            p = jnp.exp2(_dot_nt(q, k) * EXP2_SCALE - lse_ref[0, 0, rows, :][:, :1])
            p = _causal_fill(p, r0, 0.0)
            ds = _bf(p * (_dot_nt(do, v) - dl_ref[0, rows, :][:, :1]))
            dq_ref[rows, :] = _bf(_dot(ds, k) * ATT_SCALE)
            dk_acc[0:kmax, :] += _dot_tn(ds, q)
            dv_acc[0:kmax, :] += _dot_tn(_bf(p), do)
        dk_ref[...] = _bf(dk_acc[...] * ATT_SCALE)
        dv_ref[...] = _bf(dv_acc[...])

    wide = pl.BlockSpec((lp, QKW), lambda b, h: (b, h))
    narrow = pl.BlockSpec((lp, MLA_DV), lambda b, h: (b, h))
    stat = pl.BlockSpec((1, 1, lp, LANE), lambda b, h: (b, h, 0, 0))
    return pl.pallas_call(
        body, name="mla_attn_bwd", grid=(bsz, MLA_HEADS),
        in_specs=[wide, wide, narrow, narrow, stat, pl.BlockSpec((1, lp, LANE), lambda b, h: (h, b, 0))],
        out_specs=[wide, wide, narrow],
        out_shape=[jax.ShapeDtypeStruct((tp, MLA_HEADS * QKW), BF16), jax.ShapeDtypeStruct((tp, MLA_HEADS * QKW), BF16),
                   jax.ShapeDtypeStruct((tp, MLA_HEADS * MLA_DV), BF16)],
        scratch_shapes=[pltpu.VMEM((lp, QKW), F32), pltpu.VMEM((lp, MLA_DV), F32)],
        compiler_params=_cp(("parallel", "parallel"), 56),
    )(qf, kf, vf, d_o, lse, delta)


def _q_up_bwd(dqf, proj, q_norm_g, wn, wr, wt, cos_t, sin_t, dproj, bsz, lp):
    tp = bsz * lp
    tok = _wide_block(lp)
    nb = lp // tok
    hw = MLA_HEADS * LANE

    def body(dq_ref, cq_ref, g_ref, wn_ref, wr_ref, wt_ref, cos_ref, sin_ref, _,
             dcq_ref, dwn_ref, dwr_ref, dwt_ref, dg_ref):
        @pl.when(jnp.logical_and(pl.program_id(0) == 0, pl.program_id(1) == 0))
        def _():
            for r in (dwn_ref, dwr_ref, dwt_ref, dg_ref):
                r[...] = jnp.zeros_like(r)

        g = g_ref[...]
        xh, r = _rms_fwd(cq_ref[...].astype(F32))
        cqn = _bf(xh * g)
        dn = jnp.concatenate([dq_ref[:, h * QKW:h * QKW + LANE] for h in range(MLA_HEADS)], axis=1)
        dr = jnp.concatenate([dq_ref[:, h * QKW + LANE:(h + 1) * QKW] for h in range(MLA_HEADS)], axis=1).astype(F32)
        dr_c = _bf(dr * jnp.tile(cos_ref[...], (1, MLA_HEADS)))
        dr_s = _bf(dr * jnp.tile(sin_ref[...], (1, MLA_HEADS)))
        dcqn = _dot_nt(dn, wn_ref[...]) + _dot_nt(dr_c, wr_ref[...]) + _dot_nt(dr_s, wt_ref[...])
        dwn_ref[...] += _dot_tn(cqn, dn)
        dwr_ref[...] += _dot_tn(cqn, dr_c)
        dwt_ref[...] += _dot_tn(cqn, dr_s)
        dx, dg = _rms_bwd(dcqn, xh, r, g)
        dcq_ref[...] = _bf(dx)
        dg_ref[...] += dg

    aspec = pl.BlockSpec((MLA_QR, hw), lambda b, i: (0, 0))
    tspec = pl.BlockSpec((tok, LANE), lambda b, i: (i, 0))
    return pl.pallas_call(
        body, name="mla_q_up_bwd", grid=(bsz, nb),
        in_specs=[pl.BlockSpec((tok, MLA_HEADS * QKW), lambda b, i: (b * nb + i, 0)),
                  pl.BlockSpec((tok, MLA_QR), lambda b, i: (b * nb + i, C_CQ // MLA_QR)),
                  pl.BlockSpec((1, MLA_QR), lambda b, i: (0, 0)), aspec, aspec, aspec, tspec, tspec,
                  pl.BlockSpec(memory_space=pl.ANY)],
        out_specs=[pl.BlockSpec((tok, MLA_QR), lambda b, i: (b * nb + i, C_CQ // MLA_QR)), aspec, aspec, aspec,
                   pl.BlockSpec((1, MLA_QR), lambda b, i: (0, 0))],
        out_shape=[jax.ShapeDtypeStruct((tp, N_EXT), BF16)] + [jax.ShapeDtypeStruct((MLA_QR, hw), F32)] * 3
        + [jax.ShapeDtypeStruct((1, MLA_QR), F32)],
        input_output_aliases={8: 0},
        compiler_params=_cp(("arbitrary", "arbitrary")),
    )(dqf, proj, q_norm_g, wn, wr, wt, cos_t, sin_t, dproj)


def _kv_up_bwd(dkf, dvf, proj, kv_norm_g, wk, wv, cos_t, sin_t, d_lr, dproj, bsz, lp):
    tp = bsz * lp
    tok = _wide_block(lp)
    nb = lp // tok
    hw = MLA_HEADS * LANE

    def body(dk_ref, dv_ref, ckv_ref, g_ref, wk_ref, wv_ref, cos_ref, sin_ref, dlr_ref, _,
             dp_ref, dwk_ref, dwv_ref, dg_ref):
        dckv_ref, dkr_ref, dkrot_ref = (dp_ref.at[:, j * LANE:(j + 1) * LANE] for j in range(3))
        dp_ref[:, 3 * LANE:] = dlr_ref[...]
        @pl.when(jnp.logical_and(pl.program_id(0) == 0, pl.program_id(1) == 0))
        def _():
            for r in (dwk_ref, dwv_ref, dg_ref):
                r[...] = jnp.zeros_like(r)

        g = g_ref[...]
        xh, r = _rms_fwd(ckv_ref[...].astype(F32))
        cn = _bf(xh * g)
        dv = dv_ref[...]
        dn = jnp.concatenate([dk_ref[:, h * QKW:h * QKW + LANE] for h in range(MLA_HEADS)], axis=1)
        dcn = _dot_nt(dv, wv_ref[...]) + _dot_nt(dn, wk_ref[...])
        dwv_ref[...] += _dot_tn(cn, dv)
        dwk_ref[...] += _dot_tn(cn, dn)
        drope = jnp.zeros((tok, LANE), F32)
        for h in range(MLA_HEADS):
            drope += dk_ref[:, h * QKW + LANE:(h + 1) * QKW].astype(F32)
        dkr_ref[...] = _bf(drope * cos_ref[...])
        dkrot_ref[...] = _bf(drope * sin_ref[...])
        dx, dg = _rms_bwd(dcn, xh, r, g)
        dckv_ref[...] = _bf(dx)
        dg_ref[...] += dg

    aspec = pl.BlockSpec((MLA_KVR, hw), lambda b, i: (0, 0))
    tspec = pl.BlockSpec((tok, LANE), lambda b, i: (i, 0))
    ospec = pl.BlockSpec((tok, LANE), lambda b, i: (b * nb + i, 0))
    return pl.pallas_call(
        body, name="mla_kv_up_bwd", grid=(bsz, nb),
        in_specs=[pl.BlockSpec((tok, MLA_HEADS * QKW), lambda b, i: (b * nb + i, 0)),
                  pl.BlockSpec((tok, hw), lambda b, i: (b * nb + i, 0)),
                  pl.BlockSpec((tok, LANE), lambda b, i: (b * nb + i, C_CKV // LANE)),
                  pl.BlockSpec((1, MLA_KVR), lambda b, i: (0, 0)), aspec, aspec, tspec, tspec, ospec,
                  pl.BlockSpec(memory_space=pl.ANY)],
        out_specs=[pl.BlockSpec((tok, 4 * LANE), lambda b, i: (b * nb + i, C_CKV // (4 * LANE))), aspec, aspec,
                   pl.BlockSpec((1, MLA_KVR), lambda b, i: (0, 0))],
        out_shape=[jax.ShapeDtypeStruct((tp, N_EXT), BF16)] + [jax.ShapeDtypeStruct((MLA_KVR, hw), F32)] * 2
        + [jax.ShapeDtypeStruct((1, MLA_KVR), F32)],
        input_output_aliases={9: 0},
        compiler_params=_cp(("arbitrary", "arbitrary")),
    )(dkf, dvf, proj, kv_norm_g, wk, wv, cos_t, sin_t, d_lr, dproj)


def _mid_fwd(ya_in, yb_in, proj, hp, target, w_gp, w_mp, w_o, final_g, bsz, lp):
    tp = bsz * lp
    tm = _wide_block(lp)
    nb = lp // tm
    last = pl.cdiv(lp - X0, tm) - 1

    def body(ya_ref, yb_ref, gg_ref, gm_ref, h_ref, ta_ref, tb_ref, wgp_ref, wmp_ref, wo_ref, fg_ref,
             ya_out, yb_out, dh_ref, loss_ref, dfg_ref):
        @pl.when(jnp.logical_and(pl.program_id(0) == 0, pl.program_id(1) == 0))
        def _():
            loss_ref[...] = jnp.zeros_like(loss_ref)
            dfg_ref[...] = jnp.zeros_like(dfg_ref)

        y_a = _dot(ya_ref[...], wgp_ref[...])
        y_b = _dot(yb_ref[...], wmp_ref[...])
        ya_out[...] = _bf(y_a)
        yb_out[...] = _bf(y_b)
        merged = _sigmoid(gg_ref[...].astype(F32)) * y_a + _sigmoid(gm_ref[...].astype(F32)) * y_b
        h2 = h_ref[...] + _dot(_bf(merged), wo_ref[...])
        fg = fg_ref[...]
        xh, r = _rms_fwd(h2)
        pos = pl.program_id(1) * tm + lax.broadcasted_iota(jnp.int32, (tm, 1), 0)
        t = jnp.concatenate([ta_ref[0, tm - X0:, :], tb_ref[0, :tm - X0, :]], axis=0)
        err = jnp.where(pos >= X0, xh * fg - t, 0.0)
        loss_ref[...] += 0.5 * jnp.sum(jnp.mean(err * err, axis=-1, keepdims=True), axis=0, keepdims=True)
        dy = err * (1.0 / D_MODEL)
        dx, dfg = _rms_bwd(dy, xh, r, fg)
        dh_ref[...] = dx
        dfg_ref[...] += dfg

    tok = lambda c: pl.BlockSpec((tm, D_MODEL), lambda b, i: (b * nb + i, c))
    wspec = pl.BlockSpec((D_MODEL, D_MODEL), lambda b, i: (0, 0), pipeline_mode=pl.Buffered(1))
    return pl.pallas_call(
        body, name="mid_fwd", grid=(bsz, nb),
        in_specs=[tok(0), tok(0), tok(C_GG // D_MODEL), tok(C_GM // D_MODEL), tok(0),
                  pl.BlockSpec((1, tm, D_MODEL), lambda b, i: (b, jnp.maximum(i - 1, 0), 0)),
                  pl.BlockSpec((1, tm, D_MODEL), lambda b, i: (b, jnp.minimum(i, last), 0)),
                  wspec, wspec, wspec, pl.BlockSpec((1, D_MODEL), lambda b, i: (0, 0))],
        out_specs=[tok(0), tok(0), tok(0), pl.BlockSpec((1, LANE), lambda b, i: (0, 0)),
                   pl.BlockSpec((1, D_MODEL), lambda b, i: (0, 0))],
        out_shape=[jax.ShapeDtypeStruct((tp, D_MODEL), BF16), jax.ShapeDtypeStruct((tp, D_MODEL), BF16),
                   jax.ShapeDtypeStruct((tp, D_MODEL), F32), jax.ShapeDtypeStruct((1, LANE), F32),
                   jax.ShapeDtypeStruct((1, D_MODEL), F32)],
        compiler_params=_cp(("arbitrary", "arbitrary"), 56),
    )(ya_in, yb_in, proj, proj, hp, target, target, w_gp, w_mp, w_o, final_g)


def _mid_bwd(dh2, y_a, y_b, proj, ya_in, yb_in, o_b, w_o, w_gp, w_mp, bsz, lp):
    tp = bsz * lp
    tm = MXU_DEPTH if tp % MXU_DEPTH == 0 else _attn_block(lp)
    nsteps = tp // tm
    group = 3 * D_MODEL

    def body(dh_ref, ya_ref, yb_ref, mz_ref, gg_ref, gm_ref, yai_ref, ybi_ref, ob_ref, wo_ref, wgp_ref, wmp_ref,
             dyai_ref, do_ref, dp_ref, dl_ref, dwo_ref, dwgp_ref, dwmp_ref, a_o, a_gp, a_mp):
        @pl.when(pl.program_id(0) == 0)
        def _():
            for r in (a_o, a_gp, a_mp):
                r[...] = jnp.zeros_like(r)

        dh = _bf(dh_ref[...])
        dm = _dot_nt(dh, wo_ref[...])
        y_a, y_b = ya_ref[...].astype(F32), yb_ref[...].astype(F32)
        sg, sm = _sigmoid(gg_ref[...].astype(F32)), _sigmoid(gm_ref[...].astype(F32))
        d_ya, d_yb = _bf(sg * dm), _bf(sm * dm)
        dp_ref[:, D_MODEL:2 * D_MODEL] = _bf(dm * y_a * sg * (1.0 - sg))
        dp_ref[:, 2 * D_MODEL:] = _bf(dm * y_b * sm * (1.0 - sm))
        merged = _bf(sg * y_a + sm * y_b)
        dy = _dot_nt(d_yb, wmp_ref[...])
        dyai_ref[...] = _bf(_dot_nt(d_ya, wgp_ref[...]))
        a_o[...] += _dot_tn(merged, dh)
        a_gp[...] += _dot_tn(yai_ref[...], d_ya)
        a_mp[...] += _dot_tn(ybi_ref[...], d_yb)
        mz, o = mz_ref[...].astype(F32), ob_ref[...].astype(F32)
        s = _sigmoid(mz)
        do = _bf(dy * (mz * s))
        do_ref[...] = do
        dp_ref[:, :D_MODEL] = _bf(dy * o * (s * (1.0 + mz * (1.0 - s))))
        prod = do.astype(F32) * o
        for h in range(MLA_HEADS):
            dl = jnp.sum(prod[:, h * MLA_DV:(h + 1) * MLA_DV], axis=-1, keepdims=True)
            dl_ref[h] = jnp.broadcast_to(dl, (tm, LANE))

        @pl.when(pl.program_id(0) == nsteps - 1)
        def _():
            pltpu.sync_copy(a_o, dwo_ref)
            pltpu.sync_copy(a_gp, dwgp_ref)
            pltpu.sync_copy(a_mp, dwmp_ref)

    tok = lambda c: pl.BlockSpec((tm, D_MODEL), lambda i: (i, c))
    wspec = pl.BlockSpec((D_MODEL, D_MODEL), lambda i: (0, 0))
    anyspec = pl.BlockSpec(memory_space=pl.ANY)
    wshape = jax.ShapeDtypeStruct((D_MODEL, D_MODEL), F32)
    return pl.pallas_call(
        body, name="mid_bwd", grid=(nsteps,),
        in_specs=[tok(0), tok(0), tok(0), tok(C_MZ // D_MODEL), tok(C_GG // D_MODEL), tok(C_GM // D_MODEL),
                  tok(0), tok(0), tok(0), wspec, wspec, wspec],
        out_specs=[tok(0), tok(0), pl.BlockSpec((tm, group), lambda i: (i, C_MZ // group)),
                   pl.BlockSpec((MLA_HEADS, tm, LANE), lambda i: (0, i, 0)), anyspec, anyspec, anyspec],
        out_shape=[jax.ShapeDtypeStruct((tp, D_MODEL), BF16)] * 2 + [jax.ShapeDtypeStruct((tp, N_EXT), BF16),
                   jax.ShapeDtypeStruct((MLA_HEADS, tp, LANE), F32)] + [wshape] * 3,
        scratch_shapes=[pltpu.VMEM((D_MODEL, D_MODEL), F32)] * 3,
        compiler_params=_cp(("arbitrary",), 56),
    )(dh2, y_a, y_b, proj, proj, proj, ya_in, yb_in, o_b, w_o, w_gp, w_mp)


MESH_ID = pl.DeviceIdType.MESH
EXCHANGE_SEMS = [pltpu.SemaphoreType.DMA((N_DEV - 1,)), pltpu.SemaphoreType.DMA((N_DEV - 1,)), pltpu.SemaphoreType.DMA]


def _my_place():
    return lax.axis_index("x"), lax.axis_index("y"), lax.axis_index("c")


def _exchange(g_ref, recv_ref, send_sems, recv_sems, local_sem, start, same=False):
    x, y, c = _my_place()
    me = 4 * x + 2 * y + c
    own = pltpu.make_async_copy(g_ref if same else g_ref.at[me], recv_ref.at[me], local_sem)
    sends, lands = [], []
    for d in range(1, N_DEV):
        px = 1 - x if d & 4 else x
        py = 1 - y if d & 2 else y
        pc = 1 - c if d & 1 else c
        peer = 4 * px + 2 * py + pc
        for slot, group in ((me, sends),) if start else ((me, sends), (peer, lands)):
            group.append(pltpu.make_async_remote_copy(
                src_ref=g_ref if same else g_ref.at[peer], dst_ref=recv_ref.at[slot], send_sem=send_sems.at[d - 1],
                recv_sem=recv_sems.at[d - 1], device_id=(px, py, pc), device_id_type=MESH_ID))
    if start:
        own.start()
        for cp in sends:
            cp.start()
    else:
        for cp in lands:
            cp.wait_recv()
        for cp in sends:
            cp.wait_send()
        own.wait()


def _dw_in(u, dproj, slabs):
    tp = u.shape[0]
    tn = 3 * LANE
    nj = N_EXT // tn

    def body(u_ref, d_ref, g_ref, o_ref, recv_ref, send_sems, recv_sems, local_sem):
        j = pl.program_id(0)

        @pl.when(j == 0)
        def _():
            _exchange(g_ref, recv_ref, send_sems, recv_sems, local_sem, True)

        o_ref[...] = _bf(_dot_tn(d_ref[...], u_ref[...]))

        @pl.when(j == nj - 1)
        def _():
            _exchange(g_ref, recv_ref, send_sems, recv_sems, local_sem, False)

    anyspec = pl.BlockSpec(memory_space=pl.ANY)
    return pl.pallas_call(
        body, name="dw_in", grid=(nj,),
        in_specs=[pl.BlockSpec((tp, D_MODEL), lambda j: (0, 0), pipeline_mode=pl.Buffered(1)),
                  pl.BlockSpec((tp, tn), lambda j: (0, j)), anyspec],
        out_specs=[pl.BlockSpec((tn, D_MODEL), lambda j: (j, 0)), anyspec],
        out_shape=[jax.ShapeDtypeStruct((N_EXT, D_MODEL), BF16), jax.ShapeDtypeStruct(slabs.shape, slabs.dtype)],
        scratch_shapes=EXCHANGE_SEMS,
        compiler_params=_cp(("arbitrary",), 56),
    )(u, dproj, slabs)


def _dx_in(dproj, w_ext, hp, dh2, norm_g, slabs):
    tp = hp.shape[0]
    tm = 2 * TOK
    ni = tp // tm

    def body(d_ref, w_ref, h_ref, dh_ref, g_ref, s_ref, o_ref, dg_ref, recv_ref, send_sems, recv_sems, local_sem):
        i = pl.program_id(0)

        @pl.when(i == 0)
        def _():
            _exchange(s_ref, recv_ref, send_sems, recv_sems, local_sem, True)
            dg_ref[...] = jnp.zeros_like(dg_ref)

        du = _dot(d_ref[...], w_ref[...])
        g = g_ref[...]
        xh, r = _rms_fwd(h_ref[...])
        dx, dg = _rms_bwd(du, xh, r, g)
        o_ref[...] = dh_ref[...] + dx
        dg_ref[...] += dg

        @pl.when(i == ni - 1)
        def _():
            _exchange(s_ref, recv_ref, send_sems, recv_sems, local_sem, False)

    tok = pl.BlockSpec((tm, D_MODEL), lambda i: (i, 0))
    anyspec = pl.BlockSpec(memory_space=pl.ANY)
    return pl.pallas_call(
        body, name="dx_in", grid=(ni,),
        in_specs=[pl.BlockSpec((tm, N_EXT), lambda i: (i, 0)),
                  pl.BlockSpec((N_EXT, D_MODEL), lambda i: (0, 0), pipeline_mode=pl.Buffered(1)),
                  tok, tok, pl.BlockSpec((1, D_MODEL), lambda i: (0, 0)), anyspec],
        out_specs=[tok, pl.BlockSpec((1, D_MODEL), lambda i: (0, 0)), anyspec],
        out_shape=[jax.ShapeDtypeStruct((tp, D_MODEL), F32), jax.ShapeDtypeStruct((1, D_MODEL), F32),
                   jax.ShapeDtypeStruct(slabs.shape, slabs.dtype)],
        scratch_shapes=EXCHANGE_SEMS,
        compiler_params=_cp(("arbitrary",), 56),
    )(dproj, w_ext, hp, dh2, norm_g, slabs)


def _meta_grad(dhp3):
    bsz = dhp3.shape[0]

    def body(d_ref, o_ref):
        @pl.when(pl.program_id(0) == 0)
        def _():
            o_ref[...] = jnp.zeros_like(o_ref)

        o_ref[...] += d_ref[0]

    return pl.pallas_call(
        body, name="meta_grad", grid=(bsz,),
        in_specs=[pl.BlockSpec((1, N_META, D_MODEL), lambda b: (b, FRONT // N_META, 0))],
        out_specs=pl.BlockSpec((N_META, D_MODEL), lambda b: (0, 0)),
        out_shape=jax.ShapeDtypeStruct((N_META, D_MODEL), F32),
        compiler_params=_cp(("arbitrary",)),
    )(dhp3)


W_IN_SHARD = N_IN // N_DEV


def _pad_lanes(a, width=LANE):
    return jnp.pad(a, [(0, 0)] * (a.ndim - 1) + [(0, width - a.shape[-1])])


def _rot_cols(w):
    half = w.shape[-1] // 2
    return jnp.concatenate([-w[..., half:], w[..., :half]], axis=-1)


def _unrot_cols(dw):
    half = dw.shape[-1] // 2
    return jnp.concatenate([dw[..., half:], -dw[..., :half]], axis=-1)


def _w_in_rows(shards, lo, hi):
    parts = []
    for k in range(lo // W_IN_SHARD, (hi - 1) // W_IN_SHARD + 1):
        a, b = max(lo, k * W_IN_SHARD), min(hi, (k + 1) * W_IN_SHARD)
        parts.append(shards[k][a - k * W_IN_SHARD:b - k * W_IN_SHARD])
    return parts[0] if len(parts) == 1 else jnp.concatenate(parts, axis=0)


def _w_in_ext_t(shards):
    c = lambda lo, hi: _w_in_rows(shards, lo, hi)
    pad = lambda a: jnp.pad(a, ((0, LANE - a.shape[0]), (0, 0)))
    kr = c(O_KR, O_MZ)
    half = MLA_ROPE // 2
    return jnp.concatenate([
        c(O_V, O_LR), c(O_Z, O_CQ), c(O_Q, O_K), c(O_K, O_V), c(O_MZ, O_GG), c(O_GG, O_GM), c(O_GM, N_IN),
        c(O_CKV, O_KR), pad(kr), pad(jnp.concatenate([-kr[half:], kr[:half]], axis=0)), pad(c(O_LR, O_Z)),
        c(O_CQ, O_CKV)], axis=0)


def _w_in_slabs(dwt):
    half = MLA_ROPE // 2
    krot = dwt[C_KROT:C_KROT + MLA_ROPE]
    kr = dwt[C_KR:C_KR + MLA_ROPE] + jnp.concatenate([krot[half:], -krot[:half]], axis=0)
    groups = ((O_Q, GLA_KW, C_Q), (O_K, GLA_KW, C_K), (O_V, GLA_VW, C_V), (O_LR, GLA_RANK, C_LR), (O_Z, GLA_VW, C_Z),
              (O_CQ, MLA_QR, C_CQ), (O_CKV, MLA_KVR, C_CKV), (O_KR, MLA_ROPE, None), (O_MZ, D_MODEL, C_MZ),
              (O_GG, D_MODEL, C_GG), (O_GM, D_MODEL, C_GM))
    slabs = []
    for k in range(N_DEV):
        lo, hi = k * W_IN_SHARD, (k + 1) * W_IN_SHARD
        parts = []
        for first, width, row in groups:
            a, b = max(lo, first), min(hi, first + width)
            if a < b:
                parts.append(kr[a - first:b - first] if row is None else dwt[row + a - first:row + b - first])
        slabs.append(jnp.concatenate(parts, axis=0))
    return jnp.stack(slabs)


def _rope_tables(lp):
    inv = 1.0 / (ROPE_BASE ** (jnp.arange(0, MLA_ROPE, 2, dtype=F32) / MLA_ROPE))
    ang = (jnp.arange(lp, dtype=F32) - FRONT)[:, None] * inv[None, :]
    cos, sin = jnp.cos(ang), jnp.sin(ang)
    return _pad_lanes(jnp.concatenate([cos, cos], axis=1)), _pad_lanes(jnp.concatenate([sin, sin], axis=1))


def _local_step(x, loss_target, w):
    bsz, seq, _ = x.shape
    lp = X0 + seq
    tp = bsz * lp
    assert lp % TOK == 0 and (lp // GLA_CHUNK) % _gla_group(lp // GLA_CHUNK) == 0
    head = jnp.concatenate([jnp.zeros((FRONT, D_MODEL), F32), w["meta_tokens"]], axis=0)
    cos_t, sin_t = _rope_tables(lp)

    w_ext = _w_in_ext_t(w["w_in"])
    hp, u, proj, packed_all = _proj_in(x, head, w["norm_g"], w_ext, w["packed"])
    gathered = _unpack_shards(packed_all)
    for n, _, axis in PACKED:
        w[n] = _join8(gathered[n], axis)
    gw_pad = jnp.pad(w["gla_gate_w"], ((0, LANE - GLA_RANK), (0, 0)))
    uq = w["mla_w_uq"].reshape(MLA_QR, MLA_HEADS, MLA_QK)
    rope_w = uq[:, :, MLA_NOPE:]
    hw = MLA_HEADS * LANE
    wn = uq[:, :, :MLA_NOPE].reshape(MLA_QR, hw)
    wr = _pad_lanes(rope_w).reshape(MLA_QR, hw)
    wt = _pad_lanes(_rot_cols(rope_w)).reshape(MLA_QR, hw)
    ukv = w["mla_w_ukv"].reshape(MLA_KVR, MLA_HEADS, MLA_NOPE + MLA_DV)
    wk = ukv[:, :, :MLA_NOPE].reshape(MLA_KVR, hw)
    wv = ukv[:, :, MLA_NOPE:].reshape(MLA_KVR, hw)

    o_raw, ya_in, s_all = _gla_fwd(proj, gw_pad, w["gla_gate_b"], w["gla_norm_g"], bsz, lp)
    qf = _q_up(proj, w["mla_q_norm_g"], wn, wr, wt, cos_t, sin_t, bsz, lp)
    kf, vf = _kv_up(proj, w["mla_kv_norm_g"], wk, wv, cos_t, sin_t, bsz, lp)
    o_b, yb_in, lse = _attn_fwd(qf, kf, vf, proj, bsz, lp)
    y_a, y_b, dh2, loss, d_final_g = _mid_fwd(ya_in, yb_in, proj, hp, loss_target, w["gla_proj"], w["mla_proj"],
                                              w["w_out"], w["final_norm_g"], bsz, lp)
    d_ya, d_o, dproj, delta, d_w_out, d_gla_proj, d_mla_proj = _mid_bwd(
        dh2, y_a, y_b, proj, ya_in, yb_in, o_b, w["w_out"], w["gla_proj"], w["mla_proj"], bsz, lp)
    dproj, d_gate, d_gla_norm = _gla_bwd(proj, gw_pad, w["gla_gate_b"], w["gla_norm_g"], o_raw, s_all, d_ya, dproj,
                                         bsz, lp)
    d_lr, d_gw_pad, d_gate_b = _gate_bwd(d_gate, proj, gw_pad)
    dqf, dkf, dvf = _attn_bwd(qf, kf, vf, d_o, lse, delta, bsz, lp)
    dproj, d_wn, d_wr, d_wt, d_qn = _q_up_bwd(dqf, proj, w["mla_q_norm_g"], wn, wr, wt, cos_t, sin_t, dproj,
                                              bsz, lp)
    dproj, d_wk, d_wv, d_kvn = _kv_up_bwd(dkf, dvf, proj, w["mla_kv_norm_g"], wk, wv, cos_t, sin_t, d_lr, dproj,
                                          bsz, lp)

    d_rope = (d_wr.reshape(MLA_QR, MLA_HEADS, LANE)[:, :, :MLA_ROPE]
              + _unrot_cols(d_wt.reshape(MLA_QR, MLA_HEADS, LANE)[:, :, :MLA_ROPE]))
    d_uq = jnp.concatenate([d_wn.reshape(MLA_QR, MLA_HEADS, LANE), d_rope], axis=-1).reshape(MLA_QR, MLA_HEADS * MLA_QK)
    d_ukv = jnp.concatenate([d_wk.reshape(MLA_KVR, MLA_HEADS, LANE), d_wv.reshape(MLA_KVR, MLA_HEADS, LANE)],
                            axis=-1).reshape(MLA_KVR, MLA_HEADS * (MLA_NOPE + MLA_DV))
    mats = dict(gla_gate_w=d_gw_pad[:GLA_RANK], gla_proj=d_gla_proj, mla_w_uq=d_uq, mla_w_ukv=d_ukv,
                mla_proj=d_mla_proj, w_out=d_w_out)
    packed = _pack_shards({n: _bf(_split8(mats[n], axis)) for n, _, axis in PACKED})
    d_w_ext_t, packed_parts = _dw_in(u, dproj, packed)
    w_in_slabs = _w_in_slabs(d_w_ext_t)
    d_hp, d_norm_g, w_in_parts = _dx_in(dproj, w_ext, hp, dh2, w["norm_g"], w_in_slabs)
    d_hp3 = d_hp.reshape(bsz, lp, D_MODEL)
    small = dict(meta_tokens=_meta_grad(d_hp3), norm_g=d_norm_g, gla_gate_b=d_gate_b, gla_norm_g=d_gla_norm,
                 mla_q_norm_g=d_qn, mla_kv_norm_g=d_kvn, final_norm_g=d_final_g)
    return loss, d_hp3[:, X0:, :], w_in_parts, packed_parts, small


PACKED = (("gla_gate_w", (GLA_RANK, GLA_KW // N_DEV), 1),
          ("gla_proj", (D_MODEL // N_DEV, D_MODEL), 0), ("mla_w_uq", (MLA_QR, MLA_HEADS * MLA_QK // N_DEV), 1),
          ("mla_w_ukv", (MLA_KVR, MLA_HEADS * (MLA_NOPE + MLA_DV) // N_DEV), 1),
          ("mla_proj", (D_MODEL // N_DEV, D_MODEL), 0), ("w_out", (D_MODEL // N_DEV, D_MODEL), 0))
REPLICATED = (("norm_g", D_MODEL), ("gla_gate_b", GLA_KW), ("gla_norm_g", GLA_DV), ("mla_q_norm_g", MLA_QR),
              ("mla_kv_norm_g", MLA_KVR), ("final_norm_g", D_MODEL))
PACK_ROWS = 480
PACK_BLOCK = 160
SMALL_ROWS = 48
LOSS_ROW = N_META + 25
W_IN_BLOCK = 128


def _all_gather(shards):
    n_arr = len(shards)
    pieces = []
    for a, s in enumerate(shards):
        step = s.shape[0] // 4 if s.shape[0] % (4 * LANE) == 0 else s.shape[0]
        pieces += [(a, slice(r, r + step)) for r in range(0, s.shape[0], step)]
    n_pc = len(pieces)

    def body(*refs):
        x_refs, out_refs = refs[:n_arr], refs[n_arr:2 * n_arr]
        send_sems, recv_sems, local_sems = refs[2 * n_arr:]
        x, y, c = _my_place()
        me, sibling = (x, y, c), (x, y, 1 - c)
        chips = [(1 - x, y), (x, 1 - y), (1 - x, 1 - y)]

        def copy(u, k, block, to, from_input=False):
            a, rows = pieces[u]
            slab = out_refs[a].at[4 * block[0] + 2 * block[1] + block[2], rows]
            return pltpu.make_async_remote_copy(
                src_ref=x_refs[a].at[rows] if from_input else slab, dst_ref=slab,
                send_sem=send_sems.at[7 * u + k], recv_sem=recv_sems.at[7 * u + k], device_id=to,
                device_id_type=MESH_ID)

        arrays = range(n_pc)
        mine = [pltpu.make_async_copy(x_refs[a], out_refs[a].at[4 * x + 2 * y + c], local_sems.at[a])
                for a in range(n_arr)]
        for cp in mine:
            cp.start()
        first = [copy(a, 0, me, sibling, True) for a in arrays]
        first += [copy(a, 1 + j, me, (*chip, c), True) for j, chip in enumerate(chips) for a in arrays]
        for cp in first:
            cp.start()
        passed = []
        for j, chip in enumerate(chips):
            for a in arrays:
                copy(a, 1 + j, (*chip, c), me).wait_recv()
                passed.append(copy(a, 4 + j, (*chip, c), sibling))
                passed[-1].start()
        for a in arrays:
            copy(a, 0, sibling, me).wait_recv()
        for j, chip in enumerate(chips):
            for a in arrays:
                copy(a, 4 + j, (*chip, 1 - c), me).wait_recv()
        for cp in first + passed:
            cp.wait_send()
        for cp in mine:
            cp.wait()

    anyspec = pl.BlockSpec(memory_space=pl.ANY)
    return pl.pallas_call(
        body, name="weights_all_gather",
        out_shape=[jax.ShapeDtypeStruct((N_DEV,) + s.shape, s.dtype) for s in shards],
        in_specs=[anyspec] * n_arr, out_specs=[anyspec] * n_arr,
        scratch_shapes=[pltpu.SemaphoreType.DMA((7 * n_pc,)), pltpu.SemaphoreType.DMA((7 * n_pc,)),
                        pltpu.SemaphoreType.DMA((n_arr,))],
    )(*shards)


def _small_exchange(slabs):
    def body(g_ref, recv_ref, send_sems, recv_sems, local_sem):
        _exchange(g_ref, recv_ref, send_sems, recv_sems, local_sem, True)
        _exchange(g_ref, recv_ref, send_sems, recv_sems, local_sem, False)

    vmem = pl.BlockSpec(memory_space=pltpu.VMEM)
    return pl.pallas_call(
        body, name="small_exchange", out_shape=jax.ShapeDtypeStruct(slabs.shape, slabs.dtype),
        in_specs=[vmem], out_specs=vmem, scratch_shapes=EXCHANGE_SEMS,
    )(slabs)


def _adamw(parts, w, m, v, block_rows, name):
    rows, cols = w.shape

    def body(p_ref, w_ref, m_ref, v_ref, g_out, d_out, m_out, v_out):
        g = p_ref[0].astype(F32)
        for s in range(1, N_DEV):
            g = g + p_ref[s].astype(F32)
        m_new = ADAM_B1 * m_ref[...] + (1.0 - ADAM_B1) * g
        v_new = ADAM_B2 * v_ref[...] + (1.0 - ADAM_B2) * (g * g)
        m_hat = m_new / (1.0 - ADAM_B1 ** ADAM_STEP)
        v_hat = v_new / (1.0 - ADAM_B2 ** ADAM_STEP)
        g_out[...] = g
        d_out[...] = -ADAM_LR * (m_hat / (jnp.sqrt(v_hat) + ADAM_EPS) + ADAM_WD * w_ref[...])
        m_out[...] = m_new
        v_out[...] = v_new

    spec = pl.BlockSpec((block_rows, cols), lambda i: (i, 0))
    return pl.pallas_call(
        body, name=name, grid=(pl.cdiv(rows, block_rows),),
        in_specs=[pl.BlockSpec((N_DEV, block_rows, cols), lambda i: (0, i, 0)), spec, spec, spec],
        out_specs=[spec] * 4, out_shape=[jax.ShapeDtypeStruct((rows, cols), F32)] * 4,
        compiler_params=_cp(("parallel",), 48),
    )(parts, w, m, v)


def _pack_rows_of(shape):
    rows = shape[0] * shape[1] // D_MODEL
    return -(-rows // 16) * 16


def _pack_shards(shards):
    parts = []
    for n, shape, _ in PACKED:
        a = shards[n]
        lead = a.shape[:-2]
        if shape[1] != D_MODEL:
            a = a.reshape(lead + (shape[0] * shape[1] // D_MODEL, D_MODEL))
        pad = _pack_rows_of(shape) - a.shape[-2]
        parts.append(jnp.pad(a, [(0, 0)] * len(lead) + [(0, pad), (0, 0)]) if pad else a)
    return jnp.concatenate(parts, axis=-2)


def _unpack_shards(packed):
    lead, out, off = packed.shape[:-2], {}, 0
    for n, shape, _ in PACKED:
        rows = shape[0] * shape[1] // D_MODEL
        out[n] = packed[..., off:off + rows, :].reshape(lead + shape)
        off += _pack_rows_of(shape)
    return out


def _split8(full, axis):
    r, c = full.shape
    if axis == 0:
        return full.reshape(N_DEV, r // N_DEV, c)
    return full.reshape(r, N_DEV, c // N_DEV).transpose(1, 0, 2)


def _join8(shards, axis):
    _, r, c = shards.shape
    if axis == 0:
        return shards.reshape(N_DEV * r, c)
    return shards.transpose(1, 0, 2).reshape(r, N_DEV * c)


def _pack_small(meta_shard, vals, loss_row):
    rows = jnp.concatenate([vals[n].reshape(-1, LANE) for n, _ in REPLICATED] + [loss_row], axis=0)
    rows = jnp.pad(rows, ((0, SMALL_ROWS - N_META - rows.shape[0]), (0, 0)))
    return jnp.concatenate([meta_shard, jnp.broadcast_to(rows, meta_shard.shape[:-2] + rows.shape)], axis=-2)


def _unpack_small(packed):
    out, off = {"meta_tokens": packed[:N_META]}, N_META
    for n, size in REPLICATED:
        out[n] = packed[off:off + size // LANE].reshape(1, size)
        off += size // LANE
    return out


def kernel(x, meta_tokens, norm_g, w_in, gla_gate_w, gla_gate_b, gla_norm_g, gla_proj, mla_q_norm_g, mla_w_uq, mla_kv_norm_g, mla_w_ukv, mla_proj, w_out, final_norm_g, loss_target, m_meta_tokens, m_norm_g, m_w_in, m_gla_gate_w, m_gla_gate_b, m_gla_norm_g, m_gla_proj, m_mla_q_norm_g, m_mla_w_uq, m_mla_kv_norm_g, m_mla_w_ukv, m_mla_proj, m_w_out, m_final_norm_g, v_meta_tokens, v_norm_g, v_w_in, v_gla_gate_w, v_gla_gate_b, v_gla_norm_g, v_gla_proj, v_mla_q_norm_g, v_mla_w_uq, v_mla_kv_norm_g, v_mla_w_ukv, v_mla_proj, v_w_out, v_final_norm_g):
    given = dict(meta_tokens=meta_tokens, norm_g=norm_g, w_in=w_in, gla_gate_w=gla_gate_w, gla_gate_b=gla_gate_b,
                 gla_norm_g=gla_norm_g, gla_proj=gla_proj, mla_q_norm_g=mla_q_norm_g, mla_w_uq=mla_w_uq,
                 mla_kv_norm_g=mla_kv_norm_g, mla_w_ukv=mla_w_ukv, mla_proj=mla_proj, w_out=w_out,
                 final_norm_g=final_norm_g)
    mom_m = dict(meta_tokens=m_meta_tokens, norm_g=m_norm_g, w_in=m_w_in, gla_gate_w=m_gla_gate_w,
                 gla_gate_b=m_gla_gate_b, gla_norm_g=m_gla_norm_g, gla_proj=m_gla_proj, mla_q_norm_g=m_mla_q_norm_g,
                 mla_w_uq=m_mla_w_uq, mla_kv_norm_g=m_mla_kv_norm_g, mla_w_ukv=m_mla_w_ukv, mla_proj=m_mla_proj,
                 w_out=m_w_out, final_norm_g=m_final_norm_g)
    mom_v = dict(meta_tokens=v_meta_tokens, norm_g=v_norm_g, w_in=v_w_in, gla_gate_w=v_gla_gate_w,
                 gla_gate_b=v_gla_gate_b, gla_norm_g=v_gla_norm_g, gla_proj=v_gla_proj, mla_q_norm_g=v_mla_q_norm_g,
                 mla_w_uq=v_mla_w_uq, mla_kv_norm_g=v_mla_kv_norm_g, mla_w_ukv=v_mla_w_ukv, mla_proj=v_mla_proj,
                 w_out=v_w_out, final_norm_g=v_final_norm_g)
    shapes = {n: a.shape for n, a in given.items()}
    shard2d = {n: s for n, s, _ in PACKED}
    shard2d["w_in"] = (D_MODEL, W_IN_SHARD)
    shard2d["meta_tokens"] = (N_META, LANE)

    def as2d(tree):
        out = {n: tree[n].reshape(shard2d[n]) for n in shard2d}
        out.update({n: tree[n].reshape(1, size) for n, size in REPLICATED})
        return out

    w_loc, m_loc, v_loc = as2d(given), as2d(mom_m), as2d(mom_v)

    w_in_all, meta_all = _all_gather([w_loc["w_in"].T.astype(BF16), w_loc["meta_tokens"]])
    packed = _pack_shards({n: w_loc[n].astype(BF16) for n, _, _ in PACKED})
    full = {"w_in": w_in_all, "meta_tokens": _join8(meta_all, 1), "packed": packed}
    for n, _ in REPLICATED:
        full[n] = w_loc[n]

    loss_part, grad_x, w_in_parts, packed_parts, small = _local_step(x, loss_target, full)
    small_all = _small_exchange(_pack_small(_split8(small["meta_tokens"], 1), small,
                                            jnp.broadcast_to(loss_part[:, :1], (1, LANE))))

    w_in_t = [t["w_in"].T for t in (w_loc, m_loc, v_loc)]
    g_w, d_w, m_w, v_w = (o.T for o in _adamw(w_in_parts, *w_in_t, W_IN_BLOCK, "adamw_w_in"))
    g_p, d_p, m_p, v_p = _adamw(packed_parts, _pack_shards(w_loc), _pack_shards(m_loc), _pack_shards(v_loc),
                                PACK_BLOCK, "adamw_packed")
    zero_row = jnp.zeros((1, LANE), F32)
    g_s, d_s, m_s, v_s = _adamw(small_all, *(_pack_small(t["meta_tokens"], t, zero_row) for t in (w_loc, m_loc, v_loc)),
                                SMALL_ROWS, "adamw_small")
    loss = g_s[LOSS_ROW, 0]

    order = ["meta_tokens", "norm_g", "w_in", "gla_gate_w", "gla_gate_b", "gla_norm_g", "gla_proj", "mla_q_norm_g",
             "mla_w_uq", "mla_kv_norm_g", "mla_w_ukv", "mla_proj", "w_out", "final_norm_g"]
    result = [loss, grad_x]
    for w_in_out, packed_sh, packed_sm in ((g_w, g_p, g_s), (d_w, d_p, d_s), (m_w, m_p, m_s), (v_w, v_p, v_s)):
        tree = _unpack_shards(packed_sh)
        tree.update(_unpack_small(packed_sm))
        tree["w_in"] = w_in_out
        result += [tree[n].reshape(shapes[n]) for n in order]
    return tuple(result)
```

```python
import jax
import jax.numpy as jnp
from jax import lax
from jax.experimental import pallas as pl
from jax.experimental.pallas import tpu as pltpu

F32 = jnp.float32
BF16 = jnp.bfloat16

D_MODEL = 1024
N_META = 16
EPS = 1e-6
FRONT = 48
X0 = FRONT + N_META
GLA_HEADS, GLA_DK, GLA_DV, GLA_RANK, GLA_CHUNK = 4, 128, 256, 16, 64
GLA_GATE_NORMALIZER = 16.0
GLA_KW = GLA_HEADS * GLA_DK
GLA_VW = GLA_HEADS * GLA_DV
MLA_HEADS, MLA_NOPE, MLA_ROPE, MLA_DV, MLA_QR, MLA_KVR = 8, 128, 64, 128, 256, 128
MLA_QK = MLA_NOPE + MLA_ROPE
ROPE_BASE = 10000.0
LANE = 128
QKW = 2 * LANE

C_V, C_Z, C_Q, C_K = 0, 1024, 2048, 2560
C_MZ, C_GG, C_GM = 3072, 4096, 5120
C_CKV, C_KR, C_KROT, C_LR = 6144, 6272, 6400, 6528
C_CQ = 6656
N_EXT = 6912
O_Q, O_K, O_V, O_LR, O_Z, O_CQ, O_CKV, O_KR, O_MZ, O_GG, O_GM, N_IN = (
    0, 512, 1024, 2048, 2064, 3088, 3344, 3472, 3536, 4560, 5584, 6608)

ADAM_LR, ADAM_B1, ADAM_B2, ADAM_EPS, ADAM_WD, ADAM_STEP = 0.001, 0.9, 0.999, 1e-08, 0.01, 10

N_DEV = 8
TOK = 192
ATT_BLOCK = 352
MXU_DEPTH = 256


def _cp(sems=None, vmem_mb=None):
    kw = {}
    if sems is not None:
        kw["dimension_semantics"] = sems
    if vmem_mb is not None:
        kw["vmem_limit_bytes"] = vmem_mb * 1024 * 1024
    return pltpu.CompilerParams(**kw)


def _dot(a, b):
    return jnp.dot(a, b, preferred_element_type=F32)


def _dot_nt(a, b):
    return lax.dot_general(a, b, (((1,), (1,)), ((), ())), preferred_element_type=F32)


def _dot_tn(a, b):
    return lax.dot_general(a, b, (((0,), (0,)), ((), ())), preferred_element_type=F32)


def _sigmoid(x):
    return 1.0 / (1.0 + jnp.exp(-x))


def _bf(x):
    return x.astype(BF16)


def _big_tok(tp):
    return 4 * TOK if tp % (4 * TOK) == 0 else TOK


def _attn_block(lp):
    return ATT_BLOCK if lp % ATT_BLOCK == 0 else TOK


def _wide_block(lp):
    return 2 * ATT_BLOCK if lp % (2 * ATT_BLOCK) == 0 else _attn_block(lp)


def _proj_in(x, head, norm_g, w_ext, packed):
    bsz, seq, _ = x.shape
    lp = X0 + seq
    tp = bsz * lp
    tm = _attn_block(lp)
    nb = lp // tm
    last = pl.cdiv(seq, tm) - 1

    def body(xa_ref, xb_ref, hd_ref, g_ref, w_ref, p_ref, h_ref, u_ref, o_ref, pall_ref, send_sems, recv_sems, local_sem):
        first = jnp.logical_and(pl.program_id(0) == 0, pl.program_id(1) == 0)

        @pl.when(first)
        def _():
            _exchange(p_ref, pall_ref, send_sems, recv_sems, local_sem, True, same=True)

        front = jnp.where(pl.program_id(1) == 0, hd_ref[...], xa_ref[0, tm - X0:, :])
        h = jnp.concatenate([front, xb_ref[0, :tm - X0, :]], axis=0)
        h_ref[...] = h
        r = lax.rsqrt(jnp.mean(h * h, axis=-1, keepdims=True) + EPS)
        u = _bf(h * r * g_ref[...])
        u_ref[...] = u
        o_ref[...] = _bf(_dot(u, w_ref[...]))

        @pl.when(jnp.logical_and(pl.program_id(0) == bsz - 1, pl.program_id(1) == nb - 1))
        def _():
            _exchange(p_ref, pall_ref, send_sems, recv_sems, local_sem, False, same=True)

    anyspec = pl.BlockSpec(memory_space=pl.ANY)
    tok = lambda width: pl.BlockSpec((tm, width), lambda b, i: (b * nb + i, 0))
    return pl.pallas_call(
        body, name="proj_in", grid=(bsz, nb),
        in_specs=[pl.BlockSpec((1, tm, D_MODEL), lambda b, i: (b, jnp.maximum(i - 1, 0), 0)),
                  pl.BlockSpec((1, tm, D_MODEL), lambda b, i: (b, jnp.minimum(i, last), 0)),
                  pl.BlockSpec((X0, D_MODEL), lambda b, i: (0, 0)),
                  pl.BlockSpec((1, D_MODEL), lambda b, i: (0, 0)),
                  pl.BlockSpec((D_MODEL, N_EXT), lambda b, i: (0, 0), pipeline_mode=pl.Buffered(1)), anyspec],
        out_specs=[tok(D_MODEL), tok(D_MODEL), tok(N_EXT), anyspec],
        out_shape=[jax.ShapeDtypeStruct((tp, D_MODEL), F32), jax.ShapeDtypeStruct((tp, D_MODEL), BF16),
                   jax.ShapeDtypeStruct((tp, N_EXT), BF16),
                   jax.ShapeDtypeStruct((N_DEV,) + packed.shape, packed.dtype)],
        scratch_shapes=EXCHANGE_SEMS,
        compiler_params=_cp(("arbitrary", "arbitrary"), 56),
    )(x, x, head, norm_g, w_ext, packed)


def _gla_group(n_chunks):
    return 11 if n_chunks % 11 == 0 else 3


def _tri_dot(tri, x):
    hi = _bf(x)
    rest = x - hi.astype(F32)
    mid = _bf(rest)
    return _dot(tri, hi) + _dot(tri, mid) + _dot(tri, _bf(rest - mid.astype(F32)))


def _gla_gates(q_ref, k_ref, lr_ref, gw_ref, gb_ref, rows, not_first):
    z = _dot(lr_ref[rows, :], gw_ref[...]) + gb_ref[...]
    logsig = jnp.minimum(z, 0.0) - jnp.log(1.0 + jnp.exp(-jnp.abs(z)))
    row = lax.broadcasted_iota(jnp.int32, (GLA_CHUNK, GLA_KW), 0)
    live = jnp.logical_or(not_first, row >= FRONT)
    g = jnp.where(live, logsig * (1.0 / GLA_GATE_NORMALIZER), 0.0)
    ri = lax.broadcasted_iota(jnp.int32, (GLA_CHUNK, GLA_CHUNK), 0)
    ci = lax.broadcasted_iota(jnp.int32, (GLA_CHUNK, GLA_CHUNK), 1)
    tril = ci <= ri
    b = _tri_dot(_bf(tril.astype(F32)), g)
    bl = jnp.sum(jnp.where(row == GLA_CHUNK - 1, b, 0.0), axis=0, keepdims=True)
    eb, enb, elb, ebl = jnp.exp(b), jnp.exp(-b), jnp.exp(bl - b), jnp.exp(bl)
    q = q_ref[rows, :].astype(F32) * (GLA_DK ** -0.5)
    k = k_ref[rows, :].astype(F32)
    qe, ke, kl = q * eb, k * enb, k * elb
    return dict(z=z, live=live, tril=tril, row=row, eb=eb, enb=enb, elb=elb, ebl=ebl, qe=qe, ke=ke, kl=kl,
                qe_b=_bf(qe), ke_b=_bf(ke), kl_b=_bf(kl))


def _gla_in_specs(n_groups, gla_rows, rev):
    def rb(b, n):
        return b * n_groups + ((n_groups - 1 - n) if rev else n)

    return rb, [pl.BlockSpec((gla_rows, GLA_KW), lambda b, n: (rb(b, n), C_Q // GLA_KW)),
                pl.BlockSpec((gla_rows, GLA_KW), lambda b, n: (rb(b, n), C_K // GLA_KW)),
                pl.BlockSpec((gla_rows, GLA_VW), lambda b, n: (rb(b, n), C_V // GLA_VW)),
                pl.BlockSpec((gla_rows, GLA_VW), lambda b, n: (rb(b, n), C_Z // GLA_VW)),
                pl.BlockSpec((gla_rows, LANE), lambda b, n: (rb(b, n), C_LR // LANE)),
                pl.BlockSpec((LANE, GLA_KW), lambda b, n: (0, 0)),
                pl.BlockSpec((1, GLA_KW), lambda b, n: (0, 0)),
                pl.BlockSpec((1, GLA_DV), lambda b, n: (0, 0))]


def _gla_fwd(proj, gw_pad, gate_b, gla_norm_g, bsz, lp):
    n_chunks = lp // GLA_CHUNK
    gla_group = _gla_group(n_chunks)
    gla_rows = gla_group * GLA_CHUNK
    n_groups = n_chunks // gla_group
    tp = bsz * lp

    def body(q_ref, k_ref, v_ref, z_ref, lr_ref, gw_ref, gb_ref, gn_ref, oraw_ref, ya_ref, sall_ref, st_scr):
        grp = pl.program_id(1)

        @pl.when(grp == 0)
        def _():
            st_scr[...] = jnp.zeros_like(st_scr)

        chunks = [slice(j * GLA_CHUNK, (j + 1) * GLA_CHUNK) for j in range(gla_group)]
        cs = [_gla_gates(q_ref, k_ref, lr_ref, gw_ref, gb_ref, rows, True if j else grp > 0)
              for j, rows in enumerate(chunks)]
        gn = gn_ref[...]
        sts = [st_scr[h] for h in range(GLA_HEADS)]
        heads = [(slice(h * GLA_DK, (h + 1) * GLA_DK), slice(h * GLA_DV, (h + 1) * GLA_DV)) for h in range(GLA_HEADS)]
        a_all = [[_bf(jnp.where(c["tril"], _dot_nt(c["qe_b"][:, ks], c["ke_b"][:, ks]), 0.0)) for ks, _ in heads]
                 for c in cs]
        u_all = [[_dot_tn(v_ref[rows, vs], c["kl_b"][:, ks]) for ks, vs in heads] for rows, c in zip(chunks, cs)]
        for j, (rows, c) in enumerate(zip(chunks, cs)):
            for h, (ks, vs) in enumerate(heads):
                st = sts[h]
                sall_ref[0, j, h] = st
                o = _dot(a_all[j][h], v_ref[rows, vs]) + _dot_nt(c["qe_b"][:, ks], _bf(st))
                sts[h] = st * c["ebl"][:, ks] + u_all[j][h]
                oraw_ref[rows, vs] = o
                r = lax.rsqrt(jnp.mean(o * o, axis=-1, keepdims=True) + EPS)
                zg = z_ref[rows, vs].astype(F32)
                ya_ref[rows, vs] = _bf((o * r * gn) * (zg * _sigmoid(zg)))
        for h in range(GLA_HEADS):
            st_scr[h] = sts[h]

    rb, in_specs = _gla_in_specs(n_groups, gla_rows, False)
    return pl.pallas_call(
        body, name="gla_fwd", grid=(bsz, n_groups), in_specs=in_specs,
        out_specs=[pl.BlockSpec((gla_rows, GLA_VW), lambda b, n: (rb(b, n), 0)),
                   pl.BlockSpec((gla_rows, GLA_VW), lambda b, n: (rb(b, n), 0)),
                   pl.BlockSpec((1, gla_group, GLA_HEADS, GLA_DV, GLA_DK), lambda b, n: (b, n, 0, 0, 0))],
        out_shape=[jax.ShapeDtypeStruct((tp, GLA_VW), F32), jax.ShapeDtypeStruct((tp, GLA_VW), BF16),
                   jax.ShapeDtypeStruct((bsz, n_chunks, GLA_HEADS, GLA_DV, GLA_DK), F32)],
        scratch_shapes=[pltpu.VMEM((GLA_HEADS, GLA_DV, GLA_DK), F32)],
        compiler_params=_cp(("parallel", "arbitrary"), 56),
    )(proj, proj, proj, proj, proj, gw_pad, gate_b, gla_norm_g)


def _gla_bwd(proj, gw_pad, gate_b, gla_norm_g, o_raw, s_all, d_ya, dproj, bsz, lp):
    n_chunks = lp // GLA_CHUNK
    gla_group = _gla_group(n_chunks)
    gla_rows = gla_group * GLA_CHUNK
    n_groups = n_chunks // gla_group
    tp = bsz * lp

    def body(q_ref, k_ref, v_ref, z_ref, lr_ref, gw_ref, gb_ref, gn_ref, o_ref, s_ref, dya_ref, _,
             dp_ref, dz_ref, dgn_ref, dst_scr):
        dv_ref, dzg_ref = dp_ref.at[:, C_V:C_V + GLA_VW], dp_ref.at[:, C_Z:C_Z + GLA_VW]

        @pl.when(jnp.logical_and(pl.program_id(0) == 0, pl.program_id(1) == 0))
        def _():
            dgn_ref[...] = jnp.zeros_like(dgn_ref)

        @pl.when(pl.program_id(1) == 0)
        def _():
            dst_scr[...] = jnp.zeros_like(dst_scr)

        grp = n_groups - 1 - pl.program_id(1)
        chunks = [slice(j * GLA_CHUNK, (j + 1) * GLA_CHUNK) for j in range(gla_group)]
        cs = [_gla_gates(q_ref, k_ref, lr_ref, gw_ref, gb_ref, rows, True if j else grp > 0)
              for j, rows in enumerate(chunks)]
        gn = gn_ref[...]
        dgn = jnp.zeros((1, GLA_DV), F32)
        dqe_h, dke_h, dkl_h, dbl_h = ([[None] * GLA_HEADS for _ in chunks] for _ in range(4))
        dsts = [dst_scr[h] for h in range(GLA_HEADS)]
        for j in reversed(range(gla_group)):
            rows, c = chunks[j], cs[j]
            for h in range(GLA_HEADS):
                ks, vs = slice(h * GLA_DK, (h + 1) * GLA_DK), slice(h * GLA_DV, (h + 1) * GLA_DV)
                dst = dsts[h]
                v = v_ref[rows, vs]
                st = s_ref[0, j, h]
                o = o_ref[rows, vs]
                r = lax.rsqrt(jnp.mean(o * o, axis=-1, keepdims=True) + EPS)
                xh = o * r
                zg = z_ref[rows, vs].astype(F32)
                sg = _sigmoid(zg)
                dy = dya_ref[rows, vs].astype(F32)
                dzg_ref[rows, vs] = _bf(dy * (xh * gn) * (sg * (1.0 + zg * (1.0 - sg))))
                t = dy * (zg * sg)
                dgn += jnp.sum(t * xh, axis=0, keepdims=True)
                dxh = t * gn
                do_b = _bf(r * (dxh - xh * jnp.mean(dxh * xh, axis=-1, keepdims=True)))
                qe_b, ke_b, kl_b, dst_b = c["qe_b"][:, ks], c["ke_b"][:, ks], c["kl_b"][:, ks], _bf(dst)
                a = jnp.where(c["tril"], _dot_nt(qe_b, ke_b), 0.0)
                da_b = _bf(jnp.where(c["tril"], _dot_nt(do_b, v), 0.0))
                dqe_h[j][h] = _dot(da_b, ke_b) + _dot(do_b, _bf(st))
                dke_h[j][h] = _dot_tn(da_b, qe_b)
                dkl = _dot(v, dst_b)
                dkl_h[j][h] = dkl
                dv_ref[rows, vs] = _bf(_dot_tn(_bf(a), do_b) + _dot_nt(kl_b, dst_b))
                ddecay = jnp.sum(dst * st, axis=0, keepdims=True)
                dbl_h[j][h] = jnp.sum(dkl * c["kl"][:, ks], axis=0, keepdims=True) + ddecay * c["ebl"][:, ks]
                dsts[h] = dst * c["ebl"][:, ks] + _dot_tn(do_b, qe_b)
        for h in range(GLA_HEADS):
            dst_scr[h] = dsts[h]
        dgn_ref[...] += dgn
        ri = lax.broadcasted_iota(jnp.int32, (GLA_CHUNK, GLA_CHUNK), 0)
        ci = lax.broadcasted_iota(jnp.int32, (GLA_CHUNK, GLA_CHUNK), 1)
        triu = _bf((ci >= ri).astype(F32))
        for j, (rows, c) in enumerate(zip(chunks, cs)):
            dqe, dke, dkl, dbl = (jnp.concatenate(p[j], axis=1) for p in (dqe_h, dke_h, dkl_h, dbl_h))
            db = dqe * c["qe"] - dke * c["ke"] - dkl * c["kl"] + jnp.where(c["row"] == GLA_CHUNK - 1, dbl, 0.0)
            dg = _tri_dot(triu, db)
            dg = jnp.where(c["live"], dg, 0.0)
            dz_ref[rows, :] = dg * (1.0 / GLA_GATE_NORMALIZER) * _sigmoid(-c["z"])
            dp_ref[rows, C_Q:C_Q + GLA_KW] = _bf(dqe * c["eb"] * (GLA_DK ** -0.5))
            dp_ref[rows, C_K:C_K + GLA_KW] = _bf(dke * c["enb"] + dkl * c["elb"])

    rb, in_specs = _gla_in_specs(n_groups, gla_rows, True)
    wide = pl.BlockSpec((gla_rows, GLA_VW), lambda b, n: (rb(b, n), 0))
    group = C_MZ
    return pl.pallas_call(
        body, name="gla_bwd", grid=(bsz, n_groups),
        in_specs=in_specs + [wide, pl.BlockSpec((1, gla_group, GLA_HEADS, GLA_DV, GLA_DK),
                                                lambda b, n: (b, n_groups - 1 - n, 0, 0, 0)), wide,
                             pl.BlockSpec(memory_space=pl.ANY)],
        out_specs=[pl.BlockSpec((gla_rows, group), lambda b, n: (rb(b, n), 0)),
                   pl.BlockSpec((gla_rows, GLA_KW), lambda b, n: (rb(b, n), 0)),
                   pl.BlockSpec((1, GLA_DV), lambda b, n: (0, 0))],
        out_shape=[jax.ShapeDtypeStruct((tp, N_EXT), BF16), jax.ShapeDtypeStruct((tp, GLA_KW), F32),
                   jax.ShapeDtypeStruct((1, GLA_DV), F32)],
        input_output_aliases={11: 0},
        scratch_shapes=[pltpu.VMEM((GLA_HEADS, GLA_DV, GLA_DK), F32)],
        compiler_params=_cp(("arbitrary", "arbitrary"), 56),
    )(proj, proj, proj, proj, proj, gw_pad, gate_b, gla_norm_g, o_raw, s_all, d_ya, dproj)


def _gate_bwd(dz, proj, gw_pad):
    tp = dz.shape[0]
    tm = _big_tok(tp)

    def body(dz_ref, lr_ref, gw_ref, dlr_ref, dgw_ref, dgb_ref):
        @pl.when(pl.program_id(0) == 0)
        def _():
            dgw_ref[...] = jnp.zeros_like(dgw_ref)
            dgb_ref[...] = jnp.zeros_like(dgb_ref)

        dz = dz_ref[...]
        dz_b = _bf(dz)
        dlr_ref[...] = _bf(_dot_nt(dz_b, gw_ref[...]))
        dgw_ref[...] += _dot_tn(lr_ref[...], dz_b)
        dgb_ref[...] += jnp.sum(dz, axis=0, keepdims=True)

    return pl.pallas_call(
        body, name="gate_bwd", grid=(tp // tm,),
        in_specs=[pl.BlockSpec((tm, GLA_KW), lambda i: (i, 0)),
                  pl.BlockSpec((tm, LANE), lambda i: (i, C_LR // LANE)),
                  pl.BlockSpec((LANE, GLA_KW), lambda i: (0, 0))],
        out_specs=[pl.BlockSpec((tm, LANE), lambda i: (i, 0)),
                   pl.BlockSpec((LANE, GLA_KW), lambda i: (0, 0)),
                   pl.BlockSpec((1, GLA_KW), lambda i: (0, 0))],
        out_shape=[jax.ShapeDtypeStruct((tp, LANE), BF16), jax.ShapeDtypeStruct((LANE, GLA_KW), F32),
                   jax.ShapeDtypeStruct((1, GLA_KW), F32)],
        compiler_params=_cp(("arbitrary",)),
    )(dz, proj, gw_pad)


def _rms_fwd(x):
    r = lax.rsqrt(jnp.mean(x * x, axis=-1, keepdims=True) + EPS)
    return x * r, r


def _rms_bwd(dy, xh, r, g):
    dxh = dy * g
    dx = r * (dxh - xh * jnp.mean(dxh * xh, axis=-1, keepdims=True))
    return dx, jnp.sum(dy * xh, axis=0, keepdims=True)


def _q_up(proj, q_norm_g, wn, wr, wt, cos_t, sin_t, bsz, lp):
    tp = bsz * lp
    tok = _wide_block(lp)
    nb = lp // tok

    def body(cq_ref, g_ref, wn_ref, wr_ref, wt_ref, cos_ref, sin_ref, q_ref):
        xh, _ = _rms_fwd(cq_ref[...].astype(F32))
        cqn = _bf(xh * g_ref[...])
        nope = _dot(cqn, wn_ref[...])
        rope = _dot(cqn, wr_ref[...])
        rot = _dot(cqn, wt_ref[...])
        cos, sin = cos_ref[...], sin_ref[...]
        one = (lax.broadcasted_iota(jnp.int32, (tok, LANE), 1) == BIAS_LANE).astype(F32)
        for h in range(MLA_HEADS):
            sl = slice(h * LANE, (h + 1) * LANE)
            q_ref[:, h * QKW:h * QKW + LANE] = _bf(nope[:, sl])
            q_ref[:, h * QKW + LANE:(h + 1) * QKW] = _bf(rope[:, sl] * cos + rot[:, sl] * sin + one)

    wspec = pl.BlockSpec((MLA_QR, MLA_HEADS * LANE), lambda b, i: (0, 0))
    tspec = pl.BlockSpec((tok, LANE), lambda b, i: (i, 0))
    return pl.pallas_call(
        body, name="mla_q_up", grid=(bsz, nb),
        in_specs=[pl.BlockSpec((tok, MLA_QR), lambda b, i: (b * nb + i, C_CQ // MLA_QR)),
                  pl.BlockSpec((1, MLA_QR), lambda b, i: (0, 0)), wspec, wspec, wspec, tspec, tspec],
        out_specs=pl.BlockSpec((tok, MLA_HEADS * QKW), lambda b, i: (b * nb + i, 0)),
        out_shape=jax.ShapeDtypeStruct((tp, MLA_HEADS * QKW), BF16),
        compiler_params=_cp(("parallel", "parallel")),
    )(proj, q_norm_g, wn, wr, wt, cos_t, sin_t)


def _kv_up(proj, kv_norm_g, wk, wv, cos_t, sin_t, bsz, lp):
    tp = bsz * lp
    tok = _wide_block(lp)
    nb = lp // tok

    def body(ckv_ref, kr_ref, krot_ref, g_ref, wk_ref, wv_ref, cos_ref, sin_ref, k_ref, v_ref):
        xh, _ = _rms_fwd(ckv_ref[...].astype(F32))
        cn = _bf(xh * g_ref[...])
        kn = _dot(cn, wk_ref[...])
        v_ref[...] = _bf(_dot(cn, wv_ref[...]))
        pos = pl.program_id(1) * tok + lax.broadcasted_iota(jnp.int32, (tok, LANE), 0)
        lane = lax.broadcasted_iota(jnp.int32, (tok, LANE), 1)
        bias = jnp.where(jnp.logical_and(lane == BIAS_LANE, pos < FRONT), KEY_BIAS, 0.0)
        kr = _bf(kr_ref[...].astype(F32) * cos_ref[...] + krot_ref[...].astype(F32) * sin_ref[...] + bias)
        for h in range(MLA_HEADS):
            k_ref[:, h * QKW:h * QKW + LANE] = _bf(kn[:, h * LANE:(h + 1) * LANE])
            k_ref[:, h * QKW + LANE:(h + 1) * QKW] = kr

    wspec = pl.BlockSpec((MLA_KVR, MLA_HEADS * LANE), lambda b, i: (0, 0))
    tspec = pl.BlockSpec((tok, LANE), lambda b, i: (i, 0))
    return pl.pallas_call(
        body, name="mla_kv_up", grid=(bsz, nb),
        in_specs=[pl.BlockSpec((tok, LANE), lambda b, i: (b * nb + i, C_CKV // LANE)),
                  pl.BlockSpec((tok, LANE), lambda b, i: (b * nb + i, C_KR // LANE)),
                  pl.BlockSpec((tok, LANE), lambda b, i: (b * nb + i, C_KROT // LANE)),
                  pl.BlockSpec((1, MLA_KVR), lambda b, i: (0, 0)), wspec, wspec, tspec, tspec],
        out_specs=[pl.BlockSpec((tok, MLA_HEADS * QKW), lambda b, i: (b * nb + i, 0)),
                   pl.BlockSpec((tok, MLA_HEADS * LANE), lambda b, i: (b * nb + i, 0))],
        out_shape=[jax.ShapeDtypeStruct((tp, MLA_HEADS * QKW), BF16),
                   jax.ShapeDtypeStruct((tp, MLA_HEADS * LANE), BF16)],
        compiler_params=_cp(("parallel", "parallel")),
    )(proj, proj, proj, kv_norm_g, wk, wv, cos_t, sin_t)


ATT_SCALE = MLA_QK ** -0.5


KEY_BIAS = -1e30
BIAS_LANE = MLA_ROPE
NEG = 2 * KEY_BIAS
LOG2E = 1.4426950408889634
EXP2_SCALE = ATT_SCALE * LOG2E


def _causal_fill(s, r0, fill):
    tq, kmax = s.shape
    a = r0 // LANE * LANE
    mask = (a + lax.broadcasted_iota(jnp.int32, (tq, kmax - a), 1)
            <= r0 + lax.broadcasted_iota(jnp.int32, (tq, kmax - a), 0))
    right = jnp.where(mask, s[:, a:], fill)
    return jnp.concatenate([s[:, :a], right], axis=1) if a else right


def _attn_fwd(qf, kf, vf, proj, bsz, lp):
    tp = bsz * lp
    tq = _attn_block(lp)
    nh = 2

    def body(q_ref, k_ref, v_ref, mz_ref, ob_ref, yb_ref, lse_ref):
        starts = list(range(0, lp, tq))
        for pair in (starts[i:i + 2] for i in range(0, len(starts), 2)):
            work = [(r0, h) for r0 in pair for h in range(nh)]
            ss = [_causal_fill(_dot_nt(q_ref[r0:r0 + tq, h * QKW:(h + 1) * QKW],
                                       k_ref[0:r0 + tq, h * QKW:(h + 1) * QKW]), r0, NEG) for r0, h in work]
            ms = [jnp.max(s, axis=-1, keepdims=True) for s in ss]
            ps = [jnp.exp2((s - m) * EXP2_SCALE) for s, m in zip(ss, ms)]
            ls = [jnp.sum(p, axis=-1, keepdims=True) for p in ps]
            for (r0, h), p, m, l in zip(work, ps, ms, ls):
                rows, cols = slice(r0, r0 + tq), slice(h * MLA_DV, (h + 1) * MLA_DV)
                o = _dot(_bf(p), v_ref[0:r0 + tq, cols]) / l
                ob_ref[rows, cols] = _bf(o)
                mz = mz_ref[rows, cols].astype(F32)
                yb_ref[rows, cols] = _bf(o * (mz * _sigmoid(mz)))
                lse_ref[0, h, rows, :] = jnp.broadcast_to(m * EXP2_SCALE + jnp.log2(l), (tq, LANE))

    head = lambda off: pl.BlockSpec((lp, nh * MLA_DV), lambda b, h: (b, off + h))
    wide = pl.BlockSpec((lp, nh * QKW), lambda b, h: (b, h))
    return pl.pallas_call(
        body, name="mla_attn_fwd", grid=(bsz, MLA_HEADS // nh),
        in_specs=[wide, wide, head(0), head(C_MZ // (nh * MLA_DV))],
        out_specs=[head(0), head(0), pl.BlockSpec((1, nh, lp, LANE), lambda b, h: (b, h, 0, 0))],
        out_shape=[jax.ShapeDtypeStruct((tp, MLA_HEADS * MLA_DV), BF16),
                   jax.ShapeDtypeStruct((tp, MLA_HEADS * MLA_DV), BF16),
                   jax.ShapeDtypeStruct((bsz, MLA_HEADS, lp, LANE), F32)],
        compiler_params=_cp(("parallel", "parallel"), 56),
    )(qf, kf, vf, proj)


def _attn_bwd_blocks(lp):
    return [(0, X0)] + [(r0, min(MXU_DEPTH, lp - r0)) for r0 in range(X0, lp, MXU_DEPTH)]


def _attn_bwd(qf, kf, vf, d_o, lse, delta, bsz, lp):
    tp = bsz * lp

    def body(q_ref, k_ref, v_ref, do_ref, lse_ref, dl_ref, dq_ref, dk_ref, dv_ref, dk_acc, dv_acc):
        dk_acc[...] = jnp.zeros_like(dk_acc)
        dv_acc[...] = jnp.zeros_like(dv_acc)
        for r0, tq in _attn_bwd_blocks(lp):
            rows, kmax = slice(r0, r0 + tq), r0 + tq
            q, do = q_ref[rows, :], do_ref[rows, :]
            k, v = k_ref[0:kmax, :], v_ref[0:kmax, :]
            p = jnp.exp2(_dot_nt(q, k) * EXP2_SCALE - lse_ref[0, 0, rows, :][:, :1])
            p = _causal_fill(p, r0, 0.0)
            ds = _bf(p * (_dot_nt(do, v) - dl_ref[0, rows, :][:, :1]))
            dq_ref[rows, :] = _bf(_dot(ds, k) * ATT_SCALE)
            dk_acc[0:kmax, :] += _dot_tn(ds, q)
            dv_acc[0:kmax, :] += _dot_tn(_bf(p), do)
        dk_ref[...] = _bf(dk_acc[...] * ATT_SCALE)
        dv_ref[...] = _bf(dv_acc[...])

    wide = pl.BlockSpec((lp, QKW), lambda b, h: (b, h))
    narrow = pl.BlockSpec((lp, MLA_DV), lambda b, h: (b, h))
    stat = pl.BlockSpec((1, 1, lp, LANE), lambda b, h: (b, h, 0, 0))
    return pl.pallas_call(
        body, name="mla_attn_bwd", grid=(bsz, MLA_HEADS),
        in_specs=[wide, wide, narrow, narrow, stat, pl.BlockSpec((1, lp, LANE), lambda b, h: (h, b, 0))],
        out_specs=[wide, wide, narrow],
        out_shape=[jax.ShapeDtypeStruct((tp, MLA_HEADS * QKW), BF16), jax.ShapeDtypeStruct((tp, MLA_HEADS * QKW), BF16),
                   jax.ShapeDtypeStruct((tp, MLA_HEADS * MLA_DV), BF16)],
        scratch_shapes=[pltpu.VMEM((lp, QKW), F32), pltpu.VMEM((lp, MLA_DV), F32)],
        compiler_params=_cp(("parallel", "parallel"), 56),
    )(qf, kf, vf, d_o, lse, delta)


def _q_up_bwd(dqf, proj, q_norm_g, wn, wr, wt, cos_t, sin_t, dproj, bsz, lp):
    tp = bsz * lp
    tok = _wide_block(lp)
    nb = lp // tok
    hw = MLA_HEADS * LANE

    def body(dq_ref, cq_ref, g_ref, wn_ref, wr_ref, wt_ref, cos_ref, sin_ref, _,
             dcq_ref, dwn_ref, dwr_ref, dwt_ref, dg_ref):
        @pl.when(jnp.logical_and(pl.program_id(0) == 0, pl.program_id(1) == 0))
        def _():
            for r in (dwn_ref, dwr_ref, dwt_ref, dg_ref):
                r[...] = jnp.zeros_like(r)

        g = g_ref[...]
        xh, r = _rms_fwd(cq_ref[...].astype(F32))
        cqn = _bf(xh * g)
        dn = jnp.concatenate([dq_ref[:, h * QKW:h * QKW + LANE] for h in range(MLA_HEADS)], axis=1)
        dr = jnp.concatenate([dq_ref[:, h * QKW + LANE:(h + 1) * QKW] for h in range(MLA_HEADS)], axis=1).astype(F32)
        dr_c = _bf(dr * jnp.tile(cos_ref[...], (1, MLA_HEADS)))
        dr_s = _bf(dr * jnp.tile(sin_ref[...], (1, MLA_HEADS)))
        dcqn = _dot_nt(dn, wn_ref[...]) + _dot_nt(dr_c, wr_ref[...]) + _dot_nt(dr_s, wt_ref[...])
        dwn_ref[...] += _dot_tn(cqn, dn)
        dwr_ref[...] += _dot_tn(cqn, dr_c)
        dwt_ref[...] += _dot_tn(cqn, dr_s)
        dx, dg = _rms_bwd(dcqn, xh, r, g)
        dcq_ref[...] = _bf(dx)
        dg_ref[...] += dg

    aspec = pl.BlockSpec((MLA_QR, hw), lambda b, i: (0, 0))
    tspec = pl.BlockSpec((tok, LANE), lambda b, i: (i, 0))
    return pl.pallas_call(
        body, name="mla_q_up_bwd", grid=(bsz, nb),
        in_specs=[pl.BlockSpec((tok, MLA_HEADS * QKW), lambda b, i: (b * nb + i, 0)),
                  pl.BlockSpec((tok, MLA_QR), lambda b, i: (b * nb + i, C_CQ // MLA_QR)),
                  pl.BlockSpec((1, MLA_QR), lambda b, i: (0, 0)), aspec, aspec, aspec, tspec, tspec,
                  pl.BlockSpec(memory_space=pl.ANY)],
        out_specs=[pl.BlockSpec((tok, MLA_QR), lambda b, i: (b * nb + i, C_CQ // MLA_QR)), aspec, aspec, aspec,
                   pl.BlockSpec((1, MLA_QR), lambda b, i: (0, 0))],
        out_shape=[jax.ShapeDtypeStruct((tp, N_EXT), BF16)] + [jax.ShapeDtypeStruct((MLA_QR, hw), F32)] * 3
        + [jax.ShapeDtypeStruct((1, MLA_QR), F32)],
        input_output_aliases={8: 0},
        compiler_params=_cp(("arbitrary", "arbitrary")),
    )(dqf, proj, q_norm_g, wn, wr, wt, cos_t, sin_t, dproj)


def _kv_up_bwd(dkf, dvf, proj, kv_norm_g, wk, wv, cos_t, sin_t, d_lr, dproj, bsz, lp):
    tp = bsz * lp
    tok = _wide_block(lp)
    nb = lp // tok
    hw = MLA_HEADS * LANE

    def body(dk_ref, dv_ref, ckv_ref, g_ref, wk_ref, wv_ref, cos_ref, sin_ref, dlr_ref, _,
             dp_ref, dwk_ref, dwv_ref, dg_ref):
        dckv_ref, dkr_ref, dkrot_ref = (dp_ref.at[:, j * LANE:(j + 1) * LANE] for j in range(3))
        dp_ref[:, 3 * LANE:] = dlr_ref[...]
        @pl.when(jnp.logical_and(pl.program_id(0) == 0, pl.program_id(1) == 0))
        def _():
            for r in (dwk_ref, dwv_ref, dg_ref):
                r[...] = jnp.zeros_like(r)

        g = g_ref[...]
        xh, r = _rms_fwd(ckv_ref[...].astype(F32))
        cn = _bf(xh * g)
        dv = dv_ref[...]
        dn = jnp.concatenate([dk_ref[:, h * QKW:h * QKW + LANE] for h in range(MLA_HEADS)], axis=1)
        dcn = _dot_nt(dv, wv_ref[...]) + _dot_nt(dn, wk_ref[...])
        dwv_ref[...] += _dot_tn(cn, dv)
        dwk_ref[...] += _dot_tn(cn, dn)
        drope = jnp.zeros((tok, LANE), F32)
        for h in range(MLA_HEADS):
            drope += dk_ref[:, h * QKW + LANE:(h + 1) * QKW].astype(F32)
        dkr_ref[...] = _bf(drope * cos_ref[...])
        dkrot_ref[...] = _bf(drope * sin_ref[...])
        dx, dg = _rms_bwd(dcn, xh, r, g)
        dckv_ref[...] = _bf(dx)
        dg_ref[...] += dg

    aspec = pl.BlockSpec((MLA_KVR, hw), lambda b, i: (0, 0))
    tspec = pl.BlockSpec((tok, LANE), lambda b, i: (i, 0))
    ospec = pl.BlockSpec((tok, LANE), lambda b, i: (b * nb + i, 0))
    return pl.pallas_call(
        body, name="mla_kv_up_bwd", grid=(bsz, nb),
        in_specs=[pl.BlockSpec((tok, MLA_HEADS * QKW), lambda b, i: (b * nb + i, 0)),
                  pl.BlockSpec((tok, hw), lambda b, i: (b * nb + i, 0)),
                  pl.BlockSpec((tok, LANE), lambda b, i: (b * nb + i, C_CKV // LANE)),
                  pl.BlockSpec((1, MLA_KVR), lambda b, i: (0, 0)), aspec, aspec, tspec, tspec, ospec,
                  pl.BlockSpec(memory_space=pl.ANY)],
        out_specs=[pl.BlockSpec((tok, 4 * LANE), lambda b, i: (b * nb + i, C_CKV // (4 * LANE))), aspec, aspec,
                   pl.BlockSpec((1, MLA_KVR), lambda b, i: (0, 0))],
        out_shape=[jax.ShapeDtypeStruct((tp, N_EXT), BF16)] + [jax.ShapeDtypeStruct((MLA_KVR, hw), F32)] * 2
        + [jax.ShapeDtypeStruct((1, MLA_KVR), F32)],
        input_output_aliases={9: 0},
        compiler_params=_cp(("arbitrary", "arbitrary")),
    )(dkf, dvf, proj, kv_norm_g, wk, wv, cos_t, sin_t, d_lr, dproj)


def _mid_fwd(ya_in, yb_in, proj, hp, target, w_gp, w_mp, w_o, final_g, bsz, lp):
    tp = bsz * lp
    tm = _wide_block(lp)
    nb = lp // tm
    last = pl.cdiv(lp - X0, tm) - 1

    def body(ya_ref, yb_ref, gg_ref, gm_ref, h_ref, ta_ref, tb_ref, wgp_ref, wmp_ref, wo_ref, fg_ref,
             ya_out, yb_out, dh_ref, loss_ref, dfg_ref):
        @pl.when(jnp.logical_and(pl.program_id(0) == 0, pl.program_id(1) == 0))
        def _():
            loss_ref[...] = jnp.zeros_like(loss_ref)
            dfg_ref[...] = jnp.zeros_like(dfg_ref)

        y_a = _dot(ya_ref[...], wgp_ref[...])
        y_b = _dot(yb_ref[...], wmp_ref[...])
        ya_out[...] = _bf(y_a)
        yb_out[...] = _bf(y_b)
        merged = _sigmoid(gg_ref[...].astype(F32)) * y_a + _sigmoid(gm_ref[...].astype(F32)) * y_b
        h2 = h_ref[...] + _dot(_bf(merged), wo_ref[...])
        fg = fg_ref[...]
        xh, r = _rms_fwd(h2)
        pos = pl.program_id(1) * tm + lax.broadcasted_iota(jnp.int32, (tm, 1), 0)
        t = jnp.concatenate([ta_ref[0, tm - X0:, :], tb_ref[0, :tm - X0, :]], axis=0)
        err = jnp.where(pos >= X0, xh * fg - t, 0.0)
        loss_ref[...] += 0.5 * jnp.sum(jnp.mean(err * err, axis=-1, keepdims=True), axis=0, keepdims=True)
        dy = err * (1.0 / D_MODEL)
        dx, dfg = _rms_bwd(dy, xh, r, fg)
        dh_ref[...] = dx
        dfg_ref[...] += dfg

    tok = lambda c: pl.BlockSpec((tm, D_MODEL), lambda b, i: (b * nb + i, c))
    wspec = pl.BlockSpec((D_MODEL, D_MODEL), lambda b, i: (0, 0), pipeline_mode=pl.Buffered(1))
    return pl.pallas_call(
        body, name="mid_fwd", grid=(bsz, nb),
        in_specs=[tok(0), tok(0), tok(C_GG // D_MODEL), tok(C_GM // D_MODEL), tok(0),
                  pl.BlockSpec((1, tm, D_MODEL), lambda b, i: (b, jnp.maximum(i - 1, 0), 0)),
                  pl.BlockSpec((1, tm, D_MODEL), lambda b, i: (b, jnp.minimum(i, last), 0)),
                  wspec, wspec, wspec, pl.BlockSpec((1, D_MODEL), lambda b, i: (0, 0))],
        out_specs=[tok(0), tok(0), tok(0), pl.BlockSpec((1, LANE), lambda b, i: (0, 0)),
                   pl.BlockSpec((1, D_MODEL), lambda b, i: (0, 0))],
        out_shape=[jax.ShapeDtypeStruct((tp, D_MODEL), BF16), jax.ShapeDtypeStruct((tp, D_MODEL), BF16),
                   jax.ShapeDtypeStruct((tp, D_MODEL), F32), jax.ShapeDtypeStruct((1, LANE), F32),
                   jax.ShapeDtypeStruct((1, D_MODEL), F32)],
        compiler_params=_cp(("arbitrary", "arbitrary"), 56),
    )(ya_in, yb_in, proj, proj, hp, target, target, w_gp, w_mp, w_o, final_g)


def _mid_bwd(dh2, y_a, y_b, proj, ya_in, yb_in, o_b, w_o, w_gp, w_mp, bsz, lp):
    tp = bsz * lp
    tm = MXU_DEPTH if tp % MXU_DEPTH == 0 else _attn_block(lp)
    nsteps = tp // tm
    group = 3 * D_MODEL

    def body(dh_ref, ya_ref, yb_ref, mz_ref, gg_ref, gm_ref, yai_ref, ybi_ref, ob_ref, wo_ref, wgp_ref, wmp_ref,
             dyai_ref, do_ref, dp_ref, dl_ref, dwo_ref, dwgp_ref, dwmp_ref, a_o, a_gp, a_mp):
        @pl.when(pl.program_id(0) == 0)
        def _():
            for r in (a_o, a_gp, a_mp):
                r[...] = jnp.zeros_like(r)

        dh = _bf(dh_ref[...])
        dm = _dot_nt(dh, wo_ref[...])
        y_a, y_b = ya_ref[...].astype(F32), yb_ref[...].astype(F32)
        sg, sm = _sigmoid(gg_ref[...].astype(F32)), _sigmoid(gm_ref[...].astype(F32))
        d_ya, d_yb = _bf(sg * dm), _bf(sm * dm)
        dp_ref[:, D_MODEL:2 * D_MODEL] = _bf(dm * y_a * sg * (1.0 - sg))
        dp_ref[:, 2 * D_MODEL:] = _bf(dm * y_b * sm * (1.0 - sm))
        merged = _bf(sg * y_a + sm * y_b)
        dy = _dot_nt(d_yb, wmp_ref[...])
        dyai_ref[...] = _bf(_dot_nt(d_ya, wgp_ref[...]))
        a_o[...] += _dot_tn(merged, dh)
        a_gp[...] += _dot_tn(yai_ref[...], d_ya)
        a_mp[...] += _dot_tn(ybi_ref[...], d_yb)
        mz, o = mz_ref[...].astype(F32), ob_ref[...].astype(F32)
        s = _sigmoid(mz)
        do = _bf(dy * (mz * s))
        do_ref[...] = do
        dp_ref[:, :D_MODEL] = _bf(dy * o * (s * (1.0 + mz * (1.0 - s))))
        prod = do.astype(F32) * o
        for h in range(MLA_HEADS):
            dl = jnp.sum(prod[:, h * MLA_DV:(h + 1) * MLA_DV], axis=-1, keepdims=True)
            dl_ref[h] = jnp.broadcast_to(dl, (tm, LANE))

        @pl.when(pl.program_id(0) == nsteps - 1)
        def _():
            pltpu.sync_copy(a_o, dwo_ref)
            pltpu.sync_copy(a_gp, dwgp_ref)
            pltpu.sync_copy(a_mp, dwmp_ref)

    tok = lambda c: pl.BlockSpec((tm, D_MODEL), lambda i: (i, c))
    wspec = pl.BlockSpec((D_MODEL, D_MODEL), lambda i: (0, 0))
    anyspec = pl.BlockSpec(memory_space=pl.ANY)
    wshape = jax.ShapeDtypeStruct((D_MODEL, D_MODEL), F32)
    return pl.pallas_call(
        body, name="mid_bwd", grid=(nsteps,),
        in_specs=[tok(0), tok(0), tok(0), tok(C_MZ // D_MODEL), tok(C_GG // D_MODEL), tok(C_GM // D_MODEL),
                  tok(0), tok(0), tok(0), wspec, wspec, wspec],
        out_specs=[tok(0), tok(0), pl.BlockSpec((tm, group), lambda i: (i, C_MZ // group)),
                   pl.BlockSpec((MLA_HEADS, tm, LANE), lambda i: (0, i, 0)), anyspec, anyspec, anyspec],
        out_shape=[jax.ShapeDtypeStruct((tp, D_MODEL), BF16)] * 2 + [jax.ShapeDtypeStruct((tp, N_EXT), BF16),
                   jax.ShapeDtypeStruct((MLA_HEADS, tp, LANE), F32)] + [wshape] * 3,
        scratch_shapes=[pltpu.VMEM((D_MODEL, D_MODEL), F32)] * 3,
        compiler_params=_cp(("arbitrary",), 56),
    )(dh2, y_a, y_b, proj, proj, proj, ya_in, yb_in, o_b, w_o, w_gp, w_mp)


MESH_ID = pl.DeviceIdType.MESH
EXCHANGE_SEMS = [pltpu.SemaphoreType.DMA((N_DEV - 1,)), pltpu.SemaphoreType.DMA((N_DEV - 1,)), pltpu.SemaphoreType.DMA]


def _my_place():
    return lax.axis_index("x"), lax.axis_index("y"), lax.axis_index("c")


def _exchange(g_ref, recv_ref, send_sems, recv_sems, local_sem, start, same=False):
    x, y, c = _my_place()
    me = 4 * x + 2 * y + c
    own = pltpu.make_async_copy(g_ref if same else g_ref.at[me], recv_ref.at[me], local_sem)
    sends, lands = [], []
    for d in range(1, N_DEV):
        px = 1 - x if d & 4 else x
        py = 1 - y if d & 2 else y
        pc = 1 - c if d & 1 else c
        peer = 4 * px + 2 * py + pc
        for slot, group in ((me, sends),) if start else ((me, sends), (peer, lands)):
            group.append(pltpu.make_async_remote_copy(
                src_ref=g_ref if same else g_ref.at[peer], dst_ref=recv_ref.at[slot], send_sem=send_sems.at[d - 1],
                recv_sem=recv_sems.at[d - 1], device_id=(px, py, pc), device_id_type=MESH_ID))
    if start:
        own.start()
        for cp in sends:
            cp.start()
    else:
        for cp in lands:
            cp.wait_recv()
        for cp in sends:
            cp.wait_send()
        own.wait()


def _dw_in(u, dproj, slabs):
    tp = u.shape[0]
    tn = 3 * LANE
    nj = N_EXT // tn

    def body(u_ref, d_ref, g_ref, o_ref, recv_ref, send_sems, recv_sems, local_sem):
        j = pl.program_id(0)

        @pl.when(j == 0)
        def _():
            _exchange(g_ref, recv_ref, send_sems, recv_sems, local_sem, True)

        o_ref[...] = _bf(_dot_tn(d_ref[...], u_ref[...]))

        @pl.when(j == nj - 1)
        def _():
            _exchange(g_ref, recv_ref, send_sems, recv_sems, local_sem, False)

    anyspec = pl.BlockSpec(memory_space=pl.ANY)
    return pl.pallas_call(
        body, name="dw_in", grid=(nj,),
        in_specs=[pl.BlockSpec((tp, D_MODEL), lambda j: (0, 0), pipeline_mode=pl.Buffered(1)),
                  pl.BlockSpec((tp, tn), lambda j: (0, j)), anyspec],
        out_specs=[pl.BlockSpec((tn, D_MODEL), lambda j: (j, 0)), anyspec],
        out_shape=[jax.ShapeDtypeStruct((N_EXT, D_MODEL), BF16), jax.ShapeDtypeStruct(slabs.shape, slabs.dtype)],
        scratch_shapes=EXCHANGE_SEMS,
        compiler_params=_cp(("arbitrary",), 56),
    )(u, dproj, slabs)


def _dx_in(dproj, w_ext, hp, dh2, norm_g, slabs, bsz):
    tp = hp.shape[0]
    lp = tp // bsz
    tm = _attn_block(lp)
    ni = lp // tm
    steps = bsz * ni
    assert ni > 1 and tm > X0

    def body(d_ref, w_ref, h_ref, dh_ref, g_ref, s_ref, gx_ref, dg_ref, dm_ref, recv_ref,
             stage, out_sems, send_sems, recv_sems, local_sem):
        s = pl.program_id(0)
        slot = s % 2

        def out_copy(step, head):
            b, at = step // ni, step % 2
            if head:
                return pltpu.make_async_copy(stage.at[at, pl.ds(X0, tm - X0)], gx_ref.at[b, pl.ds(0, tm - X0)],
                                             out_sems.at[at])
            first = pl.multiple_of((step % ni) * tm - X0, 8)
            return pltpu.make_async_copy(stage.at[at], gx_ref.at[b, pl.ds(first, tm)], out_sems.at[at])

        @pl.when(s == 0)
        def _():
            _exchange(s_ref, recv_ref, send_sems, recv_sems, local_sem, True)
            dg_ref[...] = jnp.zeros_like(dg_ref)
            dm_ref[...] = jnp.zeros_like(dm_ref)

        du = _dot_nt(d_ref[...], w_ref[...])
        g = g_ref[...]
        xh, r = _rms_fwd(h_ref[...])
        dx, dg = _rms_bwd(du, xh, r, g)
        dg_ref[...] += dg
        stage[slot] = dh_ref[...] + dx
        head = s % ni == 0

        @pl.when(head)
        def _():
            dm_ref[...] += stage[slot, pl.ds(FRONT, N_META), :]
            out_copy(s, True).start()

        @pl.when(jnp.logical_not(head))
        def _():
            out_copy(s, False).start()

        @pl.when(s % ni == 1)
        def _():
            out_copy(s - 1, True).wait()

        @pl.when(s % ni > 1)
        def _():
            out_copy(s - 1, False).wait()

        @pl.when(jnp.logical_and(head, s > 0))
        def _():
            out_copy(s - 1, False).wait()

        @pl.when(s == steps - 1)
        def _():
            out_copy(s, False).wait()
            _exchange(s_ref, recv_ref, send_sems, recv_sems, local_sem, False)

    tok = pl.BlockSpec((tm, D_MODEL), lambda s: (s, 0))
    anyspec = pl.BlockSpec(memory_space=pl.ANY)
    return pl.pallas_call(
        body, name="dx_in", grid=(steps,),
        in_specs=[pl.BlockSpec((tm, N_EXT), lambda s: (s, 0)),
                  pl.BlockSpec((D_MODEL, N_EXT), lambda s: (0, 0), pipeline_mode=pl.Buffered(1)),
                  tok, tok, pl.BlockSpec((1, D_MODEL), lambda s: (0, 0)), anyspec],
        out_specs=[anyspec, pl.BlockSpec((1, D_MODEL), lambda s: (0, 0)),
                   pl.BlockSpec((N_META, D_MODEL), lambda s: (0, 0)), anyspec],
        out_shape=[jax.ShapeDtypeStruct((bsz, lp - X0, D_MODEL), F32), jax.ShapeDtypeStruct((1, D_MODEL), F32),
                   jax.ShapeDtypeStruct((N_META, D_MODEL), F32), jax.ShapeDtypeStruct(slabs.shape, slabs.dtype)],
        scratch_shapes=[pltpu.VMEM((2, tm, D_MODEL), F32), pltpu.SemaphoreType.DMA((2,))] + EXCHANGE_SEMS,
        compiler_params=_cp(("arbitrary",), 56),
    )(dproj, w_ext, hp, dh2, norm_g, slabs)


W_IN_SHARD = N_IN // N_DEV


def _pad_lanes(a, width=LANE):
    return jnp.pad(a, [(0, 0)] * (a.ndim - 1) + [(0, width - a.shape[-1])])


def _rot_cols(w):
    half = w.shape[-1] // 2
    return jnp.concatenate([-w[..., half:], w[..., :half]], axis=-1)


def _unrot_cols(dw):
    half = dw.shape[-1] // 2
    return jnp.concatenate([dw[..., half:], -dw[..., :half]], axis=-1)


def _w_in_cols(shards, lo, hi):
    parts = []
    for k in range(lo // W_IN_SHARD, (hi - 1) // W_IN_SHARD + 1):
        a, b = max(lo, k * W_IN_SHARD), min(hi, (k + 1) * W_IN_SHARD)
        parts.append(shards[k][:, a - k * W_IN_SHARD:b - k * W_IN_SHARD])
    return parts[0] if len(parts) == 1 else jnp.concatenate(parts, axis=1)


def _w_in_ext(shards):
    c = lambda lo, hi: _w_in_cols(shards, lo, hi)
    kr = c(O_KR, O_MZ)
    return jnp.concatenate([
        c(O_V, O_LR), c(O_Z, O_CQ), c(O_Q, O_K), c(O_K, O_V), c(O_MZ, O_GG), c(O_GG, O_GM), c(O_GM, N_IN),
        c(O_CKV, O_KR), _pad_lanes(kr), _pad_lanes(_rot_cols(kr)), _pad_lanes(c(O_LR, O_Z)), c(O_CQ, O_CKV)], axis=1)


def _w_in_slabs(dwt):
    half = MLA_ROPE // 2
    krot = dwt[C_KROT:C_KROT + MLA_ROPE]
    kr = dwt[C_KR:C_KR + MLA_ROPE] + jnp.concatenate([krot[half:], -krot[:half]], axis=0)
    groups = ((O_Q, GLA_KW, C_Q), (O_K, GLA_KW, C_K), (O_V, GLA_VW, C_V), (O_LR, GLA_RANK, C_LR), (O_Z, GLA_VW, C_Z),
              (O_CQ, MLA_QR, C_CQ), (O_CKV, MLA_KVR, C_CKV), (O_KR, MLA_ROPE, None), (O_MZ, D_MODEL, C_MZ),
              (O_GG, D_MODEL, C_GG), (O_GM, D_MODEL, C_GM))
    slabs = []
    for k in range(N_DEV):
        lo, hi = k * W_IN_SHARD, (k + 1) * W_IN_SHARD
        parts = []
        for first, width, row in groups:
            a, b = max(lo, first), min(hi, first + width)
            if a < b:
                parts.append(kr[a - first:b - first] if row is None else dwt[row + a - first:row + b - first])
        slabs.append(jnp.concatenate(parts, axis=0))
    return jnp.stack(slabs)


def _rope_tables(lp):
    inv = 1.0 / (ROPE_BASE ** (jnp.arange(0, MLA_ROPE, 2, dtype=F32) / MLA_ROPE))
    ang = (jnp.arange(lp, dtype=F32) - FRONT)[:, None] * inv[None, :]
    cos, sin = jnp.cos(ang), jnp.sin(ang)
    return _pad_lanes(jnp.concatenate([cos, cos], axis=1)), _pad_lanes(jnp.concatenate([sin, sin], axis=1))


def _local_step(x, loss_target, w):
    bsz, seq, _ = x.shape
    lp = X0 + seq
    tp = bsz * lp
    assert lp % TOK == 0 and (lp // GLA_CHUNK) % _gla_group(lp // GLA_CHUNK) == 0
    head = jnp.concatenate([jnp.zeros((FRONT, D_MODEL), F32), w["meta_tokens"]], axis=0)
    cos_t, sin_t = _rope_tables(lp)

    w_ext = _w_in_ext(w["w_in"])
    hp, u, proj, packed_all = _proj_in(x, head, w["norm_g"], w_ext, w["packed"])
    gathered = _unpack_shards(packed_all)
    for n, _, axis in PACKED:
        w[n] = _join8(gathered[n], axis)
    gw_pad = jnp.pad(w["gla_gate_w"], ((0, LANE - GLA_RANK), (0, 0)))
    uq = w["mla_w_uq"].reshape(MLA_QR, MLA_HEADS, MLA_QK)
    rope_w = uq[:, :, MLA_NOPE:]
    hw = MLA_HEADS * LANE
    wn = uq[:, :, :MLA_NOPE].reshape(MLA_QR, hw)
    wr = _pad_lanes(rope_w).reshape(MLA_QR, hw)
    wt = _pad_lanes(_rot_cols(rope_w)).reshape(MLA_QR, hw)
    ukv = w["mla_w_ukv"].reshape(MLA_KVR, MLA_HEADS, MLA_NOPE + MLA_DV)
    wk = ukv[:, :, :MLA_NOPE].reshape(MLA_KVR, hw)
    wv = ukv[:, :, MLA_NOPE:].reshape(MLA_KVR, hw)

    o_raw, ya_in, s_all = _gla_fwd(proj, gw_pad, w["gla_gate_b"], w["gla_norm_g"], bsz, lp)
    qf = _q_up(proj, w["mla_q_norm_g"], wn, wr, wt, cos_t, sin_t, bsz, lp)
    kf, vf = _kv_up(proj, w["mla_kv_norm_g"], wk, wv, cos_t, sin_t, bsz, lp)
    o_b, yb_in, lse = _attn_fwd(qf, kf, vf, proj, bsz, lp)
    y_a, y_b, dh2, loss, d_final_g = _mid_fwd(ya_in, yb_in, proj, hp, loss_target, w["gla_proj"], w["mla_proj"],
                                              w["w_out"], w["final_norm_g"], bsz, lp)
    d_ya, d_o, dproj, delta, d_w_out, d_gla_proj, d_mla_proj = _mid_bwd(
        dh2, y_a, y_b, proj, ya_in, yb_in, o_b, w["w_out"], w["gla_proj"], w["mla_proj"], bsz, lp)
    dproj, d_gate, d_gla_norm = _gla_bwd(proj, gw_pad, w["gla_gate_b"], w["gla_norm_g"], o_raw, s_all, d_ya, dproj,
                                         bsz, lp)
    d_lr, d_gw_pad, d_gate_b = _gate_bwd(d_gate, proj, gw_pad)
    dqf, dkf, dvf = _attn_bwd(qf, kf, vf, d_o, lse, delta, bsz, lp)
    dproj, d_wn, d_wr, d_wt, d_qn = _q_up_bwd(dqf, proj, w["mla_q_norm_g"], wn, wr, wt, cos_t, sin_t, dproj,
                                              bsz, lp)
    dproj, d_wk, d_wv, d_kvn = _kv_up_bwd(dkf, dvf, proj, w["mla_kv_norm_g"], wk, wv, cos_t, sin_t, d_lr, dproj,
                                          bsz, lp)

    d_rope = (d_wr.reshape(MLA_QR, MLA_HEADS, LANE)[:, :, :MLA_ROPE]
              + _unrot_cols(d_wt.reshape(MLA_QR, MLA_HEADS, LANE)[:, :, :MLA_ROPE]))
    d_uq = jnp.concatenate([d_wn.reshape(MLA_QR, MLA_HEADS, LANE), d_rope], axis=-1).reshape(MLA_QR, MLA_HEADS * MLA_QK)
    d_ukv = jnp.concatenate([d_wk.reshape(MLA_KVR, MLA_HEADS, LANE), d_wv.reshape(MLA_KVR, MLA_HEADS, LANE)],
                            axis=-1).reshape(MLA_KVR, MLA_HEADS * (MLA_NOPE + MLA_DV))
    mats = dict(gla_gate_w=d_gw_pad[:GLA_RANK], gla_proj=d_gla_proj, mla_w_uq=d_uq, mla_w_ukv=d_ukv,
                mla_proj=d_mla_proj, w_out=d_w_out)
    packed = _pack_shards({n: _bf(_split8(mats[n], axis)) for n, _, axis in PACKED})
    d_w_ext_t, packed_parts = _dw_in(u, dproj, packed)
    w_in_slabs = _w_in_slabs(d_w_ext_t)
    grad_x, d_norm_g, d_meta, w_in_parts = _dx_in(dproj, w_ext, hp, dh2, w["norm_g"], w_in_slabs, bsz)
    small = dict(meta_tokens=d_meta, norm_g=d_norm_g, gla_gate_b=d_gate_b, gla_norm_g=d_gla_norm,
                 mla_q_norm_g=d_qn, mla_kv_norm_g=d_kvn, final_norm_g=d_final_g)
    return loss, grad_x, w_in_parts, packed_parts, small


PACKED = (("gla_gate_w", (GLA_RANK, GLA_KW // N_DEV), 1),
          ("gla_proj", (D_MODEL // N_DEV, D_MODEL), 0), ("mla_w_uq", (MLA_QR, MLA_HEADS * MLA_QK // N_DEV), 1),
          ("mla_w_ukv", (MLA_KVR, MLA_HEADS * (MLA_NOPE + MLA_DV) // N_DEV), 1),
          ("mla_proj", (D_MODEL // N_DEV, D_MODEL), 0), ("w_out", (D_MODEL // N_DEV, D_MODEL), 0))
REPLICATED = (("norm_g", D_MODEL), ("gla_gate_b", GLA_KW), ("gla_norm_g", GLA_DV), ("mla_q_norm_g", MLA_QR),
              ("mla_kv_norm_g", MLA_KVR), ("final_norm_g", D_MODEL))
PACK_ROWS = 480
PACK_BLOCK = 160
SMALL_ROWS = 48
LOSS_ROW = N_META + 25
W_IN_BLOCK = 128


def _all_gather(shards):
    n_arr = len(shards)
    pieces = []
    for a, s in enumerate(shards):
        step = s.shape[0] // 4 if s.shape[0] >= 4 * LANE else s.shape[0]
        pieces += [(a, slice(r, r + step)) for r in range(0, s.shape[0], step)]
    n_pc = len(pieces)

    def body(*refs):
        x_refs, out_refs = refs[:n_arr], refs[n_arr:2 * n_arr]
        send_sems, recv_sems, local_sems = refs[2 * n_arr:]
        x, y, c = _my_place()
        me, sibling = (x, y, c), (x, y, 1 - c)
        chips = [(1 - x, y), (x, 1 - y), (1 - x, 1 - y)]

        def copy(u, k, block, to, from_input=False):
            a, rows = pieces[u]
            slab = out_refs[a].at[4 * block[0] + 2 * block[1] + block[2], rows]
            return pltpu.make_async_remote_copy(
                src_ref=x_refs[a].at[rows] if from_input else slab, dst_ref=slab,
                send_sem=send_sems.at[7 * u + k], recv_sem=recv_sems.at[7 * u + k], device_id=to,
                device_id_type=MESH_ID)

        arrays = range(n_pc)
        mine = [pltpu.make_async_copy(x_refs[a], out_refs[a].at[4 * x + 2 * y + c], local_sems.at[a])
                for a in range(n_arr)]
        for cp in mine:
            cp.start()
        first = [copy(a, 0, me, sibling, True) for a in arrays]
        first += [copy(a, 1 + j, me, (*chip, c), True) for j, chip in enumerate(chips) for a in arrays]
        for cp in first:
            cp.start()
        passed = []
        for j, chip in enumerate(chips):
            for a in arrays:
                copy(a, 1 + j, (*chip, c), me).wait_recv()
                passed.append(copy(a, 4 + j, (*chip, c), sibling))
                passed[-1].start()
        for a in arrays:
            copy(a, 0, sibling, me).wait_recv()
        for j, chip in enumerate(chips):
            for a in arrays:
                copy(a, 4 + j, (*chip, 1 - c), me).wait_recv()
        for cp in first + passed:
            cp.wait_send()
        for cp in mine:
            cp.wait()

    anyspec = pl.BlockSpec(memory_space=pl.ANY)
    return pl.pallas_call(
        body, name="weights_all_gather",
        out_shape=[jax.ShapeDtypeStruct((N_DEV,) + s.shape, s.dtype) for s in shards],
        in_specs=[anyspec] * n_arr, out_specs=[anyspec] * n_arr,
        scratch_shapes=[pltpu.SemaphoreType.DMA((7 * n_pc,)), pltpu.SemaphoreType.DMA((7 * n_pc,)),
                        pltpu.SemaphoreType.DMA((n_arr,))],
    )(*shards)


def _small_exchange(slabs):
    def body(g_ref, recv_ref, send_sems, recv_sems, local_sem):
        _exchange(g_ref, recv_ref, send_sems, recv_sems, local_sem, True)
        _exchange(g_ref, recv_ref, send_sems, recv_sems, local_sem, False)

    vmem = pl.BlockSpec(memory_space=pltpu.VMEM)
    return pl.pallas_call(
        body, name="small_exchange", out_shape=jax.ShapeDtypeStruct(slabs.shape, slabs.dtype),
        in_specs=[vmem], out_specs=vmem, scratch_shapes=EXCHANGE_SEMS,
    )(slabs)


def _adamw(parts, w, m, v, block_rows, name):
    rows, cols = w.shape

    def body(p_ref, w_ref, m_ref, v_ref, g_out, d_out, m_out, v_out):
        g = p_ref[0].astype(F32)
        for s in range(1, N_DEV):
            g = g + p_ref[s].astype(F32)
        m_new = ADAM_B1 * m_ref[...] + (1.0 - ADAM_B1) * g
        v_new = ADAM_B2 * v_ref[...] + (1.0 - ADAM_B2) * (g * g)
        m_hat = m_new / (1.0 - ADAM_B1 ** ADAM_STEP)
        v_hat = v_new / (1.0 - ADAM_B2 ** ADAM_STEP)
        g_out[...] = g
        d_out[...] = -ADAM_LR * (m_hat / (jnp.sqrt(v_hat) + ADAM_EPS) + ADAM_WD * w_ref[...])
        m_out[...] = m_new
        v_out[...] = v_new

    spec = pl.BlockSpec((block_rows, cols), lambda i: (i, 0))
    return pl.pallas_call(
        body, name=name, grid=(pl.cdiv(rows, block_rows),),
        in_specs=[pl.BlockSpec((N_DEV, block_rows, cols), lambda i: (0, i, 0)), spec, spec, spec],
        out_specs=[spec] * 4, out_shape=[jax.ShapeDtypeStruct((rows, cols), F32)] * 4,
        compiler_params=_cp(("parallel",), 48),
    )(parts, w, m, v)


def _pack_rows_of(shape):
    rows = shape[0] * shape[1] // D_MODEL
    return -(-rows // 16) * 16


def _pack_shards(shards):
    parts = []
    for n, shape, _ in PACKED:
        a = shards[n]
        lead = a.shape[:-2]
        if shape[1] != D_MODEL:
            a = a.reshape(lead + (shape[0] * shape[1] // D_MODEL, D_MODEL))
        pad = _pack_rows_of(shape) - a.shape[-2]
        parts.append(jnp.pad(a, [(0, 0)] * len(lead) + [(0, pad), (0, 0)]) if pad else a)
    return jnp.concatenate(parts, axis=-2)


def _unpack_shards(packed):
    lead, out, off = packed.shape[:-2], {}, 0
    for n, shape, _ in PACKED:
        rows = shape[0] * shape[1] // D_MODEL
        out[n] = packed[..., off:off + rows, :].reshape(lead + shape)
        off += _pack_rows_of(shape)
    return out


def _split8(full, axis):
    r, c = full.shape
    if axis == 0:
        return full.reshape(N_DEV, r // N_DEV, c)
    return full.reshape(r, N_DEV, c // N_DEV).transpose(1, 0, 2)


def _join8(shards, axis):
    _, r, c = shards.shape
    if axis == 0:
        return shards.reshape(N_DEV * r, c)
    return shards.transpose(1, 0, 2).reshape(r, N_DEV * c)


def _pack_small(meta_shard, vals, loss_row):
    rows = jnp.concatenate([vals[n].reshape(-1, LANE) for n, _ in REPLICATED] + [loss_row], axis=0)
    rows = jnp.pad(rows, ((0, SMALL_ROWS - N_META - rows.shape[0]), (0, 0)))
    return jnp.concatenate([meta_shard, jnp.broadcast_to(rows, meta_shard.shape[:-2] + rows.shape)], axis=-2)


def _unpack_small(packed):
    out, off = {"meta_tokens": packed[:N_META]}, N_META
    for n, size in REPLICATED:
        out[n] = packed[off:off + size // LANE].reshape(1, size)
        off += size // LANE
    return out


def kernel(x, meta_tokens, norm_g, w_in, gla_gate_w, gla_gate_b, gla_norm_g, gla_proj, mla_q_norm_g, mla_w_uq, mla_kv_norm_g, mla_w_ukv, mla_proj, w_out, final_norm_g, loss_target, m_meta_tokens, m_norm_g, m_w_in, m_gla_gate_w, m_gla_gate_b, m_gla_norm_g, m_gla_proj, m_mla_q_norm_g, m_mla_w_uq, m_mla_kv_norm_g, m_mla_w_ukv, m_mla_proj, m_w_out, m_final_norm_g, v_meta_tokens, v_norm_g, v_w_in, v_gla_gate_w, v_gla_gate_b, v_gla_norm_g, v_gla_proj, v_mla_q_norm_g, v_mla_w_uq, v_mla_kv_norm_g, v_mla_w_ukv, v_mla_proj, v_w_out, v_final_norm_g):
    given = dict(meta_tokens=meta_tokens, norm_g=norm_g, w_in=w_in, gla_gate_w=gla_gate_w, gla_gate_b=gla_gate_b,
                 gla_norm_g=gla_norm_g, gla_proj=gla_proj, mla_q_norm_g=mla_q_norm_g, mla_w_uq=mla_w_uq,
                 mla_kv_norm_g=mla_kv_norm_g, mla_w_ukv=mla_w_ukv, mla_proj=mla_proj, w_out=w_out,
                 final_norm_g=final_norm_g)
    mom_m = dict(meta_tokens=m_meta_tokens, norm_g=m_norm_g, w_in=m_w_in, gla_gate_w=m_gla_gate_w,
                 gla_gate_b=m_gla_gate_b, gla_norm_g=m_gla_norm_g, gla_proj=m_gla_proj, mla_q_norm_g=m_mla_q_norm_g,
                 mla_w_uq=m_mla_w_uq, mla_kv_norm_g=m_mla_kv_norm_g, mla_w_ukv=m_mla_w_ukv, mla_proj=m_mla_proj,
                 w_out=m_w_out, final_norm_g=m_final_norm_g)
    mom_v = dict(meta_tokens=v_meta_tokens, norm_g=v_norm_g, w_in=v_w_in, gla_gate_w=v_gla_gate_w,
                 gla_gate_b=v_gla_gate_b, gla_norm_g=v_gla_norm_g, gla_proj=v_gla_proj, mla_q_norm_g=v_mla_q_norm_g,
                 mla_w_uq=v_mla_w_uq, mla_kv_norm_g=v_mla_kv_norm_g, mla_w_ukv=v_mla_w_ukv, mla_proj=v_mla_proj,
                 w_out=v_w_out, final_norm_g=v_final_norm_g)
    shapes = {n: a.shape for n, a in given.items()}
    shard2d = {n: s for n, s, _ in PACKED}
    shard2d["w_in"] = (D_MODEL, W_IN_SHARD)
    shard2d["meta_tokens"] = (N_META, LANE)

    def as2d(tree):
        out = {n: tree[n].reshape(shard2d[n]) for n in shard2d}
        out.update({n: tree[n].reshape(1, size) for n, size in REPLICATED})
        return out

    w_loc, m_loc, v_loc = as2d(given), as2d(mom_m), as2d(mom_v)

    w_in_all, meta_all = _all_gather([w_loc["w_in"].astype(BF16), w_loc["meta_tokens"]])
    packed = _pack_shards({n: w_loc[n].astype(BF16) for n, _, _ in PACKED})
    full = {"w_in": w_in_all, "meta_tokens": _join8(meta_all, 1), "packed": packed}
    for n, _ in REPLICATED:
        full[n] = w_loc[n]

    loss_part, grad_x, w_in_parts, packed_parts, small = _local_step(x, loss_target, full)
    small_all = _small_exchange(_pack_small(_split8(small["meta_tokens"], 1), small,
                                            jnp.broadcast_to(loss_part[:, :1], (1, LANE))))

    w_in_t = [t["w_in"].T for t in (w_loc, m_loc, v_loc)]
    g_w, d_w, m_w, v_w = (o.T for o in _adamw(w_in_parts, *w_in_t, W_IN_BLOCK, "adamw_w_in"))
    g_p, d_p, m_p, v_p = _adamw(packed_parts, _pack_shards(w_loc), _pack_shards(m_loc), _pack_shards(v_loc),
                                PACK_BLOCK, "adamw_packed")
    zero_row = jnp.zeros((1, LANE), F32)
    g_s, d_s, m_s, v_s = _adamw(small_all, *(_pack_small(t["meta_tokens"], t, zero_row) for t in (w_loc, m_loc, v_loc)),
                                SMALL_ROWS, "adamw_small")
    loss = g_s[LOSS_ROW, 0]

    order = ["meta_tokens", "norm_g", "w_in", "gla_gate_w", "gla_gate_b", "gla_norm_g", "gla_proj", "mla_q_norm_g",
             "mla_w_uq", "mla_kv_norm_g", "mla_w_ukv", "mla_proj", "w_out", "final_norm_g"]
    result = [loss, grad_x]
    for w_in_out, packed_sh, packed_sm in ((g_w, g_p, g_s), (d_w, d_p, d_s), (m_w, m_p, m_s), (v_w, v_p, v_s)):
        tree = _unpack_shards(packed_sh)
        tree.update(_unpack_small(packed_sm))
        tree["w_in"] = w_in_out
        result += [tree[n].reshape(shapes[n]) for n in order]
    return tuple(result)
```

```python
import jax
import jax.numpy as jnp
from jax import lax
from jax.experimental import pallas as pl
from jax.experimental.pallas import tpu as pltpu

F32 = jnp.float32
BF16 = jnp.bfloat16

D_MODEL = 1024
N_META = 16
EPS = 1e-6
FRONT = 48
X0 = FRONT + N_META
GLA_HEADS, GLA_DK, GLA_DV, GLA_RANK, GLA_CHUNK = 4, 128, 256, 16, 64
GLA_GATE_NORMALIZER = 16.0
GLA_KW = GLA_HEADS * GLA_DK
GLA_VW = GLA_HEADS * GLA_DV
MLA_HEADS, MLA_NOPE, MLA_ROPE, MLA_DV, MLA_QR, MLA_KVR = 8, 128, 64, 128, 256, 128
MLA_QK = MLA_NOPE + MLA_ROPE
ROPE_BASE = 10000.0
LANE = 128
QKW = 2 * LANE

C_V, C_Z, C_Q, C_K = 0, 1024, 2048, 2560
C_MZ, C_GG, C_GM = 3072, 4096, 5120
C_CKV, C_KR, C_KROT, C_LR = 6144, 6272, 6400, 6528
C_CQ = 6656
N_EXT = 6912
O_Q, O_K, O_V, O_LR, O_Z, O_CQ, O_CKV, O_KR, O_MZ, O_GG, O_GM, N_IN = (
    0, 512, 1024, 2048, 2064, 3088, 3344, 3472, 3536, 4560, 5584, 6608)

ADAM_LR, ADAM_B1, ADAM_B2, ADAM_EPS, ADAM_WD, ADAM_STEP = 0.001, 0.9, 0.999, 1e-08, 0.01, 10

N_DEV = 8
TOK = 192
ATT_BLOCK = 352
MXU_DEPTH = 256


def _cp(sems=None, vmem_mb=None):
    kw = {}
    if sems is not None:
        kw["dimension_semantics"] = sems
    if vmem_mb is not None:
        kw["vmem_limit_bytes"] = vmem_mb * 1024 * 1024
    return pltpu.CompilerParams(**kw)


def _dot(a, b):
    return jnp.dot(a, b, preferred_element_type=F32)


def _dot_nt(a, b):
    return lax.dot_general(a, b, (((1,), (1,)), ((), ())), preferred_element_type=F32)


def _dot_tn(a, b):
    return lax.dot_general(a, b, (((0,), (0,)), ((), ())), preferred_element_type=F32)


def _sigmoid(x):
    return 1.0 / (1.0 + jnp.exp(-x))


def _bf(x):
    return x.astype(BF16)


def _big_tok(tp):
    return 4 * TOK if tp % (4 * TOK) == 0 else TOK


def _attn_block(lp):
    return ATT_BLOCK if lp % ATT_BLOCK == 0 else TOK


def _wide_block(lp):
    return 2 * ATT_BLOCK if lp % (2 * ATT_BLOCK) == 0 else _attn_block(lp)


def _proj_in(x, head, norm_g, w_ext, packed):
    bsz, seq, _ = x.shape
    lp = X0 + seq
    tp = bsz * lp
    tm = _attn_block(lp)
    nb = lp // tm
    last = pl.cdiv(seq, tm) - 1

    def body(xa_ref, xb_ref, hd_ref, g_ref, w_ref, p_ref, h_ref, u_ref, o_ref, pall_ref, send_sems, recv_sems, local_sem):
        first = jnp.logical_and(pl.program_id(0) == 0, pl.program_id(1) == 0)

        @pl.when(first)
        def _():
            _exchange(p_ref, pall_ref, send_sems, recv_sems, local_sem, True, same=True)

        front = jnp.where(pl.program_id(1) == 0, hd_ref[...], xa_ref[0, tm - X0:, :])
        h = jnp.concatenate([front, xb_ref[0, :tm - X0, :]], axis=0)
        h_ref[...] = h
        r = lax.rsqrt(jnp.mean(h * h, axis=-1, keepdims=True) + EPS)
        u = _bf(h * r * g_ref[...])
        u_ref[...] = u
        o_ref[...] = _bf(_dot(u, w_ref[...]))

        @pl.when(jnp.logical_and(pl.program_id(0) == bsz - 1, pl.program_id(1) == nb - 1))
        def _():
            _exchange(p_ref, pall_ref, send_sems, recv_sems, local_sem, False, same=True)

    anyspec = pl.BlockSpec(memory_space=pl.ANY)
    tok = lambda width: pl.BlockSpec((tm, width), lambda b, i: (b * nb + i, 0))
    return pl.pallas_call(
        body, name="proj_in", grid=(bsz, nb),
        in_specs=[pl.BlockSpec((1, tm, D_MODEL), lambda b, i: (b, jnp.maximum(i - 1, 0), 0)),
                  pl.BlockSpec((1, tm, D_MODEL), lambda b, i: (b, jnp.minimum(i, last), 0)),
                  pl.BlockSpec((X0, D_MODEL), lambda b, i: (0, 0)),
                  pl.BlockSpec((1, D_MODEL), lambda b, i: (0, 0)),
                  pl.BlockSpec((D_MODEL, N_EXT), lambda b, i: (0, 0), pipeline_mode=pl.Buffered(1)), anyspec],
        out_specs=[tok(D_MODEL), tok(D_MODEL), tok(N_EXT), anyspec],
        out_shape=[jax.ShapeDtypeStruct((tp, D_MODEL), F32), jax.ShapeDtypeStruct((tp, D_MODEL), BF16),
                   jax.ShapeDtypeStruct((tp, N_EXT), BF16),
                   jax.ShapeDtypeStruct((N_DEV,) + packed.shape, packed.dtype)],
        scratch_shapes=EXCHANGE_SEMS,
        compiler_params=_cp(("arbitrary", "arbitrary"), 56),
    )(x, x, head, norm_g, w_ext, packed)


def _gla_group(n_chunks):
    return 11 if n_chunks % 11 == 0 else 3


def _tri_dot(tri, x):
    hi = _bf(x)
    rest = x - hi.astype(F32)
    mid = _bf(rest)
    return _dot(tri, hi) + _dot(tri, mid) + _dot(tri, _bf(rest - mid.astype(F32)))


def _gla_gates(q_ref, k_ref, lr_ref, gw_ref, gb_ref, rows, not_first):
    z = _dot(lr_ref[rows, :], gw_ref[...]) + gb_ref[...]
    logsig = jnp.minimum(z, 0.0) - jnp.log(1.0 + jnp.exp(-jnp.abs(z)))
    row = lax.broadcasted_iota(jnp.int32, (GLA_CHUNK, GLA_KW), 0)
    live = jnp.logical_or(not_first, row >= FRONT)
    g = jnp.where(live, logsig * (1.0 / GLA_GATE_NORMALIZER), 0.0)
    ri = lax.broadcasted_iota(jnp.int32, (GLA_CHUNK, GLA_CHUNK), 0)
    ci = lax.broadcasted_iota(jnp.int32, (GLA_CHUNK, GLA_CHUNK), 1)
    tril = ci <= ri
    b = _tri_dot(_bf(tril.astype(F32)), g)
    bl = jnp.sum(jnp.where(row == GLA_CHUNK - 1, b, 0.0), axis=0, keepdims=True)
    eb, enb, elb, ebl = jnp.exp(b), jnp.exp(-b), jnp.exp(bl - b), jnp.exp(bl)
    q = q_ref[rows, :].astype(F32) * (GLA_DK ** -0.5)
    k = k_ref[rows, :].astype(F32)
    qe, ke, kl = q * eb, k * enb, k * elb
    return dict(z=z, live=live, tril=tril, row=row, eb=eb, enb=enb, elb=elb, ebl=ebl, qe=qe, ke=ke, kl=kl,
                qe_b=_bf(qe), ke_b=_bf(ke), kl_b=_bf(kl))


def _gla_in_specs(n_groups, gla_rows, rev):
    def rb(b, n):
        return b * n_groups + ((n_groups - 1 - n) if rev else n)

    return rb, [pl.BlockSpec((gla_rows, GLA_KW), lambda b, n: (rb(b, n), C_Q // GLA_KW)),
                pl.BlockSpec((gla_rows, GLA_KW), lambda b, n: (rb(b, n), C_K // GLA_KW)),
                pl.BlockSpec((gla_rows, GLA_VW), lambda b, n: (rb(b, n), C_V // GLA_VW)),
                pl.BlockSpec((gla_rows, GLA_VW), lambda b, n: (rb(b, n), C_Z // GLA_VW)),
                pl.BlockSpec((gla_rows, LANE), lambda b, n: (rb(b, n), C_LR // LANE)),
                pl.BlockSpec((LANE, GLA_KW), lambda b, n: (0, 0)),
                pl.BlockSpec((1, GLA_KW), lambda b, n: (0, 0)),
                pl.BlockSpec((1, GLA_DV), lambda b, n: (0, 0))]


def _gla_fwd(proj, gw_pad, gate_b, gla_norm_g, bsz, lp):
    n_chunks = lp // GLA_CHUNK
    gla_group = _gla_group(n_chunks)
    gla_rows = gla_group * GLA_CHUNK
    n_groups = n_chunks // gla_group
    tp = bsz * lp

    def body(q_ref, k_ref, v_ref, z_ref, lr_ref, gw_ref, gb_ref, gn_ref, oraw_ref, ya_ref, sall_ref, st_scr):
        grp = pl.program_id(1)

        @pl.when(grp == 0)
        def _():
            st_scr[...] = jnp.zeros_like(st_scr)

        chunks = [slice(j * GLA_CHUNK, (j + 1) * GLA_CHUNK) for j in range(gla_group)]
        cs = [_gla_gates(q_ref, k_ref, lr_ref, gw_ref, gb_ref, rows, True if j else grp > 0)
              for j, rows in enumerate(chunks)]
        gn = gn_ref[...]
        sts = [st_scr[h] for h in range(GLA_HEADS)]
        heads = [(slice(h * GLA_DK, (h + 1) * GLA_DK), slice(h * GLA_DV, (h + 1) * GLA_DV)) for h in range(GLA_HEADS)]
        a_all = [[_bf(jnp.where(c["tril"], _dot_nt(c["qe_b"][:, ks], c["ke_b"][:, ks]), 0.0)) for ks, _ in heads]
                 for c in cs]
        u_all = [[_dot_tn(v_ref[rows, vs], c["kl_b"][:, ks]) for ks, vs in heads] for rows, c in zip(chunks, cs)]
        for j, (rows, c) in enumerate(zip(chunks, cs)):
            for h, (ks, vs) in enumerate(heads):
                st = sts[h]
                sall_ref[0, j, h] = st
                o = _dot(a_all[j][h], v_ref[rows, vs]) + _dot_nt(c["qe_b"][:, ks], _bf(st))
                sts[h] = st * c["ebl"][:, ks] + u_all[j][h]
                oraw_ref[rows, vs] = o
                r = lax.rsqrt(jnp.mean(o * o, axis=-1, keepdims=True) + EPS)
                zg = z_ref[rows, vs].astype(F32)
                ya_ref[rows, vs] = _bf((o * r * gn) * (zg * _sigmoid(zg)))
        for h in range(GLA_HEADS):
            st_scr[h] = sts[h]

    rb, in_specs = _gla_in_specs(n_groups, gla_rows, False)
    return pl.pallas_call(
        body, name="gla_fwd", grid=(bsz, n_groups), in_specs=in_specs,
        out_specs=[pl.BlockSpec((gla_rows, GLA_VW), lambda b, n: (rb(b, n), 0)),
                   pl.BlockSpec((gla_rows, GLA_VW), lambda b, n: (rb(b, n), 0)),
                   pl.BlockSpec((1, gla_group, GLA_HEADS, GLA_DV, GLA_DK), lambda b, n: (b, n, 0, 0, 0))],
        out_shape=[jax.ShapeDtypeStruct((tp, GLA_VW), F32), jax.ShapeDtypeStruct((tp, GLA_VW), BF16),
                   jax.ShapeDtypeStruct((bsz, n_chunks, GLA_HEADS, GLA_DV, GLA_DK), F32)],
        scratch_shapes=[pltpu.VMEM((GLA_HEADS, GLA_DV, GLA_DK), F32)],
        compiler_params=_cp(("parallel", "arbitrary"), 56),
    )(proj, proj, proj, proj, proj, gw_pad, gate_b, gla_norm_g)


def _gla_bwd(proj, gw_pad, gate_b, gla_norm_g, o_raw, s_all, d_ya, dproj, bsz, lp):
    n_chunks = lp // GLA_CHUNK
    gla_group = _gla_group(n_chunks)
    gla_rows = gla_group * GLA_CHUNK
    n_groups = n_chunks // gla_group
    tp = bsz * lp

    def body(q_ref, k_ref, v_ref, z_ref, lr_ref, gw_ref, gb_ref, gn_ref, o_ref, s_ref, dya_ref, _,
             dp_ref, dz_ref, dgn_ref, dst_scr):
        dv_ref, dzg_ref = dp_ref.at[:, C_V:C_V + GLA_VW], dp_ref.at[:, C_Z:C_Z + GLA_VW]

        @pl.when(jnp.logical_and(pl.program_id(0) == 0, pl.program_id(1) == 0))
        def _():
            dgn_ref[...] = jnp.zeros_like(dgn_ref)

        @pl.when(pl.program_id(1) == 0)
        def _():
            dst_scr[...] = jnp.zeros_like(dst_scr)

        grp = n_groups - 1 - pl.program_id(1)
        chunks = [slice(j * GLA_CHUNK, (j + 1) * GLA_CHUNK) for j in range(gla_group)]
        cs = [_gla_gates(q_ref, k_ref, lr_ref, gw_ref, gb_ref, rows, True if j else grp > 0)
              for j, rows in enumerate(chunks)]
        gn = gn_ref[...]
        dgn = jnp.zeros((1, GLA_DV), F32)
        dqe_h, dke_h, dkl_h, dbl_h = ([[None] * GLA_HEADS for _ in chunks] for _ in range(4))
        dsts = [dst_scr[h] for h in range(GLA_HEADS)]
        for j in reversed(range(gla_group)):
            rows, c = chunks[j], cs[j]
            for h in range(GLA_HEADS):
                ks, vs = slice(h * GLA_DK, (h + 1) * GLA_DK), slice(h * GLA_DV, (h + 1) * GLA_DV)
                dst = dsts[h]
                v = v_ref[rows, vs]
                st = s_ref[0, j, h]
                o = o_ref[rows, vs]
                r = lax.rsqrt(jnp.mean(o * o, axis=-1, keepdims=True) + EPS)
                xh = o * r
                zg = z_ref[rows, vs].astype(F32)
                sg = _sigmoid(zg)
                dy = dya_ref[rows, vs].astype(F32)
                dzg_ref[rows, vs] = _bf(dy * (xh * gn) * (sg * (1.0 + zg * (1.0 - sg))))
                t = dy * (zg * sg)
                dgn += jnp.sum(t * xh, axis=0, keepdims=True)
                dxh = t * gn
                do_b = _bf(r * (dxh - xh * jnp.mean(dxh * xh, axis=-1, keepdims=True)))
                qe_b, ke_b, kl_b, dst_b = c["qe_b"][:, ks], c["ke_b"][:, ks], c["kl_b"][:, ks], _bf(dst)
                a = jnp.where(c["tril"], _dot_nt(qe_b, ke_b), 0.0)
                da_b = _bf(jnp.where(c["tril"], _dot_nt(do_b, v), 0.0))
                dqe_h[j][h] = _dot(da_b, ke_b) + _dot(do_b, _bf(st))
                dke_h[j][h] = _dot_tn(da_b, qe_b)
                dkl = _dot(v, dst_b)
                dkl_h[j][h] = dkl
                dv_ref[rows, vs] = _bf(_dot_tn(_bf(a), do_b) + _dot_nt(kl_b, dst_b))
                ddecay = jnp.sum(dst * st, axis=0, keepdims=True)
                dbl_h[j][h] = jnp.sum(dkl * c["kl"][:, ks], axis=0, keepdims=True) + ddecay * c["ebl"][:, ks]
                dsts[h] = dst * c["ebl"][:, ks] + _dot_tn(do_b, qe_b)
        for h in range(GLA_HEADS):
            dst_scr[h] = dsts[h]
        dgn_ref[...] += dgn
        ri = lax.broadcasted_iota(jnp.int32, (GLA_CHUNK, GLA_CHUNK), 0)
        ci = lax.broadcasted_iota(jnp.int32, (GLA_CHUNK, GLA_CHUNK), 1)
        triu = _bf((ci >= ri).astype(F32))
        for j, (rows, c) in enumerate(zip(chunks, cs)):
            dqe, dke, dkl, dbl = (jnp.concatenate(p[j], axis=1) for p in (dqe_h, dke_h, dkl_h, dbl_h))
            db = dqe * c["qe"] - dke * c["ke"] - dkl * c["kl"] + jnp.where(c["row"] == GLA_CHUNK - 1, dbl, 0.0)
            dg = _tri_dot(triu, db)
            dg = jnp.where(c["live"], dg, 0.0)
            dz_ref[rows, :] = dg * (1.0 / GLA_GATE_NORMALIZER) * _sigmoid(-c["z"])
            dp_ref[rows, C_Q:C_Q + GLA_KW] = _bf(dqe * c["eb"] * (GLA_DK ** -0.5))
            dp_ref[rows, C_K:C_K + GLA_KW] = _bf(dke * c["enb"] + dkl * c["elb"])

    rb, in_specs = _gla_in_specs(n_groups, gla_rows, True)
    wide = pl.BlockSpec((gla_rows, GLA_VW), lambda b, n: (rb(b, n), 0))
    group = C_MZ
    return pl.pallas_call(
        body, name="gla_bwd", grid=(bsz, n_groups),
        in_specs=in_specs + [wide, pl.BlockSpec((1, gla_group, GLA_HEADS, GLA_DV, GLA_DK),
                                                lambda b, n: (b, n_groups - 1 - n, 0, 0, 0)), wide,
                             pl.BlockSpec(memory_space=pl.ANY)],
        out_specs=[pl.BlockSpec((gla_rows, group), lambda b, n: (rb(b, n), 0)),
                   pl.BlockSpec((gla_rows, GLA_KW), lambda b, n: (rb(b, n), 0)),
                   pl.BlockSpec((1, GLA_DV), lambda b, n: (0, 0))],
        out_shape=[jax.ShapeDtypeStruct((tp, N_EXT), BF16), jax.ShapeDtypeStruct((tp, GLA_KW), F32),
                   jax.ShapeDtypeStruct((1, GLA_DV), F32)],
        input_output_aliases={11: 0},
        scratch_shapes=[pltpu.VMEM((GLA_HEADS, GLA_DV, GLA_DK), F32)],
        compiler_params=_cp(("arbitrary", "arbitrary"), 56),
    )(proj, proj, proj, proj, proj, gw_pad, gate_b, gla_norm_g, o_raw, s_all, d_ya, dproj)


def _gate_bwd(dz, proj, gw_pad):
    tp = dz.shape[0]
    tm = _big_tok(tp)

    def body(dz_ref, lr_ref, gw_ref, dlr_ref, dgw_ref, dgb_ref):
        @pl.when(pl.program_id(0) == 0)
        def _():
            dgw_ref[...] = jnp.zeros_like(dgw_ref)
            dgb_ref[...] = jnp.zeros_like(dgb_ref)

        dz = dz_ref[...]
        dz_b = _bf(dz)
        dlr_ref[...] = _bf(_dot_nt(dz_b, gw_ref[...]))
        dgw_ref[...] += _dot_tn(lr_ref[...], dz_b)
        dgb_ref[...] += jnp.sum(dz, axis=0, keepdims=True)

    return pl.pallas_call(
        body, name="gate_bwd", grid=(tp // tm,),
        in_specs=[pl.BlockSpec((tm, GLA_KW), lambda i: (i, 0)),
                  pl.BlockSpec((tm, LANE), lambda i: (i, C_LR // LANE)),
                  pl.BlockSpec((LANE, GLA_KW), lambda i: (0, 0))],
        out_specs=[pl.BlockSpec((tm, LANE), lambda i: (i, 0)),
                   pl.BlockSpec((LANE, GLA_KW), lambda i: (0, 0)),
                   pl.BlockSpec((1, GLA_KW), lambda i: (0, 0))],
        out_shape=[jax.ShapeDtypeStruct((tp, LANE), BF16), jax.ShapeDtypeStruct((LANE, GLA_KW), F32),
                   jax.ShapeDtypeStruct((1, GLA_KW), F32)],
        compiler_params=_cp(("arbitrary",)),
    )(dz, proj, gw_pad)


def _rms_fwd(x):
    r = lax.rsqrt(jnp.mean(x * x, axis=-1, keepdims=True) + EPS)
    return x * r, r


def _rms_bwd(dy, xh, r, g):
    dxh = dy * g
    dx = r * (dxh - xh * jnp.mean(dxh * xh, axis=-1, keepdims=True))
    return dx, jnp.sum(dy * xh, axis=0, keepdims=True)


def _q_up(proj, q_norm_g, wn, wr, wt, cos_t, sin_t, bsz, lp):
    tp = bsz * lp
    tok = _wide_block(lp)
    nb = lp // tok

    def body(cq_ref, g_ref, wn_ref, wr_ref, wt_ref, cos_ref, sin_ref, q_ref):
        xh, _ = _rms_fwd(cq_ref[...].astype(F32))
        cqn = _bf(xh * g_ref[...])
        nope = _dot(cqn, wn_ref[...])
        rope = _dot(cqn, wr_ref[...])
        rot = _dot(cqn, wt_ref[...])
        cos, sin = cos_ref[...], sin_ref[...]
        one = (lax.broadcasted_iota(jnp.int32, (tok, LANE), 1) == BIAS_LANE).astype(F32)
        for h in range(MLA_HEADS):
            sl = slice(h * LANE, (h + 1) * LANE)
            q_ref[:, h * QKW:h * QKW + LANE] = _bf(nope[:, sl])
            q_ref[:, h * QKW + LANE:(h + 1) * QKW] = _bf(rope[:, sl] * cos + rot[:, sl] * sin + one)

    wspec = pl.BlockSpec((MLA_QR, MLA_HEADS * LANE), lambda b, i: (0, 0))
    tspec = pl.BlockSpec((tok, LANE), lambda b, i: (i, 0))
    return pl.pallas_call(
        body, name="mla_q_up", grid=(bsz, nb),
        in_specs=[pl.BlockSpec((tok, MLA_QR), lambda b, i: (b * nb + i, C_CQ // MLA_QR)),
                  pl.BlockSpec((1, MLA_QR), lambda b, i: (0, 0)), wspec, wspec, wspec, tspec, tspec],
        out_specs=pl.BlockSpec((tok, MLA_HEADS * QKW), lambda b, i: (b * nb + i, 0)),
        out_shape=jax.ShapeDtypeStruct((tp, MLA_HEADS * QKW), BF16),
        compiler_params=_cp(("parallel", "parallel")),
    )(proj, q_norm_g, wn, wr, wt, cos_t, sin_t)


def _kv_up(proj, kv_norm_g, wk, wv, cos_t, sin_t, bsz, lp):
    tp = bsz * lp
    tok = _wide_block(lp)
    nb = lp // tok

    def body(ckv_ref, kr_ref, krot_ref, g_ref, wk_ref, wv_ref, cos_ref, sin_ref, k_ref, v_ref):
        xh, _ = _rms_fwd(ckv_ref[...].astype(F32))
        cn = _bf(xh * g_ref[...])
        kn = _dot(cn, wk_ref[...])
        v_ref[...] = _bf(_dot(cn, wv_ref[...]))
        pos = pl.program_id(1) * tok + lax.broadcasted_iota(jnp.int32, (tok, LANE), 0)
        lane = lax.broadcasted_iota(jnp.int32, (tok, LANE), 1)
        bias = jnp.where(jnp.logical_and(lane == BIAS_LANE, pos < FRONT), KEY_BIAS, 0.0)
        kr = _bf(kr_ref[...].astype(F32) * cos_ref[...] + krot_ref[...].astype(F32) * sin_ref[...] + bias)
        for h in range(MLA_HEADS):
            k_ref[:, h * QKW:h * QKW + LANE] = _bf(kn[:, h * LANE:(h + 1) * LANE])
            k_ref[:, h * QKW + LANE:(h + 1) * QKW] = kr

    wspec = pl.BlockSpec((MLA_KVR, MLA_HEADS * LANE), lambda b, i: (0, 0))
    tspec = pl.BlockSpec((tok, LANE), lambda b, i: (i, 0))
    return pl.pallas_call(
        body, name="mla_kv_up", grid=(bsz, nb),
        in_specs=[pl.BlockSpec((tok, LANE), lambda b, i: (b * nb + i, C_CKV // LANE)),
                  pl.BlockSpec((tok, LANE), lambda b, i: (b * nb + i, C_KR // LANE)),
                  pl.BlockSpec((tok, LANE), lambda b, i: (b * nb + i, C_KROT // LANE)),
                  pl.BlockSpec((1, MLA_KVR), lambda b, i: (0, 0)), wspec, wspec, tspec, tspec],
        out_specs=[pl.BlockSpec((tok, MLA_HEADS * QKW), lambda b, i: (b * nb + i, 0)),
                   pl.BlockSpec((tok, MLA_HEADS * LANE), lambda b, i: (b * nb + i, 0))],
        out_shape=[jax.ShapeDtypeStruct((tp, MLA_HEADS * QKW), BF16),
                   jax.ShapeDtypeStruct((tp, MLA_HEADS * LANE), BF16)],
        compiler_params=_cp(("parallel", "parallel")),
    )(proj, proj, proj, kv_norm_g, wk, wv, cos_t, sin_t)


ATT_SCALE = MLA_QK ** -0.5


KEY_BIAS = -1e30
BIAS_LANE = MLA_ROPE
NEG = 2 * KEY_BIAS
LOG2E = 1.4426950408889634
EXP2_SCALE = ATT_SCALE * LOG2E


def _causal_fill(s, r0, fill):
    tq, kmax = s.shape
    a = r0 // LANE * LANE
    mask = (a + lax.broadcasted_iota(jnp.int32, (tq, kmax - a), 1)
            <= r0 + lax.broadcasted_iota(jnp.int32, (tq, kmax - a), 0))
    right = jnp.where(mask, s[:, a:], fill)
    return jnp.concatenate([s[:, :a], right], axis=1) if a else right


def _attn_fwd(qf, kf, vf, proj, bsz, lp):
    tp = bsz * lp
    tq = _attn_block(lp)
    nh = 2

    def body(q_ref, k_ref, v_ref, mz_ref, ob_ref, yb_ref, lse_ref):
        starts = list(range(0, lp, tq))
        for pair in (starts[i:i + 2] for i in range(0, len(starts), 2)):
            work = [(r0, h) for r0 in pair for h in range(nh)]
            ss = [_causal_fill(_dot_nt(q_ref[r0:r0 + tq, h * QKW:(h + 1) * QKW],
                                       k_ref[0:r0 + tq, h * QKW:(h + 1) * QKW]), r0, NEG) for r0, h in work]
            ms = [jnp.max(s, axis=-1, keepdims=True) for s in ss]
            ps = [jnp.exp2((s - m) * EXP2_SCALE) for s, m in zip(ss, ms)]
            ls = [jnp.sum(p, axis=-1, keepdims=True) for p in ps]
            for (r0, h), p, m, l in zip(work, ps, ms, ls):
                rows, cols = slice(r0, r0 + tq), slice(h * MLA_DV, (h + 1) * MLA_DV)
                o = _dot(_bf(p), v_ref[0:r0 + tq, cols]) / l
                ob_ref[rows, cols] = _bf(o)
                mz = mz_ref[rows, cols].astype(F32)
                yb_ref[rows, cols] = _bf(o * (mz * _sigmoid(mz)))
                lse_ref[0, h, rows, :] = jnp.broadcast_to(m * EXP2_SCALE + jnp.log2(l), (tq, LANE))

    head = lambda off: pl.BlockSpec((lp, nh * MLA_DV), lambda b, h: (b, off + h))
    wide = pl.BlockSpec((lp, nh * QKW), lambda b, h: (b, h))
    return pl.pallas_call(
        body, name="mla_attn_fwd", grid=(bsz, MLA_HEADS // nh),
        in_specs=[wide, wide, head(0), head(C_MZ // (nh * MLA_DV))],
        out_specs=[head(0), head(0), pl.BlockSpec((1, nh, lp, LANE), lambda b, h: (b, h, 0, 0))],
        out_shape=[jax.ShapeDtypeStruct((tp, MLA_HEADS * MLA_DV), BF16),
                   jax.ShapeDtypeStruct((tp, MLA_HEADS * MLA_DV), BF16),
                   jax.ShapeDtypeStruct((bsz, MLA_HEADS, lp, LANE), F32)],
        compiler_params=_cp(("parallel", "parallel"), 56),
    )(qf, kf, vf, proj)


def _attn_bwd_blocks(lp):
    return [(0, X0)] + [(r0, min(MXU_DEPTH, lp - r0)) for r0 in range(X0, lp, MXU_DEPTH)]


def _attn_bwd(qf, kf, vf, d_o, lse, delta, bsz, lp):
    tp = bsz * lp

    def body(q_ref, k_ref, v_ref, do_ref, lse_ref, dl_ref, dq_ref, dk_ref, dv_ref, dk_acc, dv_acc):
        dk_acc[...] = jnp.zeros_like(dk_acc)
        dv_acc[...] = jnp.zeros_like(dv_acc)
        for r0, tq in _attn_bwd_blocks(lp):
            rows, kmax = slice(r0, r0 + tq), r0 + tq
            q, do = q_ref[rows, :], do_ref[rows, :]
            k, v = k_ref[0:kmax, :], v_ref[0:kmax, :]
            p = jnp.exp2(_dot_nt(q, k) * EXP2_SCALE - lse_ref[0, 0, rows, :][:, :1])
            p = _causal_fill(p, r0, 0.0)
            ds = _bf(p * (_dot_nt(do, v) - dl_ref[0, rows, :][:, :1]))
            dq_ref[rows, :] = _bf(_dot(ds, k) * ATT_SCALE)
            dk_acc[0:kmax, :] += _dot_tn(ds, q)
            dv_acc[0:kmax, :] += _dot_tn(_bf(p), do)
        dk_ref[...] = _bf(dk_acc[...] * ATT_SCALE)
        dv_ref[...] = _bf(dv_acc[...])

    wide = pl.BlockSpec((lp, QKW), lambda b, h: (b, h))
    narrow = pl.BlockSpec((lp, MLA_DV), lambda b, h: (b, h))
    stat = pl.BlockSpec((1, 1, lp, LANE), lambda b, h: (b, h, 0, 0))
    return pl.pallas_call(
        body, name="mla_attn_bwd", grid=(bsz, MLA_HEADS),
        in_specs=[wide, wide, narrow, narrow, stat, pl.BlockSpec((1, lp, LANE), lambda b, h: (h, b, 0))],
        out_specs=[wide, wide, narrow],
        out_shape=[jax.ShapeDtypeStruct((tp, MLA_HEADS * QKW), BF16), jax.ShapeDtypeStruct((tp, MLA_HEADS * QKW), BF16),
                   jax.ShapeDtypeStruct((tp, MLA_HEADS * MLA_DV), BF16)],
        scratch_shapes=[pltpu.VMEM((lp, QKW), F32), pltpu.VMEM((lp, MLA_DV), F32)],
        compiler_params=_cp(("parallel", "parallel"), 56),
    )(qf, kf, vf, d_o, lse, delta)


def _q_up_bwd(dqf, proj, q_norm_g, wn, wr, wt, cos_t, sin_t, dproj, bsz, lp):
    tp = bsz * lp
    tok = _wide_block(lp)
    nb = lp // tok
    hw = MLA_HEADS * LANE

    def body(dq_ref, cq_ref, g_ref, wn_ref, wr_ref, wt_ref, cos_ref, sin_ref, _,
             dcq_ref, dwn_ref, dwr_ref, dwt_ref, dg_ref):
        @pl.when(jnp.logical_and(pl.program_id(0) == 0, pl.program_id(1) == 0))
        def _():
            for r in (dwn_ref, dwr_ref, dwt_ref, dg_ref):
                r[...] = jnp.zeros_like(r)

        g = g_ref[...]
        xh, r = _rms_fwd(cq_ref[...].astype(F32))
        cqn = _bf(xh * g)
        dn = jnp.concatenate([dq_ref[:, h * QKW:h * QKW + LANE] for h in range(MLA_HEADS)], axis=1)
        dr = jnp.concatenate([dq_ref[:, h * QKW + LANE:(h + 1) * QKW] for h in range(MLA_HEADS)], axis=1).astype(F32)
        dr_c = _bf(dr * jnp.tile(cos_ref[...], (1, MLA_HEADS)))
        dr_s = _bf(dr * jnp.tile(sin_ref[...], (1, MLA_HEADS)))
        dcqn = _dot_nt(dn, wn_ref[...]) + _dot_nt(dr_c, wr_ref[...]) + _dot_nt(dr_s, wt_ref[...])
        dwn_ref[...] += _dot_tn(cqn, dn)
        dwr_ref[...] += _dot_tn(cqn, dr_c)
        dwt_ref[...] += _dot_tn(cqn, dr_s)
        dx, dg = _rms_bwd(dcqn, xh, r, g)
        dcq_ref[...] = _bf(dx)
        dg_ref[...] += dg

    aspec = pl.BlockSpec((MLA_QR, hw), lambda b, i: (0, 0))
    tspec = pl.BlockSpec((tok, LANE), lambda b, i: (i, 0))
    return pl.pallas_call(
        body, name="mla_q_up_bwd", grid=(bsz, nb),
        in_specs=[pl.BlockSpec((tok, MLA_HEADS * QKW), lambda b, i: (b * nb + i, 0)),
                  pl.BlockSpec((tok, MLA_QR), lambda b, i: (b * nb + i, C_CQ // MLA_QR)),
                  pl.BlockSpec((1, MLA_QR), lambda b, i: (0, 0)), aspec, aspec, aspec, tspec, tspec,
                  pl.BlockSpec(memory_space=pl.ANY)],
        out_specs=[pl.BlockSpec((tok, MLA_QR), lambda b, i: (b * nb + i, C_CQ // MLA_QR)), aspec, aspec, aspec,
                   pl.BlockSpec((1, MLA_QR), lambda b, i: (0, 0))],
        out_shape=[jax.ShapeDtypeStruct((tp, N_EXT), BF16)] + [jax.ShapeDtypeStruct((MLA_QR, hw), F32)] * 3
        + [jax.ShapeDtypeStruct((1, MLA_QR), F32)],
        input_output_aliases={8: 0},
        compiler_params=_cp(("arbitrary", "arbitrary")),
    )(dqf, proj, q_norm_g, wn, wr, wt, cos_t, sin_t, dproj)


def _kv_up_bwd(dkf, dvf, proj, kv_norm_g, wk, wv, cos_t, sin_t, d_lr, dproj, bsz, lp):
    tp = bsz * lp
    tok = _wide_block(lp)
    nb = lp // tok
    hw = MLA_HEADS * LANE

    def body(dk_ref, dv_ref, ckv_ref, g_ref, wk_ref, wv_ref, cos_ref, sin_ref, dlr_ref, _,
             dp_ref, dwk_ref, dwv_ref, dg_ref):
        dckv_ref, dkr_ref, dkrot_ref = (dp_ref.at[:, j * LANE:(j + 1) * LANE] for j in range(3))
        dp_ref[:, 3 * LANE:] = dlr_ref[...]
        @pl.when(jnp.logical_and(pl.program_id(0) == 0, pl.program_id(1) == 0))
        def _():
            for r in (dwk_ref, dwv_ref, dg_ref):
                r[...] = jnp.zeros_like(r)

        g = g_ref[...]
        xh, r = _rms_fwd(ckv_ref[...].astype(F32))
        cn = _bf(xh * g)
        dv = dv_ref[...]
        dn = jnp.concatenate([dk_ref[:, h * QKW:h * QKW + LANE] for h in range(MLA_HEADS)], axis=1)
        dcn = _dot_nt(dv, wv_ref[...]) + _dot_nt(dn, wk_ref[...])
        dwv_ref[...] += _dot_tn(cn, dv)
        dwk_ref[...] += _dot_tn(cn, dn)
        drope = jnp.zeros((tok, LANE), F32)
        for h in range(MLA_HEADS):
            drope += dk_ref[:, h * QKW + LANE:(h + 1) * QKW].astype(F32)
        dkr_ref[...] = _bf(drope * cos_ref[...])
        dkrot_ref[...] = _bf(drope * sin_ref[...])
        dx, dg = _rms_bwd(dcn, xh, r, g)
        dckv_ref[...] = _bf(dx)
        dg_ref[...] += dg

    aspec = pl.BlockSpec((MLA_KVR, hw), lambda b, i: (0, 0))
    tspec = pl.BlockSpec((tok, LANE), lambda b, i: (i, 0))
    ospec = pl.BlockSpec((tok, LANE), lambda b, i: (b * nb + i, 0))
    return pl.pallas_call(
        body, name="mla_kv_up_bwd", grid=(bsz, nb),
        in_specs=[pl.BlockSpec((tok, MLA_HEADS * QKW), lambda b, i: (b * nb + i, 0)),
                  pl.BlockSpec((tok, hw), lambda b, i: (b * nb + i, 0)),
                  pl.BlockSpec((tok, LANE), lambda b, i: (b * nb + i, C_CKV // LANE)),
                  pl.BlockSpec((1, MLA_KVR), lambda b, i: (0, 0)), aspec, aspec, tspec, tspec, ospec,
                  pl.BlockSpec(memory_space=pl.ANY)],
        out_specs=[pl.BlockSpec((tok, 4 * LANE), lambda b, i: (b * nb + i, C_CKV // (4 * LANE))), aspec, aspec,
                   pl.BlockSpec((1, MLA_KVR), lambda b, i: (0, 0))],
        out_shape=[jax.ShapeDtypeStruct((tp, N_EXT), BF16)] + [jax.ShapeDtypeStruct((MLA_KVR, hw), F32)] * 2
        + [jax.ShapeDtypeStruct((1, MLA_KVR), F32)],
        input_output_aliases={9: 0},
        compiler_params=_cp(("arbitrary", "arbitrary")),
    )(dkf, dvf, proj, kv_norm_g, wk, wv, cos_t, sin_t, d_lr, dproj)


def _mid_fwd(ya_in, yb_in, proj, hp, target, w_gp, w_mp, w_o, final_g, bsz, lp):
    tp = bsz * lp
    tm = _wide_block(lp)
    nb = lp // tm
    last = pl.cdiv(lp - X0, tm) - 1

    def body(ya_ref, yb_ref, gg_ref, gm_ref, h_ref, ta_ref, tb_ref, wgp_ref, wmp_ref, wo_ref, fg_ref,
             ya_out, yb_out, dh_ref, loss_ref, dfg_ref):
        @pl.when(jnp.logical_and(pl.program_id(0) == 0, pl.program_id(1) == 0))
        def _():
            loss_ref[...] = jnp.zeros_like(loss_ref)
            dfg_ref[...] = jnp.zeros_like(dfg_ref)

        y_a = _dot(ya_ref[...], wgp_ref[...])
        y_b = _dot(yb_ref[...], wmp_ref[...])
        ya_out[...] = _bf(y_a)
        yb_out[...] = _bf(y_b)
        merged = _sigmoid(gg_ref[...].astype(F32)) * y_a + _sigmoid(gm_ref[...].astype(F32)) * y_b
        h2 = h_ref[...] + _dot(_bf(merged), wo_ref[...])
        fg = fg_ref[...]
        xh, r = _rms_fwd(h2)
        pos = pl.program_id(1) * tm + lax.broadcasted_iota(jnp.int32, (tm, 1), 0)
        t = jnp.concatenate([ta_ref[0, tm - X0:, :], tb_ref[0, :tm - X0, :]], axis=0)
        err = jnp.where(pos >= X0, xh * fg - t, 0.0)
        loss_ref[...] += 0.5 * jnp.sum(jnp.mean(err * err, axis=-1, keepdims=True), axis=0, keepdims=True)
        dy = err * (1.0 / D_MODEL)
        dx, dfg = _rms_bwd(dy, xh, r, fg)
        dh_ref[...] = dx
        dfg_ref[...] += dfg

    tok = lambda c: pl.BlockSpec((tm, D_MODEL), lambda b, i: (b * nb + i, c))
    wspec = pl.BlockSpec((D_MODEL, D_MODEL), lambda b, i: (0, 0), pipeline_mode=pl.Buffered(1))
    return pl.pallas_call(
        body, name="mid_fwd", grid=(bsz, nb),
        in_specs=[tok(0), tok(0), tok(C_GG // D_MODEL), tok(C_GM // D_MODEL), tok(0),
                  pl.BlockSpec((1, tm, D_MODEL), lambda b, i: (b, jnp.maximum(i - 1, 0), 0)),
                  pl.BlockSpec((1, tm, D_MODEL), lambda b, i: (b, jnp.minimum(i, last), 0)),
                  wspec, wspec, wspec, pl.BlockSpec((1, D_MODEL), lambda b, i: (0, 0))],
        out_specs=[tok(0), tok(0), tok(0), pl.BlockSpec((1, LANE), lambda b, i: (0, 0)),
                   pl.BlockSpec((1, D_MODEL), lambda b, i: (0, 0))],
        out_shape=[jax.ShapeDtypeStruct((tp, D_MODEL), BF16), jax.ShapeDtypeStruct((tp, D_MODEL), BF16),
                   jax.ShapeDtypeStruct((tp, D_MODEL), F32), jax.ShapeDtypeStruct((1, LANE), F32),
                   jax.ShapeDtypeStruct((1, D_MODEL), F32)],
        compiler_params=_cp(("arbitrary", "arbitrary"), 56),
    )(ya_in, yb_in, proj, proj, hp, target, target, w_gp, w_mp, w_o, final_g)


def _mid_bwd(dh2, y_a, y_b, proj, ya_in, yb_in, o_b, w_o, w_gp, w_mp, bsz, lp):
    tp = bsz * lp
    tm = MXU_DEPTH if tp % MXU_DEPTH == 0 else _attn_block(lp)
    nsteps = tp // tm
    group = 3 * D_MODEL

    def body(dh_ref, ya_ref, yb_ref, mz_ref, gg_ref, gm_ref, yai_ref, ybi_ref, ob_ref, wo_ref, wgp_ref, wmp_ref,
             dyai_ref, do_ref, dp_ref, dl_ref, dwo_ref, dwgp_ref, dwmp_ref, a_o, a_gp, a_mp):
        @pl.when(pl.program_id(0) == 0)
        def _():
            for r in (a_o, a_gp, a_mp):
                r[...] = jnp.zeros_like(r)

        dh = _bf(dh_ref[...])
        dm = _dot_nt(dh, wo_ref[...])
        y_a, y_b = ya_ref[...].astype(F32), yb_ref[...].astype(F32)
        sg, sm = _sigmoid(gg_ref[...].astype(F32)), _sigmoid(gm_ref[...].astype(F32))
        d_ya, d_yb = _bf(sg * dm), _bf(sm * dm)
        dp_ref[:, D_MODEL:2 * D_MODEL] = _bf(dm * y_a * sg * (1.0 - sg))
        dp_ref[:, 2 * D_MODEL:] = _bf(dm * y_b * sm * (1.0 - sm))
        merged = _bf(sg * y_a + sm * y_b)
        dy = _dot_nt(d_yb, wmp_ref[...])
        dyai_ref[...] = _bf(_dot_nt(d_ya, wgp_ref[...]))
        a_o[...] += _dot_tn(merged, dh)
        a_gp[...] += _dot_tn(yai_ref[...], d_ya)
        a_mp[...] += _dot_tn(ybi_ref[...], d_yb)
        mz, o = mz_ref[...].astype(F32), ob_ref[...].astype(F32)
        s = _sigmoid(mz)
        do = _bf(dy * (mz * s))
        do_ref[...] = do
        dp_ref[:, :D_MODEL] = _bf(dy * o * (s * (1.0 + mz * (1.0 - s))))
        prod = do.astype(F32) * o
        for h in range(MLA_HEADS):
            dl = jnp.sum(prod[:, h * MLA_DV:(h + 1) * MLA_DV], axis=-1, keepdims=True)
            dl_ref[h] = jnp.broadcast_to(dl, (tm, LANE))

        @pl.when(pl.program_id(0) == nsteps - 1)
        def _():
            pltpu.sync_copy(a_o, dwo_ref)
            pltpu.sync_copy(a_gp, dwgp_ref)
            pltpu.sync_copy(a_mp, dwmp_ref)

    tok = lambda c: pl.BlockSpec((tm, D_MODEL), lambda i: (i, c))
    wspec = pl.BlockSpec((D_MODEL, D_MODEL), lambda i: (0, 0))
    anyspec = pl.BlockSpec(memory_space=pl.ANY)
    wshape = jax.ShapeDtypeStruct((D_MODEL, D_MODEL), F32)
    return pl.pallas_call(
        body, name="mid_bwd", grid=(nsteps,),
        in_specs=[tok(0), tok(0), tok(0), tok(C_MZ // D_MODEL), tok(C_GG // D_MODEL), tok(C_GM // D_MODEL),
                  tok(0), tok(0), tok(0), wspec, wspec, wspec],
        out_specs=[tok(0), tok(0), pl.BlockSpec((tm, group), lambda i: (i, C_MZ // group)),
                   pl.BlockSpec((MLA_HEADS, tm, LANE), lambda i: (0, i, 0)), anyspec, anyspec, anyspec],
        out_shape=[jax.ShapeDtypeStruct((tp, D_MODEL), BF16)] * 2 + [jax.ShapeDtypeStruct((tp, N_EXT), BF16),
                   jax.ShapeDtypeStruct((MLA_HEADS, tp, LANE), F32)] + [wshape] * 3,
        scratch_shapes=[pltpu.VMEM((D_MODEL, D_MODEL), F32)] * 3,
        compiler_params=_cp(("arbitrary",), 56),
    )(dh2, y_a, y_b, proj, proj, proj, ya_in, yb_in, o_b, w_o, w_gp, w_mp)


MESH_ID = pl.DeviceIdType.MESH
EXCHANGE_SEMS = [pltpu.SemaphoreType.DMA((N_DEV - 1,)), pltpu.SemaphoreType.DMA((N_DEV - 1,)), pltpu.SemaphoreType.DMA]


def _my_place():
    return lax.axis_index("x"), lax.axis_index("y"), lax.axis_index("c")


def _exchange(g_ref, recv_ref, send_sems, recv_sems, local_sem, start, same=False):
    x, y, c = _my_place()
    me = 4 * x + 2 * y + c
    own = pltpu.make_async_copy(g_ref if same else g_ref.at[me], recv_ref.at[me], local_sem)
    sends, lands = [], []
    for d in range(1, N_DEV):
        px = 1 - x if d & 4 else x
        py = 1 - y if d & 2 else y
        pc = 1 - c if d & 1 else c
        peer = 4 * px + 2 * py + pc
        for slot, group in ((me, sends),) if start else ((me, sends), (peer, lands)):
            group.append(pltpu.make_async_remote_copy(
                src_ref=g_ref if same else g_ref.at[peer], dst_ref=recv_ref.at[slot], send_sem=send_sems.at[d - 1],
                recv_sem=recv_sems.at[d - 1], device_id=(px, py, pc), device_id_type=MESH_ID))
    if start:
        own.start()
        for cp in sends:
            cp.start()
    else:
        for cp in lands:
            cp.wait_recv()
        for cp in sends:
            cp.wait_send()
        own.wait()


def _dw_in(u, dproj, slabs):
    tp = u.shape[0]
    tn = 3 * LANE
    nj = N_EXT // tn

    def body(u_ref, d_ref, g_ref, o_ref, recv_ref, send_sems, recv_sems, local_sem):
        j = pl.program_id(0)

        @pl.when(j == 0)
        def _():
            _exchange(g_ref, recv_ref, send_sems, recv_sems, local_sem, True)

        o_ref[...] = _bf(_dot_tn(d_ref[...], u_ref[...]))

        @pl.when(j == nj - 1)
        def _():
            _exchange(g_ref, recv_ref, send_sems, recv_sems, local_sem, False)

    anyspec = pl.BlockSpec(memory_space=pl.ANY)
    return pl.pallas_call(
        body, name="dw_in", grid=(nj,),
        in_specs=[pl.BlockSpec((tp, D_MODEL), lambda j: (0, 0), pipeline_mode=pl.Buffered(1)),
                  pl.BlockSpec((tp, tn), lambda j: (0, j)), anyspec],
        out_specs=[pl.BlockSpec((tn, D_MODEL), lambda j: (j, 0)), anyspec],
        out_shape=[jax.ShapeDtypeStruct((N_EXT, D_MODEL), BF16), jax.ShapeDtypeStruct(slabs.shape, slabs.dtype)],
        scratch_shapes=EXCHANGE_SEMS,
        compiler_params=_cp(("arbitrary",), 56),
    )(u, dproj, slabs)


def _dx_in(dproj, w_ext, hp, dh2, norm_g, slabs, bsz):
    tp = hp.shape[0]
    lp = tp // bsz
    tm = _attn_block(lp)
    ni = lp // tm
    steps = bsz * ni
    assert ni > 1 and tm > X0

    def body(d_ref, w_ref, h_ref, dh_ref, g_ref, s_ref, gx_ref, dg_ref, dm_ref, recv_ref,
             stage, out_sems, send_sems, recv_sems, local_sem):
        s = pl.program_id(0)
        slot = s % 2

        def out_copy(step, head):
            b, at = step // ni, step % 2
            if head:
                return pltpu.make_async_copy(stage.at[at, pl.ds(X0, tm - X0)], gx_ref.at[b, pl.ds(0, tm - X0)],
                                             out_sems.at[at])
            first = pl.multiple_of((step % ni) * tm - X0, 8)
            return pltpu.make_async_copy(stage.at[at], gx_ref.at[b, pl.ds(first, tm)], out_sems.at[at])

        @pl.when(s == 0)
        def _():
            _exchange(s_ref, recv_ref, send_sems, recv_sems, local_sem, True)
            dg_ref[...] = jnp.zeros_like(dg_ref)
            dm_ref[...] = jnp.zeros_like(dm_ref)

        du = _dot_nt(d_ref[...], w_ref[...])
        g = g_ref[...]
        xh, r = _rms_fwd(h_ref[...])
        dx, dg = _rms_bwd(du, xh, r, g)
        dg_ref[...] += dg
        stage[slot] = dh_ref[...] + dx
        head = s % ni == 0

        @pl.when(head)
        def _():
            dm_ref[...] += stage[slot, pl.ds(FRONT, N_META), :]
            out_copy(s, True).start()

        @pl.when(jnp.logical_not(head))
        def _():
            out_copy(s, False).start()

        @pl.when(s % ni == 1)
        def _():
            out_copy(s - 1, True).wait()

        @pl.when(s % ni > 1)
        def _():
            out_copy(s - 1, False).wait()

        @pl.when(jnp.logical_and(head, s > 0))
        def _():
            out_copy(s - 1, False).wait()

        @pl.when(s == steps - 1)
        def _():
            out_copy(s, False).wait()
            _exchange(s_ref, recv_ref, send_sems, recv_sems, local_sem, False)

    tok = pl.BlockSpec((tm, D_MODEL), lambda s: (s, 0))
    anyspec = pl.BlockSpec(memory_space=pl.ANY)
    return pl.pallas_call(
        body, name="dx_in", grid=(steps,),
        in_specs=[pl.BlockSpec((tm, N_EXT), lambda s: (s, 0)),
                  pl.BlockSpec((D_MODEL, N_EXT), lambda s: (0, 0), pipeline_mode=pl.Buffered(1)),
                  tok, tok, pl.BlockSpec((1, D_MODEL), lambda s: (0, 0)), anyspec],
        out_specs=[anyspec, pl.BlockSpec((1, D_MODEL), lambda s: (0, 0)),
                   pl.BlockSpec((N_META, D_MODEL), lambda s: (0, 0)), anyspec],
        out_shape=[jax.ShapeDtypeStruct((bsz, lp - X0, D_MODEL), F32), jax.ShapeDtypeStruct((1, D_MODEL), F32),
                   jax.ShapeDtypeStruct((N_META, D_MODEL), F32), jax.ShapeDtypeStruct(slabs.shape, slabs.dtype)],
        scratch_shapes=[pltpu.VMEM((2, tm, D_MODEL), F32), pltpu.SemaphoreType.DMA((2,))] + EXCHANGE_SEMS,
        compiler_params=_cp(("arbitrary",), 56),
    )(dproj, w_ext, hp, dh2, norm_g, slabs)


W_IN_SHARD = N_IN // N_DEV


def _pad_lanes(a, width=LANE):
    return jnp.pad(a, [(0, 0)] * (a.ndim - 1) + [(0, width - a.shape[-1])])


def _rot_cols(w):
    half = w.shape[-1] // 2
    return jnp.concatenate([-w[..., half:], w[..., :half]], axis=-1)


def _unrot_cols(dw):
    half = dw.shape[-1] // 2
    return jnp.concatenate([dw[..., half:], -dw[..., :half]], axis=-1)


def _w_in_cols(shards, lo, hi):
    parts = []
    for k in range(lo // W_IN_SHARD, (hi - 1) // W_IN_SHARD + 1):
        a, b = max(lo, k * W_IN_SHARD), min(hi, (k + 1) * W_IN_SHARD)
        parts.append(shards[k][:, a - k * W_IN_SHARD:b - k * W_IN_SHARD])
    return parts[0] if len(parts) == 1 else jnp.concatenate(parts, axis=1)


def _w_in_ext(shards):
    c = lambda lo, hi: _w_in_cols(shards, lo, hi)
    kr = c(O_KR, O_MZ)
    return jnp.concatenate([
        c(O_V, O_LR), c(O_Z, O_CQ), c(O_Q, O_K), c(O_K, O_V), c(O_MZ, O_GG), c(O_GG, O_GM), c(O_GM, N_IN),
        c(O_CKV, O_KR), _pad_lanes(kr), _pad_lanes(_rot_cols(kr)), _pad_lanes(c(O_LR, O_Z)), c(O_CQ, O_CKV)], axis=1)


def _w_in_slabs(dwt):
    half = MLA_ROPE // 2
    krot = dwt[C_KROT:C_KROT + MLA_ROPE]
    kr = dwt[C_KR:C_KR + MLA_ROPE] + jnp.concatenate([krot[half:], -krot[:half]], axis=0)
    groups = ((O_Q, GLA_KW, C_Q), (O_K, GLA_KW, C_K), (O_V, GLA_VW, C_V), (O_LR, GLA_RANK, C_LR), (O_Z, GLA_VW, C_Z),
              (O_CQ, MLA_QR, C_CQ), (O_CKV, MLA_KVR, C_CKV), (O_KR, MLA_ROPE, None), (O_MZ, D_MODEL, C_MZ),
              (O_GG, D_MODEL, C_GG), (O_GM, D_MODEL, C_GM))
    slabs = []
    for k in range(N_DEV):
        lo, hi = k * W_IN_SHARD, (k + 1) * W_IN_SHARD
        parts = []
        for first, width, row in groups:
            a, b = max(lo, first), min(hi, first + width)
            if a < b:
                parts.append(kr[a - first:b - first] if row is None else dwt[row + a - first:row + b - first])
        slabs.append(jnp.concatenate(parts, axis=0))
    return jnp.stack(slabs)


def _rope_tables(lp):
    inv = 1.0 / (ROPE_BASE ** (jnp.arange(0, MLA_ROPE, 2, dtype=F32) / MLA_ROPE))
    ang = (jnp.arange(lp, dtype=F32) - FRONT)[:, None] * inv[None, :]
    cos, sin = jnp.cos(ang), jnp.sin(ang)
    return _pad_lanes(jnp.concatenate([cos, cos], axis=1)), _pad_lanes(jnp.concatenate([sin, sin], axis=1))


def _local_step(x, loss_target, w):
    bsz, seq, _ = x.shape
    lp = X0 + seq
    tp = bsz * lp
    assert lp % TOK == 0 and (lp // GLA_CHUNK) % _gla_group(lp // GLA_CHUNK) == 0
    head = jnp.concatenate([jnp.zeros((FRONT, D_MODEL), F32), w["meta_tokens"]], axis=0)
    cos_t, sin_t = _rope_tables(lp)

    w_ext = _w_in_ext(w["w_in"])
    hp, u, proj, packed_all = _proj_in(x, head, w["norm_g"], w_ext, w["packed"])
    gathered = _unpack_shards(packed_all)
    for n, _, axis in PACKED:
        w[n] = _join8(gathered[n], axis)
    gw_pad = jnp.pad(w["gla_gate_w"], ((0, LANE - GLA_RANK), (0, 0)))
    uq = w["mla_w_uq"].reshape(MLA_QR, MLA_HEADS, MLA_QK)
    rope_w = uq[:, :, MLA_NOPE:]
    hw = MLA_HEADS * LANE
    wn = uq[:, :, :MLA_NOPE].reshape(MLA_QR, hw)
    wr = _pad_lanes(rope_w).reshape(MLA_QR, hw)
    wt = _pad_lanes(_rot_cols(rope_w)).reshape(MLA_QR, hw)
    ukv = w["mla_w_ukv"].reshape(MLA_KVR, MLA_HEADS, MLA_NOPE + MLA_DV)
    wk = ukv[:, :, :MLA_NOPE].reshape(MLA_KVR, hw)
    wv = ukv[:, :, MLA_NOPE:].reshape(MLA_KVR, hw)

    o_raw, ya_in, s_all = _gla_fwd(proj, gw_pad, w["gla_gate_b"], w["gla_norm_g"], bsz, lp)
    qf = _q_up(proj, w["mla_q_norm_g"], wn, wr, wt, cos_t, sin_t, bsz, lp)
    kf, vf = _kv_up(proj, w["mla_kv_norm_g"], wk, wv, cos_t, sin_t, bsz, lp)
    o_b, yb_in, lse = _attn_fwd(qf, kf, vf, proj, bsz, lp)
    y_a, y_b, dh2, loss, d_final_g = _mid_fwd(ya_in, yb_in, proj, hp, loss_target, w["gla_proj"], w["mla_proj"],
                                              w["w_out"], w["final_norm_g"], bsz, lp)
    d_ya, d_o, dproj, delta, d_w_out, d_gla_proj, d_mla_proj = _mid_bwd(
        dh2, y_a, y_b, proj, ya_in, yb_in, o_b, w["w_out"], w["gla_proj"], w["mla_proj"], bsz, lp)
    dproj, d_gate, d_gla_norm = _gla_bwd(proj, gw_pad, w["gla_gate_b"], w["gla_norm_g"], o_raw, s_all, d_ya, dproj,
                                         bsz, lp)
    d_lr, d_gw_pad, d_gate_b = _gate_bwd(d_gate, proj, gw_pad)
    dqf, dkf, dvf = _attn_bwd(qf, kf, vf, d_o, lse, delta, bsz, lp)
    dproj, d_wn, d_wr, d_wt, d_qn = _q_up_bwd(dqf, proj, w["mla_q_norm_g"], wn, wr, wt, cos_t, sin_t, dproj,
                                              bsz, lp)
    dproj, d_wk, d_wv, d_kvn = _kv_up_bwd(dkf, dvf, proj, w["mla_kv_norm_g"], wk, wv, cos_t, sin_t, d_lr, dproj,
                                          bsz, lp)

    d_rope = (d_wr.reshape(MLA_QR, MLA_HEADS, LANE)[:, :, :MLA_ROPE]
              + _unrot_cols(d_wt.reshape(MLA_QR, MLA_HEADS, LANE)[:, :, :MLA_ROPE]))
    d_uq = jnp.concatenate([d_wn.reshape(MLA_QR, MLA_HEADS, LANE), d_rope], axis=-1).reshape(MLA_QR, MLA_HEADS * MLA_QK)
    d_ukv = jnp.concatenate([d_wk.reshape(MLA_KVR, MLA_HEADS, LANE), d_wv.reshape(MLA_KVR, MLA_HEADS, LANE)],
                            axis=-1).reshape(MLA_KVR, MLA_HEADS * (MLA_NOPE + MLA_DV))
    mats = dict(gla_gate_w=d_gw_pad[:GLA_RANK], gla_proj=d_gla_proj, mla_w_uq=d_uq, mla_w_ukv=d_ukv,
                mla_proj=d_mla_proj, w_out=d_w_out)
    packed = _pack_shards({n: _bf(_split8(mats[n], axis)) for n, _, axis in PACKED})
    d_w_ext_t, packed_parts = _dw_in(u, dproj, packed)
    w_in_slabs = _w_in_slabs(d_w_ext_t)
    grad_x, d_norm_g, d_meta, w_in_parts = _dx_in(dproj, w_ext, hp, dh2, w["norm_g"], w_in_slabs, bsz)
    small = dict(meta_tokens=d_meta, norm_g=d_norm_g, gla_gate_b=d_gate_b, gla_norm_g=d_gla_norm,
                 mla_q_norm_g=d_qn, mla_kv_norm_g=d_kvn, final_norm_g=d_final_g)
    return loss, grad_x, w_in_parts, packed_parts, small


PACKED = (("gla_gate_w", (GLA_RANK, GLA_KW // N_DEV), 1),
          ("gla_proj", (D_MODEL // N_DEV, D_MODEL), 0), ("mla_w_uq", (MLA_QR, MLA_HEADS * MLA_QK // N_DEV), 1),
          ("mla_w_ukv", (MLA_KVR, MLA_HEADS * (MLA_NOPE + MLA_DV) // N_DEV), 1),
          ("mla_proj", (D_MODEL // N_DEV, D_MODEL), 0), ("w_out", (D_MODEL // N_DEV, D_MODEL), 0))
REPLICATED = (("norm_g", D_MODEL), ("gla_gate_b", GLA_KW), ("gla_norm_g", GLA_DV), ("mla_q_norm_g", MLA_QR),
              ("mla_kv_norm_g", MLA_KVR), ("final_norm_g", D_MODEL))
PACK_ROWS = 480
PACK_BLOCK = 160
SMALL_ROWS = 48
LOSS_ROW = N_META + 25
W_IN_BLOCK = 128


def _all_gather(shards):
    n_arr = len(shards)
    pieces = []
    for a, s in enumerate(shards):
        step = s.shape[0] // 4 if s.shape[0] >= 4 * LANE else s.shape[0]
        pieces += [(a, slice(r, r + step)) for r in range(0, s.shape[0], step)]
    n_pc = len(pieces)

    def body(*refs):
        x_refs, out_refs = refs[:n_arr], refs[n_arr:2 * n_arr]
        send_sems, recv_sems, local_sems = refs[2 * n_arr:]
        x, y, c = _my_place()
        me, sibling = (x, y, c), (x, y, 1 - c)
        chips = [(1 - x, y), (x, 1 - y), (1 - x, 1 - y)]

        def copy(u, k, block, to, from_input=False):
            a, rows = pieces[u]
            slab = out_refs[a].at[4 * block[0] + 2 * block[1] + block[2], rows]
            return pltpu.make_async_remote_copy(
                src_ref=x_refs[a].at[rows] if from_input else slab, dst_ref=slab,
                send_sem=send_sems.at[7 * u + k], recv_sem=recv_sems.at[7 * u + k], device_id=to,
                device_id_type=MESH_ID)

        arrays = range(n_pc)
        mine = [pltpu.make_async_copy(x_refs[a], out_refs[a].at[4 * x + 2 * y + c], local_sems.at[a])
                for a in range(n_arr)]
        for cp in mine:
            cp.start()
        first = [copy(a, 0, me, sibling, True) for a in arrays]
        first += [copy(a, 1 + j, me, (*chip, c), True) for j, chip in enumerate(chips) for a in arrays]
        for cp in first:
            cp.start()
        passed = []
        for j, chip in enumerate(chips):
            for a in arrays:
                copy(a, 1 + j, (*chip, c), me).wait_recv()
                passed.append(copy(a, 4 + j, (*chip, c), sibling))
                passed[-1].start()
        for a in arrays:
            copy(a, 0, sibling, me).wait_recv()
        for j, chip in enumerate(chips):
            for a in arrays:
                copy(a, 4 + j, (*chip, 1 - c), me).wait_recv()
        for cp in first + passed:
            cp.wait_send()
        for cp in mine:
            cp.wait()

    anyspec = pl.BlockSpec(memory_space=pl.ANY)
    return pl.pallas_call(
        body, name="weights_all_gather",
        out_shape=[jax.ShapeDtypeStruct((N_DEV,) + s.shape, s.dtype) for s in shards],
        in_specs=[anyspec] * n_arr, out_specs=[anyspec] * n_arr,
        scratch_shapes=[pltpu.SemaphoreType.DMA((7 * n_pc,)), pltpu.SemaphoreType.DMA((7 * n_pc,)),
                        pltpu.SemaphoreType.DMA((n_arr,))],
    )(*shards)


def _adamw(parts, w, m, v, block_rows, name, beside=None):
    rows, cols = w.shape
    steps = pl.cdiv(rows, block_rows)

    def update(p_ref, w_ref, m_ref, v_ref, g_out, d_out, m_out, v_out):
        g = p_ref[0].astype(F32)
        for s in range(1, N_DEV):
            g = g + p_ref[s].astype(F32)
        m_new = ADAM_B1 * m_ref[...] + (1.0 - ADAM_B1) * g
        v_new = ADAM_B2 * v_ref[...] + (1.0 - ADAM_B2) * (g * g)
        m_hat = m_new / (1.0 - ADAM_B1 ** ADAM_STEP)
        v_hat = v_new / (1.0 - ADAM_B2 ** ADAM_STEP)
        g_out[...] = g
        d_out[...] = -ADAM_LR * (m_hat / (jnp.sqrt(v_hat) + ADAM_EPS) + ADAM_WD * w_ref[...])
        m_out[...] = m_new
        v_out[...] = v_new

    def update_beside_exchange(p_ref, w_ref, m_ref, v_ref, s_ref, g_out, d_out, m_out, v_out, recv_ref,
                               send_sems, recv_sems, local_sem):
        @pl.when(pl.program_id(0) == 0)
        def _():
            _exchange(s_ref, recv_ref, send_sems, recv_sems, local_sem, True)

        update(p_ref, w_ref, m_ref, v_ref, g_out, d_out, m_out, v_out)

        @pl.when(pl.program_id(0) == steps - 1)
        def _():
            _exchange(s_ref, recv_ref, send_sems, recv_sems, local_sem, False)

    spec = pl.BlockSpec((block_rows, cols), lambda i: (i, 0))
    anyspec = pl.BlockSpec(memory_space=pl.ANY)
    in_specs = [pl.BlockSpec((N_DEV, block_rows, cols), lambda i: (0, i, 0)), spec, spec, spec]
    out_shape = [jax.ShapeDtypeStruct((rows, cols), F32)] * 4
    if beside is None:
        return pl.pallas_call(
            update, name=name, grid=(steps,),
            in_specs=in_specs, out_specs=[spec] * 4, out_shape=out_shape,
            compiler_params=_cp(("parallel",), 48),
        )(parts, w, m, v)
    return pl.pallas_call(
        update_beside_exchange, name=name, grid=(steps,),
        in_specs=in_specs + [anyspec], out_specs=[spec] * 4 + [anyspec],
        out_shape=out_shape + [jax.ShapeDtypeStruct(beside.shape, beside.dtype)],
        scratch_shapes=EXCHANGE_SEMS,
        compiler_params=_cp(("arbitrary",), 48),
    )(parts, w, m, v, beside)


def _pack_rows_of(shape):
    rows = shape[0] * shape[1] // D_MODEL
    return -(-rows // 16) * 16


def _pack_shards(shards):
    parts = []
    for n, shape, _ in PACKED:
        a = shards[n]
        lead = a.shape[:-2]
        if shape[1] != D_MODEL:
            a = a.reshape(lead + (shape[0] * shape[1] // D_MODEL, D_MODEL))
        pad = _pack_rows_of(shape) - a.shape[-2]
        parts.append(jnp.pad(a, [(0, 0)] * len(lead) + [(0, pad), (0, 0)]) if pad else a)
    return jnp.concatenate(parts, axis=-2)


def _unpack_shards(packed):
    lead, out, off = packed.shape[:-2], {}, 0
    for n, shape, _ in PACKED:
        rows = shape[0] * shape[1] // D_MODEL
        out[n] = packed[..., off:off + rows, :].reshape(lead + shape)
        off += _pack_rows_of(shape)
    return out


def _split8(full, axis):
    r, c = full.shape
    if axis == 0:
        return full.reshape(N_DEV, r // N_DEV, c)
    return full.reshape(r, N_DEV, c // N_DEV).transpose(1, 0, 2)


def _join8(shards, axis):
    _, r, c = shards.shape
    if axis == 0:
        return shards.reshape(N_DEV * r, c)
    return shards.transpose(1, 0, 2).reshape(r, N_DEV * c)


def _pack_small(meta_shard, vals, loss_row):
    rows = jnp.concatenate([vals[n].reshape(-1, LANE) for n, _ in REPLICATED] + [loss_row], axis=0)
    rows = jnp.pad(rows, ((0, SMALL_ROWS - N_META - rows.shape[0]), (0, 0)))
    return jnp.concatenate([meta_shard, jnp.broadcast_to(rows, meta_shard.shape[:-2] + rows.shape)], axis=-2)


def _unpack_small(packed):
    out, off = {"meta_tokens": packed[:N_META]}, N_META
    for n, size in REPLICATED:
        out[n] = packed[off:off + size // LANE].reshape(1, size)
        off += size // LANE
    return out


def kernel(x, meta_tokens, norm_g, w_in, gla_gate_w, gla_gate_b, gla_norm_g, gla_proj, mla_q_norm_g, mla_w_uq, mla_kv_norm_g, mla_w_ukv, mla_proj, w_out, final_norm_g, loss_target, m_meta_tokens, m_norm_g, m_w_in, m_gla_gate_w, m_gla_gate_b, m_gla_norm_g, m_gla_proj, m_mla_q_norm_g, m_mla_w_uq, m_mla_kv_norm_g, m_mla_w_ukv, m_mla_proj, m_w_out, m_final_norm_g, v_meta_tokens, v_norm_g, v_w_in, v_gla_gate_w, v_gla_gate_b, v_gla_norm_g, v_gla_proj, v_mla_q_norm_g, v_mla_w_uq, v_mla_kv_norm_g, v_mla_w_ukv, v_mla_proj, v_w_out, v_final_norm_g):
    given = dict(meta_tokens=meta_tokens, norm_g=norm_g, w_in=w_in, gla_gate_w=gla_gate_w, gla_gate_b=gla_gate_b,
                 gla_norm_g=gla_norm_g, gla_proj=gla_proj, mla_q_norm_g=mla_q_norm_g, mla_w_uq=mla_w_uq,
                 mla_kv_norm_g=mla_kv_norm_g, mla_w_ukv=mla_w_ukv, mla_proj=mla_proj, w_out=w_out,
                 final_norm_g=final_norm_g)
    mom_m = dict(meta_tokens=m_meta_tokens, norm_g=m_norm_g, w_in=m_w_in, gla_gate_w=m_gla_gate_w,
                 gla_gate_b=m_gla_gate_b, gla_norm_g=m_gla_norm_g, gla_proj=m_gla_proj, mla_q_norm_g=m_mla_q_norm_g,
                 mla_w_uq=m_mla_w_uq, mla_kv_norm_g=m_mla_kv_norm_g, mla_w_ukv=m_mla_w_ukv, mla_proj=m_mla_proj,
                 w_out=m_w_out, final_norm_g=m_final_norm_g)
    mom_v = dict(meta_tokens=v_meta_tokens, norm_g=v_norm_g, w_in=v_w_in, gla_gate_w=v_gla_gate_w,
                 gla_gate_b=v_gla_gate_b, gla_norm_g=v_gla_norm_g, gla_proj=v_gla_proj, mla_q_norm_g=v_mla_q_norm_g,
                 mla_w_uq=v_mla_w_uq, mla_kv_norm_g=v_mla_kv_norm_g, mla_w_ukv=v_mla_w_ukv, mla_proj=v_mla_proj,
                 w_out=v_w_out, final_norm_g=v_final_norm_g)
    shapes = {n: a.shape for n, a in given.items()}
    shard2d = {n: s for n, s, _ in PACKED}
    shard2d["w_in"] = (D_MODEL, W_IN_SHARD)
    shard2d["meta_tokens"] = (N_META, LANE)

    def as2d(tree):
        out = {n: tree[n].reshape(shard2d[n]) for n in shard2d}
        out.update({n: tree[n].reshape(1, size) for n, size in REPLICATED})
        return out

    w_loc, m_loc, v_loc = as2d(given), as2d(mom_m), as2d(mom_v)

    w_in_all, meta_all = _all_gather([w_loc["w_in"].astype(BF16), w_loc["meta_tokens"]])
    packed = _pack_shards({n: w_loc[n].astype(BF16) for n, _, _ in PACKED})
    full = {"w_in": w_in_all, "meta_tokens": _join8(meta_all, 1), "packed": packed}
    for n, _ in REPLICATED:
        full[n] = w_loc[n]

    loss_part, grad_x, w_in_parts, packed_parts, small = _local_step(x, loss_target, full)
    small_slabs = _pack_small(_split8(small["meta_tokens"], 1), small, jnp.broadcast_to(loss_part[:, :1], (1, LANE)))

    w_in_t = [t["w_in"].T for t in (w_loc, m_loc, v_loc)]
    *w_in_new, small_all = _adamw(w_in_parts, *w_in_t, W_IN_BLOCK, "adamw_w_in", beside=small_slabs)
    g_w, d_w, m_w, v_w = (o.T for o in w_in_new)
    g_p, d_p, m_p, v_p = _adamw(packed_parts, _pack_shards(w_loc), _pack_shards(m_loc), _pack_shards(v_loc),
                                PACK_BLOCK, "adamw_packed")
    zero_row = jnp.zeros((1, LANE), F32)
    g_s, d_s, m_s, v_s = _adamw(small_all, *(_pack_small(t["meta_tokens"], t, zero_row) for t in (w_loc, m_loc, v_loc)),
                                SMALL_ROWS, "adamw_small")
    loss = g_s[LOSS_ROW, 0]

    order = ["meta_tokens", "norm_g", "w_in", "gla_gate_w", "gla_gate_b", "gla_norm_g", "gla_proj", "mla_q_norm_g",
             "mla_w_uq", "mla_kv_norm_g", "mla_w_ukv", "mla_proj", "w_out", "final_norm_g"]
    result = [loss, grad_x]
    for w_in_out, packed_sh, packed_sm in ((g_w, g_p, g_s), (d_w, d_p, d_s), (m_w, m_p, m_s), (v_w, v_p, v_s)):
        tree = _unpack_shards(packed_sh)
        tree.update(_unpack_small(packed_sm))
        tree["w_in"] = w_in_out
        result += [tree[n].reshape(shapes[n]) for n in order]
    return tuple(result)
```

```python
import jax
import jax.numpy as jnp
from jax import lax
from jax.experimental import pallas as pl
from jax.experimental.pallas import tpu as pltpu

F32 = jnp.float32
BF16 = jnp.bfloat16

D_MODEL = 1024
N_META = 16
EPS = 1e-6
FRONT = 48
X0 = FRONT + N_META
GLA_HEADS, GLA_DK, GLA_DV, GLA_RANK, GLA_CHUNK = 4, 128, 256, 16, 64
GLA_GATE_NORMALIZER = 16.0
GLA_KW = GLA_HEADS * GLA_DK
GLA_VW = GLA_HEADS * GLA_DV
MLA_HEADS, MLA_NOPE, MLA_ROPE, MLA_DV, MLA_QR, MLA_KVR = 8, 128, 64, 128, 256, 128
MLA_QK = MLA_NOPE + MLA_ROPE
ROPE_BASE = 10000.0
LANE = 128
QKW = 2 * LANE

C_V, C_Z, C_Q, C_K = 0, 1024, 2048, 2560
C_MZ, C_GG, C_GM = 3072, 4096, 5120
C_CKV, C_KR, C_KROT, C_LR = 6144, 6272, 6400, 6528
C_CQ = 6656
N_EXT = 6912
O_Q, O_K, O_V, O_LR, O_Z, O_CQ, O_CKV, O_KR, O_MZ, O_GG, O_GM, N_IN = (
    0, 512, 1024, 2048, 2064, 3088, 3344, 3472, 3536, 4560, 5584, 6608)

ADAM_LR, ADAM_B1, ADAM_B2, ADAM_EPS, ADAM_WD, ADAM_STEP = 0.001, 0.9, 0.999, 1e-08, 0.01, 10

N_DEV = 8
TOK = 192
ATT_BLOCK = 352
MXU_DEPTH = 256


def _cp(sems=None, vmem_mb=None):
    kw = {}
    if sems is not None:
        kw["dimension_semantics"] = sems
    if vmem_mb is not None:
        kw["vmem_limit_bytes"] = vmem_mb * 1024 * 1024
    return pltpu.CompilerParams(**kw)


def _dot(a, b):
    return jnp.dot(a, b, preferred_element_type=F32)


def _dot_nt(a, b):
    return lax.dot_general(a, b, (((1,), (1,)), ((), ())), preferred_element_type=F32)


def _dot_tn(a, b):
    return lax.dot_general(a, b, (((0,), (0,)), ((), ())), preferred_element_type=F32)


def _sigmoid(x):
    return 1.0 / (1.0 + jnp.exp(-x))


def _bf(x):
    return x.astype(BF16)


def _big_tok(tp):
    return 4 * TOK if tp % (4 * TOK) == 0 else TOK


def _attn_block(lp):
    return ATT_BLOCK if lp % ATT_BLOCK == 0 else TOK


def _wide_block(lp):
    return 2 * ATT_BLOCK if lp % (2 * ATT_BLOCK) == 0 else _attn_block(lp)


def _proj_in(x, head, norm_g, w_ext, packed):
    bsz, seq, _ = x.shape
    lp = X0 + seq
    tp = bsz * lp
    tm = _attn_block(lp)
    nb = lp // tm
    last = pl.cdiv(seq, tm) - 1

    def body(xa_ref, xb_ref, hd_ref, g_ref, w_ref, p_ref, h_ref, u_ref, o_ref, pall_ref, send_sems, recv_sems, local_sem):
        first = jnp.logical_and(pl.program_id(0) == 0, pl.program_id(1) == 0)

        @pl.when(first)
        def _():
            _exchange(p_ref, pall_ref, send_sems, recv_sems, local_sem, True, same=True)

        front = jnp.where(pl.program_id(1) == 0, hd_ref[...], xa_ref[0, tm - X0:, :])
        h = jnp.concatenate([front, xb_ref[0, :tm - X0, :]], axis=0)
        h_ref[...] = h
        r = lax.rsqrt(jnp.mean(h * h, axis=-1, keepdims=True) + EPS)
        u = _bf(h * r * g_ref[...])
        u_ref[...] = u
        o_ref[...] = _bf(_dot(u, w_ref[...]))

        @pl.when(jnp.logical_and(pl.program_id(0) == bsz - 1, pl.program_id(1) == nb - 1))
        def _():
            _exchange(p_ref, pall_ref, send_sems, recv_sems, local_sem, False, same=True)

    anyspec = pl.BlockSpec(memory_space=pl.ANY)
    tok = lambda width: pl.BlockSpec((tm, width), lambda b, i: (b * nb + i, 0))
    return pl.pallas_call(
        body, name="proj_in", grid=(bsz, nb),
        in_specs=[pl.BlockSpec((1, tm, D_MODEL), lambda b, i: (b, jnp.maximum(i - 1, 0), 0)),
                  pl.BlockSpec((1, tm, D_MODEL), lambda b, i: (b, jnp.minimum(i, last), 0)),
                  pl.BlockSpec((X0, D_MODEL), lambda b, i: (0, 0)),
                  pl.BlockSpec((1, D_MODEL), lambda b, i: (0, 0)),
                  pl.BlockSpec((D_MODEL, N_EXT), lambda b, i: (0, 0), pipeline_mode=pl.Buffered(1)), anyspec],
        out_specs=[tok(D_MODEL), tok(D_MODEL), tok(N_EXT), anyspec],
        out_shape=[jax.ShapeDtypeStruct((tp, D_MODEL), F32), jax.ShapeDtypeStruct((tp, D_MODEL), BF16),
                   jax.ShapeDtypeStruct((tp, N_EXT), BF16),
                   jax.ShapeDtypeStruct((N_DEV,) + packed.shape, packed.dtype)],
        scratch_shapes=EXCHANGE_SEMS,
        compiler_params=_cp(("arbitrary", "arbitrary"), 56),
    )(x, x, head, norm_g, w_ext, packed)


def _gla_group(n_chunks):
    return 11 if n_chunks % 11 == 0 else 3


def _tri_dot(tri, x):
    hi = _bf(x)
    rest = x - hi.astype(F32)
    mid = _bf(rest)
    return _dot(tri, hi) + _dot(tri, mid) + _dot(tri, _bf(rest - mid.astype(F32)))


def _gla_gates(q_ref, k_ref, lr_ref, gw_ref, gb_ref, rows, not_first):
    z = _dot(lr_ref[rows, :], gw_ref[...]) + gb_ref[...]
    logsig = jnp.minimum(z, 0.0) - jnp.log(1.0 + jnp.exp(-jnp.abs(z)))
    row = lax.broadcasted_iota(jnp.int32, (GLA_CHUNK, GLA_KW), 0)
    live = jnp.logical_or(not_first, row >= FRONT)
    g = jnp.where(live, logsig * (1.0 / GLA_GATE_NORMALIZER), 0.0)
    ri = lax.broadcasted_iota(jnp.int32, (GLA_CHUNK, GLA_CHUNK), 0)
    ci = lax.broadcasted_iota(jnp.int32, (GLA_CHUNK, GLA_CHUNK), 1)
    tril = ci <= ri
    b = _tri_dot(_bf(tril.astype(F32)), g)
    bl = jnp.sum(jnp.where(row == GLA_CHUNK - 1, b, 0.0), axis=0, keepdims=True)
    eb, enb, elb, ebl = jnp.exp(b), jnp.exp(-b), jnp.exp(bl - b), jnp.exp(bl)
    q = q_ref[rows, :].astype(F32) * (GLA_DK ** -0.5)
    k = k_ref[rows, :].astype(F32)
    qe, ke, kl = q * eb, k * enb, k * elb
    return dict(z=z, live=live, tril=tril, row=row, eb=eb, enb=enb, elb=elb, ebl=ebl, qe=qe, ke=ke, kl=kl,
                qe_b=_bf(qe), ke_b=_bf(ke), kl_b=_bf(kl))


def _gla_in_specs(n_groups, gla_rows, rev):
    def rb(b, n):
        return b * n_groups + ((n_groups - 1 - n) if rev else n)

    return rb, [pl.BlockSpec((gla_rows, GLA_KW), lambda b, n: (rb(b, n), C_Q // GLA_KW)),
                pl.BlockSpec((gla_rows, GLA_KW), lambda b, n: (rb(b, n), C_K // GLA_KW)),
                pl.BlockSpec((gla_rows, GLA_VW), lambda b, n: (rb(b, n), C_V // GLA_VW)),
                pl.BlockSpec((gla_rows, GLA_VW), lambda b, n: (rb(b, n), C_Z // GLA_VW)),
                pl.BlockSpec((gla_rows, LANE), lambda b, n: (rb(b, n), C_LR // LANE)),
                pl.BlockSpec((LANE, GLA_KW), lambda b, n: (0, 0)),
                pl.BlockSpec((1, GLA_KW), lambda b, n: (0, 0)),
                pl.BlockSpec((1, GLA_DV), lambda b, n: (0, 0))]


def _gla_fwd(proj, gw_pad, gate_b, gla_norm_g, bsz, lp):
    n_chunks = lp // GLA_CHUNK
    gla_group = _gla_group(n_chunks)
    gla_rows = gla_group * GLA_CHUNK
    n_groups = n_chunks // gla_group
    tp = bsz * lp

    def body(q_ref, k_ref, v_ref, z_ref, lr_ref, gw_ref, gb_ref, gn_ref, oraw_ref, ya_ref, sall_ref, st_scr):
        grp = pl.program_id(1)

        @pl.when(grp == 0)
        def _():
            st_scr[...] = jnp.zeros_like(st_scr)

        chunks = [slice(j * GLA_CHUNK, (j + 1) * GLA_CHUNK) for j in range(gla_group)]
        cs = [_gla_gates(q_ref, k_ref, lr_ref, gw_ref, gb_ref, rows, True if j else grp > 0)
              for j, rows in enumerate(chunks)]
        gn = gn_ref[...]
        sts = [st_scr[h] for h in range(GLA_HEADS)]
        heads = [(slice(h * GLA_DK, (h + 1) * GLA_DK), slice(h * GLA_DV, (h + 1) * GLA_DV)) for h in range(GLA_HEADS)]
        a_all = [[_bf(jnp.where(c["tril"], _dot_nt(c["qe_b"][:, ks], c["ke_b"][:, ks]), 0.0)) for ks, _ in heads]
                 for c in cs]
        u_all = [[_dot_tn(v_ref[rows, vs], c["kl_b"][:, ks]) for ks, vs in heads] for rows, c in zip(chunks, cs)]
        for j, (rows, c) in enumerate(zip(chunks, cs)):
            for h, (ks, vs) in enumerate(heads):
                st = sts[h]
                sall_ref[0, j, h] = st
                o = _dot(a_all[j][h], v_ref[rows, vs]) + _dot_nt(c["qe_b"][:, ks], _bf(st))
                sts[h] = st * c["ebl"][:, ks] + u_all[j][h]
                oraw_ref[rows, vs] = o
                r = lax.rsqrt(jnp.mean(o * o, axis=-1, keepdims=True) + EPS)
                zg = z_ref[rows, vs].astype(F32)
                ya_ref[rows, vs] = _bf((o * r * gn) * (zg * _sigmoid(zg)))
        for h in range(GLA_HEADS):
            st_scr[h] = sts[h]

    rb, in_specs = _gla_in_specs(n_groups, gla_rows, False)
    return pl.pallas_call(
        body, name="gla_fwd", grid=(bsz, n_groups), in_specs=in_specs,
        out_specs=[pl.BlockSpec((gla_rows, GLA_VW), lambda b, n: (rb(b, n), 0)),
                   pl.BlockSpec((gla_rows, GLA_VW), lambda b, n: (rb(b, n), 0)),
                   pl.BlockSpec((1, gla_group, GLA_HEADS, GLA_DV, GLA_DK), lambda b, n: (b, n, 0, 0, 0))],
        out_shape=[jax.ShapeDtypeStruct((tp, GLA_VW), F32), jax.ShapeDtypeStruct((tp, GLA_VW), BF16),
                   jax.ShapeDtypeStruct((bsz, n_chunks, GLA_HEADS, GLA_DV, GLA_DK), F32)],
        scratch_shapes=[pltpu.VMEM((GLA_HEADS, GLA_DV, GLA_DK), F32)],
        compiler_params=_cp(("parallel", "arbitrary"), 56),
    )(proj, proj, proj, proj, proj, gw_pad, gate_b, gla_norm_g)


def _gla_bwd(proj, gw_pad, gate_b, gla_norm_g, o_raw, s_all, d_ya, dproj, bsz, lp):
    n_chunks = lp // GLA_CHUNK
    gla_group = _gla_group(n_chunks)
    gla_rows = gla_group * GLA_CHUNK
    n_groups = n_chunks // gla_group
    tp = bsz * lp

    def body(q_ref, k_ref, v_ref, z_ref, lr_ref, gw_ref, gb_ref, gn_ref, o_ref, s_ref, dya_ref, _,
             dp_ref, dz_ref, dgn_ref, dst_scr):
        dv_ref, dzg_ref = dp_ref.at[:, C_V:C_V + GLA_VW], dp_ref.at[:, C_Z:C_Z + GLA_VW]

        @pl.when(jnp.logical_and(pl.program_id(0) == 0, pl.program_id(1) == 0))
        def _():
            dgn_ref[...] = jnp.zeros_like(dgn_ref)

        @pl.when(pl.program_id(1) == 0)
        def _():
            dst_scr[...] = jnp.zeros_like(dst_scr)

        grp = n_groups - 1 - pl.program_id(1)
        chunks = [slice(j * GLA_CHUNK, (j + 1) * GLA_CHUNK) for j in range(gla_group)]
        cs = [_gla_gates(q_ref, k_ref, lr_ref, gw_ref, gb_ref, rows, True if j else grp > 0)
              for j, rows in enumerate(chunks)]
        gn = gn_ref[...]
        dgn = jnp.zeros((1, GLA_DV), F32)
        dqe_h, dke_h, dkl_h, dbl_h = ([[None] * GLA_HEADS for _ in chunks] for _ in range(4))
        dsts = [dst_scr[h] for h in range(GLA_HEADS)]
        for j in reversed(range(gla_group)):
            rows, c = chunks[j], cs[j]
            for h in range(GLA_HEADS):
                ks, vs = slice(h * GLA_DK, (h + 1) * GLA_DK), slice(h * GLA_DV, (h + 1) * GLA_DV)
                dst = dsts[h]
                v = v_ref[rows, vs]
                st = s_ref[0, j, h]
                o = o_ref[rows, vs]
                r = lax.rsqrt(jnp.mean(o * o, axis=-1, keepdims=True) + EPS)
                xh = o * r
                zg = z_ref[rows, vs].astype(F32)
                sg = _sigmoid(zg)
                dy = dya_ref[rows, vs].astype(F32)
                dzg_ref[rows, vs] = _bf(dy * (xh * gn) * (sg * (1.0 + zg * (1.0 - sg))))
                t = dy * (zg * sg)
                dgn += jnp.sum(t * xh, axis=0, keepdims=True)
                dxh = t * gn
                do_b = _bf(r * (dxh - xh * jnp.mean(dxh * xh, axis=-1, keepdims=True)))
                qe_b, ke_b, kl_b, dst_b = c["qe_b"][:, ks], c["ke_b"][:, ks], c["kl_b"][:, ks], _bf(dst)
                a = jnp.where(c["tril"], _dot_nt(qe_b, ke_b), 0.0)
                da_b = _bf(jnp.where(c["tril"], _dot_nt(do_b, v), 0.0))
                dqe_h[j][h] = _dot(da_b, ke_b) + _dot(do_b, _bf(st))
                dke_h[j][h] = _dot_tn(da_b, qe_b)
                dkl = _dot(v, dst_b)
                dkl_h[j][h] = dkl
                dv_ref[rows, vs] = _bf(_dot_tn(_bf(a), do_b) + _dot_nt(kl_b, dst_b))
                ddecay = jnp.sum(dst * st, axis=0, keepdims=True)
                dbl_h[j][h] = jnp.sum(dkl * c["kl"][:, ks], axis=0, keepdims=True) + ddecay * c["ebl"][:, ks]
                dsts[h] = dst * c["ebl"][:, ks] + _dot_tn(do_b, qe_b)
        for h in range(GLA_HEADS):
            dst_scr[h] = dsts[h]
        dgn_ref[...] += dgn
        ri = lax.broadcasted_iota(jnp.int32, (GLA_CHUNK, GLA_CHUNK), 0)
        ci = lax.broadcasted_iota(jnp.int32, (GLA_CHUNK, GLA_CHUNK), 1)
        triu = _bf((ci >= ri).astype(F32))
        for j, (rows, c) in enumerate(zip(chunks, cs)):
            dqe, dke, dkl, dbl = (jnp.concatenate(p[j], axis=1) for p in (dqe_h, dke_h, dkl_h, dbl_h))
            db = dqe * c["qe"] - dke * c["ke"] - dkl * c["kl"] + jnp.where(c["row"] == GLA_CHUNK - 1, dbl, 0.0)
            dg = _tri_dot(triu, db)
            dg = jnp.where(c["live"], dg, 0.0)
            dz_ref[rows, :] = dg * (1.0 / GLA_GATE_NORMALIZER) * _sigmoid(-c["z"])
            dp_ref[rows, C_Q:C_Q + GLA_KW] = _bf(dqe * c["eb"] * (GLA_DK ** -0.5))
            dp_ref[rows, C_K:C_K + GLA_KW] = _bf(dke * c["enb"] + dkl * c["elb"])

    rb, in_specs = _gla_in_specs(n_groups, gla_rows, True)
    wide = pl.BlockSpec((gla_rows, GLA_VW), lambda b, n: (rb(b, n), 0))
    group = C_MZ
    return pl.pallas_call(
        body, name="gla_bwd", grid=(bsz, n_groups),
        in_specs=in_specs + [wide, pl.BlockSpec((1, gla_group, GLA_HEADS, GLA_DV, GLA_DK),
                                                lambda b, n: (b, n_groups - 1 - n, 0, 0, 0)), wide,
                             pl.BlockSpec(memory_space=pl.ANY)],
        out_specs=[pl.BlockSpec((gla_rows, group), lambda b, n: (rb(b, n), 0)),
                   pl.BlockSpec((gla_rows, GLA_KW), lambda b, n: (rb(b, n), 0)),
                   pl.BlockSpec((1, GLA_DV), lambda b, n: (0, 0))],
        out_shape=[jax.ShapeDtypeStruct((tp, N_EXT), BF16), jax.ShapeDtypeStruct((tp, GLA_KW), F32),
                   jax.ShapeDtypeStruct((1, GLA_DV), F32)],
        input_output_aliases={11: 0},
        scratch_shapes=[pltpu.VMEM((GLA_HEADS, GLA_DV, GLA_DK), F32)],
        compiler_params=_cp(("arbitrary", "arbitrary"), 56),
    )(proj, proj, proj, proj, proj, gw_pad, gate_b, gla_norm_g, o_raw, s_all, d_ya, dproj)


def _gate_bwd(dz, proj, gw_pad):
    tp = dz.shape[0]
    tm = _big_tok(tp)

    def body(dz_ref, lr_ref, gw_ref, dlr_ref, dgw_ref, dgb_ref):
        @pl.when(pl.program_id(0) == 0)
        def _():
            dgw_ref[...] = jnp.zeros_like(dgw_ref)
            dgb_ref[...] = jnp.zeros_like(dgb_ref)

        dz = dz_ref[...]
        dz_b = _bf(dz)
        dlr_ref[...] = _bf(_dot_nt(dz_b, gw_ref[...]))
        dgw_ref[...] += _dot_tn(lr_ref[...], dz_b)
        dgb_ref[...] += jnp.sum(dz, axis=0, keepdims=True)

    return pl.pallas_call(
        body, name="gate_bwd", grid=(tp // tm,),
        in_specs=[pl.BlockSpec((tm, GLA_KW), lambda i: (i, 0)),
                  pl.BlockSpec((tm, LANE), lambda i: (i, C_LR // LANE)),
                  pl.BlockSpec((LANE, GLA_KW), lambda i: (0, 0))],
        out_specs=[pl.BlockSpec((tm, LANE), lambda i: (i, 0)),
                   pl.BlockSpec((LANE, GLA_KW), lambda i: (0, 0)),
                   pl.BlockSpec((1, GLA_KW), lambda i: (0, 0))],
        out_shape=[jax.ShapeDtypeStruct((tp, LANE), BF16), jax.ShapeDtypeStruct((LANE, GLA_KW), F32),
                   jax.ShapeDtypeStruct((1, GLA_KW), F32)],
        compiler_params=_cp(("arbitrary",)),
    )(dz, proj, gw_pad)


def _rms_fwd(x):
    r = lax.rsqrt(jnp.mean(x * x, axis=-1, keepdims=True) + EPS)
    return x * r, r


def _rms_bwd(dy, xh, r, g):
    dxh = dy * g
    dx = r * (dxh - xh * jnp.mean(dxh * xh, axis=-1, keepdims=True))
    return dx, jnp.sum(dy * xh, axis=0, keepdims=True)


def _q_up(proj, q_norm_g, wn, wr, wt, cos_t, sin_t, bsz, lp):
    tp = bsz * lp
    tok = _wide_block(lp)
    nb = lp // tok

    def body(cq_ref, g_ref, wn_ref, wr_ref, wt_ref, cos_ref, sin_ref, q_ref):
        xh, _ = _rms_fwd(cq_ref[...].astype(F32))
        cqn = _bf(xh * g_ref[...])
        nope = _dot(cqn, wn_ref[...])
        rope = _dot(cqn, wr_ref[...])
        rot = _dot(cqn, wt_ref[...])
        cos, sin = cos_ref[...], sin_ref[...]
        one = (lax.broadcasted_iota(jnp.int32, (tok, LANE), 1) == BIAS_LANE).astype(F32)
        for h in range(MLA_HEADS):
            sl = slice(h * LANE, (h + 1) * LANE)
            q_ref[:, h * QKW:h * QKW + LANE] = _bf(nope[:, sl])
            q_ref[:, h * QKW + LANE:(h + 1) * QKW] = _bf(rope[:, sl] * cos + rot[:, sl] * sin + one)

    wspec = pl.BlockSpec((MLA_QR, MLA_HEADS * LANE), lambda b, i: (0, 0))
    tspec = pl.BlockSpec((tok, LANE), lambda b, i: (i, 0))
    return pl.pallas_call(
        body, name="mla_q_up", grid=(bsz, nb),
        in_specs=[pl.BlockSpec((tok, MLA_QR), lambda b, i: (b * nb + i, C_CQ // MLA_QR)),
                  pl.BlockSpec((1, MLA_QR), lambda b, i: (0, 0)), wspec, wspec, wspec, tspec, tspec],
        out_specs=pl.BlockSpec((tok, MLA_HEADS * QKW), lambda b, i: (b * nb + i, 0)),
        out_shape=jax.ShapeDtypeStruct((tp, MLA_HEADS * QKW), BF16),
        compiler_params=_cp(("parallel", "parallel")),
    )(proj, q_norm_g, wn, wr, wt, cos_t, sin_t)


def _kv_up(proj, kv_norm_g, wk, wv, cos_t, sin_t, bsz, lp):
    tp = bsz * lp
    tok = _wide_block(lp)
    nb = lp // tok

    def body(ckv_ref, kr_ref, krot_ref, g_ref, wk_ref, wv_ref, cos_ref, sin_ref, k_ref, v_ref):
        xh, _ = _rms_fwd(ckv_ref[...].astype(F32))
        cn = _bf(xh * g_ref[...])
        kn = _dot(cn, wk_ref[...])
        v_ref[...] = _bf(_dot(cn, wv_ref[...]))
        pos = pl.program_id(1) * tok + lax.broadcasted_iota(jnp.int32, (tok, LANE), 0)
        lane = lax.broadcasted_iota(jnp.int32, (tok, LANE), 1)
        bias = jnp.where(jnp.logical_and(lane == BIAS_LANE, pos < FRONT), KEY_BIAS, 0.0)
        kr = _bf(kr_ref[...].astype(F32) * cos_ref[...] + krot_ref[...].astype(F32) * sin_ref[...] + bias)
        for h in range(MLA_HEADS):
            k_ref[:, h * QKW:h * QKW + LANE] = _bf(kn[:, h * LANE:(h + 1) * LANE])
            k_ref[:, h * QKW + LANE:(h + 1) * QKW] = kr

    wspec = pl.BlockSpec((MLA_KVR, MLA_HEADS * LANE), lambda b, i: (0, 0))
    tspec = pl.BlockSpec((tok, LANE), lambda b, i: (i, 0))
    return pl.pallas_call(
        body, name="mla_kv_up", grid=(bsz, nb),
        in_specs=[pl.BlockSpec((tok, LANE), lambda b, i: (b * nb + i, C_CKV // LANE)),
                  pl.BlockSpec((tok, LANE), lambda b, i: (b * nb + i, C_KR // LANE)),
                  pl.BlockSpec((tok, LANE), lambda b, i: (b * nb + i, C_KROT // LANE)),
                  pl.BlockSpec((1, MLA_KVR), lambda b, i: (0, 0)), wspec, wspec, tspec, tspec],
        out_specs=[pl.BlockSpec((tok, MLA_HEADS * QKW), lambda b, i: (b * nb + i, 0)),
                   pl.BlockSpec((tok, MLA_HEADS * LANE), lambda b, i: (b * nb + i, 0))],
        out_shape=[jax.ShapeDtypeStruct((tp, MLA_HEADS * QKW), BF16),
                   jax.ShapeDtypeStruct((tp, MLA_HEADS * LANE), BF16)],
        compiler_params=_cp(("parallel", "parallel")),
    )(proj, proj, proj, kv_norm_g, wk, wv, cos_t, sin_t)


ATT_SCALE = MLA_QK ** -0.5


KEY_BIAS = -1e30
BIAS_LANE = MLA_ROPE
NEG = 2 * KEY_BIAS
LOG2E = 1.4426950408889634
EXP2_SCALE = ATT_SCALE * LOG2E


def _causal_fill(s, r0, fill):
    tq, kmax = s.shape
    a = r0 // LANE * LANE
    mask = (a + lax.broadcasted_iota(jnp.int32, (tq, kmax - a), 1)
            <= r0 + lax.broadcasted_iota(jnp.int32, (tq, kmax - a), 0))
    right = jnp.where(mask, s[:, a:], fill)
    return jnp.concatenate([s[:, :a], right], axis=1) if a else right


def _attn_fwd(qf, kf, vf, proj, bsz, lp):
    tp = bsz * lp
    tq = _attn_block(lp)
    nh = 2

    def body(q_ref, k_ref, v_ref, mz_ref, ob_ref, yb_ref, lse_ref):
        starts = list(range(0, lp, tq))
        for pair in (starts[i:i + 2] for i in range(0, len(starts), 2)):
            work = [(r0, h) for r0 in pair for h in range(nh)]
            ss = [_causal_fill(_dot_nt(q_ref[r0:r0 + tq, h * QKW:(h + 1) * QKW],
                                       k_ref[0:r0 + tq, h * QKW:(h + 1) * QKW]), r0, NEG) for r0, h in work]
            ms = [jnp.max(s, axis=-1, keepdims=True) for s in ss]
            ps = [jnp.exp2((s - m) * EXP2_SCALE) for s, m in zip(ss, ms)]
            ls = [jnp.sum(p, axis=-1, keepdims=True) for p in ps]
            for (r0, h), p, m, l in zip(work, ps, ms, ls):
                rows, cols = slice(r0, r0 + tq), slice(h * MLA_DV, (h + 1) * MLA_DV)
                o = _dot(_bf(p), v_ref[0:r0 + tq, cols]) / l
                ob_ref[rows, cols] = _bf(o)
                mz = mz_ref[rows, cols].astype(F32)
                yb_ref[rows, cols] = _bf(o * (mz * _sigmoid(mz)))
                lse_ref[0, h, rows, :] = jnp.broadcast_to(m * EXP2_SCALE + jnp.log2(l), (tq, LANE))

    head = lambda off: pl.BlockSpec((lp, nh * MLA_DV), lambda b, h: (b, off + h))
    wide = pl.BlockSpec((lp, nh * QKW), lambda b, h: (b, h))
    return pl.pallas_call(
        body, name="mla_attn_fwd", grid=(bsz, MLA_HEADS // nh),
        in_specs=[wide, wide, head(0), head(C_MZ // (nh * MLA_DV))],
        out_specs=[head(0), head(0), pl.BlockSpec((1, nh, lp, LANE), lambda b, h: (b, h, 0, 0))],
        out_shape=[jax.ShapeDtypeStruct((tp, MLA_HEADS * MLA_DV), BF16),
                   jax.ShapeDtypeStruct((tp, MLA_HEADS * MLA_DV), BF16),
                   jax.ShapeDtypeStruct((bsz, MLA_HEADS, lp, LANE), F32)],
        compiler_params=_cp(("parallel", "parallel"), 56),
    )(qf, kf, vf, proj)


def _attn_bwd_blocks(lp):
    return [(0, X0)] + [(r0, min(MXU_DEPTH, lp - r0)) for r0 in range(X0, lp, MXU_DEPTH)]


def _attn_bwd(qf, kf, vf, d_o, lse, delta, bsz, lp):
    tp = bsz * lp

    def body(q_ref, k_ref, v_ref, do_ref, lse_ref, dl_ref, dq_ref, dk_ref, dv_ref, dk_acc, dv_acc):
        dk_acc[...] = jnp.zeros_like(dk_acc)
        dv_acc[...] = jnp.zeros_like(dv_acc)
        for r0, tq in _attn_bwd_blocks(lp):
            rows, kmax = slice(r0, r0 + tq), r0 + tq
            q, do = q_ref[rows, :], do_ref[rows, :]
            k, v = k_ref[0:kmax, :], v_ref[0:kmax, :]
            p = jnp.exp2(_dot_nt(q, k) * EXP2_SCALE - lse_ref[0, 0, rows, :][:, :1])
            p = _causal_fill(p, r0, 0.0)
            ds = _bf(p * (_dot_nt(do, v) - dl_ref[0, rows, :][:, :1]))
            dq_ref[rows, :] = _bf(_dot(ds, k) * ATT_SCALE)
            dk_acc[0:kmax, :] += _dot_tn(ds, q)
            dv_acc[0:kmax, :] += _dot_tn(_bf(p), do)
        dk_ref[...] = _bf(dk_acc[...] * ATT_SCALE)
        dv_ref[...] = _bf(dv_acc[...])

    wide = pl.BlockSpec((lp, QKW), lambda b, h: (b, h))
    narrow = pl.BlockSpec((lp, MLA_DV), lambda b, h: (b, h))
    stat = pl.BlockSpec((1, 1, lp, LANE), lambda b, h: (b, h, 0, 0))
    return pl.pallas_call(
        body, name="mla_attn_bwd", grid=(bsz, MLA_HEADS),
        in_specs=[wide, wide, narrow, narrow, stat, pl.BlockSpec((1, lp, LANE), lambda b, h: (h, b, 0))],
        out_specs=[wide, wide, narrow],
        out_shape=[jax.ShapeDtypeStruct((tp, MLA_HEADS * QKW), BF16), jax.ShapeDtypeStruct((tp, MLA_HEADS * QKW), BF16),
                   jax.ShapeDtypeStruct((tp, MLA_HEADS * MLA_DV), BF16)],
        scratch_shapes=[pltpu.VMEM((lp, QKW), F32), pltpu.VMEM((lp, MLA_DV), F32)],
        compiler_params=_cp(("parallel", "parallel"), 56),
    )(qf, kf, vf, d_o, lse, delta)


def _q_up_bwd(dqf, proj, q_norm_g, wn, wr, wt, cos_t, sin_t, dproj, bsz, lp):
    tp = bsz * lp
    tok = _wide_block(lp)
    nb = lp // tok
    hw = MLA_HEADS * LANE

    def body(dq_ref, cq_ref, g_ref, wn_ref, wr_ref, wt_ref, cos_ref, sin_ref, _,
             dcq_ref, dwn_ref, dwr_ref, dwt_ref, dg_ref):
        @pl.when(jnp.logical_and(pl.program_id(0) == 0, pl.program_id(1) == 0))
        def _():
            for r in (dwn_ref, dwr_ref, dwt_ref, dg_ref):
                r[...] = jnp.zeros_like(r)

        g = g_ref[...]
        xh, r = _rms_fwd(cq_ref[...].astype(F32))
        cqn = _bf(xh * g)
        dn = jnp.concatenate([dq_ref[:, h * QKW:h * QKW + LANE] for h in range(MLA_HEADS)], axis=1)
        dr = jnp.concatenate([dq_ref[:, h * QKW + LANE:(h + 1) * QKW] for h in range(MLA_HEADS)], axis=1).astype(F32)
        dr_c = _bf(dr * jnp.tile(cos_ref[...], (1, MLA_HEADS)))
        dr_s = _bf(dr * jnp.tile(sin_ref[...], (1, MLA_HEADS)))
        dcqn = _dot_nt(dn, wn_ref[...]) + _dot_nt(dr_c, wr_ref[...]) + _dot_nt(dr_s, wt_ref[...])
        dwn_ref[...] += _dot_tn(cqn, dn)
        dwr_ref[...] += _dot_tn(cqn, dr_c)
        dwt_ref[...] += _dot_tn(cqn, dr_s)
        dx, dg = _rms_bwd(dcqn, xh, r, g)
        dcq_ref[...] = _bf(dx)
        dg_ref[...] += dg

    aspec = pl.BlockSpec((MLA_QR, hw), lambda b, i: (0, 0))
    tspec = pl.BlockSpec((tok, LANE), lambda b, i: (i, 0))
    return pl.pallas_call(
        body, name="mla_q_up_bwd", grid=(bsz, nb),
        in_specs=[pl.BlockSpec((tok, MLA_HEADS * QKW), lambda b, i: (b * nb + i, 0)),
                  pl.BlockSpec((tok, MLA_QR), lambda b, i: (b * nb + i, C_CQ // MLA_QR)),
                  pl.BlockSpec((1, MLA_QR), lambda b, i: (0, 0)), aspec, aspec, aspec, tspec, tspec,
                  pl.BlockSpec(memory_space=pl.ANY)],
        out_specs=[pl.BlockSpec((tok, MLA_QR), lambda b, i: (b * nb + i, C_CQ // MLA_QR)), aspec, aspec, aspec,
                   pl.BlockSpec((1, MLA_QR), lambda b, i: (0, 0))],
        out_shape=[jax.ShapeDtypeStruct((tp, N_EXT), BF16)] + [jax.ShapeDtypeStruct((MLA_QR, hw), F32)] * 3
        + [jax.ShapeDtypeStruct((1, MLA_QR), F32)],
        input_output_aliases={8: 0},
        compiler_params=_cp(("arbitrary", "arbitrary")),
    )(dqf, proj, q_norm_g, wn, wr, wt, cos_t, sin_t, dproj)


def _kv_up_bwd(dkf, dvf, proj, kv_norm_g, wk, wv, cos_t, sin_t, d_lr, dproj, bsz, lp):
    tp = bsz * lp
    tok = _wide_block(lp)
    nb = lp // tok
    hw = MLA_HEADS * LANE

    def body(dk_ref, dv_ref, ckv_ref, g_ref, wk_ref, wv_ref, cos_ref, sin_ref, dlr_ref, _,
             dp_ref, dwk_ref, dwv_ref, dg_ref):
        dckv_ref, dkr_ref, dkrot_ref = (dp_ref.at[:, j * LANE:(j + 1) * LANE] for j in range(3))
        dp_ref[:, 3 * LANE:] = dlr_ref[...]
        @pl.when(jnp.logical_and(pl.program_id(0) == 0, pl.program_id(1) == 0))
        def _():
            for r in (dwk_ref, dwv_ref, dg_ref):
                r[...] = jnp.zeros_like(r)

        g = g_ref[...]
        xh, r = _rms_fwd(ckv_ref[...].astype(F32))
        cn = _bf(xh * g)
        dv = dv_ref[...]
        dn = jnp.concatenate([dk_ref[:, h * QKW:h * QKW + LANE] for h in range(MLA_HEADS)], axis=1)
        dcn = _dot_nt(dv, wv_ref[...]) + _dot_nt(dn, wk_ref[...])
        dwv_ref[...] += _dot_tn(cn, dv)
        dwk_ref[...] += _dot_tn(cn, dn)
        drope = jnp.zeros((tok, LANE), F32)
        for h in range(MLA_HEADS):
            drope += dk_ref[:, h * QKW + LANE:(h + 1) * QKW].astype(F32)
        dkr_ref[...] = _bf(drope * cos_ref[...])
        dkrot_ref[...] = _bf(drope * sin_ref[...])
        dx, dg = _rms_bwd(dcn, xh, r, g)
        dckv_ref[...] = _bf(dx)
        dg_ref[...] += dg

    aspec = pl.BlockSpec((MLA_KVR, hw), lambda b, i: (0, 0))
    tspec = pl.BlockSpec((tok, LANE), lambda b, i: (i, 0))
    ospec = pl.BlockSpec((tok, LANE), lambda b, i: (b * nb + i, 0))
    return pl.pallas_call(
        body, name="mla_kv_up_bwd", grid=(bsz, nb),
        in_specs=[pl.BlockSpec((tok, MLA_HEADS * QKW), lambda b, i: (b * nb + i, 0)),
                  pl.BlockSpec((tok, hw), lambda b, i: (b * nb + i, 0)),
                  pl.BlockSpec((tok, LANE), lambda b, i: (b * nb + i, C_CKV // LANE)),
                  pl.BlockSpec((1, MLA_KVR), lambda b, i: (0, 0)), aspec, aspec, tspec, tspec, ospec,
                  pl.BlockSpec(memory_space=pl.ANY)],
        out_specs=[pl.BlockSpec((tok, 4 * LANE), lambda b, i: (b * nb + i, C_CKV // (4 * LANE))), aspec, aspec,
                   pl.BlockSpec((1, MLA_KVR), lambda b, i: (0, 0))],
        out_shape=[jax.ShapeDtypeStruct((tp, N_EXT), BF16)] + [jax.ShapeDtypeStruct((MLA_KVR, hw), F32)] * 2
        + [jax.ShapeDtypeStruct((1, MLA_KVR), F32)],
        input_output_aliases={9: 0},
        compiler_params=_cp(("arbitrary", "arbitrary")),
    )(dkf, dvf, proj, kv_norm_g, wk, wv, cos_t, sin_t, d_lr, dproj)


def _mid_fwd(ya_in, yb_in, proj, hp, target, w_gp, w_mp, w_o, final_g, bsz, lp):
    tp = bsz * lp
    tm = _wide_block(lp)
    nb = lp // tm
    last = pl.cdiv(lp - X0, tm) - 1

    def body(ya_ref, yb_ref, gg_ref, gm_ref, h_ref, ta_ref, tb_ref, wgp_ref, wmp_ref, wo_ref, fg_ref,
             ya_out, yb_out, dh_ref, loss_ref, dfg_ref):
        @pl.when(jnp.logical_and(pl.program_id(0) == 0, pl.program_id(1) == 0))
        def _():
            loss_ref[...] = jnp.zeros_like(loss_ref)
            dfg_ref[...] = jnp.zeros_like(dfg_ref)

        y_a = _dot(ya_ref[...], wgp_ref[...])
        y_b = _dot(yb_ref[...], wmp_ref[...])
        ya_out[...] = _bf(y_a)
        yb_out[...] = _bf(y_b)
        merged = _sigmoid(gg_ref[...].astype(F32)) * y_a + _sigmoid(gm_ref[...].astype(F32)) * y_b
        h2 = h_ref[...] + _dot(_bf(merged), wo_ref[...])
        fg = fg_ref[...]
        xh, r = _rms_fwd(h2)
        pos = pl.program_id(1) * tm + lax.broadcasted_iota(jnp.int32, (tm, 1), 0)
        t = jnp.concatenate([ta_ref[0, tm - X0:, :], tb_ref[0, :tm - X0, :]], axis=0)
        err = jnp.where(pos >= X0, xh * fg - t, 0.0)
        loss_ref[...] += 0.5 * jnp.sum(jnp.mean(err * err, axis=-1, keepdims=True), axis=0, keepdims=True)
        dy = err * (1.0 / D_MODEL)
        dx, dfg = _rms_bwd(dy, xh, r, fg)
        dh_ref[...] = dx
        dfg_ref[...] += dfg

    tok = lambda c: pl.BlockSpec((tm, D_MODEL), lambda b, i: (b * nb + i, c))
    wspec = pl.BlockSpec((D_MODEL, D_MODEL), lambda b, i: (0, 0), pipeline_mode=pl.Buffered(1))
    return pl.pallas_call(
        body, name="mid_fwd", grid=(bsz, nb),
        in_specs=[tok(0), tok(0), tok(C_GG // D_MODEL), tok(C_GM // D_MODEL), tok(0),
                  pl.BlockSpec((1, tm, D_MODEL), lambda b, i: (b, jnp.maximum(i - 1, 0), 0)),
                  pl.BlockSpec((1, tm, D_MODEL), lambda b, i: (b, jnp.minimum(i, last), 0)),
                  wspec, wspec, wspec, pl.BlockSpec((1, D_MODEL), lambda b, i: (0, 0))],
        out_specs=[tok(0), tok(0), tok(0), pl.BlockSpec((1, LANE), lambda b, i: (0, 0)),
                   pl.BlockSpec((1, D_MODEL), lambda b, i: (0, 0))],
        out_shape=[jax.ShapeDtypeStruct((tp, D_MODEL), BF16), jax.ShapeDtypeStruct((tp, D_MODEL), BF16),
                   jax.ShapeDtypeStruct((tp, D_MODEL), F32), jax.ShapeDtypeStruct((1, LANE), F32),
                   jax.ShapeDtypeStruct((1, D_MODEL), F32)],
        compiler_params=_cp(("arbitrary", "arbitrary"), 56),
    )(ya_in, yb_in, proj, proj, hp, target, target, w_gp, w_mp, w_o, final_g)


def _mid_bwd(dh2, y_a, y_b, proj, ya_in, yb_in, o_b, w_o, w_gp, w_mp, bsz, lp):
    tp = bsz * lp
    tm = MXU_DEPTH if tp % MXU_DEPTH == 0 else _attn_block(lp)
    nsteps = tp // tm
    group = 3 * D_MODEL

    def body(dh_ref, ya_ref, yb_ref, mz_ref, gg_ref, gm_ref, yai_ref, ybi_ref, ob_ref, wo_ref, wgp_ref, wmp_ref,
             dyai_ref, do_ref, dp_ref, dl_ref, dwo_ref, dwgp_ref, dwmp_ref, a_o, a_gp, a_mp):
        @pl.when(pl.program_id(0) == 0)
        def _():
            for r in (a_o, a_gp, a_mp):
                r[...] = jnp.zeros_like(r)

        dh = _bf(dh_ref[...])
        dm = _dot_nt(dh, wo_ref[...])
        y_a, y_b = ya_ref[...].astype(F32), yb_ref[...].astype(F32)
        sg, sm = _sigmoid(gg_ref[...].astype(F32)), _sigmoid(gm_ref[...].astype(F32))
        d_ya, d_yb = _bf(sg * dm), _bf(sm * dm)
        dp_ref[:, D_MODEL:2 * D_MODEL] = _bf(dm * y_a * sg * (1.0 - sg))
        dp_ref[:, 2 * D_MODEL:] = _bf(dm * y_b * sm * (1.0 - sm))
        merged = _bf(sg * y_a + sm * y_b)
        dy = _dot_nt(d_yb, wmp_ref[...])
        dyai_ref[...] = _bf(_dot_nt(d_ya, wgp_ref[...]))
        a_o[...] += _dot_tn(merged, dh)
        a_gp[...] += _dot_tn(yai_ref[...], d_ya)
        a_mp[...] += _dot_tn(ybi_ref[...], d_yb)
        mz, o = mz_ref[...].astype(F32), ob_ref[...].astype(F32)
        s = _sigmoid(mz)
        do = _bf(dy * (mz * s))
        do_ref[...] = do
        dp_ref[:, :D_MODEL] = _bf(dy * o * (s * (1.0 + mz * (1.0 - s))))
        prod = do.astype(F32) * o
        for h in range(MLA_HEADS):
            dl = jnp.sum(prod[:, h * MLA_DV:(h + 1) * MLA_DV], axis=-1, keepdims=True)
            dl_ref[h] = jnp.broadcast_to(dl, (tm, LANE))

        @pl.when(pl.program_id(0) == nsteps - 1)
        def _():
            pltpu.sync_copy(a_o, dwo_ref)
            pltpu.sync_copy(a_gp, dwgp_ref)
            pltpu.sync_copy(a_mp, dwmp_ref)

    tok = lambda c: pl.BlockSpec((tm, D_MODEL), lambda i: (i, c))
    wspec = pl.BlockSpec((D_MODEL, D_MODEL), lambda i: (0, 0))
    anyspec = pl.BlockSpec(memory_space=pl.ANY)
    wshape = jax.ShapeDtypeStruct((D_MODEL, D_MODEL), F32)
    return pl.pallas_call(
        body, name="mid_bwd", grid=(nsteps,),
        in_specs=[tok(0), tok(0), tok(0), tok(C_MZ // D_MODEL), tok(C_GG // D_MODEL), tok(C_GM // D_MODEL),
                  tok(0), tok(0), tok(0), wspec, wspec, wspec],
        out_specs=[tok(0), tok(0), pl.BlockSpec((tm, group), lambda i: (i, C_MZ // group)),
                   pl.BlockSpec((MLA_HEADS, tm, LANE), lambda i: (0, i, 0)), anyspec, anyspec, anyspec],
        out_shape=[jax.ShapeDtypeStruct((tp, D_MODEL), BF16)] * 2 + [jax.ShapeDtypeStruct((tp, N_EXT), BF16),
                   jax.ShapeDtypeStruct((MLA_HEADS, tp, LANE), F32)] + [wshape] * 3,
        scratch_shapes=[pltpu.VMEM((D_MODEL, D_MODEL), F32)] * 3,
        compiler_params=_cp(("arbitrary",), 56),
    )(dh2, y_a, y_b, proj, proj, proj, ya_in, yb_in, o_b, w_o, w_gp, w_mp)


MESH_ID = pl.DeviceIdType.MESH
EXCHANGE_SEMS = [pltpu.SemaphoreType.DMA((N_DEV - 1,)), pltpu.SemaphoreType.DMA((N_DEV - 1,)), pltpu.SemaphoreType.DMA]


def _my_place():
    return lax.axis_index("x"), lax.axis_index("y"), lax.axis_index("c")


def _exchange(g_ref, recv_ref, send_sems, recv_sems, local_sem, start, same=False):
    x, y, c = _my_place()
    me = 4 * x + 2 * y + c
    own = pltpu.make_async_copy(g_ref if same else g_ref.at[me], recv_ref.at[me], local_sem)
    sends, lands = [], []
    for d in range(1, N_DEV):
        px = 1 - x if d & 4 else x
        py = 1 - y if d & 2 else y
        pc = 1 - c if d & 1 else c
        peer = 4 * px + 2 * py + pc
        for slot, group in ((me, sends),) if start else ((me, sends), (peer, lands)):
            group.append(pltpu.make_async_remote_copy(
                src_ref=g_ref if same else g_ref.at[peer], dst_ref=recv_ref.at[slot], send_sem=send_sems.at[d - 1],
                recv_sem=recv_sems.at[d - 1], device_id=(px, py, pc), device_id_type=MESH_ID))
    if start:
        own.start()
        for cp in sends:
            cp.start()
    else:
        for cp in lands:
            cp.wait_recv()
        for cp in sends:
            cp.wait_send()
        own.wait()


def _dw_in(u, dproj, slabs):
    tp = u.shape[0]
    tn = 3 * LANE
    nj = N_EXT // tn

    def body(u_ref, d_ref, g_ref, o_ref, recv_ref, send_sems, recv_sems, local_sem):
        j = pl.program_id(0)

        @pl.when(j == 0)
        def _():
            _exchange(g_ref, recv_ref, send_sems, recv_sems, local_sem, True)

        o_ref[...] = _bf(_dot_tn(d_ref[...], u_ref[...]))

        @pl.when(j == nj - 1)
        def _():
            _exchange(g_ref, recv_ref, send_sems, recv_sems, local_sem, False)

    anyspec = pl.BlockSpec(memory_space=pl.ANY)
    return pl.pallas_call(
        body, name="dw_in", grid=(nj,),
        in_specs=[pl.BlockSpec((tp, D_MODEL), lambda j: (0, 0), pipeline_mode=pl.Buffered(1)),
                  pl.BlockSpec((tp, tn), lambda j: (0, j)), anyspec],
        out_specs=[pl.BlockSpec((tn, D_MODEL), lambda j: (j, 0)), anyspec],
        out_shape=[jax.ShapeDtypeStruct((N_EXT, D_MODEL), BF16), jax.ShapeDtypeStruct(slabs.shape, slabs.dtype)],
        scratch_shapes=EXCHANGE_SEMS,
        compiler_params=_cp(("arbitrary",), 56),
    )(u, dproj, slabs)


def _dx_in(dproj, w_ext, hp, dh2, norm_g, slabs, bsz):
    tp = hp.shape[0]
    lp = tp // bsz
    tm = _attn_block(lp)
    ni = lp // tm
    steps = bsz * ni
    assert ni > 1 and tm > X0

    def body(d_ref, w_ref, h_ref, dh_ref, g_ref, s_ref, gx_ref, dg_ref, dm_ref, recv_ref,
             stage, out_sems, send_sems, recv_sems, local_sem):
        s = pl.program_id(0)
        slot = s % 2

        def out_copy(step, head):
            b, at = step // ni, step % 2
            if head:
                return pltpu.make_async_copy(stage.at[at, pl.ds(X0, tm - X0)], gx_ref.at[b, pl.ds(0, tm - X0)],
                                             out_sems.at[at])
            first = pl.multiple_of((step % ni) * tm - X0, 8)
            return pltpu.make_async_copy(stage.at[at], gx_ref.at[b, pl.ds(first, tm)], out_sems.at[at])

        @pl.when(s == 0)
        def _():
            _exchange(s_ref, recv_ref, send_sems, recv_sems, local_sem, True)
            dg_ref[...] = jnp.zeros_like(dg_ref)
            dm_ref[...] = jnp.zeros_like(dm_ref)

        du = _dot_nt(d_ref[...], w_ref[...])
        g = g_ref[...]
        xh, r = _rms_fwd(h_ref[...])
        dx, dg = _rms_bwd(du, xh, r, g)
        dg_ref[...] += dg
        stage[slot] = dh_ref[...] + dx
        head = s % ni == 0

        @pl.when(head)
        def _():
            dm_ref[...] += stage[slot, pl.ds(FRONT, N_META), :]
            out_copy(s, True).start()

        @pl.when(jnp.logical_not(head))
        def _():
            out_copy(s, False).start()

        @pl.when(s % ni == 1)
        def _():
            out_copy(s - 1, True).wait()

        @pl.when(s % ni > 1)
        def _():
            out_copy(s - 1, False).wait()

        @pl.when(jnp.logical_and(head, s > 0))
        def _():
            out_copy(s - 1, False).wait()

        @pl.when(s == steps - 1)
        def _():
            out_copy(s, False).wait()
            _exchange(s_ref, recv_ref, send_sems, recv_sems, local_sem, False)

    tok = pl.BlockSpec((tm, D_MODEL), lambda s: (s, 0))
    anyspec = pl.BlockSpec(memory_space=pl.ANY)
    return pl.pallas_call(
        body, name="dx_in", grid=(steps,),
        in_specs=[pl.BlockSpec((tm, N_EXT), lambda s: (s, 0)),
                  pl.BlockSpec((D_MODEL, N_EXT), lambda s: (0, 0), pipeline_mode=pl.Buffered(1)),
                  tok, tok, pl.BlockSpec((1, D_MODEL), lambda s: (0, 0)), anyspec],
        out_specs=[anyspec, pl.BlockSpec((1, D_MODEL), lambda s: (0, 0)),
                   pl.BlockSpec((N_META, D_MODEL), lambda s: (0, 0)), anyspec],
        out_shape=[jax.ShapeDtypeStruct((bsz, lp - X0, D_MODEL), F32), jax.ShapeDtypeStruct((1, D_MODEL), F32),
                   jax.ShapeDtypeStruct((N_META, D_MODEL), F32), jax.ShapeDtypeStruct(slabs.shape, slabs.dtype)],
        scratch_shapes=[pltpu.VMEM((2, tm, D_MODEL), F32), pltpu.SemaphoreType.DMA((2,))] + EXCHANGE_SEMS,
        compiler_params=_cp(("arbitrary",), 56),
    )(dproj, w_ext, hp, dh2, norm_g, slabs)


W_IN_SHARD = N_IN // N_DEV


def _pad_lanes(a, width=LANE):
    return jnp.pad(a, [(0, 0)] * (a.ndim - 1) + [(0, width - a.shape[-1])])


def _rot_cols(w):
    half = w.shape[-1] // 2
    return jnp.concatenate([-w[..., half:], w[..., :half]], axis=-1)


def _unrot_cols(dw):
    half = dw.shape[-1] // 2
    return jnp.concatenate([dw[..., half:], -dw[..., :half]], axis=-1)


def _w_in_cols(shards, lo, hi):
    parts = []
    for k in range(lo // W_IN_SHARD, (hi - 1) // W_IN_SHARD + 1):
        a, b = max(lo, k * W_IN_SHARD), min(hi, (k + 1) * W_IN_SHARD)
        parts.append(shards[k][:, a - k * W_IN_SHARD:b - k * W_IN_SHARD])
    return parts[0] if len(parts) == 1 else jnp.concatenate(parts, axis=1)


def _w_in_ext(shards):
    c = lambda lo, hi: _w_in_cols(shards, lo, hi)
    kr = c(O_KR, O_MZ)
    return jnp.concatenate([
        c(O_V, O_LR), c(O_Z, O_CQ), c(O_Q, O_K), c(O_K, O_V), c(O_MZ, O_GG), c(O_GG, O_GM), c(O_GM, N_IN),
        c(O_CKV, O_KR), _pad_lanes(kr), _pad_lanes(_rot_cols(kr)), _pad_lanes(c(O_LR, O_Z)), c(O_CQ, O_CKV)], axis=1)


def _w_in_slabs(dwt):
    half = MLA_ROPE // 2
    krot = dwt[C_KROT:C_KROT + MLA_ROPE]
    kr = dwt[C_KR:C_KR + MLA_ROPE] + jnp.concatenate([krot[half:], -krot[:half]], axis=0)
    groups = ((O_Q, GLA_KW, C_Q), (O_K, GLA_KW, C_K), (O_V, GLA_VW, C_V), (O_LR, GLA_RANK, C_LR), (O_Z, GLA_VW, C_Z),
              (O_CQ, MLA_QR, C_CQ), (O_CKV, MLA_KVR, C_CKV), (O_KR, MLA_ROPE, None), (O_MZ, D_MODEL, C_MZ),
              (O_GG, D_MODEL, C_GG), (O_GM, D_MODEL, C_GM))
    slabs = []
    for k in range(N_DEV):
        lo, hi = k * W_IN_SHARD, (k + 1) * W_IN_SHARD
        parts = []
        for first, width, row in groups:
            a, b = max(lo, first), min(hi, first + width)
            if a < b:
                parts.append(kr[a - first:b - first] if row is None else dwt[row + a - first:row + b - first])
        slabs.append(jnp.concatenate(parts, axis=0))
    return jnp.stack(slabs)


def _rope_tables(lp):
    inv = 1.0 / (ROPE_BASE ** (jnp.arange(0, MLA_ROPE, 2, dtype=F32) / MLA_ROPE))
    ang = (jnp.arange(lp, dtype=F32) - FRONT)[:, None] * inv[None, :]
    cos, sin = jnp.cos(ang), jnp.sin(ang)
    return _pad_lanes(jnp.concatenate([cos, cos], axis=1)), _pad_lanes(jnp.concatenate([sin, sin], axis=1))


def _local_step(x, loss_target, w):
    bsz, seq, _ = x.shape
    lp = X0 + seq
    tp = bsz * lp
    assert lp % TOK == 0 and (lp // GLA_CHUNK) % _gla_group(lp // GLA_CHUNK) == 0
    head = jnp.concatenate([jnp.zeros((FRONT, D_MODEL), F32), w["meta_tokens"]], axis=0)
    cos_t, sin_t = _rope_tables(lp)

    w_ext = _w_in_ext(w["w_in"])
    hp, u, proj, packed_all = _proj_in(x, head, w["norm_g"], w_ext, w["packed"])
    gathered = _unpack_shards(packed_all)
    for n, _, axis in PACKED:
        w[n] = _join8(gathered[n], axis)
    gw_pad = jnp.pad(w["gla_gate_w"], ((0, LANE - GLA_RANK), (0, 0)))
    uq = w["mla_w_uq"].reshape(MLA_QR, MLA_HEADS, MLA_QK)
    rope_w = uq[:, :, MLA_NOPE:]
    hw = MLA_HEADS * LANE
    wn = uq[:, :, :MLA_NOPE].reshape(MLA_QR, hw)
    wr = _pad_lanes(rope_w).reshape(MLA_QR, hw)
    wt = _pad_lanes(_rot_cols(rope_w)).reshape(MLA_QR, hw)
    ukv = w["mla_w_ukv"].reshape(MLA_KVR, MLA_HEADS, MLA_NOPE + MLA_DV)
    wk = ukv[:, :, :MLA_NOPE].reshape(MLA_KVR, hw)
    wv = ukv[:, :, MLA_NOPE:].reshape(MLA_KVR, hw)

    o_raw, ya_in, s_all = _gla_fwd(proj, gw_pad, w["gla_gate_b"], w["gla_norm_g"], bsz, lp)
    qf = _q_up(proj, w["mla_q_norm_g"], wn, wr, wt, cos_t, sin_t, bsz, lp)
    kf, vf = _kv_up(proj, w["mla_kv_norm_g"], wk, wv, cos_t, sin_t, bsz, lp)
    o_b, yb_in, lse = _attn_fwd(qf, kf, vf, proj, bsz, lp)
    y_a, y_b, dh2, loss, d_final_g = _mid_fwd(ya_in, yb_in, proj, hp, loss_target, w["gla_proj"], w["mla_proj"],
                                              w["w_out"], w["final_norm_g"], bsz, lp)
    d_ya, d_o, dproj, delta, d_w_out, d_gla_proj, d_mla_proj = _mid_bwd(
        dh2, y_a, y_b, proj, ya_in, yb_in, o_b, w["w_out"], w["gla_proj"], w["mla_proj"], bsz, lp)
    dproj, d_gate, d_gla_norm = _gla_bwd(proj, gw_pad, w["gla_gate_b"], w["gla_norm_g"], o_raw, s_all, d_ya, dproj,
                                         bsz, lp)
    d_lr, d_gw_pad, d_gate_b = _gate_bwd(d_gate, proj, gw_pad)
    dqf, dkf, dvf = _attn_bwd(qf, kf, vf, d_o, lse, delta, bsz, lp)
    dproj, d_wn, d_wr, d_wt, d_qn = _q_up_bwd(dqf, proj, w["mla_q_norm_g"], wn, wr, wt, cos_t, sin_t, dproj,
                                              bsz, lp)
    dproj, d_wk, d_wv, d_kvn = _kv_up_bwd(dkf, dvf, proj, w["mla_kv_norm_g"], wk, wv, cos_t, sin_t, d_lr, dproj,
                                          bsz, lp)

    d_rope = (d_wr.reshape(MLA_QR, MLA_HEADS, LANE)[:, :, :MLA_ROPE]
              + _unrot_cols(d_wt.reshape(MLA_QR, MLA_HEADS, LANE)[:, :, :MLA_ROPE]))
    d_uq = jnp.concatenate([d_wn.reshape(MLA_QR, MLA_HEADS, LANE), d_rope], axis=-1).reshape(MLA_QR, MLA_HEADS * MLA_QK)
    d_ukv = jnp.concatenate([d_wk.reshape(MLA_KVR, MLA_HEADS, LANE), d_wv.reshape(MLA_KVR, MLA_HEADS, LANE)],
                            axis=-1).reshape(MLA_KVR, MLA_HEADS * (MLA_NOPE + MLA_DV))
    mats = dict(gla_gate_w=d_gw_pad[:GLA_RANK], gla_proj=d_gla_proj, mla_w_uq=d_uq, mla_w_ukv=d_ukv,
                mla_proj=d_mla_proj, w_out=d_w_out)
    packed = _pack_shards({n: _bf(_split8(mats[n], axis)) for n, _, axis in PACKED})
    d_w_ext_t, packed_parts = _dw_in(u, dproj, packed)
    w_in_slabs = _w_in_slabs(d_w_ext_t)
    grad_x, d_norm_g, d_meta, w_in_parts = _dx_in(dproj, w_ext, hp, dh2, w["norm_g"], w_in_slabs, bsz)
    small = dict(meta_tokens=d_meta, norm_g=d_norm_g, gla_gate_b=d_gate_b, gla_norm_g=d_gla_norm,
                 mla_q_norm_g=d_qn, mla_kv_norm_g=d_kvn, final_norm_g=d_final_g)
    return loss, grad_x, w_in_parts, packed_parts, small


PACKED = (("gla_gate_w", (GLA_RANK, GLA_KW // N_DEV), 1),
          ("gla_proj", (D_MODEL // N_DEV, D_MODEL), 0), ("mla_w_uq", (MLA_QR, MLA_HEADS * MLA_QK // N_DEV), 1),
          ("mla_w_ukv", (MLA_KVR, MLA_HEADS * (MLA_NOPE + MLA_DV) // N_DEV), 1),
          ("mla_proj", (D_MODEL // N_DEV, D_MODEL), 0), ("w_out", (D_MODEL // N_DEV, D_MODEL), 0))
REPLICATED = (("norm_g", D_MODEL), ("gla_gate_b", GLA_KW), ("gla_norm_g", GLA_DV), ("mla_q_norm_g", MLA_QR),
              ("mla_kv_norm_g", MLA_KVR), ("final_norm_g", D_MODEL))
PACK_ROWS = 480
PACK_BLOCK = 160
SMALL_ROWS = 48
LOSS_ROW = N_META + 25
W_IN_BLOCK = 128


def _all_gather(shards):
    n_arr = len(shards)
    pieces = []
    for a, s in enumerate(shards):
        step = s.shape[0] // 4 if s.shape[0] >= 4 * LANE else s.shape[0]
        pieces += [(a, slice(r, r + step)) for r in range(0, s.shape[0], step)]
    n_pc = len(pieces)

    def body(*refs):
        x_refs, out_refs = refs[:n_arr], refs[n_arr:2 * n_arr]
        send_sems, recv_sems, local_sems = refs[2 * n_arr:]
        x, y, c = _my_place()
        me, sibling = (x, y, c), (x, y, 1 - c)
        chips = [(1 - x, y), (x, 1 - y), (1 - x, 1 - y)]

        def copy(u, k, block, to, from_input=False):
            a, rows = pieces[u]
            slab = out_refs[a].at[4 * block[0] + 2 * block[1] + block[2], rows]
            return pltpu.make_async_remote_copy(
                src_ref=x_refs[a].at[rows] if from_input else slab, dst_ref=slab,
                send_sem=send_sems.at[7 * u + k], recv_sem=recv_sems.at[7 * u + k], device_id=to,
                device_id_type=MESH_ID)

        arrays = range(n_pc)
        mine = [pltpu.make_async_copy(x_refs[a], out_refs[a].at[4 * x + 2 * y + c], local_sems.at[a])
                for a in range(n_arr)]
        for cp in mine:
            cp.start()
        first = [copy(a, 0, me, sibling, True) for a in arrays]
        first += [copy(a, 1 + j, me, (*chip, c), True) for j, chip in enumerate(chips) for a in arrays]
        for cp in first:
            cp.start()
        passed = []
        for j, chip in enumerate(chips):
            for a in arrays:
                copy(a, 1 + j, (*chip, c), me).wait_recv()
                passed.append(copy(a, 4 + j, (*chip, c), sibling))
                passed[-1].start()
        for a in arrays:
            copy(a, 0, sibling, me).wait_recv()
        for j, chip in enumerate(chips):
            for a in arrays:
                copy(a, 4 + j, (*chip, 1 - c), me).wait_recv()
        for cp in first + passed:
            cp.wait_send()
        for cp in mine:
            cp.wait()

    anyspec = pl.BlockSpec(memory_space=pl.ANY)
    return pl.pallas_call(
        body, name="weights_all_gather",
        out_shape=[jax.ShapeDtypeStruct((N_DEV,) + s.shape, s.dtype) for s in shards],
        in_specs=[anyspec] * n_arr, out_specs=[anyspec] * n_arr,
        scratch_shapes=[pltpu.SemaphoreType.DMA((7 * n_pc,)), pltpu.SemaphoreType.DMA((7 * n_pc,)),
                        pltpu.SemaphoreType.DMA((n_arr,))],
    )(*shards)


def _adamw(parts, w, m, v, block_rows, name, beside=None):
    rows, cols = w.shape
    steps = pl.cdiv(rows, block_rows)

    def update(p_ref, w_ref, m_ref, v_ref, g_out, d_out, m_out, v_out):
        g = p_ref[0].astype(F32)
        for s in range(1, N_DEV):
            g = g + p_ref[s].astype(F32)
        m_new = ADAM_B1 * m_ref[...] + (1.0 - ADAM_B1) * g
        v_new = ADAM_B2 * v_ref[...] + (1.0 - ADAM_B2) * (g * g)
        m_hat = m_new / (1.0 - ADAM_B1 ** ADAM_STEP)
        v_hat = v_new / (1.0 - ADAM_B2 ** ADAM_STEP)
        d = -ADAM_LR * (m_hat / (jnp.sqrt(v_hat) + ADAM_EPS) + ADAM_WD * w_ref[...])
        for o, val in ((g_out, g), (d_out, d), (m_out, m_new), (v_out, v_new)):
            o[...] = val.reshape(o.shape)

    def update_beside_exchange(p_ref, w_ref, m_ref, v_ref, s_ref, g_out, d_out, m_out, v_out, recv_ref,
                               send_sems, recv_sems, local_sem):
        @pl.when(pl.program_id(0) == 0)
        def _():
            _exchange(s_ref, recv_ref, send_sems, recv_sems, local_sem, True)

        update(p_ref, w_ref, m_ref, v_ref, g_out, d_out, m_out, v_out)

        @pl.when(pl.program_id(0) == steps - 1)
        def _():
            _exchange(s_ref, recv_ref, send_sems, recv_sems, local_sem, False)

    spec = pl.BlockSpec((block_rows, cols), lambda i: (i, 0))
    anyspec = pl.BlockSpec(memory_space=pl.ANY)
    in_specs = [pl.BlockSpec((N_DEV, block_rows, cols), lambda i: (0, i, 0)), spec, spec, spec]
    out_shape = [jax.ShapeDtypeStruct((rows, cols), F32)] * 4
    if beside is None:
        return pl.pallas_call(
            update, name=name, grid=(steps,),
            in_specs=in_specs, out_specs=[spec] * 4, out_shape=out_shape,
            compiler_params=_cp(("parallel",), 48),
        )(parts, w, m, v)
    fold = cols // LANE
    lanes = pl.BlockSpec((block_rows * fold, LANE), lambda i: (i, 0))
    *new, recv = pl.pallas_call(
        update_beside_exchange, name=name, grid=(steps,),
        in_specs=in_specs + [anyspec], out_specs=[lanes] * 4 + [anyspec],
        out_shape=[jax.ShapeDtypeStruct((rows * fold, LANE), F32)] * 4
        + [jax.ShapeDtypeStruct(beside.shape, beside.dtype)],
        scratch_shapes=EXCHANGE_SEMS,
        compiler_params=_cp(("arbitrary",), 48),
    )(parts, w, m, v, beside)
    return [o.reshape(rows, cols) for o in new] + [recv]


def _pack_rows_of(shape):
    rows = shape[0] * shape[1] // D_MODEL
    return -(-rows // 16) * 16


def _pack_shards(shards):
    parts = []
    for n, shape, _ in PACKED:
        a = shards[n]
        lead = a.shape[:-2]
        if shape[1] != D_MODEL:
            a = a.reshape(lead + (shape[0] * shape[1] // D_MODEL, D_MODEL))
        pad = _pack_rows_of(shape) - a.shape[-2]
        parts.append(jnp.pad(a, [(0, 0)] * len(lead) + [(0, pad), (0, 0)]) if pad else a)
    return jnp.concatenate(parts, axis=-2)


def _unpack_shards(packed):
    lead, out, off = packed.shape[:-2], {}, 0
    for n, shape, _ in PACKED:
        rows = shape[0] * shape[1] // D_MODEL
        out[n] = packed[..., off:off + rows, :].reshape(lead + shape)
        off += _pack_rows_of(shape)
    return out


def _split8(full, axis):
    r, c = full.shape
    if axis == 0:
        return full.reshape(N_DEV, r // N_DEV, c)
    return full.reshape(r, N_DEV, c // N_DEV).transpose(1, 0, 2)


def _join8(shards, axis):
    _, r, c = shards.shape
    if axis == 0:
        return shards.reshape(N_DEV * r, c)
    return shards.transpose(1, 0, 2).reshape(r, N_DEV * c)


def _pack_small(meta_shard, vals, loss_row):
    rows = jnp.concatenate([vals[n].reshape(-1, LANE) for n, _ in REPLICATED] + [loss_row], axis=0)
    rows = jnp.pad(rows, ((0, SMALL_ROWS - N_META - rows.shape[0]), (0, 0)))
    return jnp.concatenate([meta_shard, jnp.broadcast_to(rows, meta_shard.shape[:-2] + rows.shape)], axis=-2)


def _unpack_small(packed):
    out, off = {"meta_tokens": packed[:N_META]}, N_META
    for n, size in REPLICATED:
        out[n] = packed[off:off + size // LANE].reshape(1, size)
        off += size // LANE
    return out


def kernel(x, meta_tokens, norm_g, w_in, gla_gate_w, gla_gate_b, gla_norm_g, gla_proj, mla_q_norm_g, mla_w_uq, mla_kv_norm_g, mla_w_ukv, mla_proj, w_out, final_norm_g, loss_target, m_meta_tokens, m_norm_g, m_w_in, m_gla_gate_w, m_gla_gate_b, m_gla_norm_g, m_gla_proj, m_mla_q_norm_g, m_mla_w_uq, m_mla_kv_norm_g, m_mla_w_ukv, m_mla_proj, m_w_out, m_final_norm_g, v_meta_tokens, v_norm_g, v_w_in, v_gla_gate_w, v_gla_gate_b, v_gla_norm_g, v_gla_proj, v_mla_q_norm_g, v_mla_w_uq, v_mla_kv_norm_g, v_mla_w_ukv, v_mla_proj, v_w_out, v_final_norm_g):
    given = dict(meta_tokens=meta_tokens, norm_g=norm_g, w_in=w_in, gla_gate_w=gla_gate_w, gla_gate_b=gla_gate_b,
                 gla_norm_g=gla_norm_g, gla_proj=gla_proj, mla_q_norm_g=mla_q_norm_g, mla_w_uq=mla_w_uq,
                 mla_kv_norm_g=mla_kv_norm_g, mla_w_ukv=mla_w_ukv, mla_proj=mla_proj, w_out=w_out,
                 final_norm_g=final_norm_g)
    mom_m = dict(meta_tokens=m_meta_tokens, norm_g=m_norm_g, w_in=m_w_in, gla_gate_w=m_gla_gate_w,
                 gla_gate_b=m_gla_gate_b, gla_norm_g=m_gla_norm_g, gla_proj=m_gla_proj, mla_q_norm_g=m_mla_q_norm_g,
                 mla_w_uq=m_mla_w_uq, mla_kv_norm_g=m_mla_kv_norm_g, mla_w_ukv=m_mla_w_ukv, mla_proj=m_mla_proj,
                 w_out=m_w_out, final_norm_g=m_final_norm_g)
    mom_v = dict(meta_tokens=v_meta_tokens, norm_g=v_norm_g, w_in=v_w_in, gla_gate_w=v_gla_gate_w,
                 gla_gate_b=v_gla_gate_b, gla_norm_g=v_gla_norm_g, gla_proj=v_gla_proj, mla_q_norm_g=v_mla_q_norm_g,
                 mla_w_uq=v_mla_w_uq, mla_kv_norm_g=v_mla_kv_norm_g, mla_w_ukv=v_mla_w_ukv, mla_proj=v_mla_proj,
                 w_out=v_w_out, final_norm_g=v_final_norm_g)
    shapes = {n: a.shape for n, a in given.items()}
    shard2d = {n: s for n, s, _ in PACKED}
    shard2d["w_in"] = (D_MODEL, W_IN_SHARD)
    shard2d["meta_tokens"] = (N_META, LANE)

    def as2d(tree):
        out = {n: tree[n].reshape(shard2d[n]) for n in shard2d}
        out.update({n: tree[n].reshape(1, size) for n, size in REPLICATED})
        return out

    w_loc, m_loc, v_loc = as2d(given), as2d(mom_m), as2d(mom_v)

    w_in_all, meta_all = _all_gather([w_loc["w_in"].astype(BF16), w_loc["meta_tokens"]])
    packed = _pack_shards({n: w_loc[n].astype(BF16) for n, _, _ in PACKED})
    full = {"w_in": w_in_all, "meta_tokens": _join8(meta_all, 1), "packed": packed}
    for n, _ in REPLICATED:
        full[n] = w_loc[n]

    loss_part, grad_x, w_in_parts, packed_parts, small = _local_step(x, loss_target, full)
    small_slabs = _pack_small(_split8(small["meta_tokens"], 1), small, jnp.broadcast_to(loss_part[:, :1], (1, LANE)))

    w_in_t = [t["w_in"].T for t in (w_loc, m_loc, v_loc)]
    *w_in_new, small_all = _adamw(w_in_parts, *w_in_t, W_IN_BLOCK, "adamw_w_in", beside=small_slabs)
    g_w, d_w, m_w, v_w = (o.T for o in w_in_new)
    g_p, d_p, m_p, v_p = _adamw(packed_parts, _pack_shards(w_loc), _pack_shards(m_loc), _pack_shards(v_loc),
                                PACK_BLOCK, "adamw_packed")
    zero_row = jnp.zeros((1, LANE), F32)
    g_s, d_s, m_s, v_s = _adamw(small_all, *(_pack_small(t["meta_tokens"], t, zero_row) for t in (w_loc, m_loc, v_loc)),
                                SMALL_ROWS, "adamw_small")
    loss = g_s[LOSS_ROW, 0]

    order = ["meta_tokens", "norm_g", "w_in", "gla_gate_w", "gla_gate_b", "gla_norm_g", "gla_proj", "mla_q_norm_g",
             "mla_w_uq", "mla_kv_norm_g", "mla_w_ukv", "mla_proj", "w_out", "final_norm_g"]
    result = [loss, grad_x]
    for w_in_out, packed_sh, packed_sm in ((g_w, g_p, g_s), (d_w, d_p, d_s), (m_w, m_p, m_s), (v_w, v_p, v_s)):
        tree = _unpack_shards(packed_sh)
        tree.update(_unpack_small(packed_sm))
        tree["w_in"] = w_in_out
        result += [tree[n].reshape(shapes[n]) for n in order]
    return tuple(result)
```

```python
import jax
import jax.numpy as jnp
from jax import lax
from jax.experimental import pallas as pl
from jax.experimental.pallas import tpu as pltpu

F32 = jnp.float32
BF16 = jnp.bfloat16

D_MODEL = 1024
N_META = 16
EPS = 1e-6
FRONT = 48
X0 = FRONT + N_META
GLA_HEADS, GLA_DK, GLA_DV, GLA_RANK, GLA_CHUNK = 4, 128, 256, 16, 64
GLA_GATE_NORMALIZER = 16.0
GLA_KW = GLA_HEADS * GLA_DK
GLA_VW = GLA_HEADS * GLA_DV
MLA_HEADS, MLA_NOPE, MLA_ROPE, MLA_DV, MLA_QR, MLA_KVR = 8, 128, 64, 128, 256, 128
MLA_QK = MLA_NOPE + MLA_ROPE
ROPE_BASE = 10000.0
LANE = 128
QKW = 2 * LANE

C_V, C_Z, C_Q, C_K = 0, 1024, 2048, 2560
C_MZ, C_GG, C_GM = 3072, 4096, 5120
C_CKV, C_KR, C_KROT, C_LR = 6144, 6272, 6400, 6528
C_CQ = 6656
N_EXT = 6912
O_Q, O_K, O_V, O_LR, O_Z, O_CQ, O_CKV, O_KR, O_MZ, O_GG, O_GM, N_IN = (
    0, 512, 1024, 2048, 2064, 3088, 3344, 3472, 3536, 4560, 5584, 6608)

ADAM_LR, ADAM_B1, ADAM_B2, ADAM_EPS, ADAM_WD, ADAM_STEP = 0.001, 0.9, 0.999, 1e-08, 0.01, 10

N_DEV = 8
TOK = 192
ATT_BLOCK = 352
MXU_DEPTH = 256


def _cp(sems=None, vmem_mb=None):
    kw = {}
    if sems is not None:
        kw["dimension_semantics"] = sems
    if vmem_mb is not None:
        kw["vmem_limit_bytes"] = vmem_mb * 1024 * 1024
    return pltpu.CompilerParams(**kw)


def _dot(a, b):
    return jnp.dot(a, b, preferred_element_type=F32)


def _dot_nt(a, b):
    return lax.dot_general(a, b, (((1,), (1,)), ((), ())), preferred_element_type=F32)


def _dot_tn(a, b):
    return lax.dot_general(a, b, (((0,), (0,)), ((), ())), preferred_element_type=F32)


def _sigmoid(x):
    return 1.0 / (1.0 + jnp.exp(-x))


def _bf(x):
    return x.astype(BF16)


def _big_tok(tp):
    return 4 * TOK if tp % (4 * TOK) == 0 else TOK


def _attn_block(lp):
    return ATT_BLOCK if lp % ATT_BLOCK == 0 else TOK


def _wide_block(lp):
    return 2 * ATT_BLOCK if lp % (2 * ATT_BLOCK) == 0 else _attn_block(lp)


def _proj_in(x, head, norm_g, w_ext, packed):
    bsz, seq, _ = x.shape
    lp = X0 + seq
    tp = bsz * lp
    tm = _attn_block(lp)
    nb = lp // tm
    last = pl.cdiv(seq, tm) - 1

    def body(xa_ref, xb_ref, hd_ref, g_ref, w_ref, p_ref, h_ref, u_ref, o_ref, pall_ref, send_sems, recv_sems, local_sem):
        first = jnp.logical_and(pl.program_id(0) == 0, pl.program_id(1) == 0)

        @pl.when(first)
        def _():
            _exchange(p_ref, pall_ref, send_sems, recv_sems, local_sem, True, same=True)

        front = jnp.where(pl.program_id(1) == 0, hd_ref[...], xa_ref[0, tm - X0:, :])
        h = jnp.concatenate([front, xb_ref[0, :tm - X0, :]], axis=0)
        h_ref[...] = h
        r = lax.rsqrt(jnp.mean(h * h, axis=-1, keepdims=True) + EPS)
        u = _bf(h * r * g_ref[...])
        u_ref[...] = u
        o_ref[...] = _bf(_dot(u, w_ref[...]))

        @pl.when(jnp.logical_and(pl.program_id(0) == bsz - 1, pl.program_id(1) == nb - 1))
        def _():
            _exchange(p_ref, pall_ref, send_sems, recv_sems, local_sem, False, same=True)

    anyspec = pl.BlockSpec(memory_space=pl.ANY)
    tok = lambda width: pl.BlockSpec((tm, width), lambda b, i: (b * nb + i, 0))
    return pl.pallas_call(
        body, name="proj_in", grid=(bsz, nb),
        in_specs=[pl.BlockSpec((1, tm, D_MODEL), lambda b, i: (b, jnp.maximum(i - 1, 0), 0)),
                  pl.BlockSpec((1, tm, D_MODEL), lambda b, i: (b, jnp.minimum(i, last), 0)),
                  pl.BlockSpec((X0, D_MODEL), lambda b, i: (0, 0)),
                  pl.BlockSpec((1, D_MODEL), lambda b, i: (0, 0)),
                  pl.BlockSpec((D_MODEL, N_EXT), lambda b, i: (0, 0), pipeline_mode=pl.Buffered(1)), anyspec],
        out_specs=[tok(D_MODEL), tok(D_MODEL), tok(N_EXT), anyspec],
        out_shape=[jax.ShapeDtypeStruct((tp, D_MODEL), F32), jax.ShapeDtypeStruct((tp, D_MODEL), BF16),
                   jax.ShapeDtypeStruct((tp, N_EXT), BF16),
                   jax.ShapeDtypeStruct((N_DEV,) + packed.shape, packed.dtype)],
        scratch_shapes=EXCHANGE_SEMS,
        compiler_params=_cp(("arbitrary", "arbitrary"), 56),
    )(x, x, head, norm_g, w_ext, packed)


def _gla_group(n_chunks):
    return 11 if n_chunks % 11 == 0 else 3


def _tri_dot(tri, x):
    hi = _bf(x)
    rest = x - hi.astype(F32)
    mid = _bf(rest)
    return _dot(tri, hi) + _dot(tri, mid) + _dot(tri, _bf(rest - mid.astype(F32)))


def _gla_gates(q_ref, k_ref, lr_ref, gw_ref, gb_ref, rows, not_first):
    z = _dot(lr_ref[rows, :], gw_ref[...]) + gb_ref[...]
    logsig = jnp.minimum(z, 0.0) - jnp.log(1.0 + jnp.exp(-jnp.abs(z)))
    row = lax.broadcasted_iota(jnp.int32, (GLA_CHUNK, GLA_KW), 0)
    live = jnp.logical_or(not_first, row >= FRONT)
    g = jnp.where(live, logsig * (1.0 / GLA_GATE_NORMALIZER), 0.0)
    ri = lax.broadcasted_iota(jnp.int32, (GLA_CHUNK, GLA_CHUNK), 0)
    ci = lax.broadcasted_iota(jnp.int32, (GLA_CHUNK, GLA_CHUNK), 1)
    tril = ci <= ri
    b = _tri_dot(_bf(tril.astype(F32)), g)
    bl = jnp.sum(jnp.where(row == GLA_CHUNK - 1, b, 0.0), axis=0, keepdims=True)
    eb, enb, elb, ebl = jnp.exp(b), jnp.exp(-b), jnp.exp(bl - b), jnp.exp(bl)
    q = q_ref[rows, :].astype(F32) * (GLA_DK ** -0.5)
    k = k_ref[rows, :].astype(F32)
    qe, ke, kl = q * eb, k * enb, k * elb
    return dict(z=z, live=live, tril=tril, row=row, eb=eb, enb=enb, elb=elb, ebl=ebl, qe=qe, ke=ke, kl=kl,
                qe_b=_bf(qe), ke_b=_bf(ke), kl_b=_bf(kl))


def _gla_in_specs(n_groups, gla_rows, rev):
    def rb(b, n):
        return b * n_groups + ((n_groups - 1 - n) if rev else n)

    return rb, [pl.BlockSpec((gla_rows, GLA_KW), lambda b, n: (rb(b, n), C_Q // GLA_KW)),
                pl.BlockSpec((gla_rows, GLA_KW), lambda b, n: (rb(b, n), C_K // GLA_KW)),
                pl.BlockSpec((gla_rows, GLA_VW), lambda b, n: (rb(b, n), C_V // GLA_VW)),
                pl.BlockSpec((gla_rows, GLA_VW), lambda b, n: (rb(b, n), C_Z // GLA_VW)),
                pl.BlockSpec((gla_rows, LANE), lambda b, n: (rb(b, n), C_LR // LANE)),
                pl.BlockSpec((LANE, GLA_KW), lambda b, n: (0, 0)),
                pl.BlockSpec((1, GLA_KW), lambda b, n: (0, 0)),
                pl.BlockSpec((1, GLA_DV), lambda b, n: (0, 0))]


def _gla_fwd(proj, gw_pad, gate_b, gla_norm_g, bsz, lp):
    n_chunks = lp // GLA_CHUNK
    gla_group = _gla_group(n_chunks)
    gla_rows = gla_group * GLA_CHUNK
    n_groups = n_chunks // gla_group
    tp = bsz * lp

    def body(q_ref, k_ref, v_ref, z_ref, lr_ref, gw_ref, gb_ref, gn_ref, oraw_ref, ya_ref, sall_ref, st_scr):
        grp = pl.program_id(1)

        @pl.when(grp == 0)
        def _():
            st_scr[...] = jnp.zeros_like(st_scr)

        chunks = [slice(j * GLA_CHUNK, (j + 1) * GLA_CHUNK) for j in range(gla_group)]
        cs = [_gla_gates(q_ref, k_ref, lr_ref, gw_ref, gb_ref, rows, True if j else grp > 0)
              for j, rows in enumerate(chunks)]
        gn = gn_ref[...]
        sts = [st_scr[h] for h in range(GLA_HEADS)]
        heads = [(slice(h * GLA_DK, (h + 1) * GLA_DK), slice(h * GLA_DV, (h + 1) * GLA_DV)) for h in range(GLA_HEADS)]
        a_all = [[_bf(jnp.where(c["tril"], _dot_nt(c["qe_b"][:, ks], c["ke_b"][:, ks]), 0.0)) for ks, _ in heads]
                 for c in cs]
        u_all = [[_dot_tn(v_ref[rows, vs], c["kl_b"][:, ks]) for ks, vs in heads] for rows, c in zip(chunks, cs)]
        for j, (rows, c) in enumerate(zip(chunks, cs)):
            for h, (ks, vs) in enumerate(heads):
                st = sts[h]
                sall_ref[0, j, h] = st
                o = _dot(a_all[j][h], v_ref[rows, vs]) + _dot_nt(c["qe_b"][:, ks], _bf(st))
                sts[h] = st * c["ebl"][:, ks] + u_all[j][h]
                oraw_ref[rows, vs] = o
                r = lax.rsqrt(jnp.mean(o * o, axis=-1, keepdims=True) + EPS)
                zg = z_ref[rows, vs].astype(F32)
                ya_ref[rows, vs] = _bf((o * r * gn) * (zg * _sigmoid(zg)))
        for h in range(GLA_HEADS):
            st_scr[h] = sts[h]

    rb, in_specs = _gla_in_specs(n_groups, gla_rows, False)
    return pl.pallas_call(
        body, name="gla_fwd", grid=(bsz, n_groups), in_specs=in_specs,
        out_specs=[pl.BlockSpec((gla_rows, GLA_VW), lambda b, n: (rb(b, n), 0)),
                   pl.BlockSpec((gla_rows, GLA_VW), lambda b, n: (rb(b, n), 0)),
                   pl.BlockSpec((1, gla_group, GLA_HEADS, GLA_DV, GLA_DK), lambda b, n: (b, n, 0, 0, 0))],
        out_shape=[jax.ShapeDtypeStruct((tp, GLA_VW), F32), jax.ShapeDtypeStruct((tp, GLA_VW), BF16),
                   jax.ShapeDtypeStruct((bsz, n_chunks, GLA_HEADS, GLA_DV, GLA_DK), F32)],
        scratch_shapes=[pltpu.VMEM((GLA_HEADS, GLA_DV, GLA_DK), F32)],
        compiler_params=_cp(("parallel", "arbitrary"), 56),
    )(proj, proj, proj, proj, proj, gw_pad, gate_b, gla_norm_g)


def _gla_bwd(proj, gw_pad, gate_b, gla_norm_g, o_raw, s_all, d_ya, dproj, bsz, lp):
    n_chunks = lp // GLA_CHUNK
    gla_group = _gla_group(n_chunks)
    gla_rows = gla_group * GLA_CHUNK
    n_groups = n_chunks // gla_group
    tp = bsz * lp

    def body(q_ref, k_ref, v_ref, z_ref, lr_ref, gw_ref, gb_ref, gn_ref, o_ref, s_ref, dya_ref, _,
             dp_ref, dz_ref, dgn_ref, dst_scr):
        dv_ref, dzg_ref = dp_ref.at[:, C_V:C_V + GLA_VW], dp_ref.at[:, C_Z:C_Z + GLA_VW]

        @pl.when(jnp.logical_and(pl.program_id(0) == 0, pl.program_id(1) == 0))
        def _():
            dgn_ref[...] = jnp.zeros_like(dgn_ref)

        @pl.when(pl.program_id(1) == 0)
        def _():
            dst_scr[...] = jnp.zeros_like(dst_scr)

        grp = n_groups - 1 - pl.program_id(1)
        chunks = [slice(j * GLA_CHUNK, (j + 1) * GLA_CHUNK) for j in range(gla_group)]
        cs = [_gla_gates(q_ref, k_ref, lr_ref, gw_ref, gb_ref, rows, True if j else grp > 0)
              for j, rows in enumerate(chunks)]
        gn = gn_ref[...]
        dgn = jnp.zeros((1, GLA_DV), F32)
        dqe_h, dke_h, dkl_h, dbl_h = ([[None] * GLA_HEADS for _ in chunks] for _ in range(4))
        dsts = [dst_scr[h] for h in range(GLA_HEADS)]
        for j in reversed(range(gla_group)):
            rows, c = chunks[j], cs[j]
            for h in range(GLA_HEADS):
                ks, vs = slice(h * GLA_DK, (h + 1) * GLA_DK), slice(h * GLA_DV, (h + 1) * GLA_DV)
                dst = dsts[h]
                v = v_ref[rows, vs]
                st = s_ref[0, j, h]
                o = o_ref[rows, vs]
                r = lax.rsqrt(jnp.mean(o * o, axis=-1, keepdims=True) + EPS)
                xh = o * r
                zg = z_ref[rows, vs].astype(F32)
                sg = _sigmoid(zg)
                dy = dya_ref[rows, vs].astype(F32)
                dzg_ref[rows, vs] = _bf(dy * (xh * gn) * (sg * (1.0 + zg * (1.0 - sg))))
                t = dy * (zg * sg)
                dgn += jnp.sum(t * xh, axis=0, keepdims=True)
                dxh = t * gn
                do_b = _bf(r * (dxh - xh * jnp.mean(dxh * xh, axis=-1, keepdims=True)))
                qe_b, ke_b, kl_b, dst_b = c["qe_b"][:, ks], c["ke_b"][:, ks], c["kl_b"][:, ks], _bf(dst)
                a = jnp.where(c["tril"], _dot_nt(qe_b, ke_b), 0.0)
                da_b = _bf(jnp.where(c["tril"], _dot_nt(do_b, v), 0.0))
                dqe_h[j][h] = _dot(da_b, ke_b) + _dot(do_b, _bf(st))
                dke_h[j][h] = _dot_tn(da_b, qe_b)
                dkl = _dot(v, dst_b)
                dkl_h[j][h] = dkl
                dv_ref[rows, vs] = _bf(_dot_tn(_bf(a), do_b) + _dot_nt(kl_b, dst_b))
                ddecay = jnp.sum(dst * st, axis=0, keepdims=True)
                dbl_h[j][h] = jnp.sum(dkl * c["kl"][:, ks], axis=0, keepdims=True) + ddecay * c["ebl"][:, ks]
                dsts[h] = dst * c["ebl"][:, ks] + _dot_tn(do_b, qe_b)
        for h in range(GLA_HEADS):
            dst_scr[h] = dsts[h]
        dgn_ref[...] += dgn
        ri = lax.broadcasted_iota(jnp.int32, (GLA_CHUNK, GLA_CHUNK), 0)
        ci = lax.broadcasted_iota(jnp.int32, (GLA_CHUNK, GLA_CHUNK), 1)
        triu = _bf((ci >= ri).astype(F32))
        for j, (rows, c) in enumerate(zip(chunks, cs)):
            dqe, dke, dkl, dbl = (jnp.concatenate(p[j], axis=1) for p in (dqe_h, dke_h, dkl_h, dbl_h))
            db = dqe * c["qe"] - dke * c["ke"] - dkl * c["kl"] + jnp.where(c["row"] == GLA_CHUNK - 1, dbl, 0.0)
            dg = _tri_dot(triu, db)
            dg = jnp.where(c["live"], dg, 0.0)
            dz_ref[rows, :] = dg * (1.0 / GLA_GATE_NORMALIZER) * _sigmoid(-c["z"])
            dp_ref[rows, C_Q:C_Q + GLA_KW] = _bf(dqe * c["eb"] * (GLA_DK ** -0.5))
            dp_ref[rows, C_K:C_K + GLA_KW] = _bf(dke * c["enb"] + dkl * c["elb"])

    rb, in_specs = _gla_in_specs(n_groups, gla_rows, True)
    wide = pl.BlockSpec((gla_rows, GLA_VW), lambda b, n: (rb(b, n), 0))
    group = C_MZ
    return pl.pallas_call(
        body, name="gla_bwd", grid=(bsz, n_groups),
        in_specs=in_specs + [wide, pl.BlockSpec((1, gla_group, GLA_HEADS, GLA_DV, GLA_DK),
                                                lambda b, n: (b, n_groups - 1 - n, 0, 0, 0)), wide,
                             pl.BlockSpec(memory_space=pl.ANY)],
        out_specs=[pl.BlockSpec((gla_rows, group), lambda b, n: (rb(b, n), 0)),
                   pl.BlockSpec((gla_rows, GLA_KW), lambda b, n: (rb(b, n), 0)),
                   pl.BlockSpec((1, GLA_DV), lambda b, n: (0, 0))],
        out_shape=[jax.ShapeDtypeStruct((tp, N_EXT), BF16), jax.ShapeDtypeStruct((tp, GLA_KW), F32),
                   jax.ShapeDtypeStruct((1, GLA_DV), F32)],
        input_output_aliases={11: 0},
        scratch_shapes=[pltpu.VMEM((GLA_HEADS, GLA_DV, GLA_DK), F32)],
        compiler_params=_cp(("arbitrary", "arbitrary"), 56),
    )(proj, proj, proj, proj, proj, gw_pad, gate_b, gla_norm_g, o_raw, s_all, d_ya, dproj)


def _gate_bwd(dz, proj, gw_pad):
    tp = dz.shape[0]
    tm = _big_tok(tp)

    def body(dz_ref, lr_ref, gw_ref, dlr_ref, dgw_ref, dgb_ref):
        @pl.when(pl.program_id(0) == 0)
        def _():
            dgw_ref[...] = jnp.zeros_like(dgw_ref)
            dgb_ref[...] = jnp.zeros_like(dgb_ref)

        dz = dz_ref[...]
        dz_b = _bf(dz)
        dlr_ref[...] = _bf(_dot_nt(dz_b, gw_ref[...]))
        dgw_ref[...] += _dot_tn(lr_ref[...], dz_b)
        dgb_ref[...] += jnp.sum(dz, axis=0, keepdims=True)

    return pl.pallas_call(
        body, name="gate_bwd", grid=(tp // tm,),
        in_specs=[pl.BlockSpec((tm, GLA_KW), lambda i: (i, 0)),
                  pl.BlockSpec((tm, LANE), lambda i: (i, C_LR // LANE)),
                  pl.BlockSpec((LANE, GLA_KW), lambda i: (0, 0))],
        out_specs=[pl.BlockSpec((tm, LANE), lambda i: (i, 0)),
                   pl.BlockSpec((LANE, GLA_KW), lambda i: (0, 0)),
                   pl.BlockSpec((1, GLA_KW), lambda i: (0, 0))],
        out_shape=[jax.ShapeDtypeStruct((tp, LANE), BF16), jax.ShapeDtypeStruct((LANE, GLA_KW), F32),
                   jax.ShapeDtypeStruct((1, GLA_KW), F32)],
        compiler_params=_cp(("arbitrary",)),
    )(dz, proj, gw_pad)


def _rms_fwd(x):
    r = lax.rsqrt(jnp.mean(x * x, axis=-1, keepdims=True) + EPS)
    return x * r, r


def _rms_bwd(dy, xh, r, g):
    dxh = dy * g
    dx = r * (dxh - xh * jnp.mean(dxh * xh, axis=-1, keepdims=True))
    return dx, jnp.sum(dy * xh, axis=0, keepdims=True)


def _q_up(proj, q_norm_g, wn, wr, wt, cos_t, sin_t, bsz, lp):
    tp = bsz * lp
    tok = _wide_block(lp)
    nb = lp // tok

    def body(cq_ref, g_ref, wn_ref, wr_ref, wt_ref, cos_ref, sin_ref, q_ref):
        xh, _ = _rms_fwd(cq_ref[...].astype(F32))
        cqn = _bf(xh * g_ref[...])
        nope = _dot(cqn, wn_ref[...])
        rope = _dot(cqn, wr_ref[...])
        rot = _dot(cqn, wt_ref[...])
        cos, sin = cos_ref[...], sin_ref[...]
        one = (lax.broadcasted_iota(jnp.int32, (tok, LANE), 1) == BIAS_LANE).astype(F32)
        for h in range(MLA_HEADS):
            sl = slice(h * LANE, (h + 1) * LANE)
            q_ref[:, h * QKW:h * QKW + LANE] = _bf(nope[:, sl])
            q_ref[:, h * QKW + LANE:(h + 1) * QKW] = _bf(rope[:, sl] * cos + rot[:, sl] * sin + one)

    wspec = pl.BlockSpec((MLA_QR, MLA_HEADS * LANE), lambda b, i: (0, 0))
    tspec = pl.BlockSpec((tok, LANE), lambda b, i: (i, 0))
    return pl.pallas_call(
        body, name="mla_q_up", grid=(bsz, nb),
        in_specs=[pl.BlockSpec((tok, MLA_QR), lambda b, i: (b * nb + i, C_CQ // MLA_QR)),
                  pl.BlockSpec((1, MLA_QR), lambda b, i: (0, 0)), wspec, wspec, wspec, tspec, tspec],
        out_specs=pl.BlockSpec((tok, MLA_HEADS * QKW), lambda b, i: (b * nb + i, 0)),
        out_shape=jax.ShapeDtypeStruct((tp, MLA_HEADS * QKW), BF16),
        compiler_params=_cp(("parallel", "parallel")),
    )(proj, q_norm_g, wn, wr, wt, cos_t, sin_t)


def _kv_up(proj, kv_norm_g, wk, wv, cos_t, sin_t, bsz, lp):
    tp = bsz * lp
    tok = _wide_block(lp)
    nb = lp // tok

    def body(ckv_ref, kr_ref, krot_ref, g_ref, wk_ref, wv_ref, cos_ref, sin_ref, k_ref, v_ref):
        xh, _ = _rms_fwd(ckv_ref[...].astype(F32))
        cn = _bf(xh * g_ref[...])
        kn = _dot(cn, wk_ref[...])
        v_ref[...] = _bf(_dot(cn, wv_ref[...]))
        pos = pl.program_id(1) * tok + lax.broadcasted_iota(jnp.int32, (tok, LANE), 0)
        lane = lax.broadcasted_iota(jnp.int32, (tok, LANE), 1)
        bias = jnp.where(jnp.logical_and(lane == BIAS_LANE, pos < FRONT), KEY_BIAS, 0.0)
        kr = _bf(kr_ref[...].astype(F32) * cos_ref[...] + krot_ref[...].astype(F32) * sin_ref[...] + bias)
        for h in range(MLA_HEADS):
            k_ref[:, h * QKW:h * QKW + LANE] = _bf(kn[:, h * LANE:(h + 1) * LANE])
            k_ref[:, h * QKW + LANE:(h + 1) * QKW] = kr

    wspec = pl.BlockSpec((MLA_KVR, MLA_HEADS * LANE), lambda b, i: (0, 0))
    tspec = pl.BlockSpec((tok, LANE), lambda b, i: (i, 0))
    return pl.pallas_call(
        body, name="mla_kv_up", grid=(bsz, nb),
        in_specs=[pl.BlockSpec((tok, LANE), lambda b, i: (b * nb + i, C_CKV // LANE)),
                  pl.BlockSpec((tok, LANE), lambda b, i: (b * nb + i, C_KR // LANE)),
                  pl.BlockSpec((tok, LANE), lambda b, i: (b * nb + i, C_KROT // LANE)),
                  pl.BlockSpec((1, MLA_KVR), lambda b, i: (0, 0)), wspec, wspec, tspec, tspec],
        out_specs=[pl.BlockSpec((tok, MLA_HEADS * QKW), lambda b, i: (b * nb + i, 0)),
                   pl.BlockSpec((tok, MLA_HEADS * LANE), lambda b, i: (b * nb + i, 0))],
        out_shape=[jax.ShapeDtypeStruct((tp, MLA_HEADS * QKW), BF16),
                   jax.ShapeDtypeStruct((tp, MLA_HEADS * LANE), BF16)],
        compiler_params=_cp(("parallel", "parallel")),
    )(proj, proj, proj, kv_norm_g, wk, wv, cos_t, sin_t)


ATT_SCALE = MLA_QK ** -0.5


KEY_BIAS = -1e30
BIAS_LANE = MLA_ROPE
NEG = 2 * KEY_BIAS
LOG2E = 1.4426950408889634
EXP2_SCALE = ATT_SCALE * LOG2E


def _causal_fill(s, r0, fill):
    tq, kmax = s.shape
    a = r0 // LANE * LANE
    mask = (a + lax.broadcasted_iota(jnp.int32, (tq, kmax - a), 1)
            <= r0 + lax.broadcasted_iota(jnp.int32, (tq, kmax - a), 0))
    right = jnp.where(mask, s[:, a:], fill)
    return jnp.concatenate([s[:, :a], right], axis=1) if a else right


def _attn_fwd(qf, kf, vf, proj, bsz, lp):
    tp = bsz * lp
    tq = _attn_block(lp)
    nh = 2

    def body(q_ref, k_ref, v_ref, mz_ref, ob_ref, yb_ref, lse_ref):
        starts = list(range(0, lp, tq))
        for pair in (starts[i:i + 2] for i in range(0, len(starts), 2)):
            work = [(r0, h) for r0 in pair for h in range(nh)]
            ss = [_causal_fill(_dot_nt(q_ref[r0:r0 + tq, h * QKW:(h + 1) * QKW],
                                       k_ref[0:r0 + tq, h * QKW:(h + 1) * QKW]), r0, NEG) for r0, h in work]
            ms = [jnp.max(s, axis=-1, keepdims=True) for s in ss]
            ps = [jnp.exp2((s - m) * EXP2_SCALE) for s, m in zip(ss, ms)]
            ls = [jnp.sum(p, axis=-1, keepdims=True) for p in ps]
            for (r0, h), p, m, l in zip(work, ps, ms, ls):
                rows, cols = slice(r0, r0 + tq), slice(h * MLA_DV, (h + 1) * MLA_DV)
                o = _dot(_bf(p), v_ref[0:r0 + tq, cols]) / l
                ob_ref[rows, cols] = _bf(o)
                mz = mz_ref[rows, cols].astype(F32)
                yb_ref[rows, cols] = _bf(o * (mz * _sigmoid(mz)))
                lse_ref[0, h, rows, :] = jnp.broadcast_to(m * EXP2_SCALE + jnp.log2(l), (tq, LANE))

    head = lambda off: pl.BlockSpec((lp, nh * MLA_DV), lambda b, h: (b, off + h))
    wide = pl.BlockSpec((lp, nh * QKW), lambda b, h: (b, h))
    return pl.pallas_call(
        body, name="mla_attn_fwd", grid=(bsz, MLA_HEADS // nh),
        in_specs=[wide, wide, head(0), head(C_MZ // (nh * MLA_DV))],
        out_specs=[head(0), head(0), pl.BlockSpec((1, nh, lp, LANE), lambda b, h: (b, h, 0, 0))],
        out_shape=[jax.ShapeDtypeStruct((tp, MLA_HEADS * MLA_DV), BF16),
                   jax.ShapeDtypeStruct((tp, MLA_HEADS * MLA_DV), BF16),
                   jax.ShapeDtypeStruct((bsz, MLA_HEADS, lp, LANE), F32)],
        compiler_params=_cp(("parallel", "parallel"), 56),
    )(qf, kf, vf, proj)


def _attn_bwd_blocks(lp):
    return [(0, X0)] + [(r0, min(MXU_DEPTH, lp - r0)) for r0 in range(X0, lp, MXU_DEPTH)]


def _attn_bwd(qf, kf, vf, d_o, lse, delta, bsz, lp):
    tp = bsz * lp

    def body(q_ref, k_ref, v_ref, do_ref, lse_ref, dl_ref, dq_ref, dk_ref, dv_ref, dk_acc, dv_acc):
        dk_acc[...] = jnp.zeros_like(dk_acc)
        dv_acc[...] = jnp.zeros_like(dv_acc)
        for r0, tq in _attn_bwd_blocks(lp):
            rows, kmax = slice(r0, r0 + tq), r0 + tq
            q, do = q_ref[rows, :], do_ref[rows, :]
            k, v = k_ref[0:kmax, :], v_ref[0:kmax, :]
            p = jnp.exp2(_dot_nt(q, k) * EXP2_SCALE - lse_ref[0, 0, rows, :][:, :1])
            p = _causal_fill(p, r0, 0.0)
            ds = _bf(p * (_dot_nt(do, v) - dl_ref[0, rows, :][:, :1]))
            dq_ref[rows, :] = _bf(_dot(ds, k) * ATT_SCALE)
            dk_acc[0:kmax, :] += _dot_tn(ds, q)
            dv_acc[0:kmax, :] += _dot_tn(_bf(p), do)
        dk_ref[...] = _bf(dk_acc[...] * ATT_SCALE)
        dv_ref[...] = _bf(dv_acc[...])

    wide = pl.BlockSpec((lp, QKW), lambda b, h: (b, h))
    narrow = pl.BlockSpec((lp, MLA_DV), lambda b, h: (b, h))
    stat = pl.BlockSpec((1, 1, lp, LANE), lambda b, h: (b, h, 0, 0))
    return pl.pallas_call(
        body, name="mla_attn_bwd", grid=(bsz, MLA_HEADS),
        in_specs=[wide, wide, narrow, narrow, stat, pl.BlockSpec((1, lp, LANE), lambda b, h: (h, b, 0))],
        out_specs=[wide, wide, narrow],
        out_shape=[jax.ShapeDtypeStruct((tp, MLA_HEADS * QKW), BF16), jax.ShapeDtypeStruct((tp, MLA_HEADS * QKW), BF16),
                   jax.ShapeDtypeStruct((tp, MLA_HEADS * MLA_DV), BF16)],
        scratch_shapes=[pltpu.VMEM((lp, QKW), F32), pltpu.VMEM((lp, MLA_DV), F32)],
        compiler_params=_cp(("parallel", "parallel"), 56),
    )(qf, kf, vf, d_o, lse, delta)


def _q_up_bwd(dqf, proj, q_norm_g, wn, wr, wt, cos_t, sin_t, dproj, bsz, lp):
    tp = bsz * lp
    tok = _wide_block(lp)
    nb = lp // tok
    hw = MLA_HEADS * LANE

    def body(dq_ref, cq_ref, g_ref, wn_ref, wr_ref, wt_ref, cos_ref, sin_ref, _,
             dcq_ref, dwn_ref, dwr_ref, dwt_ref, dg_ref):
        @pl.when(jnp.logical_and(pl.program_id(0) == 0, pl.program_id(1) == 0))
        def _():
            for r in (dwn_ref, dwr_ref, dwt_ref, dg_ref):
                r[...] = jnp.zeros_like(r)

        g = g_ref[...]
        xh, r = _rms_fwd(cq_ref[...].astype(F32))
        cqn = _bf(xh * g)
        dn = jnp.concatenate([dq_ref[:, h * QKW:h * QKW + LANE] for h in range(MLA_HEADS)], axis=1)
        dr = jnp.concatenate([dq_ref[:, h * QKW + LANE:(h + 1) * QKW] for h in range(MLA_HEADS)], axis=1).astype(F32)
        dr_c = _bf(dr * jnp.tile(cos_ref[...], (1, MLA_HEADS)))
        dr_s = _bf(dr * jnp.tile(sin_ref[...], (1, MLA_HEADS)))
        dcqn = _dot_nt(dn, wn_ref[...]) + _dot_nt(dr_c, wr_ref[...]) + _dot_nt(dr_s, wt_ref[...])
        dwn_ref[...] += _dot_tn(cqn, dn)
        dwr_ref[...] += _dot_tn(cqn, dr_c)
        dwt_ref[...] += _dot_tn(cqn, dr_s)
        dx, dg = _rms_bwd(dcqn, xh, r, g)
        dcq_ref[...] = _bf(dx)
        dg_ref[...] += dg

    aspec = pl.BlockSpec((MLA_QR, hw), lambda b, i: (0, 0))
    tspec = pl.BlockSpec((tok, LANE), lambda b, i: (i, 0))
    return pl.pallas_call(
        body, name="mla_q_up_bwd", grid=(bsz, nb),
        in_specs=[pl.BlockSpec((tok, MLA_HEADS * QKW), lambda b, i: (b * nb + i, 0)),
                  pl.BlockSpec((tok, MLA_QR), lambda b, i: (b * nb + i, C_CQ // MLA_QR)),
                  pl.BlockSpec((1, MLA_QR), lambda b, i: (0, 0)), aspec, aspec, aspec, tspec, tspec,
                  pl.BlockSpec(memory_space=pl.ANY)],
        out_specs=[pl.BlockSpec((tok, MLA_QR), lambda b, i: (b * nb + i, C_CQ // MLA_QR)), aspec, aspec, aspec,
                   pl.BlockSpec((1, MLA_QR), lambda b, i: (0, 0))],
        out_shape=[jax.ShapeDtypeStruct((tp, N_EXT), BF16)] + [jax.ShapeDtypeStruct((MLA_QR, hw), F32)] * 3
        + [jax.ShapeDtypeStruct((1, MLA_QR), F32)],
        input_output_aliases={8: 0},
        compiler_params=_cp(("arbitrary", "arbitrary")),
    )(dqf, proj, q_norm_g, wn, wr, wt, cos_t, sin_t, dproj)


def _kv_up_bwd(dkf, dvf, proj, kv_norm_g, wk, wv, cos_t, sin_t, d_lr, dproj, bsz, lp):
    tp = bsz * lp
    tok = _wide_block(lp)
    nb = lp // tok
    hw = MLA_HEADS * LANE

    def body(dk_ref, dv_ref, ckv_ref, g_ref, wk_ref, wv_ref, cos_ref, sin_ref, dlr_ref, _,
             dp_ref, dwk_ref, dwv_ref, dg_ref):
        dckv_ref, dkr_ref, dkrot_ref = (dp_ref.at[:, j * LANE:(j + 1) * LANE] for j in range(3))
        dp_ref[:, 3 * LANE:] = dlr_ref[...]
        @pl.when(jnp.logical_and(pl.program_id(0) == 0, pl.program_id(1) == 0))
        def _():
            for r in (dwk_ref, dwv_ref, dg_ref):
                r[...] = jnp.zeros_like(r)

        g = g_ref[...]
        xh, r = _rms_fwd(ckv_ref[...].astype(F32))
        cn = _bf(xh * g)
        dv = dv_ref[...]
        dn = jnp.concatenate([dk_ref[:, h * QKW:h * QKW + LANE] for h in range(MLA_HEADS)], axis=1)
        dcn = _dot_nt(dv, wv_ref[...]) + _dot_nt(dn, wk_ref[...])
        dwv_ref[...] += _dot_tn(cn, dv)
        dwk_ref[...] += _dot_tn(cn, dn)
        drope = jnp.zeros((tok, LANE), F32)
        for h in range(MLA_HEADS):
            drope += dk_ref[:, h * QKW + LANE:(h + 1) * QKW].astype(F32)
        dkr_ref[...] = _bf(drope * cos_ref[...])
        dkrot_ref[...] = _bf(drope * sin_ref[...])
        dx, dg = _rms_bwd(dcn, xh, r, g)
        dckv_ref[...] = _bf(dx)
        dg_ref[...] += dg

    aspec = pl.BlockSpec((MLA_KVR, hw), lambda b, i: (0, 0))
    tspec = pl.BlockSpec((tok, LANE), lambda b, i: (i, 0))
    ospec = pl.BlockSpec((tok, LANE), lambda b, i: (b * nb + i, 0))
    return pl.pallas_call(
        body, name="mla_kv_up_bwd", grid=(bsz, nb),
        in_specs=[pl.BlockSpec((tok, MLA_HEADS * QKW), lambda b, i: (b * nb + i, 0)),
                  pl.BlockSpec((tok, hw), lambda b, i: (b * nb + i, 0)),
                  pl.BlockSpec((tok, LANE), lambda b, i: (b * nb + i, C_CKV // LANE)),
                  pl.BlockSpec((1, MLA_KVR), lambda b, i: (0, 0)), aspec, aspec, tspec, tspec, ospec,
                  pl.BlockSpec(memory_space=pl.ANY)],
        out_specs=[pl.BlockSpec((tok, 4 * LANE), lambda b, i: (b * nb + i, C_CKV // (4 * LANE))), aspec, aspec,
                   pl.BlockSpec((1, MLA_KVR), lambda b, i: (0, 0))],
        out_shape=[jax.ShapeDtypeStruct((tp, N_EXT), BF16)] + [jax.ShapeDtypeStruct((MLA_KVR, hw), F32)] * 2
        + [jax.ShapeDtypeStruct((1, MLA_KVR), F32)],
        input_output_aliases={9: 0},
        compiler_params=_cp(("arbitrary", "arbitrary")),
    )(dkf, dvf, proj, kv_norm_g, wk, wv, cos_t, sin_t, d_lr, dproj)


def _mid_fwd(ya_in, yb_in, proj, hp, target, w_gp, w_mp, w_o, final_g, bsz, lp):
    tp = bsz * lp
    tm = _wide_block(lp)
    nb = lp // tm
    last = pl.cdiv(lp - X0, tm) - 1

    def body(ya_ref, yb_ref, gg_ref, gm_ref, h_ref, ta_ref, tb_ref, wgp_ref, wmp_ref, wo_ref, fg_ref,
             ya_out, yb_out, dh_ref, loss_ref, dfg_ref):
        @pl.when(jnp.logical_and(pl.program_id(0) == 0, pl.program_id(1) == 0))
        def _():
            loss_ref[...] = jnp.zeros_like(loss_ref)
            dfg_ref[...] = jnp.zeros_like(dfg_ref)

        y_a = _dot(ya_ref[...], wgp_ref[...])
        y_b = _dot(yb_ref[...], wmp_ref[...])
        ya_out[...] = _bf(y_a)
        yb_out[...] = _bf(y_b)
        merged = _sigmoid(gg_ref[...].astype(F32)) * y_a + _sigmoid(gm_ref[...].astype(F32)) * y_b
        h2 = h_ref[...] + _dot(_bf(merged), wo_ref[...])
        fg = fg_ref[...]
        xh, r = _rms_fwd(h2)
        pos = pl.program_id(1) * tm + lax.broadcasted_iota(jnp.int32, (tm, 1), 0)
        t = jnp.concatenate([ta_ref[0, tm - X0:, :], tb_ref[0, :tm - X0, :]], axis=0)
        err = jnp.where(pos >= X0, xh * fg - t, 0.0)
        loss_ref[...] += 0.5 * jnp.sum(jnp.mean(err * err, axis=-1, keepdims=True), axis=0, keepdims=True)
        dy = err * (1.0 / D_MODEL)
        dx, dfg = _rms_bwd(dy, xh, r, fg)
        dh_ref[...] = dx
        dfg_ref[...] += dfg

    tok = lambda c: pl.BlockSpec((tm, D_MODEL), lambda b, i: (b * nb + i, c))
    wspec = pl.BlockSpec((D_MODEL, D_MODEL), lambda b, i: (0, 0), pipeline_mode=pl.Buffered(1))
    return pl.pallas_call(
        body, name="mid_fwd", grid=(bsz, nb),
        in_specs=[tok(0), tok(0), tok(C_GG // D_MODEL), tok(C_GM // D_MODEL), tok(0),
                  pl.BlockSpec((1, tm, D_MODEL), lambda b, i: (b, jnp.maximum(i - 1, 0), 0)),
                  pl.BlockSpec((1, tm, D_MODEL), lambda b, i: (b, jnp.minimum(i, last), 0)),
                  wspec, wspec, wspec, pl.BlockSpec((1, D_MODEL), lambda b, i: (0, 0))],
        out_specs=[tok(0), tok(0), tok(0), pl.BlockSpec((1, LANE), lambda b, i: (0, 0)),
                   pl.BlockSpec((1, D_MODEL), lambda b, i: (0, 0))],
        out_shape=[jax.ShapeDtypeStruct((tp, D_MODEL), BF16), jax.ShapeDtypeStruct((tp, D_MODEL), BF16),
                   jax.ShapeDtypeStruct((tp, D_MODEL), F32), jax.ShapeDtypeStruct((1, LANE), F32),
                   jax.ShapeDtypeStruct((1, D_MODEL), F32)],
        compiler_params=_cp(("arbitrary", "arbitrary"), 56),
    )(ya_in, yb_in, proj, proj, hp, target, target, w_gp, w_mp, w_o, final_g)


def _mid_bwd(dh2, y_a, y_b, proj, ya_in, yb_in, o_b, w_o, w_gp, w_mp, bsz, lp):
    tp = bsz * lp
    tm = MXU_DEPTH if tp % MXU_DEPTH == 0 else _attn_block(lp)
    nsteps = tp // tm
    group = 3 * D_MODEL

    def body(dh_ref, ya_ref, yb_ref, mz_ref, gg_ref, gm_ref, yai_ref, ybi_ref, ob_ref, wo_ref, wgp_ref, wmp_ref,
             dyai_ref, do_ref, dp_ref, dl_ref, dwo_ref, dwgp_ref, dwmp_ref, a_o, a_gp, a_mp):
        @pl.when(pl.program_id(0) == 0)
        def _():
            for r in (a_o, a_gp, a_mp):
                r[...] = jnp.zeros_like(r)

        dh = _bf(dh_ref[...])
        dm = _dot_nt(dh, wo_ref[...])
        y_a, y_b = ya_ref[...].astype(F32), yb_ref[...].astype(F32)
        sg, sm = _sigmoid(gg_ref[...].astype(F32)), _sigmoid(gm_ref[...].astype(F32))
        d_ya, d_yb = _bf(sg * dm), _bf(sm * dm)
        dp_ref[:, D_MODEL:2 * D_MODEL] = _bf(dm * y_a * sg * (1.0 - sg))
        dp_ref[:, 2 * D_MODEL:] = _bf(dm * y_b * sm * (1.0 - sm))
        merged = _bf(sg * y_a + sm * y_b)
        dy = _dot_nt(d_yb, wmp_ref[...])
        dyai_ref[...] = _bf(_dot_nt(d_ya, wgp_ref[...]))
        a_o[...] += _dot_tn(merged, dh)
        a_gp[...] += _dot_tn(yai_ref[...], d_ya)
        a_mp[...] += _dot_tn(ybi_ref[...], d_yb)
        mz, o = mz_ref[...].astype(F32), ob_ref[...].astype(F32)
        s = _sigmoid(mz)
        do = _bf(dy * (mz * s))
        do_ref[...] = do
        dp_ref[:, :D_MODEL] = _bf(dy * o * (s * (1.0 + mz * (1.0 - s))))
        prod = do.astype(F32) * o
        for h in range(MLA_HEADS):
            dl = jnp.sum(prod[:, h * MLA_DV:(h + 1) * MLA_DV], axis=-1, keepdims=True)
            dl_ref[h] = jnp.broadcast_to(dl, (tm, LANE))

        @pl.when(pl.program_id(0) == nsteps - 1)
        def _():
            pltpu.sync_copy(a_o, dwo_ref)
            pltpu.sync_copy(a_gp, dwgp_ref)
            pltpu.sync_copy(a_mp, dwmp_ref)

    tok = lambda c: pl.BlockSpec((tm, D_MODEL), lambda i: (i, c))
    wspec = pl.BlockSpec((D_MODEL, D_MODEL), lambda i: (0, 0))
    anyspec = pl.BlockSpec(memory_space=pl.ANY)
    wshape = jax.ShapeDtypeStruct((D_MODEL, D_MODEL), F32)
    return pl.pallas_call(
        body, name="mid_bwd", grid=(nsteps,),
        in_specs=[tok(0), tok(0), tok(0), tok(C_MZ // D_MODEL), tok(C_GG // D_MODEL), tok(C_GM // D_MODEL),
                  tok(0), tok(0), tok(0), wspec, wspec, wspec],
        out_specs=[tok(0), tok(0), pl.BlockSpec((tm, group), lambda i: (i, C_MZ // group)),
                   pl.BlockSpec((MLA_HEADS, tm, LANE), lambda i: (0, i, 0)), anyspec, anyspec, anyspec],
        out_shape=[jax.ShapeDtypeStruct((tp, D_MODEL), BF16)] * 2 + [jax.ShapeDtypeStruct((tp, N_EXT), BF16),
                   jax.ShapeDtypeStruct((MLA_HEADS, tp, LANE), F32)] + [wshape] * 3,
        scratch_shapes=[pltpu.VMEM((D_MODEL, D_MODEL), F32)] * 3,
        compiler_params=_cp(("arbitrary",), 56),
    )(dh2, y_a, y_b, proj, proj, proj, ya_in, yb_in, o_b, w_o, w_gp, w_mp)


MESH_ID = pl.DeviceIdType.MESH
EXCHANGE_SEMS = [pltpu.SemaphoreType.DMA((N_DEV - 1,)), pltpu.SemaphoreType.DMA((N_DEV - 1,)), pltpu.SemaphoreType.DMA]


def _my_place():
    return lax.axis_index("x"), lax.axis_index("y"), lax.axis_index("c")


def _exchange(g_ref, recv_ref, send_sems, recv_sems, local_sem, start, same=False):
    x, y, c = _my_place()
    me = 4 * x + 2 * y + c
    own = pltpu.make_async_copy(g_ref if same else g_ref.at[me], recv_ref.at[me], local_sem)
    sends, lands = [], []
    for d in range(1, N_DEV):
        px = 1 - x if d & 4 else x
        py = 1 - y if d & 2 else y
        pc = 1 - c if d & 1 else c
        peer = 4 * px + 2 * py + pc
        for slot, group in ((me, sends),) if start else ((me, sends), (peer, lands)):
            group.append(pltpu.make_async_remote_copy(
                src_ref=g_ref if same else g_ref.at[peer], dst_ref=recv_ref.at[slot], send_sem=send_sems.at[d - 1],
                recv_sem=recv_sems.at[d - 1], device_id=(px, py, pc), device_id_type=MESH_ID))
    if start:
        own.start()
        for cp in sends:
            cp.start()
    else:
        for cp in lands:
            cp.wait_recv()
        for cp in sends:
            cp.wait_send()
        own.wait()


def _dw_in(u, dproj, slabs):
    tp = u.shape[0]
    tn = 3 * LANE
    nj = N_EXT // tn

    def body(u_ref, d_ref, g_ref, o_ref, recv_ref, send_sems, recv_sems, local_sem):
        j = pl.program_id(0)

        @pl.when(j == 0)
        def _():
            _exchange(g_ref, recv_ref, send_sems, recv_sems, local_sem, True)

        o_ref[...] = _bf(_dot_tn(d_ref[...], u_ref[...]))

        @pl.when(j == nj - 1)
        def _():
            _exchange(g_ref, recv_ref, send_sems, recv_sems, local_sem, False)

    anyspec = pl.BlockSpec(memory_space=pl.ANY)
    return pl.pallas_call(
        body, name="dw_in", grid=(nj,),
        in_specs=[pl.BlockSpec((tp, D_MODEL), lambda j: (0, 0), pipeline_mode=pl.Buffered(1)),
                  pl.BlockSpec((tp, tn), lambda j: (0, j)), anyspec],
        out_specs=[pl.BlockSpec((tn, D_MODEL), lambda j: (j, 0)), anyspec],
        out_shape=[jax.ShapeDtypeStruct((N_EXT, D_MODEL), BF16), jax.ShapeDtypeStruct(slabs.shape, slabs.dtype)],
        scratch_shapes=EXCHANGE_SEMS,
        compiler_params=_cp(("arbitrary",), 56),
    )(u, dproj, slabs)


def _dx_in(dproj, w_ext, hp, dh2, norm_g, slabs, bsz):
    tp = hp.shape[0]
    lp = tp // bsz
    tm = _attn_block(lp)
    ni = lp // tm
    steps = bsz * ni
    assert ni > 1 and tm > X0

    def body(d_ref, w_ref, h_ref, dh_ref, g_ref, s_ref, gx_ref, dg_ref, dm_ref, recv_ref,
             stage, out_sems, send_sems, recv_sems, local_sem):
        s = pl.program_id(0)
        slot = s % 2

        def out_copy(step, head):
            b, at = step // ni, step % 2
            if head:
                return pltpu.make_async_copy(stage.at[at, pl.ds(X0, tm - X0)], gx_ref.at[b, pl.ds(0, tm - X0)],
                                             out_sems.at[at])
            first = pl.multiple_of((step % ni) * tm - X0, 8)
            return pltpu.make_async_copy(stage.at[at], gx_ref.at[b, pl.ds(first, tm)], out_sems.at[at])

        @pl.when(s == 0)
        def _():
            _exchange(s_ref, recv_ref, send_sems, recv_sems, local_sem, True)
            dg_ref[...] = jnp.zeros_like(dg_ref)
            dm_ref[...] = jnp.zeros_like(dm_ref)

        du = _dot_nt(d_ref[...], w_ref[...])
        g = g_ref[...]
        xh, r = _rms_fwd(h_ref[...])
        dx, dg = _rms_bwd(du, xh, r, g)
        dg_ref[...] += dg
        stage[slot] = dh_ref[...] + dx
        head = s % ni == 0

        @pl.when(head)
        def _():
            dm_ref[...] += stage[slot, pl.ds(FRONT, N_META), :]
            out_copy(s, True).start()

        @pl.when(jnp.logical_not(head))
        def _():
            out_copy(s, False).start()

        @pl.when(s % ni == 1)
        def _():
            out_copy(s - 1, True).wait()

        @pl.when(s % ni > 1)
        def _():
            out_copy(s - 1, False).wait()

        @pl.when(jnp.logical_and(head, s > 0))
        def _():
            out_copy(s - 1, False).wait()

        @pl.when(s == steps - 1)
        def _():
            out_copy(s, False).wait()
            _exchange(s_ref, recv_ref, send_sems, recv_sems, local_sem, False)

    tok = pl.BlockSpec((tm, D_MODEL), lambda s: (s, 0))
    anyspec = pl.BlockSpec(memory_space=pl.ANY)
    return pl.pallas_call(
        body, name="dx_in", grid=(steps,),
        in_specs=[pl.BlockSpec((tm, N_EXT), lambda s: (s, 0)),
                  pl.BlockSpec((D_MODEL, N_EXT), lambda s: (0, 0), pipeline_mode=pl.Buffered(1)),
                  tok, tok, pl.BlockSpec((1, D_MODEL), lambda s: (0, 0)), anyspec],
        out_specs=[anyspec, pl.BlockSpec((1, D_MODEL), lambda s: (0, 0)),
                   pl.BlockSpec((N_META, D_MODEL), lambda s: (0, 0)), anyspec],
        out_shape=[jax.ShapeDtypeStruct((bsz, lp - X0, D_MODEL), F32), jax.ShapeDtypeStruct((1, D_MODEL), F32),
                   jax.ShapeDtypeStruct((N_META, D_MODEL), F32), jax.ShapeDtypeStruct(slabs.shape, slabs.dtype)],
        scratch_shapes=[pltpu.VMEM((2, tm, D_MODEL), F32), pltpu.SemaphoreType.DMA((2,))] + EXCHANGE_SEMS,
        compiler_params=_cp(("arbitrary",), 56),
    )(dproj, w_ext, hp, dh2, norm_g, slabs)


W_IN_SHARD = N_IN // N_DEV


def _pad_lanes(a, width=LANE):
    return jnp.pad(a, [(0, 0)] * (a.ndim - 1) + [(0, width - a.shape[-1])])


def _rot_cols(w):
    half = w.shape[-1] // 2
    return jnp.concatenate([-w[..., half:], w[..., :half]], axis=-1)


def _unrot_cols(dw):
    half = dw.shape[-1] // 2
    return jnp.concatenate([dw[..., half:], -dw[..., :half]], axis=-1)


def _w_in_cols(shards, lo, hi):
    parts = []
    for k in range(lo // W_IN_SHARD, (hi - 1) // W_IN_SHARD + 1):
        a, b = max(lo, k * W_IN_SHARD), min(hi, (k + 1) * W_IN_SHARD)
        parts.append(shards[k][:, a - k * W_IN_SHARD:b - k * W_IN_SHARD])
    return parts[0] if len(parts) == 1 else jnp.concatenate(parts, axis=1)


def _w_in_ext(shards):
    c = lambda lo, hi: _w_in_cols(shards, lo, hi)
    kr = c(O_KR, O_MZ)
    return jnp.concatenate([
        c(O_V, O_LR), c(O_Z, O_CQ), c(O_Q, O_K), c(O_K, O_V), c(O_MZ, O_GG), c(O_GG, O_GM), c(O_GM, N_IN),
        c(O_CKV, O_KR), _pad_lanes(kr), _pad_lanes(_rot_cols(kr)), _pad_lanes(c(O_LR, O_Z)), c(O_CQ, O_CKV)], axis=1)


def _w_in_slabs(dwt):
    half = MLA_ROPE // 2
    krot = dwt[C_KROT:C_KROT + MLA_ROPE]
    kr = dwt[C_KR:C_KR + MLA_ROPE] + jnp.concatenate([krot[half:], -krot[:half]], axis=0)
    groups = ((O_Q, GLA_KW, C_Q), (O_K, GLA_KW, C_K), (O_V, GLA_VW, C_V), (O_LR, GLA_RANK, C_LR), (O_Z, GLA_VW, C_Z),
              (O_CQ, MLA_QR, C_CQ), (O_CKV, MLA_KVR, C_CKV), (O_KR, MLA_ROPE, None), (O_MZ, D_MODEL, C_MZ),
              (O_GG, D_MODEL, C_GG), (O_GM, D_MODEL, C_GM))
    slabs = []
    for k in range(N_DEV):
        lo, hi = k * W_IN_SHARD, (k + 1) * W_IN_SHARD
        parts = []
        for first, width, row in groups:
            a, b = max(lo, first), min(hi, first + width)
            if a < b:
                parts.append(kr[a - first:b - first] if row is None else dwt[row + a - first:row + b - first])
        slabs.append(jnp.concatenate(parts, axis=0))
    return jnp.stack(slabs)


def _rope_tables(lp):
    inv = 1.0 / (ROPE_BASE ** (jnp.arange(0, MLA_ROPE, 2, dtype=F32) / MLA_ROPE))
    ang = (jnp.arange(lp, dtype=F32) - FRONT)[:, None] * inv[None, :]
    cos, sin = jnp.cos(ang), jnp.sin(ang)
    return _pad_lanes(jnp.concatenate([cos, cos], axis=1)), _pad_lanes(jnp.concatenate([sin, sin], axis=1))


def _local_step(x, loss_target, w):
    bsz, seq, _ = x.shape
    lp = X0 + seq
    tp = bsz * lp
    assert lp % TOK == 0 and (lp // GLA_CHUNK) % _gla_group(lp // GLA_CHUNK) == 0
    head = jnp.concatenate([jnp.zeros((FRONT, D_MODEL), F32), w["meta_tokens"]], axis=0)
    cos_t, sin_t = _rope_tables(lp)

    w_ext = _w_in_ext(w["w_in"])
    hp, u, proj, packed_all = _proj_in(x, head, w["norm_g"], w_ext, w["packed"])
    gathered = _unpack_shards(packed_all)
    for n, _, axis in PACKED:
        w[n] = _join8(gathered[n], axis)
    gw_pad = jnp.pad(w["gla_gate_w"], ((0, LANE - GLA_RANK), (0, 0)))
    uq = w["mla_w_uq"].reshape(MLA_QR, MLA_HEADS, MLA_QK)
    rope_w = uq[:, :, MLA_NOPE:]
    hw = MLA_HEADS * LANE
    wn = uq[:, :, :MLA_NOPE].reshape(MLA_QR, hw)
    wr = _pad_lanes(rope_w).reshape(MLA_QR, hw)
    wt = _pad_lanes(_rot_cols(rope_w)).reshape(MLA_QR, hw)
    ukv = w["mla_w_ukv"].reshape(MLA_KVR, MLA_HEADS, MLA_NOPE + MLA_DV)
    wk = ukv[:, :, :MLA_NOPE].reshape(MLA_KVR, hw)
    wv = ukv[:, :, MLA_NOPE:].reshape(MLA_KVR, hw)

    o_raw, ya_in, s_all = _gla_fwd(proj, gw_pad, w["gla_gate_b"], w["gla_norm_g"], bsz, lp)
    qf = _q_up(proj, w["mla_q_norm_g"], wn, wr, wt, cos_t, sin_t, bsz, lp)
    kf, vf = _kv_up(proj, w["mla_kv_norm_g"], wk, wv, cos_t, sin_t, bsz, lp)
    o_b, yb_in, lse = _attn_fwd(qf, kf, vf, proj, bsz, lp)
    y_a, y_b, dh2, loss, d_final_g = _mid_fwd(ya_in, yb_in, proj, hp, loss_target, w["gla_proj"], w["mla_proj"],
                                              w["w_out"], w["final_norm_g"], bsz, lp)
    d_ya, d_o, dproj, delta, d_w_out, d_gla_proj, d_mla_proj = _mid_bwd(
        dh2, y_a, y_b, proj, ya_in, yb_in, o_b, w["w_out"], w["gla_proj"], w["mla_proj"], bsz, lp)
    dproj, d_gate, d_gla_norm = _gla_bwd(proj, gw_pad, w["gla_gate_b"], w["gla_norm_g"], o_raw, s_all, d_ya, dproj,
                                         bsz, lp)
    d_lr, d_gw_pad, d_gate_b = _gate_bwd(d_gate, proj, gw_pad)
    dqf, dkf, dvf = _attn_bwd(qf, kf, vf, d_o, lse, delta, bsz, lp)
    dproj, d_wn, d_wr, d_wt, d_qn = _q_up_bwd(dqf, proj, w["mla_q_norm_g"], wn, wr, wt, cos_t, sin_t, dproj,
                                              bsz, lp)
    dproj, d_wk, d_wv, d_kvn = _kv_up_bwd(dkf, dvf, proj, w["mla_kv_norm_g"], wk, wv, cos_t, sin_t, d_lr, dproj,
                                          bsz, lp)

    d_rope = (d_wr.reshape(MLA_QR, MLA_HEADS, LANE)[:, :, :MLA_ROPE]
              + _unrot_cols(d_wt.reshape(MLA_QR, MLA_HEADS, LANE)[:, :, :MLA_ROPE]))
    d_uq = jnp.concatenate([d_wn.reshape(MLA_QR, MLA_HEADS, LANE), d_rope], axis=-1).reshape(MLA_QR, MLA_HEADS * MLA_QK)
    d_ukv = jnp.concatenate([d_wk.reshape(MLA_KVR, MLA_HEADS, LANE), d_wv.reshape(MLA_KVR, MLA_HEADS, LANE)],
                            axis=-1).reshape(MLA_KVR, MLA_HEADS * (MLA_NOPE + MLA_DV))
    mats = dict(gla_gate_w=d_gw_pad[:GLA_RANK], gla_proj=d_gla_proj, mla_w_uq=d_uq, mla_w_ukv=d_ukv,
                mla_proj=d_mla_proj, w_out=d_w_out)
    packed = _pack_shards({n: _bf(_split8(mats[n], axis)) for n, _, axis in PACKED})
    d_w_ext_t, packed_parts = _dw_in(u, dproj, packed)
    w_in_slabs = _w_in_slabs(d_w_ext_t)
    grad_x, d_norm_g, d_meta, w_in_parts = _dx_in(dproj, w_ext, hp, dh2, w["norm_g"], w_in_slabs, bsz)
    small = dict(meta_tokens=d_meta, norm_g=d_norm_g, gla_gate_b=d_gate_b, gla_norm_g=d_gla_norm,
                 mla_q_norm_g=d_qn, mla_kv_norm_g=d_kvn, final_norm_g=d_final_g)
    return loss, grad_x, w_in_parts, packed_parts, small


PACKED = (("gla_gate_w", (GLA_RANK, GLA_KW // N_DEV), 1),
          ("gla_proj", (D_MODEL // N_DEV, D_MODEL), 0), ("mla_w_uq", (MLA_QR, MLA_HEADS * MLA_QK // N_DEV), 1),
          ("mla_w_ukv", (MLA_KVR, MLA_HEADS * (MLA_NOPE + MLA_DV) // N_DEV), 1),
          ("mla_proj", (D_MODEL // N_DEV, D_MODEL), 0), ("w_out", (D_MODEL // N_DEV, D_MODEL), 0))
REPLICATED = (("norm_g", D_MODEL), ("gla_gate_b", GLA_KW), ("gla_norm_g", GLA_DV), ("mla_q_norm_g", MLA_QR),
              ("mla_kv_norm_g", MLA_KVR), ("final_norm_g", D_MODEL))
PACK_ROWS = 480
PACK_BLOCK = 160
SMALL_ROWS = 48
LOSS_ROW = N_META + 25
W_IN_BLOCK = 128


def _all_gather(shards):
    n_arr = len(shards)
    pieces = []
    for a, s in enumerate(shards):
        step = s.shape[0] // 4 if s.shape[0] >= 4 * LANE else s.shape[0]
        pieces += [(a, slice(r, r + step)) for r in range(0, s.shape[0], step)]
    n_pc = len(pieces)

    def body(*refs):
        x_refs, out_refs = refs[:n_arr], refs[n_arr:2 * n_arr]
        send_sems, recv_sems, local_sems = refs[2 * n_arr:]
        x, y, c = _my_place()
        me, sibling = (x, y, c), (x, y, 1 - c)
        chips = [(1 - x, y), (x, 1 - y), (1 - x, 1 - y)]

        def copy(u, k, block, to, from_input=False):
            a, rows = pieces[u]
            slab = out_refs[a].at[4 * block[0] + 2 * block[1] + block[2], rows]
            return pltpu.make_async_remote_copy(
                src_ref=x_refs[a].at[rows] if from_input else slab, dst_ref=slab,
                send_sem=send_sems.at[7 * u + k], recv_sem=recv_sems.at[7 * u + k], device_id=to,
                device_id_type=MESH_ID)

        arrays = range(n_pc)
        mine = [pltpu.make_async_copy(x_refs[a], out_refs[a].at[4 * x + 2 * y + c], local_sems.at[a])
                for a in range(n_arr)]
        first = [copy(a, 1 + j, me, (*chip, c), True) for a in arrays for j, chip in enumerate(chips)]
        first += [copy(a, 0, me, sibling, True) for a in arrays]
        for cp in first + mine:
            cp.start()
        passed = []
        for j, chip in enumerate(chips):
            for a in arrays:
                copy(a, 1 + j, (*chip, c), me).wait_recv()
                passed.append(copy(a, 4 + j, (*chip, c), sibling))
                passed[-1].start()
        for a in arrays:
            copy(a, 0, sibling, me).wait_recv()
        for j, chip in enumerate(chips):
            for a in arrays:
                copy(a, 4 + j, (*chip, 1 - c), me).wait_recv()
        for cp in first + passed:
            cp.wait_send()
        for cp in mine:
            cp.wait()

    anyspec = pl.BlockSpec(memory_space=pl.ANY)
    return pl.pallas_call(
        body, name="weights_all_gather",
        out_shape=[jax.ShapeDtypeStruct((N_DEV,) + s.shape, s.dtype) for s in shards],
        in_specs=[anyspec] * n_arr, out_specs=[anyspec] * n_arr,
        scratch_shapes=[pltpu.SemaphoreType.DMA((7 * n_pc,)), pltpu.SemaphoreType.DMA((7 * n_pc,)),
                        pltpu.SemaphoreType.DMA((n_arr,))],
    )(*shards)


def _adamw(parts, w, m, v, block_rows, name, beside=None):
    rows, cols = w.shape
    steps = pl.cdiv(rows, block_rows)

    def update(p_ref, w_ref, m_ref, v_ref, g_out, d_out, m_out, v_out):
        g = p_ref[0].astype(F32)
        for s in range(1, N_DEV):
            g = g + p_ref[s].astype(F32)
        m_new = ADAM_B1 * m_ref[...] + (1.0 - ADAM_B1) * g
        v_new = ADAM_B2 * v_ref[...] + (1.0 - ADAM_B2) * (g * g)
        m_hat = m_new / (1.0 - ADAM_B1 ** ADAM_STEP)
        v_hat = v_new / (1.0 - ADAM_B2 ** ADAM_STEP)
        g_out[...] = g
        d_out[...] = -ADAM_LR * (m_hat / (jnp.sqrt(v_hat) + ADAM_EPS) + ADAM_WD * w_ref[...])
        m_out[...] = m_new
        v_out[...] = v_new

    def update_beside_exchange(p_ref, w_ref, m_ref, v_ref, s_ref, g_out, d_out, m_out, v_out, recv_ref,
                               send_sems, recv_sems, local_sem):
        @pl.when(pl.program_id(0) == 0)
        def _():
            _exchange(s_ref, recv_ref, send_sems, recv_sems, local_sem, True)

        update(p_ref, w_ref, m_ref, v_ref, g_out, d_out, m_out, v_out)

        @pl.when(pl.program_id(0) == steps - 1)
        def _():
            _exchange(s_ref, recv_ref, send_sems, recv_sems, local_sem, False)

    spec = pl.BlockSpec((block_rows, cols), lambda i: (i, 0))
    anyspec = pl.BlockSpec(memory_space=pl.ANY)
    in_specs = [pl.BlockSpec((N_DEV, block_rows, cols), lambda i: (0, i, 0)), spec, spec, spec]
    out_shape = [jax.ShapeDtypeStruct((rows, cols), F32)] * 4
    if beside is None:
        return pl.pallas_call(
            update, name=name, grid=(steps,),
            in_specs=in_specs, out_specs=[spec] * 4, out_shape=out_shape,
            compiler_params=_cp(("parallel",), 48),
        )(parts, w, m, v)
    return pl.pallas_call(
        update_beside_exchange, name=name, grid=(steps,),
        in_specs=in_specs + [anyspec], out_specs=[spec] * 4 + [anyspec],
        out_shape=out_shape + [jax.ShapeDtypeStruct(beside.shape, beside.dtype)],
        scratch_shapes=EXCHANGE_SEMS,
        compiler_params=_cp(("arbitrary",), 48),
    )(parts, w, m, v, beside)


def _pack_rows_of(shape):
    rows = shape[0] * shape[1] // D_MODEL
    return -(-rows // 16) * 16


def _pack_shards(shards):
    parts = []
    for n, shape, _ in PACKED:
        a = shards[n]
        lead = a.shape[:-2]
        if shape[1] != D_MODEL:
            a = a.reshape(lead + (shape[0] * shape[1] // D_MODEL, D_MODEL))
        pad = _pack_rows_of(shape) - a.shape[-2]
        parts.append(jnp.pad(a, [(0, 0)] * len(lead) + [(0, pad), (0, 0)]) if pad else a)
    return jnp.concatenate(parts, axis=-2)


def _unpack_shards(packed):
    lead, out, off = packed.shape[:-2], {}, 0
    for n, shape, _ in PACKED:
        rows = shape[0] * shape[1] // D_MODEL
        out[n] = packed[..., off:off + rows, :].reshape(lead + shape)
        off += _pack_rows_of(shape)
    return out


def _split8(full, axis):
    r, c = full.shape
    if axis == 0:
        return full.reshape(N_DEV, r // N_DEV, c)
    return full.reshape(r, N_DEV, c // N_DEV).transpose(1, 0, 2)


def _join8(shards, axis):
    _, r, c = shards.shape
    if axis == 0:
        return shards.reshape(N_DEV * r, c)
    return shards.transpose(1, 0, 2).reshape(r, N_DEV * c)


def _pack_small(meta_shard, vals, loss_row):
    rows = jnp.concatenate([vals[n].reshape(-1, LANE) for n, _ in REPLICATED] + [loss_row], axis=0)
    rows = jnp.pad(rows, ((0, SMALL_ROWS - N_META - rows.shape[0]), (0, 0)))
    return jnp.concatenate([meta_shard, jnp.broadcast_to(rows, meta_shard.shape[:-2] + rows.shape)], axis=-2)


def _unpack_small(packed):
    out, off = {"meta_tokens": packed[:N_META]}, N_META
    for n, size in REPLICATED:
        out[n] = packed[off:off + size // LANE].reshape(1, size)
        off += size // LANE
    return out


def kernel(x, meta_tokens, norm_g, w_in, gla_gate_w, gla_gate_b, gla_norm_g, gla_proj, mla_q_norm_g, mla_w_uq, mla_kv_norm_g, mla_w_ukv, mla_proj, w_out, final_norm_g, loss_target, m_meta_tokens, m_norm_g, m_w_in, m_gla_gate_w, m_gla_gate_b, m_gla_norm_g, m_gla_proj, m_mla_q_norm_g, m_mla_w_uq, m_mla_kv_norm_g, m_mla_w_ukv, m_mla_proj, m_w_out, m_final_norm_g, v_meta_tokens, v_norm_g, v_w_in, v_gla_gate_w, v_gla_gate_b, v_gla_norm_g, v_gla_proj, v_mla_q_norm_g, v_mla_w_uq, v_mla_kv_norm_g, v_mla_w_ukv, v_mla_proj, v_w_out, v_final_norm_g):
    given = dict(meta_tokens=meta_tokens, norm_g=norm_g, w_in=w_in, gla_gate_w=gla_gate_w, gla_gate_b=gla_gate_b,
                 gla_norm_g=gla_norm_g, gla_proj=gla_proj, mla_q_norm_g=mla_q_norm_g, mla_w_uq=mla_w_uq,
                 mla_kv_norm_g=mla_kv_norm_g, mla_w_ukv=mla_w_ukv, mla_proj=mla_proj, w_out=w_out,
                 final_norm_g=final_norm_g)
    mom_m = dict(meta_tokens=m_meta_tokens, norm_g=m_norm_g, w_in=m_w_in, gla_gate_w=m_gla_gate_w,
                 gla_gate_b=m_gla_gate_b, gla_norm_g=m_gla_norm_g, gla_proj=m_gla_proj, mla_q_norm_g=m_mla_q_norm_g,
                 mla_w_uq=m_mla_w_uq, mla_kv_norm_g=m_mla_kv_norm_g, mla_w_ukv=m_mla_w_ukv, mla_proj=m_mla_proj,
                 w_out=m_w_out, final_norm_g=m_final_norm_g)
    mom_v = dict(meta_tokens=v_meta_tokens, norm_g=v_norm_g, w_in=v_w_in, gla_gate_w=v_gla_gate_w,
                 gla_gate_b=v_gla_gate_b, gla_norm_g=v_gla_norm_g, gla_proj=v_gla_proj, mla_q_norm_g=v_mla_q_norm_g,
                 mla_w_uq=v_mla_w_uq, mla_kv_norm_g=v_mla_kv_norm_g, mla_w_ukv=v_mla_w_ukv, mla_proj=v_mla_proj,
                 w_out=v_w_out, final_norm_g=v_final_norm_g)
    shapes = {n: a.shape for n, a in given.items()}
    shard2d = {n: s for n, s, _ in PACKED}
    shard2d["w_in"] = (D_MODEL, W_IN_SHARD)
    shard2d["meta_tokens"] = (N_META, LANE)

    def as2d(tree):
        out = {n: tree[n].reshape(shard2d[n]) for n in shard2d}
        out.update({n: tree[n].reshape(1, size) for n, size in REPLICATED})
        return out

    w_loc, m_loc, v_loc = as2d(given), as2d(mom_m), as2d(mom_v)

    w_in_all, meta_all = _all_gather([w_loc["w_in"].astype(BF16), w_loc["meta_tokens"]])
    packed = _pack_shards({n: w_loc[n].astype(BF16) for n, _, _ in PACKED})
    full = {"w_in": w_in_all, "meta_tokens": _join8(meta_all, 1), "packed": packed}
    for n, _ in REPLICATED:
        full[n] = w_loc[n]

    loss_part, grad_x, w_in_parts, packed_parts, small = _local_step(x, loss_target, full)
    small_slabs = _pack_small(_split8(small["meta_tokens"], 1), small, jnp.broadcast_to(loss_part[:, :1], (1, LANE)))

    w_in_t = [t["w_in"].T for t in (w_loc, m_loc, v_loc)]
    *w_in_new, small_all = _adamw(w_in_parts, *w_in_t, W_IN_BLOCK, "adamw_w_in", beside=small_slabs)
    g_w, d_w, m_w, v_w = (o.T for o in w_in_new)
    g_p, d_p, m_p, v_p = _adamw(packed_parts, _pack_shards(w_loc), _pack_shards(m_loc), _pack_shards(v_loc),
                                PACK_BLOCK, "adamw_packed")
    zero_row = jnp.zeros((1, LANE), F32)
    g_s, d_s, m_s, v_s = _adamw(small_all, *(_pack_small(t["meta_tokens"], t, zero_row) for t in (w_loc, m_loc, v_loc)),
                                SMALL_ROWS, "adamw_small")
    loss = g_s[LOSS_ROW, 0]

    order = ["meta_tokens", "norm_g", "w_in", "gla_gate_w", "gla_gate_b", "gla_norm_g", "gla_proj", "mla_q_norm_g",
             "mla_w_uq", "mla_kv_norm_g", "mla_w_ukv", "mla_proj", "w_out", "final_norm_g"]
    result = [loss, grad_x]
    for w_in_out, packed_sh, packed_sm in ((g_w, g_p, g_s), (d_w, d_p, d_s), (m_w, m_p, m_s), (v_w, v_p, v_s)):
        tree = _unpack_shards(packed_sh)
        tree.update(_unpack_small(packed_sm))
        tree["w_in"] = w_in_out
        result += [tree[n].reshape(shapes[n]) for n in order]
    return tuple(result)
```
